```python
import math
import jax
import jax.numpy as jnp
from jax import lax
import numpy as np

D_MODEL = 1024
BATCH = 8
SEQ = 4096
DEPTH = 2

N_MIXERS = 2
MEM_LEN = 256
EPS = 1e-6

A_HEADS = 12
A_KV_HEADS = 2
A_HEAD_DIM = 64
WINDOW = 128
BLOCK = 128
N_BUCKETS = 32
MAX_DISTANCE = 128

B_QK_HEADS = 3
B_V_HEADS = 6
B_HEAD_DIM = 128
B_CONV = 4
CHUNK = 64

X_HEADS = 4
X_HEAD_DIM = 64

D_FF = 2816
FFN_CONV = 3

A_Q = A_HEADS * A_HEAD_DIM
A_KV = A_KV_HEADS * A_HEAD_DIM
X_Q = X_HEADS * X_HEAD_DIM
B_QK = B_QK_HEADS * B_HEAD_DIM
B_V = B_V_HEADS * B_HEAD_DIM
B_QKV = 2 * B_QK + B_V
IN_A = A_Q + 2 * A_KV + X_Q
IN_B = B_QKV + B_V + 2 * B_V_HEADS + X_Q
MIX_WIDTH = A_Q + X_Q
N_A_LAYERS = (DEPTH + 1) // 2
N_B_LAYERS = DEPTH // 2

kernel_name = "hybrid_swa_sink_deltanet_memxattn_convglu"


def rms_norm(x, g):
    xf = x.astype(jnp.float32)
    y = xf * lax.rsqrt(jnp.mean(xf * xf, axis=-1, keepdims=True) + EPS)
    return (y * g.astype(jnp.float32)).astype(x.dtype)


def l2_normalize(x):
    xf = x.astype(jnp.float32)
    return xf * lax.rsqrt(jnp.sum(xf * xf, axis=-1, keepdims=True) + EPS)


def causal_depthwise_conv(x, w):
    k = w.shape[0]
    return lax.conv_general_dilated(
        x, w[:, None, :].astype(x.dtype), window_strides=(1,), padding=[(k - 1, 0)],
        dimension_numbers=('NWC', 'WIO', 'NWC'), feature_group_count=x.shape[-1])


def t5_causal_bucket(dist):
    n = jnp.maximum(dist, 0)
    max_exact = N_BUCKETS // 2
    nf = jnp.maximum(n, 1).astype(jnp.float32)
    large = max_exact + (jnp.log(nf / max_exact) / math.log(MAX_DISTANCE / max_exact)
                         * (N_BUCKETS - max_exact)).astype(jnp.int32)
    large = jnp.minimum(large, N_BUCKETS - 1)
    return jnp.where(n < max_exact, n, large)


def swa_sink_attention(q, k, v, sinks, rel_bias):
    b, s, _, dh = q.shape
    nb = s // BLOCK
    grp = A_HEADS // A_KV_HEADS
    qb = q.reshape(b, nb, BLOCK, A_KV_HEADS, grp, dh)

    def band(t):
        tb = t.reshape(b, nb, BLOCK, A_KV_HEADS, dh)
        prev = jnp.concatenate([jnp.zeros_like(tb[:, :1]), tb[:, :-1]], axis=1)
        return jnp.concatenate([prev, tb], axis=2)

    kb, vb = band(k), band(v)
    qi = jnp.arange(BLOCK)[:, None]
    kj = jnp.arange(2 * BLOCK)[None, :]
    dist = BLOCK + qi - kj
    band_ok = (dist >= 0) & (dist < WINDOW)
    blk_ok = (jnp.arange(nb)[:, None, None] > 0) | (kj[None] >= BLOCK)
    mask = band_ok[None] & blk_ok
    bias = rel_bias[t5_causal_bucket(dist)]
    bias = jnp.transpose(bias, (2, 0, 1)).reshape(A_KV_HEADS, grp, BLOCK, 2 * BLOCK)

    scores = jnp.einsum('bnqhgd,bnkhd->bnhgqk', qb, kb,
                        preferred_element_type=jnp.float32) * (dh ** -0.5)
    scores = scores + bias.astype(jnp.float32)
    scores = jnp.where(mask[None, :, None, None], scores, -jnp.inf)
    sink = sinks.astype(jnp.float32).reshape(1, 1, A_KV_HEADS, grp, 1, 1)
    m = jnp.maximum(jnp.max(scores, axis=-1, keepdims=True), sink)
    p = jnp.exp(scores - m)
    probs = p / (jnp.sum(p, axis=-1, keepdims=True) + jnp.exp(sink - m))
    out = jnp.einsum('bnhgqk,bnkhd->bnqhgd', probs.astype(v.dtype), vb)
    return out.reshape(b, s, A_HEADS * dh)


def memory_cross_attention(q, mem_k, mem_v):
    b, s = q.shape[0], q.shape[1]
    scores = jnp.einsum('bshd,bmhd->bhsm', q, mem_k,
                        preferred_element_type=jnp.float32) * (X_HEAD_DIM ** -0.5)
    probs = jax.nn.softmax(scores, axis=-1).astype(mem_v.dtype)
    out = jnp.einsum('bhsm,bmhd->bshd', probs, mem_v)
    return out.reshape(b, s, X_Q)


def gated_delta_rule(q, k, v, g, beta):
    b, s, h, dk = q.shape
    dv = v.shape[-1]
    nc = s // CHUNK
    f32 = jnp.float32

    def chunks(t):
        t = t.astype(f32).reshape((b, nc, CHUNK, h) + t.shape[3:])
        return jnp.moveaxis(t, 3, 1)

    qc = chunks(q) * (dk ** -0.5)
    kc = chunks(k)
    vc = chunks(v)
    bc = chunks(beta)
    gc = jnp.cumsum(chunks(g), axis=-1)
    idx = jnp.arange(CHUNK)
    strict = idx[:, None] > idx[None, :]
    incl = idx[:, None] >= idx[None, :]
    gdiff = gc[..., :, None] - gc[..., None, :]
    decay_incl = jnp.exp(jnp.where(incl, gdiff, -jnp.inf))
    kk = jnp.einsum('bhntd,bhnsd->bhnts', kc, kc)
    a_mat = bc[..., :, None] * kk * jnp.where(strict, decay_incl, 0.0)
    eye = jnp.eye(CHUNK, dtype=f32)
    rhs = jnp.concatenate([bc[..., None] * vc, (bc * jnp.exp(gc))[..., None] * kc], axis=-1)
    sol = lax.linalg.triangular_solve(eye + a_mat, rhs, left_side=True, lower=True,
                                      unit_diagonal=True)
    u, w = sol[..., :dv], sol[..., dv:]
    attn_qk = jnp.einsum('bhntd,bhnsd->bhnts', qc, kc) * decay_incl
    q_decay = qc * jnp.exp(gc)[..., None]
    k_tail = kc * jnp.exp(gc[..., -1:] - gc)[..., None]
    decay_chunk = jnp.exp(gc[..., -1])

    def step(state, xs):
        u_n, w_n, a_n, qd_n, kt_n, dc_n = xs
        delta = u_n - jnp.einsum('bhtk,bhkv->bhtv', w_n, state)
        out = (jnp.einsum('bhtk,bhkv->bhtv', qd_n, state)
               + jnp.einsum('bhts,bhsv->bhtv', a_n, delta))
        state = dc_n[..., None, None] * state + jnp.einsum('bhsk,bhsv->bhkv', kt_n, delta)
        return state, out

    xs = tuple(jnp.moveaxis(t, 2, 0) for t in (u, w, attn_qk, q_decay, k_tail, decay_chunk))
    state0 = jnp.zeros((b, h, dk, dv), f32)
    _, out = lax.scan(step, state0, xs)
    return jnp.transpose(out, (1, 0, 3, 2, 4)).reshape(b, s, h, dv)


def swa_group(proj, sinks, rel_bias):
    b, s, _ = proj.shape
    q, k, v = jnp.split(proj, [A_Q, A_Q + A_KV], axis=-1)
    q = q.reshape(b, s, A_HEADS, A_HEAD_DIM)
    k = k.reshape(b, s, A_KV_HEADS, A_HEAD_DIM)
    v = v.reshape(b, s, A_KV_HEADS, A_HEAD_DIM)
    return swa_sink_attention(q, k, v, sinks, rel_bias)


def deltanet_group(proj, conv_w, a_log, dt_bias, norm_g):
    b, s, _ = proj.shape
    f32 = jnp.float32
    qkv, z, beta_logit, a_logit = jnp.split(
        proj, [B_QKV, B_QKV + B_V, B_QKV + B_V + B_V_HEADS], axis=-1)
    qkv = jax.nn.silu(causal_depthwise_conv(qkv, conv_w))
    q, k, v = jnp.split(qkv, [B_QK, 2 * B_QK], axis=-1)
    rep = B_V_HEADS // B_QK_HEADS
    q = jnp.repeat(l2_normalize(q.reshape(b, s, B_QK_HEADS, B_HEAD_DIM)), rep, axis=2)
    k = jnp.repeat(l2_normalize(k.reshape(b, s, B_QK_HEADS, B_HEAD_DIM)), rep, axis=2)
    v = v.reshape(b, s, B_V_HEADS, B_HEAD_DIM)
    beta = jax.nn.sigmoid(beta_logit.astype(f32))
    g = -jnp.exp(a_log.astype(f32)) * jax.nn.softplus(a_logit.astype(f32) + dt_bias.astype(f32))
    o = gated_delta_rule(q, k, v, g, beta)
    o = o * lax.rsqrt(jnp.mean(o * o, axis=-1, keepdims=True) + EPS) * norm_g.astype(f32)
    o = o * jax.nn.silu(z.reshape(b, s, B_V_HEADS, B_HEAD_DIM).astype(f32))
    return o.reshape(b, s, B_V).astype(proj.dtype)


def conv_glu(hn, w_gate_up, conv_w, conv_b, w_down):
    gate, up = jnp.split(hn @ w_gate_up, [D_FF], axis=-1)
    gate = causal_depthwise_conv(gate, conv_w) + conv_b
    return (jax.nn.silu(gate) * up) @ w_down


def _fwd_setup_inputs(seed: int = 0) -> dict:
    key = jax.random.key(seed)
    ks = jax.random.split(key, 24)
    f32 = jnp.float32

    def dense(k, shape, fan_in):
        return jax.random.normal(k, shape, f32) * (fan_in ** -0.5)

    def gain(k, shape):
        return 1.0 + 0.02 * jax.random.normal(k, shape, f32)

    dt = jnp.exp(jax.random.uniform(ks[13], (N_B_LAYERS, B_V_HEADS), f32,
                                    math.log(1e-3), math.log(1e-1)))
    return {
        'x': jax.random.normal(ks[0], (BATCH, SEQ, D_MODEL), f32),
        'mem': jax.random.normal(ks[1], (BATCH, MEM_LEN, D_MODEL), f32),
        'rel_bias': 0.5 * jax.random.normal(ks[2], (N_BUCKETS, A_HEADS), f32),
        'norm_mix_g': gain(ks[3], (DEPTH, D_MODEL)),
        'norm_mem_g': gain(ks[4], (DEPTH, D_MODEL)),
        'w_mem_kv': dense(ks[5], (DEPTH, D_MODEL, 2 * X_Q), D_MODEL),
        'w_out': dense(ks[6], (DEPTH, MIX_WIDTH, D_MODEL), MIX_WIDTH),
        'w_in_a': dense(ks[7], (N_A_LAYERS, D_MODEL, IN_A), D_MODEL),
        'sinks_a': 0.5 * jax.random.normal(ks[8], (N_A_LAYERS, A_HEADS), f32),
        'w_in_b': dense(ks[9], (N_B_LAYERS, D_MODEL, IN_B), D_MODEL),
        'conv_qkv_b': dense(ks[10], (N_B_LAYERS, B_CONV, B_QKV), B_CONV),
        'a_log_b': jnp.log(jax.random.uniform(ks[11], (N_B_LAYERS, B_V_HEADS), f32, 1.0, 16.0)),
        'dt_bias_b': dt + jnp.log(-jnp.expm1(-dt)),
        'out_norm_g_b': gain(ks[12], (N_B_LAYERS, B_HEAD_DIM)),
        'norm_ffn_g': gain(ks[14], (DEPTH, D_MODEL)),
        'w_gate_up': dense(ks[15], (DEPTH, D_MODEL, 2 * D_FF), D_MODEL),
        'ffn_conv_w': dense(ks[16], (DEPTH, FFN_CONV, D_FF), FFN_CONV),
        'ffn_conv_b': 0.02 * jax.random.normal(ks[17], (DEPTH, D_FF), f32),
        'w_down': dense(ks[18], (DEPTH, D_FF, D_MODEL), D_FF),
        'final_norm_g': gain(ks[19], (D_MODEL,)),
    }


def _fwd_reference(x, mem, rel_bias, norm_mix_g, norm_mem_g, w_mem_kv, w_out, w_in_a, sinks_a,
              w_in_b, conv_qkv_b, a_log_b, dt_bias_b, out_norm_g_b, norm_ffn_g, w_gate_up,
              ffn_conv_w, ffn_conv_b, w_down, final_norm_g):
    b, s, _ = x.shape
    h = x
    for i in range(DEPTH):
        j = i // N_MIXERS
        hn = rms_norm(h, norm_mix_g[i])
        mem_kv = rms_norm(mem, norm_mem_g[i]) @ w_mem_kv[i]
        mem_k = mem_kv[..., :X_Q].reshape(b, MEM_LEN, X_HEADS, X_HEAD_DIM)
        mem_v = mem_kv[..., X_Q:].reshape(b, MEM_LEN, X_HEADS, X_HEAD_DIM)
        if i % N_MIXERS == 0:
            proj = hn @ w_in_a[j]
            self_out = swa_group(proj[..., :IN_A - X_Q], sinks_a[j], rel_bias)
        else:
            proj = hn @ w_in_b[j]
            self_out = deltanet_group(proj[..., :IN_B - X_Q], conv_qkv_b[j], a_log_b[j],
                                      dt_bias_b[j], out_norm_g_b[j])
        xq = proj[..., -X_Q:].reshape(b, s, X_HEADS, X_HEAD_DIM)
        cross_out = memory_cross_attention(xq, mem_k, mem_v)
        h = h + jnp.concatenate([self_out, cross_out], axis=-1) @ w_out[i]
        h = h + conv_glu(rms_norm(h, norm_ffn_g[i]), w_gate_up[i], ffn_conv_w[i],
                         ffn_conv_b[i], w_down[i])
    return rms_norm(h, final_norm_g)


import jax as _jax
import jax.numpy as _jnp

TWIN_FORMAT = 'train_step'
FWD_PARAMS = ['x', 'mem', 'rel_bias', 'norm_mix_g', 'norm_mem_g', 'w_mem_kv', 'w_out', 'w_in_a', 'sinks_a', 'w_in_b', 'conv_qkv_b', 'a_log_b', 'dt_bias_b', 'out_norm_g_b', 'norm_ffn_g', 'w_gate_up', 'ffn_conv_w', 'ffn_conv_b', 'w_down', 'final_norm_g']
TWIN_WEIGHTS = ['rel_bias', 'norm_mix_g', 'norm_mem_g', 'w_mem_kv', 'w_out', 'w_in_a', 'sinks_a', 'w_in_b', 'conv_qkv_b', 'a_log_b', 'dt_bias_b', 'out_norm_g_b', 'norm_ffn_g', 'w_gate_up', 'ffn_conv_w', 'ffn_conv_b', 'w_down', 'final_norm_g']
TWIN_DIFF_INPUT = 'x'
TWIN_INPUTS = ['x', 'mem', 'rel_bias', 'norm_mix_g', 'norm_mem_g', 'w_mem_kv', 'w_out', 'w_in_a', 'sinks_a', 'w_in_b', 'conv_qkv_b', 'a_log_b', 'dt_bias_b', 'out_norm_g_b', 'norm_ffn_g', 'w_gate_up', 'ffn_conv_w', 'ffn_conv_b', 'w_down', 'final_norm_g', 'loss_target', 'm_rel_bias', 'm_norm_mix_g', 'm_norm_mem_g', 'm_w_mem_kv', 'm_w_out', 'm_w_in_a', 'm_sinks_a', 'm_w_in_b', 'm_conv_qkv_b', 'm_a_log_b', 'm_dt_bias_b', 'm_out_norm_g_b', 'm_norm_ffn_g', 'm_w_gate_up', 'm_ffn_conv_w', 'm_ffn_conv_b', 'm_w_down', 'm_final_norm_g', 'v_rel_bias', 'v_norm_mix_g', 'v_norm_mem_g', 'v_w_mem_kv', 'v_w_out', 'v_w_in_a', 'v_sinks_a', 'v_w_in_b', 'v_conv_qkv_b', 'v_a_log_b', 'v_dt_bias_b', 'v_out_norm_g_b', 'v_norm_ffn_g', 'v_w_gate_up', 'v_ffn_conv_w', 'v_ffn_conv_b', 'v_w_down', 'v_final_norm_g']
TWIN_OUTPUTS = ['loss', 'grad_x', 'grad_rel_bias', 'grad_norm_mix_g', 'grad_norm_mem_g', 'grad_w_mem_kv', 'grad_w_out', 'grad_w_in_a', 'grad_sinks_a', 'grad_w_in_b', 'grad_conv_qkv_b', 'grad_a_log_b', 'grad_dt_bias_b', 'grad_out_norm_g_b', 'grad_norm_ffn_g', 'grad_w_gate_up', 'grad_ffn_conv_w', 'grad_ffn_conv_b', 'grad_w_down', 'grad_final_norm_g', 'delta_rel_bias', 'delta_norm_mix_g', 'delta_norm_mem_g', 'delta_w_mem_kv', 'delta_w_out', 'delta_w_in_a', 'delta_sinks_a', 'delta_w_in_b', 'delta_conv_qkv_b', 'delta_a_log_b', 'delta_dt_bias_b', 'delta_out_norm_g_b', 'delta_norm_ffn_g', 'delta_w_gate_up', 'delta_ffn_conv_w', 'delta_ffn_conv_b', 'delta_w_down', 'delta_final_norm_g', 'new_m_rel_bias', 'new_m_norm_mix_g', 'new_m_norm_mem_g', 'new_m_w_mem_kv', 'new_m_w_out', 'new_m_w_in_a', 'new_m_sinks_a', 'new_m_w_in_b', 'new_m_conv_qkv_b', 'new_m_a_log_b', 'new_m_dt_bias_b', 'new_m_out_norm_g_b', 'new_m_norm_ffn_g', 'new_m_w_gate_up', 'new_m_ffn_conv_w', 'new_m_ffn_conv_b', 'new_m_w_down', 'new_m_final_norm_g', 'new_v_rel_bias', 'new_v_norm_mix_g', 'new_v_norm_mem_g', 'new_v_w_mem_kv', 'new_v_w_out', 'new_v_w_in_a', 'new_v_sinks_a', 'new_v_w_in_b', 'new_v_conv_qkv_b', 'new_v_a_log_b', 'new_v_dt_bias_b', 'new_v_out_norm_g_b', 'new_v_norm_ffn_g', 'new_v_w_gate_up', 'new_v_ffn_conv_w', 'new_v_ffn_conv_b', 'new_v_w_down', 'new_v_final_norm_g']
TWIN_LEAF_KINDS = {'loss': 'loss', 'grad_x': 'grad_x', 'grad_rel_bias': 'grad_w', 'grad_norm_mix_g': 'grad_w', 'grad_norm_mem_g': 'grad_w', 'grad_w_mem_kv': 'grad_w', 'grad_w_out': 'grad_w', 'grad_w_in_a': 'grad_w', 'grad_sinks_a': 'grad_w', 'grad_w_in_b': 'grad_w', 'grad_conv_qkv_b': 'grad_w', 'grad_a_log_b': 'grad_w', 'grad_dt_bias_b': 'grad_w', 'grad_out_norm_g_b': 'grad_w', 'grad_norm_ffn_g': 'grad_w', 'grad_w_gate_up': 'grad_w', 'grad_ffn_conv_w': 'grad_w', 'grad_ffn_conv_b': 'grad_w', 'grad_w_down': 'grad_w', 'grad_final_norm_g': 'grad_w', 'delta_rel_bias': 'delta_w', 'delta_norm_mix_g': 'delta_w', 'delta_norm_mem_g': 'delta_w', 'delta_w_mem_kv': 'delta_w', 'delta_w_out': 'delta_w', 'delta_w_in_a': 'delta_w', 'delta_sinks_a': 'delta_w', 'delta_w_in_b': 'delta_w', 'delta_conv_qkv_b': 'delta_w', 'delta_a_log_b': 'delta_w', 'delta_dt_bias_b': 'delta_w', 'delta_out_norm_g_b': 'delta_w', 'delta_norm_ffn_g': 'delta_w', 'delta_w_gate_up': 'delta_w', 'delta_ffn_conv_w': 'delta_w', 'delta_ffn_conv_b': 'delta_w', 'delta_w_down': 'delta_w', 'delta_final_norm_g': 'delta_w', 'new_m_rel_bias': 'new_m', 'new_m_norm_mix_g': 'new_m', 'new_m_norm_mem_g': 'new_m', 'new_m_w_mem_kv': 'new_m', 'new_m_w_out': 'new_m', 'new_m_w_in_a': 'new_m', 'new_m_sinks_a': 'new_m', 'new_m_w_in_b': 'new_m', 'new_m_conv_qkv_b': 'new_m', 'new_m_a_log_b': 'new_m', 'new_m_dt_bias_b': 'new_m', 'new_m_out_norm_g_b': 'new_m', 'new_m_norm_ffn_g': 'new_m', 'new_m_w_gate_up': 'new_m', 'new_m_ffn_conv_w': 'new_m', 'new_m_ffn_conv_b': 'new_m', 'new_m_w_down': 'new_m', 'new_m_final_norm_g': 'new_m', 'new_v_rel_bias': 'new_v', 'new_v_norm_mix_g': 'new_v', 'new_v_norm_mem_g': 'new_v', 'new_v_w_mem_kv': 'new_v', 'new_v_w_out': 'new_v', 'new_v_w_in_a': 'new_v', 'new_v_sinks_a': 'new_v', 'new_v_w_in_b': 'new_v', 'new_v_conv_qkv_b': 'new_v', 'new_v_a_log_b': 'new_v', 'new_v_dt_bias_b': 'new_v', 'new_v_out_norm_g_b': 'new_v', 'new_v_norm_ffn_g': 'new_v', 'new_v_w_gate_up': 'new_v', 'new_v_ffn_conv_w': 'new_v', 'new_v_ffn_conv_b': 'new_v', 'new_v_w_down': 'new_v', 'new_v_final_norm_g': 'new_v'}


def _forward(args):
    return _fwd_reference(*[args[k] for k in FWD_PARAMS])


def _output_shape():
    out = _jax.eval_shape(lambda: _forward(_fwd_setup_inputs(0)))
    return out.shape, out.dtype

N_MICROBATCH = 1
ADAM_LR = 0.001
ADAM_B1 = 0.9
ADAM_B2 = 0.999
ADAM_EPS = 1e-08
ADAM_WD = 0.01
ADAM_STEP = 10
PER_EXAMPLE_BATCH_AXIS = {'x': 0, 'mem': 0, 'loss_target': 0}
SHARED_INPUTS = []
_WEIGHT_DTYPES = {'rel_bias': _jnp.float32, 'norm_mix_g': _jnp.float32, 'norm_mem_g': _jnp.float32, 'w_mem_kv': _jnp.float32, 'w_out': _jnp.float32, 'w_in_a': _jnp.float32, 'sinks_a': _jnp.float32, 'w_in_b': _jnp.float32, 'conv_qkv_b': _jnp.float32, 'a_log_b': _jnp.float32, 'dt_bias_b': _jnp.float32, 'out_norm_g_b': _jnp.float32, 'norm_ffn_g': _jnp.float32, 'w_gate_up': _jnp.float32, 'ffn_conv_w': _jnp.float32, 'ffn_conv_b': _jnp.float32, 'w_down': _jnp.float32, 'final_norm_g': _jnp.float32}
MOMENT_SCALE = {'rel_bias': 6.143604e-02, 'norm_mix_g': 1.045121e-01, 'norm_mem_g': 1.722540e-02, 'w_mem_kv': 2.193998e-02, 'w_out': 6.313466e-02, 'w_in_a': 6.707576e-02, 'sinks_a': 3.106374e-02, 'w_in_b': 8.245359e-02, 'conv_qkv_b': 8.534375e-02, 'a_log_b': 3.673356e-01, 'dt_bias_b': 3.504897e-01, 'out_norm_g_b': 2.038794e-01, 'norm_ffn_g': 1.543603e-01, 'w_gate_up': 6.001873e-02, 'ffn_conv_w': 6.127365e-02, 'ffn_conv_b': 5.732835e-02, 'w_down': 9.810880e-02, 'final_norm_g': 3.193387e+01}


def _to_microbatches(a, axis):
    t = _jnp.moveaxis(a, axis, 0)
    t = t.reshape((N_MICROBATCH, t.shape[0] // N_MICROBATCH) + t.shape[1:])
    return _jnp.moveaxis(t, 1, axis + 1)


def setup_inputs(seed: int = 0) -> dict:
    inp = _fwd_setup_inputs(seed)
    key = _jax.random.fold_in(_jax.random.key(seed), 7919)
    shape, _ = _output_shape()
    out = dict(inp)
    out["loss_target"] = _jax.random.normal(_jax.random.fold_in(key, 0), shape, _jnp.float32)
    for i, name in enumerate(TWIN_WEIGHTS):
        w = inp[name].astype(_jnp.float32)
        if MOMENT_SCALE is None:
            s = _jnp.sqrt(_jnp.mean(_jnp.square(w)) + 1e-30)
        else:
            s = MOMENT_SCALE[name]
        km, kv = _jax.random.split(_jax.random.fold_in(key, i + 1))
        out[name] = w
        out["m_" + name] = s * _jax.random.normal(km, w.shape, _jnp.float32)
        out["v_" + name] = (s * s) * _jax.random.uniform(kv, w.shape, _jnp.float32, 0.5, 1.5)
    if N_MICROBATCH > 1:
        for name, axis in PER_EXAMPLE_BATCH_AXIS.items():
            out[name] = _to_microbatches(out[name], axis)
    return {'x': out['x'], 'mem': out['mem'], 'rel_bias': out['rel_bias'], 'norm_mix_g': out['norm_mix_g'], 'norm_mem_g': out['norm_mem_g'], 'w_mem_kv': out['w_mem_kv'], 'w_out': out['w_out'], 'w_in_a': out['w_in_a'], 'sinks_a': out['sinks_a'], 'w_in_b': out['w_in_b'], 'conv_qkv_b': out['conv_qkv_b'], 'a_log_b': out['a_log_b'], 'dt_bias_b': out['dt_bias_b'], 'out_norm_g_b': out['out_norm_g_b'], 'norm_ffn_g': out['norm_ffn_g'], 'w_gate_up': out['w_gate_up'], 'ffn_conv_w': out['ffn_conv_w'], 'ffn_conv_b': out['ffn_conv_b'], 'w_down': out['w_down'], 'final_norm_g': out['final_norm_g'], 'loss_target': out['loss_target'], 'm_rel_bias': out['m_rel_bias'], 'm_norm_mix_g': out['m_norm_mix_g'], 'm_norm_mem_g': out['m_norm_mem_g'], 'm_w_mem_kv': out['m_w_mem_kv'], 'm_w_out': out['m_w_out'], 'm_w_in_a': out['m_w_in_a'], 'm_sinks_a': out['m_sinks_a'], 'm_w_in_b': out['m_w_in_b'], 'm_conv_qkv_b': out['m_conv_qkv_b'], 'm_a_log_b': out['m_a_log_b'], 'm_dt_bias_b': out['m_dt_bias_b'], 'm_out_norm_g_b': out['m_out_norm_g_b'], 'm_norm_ffn_g': out['m_norm_ffn_g'], 'm_w_gate_up': out['m_w_gate_up'], 'm_ffn_conv_w': out['m_ffn_conv_w'], 'm_ffn_conv_b': out['m_ffn_conv_b'], 'm_w_down': out['m_w_down'], 'm_final_norm_g': out['m_final_norm_g'], 'v_rel_bias': out['v_rel_bias'], 'v_norm_mix_g': out['v_norm_mix_g'], 'v_norm_mem_g': out['v_norm_mem_g'], 'v_w_mem_kv': out['v_w_mem_kv'], 'v_w_out': out['v_w_out'], 'v_w_in_a': out['v_w_in_a'], 'v_sinks_a': out['v_sinks_a'], 'v_w_in_b': out['v_w_in_b'], 'v_conv_qkv_b': out['v_conv_qkv_b'], 'v_a_log_b': out['v_a_log_b'], 'v_dt_bias_b': out['v_dt_bias_b'], 'v_out_norm_g_b': out['v_out_norm_g_b'], 'v_norm_ffn_g': out['v_norm_ffn_g'], 'v_w_gate_up': out['v_w_gate_up'], 'v_ffn_conv_w': out['v_ffn_conv_w'], 'v_ffn_conv_b': out['v_ffn_conv_b'], 'v_w_down': out['v_w_down'], 'v_final_norm_g': out['v_final_norm_g']}


def _loss(weights, diff, rest, loss_target):
    with _jax.named_scope("forward"):
        args = {**rest, TWIN_DIFF_INPUT: diff, **{k: w.astype(_WEIGHT_DTYPES[k]) for k, w in weights.items()}}
        y = _forward(args)
    with _jax.named_scope("loss_head"):
        err = _jnp.square(y.astype(_jnp.float32) - loss_target)
        return 0.5 * _jnp.sum(_jnp.mean(err, axis=-1)) if err.ndim else 0.5 * err


def _adamw(w, g, m, v):
    m = ADAM_B1 * m + (1.0 - ADAM_B1) * g
    v = ADAM_B2 * v + (1.0 - ADAM_B2) * _jnp.square(g)
    m_hat = m / (1.0 - ADAM_B1 ** ADAM_STEP)
    v_hat = v / (1.0 - ADAM_B2 ** ADAM_STEP)
    delta = -ADAM_LR * (m_hat / (_jnp.sqrt(v_hat) + ADAM_EPS) + ADAM_WD * w)
    return delta, m, v


def reference(x, mem, rel_bias, norm_mix_g, norm_mem_g, w_mem_kv, w_out, w_in_a, sinks_a, w_in_b, conv_qkv_b, a_log_b, dt_bias_b, out_norm_g_b, norm_ffn_g, w_gate_up, ffn_conv_w, ffn_conv_b, w_down, final_norm_g, loss_target, m_rel_bias, m_norm_mix_g, m_norm_mem_g, m_w_mem_kv, m_w_out, m_w_in_a, m_sinks_a, m_w_in_b, m_conv_qkv_b, m_a_log_b, m_dt_bias_b, m_out_norm_g_b, m_norm_ffn_g, m_w_gate_up, m_ffn_conv_w, m_ffn_conv_b, m_w_down, m_final_norm_g, v_rel_bias, v_norm_mix_g, v_norm_mem_g, v_w_mem_kv, v_w_out, v_w_in_a, v_sinks_a, v_w_in_b, v_conv_qkv_b, v_a_log_b, v_dt_bias_b, v_out_norm_g_b, v_norm_ffn_g, v_w_gate_up, v_ffn_conv_w, v_ffn_conv_b, v_w_down, v_final_norm_g):
    given = dict(x=x, mem=mem, rel_bias=rel_bias, norm_mix_g=norm_mix_g, norm_mem_g=norm_mem_g, w_mem_kv=w_mem_kv, w_out=w_out, w_in_a=w_in_a, sinks_a=sinks_a, w_in_b=w_in_b, conv_qkv_b=conv_qkv_b, a_log_b=a_log_b, dt_bias_b=dt_bias_b, out_norm_g_b=out_norm_g_b, norm_ffn_g=norm_ffn_g, w_gate_up=w_gate_up, ffn_conv_w=ffn_conv_w, ffn_conv_b=ffn_conv_b, w_down=w_down, final_norm_g=final_norm_g, loss_target=loss_target, m_rel_bias=m_rel_bias, m_norm_mix_g=m_norm_mix_g, m_norm_mem_g=m_norm_mem_g, m_w_mem_kv=m_w_mem_kv, m_w_out=m_w_out, m_w_in_a=m_w_in_a, m_sinks_a=m_sinks_a, m_w_in_b=m_w_in_b, m_conv_qkv_b=m_conv_qkv_b, m_a_log_b=m_a_log_b, m_dt_bias_b=m_dt_bias_b, m_out_norm_g_b=m_out_norm_g_b, m_norm_ffn_g=m_norm_ffn_g, m_w_gate_up=m_w_gate_up, m_ffn_conv_w=m_ffn_conv_w, m_ffn_conv_b=m_ffn_conv_b, m_w_down=m_w_down, m_final_norm_g=m_final_norm_g, v_rel_bias=v_rel_bias, v_norm_mix_g=v_norm_mix_g, v_norm_mem_g=v_norm_mem_g, v_w_mem_kv=v_w_mem_kv, v_w_out=v_w_out, v_w_in_a=v_w_in_a, v_sinks_a=v_sinks_a, v_w_in_b=v_w_in_b, v_conv_qkv_b=v_conv_qkv_b, v_a_log_b=v_a_log_b, v_dt_bias_b=v_dt_bias_b, v_out_norm_g_b=v_out_norm_g_b, v_norm_ffn_g=v_norm_ffn_g, v_w_gate_up=v_w_gate_up, v_ffn_conv_w=v_ffn_conv_w, v_ffn_conv_b=v_ffn_conv_b, v_w_down=v_w_down, v_final_norm_g=v_final_norm_g)
    weights = {n: given[n] for n in TWIN_WEIGHTS}
    shared = {n: given[n] for n in SHARED_INPUTS}
    per_example = {n: given[n] for n in ['x', 'mem']}
    grad_fn = _jax.value_and_grad(_loss, argnums=(0, 1))

    def one_microbatch(ex, loss_target):
        ex = dict(ex)
        diff = ex.pop(TWIN_DIFF_INPUT)
        return grad_fn(weights, diff, {**shared, **ex}, loss_target)

    if N_MICROBATCH == 1:
        loss, (grad_w, grad_x) = one_microbatch(per_example, given["loss_target"])
    else:
        def body(carry, xs):
            loss_sum, grad_sum = carry
            l_k, (gw_k, gx_k) = one_microbatch(xs[0], xs[1])
            with _jax.named_scope("update"):
                return (loss_sum + l_k, _jax.tree.map(_jnp.add, grad_sum, gw_k)), gx_k

        init = (_jnp.zeros((), _jnp.float32), _jax.tree.map(_jnp.zeros_like, weights))
        (loss, grad_w), grad_x = _jax.lax.scan(body, init, (per_example, given["loss_target"]))
    with _jax.named_scope("update"):
        delta_w, new_m, new_v = {}, {}, {}
        for n in TWIN_WEIGHTS:
            delta_w[n], new_m[n], new_v[n] = _adamw(weights[n], grad_w[n], given["m_" + n], given["v_" + n])
    return (loss, grad_x, *[grad_w[n] for n in TWIN_WEIGHTS], *[delta_w[n] for n in TWIN_WEIGHTS],
            *[new_m[n] for n in TWIN_WEIGHTS], *[new_v[n] for n in TWIN_WEIGHTS])
```

```python
import functools
import math

import numpy as np
import jax
import jax.numpy as jnp
from jax import lax
from jax.experimental import pallas as pl
from jax.experimental.pallas import tpu as pltpu

f32 = jnp.float32
bf16 = jnp.bfloat16
HI = lax.Precision.HIGHEST
MESH = pl.DeviceIdType.MESH

D = 1024
MEM_LEN = 256
EPS = 1e-6
A_HEADS, A_KV, A_DH = 12, 2, 64
A_Q = 768
BLK = 128
N_BUCKETS, MAX_DIST = 32, 128
B_QK, B_V, B_DH = 384, 768, 128
B_QKV = 1536
CHUNK = 64
X_Q = 256
D_FF = 2816
IN_A = 1280
IN_B = 2572
IN_B_PAD = 2688
LANE = 128
VMEM_LIMIT = 56 * 1024 * 1024

LR, B1, B2, AEPS, WD, STEP = 0.001, 0.9, 0.999, 1e-08, 0.01, 10


def _cp(sem=None):
    return pltpu.CompilerParams(dimension_semantics=sem, vmem_limit_bytes=VMEM_LIMIT)


def _dg(a, b, ca, cb, prec=None):
    return lax.dot_general(a, b, (((ca,), (cb,)), ((), ())), precision=prec, preferred_element_type=f32)


@jax.custom_vjp
def bdot(a, b):
    return _dg(a.astype(bf16), b.astype(bf16), 1, 0)


def _bdot_f(a, b):
    return bdot(a, b), (a, b)


def _bdot_b(res, g):
    a, b = res
    gb = g.astype(bf16)
    return _dg(gb, b.astype(bf16), 1, 1), _dg(a.astype(bf16), gb, 0, 0)


bdot.defvjp(_bdot_f, _bdot_b)


@jax.custom_vjp
def bdot_nt(a, b):
    return _dg(a.astype(bf16), b.astype(bf16), 1, 1)


def _bdot_nt_f(a, b):
    return bdot_nt(a, b), (a, b)


def _bdot_nt_b(res, g):
    a, b = res
    gb = g.astype(bf16)
    return _dg(gb, b.astype(bf16), 1, 0), _dg(gb, a.astype(bf16), 0, 0)


bdot_nt.defvjp(_bdot_nt_f, _bdot_nt_b)


def _shift_rows(x, s, down):
    n = x.shape[0]
    row = lax.broadcasted_iota(jnp.int32, x.shape, 0)
    if down:
        return jnp.where(row >= s, pltpu.roll(x, s, 0), 0.0)
    return jnp.where(row < n - s, pltpu.roll(x, n - s, 0), 0.0)


@functools.partial(jax.custom_vjp, nondiff_argnums=(1,))
def shift_down(x, s):
    return _shift_rows(x, s, True)


def _sd_f(x, s):
    return _shift_rows(x, s, True), None


def _sd_b(s, _, g):
    return (_shift_rows(g, s, False),)


shift_down.defvjp(_sd_f, _sd_b)


def _sigmoid(x):
    return 1.0 / (1.0 + jnp.exp(-x))


def _silu(x):
    return x * _sigmoid(x)


def _rms(x, g):
    return x * lax.rsqrt(jnp.mean(x * x, axis=-1, keepdims=True) + EPS) * g


def _tile(n, cap):
    u = n // LANE
    best = 1
    for d in range(1, u + 1):
        if u % d == 0 and d * LANE <= cap:
            best = d
    return best * LANE


def mm_nn(a, w, res=None, out_dtype=f32, name="mm_nn"):
    M, K = a.shape
    N = w.shape[1]
    tm, tn = min(512, M), _tile(N, 640)

    def body(*refs):
        if res is None:
            a_ref, w_ref, o_ref = refs
            o_ref[...] = _dg(a_ref[...].astype(bf16), w_ref[...], 1, 0).astype(out_dtype)
        else:
            a_ref, w_ref, r_ref, o_ref = refs
            o_ref[...] = (r_ref[...] + _dg(a_ref[...].astype(bf16), w_ref[...], 1, 0)).astype(out_dtype)

    in_specs = [pl.BlockSpec((tm, K), lambda n, m: (m, 0)), pl.BlockSpec((K, tn), lambda n, m: (0, n))]
    args = [a, w]
    if res is not None:
        in_specs.append(pl.BlockSpec((tm, tn), lambda n, m: (m, n)))
        args.append(res)
    return pl.pallas_call(
        body, name=name, grid=(N // tn, M // tm), in_specs=in_specs,
        out_specs=pl.BlockSpec((tm, tn), lambda n, m: (m, n)),
        out_shape=jax.ShapeDtypeStruct((M, N), out_dtype),
        compiler_params=_cp(("parallel", "parallel")),
    )(*args)


def mm_nt(dy, w, name="mm_nt"):
    M, N = dy.shape
    K = w.shape[0]
    tm, tn = min(512, M), _tile(N, 512)

    def body(dy_ref, w_ref, o_ref):
        @pl.when(pl.program_id(1) == 0)
        def _():
            o_ref[...] = jnp.zeros_like(o_ref)
        o_ref[...] += _dg(dy_ref[...].astype(bf16), w_ref[...], 1, 1)

    return pl.pallas_call(
        body, name=name, grid=(M // tm, N // tn),
        in_specs=[pl.BlockSpec((tm, tn), lambda m, n: (m, n)), pl.BlockSpec((K, tn), lambda m, n: (0, n))],
        out_specs=pl.BlockSpec((tm, K), lambda m, n: (m, 0)),
        out_shape=jax.ShapeDtypeStruct((M, K), f32),
        compiler_params=_cp(("parallel", "arbitrary")),
    )(dy, w)


def mm_tn(a, dy, name="mm_tn"):
    M, K = a.shape
    N = dy.shape[1]
    tm, tk, tn = min(512, M), _tile(K, 1024), _tile(N, 640)

    def body(a_ref, dy_ref, o_ref):
        @pl.when(pl.program_id(2) == 0)
        def _():
            o_ref[...] = jnp.zeros_like(o_ref)
        o_ref[...] += _dg(a_ref[...].astype(bf16), dy_ref[...].astype(bf16), 0, 0)

    return pl.pallas_call(
        body, name=name, grid=(K // tk, N // tn, M // tm),
        in_specs=[pl.BlockSpec((tm, tk), lambda k, n, m: (m, k)), pl.BlockSpec((tm, tn), lambda k, n, m: (m, n))],
        out_specs=pl.BlockSpec((tk, tn), lambda k, n, m: (k, n)),
        out_shape=jax.ShapeDtypeStruct((K, N), f32),
        compiler_params=_cp(("parallel", "parallel", "arbitrary")),
    )(a, dy)


def rms_fwd(h, g, name):
    S = h.shape[0]
    t = min(512, S)

    def body(h_ref, g_ref, o_ref):
        o_ref[...] = _rms(h_ref[...], g_ref[...]).astype(bf16)

    return pl.pallas_call(
        body, name=name, grid=(S // t,),
        in_specs=[pl.BlockSpec((t, D), lambda i: (i, 0)), pl.BlockSpec((1, D), lambda i: (0, 0))],
        out_specs=pl.BlockSpec((t, D), lambda i: (i, 0)),
        out_shape=jax.ShapeDtypeStruct((S, D), bf16),
        compiler_params=_cp(("parallel",)),
    )(h, g.reshape(1, D))


def rms_bwd(h, g, dn, dres, name):
    S = h.shape[0]
    t = min(512, S)

    def body(h_ref, g_ref, dn_ref, dr_ref, dh_ref, dg_ref):
        @pl.when(pl.program_id(0) == 0)
        def _():
            dg_ref[...] = jnp.zeros_like(dg_ref)
        _, vjp = jax.vjp(_rms, h_ref[...], g_ref[...])
        dh, dg = vjp(dn_ref[...])
        dh_ref[...] = dr_ref[...] + dh
        dg_ref[...] += dg

    tok = pl.BlockSpec((t, D), lambda i: (i, 0))
    vec = pl.BlockSpec((1, D), lambda i: (0, 0))
    return pl.pallas_call(
        body, name=name, grid=(S // t,), in_specs=[tok, vec, tok, tok], out_specs=[tok, vec],
        out_shape=[jax.ShapeDtypeStruct((S, D), f32), jax.ShapeDtypeStruct((1, D), f32)],
        compiler_params=_cp(("arbitrary",)),
    )(h, g.reshape(1, D), dn, dres)


def loss_head(h, g, target):
    S = h.shape[0]
    t = min(512, S)

    def f(hh, gg, tt):
        err = _rms(hh, gg) - tt
        return 0.5 * jnp.sum(jnp.mean(err * err, axis=-1, keepdims=True), axis=0, keepdims=True)

    def body(h_ref, g_ref, t_ref, loss_ref, dh_ref, dg_ref):
        @pl.when(pl.program_id(0) == 0)
        def _():
            dg_ref[...] = jnp.zeros_like(dg_ref)
            loss_ref[...] = jnp.zeros_like(loss_ref)
        val, vjp = jax.vjp(lambda a, b: f(a, b, t_ref[...]), h_ref[...], g_ref[...])
        dh, dg = vjp(jnp.ones((1, 1), f32))
        dh_ref[...] = dh
        dg_ref[...] += dg
        loss_ref[...] += jnp.broadcast_to(val, loss_ref.shape)

    tok = pl.BlockSpec((t, D), lambda i: (i, 0))
    vec = pl.BlockSpec((1, D), lambda i: (0, 0))
    return pl.pallas_call(
        body, name="loss_head", grid=(S // t,), in_specs=[tok, vec, tok],
        out_specs=[pl.BlockSpec((1, LANE), lambda i: (0, 0)), tok, vec],
        out_shape=[jax.ShapeDtypeStruct((1, LANE), f32), jax.ShapeDtypeStruct((S, D), f32),
                   jax.ShapeDtypeStruct((1, D), f32)],
        compiler_params=_cp(("arbitrary",)),
    )(h, g.reshape(1, D), target)


def memkv_fwd(mem, g, w, name):
    def body(m_ref, g_ref, w_ref, o_ref):
        o_ref[...] = _dg(_rms(m_ref[...], g_ref[...]).astype(bf16), w_ref[...], 1, 0)

    return pl.pallas_call(
        body, name=name, out_shape=jax.ShapeDtypeStruct((MEM_LEN, 2 * X_Q), f32), compiler_params=_cp(),
    )(mem, g.reshape(1, D), w)


def memkv_bwd(mem, g, w, dkv, name):
    def body(m_ref, g_ref, w_ref, d_ref, dg_ref, dw_ref):
        n, vjp = jax.vjp(lambda gg: _rms(m_ref[...], gg), g_ref[...])
        db = d_ref[...].astype(bf16)
        dw_ref[...] = _dg(n.astype(bf16), db, 0, 0)
        dg_ref[...] = vjp(_dg(db, w_ref[...], 1, 1))[0]

    return pl.pallas_call(
        body, name=name,
        out_shape=[jax.ShapeDtypeStruct((1, D), f32), jax.ShapeDtypeStruct((D, 2 * X_Q), f32)],
        compiler_params=_cp(),
    )(mem, g.reshape(1, D), w, dkv)


def _xattn_f(xq, mk, mv):
    lane = lax.broadcasted_iota(jnp.int32, (1, X_Q), 1)
    out = jnp.zeros(xq.shape, f32)
    for hd in range(4):
        msk = (lane // 64 == hd).astype(f32)
        s = bdot_nt(xq * msk, mk) * (64 ** -0.5)
        m = lax.stop_gradient(jnp.max(s, axis=-1, keepdims=True))
        p = jnp.exp(s - m)
        p = p / jnp.sum(p, axis=-1, keepdims=True)
        out = out + bdot(p, mv * msk)
    return out


def xattn_fwd(proj, col, kv, name):
    S = proj.shape[0]
    t = min(512, S)
    cb = col // X_Q

    def body(q_ref, k_ref, v_ref, o_ref):
        o_ref[...] = _xattn_f(q_ref[...], k_ref[...], v_ref[...])

    return pl.pallas_call(
        body, name=name, grid=(S // t,),
        in_specs=[pl.BlockSpec((t, X_Q), lambda i: (i, cb)), pl.BlockSpec((MEM_LEN, X_Q), lambda i: (0, 0)),
                  pl.BlockSpec((MEM_LEN, X_Q), lambda i: (0, 1))],
        out_specs=pl.BlockSpec((t, X_Q), lambda i: (i, 0)),
        out_shape=jax.ShapeDtypeStruct((S, X_Q), f32),
        compiler_params=_cp(("parallel",)),
    )(proj, kv, kv)


def xattn_bwd(proj, col, kv, dmix, name):
    S = proj.shape[0]
    t = min(512, S)
    cb = col // X_Q

    def body(q_ref, k_ref, v_ref, do_ref, dq_ref, dk_ref, dv_ref):
        @pl.when(pl.program_id(0) == 0)
        def _():
            dk_ref[...] = jnp.zeros_like(dk_ref)
            dv_ref[...] = jnp.zeros_like(dv_ref)
        _, vjp = jax.vjp(_xattn_f, q_ref[...], k_ref[...], v_ref[...])
        dq, dk, dv = vjp(do_ref[...])
        dq_ref[...] = dq
        dk_ref[...] += dk
        dv_ref[...] += dv

    kvb = pl.BlockSpec((MEM_LEN, X_Q), lambda i: (0, 0))
    dq, dk, dv = pl.pallas_call(
        body, name=name, grid=(S // t,),
        in_specs=[pl.BlockSpec((t, X_Q), lambda i: (i, cb)), kvb,
                  pl.BlockSpec((MEM_LEN, X_Q), lambda i: (0, 1)), pl.BlockSpec((t, X_Q), lambda i: (i, 3))],
        out_specs=[pl.BlockSpec((t, X_Q), lambda i: (i, 0)), kvb, kvb],
        out_shape=[jax.ShapeDtypeStruct((S, X_Q), f32), jax.ShapeDtypeStruct((MEM_LEN, X_Q), f32),
                   jax.ShapeDtypeStruct((MEM_LEN, X_Q), f32)],
        compiler_params=_cp(("arbitrary",)),
    )(proj, kv, kv, dmix)
    return dq, jnp.concatenate([dk, dv], axis=1)


def _bucket_map():
    qi = np.arange(BLK)[:, None]
    kj = np.arange(2 * BLK)[None, :]
    n = np.maximum(BLK + qi - kj, 0)
    max_exact = N_BUCKETS // 2
    nf = np.maximum(n, 1).astype(np.float64)
    large = max_exact + (np.log(nf / max_exact) / math.log(MAX_DIST / max_exact)
                         * (N_BUCKETS - max_exact)).astype(np.int32)
    large = np.minimum(large, N_BUCKETS - 1)
    return np.where(n < max_exact, n, large).astype(np.int32)


def bias_build(rel_bias):
    def body(rb_ref, bk_ref, o_ref):
        bk = bk_ref[...]
        for h in range(A_HEADS):
            acc = jnp.zeros((BLK, 2 * BLK), f32)
            for b in range(N_BUCKETS):
                acc = jnp.where(bk == b, rb_ref[b, h], acc)
            o_ref[h] = acc

    return pl.pallas_call(
        body, name="bias_build",
        in_specs=[pl.BlockSpec(memory_space=pltpu.SMEM), pl.BlockSpec(memory_space=pltpu.VMEM)],
        out_specs=pl.BlockSpec(memory_space=pltpu.VMEM),
        out_shape=jax.ShapeDtypeStruct((A_HEADS, BLK, 2 * BLK), f32), compiler_params=_cp(),
    )(rel_bias, jnp.asarray(_bucket_map()))


def bias_grad(dbias):
    def body(d_ref, bk_ref, o_ref):
        bk = bk_ref[...]
        row = lax.broadcasted_iota(jnp.int32, (N_BUCKETS, LANE), 0)
        lane = lax.broadcasted_iota(jnp.int32, (N_BUCKETS, LANE), 1)
        acc = jnp.zeros((N_BUCKETS, LANE), f32)
        for h in range(A_HEADS):
            d = d_ref[h]
            for b in range(N_BUCKETS):
                s = jnp.sum(jnp.where(bk == b, d, 0.0), keepdims=True)
                acc = acc + jnp.where((row == b) & (lane == h), s, 0.0)
        o_ref[...] = acc

    return pl.pallas_call(
        body, name="bias_grad", out_shape=jax.ShapeDtypeStruct((N_BUCKETS, LANE), f32), compiler_params=_cp(),
    )(dbias, jnp.asarray(_bucket_map()))


def _swa_f(qb, kp, kc, vp, vc, bias, sk, first):
    kband = jnp.concatenate([kp, kc], axis=0)
    vband = jnp.concatenate([vp, vc], axis=0)
    qi = lax.broadcasted_iota(jnp.int32, (BLK, 2 * BLK), 0)
    kj = lax.broadcasted_iota(jnp.int32, (BLK, 2 * BLK), 1)
    rel = kj - qi
    ok = (rel >= 1) & (rel <= BLK) & ((kj >= BLK) | jnp.logical_not(first))
    lane = lax.broadcasted_iota(jnp.int32, (1, LANE), 1)
    lane_b = lax.broadcasted_iota(jnp.int32, (BLK, LANE), 1)
    outs = []
    for p in range(A_HEADS // 2):
        qp = qb[:, LANE * p:LANE * (p + 1)]
        acc = jnp.zeros((BLK, LANE), f32)
        for g in range(2):
            h = g * (A_HEADS // 2) + p
            msk = (lane // A_DH == g).astype(f32)
            s = bdot_nt(qp * msk, kband) * (A_DH ** -0.5) + bias[h]
            s = jnp.where(ok, s, -1e30)
            skb = jnp.broadcast_to(sk[h:h + 1, :], (BLK, LANE))
            sink = jnp.sum(jnp.where(lane_b == 0, skb, 0.0), axis=-1, keepdims=True)
            m = lax.stop_gradient(jnp.maximum(jnp.max(s, axis=-1, keepdims=True), sink))
            e = jnp.exp(s - m)
            prob = e / (jnp.sum(e, axis=-1, keepdims=True) + jnp.exp(sink - m))
            acc = acc + bdot(prob, vband) * msk
        outs.append(acc)
    return jnp.concatenate(outs, axis=1)


def _swa_specs(nb, rev):
    bi = (lambda i: nb - 1 - i) if rev else (lambda i: i)
    return [
        pl.BlockSpec((BLK, A_Q), lambda i: (bi(i), 0)),
        pl.BlockSpec((BLK, LANE), lambda i: (jnp.maximum(bi(i) - 1, 0), 6)),
        pl.BlockSpec((BLK, LANE), lambda i: (bi(i), 6)),
        pl.BlockSpec((BLK, LANE), lambda i: (jnp.maximum(bi(i) - 1, 0), 7)),
        pl.BlockSpec((BLK, LANE), lambda i: (bi(i), 7)),
        pl.BlockSpec((A_HEADS, BLK, 2 * BLK), lambda i: (0, 0, 0)),
        pl.BlockSpec((16, LANE), lambda i: (0, 0)),
    ]


def swa_fwd(proj, bias, sk):
    S = proj.shape[0]
    nb = S // BLK

    def body(q_ref, kp_ref, kc_ref, vp_ref, vc_ref, b_ref, s_ref, o_ref):
        o_ref[...] = _swa_f(q_ref[...], kp_ref[...], kc_ref[...], vp_ref[...], vc_ref[...], b_ref[...], s_ref[...],
                            pl.program_id(0) == 0)

    return pl.pallas_call(
        body, name="swa_fwd", grid=(nb,), in_specs=_swa_specs(nb, False),
        out_specs=pl.BlockSpec((BLK, A_Q), lambda i: (i, 0)),
        out_shape=jax.ShapeDtypeStruct((S, A_Q), f32), compiler_params=_cp(("parallel",)),
    )(proj, proj, proj, proj, proj, bias, sk)


def swa_bwd(proj, bias, sk, dmix):
    S = proj.shape[0]
    nb = S // BLK

    def body(q_ref, kp_ref, kc_ref, vp_ref, vc_ref, b_ref, s_ref, do_ref, dqkv_ref, db_ref, ds_ref, ck, cv):
        i = pl.program_id(0)

        @pl.when(i == 0)
        def _():
            db_ref[...] = jnp.zeros_like(db_ref)
            ds_ref[...] = jnp.zeros_like(ds_ref)
            ck[...] = jnp.zeros_like(ck)
            cv[...] = jnp.zeros_like(cv)
        first = i == nb - 1
        _, vjp = jax.vjp(lambda *a: _swa_f(*a, first), q_ref[...], kp_ref[...], kc_ref[...], vp_ref[...],
                         vc_ref[...], b_ref[...], s_ref[...])
        dq, dkp, dkc, dvp, dvc, db, ds = vjp(do_ref[...])
        dqkv_ref[...] = jnp.concatenate([dq, dkc + ck[...], dvc + cv[...]], axis=1)
        ck[...] = dkp
        cv[...] = dvp
        db_ref[...] += db
        ds_ref[...] += ds

    return pl.pallas_call(
        body, name="swa_bwd", grid=(nb,),
        in_specs=_swa_specs(nb, True) + [pl.BlockSpec((BLK, A_Q), lambda i: (nb - 1 - i, 0))],
        out_specs=[pl.BlockSpec((BLK, D), lambda i: (nb - 1 - i, 0)),
                   pl.BlockSpec((A_HEADS, BLK, 2 * BLK), lambda i: (0, 0, 0)),
                   pl.BlockSpec((16, LANE), lambda i: (0, 0))],
        out_shape=[jax.ShapeDtypeStruct((S, D), f32), jax.ShapeDtypeStruct((A_HEADS, BLK, 2 * BLK), f32),
                   jax.ShapeDtypeStruct((16, LANE), f32)],
        scratch_shapes=[pltpu.VMEM((BLK, LANE), f32), pltpu.VMEM((BLK, LANE), f32)],
        compiler_params=_cp(("arbitrary",)),
    )(proj, proj, proj, proj, proj, bias, sk, dmix)


def _dnprep_f(x, w, is_qk):
    c = (w[3:4] * x + w[2:3] * shift_down(x, 1) + w[1:2] * shift_down(x, 2) + w[0:1] * shift_down(x, 3))
    a = _silu(c)
    n = a * lax.rsqrt(jnp.sum(a * a, axis=-1, keepdims=True) + EPS)
    return jnp.where(is_qk, n, a)


def dnprep_fwd(proj, cw):
    S = proj.shape[0]
    nblk = B_QKV // LANE

    def body(x_ref, w_ref, o_ref):
        o_ref[...] = _dnprep_f(x_ref[...], w_ref[...], pl.program_id(0) < 2 * B_QK // LANE)

    return pl.pallas_call(
        body, name="dnprep_fwd", grid=(nblk,),
        in_specs=[pl.BlockSpec((S, LANE), lambda j: (0, j)), pl.BlockSpec((4, LANE), lambda j: (0, j))],
        out_specs=pl.BlockSpec((S, LANE), lambda j: (0, j)),
        out_shape=jax.ShapeDtypeStruct((S, B_QKV), f32), compiler_params=_cp(("parallel",)),
    )(proj, cw)


def dnprep_bwd(proj, cw, dqkvn):
    S = proj.shape[0]
    nblk = B_QKV // LANE

    def body(x_ref, w_ref, d_ref, dx_ref, dw_ref):
        is_qk = pl.program_id(0) < 2 * B_QK // LANE
        _, vjp = jax.vjp(lambda a, b: _dnprep_f(a, b, is_qk), x_ref[...], w_ref[...])
        dx, dw = vjp(d_ref[...])
        dx_ref[...] = dx
        dw_ref[...] = dw

    col = pl.BlockSpec((S, LANE), lambda j: (0, j))
    wsp = pl.BlockSpec((4, LANE), lambda j: (0, j))
    return pl.pallas_call(
        body, name="dnprep_bwd", grid=(nblk,), in_specs=[col, wsp, col], out_specs=[col, wsp],
        out_shape=[jax.ShapeDtypeStruct((S, B_QKV), f32), jax.ShapeDtypeStruct((4, B_QKV), f32)],
        compiler_params=_cp(("parallel",)),
    )(proj, cw, dqkvn)


def _hdot(a, b, ca=1, cb=0):
    return _dg(a, b, ca, cb, HI)


def _chunk_f(S0, q, k, v, g, beta):
    C = CHUNK
    r = lax.broadcasted_iota(jnp.int32, (C, C), 0)
    c = lax.broadcasted_iota(jnp.int32, (C, C), 1)
    incl = r >= c
    strict = r > c
    eye = (r == c).astype(f32)
    gc = _hdot(incl.astype(f32), jnp.broadcast_to(g, (C, LANE)))[:, :1]
    g_row = _hdot(jnp.ones((C, C), f32), eye * gc)
    decay = jnp.where(incl, jnp.exp(jnp.where(incl, gc - g_row, 0.0)), 0.0)
    a_mat = beta * _hdot(k, k, 1, 1) * jnp.where(strict, decay, 0.0)
    eg = jnp.exp(gc)
    pw = -a_mat
    inv = eye + pw
    for _ in range(5):
        pw = _hdot(pw, pw)
        inv = inv + _hdot(inv, pw)
    u = _hdot(inv, beta * v)
    w = _hdot(inv, (beta * eg) * k)
    qc = q * (B_DH ** -0.5)
    attn = _hdot(qc, k, 1, 1) * decay
    last = (lax.broadcasted_iota(jnp.int32, (C, 1), 0) == C - 1).astype(f32)
    g_last = jnp.sum(gc * last, axis=0, keepdims=True)
    delta = u - _hdot(w, S0)
    out = _hdot(qc * eg, S0) + _hdot(attn, delta)
    S1 = jnp.exp(g_last) * S0 + _hdot(k * jnp.exp(g_last - gc), delta, 0, 0)
    return out, S1


def _dn_f(S0, q, k, v, z, seg, prm, h):
    lane = lax.broadcasted_iota(jnp.int32, (1, LANE), 1)
    selb = (lane == h).astype(f32)
    sela = (lane == h + 6).astype(f32)
    bl = jnp.sum(seg * selb, axis=-1, keepdims=True)
    al = jnp.sum(seg * sela, axis=-1, keepdims=True)
    a_log = jnp.sum(prm[0:1] * selb, axis=-1, keepdims=True)
    dtb = jnp.sum(prm[1:2] * selb, axis=-1, keepdims=True)
    beta = _sigmoid(bl)
    xx = al + dtb
    g = -jnp.exp(a_log) * (jnp.maximum(xx, 0.0) + jnp.log(1.0 + jnp.exp(-jnp.abs(xx))))
    out, S1 = _chunk_f(S0, q, k, v, g, beta)
    o = out * lax.rsqrt(jnp.mean(out * out, axis=-1, keepdims=True) + EPS) * prm[2:3]
    return o * _silu(z), S1


def _dn_specs(nc, rev):
    ci = (lambda n: nc - 1 - n) if rev else (lambda n: n)
    return [
        pl.BlockSpec((CHUNK, LANE), lambda n, h: (ci(n), h // 2)),
        pl.BlockSpec((CHUNK, LANE), lambda n, h: (ci(n), 3 + h // 2)),
        pl.BlockSpec((CHUNK, LANE), lambda n, h: (ci(n), 6 + h)),
        pl.BlockSpec((CHUNK, LANE), lambda n, h: (ci(n), 12 + h)),
        pl.BlockSpec((CHUNK, LANE), lambda n, h: (ci(n), 20)),
        pl.BlockSpec((8, LANE), lambda n, h: (0, 0)),
    ]


def dn_fwd(qkvn, proj, prm):
    S = proj.shape[0]
    nc = S // CHUNK

    def body(q_ref, k_ref, v_ref, z_ref, s_ref, p_ref, y_ref, st_ref, st):
        h = pl.program_id(1)

        @pl.when(pl.program_id(0) == 0)
        def _():
            st[h] = jnp.zeros((B_DH, B_DH), f32)
        S0 = st[h]
        st_ref[0, 0] = S0
        y, S1 = _dn_f(S0, q_ref[...], k_ref[...], v_ref[...], z_ref[...], s_ref[...], p_ref[...], h)
        y_ref[...] = y
        st[h] = S1

    return pl.pallas_call(
        body, name="dn_fwd", grid=(nc, 6), in_specs=_dn_specs(nc, False),
        out_specs=[pl.BlockSpec((CHUNK, LANE), lambda n, h: (n, h)),
                   pl.BlockSpec((1, 1, B_DH, B_DH), lambda n, h: (n, h, 0, 0))],
        out_shape=[jax.ShapeDtypeStruct((S, B_V), f32), jax.ShapeDtypeStruct((nc, 6, B_DH, B_DH), f32)],
        scratch_shapes=[pltpu.VMEM((6, B_DH, B_DH), f32)],
        compiler_params=_cp(("arbitrary", "arbitrary")),
    )(qkvn, qkvn, qkvn, proj, proj, prm)


def dn_bwd(qkvn, proj, prm, states, dmix):
    S = proj.shape[0]
    nc = S // CHUNK

    def body(q_ref, k_ref, v_ref, z_ref, s_ref, p_ref, st_ref, dy_ref,
             dq_ref, dk_ref, dv_ref, dz_ref, dseg_ref, dprm_ref, dst):
        n = pl.program_id(0)
        h = pl.program_id(1)

        @pl.when(n == 0)
        def _():
            dst[h] = jnp.zeros((B_DH, B_DH), f32)

        @pl.when((n == 0) & (h == 0))
        def _():
            dprm_ref[...] = jnp.zeros_like(dprm_ref)
        _, vjp = jax.vjp(lambda *a: _dn_f(*a, h), st_ref[0, 0], q_ref[...], k_ref[...], v_ref[...], z_ref[...],
                         s_ref[...], p_ref[...])
        dS0, dq, dk, dv, dz, dseg, dprm = vjp((dy_ref[...], dst[h]))
        dst[h] = dS0
        dv_ref[...] = dv
        dz_ref[...] = dz
        dprm_ref[...] += dprm

        @pl.when(h % 2 == 0)
        def _():
            dq_ref[...] = dq
            dk_ref[...] = dk

        @pl.when(h % 2 == 1)
        def _():
            dq_ref[...] += dq
            dk_ref[...] += dk

        @pl.when(h == 0)
        def _():
            dseg_ref[...] = dseg

        @pl.when(h != 0)
        def _():
            dseg_ref[...] += dseg

    ci = lambda n: nc - 1 - n
    blk = (CHUNK, LANE)
    return pl.pallas_call(
        body, name="dn_bwd", grid=(nc, 6),
        in_specs=_dn_specs(nc, True) + [pl.BlockSpec((1, 1, B_DH, B_DH), lambda n, h: (ci(n), h, 0, 0)),
                                        pl.BlockSpec(blk, lambda n, h: (ci(n), h))],
        out_specs=[pl.BlockSpec(blk, lambda n, h: (ci(n), h // 2)), pl.BlockSpec(blk, lambda n, h: (ci(n), h // 2)),
                   pl.BlockSpec(blk, lambda n, h: (ci(n), h)), pl.BlockSpec(blk, lambda n, h: (ci(n), h)),
                   pl.BlockSpec(blk, lambda n, h: (ci(n), 0)), pl.BlockSpec((8, LANE), lambda n, h: (0, 0))],
        out_shape=[jax.ShapeDtypeStruct((S, B_QK), f32), jax.ShapeDtypeStruct((S, B_QK), f32),
                   jax.ShapeDtypeStruct((S, B_V), f32), jax.ShapeDtypeStruct((S, B_V), f32),
                   jax.ShapeDtypeStruct((S, LANE), f32), jax.ShapeDtypeStruct((8, LANE), f32)],
        scratch_shapes=[pltpu.VMEM((6, B_DH, B_DH), f32)],
        compiler_params=_cp(("arbitrary", "arbitrary")),
    )(qkvn, qkvn, qkvn, proj, proj, prm, states, dmix)


def _glu_f(gu, w, b):
    gate, up = gu[:, :LANE], gu[:, LANE:]
    c = w[2:3] * gate + w[1:2] * shift_down(gate, 1) + w[0:1] * shift_down(gate, 2) + b
    return _silu(c) * up


def glu_fwd(gu, w, b, name):
    S = gu.shape[0]
    nblk = D_FF // LANE

    def body(g_ref, w_ref, b_ref, o_ref):
        o_ref[...] = _glu_f(g_ref[...], w_ref[...], b_ref[...]).astype(bf16)

    return pl.pallas_call(
        body, name=name, grid=(nblk,),
        in_specs=[pl.BlockSpec((S, 2 * LANE), lambda j: (0, j)), pl.BlockSpec((3, LANE), lambda j: (0, j)),
                  pl.BlockSpec((1, LANE), lambda j: (0, j))],
        out_specs=pl.BlockSpec((S, LANE), lambda j: (0, j)),
        out_shape=jax.ShapeDtypeStruct((S, D_FF), bf16), compiler_params=_cp(("parallel",)),
    )(gu, w, b.reshape(1, D_FF))


def glu_bwd(gu, w, b, dact, name):
    S = gu.shape[0]
    nblk = D_FF // LANE

    def body(g_ref, w_ref, b_ref, d_ref, dg_ref, dw_ref, db_ref):
        _, vjp = jax.vjp(_glu_f, g_ref[...], w_ref[...], b_ref[...])
        dg, dw, db = vjp(d_ref[...])
        dg_ref[...] = dg.astype(bf16)
        dw_ref[...] = dw
        db_ref[...] = db

    gsp = pl.BlockSpec((S, 2 * LANE), lambda j: (0, j))
    wsp = pl.BlockSpec((3, LANE), lambda j: (0, j))
    bsp = pl.BlockSpec((1, LANE), lambda j: (0, j))
    return pl.pallas_call(
        body, name=name, grid=(nblk,),
        in_specs=[gsp, wsp, bsp, pl.BlockSpec((S, LANE), lambda j: (0, j))], out_specs=[gsp, wsp, bsp],
        out_shape=[jax.ShapeDtypeStruct((S, 2 * D_FF), bf16), jax.ShapeDtypeStruct((3, D_FF), f32),
                   jax.ShapeDtypeStruct((1, D_FF), f32)],
        compiler_params=_cp(("parallel",)),
    )(gu, w, b.reshape(1, D_FF), dact)


def _pair_cols(w):
    lead = w.shape[:-1]
    return w.reshape(lead + (2, 6, A_DH)).swapaxes(-3, -2).reshape(lead + (A_Q,))


def _unpair_cols(w):
    lead = w.shape[:-1]
    return w.reshape(lead + (6, 2, A_DH)).swapaxes(-3, -2).reshape(lead + (A_Q,))


def _lay_in_a(w):
    return jnp.concatenate([_pair_cols(w[:, :A_Q]), w[:, A_Q:]], axis=1)


def _unlay_in_a(w):
    return jnp.concatenate([_unpair_cols(w[:, :A_Q]), w[:, A_Q:]], axis=1)


def _lay_out_a(w):
    return jnp.concatenate([_pair_cols(w[:A_Q].T).T, w[A_Q:]], axis=0)


def _unlay_out_a(w):
    return jnp.concatenate([_unpair_cols(w[:A_Q].T).T, w[A_Q:]], axis=0)


def _lay_in_b(w):
    return jnp.concatenate([w[:, :2304], w[:, 2316:], w[:, 2304:2316],
                            jnp.zeros((w.shape[0], LANE - 12), w.dtype)], axis=1)


def _unlay_in_b(w):
    return jnp.concatenate([w[:, :2304], w[:, 2560:2572], w[:, 2304:2560]], axis=1)


def _lay_gu(w):
    return w.reshape(D, 2, D_FF // LANE, LANE).swapaxes(1, 2).reshape(D, 2 * D_FF)


def _unlay_gu(w):
    return w.reshape(D, D_FF // LANE, 2, LANE).swapaxes(1, 2).reshape(D, 2 * D_FF)


def _local_step(x, mem, target, P):
    sk = jnp.zeros((16, LANE), f32).at[:A_HEADS].set(jnp.broadcast_to(P["sinks"][:, None], (A_HEADS, LANE)))
    prm = jnp.zeros((8, LANE), f32).at[0, :6].set(P["a_log"]).at[1, :6].set(P["dt_bias"]).at[2].set(P["out_norm_g"])
    bias = bias_build(P["rel_bias"])
    saved = []
    h = x
    for i in range(2):
        n1 = rms_fwd(h, P["g_mix"][i], f"rms_mix{i}")
        kv = memkv_fwd(mem, P["g_mem"][i], P["w_mem"][i], f"memkv{i}")
        if i == 0:
            proj = mm_nn(n1, P["w_in_a"], name="proj_a")
            self_out = swa_fwd(proj, bias, sk)
            cross = xattn_fwd(proj, A_Q + 2 * LANE, kv, "xattn_a")
            extra = ()
        else:
            proj = mm_nn(n1, P["w_in_b"], name="proj_b")
            qkvn = dnprep_fwd(proj, P["conv_qkv"])
            self_out, states = dn_fwd(qkvn, proj, prm)
            cross = xattn_fwd(proj, 2304, kv, "xattn_b")
            extra = (qkvn, states)
        mix = jnp.concatenate([self_out, cross], axis=1)
        h2 = mm_nn(mix, P["w_out"][i], res=h, name=f"out_proj{i}")
        n2 = rms_fwd(h2, P["g_ffn"][i], f"rms_ffn{i}")
        gu = mm_nn(n2, P["w_gu"][i], name=f"gate_up{i}")
        act = glu_fwd(gu, P["ffn_cw"][i], P["ffn_cb"][i], f"glu{i}")
        h3 = mm_nn(act, P["w_down"][i], res=h2, name=f"down{i}")
        saved.append((h, n1, kv, proj, mix, h2, n2, gu, act, extra))
        h = h3

    loss, dh, dg_fin = loss_head(h, P["g_fin"], target)
    G = {"g_fin": dg_fin[0], "g_mix": [None, None], "g_mem": [None, None], "g_ffn": [None, None],
         "w_mem": [None, None], "w_out": [None, None], "w_gu": [None, None], "w_down": [None, None],
         "ffn_cw": [None, None], "ffn_cb": [None, None]}
    for i in (1, 0):
        hin, n1, kv, proj, mix, h2, n2, gu, act, extra = saved[i]
        dact = mm_nt(dh, P["w_down"][i], name=f"d_act{i}")
        G["w_down"][i] = mm_tn(act, dh, name=f"dw_down{i}")
        dgu, dcw, dcb = glu_bwd(gu, P["ffn_cw"][i], P["ffn_cb"][i], dact, f"glu_bwd{i}")
        G["ffn_cw"][i], G["ffn_cb"][i] = dcw, dcb[0]
        dn2 = mm_nt(dgu, P["w_gu"][i], name=f"d_n2_{i}")
        G["w_gu"][i] = mm_tn(n2, dgu, name=f"dw_gu{i}")
        dh2, dg = rms_bwd(h2, P["g_ffn"][i], dn2, dh, f"rms_ffn_bwd{i}")
        G["g_ffn"][i] = dg[0]
        dmix = mm_nt(dh2, P["w_out"][i], name=f"d_mix{i}")
        G["w_out"][i] = mm_tn(mix, dh2, name=f"dw_out{i}")
        if i == 0:
            dqkv, dbias, dsk = swa_bwd(proj, bias, sk, dmix)
            dxq, dkv = xattn_bwd(proj, A_Q + 2 * LANE, kv, dmix, "xattn_a_bwd")
            dproj = jnp.concatenate([dqkv, dxq], axis=1)
            G["sinks"] = dsk[:A_HEADS, 0]
            G["rel_bias"] = bias_grad(dbias)[:, :A_HEADS]
            w_in, gname = P["w_in_a"], "w_in_a"
        else:
            qkvn, states = extra
            dq, dk, dv, dz, dseg, dprm = dn_bwd(qkvn, proj, prm, states, dmix)
            draw, dconv = dnprep_bwd(proj, P["conv_qkv"], jnp.concatenate([dq, dk, dv], axis=1))
            dxq, dkv = xattn_bwd(proj, 2304, kv, dmix, "xattn_b_bwd")
            dproj = jnp.concatenate([draw, dz, dxq, dseg], axis=1)
            G["conv_qkv"] = dconv
            G["a_log"], G["dt_bias"], G["out_norm_g"] = dprm[0, :6], dprm[1, :6], dprm[2]
            w_in, gname = P["w_in_b"], "w_in_b"
        dn1 = mm_nt(dproj, w_in, name=f"d_n1_{i}")
        G[gname] = mm_tn(n1, dproj, name=f"d{gname}")
        dh, dg = rms_bwd(hin, P["g_mix"][i], dn1, dh2, f"rms_mix_bwd{i}")
        G["g_mix"][i] = dg[0]
        dgm, dwm = memkv_bwd(mem, P["g_mem"][i], P["w_mem"][i], dkv, f"memkv_bwd{i}")
        G["g_mem"][i], G["w_mem"][i] = dgm[0], dwm
    return loss, dh, G


def _prepare(full):
    return {
        "rel_bias": full["rel_bias"], "sinks": full["sinks_a"][0], "a_log": full["a_log_b"][0],
        "dt_bias": full["dt_bias_b"][0], "out_norm_g": full["out_norm_g_b"][0],
        "g_mix": full["norm_mix_g"], "g_mem": full["norm_mem_g"], "g_ffn": full["norm_ffn_g"],
        "g_fin": full["final_norm_g"], "conv_qkv": full["conv_qkv_b"][0],
        "ffn_cw": [full["ffn_conv_w"][0], full["ffn_conv_w"][1]],
        "ffn_cb": [full["ffn_conv_b"][0], full["ffn_conv_b"][1]],
        "w_mem": [full["w_mem_kv"][0], full["w_mem_kv"][1]],
        "w_out": [_lay_out_a(full["w_out"][0]), full["w_out"][1]],
        "w_in_a": _lay_in_a(full["w_in_a"][0]), "w_in_b": _lay_in_b(full["w_in_b"][0]),
        "w_gu": [_lay_gu(full["w_gate_up"][0]), _lay_gu(full["w_gate_up"][1])],
        "w_down": [full["w_down"][0], full["w_down"][1]],
    }


def _grads_to_ref(G):
    return {
        "rel_bias": G["rel_bias"], "norm_mix_g": jnp.stack(G["g_mix"]), "norm_mem_g": jnp.stack(G["g_mem"]),
        "w_mem_kv": jnp.stack(G["w_mem"]),
        "w_out": jnp.stack([_unlay_out_a(G["w_out"][0]), G["w_out"][1]]),
        "w_in_a": _unlay_in_a(G["w_in_a"])[None], "sinks_a": G["sinks"][None],
        "w_in_b": _unlay_in_b(G["w_in_b"])[None], "conv_qkv_b": G["conv_qkv"][None],
        "a_log_b": G["a_log"][None], "dt_bias_b": G["dt_bias"][None], "out_norm_g_b": G["out_norm_g"][None],
        "norm_ffn_g": jnp.stack(G["g_ffn"]),
        "w_gate_up": jnp.stack([_unlay_gu(G["w_gu"][0]), _unlay_gu(G["w_gu"][1])]),
        "ffn_conv_w": jnp.stack(G["ffn_cw"]), "ffn_conv_b": jnp.stack(G["ffn_cb"]),
        "w_down": jnp.stack(G["w_down"]), "final_norm_g": G["g_fin"],
    }


ANY = pl.BlockSpec(memory_space=pl.ANY)


def _place():
    return lax.axis_index("x"), lax.axis_index("y"), lax.axis_index("c")


def _chip_exchange(bufs, out_shapes, src_of, name):
    n = len(bufs)

    def body(*refs):
        ins, outs = refs[:n], refs[n:2 * n]
        ssem, rsem, lsem = refs[2 * n:]
        x, y, c = _place()
        me = 2 * x + y
        peers = [(1 - x, y), (x, 1 - y), (1 - x, 1 - y)]

        def remote(j, k, slot):
            px, py = peers[k]
            return pltpu.make_async_remote_copy(
                src_ref=src_of(j, ins[j], 2 * px + py), dst_ref=outs[j].at[slot],
                send_sem=ssem.at[3 * j + k], recv_sem=rsem.at[3 * j + k],
                device_id=(px, py, c), device_id_type=MESH)

        started = []
        for j in range(n):
            lc = pltpu.make_async_copy(src_of(j, ins[j], me), outs[j].at[me], lsem.at[j])
            lc.start()
            started.append(lc)
        sends = [remote(j, k, me) for j in range(n) for k in range(3)]
        for cp in sends:
            cp.start()
        for j in range(n):
            for k in range(3):
                px, py = peers[k]
                remote(j, k, 2 * px + py).wait_recv()
        for cp in sends:
            cp.wait_send()
        for lc in started:
            lc.wait()

    return pl.pallas_call(
        body, name=name, in_specs=[ANY] * n, out_specs=[ANY] * n, out_shape=out_shapes,
        scratch_shapes=[pltpu.SemaphoreType.DMA((3 * n,)), pltpu.SemaphoreType.DMA((3 * n,)),
                        pltpu.SemaphoreType.DMA((n,))],
    )(*bufs)


def allgather_chips(bufs):
    shapes = [jax.ShapeDtypeStruct((4,) + b.shape, b.dtype) for b in bufs]
    return _chip_exchange(bufs, shapes, lambda j, ref, chip: ref, "weight_allgather")


def chip_scatter(g):
    return _chip_exchange([g], [jax.ShapeDtypeStruct(g.shape, g.dtype)], lambda j, ref, chip: ref.at[chip],
                          "grad_scatter")[0]


def sibling_exchange(p):
    def body(p_ref, o_ref, ssem, rsem):
        x, y, c = _place()
        cp = pltpu.make_async_remote_copy(src_ref=p_ref, dst_ref=o_ref, send_sem=ssem, recv_sem=rsem,
                                          device_id=(x, y, 1 - c), device_id_type=MESH)
        cp.start()
        cp.wait()

    return pl.pallas_call(
        body, name="sibling_exchange", in_specs=[ANY], out_specs=ANY,
        out_shape=jax.ShapeDtypeStruct(p.shape, p.dtype),
        scratch_shapes=[pltpu.SemaphoreType.DMA, pltpu.SemaphoreType.DMA],
    )(p)


def allreduce_small(buf):
    R = buf.shape[0]

    def body(b_ref, o_ref, recv, ssem, rsem):
        x, y, c = _place()
        me = 4 * x + 2 * y + c

        def peer(k):
            return (1 - x if k & 4 else x, 1 - y if k & 2 else y, 1 - c if k & 1 else c)

        def remote(k, slot):
            return pltpu.make_async_remote_copy(
                src_ref=b_ref, dst_ref=recv.at[slot], send_sem=ssem.at[k - 1], recv_sem=rsem.at[k - 1],
                device_id=peer(k), device_id_type=MESH)

        sends = [remote(k, me) for k in range(1, 8)]
        for cp in sends:
            cp.start()
        recv[me] = b_ref[...]
        for k in range(1, 8):
            px, py, pc = peer(k)
            remote(k, 4 * px + 2 * py + pc).wait_recv()
        for cp in sends:
            cp.wait_send()
        total = recv[0]
        for j in range(1, 8):
            total = total + recv[j]
        o_ref[...] = total

    return pl.pallas_call(
        body, name="small_allreduce",
        in_specs=[pl.BlockSpec(memory_space=pltpu.VMEM)], out_specs=pl.BlockSpec(memory_space=pltpu.VMEM),
        out_shape=jax.ShapeDtypeStruct(buf.shape, f32),
        scratch_shapes=[pltpu.VMEM((8, R, LANE), f32), pltpu.SemaphoreType.DMA((7,)), pltpu.SemaphoreType.DMA((7,))],
    )(buf)


def sum_slots(recv):
    _, R, C = recv.shape
    tr = 512

    def body(r_ref, o_ref):
        acc = r_ref[0].astype(f32)
        for s in range(1, 4):
            acc = acc + r_ref[s].astype(f32)
        o_ref[...] = acc

    return pl.pallas_call(
        body, name="sum_slots", grid=(R // tr,),
        in_specs=[pl.BlockSpec((4, tr, C), lambda i: (0, i, 0))], out_specs=pl.BlockSpec((tr, C), lambda i: (i, 0)),
        out_shape=jax.ShapeDtypeStruct((R, C), f32), compiler_params=_cp(("parallel",)),
    )(recv)


def _adamw_math(w, g, m, v):
    m = B1 * m + (1.0 - B1) * g
    v = B2 * v + (1.0 - B2) * (g * g)
    m_hat = m / (1.0 - B1 ** STEP)
    v_hat = v / (1.0 - B2 ** STEP)
    delta = -LR * (m_hat / (jnp.sqrt(v_hat) + AEPS) + WD * w)
    return delta, m, v


def adamw_rows(w, m, v, ga, gb, row0, name):
    rows, C = w.shape
    tr = ADAM_ROWS
    b0 = row0 // tr

    def body(w_ref, m_ref, v_ref, ga_ref, gb_ref, g_ref, d_ref, nm_ref, nv_ref):
        g = ga_ref[...] + gb_ref[...]
        d, nm, nv = _adamw_math(w_ref[...], g, m_ref[...], v_ref[...])
        g_ref[...] = g
        d_ref[...] = d
        nm_ref[...] = nm
        nv_ref[...] = nv

    own = pl.BlockSpec((tr, C), lambda i: (i, 0))
    off = pl.BlockSpec((tr, C), lambda i: (b0 + i, 0))
    return pl.pallas_call(
        body, name=name, grid=(rows // tr,), in_specs=[own, own, own, off, off], out_specs=[own] * 4,
        out_shape=[jax.ShapeDtypeStruct((rows, C), f32)] * 4, compiler_params=_cp(("parallel",)),
    )(w, m, v, ga, gb)


def adamw_small(w, m, v, g):
    def body(w_ref, m_ref, v_ref, g_ref, d_ref, nm_ref, nv_ref):
        d, nm, nv = _adamw_math(w_ref[...], g_ref[...], m_ref[...], v_ref[...])
        d_ref[...] = d
        nm_ref[...] = nm
        nv_ref[...] = nv

    return pl.pallas_call(body, name="adamw_small", out_shape=[jax.ShapeDtypeStruct(w.shape, f32)] * 3)(w, m, v, g)


ADAM_ROWS = 64
PACK_C = 1024
BIG = (("w_mem_kv", 1), ("w_out", 1), ("w_in_a", 2), ("w_gate_up", 2), ("w_down", 1), ("w_in_b", 2))
CONV = (("conv_qkv_b", 2), ("ffn_conv_w", 2))
SMALL = ("rel_bias", "norm_mix_g", "norm_mem_g", "sinks_a", "a_log_b", "dt_bias_b", "out_norm_g_b", "norm_ffn_g",
         "ffn_conv_b", "final_norm_g")
WEIGHTS = ("rel_bias", "norm_mix_g", "norm_mem_g", "w_mem_kv", "w_out", "w_in_a", "sinks_a", "w_in_b", "conv_qkv_b",
           "a_log_b", "dt_bias_b", "out_norm_g_b", "norm_ffn_g", "w_gate_up", "ffn_conv_w", "ffn_conv_b", "w_down",
           "final_norm_g")
ARGS = ("x", "mem") + WEIGHTS + ("loss_target",) + tuple("m_" + n for n in WEIGHTS) + tuple("v_" + n for n in WEIGHTS)


def _rows(a, width):
    flat = a.reshape(-1)
    pad = (-flat.shape[0]) % width
    if pad:
        flat = jnp.concatenate([flat, jnp.zeros((pad,), a.dtype)])
    return flat.reshape(-1, width)


def _nrows(shape, width):
    return -(-math.prod(shape) // width)


def _pack(arrs, width, total_rows, dtype):
    parts = [_rows(a.astype(dtype), width) for a in arrs]
    used = sum(p.shape[0] for p in parts)
    if total_rows > used:
        parts.append(jnp.zeros((total_rows - used, width), dtype))
    return jnp.concatenate(parts, axis=0)


def _unpack(buf, shapes, width):
    out, r = [], 0
    for s in shapes:
        n = _nrows(s, width)
        out.append(buf[r:r + n].reshape(-1)[:math.prod(s)].reshape(s))
        r += n
    return out


def _pad_to(n, mult):
    return -(-n // mult) * mult


def _split_chips(full, axis):
    s = full.shape
    return jnp.moveaxis(full.reshape(s[:axis] + (4, s[axis] // 4) + s[axis + 1:]), axis, 0)


def _join_chips(parts, axis):
    m = jnp.moveaxis(parts, 0, axis)
    s = m.shape
    return m.reshape(s[:axis] + (4 * s[axis + 1],) + s[axis + 2:])


def kernel(x, mem, rel_bias, norm_mix_g, norm_mem_g, w_mem_kv, w_out, w_in_a, sinks_a, w_in_b, conv_qkv_b, a_log_b, dt_bias_b, out_norm_g_b, norm_ffn_g, w_gate_up, ffn_conv_w, ffn_conv_b, w_down, final_norm_g, loss_target, m_rel_bias, m_norm_mix_g, m_norm_mem_g, m_w_mem_kv, m_w_out, m_w_in_a, m_sinks_a, m_w_in_b, m_conv_qkv_b, m_a_log_b, m_dt_bias_b, m_out_norm_g_b, m_norm_ffn_g, m_w_gate_up, m_ffn_conv_w, m_ffn_conv_b, m_w_down, m_final_norm_g, v_rel_bias, v_norm_mix_g, v_norm_mem_g, v_w_mem_kv, v_w_out, v_w_in_a, v_sinks_a, v_w_in_b, v_conv_qkv_b, v_a_log_b, v_dt_bias_b, v_out_norm_g_b, v_norm_ffn_g, v_w_gate_up, v_ffn_conv_w, v_ffn_conv_b, v_w_down, v_final_norm_g):
    A = dict(zip(ARGS, (x, mem, rel_bias, norm_mix_g, norm_mem_g, w_mem_kv, w_out, w_in_a, sinks_a, w_in_b, conv_qkv_b, a_log_b, dt_bias_b, out_norm_g_b, norm_ffn_g, w_gate_up, ffn_conv_w, ffn_conv_b, w_down, final_norm_g, loss_target, m_rel_bias, m_norm_mix_g, m_norm_mem_g, m_w_mem_kv, m_w_out, m_w_in_a, m_sinks_a, m_w_in_b, m_conv_qkv_b, m_a_log_b, m_dt_bias_b, m_out_norm_g_b, m_norm_ffn_g, m_w_gate_up, m_ffn_conv_w, m_ffn_conv_b, m_w_down, m_final_norm_g, v_rel_bias, v_norm_mix_g, v_norm_mem_g, v_w_mem_kv, v_w_out, v_w_in_a, v_sinks_a, v_w_in_b, v_conv_qkv_b, v_a_log_b, v_dt_bias_b, v_out_norm_g_b, v_norm_ffn_g, v_w_gate_up, v_ffn_conv_w, v_ffn_conv_b, v_w_down, v_final_norm_g)))
    chip = 2 * lax.axis_index("x") + lax.axis_index("y")

    big_shapes = [A[n].shape for n, _ in BIG]
    big_rows = [_nrows(s, PACK_C) for s in big_shapes]
    big_off = [sum(big_rows[:j]) for j in range(len(BIG))]
    R = _pad_to(sum(big_rows), 512)
    conv_shapes = [A[n].shape for n, _ in CONV]
    conv_total = _pad_to(sum(_nrows(s, LANE) for s in conv_shapes), 8)
    wpack = _pack([A[n] for n, _ in BIG], PACK_C, R, bf16)
    cpack = _pack([A[n] for n, _ in CONV], LANE, conv_total, f32)
    gw, gc = allgather_chips([wpack, cpack])
    full = {n: A[n] for n in SMALL}
    parts = zip(*[_unpack(gw[s], big_shapes, PACK_C) for s in range(4)])
    for (n, axis), p in zip(BIG, parts):
        full[n] = _join_chips(jnp.stack(p), axis)
    parts = zip(*[_unpack(gc[s], conv_shapes, LANE) for s in range(4)])
    for (n, axis), p in zip(CONV, parts):
        full[n] = _join_chips(jnp.stack(p), axis)

    loss, dx, G = _local_step(x[0], mem[0], loss_target[0], _prepare(full))
    gfull = _grads_to_ref(G)

    gpack = jnp.concatenate(
        [_split_chips(gfull[n], axis).astype(bf16).reshape(4, -1, PACK_C) for n, axis in BIG]
        + [jnp.zeros((4, R - sum(big_rows), PACK_C), bf16)], axis=1)
    part = sum_slots(chip_scatter(gpack))
    sib = sibling_exchange(part)

    sm_shapes = [A[n].shape for n in SMALL] + [gfull[n].shape for n, _ in CONV] + [(LANE,)]
    sm_rows = _pad_to(sum(_nrows(s, LANE) for s in sm_shapes), 8)
    sbuf = _pack([gfull[n] for n in SMALL] + [gfull[n] for n, _ in CONV] + [loss[0]], LANE, sm_rows, f32)
    tot = _unpack(allreduce_small(sbuf), sm_shapes, LANE)
    gsmall = dict(zip(SMALL, tot[:len(SMALL)]))
    for (n, axis), t in zip(CONV, tot[len(SMALL):len(SMALL) + len(CONV)]):
        sh = A[n].shape[axis]
        gsmall[n] = lax.dynamic_slice_in_dim(t, chip * sh, sh, axis)
    loss_out = tot[-1][0]

    out = {}
    for (n, axis), rows, off in zip(BIG, big_rows, big_off):
        padded = _pad_to(rows, ADAM_ROWS)
        w2, m2, v2 = (_pack([A[p + n]], PACK_C, padded, f32) if padded != rows else A[p + n].reshape(rows, PACK_C)
                      for p in ("", "m_", "v_"))
        res = adamw_rows(w2, m2, v2, part, sib, off, "adamw_" + n)
        for key, r in zip(("grad_", "delta_", "new_m_", "new_v_"), res):
            out[key + n] = r[:rows].reshape(A[n].shape)
    names = SMALL + tuple(n for n, _ in CONV)
    shapes = [A[n].shape for n in names]
    rows = _pad_to(sum(_nrows(s, LANE) for s in shapes), 8)
    packs = [_pack([src[n] for n in names], LANE, rows, f32)
             for src in ({n: A[n] for n in names}, {n: A["m_" + n] for n in names}, {n: A["v_" + n] for n in names}, gsmall)]
    res = adamw_small(*packs)
    for key, r in zip(("delta_", "new_m_", "new_v_"), res):
        for n, a in zip(names, _unpack(r, shapes, LANE)):
            out[key + n] = a
    for n in names:
        out["grad_" + n] = gsmall[n]
    return (loss_out, dx[None], *[out["grad_" + n] for n in WEIGHTS], *[out["delta_" + n] for n in WEIGHTS],
            *[out["new_m_" + n] for n in WEIGHTS], *[out["new_v_" + n] for n in WEIGHTS])
```

```python
import functools
import math

import numpy as np
import jax
import jax.numpy as jnp
from jax import lax
from jax.experimental import pallas as pl
from jax.experimental.pallas import tpu as pltpu

f32 = jnp.float32
bf16 = jnp.bfloat16
HI = lax.Precision.HIGHEST
MESH = pl.DeviceIdType.MESH

D = 1024
MEM_LEN = 256
EPS = 1e-6
A_HEADS, A_KV, A_DH = 12, 2, 64
A_Q = 768
BLK = 128
N_BUCKETS, MAX_DIST = 32, 128
B_QK, B_V, B_DH = 384, 768, 128
B_QKV = 1536
CHUNK = 64
X_Q = 256
D_FF = 2816
IN_A = 1280
IN_B = 2572
IN_B_PAD = 2688
LANE = 128
VMEM_LIMIT = 56 * 1024 * 1024

LR, B1, B2, AEPS, WD, STEP = 0.001, 0.9, 0.999, 1e-08, 0.01, 10


def _cp(sem=None):
    return pltpu.CompilerParams(dimension_semantics=sem, vmem_limit_bytes=VMEM_LIMIT)


def _dg(a, b, ca, cb, prec=None):
    return lax.dot_general(a, b, (((ca,), (cb,)), ((), ())), precision=prec, preferred_element_type=f32)


@jax.custom_vjp
def bdot(a, b):
    return _dg(a.astype(bf16), b.astype(bf16), 1, 0)


def _bdot_f(a, b):
    return bdot(a, b), (a, b)


def _bdot_b(res, g):
    a, b = res
    gb = g.astype(bf16)
    return _dg(gb, b.astype(bf16), 1, 1), _dg(a.astype(bf16), gb, 0, 0)


bdot.defvjp(_bdot_f, _bdot_b)


@jax.custom_vjp
def bdot_nt(a, b):
    return _dg(a.astype(bf16), b.astype(bf16), 1, 1)


def _bdot_nt_f(a, b):
    return bdot_nt(a, b), (a, b)


def _bdot_nt_b(res, g):
    a, b = res
    gb = g.astype(bf16)
    return _dg(gb, b.astype(bf16), 1, 0), _dg(gb, a.astype(bf16), 0, 0)


bdot_nt.defvjp(_bdot_nt_f, _bdot_nt_b)


def _shift_rows(x, s, down):
    n = x.shape[0]
    row = lax.broadcasted_iota(jnp.int32, x.shape, 0)
    if down:
        return jnp.where(row >= s, pltpu.roll(x, s, 0), 0.0)
    return jnp.where(row < n - s, pltpu.roll(x, n - s, 0), 0.0)


@functools.partial(jax.custom_vjp, nondiff_argnums=(1,))
def shift_down(x, s):
    return _shift_rows(x, s, True)


def _sd_f(x, s):
    return _shift_rows(x, s, True), None


def _sd_b(s, _, g):
    return (_shift_rows(g, s, False),)


shift_down.defvjp(_sd_f, _sd_b)


def _sigmoid(x):
    return 1.0 / (1.0 + jnp.exp(-x))


def _silu(x):
    return x * _sigmoid(x)


def _rms(x, g):
    return x * lax.rsqrt(jnp.mean(x * x, axis=-1, keepdims=True) + EPS) * g


def _tile(n, cap):
    u = n // LANE
    best = 1
    for d in range(1, u + 1):
        if u % d == 0 and d * LANE <= cap:
            best = d
    return best * LANE


def mm_nn(a, w, res=None, out_dtype=f32, name="mm_nn"):
    M, K = a.shape
    N = w.shape[1]
    tm, tn = min(512, M), _tile(N, 640)

    def body(*refs):
        if res is None:
            a_ref, w_ref, o_ref = refs
            o_ref[...] = _dg(a_ref[...].astype(bf16), w_ref[...], 1, 0).astype(out_dtype)
        else:
            a_ref, w_ref, r_ref, o_ref = refs
            o_ref[...] = (r_ref[...] + _dg(a_ref[...].astype(bf16), w_ref[...], 1, 0)).astype(out_dtype)

    in_specs = [pl.BlockSpec((tm, K), lambda n, m: (m, 0)), pl.BlockSpec((K, tn), lambda n, m: (0, n))]
    args = [a, w]
    if res is not None:
        in_specs.append(pl.BlockSpec((tm, tn), lambda n, m: (m, n)))
        args.append(res)
    return pl.pallas_call(
        body, name=name, grid=(N // tn, M // tm), in_specs=in_specs,
        out_specs=pl.BlockSpec((tm, tn), lambda n, m: (m, n)),
        out_shape=jax.ShapeDtypeStruct((M, N), out_dtype),
        compiler_params=_cp(("parallel", "parallel")),
    )(*args)


def mm_nt(dy, w, name="mm_nt"):
    M, N = dy.shape
    K = w.shape[0]
    tm, tn = min(512, M), _tile(N, 512)

    def body(dy_ref, w_ref, o_ref):
        @pl.when(pl.program_id(1) == 0)
        def _():
            o_ref[...] = jnp.zeros_like(o_ref)
        o_ref[...] += _dg(dy_ref[...].astype(bf16), w_ref[...], 1, 1)

    return pl.pallas_call(
        body, name=name, grid=(M // tm, N // tn),
        in_specs=[pl.BlockSpec((tm, tn), lambda m, n: (m, n)), pl.BlockSpec((K, tn), lambda m, n: (0, n))],
        out_specs=pl.BlockSpec((tm, K), lambda m, n: (m, 0)),
        out_shape=jax.ShapeDtypeStruct((M, K), f32),
        compiler_params=_cp(("parallel", "arbitrary")),
    )(dy, w)


def mm_tn(a, dy, name="mm_tn"):
    M, K = a.shape
    N = dy.shape[1]
    tm, tk, tn = min(512, M), _tile(K, 1408), _tile(N, 1024)

    def body(a_ref, dy_ref, o_ref):
        @pl.when(pl.program_id(2) == 0)
        def _():
            o_ref[...] = jnp.zeros_like(o_ref)
        o_ref[...] += _dg(a_ref[...].astype(bf16), dy_ref[...].astype(bf16), 0, 0)

    return pl.pallas_call(
        body, name=name, grid=(K // tk, N // tn, M // tm),
        in_specs=[pl.BlockSpec((tm, tk), lambda k, n, m: (m, k)), pl.BlockSpec((tm, tn), lambda k, n, m: (m, n))],
        out_specs=pl.BlockSpec((tk, tn), lambda k, n, m: (k, n)),
        out_shape=jax.ShapeDtypeStruct((K, N), f32),
        compiler_params=_cp(("parallel", "parallel", "arbitrary")),
    )(a, dy)


def rms_fwd(h, g, name):
    S = h.shape[0]
    t = min(512, S)

    def body(h_ref, g_ref, o_ref):
        o_ref[...] = _rms(h_ref[...], g_ref[...]).astype(bf16)

    return pl.pallas_call(
        body, name=name, grid=(S // t,),
        in_specs=[pl.BlockSpec((t, D), lambda i: (i, 0)), pl.BlockSpec((1, D), lambda i: (0, 0))],
        out_specs=pl.BlockSpec((t, D), lambda i: (i, 0)),
        out_shape=jax.ShapeDtypeStruct((S, D), bf16),
        compiler_params=_cp(("parallel",)),
    )(h, g.reshape(1, D))


def rms_bwd(h, g, dn, dres, name):
    S = h.shape[0]
    t = min(512, S)

    def body(h_ref, g_ref, dn_ref, dr_ref, dh_ref, dg_ref):
        @pl.when(pl.program_id(0) == 0)
        def _():
            dg_ref[...] = jnp.zeros_like(dg_ref)
        _, vjp = jax.vjp(_rms, h_ref[...], g_ref[...])
        dh, dg = vjp(dn_ref[...])
        dh_ref[...] = dr_ref[...] + dh
        dg_ref[...] += dg

    tok = pl.BlockSpec((t, D), lambda i: (i, 0))
    vec = pl.BlockSpec((1, D), lambda i: (0, 0))
    return pl.pallas_call(
        body, name=name, grid=(S // t,), in_specs=[tok, vec, tok, tok], out_specs=[tok, vec],
        out_shape=[jax.ShapeDtypeStruct((S, D), f32), jax.ShapeDtypeStruct((1, D), f32)],
        compiler_params=_cp(("arbitrary",)),
    )(h, g.reshape(1, D), dn, dres)


def loss_head(h, g, target):
    S = h.shape[0]
    t = min(512, S)

    def f(hh, gg, tt):
        err = _rms(hh, gg) - tt
        return 0.5 * jnp.sum(jnp.mean(err * err, axis=-1, keepdims=True), axis=0, keepdims=True)

    def body(h_ref, g_ref, t_ref, loss_ref, dh_ref, dg_ref):
        @pl.when(pl.program_id(0) == 0)
        def _():
            dg_ref[...] = jnp.zeros_like(dg_ref)
            loss_ref[...] = jnp.zeros_like(loss_ref)
        val, vjp = jax.vjp(lambda a, b: f(a, b, t_ref[...]), h_ref[...], g_ref[...])
        dh, dg = vjp(jnp.ones((1, 1), f32))
        dh_ref[...] = dh
        dg_ref[...] += dg
        loss_ref[...] += jnp.broadcast_to(val, loss_ref.shape)

    tok = pl.BlockSpec((t, D), lambda i: (i, 0))
    vec = pl.BlockSpec((1, D), lambda i: (0, 0))
    return pl.pallas_call(
        body, name="loss_head", grid=(S // t,), in_specs=[tok, vec, tok],
        out_specs=[pl.BlockSpec((1, LANE), lambda i: (0, 0)), tok, vec],
        out_shape=[jax.ShapeDtypeStruct((1, LANE), f32), jax.ShapeDtypeStruct((S, D), f32),
                   jax.ShapeDtypeStruct((1, D), f32)],
        compiler_params=_cp(("arbitrary",)),
    )(h, g.reshape(1, D), target)


def memkv_fwd(mem, g, w, name):
    def body(m_ref, g_ref, w_ref, o_ref):
        o_ref[...] = _dg(_rms(m_ref[...], g_ref[...]).astype(bf16), w_ref[...], 1, 0)

    return pl.pallas_call(
        body, name=name, out_shape=jax.ShapeDtypeStruct((MEM_LEN, 2 * X_Q), f32), compiler_params=_cp(),
    )(mem, g.reshape(1, D), w)


def memkv_bwd(mem, g, w, dkv, name):
    def body(m_ref, g_ref, w_ref, d_ref, dg_ref, dw_ref):
        n, vjp = jax.vjp(lambda gg: _rms(m_ref[...], gg), g_ref[...])
        db = d_ref[...].astype(bf16)
        dw_ref[...] = _dg(n.astype(bf16), db, 0, 0)
        dg_ref[...] = vjp(_dg(db, w_ref[...], 1, 1))[0]

    return pl.pallas_call(
        body, name=name,
        out_shape=[jax.ShapeDtypeStruct((1, D), f32), jax.ShapeDtypeStruct((D, 2 * X_Q), f32)],
        compiler_params=_cp(),
    )(mem, g.reshape(1, D), w, dkv)


def _xattn_f(xq, mk, mv):
    lane = lax.broadcasted_iota(jnp.int32, (1, X_Q), 1)
    out = jnp.zeros(xq.shape, f32)
    for hd in range(4):
        msk = (lane // 64 == hd).astype(f32)
        s = bdot_nt(xq * msk, mk) * (64 ** -0.5)
        m = lax.stop_gradient(jnp.max(s, axis=-1, keepdims=True))
        p = jnp.exp(s - m)
        p = p / jnp.sum(p, axis=-1, keepdims=True)
        out = out + bdot(p, mv * msk)
    return out


def xattn_fwd(proj, col, kv, name):
    S = proj.shape[0]
    t = min(512, S)
    cb = col // X_Q

    def body(q_ref, k_ref, v_ref, o_ref):
        o_ref[...] = _xattn_f(q_ref[...], k_ref[...], v_ref[...])

    return pl.pallas_call(
        body, name=name, grid=(S // t,),
        in_specs=[pl.BlockSpec((t, X_Q), lambda i: (i, cb)), pl.BlockSpec((MEM_LEN, X_Q), lambda i: (0, 0)),
                  pl.BlockSpec((MEM_LEN, X_Q), lambda i: (0, 1))],
        out_specs=pl.BlockSpec((t, X_Q), lambda i: (i, 0)),
        out_shape=jax.ShapeDtypeStruct((S, X_Q), f32),
        compiler_params=_cp(("parallel",)),
    )(proj, kv, kv)


def xattn_bwd(proj, col, kv, dmix, name):
    S = proj.shape[0]
    t = min(512, S)
    cb = col // X_Q

    def body(q_ref, k_ref, v_ref, do_ref, dq_ref, dk_ref, dv_ref):
        @pl.when(pl.program_id(0) == 0)
        def _():
            dk_ref[...] = jnp.zeros_like(dk_ref)
            dv_ref[...] = jnp.zeros_like(dv_ref)
        _, vjp = jax.vjp(_xattn_f, q_ref[...], k_ref[...], v_ref[...])
        dq, dk, dv = vjp(do_ref[...])
        dq_ref[...] = dq
        dk_ref[...] += dk
        dv_ref[...] += dv

    kvb = pl.BlockSpec((MEM_LEN, X_Q), lambda i: (0, 0))
    dq, dk, dv = pl.pallas_call(
        body, name=name, grid=(S // t,),
        in_specs=[pl.BlockSpec((t, X_Q), lambda i: (i, cb)), kvb,
                  pl.BlockSpec((MEM_LEN, X_Q), lambda i: (0, 1)), pl.BlockSpec((t, X_Q), lambda i: (i, 3))],
        out_specs=[pl.BlockSpec((t, X_Q), lambda i: (i, 0)), kvb, kvb],
        out_shape=[jax.ShapeDtypeStruct((S, X_Q), f32), jax.ShapeDtypeStruct((MEM_LEN, X_Q), f32),
                   jax.ShapeDtypeStruct((MEM_LEN, X_Q), f32)],
        compiler_params=_cp(("arbitrary",)),
    )(proj, kv, kv, dmix)
    return dq, jnp.concatenate([dk, dv], axis=1)


def _bucket_map():
    qi = np.arange(BLK)[:, None]
    kj = np.arange(2 * BLK)[None, :]
    n = np.maximum(BLK + qi - kj, 0)
    max_exact = N_BUCKETS // 2
    nf = np.maximum(n, 1).astype(np.float64)
    large = max_exact + (np.log(nf / max_exact) / math.log(MAX_DIST / max_exact)
                         * (N_BUCKETS - max_exact)).astype(np.int32)
    large = np.minimum(large, N_BUCKETS - 1)
    return np.where(n < max_exact, n, large).astype(np.int32)


def bias_build(rel_bias):
    def body(rb_ref, bk_ref, o_ref):
        bk = bk_ref[...]
        for h in range(A_HEADS):
            acc = jnp.zeros((BLK, 2 * BLK), f32)
            for b in range(N_BUCKETS):
                acc = jnp.where(bk == b, rb_ref[b, h], acc)
            o_ref[h] = acc

    return pl.pallas_call(
        body, name="bias_build",
        in_specs=[pl.BlockSpec(memory_space=pltpu.SMEM), pl.BlockSpec(memory_space=pltpu.VMEM)],
        out_specs=pl.BlockSpec(memory_space=pltpu.VMEM),
        out_shape=jax.ShapeDtypeStruct((A_HEADS, BLK, 2 * BLK), f32), compiler_params=_cp(),
    )(rel_bias, jnp.asarray(_bucket_map()))


def bias_grad(dbias):
    def body(d_ref, bk_ref, o_ref):
        bk = bk_ref[...]
        row = lax.broadcasted_iota(jnp.int32, (N_BUCKETS, LANE), 0)
        lane = lax.broadcasted_iota(jnp.int32, (N_BUCKETS, LANE), 1)
        acc = jnp.zeros((N_BUCKETS, LANE), f32)
        for h in range(A_HEADS):
            d = d_ref[h]
            for b in range(N_BUCKETS):
                s = jnp.sum(jnp.where(bk == b, d, 0.0), keepdims=True)
                acc = acc + jnp.where((row == b) & (lane == h), s, 0.0)
        o_ref[...] = acc

    return pl.pallas_call(
        body, name="bias_grad", out_shape=jax.ShapeDtypeStruct((N_BUCKETS, LANE), f32), compiler_params=_cp(),
    )(dbias, jnp.asarray(_bucket_map()))


def _swa_f(qb, kp, kc, vp, vc, bias, sk, first):
    kband = jnp.concatenate([kp, kc], axis=0)
    vband = jnp.concatenate([vp, vc], axis=0)
    qi = lax.broadcasted_iota(jnp.int32, (BLK, 2 * BLK), 0)
    kj = lax.broadcasted_iota(jnp.int32, (BLK, 2 * BLK), 1)
    rel = kj - qi
    ok = (rel >= 1) & (rel <= BLK) & ((kj >= BLK) | jnp.logical_not(first))
    lane = lax.broadcasted_iota(jnp.int32, (1, LANE), 1)
    lane_b = lax.broadcasted_iota(jnp.int32, (BLK, LANE), 1)
    outs = []
    for p in range(A_HEADS // 2):
        qp = qb[:, LANE * p:LANE * (p + 1)]
        acc = jnp.zeros((BLK, LANE), f32)
        for g in range(2):
            h = g * (A_HEADS // 2) + p
            msk = (lane // A_DH == g).astype(f32)
            s = bdot_nt(qp * msk, kband) * (A_DH ** -0.5) + bias[h]
            s = jnp.where(ok, s, -1e30)
            skb = jnp.broadcast_to(sk[h:h + 1, :], (BLK, LANE))
            sink = jnp.sum(jnp.where(lane_b == 0, skb, 0.0), axis=-1, keepdims=True)
            m = lax.stop_gradient(jnp.maximum(jnp.max(s, axis=-1, keepdims=True), sink))
            e = jnp.exp(s - m)
            prob = e / (jnp.sum(e, axis=-1, keepdims=True) + jnp.exp(sink - m))
            acc = acc + bdot(prob, vband) * msk
        outs.append(acc)
    return jnp.concatenate(outs, axis=1)


def _swa_specs(nb, rev):
    bi = (lambda i: nb - 1 - i) if rev else (lambda i: i)
    return [
        pl.BlockSpec((BLK, A_Q), lambda i: (bi(i), 0)),
        pl.BlockSpec((BLK, LANE), lambda i: (jnp.maximum(bi(i) - 1, 0), 6)),
        pl.BlockSpec((BLK, LANE), lambda i: (bi(i), 6)),
        pl.BlockSpec((BLK, LANE), lambda i: (jnp.maximum(bi(i) - 1, 0), 7)),
        pl.BlockSpec((BLK, LANE), lambda i: (bi(i), 7)),
        pl.BlockSpec((A_HEADS, BLK, 2 * BLK), lambda i: (0, 0, 0)),
        pl.BlockSpec((16, LANE), lambda i: (0, 0)),
    ]


def swa_fwd(proj, bias, sk):
    S = proj.shape[0]
    nb = S // BLK

    def body(q_ref, kp_ref, kc_ref, vp_ref, vc_ref, b_ref, s_ref, o_ref):
        o_ref[...] = _swa_f(q_ref[...], kp_ref[...], kc_ref[...], vp_ref[...], vc_ref[...], b_ref[...], s_ref[...],
                            pl.program_id(0) == 0)

    return pl.pallas_call(
        body, name="swa_fwd", grid=(nb,), in_specs=_swa_specs(nb, False),
        out_specs=pl.BlockSpec((BLK, A_Q), lambda i: (i, 0)),
        out_shape=jax.ShapeDtypeStruct((S, A_Q), f32), compiler_params=_cp(("parallel",)),
    )(proj, proj, proj, proj, proj, bias, sk)


def swa_bwd(proj, bias, sk, dmix):
    S = proj.shape[0]
    nb = S // BLK

    def body(q_ref, kp_ref, kc_ref, vp_ref, vc_ref, b_ref, s_ref, do_ref, dqkv_ref, db_ref, ds_ref, ck, cv):
        i = pl.program_id(0)

        @pl.when(i == 0)
        def _():
            db_ref[...] = jnp.zeros_like(db_ref)
            ds_ref[...] = jnp.zeros_like(ds_ref)
            ck[...] = jnp.zeros_like(ck)
            cv[...] = jnp.zeros_like(cv)
        first = i == nb - 1
        _, vjp = jax.vjp(lambda *a: _swa_f(*a, first), q_ref[...], kp_ref[...], kc_ref[...], vp_ref[...],
                         vc_ref[...], b_ref[...], s_ref[...])
        dq, dkp, dkc, dvp, dvc, db, ds = vjp(do_ref[...])
        dqkv_ref[...] = jnp.concatenate([dq, dkc + ck[...], dvc + cv[...]], axis=1)
        ck[...] = dkp
        cv[...] = dvp
        db_ref[...] += db
        ds_ref[...] += ds

    return pl.pallas_call(
        body, name="swa_bwd", grid=(nb,),
        in_specs=_swa_specs(nb, True) + [pl.BlockSpec((BLK, A_Q), lambda i: (nb - 1 - i, 0))],
        out_specs=[pl.BlockSpec((BLK, D), lambda i: (nb - 1 - i, 0)),
                   pl.BlockSpec((A_HEADS, BLK, 2 * BLK), lambda i: (0, 0, 0)),
                   pl.BlockSpec((16, LANE), lambda i: (0, 0))],
        out_shape=[jax.ShapeDtypeStruct((S, D), f32), jax.ShapeDtypeStruct((A_HEADS, BLK, 2 * BLK), f32),
                   jax.ShapeDtypeStruct((16, LANE), f32)],
        scratch_shapes=[pltpu.VMEM((BLK, LANE), f32), pltpu.VMEM((BLK, LANE), f32)],
        compiler_params=_cp(("arbitrary",)),
    )(proj, proj, proj, proj, proj, bias, sk, dmix)


def _dnprep_f(x, w, is_qk):
    c = (w[3:4] * x + w[2:3] * shift_down(x, 1) + w[1:2] * shift_down(x, 2) + w[0:1] * shift_down(x, 3))
    a = _silu(c)
    n = a * lax.rsqrt(jnp.sum(a * a, axis=-1, keepdims=True) + EPS)
    return jnp.where(is_qk, n, a)


def dnprep_fwd(proj, cw):
    S = proj.shape[0]
    nblk = B_QKV // LANE

    def body(x_ref, w_ref, o_ref):
        o_ref[...] = _dnprep_f(x_ref[...], w_ref[...], pl.program_id(0) < 2 * B_QK // LANE)

    return pl.pallas_call(
        body, name="dnprep_fwd", grid=(nblk,),
        in_specs=[pl.BlockSpec((S, LANE), lambda j: (0, j)), pl.BlockSpec((4, LANE), lambda j: (0, j))],
        out_specs=pl.BlockSpec((S, LANE), lambda j: (0, j)),
        out_shape=jax.ShapeDtypeStruct((S, B_QKV), f32), compiler_params=_cp(("parallel",)),
    )(proj, cw)


def dnprep_bwd(proj, cw, dqkvn):
    S = proj.shape[0]
    nblk = B_QKV // LANE

    def body(x_ref, w_ref, d_ref, dx_ref, dw_ref):
        is_qk = pl.program_id(0) < 2 * B_QK // LANE
        _, vjp = jax.vjp(lambda a, b: _dnprep_f(a, b, is_qk), x_ref[...], w_ref[...])
        dx, dw = vjp(d_ref[...])
        dx_ref[...] = dx
        dw_ref[...] = dw

    col = pl.BlockSpec((S, LANE), lambda j: (0, j))
    wsp = pl.BlockSpec((4, LANE), lambda j: (0, j))
    return pl.pallas_call(
        body, name="dnprep_bwd", grid=(nblk,), in_specs=[col, wsp, col], out_specs=[col, wsp],
        out_shape=[jax.ShapeDtypeStruct((S, B_QKV), f32), jax.ShapeDtypeStruct((4, B_QKV), f32)],
        compiler_params=_cp(("parallel",)),
    )(proj, cw, dqkvn)


def _hdot(a, b, ca=1, cb=0):
    return _dg(a, b, ca, cb, HI)


def _dnc_f(q, k, v, seg, prm, h):
    C = CHUNK
    lane = lax.broadcasted_iota(jnp.int32, (1, LANE), 1)
    selb = (lane == h).astype(f32)
    sela = (lane == h + 6).astype(f32)
    beta = _sigmoid(jnp.sum(seg * selb, axis=-1, keepdims=True))
    xx = jnp.sum(seg * sela, axis=-1, keepdims=True) + jnp.sum(prm[1:2] * selb, axis=-1, keepdims=True)
    a_log = jnp.sum(prm[0:1] * selb, axis=-1, keepdims=True)
    g = -jnp.exp(a_log) * (jnp.maximum(xx, 0.0) + jnp.log(1.0 + jnp.exp(-jnp.abs(xx))))
    r = lax.broadcasted_iota(jnp.int32, (C, C), 0)
    c = lax.broadcasted_iota(jnp.int32, (C, C), 1)
    incl = r >= c
    strict = r > c
    eye = (r == c).astype(f32)
    gc = _hdot(incl.astype(f32), jnp.broadcast_to(g, (C, LANE)))[:, :1]
    g_row = _hdot(jnp.ones((C, C), f32), eye * gc)
    decay = jnp.where(incl, jnp.exp(jnp.where(incl, gc - g_row, 0.0)), 0.0)
    a_mat = beta * _hdot(k, k, 1, 1) * jnp.where(strict, decay, 0.0)
    eg = jnp.exp(gc)
    pw = -a_mat
    inv = eye + pw
    for _ in range(5):
        pw = _hdot(pw, pw)
        inv = inv + _hdot(inv, pw)
    u = _hdot(inv, beta * v)
    w = _hdot(inv, (beta * eg) * k)
    qc = q * (B_DH ** -0.5)
    attn = _hdot(qc, k, 1, 1) * decay
    last = (lax.broadcasted_iota(jnp.int32, (C, 1), 0) == C - 1).astype(f32)
    g_last = jnp.sum(gc * last, axis=0, keepdims=True)
    return u, w, qc * eg, k * jnp.exp(g_last - gc), attn, jnp.broadcast_to(jnp.exp(g_last), (1, LANE))


def _dns_f(S0, u, w, qd, kt, attn, dcrow):
    lane = lax.broadcasted_iota(jnp.int32, (1, LANE), 1)
    dc = jnp.sum(jnp.where(lane == 0, dcrow, 0.0), axis=-1, keepdims=True)
    delta = u - _hdot(w, S0)
    out = _hdot(qd, S0) + _hdot(attn, delta)
    return out, dc * S0 + _hdot(kt, delta, 0, 0)


def _dnpost_f(o, z, grow):
    outs = []
    for h in range(6):
        oh = o[:, LANE * h:LANE * (h + 1)]
        outs.append(oh * lax.rsqrt(jnp.mean(oh * oh, axis=-1, keepdims=True) + EPS) * grow
                    * _silu(z[:, LANE * h:LANE * (h + 1)]))
    return jnp.concatenate(outs, axis=1)


def _hs(h):
    return slice(LANE * h, LANE * (h + 1))


def _dnc_in_specs():
    return [
        pl.BlockSpec((CHUNK, B_QK), lambda n: (n, 0)),
        pl.BlockSpec((CHUNK, B_QK), lambda n: (n, 1)),
        pl.BlockSpec((CHUNK, B_V), lambda n: (n, 1)),
        pl.BlockSpec((CHUNK, LANE), lambda n: (n, 20)),
        pl.BlockSpec((8, LANE), lambda n: (0, 0)),
    ]


def _dnc_out_specs(rev_nc=None):
    ci = (lambda n: n) if rev_nc is None else (lambda n: rev_nc - 1 - n)
    wide = pl.BlockSpec((CHUNK, B_V), lambda n: (ci(n), 0))
    return [wide, wide, wide, wide, pl.BlockSpec((1, 6, CHUNK, CHUNK), lambda n: (ci(n), 0, 0, 0)),
            pl.BlockSpec((1, 8, LANE), lambda n: (ci(n), 0, 0))]


def _dnc_shapes(S):
    nc = S // CHUNK
    wide = jax.ShapeDtypeStruct((S, B_V), f32)
    return [wide, wide, wide, wide, jax.ShapeDtypeStruct((nc, 6, CHUNK, CHUNK), f32),
            jax.ShapeDtypeStruct((nc, 8, LANE), f32)]


def dnc_fwd(qkvn, proj, prm):
    S = proj.shape[0]

    def body(q_ref, k_ref, v_ref, s_ref, p_ref, u_ref, w_ref, qd_ref, kt_ref, at_ref, dc_ref):
        dc_ref[...] = jnp.zeros_like(dc_ref)
        for h in range(6):
            u, w, qd, kt, attn, dc = _dnc_f(q_ref[:, _hs(h // 2)], k_ref[:, _hs(h // 2)], v_ref[:, _hs(h)],
                                           s_ref[...], p_ref[...], h)
            u_ref[:, _hs(h)] = u
            w_ref[:, _hs(h)] = w
            qd_ref[:, _hs(h)] = qd
            kt_ref[:, _hs(h)] = kt
            at_ref[0, h] = attn
            dc_ref[0, h:h + 1, :] = dc

    return pl.pallas_call(
        body, name="dn_chunk_fwd", grid=(S // CHUNK,), in_specs=_dnc_in_specs(), out_specs=_dnc_out_specs(),
        out_shape=_dnc_shapes(S), compiler_params=_cp(("parallel",)),
    )(qkvn, qkvn, qkvn, proj, prm)


def dnc_bwd(qkvn, proj, prm, cots):
    S = proj.shape[0]

    def body(q_ref, k_ref, v_ref, s_ref, p_ref, du_ref, dw_ref, dqd_ref, dkt_ref, dat_ref, ddc_ref,
             dx_ref, dseg_ref, dprm_ref):
        @pl.when(pl.program_id(0) == 0)
        def _():
            dprm_ref[...] = jnp.zeros_like(dprm_ref)
        dqs, dks, dvs = [], [], []
        dseg = jnp.zeros((CHUNK, LANE), f32)
        dprm = jnp.zeros((8, LANE), f32)
        for h in range(6):
            _, vjp = jax.vjp(lambda *a: _dnc_f(*a, h), q_ref[:, _hs(h // 2)], k_ref[:, _hs(h // 2)],
                             v_ref[:, _hs(h)], s_ref[...], p_ref[...])
            dq, dk, dv, ds, dp = vjp((du_ref[:, _hs(h)], dw_ref[:, _hs(h)], dqd_ref[:, _hs(h)], dkt_ref[:, _hs(h)],
                                      dat_ref[0, h], ddc_ref[0, h:h + 1, :]))
            dqs.append(dq)
            dks.append(dk)
            dvs.append(dv)
            dseg = dseg + ds
            dprm = dprm + dp
        dx_ref[...] = jnp.concatenate([dqs[0] + dqs[1], dqs[2] + dqs[3], dqs[4] + dqs[5],
                                       dks[0] + dks[1], dks[2] + dks[3], dks[4] + dks[5]] + dvs, axis=1)
        dseg_ref[...] = dseg
        dprm_ref[...] += dprm

    return pl.pallas_call(
        body, name="dn_chunk_bwd", grid=(S // CHUNK,), in_specs=_dnc_in_specs() + _dnc_out_specs(),
        out_specs=[pl.BlockSpec((CHUNK, B_QKV), lambda n: (n, 0)), pl.BlockSpec((CHUNK, LANE), lambda n: (n, 0)),
                   pl.BlockSpec((8, LANE), lambda n: (0, 0))],
        out_shape=[jax.ShapeDtypeStruct((S, B_QKV), f32), jax.ShapeDtypeStruct((S, LANE), f32),
                   jax.ShapeDtypeStruct((8, LANE), f32)],
        compiler_params=_cp(("arbitrary",)),
    )(qkvn, qkvn, qkvn, proj, prm, *cots)


def dns_fwd(chunked):
    u = chunked[0]
    S = u.shape[0]
    nc = S // CHUNK

    def body(u_ref, w_ref, qd_ref, kt_ref, at_ref, dc_ref, o_ref, st_ref, st):
        @pl.when(pl.program_id(0) == 0)
        def _():
            st[...] = jnp.zeros_like(st)
        for h in range(6):
            S0 = st[h]
            st_ref[0, h] = S0
            out, S1 = _dns_f(S0, u_ref[:, _hs(h)], w_ref[:, _hs(h)], qd_ref[:, _hs(h)], kt_ref[:, _hs(h)],
                             at_ref[0, h], dc_ref[0, h:h + 1, :])
            o_ref[:, _hs(h)] = out
            st[h] = S1

    return pl.pallas_call(
        body, name="dn_scan_fwd", grid=(nc,), in_specs=_dnc_out_specs(),
        out_specs=[pl.BlockSpec((CHUNK, B_V), lambda n: (n, 0)),
                   pl.BlockSpec((1, 6, B_DH, B_DH), lambda n: (n, 0, 0, 0))],
        out_shape=[jax.ShapeDtypeStruct((S, B_V), f32), jax.ShapeDtypeStruct((nc, 6, B_DH, B_DH), f32)],
        scratch_shapes=[pltpu.VMEM((6, B_DH, B_DH), f32)],
        compiler_params=_cp(("arbitrary",)),
    )(*chunked)


def dns_bwd(chunked, states, do):
    S = do.shape[0]
    nc = S // CHUNK

    def body(u_ref, w_ref, qd_ref, kt_ref, at_ref, dc_ref, st_ref, do_ref,
             du_ref, dw_ref, dqd_ref, dkt_ref, dat_ref, ddc_ref, dst):
        @pl.when(pl.program_id(0) == 0)
        def _():
            dst[...] = jnp.zeros_like(dst)
        ddc_ref[...] = jnp.zeros_like(ddc_ref)
        for h in range(6):
            _, vjp = jax.vjp(_dns_f, st_ref[0, h], u_ref[:, _hs(h)], w_ref[:, _hs(h)], qd_ref[:, _hs(h)],
                             kt_ref[:, _hs(h)], at_ref[0, h], dc_ref[0, h:h + 1, :])
            dS0, du, dw, dqd, dkt, dat, ddc = vjp((do_ref[:, _hs(h)], dst[h]))
            dst[h] = dS0
            du_ref[:, _hs(h)] = du
            dw_ref[:, _hs(h)] = dw
            dqd_ref[:, _hs(h)] = dqd
            dkt_ref[:, _hs(h)] = dkt
            dat_ref[0, h] = dat
            ddc_ref[0, h:h + 1, :] = ddc

    return pl.pallas_call(
        body, name="dn_scan_bwd", grid=(nc,),
        in_specs=_dnc_out_specs(nc) + [pl.BlockSpec((1, 6, B_DH, B_DH), lambda n: (nc - 1 - n, 0, 0, 0)),
                                       pl.BlockSpec((CHUNK, B_V), lambda n: (nc - 1 - n, 0))],
        out_specs=_dnc_out_specs(nc), out_shape=_dnc_shapes(S),
        scratch_shapes=[pltpu.VMEM((6, B_DH, B_DH), f32)],
        compiler_params=_cp(("arbitrary",)),
    )(*chunked, states, do)


def dnpost_fwd(o, proj, prm):
    S = o.shape[0]
    t = min(512, S)

    def body(o_ref, z_ref, p_ref, y_ref):
        y_ref[...] = _dnpost_f(o_ref[...], z_ref[...], p_ref[2:3, :])

    tok = pl.BlockSpec((t, B_V), lambda i: (i, 0))
    return pl.pallas_call(
        body, name="dn_post_fwd", grid=(S // t,),
        in_specs=[tok, pl.BlockSpec((t, B_V), lambda i: (i, 2)), pl.BlockSpec((8, LANE), lambda i: (0, 0))],
        out_specs=tok, out_shape=jax.ShapeDtypeStruct((S, B_V), f32), compiler_params=_cp(("parallel",)),
    )(o, proj, prm)


def dnpost_bwd(o, proj, prm, dmix):
    S = o.shape[0]
    t = min(512, S)

    def body(o_ref, z_ref, p_ref, dy_ref, do_ref, dz_ref, dg_ref):
        @pl.when(pl.program_id(0) == 0)
        def _():
            dg_ref[...] = jnp.zeros_like(dg_ref)
        _, vjp = jax.vjp(_dnpost_f, o_ref[...], z_ref[...], p_ref[2:3, :])
        do, dz, dg = vjp(dy_ref[...])
        do_ref[...] = do
        dz_ref[...] = dz
        dg_ref[...] += dg

    tok = pl.BlockSpec((t, B_V), lambda i: (i, 0))
    return pl.pallas_call(
        body, name="dn_post_bwd", grid=(S // t,),
        in_specs=[tok, pl.BlockSpec((t, B_V), lambda i: (i, 2)), pl.BlockSpec((8, LANE), lambda i: (0, 0)), tok],
        out_specs=[tok, tok, pl.BlockSpec((1, LANE), lambda i: (0, 0))],
        out_shape=[jax.ShapeDtypeStruct((S, B_V), f32), jax.ShapeDtypeStruct((S, B_V), f32),
                   jax.ShapeDtypeStruct((1, LANE), f32)],
        compiler_params=_cp(("arbitrary",)),
    )(o, proj, prm, dmix)


def _glu_f(gu, w, b):
    gate, up = gu[:, :LANE], gu[:, LANE:]
    c = w[2:3] * gate + w[1:2] * shift_down(gate, 1) + w[0:1] * shift_down(gate, 2) + b
    return _silu(c) * up


def glu_fwd(gu, w, b, name):
    S = gu.shape[0]
    nblk = D_FF // LANE

    def body(g_ref, w_ref, b_ref, o_ref):
        o_ref[...] = _glu_f(g_ref[...], w_ref[...], b_ref[...]).astype(bf16)

    return pl.pallas_call(
        body, name=name, grid=(nblk,),
        in_specs=[pl.BlockSpec((S, 2 * LANE), lambda j: (0, j)), pl.BlockSpec((3, LANE), lambda j: (0, j)),
                  pl.BlockSpec((1, LANE), lambda j: (0, j))],
        out_specs=pl.BlockSpec((S, LANE), lambda j: (0, j)),
        out_shape=jax.ShapeDtypeStruct((S, D_FF), bf16), compiler_params=_cp(("parallel",)),
    )(gu, w, b.reshape(1, D_FF))


def glu_bwd(gu, w, b, dact, name):
    S = gu.shape[0]
    nblk = D_FF // LANE

    def body(g_ref, w_ref, b_ref, d_ref, dg_ref, dw_ref, db_ref):
        _, vjp = jax.vjp(_glu_f, g_ref[...], w_ref[...], b_ref[...])
        dg, dw, db = vjp(d_ref[...])
        dg_ref[...] = dg.astype(bf16)
        dw_ref[...] = dw
        db_ref[...] = db

    gsp = pl.BlockSpec((S, 2 * LANE), lambda j: (0, j))
    wsp = pl.BlockSpec((3, LANE), lambda j: (0, j))
    bsp = pl.BlockSpec((1, LANE), lambda j: (0, j))
    return pl.pallas_call(
        body, name=name, grid=(nblk,),
        in_specs=[gsp, wsp, bsp, pl.BlockSpec((S, LANE), lambda j: (0, j))], out_specs=[gsp, wsp, bsp],
        out_shape=[jax.ShapeDtypeStruct((S, 2 * D_FF), bf16), jax.ShapeDtypeStruct((3, D_FF), f32),
                   jax.ShapeDtypeStruct((1, D_FF), f32)],
        compiler_params=_cp(("parallel",)),
    )(gu, w, b.reshape(1, D_FF), dact)


def _pair_cols(w):
    lead = w.shape[:-1]
    return w.reshape(lead + (2, 6, A_DH)).swapaxes(-3, -2).reshape(lead + (A_Q,))


def _unpair_cols(w):
    lead = w.shape[:-1]
    return w.reshape(lead + (6, 2, A_DH)).swapaxes(-3, -2).reshape(lead + (A_Q,))


def _lay_in_a(w):
    return jnp.concatenate([_pair_cols(w[:, :A_Q]), w[:, A_Q:]], axis=1)


def _unlay_in_a(w):
    return jnp.concatenate([_unpair_cols(w[:, :A_Q]), w[:, A_Q:]], axis=1)


def _lay_out_a(w):
    return jnp.concatenate([_pair_cols(w[:A_Q].T).T, w[A_Q:]], axis=0)


def _unlay_out_a(w):
    return jnp.concatenate([_unpair_cols(w[:A_Q].T).T, w[A_Q:]], axis=0)


def _lay_in_b(w):
    return jnp.concatenate([w[:, :2304], w[:, 2316:], w[:, 2304:2316],
                            jnp.zeros((w.shape[0], LANE - 12), w.dtype)], axis=1)


def _unlay_in_b(w):
    return jnp.concatenate([w[:, :2304], w[:, 2560:2572], w[:, 2304:2560]], axis=1)


def _lay_gu(w):
    return w.reshape(D, 2, D_FF // LANE, LANE).swapaxes(1, 2).reshape(D, 2 * D_FF)


def _unlay_gu(w):
    return w.reshape(D, D_FF // LANE, 2, LANE).swapaxes(1, 2).reshape(D, 2 * D_FF)


def _local_step(x, mem, target, P):
    sk = jnp.zeros((16, LANE), f32).at[:A_HEADS].set(jnp.broadcast_to(P["sinks"][:, None], (A_HEADS, LANE)))
    prm = jnp.zeros((8, LANE), f32).at[0, :6].set(P["a_log"]).at[1, :6].set(P["dt_bias"]).at[2].set(P["out_norm_g"])
    bias = bias_build(P["rel_bias"])
    saved = []
    h = x
    for i in range(2):
        n1 = rms_fwd(h, P["g_mix"][i], f"rms_mix{i}")
        kv = memkv_fwd(mem, P["g_mem"][i], P["w_mem"][i], f"memkv{i}")
        if i == 0:
            proj = mm_nn(n1, P["w_in_a"], name="proj_a")
            self_out = swa_fwd(proj, bias, sk)
            cross = xattn_fwd(proj, A_Q + 2 * LANE, kv, "xattn_a")
            extra = ()
        else:
            proj = mm_nn(n1, P["w_in_b"], name="proj_b")
            qkvn = dnprep_fwd(proj, P["conv_qkv"])
            chunked = dnc_fwd(qkvn, proj, prm)
            o, states = dns_fwd(chunked)
            self_out = dnpost_fwd(o, proj, prm)
            cross = xattn_fwd(proj, 2304, kv, "xattn_b")
            extra = (qkvn, chunked, states, o)
        mix = jnp.concatenate([self_out, cross], axis=1)
        h2 = mm_nn(mix, P["w_out"][i], res=h, name=f"out_proj{i}")
        n2 = rms_fwd(h2, P["g_ffn"][i], f"rms_ffn{i}")
        gu = mm_nn(n2, P["w_gu"][i], name=f"gate_up{i}")
        act = glu_fwd(gu, P["ffn_cw"][i], P["ffn_cb"][i], f"glu{i}")
        h3 = mm_nn(act, P["w_down"][i], res=h2, name=f"down{i}")
        saved.append((h, n1, kv, proj, mix, h2, n2, gu, act, extra))
        h = h3

    loss, dh, dg_fin = loss_head(h, P["g_fin"], target)
    G = {"g_fin": dg_fin[0], "g_mix": [None, None], "g_mem": [None, None], "g_ffn": [None, None],
         "w_mem": [None, None], "w_out": [None, None], "w_gu": [None, None], "w_down": [None, None],
         "ffn_cw": [None, None], "ffn_cb": [None, None]}
    for i in (1, 0):
        hin, n1, kv, proj, mix, h2, n2, gu, act, extra = saved[i]
        dact = mm_nt(dh, P["w_down"][i], name=f"d_act{i}")
        G["w_down"][i] = mm_tn(act, dh, name=f"dw_down{i}")
        dgu, dcw, dcb = glu_bwd(gu, P["ffn_cw"][i], P["ffn_cb"][i], dact, f"glu_bwd{i}")
        G["ffn_cw"][i], G["ffn_cb"][i] = dcw, dcb[0]
        dn2 = mm_nt(dgu, P["w_gu"][i], name=f"d_n2_{i}")
        G["w_gu"][i] = mm_tn(n2, dgu, name=f"dw_gu{i}")
        dh2, dg = rms_bwd(h2, P["g_ffn"][i], dn2, dh, f"rms_ffn_bwd{i}")
        G["g_ffn"][i] = dg[0]
        dmix = mm_nt(dh2, P["w_out"][i], name=f"d_mix{i}")
        G["w_out"][i] = mm_tn(mix, dh2, name=f"dw_out{i}")
        if i == 0:
            dqkv, dbias, dsk = swa_bwd(proj, bias, sk, dmix)
            dxq, dkv = xattn_bwd(proj, A_Q + 2 * LANE, kv, dmix, "xattn_a_bwd")
            dproj = jnp.concatenate([dqkv, dxq], axis=1)
            G["sinks"] = dsk[:A_HEADS, 0]
            G["rel_bias"] = bias_grad(dbias)[:, :A_HEADS]
            w_in, gname = P["w_in_a"], "w_in_a"
        else:
            qkvn, chunked, states, o = extra
            do, dz, dgo = dnpost_bwd(o, proj, prm, dmix)
            dqkvn, dseg, dprm = dnc_bwd(qkvn, proj, prm, dns_bwd(chunked, states, do))
            draw, dconv = dnprep_bwd(proj, P["conv_qkv"], dqkvn)
            dxq, dkv = xattn_bwd(proj, 2304, kv, dmix, "xattn_b_bwd")
            dproj = jnp.concatenate([draw, dz, dxq, dseg], axis=1)
            G["conv_qkv"] = dconv
            G["a_log"], G["dt_bias"], G["out_norm_g"] = dprm[0, :6], dprm[1, :6], dgo[0]
            w_in, gname = P["w_in_b"], "w_in_b"
        dn1 = mm_nt(dproj, w_in, name=f"d_n1_{i}")
        G[gname] = mm_tn(n1, dproj, name=f"d{gname}")
        dh, dg = rms_bwd(hin, P["g_mix"][i], dn1, dh2, f"rms_mix_bwd{i}")
        G["g_mix"][i] = dg[0]
        dgm, dwm = memkv_bwd(mem, P["g_mem"][i], P["w_mem"][i], dkv, f"memkv_bwd{i}")
        G["g_mem"][i], G["w_mem"][i] = dgm[0], dwm
    return loss, dh, G


def _prepare(full):
    return {
        "rel_bias": full["rel_bias"], "sinks": full["sinks_a"][0], "a_log": full["a_log_b"][0],
        "dt_bias": full["dt_bias_b"][0], "out_norm_g": full["out_norm_g_b"][0],
        "g_mix": full["norm_mix_g"], "g_mem": full["norm_mem_g"], "g_ffn": full["norm_ffn_g"],
        "g_fin": full["final_norm_g"], "conv_qkv": full["conv_qkv_b"][0],
        "ffn_cw": [full["ffn_conv_w"][0], full["ffn_conv_w"][1]],
        "ffn_cb": [full["ffn_conv_b"][0], full["ffn_conv_b"][1]],
        "w_mem": [full["w_mem_kv"][0], full["w_mem_kv"][1]],
        "w_out": [_lay_out_a(full["w_out"][0]), full["w_out"][1]],
        "w_in_a": _lay_in_a(full["w_in_a"][0]), "w_in_b": _lay_in_b(full["w_in_b"][0]),
        "w_gu": [_lay_gu(full["w_gate_up"][0]), _lay_gu(full["w_gate_up"][1])],
        "w_down": [full["w_down"][0], full["w_down"][1]],
    }


def _grads_to_ref(G):
    return {
        "rel_bias": G["rel_bias"], "norm_mix_g": jnp.stack(G["g_mix"]), "norm_mem_g": jnp.stack(G["g_mem"]),
        "w_mem_kv": jnp.stack(G["w_mem"]),
        "w_out": jnp.stack([_unlay_out_a(G["w_out"][0]), G["w_out"][1]]),
        "w_in_a": _unlay_in_a(G["w_in_a"])[None], "sinks_a": G["sinks"][None],
        "w_in_b": _unlay_in_b(G["w_in_b"])[None], "conv_qkv_b": G["conv_qkv"][None],
        "a_log_b": G["a_log"][None], "dt_bias_b": G["dt_bias"][None], "out_norm_g_b": G["out_norm_g"][None],
        "norm_ffn_g": jnp.stack(G["g_ffn"]),
        "w_gate_up": jnp.stack([_unlay_gu(G["w_gu"][0]), _unlay_gu(G["w_gu"][1])]),
        "ffn_conv_w": jnp.stack(G["ffn_cw"]), "ffn_conv_b": jnp.stack(G["ffn_cb"]),
        "w_down": jnp.stack(G["w_down"]), "final_norm_g": G["g_fin"],
    }


ANY = pl.BlockSpec(memory_space=pl.ANY)


def _place():
    return lax.axis_index("x"), lax.axis_index("y"), lax.axis_index("c")


def _chip_exchange(bufs, out_shapes, src_of, name):
    n = len(bufs)

    def body(*refs):
        ins, outs = refs[:n], refs[n:2 * n]
        ssem, rsem, lsem = refs[2 * n:]
        x, y, c = _place()
        me = 2 * x + y
        peers = [(1 - x, y), (x, 1 - y), (1 - x, 1 - y)]

        def remote(j, k, slot):
            px, py = peers[k]
            return pltpu.make_async_remote_copy(
                src_ref=src_of(j, ins[j], 2 * px + py), dst_ref=outs[j].at[slot],
                send_sem=ssem.at[3 * j + k], recv_sem=rsem.at[3 * j + k],
                device_id=(px, py, c), device_id_type=MESH)

        started = []
        for j in range(n):
            lc = pltpu.make_async_copy(src_of(j, ins[j], me), outs[j].at[me], lsem.at[j])
            lc.start()
            started.append(lc)
        sends = [remote(j, k, me) for j in range(n) for k in range(3)]
        for cp in sends:
            cp.start()
        for j in range(n):
            for k in range(3):
                px, py = peers[k]
                remote(j, k, 2 * px + py).wait_recv()
        for cp in sends:
            cp.wait_send()
        for lc in started:
            lc.wait()

    return pl.pallas_call(
        body, name=name, in_specs=[ANY] * n, out_specs=[ANY] * n, out_shape=out_shapes,
        scratch_shapes=[pltpu.SemaphoreType.DMA((3 * n,)), pltpu.SemaphoreType.DMA((3 * n,)),
                        pltpu.SemaphoreType.DMA((n,))],
    )(*bufs)


def allgather_chips(bufs):
    shapes = [jax.ShapeDtypeStruct((4,) + b.shape, b.dtype) for b in bufs]
    return _chip_exchange(bufs, shapes, lambda j, ref, chip: ref, "weight_allgather")


def chip_scatter(g):
    return _chip_exchange([g], [jax.ShapeDtypeStruct(g.shape, g.dtype)], lambda j, ref, chip: ref.at[chip],
                          "grad_scatter")[0]


def sibling_exchange(p):
    def body(p_ref, o_ref, ssem, rsem):
        x, y, c = _place()
        cp = pltpu.make_async_remote_copy(src_ref=p_ref, dst_ref=o_ref, send_sem=ssem, recv_sem=rsem,
                                          device_id=(x, y, 1 - c), device_id_type=MESH)
        cp.start()
        cp.wait()

    return pl.pallas_call(
        body, name="sibling_exchange", in_specs=[ANY], out_specs=ANY,
        out_shape=jax.ShapeDtypeStruct(p.shape, p.dtype),
        scratch_shapes=[pltpu.SemaphoreType.DMA, pltpu.SemaphoreType.DMA],
    )(p)


def allreduce_small(buf):
    R = buf.shape[0]

    def body(b_ref, o_ref, recv, ssem, rsem):
        x, y, c = _place()
        me = 4 * x + 2 * y + c

        def peer(k):
            return (1 - x if k & 4 else x, 1 - y if k & 2 else y, 1 - c if k & 1 else c)

        def remote(k, slot):
            return pltpu.make_async_remote_copy(
                src_ref=b_ref, dst_ref=recv.at[slot], send_sem=ssem.at[k - 1], recv_sem=rsem.at[k - 1],
                device_id=peer(k), device_id_type=MESH)

        sends = [remote(k, me) for k in range(1, 8)]
        for cp in sends:
            cp.start()
        recv[me] = b_ref[...]
        for k in range(1, 8):
            px, py, pc = peer(k)
            remote(k, 4 * px + 2 * py + pc).wait_recv()
        for cp in sends:
            cp.wait_send()
        total = recv[0]
        for j in range(1, 8):
            total = total + recv[j]
        o_ref[...] = total

    return pl.pallas_call(
        body, name="small_allreduce",
        in_specs=[pl.BlockSpec(memory_space=pltpu.VMEM)], out_specs=pl.BlockSpec(memory_space=pltpu.VMEM),
        out_shape=jax.ShapeDtypeStruct(buf.shape, f32),
        scratch_shapes=[pltpu.VMEM((8, R, LANE), f32), pltpu.SemaphoreType.DMA((7,)), pltpu.SemaphoreType.DMA((7,))],
    )(buf)


def sum_slots(recv):
    _, R, C = recv.shape
    tr = 512

    def body(r_ref, o_ref):
        acc = r_ref[0].astype(f32)
        for s in range(1, 4):
            acc = acc + r_ref[s].astype(f32)
        o_ref[...] = acc

    return pl.pallas_call(
        body, name="sum_slots", grid=(R // tr,),
        in_specs=[pl.BlockSpec((4, tr, C), lambda i: (0, i, 0))], out_specs=pl.BlockSpec((tr, C), lambda i: (i, 0)),
        out_shape=jax.ShapeDtypeStruct((R, C), f32), compiler_params=_cp(("parallel",)),
    )(recv)


def _adamw_math(w, g, m, v):
    m = B1 * m + (1.0 - B1) * g
    v = B2 * v + (1.0 - B2) * (g * g)
    m_hat = m / (1.0 - B1 ** STEP)
    v_hat = v / (1.0 - B2 ** STEP)
    delta = -LR * (m_hat / (jnp.sqrt(v_hat) + AEPS) + WD * w)
    return delta, m, v


def adamw_rows(w, m, v, ga, gb, row0, name):
    rows, C = w.shape
    tr = ADAM_ROWS
    b0 = row0 // tr

    def body(w_ref, m_ref, v_ref, ga_ref, gb_ref, g_ref, d_ref, nm_ref, nv_ref):
        g = ga_ref[...] + gb_ref[...]
        d, nm, nv = _adamw_math(w_ref[...], g, m_ref[...], v_ref[...])
        g_ref[...] = g
        d_ref[...] = d
        nm_ref[...] = nm
        nv_ref[...] = nv

    own = pl.BlockSpec((tr, C), lambda i: (i, 0))
    off = pl.BlockSpec((tr, C), lambda i: (b0 + i, 0))
    return pl.pallas_call(
        body, name=name, grid=(rows // tr,), in_specs=[own, own, own, off, off], out_specs=[own] * 4,
        out_shape=[jax.ShapeDtypeStruct((rows, C), f32)] * 4, compiler_params=_cp(("parallel",)),
    )(w, m, v, ga, gb)


def adamw_small(w, m, v, g):
    def body(w_ref, m_ref, v_ref, g_ref, d_ref, nm_ref, nv_ref):
        d, nm, nv = _adamw_math(w_ref[...], g_ref[...], m_ref[...], v_ref[...])
        d_ref[...] = d
        nm_ref[...] = nm
        nv_ref[...] = nv

    return pl.pallas_call(body, name="adamw_small", out_shape=[jax.ShapeDtypeStruct(w.shape, f32)] * 3)(w, m, v, g)


ADAM_ROWS = 64
PACK_C = 1024
BIG = (("w_mem_kv", 1), ("w_out", 1), ("w_in_a", 2), ("w_gate_up", 2), ("w_down", 1), ("w_in_b", 2))
CONV = (("conv_qkv_b", 2), ("ffn_conv_w", 2))
SMALL = ("rel_bias", "norm_mix_g", "norm_mem_g", "sinks_a", "a_log_b", "dt_bias_b", "out_norm_g_b", "norm_ffn_g",
         "ffn_conv_b", "final_norm_g")
WEIGHTS = ("rel_bias", "norm_mix_g", "norm_mem_g", "w_mem_kv", "w_out", "w_in_a", "sinks_a", "w_in_b", "conv_qkv_b",
           "a_log_b", "dt_bias_b", "out_norm_g_b", "norm_ffn_g", "w_gate_up", "ffn_conv_w", "ffn_conv_b", "w_down",
           "final_norm_g")
ARGS = ("x", "mem") + WEIGHTS + ("loss_target",) + tuple("m_" + n for n in WEIGHTS) + tuple("v_" + n for n in WEIGHTS)


def _rows(a, width):
    flat = a.reshape(-1)
    pad = (-flat.shape[0]) % width
    if pad:
        flat = jnp.concatenate([flat, jnp.zeros((pad,), a.dtype)])
    return flat.reshape(-1, width)


def _nrows(shape, width):
    return -(-math.prod(shape) // width)


def _pack(arrs, width, total_rows, dtype):
    parts = [_rows(a.astype(dtype), width) for a in arrs]
    used = sum(p.shape[0] for p in parts)
    if total_rows > used:
        parts.append(jnp.zeros((total_rows - used, width), dtype))
    return jnp.concatenate(parts, axis=0)


def _unpack(buf, shapes, width):
    out, r = [], 0
    for s in shapes:
        n = _nrows(s, width)
        out.append(buf[r:r + n].reshape(-1)[:math.prod(s)].reshape(s))
        r += n
    return out


def _pad_to(n, mult):
    return -(-n // mult) * mult


def _split_chips(full, axis):
    s = full.shape
    return jnp.moveaxis(full.reshape(s[:axis] + (4, s[axis] // 4) + s[axis + 1:]), axis, 0)


def _join_chips(parts, axis):
    m = jnp.moveaxis(parts, 0, axis)
    s = m.shape
    return m.reshape(s[:axis] + (4 * s[axis + 1],) + s[axis + 2:])


def kernel(x, mem, rel_bias, norm_mix_g, norm_mem_g, w_mem_kv, w_out, w_in_a, sinks_a, w_in_b, conv_qkv_b, a_log_b, dt_bias_b, out_norm_g_b, norm_ffn_g, w_gate_up, ffn_conv_w, ffn_conv_b, w_down, final_norm_g, loss_target, m_rel_bias, m_norm_mix_g, m_norm_mem_g, m_w_mem_kv, m_w_out, m_w_in_a, m_sinks_a, m_w_in_b, m_conv_qkv_b, m_a_log_b, m_dt_bias_b, m_out_norm_g_b, m_norm_ffn_g, m_w_gate_up, m_ffn_conv_w, m_ffn_conv_b, m_w_down, m_final_norm_g, v_rel_bias, v_norm_mix_g, v_norm_mem_g, v_w_mem_kv, v_w_out, v_w_in_a, v_sinks_a, v_w_in_b, v_conv_qkv_b, v_a_log_b, v_dt_bias_b, v_out_norm_g_b, v_norm_ffn_g, v_w_gate_up, v_ffn_conv_w, v_ffn_conv_b, v_w_down, v_final_norm_g):
    A = dict(zip(ARGS, (x, mem, rel_bias, norm_mix_g, norm_mem_g, w_mem_kv, w_out, w_in_a, sinks_a, w_in_b, conv_qkv_b, a_log_b, dt_bias_b, out_norm_g_b, norm_ffn_g, w_gate_up, ffn_conv_w, ffn_conv_b, w_down, final_norm_g, loss_target, m_rel_bias, m_norm_mix_g, m_norm_mem_g, m_w_mem_kv, m_w_out, m_w_in_a, m_sinks_a, m_w_in_b, m_conv_qkv_b, m_a_log_b, m_dt_bias_b, m_out_norm_g_b, m_norm_ffn_g, m_w_gate_up, m_ffn_conv_w, m_ffn_conv_b, m_w_down, m_final_norm_g, v_rel_bias, v_norm_mix_g, v_norm_mem_g, v_w_mem_kv, v_w_out, v_w_in_a, v_sinks_a, v_w_in_b, v_conv_qkv_b, v_a_log_b, v_dt_bias_b, v_out_norm_g_b, v_norm_ffn_g, v_w_gate_up, v_ffn_conv_w, v_ffn_conv_b, v_w_down, v_final_norm_g)))
    chip = 2 * lax.axis_index("x") + lax.axis_index("y")

    big_shapes = [A[n].shape for n, _ in BIG]
    big_rows = [_nrows(s, PACK_C) for s in big_shapes]
    big_off = [sum(big_rows[:j]) for j in range(len(BIG))]
    R = _pad_to(sum(big_rows), 512)
    conv_shapes = [A[n].shape for n, _ in CONV]
    conv_total = _pad_to(sum(_nrows(s, LANE) for s in conv_shapes), 8)
    wpack = _pack([A[n] for n, _ in BIG], PACK_C, R, bf16)
    cpack = _pack([A[n] for n, _ in CONV], LANE, conv_total, f32)
    gw, gc = allgather_chips([wpack, cpack])
    full = {n: A[n] for n in SMALL}
    parts = zip(*[_unpack(gw[s], big_shapes, PACK_C) for s in range(4)])
    for (n, axis), p in zip(BIG, parts):
        full[n] = _join_chips(jnp.stack(p), axis)
    parts = zip(*[_unpack(gc[s], conv_shapes, LANE) for s in range(4)])
    for (n, axis), p in zip(CONV, parts):
        full[n] = _join_chips(jnp.stack(p), axis)

    loss, dx, G = _local_step(x[0], mem[0], loss_target[0], _prepare(full))
    gfull = _grads_to_ref(G)

    gpack = jnp.concatenate(
        [_split_chips(gfull[n], axis).astype(bf16).reshape(4, -1, PACK_C) for n, axis in BIG]
        + [jnp.zeros((4, R - sum(big_rows), PACK_C), bf16)], axis=1)
    part = sum_slots(chip_scatter(gpack))
    sib = sibling_exchange(part)

    sm_shapes = [A[n].shape for n in SMALL] + [gfull[n].shape for n, _ in CONV] + [(LANE,)]
    sm_rows = _pad_to(sum(_nrows(s, LANE) for s in sm_shapes), 8)
    sbuf = _pack([gfull[n] for n in SMALL] + [gfull[n] for n, _ in CONV] + [loss[0]], LANE, sm_rows, f32)
    tot = _unpack(allreduce_small(sbuf), sm_shapes, LANE)
    gsmall = dict(zip(SMALL, tot[:len(SMALL)]))
    for (n, axis), t in zip(CONV, tot[len(SMALL):len(SMALL) + len(CONV)]):
        sh = A[n].shape[axis]
        gsmall[n] = lax.dynamic_slice_in_dim(t, chip * sh, sh, axis)
    loss_out = tot[-1][0]

    out = {}
    for (n, axis), rows, off in zip(BIG, big_rows, big_off):
        padded = _pad_to(rows, ADAM_ROWS)
        w2, m2, v2 = (_pack([A[p + n]], PACK_C, padded, f32) if padded != rows else A[p + n].reshape(rows, PACK_C)
                      for p in ("", "m_", "v_"))
        res = adamw_rows(w2, m2, v2, part, sib, off, "adamw_" + n)
        for key, r in zip(("grad_", "delta_", "new_m_", "new_v_"), res):
            out[key + n] = r[:rows].reshape(A[n].shape)
    names = SMALL + tuple(n for n, _ in CONV)
    shapes = [A[n].shape for n in names]
    rows = _pad_to(sum(_nrows(s, LANE) for s in shapes), 8)
    packs = [_pack([src[n] for n in names], LANE, rows, f32)
             for src in ({n: A[n] for n in names}, {n: A["m_" + n] for n in names}, {n: A["v_" + n] for n in names}, gsmall)]
    res = adamw_small(*packs)
    for key, r in zip(("delta_", "new_m_", "new_v_"), res):
        for n, a in zip(names, _unpack(r, shapes, LANE)):
            out[key + n] = a
    for n in names:
        out["grad_" + n] = gsmall[n]
    return (loss_out, dx[None], *[out["grad_" + n] for n in WEIGHTS], *[out["delta_" + n] for n in WEIGHTS],
            *[out["new_m_" + n] for n in WEIGHTS], *[out["new_v_" + n] for n in WEIGHTS])
```

```python
import functools
import math

import numpy as np
import jax
import jax.numpy as jnp
from jax import lax
from jax.experimental import pallas as pl
from jax.experimental.pallas import tpu as pltpu

f32 = jnp.float32
bf16 = jnp.bfloat16
HI = lax.Precision.HIGHEST
MESH = pl.DeviceIdType.MESH

D = 1024
MEM_LEN = 256
EPS = 1e-6
A_HEADS, A_KV, A_DH = 12, 2, 64
A_Q = 768
BLK = 128
N_BUCKETS, MAX_DIST = 32, 128
B_QK, B_V, B_DH = 384, 768, 128
B_QKV = 1536
CHUNK = 64
X_Q = 256
D_FF = 2816
IN_A = 1280
IN_B = 2572
IN_B_PAD = 2688
LANE = 128
VMEM_LIMIT = 56 * 1024 * 1024

LR, B1, B2, AEPS, WD, STEP = 0.001, 0.9, 0.999, 1e-08, 0.01, 10


def _cp(sem=None):
    return pltpu.CompilerParams(dimension_semantics=sem, vmem_limit_bytes=VMEM_LIMIT)


def _dg(a, b, ca, cb, prec=None):
    return lax.dot_general(a, b, (((ca,), (cb,)), ((), ())), precision=prec, preferred_element_type=f32)


@jax.custom_vjp
def bdot(a, b):
    return _dg(a.astype(bf16), b.astype(bf16), 1, 0)


def _bdot_f(a, b):
    return bdot(a, b), (a, b)


def _bdot_b(res, g):
    a, b = res
    gb = g.astype(bf16)
    return _dg(gb, b.astype(bf16), 1, 1), _dg(a.astype(bf16), gb, 0, 0)


bdot.defvjp(_bdot_f, _bdot_b)


@jax.custom_vjp
def bdot_nt(a, b):
    return _dg(a.astype(bf16), b.astype(bf16), 1, 1)


def _bdot_nt_f(a, b):
    return bdot_nt(a, b), (a, b)


def _bdot_nt_b(res, g):
    a, b = res
    gb = g.astype(bf16)
    return _dg(gb, b.astype(bf16), 1, 0), _dg(gb, a.astype(bf16), 0, 0)


bdot_nt.defvjp(_bdot_nt_f, _bdot_nt_b)


def _shift_rows(x, s, down):
    n = x.shape[0]
    row = lax.broadcasted_iota(jnp.int32, x.shape, 0)
    if down:
        return jnp.where(row >= s, pltpu.roll(x, s, 0), 0.0)
    return jnp.where(row < n - s, pltpu.roll(x, n - s, 0), 0.0)


@functools.partial(jax.custom_vjp, nondiff_argnums=(1,))
def shift_down(x, s):
    return _shift_rows(x, s, True)


def _sd_f(x, s):
    return _shift_rows(x, s, True), None


def _sd_b(s, _, g):
    return (_shift_rows(g, s, False),)


shift_down.defvjp(_sd_f, _sd_b)


def _sigmoid(x):
    return 1.0 / (1.0 + jnp.exp(-x))


def _silu(x):
    return x * _sigmoid(x)


def _rms(x, g):
    return x * lax.rsqrt(jnp.mean(x * x, axis=-1, keepdims=True) + EPS) * g


def _tile(n, cap):
    u = n // LANE
    best = 1
    for d in range(1, u + 1):
        if u % d == 0 and d * LANE <= cap:
            best = d
    return best * LANE


def mm_nn(a, w, res=None, out_dtype=f32, name="mm_nn"):
    M, K = a.shape
    N = w.shape[1]
    tm, tn = min(512, M), _tile(N, 640)

    def body(*refs):
        if res is None:
            a_ref, w_ref, o_ref = refs
            o_ref[...] = _dg(a_ref[...].astype(bf16), w_ref[...], 1, 0).astype(out_dtype)
        else:
            a_ref, w_ref, r_ref, o_ref = refs
            o_ref[...] = (r_ref[...] + _dg(a_ref[...].astype(bf16), w_ref[...], 1, 0)).astype(out_dtype)

    in_specs = [pl.BlockSpec((tm, K), lambda n, m: (m, 0)), pl.BlockSpec((K, tn), lambda n, m: (0, n))]
    args = [a, w]
    if res is not None:
        in_specs.append(pl.BlockSpec((tm, tn), lambda n, m: (m, n)))
        args.append(res)
    return pl.pallas_call(
        body, name=name, grid=(N // tn, M // tm), in_specs=in_specs,
        out_specs=pl.BlockSpec((tm, tn), lambda n, m: (m, n)),
        out_shape=jax.ShapeDtypeStruct((M, N), out_dtype),
        compiler_params=_cp(("parallel", "parallel")),
    )(*args)


def mm_nt(dy, w, name="mm_nt"):
    M, N = dy.shape
    K = w.shape[0]
    tm, tn = min(512, M), _tile(N, 512)

    def body(dy_ref, w_ref, o_ref):
        @pl.when(pl.program_id(1) == 0)
        def _():
            o_ref[...] = jnp.zeros_like(o_ref)
        o_ref[...] += _dg(dy_ref[...].astype(bf16), w_ref[...], 1, 1)

    return pl.pallas_call(
        body, name=name, grid=(M // tm, N // tn),
        in_specs=[pl.BlockSpec((tm, tn), lambda m, n: (m, n)), pl.BlockSpec((K, tn), lambda m, n: (0, n))],
        out_specs=pl.BlockSpec((tm, K), lambda m, n: (m, 0)),
        out_shape=jax.ShapeDtypeStruct((M, K), f32),
        compiler_params=_cp(("parallel", "arbitrary")),
    )(dy, w)


def mm_tn(a, dy, name="mm_tn"):
    M, K = a.shape
    N = dy.shape[1]
    tm, tk, tn = min(512, M), _tile(K, 1408), _tile(N, 1024)

    def body(a_ref, dy_ref, o_ref):
        @pl.when(pl.program_id(2) == 0)
        def _():
            o_ref[...] = jnp.zeros_like(o_ref)
        o_ref[...] += _dg(a_ref[...].astype(bf16), dy_ref[...].astype(bf16), 0, 0)

    return pl.pallas_call(
        body, name=name, grid=(K // tk, N // tn, M // tm),
        in_specs=[pl.BlockSpec((tm, tk), lambda k, n, m: (m, k)), pl.BlockSpec((tm, tn), lambda k, n, m: (m, n))],
        out_specs=pl.BlockSpec((tk, tn), lambda k, n, m: (k, n)),
        out_shape=jax.ShapeDtypeStruct((K, N), f32),
        compiler_params=_cp(("parallel", "parallel", "arbitrary")),
    )(a, dy)


def rms_fwd(h, g, name):
    S = h.shape[0]
    t = min(512, S)

    def body(h_ref, g_ref, o_ref):
        o_ref[...] = _rms(h_ref[...], g_ref[...]).astype(bf16)

    return pl.pallas_call(
        body, name=name, grid=(S // t,),
        in_specs=[pl.BlockSpec((t, D), lambda i: (i, 0)), pl.BlockSpec((1, D), lambda i: (0, 0))],
        out_specs=pl.BlockSpec((t, D), lambda i: (i, 0)),
        out_shape=jax.ShapeDtypeStruct((S, D), bf16),
        compiler_params=_cp(("parallel",)),
    )(h, g.reshape(1, D))


def rms_bwd(h, g, dn, dres, name):
    S = h.shape[0]
    t = min(512, S)

    def body(h_ref, g_ref, dn_ref, dr_ref, dh_ref, dg_ref):
        @pl.when(pl.program_id(0) == 0)
        def _():
            dg_ref[...] = jnp.zeros_like(dg_ref)
        _, vjp = jax.vjp(_rms, h_ref[...], g_ref[...])
        dh, dg = vjp(dn_ref[...])
        dh_ref[...] = dr_ref[...] + dh
        dg_ref[...] += dg

    tok = pl.BlockSpec((t, D), lambda i: (i, 0))
    vec = pl.BlockSpec((1, D), lambda i: (0, 0))
    return pl.pallas_call(
        body, name=name, grid=(S // t,), in_specs=[tok, vec, tok, tok], out_specs=[tok, vec],
        out_shape=[jax.ShapeDtypeStruct((S, D), f32), jax.ShapeDtypeStruct((1, D), f32)],
        compiler_params=_cp(("arbitrary",)),
    )(h, g.reshape(1, D), dn, dres)


def loss_head(h, g, target):
    S = h.shape[0]
    t = min(512, S)

    def f(hh, gg, tt):
        err = _rms(hh, gg) - tt
        return 0.5 * jnp.sum(jnp.mean(err * err, axis=-1, keepdims=True), axis=0, keepdims=True)

    def body(h_ref, g_ref, t_ref, loss_ref, dh_ref, dg_ref):
        @pl.when(pl.program_id(0) == 0)
        def _():
            dg_ref[...] = jnp.zeros_like(dg_ref)
            loss_ref[...] = jnp.zeros_like(loss_ref)
        val, vjp = jax.vjp(lambda a, b: f(a, b, t_ref[...]), h_ref[...], g_ref[...])
        dh, dg = vjp(jnp.ones((1, 1), f32))
        dh_ref[...] = dh
        dg_ref[...] += dg
        loss_ref[...] += jnp.broadcast_to(val, loss_ref.shape)

    tok = pl.BlockSpec((t, D), lambda i: (i, 0))
    vec = pl.BlockSpec((1, D), lambda i: (0, 0))
    return pl.pallas_call(
        body, name="loss_head", grid=(S // t,), in_specs=[tok, vec, tok],
        out_specs=[pl.BlockSpec((1, LANE), lambda i: (0, 0)), tok, vec],
        out_shape=[jax.ShapeDtypeStruct((1, LANE), f32), jax.ShapeDtypeStruct((S, D), f32),
                   jax.ShapeDtypeStruct((1, D), f32)],
        compiler_params=_cp(("arbitrary",)),
    )(h, g.reshape(1, D), target)


def memkv_fwd(mem, g, w, name):
    def body(m_ref, g_ref, w_ref, o_ref):
        o_ref[...] = _dg(_rms(m_ref[...], g_ref[...]).astype(bf16), w_ref[...], 1, 0)

    return pl.pallas_call(
        body, name=name, out_shape=jax.ShapeDtypeStruct((MEM_LEN, 2 * X_Q), f32), compiler_params=_cp(),
    )(mem, g.reshape(1, D), w)


def memkv_bwd(mem, g, w, dkv, name):
    def body(m_ref, g_ref, w_ref, d_ref, dg_ref, dw_ref):
        n, vjp = jax.vjp(lambda gg: _rms(m_ref[...], gg), g_ref[...])
        db = d_ref[...].astype(bf16)
        dw_ref[...] = _dg(n.astype(bf16), db, 0, 0)
        dg_ref[...] = vjp(_dg(db, w_ref[...], 1, 1))[0]

    return pl.pallas_call(
        body, name=name,
        out_shape=[jax.ShapeDtypeStruct((1, D), f32), jax.ShapeDtypeStruct((D, 2 * X_Q), f32)],
        compiler_params=_cp(),
    )(mem, g.reshape(1, D), w, dkv)


def _xattn_f(xq, mk, mv):
    lane = lax.broadcasted_iota(jnp.int32, (1, X_Q), 1)
    out = jnp.zeros(xq.shape, f32)
    for hd in range(4):
        msk = (lane // 64 == hd).astype(f32)
        s = bdot_nt(xq * msk, mk) * (64 ** -0.5)
        m = lax.stop_gradient(jnp.max(s, axis=-1, keepdims=True))
        p = jnp.exp(s - m)
        p = p / jnp.sum(p, axis=-1, keepdims=True)
        out = out + bdot(p, mv * msk)
    return out


def xattn_fwd(proj, col, kv, name):
    S = proj.shape[0]
    t = min(512, S)
    cb = col // X_Q

    def body(q_ref, k_ref, v_ref, o_ref):
        o_ref[...] = _xattn_f(q_ref[...], k_ref[...], v_ref[...])

    return pl.pallas_call(
        body, name=name, grid=(S // t,),
        in_specs=[pl.BlockSpec((t, X_Q), lambda i: (i, cb)), pl.BlockSpec((MEM_LEN, X_Q), lambda i: (0, 0)),
                  pl.BlockSpec((MEM_LEN, X_Q), lambda i: (0, 1))],
        out_specs=pl.BlockSpec((t, X_Q), lambda i: (i, 0)),
        out_shape=jax.ShapeDtypeStruct((S, X_Q), f32),
        compiler_params=_cp(("parallel",)),
    )(proj, kv, kv)


def xattn_bwd(proj, col, kv, dmix, name):
    S = proj.shape[0]
    t = min(512, S)
    cb = col // X_Q

    def body(q_ref, k_ref, v_ref, do_ref, dq_ref, dk_ref, dv_ref):
        @pl.when(pl.program_id(0) == 0)
        def _():
            dk_ref[...] = jnp.zeros_like(dk_ref)
            dv_ref[...] = jnp.zeros_like(dv_ref)
        _, vjp = jax.vjp(_xattn_f, q_ref[...], k_ref[...], v_ref[...])
        dq, dk, dv = vjp(do_ref[...])
        dq_ref[...] = dq
        dk_ref[...] += dk
        dv_ref[...] += dv

    kvb = pl.BlockSpec((MEM_LEN, X_Q), lambda i: (0, 0))
    dq, dk, dv = pl.pallas_call(
        body, name=name, grid=(S // t,),
        in_specs=[pl.BlockSpec((t, X_Q), lambda i: (i, cb)), kvb,
                  pl.BlockSpec((MEM_LEN, X_Q), lambda i: (0, 1)), pl.BlockSpec((t, X_Q), lambda i: (i, 3))],
        out_specs=[pl.BlockSpec((t, X_Q), lambda i: (i, 0)), kvb, kvb],
        out_shape=[jax.ShapeDtypeStruct((S, X_Q), f32), jax.ShapeDtypeStruct((MEM_LEN, X_Q), f32),
                   jax.ShapeDtypeStruct((MEM_LEN, X_Q), f32)],
        compiler_params=_cp(("arbitrary",)),
    )(proj, kv, kv, dmix)
    return dq, jnp.concatenate([dk, dv], axis=1)


def _bucket_map():
    qi = np.arange(BLK)[:, None]
    kj = np.arange(2 * BLK)[None, :]
    n = np.maximum(BLK + qi - kj, 0)
    max_exact = N_BUCKETS // 2
    nf = np.maximum(n, 1).astype(np.float64)
    large = max_exact + (np.log(nf / max_exact) / math.log(MAX_DIST / max_exact)
                         * (N_BUCKETS - max_exact)).astype(np.int32)
    large = np.minimum(large, N_BUCKETS - 1)
    return np.where(n < max_exact, n, large).astype(np.int32)


def bias_build(rel_bias):
    def body(rb_ref, bk_ref, o_ref):
        bk = bk_ref[...]
        for h in range(A_HEADS):
            acc = jnp.zeros((BLK, 2 * BLK), f32)
            for b in range(N_BUCKETS):
                acc = jnp.where(bk == b, rb_ref[b, h], acc)
            o_ref[h] = acc

    return pl.pallas_call(
        body, name="bias_build",
        in_specs=[pl.BlockSpec(memory_space=pltpu.SMEM), pl.BlockSpec(memory_space=pltpu.VMEM)],
        out_specs=pl.BlockSpec(memory_space=pltpu.VMEM),
        out_shape=jax.ShapeDtypeStruct((A_HEADS, BLK, 2 * BLK), f32), compiler_params=_cp(),
    )(rel_bias, jnp.asarray(_bucket_map()))


def bias_grad(dbias):
    def body(d_ref, bk_ref, o_ref):
        bk = bk_ref[...]
        row = lax.broadcasted_iota(jnp.int32, (N_BUCKETS, LANE), 0)
        lane = lax.broadcasted_iota(jnp.int32, (N_BUCKETS, LANE), 1)
        acc = jnp.zeros((N_BUCKETS, LANE), f32)
        for h in range(A_HEADS):
            d = d_ref[h]
            for b in range(N_BUCKETS):
                s = jnp.sum(jnp.where(bk == b, d, 0.0), keepdims=True)
                acc = acc + jnp.where((row == b) & (lane == h), s, 0.0)
        o_ref[...] = acc

    return pl.pallas_call(
        body, name="bias_grad", out_shape=jax.ShapeDtypeStruct((N_BUCKETS, LANE), f32), compiler_params=_cp(),
    )(dbias, jnp.asarray(_bucket_map()))


def _swa_f(qb, kp, kc, vp, vc, bias, sk, first):
    kband = jnp.concatenate([kp, kc], axis=0)
    vband = jnp.concatenate([vp, vc], axis=0)
    qi = lax.broadcasted_iota(jnp.int32, (BLK, 2 * BLK), 0)
    kj = lax.broadcasted_iota(jnp.int32, (BLK, 2 * BLK), 1)
    rel = kj - qi
    ok = (rel >= 1) & (rel <= BLK) & ((kj >= BLK) | jnp.logical_not(first))
    lane = lax.broadcasted_iota(jnp.int32, (1, LANE), 1)
    lane_b = lax.broadcasted_iota(jnp.int32, (BLK, LANE), 1)
    outs = []
    for p in range(A_HEADS // 2):
        qp = qb[:, LANE * p:LANE * (p + 1)]
        acc = jnp.zeros((BLK, LANE), f32)
        for g in range(2):
            h = g * (A_HEADS // 2) + p
            msk = (lane // A_DH == g).astype(f32)
            s = bdot_nt(qp * msk, kband) * (A_DH ** -0.5) + bias[h]
            s = jnp.where(ok, s, -1e30)
            skb = jnp.broadcast_to(sk[h:h + 1, :], (BLK, LANE))
            sink = jnp.sum(jnp.where(lane_b == 0, skb, 0.0), axis=-1, keepdims=True)
            m = lax.stop_gradient(jnp.maximum(jnp.max(s, axis=-1, keepdims=True), sink))
            e = jnp.exp(s - m)
            prob = e / (jnp.sum(e, axis=-1, keepdims=True) + jnp.exp(sink - m))
            acc = acc + bdot(prob, vband) * msk
        outs.append(acc)
    return jnp.concatenate(outs, axis=1)


def _swa_specs(nb, rev):
    bi = (lambda i: nb - 1 - i) if rev else (lambda i: i)
    return [
        pl.BlockSpec((BLK, A_Q), lambda i: (bi(i), 0)),
        pl.BlockSpec((BLK, LANE), lambda i: (jnp.maximum(bi(i) - 1, 0), 6)),
        pl.BlockSpec((BLK, LANE), lambda i: (bi(i), 6)),
        pl.BlockSpec((BLK, LANE), lambda i: (jnp.maximum(bi(i) - 1, 0), 7)),
        pl.BlockSpec((BLK, LANE), lambda i: (bi(i), 7)),
        pl.BlockSpec((A_HEADS, BLK, 2 * BLK), lambda i: (0, 0, 0)),
        pl.BlockSpec((16, LANE), lambda i: (0, 0)),
    ]


def swa_fwd(proj, bias, sk):
    S = proj.shape[0]
    nb = S // BLK

    def body(q_ref, kp_ref, kc_ref, vp_ref, vc_ref, b_ref, s_ref, o_ref):
        o_ref[...] = _swa_f(q_ref[...], kp_ref[...], kc_ref[...], vp_ref[...], vc_ref[...], b_ref[...], s_ref[...],
                            pl.program_id(0) == 0)

    return pl.pallas_call(
        body, name="swa_fwd", grid=(nb,), in_specs=_swa_specs(nb, False),
        out_specs=pl.BlockSpec((BLK, A_Q), lambda i: (i, 0)),
        out_shape=jax.ShapeDtypeStruct((S, A_Q), f32), compiler_params=_cp(("parallel",)),
    )(proj, proj, proj, proj, proj, bias, sk)


def swa_bwd(proj, bias, sk, dmix):
    S = proj.shape[0]
    nb = S // BLK

    def body(q_ref, kp_ref, kc_ref, vp_ref, vc_ref, b_ref, s_ref, do_ref, dqkv_ref, db_ref, ds_ref, ck, cv):
        i = pl.program_id(0)

        @pl.when(i == 0)
        def _():
            db_ref[...] = jnp.zeros_like(db_ref)
            ds_ref[...] = jnp.zeros_like(ds_ref)
            ck[...] = jnp.zeros_like(ck)
            cv[...] = jnp.zeros_like(cv)
        first = i == nb - 1
        _, vjp = jax.vjp(lambda *a: _swa_f(*a, first), q_ref[...], kp_ref[...], kc_ref[...], vp_ref[...],
                         vc_ref[...], b_ref[...], s_ref[...])
        dq, dkp, dkc, dvp, dvc, db, ds = vjp(do_ref[...])
        dqkv_ref[...] = jnp.concatenate([dq, dkc + ck[...], dvc + cv[...]], axis=1)
        ck[...] = dkp
        cv[...] = dvp
        db_ref[...] += db
        ds_ref[...] += ds

    return pl.pallas_call(
        body, name="swa_bwd", grid=(nb,),
        in_specs=_swa_specs(nb, True) + [pl.BlockSpec((BLK, A_Q), lambda i: (nb - 1 - i, 0))],
        out_specs=[pl.BlockSpec((BLK, D), lambda i: (nb - 1 - i, 0)),
                   pl.BlockSpec((A_HEADS, BLK, 2 * BLK), lambda i: (0, 0, 0)),
                   pl.BlockSpec((16, LANE), lambda i: (0, 0))],
        out_shape=[jax.ShapeDtypeStruct((S, D), f32), jax.ShapeDtypeStruct((A_HEADS, BLK, 2 * BLK), f32),
                   jax.ShapeDtypeStruct((16, LANE), f32)],
        scratch_shapes=[pltpu.VMEM((BLK, LANE), f32), pltpu.VMEM((BLK, LANE), f32)],
        compiler_params=_cp(("arbitrary",)),
    )(proj, proj, proj, proj, proj, bias, sk, dmix)


def _dnprep_f(x, w, is_qk):
    c = (w[3:4] * x + w[2:3] * shift_down(x, 1) + w[1:2] * shift_down(x, 2) + w[0:1] * shift_down(x, 3))
    a = _silu(c)
    n = a * lax.rsqrt(jnp.sum(a * a, axis=-1, keepdims=True) + EPS)
    return jnp.where(is_qk, n, a)


def dnprep_fwd(proj, cw):
    S = proj.shape[0]
    nblk = B_QKV // LANE

    def body(x_ref, w_ref, o_ref):
        o_ref[...] = _dnprep_f(x_ref[...], w_ref[...], pl.program_id(0) < 2 * B_QK // LANE)

    return pl.pallas_call(
        body, name="dnprep_fwd", grid=(nblk,),
        in_specs=[pl.BlockSpec((S, LANE), lambda j: (0, j)), pl.BlockSpec((4, LANE), lambda j: (0, j))],
        out_specs=pl.BlockSpec((S, LANE), lambda j: (0, j)),
        out_shape=jax.ShapeDtypeStruct((S, B_QKV), f32), compiler_params=_cp(("parallel",)),
    )(proj, cw)


def dnprep_bwd(proj, cw, dqkvn):
    S = proj.shape[0]
    nblk = B_QKV // LANE

    def body(x_ref, w_ref, d_ref, dx_ref, dw_ref):
        is_qk = pl.program_id(0) < 2 * B_QK // LANE
        _, vjp = jax.vjp(lambda a, b: _dnprep_f(a, b, is_qk), x_ref[...], w_ref[...])
        dx, dw = vjp(d_ref[...])
        dx_ref[...] = dx
        dw_ref[...] = dw

    col = pl.BlockSpec((S, LANE), lambda j: (0, j))
    wsp = pl.BlockSpec((4, LANE), lambda j: (0, j))
    return pl.pallas_call(
        body, name="dnprep_bwd", grid=(nblk,), in_specs=[col, wsp, col], out_specs=[col, wsp],
        out_shape=[jax.ShapeDtypeStruct((S, B_QKV), f32), jax.ShapeDtypeStruct((4, B_QKV), f32)],
        compiler_params=_cp(("parallel",)),
    )(proj, cw, dqkvn)


def _hdot(a, b, ca=1, cb=0):
    return _dg(a, b, ca, cb, HI)


def _bdg(a, b, ca, cb):
    return lax.dot_general(a, b, (((ca,), (cb,)), ((0,), (0,))), precision=HI, preferred_element_type=f32)


@jax.custom_vjp
def hbd(a, b):
    return _bdg(a, b, 2, 1)


@jax.custom_vjp
def hbd_nt(a, b):
    return _bdg(a, b, 2, 2)


@jax.custom_vjp
def hbd_tn(a, b):
    return _bdg(a, b, 1, 1)


hbd.defvjp(lambda a, b: (hbd(a, b), (a, b)), lambda r, g: (hbd_nt(g, r[1]), hbd_tn(r[0], g)))
hbd_nt.defvjp(lambda a, b: (hbd_nt(a, b), (a, b)), lambda r, g: (hbd(g, r[1]), hbd_tn(g, r[0])))
hbd_tn.defvjp(lambda a, b: (hbd_tn(a, b), (a, b)), lambda r, g: (hbd_nt(r[1], g), hbd(r[0], g)))


def _stack(xs):
    return jnp.concatenate([x[None] for x in xs], axis=0)


def _lane_col(x, j):
    lane = lax.broadcasted_iota(jnp.int32, (1, LANE), 1)
    return jnp.sum(jnp.where(lane == j, x, 0.0), axis=-1, keepdims=True)


def _dnc_f(q, k, v, seg, prm):
    C = CHUNK
    beta_all = _sigmoid(seg)
    xx = seg + prm[1:2]
    g_all = -jnp.exp(prm[0:1]) * (jnp.maximum(xx, 0.0) + jnp.log(1.0 + jnp.exp(-jnp.abs(xx))))
    r2 = lax.broadcasted_iota(jnp.int32, (C, C), 0)
    c2 = lax.broadcasted_iota(jnp.int32, (C, C), 1)
    gc_all = _hdot((r2 >= c2).astype(f32), g_all)
    beta = _stack([_lane_col(beta_all, h) for h in range(6)])
    gc = _stack([_lane_col(gc_all, 6 + h) for h in range(6)])
    r = lax.broadcasted_iota(jnp.int32, (1, C, C), 1)
    c = lax.broadcasted_iota(jnp.int32, (1, C, C), 2)
    incl = r >= c
    strict = r > c
    eye = (r == c).astype(f32)
    g_row = hbd(jnp.ones((6, C, C), f32), eye * gc)
    decay = jnp.where(incl, jnp.exp(jnp.where(incl, gc - g_row, 0.0)), 0.0)
    a_mat = beta * hbd_nt(k, k) * jnp.where(strict, decay, 0.0)
    eg = jnp.exp(gc)
    pw = -a_mat
    inv = eye + pw
    for _ in range(5):
        pw = hbd(pw, pw)
        inv = inv + hbd(inv, pw)
    u = hbd(inv, beta * v)
    w = hbd(inv, (beta * eg) * k)
    qc = q * (B_DH ** -0.5)
    attn = hbd_nt(qc, k) * decay
    last = (lax.broadcasted_iota(jnp.int32, (1, C, 1), 1) == C - 1).astype(f32)
    g_last = jnp.sum(gc * last, axis=1, keepdims=True)
    dc = jnp.broadcast_to(jnp.exp(g_last), (6, 1, LANE)).reshape(6, LANE)
    return u, w, qc * eg, k * jnp.exp(g_last - gc), attn, dc


def _dns_f(S0, u, w, qd, kt, attn, dcrows):
    dc = _lane_col(dcrows, 0).reshape(6, 1, 1)
    delta = u - hbd(w, S0)
    out = hbd(qd, S0) + hbd(attn, delta)
    return out, dc * S0 + hbd_tn(kt, delta)


def _dnpost_f(o, z, grow):
    outs = []
    for h in range(6):
        oh = o[:, LANE * h:LANE * (h + 1)]
        outs.append(oh * lax.rsqrt(jnp.mean(oh * oh, axis=-1, keepdims=True) + EPS) * grow
                    * _silu(z[:, LANE * h:LANE * (h + 1)]))
    return jnp.concatenate(outs, axis=1)


def _hs(h):
    return slice(LANE * h, LANE * (h + 1))


def _heads(ref, share):
    return _stack([ref[:, _hs(h // share)] for h in range(6)])


def _put_heads(ref, val):
    for h in range(6):
        ref[:, _hs(h)] = val[h]


def _dnc_in_specs():
    return [
        pl.BlockSpec((CHUNK, B_QK), lambda n: (n, 0)),
        pl.BlockSpec((CHUNK, B_QK), lambda n: (n, 1)),
        pl.BlockSpec((CHUNK, B_V), lambda n: (n, 1)),
        pl.BlockSpec((CHUNK, LANE), lambda n: (n, 20)),
        pl.BlockSpec((8, LANE), lambda n: (0, 0)),
    ]


def _dnc_out_specs(rev_nc=None):
    ci = (lambda n: n) if rev_nc is None else (lambda n: rev_nc - 1 - n)
    wide = pl.BlockSpec((CHUNK, B_V), lambda n: (ci(n), 0))
    return [wide, wide, wide, wide, pl.BlockSpec((1, 6, CHUNK, CHUNK), lambda n: (ci(n), 0, 0, 0)),
            pl.BlockSpec((1, 8, LANE), lambda n: (ci(n), 0, 0))]


def _dnc_shapes(S):
    nc = S // CHUNK
    wide = jax.ShapeDtypeStruct((S, B_V), f32)
    return [wide, wide, wide, wide, jax.ShapeDtypeStruct((nc, 6, CHUNK, CHUNK), f32),
            jax.ShapeDtypeStruct((nc, 8, LANE), f32)]


def dnc_fwd(qkvn, proj, prm):
    S = proj.shape[0]

    def body(q_ref, k_ref, v_ref, s_ref, p_ref, u_ref, w_ref, qd_ref, kt_ref, at_ref, dc_ref):
        u, w, qd, kt, attn, dc = _dnc_f(_heads(q_ref, 2), _heads(k_ref, 2), _heads(v_ref, 1), s_ref[...], p_ref[...])
        _put_heads(u_ref, u)
        _put_heads(w_ref, w)
        _put_heads(qd_ref, qd)
        _put_heads(kt_ref, kt)
        at_ref[0] = attn
        dc_ref[0] = jnp.concatenate([dc, jnp.zeros((2, LANE), f32)], axis=0)

    return pl.pallas_call(
        body, name="dn_chunk_fwd", grid=(S // CHUNK,), in_specs=_dnc_in_specs(), out_specs=_dnc_out_specs(),
        out_shape=_dnc_shapes(S), compiler_params=_cp(("parallel",)),
    )(qkvn, qkvn, qkvn, proj, prm)


def dnc_bwd(qkvn, proj, prm, cots):
    S = proj.shape[0]

    def body(q_ref, k_ref, v_ref, s_ref, p_ref, du_ref, dw_ref, dqd_ref, dkt_ref, dat_ref, ddc_ref,
             dx_ref, dseg_ref, dprm_ref):
        @pl.when(pl.program_id(0) == 0)
        def _():
            dprm_ref[...] = jnp.zeros_like(dprm_ref)
        _, vjp = jax.vjp(_dnc_f, _heads(q_ref, 2), _heads(k_ref, 2), _heads(v_ref, 1), s_ref[...], p_ref[...])
        dq, dk, dv, dseg, dprm = vjp((_heads(du_ref, 1), _heads(dw_ref, 1), _heads(dqd_ref, 1), _heads(dkt_ref, 1),
                                      dat_ref[0], ddc_ref[0, 0:6, :]))
        dx_ref[...] = jnp.concatenate([dq[0] + dq[1], dq[2] + dq[3], dq[4] + dq[5],
                                       dk[0] + dk[1], dk[2] + dk[3], dk[4] + dk[5]] + [dv[h] for h in range(6)], axis=1)
        dseg_ref[...] = dseg
        dprm_ref[...] += dprm

    return pl.pallas_call(
        body, name="dn_chunk_bwd", grid=(S // CHUNK,), in_specs=_dnc_in_specs() + _dnc_out_specs(),
        out_specs=[pl.BlockSpec((CHUNK, B_QKV), lambda n: (n, 0)), pl.BlockSpec((CHUNK, LANE), lambda n: (n, 0)),
                   pl.BlockSpec((8, LANE), lambda n: (0, 0))],
        out_shape=[jax.ShapeDtypeStruct((S, B_QKV), f32), jax.ShapeDtypeStruct((S, LANE), f32),
                   jax.ShapeDtypeStruct((8, LANE), f32)],
        compiler_params=_cp(("arbitrary",)),
    )(qkvn, qkvn, qkvn, proj, prm, *cots)


def dns_fwd(chunked):
    u = chunked[0]
    S = u.shape[0]
    nc = S // CHUNK

    def body(u_ref, w_ref, qd_ref, kt_ref, at_ref, dc_ref, o_ref, st_ref, st):
        @pl.when(pl.program_id(0) == 0)
        def _():
            st[...] = jnp.zeros_like(st)
        S0 = st[...]
        st_ref[0] = S0
        out, S1 = _dns_f(S0, _heads(u_ref, 1), _heads(w_ref, 1), _heads(qd_ref, 1), _heads(kt_ref, 1),
                         at_ref[0], dc_ref[0, 0:6, :])
        _put_heads(o_ref, out)
        st[...] = S1

    return pl.pallas_call(
        body, name="dn_scan_fwd", grid=(nc,), in_specs=_dnc_out_specs(),
        out_specs=[pl.BlockSpec((CHUNK, B_V), lambda n: (n, 0)),
                   pl.BlockSpec((1, 6, B_DH, B_DH), lambda n: (n, 0, 0, 0))],
        out_shape=[jax.ShapeDtypeStruct((S, B_V), f32), jax.ShapeDtypeStruct((nc, 6, B_DH, B_DH), f32)],
        scratch_shapes=[pltpu.VMEM((6, B_DH, B_DH), f32)],
        compiler_params=_cp(("arbitrary",)),
    )(*chunked)


def dns_bwd(chunked, states, do):
    S = do.shape[0]
    nc = S // CHUNK

    def body(u_ref, w_ref, qd_ref, kt_ref, at_ref, dc_ref, st_ref, do_ref,
             du_ref, dw_ref, dqd_ref, dkt_ref, dat_ref, ddc_ref, dst):
        @pl.when(pl.program_id(0) == 0)
        def _():
            dst[...] = jnp.zeros_like(dst)
        _, vjp = jax.vjp(_dns_f, st_ref[0], _heads(u_ref, 1), _heads(w_ref, 1), _heads(qd_ref, 1), _heads(kt_ref, 1),
                         at_ref[0], dc_ref[0, 0:6, :])
        dS0, du, dw, dqd, dkt, dat, ddc = vjp((_heads(do_ref, 1), dst[...]))
        dst[...] = dS0
        _put_heads(du_ref, du)
        _put_heads(dw_ref, dw)
        _put_heads(dqd_ref, dqd)
        _put_heads(dkt_ref, dkt)
        dat_ref[0] = dat
        ddc_ref[0] = jnp.concatenate([ddc, jnp.zeros((2, LANE), f32)], axis=0)

    return pl.pallas_call(
        body, name="dn_scan_bwd", grid=(nc,),
        in_specs=_dnc_out_specs(nc) + [pl.BlockSpec((1, 6, B_DH, B_DH), lambda n: (nc - 1 - n, 0, 0, 0)),
                                       pl.BlockSpec((CHUNK, B_V), lambda n: (nc - 1 - n, 0))],
        out_specs=_dnc_out_specs(nc), out_shape=_dnc_shapes(S),
        scratch_shapes=[pltpu.VMEM((6, B_DH, B_DH), f32)],
        compiler_params=_cp(("arbitrary",)),
    )(*chunked, states, do)


def dnpost_fwd(o, proj, prm):
    S = o.shape[0]
    t = min(512, S)

    def body(o_ref, z_ref, p_ref, y_ref):
        y_ref[...] = _dnpost_f(o_ref[...], z_ref[...], p_ref[2:3, :])

    tok = pl.BlockSpec((t, B_V), lambda i: (i, 0))
    return pl.pallas_call(
        body, name="dn_post_fwd", grid=(S // t,),
        in_specs=[tok, pl.BlockSpec((t, B_V), lambda i: (i, 2)), pl.BlockSpec((8, LANE), lambda i: (0, 0))],
        out_specs=tok, out_shape=jax.ShapeDtypeStruct((S, B_V), f32), compiler_params=_cp(("parallel",)),
    )(o, proj, prm)


def dnpost_bwd(o, proj, prm, dmix):
    S = o.shape[0]
    t = min(512, S)

    def body(o_ref, z_ref, p_ref, dy_ref, do_ref, dz_ref, dg_ref):
        @pl.when(pl.program_id(0) == 0)
        def _():
            dg_ref[...] = jnp.zeros_like(dg_ref)
        _, vjp = jax.vjp(_dnpost_f, o_ref[...], z_ref[...], p_ref[2:3, :])
        do, dz, dg = vjp(dy_ref[...])
        do_ref[...] = do
        dz_ref[...] = dz
        dg_ref[...] += dg

    tok = pl.BlockSpec((t, B_V), lambda i: (i, 0))
    return pl.pallas_call(
        body, name="dn_post_bwd", grid=(S // t,),
        in_specs=[tok, pl.BlockSpec((t, B_V), lambda i: (i, 2)), pl.BlockSpec((8, LANE), lambda i: (0, 0)), tok],
        out_specs=[tok, tok, pl.BlockSpec((1, LANE), lambda i: (0, 0))],
        out_shape=[jax.ShapeDtypeStruct((S, B_V), f32), jax.ShapeDtypeStruct((S, B_V), f32),
                   jax.ShapeDtypeStruct((1, LANE), f32)],
        compiler_params=_cp(("arbitrary",)),
    )(o, proj, prm, dmix)


def _glu_f(gu, w, b):
    gate, up = gu[:, :LANE], gu[:, LANE:]
    c = w[2:3] * gate + w[1:2] * shift_down(gate, 1) + w[0:1] * shift_down(gate, 2) + b
    return _silu(c) * up


def glu_fwd(gu, w, b, name):
    S = gu.shape[0]
    nblk = D_FF // LANE

    def body(g_ref, w_ref, b_ref, o_ref):
        o_ref[...] = _glu_f(g_ref[...], w_ref[...], b_ref[...]).astype(bf16)

    return pl.pallas_call(
        body, name=name, grid=(nblk,),
        in_specs=[pl.BlockSpec((S, 2 * LANE), lambda j: (0, j)), pl.BlockSpec((3, LANE), lambda j: (0, j)),
                  pl.BlockSpec((1, LANE), lambda j: (0, j))],
        out_specs=pl.BlockSpec((S, LANE), lambda j: (0, j)),
        out_shape=jax.ShapeDtypeStruct((S, D_FF), bf16), compiler_params=_cp(("parallel",)),
    )(gu, w, b.reshape(1, D_FF))


def glu_bwd(gu, w, b, dact, name):
    S = gu.shape[0]
    nblk = D_FF // LANE

    def body(g_ref, w_ref, b_ref, d_ref, dg_ref, dw_ref, db_ref):
        _, vjp = jax.vjp(_glu_f, g_ref[...], w_ref[...], b_ref[...])
        dg, dw, db = vjp(d_ref[...])
        dg_ref[...] = dg.astype(bf16)
        dw_ref[...] = dw
        db_ref[...] = db

    gsp = pl.BlockSpec((S, 2 * LANE), lambda j: (0, j))
    wsp = pl.BlockSpec((3, LANE), lambda j: (0, j))
    bsp = pl.BlockSpec((1, LANE), lambda j: (0, j))
    return pl.pallas_call(
        body, name=name, grid=(nblk,),
        in_specs=[gsp, wsp, bsp, pl.BlockSpec((S, LANE), lambda j: (0, j))], out_specs=[gsp, wsp, bsp],
        out_shape=[jax.ShapeDtypeStruct((S, 2 * D_FF), bf16), jax.ShapeDtypeStruct((3, D_FF), f32),
                   jax.ShapeDtypeStruct((1, D_FF), f32)],
        compiler_params=_cp(("parallel",)),
    )(gu, w, b.reshape(1, D_FF), dact)


def _pair_cols(w):
    lead = w.shape[:-1]
    return w.reshape(lead + (2, 6, A_DH)).swapaxes(-3, -2).reshape(lead + (A_Q,))


def _unpair_cols(w):
    lead = w.shape[:-1]
    return w.reshape(lead + (6, 2, A_DH)).swapaxes(-3, -2).reshape(lead + (A_Q,))


def _lay_in_a(w):
    return jnp.concatenate([_pair_cols(w[:, :A_Q]), w[:, A_Q:]], axis=1)


def _unlay_in_a(w):
    return jnp.concatenate([_unpair_cols(w[:, :A_Q]), w[:, A_Q:]], axis=1)


def _lay_out_a(w):
    return jnp.concatenate([_pair_cols(w[:A_Q].T).T, w[A_Q:]], axis=0)


def _unlay_out_a(w):
    return jnp.concatenate([_unpair_cols(w[:A_Q].T).T, w[A_Q:]], axis=0)


def _lay_in_b(w):
    return jnp.concatenate([w[:, :2304], w[:, 2316:], w[:, 2304:2316],
                            jnp.zeros((w.shape[0], LANE - 12), w.dtype)], axis=1)


def _unlay_in_b(w):
    return jnp.concatenate([w[:, :2304], w[:, 2560:2572], w[:, 2304:2560]], axis=1)


def _lay_gu(w):
    return w.reshape(D, 2, D_FF // LANE, LANE).swapaxes(1, 2).reshape(D, 2 * D_FF)


def _unlay_gu(w):
    return w.reshape(D, D_FF // LANE, 2, LANE).swapaxes(1, 2).reshape(D, 2 * D_FF)


def _local_step(x, mem, target, P):
    sk = jnp.zeros((16, LANE), f32).at[:A_HEADS].set(jnp.broadcast_to(P["sinks"][:, None], (A_HEADS, LANE)))
    prm = jnp.zeros((8, LANE), f32).at[0, 6:12].set(P["a_log"]).at[1, 6:12].set(P["dt_bias"]).at[2].set(P["out_norm_g"])
    bias = bias_build(P["rel_bias"])
    saved = []
    h = x
    for i in range(2):
        n1 = rms_fwd(h, P["g_mix"][i], f"rms_mix{i}")
        kv = memkv_fwd(mem, P["g_mem"][i], P["w_mem"][i], f"memkv{i}")
        if i == 0:
            proj = mm_nn(n1, P["w_in_a"], name="proj_a")
            self_out = swa_fwd(proj, bias, sk)
            cross = xattn_fwd(proj, A_Q + 2 * LANE, kv, "xattn_a")
            extra = ()
        else:
            proj = mm_nn(n1, P["w_in_b"], name="proj_b")
            qkvn = dnprep_fwd(proj, P["conv_qkv"])
            chunked = dnc_fwd(qkvn, proj, prm)
            o, states = dns_fwd(chunked)
            self_out = dnpost_fwd(o, proj, prm)
            cross = xattn_fwd(proj, 2304, kv, "xattn_b")
            extra = (qkvn, chunked, states, o)
        mix = jnp.concatenate([self_out, cross], axis=1)
        h2 = mm_nn(mix, P["w_out"][i], res=h, name=f"out_proj{i}")
        n2 = rms_fwd(h2, P["g_ffn"][i], f"rms_ffn{i}")
        gu = mm_nn(n2, P["w_gu"][i], name=f"gate_up{i}")
        act = glu_fwd(gu, P["ffn_cw"][i], P["ffn_cb"][i], f"glu{i}")
        h3 = mm_nn(act, P["w_down"][i], res=h2, name=f"down{i}")
        saved.append((h, n1, kv, proj, mix, h2, n2, gu, act, extra))
        h = h3

    loss, dh, dg_fin = loss_head(h, P["g_fin"], target)
    G = {"g_fin": dg_fin[0], "g_mix": [None, None], "g_mem": [None, None], "g_ffn": [None, None],
         "w_mem": [None, None], "w_out": [None, None], "w_gu": [None, None], "w_down": [None, None],
         "ffn_cw": [None, None], "ffn_cb": [None, None]}
    for i in (1, 0):
        hin, n1, kv, proj, mix, h2, n2, gu, act, extra = saved[i]
        dact = mm_nt(dh, P["w_down"][i], name=f"d_act{i}")
        G["w_down"][i] = mm_tn(act, dh, name=f"dw_down{i}")
        dgu, dcw, dcb = glu_bwd(gu, P["ffn_cw"][i], P["ffn_cb"][i], dact, f"glu_bwd{i}")
        G["ffn_cw"][i], G["ffn_cb"][i] = dcw, dcb[0]
        dn2 = mm_nt(dgu, P["w_gu"][i], name=f"d_n2_{i}")
        G["w_gu"][i] = mm_tn(n2, dgu, name=f"dw_gu{i}")
        dh2, dg = rms_bwd(h2, P["g_ffn"][i], dn2, dh, f"rms_ffn_bwd{i}")
        G["g_ffn"][i] = dg[0]
        dmix = mm_nt(dh2, P["w_out"][i], name=f"d_mix{i}")
        G["w_out"][i] = mm_tn(mix, dh2, name=f"dw_out{i}")
        if i == 0:
            dqkv, dbias, dsk = swa_bwd(proj, bias, sk, dmix)
            dxq, dkv = xattn_bwd(proj, A_Q + 2 * LANE, kv, dmix, "xattn_a_bwd")
            dproj = jnp.concatenate([dqkv, dxq], axis=1)
            G["sinks"] = dsk[:A_HEADS, 0]
            G["rel_bias"] = bias_grad(dbias)[:, :A_HEADS]
            w_in, gname = P["w_in_a"], "w_in_a"
        else:
            qkvn, chunked, states, o = extra
            do, dz, dgo = dnpost_bwd(o, proj, prm, dmix)
            dqkvn, dseg, dprm = dnc_bwd(qkvn, proj, prm, dns_bwd(chunked, states, do))
            draw, dconv = dnprep_bwd(proj, P["conv_qkv"], dqkvn)
            dxq, dkv = xattn_bwd(proj, 2304, kv, dmix, "xattn_b_bwd")
            dproj = jnp.concatenate([draw, dz, dxq, dseg], axis=1)
            G["conv_qkv"] = dconv
            G["a_log"], G["dt_bias"], G["out_norm_g"] = dprm[0, 6:12], dprm[1, 6:12], dgo[0]
            w_in, gname = P["w_in_b"], "w_in_b"
        dn1 = mm_nt(dproj, w_in, name=f"d_n1_{i}")
        G[gname] = mm_tn(n1, dproj, name=f"d{gname}")
        dh, dg = rms_bwd(hin, P["g_mix"][i], dn1, dh2, f"rms_mix_bwd{i}")
        G["g_mix"][i] = dg[0]
        dgm, dwm = memkv_bwd(mem, P["g_mem"][i], P["w_mem"][i], dkv, f"memkv_bwd{i}")
        G["g_mem"][i], G["w_mem"][i] = dgm[0], dwm
    return loss, dh, G


def _prepare(full):
    return {
        "rel_bias": full["rel_bias"], "sinks": full["sinks_a"][0], "a_log": full["a_log_b"][0],
        "dt_bias": full["dt_bias_b"][0], "out_norm_g": full["out_norm_g_b"][0],
        "g_mix": full["norm_mix_g"], "g_mem": full["norm_mem_g"], "g_ffn": full["norm_ffn_g"],
        "g_fin": full["final_norm_g"], "conv_qkv": full["conv_qkv_b"][0],
        "ffn_cw": [full["ffn_conv_w"][0], full["ffn_conv_w"][1]],
        "ffn_cb": [full["ffn_conv_b"][0], full["ffn_conv_b"][1]],
        "w_mem": [full["w_mem_kv"][0], full["w_mem_kv"][1]],
        "w_out": [_lay_out_a(full["w_out"][0]), full["w_out"][1]],
        "w_in_a": _lay_in_a(full["w_in_a"][0]), "w_in_b": _lay_in_b(full["w_in_b"][0]),
        "w_gu": [_lay_gu(full["w_gate_up"][0]), _lay_gu(full["w_gate_up"][1])],
        "w_down": [full["w_down"][0], full["w_down"][1]],
    }


def _grads_to_ref(G):
    return {
        "rel_bias": G["rel_bias"], "norm_mix_g": jnp.stack(G["g_mix"]), "norm_mem_g": jnp.stack(G["g_mem"]),
        "w_mem_kv": jnp.stack(G["w_mem"]),
        "w_out": jnp.stack([_unlay_out_a(G["w_out"][0]), G["w_out"][1]]),
        "w_in_a": _unlay_in_a(G["w_in_a"])[None], "sinks_a": G["sinks"][None],
        "w_in_b": _unlay_in_b(G["w_in_b"])[None], "conv_qkv_b": G["conv_qkv"][None],
        "a_log_b": G["a_log"][None], "dt_bias_b": G["dt_bias"][None], "out_norm_g_b": G["out_norm_g"][None],
        "norm_ffn_g": jnp.stack(G["g_ffn"]),
        "w_gate_up": jnp.stack([_unlay_gu(G["w_gu"][0]), _unlay_gu(G["w_gu"][1])]),
        "ffn_conv_w": jnp.stack(G["ffn_cw"]), "ffn_conv_b": jnp.stack(G["ffn_cb"]),
        "w_down": jnp.stack(G["w_down"]), "final_norm_g": G["g_fin"],
    }


ANY = pl.BlockSpec(memory_space=pl.ANY)


def _place():
    return lax.axis_index("x"), lax.axis_index("y"), lax.axis_index("c")


def _chip_exchange(bufs, out_shapes, src_of, name):
    n = len(bufs)

    def body(*refs):
        ins, outs = refs[:n], refs[n:2 * n]
        ssem, rsem, lsem = refs[2 * n:]
        x, y, c = _place()
        me = 2 * x + y
        peers = [(1 - x, y), (x, 1 - y), (1 - x, 1 - y)]

        def remote(j, k, slot):
            px, py = peers[k]
            return pltpu.make_async_remote_copy(
                src_ref=src_of(j, ins[j], 2 * px + py), dst_ref=outs[j].at[slot],
                send_sem=ssem.at[3 * j + k], recv_sem=rsem.at[3 * j + k],
                device_id=(px, py, c), device_id_type=MESH)

        started = []
        for j in range(n):
            lc = pltpu.make_async_copy(src_of(j, ins[j], me), outs[j].at[me], lsem.at[j])
            lc.start()
            started.append(lc)
        sends = [remote(j, k, me) for j in range(n) for k in range(3)]
        for cp in sends:
            cp.start()
        for j in range(n):
            for k in range(3):
                px, py = peers[k]
                remote(j, k, 2 * px + py).wait_recv()
        for cp in sends:
            cp.wait_send()
        for lc in started:
            lc.wait()

    return pl.pallas_call(
        body, name=name, in_specs=[ANY] * n, out_specs=[ANY] * n, out_shape=out_shapes,
        scratch_shapes=[pltpu.SemaphoreType.DMA((3 * n,)), pltpu.SemaphoreType.DMA((3 * n,)),
                        pltpu.SemaphoreType.DMA((n,))],
    )(*bufs)


def allgather_chips(bufs):
    shapes = [jax.ShapeDtypeStruct((4,) + b.shape, b.dtype) for b in bufs]
    return _chip_exchange(bufs, shapes, lambda j, ref, chip: ref, "weight_allgather")


def chip_scatter(g):
    return _chip_exchange([g], [jax.ShapeDtypeStruct(g.shape, g.dtype)], lambda j, ref, chip: ref.at[chip],
                          "grad_scatter")[0]


def sibling_exchange(p):
    def body(p_ref, o_ref, ssem, rsem):
        x, y, c = _place()
        cp = pltpu.make_async_remote_copy(src_ref=p_ref, dst_ref=o_ref, send_sem=ssem, recv_sem=rsem,
                                          device_id=(x, y, 1 - c), device_id_type=MESH)
        cp.start()
        cp.wait()

    return pl.pallas_call(
        body, name="sibling_exchange", in_specs=[ANY], out_specs=ANY,
        out_shape=jax.ShapeDtypeStruct(p.shape, p.dtype),
        scratch_shapes=[pltpu.SemaphoreType.DMA, pltpu.SemaphoreType.DMA],
    )(p)


def allreduce_small(buf):
    R = buf.shape[0]

    def body(b_ref, o_ref, recv, ssem, rsem):
        x, y, c = _place()
        me = 4 * x + 2 * y + c

        def peer(k):
            return (1 - x if k & 4 else x, 1 - y if k & 2 else y, 1 - c if k & 1 else c)

        def remote(k, slot):
            return pltpu.make_async_remote_copy(
                src_ref=b_ref, dst_ref=recv.at[slot], send_sem=ssem.at[k - 1], recv_sem=rsem.at[k - 1],
                device_id=peer(k), device_id_type=MESH)

        sends = [remote(k, me) for k in range(1, 8)]
        for cp in sends:
            cp.start()
        recv[me] = b_ref[...]
        for k in range(1, 8):
            px, py, pc = peer(k)
            remote(k, 4 * px + 2 * py + pc).wait_recv()
        for cp in sends:
            cp.wait_send()
        total = recv[0]
        for j in range(1, 8):
            total = total + recv[j]
        o_ref[...] = total

    return pl.pallas_call(
        body, name="small_allreduce",
        in_specs=[pl.BlockSpec(memory_space=pltpu.VMEM)], out_specs=pl.BlockSpec(memory_space=pltpu.VMEM),
        out_shape=jax.ShapeDtypeStruct(buf.shape, f32),
        scratch_shapes=[pltpu.VMEM((8, R, LANE), f32), pltpu.SemaphoreType.DMA((7,)), pltpu.SemaphoreType.DMA((7,))],
    )(buf)


def sum_slots(recv):
    _, R, C = recv.shape
    tr = 512

    def body(r_ref, o_ref):
        acc = r_ref[0].astype(f32)
        for s in range(1, 4):
            acc = acc + r_ref[s].astype(f32)
        o_ref[...] = acc

    return pl.pallas_call(
        body, name="sum_slots", grid=(R // tr,),
        in_specs=[pl.BlockSpec((4, tr, C), lambda i: (0, i, 0))], out_specs=pl.BlockSpec((tr, C), lambda i: (i, 0)),
        out_shape=jax.ShapeDtypeStruct((R, C), f32), compiler_params=_cp(("parallel",)),
    )(recv)


def _adamw_math(w, g, m, v):
    m = B1 * m + (1.0 - B1) * g
    v = B2 * v + (1.0 - B2) * (g * g)
    m_hat = m / (1.0 - B1 ** STEP)
    v_hat = v / (1.0 - B2 ** STEP)
    delta = -LR * (m_hat / (jnp.sqrt(v_hat) + AEPS) + WD * w)
    return delta, m, v


def adamw_rows(w, m, v, ga, gb, row0, name):
    rows, C = w.shape
    tr = ADAM_ROWS
    b0 = row0 // tr

    def body(w_ref, m_ref, v_ref, ga_ref, gb_ref, g_ref, d_ref, nm_ref, nv_ref):
        g = ga_ref[...] + gb_ref[...]
        d, nm, nv = _adamw_math(w_ref[...], g, m_ref[...], v_ref[...])
        g_ref[...] = g
        d_ref[...] = d
        nm_ref[...] = nm
        nv_ref[...] = nv

    own = pl.BlockSpec((tr, C), lambda i: (i, 0))
    off = pl.BlockSpec((tr, C), lambda i: (b0 + i, 0))
    return pl.pallas_call(
        body, name=name, grid=(rows // tr,), in_specs=[own, own, own, off, off], out_specs=[own] * 4,
        out_shape=[jax.ShapeDtypeStruct((rows, C), f32)] * 4, compiler_params=_cp(("parallel",)),
    )(w, m, v, ga, gb)


def adamw_small(w, m, v, g):
    def body(w_ref, m_ref, v_ref, g_ref, d_ref, nm_ref, nv_ref):
        d, nm, nv = _adamw_math(w_ref[...], g_ref[...], m_ref[...], v_ref[...])
        d_ref[...] = d
        nm_ref[...] = nm
        nv_ref[...] = nv

    return pl.pallas_call(body, name="adamw_small", out_shape=[jax.ShapeDtypeStruct(w.shape, f32)] * 3)(w, m, v, g)


ADAM_ROWS = 64
PACK_C = 1024
BIG = (("w_mem_kv", 1), ("w_out", 1), ("w_in_a", 2), ("w_gate_up", 2), ("w_down", 1), ("w_in_b", 2))
CONV = (("conv_qkv_b", 2), ("ffn_conv_w", 2))
SMALL = ("rel_bias", "norm_mix_g", "norm_mem_g", "sinks_a", "a_log_b", "dt_bias_b", "out_norm_g_b", "norm_ffn_g",
         "ffn_conv_b", "final_norm_g")
WEIGHTS = ("rel_bias", "norm_mix_g", "norm_mem_g", "w_mem_kv", "w_out", "w_in_a", "sinks_a", "w_in_b", "conv_qkv_b",
           "a_log_b", "dt_bias_b", "out_norm_g_b", "norm_ffn_g", "w_gate_up", "ffn_conv_w", "ffn_conv_b", "w_down",
           "final_norm_g")
ARGS = ("x", "mem") + WEIGHTS + ("loss_target",) + tuple("m_" + n for n in WEIGHTS) + tuple("v_" + n for n in WEIGHTS)


def _rows(a, width):
    flat = a.reshape(-1)
    pad = (-flat.shape[0]) % width
    if pad:
        flat = jnp.concatenate([flat, jnp.zeros((pad,), a.dtype)])
    return flat.reshape(-1, width)


def _nrows(shape, width):
    return -(-math.prod(shape) // width)


def _pack(arrs, width, total_rows, dtype):
    parts = [_rows(a.astype(dtype), width) for a in arrs]
    used = sum(p.shape[0] for p in parts)
    if total_rows > used:
        parts.append(jnp.zeros((total_rows - used, width), dtype))
    return jnp.concatenate(parts, axis=0)


def _unpack(buf, shapes, width):
    out, r = [], 0
    for s in shapes:
        n = _nrows(s, width)
        out.append(buf[r:r + n].reshape(-1)[:math.prod(s)].reshape(s))
        r += n
    return out


def _pad_to(n, mult):
    return -(-n // mult) * mult


def _split_chips(full, axis):
    s = full.shape
    return jnp.moveaxis(full.reshape(s[:axis] + (4, s[axis] // 4) + s[axis + 1:]), axis, 0)


def _join_chips(parts, axis):
    m = jnp.moveaxis(parts, 0, axis)
    s = m.shape
    return m.reshape(s[:axis] + (4 * s[axis + 1],) + s[axis + 2:])


def kernel(x, mem, rel_bias, norm_mix_g, norm_mem_g, w_mem_kv, w_out, w_in_a, sinks_a, w_in_b, conv_qkv_b, a_log_b, dt_bias_b, out_norm_g_b, norm_ffn_g, w_gate_up, ffn_conv_w, ffn_conv_b, w_down, final_norm_g, loss_target, m_rel_bias, m_norm_mix_g, m_norm_mem_g, m_w_mem_kv, m_w_out, m_w_in_a, m_sinks_a, m_w_in_b, m_conv_qkv_b, m_a_log_b, m_dt_bias_b, m_out_norm_g_b, m_norm_ffn_g, m_w_gate_up, m_ffn_conv_w, m_ffn_conv_b, m_w_down, m_final_norm_g, v_rel_bias, v_norm_mix_g, v_norm_mem_g, v_w_mem_kv, v_w_out, v_w_in_a, v_sinks_a, v_w_in_b, v_conv_qkv_b, v_a_log_b, v_dt_bias_b, v_out_norm_g_b, v_norm_ffn_g, v_w_gate_up, v_ffn_conv_w, v_ffn_conv_b, v_w_down, v_final_norm_g):
    A = dict(zip(ARGS, (x, mem, rel_bias, norm_mix_g, norm_mem_g, w_mem_kv, w_out, w_in_a, sinks_a, w_in_b, conv_qkv_b, a_log_b, dt_bias_b, out_norm_g_b, norm_ffn_g, w_gate_up, ffn_conv_w, ffn_conv_b, w_down, final_norm_g, loss_target, m_rel_bias, m_norm_mix_g, m_norm_mem_g, m_w_mem_kv, m_w_out, m_w_in_a, m_sinks_a, m_w_in_b, m_conv_qkv_b, m_a_log_b, m_dt_bias_b, m_out_norm_g_b, m_norm_ffn_g, m_w_gate_up, m_ffn_conv_w, m_ffn_conv_b, m_w_down, m_final_norm_g, v_rel_bias, v_norm_mix_g, v_norm_mem_g, v_w_mem_kv, v_w_out, v_w_in_a, v_sinks_a, v_w_in_b, v_conv_qkv_b, v_a_log_b, v_dt_bias_b, v_out_norm_g_b, v_norm_ffn_g, v_w_gate_up, v_ffn_conv_w, v_ffn_conv_b, v_w_down, v_final_norm_g)))
    chip = 2 * lax.axis_index("x") + lax.axis_index("y")

    big_shapes = [A[n].shape for n, _ in BIG]
    big_rows = [_nrows(s, PACK_C) for s in big_shapes]
    big_off = [sum(big_rows[:j]) for j in range(len(BIG))]
    R = _pad_to(sum(big_rows), 512)
    conv_shapes = [A[n].shape for n, _ in CONV]
    conv_total = _pad_to(sum(_nrows(s, LANE) for s in conv_shapes), 8)
    wpack = _pack([A[n] for n, _ in BIG], PACK_C, R, bf16)
    cpack = _pack([A[n] for n, _ in CONV], LANE, conv_total, f32)
    gw, gc = allgather_chips([wpack, cpack])
    full = {n: A[n] for n in SMALL}
    parts = zip(*[_unpack(gw[s], big_shapes, PACK_C) for s in range(4)])
    for (n, axis), p in zip(BIG, parts):
        full[n] = _join_chips(jnp.stack(p), axis)
    parts = zip(*[_unpack(gc[s], conv_shapes, LANE) for s in range(4)])
    for (n, axis), p in zip(CONV, parts):
        full[n] = _join_chips(jnp.stack(p), axis)

    loss, dx, G = _local_step(x[0], mem[0], loss_target[0], _prepare(full))
    gfull = _grads_to_ref(G)

    gpack = jnp.concatenate(
        [_split_chips(gfull[n], axis).astype(bf16).reshape(4, -1, PACK_C) for n, axis in BIG]
        + [jnp.zeros((4, R - sum(big_rows), PACK_C), bf16)], axis=1)
    part = sum_slots(chip_scatter(gpack))
    sib = sibling_exchange(part)

    sm_shapes = [A[n].shape for n in SMALL] + [gfull[n].shape for n, _ in CONV] + [(LANE,)]
    sm_rows = _pad_to(sum(_nrows(s, LANE) for s in sm_shapes), 8)
    sbuf = _pack([gfull[n] for n in SMALL] + [gfull[n] for n, _ in CONV] + [loss[0]], LANE, sm_rows, f32)
    tot = _unpack(allreduce_small(sbuf), sm_shapes, LANE)
    gsmall = dict(zip(SMALL, tot[:len(SMALL)]))
    for (n, axis), t in zip(CONV, tot[len(SMALL):len(SMALL) + len(CONV)]):
        sh = A[n].shape[axis]
        gsmall[n] = lax.dynamic_slice_in_dim(t, chip * sh, sh, axis)
    loss_out = tot[-1][0]

    out = {}
    for (n, axis), rows, off in zip(BIG, big_rows, big_off):
        padded = _pad_to(rows, ADAM_ROWS)
        w2, m2, v2 = (_pack([A[p + n]], PACK_C, padded, f32) if padded != rows else A[p + n].reshape(rows, PACK_C)
                      for p in ("", "m_", "v_"))
        res = adamw_rows(w2, m2, v2, part, sib, off, "adamw_" + n)
        for key, r in zip(("grad_", "delta_", "new_m_", "new_v_"), res):
            out[key + n] = r[:rows].reshape(A[n].shape)
    names = SMALL + tuple(n for n, _ in CONV)
    shapes = [A[n].shape for n in names]
    rows = _pad_to(sum(_nrows(s, LANE) for s in shapes), 8)
    packs = [_pack([src[n] for n in names], LANE, rows, f32)
             for src in ({n: A[n] for n in names}, {n: A["m_" + n] for n in names}, {n: A["v_" + n] for n in names}, gsmall)]
    res = adamw_small(*packs)
    for key, r in zip(("delta_", "new_m_", "new_v_"), res):
        for n, a in zip(names, _unpack(r, shapes, LANE)):
            out[key + n] = a
    for n in names:
        out["grad_" + n] = gsmall[n]
    return (loss_out, dx[None], *[out["grad_" + n] for n in WEIGHTS], *[out["delta_" + n] for n in WEIGHTS],
            *[out["new_m_" + n] for n in WEIGHTS], *[out["new_v_" + n] for n in WEIGHTS])
```

```python
import functools
import math

import numpy as np
import jax
import jax.numpy as jnp
from jax import lax
from jax.experimental import pallas as pl
from jax.experimental.pallas import tpu as pltpu

f32 = jnp.float32
bf16 = jnp.bfloat16
HI = lax.Precision.HIGHEST
MESH = pl.DeviceIdType.MESH

D = 1024
MEM_LEN = 256
EPS = 1e-6
A_HEADS, A_KV, A_DH = 12, 2, 64
A_Q = 768
BLK = 128
N_BUCKETS, MAX_DIST = 32, 128
B_QK, B_V, B_DH = 384, 768, 128
B_QKV = 1536
CHUNK = 64
X_Q = 256
D_FF = 2816
IN_A = 1280
IN_B = 2572
IN_B_PAD = 2688
LANE = 128
VMEM_LIMIT = 56 * 1024 * 1024

LR, B1, B2, AEPS, WD, STEP = 0.001, 0.9, 0.999, 1e-08, 0.01, 10


def _cp(sem=None):
    return pltpu.CompilerParams(dimension_semantics=sem, vmem_limit_bytes=VMEM_LIMIT)


def _dg(a, b, ca, cb, prec=None):
    return lax.dot_general(a, b, (((ca,), (cb,)), ((), ())), precision=prec, preferred_element_type=f32)


@jax.custom_vjp
def bdot(a, b):
    return _dg(a.astype(bf16), b.astype(bf16), 1, 0)


def _bdot_f(a, b):
    return bdot(a, b), (a, b)


def _bdot_b(res, g):
    a, b = res
    gb = g.astype(bf16)
    return _dg(gb, b.astype(bf16), 1, 1), _dg(a.astype(bf16), gb, 0, 0)


bdot.defvjp(_bdot_f, _bdot_b)


@jax.custom_vjp
def bdot_nt(a, b):
    return _dg(a.astype(bf16), b.astype(bf16), 1, 1)


def _bdot_nt_f(a, b):
    return bdot_nt(a, b), (a, b)


def _bdot_nt_b(res, g):
    a, b = res
    gb = g.astype(bf16)
    return _dg(gb, b.astype(bf16), 1, 0), _dg(gb, a.astype(bf16), 0, 0)


bdot_nt.defvjp(_bdot_nt_f, _bdot_nt_b)


def _shift_rows(x, s, down):
    n = x.shape[0]
    row = lax.broadcasted_iota(jnp.int32, x.shape, 0)
    if down:
        return jnp.where(row >= s, pltpu.roll(x, s, 0), 0.0)
    return jnp.where(row < n - s, pltpu.roll(x, n - s, 0), 0.0)


@functools.partial(jax.custom_vjp, nondiff_argnums=(1,))
def shift_down(x, s):
    return _shift_rows(x, s, True)


def _sd_f(x, s):
    return _shift_rows(x, s, True), None


def _sd_b(s, _, g):
    return (_shift_rows(g, s, False),)


shift_down.defvjp(_sd_f, _sd_b)


def _sigmoid(x):
    return 1.0 / (1.0 + jnp.exp(-x))


def _silu(x):
    return x * _sigmoid(x)


def _rms(x, g):
    return x * lax.rsqrt(jnp.mean(x * x, axis=-1, keepdims=True) + EPS) * g


def _tile(n, cap):
    u = n // LANE
    best = 1
    for d in range(1, u + 1):
        if u % d == 0 and d * LANE <= cap:
            best = d
    return best * LANE


def mm_nn(a, w, res=None, out_dtype=f32, name="mm_nn"):
    M, K = a.shape
    N = w.shape[1]
    tm, tn = min(512, M), _tile(N, 640)

    def body(*refs):
        if res is None:
            a_ref, w_ref, o_ref = refs
            o_ref[...] = _dg(a_ref[...].astype(bf16), w_ref[...], 1, 0).astype(out_dtype)
        else:
            a_ref, w_ref, r_ref, o_ref = refs
            o_ref[...] = (r_ref[...] + _dg(a_ref[...].astype(bf16), w_ref[...], 1, 0)).astype(out_dtype)

    in_specs = [pl.BlockSpec((tm, K), lambda n, m: (m, 0)), pl.BlockSpec((K, tn), lambda n, m: (0, n))]
    args = [a, w]
    if res is not None:
        in_specs.append(pl.BlockSpec((tm, tn), lambda n, m: (m, n)))
        args.append(res)
    return pl.pallas_call(
        body, name=name, grid=(N // tn, M // tm), in_specs=in_specs,
        out_specs=pl.BlockSpec((tm, tn), lambda n, m: (m, n)),
        out_shape=jax.ShapeDtypeStruct((M, N), out_dtype),
        compiler_params=_cp(("parallel", "parallel")),
    )(*args)


def mm_nt(dy, w, name="mm_nt"):
    M, N = dy.shape
    K = w.shape[0]
    tm, tn = min(512, M), _tile(N, 512)

    def body(dy_ref, w_ref, o_ref):
        @pl.when(pl.program_id(1) == 0)
        def _():
            o_ref[...] = jnp.zeros_like(o_ref)
        o_ref[...] += _dg(dy_ref[...].astype(bf16), w_ref[...], 1, 1)

    return pl.pallas_call(
        body, name=name, grid=(M // tm, N // tn),
        in_specs=[pl.BlockSpec((tm, tn), lambda m, n: (m, n)), pl.BlockSpec((K, tn), lambda m, n: (0, n))],
        out_specs=pl.BlockSpec((tm, K), lambda m, n: (m, 0)),
        out_shape=jax.ShapeDtypeStruct((M, K), f32),
        compiler_params=_cp(("parallel", "arbitrary")),
    )(dy, w)


def mm_tn(a, dy, name="mm_tn"):
    M, K = a.shape
    N = dy.shape[1]
    tm, tk, tn = min(512, M), _tile(K, 1408), _tile(N, 1024)

    def body(a_ref, dy_ref, o_ref):
        @pl.when(pl.program_id(2) == 0)
        def _():
            o_ref[...] = jnp.zeros_like(o_ref)
        o_ref[...] += _dg(a_ref[...].astype(bf16), dy_ref[...].astype(bf16), 0, 0)

    return pl.pallas_call(
        body, name=name, grid=(K // tk, N // tn, M // tm),
        in_specs=[pl.BlockSpec((tm, tk), lambda k, n, m: (m, k)), pl.BlockSpec((tm, tn), lambda k, n, m: (m, n))],
        out_specs=pl.BlockSpec((tk, tn), lambda k, n, m: (k, n)),
        out_shape=jax.ShapeDtypeStruct((K, N), f32),
        compiler_params=_cp(("parallel", "parallel", "arbitrary")),
    )(a, dy)


def rms_fwd(h, g, name):
    S = h.shape[0]
    t = min(512, S)

    def body(h_ref, g_ref, o_ref):
        o_ref[...] = _rms(h_ref[...], g_ref[...]).astype(bf16)

    return pl.pallas_call(
        body, name=name, grid=(S // t,),
        in_specs=[pl.BlockSpec((t, D), lambda i: (i, 0)), pl.BlockSpec((1, D), lambda i: (0, 0))],
        out_specs=pl.BlockSpec((t, D), lambda i: (i, 0)),
        out_shape=jax.ShapeDtypeStruct((S, D), bf16),
        compiler_params=_cp(("parallel",)),
    )(h, g.reshape(1, D))


def rms_bwd(h, g, dn, dres, name):
    S = h.shape[0]
    t = min(512, S)

    def body(h_ref, g_ref, dn_ref, dr_ref, dh_ref, dg_ref):
        @pl.when(pl.program_id(0) == 0)
        def _():
            dg_ref[...] = jnp.zeros_like(dg_ref)
        _, vjp = jax.vjp(_rms, h_ref[...], g_ref[...])
        dh, dg = vjp(dn_ref[...])
        dh_ref[...] = dr_ref[...] + dh
        dg_ref[...] += dg

    tok = pl.BlockSpec((t, D), lambda i: (i, 0))
    vec = pl.BlockSpec((1, D), lambda i: (0, 0))
    return pl.pallas_call(
        body, name=name, grid=(S // t,), in_specs=[tok, vec, tok, tok], out_specs=[tok, vec],
        out_shape=[jax.ShapeDtypeStruct((S, D), f32), jax.ShapeDtypeStruct((1, D), f32)],
        compiler_params=_cp(("arbitrary",)),
    )(h, g.reshape(1, D), dn, dres)


def loss_head(h, g, target):
    S = h.shape[0]
    t = min(512, S)

    def f(hh, gg, tt):
        err = _rms(hh, gg) - tt
        return 0.5 * jnp.sum(jnp.mean(err * err, axis=-1, keepdims=True), axis=0, keepdims=True)

    def body(h_ref, g_ref, t_ref, loss_ref, dh_ref, dg_ref):
        @pl.when(pl.program_id(0) == 0)
        def _():
            dg_ref[...] = jnp.zeros_like(dg_ref)
            loss_ref[...] = jnp.zeros_like(loss_ref)
        val, vjp = jax.vjp(lambda a, b: f(a, b, t_ref[...]), h_ref[...], g_ref[...])
        dh, dg = vjp(jnp.ones((1, 1), f32))
        dh_ref[...] = dh
        dg_ref[...] += dg
        loss_ref[...] += jnp.broadcast_to(val, loss_ref.shape)

    tok = pl.BlockSpec((t, D), lambda i: (i, 0))
    vec = pl.BlockSpec((1, D), lambda i: (0, 0))
    return pl.pallas_call(
        body, name="loss_head", grid=(S // t,), in_specs=[tok, vec, tok],
        out_specs=[pl.BlockSpec((1, LANE), lambda i: (0, 0)), tok, vec],
        out_shape=[jax.ShapeDtypeStruct((1, LANE), f32), jax.ShapeDtypeStruct((S, D), f32),
                   jax.ShapeDtypeStruct((1, D), f32)],
        compiler_params=_cp(("arbitrary",)),
    )(h, g.reshape(1, D), target)


def memkv_fwd(mem, g, w, name):
    def body(m_ref, g_ref, w_ref, o_ref):
        o_ref[...] = _dg(_rms(m_ref[...], g_ref[...]).astype(bf16), w_ref[...], 1, 0)

    return pl.pallas_call(
        body, name=name, out_shape=jax.ShapeDtypeStruct((MEM_LEN, 2 * X_Q), f32), compiler_params=_cp(),
    )(mem, g.reshape(1, D), w)


def memkv_bwd(mem, g, w, dkv, name):
    def body(m_ref, g_ref, w_ref, d_ref, dg_ref, dw_ref):
        n, vjp = jax.vjp(lambda gg: _rms(m_ref[...], gg), g_ref[...])
        db = d_ref[...].astype(bf16)
        dw_ref[...] = _dg(n.astype(bf16), db, 0, 0)
        dg_ref[...] = vjp(_dg(db, w_ref[...], 1, 1))[0]

    return pl.pallas_call(
        body, name=name,
        out_shape=[jax.ShapeDtypeStruct((1, D), f32), jax.ShapeDtypeStruct((D, 2 * X_Q), f32)],
        compiler_params=_cp(),
    )(mem, g.reshape(1, D), w, dkv)


def _xattn_f(xq, mk, mv):
    lane = lax.broadcasted_iota(jnp.int32, (1, X_Q), 1)
    out = jnp.zeros(xq.shape, f32)
    for hd in range(4):
        msk = (lane // 64 == hd).astype(f32)
        s = bdot_nt(xq * msk, mk) * (64 ** -0.5)
        m = lax.stop_gradient(jnp.max(s, axis=-1, keepdims=True))
        p = jnp.exp(s - m)
        p = p / jnp.sum(p, axis=-1, keepdims=True)
        out = out + bdot(p, mv * msk)
    return out


def xattn_fwd(proj, col, kv, name):
    S = proj.shape[0]
    t = min(512, S)
    cb = col // X_Q

    def body(q_ref, k_ref, v_ref, o_ref):
        o_ref[...] = _xattn_f(q_ref[...], k_ref[...], v_ref[...])

    return pl.pallas_call(
        body, name=name, grid=(S // t,),
        in_specs=[pl.BlockSpec((t, X_Q), lambda i: (i, cb)), pl.BlockSpec((MEM_LEN, X_Q), lambda i: (0, 0)),
                  pl.BlockSpec((MEM_LEN, X_Q), lambda i: (0, 1))],
        out_specs=pl.BlockSpec((t, X_Q), lambda i: (i, 0)),
        out_shape=jax.ShapeDtypeStruct((S, X_Q), f32),
        compiler_params=_cp(("parallel",)),
    )(proj, kv, kv)


def xattn_bwd(proj, col, kv, dmix, name):
    S = proj.shape[0]
    t = min(512, S)
    cb = col // X_Q

    def body(q_ref, k_ref, v_ref, do_ref, dq_ref, dk_ref, dv_ref):
        @pl.when(pl.program_id(0) == 0)
        def _():
            dk_ref[...] = jnp.zeros_like(dk_ref)
            dv_ref[...] = jnp.zeros_like(dv_ref)
        _, vjp = jax.vjp(_xattn_f, q_ref[...], k_ref[...], v_ref[...])
        dq, dk, dv = vjp(do_ref[...])
        dq_ref[...] = dq
        dk_ref[...] += dk
        dv_ref[...] += dv

    kvb = pl.BlockSpec((MEM_LEN, X_Q), lambda i: (0, 0))
    dq, dk, dv = pl.pallas_call(
        body, name=name, grid=(S // t,),
        in_specs=[pl.BlockSpec((t, X_Q), lambda i: (i, cb)), kvb,
                  pl.BlockSpec((MEM_LEN, X_Q), lambda i: (0, 1)), pl.BlockSpec((t, X_Q), lambda i: (i, 3))],
        out_specs=[pl.BlockSpec((t, X_Q), lambda i: (i, 0)), kvb, kvb],
        out_shape=[jax.ShapeDtypeStruct((S, X_Q), f32), jax.ShapeDtypeStruct((MEM_LEN, X_Q), f32),
                   jax.ShapeDtypeStruct((MEM_LEN, X_Q), f32)],
        compiler_params=_cp(("arbitrary",)),
    )(proj, kv, kv, dmix)
    return dq, jnp.concatenate([dk, dv], axis=1)


def _bucket_map():
    qi = np.arange(BLK)[:, None]
    kj = np.arange(2 * BLK)[None, :]
    n = np.maximum(BLK + qi - kj, 0)
    max_exact = N_BUCKETS // 2
    nf = np.maximum(n, 1).astype(np.float64)
    large = max_exact + (np.log(nf / max_exact) / math.log(MAX_DIST / max_exact)
                         * (N_BUCKETS - max_exact)).astype(np.int32)
    large = np.minimum(large, N_BUCKETS - 1)
    return np.where(n < max_exact, n, large).astype(np.int32)


def bias_build(rel_bias):
    def body(rb_ref, bk_ref, o_ref):
        bk = bk_ref[...]
        for h in range(A_HEADS):
            acc = jnp.zeros((BLK, 2 * BLK), f32)
            for b in range(N_BUCKETS):
                acc = jnp.where(bk == b, rb_ref[b, h], acc)
            o_ref[h] = acc

    return pl.pallas_call(
        body, name="bias_build",
        in_specs=[pl.BlockSpec(memory_space=pltpu.SMEM), pl.BlockSpec(memory_space=pltpu.VMEM)],
        out_specs=pl.BlockSpec(memory_space=pltpu.VMEM),
        out_shape=jax.ShapeDtypeStruct((A_HEADS, BLK, 2 * BLK), f32), compiler_params=_cp(),
    )(rel_bias, jnp.asarray(_bucket_map()))


def bias_grad(dbias):
    def body(d_ref, bk_ref, o_ref):
        bk = bk_ref[...]
        row = lax.broadcasted_iota(jnp.int32, (N_BUCKETS, LANE), 0)
        lane = lax.broadcasted_iota(jnp.int32, (N_BUCKETS, LANE), 1)
        acc = jnp.zeros((N_BUCKETS, LANE), f32)
        for h in range(A_HEADS):
            d = d_ref[h]
            for b in range(N_BUCKETS):
                s = jnp.sum(jnp.where(bk == b, d, 0.0), keepdims=True)
                acc = acc + jnp.where((row == b) & (lane == h), s, 0.0)
        o_ref[...] = acc

    return pl.pallas_call(
        body, name="bias_grad", out_shape=jax.ShapeDtypeStruct((N_BUCKETS, LANE), f32), compiler_params=_cp(),
    )(dbias, jnp.asarray(_bucket_map()))


def _swa_f(qb, kp, kc, vp, vc, bias, sk, first):
    kband = jnp.concatenate([kp, kc], axis=0)
    vband = jnp.concatenate([vp, vc], axis=0)
    qi = lax.broadcasted_iota(jnp.int32, (BLK, 2 * BLK), 0)
    kj = lax.broadcasted_iota(jnp.int32, (BLK, 2 * BLK), 1)
    rel = kj - qi
    ok = (rel >= 1) & (rel <= BLK) & ((kj >= BLK) | jnp.logical_not(first))
    lane = lax.broadcasted_iota(jnp.int32, (1, LANE), 1)
    lane_b = lax.broadcasted_iota(jnp.int32, (BLK, LANE), 1)
    outs = []
    for p in range(A_HEADS // 2):
        qp = qb[:, LANE * p:LANE * (p + 1)]
        acc = jnp.zeros((BLK, LANE), f32)
        for g in range(2):
            h = g * (A_HEADS // 2) + p
            msk = (lane // A_DH == g).astype(f32)
            s = bdot_nt(qp * msk, kband) * (A_DH ** -0.5) + bias[h]
            s = jnp.where(ok, s, -1e30)
            skb = jnp.broadcast_to(sk[h:h + 1, :], (BLK, LANE))
            sink = jnp.sum(jnp.where(lane_b == 0, skb, 0.0), axis=-1, keepdims=True)
            m = lax.stop_gradient(jnp.maximum(jnp.max(s, axis=-1, keepdims=True), sink))
            e = jnp.exp(s - m)
            prob = e / (jnp.sum(e, axis=-1, keepdims=True) + jnp.exp(sink - m))
            acc = acc + bdot(prob, vband) * msk
        outs.append(acc)
    return jnp.concatenate(outs, axis=1)


def _swa_specs(nb, rev):
    bi = (lambda i: nb - 1 - i) if rev else (lambda i: i)
    return [
        pl.BlockSpec((BLK, A_Q), lambda i: (bi(i), 0)),
        pl.BlockSpec((BLK, LANE), lambda i: (jnp.maximum(bi(i) - 1, 0), 6)),
        pl.BlockSpec((BLK, LANE), lambda i: (bi(i), 6)),
        pl.BlockSpec((BLK, LANE), lambda i: (jnp.maximum(bi(i) - 1, 0), 7)),
        pl.BlockSpec((BLK, LANE), lambda i: (bi(i), 7)),
        pl.BlockSpec((A_HEADS, BLK, 2 * BLK), lambda i: (0, 0, 0)),
        pl.BlockSpec((16, LANE), lambda i: (0, 0)),
    ]


def swa_fwd(proj, bias, sk):
    S = proj.shape[0]
    nb = S // BLK

    def body(q_ref, kp_ref, kc_ref, vp_ref, vc_ref, b_ref, s_ref, o_ref):
        o_ref[...] = _swa_f(q_ref[...], kp_ref[...], kc_ref[...], vp_ref[...], vc_ref[...], b_ref[...], s_ref[...],
                            pl.program_id(0) == 0)

    return pl.pallas_call(
        body, name="swa_fwd", grid=(nb,), in_specs=_swa_specs(nb, False),
        out_specs=pl.BlockSpec((BLK, A_Q), lambda i: (i, 0)),
        out_shape=jax.ShapeDtypeStruct((S, A_Q), f32), compiler_params=_cp(("parallel",)),
    )(proj, proj, proj, proj, proj, bias, sk)


def swa_bwd(proj, bias, sk, dmix):
    S = proj.shape[0]
    nb = S // BLK

    def body(q_ref, kp_ref, kc_ref, vp_ref, vc_ref, b_ref, s_ref, do_ref, dqkv_ref, db_ref, ds_ref, ck, cv):
        i = pl.program_id(0)

        @pl.when(i == 0)
        def _():
            db_ref[...] = jnp.zeros_like(db_ref)
            ds_ref[...] = jnp.zeros_like(ds_ref)
            ck[...] = jnp.zeros_like(ck)
            cv[...] = jnp.zeros_like(cv)
        first = i == nb - 1
        _, vjp = jax.vjp(lambda *a: _swa_f(*a, first), q_ref[...], kp_ref[...], kc_ref[...], vp_ref[...],
                         vc_ref[...], b_ref[...], s_ref[...])
        dq, dkp, dkc, dvp, dvc, db, ds = vjp(do_ref[...])
        dqkv_ref[...] = jnp.concatenate([dq, dkc + ck[...], dvc + cv[...]], axis=1)
        ck[...] = dkp
        cv[...] = dvp
        db_ref[...] += db
        ds_ref[...] += ds

    return pl.pallas_call(
        body, name="swa_bwd", grid=(nb,),
        in_specs=_swa_specs(nb, True) + [pl.BlockSpec((BLK, A_Q), lambda i: (nb - 1 - i, 0))],
        out_specs=[pl.BlockSpec((BLK, D), lambda i: (nb - 1 - i, 0)),
                   pl.BlockSpec((A_HEADS, BLK, 2 * BLK), lambda i: (0, 0, 0)),
                   pl.BlockSpec((16, LANE), lambda i: (0, 0))],
        out_shape=[jax.ShapeDtypeStruct((S, D), f32), jax.ShapeDtypeStruct((A_HEADS, BLK, 2 * BLK), f32),
                   jax.ShapeDtypeStruct((16, LANE), f32)],
        scratch_shapes=[pltpu.VMEM((BLK, LANE), f32), pltpu.VMEM((BLK, LANE), f32)],
        compiler_params=_cp(("arbitrary",)),
    )(proj, proj, proj, proj, proj, bias, sk, dmix)


def _dnprep_f(x, w, is_qk):
    c = (w[3:4] * x + w[2:3] * shift_down(x, 1) + w[1:2] * shift_down(x, 2) + w[0:1] * shift_down(x, 3))
    a = _silu(c)
    n = a * lax.rsqrt(jnp.sum(a * a, axis=-1, keepdims=True) + EPS)
    return jnp.where(is_qk, n, a)


def dnprep_fwd(proj, cw):
    S = proj.shape[0]
    nblk = B_QKV // LANE

    def body(x_ref, w_ref, o_ref):
        o_ref[...] = _dnprep_f(x_ref[...], w_ref[...], pl.program_id(0) < 2 * B_QK // LANE)

    return pl.pallas_call(
        body, name="dnprep_fwd", grid=(nblk,),
        in_specs=[pl.BlockSpec((S, LANE), lambda j: (0, j)), pl.BlockSpec((4, LANE), lambda j: (0, j))],
        out_specs=pl.BlockSpec((S, LANE), lambda j: (0, j)),
        out_shape=jax.ShapeDtypeStruct((S, B_QKV), f32), compiler_params=_cp(("parallel",)),
    )(proj, cw)


def dnprep_bwd(proj, cw, dqkvn):
    S = proj.shape[0]
    nblk = B_QKV // LANE

    def body(x_ref, w_ref, d_ref, dx_ref, dw_ref):
        is_qk = pl.program_id(0) < 2 * B_QK // LANE
        _, vjp = jax.vjp(lambda a, b: _dnprep_f(a, b, is_qk), x_ref[...], w_ref[...])
        dx, dw = vjp(d_ref[...])
        dx_ref[...] = dx
        dw_ref[...] = dw

    col = pl.BlockSpec((S, LANE), lambda j: (0, j))
    wsp = pl.BlockSpec((4, LANE), lambda j: (0, j))
    return pl.pallas_call(
        body, name="dnprep_bwd", grid=(nblk,), in_specs=[col, wsp, col], out_specs=[col, wsp],
        out_shape=[jax.ShapeDtypeStruct((S, B_QKV), f32), jax.ShapeDtypeStruct((4, B_QKV), f32)],
        compiler_params=_cp(("parallel",)),
    )(proj, cw, dqkvn)


def _hdot(a, b, ca=1, cb=0):
    return _dg(a, b, ca, cb, HI)


def _bdg(a, b, ca, cb):
    return lax.dot_general(a, b, (((ca,), (cb,)), ((0,), (0,))), precision=HI, preferred_element_type=f32)


@jax.custom_vjp
def hbd(a, b):
    return _bdg(a, b, 2, 1)


@jax.custom_vjp
def hbd_nt(a, b):
    return _bdg(a, b, 2, 2)


@jax.custom_vjp
def hbd_tn(a, b):
    return _bdg(a, b, 1, 1)


hbd.defvjp(lambda a, b: (hbd(a, b), (a, b)), lambda r, g: (hbd_nt(g, r[1]), hbd_tn(r[0], g)))
hbd_nt.defvjp(lambda a, b: (hbd_nt(a, b), (a, b)), lambda r, g: (hbd(g, r[1]), hbd_tn(g, r[0])))
hbd_tn.defvjp(lambda a, b: (hbd_tn(a, b), (a, b)), lambda r, g: (hbd_nt(r[1], g), hbd(r[0], g)))


def _stack(xs):
    return jnp.concatenate([x[None] for x in xs], axis=0)


def _lane_col(x, j):
    lane = lax.broadcasted_iota(jnp.int32, (1, LANE), 1)
    return jnp.sum(jnp.where(lane == j, x, 0.0), axis=-1, keepdims=True)


def _dnc_f(q, k, v, seg, prm):
    C = CHUNK
    beta_all = _sigmoid(seg)
    xx = seg + prm[1:2]
    g_all = -jnp.exp(prm[0:1]) * (jnp.maximum(xx, 0.0) + jnp.log(1.0 + jnp.exp(-jnp.abs(xx))))
    r2 = lax.broadcasted_iota(jnp.int32, (C, C), 0)
    c2 = lax.broadcasted_iota(jnp.int32, (C, C), 1)
    gc_all = _hdot((r2 >= c2).astype(f32), g_all)
    beta = _stack([_lane_col(beta_all, h) for h in range(6)])
    gc = _stack([_lane_col(gc_all, 6 + h) for h in range(6)])
    r = lax.broadcasted_iota(jnp.int32, (1, C, C), 1)
    c = lax.broadcasted_iota(jnp.int32, (1, C, C), 2)
    incl = r >= c
    strict = r > c
    eye = (r == c).astype(f32)
    g_row = hbd(jnp.ones((6, C, C), f32), eye * gc)
    decay = jnp.where(incl, jnp.exp(jnp.where(incl, gc - g_row, 0.0)), 0.0)
    a_mat = beta * hbd_nt(k, k) * jnp.where(strict, decay, 0.0)
    eg = jnp.exp(gc)
    pw = -a_mat
    inv = eye + pw
    for _ in range(5):
        pw = hbd(pw, pw)
        inv = inv + hbd(inv, pw)
    u = hbd(inv, beta * v)
    w = hbd(inv, (beta * eg) * k)
    qc = q * (B_DH ** -0.5)
    attn = hbd_nt(qc, k) * decay
    last = (lax.broadcasted_iota(jnp.int32, (1, C, 1), 1) == C - 1).astype(f32)
    g_last = jnp.sum(gc * last, axis=1, keepdims=True)
    dc = jnp.broadcast_to(jnp.exp(g_last), (6, 1, LANE)).reshape(6, LANE)
    return u, w, qc * eg, k * jnp.exp(g_last - gc), attn, dc


def _dns_f(S0, u, w, qd, kt, attn, dcrows):
    dc = _lane_col(dcrows, 0).reshape(6, 1, 1)
    delta = u - hbd(w, S0)
    out = hbd(qd, S0) + hbd(attn, delta)
    return out, dc * S0 + hbd_tn(kt, delta)


def _dnpost_f(o, z, grow):
    outs = []
    for h in range(6):
        oh = o[:, LANE * h:LANE * (h + 1)]
        outs.append(oh * lax.rsqrt(jnp.mean(oh * oh, axis=-1, keepdims=True) + EPS) * grow
                    * _silu(z[:, LANE * h:LANE * (h + 1)]))
    return jnp.concatenate(outs, axis=1)


def _hs(h):
    return slice(LANE * h, LANE * (h + 1))


def _heads(ref, share):
    return _stack([ref[:, _hs(h // share)] for h in range(6)])


def _put_heads(ref, val):
    for h in range(6):
        ref[:, _hs(h)] = val[h]


def _dnc_in_specs():
    return [
        pl.BlockSpec((CHUNK, B_QK), lambda n: (n, 0)),
        pl.BlockSpec((CHUNK, B_QK), lambda n: (n, 1)),
        pl.BlockSpec((CHUNK, B_V), lambda n: (n, 1)),
        pl.BlockSpec((CHUNK, LANE), lambda n: (n, 20)),
        pl.BlockSpec((8, LANE), lambda n: (0, 0)),
    ]


def _dnc_out_specs(rev_nc=None):
    ci = (lambda n: n) if rev_nc is None else (lambda n: rev_nc - 1 - n)
    wide = pl.BlockSpec((CHUNK, B_V), lambda n: (ci(n), 0))
    return [wide, wide, wide, wide, pl.BlockSpec((1, 6, CHUNK, CHUNK), lambda n: (ci(n), 0, 0, 0)),
            pl.BlockSpec((1, 8, LANE), lambda n: (ci(n), 0, 0))]


def _dnc_shapes(S):
    nc = S // CHUNK
    wide = jax.ShapeDtypeStruct((S, B_V), f32)
    return [wide, wide, wide, wide, jax.ShapeDtypeStruct((nc, 6, CHUNK, CHUNK), f32),
            jax.ShapeDtypeStruct((nc, 8, LANE), f32)]


def dnc_fwd(qkvn, proj, prm):
    S = proj.shape[0]

    def body(q_ref, k_ref, v_ref, s_ref, p_ref, u_ref, w_ref, qd_ref, kt_ref, at_ref, dc_ref):
        u, w, qd, kt, attn, dc = _dnc_f(_heads(q_ref, 2), _heads(k_ref, 2), _heads(v_ref, 1), s_ref[...], p_ref[...])
        _put_heads(u_ref, u)
        _put_heads(w_ref, w)
        _put_heads(qd_ref, qd)
        _put_heads(kt_ref, kt)
        at_ref[0] = attn
        dc_ref[0] = jnp.concatenate([dc, jnp.zeros((2, LANE), f32)], axis=0)

    return pl.pallas_call(
        body, name="dn_chunk_fwd", grid=(S // CHUNK,), in_specs=_dnc_in_specs(), out_specs=_dnc_out_specs(),
        out_shape=_dnc_shapes(S), compiler_params=_cp(("parallel",)),
    )(qkvn, qkvn, qkvn, proj, prm)


def dnc_bwd(qkvn, proj, prm, cots):
    S = proj.shape[0]

    def body(q_ref, k_ref, v_ref, s_ref, p_ref, du_ref, dw_ref, dqd_ref, dkt_ref, dat_ref, ddc_ref,
             dx_ref, dseg_ref, dprm_ref):
        @pl.when(pl.program_id(0) == 0)
        def _():
            dprm_ref[...] = jnp.zeros_like(dprm_ref)
        _, vjp = jax.vjp(_dnc_f, _heads(q_ref, 2), _heads(k_ref, 2), _heads(v_ref, 1), s_ref[...], p_ref[...])
        dq, dk, dv, dseg, dprm = vjp((_heads(du_ref, 1), _heads(dw_ref, 1), _heads(dqd_ref, 1), _heads(dkt_ref, 1),
                                      dat_ref[0], ddc_ref[0, 0:6, :]))
        dx_ref[...] = jnp.concatenate([dq[0] + dq[1], dq[2] + dq[3], dq[4] + dq[5],
                                       dk[0] + dk[1], dk[2] + dk[3], dk[4] + dk[5]] + [dv[h] for h in range(6)], axis=1)
        dseg_ref[...] = dseg
        dprm_ref[...] += dprm

    return pl.pallas_call(
        body, name="dn_chunk_bwd", grid=(S // CHUNK,), in_specs=_dnc_in_specs() + _dnc_out_specs(),
        out_specs=[pl.BlockSpec((CHUNK, B_QKV), lambda n: (n, 0)), pl.BlockSpec((CHUNK, LANE), lambda n: (n, 0)),
                   pl.BlockSpec((8, LANE), lambda n: (0, 0))],
        out_shape=[jax.ShapeDtypeStruct((S, B_QKV), f32), jax.ShapeDtypeStruct((S, LANE), f32),
                   jax.ShapeDtypeStruct((8, LANE), f32)],
        compiler_params=_cp(("arbitrary",)),
    )(qkvn, qkvn, qkvn, proj, prm, *cots)


def dns_fwd(chunked):
    u = chunked[0]
    S = u.shape[0]
    nc = S // CHUNK

    def body(u_ref, w_ref, qd_ref, kt_ref, at_ref, dc_ref, o_ref, st_ref, st):
        @pl.when(pl.program_id(0) == 0)
        def _():
            st[...] = jnp.zeros_like(st)
        S0 = st[...]
        st_ref[0] = S0
        out, S1 = _dns_f(S0, _heads(u_ref, 1), _heads(w_ref, 1), _heads(qd_ref, 1), _heads(kt_ref, 1),
                         at_ref[0], dc_ref[0, 0:6, :])
        _put_heads(o_ref, out)
        st[...] = S1

    return pl.pallas_call(
        body, name="dn_scan_fwd", grid=(nc,), in_specs=_dnc_out_specs(),
        out_specs=[pl.BlockSpec((CHUNK, B_V), lambda n: (n, 0)),
                   pl.BlockSpec((1, 6, B_DH, B_DH), lambda n: (n, 0, 0, 0))],
        out_shape=[jax.ShapeDtypeStruct((S, B_V), f32), jax.ShapeDtypeStruct((nc, 6, B_DH, B_DH), f32)],
        scratch_shapes=[pltpu.VMEM((6, B_DH, B_DH), f32)],
        compiler_params=_cp(("arbitrary",)),
    )(*chunked)


def dns_bwd(chunked, states, do):
    S = do.shape[0]
    nc = S // CHUNK

    def body(u_ref, w_ref, qd_ref, kt_ref, at_ref, dc_ref, st_ref, do_ref,
             du_ref, dw_ref, dqd_ref, dkt_ref, dat_ref, ddc_ref, dst):
        @pl.when(pl.program_id(0) == 0)
        def _():
            dst[...] = jnp.zeros_like(dst)
        _, vjp = jax.vjp(_dns_f, st_ref[0], _heads(u_ref, 1), _heads(w_ref, 1), _heads(qd_ref, 1), _heads(kt_ref, 1),
                         at_ref[0], dc_ref[0, 0:6, :])
        dS0, du, dw, dqd, dkt, dat, ddc = vjp((_heads(do_ref, 1), dst[...]))
        dst[...] = dS0
        _put_heads(du_ref, du)
        _put_heads(dw_ref, dw)
        _put_heads(dqd_ref, dqd)
        _put_heads(dkt_ref, dkt)
        dat_ref[0] = dat
        ddc_ref[0] = jnp.concatenate([ddc, jnp.zeros((2, LANE), f32)], axis=0)

    return pl.pallas_call(
        body, name="dn_scan_bwd", grid=(nc,),
        in_specs=_dnc_out_specs(nc) + [pl.BlockSpec((1, 6, B_DH, B_DH), lambda n: (nc - 1 - n, 0, 0, 0)),
                                       pl.BlockSpec((CHUNK, B_V), lambda n: (nc - 1 - n, 0))],
        out_specs=_dnc_out_specs(nc), out_shape=_dnc_shapes(S),
        scratch_shapes=[pltpu.VMEM((6, B_DH, B_DH), f32)],
        compiler_params=_cp(("arbitrary",)),
    )(*chunked, states, do)


def dnpost_fwd(o, proj, prm):
    S = o.shape[0]
    t = min(512, S)

    def body(o_ref, z_ref, p_ref, y_ref):
        y_ref[...] = _dnpost_f(o_ref[...], z_ref[...], p_ref[2:3, :])

    tok = pl.BlockSpec((t, B_V), lambda i: (i, 0))
    return pl.pallas_call(
        body, name="dn_post_fwd", grid=(S // t,),
        in_specs=[tok, pl.BlockSpec((t, B_V), lambda i: (i, 2)), pl.BlockSpec((8, LANE), lambda i: (0, 0))],
        out_specs=tok, out_shape=jax.ShapeDtypeStruct((S, B_V), f32), compiler_params=_cp(("parallel",)),
    )(o, proj, prm)


def dnpost_bwd(o, proj, prm, dmix):
    S = o.shape[0]
    t = min(512, S)

    def body(o_ref, z_ref, p_ref, dy_ref, do_ref, dz_ref, dg_ref):
        @pl.when(pl.program_id(0) == 0)
        def _():
            dg_ref[...] = jnp.zeros_like(dg_ref)
        _, vjp = jax.vjp(_dnpost_f, o_ref[...], z_ref[...], p_ref[2:3, :])
        do, dz, dg = vjp(dy_ref[...])
        do_ref[...] = do
        dz_ref[...] = dz
        dg_ref[...] += dg

    tok = pl.BlockSpec((t, B_V), lambda i: (i, 0))
    return pl.pallas_call(
        body, name="dn_post_bwd", grid=(S // t,),
        in_specs=[tok, pl.BlockSpec((t, B_V), lambda i: (i, 2)), pl.BlockSpec((8, LANE), lambda i: (0, 0)), tok],
        out_specs=[tok, tok, pl.BlockSpec((1, LANE), lambda i: (0, 0))],
        out_shape=[jax.ShapeDtypeStruct((S, B_V), f32), jax.ShapeDtypeStruct((S, B_V), f32),
                   jax.ShapeDtypeStruct((1, LANE), f32)],
        compiler_params=_cp(("arbitrary",)),
    )(o, proj, prm, dmix)


N_FF_BLK = D_FF // LANE
GU_SHARD = 2 * D_FF // 4


def _glu_f(gate, up, w, b):
    c = w[2:3] * gate + w[1:2] * shift_down(gate, 1) + w[0:1] * shift_down(gate, 2) + b
    return _silu(c) * up


def glu_fwd(gu, w, b, name):
    S = gu.shape[0]

    def body(g_ref, u_ref, w_ref, b_ref, o_ref):
        o_ref[...] = _glu_f(g_ref[...], u_ref[...], w_ref[...], b_ref[...]).astype(bf16)

    col = pl.BlockSpec((S, LANE), lambda j: (0, j))
    return pl.pallas_call(
        body, name=name, grid=(N_FF_BLK,),
        in_specs=[col, pl.BlockSpec((S, LANE), lambda j: (0, N_FF_BLK + j)), pl.BlockSpec((3, LANE), lambda j: (0, j)),
                  pl.BlockSpec((1, LANE), lambda j: (0, j))],
        out_specs=col, out_shape=jax.ShapeDtypeStruct((S, D_FF), bf16), compiler_params=_cp(("parallel",)),
    )(gu, gu, w, b.reshape(1, D_FF))


def glu_bwd(gu, w, b, dact, name):
    S = gu.shape[0]

    def body(g_ref, u_ref, w_ref, b_ref, d_ref, dg_ref, dw_ref, db_ref):
        _, vjp = jax.vjp(_glu_f, g_ref[...], u_ref[...], w_ref[...], b_ref[...])
        dg, du, dw, db = vjp(d_ref[...])
        dg_ref[0] = dg.astype(bf16)
        dg_ref[1] = du.astype(bf16)
        dw_ref[...] = dw
        db_ref[...] = db

    col = pl.BlockSpec((S, LANE), lambda j: (0, j))
    wsp = pl.BlockSpec((3, LANE), lambda j: (0, j))
    bsp = pl.BlockSpec((1, LANE), lambda j: (0, j))
    return pl.pallas_call(
        body, name=name, grid=(N_FF_BLK,),
        in_specs=[col, pl.BlockSpec((S, LANE), lambda j: (0, N_FF_BLK + j)), wsp, bsp, col],
        out_specs=[pl.BlockSpec((2, S, LANE), lambda j: (0, 0, j)), wsp, bsp],
        out_shape=[jax.ShapeDtypeStruct((2, S, D_FF), bf16), jax.ShapeDtypeStruct((3, D_FF), f32),
                   jax.ShapeDtypeStruct((1, D_FF), f32)],
        compiler_params=_cp(("parallel",)),
    )(gu, gu, w, b.reshape(1, D_FF), dact)


def gu_fwd(n2, wg, name):
    S = n2.shape[0]
    tm = min(512, S)

    def body(a_ref, w_ref, o_ref):
        o_ref[...] = _dg(a_ref[...], w_ref[...], 1, 0)

    return pl.pallas_call(
        body, name=name, grid=(4, S // tm),
        in_specs=[pl.BlockSpec((tm, D), lambda s, m: (m, 0)), pl.BlockSpec((None, D, GU_SHARD), lambda s, m: (s, 0, 0))],
        out_specs=pl.BlockSpec((tm, GU_SHARD), lambda s, m: (m, s)),
        out_shape=jax.ShapeDtypeStruct((S, 2 * D_FF), f32), compiler_params=_cp(("parallel", "parallel")),
    )(n2, wg)


def gu_bwd_x(dgu, wg, name):
    S = dgu.shape[1]
    tm = min(512, S)

    def body(d_ref, w_ref, o_ref):
        @pl.when(pl.program_id(1) == 0)
        def _():
            o_ref[...] = jnp.zeros_like(o_ref)
        o_ref[...] += _dg(d_ref[...], w_ref[...], 1, 1)

    return pl.pallas_call(
        body, name=name, grid=(S // tm, 4),
        in_specs=[pl.BlockSpec((None, tm, GU_SHARD), lambda m, s: (s // 2, m, s % 2)),
                  pl.BlockSpec((None, D, GU_SHARD), lambda m, s: (s, 0, 0))],
        out_specs=pl.BlockSpec((tm, D), lambda m, s: (m, 0)),
        out_shape=jax.ShapeDtypeStruct((S, D), f32), compiler_params=_cp(("parallel", "arbitrary")),
    )(dgu, wg)


def gu_bwd_w(n2, dgu, name):
    S = n2.shape[0]
    tm = min(512, S)
    nm = S // tm

    def body(a_ref, d_ref, o_ref, acc):
        @pl.when(pl.program_id(1) == 0)
        def _():
            acc[...] = jnp.zeros_like(acc)
        acc[...] += _dg(a_ref[...], d_ref[...], 0, 0)

        @pl.when(pl.program_id(1) == nm - 1)
        def _():
            o_ref[...] = acc[...].astype(bf16)

    return pl.pallas_call(
        body, name=name, grid=(4, nm),
        in_specs=[pl.BlockSpec((tm, D), lambda s, m: (m, 0)),
                  pl.BlockSpec((None, tm, GU_SHARD), lambda s, m: (s // 2, m, s % 2))],
        out_specs=pl.BlockSpec((None, D, GU_SHARD), lambda s, m: (s, 0, 0)),
        out_shape=jax.ShapeDtypeStruct((4, D, GU_SHARD), bf16),
        scratch_shapes=[pltpu.VMEM((D, GU_SHARD), f32)],
        compiler_params=_cp(("parallel", "arbitrary")),
    )(n2, dgu)


def _pair_cols(w):
    lead = w.shape[:-1]
    return w.reshape(lead + (2, 6, A_DH)).swapaxes(-3, -2).reshape(lead + (A_Q,))


def _unpair_cols(w):
    lead = w.shape[:-1]
    return w.reshape(lead + (6, 2, A_DH)).swapaxes(-3, -2).reshape(lead + (A_Q,))


def _lay_in_a(w):
    return jnp.concatenate([_pair_cols(w[:, :A_Q]), w[:, A_Q:]], axis=1)


def _unlay_in_a(w):
    return jnp.concatenate([_unpair_cols(w[:, :A_Q]), w[:, A_Q:]], axis=1)


def _lay_out_a(w):
    return jnp.concatenate([_pair_cols(w[:A_Q].T).T, w[A_Q:]], axis=0)


def _unlay_out_a(w):
    return jnp.concatenate([_unpair_cols(w[:A_Q].T).T, w[A_Q:]], axis=0)


def _lay_in_b(w):
    return jnp.concatenate([w[:, :2304], w[:, 2316:], w[:, 2304:2316],
                            jnp.zeros((w.shape[0], LANE - 12), w.dtype)], axis=1)


def _unlay_in_b(w):
    return jnp.concatenate([w[:, :2304], w[:, 2560:2572], w[:, 2304:2560]], axis=1)


def _chip_cols(w):
    return jnp.moveaxis(w.reshape(w.shape[0], 4, w.shape[1] // 4), 1, 0)


def _unchip_cols(w):
    return jnp.moveaxis(w, 0, 1).reshape(w.shape[1], 4 * w.shape[2])


def _local_step(x, mem, target, P):
    sk = jnp.zeros((16, LANE), f32).at[:A_HEADS].set(jnp.broadcast_to(P["sinks"][:, None], (A_HEADS, LANE)))
    prm = jnp.zeros((8, LANE), f32).at[0, 6:12].set(P["a_log"]).at[1, 6:12].set(P["dt_bias"]).at[2].set(P["out_norm_g"])
    bias = bias_build(P["rel_bias"])
    saved = []
    h = x
    for i in range(2):
        n1 = rms_fwd(h, P["g_mix"][i], f"rms_mix{i}")
        kv = memkv_fwd(mem, P["g_mem"][i], P["w_mem"][i], f"memkv{i}")
        if i == 0:
            proj = mm_nn(n1, P["w_in_a"], name="proj_a")
            self_out = swa_fwd(proj, bias, sk)
            cross = xattn_fwd(proj, A_Q + 2 * LANE, kv, "xattn_a")
            extra = ()
        else:
            proj = mm_nn(n1, P["w_in_b"], name="proj_b")
            qkvn = dnprep_fwd(proj, P["conv_qkv"])
            chunked = dnc_fwd(qkvn, proj, prm)
            o, states = dns_fwd(chunked)
            self_out = dnpost_fwd(o, proj, prm)
            cross = xattn_fwd(proj, 2304, kv, "xattn_b")
            extra = (qkvn, chunked, states, o)
        mix = jnp.concatenate([self_out, cross], axis=1)
        h2 = mm_nn(mix, P["w_out"][i], res=h, name=f"out_proj{i}")
        n2 = rms_fwd(h2, P["g_ffn"][i], f"rms_ffn{i}")
        gu = gu_fwd(n2, P["w_gu"][i], f"gate_up{i}")
        act = glu_fwd(gu, P["ffn_cw"][i], P["ffn_cb"][i], f"glu{i}")
        h3 = mm_nn(act, P["w_down"][i], res=h2, name=f"down{i}")
        saved.append((h, n1, kv, proj, mix, h2, n2, gu, act, extra))
        h = h3

    loss, dh, dg_fin = loss_head(h, P["g_fin"], target)
    G = {"g_fin": dg_fin[0], "g_mix": [None, None], "g_mem": [None, None], "g_ffn": [None, None],
         "w_mem": [None, None], "w_out": [None, None], "w_gu": [None, None], "w_down": [None, None],
         "ffn_cw": [None, None], "ffn_cb": [None, None]}
    for i in (1, 0):
        hin, n1, kv, proj, mix, h2, n2, gu, act, extra = saved[i]
        dact = mm_nt(dh, P["w_down"][i], name=f"d_act{i}")
        G["w_down"][i] = mm_tn(act, dh, name=f"dw_down{i}")
        dgu, dcw, dcb = glu_bwd(gu, P["ffn_cw"][i], P["ffn_cb"][i], dact, f"glu_bwd{i}")
        G["ffn_cw"][i], G["ffn_cb"][i] = dcw, dcb[0]
        dn2 = gu_bwd_x(dgu, P["w_gu"][i], f"d_n2_{i}")
        G["w_gu"][i] = gu_bwd_w(n2, dgu, f"dw_gu{i}")
        dh2, dg = rms_bwd(h2, P["g_ffn"][i], dn2, dh, f"rms_ffn_bwd{i}")
        G["g_ffn"][i] = dg[0]
        dmix = mm_nt(dh2, P["w_out"][i], name=f"d_mix{i}")
        G["w_out"][i] = mm_tn(mix, dh2, name=f"dw_out{i}")
        if i == 0:
            dqkv, dbias, dsk = swa_bwd(proj, bias, sk, dmix)
            dxq, dkv = xattn_bwd(proj, A_Q + 2 * LANE, kv, dmix, "xattn_a_bwd")
            dproj = jnp.concatenate([dqkv, dxq], axis=1)
            G["sinks"] = dsk[:A_HEADS, 0]
            G["rel_bias"] = bias_grad(dbias)[:, :A_HEADS]
            w_in, gname = P["w_in_a"], "w_in_a"
        else:
            qkvn, chunked, states, o = extra
            do, dz, dgo = dnpost_bwd(o, proj, prm, dmix)
            dqkvn, dseg, dprm = dnc_bwd(qkvn, proj, prm, dns_bwd(chunked, states, do))
            draw, dconv = dnprep_bwd(proj, P["conv_qkv"], dqkvn)
            dxq, dkv = xattn_bwd(proj, 2304, kv, dmix, "xattn_b_bwd")
            dproj = jnp.concatenate([draw, dz, dxq, dseg], axis=1)
            G["conv_qkv"] = dconv
            G["a_log"], G["dt_bias"], G["out_norm_g"] = dprm[0, 6:12], dprm[1, 6:12], dgo[0]
            w_in, gname = P["w_in_b"], "w_in_b"
        dn1 = mm_nt(dproj, w_in, name=f"d_n1_{i}")
        G[gname] = mm_tn(n1, dproj, name=f"d{gname}")
        dh, dg = rms_bwd(hin, P["g_mix"][i], dn1, dh2, f"rms_mix_bwd{i}")
        G["g_mix"][i] = dg[0]
        dgm, dwm = memkv_bwd(mem, P["g_mem"][i], P["w_mem"][i], dkv, f"memkv_bwd{i}")
        G["g_mem"][i], G["w_mem"][i] = dgm[0], dwm
    return loss, dh, G


def _prepare(full, w_gu=None):
    return {
        "rel_bias": full["rel_bias"], "sinks": full["sinks_a"][0], "a_log": full["a_log_b"][0],
        "dt_bias": full["dt_bias_b"][0], "out_norm_g": full["out_norm_g_b"][0],
        "g_mix": full["norm_mix_g"], "g_mem": full["norm_mem_g"], "g_ffn": full["norm_ffn_g"],
        "g_fin": full["final_norm_g"], "conv_qkv": full["conv_qkv_b"][0],
        "ffn_cw": [full["ffn_conv_w"][0], full["ffn_conv_w"][1]],
        "ffn_cb": [full["ffn_conv_b"][0], full["ffn_conv_b"][1]],
        "w_mem": [full["w_mem_kv"][0], full["w_mem_kv"][1]],
        "w_out": [_lay_out_a(full["w_out"][0]), full["w_out"][1]],
        "w_in_a": _lay_in_a(full["w_in_a"][0]), "w_in_b": _lay_in_b(full["w_in_b"][0]),
        "w_gu": w_gu if w_gu is not None else [_chip_cols(full["w_gate_up"][0]), _chip_cols(full["w_gate_up"][1])],
        "w_down": [full["w_down"][0], full["w_down"][1]],
    }


def _grads_to_ref(G):
    return {
        "rel_bias": G["rel_bias"], "norm_mix_g": jnp.stack(G["g_mix"]), "norm_mem_g": jnp.stack(G["g_mem"]),
        "w_mem_kv": jnp.stack(G["w_mem"]),
        "w_out": jnp.stack([_unlay_out_a(G["w_out"][0]), G["w_out"][1]]),
        "w_in_a": _unlay_in_a(G["w_in_a"])[None], "sinks_a": G["sinks"][None],
        "w_in_b": _unlay_in_b(G["w_in_b"])[None], "conv_qkv_b": G["conv_qkv"][None],
        "a_log_b": G["a_log"][None], "dt_bias_b": G["dt_bias"][None], "out_norm_g_b": G["out_norm_g"][None],
        "norm_ffn_g": jnp.stack(G["g_ffn"]),
        "w_gate_up": jnp.stack([_unchip_cols(G["w_gu"][0]), _unchip_cols(G["w_gu"][1])]).astype(f32),
        "ffn_conv_w": jnp.stack(G["ffn_cw"]), "ffn_conv_b": jnp.stack(G["ffn_cb"]),
        "w_down": jnp.stack(G["w_down"]), "final_norm_g": G["g_fin"],
    }


ANY = pl.BlockSpec(memory_space=pl.ANY)


def _place():
    return lax.axis_index("x"), lax.axis_index("y"), lax.axis_index("c")


def _chip_exchange(bufs, out_shapes, src_of, name):
    n = len(bufs)

    def body(*refs):
        ins, outs = refs[:n], refs[n:2 * n]
        ssem, rsem, lsem = refs[2 * n:]
        x, y, c = _place()
        me = 2 * x + y
        peers = [(1 - x, y), (x, 1 - y), (1 - x, 1 - y)]

        def remote(j, k, slot):
            px, py = peers[k]
            return pltpu.make_async_remote_copy(
                src_ref=src_of(j, ins[j], 2 * px + py), dst_ref=outs[j].at[slot],
                send_sem=ssem.at[3 * j + k], recv_sem=rsem.at[3 * j + k],
                device_id=(px, py, c), device_id_type=MESH)

        started = []
        for j in range(n):
            lc = pltpu.make_async_copy(src_of(j, ins[j], me), outs[j].at[me], lsem.at[j])
            lc.start()
            started.append(lc)
        sends = [remote(j, k, me) for j in range(n) for k in range(3)]
        for cp in sends:
            cp.start()
        for j in range(n):
            for k in range(3):
                px, py = peers[k]
                remote(j, k, 2 * px + py).wait_recv()
        for cp in sends:
            cp.wait_send()
        for lc in started:
            lc.wait()

    return pl.pallas_call(
        body, name=name, in_specs=[ANY] * n, out_specs=[ANY] * n, out_shape=out_shapes,
        scratch_shapes=[pltpu.SemaphoreType.DMA((3 * n,)), pltpu.SemaphoreType.DMA((3 * n,)),
                        pltpu.SemaphoreType.DMA((n,))],
    )(*bufs)


def chip_scatter(gs):
    return _chip_exchange(gs, [jax.ShapeDtypeStruct(g.shape, g.dtype) for g in gs], lambda j, ref, chip: ref.at[chip],
                          "grad_scatter")


def allreduce_small(buf):
    R = buf.shape[0]

    def body(b_ref, o_ref, recv, ssem, rsem):
        x, y, c = _place()
        me = 4 * x + 2 * y + c

        def peer(k):
            return (1 - x if k & 4 else x, 1 - y if k & 2 else y, 1 - c if k & 1 else c)

        def remote(k, slot):
            return pltpu.make_async_remote_copy(
                src_ref=b_ref, dst_ref=recv.at[slot], send_sem=ssem.at[k - 1], recv_sem=rsem.at[k - 1],
                device_id=peer(k), device_id_type=MESH)

        sends = [remote(k, me) for k in range(1, 8)]
        for cp in sends:
            cp.start()
        recv[me] = b_ref[...]
        for k in range(1, 8):
            px, py, pc = peer(k)
            remote(k, 4 * px + 2 * py + pc).wait_recv()
        for cp in sends:
            cp.wait_send()
        total = recv[0]
        for j in range(1, 8):
            total = total + recv[j]
        o_ref[...] = total

    return pl.pallas_call(
        body, name="small_allreduce",
        in_specs=[pl.BlockSpec(memory_space=pltpu.VMEM)], out_specs=pl.BlockSpec(memory_space=pltpu.VMEM),
        out_shape=jax.ShapeDtypeStruct(buf.shape, f32),
        scratch_shapes=[pltpu.VMEM((8, R, LANE), f32), pltpu.SemaphoreType.DMA((7,)), pltpu.SemaphoreType.DMA((7,))],
    )(buf)


def sum_slots(recv, name):
    _, R, C = recv.shape
    tr = _row_tile(R)

    def body(r_ref, o_ref):
        acc = r_ref[0].astype(f32)
        for s in range(1, 4):
            acc = acc + r_ref[s].astype(f32)
        o_ref[...] = acc

    return pl.pallas_call(
        body, name=name, grid=(R // tr,),
        in_specs=[pl.BlockSpec((4, tr, C), lambda i: (0, i, 0))], out_specs=pl.BlockSpec((tr, C), lambda i: (i, 0)),
        out_shape=jax.ShapeDtypeStruct((R, C), f32), compiler_params=_cp(("parallel",)),
    )(recv)


def _half(ref, core, axis=0):
    half = ref.shape[axis] // 2
    idx = (slice(None),) * axis + (pl.ds(core * half, half),)
    return ref.at[idx]


def gather_weights(bufs, smalls):
    nb, n = len(bufs), len(bufs) + len(smalls)

    def body(*refs):
        ins, outs = refs[:n], refs[n:2 * n]
        isend, irecv, dsend, drecv, lsem = refs[2 * n:]
        x, y, c = _place()
        me = 2 * x + y
        peers = [(1 - x, y), (x, 1 - y), (1 - x, 1 - y)]
        chips = [2 * px + py for px, py in peers]

        def rows(ref, j, core):
            return _half(ref, core) if j < nb else ref

        def ici(j, k, slot):
            px, py = peers[k]
            return pltpu.make_async_remote_copy(
                src_ref=rows(ins[j], j, c), dst_ref=rows(outs[j].at[slot], j, c),
                send_sem=isend.at[3 * j + k], recv_sem=irecv.at[3 * j + k],
                device_id=(px, py, c), device_id_type=MESH)

        def d2d(j, k, core):
            blk = rows(outs[j].at[chips[k]], j, core)
            return pltpu.make_async_remote_copy(
                src_ref=blk, dst_ref=blk, send_sem=dsend.at[3 * j + k], recv_sem=drecv.at[3 * j + k],
                device_id=(x, y, 1 - c), device_id_type=MESH)

        locs = [pltpu.make_async_copy(ins[j], outs[j].at[me], lsem.at[j]) for j in range(n)]
        for lc in locs:
            lc.start()
        sends = [ici(j, k, me) for j in range(n) for k in range(3)]
        for cp in sends:
            cp.start()
        for j in range(n):
            for k in range(3):
                ici(j, k, chips[k]).wait_recv()
                if j < nb:
                    fw = d2d(j, k, c)
                    fw.start()
                    sends.append(fw)
        for j in range(nb):
            for k in range(3):
                d2d(j, k, 1 - c).wait_recv()
        for cp in sends:
            cp.wait_send()
        for lc in locs:
            lc.wait()

    arrs = list(bufs) + list(smalls)
    return pl.pallas_call(
        body, name="weight_allgather", in_specs=[ANY] * n, out_specs=[ANY] * n,
        out_shape=[jax.ShapeDtypeStruct((4,) + a.shape, a.dtype) for a in arrs],
        scratch_shapes=[pltpu.SemaphoreType.DMA((3 * n,)), pltpu.SemaphoreType.DMA((3 * n,)),
                        pltpu.SemaphoreType.DMA((3 * nb,)), pltpu.SemaphoreType.DMA((3 * nb,)),
                        pltpu.SemaphoreType.DMA((n,))],
    )(*arrs)


def pair_exchange(gbufs):
    n = len(gbufs)

    def body(*refs):
        ins, outs = refs[:n], refs[n:2 * n]
        ssem, rsem = refs[2 * n:]
        x, y, c = _place()
        cps = [pltpu.make_async_remote_copy(
            src_ref=_half(ins[j], 1 - c, axis=1), dst_ref=outs[j], send_sem=ssem.at[j], recv_sem=rsem.at[j],
            device_id=(x, y, 1 - c), device_id_type=MESH) for j in range(n)]
        for cp in cps:
            cp.start()
        for cp in cps:
            cp.wait()

    return pl.pallas_call(
        body, name="pair_exchange", in_specs=[ANY] * n, out_specs=[ANY] * n,
        out_shape=[jax.ShapeDtypeStruct((4, g.shape[1] // 2, g.shape[2]), g.dtype) for g in gbufs],
        scratch_shapes=[pltpu.SemaphoreType.DMA((n,)), pltpu.SemaphoreType.DMA((n,))],
    )(*gbufs)


def _row_tile(rows):
    t = 256
    while rows % t:
        t //= 2
    return t


def pair_sum(mine, theirs, core, name):
    _, R, C = mine.shape
    half = R // 2
    tr = _row_tile(half)
    nt = half // tr

    def body(c_ref, a_ref, b_ref, o_ref):
        o_ref[...] = (a_ref[...].astype(f32) + b_ref[...].astype(f32)).astype(bf16)

    return pl.pallas_call(
        body, name=name, out_shape=jax.ShapeDtypeStruct(theirs.shape, bf16),
        grid_spec=pltpu.PrefetchScalarGridSpec(
            num_scalar_prefetch=1, grid=(4, nt),
            in_specs=[pl.BlockSpec((None, tr, C), lambda s, i, c_ref: (s, c_ref[0] * nt + i, 0)),
                      pl.BlockSpec((None, tr, C), lambda s, i, c_ref: (s, i, 0))],
            out_specs=pl.BlockSpec((None, tr, C), lambda s, i, c_ref: (s, i, 0))),
        compiler_params=_cp(("parallel", "parallel")),
    )(jnp.reshape(core, (1,)).astype(jnp.int32), mine, theirs)


def final_exchange(fins):
    n = len(fins)

    def body(*refs):
        ins, outs = refs[:n], refs[n:2 * n]
        ssem, rsem, lsem = refs[2 * n:]
        x, y, c = _place()
        locs = [pltpu.make_async_copy(ins[j], _half(outs[j], c), lsem.at[j]) for j in range(n)]
        cps = [pltpu.make_async_remote_copy(
            src_ref=ins[j], dst_ref=_half(outs[j], c), send_sem=ssem.at[j], recv_sem=rsem.at[j],
            device_id=(x, y, 1 - c), device_id_type=MESH) for j in range(n)]
        for cp in locs + cps:
            cp.start()
        for cp in cps:
            cp.wait()
        for lc in locs:
            lc.wait()

    return pl.pallas_call(
        body, name="final_exchange", in_specs=[ANY] * n, out_specs=[ANY] * n,
        out_shape=[jax.ShapeDtypeStruct((2 * f.shape[0], f.shape[1]), f.dtype) for f in fins],
        scratch_shapes=[pltpu.SemaphoreType.DMA((n,)), pltpu.SemaphoreType.DMA((n,)), pltpu.SemaphoreType.DMA((n,))],
    )(*fins)


def adamw_big(w, m, v, gs, row0, name):
    L, R, C = w.shape
    tr = _row_tile(math.gcd(R, row0) if row0 else R)
    b0 = row0 // tr

    def body(*refs):
        w_ref, m_ref, v_ref = refs[:3]
        g_refs = refs[3:3 + L]
        g_ref, d_ref, nm_ref, nv_ref = refs[3 + L:]
        g = g_refs[0][...]
        for l in range(1, L):
            g = jnp.where(pl.program_id(0) == l, g_refs[l][...], g)
        d, nm, nv = _adamw_math(w_ref[...], g, m_ref[...], v_ref[...])
        g_ref[...] = g
        d_ref[...] = d
        nm_ref[...] = nm
        nv_ref[...] = nv

    own = pl.BlockSpec((None, tr, C), lambda l, i: (l, i, 0))
    off = pl.BlockSpec((tr, C), lambda l, i: (b0 + i, 0))
    return pl.pallas_call(
        body, name=name, grid=(L, R // tr), in_specs=[own, own, own] + [off] * L, out_specs=[own] * 4,
        out_shape=[jax.ShapeDtypeStruct((L, R, C), f32)] * 4, compiler_params=_cp(("parallel", "parallel")),
    )(w, m, v, *gs)


def _adamw_math(w, g, m, v):
    m = B1 * m + (1.0 - B1) * g
    v = B2 * v + (1.0 - B2) * (g * g)
    m_hat = m / (1.0 - B1 ** STEP)
    v_hat = v / (1.0 - B2 ** STEP)
    delta = -LR * (m_hat / (jnp.sqrt(v_hat) + AEPS) + WD * w)
    return delta, m, v


def adamw_small(w, m, v, g):
    def body(w_ref, m_ref, v_ref, g_ref, d_ref, nm_ref, nv_ref):
        d, nm, nv = _adamw_math(w_ref[...], g_ref[...], m_ref[...], v_ref[...])
        d_ref[...] = d
        nm_ref[...] = nm
        nv_ref[...] = nv

    return pl.pallas_call(body, name="adamw_small", out_shape=[jax.ShapeDtypeStruct(w.shape, f32)] * 3)(w, m, v, g)


CONV =(("conv_qkv_b", 2), ("ffn_conv_w", 2))
SMALL = ("rel_bias", "norm_mix_g", "norm_mem_g", "sinks_a", "a_log_b", "dt_bias_b", "out_norm_g_b", "norm_ffn_g",
         "ffn_conv_b", "final_norm_g")
WEIGHTS = ("rel_bias", "norm_mix_g", "norm_mem_g", "w_mem_kv", "w_out", "w_in_a", "sinks_a", "w_in_b", "conv_qkv_b",
           "a_log_b", "dt_bias_b", "out_norm_g_b", "norm_ffn_g", "w_gate_up", "ffn_conv_w", "ffn_conv_b", "w_down",
           "final_norm_g")
ARGS = ("x", "mem") + WEIGHTS + ("loss_target",) + tuple("m_" + n for n in WEIGHTS) + tuple("v_" + n for n in WEIGHTS)


def _rows(a, width):
    flat = a.reshape(-1)
    pad = (-flat.shape[0]) % width
    if pad:
        flat = jnp.concatenate([flat, jnp.zeros((pad,), a.dtype)])
    return flat.reshape(-1, width)


def _nrows(shape, width):
    return -(-math.prod(shape) // width)


def _pack(arrs, width, total_rows, dtype):
    parts = [_rows(a.astype(dtype), width) for a in arrs]
    used = sum(p.shape[0] for p in parts)
    if total_rows > used:
        parts.append(jnp.zeros((total_rows - used, width), dtype))
    return jnp.concatenate(parts, axis=0)


def _unpack(buf, shapes, width):
    out, r = [], 0
    for s in shapes:
        n = _nrows(s, width)
        out.append(buf[r:r + n].reshape(-1)[:math.prod(s)].reshape(s))
        r += n
    return out


def _pad_to(n, mult):
    return -(-n // mult) * mult


def kernel(x, mem, rel_bias, norm_mix_g, norm_mem_g, w_mem_kv, w_out, w_in_a, sinks_a, w_in_b, conv_qkv_b, a_log_b, dt_bias_b, out_norm_g_b, norm_ffn_g, w_gate_up, ffn_conv_w, ffn_conv_b, w_down, final_norm_g, loss_target, m_rel_bias, m_norm_mix_g, m_norm_mem_g, m_w_mem_kv, m_w_out, m_w_in_a, m_sinks_a, m_w_in_b, m_conv_qkv_b, m_a_log_b, m_dt_bias_b, m_out_norm_g_b, m_norm_ffn_g, m_w_gate_up, m_ffn_conv_w, m_ffn_conv_b, m_w_down, m_final_norm_g, v_rel_bias, v_norm_mix_g, v_norm_mem_g, v_w_mem_kv, v_w_out, v_w_in_a, v_sinks_a, v_w_in_b, v_conv_qkv_b, v_a_log_b, v_dt_bias_b, v_out_norm_g_b, v_norm_ffn_g, v_w_gate_up, v_ffn_conv_w, v_ffn_conv_b, v_w_down, v_final_norm_g):
    A = dict(zip(ARGS, (x, mem, rel_bias, norm_mix_g, norm_mem_g, w_mem_kv, w_out, w_in_a, sinks_a, w_in_b, conv_qkv_b, a_log_b, dt_bias_b, out_norm_g_b, norm_ffn_g, w_gate_up, ffn_conv_w, ffn_conv_b, w_down, final_norm_g, loss_target, m_rel_bias, m_norm_mix_g, m_norm_mem_g, m_w_mem_kv, m_w_out, m_w_in_a, m_sinks_a, m_w_in_b, m_conv_qkv_b, m_a_log_b, m_dt_bias_b, m_out_norm_g_b, m_norm_ffn_g, m_w_gate_up, m_ffn_conv_w, m_ffn_conv_b, m_w_down, m_final_norm_g, v_rel_bias, v_norm_mix_g, v_norm_mem_g, v_w_mem_kv, v_w_out, v_w_in_a, v_sinks_a, v_w_in_b, v_conv_qkv_b, v_a_log_b, v_dt_bias_b, v_out_norm_g_b, v_norm_ffn_g, v_w_gate_up, v_ffn_conv_w, v_ffn_conv_b, v_w_down, v_final_norm_g)))
    chip = 2 * lax.axis_index("x") + lax.axis_index("y")
    core = lax.axis_index("c")
    n_down, n_out, n_mem = w_down.shape[1], w_out.shape[1], w_mem_kv.shape[1]

    shards = [w_gate_up[0], w_gate_up[1],
              jnp.concatenate([w_down.reshape(2 * n_down, D), w_out.reshape(2 * n_out, D)], axis=0),
              w_mem_kv.reshape(2 * n_mem, 2 * X_Q), w_in_a[0], w_in_b[0]]
    got = gather_weights([s.astype(bf16) for s in shards], [conv_qkv_b[0], ffn_conv_w.reshape(6, -1)])
    g_gu0, g_gu1, g_rows, g_mem, g_ina, g_inb, g_cq, g_cf = got

    def layers(slab, n):
        return jnp.moveaxis(slab.reshape(4, 2, n, slab.shape[-1]), 0, 1).reshape(2, 4 * n, slab.shape[-1])

    full = {n: A[n] for n in SMALL}
    full["w_down"] = layers(g_rows[:, :2 * n_down], n_down)
    full["w_out"] = layers(g_rows[:, 2 * n_down:], n_out)
    full["w_mem_kv"] = layers(g_mem, n_mem)
    full["w_in_a"] = _unchip_cols(g_ina)[None]
    full["w_in_b"] = _unchip_cols(g_inb)[None]
    full["conv_qkv_b"] = _unchip_cols(g_cq)[None]
    full["ffn_conv_w"] = _unchip_cols(g_cf).reshape(2, 3, D_FF)

    loss, dx, G = _local_step(x[0], mem[0], loss_target[0], _prepare(full, w_gu=[g_gu0, g_gu1]))
    gfull = _grads_to_ref(G)

    def chip_rows(g, n):
        return g.reshape(4, n, g.shape[-1])

    partial = [G["w_gu"][0], G["w_gu"][1],
               jnp.concatenate([chip_rows(G["w_down"][0], n_down), chip_rows(G["w_down"][1], n_down),
                                chip_rows(_unlay_out_a(G["w_out"][0]), n_out), chip_rows(G["w_out"][1], n_out)],
                               axis=1).astype(bf16),
               jnp.concatenate([chip_rows(G["w_mem"][0], n_mem), chip_rows(G["w_mem"][1], n_mem)], axis=1).astype(bf16),
               _chip_cols(_unlay_in_a(G["w_in_a"])).astype(bf16), _chip_cols(_unlay_in_b(G["w_in_b"])).astype(bf16)]
    theirs = pair_exchange(partial)
    names = ("gu0", "gu1", "rows", "mem", "in_a", "in_b")
    pair = [pair_sum(p, t, core, "pair_sum_" + nm) for p, t, nm in zip(partial, theirs, names)]
    arrived = chip_scatter(pair)
    halves = [sum_slots(r, "sum_slots_" + nm) for r, nm in zip(arrived, names)]
    f_gu0, f_gu1, f_rows, f_mem, f_ina, f_inb = final_exchange(halves)

    sm_shapes = [A[n].shape for n in SMALL] + [gfull[n].shape for n, _ in CONV] + [(LANE,)]
    sm_rows = _pad_to(sum(_nrows(s, LANE) for s in sm_shapes), 8)
    sbuf = _pack([gfull[n] for n in SMALL] + [gfull[n] for n, _ in CONV] + [loss[0]], LANE, sm_rows, f32)
    tot = _unpack(allreduce_small(sbuf), sm_shapes, LANE)
    gsmall = dict(zip(SMALL, tot[:len(SMALL)]))
    for (n, axis), t in zip(CONV, tot[len(SMALL):len(SMALL) + len(CONV)]):
        sh = A[n].shape[axis]
        gsmall[n] = lax.dynamic_slice_in_dim(t, chip * sh, sh, axis)
    loss_out = tot[-1][0]

    out = {}
    plan = (("w_gate_up", (2, D, GU_SHARD), [f_gu0, f_gu1], 0), ("w_down", (1, 2 * n_down, D), [f_rows], 0),
            ("w_out", (1, 2 * n_out, D), [f_rows], 2 * n_down), ("w_mem_kv", (1, 2 * n_mem, 2 * X_Q), [f_mem], 0),
            ("w_in_a", (1, D, IN_A // 4), [f_ina], 0), ("w_in_b", (1, D, IN_B // 4), [f_inb], 0))
    for n, shape3, gs, row0 in plan:
        res = adamw_big(A[n].reshape(shape3), A["m_" + n].reshape(shape3), A["v_" + n].reshape(shape3), gs, row0,
                        "adamw_" + n)
        for key, r in zip(("grad_", "delta_", "new_m_", "new_v_"), res):
            out[key + n] = r.reshape(A[n].shape)
    names = SMALL + tuple(n for n, _ in CONV)
    shapes = [A[n].shape for n in names]
    rows = _pad_to(sum(_nrows(s, LANE) for s in shapes), 8)
    packs = [_pack([src[n] for n in names], LANE, rows, f32)
             for src in ({n: A[n] for n in names}, {n: A["m_" + n] for n in names}, {n: A["v_" + n] for n in names}, gsmall)]
    res = adamw_small(*packs)
    for key, r in zip(("delta_", "new_m_", "new_v_"), res):
        for n, a in zip(names, _unpack(r, shapes, LANE)):
            out[key + n] = a
    for n in names:
        out["grad_" + n] = gsmall[n]
    return (loss_out, dx[None], *[out["grad_" + n] for n in WEIGHTS], *[out["delta_" + n] for n in WEIGHTS],
            *[out["new_m_" + n] for n in WEIGHTS], *[out["new_v_" + n] for n in WEIGHTS])
```

```python
import functools
import math

import numpy as np
import jax
import jax.numpy as jnp
from jax import lax
from jax.experimental import pallas as pl
from jax.experimental.pallas import tpu as pltpu

f32 = jnp.float32
bf16 = jnp.bfloat16
HI = lax.Precision.HIGHEST
MESH = pl.DeviceIdType.MESH

D = 1024
MEM_LEN = 256
EPS = 1e-6
A_HEADS, A_KV, A_DH = 12, 2, 64
A_Q = 768
BLK = 128
N_BUCKETS, MAX_DIST = 32, 128
B_QK, B_V, B_DH = 384, 768, 128
B_QKV = 1536
CHUNK = 64
X_Q = 256
D_FF = 2816
IN_A = 1280
IN_B = 2572
IN_B_PAD = 2688
LANE = 128
VMEM_LIMIT = 56 * 1024 * 1024

LR, B1, B2, AEPS, WD, STEP = 0.001, 0.9, 0.999, 1e-08, 0.01, 10


def _cp(sem=None):
    return pltpu.CompilerParams(dimension_semantics=sem, vmem_limit_bytes=VMEM_LIMIT)


def _dg(a, b, ca, cb, prec=None):
    return lax.dot_general(a, b, (((ca,), (cb,)), ((), ())), precision=prec, preferred_element_type=f32)


@jax.custom_vjp
def bdot(a, b):
    return _dg(a.astype(bf16), b.astype(bf16), 1, 0)


def _bdot_f(a, b):
    return bdot(a, b), (a, b)


def _bdot_b(res, g):
    a, b = res
    gb = g.astype(bf16)
    return _dg(gb, b.astype(bf16), 1, 1), _dg(a.astype(bf16), gb, 0, 0)


bdot.defvjp(_bdot_f, _bdot_b)


@jax.custom_vjp
def bdot_nt(a, b):
    return _dg(a.astype(bf16), b.astype(bf16), 1, 1)


def _bdot_nt_f(a, b):
    return bdot_nt(a, b), (a, b)


def _bdot_nt_b(res, g):
    a, b = res
    gb = g.astype(bf16)
    return _dg(gb, b.astype(bf16), 1, 0), _dg(gb, a.astype(bf16), 0, 0)


bdot_nt.defvjp(_bdot_nt_f, _bdot_nt_b)


def _shift_rows(x, s, down):
    n = x.shape[0]
    row = lax.broadcasted_iota(jnp.int32, x.shape, 0)
    if down:
        return jnp.where(row >= s, pltpu.roll(x, s, 0), 0.0)
    return jnp.where(row < n - s, pltpu.roll(x, n - s, 0), 0.0)


@functools.partial(jax.custom_vjp, nondiff_argnums=(1,))
def shift_down(x, s):
    return _shift_rows(x, s, True)


def _sd_f(x, s):
    return _shift_rows(x, s, True), None


def _sd_b(s, _, g):
    return (_shift_rows(g, s, False),)


shift_down.defvjp(_sd_f, _sd_b)


def _sigmoid(x):
    return 1.0 / (1.0 + jnp.exp(-x))


def _silu(x):
    return x * _sigmoid(x)


def _rms(x, g):
    return x * lax.rsqrt(jnp.mean(x * x, axis=-1, keepdims=True) + EPS) * g


def _tile(n, cap):
    u = n // LANE
    best = 1
    for d in range(1, u + 1):
        if u % d == 0 and d * LANE <= cap:
            best = d
    return best * LANE


def mm_nn(a, w, res=None, out_dtype=f32, name="mm_nn"):
    M, K = a.shape
    N = w.shape[1]
    tm, tn = min(512, M), _tile(N, 640)

    def body(*refs):
        if res is None:
            a_ref, w_ref, o_ref = refs
            o_ref[...] = _dg(a_ref[...].astype(bf16), w_ref[...], 1, 0).astype(out_dtype)
        else:
            a_ref, w_ref, r_ref, o_ref = refs
            o_ref[...] = (r_ref[...] + _dg(a_ref[...].astype(bf16), w_ref[...], 1, 0)).astype(out_dtype)

    in_specs = [pl.BlockSpec((tm, K), lambda n, m: (m, 0)), pl.BlockSpec((K, tn), lambda n, m: (0, n))]
    args = [a, w]
    if res is not None:
        in_specs.append(pl.BlockSpec((tm, tn), lambda n, m: (m, n)))
        args.append(res)
    return pl.pallas_call(
        body, name=name, grid=(N // tn, M // tm), in_specs=in_specs,
        out_specs=pl.BlockSpec((tm, tn), lambda n, m: (m, n)),
        out_shape=jax.ShapeDtypeStruct((M, N), out_dtype),
        compiler_params=_cp(("parallel", "parallel")),
    )(*args)


def mm_nt(dy, w, name="mm_nt"):
    M, N = dy.shape
    K = w.shape[0]
    tm, tn = min(512, M), _tile(N, 512)

    def body(dy_ref, w_ref, o_ref):
        @pl.when(pl.program_id(1) == 0)
        def _():
            o_ref[...] = jnp.zeros_like(o_ref)
        o_ref[...] += _dg(dy_ref[...].astype(bf16), w_ref[...], 1, 1)

    return pl.pallas_call(
        body, name=name, grid=(M // tm, N // tn),
        in_specs=[pl.BlockSpec((tm, tn), lambda m, n: (m, n)), pl.BlockSpec((K, tn), lambda m, n: (0, n))],
        out_specs=pl.BlockSpec((tm, K), lambda m, n: (m, 0)),
        out_shape=jax.ShapeDtypeStruct((M, K), f32),
        compiler_params=_cp(("parallel", "arbitrary")),
    )(dy, w)


def mm_tn(a, dy, name="mm_tn"):
    M, K = a.shape
    N = dy.shape[1]
    tm, tk, tn = min(512, M), _tile(K, 1408), _tile(N, 1024)

    def body(a_ref, dy_ref, o_ref):
        @pl.when(pl.program_id(2) == 0)
        def _():
            o_ref[...] = jnp.zeros_like(o_ref)
        o_ref[...] += _dg(a_ref[...].astype(bf16), dy_ref[...].astype(bf16), 0, 0)

    return pl.pallas_call(
        body, name=name, grid=(K // tk, N // tn, M // tm),
        in_specs=[pl.BlockSpec((tm, tk), lambda k, n, m: (m, k)), pl.BlockSpec((tm, tn), lambda k, n, m: (m, n))],
        out_specs=pl.BlockSpec((tk, tn), lambda k, n, m: (k, n)),
        out_shape=jax.ShapeDtypeStruct((K, N), f32),
        compiler_params=_cp(("parallel", "parallel", "arbitrary")),
    )(a, dy)


def rms_fwd(h, g, name):
    S = h.shape[0]
    t = min(512, S)

    def body(h_ref, g_ref, o_ref):
        o_ref[...] = _rms(h_ref[...], g_ref[...]).astype(bf16)

    return pl.pallas_call(
        body, name=name, grid=(S // t,),
        in_specs=[pl.BlockSpec((t, D), lambda i: (i, 0)), pl.BlockSpec((1, D), lambda i: (0, 0))],
        out_specs=pl.BlockSpec((t, D), lambda i: (i, 0)),
        out_shape=jax.ShapeDtypeStruct((S, D), bf16),
        compiler_params=_cp(("parallel",)),
    )(h, g.reshape(1, D))


def rms_bwd(h, g, dn, dres, name):
    S = h.shape[0]
    t = min(512, S)

    def body(h_ref, g_ref, dn_ref, dr_ref, dh_ref, dg_ref):
        @pl.when(pl.program_id(0) == 0)
        def _():
            dg_ref[...] = jnp.zeros_like(dg_ref)
        _, vjp = jax.vjp(_rms, h_ref[...], g_ref[...])
        dh, dg = vjp(dn_ref[...])
        dh_ref[...] = dr_ref[...] + dh
        dg_ref[...] += dg

    tok = pl.BlockSpec((t, D), lambda i: (i, 0))
    vec = pl.BlockSpec((1, D), lambda i: (0, 0))
    return pl.pallas_call(
        body, name=name, grid=(S // t,), in_specs=[tok, vec, tok, tok], out_specs=[tok, vec],
        out_shape=[jax.ShapeDtypeStruct((S, D), f32), jax.ShapeDtypeStruct((1, D), f32)],
        compiler_params=_cp(("arbitrary",)),
    )(h, g.reshape(1, D), dn, dres)


def loss_head(h, g, target):
    S = h.shape[0]
    t = min(512, S)

    def f(hh, gg, tt):
        err = _rms(hh, gg) - tt
        return 0.5 * jnp.sum(jnp.mean(err * err, axis=-1, keepdims=True), axis=0, keepdims=True)

    def body(h_ref, g_ref, t_ref, loss_ref, dh_ref, dg_ref):
        @pl.when(pl.program_id(0) == 0)
        def _():
            dg_ref[...] = jnp.zeros_like(dg_ref)
            loss_ref[...] = jnp.zeros_like(loss_ref)
        val, vjp = jax.vjp(lambda a, b: f(a, b, t_ref[...]), h_ref[...], g_ref[...])
        dh, dg = vjp(jnp.ones((1, 1), f32))
        dh_ref[...] = dh
        dg_ref[...] += dg
        loss_ref[...] += jnp.broadcast_to(val, loss_ref.shape)

    tok = pl.BlockSpec((t, D), lambda i: (i, 0))
    vec = pl.BlockSpec((1, D), lambda i: (0, 0))
    return pl.pallas_call(
        body, name="loss_head", grid=(S // t,), in_specs=[tok, vec, tok],
        out_specs=[pl.BlockSpec((1, LANE), lambda i: (0, 0)), tok, vec],
        out_shape=[jax.ShapeDtypeStruct((1, LANE), f32), jax.ShapeDtypeStruct((S, D), f32),
                   jax.ShapeDtypeStruct((1, D), f32)],
        compiler_params=_cp(("arbitrary",)),
    )(h, g.reshape(1, D), target)


def memkv_fwd(mem, g, w, name):
    def body(m_ref, g_ref, w_ref, o_ref):
        o_ref[...] = _dg(_rms(m_ref[...], g_ref[...]).astype(bf16), w_ref[...], 1, 0)

    return pl.pallas_call(
        body, name=name, out_shape=jax.ShapeDtypeStruct((MEM_LEN, 2 * X_Q), f32), compiler_params=_cp(),
    )(mem, g.reshape(1, D), w)


def memkv_bwd(mem, g, w, dkv, name):
    def body(m_ref, g_ref, w_ref, d_ref, dg_ref, dw_ref):
        n, vjp = jax.vjp(lambda gg: _rms(m_ref[...], gg), g_ref[...])
        db = d_ref[...].astype(bf16)
        dw_ref[...] = _dg(n.astype(bf16), db, 0, 0)
        dg_ref[...] = vjp(_dg(db, w_ref[...], 1, 1))[0]

    return pl.pallas_call(
        body, name=name,
        out_shape=[jax.ShapeDtypeStruct((1, D), f32), jax.ShapeDtypeStruct((D, 2 * X_Q), f32)],
        compiler_params=_cp(),
    )(mem, g.reshape(1, D), w, dkv)


def _xattn_f(xq, mk, mv):
    lane = lax.broadcasted_iota(jnp.int32, (1, X_Q), 1)
    out = jnp.zeros(xq.shape, f32)
    for hd in range(4):
        msk = (lane // 64 == hd).astype(f32)
        s = bdot_nt(xq * msk, mk) * (64 ** -0.5)
        m = lax.stop_gradient(jnp.max(s, axis=-1, keepdims=True))
        p = jnp.exp(s - m)
        p = p / jnp.sum(p, axis=-1, keepdims=True)
        out = out + bdot(p, mv * msk)
    return out


def xattn_fwd(proj, col, kv, name):
    S = proj.shape[0]
    t = min(512, S)
    cb = col // X_Q

    def body(q_ref, k_ref, v_ref, o_ref):
        o_ref[...] = _xattn_f(q_ref[...], k_ref[...], v_ref[...])

    return pl.pallas_call(
        body, name=name, grid=(S // t,),
        in_specs=[pl.BlockSpec((t, X_Q), lambda i: (i, cb)), pl.BlockSpec((MEM_LEN, X_Q), lambda i: (0, 0)),
                  pl.BlockSpec((MEM_LEN, X_Q), lambda i: (0, 1))],
        out_specs=pl.BlockSpec((t, X_Q), lambda i: (i, 0)),
        out_shape=jax.ShapeDtypeStruct((S, X_Q), f32),
        compiler_params=_cp(("parallel",)),
    )(proj, kv, kv)


def xattn_bwd(proj, col, kv, dmix, name):
    S = proj.shape[0]
    t = min(512, S)
    cb = col // X_Q

    def body(q_ref, k_ref, v_ref, do_ref, dq_ref, dk_ref, dv_ref):
        @pl.when(pl.program_id(0) == 0)
        def _():
            dk_ref[...] = jnp.zeros_like(dk_ref)
            dv_ref[...] = jnp.zeros_like(dv_ref)
        _, vjp = jax.vjp(_xattn_f, q_ref[...], k_ref[...], v_ref[...])
        dq, dk, dv = vjp(do_ref[...])
        dq_ref[...] = dq
        dk_ref[...] += dk
        dv_ref[...] += dv

    kvb = pl.BlockSpec((MEM_LEN, X_Q), lambda i: (0, 0))
    dq, dk, dv = pl.pallas_call(
        body, name=name, grid=(S // t,),
        in_specs=[pl.BlockSpec((t, X_Q), lambda i: (i, cb)), kvb,
                  pl.BlockSpec((MEM_LEN, X_Q), lambda i: (0, 1)), pl.BlockSpec((t, X_Q), lambda i: (i, 3))],
        out_specs=[pl.BlockSpec((t, X_Q), lambda i: (i, 0)), kvb, kvb],
        out_shape=[jax.ShapeDtypeStruct((S, X_Q), f32), jax.ShapeDtypeStruct((MEM_LEN, X_Q), f32),
                   jax.ShapeDtypeStruct((MEM_LEN, X_Q), f32)],
        compiler_params=_cp(("arbitrary",)),
    )(proj, kv, kv, dmix)
    return dq, jnp.concatenate([dk, dv], axis=1)


def _bucket_map():
    qi = np.arange(BLK)[:, None]
    kj = np.arange(2 * BLK)[None, :]
    n = np.maximum(BLK + qi - kj, 0)
    max_exact = N_BUCKETS // 2
    nf = np.maximum(n, 1).astype(np.float64)
    large = max_exact + (np.log(nf / max_exact) / math.log(MAX_DIST / max_exact)
                         * (N_BUCKETS - max_exact)).astype(np.int32)
    large = np.minimum(large, N_BUCKETS - 1)
    return np.where(n < max_exact, n, large).astype(np.int32)


def bias_build(rel_bias):
    def body(rb_ref, bk_ref, o_ref):
        bk = bk_ref[...]
        for h in range(A_HEADS):
            acc = jnp.zeros((BLK, 2 * BLK), f32)
            for b in range(N_BUCKETS):
                acc = jnp.where(bk == b, rb_ref[b, h], acc)
            o_ref[h] = acc

    return pl.pallas_call(
        body, name="bias_build",
        in_specs=[pl.BlockSpec(memory_space=pltpu.SMEM), pl.BlockSpec(memory_space=pltpu.VMEM)],
        out_specs=pl.BlockSpec(memory_space=pltpu.VMEM),
        out_shape=jax.ShapeDtypeStruct((A_HEADS, BLK, 2 * BLK), f32), compiler_params=_cp(),
    )(rel_bias, jnp.asarray(_bucket_map()))


def bias_grad(dbias):
    def body(d_ref, bk_ref, o_ref):
        bk = bk_ref[...]
        row = lax.broadcasted_iota(jnp.int32, (N_BUCKETS, LANE), 0)
        lane = lax.broadcasted_iota(jnp.int32, (N_BUCKETS, LANE), 1)
        acc = jnp.zeros((N_BUCKETS, LANE), f32)
        for h in range(A_HEADS):
            d = d_ref[h]
            for b in range(N_BUCKETS):
                s = jnp.sum(jnp.where(bk == b, d, 0.0), keepdims=True)
                acc = acc + jnp.where((row == b) & (lane == h), s, 0.0)
        o_ref[...] = acc

    return pl.pallas_call(
        body, name="bias_grad", out_shape=jax.ShapeDtypeStruct((N_BUCKETS, LANE), f32), compiler_params=_cp(),
    )(dbias, jnp.asarray(_bucket_map()))


def _swa_f(qb, kp, kc, vp, vc, bias, sk, first):
    kband = jnp.concatenate([kp, kc], axis=0)
    vband = jnp.concatenate([vp, vc], axis=0)
    qi = lax.broadcasted_iota(jnp.int32, (BLK, 2 * BLK), 0)
    kj = lax.broadcasted_iota(jnp.int32, (BLK, 2 * BLK), 1)
    rel = kj - qi
    ok = (rel >= 1) & (rel <= BLK) & ((kj >= BLK) | jnp.logical_not(first))
    lane = lax.broadcasted_iota(jnp.int32, (1, LANE), 1)
    lane_b = lax.broadcasted_iota(jnp.int32, (BLK, LANE), 1)
    outs = []
    for p in range(A_HEADS // 2):
        qp = qb[:, LANE * p:LANE * (p + 1)]
        acc = jnp.zeros((BLK, LANE), f32)
        for g in range(2):
            h = g * (A_HEADS // 2) + p
            msk = (lane // A_DH == g).astype(f32)
            s = bdot_nt(qp * msk, kband) * (A_DH ** -0.5) + bias[h]
            s = jnp.where(ok, s, -1e30)
            skb = jnp.broadcast_to(sk[h:h + 1, :], (BLK, LANE))
            sink = jnp.sum(jnp.where(lane_b == 0, skb, 0.0), axis=-1, keepdims=True)
            m = lax.stop_gradient(jnp.maximum(jnp.max(s, axis=-1, keepdims=True), sink))
            e = jnp.exp(s - m)
            prob = e / (jnp.sum(e, axis=-1, keepdims=True) + jnp.exp(sink - m))
            acc = acc + bdot(prob, vband) * msk
        outs.append(acc)
    return jnp.concatenate(outs, axis=1)


def _swa_specs(nb, rev):
    bi = (lambda i: nb - 1 - i) if rev else (lambda i: i)
    return [
        pl.BlockSpec((BLK, A_Q), lambda i: (bi(i), 0)),
        pl.BlockSpec((BLK, LANE), lambda i: (jnp.maximum(bi(i) - 1, 0), 6)),
        pl.BlockSpec((BLK, LANE), lambda i: (bi(i), 6)),
        pl.BlockSpec((BLK, LANE), lambda i: (jnp.maximum(bi(i) - 1, 0), 7)),
        pl.BlockSpec((BLK, LANE), lambda i: (bi(i), 7)),
        pl.BlockSpec((A_HEADS, BLK, 2 * BLK), lambda i: (0, 0, 0)),
        pl.BlockSpec((16, LANE), lambda i: (0, 0)),
    ]


def swa_fwd(proj, bias, sk):
    S = proj.shape[0]
    nb = S // BLK

    def body(q_ref, kp_ref, kc_ref, vp_ref, vc_ref, b_ref, s_ref, o_ref):
        o_ref[...] = _swa_f(q_ref[...], kp_ref[...], kc_ref[...], vp_ref[...], vc_ref[...], b_ref[...], s_ref[...],
                            pl.program_id(0) == 0)

    return pl.pallas_call(
        body, name="swa_fwd", grid=(nb,), in_specs=_swa_specs(nb, False),
        out_specs=pl.BlockSpec((BLK, A_Q), lambda i: (i, 0)),
        out_shape=jax.ShapeDtypeStruct((S, A_Q), f32), compiler_params=_cp(("parallel",)),
    )(proj, proj, proj, proj, proj, bias, sk)


def swa_bwd(proj, bias, sk, dmix):
    S = proj.shape[0]
    nb = S // BLK

    def body(q_ref, kp_ref, kc_ref, vp_ref, vc_ref, b_ref, s_ref, do_ref, dqkv_ref, db_ref, ds_ref, ck, cv):
        i = pl.program_id(0)

        @pl.when(i == 0)
        def _():
            db_ref[...] = jnp.zeros_like(db_ref)
            ds_ref[...] = jnp.zeros_like(ds_ref)
            ck[...] = jnp.zeros_like(ck)
            cv[...] = jnp.zeros_like(cv)
        first = i == nb - 1
        _, vjp = jax.vjp(lambda *a: _swa_f(*a, first), q_ref[...], kp_ref[...], kc_ref[...], vp_ref[...],
                         vc_ref[...], b_ref[...], s_ref[...])
        dq, dkp, dkc, dvp, dvc, db, ds = vjp(do_ref[...])
        dqkv_ref[...] = jnp.concatenate([dq, dkc + ck[...], dvc + cv[...]], axis=1)
        ck[...] = dkp
        cv[...] = dvp
        db_ref[...] += db
        ds_ref[...] += ds

    return pl.pallas_call(
        body, name="swa_bwd", grid=(nb,),
        in_specs=_swa_specs(nb, True) + [pl.BlockSpec((BLK, A_Q), lambda i: (nb - 1 - i, 0))],
        out_specs=[pl.BlockSpec((BLK, D), lambda i: (nb - 1 - i, 0)),
                   pl.BlockSpec((A_HEADS, BLK, 2 * BLK), lambda i: (0, 0, 0)),
                   pl.BlockSpec((16, LANE), lambda i: (0, 0))],
        out_shape=[jax.ShapeDtypeStruct((S, D), f32), jax.ShapeDtypeStruct((A_HEADS, BLK, 2 * BLK), f32),
                   jax.ShapeDtypeStruct((16, LANE), f32)],
        scratch_shapes=[pltpu.VMEM((BLK, LANE), f32), pltpu.VMEM((BLK, LANE), f32)],
        compiler_params=_cp(("arbitrary",)),
    )(proj, proj, proj, proj, proj, bias, sk, dmix)


def _dnprep_f(x, w, is_qk):
    c = (w[3:4] * x + w[2:3] * shift_down(x, 1) + w[1:2] * shift_down(x, 2) + w[0:1] * shift_down(x, 3))
    a = _silu(c)
    n = a * lax.rsqrt(jnp.sum(a * a, axis=-1, keepdims=True) + EPS)
    return jnp.where(is_qk, n, a)


def dnprep_fwd(proj, cw):
    S = proj.shape[0]
    nblk = B_QKV // LANE

    def body(x_ref, w_ref, o_ref):
        o_ref[...] = _dnprep_f(x_ref[...], w_ref[...], pl.program_id(0) < 2 * B_QK // LANE)

    return pl.pallas_call(
        body, name="dnprep_fwd", grid=(nblk,),
        in_specs=[pl.BlockSpec((S, LANE), lambda j: (0, j)), pl.BlockSpec((4, LANE), lambda j: (0, j))],
        out_specs=pl.BlockSpec((S, LANE), lambda j: (0, j)),
        out_shape=jax.ShapeDtypeStruct((S, B_QKV), f32), compiler_params=_cp(("parallel",)),
    )(proj, cw)


def dnprep_bwd(proj, cw, dqkvn):
    S = proj.shape[0]
    nblk = B_QKV // LANE

    def body(x_ref, w_ref, d_ref, dx_ref, dw_ref):
        is_qk = pl.program_id(0) < 2 * B_QK // LANE
        _, vjp = jax.vjp(lambda a, b: _dnprep_f(a, b, is_qk), x_ref[...], w_ref[...])
        dx, dw = vjp(d_ref[...])
        dx_ref[...] = dx
        dw_ref[...] = dw

    col = pl.BlockSpec((S, LANE), lambda j: (0, j))
    wsp = pl.BlockSpec((4, LANE), lambda j: (0, j))
    return pl.pallas_call(
        body, name="dnprep_bwd", grid=(nblk,), in_specs=[col, wsp, col], out_specs=[col, wsp],
        out_shape=[jax.ShapeDtypeStruct((S, B_QKV), f32), jax.ShapeDtypeStruct((4, B_QKV), f32)],
        compiler_params=_cp(("parallel",)),
    )(proj, cw, dqkvn)


def _hdot(a, b, ca=1, cb=0):
    return _dg(a, b, ca, cb, HI)


def _bdg(a, b, ca, cb):
    return lax.dot_general(a, b, (((ca,), (cb,)), ((0,), (0,))), precision=HI, preferred_element_type=f32)


@jax.custom_vjp
def hbd(a, b):
    return _bdg(a, b, 2, 1)


@jax.custom_vjp
def hbd_nt(a, b):
    return _bdg(a, b, 2, 2)


@jax.custom_vjp
def hbd_tn(a, b):
    return _bdg(a, b, 1, 1)


hbd.defvjp(lambda a, b: (hbd(a, b), (a, b)), lambda r, g: (hbd_nt(g, r[1]), hbd_tn(r[0], g)))
hbd_nt.defvjp(lambda a, b: (hbd_nt(a, b), (a, b)), lambda r, g: (hbd(g, r[1]), hbd_tn(g, r[0])))
hbd_tn.defvjp(lambda a, b: (hbd_tn(a, b), (a, b)), lambda r, g: (hbd_nt(r[1], g), hbd(r[0], g)))


def _stack(xs):
    return jnp.concatenate([x[None] for x in xs], axis=0)


def _lane_col(x, j):
    lane = lax.broadcasted_iota(jnp.int32, (1, LANE), 1)
    return jnp.sum(jnp.where(lane == j, x, 0.0), axis=-1, keepdims=True)


def _dnc_f(q, k, v, seg, prm):
    C = CHUNK
    beta_all = _sigmoid(seg)
    xx = seg + prm[1:2]
    g_all = -jnp.exp(prm[0:1]) * (jnp.maximum(xx, 0.0) + jnp.log(1.0 + jnp.exp(-jnp.abs(xx))))
    r2 = lax.broadcasted_iota(jnp.int32, (C, C), 0)
    c2 = lax.broadcasted_iota(jnp.int32, (C, C), 1)
    gc_all = _hdot((r2 >= c2).astype(f32), g_all)
    beta = _stack([_lane_col(beta_all, h) for h in range(6)])
    gc = _stack([_lane_col(gc_all, 6 + h) for h in range(6)])
    r = lax.broadcasted_iota(jnp.int32, (1, C, C), 1)
    c = lax.broadcasted_iota(jnp.int32, (1, C, C), 2)
    incl = r >= c
    strict = r > c
    eye = (r == c).astype(f32)
    g_row = hbd(jnp.ones((6, C, C), f32), eye * gc)
    decay = jnp.where(incl, jnp.exp(jnp.where(incl, gc - g_row, 0.0)), 0.0)
    a_mat = beta * hbd_nt(k, k) * jnp.where(strict, decay, 0.0)
    eg = jnp.exp(gc)
    pw = -a_mat
    inv = eye + pw
    for _ in range(5):
        pw = hbd(pw, pw)
        inv = inv + hbd(inv, pw)
    u = hbd(inv, beta * v)
    w = hbd(inv, (beta * eg) * k)
    qc = q * (B_DH ** -0.5)
    attn = hbd_nt(qc, k) * decay
    last = (lax.broadcasted_iota(jnp.int32, (1, C, 1), 1) == C - 1).astype(f32)
    g_last = jnp.sum(gc * last, axis=1, keepdims=True)
    dc = jnp.broadcast_to(jnp.exp(g_last), (6, 1, LANE)).reshape(6, LANE)
    return u, w, qc * eg, k * jnp.exp(g_last - gc), attn, dc


def _dns_f(S0, u, w, qd, kt, attn, dcrows):
    dc = _lane_col(dcrows, 0).reshape(6, 1, 1)
    delta = u - hbd(w, S0)
    out = hbd(qd, S0) + hbd(attn, delta)
    return out, dc * S0 + hbd_tn(kt, delta)


def _dnpost_f(o, z, grow):
    outs = []
    for h in range(6):
        oh = o[:, LANE * h:LANE * (h + 1)]
        outs.append(oh * lax.rsqrt(jnp.mean(oh * oh, axis=-1, keepdims=True) + EPS) * grow
                    * _silu(z[:, LANE * h:LANE * (h + 1)]))
    return jnp.concatenate(outs, axis=1)


def _hs(h):
    return slice(LANE * h, LANE * (h + 1))


def _heads(ref, share):
    return _stack([ref[:, _hs(h // share)] for h in range(6)])


def _put_heads(ref, val):
    for h in range(6):
        ref[:, _hs(h)] = val[h]


def _dnc_in_specs():
    return [
        pl.BlockSpec((CHUNK, B_QK), lambda n: (n, 0)),
        pl.BlockSpec((CHUNK, B_QK), lambda n: (n, 1)),
        pl.BlockSpec((CHUNK, B_V), lambda n: (n, 1)),
        pl.BlockSpec((CHUNK, LANE), lambda n: (n, 20)),
        pl.BlockSpec((8, LANE), lambda n: (0, 0)),
    ]


def _dnc_out_specs(rev_nc=None):
    ci = (lambda n: n) if rev_nc is None else (lambda n: rev_nc - 1 - n)
    wide = pl.BlockSpec((CHUNK, B_V), lambda n: (ci(n), 0))
    return [wide, wide, wide, wide, pl.BlockSpec((1, 6, CHUNK, CHUNK), lambda n: (ci(n), 0, 0, 0)),
            pl.BlockSpec((1, 8, LANE), lambda n: (ci(n), 0, 0))]


def _dnc_shapes(S):
    nc = S // CHUNK
    wide = jax.ShapeDtypeStruct((S, B_V), f32)
    return [wide, wide, wide, wide, jax.ShapeDtypeStruct((nc, 6, CHUNK, CHUNK), f32),
            jax.ShapeDtypeStruct((nc, 8, LANE), f32)]


def dnc_fwd(qkvn, proj, prm):
    S = proj.shape[0]

    def body(q_ref, k_ref, v_ref, s_ref, p_ref, u_ref, w_ref, qd_ref, kt_ref, at_ref, dc_ref):
        u, w, qd, kt, attn, dc = _dnc_f(_heads(q_ref, 2), _heads(k_ref, 2), _heads(v_ref, 1), s_ref[...], p_ref[...])
        _put_heads(u_ref, u)
        _put_heads(w_ref, w)
        _put_heads(qd_ref, qd)
        _put_heads(kt_ref, kt)
        at_ref[0] = attn
        dc_ref[0] = jnp.concatenate([dc, jnp.zeros((2, LANE), f32)], axis=0)

    return pl.pallas_call(
        body, name="dn_chunk_fwd", grid=(S // CHUNK,), in_specs=_dnc_in_specs(), out_specs=_dnc_out_specs(),
        out_shape=_dnc_shapes(S), compiler_params=_cp(("parallel",)),
    )(qkvn, qkvn, qkvn, proj, prm)


def dnc_bwd(qkvn, proj, prm, cots):
    S = proj.shape[0]

    def body(q_ref, k_ref, v_ref, s_ref, p_ref, du_ref, dw_ref, dqd_ref, dkt_ref, dat_ref, ddc_ref,
             dx_ref, dseg_ref, dprm_ref):
        @pl.when(pl.program_id(0) == 0)
        def _():
            dprm_ref[...] = jnp.zeros_like(dprm_ref)
        _, vjp = jax.vjp(_dnc_f, _heads(q_ref, 2), _heads(k_ref, 2), _heads(v_ref, 1), s_ref[...], p_ref[...])
        dq, dk, dv, dseg, dprm = vjp((_heads(du_ref, 1), _heads(dw_ref, 1), _heads(dqd_ref, 1), _heads(dkt_ref, 1),
                                      dat_ref[0], ddc_ref[0, 0:6, :]))
        dx_ref[...] = jnp.concatenate([dq[0] + dq[1], dq[2] + dq[3], dq[4] + dq[5],
                                       dk[0] + dk[1], dk[2] + dk[3], dk[4] + dk[5]] + [dv[h] for h in range(6)], axis=1)
        dseg_ref[...] = dseg
        dprm_ref[...] += dprm

    return pl.pallas_call(
        body, name="dn_chunk_bwd", grid=(S // CHUNK,), in_specs=_dnc_in_specs() + _dnc_out_specs(),
        out_specs=[pl.BlockSpec((CHUNK, B_QKV), lambda n: (n, 0)), pl.BlockSpec((CHUNK, LANE), lambda n: (n, 0)),
                   pl.BlockSpec((8, LANE), lambda n: (0, 0))],
        out_shape=[jax.ShapeDtypeStruct((S, B_QKV), f32), jax.ShapeDtypeStruct((S, LANE), f32),
                   jax.ShapeDtypeStruct((8, LANE), f32)],
        compiler_params=_cp(("arbitrary",)),
    )(qkvn, qkvn, qkvn, proj, prm, *cots)


def dns_fwd(chunked):
    u = chunked[0]
    S = u.shape[0]
    nc = S // CHUNK

    def body(u_ref, w_ref, qd_ref, kt_ref, at_ref, dc_ref, o_ref, st_ref, st):
        @pl.when(pl.program_id(0) == 0)
        def _():
            st[...] = jnp.zeros_like(st)
        S0 = st[...]
        st_ref[0] = S0
        out, S1 = _dns_f(S0, _heads(u_ref, 1), _heads(w_ref, 1), _heads(qd_ref, 1), _heads(kt_ref, 1),
                         at_ref[0], dc_ref[0, 0:6, :])
        _put_heads(o_ref, out)
        st[...] = S1

    return pl.pallas_call(
        body, name="dn_scan_fwd", grid=(nc,), in_specs=_dnc_out_specs(),
        out_specs=[pl.BlockSpec((CHUNK, B_V), lambda n: (n, 0)),
                   pl.BlockSpec((1, 6, B_DH, B_DH), lambda n: (n, 0, 0, 0))],
        out_shape=[jax.ShapeDtypeStruct((S, B_V), f32), jax.ShapeDtypeStruct((nc, 6, B_DH, B_DH), f32)],
        scratch_shapes=[pltpu.VMEM((6, B_DH, B_DH), f32)],
        compiler_params=_cp(("arbitrary",)),
    )(*chunked)


def dns_bwd(chunked, states, do):
    S = do.shape[0]
    nc = S // CHUNK

    def body(u_ref, w_ref, qd_ref, kt_ref, at_ref, dc_ref, st_ref, do_ref,
             du_ref, dw_ref, dqd_ref, dkt_ref, dat_ref, ddc_ref, dst):
        @pl.when(pl.program_id(0) == 0)
        def _():
            dst[...] = jnp.zeros_like(dst)
        _, vjp = jax.vjp(_dns_f, st_ref[0], _heads(u_ref, 1), _heads(w_ref, 1), _heads(qd_ref, 1), _heads(kt_ref, 1),
                         at_ref[0], dc_ref[0, 0:6, :])
        dS0, du, dw, dqd, dkt, dat, ddc = vjp((_heads(do_ref, 1), dst[...]))
        dst[...] = dS0
        _put_heads(du_ref, du)
        _put_heads(dw_ref, dw)
        _put_heads(dqd_ref, dqd)
        _put_heads(dkt_ref, dkt)
        dat_ref[0] = dat
        ddc_ref[0] = jnp.concatenate([ddc, jnp.zeros((2, LANE), f32)], axis=0)

    return pl.pallas_call(
        body, name="dn_scan_bwd", grid=(nc,),
        in_specs=_dnc_out_specs(nc) + [pl.BlockSpec((1, 6, B_DH, B_DH), lambda n: (nc - 1 - n, 0, 0, 0)),
                                       pl.BlockSpec((CHUNK, B_V), lambda n: (nc - 1 - n, 0))],
        out_specs=_dnc_out_specs(nc), out_shape=_dnc_shapes(S),
        scratch_shapes=[pltpu.VMEM((6, B_DH, B_DH), f32)],
        compiler_params=_cp(("arbitrary",)),
    )(*chunked, states, do)


def dnpost_fwd(o, proj, prm):
    S = o.shape[0]
    t = min(512, S)

    def body(o_ref, z_ref, p_ref, y_ref):
        y_ref[...] = _dnpost_f(o_ref[...], z_ref[...], p_ref[2:3, :])

    tok = pl.BlockSpec((t, B_V), lambda i: (i, 0))
    return pl.pallas_call(
        body, name="dn_post_fwd", grid=(S // t,),
        in_specs=[tok, pl.BlockSpec((t, B_V), lambda i: (i, 2)), pl.BlockSpec((8, LANE), lambda i: (0, 0))],
        out_specs=tok, out_shape=jax.ShapeDtypeStruct((S, B_V), f32), compiler_params=_cp(("parallel",)),
    )(o, proj, prm)


def dnpost_bwd(o, proj, prm, dmix):
    S = o.shape[0]
    t = min(512, S)

    def body(o_ref, z_ref, p_ref, dy_ref, do_ref, dz_ref, dg_ref):
        @pl.when(pl.program_id(0) == 0)
        def _():
            dg_ref[...] = jnp.zeros_like(dg_ref)
        _, vjp = jax.vjp(_dnpost_f, o_ref[...], z_ref[...], p_ref[2:3, :])
        do, dz, dg = vjp(dy_ref[...])
        do_ref[...] = do
        dz_ref[...] = dz
        dg_ref[...] += dg

    tok = pl.BlockSpec((t, B_V), lambda i: (i, 0))
    return pl.pallas_call(
        body, name="dn_post_bwd", grid=(S // t,),
        in_specs=[tok, pl.BlockSpec((t, B_V), lambda i: (i, 2)), pl.BlockSpec((8, LANE), lambda i: (0, 0)), tok],
        out_specs=[tok, tok, pl.BlockSpec((1, LANE), lambda i: (0, 0))],
        out_shape=[jax.ShapeDtypeStruct((S, B_V), f32), jax.ShapeDtypeStruct((S, B_V), f32),
                   jax.ShapeDtypeStruct((1, LANE), f32)],
        compiler_params=_cp(("arbitrary",)),
    )(o, proj, prm, dmix)


N_FF_BLK = D_FF // LANE
GU_SHARD = 2 * D_FF // 4


def _glu_f(gate, up, w, b):
    c = w[2:3] * gate + w[1:2] * shift_down(gate, 1) + w[0:1] * shift_down(gate, 2) + b
    return _silu(c) * up


def glu_fwd(gu, w, b, name):
    S = gu.shape[0]

    def body(g_ref, u_ref, w_ref, b_ref, o_ref):
        o_ref[...] = _glu_f(g_ref[...], u_ref[...], w_ref[...], b_ref[...]).astype(bf16)

    col = pl.BlockSpec((S, LANE), lambda j: (0, j))
    return pl.pallas_call(
        body, name=name, grid=(N_FF_BLK,),
        in_specs=[col, pl.BlockSpec((S, LANE), lambda j: (0, N_FF_BLK + j)), pl.BlockSpec((3, LANE), lambda j: (0, j)),
                  pl.BlockSpec((1, LANE), lambda j: (0, j))],
        out_specs=col, out_shape=jax.ShapeDtypeStruct((S, D_FF), bf16), compiler_params=_cp(("parallel",)),
    )(gu, gu, w, b.reshape(1, D_FF))


def glu_bwd(gu, w, b, dact, name):
    S = gu.shape[0]

    def body(g_ref, u_ref, w_ref, b_ref, d_ref, dg_ref, dw_ref, db_ref):
        _, vjp = jax.vjp(_glu_f, g_ref[...], u_ref[...], w_ref[...], b_ref[...])
        dg, du, dw, db = vjp(d_ref[...])
        dg_ref[0] = dg.astype(bf16)
        dg_ref[1] = du.astype(bf16)
        dw_ref[...] = dw
        db_ref[...] = db

    col = pl.BlockSpec((S, LANE), lambda j: (0, j))
    wsp = pl.BlockSpec((3, LANE), lambda j: (0, j))
    bsp = pl.BlockSpec((1, LANE), lambda j: (0, j))
    return pl.pallas_call(
        body, name=name, grid=(N_FF_BLK,),
        in_specs=[col, pl.BlockSpec((S, LANE), lambda j: (0, N_FF_BLK + j)), wsp, bsp, col],
        out_specs=[pl.BlockSpec((2, S, LANE), lambda j: (0, 0, j)), wsp, bsp],
        out_shape=[jax.ShapeDtypeStruct((2, S, D_FF), bf16), jax.ShapeDtypeStruct((3, D_FF), f32),
                   jax.ShapeDtypeStruct((1, D_FF), f32)],
        compiler_params=_cp(("parallel",)),
    )(gu, gu, w, b.reshape(1, D_FF), dact)


def gu_fwd(n2, wg, name):
    S = n2.shape[0]
    tm = min(512, S)

    def body(a_ref, w_ref, o_ref):
        o_ref[...] = _dg(a_ref[...], w_ref[...], 1, 0)

    return pl.pallas_call(
        body, name=name, grid=(4, S // tm),
        in_specs=[pl.BlockSpec((tm, D), lambda s, m: (m, 0)), pl.BlockSpec((None, D, GU_SHARD), lambda s, m: (s, 0, 0))],
        out_specs=pl.BlockSpec((tm, GU_SHARD), lambda s, m: (m, s)),
        out_shape=jax.ShapeDtypeStruct((S, 2 * D_FF), f32), compiler_params=_cp(("parallel", "parallel")),
    )(n2, wg)


def gu_bwd_x(dgu, wg, name):
    S = dgu.shape[1]
    tm = min(512, S)

    def body(d_ref, w_ref, o_ref):
        @pl.when(pl.program_id(1) == 0)
        def _():
            o_ref[...] = jnp.zeros_like(o_ref)
        o_ref[...] += _dg(d_ref[...], w_ref[...], 1, 1)

    return pl.pallas_call(
        body, name=name, grid=(S // tm, 4),
        in_specs=[pl.BlockSpec((None, tm, GU_SHARD), lambda m, s: (s // 2, m, s % 2)),
                  pl.BlockSpec((None, D, GU_SHARD), lambda m, s: (s, 0, 0))],
        out_specs=pl.BlockSpec((tm, D), lambda m, s: (m, 0)),
        out_shape=jax.ShapeDtypeStruct((S, D), f32), compiler_params=_cp(("parallel", "arbitrary")),
    )(dgu, wg)


def gu_bwd_w(n2, dgu, name):
    S = n2.shape[0]
    tm = min(512, S)
    nm = S // tm

    def body(a_ref, d_ref, o_ref, acc):
        @pl.when(pl.program_id(1) == 0)
        def _():
            acc[...] = jnp.zeros_like(acc)
        acc[...] += _dg(a_ref[...], d_ref[...], 0, 0)

        @pl.when(pl.program_id(1) == nm - 1)
        def _():
            o_ref[...] = acc[...].astype(bf16)

    return pl.pallas_call(
        body, name=name, grid=(4, nm),
        in_specs=[pl.BlockSpec((tm, D), lambda s, m: (m, 0)),
                  pl.BlockSpec((None, tm, GU_SHARD), lambda s, m: (s // 2, m, s % 2))],
        out_specs=pl.BlockSpec((None, D, GU_SHARD), lambda s, m: (s, 0, 0)),
        out_shape=jax.ShapeDtypeStruct((4, D, GU_SHARD), bf16),
        scratch_shapes=[pltpu.VMEM((D, GU_SHARD), f32)],
        compiler_params=_cp(("parallel", "arbitrary")),
    )(n2, dgu)


def _pair_cols(w):
    lead = w.shape[:-1]
    return w.reshape(lead + (2, 6, A_DH)).swapaxes(-3, -2).reshape(lead + (A_Q,))


def _unpair_cols(w):
    lead = w.shape[:-1]
    return w.reshape(lead + (6, 2, A_DH)).swapaxes(-3, -2).reshape(lead + (A_Q,))


def _lay_in_a(w):
    return jnp.concatenate([_pair_cols(w[:, :A_Q]), w[:, A_Q:]], axis=1)


def _unlay_in_a(w):
    return jnp.concatenate([_unpair_cols(w[:, :A_Q]), w[:, A_Q:]], axis=1)


def _lay_out_a(w):
    return jnp.concatenate([_pair_cols(w[:A_Q].T).T, w[A_Q:]], axis=0)


def _unlay_out_a(w):
    return jnp.concatenate([_unpair_cols(w[:A_Q].T).T, w[A_Q:]], axis=0)


def _lay_in_b(w):
    return jnp.concatenate([w[:, :2304], w[:, 2316:], w[:, 2304:2316],
                            jnp.zeros((w.shape[0], LANE - 12), w.dtype)], axis=1)


def _unlay_in_b(w):
    return jnp.concatenate([w[:, :2304], w[:, 2560:2572], w[:, 2304:2560]], axis=1)


def _chip_cols(w):
    return jnp.moveaxis(w.reshape(w.shape[0], 4, w.shape[1] // 4), 1, 0)


def _unchip_cols(w):
    return jnp.moveaxis(w, 0, 1).reshape(w.shape[1], 4 * w.shape[2])


def _local_step(x, mem, target, P):
    sk = jnp.zeros((16, LANE), f32).at[:A_HEADS].set(jnp.broadcast_to(P["sinks"][:, None], (A_HEADS, LANE)))
    prm = jnp.zeros((8, LANE), f32).at[0, 6:12].set(P["a_log"]).at[1, 6:12].set(P["dt_bias"]).at[2].set(P["out_norm_g"])
    bias = bias_build(P["rel_bias"])
    saved = []
    h = x
    for i in range(2):
        n1 = rms_fwd(h, P["g_mix"][i], f"rms_mix{i}")
        kv = memkv_fwd(mem, P["g_mem"][i], P["w_mem"][i], f"memkv{i}")
        if i == 0:
            proj = mm_nn(n1, P["w_in_a"], name="proj_a")
            self_out = swa_fwd(proj, bias, sk)
            cross = xattn_fwd(proj, A_Q + 2 * LANE, kv, "xattn_a")
            extra = ()
        else:
            proj = mm_nn(n1, P["w_in_b"], name="proj_b")
            qkvn = dnprep_fwd(proj, P["conv_qkv"])
            chunked = dnc_fwd(qkvn, proj, prm)
            o, states = dns_fwd(chunked)
            self_out = dnpost_fwd(o, proj, prm)
            cross = xattn_fwd(proj, 2304, kv, "xattn_b")
            extra = (qkvn, chunked, states, o)
        mix = jnp.concatenate([self_out, cross], axis=1)
        h2 = mm_nn(mix, P["w_out"][i], res=h, name=f"out_proj{i}")
        n2 = rms_fwd(h2, P["g_ffn"][i], f"rms_ffn{i}")
        gu = gu_fwd(n2, P["w_gu"][i], f"gate_up{i}")
        act = glu_fwd(gu, P["ffn_cw"][i], P["ffn_cb"][i], f"glu{i}")
        h3 = mm_nn(act, P["w_down"][i], res=h2, name=f"down{i}")
        saved.append((h, n1, kv, proj, mix, h2, n2, gu, act, extra))
        h = h3

    loss, dh, dg_fin = loss_head(h, P["g_fin"], target)
    G = {"g_fin": dg_fin[0], "g_mix": [None, None], "g_mem": [None, None], "g_ffn": [None, None],
         "w_mem": [None, None], "w_out": [None, None], "w_gu": [None, None], "w_down": [None, None],
         "ffn_cw": [None, None], "ffn_cb": [None, None]}
    for i in (1, 0):
        hin, n1, kv, proj, mix, h2, n2, gu, act, extra = saved[i]
        dact = mm_nt(dh, P["w_down"][i], name=f"d_act{i}")
        G["w_down"][i] = mm_tn(act, dh, name=f"dw_down{i}")
        dgu, dcw, dcb = glu_bwd(gu, P["ffn_cw"][i], P["ffn_cb"][i], dact, f"glu_bwd{i}")
        G["ffn_cw"][i], G["ffn_cb"][i] = dcw, dcb[0]
        dn2 = gu_bwd_x(dgu, P["w_gu"][i], f"d_n2_{i}")
        G["w_gu"][i] = gu_bwd_w(n2, dgu, f"dw_gu{i}")
        dh2, dg = rms_bwd(h2, P["g_ffn"][i], dn2, dh, f"rms_ffn_bwd{i}")
        G["g_ffn"][i] = dg[0]
        dmix = mm_nt(dh2, P["w_out"][i], name=f"d_mix{i}")
        G["w_out"][i] = mm_tn(mix, dh2, name=f"dw_out{i}")
        if i == 0:
            dqkv, dbias, dsk = swa_bwd(proj, bias, sk, dmix)
            dxq, dkv = xattn_bwd(proj, A_Q + 2 * LANE, kv, dmix, "xattn_a_bwd")
            dproj = jnp.concatenate([dqkv, dxq], axis=1)
            G["sinks"] = dsk[:A_HEADS, 0]
            G["rel_bias"] = bias_grad(dbias)[:, :A_HEADS]
            w_in, gname = P["w_in_a"], "w_in_a"
        else:
            qkvn, chunked, states, o = extra
            do, dz, dgo = dnpost_bwd(o, proj, prm, dmix)
            dqkvn, dseg, dprm = dnc_bwd(qkvn, proj, prm, dns_bwd(chunked, states, do))
            draw, dconv = dnprep_bwd(proj, P["conv_qkv"], dqkvn)
            dxq, dkv = xattn_bwd(proj, 2304, kv, dmix, "xattn_b_bwd")
            dproj = jnp.concatenate([draw, dz, dxq, dseg], axis=1)
            G["conv_qkv"] = dconv
            G["a_log"], G["dt_bias"], G["out_norm_g"] = dprm[0, 6:12], dprm[1, 6:12], dgo[0]
            w_in, gname = P["w_in_b"], "w_in_b"
        dn1 = mm_nt(dproj, w_in, name=f"d_n1_{i}")
        G[gname] = mm_tn(n1, dproj, name=f"d{gname}")
        dh, dg = rms_bwd(hin, P["g_mix"][i], dn1, dh2, f"rms_mix_bwd{i}")
        G["g_mix"][i] = dg[0]
        dgm, dwm = memkv_bwd(mem, P["g_mem"][i], P["w_mem"][i], dkv, f"memkv_bwd{i}")
        G["g_mem"][i], G["w_mem"][i] = dgm[0], dwm
    return loss, dh, G


def _prepare(full, w_gu=None):
    return {
        "rel_bias": full["rel_bias"], "sinks": full["sinks_a"][0], "a_log": full["a_log_b"][0],
        "dt_bias": full["dt_bias_b"][0], "out_norm_g": full["out_norm_g_b"][0],
        "g_mix": full["norm_mix_g"], "g_mem": full["norm_mem_g"], "g_ffn": full["norm_ffn_g"],
        "g_fin": full["final_norm_g"], "conv_qkv": full["conv_qkv_b"][0],
        "ffn_cw": [full["ffn_conv_w"][0], full["ffn_conv_w"][1]],
        "ffn_cb": [full["ffn_conv_b"][0], full["ffn_conv_b"][1]],
        "w_mem": [full["w_mem_kv"][0], full["w_mem_kv"][1]],
        "w_out": [_lay_out_a(full["w_out"][0]), full["w_out"][1]],
        "w_in_a": _lay_in_a(full["w_in_a"][0]), "w_in_b": _lay_in_b(full["w_in_b"][0]),
        "w_gu": w_gu if w_gu is not None else [_chip_cols(full["w_gate_up"][0]), _chip_cols(full["w_gate_up"][1])],
        "w_down": [full["w_down"][0], full["w_down"][1]],
    }


def _grads_to_ref(G):
    return {
        "rel_bias": G["rel_bias"], "norm_mix_g": jnp.stack(G["g_mix"]), "norm_mem_g": jnp.stack(G["g_mem"]),
        "w_mem_kv": jnp.stack(G["w_mem"]),
        "w_out": jnp.stack([_unlay_out_a(G["w_out"][0]), G["w_out"][1]]),
        "w_in_a": _unlay_in_a(G["w_in_a"])[None], "sinks_a": G["sinks"][None],
        "w_in_b": _unlay_in_b(G["w_in_b"])[None], "conv_qkv_b": G["conv_qkv"][None],
        "a_log_b": G["a_log"][None], "dt_bias_b": G["dt_bias"][None], "out_norm_g_b": G["out_norm_g"][None],
        "norm_ffn_g": jnp.stack(G["g_ffn"]),
        "w_gate_up": jnp.stack([_unchip_cols(G["w_gu"][0]), _unchip_cols(G["w_gu"][1])]).astype(f32),
        "ffn_conv_w": jnp.stack(G["ffn_cw"]), "ffn_conv_b": jnp.stack(G["ffn_cb"]),
        "w_down": jnp.stack(G["w_down"]), "final_norm_g": G["g_fin"],
    }


ANY = pl.BlockSpec(memory_space=pl.ANY)


def _place():
    return lax.axis_index("x"), lax.axis_index("y"), lax.axis_index("c")


def chip_scatter(gs):
    n = len(gs)

    def body(*refs):
        ins, outs = refs[:n], refs[n:2 * n]
        ssem, rsem = refs[2 * n:]
        x, y, c = _place()
        me = 2 * x + y
        peers = [(1 - x, y), (x, 1 - y), (1 - x, 1 - y)]

        def remote(j, k, slot):
            px, py = peers[k]
            return pltpu.make_async_remote_copy(
                src_ref=ins[j].at[2 * px + py], dst_ref=outs[j].at[slot],
                send_sem=ssem.at[3 * j + k], recv_sem=rsem.at[3 * j + k],
                device_id=(px, py, c), device_id_type=MESH)

        sends = [remote(j, k, me) for j in range(n) for k in range(3)]
        for cp in sends:
            cp.start()
        for j in range(n):
            for k in range(3):
                px, py = peers[k]
                remote(j, k, 2 * px + py).wait_recv()
        for cp in sends:
            cp.wait_send()

    return pl.pallas_call(
        body, name="grad_scatter", in_specs=[ANY] * n, out_specs=[ANY] * n,
        out_shape=[jax.ShapeDtypeStruct(g.shape, g.dtype) for g in gs],
        scratch_shapes=[pltpu.SemaphoreType.DMA((3 * n,)), pltpu.SemaphoreType.DMA((3 * n,))],
    )(*gs)


def allreduce_small(buf):
    R = buf.shape[0]

    def body(b_ref, o_ref, recv, ssem, rsem):
        x, y, c = _place()
        me = 4 * x + 2 * y + c

        def peer(k):
            return (1 - x if k & 4 else x, 1 - y if k & 2 else y, 1 - c if k & 1 else c)

        def remote(k, slot):
            return pltpu.make_async_remote_copy(
                src_ref=b_ref, dst_ref=recv.at[slot], send_sem=ssem.at[k - 1], recv_sem=rsem.at[k - 1],
                device_id=peer(k), device_id_type=MESH)

        sends = [remote(k, me) for k in range(1, 8)]
        for cp in sends:
            cp.start()
        recv[me] = b_ref[...]
        for k in range(1, 8):
            px, py, pc = peer(k)
            remote(k, 4 * px + 2 * py + pc).wait_recv()
        for cp in sends:
            cp.wait_send()
        total = recv[0]
        for j in range(1, 8):
            total = total + recv[j]
        o_ref[...] = total

    return pl.pallas_call(
        body, name="small_allreduce",
        in_specs=[pl.BlockSpec(memory_space=pltpu.VMEM)], out_specs=pl.BlockSpec(memory_space=pltpu.VMEM),
        out_shape=jax.ShapeDtypeStruct(buf.shape, f32),
        scratch_shapes=[pltpu.VMEM((8, R, LANE), f32), pltpu.SemaphoreType.DMA((7,)), pltpu.SemaphoreType.DMA((7,))],
    )(buf)


def sum_slots(own, recv, chip, core, name):
    _, R, C = recv.shape
    tr = _row_tile(R)
    nt = R // tr

    def body(p_ref, a_ref, r_ref, o_ref):
        acc = jnp.zeros((tr, C), f32)
        for s in range(4):
            acc = acc + jnp.where(p_ref[0] == s, a_ref[s], r_ref[s]).astype(f32)
        o_ref[...] = acc

    slots = pl.BlockSpec((4, tr, C), lambda i, p_ref: (0, i, 0))
    return pl.pallas_call(
        body, name=name, out_shape=jax.ShapeDtypeStruct((2 * R, C), f32),
        grid_spec=pltpu.PrefetchScalarGridSpec(
            num_scalar_prefetch=1, grid=(nt,), in_specs=[slots, slots],
            out_specs=pl.BlockSpec((tr, C), lambda i, p_ref: (p_ref[1] * nt + i, 0))),
        compiler_params=_cp(("parallel",)),
    )(jnp.stack([chip, core]).astype(jnp.int32), own, recv)


def _half(ref, core, axis=0):
    half = ref.shape[axis] // 2
    idx = (slice(None),) * axis + (pl.ds(core * half, half),)
    return ref.at[idx]


def gather_weights(bufs, smalls):
    nb, n = len(bufs), len(bufs) + len(smalls)

    def body(*refs):
        outs = refs[n:2 * n]
        isend, irecv, dsend, drecv = refs[2 * n:]
        x, y, c = _place()
        me = 2 * x + y
        peers = [(1 - x, y), (x, 1 - y), (1 - x, 1 - y)]
        chips = [2 * px + py for px, py in peers]

        def rows(ref, j, core):
            return _half(ref, core) if j < nb else ref

        def ici(j, k, slot):
            px, py = peers[k]
            return pltpu.make_async_remote_copy(
                src_ref=rows(outs[j].at[me], j, c), dst_ref=rows(outs[j].at[slot], j, c),
                send_sem=isend.at[3 * j + k], recv_sem=irecv.at[3 * j + k],
                device_id=(px, py, c), device_id_type=MESH)

        def d2d(j, k, core):
            blk = rows(outs[j].at[chips[k]], j, core)
            return pltpu.make_async_remote_copy(
                src_ref=blk, dst_ref=blk, send_sem=dsend.at[3 * j + k], recv_sem=drecv.at[3 * j + k],
                device_id=(x, y, 1 - c), device_id_type=MESH)

        sends = [ici(j, k, me) for j in range(n) for k in range(3)]
        for cp in sends:
            cp.start()
        for j in range(n):
            for k in range(3):
                ici(j, k, chips[k]).wait_recv()
                if j < nb:
                    fw = d2d(j, k, c)
                    fw.start()
                    sends.append(fw)
        for j in range(nb):
            for k in range(3):
                d2d(j, k, 1 - c).wait_recv()
        for cp in sends:
            cp.wait_send()

    arrs = list(bufs) + list(smalls)
    return pl.pallas_call(
        body, name="weight_allgather", in_specs=[ANY] * n, out_specs=[ANY] * n,
        out_shape=[jax.ShapeDtypeStruct(a.shape, a.dtype) for a in arrs],
        input_output_aliases={j: j for j in range(n)},
        scratch_shapes=[pltpu.SemaphoreType.DMA((3 * n,)), pltpu.SemaphoreType.DMA((3 * n,)),
                        pltpu.SemaphoreType.DMA((3 * nb,)), pltpu.SemaphoreType.DMA((3 * nb,))],
    )(*arrs)


def pair_exchange(gbufs):
    n = len(gbufs)

    def body(*refs):
        ins, outs = refs[:n], refs[n:2 * n]
        ssem, rsem = refs[2 * n:]
        x, y, c = _place()
        cps = [pltpu.make_async_remote_copy(
            src_ref=_half(ins[j], 1 - c, axis=1), dst_ref=outs[j], send_sem=ssem.at[j], recv_sem=rsem.at[j],
            device_id=(x, y, 1 - c), device_id_type=MESH) for j in range(n)]
        for cp in cps:
            cp.start()
        for cp in cps:
            cp.wait()

    return pl.pallas_call(
        body, name="pair_exchange", in_specs=[ANY] * n, out_specs=[ANY] * n,
        out_shape=[jax.ShapeDtypeStruct((4, g.shape[1] // 2, g.shape[2]), g.dtype) for g in gbufs],
        scratch_shapes=[pltpu.SemaphoreType.DMA((n,)), pltpu.SemaphoreType.DMA((n,))],
    )(*gbufs)


def _row_tile(rows):
    t = 256
    while rows % t:
        t //= 2
    return t


def pair_sum(mine, theirs, core, name):
    _, R, C = mine.shape
    half = R // 2
    tr = _row_tile(half)
    nt = half // tr

    def body(c_ref, a_ref, b_ref, o_ref):
        o_ref[...] = (a_ref[...].astype(f32) + b_ref[...].astype(f32)).astype(bf16)

    return pl.pallas_call(
        body, name=name, out_shape=jax.ShapeDtypeStruct(theirs.shape, bf16),
        grid_spec=pltpu.PrefetchScalarGridSpec(
            num_scalar_prefetch=1, grid=(4, nt),
            in_specs=[pl.BlockSpec((None, tr, C), lambda s, i, c_ref: (s, c_ref[0] * nt + i, 0)),
                      pl.BlockSpec((None, tr, C), lambda s, i, c_ref: (s, i, 0))],
            out_specs=pl.BlockSpec((None, tr, C), lambda s, i, c_ref: (s, i, 0))),
        compiler_params=_cp(("parallel", "parallel")),
    )(jnp.reshape(core, (1,)).astype(jnp.int32), mine, theirs)


def final_exchange(fins):
    n = len(fins)

    def body(*refs):
        outs = refs[n:2 * n]
        ssem, rsem = refs[2 * n:]
        x, y, c = _place()
        cps = [pltpu.make_async_remote_copy(
            src_ref=_half(outs[j], c), dst_ref=_half(outs[j], c), send_sem=ssem.at[j], recv_sem=rsem.at[j],
            device_id=(x, y, 1 - c), device_id_type=MESH) for j in range(n)]
        for cp in cps:
            cp.start()
        for cp in cps:
            cp.wait()

    return pl.pallas_call(
        body, name="final_exchange", in_specs=[ANY] * n, out_specs=[ANY] * n,
        out_shape=[jax.ShapeDtypeStruct(f.shape, f.dtype) for f in fins],
        input_output_aliases={j: j for j in range(n)},
        scratch_shapes=[pltpu.SemaphoreType.DMA((n,)), pltpu.SemaphoreType.DMA((n,))],
    )(*fins)


def adamw_big(w, m, v, gs, row0, name):
    L, R, C = w.shape
    tr = _row_tile(math.gcd(R, row0) if row0 else R)
    b0 = row0 // tr

    def body(*refs):
        w_ref, m_ref, v_ref = refs[:3]
        g_refs = refs[3:3 + L]
        g_ref, d_ref, nm_ref, nv_ref = refs[3 + L:]
        g = g_refs[0][...]
        for l in range(1, L):
            g = jnp.where(pl.program_id(0) == l, g_refs[l][...], g)
        d, nm, nv = _adamw_math(w_ref[...], g, m_ref[...], v_ref[...])
        g_ref[...] = g
        d_ref[...] = d
        nm_ref[...] = nm
        nv_ref[...] = nv

    own = pl.BlockSpec((None, tr, C), lambda l, i: (l, i, 0))
    off = pl.BlockSpec((tr, C), lambda l, i: (b0 + i, 0))
    return pl.pallas_call(
        body, name=name, grid=(L, R // tr), in_specs=[own, own, own] + [off] * L, out_specs=[own] * 4,
        out_shape=[jax.ShapeDtypeStruct((L, R, C), f32)] * 4, compiler_params=_cp(("parallel", "parallel")),
    )(w, m, v, *gs)


def _adamw_math(w, g, m, v):
    m = B1 * m + (1.0 - B1) * g
    v = B2 * v + (1.0 - B2) * (g * g)
    m_hat = m / (1.0 - B1 ** STEP)
    v_hat = v / (1.0 - B2 ** STEP)
    delta = -LR * (m_hat / (jnp.sqrt(v_hat) + AEPS) + WD * w)
    return delta, m, v


def adamw_small(w, m, v, g):
    def body(w_ref, m_ref, v_ref, g_ref, d_ref, nm_ref, nv_ref):
        d, nm, nv = _adamw_math(w_ref[...], g_ref[...], m_ref[...], v_ref[...])
        d_ref[...] = d
        nm_ref[...] = nm
        nv_ref[...] = nv

    return pl.pallas_call(body, name="adamw_small", out_shape=[jax.ShapeDtypeStruct(w.shape, f32)] * 3)(w, m, v, g)


CONV =(("conv_qkv_b", 2), ("ffn_conv_w", 2))
SMALL = ("rel_bias", "norm_mix_g", "norm_mem_g", "sinks_a", "a_log_b", "dt_bias_b", "out_norm_g_b", "norm_ffn_g",
         "ffn_conv_b", "final_norm_g")
WEIGHTS = ("rel_bias", "norm_mix_g", "norm_mem_g", "w_mem_kv", "w_out", "w_in_a", "sinks_a", "w_in_b", "conv_qkv_b",
           "a_log_b", "dt_bias_b", "out_norm_g_b", "norm_ffn_g", "w_gate_up", "ffn_conv_w", "ffn_conv_b", "w_down",
           "final_norm_g")
ARGS = ("x", "mem") + WEIGHTS + ("loss_target",) + tuple("m_" + n for n in WEIGHTS) + tuple("v_" + n for n in WEIGHTS)


def _rows(a, width):
    flat = a.reshape(-1)
    pad = (-flat.shape[0]) % width
    if pad:
        flat = jnp.concatenate([flat, jnp.zeros((pad,), a.dtype)])
    return flat.reshape(-1, width)


def _nrows(shape, width):
    return -(-math.prod(shape) // width)


def _pack(arrs, width, total_rows, dtype):
    parts = [_rows(a.astype(dtype), width) for a in arrs]
    used = sum(p.shape[0] for p in parts)
    if total_rows > used:
        parts.append(jnp.zeros((total_rows - used, width), dtype))
    return jnp.concatenate(parts, axis=0)


def _unpack(buf, shapes, width):
    out, r = [], 0
    for s in shapes:
        n = _nrows(s, width)
        out.append(buf[r:r + n].reshape(-1)[:math.prod(s)].reshape(s))
        r += n
    return out


def _pad_to(n, mult):
    return -(-n // mult) * mult


def kernel(x, mem, rel_bias, norm_mix_g, norm_mem_g, w_mem_kv, w_out, w_in_a, sinks_a, w_in_b, conv_qkv_b, a_log_b, dt_bias_b, out_norm_g_b, norm_ffn_g, w_gate_up, ffn_conv_w, ffn_conv_b, w_down, final_norm_g, loss_target, m_rel_bias, m_norm_mix_g, m_norm_mem_g, m_w_mem_kv, m_w_out, m_w_in_a, m_sinks_a, m_w_in_b, m_conv_qkv_b, m_a_log_b, m_dt_bias_b, m_out_norm_g_b, m_norm_ffn_g, m_w_gate_up, m_ffn_conv_w, m_ffn_conv_b, m_w_down, m_final_norm_g, v_rel_bias, v_norm_mix_g, v_norm_mem_g, v_w_mem_kv, v_w_out, v_w_in_a, v_sinks_a, v_w_in_b, v_conv_qkv_b, v_a_log_b, v_dt_bias_b, v_out_norm_g_b, v_norm_ffn_g, v_w_gate_up, v_ffn_conv_w, v_ffn_conv_b, v_w_down, v_final_norm_g):
    A = dict(zip(ARGS, (x, mem, rel_bias, norm_mix_g, norm_mem_g, w_mem_kv, w_out, w_in_a, sinks_a, w_in_b, conv_qkv_b, a_log_b, dt_bias_b, out_norm_g_b, norm_ffn_g, w_gate_up, ffn_conv_w, ffn_conv_b, w_down, final_norm_g, loss_target, m_rel_bias, m_norm_mix_g, m_norm_mem_g, m_w_mem_kv, m_w_out, m_w_in_a, m_sinks_a, m_w_in_b, m_conv_qkv_b, m_a_log_b, m_dt_bias_b, m_out_norm_g_b, m_norm_ffn_g, m_w_gate_up, m_ffn_conv_w, m_ffn_conv_b, m_w_down, m_final_norm_g, v_rel_bias, v_norm_mix_g, v_norm_mem_g, v_w_mem_kv, v_w_out, v_w_in_a, v_sinks_a, v_w_in_b, v_conv_qkv_b, v_a_log_b, v_dt_bias_b, v_out_norm_g_b, v_norm_ffn_g, v_w_gate_up, v_ffn_conv_w, v_ffn_conv_b, v_w_down, v_final_norm_g)))
    chip = 2 * lax.axis_index("x") + lax.axis_index("y")
    core = lax.axis_index("c")
    n_down, n_out, n_mem = w_down.shape[1], w_out.shape[1], w_mem_kv.shape[1]

    shards = [w_gate_up[0], w_gate_up[1],
              jnp.concatenate([w_down.reshape(2 * n_down, D), w_out.reshape(2 * n_out, D)], axis=0),
              w_mem_kv.reshape(2 * n_mem, 2 * X_Q), w_in_a[0], w_in_b[0]]
    def own_slot(shard):
        return lax.dynamic_update_index_in_dim(lax.empty((4,) + shard.shape, shard.dtype), shard, chip, 0)

    got = gather_weights([own_slot(s.astype(bf16)) for s in shards],
                         [own_slot(conv_qkv_b[0]), own_slot(ffn_conv_w.reshape(6, -1))])
    g_gu0, g_gu1, g_rows, g_mem, g_ina, g_inb, g_cq, g_cf = got

    def layers(slab, n):
        return jnp.moveaxis(slab.reshape(4, 2, n, slab.shape[-1]), 0, 1).reshape(2, 4 * n, slab.shape[-1])

    full = {n: A[n] for n in SMALL}
    full["w_down"] = layers(g_rows[:, :2 * n_down], n_down)
    full["w_out"] = layers(g_rows[:, 2 * n_down:], n_out)
    full["w_mem_kv"] = layers(g_mem, n_mem)
    full["w_in_a"] = _unchip_cols(g_ina)[None]
    full["w_in_b"] = _unchip_cols(g_inb)[None]
    full["conv_qkv_b"] = _unchip_cols(g_cq)[None]
    full["ffn_conv_w"] = _unchip_cols(g_cf).reshape(2, 3, D_FF)

    loss, dx, G = _local_step(x[0], mem[0], loss_target[0], _prepare(full, w_gu=[g_gu0, g_gu1]))
    gfull = _grads_to_ref(G)

    def chip_rows(g, n):
        return g.reshape(4, n, g.shape[-1])

    partial = [G["w_gu"][0], G["w_gu"][1],
               jnp.concatenate([chip_rows(G["w_down"][0], n_down), chip_rows(G["w_down"][1], n_down),
                                chip_rows(_unlay_out_a(G["w_out"][0]), n_out), chip_rows(G["w_out"][1], n_out)],
                               axis=1).astype(bf16),
               jnp.concatenate([chip_rows(G["w_mem"][0], n_mem), chip_rows(G["w_mem"][1], n_mem)], axis=1).astype(bf16),
               _chip_cols(_unlay_in_a(G["w_in_a"])).astype(bf16), _chip_cols(_unlay_in_b(G["w_in_b"])).astype(bf16)]
    theirs = pair_exchange(partial)
    names = ("gu0", "gu1", "rows", "mem", "in_a", "in_b")
    pair = [pair_sum(p, t, core, "pair_sum_" + nm) for p, t, nm in zip(partial, theirs, names)]
    arrived = chip_scatter(pair)
    halves = [sum_slots(p, r, chip, core, "sum_slots_" + nm) for p, r, nm in zip(pair, arrived, names)]
    f_gu0, f_gu1, f_rows, f_mem, f_ina, f_inb = final_exchange(halves)

    sm_shapes = [A[n].shape for n in SMALL] + [gfull[n].shape for n, _ in CONV] + [(LANE,)]
    sm_rows = _pad_to(sum(_nrows(s, LANE) for s in sm_shapes), 8)
    sbuf = _pack([gfull[n] for n in SMALL] + [gfull[n] for n, _ in CONV] + [loss[0]], LANE, sm_rows, f32)
    tot = _unpack(allreduce_small(sbuf), sm_shapes, LANE)
    gsmall = dict(zip(SMALL, tot[:len(SMALL)]))
    for (n, axis), t in zip(CONV, tot[len(SMALL):len(SMALL) + len(CONV)]):
        sh = A[n].shape[axis]
        gsmall[n] = lax.dynamic_slice_in_dim(t, chip * sh, sh, axis)
    loss_out = tot[-1][0]

    out = {}
    plan = (("w_gate_up", (2, D, GU_SHARD), [f_gu0, f_gu1], 0), ("w_down", (1, 2 * n_down, D), [f_rows], 0),
            ("w_out", (1, 2 * n_out, D), [f_rows], 2 * n_down), ("w_mem_kv", (1, 2 * n_mem, 2 * X_Q), [f_mem], 0),
            ("w_in_a", (1, D, IN_A // 4), [f_ina], 0), ("w_in_b", (1, D, IN_B // 4), [f_inb], 0))
    for n, shape3, gs, row0 in plan:
        res = adamw_big(A[n].reshape(shape3), A["m_" + n].reshape(shape3), A["v_" + n].reshape(shape3), gs, row0,
                        "adamw_" + n)
        for key, r in zip(("grad_", "delta_", "new_m_", "new_v_"), res):
            out[key + n] = r.reshape(A[n].shape)
    names = SMALL + tuple(n for n, _ in CONV)
    shapes = [A[n].shape for n in names]
    rows = _pad_to(sum(_nrows(s, LANE) for s in shapes), 8)
    packs = [_pack([src[n] for n in names], LANE, rows, f32)
             for src in ({n: A[n] for n in names}, {n: A["m_" + n] for n in names}, {n: A["v_" + n] for n in names}, gsmall)]
    res = adamw_small(*packs)
    for key, r in zip(("delta_", "new_m_", "new_v_"), res):
        for n, a in zip(names, _unpack(r, shapes, LANE)):
            out[key + n] = a
    for n in names:
        out["grad_" + n] = gsmall[n]
    return (loss_out, dx[None], *[out["grad_" + n] for n in WEIGHTS], *[out["delta_" + n] for n in WEIGHTS],
            *[out["new_m_" + n] for n in WEIGHTS], *[out["new_v_" + n] for n in WEIGHTS])
```

```python
import functools
import math

import numpy as np
import jax
import jax.numpy as jnp
from jax import lax
from jax.experimental import pallas as pl
from jax.experimental.pallas import tpu as pltpu

f32 = jnp.float32
bf16 = jnp.bfloat16
HI = lax.Precision.HIGHEST
MESH = pl.DeviceIdType.MESH

D = 1024
MEM_LEN = 256
EPS = 1e-6
A_HEADS, A_KV, A_DH = 12, 2, 64
A_Q = 768
BLK = 128
N_BUCKETS, MAX_DIST = 32, 128
B_QK, B_V, B_DH = 384, 768, 128
B_QKV = 1536
CHUNK = 64
X_Q = 256
D_FF = 2816
IN_A = 1280
IN_B = 2572
IN_B_PAD = 2688
LANE = 128
VMEM_LIMIT = 56 * 1024 * 1024

LR, B1, B2, AEPS, WD, STEP = 0.001, 0.9, 0.999, 1e-08, 0.01, 10


def _cp(sem=None):
    return pltpu.CompilerParams(dimension_semantics=sem, vmem_limit_bytes=VMEM_LIMIT)


def _dg(a, b, ca, cb, prec=None):
    return lax.dot_general(a, b, (((ca,), (cb,)), ((), ())), precision=prec, preferred_element_type=f32)


@jax.custom_vjp
def bdot(a, b):
    return _dg(a.astype(bf16), b.astype(bf16), 1, 0)


def _bdot_f(a, b):
    return bdot(a, b), (a, b)


def _bdot_b(res, g):
    a, b = res
    gb = g.astype(bf16)
    return _dg(gb, b.astype(bf16), 1, 1), _dg(a.astype(bf16), gb, 0, 0)


bdot.defvjp(_bdot_f, _bdot_b)


@jax.custom_vjp
def bdot_nt(a, b):
    return _dg(a.astype(bf16), b.astype(bf16), 1, 1)


def _bdot_nt_f(a, b):
    return bdot_nt(a, b), (a, b)


def _bdot_nt_b(res, g):
    a, b = res
    gb = g.astype(bf16)
    return _dg(gb, b.astype(bf16), 1, 0), _dg(gb, a.astype(bf16), 0, 0)


bdot_nt.defvjp(_bdot_nt_f, _bdot_nt_b)


def _shift_rows(x, s, down):
    n = x.shape[0]
    row = lax.broadcasted_iota(jnp.int32, x.shape, 0)
    if down:
        return jnp.where(row >= s, pltpu.roll(x, s, 0), 0.0)
    return jnp.where(row < n - s, pltpu.roll(x, n - s, 0), 0.0)


@functools.partial(jax.custom_vjp, nondiff_argnums=(1,))
def shift_down(x, s):
    return _shift_rows(x, s, True)


def _sd_f(x, s):
    return _shift_rows(x, s, True), None


def _sd_b(s, _, g):
    return (_shift_rows(g, s, False),)


shift_down.defvjp(_sd_f, _sd_b)


def _sigmoid(x):
    return 1.0 / (1.0 + jnp.exp(-x))


def _silu(x):
    return x * _sigmoid(x)


def _rms(x, g):
    return x * lax.rsqrt(jnp.mean(x * x, axis=-1, keepdims=True) + EPS) * g


def _tile(n, cap):
    u = n // LANE
    best = 1
    for d in range(1, u + 1):
        if u % d == 0 and d * LANE <= cap:
            best = d
    return best * LANE


def mm_nn(a, w, res=None, out_dtype=f32, name="mm_nn"):
    M, K = a.shape
    N = w.shape[1]
    tm, tn = min(512, M), _tile(N, 1024)

    def body(*refs):
        if res is None:
            a_ref, w_ref, o_ref = refs
            o_ref[...] = _dg(a_ref[...].astype(bf16), w_ref[...], 1, 0).astype(out_dtype)
        else:
            a_ref, w_ref, r_ref, o_ref = refs
            o_ref[...] = (r_ref[...] + _dg(a_ref[...].astype(bf16), w_ref[...], 1, 0)).astype(out_dtype)

    in_specs = [pl.BlockSpec((tm, K), lambda n, m: (m, 0)), pl.BlockSpec((K, tn), lambda n, m: (0, n))]
    args = [a, w]
    if res is not None:
        in_specs.append(pl.BlockSpec((tm, tn), lambda n, m: (m, n)))
        args.append(res)
    return pl.pallas_call(
        body, name=name, grid=(N // tn, M // tm), in_specs=in_specs,
        out_specs=pl.BlockSpec((tm, tn), lambda n, m: (m, n)),
        out_shape=jax.ShapeDtypeStruct((M, N), out_dtype),
        compiler_params=_cp(("parallel", "parallel")),
    )(*args)


def mm_nt(dy, w, name="mm_nt"):
    M, N = dy.shape
    K = w.shape[0]
    tm, tn = min(512, M), _tile(N, 1024)

    def body(dy_ref, w_ref, o_ref):
        @pl.when(pl.program_id(1) == 0)
        def _():
            o_ref[...] = jnp.zeros_like(o_ref)
        o_ref[...] += _dg(dy_ref[...].astype(bf16), w_ref[...], 1, 1)

    return pl.pallas_call(
        body, name=name, grid=(M // tm, N // tn),
        in_specs=[pl.BlockSpec((tm, tn), lambda m, n: (m, n)), pl.BlockSpec((K, tn), lambda m, n: (0, n))],
        out_specs=pl.BlockSpec((tm, K), lambda m, n: (m, 0)),
        out_shape=jax.ShapeDtypeStruct((M, K), f32),
        compiler_params=_cp(("parallel", "arbitrary")),
    )(dy, w)


def mm_tn(a, dy, name="mm_tn"):
    M, K = a.shape
    N = dy.shape[1]
    tm, tk, tn = min(512, M), _tile(K, 1408), _tile(N, 1024)

    def body(a_ref, dy_ref, o_ref):
        @pl.when(pl.program_id(2) == 0)
        def _():
            o_ref[...] = jnp.zeros_like(o_ref)
        o_ref[...] += _dg(a_ref[...].astype(bf16), dy_ref[...].astype(bf16), 0, 0)

    return pl.pallas_call(
        body, name=name, grid=(K // tk, N // tn, M // tm),
        in_specs=[pl.BlockSpec((tm, tk), lambda k, n, m: (m, k)), pl.BlockSpec((tm, tn), lambda k, n, m: (m, n))],
        out_specs=pl.BlockSpec((tk, tn), lambda k, n, m: (k, n)),
        out_shape=jax.ShapeDtypeStruct((K, N), f32),
        compiler_params=_cp(("parallel", "parallel", "arbitrary")),
    )(a, dy)


def rms_fwd(h, g, name):
    S = h.shape[0]
    t = min(512, S)

    def body(h_ref, g_ref, o_ref):
        o_ref[...] = _rms(h_ref[...], g_ref[...]).astype(bf16)

    return pl.pallas_call(
        body, name=name, grid=(S // t,),
        in_specs=[pl.BlockSpec((t, D), lambda i: (i, 0)), pl.BlockSpec((1, D), lambda i: (0, 0))],
        out_specs=pl.BlockSpec((t, D), lambda i: (i, 0)),
        out_shape=jax.ShapeDtypeStruct((S, D), bf16),
        compiler_params=_cp(("parallel",)),
    )(h, g.reshape(1, D))


def rms_bwd(h, g, dn, dres, name):
    S = h.shape[0]
    t = min(512, S)

    def body(h_ref, g_ref, dn_ref, dr_ref, dh_ref, dg_ref):
        @pl.when(pl.program_id(0) == 0)
        def _():
            dg_ref[...] = jnp.zeros_like(dg_ref)
        _, vjp = jax.vjp(_rms, h_ref[...], g_ref[...])
        dh, dg = vjp(dn_ref[...])
        dh_ref[...] = dr_ref[...] + dh
        dg_ref[...] += dg

    tok = pl.BlockSpec((t, D), lambda i: (i, 0))
    vec = pl.BlockSpec((1, D), lambda i: (0, 0))
    return pl.pallas_call(
        body, name=name, grid=(S // t,), in_specs=[tok, vec, tok, tok], out_specs=[tok, vec],
        out_shape=[jax.ShapeDtypeStruct((S, D), f32), jax.ShapeDtypeStruct((1, D), f32)],
        compiler_params=_cp(("arbitrary",)),
    )(h, g.reshape(1, D), dn, dres)


def loss_head(h, g, target):
    S = h.shape[0]
    t = min(512, S)

    def f(hh, gg, tt):
        err = _rms(hh, gg) - tt
        return 0.5 * jnp.sum(jnp.mean(err * err, axis=-1, keepdims=True), axis=0, keepdims=True)

    def body(h_ref, g_ref, t_ref, loss_ref, dh_ref, dg_ref):
        @pl.when(pl.program_id(0) == 0)
        def _():
            dg_ref[...] = jnp.zeros_like(dg_ref)
            loss_ref[...] = jnp.zeros_like(loss_ref)
        val, vjp = jax.vjp(lambda a, b: f(a, b, t_ref[...]), h_ref[...], g_ref[...])
        dh, dg = vjp(jnp.ones((1, 1), f32))
        dh_ref[...] = dh
        dg_ref[...] += dg
        loss_ref[...] += jnp.broadcast_to(val, loss_ref.shape)

    tok = pl.BlockSpec((t, D), lambda i: (i, 0))
    vec = pl.BlockSpec((1, D), lambda i: (0, 0))
    return pl.pallas_call(
        body, name="loss_head", grid=(S // t,), in_specs=[tok, vec, tok],
        out_specs=[pl.BlockSpec((1, LANE), lambda i: (0, 0)), tok, vec],
        out_shape=[jax.ShapeDtypeStruct((1, LANE), f32), jax.ShapeDtypeStruct((S, D), f32),
                   jax.ShapeDtypeStruct((1, D), f32)],
        compiler_params=_cp(("arbitrary",)),
    )(h, g.reshape(1, D), target)


def memkv_fwd(mem, g, w, name):
    def body(m_ref, g_ref, w_ref, o_ref):
        o_ref[...] = _dg(_rms(m_ref[...], g_ref[...]).astype(bf16), w_ref[...], 1, 0)

    return pl.pallas_call(
        body, name=name, out_shape=jax.ShapeDtypeStruct((MEM_LEN, 2 * X_Q), f32), compiler_params=_cp(),
    )(mem, g.reshape(1, D), w)


def memkv_bwd(mem, g, w, dkv, name):
    def body(m_ref, g_ref, w_ref, d_ref, dg_ref, dw_ref):
        n, vjp = jax.vjp(lambda gg: _rms(m_ref[...], gg), g_ref[...])
        db = d_ref[...].astype(bf16)
        dw_ref[...] = _dg(n.astype(bf16), db, 0, 0)
        dg_ref[...] = vjp(_dg(db, w_ref[...], 1, 1))[0]

    return pl.pallas_call(
        body, name=name,
        out_shape=[jax.ShapeDtypeStruct((1, D), f32), jax.ShapeDtypeStruct((D, 2 * X_Q), f32)],
        compiler_params=_cp(),
    )(mem, g.reshape(1, D), w, dkv)


def _xattn_f(xq, mk, mv):
    lane = lax.broadcasted_iota(jnp.int32, (1, X_Q), 1)
    out = jnp.zeros(xq.shape, f32)
    for hd in range(4):
        msk = (lane // 64 == hd).astype(f32)
        s = bdot_nt(xq * msk, mk) * (64 ** -0.5)
        m = lax.stop_gradient(jnp.max(s, axis=-1, keepdims=True))
        p = jnp.exp(s - m)
        p = p / jnp.sum(p, axis=-1, keepdims=True)
        out = out + bdot(p, mv * msk)
    return out


def xattn_fwd(proj, col, kv, name):
    S = proj.shape[0]
    t = min(512, S)
    cb = col // X_Q

    def body(q_ref, k_ref, v_ref, o_ref):
        o_ref[...] = _xattn_f(q_ref[...], k_ref[...], v_ref[...])

    return pl.pallas_call(
        body, name=name, grid=(S // t,),
        in_specs=[pl.BlockSpec((t, X_Q), lambda i: (i, cb)), pl.BlockSpec((MEM_LEN, X_Q), lambda i: (0, 0)),
                  pl.BlockSpec((MEM_LEN, X_Q), lambda i: (0, 1))],
        out_specs=pl.BlockSpec((t, X_Q), lambda i: (i, 0)),
        out_shape=jax.ShapeDtypeStruct((S, X_Q), f32),
        compiler_params=_cp(("parallel",)),
    )(proj, kv, kv)


def xattn_bwd(proj, col, kv, dmix, name):
    S = proj.shape[0]
    t = min(512, S)
    cb = col // X_Q

    def body(q_ref, k_ref, v_ref, do_ref, dq_ref, dk_ref, dv_ref):
        @pl.when(pl.program_id(0) == 0)
        def _():
            dk_ref[...] = jnp.zeros_like(dk_ref)
            dv_ref[...] = jnp.zeros_like(dv_ref)
        _, vjp = jax.vjp(_xattn_f, q_ref[...], k_ref[...], v_ref[...])
        dq, dk, dv = vjp(do_ref[...])
        dq_ref[...] = dq
        dk_ref[...] += dk
        dv_ref[...] += dv

    kvb = pl.BlockSpec((MEM_LEN, X_Q), lambda i: (0, 0))
    dq, dk, dv = pl.pallas_call(
        body, name=name, grid=(S // t,),
        in_specs=[pl.BlockSpec((t, X_Q), lambda i: (i, cb)), kvb,
                  pl.BlockSpec((MEM_LEN, X_Q), lambda i: (0, 1)), pl.BlockSpec((t, X_Q), lambda i: (i, 3))],
        out_specs=[pl.BlockSpec((t, X_Q), lambda i: (i, 0)), kvb, kvb],
        out_shape=[jax.ShapeDtypeStruct((S, X_Q), f32), jax.ShapeDtypeStruct((MEM_LEN, X_Q), f32),
                   jax.ShapeDtypeStruct((MEM_LEN, X_Q), f32)],
        compiler_params=_cp(("arbitrary",)),
    )(proj, kv, kv, dmix)
    return dq, jnp.concatenate([dk, dv], axis=1)


def _bucket_map():
    qi = np.arange(BLK)[:, None]
    kj = np.arange(2 * BLK)[None, :]
    n = np.maximum(BLK + qi - kj, 0)
    max_exact = N_BUCKETS // 2
    nf = np.maximum(n, 1).astype(np.float64)
    large = max_exact + (np.log(nf / max_exact) / math.log(MAX_DIST / max_exact)
                         * (N_BUCKETS - max_exact)).astype(np.int32)
    large = np.minimum(large, N_BUCKETS - 1)
    return np.where(n < max_exact, n, large).astype(np.int32)


def bias_build(rel_bias):
    def body(rb_ref, bk_ref, o_ref):
        bk = bk_ref[...]
        for h in range(A_HEADS):
            acc = jnp.zeros((BLK, 2 * BLK), f32)
            for b in range(N_BUCKETS):
                acc = jnp.where(bk == b, rb_ref[b, h], acc)
            o_ref[h] = acc

    return pl.pallas_call(
        body, name="bias_build",
        in_specs=[pl.BlockSpec(memory_space=pltpu.SMEM), pl.BlockSpec(memory_space=pltpu.VMEM)],
        out_specs=pl.BlockSpec(memory_space=pltpu.VMEM),
        out_shape=jax.ShapeDtypeStruct((A_HEADS, BLK, 2 * BLK), f32), compiler_params=_cp(),
    )(rel_bias, jnp.asarray(_bucket_map()))


def bias_grad(dbias):
    def body(d_ref, bk_ref, o_ref):
        bk = bk_ref[...]
        row = lax.broadcasted_iota(jnp.int32, (N_BUCKETS, LANE), 0)
        lane = lax.broadcasted_iota(jnp.int32, (N_BUCKETS, LANE), 1)
        acc = jnp.zeros((N_BUCKETS, LANE), f32)
        for h in range(A_HEADS):
            d = d_ref[h]
            for b in range(N_BUCKETS):
                s = jnp.sum(jnp.where(bk == b, d, 0.0), keepdims=True)
                acc = acc + jnp.where((row == b) & (lane == h), s, 0.0)
        o_ref[...] = acc

    return pl.pallas_call(
        body, name="bias_grad", out_shape=jax.ShapeDtypeStruct((N_BUCKETS, LANE), f32), compiler_params=_cp(),
    )(dbias, jnp.asarray(_bucket_map()))


def _swa_f(qb, kp, kc, vp, vc, bias, sk, first):
    kband = jnp.concatenate([kp, kc], axis=0)
    vband = jnp.concatenate([vp, vc], axis=0)
    qi = lax.broadcasted_iota(jnp.int32, (BLK, 2 * BLK), 0)
    kj = lax.broadcasted_iota(jnp.int32, (BLK, 2 * BLK), 1)
    rel = kj - qi
    ok = (rel >= 1) & (rel <= BLK) & ((kj >= BLK) | jnp.logical_not(first))
    lane = lax.broadcasted_iota(jnp.int32, (1, LANE), 1)
    lane_b = lax.broadcasted_iota(jnp.int32, (BLK, LANE), 1)
    outs = []
    for p in range(A_HEADS // 2):
        qp = qb[:, LANE * p:LANE * (p + 1)]
        acc = jnp.zeros((BLK, LANE), f32)
        for g in range(2):
            h = g * (A_HEADS // 2) + p
            msk = (lane // A_DH == g).astype(f32)
            s = bdot_nt(qp * msk, kband) * (A_DH ** -0.5) + bias[h]
            s = jnp.where(ok, s, -1e30)
            skb = jnp.broadcast_to(sk[h:h + 1, :], (BLK, LANE))
            sink = jnp.sum(jnp.where(lane_b == 0, skb, 0.0), axis=-1, keepdims=True)
            m = lax.stop_gradient(jnp.maximum(jnp.max(s, axis=-1, keepdims=True), sink))
            e = jnp.exp(s - m)
            prob = e / (jnp.sum(e, axis=-1, keepdims=True) + jnp.exp(sink - m))
            acc = acc + bdot(prob, vband) * msk
        outs.append(acc)
    return jnp.concatenate(outs, axis=1)


def _swa_specs(nb, rev):
    bi = (lambda i: nb - 1 - i) if rev else (lambda i: i)
    return [
        pl.BlockSpec((BLK, A_Q), lambda i: (bi(i), 0)),
        pl.BlockSpec((BLK, LANE), lambda i: (jnp.maximum(bi(i) - 1, 0), 6)),
        pl.BlockSpec((BLK, LANE), lambda i: (bi(i), 6)),
        pl.BlockSpec((BLK, LANE), lambda i: (jnp.maximum(bi(i) - 1, 0), 7)),
        pl.BlockSpec((BLK, LANE), lambda i: (bi(i), 7)),
        pl.BlockSpec((A_HEADS, BLK, 2 * BLK), lambda i: (0, 0, 0)),
        pl.BlockSpec((16, LANE), lambda i: (0, 0)),
    ]


def swa_fwd(proj, bias, sk):
    S = proj.shape[0]
    nb = S // BLK

    def body(q_ref, kp_ref, kc_ref, vp_ref, vc_ref, b_ref, s_ref, o_ref):
        o_ref[...] = _swa_f(q_ref[...], kp_ref[...], kc_ref[...], vp_ref[...], vc_ref[...], b_ref[...], s_ref[...],
                            pl.program_id(0) == 0)

    return pl.pallas_call(
        body, name="swa_fwd", grid=(nb,), in_specs=_swa_specs(nb, False),
        out_specs=pl.BlockSpec((BLK, A_Q), lambda i: (i, 0)),
        out_shape=jax.ShapeDtypeStruct((S, A_Q), f32), compiler_params=_cp(("parallel",)),
    )(proj, proj, proj, proj, proj, bias, sk)


def swa_bwd(proj, bias, sk, dmix):
    S = proj.shape[0]
    nb = S // BLK

    def body(q_ref, kp_ref, kc_ref, vp_ref, vc_ref, b_ref, s_ref, do_ref, dqkv_ref, db_ref, ds_ref, ck, cv):
        i = pl.program_id(0)

        @pl.when(i == 0)
        def _():
            db_ref[...] = jnp.zeros_like(db_ref)
            ds_ref[...] = jnp.zeros_like(ds_ref)
            ck[...] = jnp.zeros_like(ck)
            cv[...] = jnp.zeros_like(cv)
        first = i == nb - 1
        _, vjp = jax.vjp(lambda *a: _swa_f(*a, first), q_ref[...], kp_ref[...], kc_ref[...], vp_ref[...],
                         vc_ref[...], b_ref[...], s_ref[...])
        dq, dkp, dkc, dvp, dvc, db, ds = vjp(do_ref[...])
        dqkv_ref[...] = jnp.concatenate([dq, dkc + ck[...], dvc + cv[...]], axis=1)
        ck[...] = dkp
        cv[...] = dvp
        db_ref[...] += db
        ds_ref[...] += ds

    return pl.pallas_call(
        body, name="swa_bwd", grid=(nb,),
        in_specs=_swa_specs(nb, True) + [pl.BlockSpec((BLK, A_Q), lambda i: (nb - 1 - i, 0))],
        out_specs=[pl.BlockSpec((BLK, D), lambda i: (nb - 1 - i, 0)),
                   pl.BlockSpec((A_HEADS, BLK, 2 * BLK), lambda i: (0, 0, 0)),
                   pl.BlockSpec((16, LANE), lambda i: (0, 0))],
        out_shape=[jax.ShapeDtypeStruct((S, D), f32), jax.ShapeDtypeStruct((A_HEADS, BLK, 2 * BLK), f32),
                   jax.ShapeDtypeStruct((16, LANE), f32)],
        scratch_shapes=[pltpu.VMEM((BLK, LANE), f32), pltpu.VMEM((BLK, LANE), f32)],
        compiler_params=_cp(("arbitrary",)),
    )(proj, proj, proj, proj, proj, bias, sk, dmix)


def _dnprep_f(x, w, is_qk):
    c = (w[3:4] * x + w[2:3] * shift_down(x, 1) + w[1:2] * shift_down(x, 2) + w[0:1] * shift_down(x, 3))
    a = _silu(c)
    n = a * lax.rsqrt(jnp.sum(a * a, axis=-1, keepdims=True) + EPS)
    return jnp.where(is_qk, n, a)


def dnprep_fwd(proj, cw):
    S = proj.shape[0]
    nblk = B_QKV // LANE

    def body(x_ref, w_ref, o_ref):
        o_ref[...] = _dnprep_f(x_ref[...], w_ref[...], pl.program_id(0) < 2 * B_QK // LANE)

    return pl.pallas_call(
        body, name="dnprep_fwd", grid=(nblk,),
        in_specs=[pl.BlockSpec((S, LANE), lambda j: (0, j)), pl.BlockSpec((4, LANE), lambda j: (0, j))],
        out_specs=pl.BlockSpec((S, LANE), lambda j: (0, j)),
        out_shape=jax.ShapeDtypeStruct((S, B_QKV), f32), compiler_params=_cp(("parallel",)),
    )(proj, cw)


def dnprep_bwd(proj, cw, dqkvn):
    S = proj.shape[0]
    nblk = B_QKV // LANE

    def body(x_ref, w_ref, d_ref, dx_ref, dw_ref):
        is_qk = pl.program_id(0) < 2 * B_QK // LANE
        _, vjp = jax.vjp(lambda a, b: _dnprep_f(a, b, is_qk), x_ref[...], w_ref[...])
        dx, dw = vjp(d_ref[...])
        dx_ref[...] = dx
        dw_ref[...] = dw

    col = pl.BlockSpec((S, LANE), lambda j: (0, j))
    wsp = pl.BlockSpec((4, LANE), lambda j: (0, j))
    return pl.pallas_call(
        body, name="dnprep_bwd", grid=(nblk,), in_specs=[col, wsp, col], out_specs=[col, wsp],
        out_shape=[jax.ShapeDtypeStruct((S, B_QKV), f32), jax.ShapeDtypeStruct((4, B_QKV), f32)],
        compiler_params=_cp(("parallel",)),
    )(proj, cw, dqkvn)


def _hdot(a, b, ca=1, cb=0):
    return _dg(a, b, ca, cb, HI)


def _bdg(a, b, ca, cb):
    dn = (((ca,), (cb,)), ((0,), (0,)))
    ah, bh = a.astype(bf16), b.astype(bf16)
    al, bl = (a - ah.astype(f32)).astype(bf16), (b - bh.astype(f32)).astype(bf16)
    return (lax.dot_general(ah, bh, dn, preferred_element_type=f32)
            + lax.dot_general(ah, bl, dn, preferred_element_type=f32)
            + lax.dot_general(al, bh, dn, preferred_element_type=f32))


@jax.custom_vjp
def hbd(a, b):
    return _bdg(a, b, 2, 1)


@jax.custom_vjp
def hbd_nt(a, b):
    return _bdg(a, b, 2, 2)


@jax.custom_vjp
def hbd_tn(a, b):
    return _bdg(a, b, 1, 1)


hbd.defvjp(lambda a, b: (hbd(a, b), (a, b)), lambda r, g: (hbd_nt(g, r[1]), hbd_tn(r[0], g)))
hbd_nt.defvjp(lambda a, b: (hbd_nt(a, b), (a, b)), lambda r, g: (hbd(g, r[1]), hbd_tn(g, r[0])))
hbd_tn.defvjp(lambda a, b: (hbd_tn(a, b), (a, b)), lambda r, g: (hbd_nt(r[1], g), hbd(r[0], g)))


def _stack(xs):
    return jnp.concatenate([x[None] for x in xs], axis=0)


def _lane_col(x, j):
    lane = lax.broadcasted_iota(jnp.int32, (1, LANE), 1)
    return jnp.sum(jnp.where(lane == j, x, 0.0), axis=-1, keepdims=True)


def _dnc_f(q, k, v, seg, prm):
    C = CHUNK
    beta_all = _sigmoid(seg)
    xx = seg + prm[1:2]
    g_all = -jnp.exp(prm[0:1]) * (jnp.maximum(xx, 0.0) + jnp.log(1.0 + jnp.exp(-jnp.abs(xx))))
    r2 = lax.broadcasted_iota(jnp.int32, (C, C), 0)
    c2 = lax.broadcasted_iota(jnp.int32, (C, C), 1)
    gc_all = _hdot((r2 >= c2).astype(f32), g_all)
    beta = _stack([_lane_col(beta_all, h) for h in range(6)])
    gc = _stack([_lane_col(gc_all, 6 + h) for h in range(6)])
    r = lax.broadcasted_iota(jnp.int32, (1, C, C), 1)
    c = lax.broadcasted_iota(jnp.int32, (1, C, C), 2)
    incl = r >= c
    strict = r > c
    eye = (r == c).astype(f32)
    g_row = hbd(jnp.ones((6, C, C), f32), eye * gc)
    decay = jnp.where(incl, jnp.exp(jnp.where(incl, gc - g_row, 0.0)), 0.0)
    a_mat = beta * hbd_nt(k, k) * jnp.where(strict, decay, 0.0)
    eg = jnp.exp(gc)
    pw = -a_mat
    inv = eye + pw
    for _ in range(5):
        pw = hbd(pw, pw)
        inv = inv + hbd(inv, pw)
    u = hbd(inv, beta * v)
    w = hbd(inv, (beta * eg) * k)
    qc = q * (B_DH ** -0.5)
    attn = hbd_nt(qc, k) * decay
    last = (lax.broadcasted_iota(jnp.int32, (1, C, 1), 1) == C - 1).astype(f32)
    g_last = jnp.sum(gc * last, axis=1, keepdims=True)
    dc = jnp.broadcast_to(jnp.exp(g_last), (6, 1, LANE)).reshape(6, LANE)
    return u, w, qc * eg, k * jnp.exp(g_last - gc), attn, dc


def _dns_f(S0, u, w, qd, kt, attn, dcrows):
    dc = _lane_col(dcrows, 0).reshape(6, 1, 1)
    delta = u - hbd(w, S0)
    out = hbd(qd, S0) + hbd(attn, delta)
    return out, dc * S0 + hbd_tn(kt, delta)


def _dnpost_f(o, z, grow):
    outs = []
    for h in range(6):
        oh = o[:, LANE * h:LANE * (h + 1)]
        outs.append(oh * lax.rsqrt(jnp.mean(oh * oh, axis=-1, keepdims=True) + EPS) * grow
                    * _silu(z[:, LANE * h:LANE * (h + 1)]))
    return jnp.concatenate(outs, axis=1)


def _hs(h):
    return slice(LANE * h, LANE * (h + 1))


def _heads(ref, share):
    return _stack([ref[:, _hs(h // share)] for h in range(6)])


def _put_heads(ref, val):
    for h in range(6):
        ref[:, _hs(h)] = val[h]


def _dnc_in_specs():
    return [
        pl.BlockSpec((CHUNK, B_QK), lambda n: (n, 0)),
        pl.BlockSpec((CHUNK, B_QK), lambda n: (n, 1)),
        pl.BlockSpec((CHUNK, B_V), lambda n: (n, 1)),
        pl.BlockSpec((CHUNK, LANE), lambda n: (n, 20)),
        pl.BlockSpec((8, LANE), lambda n: (0, 0)),
    ]


def _dnc_out_specs(rev_nc=None):
    ci = (lambda n: n) if rev_nc is None else (lambda n: rev_nc - 1 - n)
    wide = pl.BlockSpec((CHUNK, B_V), lambda n: (ci(n), 0))
    return [wide, wide, wide, wide, pl.BlockSpec((1, 6, CHUNK, CHUNK), lambda n: (ci(n), 0, 0, 0)),
            pl.BlockSpec((1, 8, LANE), lambda n: (ci(n), 0, 0))]


def _dnc_shapes(S):
    nc = S // CHUNK
    wide = jax.ShapeDtypeStruct((S, B_V), f32)
    return [wide, wide, wide, wide, jax.ShapeDtypeStruct((nc, 6, CHUNK, CHUNK), f32),
            jax.ShapeDtypeStruct((nc, 8, LANE), f32)]


def dnc_fwd(qkvn, proj, prm):
    S = proj.shape[0]

    def body(q_ref, k_ref, v_ref, s_ref, p_ref, u_ref, w_ref, qd_ref, kt_ref, at_ref, dc_ref):
        u, w, qd, kt, attn, dc = _dnc_f(_heads(q_ref, 2), _heads(k_ref, 2), _heads(v_ref, 1), s_ref[...], p_ref[...])
        _put_heads(u_ref, u)
        _put_heads(w_ref, w)
        _put_heads(qd_ref, qd)
        _put_heads(kt_ref, kt)
        at_ref[0] = attn
        dc_ref[0] = jnp.concatenate([dc, jnp.zeros((2, LANE), f32)], axis=0)

    return pl.pallas_call(
        body, name="dn_chunk_fwd", grid=(S // CHUNK,), in_specs=_dnc_in_specs(), out_specs=_dnc_out_specs(),
        out_shape=_dnc_shapes(S), compiler_params=_cp(("parallel",)),
    )(qkvn, qkvn, qkvn, proj, prm)


def dnc_bwd(qkvn, proj, prm, cots):
    S = proj.shape[0]

    def body(q_ref, k_ref, v_ref, s_ref, p_ref, du_ref, dw_ref, dqd_ref, dkt_ref, dat_ref, ddc_ref,
             dx_ref, dseg_ref, dprm_ref):
        @pl.when(pl.program_id(0) == 0)
        def _():
            dprm_ref[...] = jnp.zeros_like(dprm_ref)
        _, vjp = jax.vjp(_dnc_f, _heads(q_ref, 2), _heads(k_ref, 2), _heads(v_ref, 1), s_ref[...], p_ref[...])
        dq, dk, dv, dseg, dprm = vjp((_heads(du_ref, 1), _heads(dw_ref, 1), _heads(dqd_ref, 1), _heads(dkt_ref, 1),
                                      dat_ref[0], ddc_ref[0, 0:6, :]))
        dx_ref[...] = jnp.concatenate([dq[0] + dq[1], dq[2] + dq[3], dq[4] + dq[5],
                                       dk[0] + dk[1], dk[2] + dk[3], dk[4] + dk[5]] + [dv[h] for h in range(6)], axis=1)
        dseg_ref[...] = dseg
        dprm_ref[...] += dprm

    return pl.pallas_call(
        body, name="dn_chunk_bwd", grid=(S // CHUNK,), in_specs=_dnc_in_specs() + _dnc_out_specs(),
        out_specs=[pl.BlockSpec((CHUNK, B_QKV), lambda n: (n, 0)), pl.BlockSpec((CHUNK, LANE), lambda n: (n, 0)),
                   pl.BlockSpec((8, LANE), lambda n: (0, 0))],
        out_shape=[jax.ShapeDtypeStruct((S, B_QKV), f32), jax.ShapeDtypeStruct((S, LANE), f32),
                   jax.ShapeDtypeStruct((8, LANE), f32)],
        compiler_params=_cp(("arbitrary",)),
    )(qkvn, qkvn, qkvn, proj, prm, *cots)


def dns_fwd(chunked):
    u = chunked[0]
    S = u.shape[0]
    nc = S // CHUNK

    def body(u_ref, w_ref, qd_ref, kt_ref, at_ref, dc_ref, o_ref, st_ref, st):
        @pl.when(pl.program_id(0) == 0)
        def _():
            st[...] = jnp.zeros_like(st)
        S0 = st[...]
        st_ref[0] = S0
        out, S1 = _dns_f(S0, _heads(u_ref, 1), _heads(w_ref, 1), _heads(qd_ref, 1), _heads(kt_ref, 1),
                         at_ref[0], dc_ref[0, 0:6, :])
        _put_heads(o_ref, out)
        st[...] = S1

    return pl.pallas_call(
        body, name="dn_scan_fwd", grid=(nc,), in_specs=_dnc_out_specs(),
        out_specs=[pl.BlockSpec((CHUNK, B_V), lambda n: (n, 0)),
                   pl.BlockSpec((1, 6, B_DH, B_DH), lambda n: (n, 0, 0, 0))],
        out_shape=[jax.ShapeDtypeStruct((S, B_V), f32), jax.ShapeDtypeStruct((nc, 6, B_DH, B_DH), f32)],
        scratch_shapes=[pltpu.VMEM((6, B_DH, B_DH), f32)],
        compiler_params=_cp(("arbitrary",)),
    )(*chunked)


def dns_bwd(chunked, states, do):
    S = do.shape[0]
    nc = S // CHUNK

    def body(u_ref, w_ref, qd_ref, kt_ref, at_ref, dc_ref, st_ref, do_ref,
             du_ref, dw_ref, dqd_ref, dkt_ref, dat_ref, ddc_ref, dst):
        @pl.when(pl.program_id(0) == 0)
        def _():
            dst[...] = jnp.zeros_like(dst)
        _, vjp = jax.vjp(_dns_f, st_ref[0], _heads(u_ref, 1), _heads(w_ref, 1), _heads(qd_ref, 1), _heads(kt_ref, 1),
                         at_ref[0], dc_ref[0, 0:6, :])
        dS0, du, dw, dqd, dkt, dat, ddc = vjp((_heads(do_ref, 1), dst[...]))
        dst[...] = dS0
        _put_heads(du_ref, du)
        _put_heads(dw_ref, dw)
        _put_heads(dqd_ref, dqd)
        _put_heads(dkt_ref, dkt)
        dat_ref[0] = dat
        ddc_ref[0] = jnp.concatenate([ddc, jnp.zeros((2, LANE), f32)], axis=0)

    return pl.pallas_call(
        body, name="dn_scan_bwd", grid=(nc,),
        in_specs=_dnc_out_specs(nc) + [pl.BlockSpec((1, 6, B_DH, B_DH), lambda n: (nc - 1 - n, 0, 0, 0)),
                                       pl.BlockSpec((CHUNK, B_V), lambda n: (nc - 1 - n, 0))],
        out_specs=_dnc_out_specs(nc), out_shape=_dnc_shapes(S),
        scratch_shapes=[pltpu.VMEM((6, B_DH, B_DH), f32)],
        compiler_params=_cp(("arbitrary",)),
    )(*chunked, states, do)


def dnpost_fwd(o, proj, prm):
    S = o.shape[0]
    t = min(512, S)

    def body(o_ref, z_ref, p_ref, y_ref):
        y_ref[...] = _dnpost_f(o_ref[...], z_ref[...], p_ref[2:3, :])

    tok = pl.BlockSpec((t, B_V), lambda i: (i, 0))
    return pl.pallas_call(
        body, name="dn_post_fwd", grid=(S // t,),
        in_specs=[tok, pl.BlockSpec((t, B_V), lambda i: (i, 2)), pl.BlockSpec((8, LANE), lambda i: (0, 0))],
        out_specs=tok, out_shape=jax.ShapeDtypeStruct((S, B_V), f32), compiler_params=_cp(("parallel",)),
    )(o, proj, prm)


def dnpost_bwd(o, proj, prm, dmix):
    S = o.shape[0]
    t = min(512, S)

    def body(o_ref, z_ref, p_ref, dy_ref, do_ref, dz_ref, dg_ref):
        @pl.when(pl.program_id(0) == 0)
        def _():
            dg_ref[...] = jnp.zeros_like(dg_ref)
        _, vjp = jax.vjp(_dnpost_f, o_ref[...], z_ref[...], p_ref[2:3, :])
        do, dz, dg = vjp(dy_ref[...])
        do_ref[...] = do
        dz_ref[...] = dz
        dg_ref[...] += dg

    tok = pl.BlockSpec((t, B_V), lambda i: (i, 0))
    return pl.pallas_call(
        body, name="dn_post_bwd", grid=(S // t,),
        in_specs=[tok, pl.BlockSpec((t, B_V), lambda i: (i, 2)), pl.BlockSpec((8, LANE), lambda i: (0, 0)), tok],
        out_specs=[tok, tok, pl.BlockSpec((1, LANE), lambda i: (0, 0))],
        out_shape=[jax.ShapeDtypeStruct((S, B_V), f32), jax.ShapeDtypeStruct((S, B_V), f32),
                   jax.ShapeDtypeStruct((1, LANE), f32)],
        compiler_params=_cp(("arbitrary",)),
    )(o, proj, prm, dmix)


N_FF_BLK = D_FF // LANE
GU_SHARD = 2 * D_FF // 4


def _glu_f(gate, up, w, b):
    c = w[2:3] * gate + w[1:2] * shift_down(gate, 1) + w[0:1] * shift_down(gate, 2) + b
    return _silu(c) * up


def glu_fwd(gu, w, b, name):
    S = gu.shape[0]

    def body(g_ref, u_ref, w_ref, b_ref, o_ref):
        o_ref[...] = _glu_f(g_ref[...], u_ref[...], w_ref[...], b_ref[...]).astype(bf16)

    col = pl.BlockSpec((S, LANE), lambda j: (0, j))
    return pl.pallas_call(
        body, name=name, grid=(N_FF_BLK,),
        in_specs=[col, pl.BlockSpec((S, LANE), lambda j: (0, N_FF_BLK + j)), pl.BlockSpec((3, LANE), lambda j: (0, j)),
                  pl.BlockSpec((1, LANE), lambda j: (0, j))],
        out_specs=col, out_shape=jax.ShapeDtypeStruct((S, D_FF), bf16), compiler_params=_cp(("parallel",)),
    )(gu, gu, w, b.reshape(1, D_FF))


def glu_bwd(gu, w, b, dact, name):
    S = gu.shape[0]

    def body(g_ref, u_ref, w_ref, b_ref, d_ref, dg_ref, dw_ref, db_ref):
        _, vjp = jax.vjp(_glu_f, g_ref[...], u_ref[...], w_ref[...], b_ref[...])
        dg, du, dw, db = vjp(d_ref[...])
        dg_ref[0] = dg.astype(bf16)
        dg_ref[1] = du.astype(bf16)
        dw_ref[...] = dw
        db_ref[...] = db

    col = pl.BlockSpec((S, LANE), lambda j: (0, j))
    wsp = pl.BlockSpec((3, LANE), lambda j: (0, j))
    bsp = pl.BlockSpec((1, LANE), lambda j: (0, j))
    return pl.pallas_call(
        body, name=name, grid=(N_FF_BLK,),
        in_specs=[col, pl.BlockSpec((S, LANE), lambda j: (0, N_FF_BLK + j)), wsp, bsp, col],
        out_specs=[pl.BlockSpec((2, S, LANE), lambda j: (0, 0, j)), wsp, bsp],
        out_shape=[jax.ShapeDtypeStruct((2, S, D_FF), bf16), jax.ShapeDtypeStruct((3, D_FF), f32),
                   jax.ShapeDtypeStruct((1, D_FF), f32)],
        compiler_params=_cp(("parallel",)),
    )(gu, gu, w, b.reshape(1, D_FF), dact)


def gu_fwd(n2, wg, name):
    S = n2.shape[0]
    tm = min(512, S)

    def body(a_ref, w_ref, o_ref):
        o_ref[...] = _dg(a_ref[...], w_ref[...], 1, 0)

    return pl.pallas_call(
        body, name=name, grid=(4, S // tm),
        in_specs=[pl.BlockSpec((tm, D), lambda s, m: (m, 0)), pl.BlockSpec((None, D, GU_SHARD), lambda s, m: (s, 0, 0))],
        out_specs=pl.BlockSpec((tm, GU_SHARD), lambda s, m: (m, s)),
        out_shape=jax.ShapeDtypeStruct((S, 2 * D_FF), f32), compiler_params=_cp(("parallel", "parallel")),
    )(n2, wg)


def gu_bwd_x(dgu, wg, name):
    S = dgu.shape[1]
    tm = min(512, S)

    def body(d_ref, w_ref, o_ref):
        @pl.when(pl.program_id(1) == 0)
        def _():
            o_ref[...] = jnp.zeros_like(o_ref)
        o_ref[...] += _dg(d_ref[...], w_ref[...], 1, 1)

    return pl.pallas_call(
        body, name=name, grid=(S // tm, 4),
        in_specs=[pl.BlockSpec((None, tm, GU_SHARD), lambda m, s: (s // 2, m, s % 2)),
                  pl.BlockSpec((None, D, GU_SHARD), lambda m, s: (s, 0, 0))],
        out_specs=pl.BlockSpec((tm, D), lambda m, s: (m, 0)),
        out_shape=jax.ShapeDtypeStruct((S, D), f32), compiler_params=_cp(("parallel", "arbitrary")),
    )(dgu, wg)


def gu_bwd_w(n2, dgu, name):
    S = n2.shape[0]
    tm = min(512, S)
    nm = S // tm

    def body(a_ref, d_ref, o_ref, acc):
        @pl.when(pl.program_id(1) == 0)
        def _():
            acc[...] = jnp.zeros_like(acc)
        acc[...] += _dg(a_ref[...], d_ref[...], 0, 0)

        @pl.when(pl.program_id(1) == nm - 1)
        def _():
            o_ref[...] = acc[...].astype(bf16)

    return pl.pallas_call(
        body, name=name, grid=(4, nm),
        in_specs=[pl.BlockSpec((tm, D), lambda s, m: (m, 0)),
                  pl.BlockSpec((None, tm, GU_SHARD), lambda s, m: (s // 2, m, s % 2))],
        out_specs=pl.BlockSpec((None, D, GU_SHARD), lambda s, m: (s, 0, 0)),
        out_shape=jax.ShapeDtypeStruct((4, D, GU_SHARD), bf16),
        scratch_shapes=[pltpu.VMEM((D, GU_SHARD), f32)],
        compiler_params=_cp(("parallel", "arbitrary")),
    )(n2, dgu)


def _pair_cols(w):
    lead = w.shape[:-1]
    return w.reshape(lead + (2, 6, A_DH)).swapaxes(-3, -2).reshape(lead + (A_Q,))


def _unpair_cols(w):
    lead = w.shape[:-1]
    return w.reshape(lead + (6, 2, A_DH)).swapaxes(-3, -2).reshape(lead + (A_Q,))


def _lay_in_a(w):
    return jnp.concatenate([_pair_cols(w[:, :A_Q]), w[:, A_Q:]], axis=1)


def _unlay_in_a(w):
    return jnp.concatenate([_unpair_cols(w[:, :A_Q]), w[:, A_Q:]], axis=1)


def _lay_out_a(w):
    return jnp.concatenate([_pair_cols(w[:A_Q].T).T, w[A_Q:]], axis=0)


def _unlay_out_a(w):
    return jnp.concatenate([_unpair_cols(w[:A_Q].T).T, w[A_Q:]], axis=0)


def _lay_in_b(w):
    return jnp.concatenate([w[:, :2304], w[:, 2316:], w[:, 2304:2316],
                            jnp.zeros((w.shape[0], LANE - 12), w.dtype)], axis=1)


def _unlay_in_b(w):
    return jnp.concatenate([w[:, :2304], w[:, 2560:2572], w[:, 2304:2560]], axis=1)


def _chip_cols(w):
    return jnp.moveaxis(w.reshape(w.shape[0], 4, w.shape[1] // 4), 1, 0)


def _unchip_cols(w):
    return jnp.moveaxis(w, 0, 1).reshape(w.shape[1], 4 * w.shape[2])


def _local_step(x, mem, target, P):
    sk = jnp.zeros((16, LANE), f32).at[:A_HEADS].set(jnp.broadcast_to(P["sinks"][:, None], (A_HEADS, LANE)))
    prm = jnp.zeros((8, LANE), f32).at[0, 6:12].set(P["a_log"]).at[1, 6:12].set(P["dt_bias"]).at[2].set(P["out_norm_g"])
    bias = bias_build(P["rel_bias"])
    saved = []
    h = x
    for i in range(2):
        n1 = rms_fwd(h, P["g_mix"][i], f"rms_mix{i}")
        kv = memkv_fwd(mem, P["g_mem"][i], P["w_mem"][i], f"memkv{i}")
        if i == 0:
            proj = mm_nn(n1, P["w_in_a"], name="proj_a")
            self_out = swa_fwd(proj, bias, sk)
            cross = xattn_fwd(proj, A_Q + 2 * LANE, kv, "xattn_a")
            extra = ()
        else:
            proj = mm_nn(n1, P["w_in_b"], name="proj_b")
            qkvn = dnprep_fwd(proj, P["conv_qkv"])
            chunked = dnc_fwd(qkvn, proj, prm)
            o, states = dns_fwd(chunked)
            self_out = dnpost_fwd(o, proj, prm)
            cross = xattn_fwd(proj, 2304, kv, "xattn_b")
            extra = (qkvn, chunked, states, o)
        mix = jnp.concatenate([self_out, cross], axis=1)
        h2 = mm_nn(mix, P["w_out"][i], res=h, name=f"out_proj{i}")
        n2 = rms_fwd(h2, P["g_ffn"][i], f"rms_ffn{i}")
        gu = gu_fwd(n2, P["w_gu"][i], f"gate_up{i}")
        act = glu_fwd(gu, P["ffn_cw"][i], P["ffn_cb"][i], f"glu{i}")
        h3 = mm_nn(act, P["w_down"][i], res=h2, name=f"down{i}")
        saved.append((h, n1, kv, proj, mix, h2, n2, gu, act, extra))
        h = h3

    loss, dh, dg_fin = loss_head(h, P["g_fin"], target)
    G = {"g_fin": dg_fin[0], "g_mix": [None, None], "g_mem": [None, None], "g_ffn": [None, None],
         "w_mem": [None, None], "w_out": [None, None], "w_gu": [None, None], "w_down": [None, None],
         "ffn_cw": [None, None], "ffn_cb": [None, None]}
    for i in (1, 0):
        hin, n1, kv, proj, mix, h2, n2, gu, act, extra = saved[i]
        dact = mm_nt(dh, P["w_down"][i], name=f"d_act{i}")
        G["w_down"][i] = mm_tn(act, dh, name=f"dw_down{i}")
        dgu, dcw, dcb = glu_bwd(gu, P["ffn_cw"][i], P["ffn_cb"][i], dact, f"glu_bwd{i}")
        G["ffn_cw"][i], G["ffn_cb"][i] = dcw, dcb[0]
        dn2 = gu_bwd_x(dgu, P["w_gu"][i], f"d_n2_{i}")
        G["w_gu"][i] = gu_bwd_w(n2, dgu, f"dw_gu{i}")
        dh2, dg = rms_bwd(h2, P["g_ffn"][i], dn2, dh, f"rms_ffn_bwd{i}")
        G["g_ffn"][i] = dg[0]
        dmix = mm_nt(dh2, P["w_out"][i], name=f"d_mix{i}")
        G["w_out"][i] = mm_tn(mix, dh2, name=f"dw_out{i}")
        if i == 0:
            dqkv, dbias, dsk = swa_bwd(proj, bias, sk, dmix)
            dxq, dkv = xattn_bwd(proj, A_Q + 2 * LANE, kv, dmix, "xattn_a_bwd")
            dproj = jnp.concatenate([dqkv, dxq], axis=1)
            G["sinks"] = dsk[:A_HEADS, 0]
            G["rel_bias"] = bias_grad(dbias)[:, :A_HEADS]
            w_in, gname = P["w_in_a"], "w_in_a"
        else:
            qkvn, chunked, states, o = extra
            do, dz, dgo = dnpost_bwd(o, proj, prm, dmix)
            dqkvn, dseg, dprm = dnc_bwd(qkvn, proj, prm, dns_bwd(chunked, states, do))
            draw, dconv = dnprep_bwd(proj, P["conv_qkv"], dqkvn)
            dxq, dkv = xattn_bwd(proj, 2304, kv, dmix, "xattn_b_bwd")
            dproj = jnp.concatenate([draw, dz, dxq, dseg], axis=1)
            G["conv_qkv"] = dconv
            G["a_log"], G["dt_bias"], G["out_norm_g"] = dprm[0, 6:12], dprm[1, 6:12], dgo[0]
            w_in, gname = P["w_in_b"], "w_in_b"
        dn1 = mm_nt(dproj, w_in, name=f"d_n1_{i}")
        G[gname] = mm_tn(n1, dproj, name=f"d{gname}")
        dh, dg = rms_bwd(hin, P["g_mix"][i], dn1, dh2, f"rms_mix_bwd{i}")
        G["g_mix"][i] = dg[0]
        dgm, dwm = memkv_bwd(mem, P["g_mem"][i], P["w_mem"][i], dkv, f"memkv_bwd{i}")
        G["g_mem"][i], G["w_mem"][i] = dgm[0], dwm
    return loss, dh, G


def _prepare(full, w_gu=None):
    return {
        "rel_bias": full["rel_bias"], "sinks": full["sinks_a"][0], "a_log": full["a_log_b"][0],
        "dt_bias": full["dt_bias_b"][0], "out_norm_g": full["out_norm_g_b"][0],
        "g_mix": full["norm_mix_g"], "g_mem": full["norm_mem_g"], "g_ffn": full["norm_ffn_g"],
        "g_fin": full["final_norm_g"], "conv_qkv": full["conv_qkv_b"][0],
        "ffn_cw": [full["ffn_conv_w"][0], full["ffn_conv_w"][1]],
        "ffn_cb": [full["ffn_conv_b"][0], full["ffn_conv_b"][1]],
        "w_mem": [full["w_mem_kv"][0], full["w_mem_kv"][1]],
        "w_out": [_lay_out_a(full["w_out"][0]), full["w_out"][1]],
        "w_in_a": _lay_in_a(full["w_in_a"][0]), "w_in_b": _lay_in_b(full["w_in_b"][0]),
        "w_gu": w_gu if w_gu is not None else [_chip_cols(full["w_gate_up"][0]), _chip_cols(full["w_gate_up"][1])],
        "w_down": [full["w_down"][0], full["w_down"][1]],
    }


def _grads_to_ref(G):
    return {
        "rel_bias": G["rel_bias"], "norm_mix_g": jnp.stack(G["g_mix"]), "norm_mem_g": jnp.stack(G["g_mem"]),
        "w_mem_kv": jnp.stack(G["w_mem"]),
        "w_out": jnp.stack([_unlay_out_a(G["w_out"][0]), G["w_out"][1]]),
        "w_in_a": _unlay_in_a(G["w_in_a"])[None], "sinks_a": G["sinks"][None],
        "w_in_b": _unlay_in_b(G["w_in_b"])[None], "conv_qkv_b": G["conv_qkv"][None],
        "a_log_b": G["a_log"][None], "dt_bias_b": G["dt_bias"][None], "out_norm_g_b": G["out_norm_g"][None],
        "norm_ffn_g": jnp.stack(G["g_ffn"]),
        "w_gate_up": jnp.stack([_unchip_cols(G["w_gu"][0]), _unchip_cols(G["w_gu"][1])]).astype(f32),
        "ffn_conv_w": jnp.stack(G["ffn_cw"]), "ffn_conv_b": jnp.stack(G["ffn_cb"]),
        "w_down": jnp.stack(G["w_down"]), "final_norm_g": G["g_fin"],
    }


ANY = pl.BlockSpec(memory_space=pl.ANY)


def _place():
    return lax.axis_index("x"), lax.axis_index("y"), lax.axis_index("c")


def chip_scatter(gs):
    n = len(gs)

    def body(*refs):
        ins, outs = refs[:n], refs[n:2 * n]
        ssem, rsem = refs[2 * n:]
        x, y, c = _place()
        me = 2 * x + y
        peers = [(1 - x, y), (x, 1 - y), (1 - x, 1 - y)]

        def remote(j, k, slot):
            px, py = peers[k]
            return pltpu.make_async_remote_copy(
                src_ref=ins[j].at[2 * px + py], dst_ref=outs[j].at[slot],
                send_sem=ssem.at[3 * j + k], recv_sem=rsem.at[3 * j + k],
                device_id=(px, py, c), device_id_type=MESH)

        sends = [remote(j, k, me) for j in range(n) for k in range(3)]
        for cp in sends:
            cp.start()
        for j in range(n):
            for k in range(3):
                px, py = peers[k]
                remote(j, k, 2 * px + py).wait_recv()
        for cp in sends:
            cp.wait_send()

    return pl.pallas_call(
        body, name="grad_scatter", in_specs=[ANY] * n, out_specs=[ANY] * n,
        out_shape=[jax.ShapeDtypeStruct(g.shape, g.dtype) for g in gs],
        scratch_shapes=[pltpu.SemaphoreType.DMA((3 * n,)), pltpu.SemaphoreType.DMA((3 * n,))],
    )(*gs)


def allreduce_small(buf):
    R = buf.shape[0]

    def body(b_ref, o_ref, recv, ssem, rsem):
        x, y, c = _place()
        me = 4 * x + 2 * y + c

        def peer(k):
            return (1 - x if k & 4 else x, 1 - y if k & 2 else y, 1 - c if k & 1 else c)

        def remote(k, slot):
            return pltpu.make_async_remote_copy(
                src_ref=b_ref, dst_ref=recv.at[slot], send_sem=ssem.at[k - 1], recv_sem=rsem.at[k - 1],
                device_id=peer(k), device_id_type=MESH)

        sends = [remote(k, me) for k in range(1, 8)]
        for cp in sends:
            cp.start()
        recv[me] = b_ref[...]
        for k in range(1, 8):
            px, py, pc = peer(k)
            remote(k, 4 * px + 2 * py + pc).wait_recv()
        for cp in sends:
            cp.wait_send()
        total = recv[0]
        for j in range(1, 8):
            total = total + recv[j]
        o_ref[...] = total

    return pl.pallas_call(
        body, name="small_allreduce",
        in_specs=[pl.BlockSpec(memory_space=pltpu.VMEM)], out_specs=pl.BlockSpec(memory_space=pltpu.VMEM),
        out_shape=jax.ShapeDtypeStruct(buf.shape, f32),
        scratch_shapes=[pltpu.VMEM((8, R, LANE), f32), pltpu.SemaphoreType.DMA((7,)), pltpu.SemaphoreType.DMA((7,))],
    )(buf)


def sum_slots(own, recv, chip, core, name):
    _, R, C = recv.shape
    tr = _row_tile(R)
    nt = R // tr

    def body(p_ref, a_ref, r_ref, o_ref):
        acc = jnp.zeros((tr, C), f32)
        for s in range(4):
            acc = acc + jnp.where(p_ref[0] == s, a_ref[s], r_ref[s]).astype(f32)
        o_ref[...] = acc

    slots = pl.BlockSpec((4, tr, C), lambda i, p_ref: (0, i, 0))
    return pl.pallas_call(
        body, name=name, out_shape=jax.ShapeDtypeStruct((2 * R, C), f32),
        grid_spec=pltpu.PrefetchScalarGridSpec(
            num_scalar_prefetch=1, grid=(nt,), in_specs=[slots, slots],
            out_specs=pl.BlockSpec((tr, C), lambda i, p_ref: (p_ref[1] * nt + i, 0))),
        compiler_params=_cp(("parallel",)),
    )(jnp.stack([chip, core]).astype(jnp.int32), own, recv)


def _half(ref, core, axis=0):
    half = ref.shape[axis] // 2
    idx = (slice(None),) * axis + (pl.ds(core * half, half),)
    return ref.at[idx]


def gather_weights(bufs, smalls):
    nb, n = len(bufs), len(bufs) + len(smalls)

    def body(*refs):
        outs = refs[n:2 * n]
        isend, irecv, dsend, drecv = refs[2 * n:]
        x, y, c = _place()
        me = 2 * x + y
        peers = [(1 - x, y), (x, 1 - y), (1 - x, 1 - y)]
        chips = [2 * px + py for px, py in peers]

        def rows(ref, j, core):
            return _half(ref, core) if j < nb else ref

        def ici(j, k, slot):
            px, py = peers[k]
            return pltpu.make_async_remote_copy(
                src_ref=rows(outs[j].at[me], j, c), dst_ref=rows(outs[j].at[slot], j, c),
                send_sem=isend.at[3 * j + k], recv_sem=irecv.at[3 * j + k],
                device_id=(px, py, c), device_id_type=MESH)

        def d2d(j, k, core):
            blk = rows(outs[j].at[chips[k]], j, core)
            return pltpu.make_async_remote_copy(
                src_ref=blk, dst_ref=blk, send_sem=dsend.at[3 * j + k], recv_sem=drecv.at[3 * j + k],
                device_id=(x, y, 1 - c), device_id_type=MESH)

        sends = [ici(j, k, me) for j in range(n) for k in range(3)]
        for cp in sends:
            cp.start()
        for j in range(n):
            for k in range(3):
                ici(j, k, chips[k]).wait_recv()
                if j < nb:
                    fw = d2d(j, k, c)
                    fw.start()
                    sends.append(fw)
        for j in range(nb):
            for k in range(3):
                d2d(j, k, 1 - c).wait_recv()
        for cp in sends:
            cp.wait_send()

    arrs = list(bufs) + list(smalls)
    return pl.pallas_call(
        body, name="weight_allgather", in_specs=[ANY] * n, out_specs=[ANY] * n,
        out_shape=[jax.ShapeDtypeStruct(a.shape, a.dtype) for a in arrs],
        input_output_aliases={j: j for j in range(n)},
        scratch_shapes=[pltpu.SemaphoreType.DMA((3 * n,)), pltpu.SemaphoreType.DMA((3 * n,)),
                        pltpu.SemaphoreType.DMA((3 * nb,)), pltpu.SemaphoreType.DMA((3 * nb,))],
    )(*arrs)


def pair_exchange(gbufs):
    n = len(gbufs)

    def body(*refs):
        ins, outs = refs[:n], refs[n:2 * n]
        ssem, rsem = refs[2 * n:]
        x, y, c = _place()
        cps = [pltpu.make_async_remote_copy(
            src_ref=_half(ins[j], 1 - c, axis=1), dst_ref=outs[j], send_sem=ssem.at[j], recv_sem=rsem.at[j],
            device_id=(x, y, 1 - c), device_id_type=MESH) for j in range(n)]
        for cp in cps:
            cp.start()
        for cp in cps:
            cp.wait()

    return pl.pallas_call(
        body, name="pair_exchange", in_specs=[ANY] * n, out_specs=[ANY] * n,
        out_shape=[jax.ShapeDtypeStruct((4, g.shape[1] // 2, g.shape[2]), g.dtype) for g in gbufs],
        scratch_shapes=[pltpu.SemaphoreType.DMA((n,)), pltpu.SemaphoreType.DMA((n,))],
    )(*gbufs)


def _row_tile(rows):
    t = 256
    while rows % t:
        t //= 2
    return t


def pair_sum(mine, theirs, core, name):
    _, R, C = mine.shape
    half = R // 2
    tr = _row_tile(half)
    nt = half // tr

    def body(c_ref, a_ref, b_ref, o_ref):
        o_ref[...] = (a_ref[...].astype(f32) + b_ref[...].astype(f32)).astype(bf16)

    return pl.pallas_call(
        body, name=name, out_shape=jax.ShapeDtypeStruct(theirs.shape, bf16),
        grid_spec=pltpu.PrefetchScalarGridSpec(
            num_scalar_prefetch=1, grid=(4, nt),
            in_specs=[pl.BlockSpec((None, tr, C), lambda s, i, c_ref: (s, c_ref[0] * nt + i, 0)),
                      pl.BlockSpec((None, tr, C), lambda s, i, c_ref: (s, i, 0))],
            out_specs=pl.BlockSpec((None, tr, C), lambda s, i, c_ref: (s, i, 0))),
        compiler_params=_cp(("parallel", "parallel")),
    )(jnp.reshape(core, (1,)).astype(jnp.int32), mine, theirs)


def final_exchange(fins):
    n = len(fins)

    def body(*refs):
        outs = refs[n:2 * n]
        ssem, rsem = refs[2 * n:]
        x, y, c = _place()
        cps = [pltpu.make_async_remote_copy(
            src_ref=_half(outs[j], c), dst_ref=_half(outs[j], c), send_sem=ssem.at[j], recv_sem=rsem.at[j],
            device_id=(x, y, 1 - c), device_id_type=MESH) for j in range(n)]
        for cp in cps:
            cp.start()
        for cp in cps:
            cp.wait()

    return pl.pallas_call(
        body, name="final_exchange", in_specs=[ANY] * n, out_specs=[ANY] * n,
        out_shape=[jax.ShapeDtypeStruct(f.shape, f.dtype) for f in fins],
        input_output_aliases={j: j for j in range(n)},
        scratch_shapes=[pltpu.SemaphoreType.DMA((n,)), pltpu.SemaphoreType.DMA((n,))],
    )(*fins)


def adamw_big(w, m, v, gs, row0, name):
    L, R, C = w.shape
    tr = _row_tile(math.gcd(R, row0) if row0 else R)
    b0 = row0 // tr

    def body(*refs):
        w_ref, m_ref, v_ref = refs[:3]
        g_refs = refs[3:3 + L]
        g_ref, d_ref, nm_ref, nv_ref = refs[3 + L:]
        g = g_refs[0][...]
        for l in range(1, L):
            g = jnp.where(pl.program_id(0) == l, g_refs[l][...], g)
        d, nm, nv = _adamw_math(w_ref[...], g, m_ref[...], v_ref[...])
        g_ref[...] = g
        d_ref[...] = d
        nm_ref[...] = nm
        nv_ref[...] = nv

    own = pl.BlockSpec((None, tr, C), lambda l, i: (l, i, 0))
    off = pl.BlockSpec((tr, C), lambda l, i: (b0 + i, 0))
    return pl.pallas_call(
        body, name=name, grid=(L, R // tr), in_specs=[own, own, own] + [off] * L, out_specs=[own] * 4,
        out_shape=[jax.ShapeDtypeStruct((L, R, C), f32)] * 4, compiler_params=_cp(("parallel", "parallel")),
    )(w, m, v, *gs)


def _adamw_math(w, g, m, v):
    m = B1 * m + (1.0 - B1) * g
    v = B2 * v + (1.0 - B2) * (g * g)
    m_hat = m / (1.0 - B1 ** STEP)
    v_hat = v / (1.0 - B2 ** STEP)
    delta = -LR * (m_hat / (jnp.sqrt(v_hat) + AEPS) + WD * w)
    return delta, m, v


def adamw_small(w, m, v, g):
    def body(w_ref, m_ref, v_ref, g_ref, d_ref, nm_ref, nv_ref):
        d, nm, nv = _adamw_math(w_ref[...], g_ref[...], m_ref[...], v_ref[...])
        d_ref[...] = d
        nm_ref[...] = nm
        nv_ref[...] = nv

    return pl.pallas_call(body, name="adamw_small", out_shape=[jax.ShapeDtypeStruct(w.shape, f32)] * 3)(w, m, v, g)


CONV =(("conv_qkv_b", 2), ("ffn_conv_w", 2))
SMALL = ("rel_bias", "norm_mix_g", "norm_mem_g", "sinks_a", "a_log_b", "dt_bias_b", "out_norm_g_b", "norm_ffn_g",
         "ffn_conv_b", "final_norm_g")
WEIGHTS = ("rel_bias", "norm_mix_g", "norm_mem_g", "w_mem_kv", "w_out", "w_in_a", "sinks_a", "w_in_b", "conv_qkv_b",
           "a_log_b", "dt_bias_b", "out_norm_g_b", "norm_ffn_g", "w_gate_up", "ffn_conv_w", "ffn_conv_b", "w_down",
           "final_norm_g")
ARGS = ("x", "mem") + WEIGHTS + ("loss_target",) + tuple("m_" + n for n in WEIGHTS) + tuple("v_" + n for n in WEIGHTS)


def _rows(a, width):
    flat = a.reshape(-1)
    pad = (-flat.shape[0]) % (8 * width)
    if pad:
        flat = jnp.concatenate([flat, jnp.zeros((pad,), a.dtype)])
    return flat.reshape(-1, width)


def _nrows(shape, width):
    return _pad_to(-(-math.prod(shape) // width), 8)


def _pack(arrs, width, total_rows, dtype):
    parts = [_rows(a.astype(dtype), width) for a in arrs]
    used = sum(p.shape[0] for p in parts)
    if total_rows > used:
        parts.append(jnp.zeros((total_rows - used, width), dtype))
    return jnp.concatenate(parts, axis=0)


def _unpack(buf, shapes, width):
    out, r = [], 0
    for s in shapes:
        n = _nrows(s, width)
        out.append(buf[r:r + n].reshape(-1)[:math.prod(s)].reshape(s))
        r += n
    return out


def _pad_to(n, mult):
    return -(-n // mult) * mult


def kernel(x, mem, rel_bias, norm_mix_g, norm_mem_g, w_mem_kv, w_out, w_in_a, sinks_a, w_in_b, conv_qkv_b, a_log_b, dt_bias_b, out_norm_g_b, norm_ffn_g, w_gate_up, ffn_conv_w, ffn_conv_b, w_down, final_norm_g, loss_target, m_rel_bias, m_norm_mix_g, m_norm_mem_g, m_w_mem_kv, m_w_out, m_w_in_a, m_sinks_a, m_w_in_b, m_conv_qkv_b, m_a_log_b, m_dt_bias_b, m_out_norm_g_b, m_norm_ffn_g, m_w_gate_up, m_ffn_conv_w, m_ffn_conv_b, m_w_down, m_final_norm_g, v_rel_bias, v_norm_mix_g, v_norm_mem_g, v_w_mem_kv, v_w_out, v_w_in_a, v_sinks_a, v_w_in_b, v_conv_qkv_b, v_a_log_b, v_dt_bias_b, v_out_norm_g_b, v_norm_ffn_g, v_w_gate_up, v_ffn_conv_w, v_ffn_conv_b, v_w_down, v_final_norm_g):
    A = dict(zip(ARGS, (x, mem, rel_bias, norm_mix_g, norm_mem_g, w_mem_kv, w_out, w_in_a, sinks_a, w_in_b, conv_qkv_b, a_log_b, dt_bias_b, out_norm_g_b, norm_ffn_g, w_gate_up, ffn_conv_w, ffn_conv_b, w_down, final_norm_g, loss_target, m_rel_bias, m_norm_mix_g, m_norm_mem_g, m_w_mem_kv, m_w_out, m_w_in_a, m_sinks_a, m_w_in_b, m_conv_qkv_b, m_a_log_b, m_dt_bias_b, m_out_norm_g_b, m_norm_ffn_g, m_w_gate_up, m_ffn_conv_w, m_ffn_conv_b, m_w_down, m_final_norm_g, v_rel_bias, v_norm_mix_g, v_norm_mem_g, v_w_mem_kv, v_w_out, v_w_in_a, v_sinks_a, v_w_in_b, v_conv_qkv_b, v_a_log_b, v_dt_bias_b, v_out_norm_g_b, v_norm_ffn_g, v_w_gate_up, v_ffn_conv_w, v_ffn_conv_b, v_w_down, v_final_norm_g)))
    chip = 2 * lax.axis_index("x") + lax.axis_index("y")
    core = lax.axis_index("c")
    n_down, n_out, n_mem = w_down.shape[1], w_out.shape[1], w_mem_kv.shape[1]

    shards = [w_gate_up[0], w_gate_up[1],
              jnp.concatenate([w_down.reshape(2 * n_down, D), w_out.reshape(2 * n_out, D)], axis=0),
              w_mem_kv.reshape(2 * n_mem, 2 * X_Q), w_in_a[0], w_in_b[0]]
    def own_slot(shard):
        return lax.dynamic_update_index_in_dim(lax.empty((4,) + shard.shape, shard.dtype), shard, chip, 0)

    got = gather_weights([own_slot(s.astype(bf16)) for s in shards],
                         [own_slot(conv_qkv_b[0]), own_slot(ffn_conv_w.reshape(6, -1))])
    g_gu0, g_gu1, g_rows, g_mem, g_ina, g_inb, g_cq, g_cf = got

    def layers(slab, n):
        return jnp.moveaxis(slab.reshape(4, 2, n, slab.shape[-1]), 0, 1).reshape(2, 4 * n, slab.shape[-1])

    full = {n: A[n] for n in SMALL}
    full["w_down"] = layers(g_rows[:, :2 * n_down], n_down)
    full["w_out"] = layers(g_rows[:, 2 * n_down:], n_out)
    full["w_mem_kv"] = layers(g_mem, n_mem)
    full["w_in_a"] = _unchip_cols(g_ina)[None]
    full["w_in_b"] = _unchip_cols(g_inb)[None]
    full["conv_qkv_b"] = _unchip_cols(g_cq)[None]
    full["ffn_conv_w"] = _unchip_cols(g_cf).reshape(2, 3, D_FF)

    loss, dx, G = _local_step(x[0], mem[0], loss_target[0], _prepare(full, w_gu=[g_gu0, g_gu1]))
    gfull = _grads_to_ref(G)

    def chip_rows(g, n):
        return g.reshape(4, n, g.shape[-1])

    partial = [G["w_gu"][0], G["w_gu"][1],
               jnp.concatenate([chip_rows(G["w_down"][0], n_down), chip_rows(G["w_down"][1], n_down),
                                chip_rows(_unlay_out_a(G["w_out"][0]), n_out), chip_rows(G["w_out"][1], n_out)],
                               axis=1).astype(bf16),
               jnp.concatenate([chip_rows(G["w_mem"][0], n_mem), chip_rows(G["w_mem"][1], n_mem)], axis=1).astype(bf16),
               _chip_cols(_unlay_in_a(G["w_in_a"])).astype(bf16), _chip_cols(_unlay_in_b(G["w_in_b"])).astype(bf16)]
    theirs = pair_exchange(partial)
    names = ("gu0", "gu1", "rows", "mem", "in_a", "in_b")
    pair = [pair_sum(p, t, core, "pair_sum_" + nm) for p, t, nm in zip(partial, theirs, names)]
    arrived = chip_scatter(pair)
    halves = [sum_slots(p, r, chip, core, "sum_slots_" + nm) for p, r, nm in zip(pair, arrived, names)]
    f_gu0, f_gu1, f_rows, f_mem, f_ina, f_inb = final_exchange(halves)

    sm_shapes = [A[n].shape for n in SMALL] + [gfull[n].shape for n, _ in CONV] + [(LANE,)]
    sm_rows = _pad_to(sum(_nrows(s, LANE) for s in sm_shapes), 8)
    sbuf = _pack([gfull[n] for n in SMALL] + [gfull[n] for n, _ in CONV] + [loss[0]], LANE, sm_rows, f32)
    tot = _unpack(allreduce_small(sbuf), sm_shapes, LANE)
    gsmall = dict(zip(SMALL, tot[:len(SMALL)]))
    for (n, axis), t in zip(CONV, tot[len(SMALL):len(SMALL) + len(CONV)]):
        sh = A[n].shape[axis]
        gsmall[n] = lax.dynamic_slice_in_dim(t, chip * sh, sh, axis)
    loss_out = tot[-1][0]

    out = {}
    plan = (("w_gate_up", (2, D, GU_SHARD), [f_gu0, f_gu1], 0), ("w_down", (1, 2 * n_down, D), [f_rows], 0),
            ("w_out", (1, 2 * n_out, D), [f_rows], 2 * n_down), ("w_mem_kv", (1, 2 * n_mem, 2 * X_Q), [f_mem], 0),
            ("w_in_a", (1, D, IN_A // 4), [f_ina], 0), ("w_in_b", (1, D, IN_B // 4), [f_inb], 0))
    for n, shape3, gs, row0 in plan:
        res = adamw_big(A[n].reshape(shape3), A["m_" + n].reshape(shape3), A["v_" + n].reshape(shape3), gs, row0,
                        "adamw_" + n)
        for key, r in zip(("grad_", "delta_", "new_m_", "new_v_"), res):
            out[key + n] = r.reshape(A[n].shape)
    names = SMALL + tuple(n for n, _ in CONV)
    shapes = [A[n].shape for n in names]
    rows = _pad_to(sum(_nrows(s, LANE) for s in shapes), 8)
    packs = [_pack([src[n] for n in names], LANE, rows, f32)
             for src in ({n: A[n] for n in names}, {n: A["m_" + n] for n in names}, {n: A["v_" + n] for n in names}, gsmall)]
    res = adamw_small(*packs)
    for key, r in zip(("delta_", "new_m_", "new_v_"), res):
        for n, a in zip(names, _unpack(r, shapes, LANE)):
            out[key + n] = a
    for n in names:
        out["grad_" + n] = gsmall[n]
    return (loss_out, dx[None], *[out["grad_" + n] for n in WEIGHTS], *[out["delta_" + n] for n in WEIGHTS],
            *[out["new_m_" + n] for n in WEIGHTS], *[out["new_v_" + n] for n in WEIGHTS])
```

```python
import functools
import math

import numpy as np
import jax
import jax.numpy as jnp
from jax import lax
from jax.experimental import pallas as pl
from jax.experimental.pallas import tpu as pltpu

f32 = jnp.float32
bf16 = jnp.bfloat16
HI = lax.Precision.HIGHEST
MESH = pl.DeviceIdType.MESH

D = 1024
MEM_LEN = 256
EPS = 1e-6
A_HEADS, A_KV, A_DH = 12, 2, 64
A_Q = 768
BLK = 128
N_BUCKETS, MAX_DIST = 32, 128
B_QK, B_V, B_DH = 384, 768, 128
B_QKV = 1536
CHUNK = 64
X_Q = 256
D_FF = 2816
IN_A = 1280
IN_B = 2572
IN_B_PAD = 2688
LANE = 128
VMEM_LIMIT = 56 * 1024 * 1024

LR, B1, B2, AEPS, WD, STEP = 0.001, 0.9, 0.999, 1e-08, 0.01, 10


def _cp(sem=None):
    return pltpu.CompilerParams(dimension_semantics=sem, vmem_limit_bytes=VMEM_LIMIT)


def _dg(a, b, ca, cb, prec=None):
    return lax.dot_general(a, b, (((ca,), (cb,)), ((), ())), precision=prec, preferred_element_type=f32)


@jax.custom_vjp
def bdot(a, b):
    return _dg(a.astype(bf16), b.astype(bf16), 1, 0)


def _bdot_f(a, b):
    return bdot(a, b), (a, b)


def _bdot_b(res, g):
    a, b = res
    gb = g.astype(bf16)
    return _dg(gb, b.astype(bf16), 1, 1), _dg(a.astype(bf16), gb, 0, 0)


bdot.defvjp(_bdot_f, _bdot_b)


@jax.custom_vjp
def bdot_nt(a, b):
    return _dg(a.astype(bf16), b.astype(bf16), 1, 1)


def _bdot_nt_f(a, b):
    return bdot_nt(a, b), (a, b)


def _bdot_nt_b(res, g):
    a, b = res
    gb = g.astype(bf16)
    return _dg(gb, b.astype(bf16), 1, 0), _dg(gb, a.astype(bf16), 0, 0)


bdot_nt.defvjp(_bdot_nt_f, _bdot_nt_b)


def _shift_rows(x, s, down):
    n = x.shape[0]
    row = lax.broadcasted_iota(jnp.int32, x.shape, 0)
    if down:
        return jnp.where(row >= s, pltpu.roll(x, s, 0), 0.0)
    return jnp.where(row < n - s, pltpu.roll(x, n - s, 0), 0.0)


@functools.partial(jax.custom_vjp, nondiff_argnums=(1,))
def shift_down(x, s):
    return _shift_rows(x, s, True)


def _sd_f(x, s):
    return _shift_rows(x, s, True), None


def _sd_b(s, _, g):
    return (_shift_rows(g, s, False),)


shift_down.defvjp(_sd_f, _sd_b)


def _sigmoid(x):
    return 1.0 / (1.0 + jnp.exp(-x))


def _silu(x):
    return x * _sigmoid(x)


def _rms(x, g):
    return x * lax.rsqrt(jnp.mean(x * x, axis=-1, keepdims=True) + EPS) * g


def _tile(n, cap):
    u = n // LANE
    best = 1
    for d in range(1, u + 1):
        if u % d == 0 and d * LANE <= cap:
            best = d
    return best * LANE


def mm_nn(a, w, res=None, out_dtype=f32, name="mm_nn"):
    M, K = a.shape
    N = w.shape[1]
    tm, tn = min(512, M), _tile(N, 1024)

    def body(*refs):
        if res is None:
            a_ref, w_ref, o_ref = refs
            o_ref[...] = _dg(a_ref[...].astype(bf16), w_ref[...], 1, 0).astype(out_dtype)
        else:
            a_ref, w_ref, r_ref, o_ref = refs
            o_ref[...] = (r_ref[...] + _dg(a_ref[...].astype(bf16), w_ref[...], 1, 0)).astype(out_dtype)

    in_specs = [pl.BlockSpec((tm, K), lambda n, m: (m, 0)), pl.BlockSpec((K, tn), lambda n, m: (0, n))]
    args = [a, w]
    if res is not None:
        in_specs.append(pl.BlockSpec((tm, tn), lambda n, m: (m, n)))
        args.append(res)
    return pl.pallas_call(
        body, name=name, grid=(N // tn, M // tm), in_specs=in_specs,
        out_specs=pl.BlockSpec((tm, tn), lambda n, m: (m, n)),
        out_shape=jax.ShapeDtypeStruct((M, N), out_dtype),
        compiler_params=_cp(("parallel", "parallel")),
    )(*args)


def mm_nt(dy, w, name="mm_nt"):
    M, N = dy.shape
    K = w.shape[0]
    tm, tn = min(512, M), _tile(N, 1024)

    def body(dy_ref, w_ref, o_ref):
        @pl.when(pl.program_id(1) == 0)
        def _():
            o_ref[...] = jnp.zeros_like(o_ref)
        o_ref[...] += _dg(dy_ref[...].astype(bf16), w_ref[...], 1, 1)

    return pl.pallas_call(
        body, name=name, grid=(M // tm, N // tn),
        in_specs=[pl.BlockSpec((tm, tn), lambda m, n: (m, n)), pl.BlockSpec((K, tn), lambda m, n: (0, n))],
        out_specs=pl.BlockSpec((tm, K), lambda m, n: (m, 0)),
        out_shape=jax.ShapeDtypeStruct((M, K), f32),
        compiler_params=_cp(("parallel", "arbitrary")),
    )(dy, w)


def mm_tn(a, dy, name="mm_tn"):
    M, K = a.shape
    N = dy.shape[1]
    tm, tk, tn = min(512, M), _tile(K, 1408), _tile(N, 1024)

    def body(a_ref, dy_ref, o_ref):
        @pl.when(pl.program_id(2) == 0)
        def _():
            o_ref[...] = jnp.zeros_like(o_ref)
        o_ref[...] += _dg(a_ref[...].astype(bf16), dy_ref[...].astype(bf16), 0, 0)

    return pl.pallas_call(
        body, name=name, grid=(K // tk, N // tn, M // tm),
        in_specs=[pl.BlockSpec((tm, tk), lambda k, n, m: (m, k)), pl.BlockSpec((tm, tn), lambda k, n, m: (m, n))],
        out_specs=pl.BlockSpec((tk, tn), lambda k, n, m: (k, n)),
        out_shape=jax.ShapeDtypeStruct((K, N), f32),
        compiler_params=_cp(("parallel", "parallel", "arbitrary")),
    )(a, dy)


def rms_fwd(h, g, name):
    S = h.shape[0]
    t = min(512, S)

    def body(h_ref, g_ref, o_ref):
        o_ref[...] = _rms(h_ref[...], g_ref[...]).astype(bf16)

    return pl.pallas_call(
        body, name=name, grid=(S // t,),
        in_specs=[pl.BlockSpec((t, D), lambda i: (i, 0)), pl.BlockSpec((1, D), lambda i: (0, 0))],
        out_specs=pl.BlockSpec((t, D), lambda i: (i, 0)),
        out_shape=jax.ShapeDtypeStruct((S, D), bf16),
        compiler_params=_cp(("parallel",)),
    )(h, g.reshape(1, D))


def rms_bwd(h, g, dn, dres, name):
    S = h.shape[0]
    t = min(512, S)

    def body(h_ref, g_ref, dn_ref, dr_ref, dh_ref, dg_ref):
        @pl.when(pl.program_id(0) == 0)
        def _():
            dg_ref[...] = jnp.zeros_like(dg_ref)
        _, vjp = jax.vjp(_rms, h_ref[...], g_ref[...])
        dh, dg = vjp(dn_ref[...])
        dh_ref[...] = dr_ref[...] + dh
        dg_ref[...] += dg

    tok = pl.BlockSpec((t, D), lambda i: (i, 0))
    vec = pl.BlockSpec((1, D), lambda i: (0, 0))
    return pl.pallas_call(
        body, name=name, grid=(S // t,), in_specs=[tok, vec, tok, tok], out_specs=[tok, vec],
        out_shape=[jax.ShapeDtypeStruct((S, D), f32), jax.ShapeDtypeStruct((1, D), f32)],
        compiler_params=_cp(("arbitrary",)),
    )(h, g.reshape(1, D), dn, dres)


def loss_head(h, g, target):
    S = h.shape[0]
    t = min(512, S)

    def f(hh, gg, tt):
        err = _rms(hh, gg) - tt
        return 0.5 * jnp.sum(jnp.mean(err * err, axis=-1, keepdims=True), axis=0, keepdims=True)

    def body(h_ref, g_ref, t_ref, loss_ref, dh_ref, dg_ref):
        @pl.when(pl.program_id(0) == 0)
        def _():
            dg_ref[...] = jnp.zeros_like(dg_ref)
            loss_ref[...] = jnp.zeros_like(loss_ref)
        val, vjp = jax.vjp(lambda a, b: f(a, b, t_ref[...]), h_ref[...], g_ref[...])
        dh, dg = vjp(jnp.ones((1, 1), f32))
        dh_ref[...] = dh
        dg_ref[...] += dg
        loss_ref[...] += jnp.broadcast_to(val, loss_ref.shape)

    tok = pl.BlockSpec((t, D), lambda i: (i, 0))
    vec = pl.BlockSpec((1, D), lambda i: (0, 0))
    return pl.pallas_call(
        body, name="loss_head", grid=(S // t,), in_specs=[tok, vec, tok],
        out_specs=[pl.BlockSpec((1, LANE), lambda i: (0, 0)), tok, vec],
        out_shape=[jax.ShapeDtypeStruct((1, LANE), f32), jax.ShapeDtypeStruct((S, D), f32),
                   jax.ShapeDtypeStruct((1, D), f32)],
        compiler_params=_cp(("arbitrary",)),
    )(h, g.reshape(1, D), target)


def memkv_fwd(mem, g, w, name):
    def body(m_ref, g_ref, w_ref, o_ref):
        o_ref[...] = _dg(_rms(m_ref[...], g_ref[...]).astype(bf16), w_ref[...], 1, 0)

    return pl.pallas_call(
        body, name=name, out_shape=jax.ShapeDtypeStruct((MEM_LEN, 2 * X_Q), f32), compiler_params=_cp(),
    )(mem, g.reshape(1, D), w)


def memkv_bwd(mem, g, w, dkv, name):
    def body(m_ref, g_ref, w_ref, d_ref, dg_ref, dw_ref):
        n, vjp = jax.vjp(lambda gg: _rms(m_ref[...], gg), g_ref[...])
        db = d_ref[...].astype(bf16)
        dw_ref[...] = _dg(n.astype(bf16), db, 0, 0)
        dg_ref[...] = vjp(_dg(db, w_ref[...], 1, 1))[0]

    return pl.pallas_call(
        body, name=name,
        out_shape=[jax.ShapeDtypeStruct((1, D), f32), jax.ShapeDtypeStruct((D, 2 * X_Q), f32)],
        compiler_params=_cp(),
    )(mem, g.reshape(1, D), w, dkv)


def _xattn_f(xq, mk, mv):
    lane = lax.broadcasted_iota(jnp.int32, (1, X_Q), 1)
    out = jnp.zeros(xq.shape, f32)
    for hd in range(4):
        msk = (lane // 64 == hd).astype(f32)
        s = bdot_nt(xq * msk, mk) * (64 ** -0.5)
        m = lax.stop_gradient(jnp.max(s, axis=-1, keepdims=True))
        p = jnp.exp(s - m)
        p = p / jnp.sum(p, axis=-1, keepdims=True)
        out = out + bdot(p, mv * msk)
    return out


def xattn_fwd(proj, col, kv, name):
    S = proj.shape[0]
    t = min(512, S)
    cb = col // X_Q

    def body(q_ref, k_ref, v_ref, o_ref):
        o_ref[...] = _xattn_f(q_ref[...], k_ref[...], v_ref[...])

    return pl.pallas_call(
        body, name=name, grid=(S // t,),
        in_specs=[pl.BlockSpec((t, X_Q), lambda i: (i, cb)), pl.BlockSpec((MEM_LEN, X_Q), lambda i: (0, 0)),
                  pl.BlockSpec((MEM_LEN, X_Q), lambda i: (0, 1))],
        out_specs=pl.BlockSpec((t, X_Q), lambda i: (i, 0)),
        out_shape=jax.ShapeDtypeStruct((S, X_Q), f32),
        compiler_params=_cp(("parallel",)),
    )(proj, kv, kv)


def xattn_bwd(proj, col, kv, dmix, name):
    S = proj.shape[0]
    t = min(512, S)
    cb = col // X_Q

    def body(q_ref, k_ref, v_ref, do_ref, dq_ref, dk_ref, dv_ref):
        @pl.when(pl.program_id(0) == 0)
        def _():
            dk_ref[...] = jnp.zeros_like(dk_ref)
            dv_ref[...] = jnp.zeros_like(dv_ref)
        _, vjp = jax.vjp(_xattn_f, q_ref[...], k_ref[...], v_ref[...])
        dq, dk, dv = vjp(do_ref[...])
        dq_ref[...] = dq
        dk_ref[...] += dk
        dv_ref[...] += dv

    kvb = pl.BlockSpec((MEM_LEN, X_Q), lambda i: (0, 0))
    dq, dk, dv = pl.pallas_call(
        body, name=name, grid=(S // t,),
        in_specs=[pl.BlockSpec((t, X_Q), lambda i: (i, cb)), kvb,
                  pl.BlockSpec((MEM_LEN, X_Q), lambda i: (0, 1)), pl.BlockSpec((t, X_Q), lambda i: (i, 3))],
        out_specs=[pl.BlockSpec((t, X_Q), lambda i: (i, 0)), kvb, kvb],
        out_shape=[jax.ShapeDtypeStruct((S, X_Q), f32), jax.ShapeDtypeStruct((MEM_LEN, X_Q), f32),
                   jax.ShapeDtypeStruct((MEM_LEN, X_Q), f32)],
        compiler_params=_cp(("arbitrary",)),
    )(proj, kv, kv, dmix)
    return dq, jnp.concatenate([dk, dv], axis=1)


def _bucket_map():
    qi = np.arange(BLK)[:, None]
    kj = np.arange(2 * BLK)[None, :]
    n = np.maximum(BLK + qi - kj, 0)
    max_exact = N_BUCKETS // 2
    nf = np.maximum(n, 1).astype(np.float64)
    large = max_exact + (np.log(nf / max_exact) / math.log(MAX_DIST / max_exact)
                         * (N_BUCKETS - max_exact)).astype(np.int32)
    large = np.minimum(large, N_BUCKETS - 1)
    return np.where(n < max_exact, n, large).astype(np.int32)


def bias_build(rel_bias):
    def body(rb_ref, bk_ref, o_ref):
        bk = bk_ref[...]
        for h in range(A_HEADS):
            acc = jnp.zeros((BLK, 2 * BLK), f32)
            for b in range(N_BUCKETS):
                acc = jnp.where(bk == b, rb_ref[b, h], acc)
            o_ref[h] = acc

    return pl.pallas_call(
        body, name="bias_build",
        in_specs=[pl.BlockSpec(memory_space=pltpu.SMEM), pl.BlockSpec(memory_space=pltpu.VMEM)],
        out_specs=pl.BlockSpec(memory_space=pltpu.VMEM),
        out_shape=jax.ShapeDtypeStruct((A_HEADS, BLK, 2 * BLK), f32), compiler_params=_cp(),
    )(rel_bias, jnp.asarray(_bucket_map()))


def bias_grad(dbias):
    def body(d_ref, bk_ref, o_ref):
        bk = bk_ref[...]
        row = lax.broadcasted_iota(jnp.int32, (N_BUCKETS, LANE), 0)
        lane = lax.broadcasted_iota(jnp.int32, (N_BUCKETS, LANE), 1)
        acc = jnp.zeros((N_BUCKETS, LANE), f32)
        for h in range(A_HEADS):
            d = d_ref[h]
            for b in range(N_BUCKETS):
                s = jnp.sum(jnp.where(bk == b, d, 0.0), keepdims=True)
                acc = acc + jnp.where((row == b) & (lane == h), s, 0.0)
        o_ref[...] = acc

    return pl.pallas_call(
        body, name="bias_grad", out_shape=jax.ShapeDtypeStruct((N_BUCKETS, LANE), f32), compiler_params=_cp(),
    )(dbias, jnp.asarray(_bucket_map()))


def _swa_f(qb, kp, kc, vp, vc, bias, sk, first):
    kband = jnp.concatenate([kp, kc], axis=0)
    vband = jnp.concatenate([vp, vc], axis=0)
    qi = lax.broadcasted_iota(jnp.int32, (BLK, 2 * BLK), 0)
    kj = lax.broadcasted_iota(jnp.int32, (BLK, 2 * BLK), 1)
    rel = kj - qi
    ok = (rel >= 1) & (rel <= BLK) & ((kj >= BLK) | jnp.logical_not(first))
    lane = lax.broadcasted_iota(jnp.int32, (1, LANE), 1)
    lane_b = lax.broadcasted_iota(jnp.int32, (BLK, LANE), 1)
    outs = []
    for p in range(A_HEADS // 2):
        qp = qb[:, LANE * p:LANE * (p + 1)]
        acc = jnp.zeros((BLK, LANE), f32)
        for g in range(2):
            h = g * (A_HEADS // 2) + p
            msk = (lane // A_DH == g).astype(f32)
            s = bdot_nt(qp * msk, kband) * (A_DH ** -0.5) + bias[h]
            s = jnp.where(ok, s, -1e30)
            skb = jnp.broadcast_to(sk[h:h + 1, :], (BLK, LANE))
            sink = jnp.sum(jnp.where(lane_b == 0, skb, 0.0), axis=-1, keepdims=True)
            m = lax.stop_gradient(jnp.maximum(jnp.max(s, axis=-1, keepdims=True), sink))
            e = jnp.exp(s - m)
            prob = e / (jnp.sum(e, axis=-1, keepdims=True) + jnp.exp(sink - m))
            acc = acc + bdot(prob, vband) * msk
        outs.append(acc)
    return jnp.concatenate(outs, axis=1)


def _swa_specs(nb, rev):
    bi = (lambda i: nb - 1 - i) if rev else (lambda i: i)
    return [
        pl.BlockSpec((BLK, A_Q), lambda i: (bi(i), 0)),
        pl.BlockSpec((BLK, LANE), lambda i: (jnp.maximum(bi(i) - 1, 0), 6)),
        pl.BlockSpec((BLK, LANE), lambda i: (bi(i), 6)),
        pl.BlockSpec((BLK, LANE), lambda i: (jnp.maximum(bi(i) - 1, 0), 7)),
        pl.BlockSpec((BLK, LANE), lambda i: (bi(i), 7)),
        pl.BlockSpec((A_HEADS, BLK, 2 * BLK), lambda i: (0, 0, 0)),
        pl.BlockSpec((16, LANE), lambda i: (0, 0)),
    ]


def swa_fwd(proj, bias, sk):
    S = proj.shape[0]
    nb = S // BLK

    def body(q_ref, kp_ref, kc_ref, vp_ref, vc_ref, b_ref, s_ref, o_ref):
        o_ref[...] = _swa_f(q_ref[...], kp_ref[...], kc_ref[...], vp_ref[...], vc_ref[...], b_ref[...], s_ref[...],
                            pl.program_id(0) == 0)

    return pl.pallas_call(
        body, name="swa_fwd", grid=(nb,), in_specs=_swa_specs(nb, False),
        out_specs=pl.BlockSpec((BLK, A_Q), lambda i: (i, 0)),
        out_shape=jax.ShapeDtypeStruct((S, A_Q), f32), compiler_params=_cp(("parallel",)),
    )(proj, proj, proj, proj, proj, bias, sk)


def swa_bwd(proj, bias, sk, dmix):
    S = proj.shape[0]
    nb = S // BLK

    def body(q_ref, kp_ref, kc_ref, vp_ref, vc_ref, b_ref, s_ref, do_ref, dqkv_ref, db_ref, ds_ref, ck, cv):
        i = pl.program_id(0)

        @pl.when(i == 0)
        def _():
            db_ref[...] = jnp.zeros_like(db_ref)
            ds_ref[...] = jnp.zeros_like(ds_ref)
            ck[...] = jnp.zeros_like(ck)
            cv[...] = jnp.zeros_like(cv)
        first = i == nb - 1
        _, vjp = jax.vjp(lambda *a: _swa_f(*a, first), q_ref[...], kp_ref[...], kc_ref[...], vp_ref[...],
                         vc_ref[...], b_ref[...], s_ref[...])
        dq, dkp, dkc, dvp, dvc, db, ds = vjp(do_ref[...])
        dqkv_ref[...] = jnp.concatenate([dq, dkc + ck[...], dvc + cv[...]], axis=1)
        ck[...] = dkp
        cv[...] = dvp
        db_ref[...] += db
        ds_ref[...] += ds

    return pl.pallas_call(
        body, name="swa_bwd", grid=(nb,),
        in_specs=_swa_specs(nb, True) + [pl.BlockSpec((BLK, A_Q), lambda i: (nb - 1 - i, 0))],
        out_specs=[pl.BlockSpec((BLK, D), lambda i: (nb - 1 - i, 0)),
                   pl.BlockSpec((A_HEADS, BLK, 2 * BLK), lambda i: (0, 0, 0)),
                   pl.BlockSpec((16, LANE), lambda i: (0, 0))],
        out_shape=[jax.ShapeDtypeStruct((S, D), f32), jax.ShapeDtypeStruct((A_HEADS, BLK, 2 * BLK), f32),
                   jax.ShapeDtypeStruct((16, LANE), f32)],
        scratch_shapes=[pltpu.VMEM((BLK, LANE), f32), pltpu.VMEM((BLK, LANE), f32)],
        compiler_params=_cp(("arbitrary",)),
    )(proj, proj, proj, proj, proj, bias, sk, dmix)


def _dnprep_f(x, w, is_qk):
    c = (w[3:4] * x + w[2:3] * shift_down(x, 1) + w[1:2] * shift_down(x, 2) + w[0:1] * shift_down(x, 3))
    a = _silu(c)
    n = a * lax.rsqrt(jnp.sum(a * a, axis=-1, keepdims=True) + EPS)
    return jnp.where(is_qk, n, a)


def dnprep_fwd(proj, cw):
    S = proj.shape[0]
    nblk = B_QKV // LANE

    def body(x_ref, w_ref, o_ref):
        o_ref[...] = _dnprep_f(x_ref[...], w_ref[...], pl.program_id(0) < 2 * B_QK // LANE)

    return pl.pallas_call(
        body, name="dnprep_fwd", grid=(nblk,),
        in_specs=[pl.BlockSpec((S, LANE), lambda j: (0, j)), pl.BlockSpec((4, LANE), lambda j: (0, j))],
        out_specs=pl.BlockSpec((S, LANE), lambda j: (0, j)),
        out_shape=jax.ShapeDtypeStruct((S, B_QKV), f32), compiler_params=_cp(("parallel",)),
    )(proj, cw)


def dnprep_bwd(proj, cw, dqkvn):
    S = proj.shape[0]
    nblk = B_QKV // LANE

    def body(x_ref, w_ref, d_ref, dx_ref, dw_ref):
        is_qk = pl.program_id(0) < 2 * B_QK // LANE
        _, vjp = jax.vjp(lambda a, b: _dnprep_f(a, b, is_qk), x_ref[...], w_ref[...])
        dx, dw = vjp(d_ref[...])
        dx_ref[...] = dx
        dw_ref[...] = dw

    col = pl.BlockSpec((S, LANE), lambda j: (0, j))
    wsp = pl.BlockSpec((4, LANE), lambda j: (0, j))
    return pl.pallas_call(
        body, name="dnprep_bwd", grid=(nblk,), in_specs=[col, wsp, col], out_specs=[col, wsp],
        out_shape=[jax.ShapeDtypeStruct((S, B_QKV), f32), jax.ShapeDtypeStruct((4, B_QKV), f32)],
        compiler_params=_cp(("parallel",)),
    )(proj, cw, dqkvn)


def _hdot(a, b, ca=1, cb=0):
    return _dg(a, b, ca, cb, HI)


def _bdg(a, b, ca, cb):
    dn = (((ca,), (cb,)), ((0,), (0,)))
    ah, bh = a.astype(bf16), b.astype(bf16)
    al, bl = (a - ah.astype(f32)).astype(bf16), (b - bh.astype(f32)).astype(bf16)
    return (lax.dot_general(ah, bh, dn, preferred_element_type=f32)
            + lax.dot_general(ah, bl, dn, preferred_element_type=f32)
            + lax.dot_general(al, bh, dn, preferred_element_type=f32))


@jax.custom_vjp
def hbd(a, b):
    return _bdg(a, b, 2, 1)


@jax.custom_vjp
def hbd_nt(a, b):
    return _bdg(a, b, 2, 2)


@jax.custom_vjp
def hbd_tn(a, b):
    return _bdg(a, b, 1, 1)


hbd.defvjp(lambda a, b: (hbd(a, b), (a, b)), lambda r, g: (hbd_nt(g, r[1]), hbd_tn(r[0], g)))
hbd_nt.defvjp(lambda a, b: (hbd_nt(a, b), (a, b)), lambda r, g: (hbd(g, r[1]), hbd_tn(g, r[0])))
hbd_tn.defvjp(lambda a, b: (hbd_tn(a, b), (a, b)), lambda r, g: (hbd_nt(r[1], g), hbd(r[0], g)))


def _stack(xs):
    return jnp.concatenate([x[None] for x in xs], axis=0)


def _lane_col(x, j):
    lane = lax.broadcasted_iota(jnp.int32, (1, LANE), 1)
    return jnp.sum(jnp.where(lane == j, x, 0.0), axis=-1, keepdims=True)


def _dnc_f(q, k, v, seg, prm):
    C = CHUNK
    beta_all = _sigmoid(seg)
    xx = seg + prm[1:2]
    g_all = -jnp.exp(prm[0:1]) * (jnp.maximum(xx, 0.0) + jnp.log(1.0 + jnp.exp(-jnp.abs(xx))))
    r2 = lax.broadcasted_iota(jnp.int32, (C, C), 0)
    c2 = lax.broadcasted_iota(jnp.int32, (C, C), 1)
    gc_all = _hdot((r2 >= c2).astype(f32), g_all)
    beta = _stack([_lane_col(beta_all, h) for h in range(6)])
    gc = _stack([_lane_col(gc_all, 6 + h) for h in range(6)])
    r = lax.broadcasted_iota(jnp.int32, (1, C, C), 1)
    c = lax.broadcasted_iota(jnp.int32, (1, C, C), 2)
    incl = r >= c
    strict = r > c
    eye = (r == c).astype(f32)
    g_row = hbd(jnp.ones((6, C, C), f32), eye * gc)
    decay = jnp.where(incl, jnp.exp(jnp.where(incl, gc - g_row, 0.0)), 0.0)
    a_mat = beta * hbd_nt(k, k) * jnp.where(strict, decay, 0.0)
    eg = jnp.exp(gc)
    pw = -a_mat
    inv = eye + pw
    for _ in range(5):
        pw = hbd(pw, pw)
        inv = inv + hbd(inv, pw)
    u = hbd(inv, beta * v)
    w = hbd(inv, (beta * eg) * k)
    qc = q * (B_DH ** -0.5)
    attn = hbd_nt(qc, k) * decay
    last = (lax.broadcasted_iota(jnp.int32, (1, C, 1), 1) == C - 1).astype(f32)
    g_last = jnp.sum(gc * last, axis=1, keepdims=True)
    dc = jnp.broadcast_to(jnp.exp(g_last), (6, 1, LANE)).reshape(6, LANE)
    return u, w, qc * eg, k * jnp.exp(g_last - gc), attn, dc


def _dns_f(S0, u, w, qd, kt, attn, dcrows):
    dc = _lane_col(dcrows, 0).reshape(6, 1, 1)
    delta = u - hbd(w, S0)
    out = hbd(qd, S0) + hbd(attn, delta)
    return out, dc * S0 + hbd_tn(kt, delta)


def _dnpost_f(o, z, grow):
    outs = []
    for h in range(6):
        oh = o[:, LANE * h:LANE * (h + 1)]
        outs.append(oh * lax.rsqrt(jnp.mean(oh * oh, axis=-1, keepdims=True) + EPS) * grow
                    * _silu(z[:, LANE * h:LANE * (h + 1)]))
    return jnp.concatenate(outs, axis=1)


def _hs(h):
    return slice(LANE * h, LANE * (h + 1))


def _heads(ref, share):
    return _stack([ref[:, _hs(h // share)] for h in range(6)])


def _put_heads(ref, val):
    for h in range(6):
        ref[:, _hs(h)] = val[h]


def _dnc_in_specs():
    return [
        pl.BlockSpec((CHUNK, B_QK), lambda n: (n, 0)),
        pl.BlockSpec((CHUNK, B_QK), lambda n: (n, 1)),
        pl.BlockSpec((CHUNK, B_V), lambda n: (n, 1)),
        pl.BlockSpec((CHUNK, LANE), lambda n: (n, 20)),
        pl.BlockSpec((8, LANE), lambda n: (0, 0)),
    ]


def _dnc_out_specs(rev_nc=None):
    ci = (lambda n: n) if rev_nc is None else (lambda n: rev_nc - 1 - n)
    wide = pl.BlockSpec((CHUNK, B_V), lambda n: (ci(n), 0))
    return [wide, wide, wide, wide, pl.BlockSpec((1, 6, CHUNK, CHUNK), lambda n: (ci(n), 0, 0, 0)),
            pl.BlockSpec((1, 8, LANE), lambda n: (ci(n), 0, 0))]


def _dnc_shapes(S):
    nc = S // CHUNK
    wide = jax.ShapeDtypeStruct((S, B_V), f32)
    return [wide, wide, wide, wide, jax.ShapeDtypeStruct((nc, 6, CHUNK, CHUNK), f32),
            jax.ShapeDtypeStruct((nc, 8, LANE), f32)]


def dnc_fwd(qkvn, proj, prm):
    S = proj.shape[0]

    def body(q_ref, k_ref, v_ref, s_ref, p_ref, u_ref, w_ref, qd_ref, kt_ref, at_ref, dc_ref):
        u, w, qd, kt, attn, dc = _dnc_f(_heads(q_ref, 2), _heads(k_ref, 2), _heads(v_ref, 1), s_ref[...], p_ref[...])
        _put_heads(u_ref, u)
        _put_heads(w_ref, w)
        _put_heads(qd_ref, qd)
        _put_heads(kt_ref, kt)
        at_ref[0] = attn
        dc_ref[0] = jnp.concatenate([dc, jnp.zeros((2, LANE), f32)], axis=0)

    return pl.pallas_call(
        body, name="dn_chunk_fwd", grid=(S // CHUNK,), in_specs=_dnc_in_specs(), out_specs=_dnc_out_specs(),
        out_shape=_dnc_shapes(S), compiler_params=_cp(("parallel",)),
    )(qkvn, qkvn, qkvn, proj, prm)


def dnc_bwd(qkvn, proj, prm, cots):
    S = proj.shape[0]

    def body(q_ref, k_ref, v_ref, s_ref, p_ref, du_ref, dw_ref, dqd_ref, dkt_ref, dat_ref, ddc_ref,
             dx_ref, dseg_ref, dprm_ref):
        @pl.when(pl.program_id(0) == 0)
        def _():
            dprm_ref[...] = jnp.zeros_like(dprm_ref)
        _, vjp = jax.vjp(_dnc_f, _heads(q_ref, 2), _heads(k_ref, 2), _heads(v_ref, 1), s_ref[...], p_ref[...])
        dq, dk, dv, dseg, dprm = vjp((_heads(du_ref, 1), _heads(dw_ref, 1), _heads(dqd_ref, 1), _heads(dkt_ref, 1),
                                      dat_ref[0], ddc_ref[0, 0:6, :]))
        dx_ref[...] = jnp.concatenate([dq[0] + dq[1], dq[2] + dq[3], dq[4] + dq[5],
                                       dk[0] + dk[1], dk[2] + dk[3], dk[4] + dk[5]] + [dv[h] for h in range(6)], axis=1)
        dseg_ref[...] = dseg
        dprm_ref[...] += dprm

    return pl.pallas_call(
        body, name="dn_chunk_bwd", grid=(S // CHUNK,), in_specs=_dnc_in_specs() + _dnc_out_specs(),
        out_specs=[pl.BlockSpec((CHUNK, B_QKV), lambda n: (n, 0)), pl.BlockSpec((CHUNK, LANE), lambda n: (n, 0)),
                   pl.BlockSpec((8, LANE), lambda n: (0, 0))],
        out_shape=[jax.ShapeDtypeStruct((S, B_QKV), f32), jax.ShapeDtypeStruct((S, LANE), f32),
                   jax.ShapeDtypeStruct((8, LANE), f32)],
        compiler_params=_cp(("arbitrary",)),
    )(qkvn, qkvn, qkvn, proj, prm, *cots)


def dns_fwd(chunked):
    u = chunked[0]
    S = u.shape[0]
    nc = S // CHUNK

    def body(u_ref, w_ref, qd_ref, kt_ref, at_ref, dc_ref, o_ref, st_ref, st):
        @pl.when(pl.program_id(0) == 0)
        def _():
            st[...] = jnp.zeros_like(st)
        S0 = st[...]
        st_ref[0] = S0
        out, S1 = _dns_f(S0, _heads(u_ref, 1), _heads(w_ref, 1), _heads(qd_ref, 1), _heads(kt_ref, 1),
                         at_ref[0], dc_ref[0, 0:6, :])
        _put_heads(o_ref, out)
        st[...] = S1

    return pl.pallas_call(
        body, name="dn_scan_fwd", grid=(nc,), in_specs=_dnc_out_specs(),
        out_specs=[pl.BlockSpec((CHUNK, B_V), lambda n: (n, 0)),
                   pl.BlockSpec((1, 6, B_DH, B_DH), lambda n: (n, 0, 0, 0))],
        out_shape=[jax.ShapeDtypeStruct((S, B_V), f32), jax.ShapeDtypeStruct((nc, 6, B_DH, B_DH), f32)],
        scratch_shapes=[pltpu.VMEM((6, B_DH, B_DH), f32)],
        compiler_params=_cp(("arbitrary",)),
    )(*chunked)


def dns_bwd(chunked, states, do):
    S = do.shape[0]
    nc = S // CHUNK

    def body(u_ref, w_ref, qd_ref, kt_ref, at_ref, dc_ref, st_ref, do_ref,
             du_ref, dw_ref, dqd_ref, dkt_ref, dat_ref, ddc_ref, dst):
        @pl.when(pl.program_id(0) == 0)
        def _():
            dst[...] = jnp.zeros_like(dst)
        _, vjp = jax.vjp(_dns_f, st_ref[0], _heads(u_ref, 1), _heads(w_ref, 1), _heads(qd_ref, 1), _heads(kt_ref, 1),
                         at_ref[0], dc_ref[0, 0:6, :])
        dS0, du, dw, dqd, dkt, dat, ddc = vjp((_heads(do_ref, 1), dst[...]))
        dst[...] = dS0
        _put_heads(du_ref, du)
        _put_heads(dw_ref, dw)
        _put_heads(dqd_ref, dqd)
        _put_heads(dkt_ref, dkt)
        dat_ref[0] = dat
        ddc_ref[0] = jnp.concatenate([ddc, jnp.zeros((2, LANE), f32)], axis=0)

    return pl.pallas_call(
        body, name="dn_scan_bwd", grid=(nc,),
        in_specs=_dnc_out_specs(nc) + [pl.BlockSpec((1, 6, B_DH, B_DH), lambda n: (nc - 1 - n, 0, 0, 0)),
                                       pl.BlockSpec((CHUNK, B_V), lambda n: (nc - 1 - n, 0))],
        out_specs=_dnc_out_specs(nc), out_shape=_dnc_shapes(S),
        scratch_shapes=[pltpu.VMEM((6, B_DH, B_DH), f32)],
        compiler_params=_cp(("arbitrary",)),
    )(*chunked, states, do)


def dnpost_fwd(o, proj, prm):
    S = o.shape[0]
    t = min(512, S)

    def body(o_ref, z_ref, p_ref, y_ref):
        y_ref[...] = _dnpost_f(o_ref[...], z_ref[...], p_ref[2:3, :])

    tok = pl.BlockSpec((t, B_V), lambda i: (i, 0))
    return pl.pallas_call(
        body, name="dn_post_fwd", grid=(S // t,),
        in_specs=[tok, pl.BlockSpec((t, B_V), lambda i: (i, 2)), pl.BlockSpec((8, LANE), lambda i: (0, 0))],
        out_specs=tok, out_shape=jax.ShapeDtypeStruct((S, B_V), f32), compiler_params=_cp(("parallel",)),
    )(o, proj, prm)


def dnpost_bwd(o, proj, prm, dmix):
    S = o.shape[0]
    t = min(512, S)

    def body(o_ref, z_ref, p_ref, dy_ref, do_ref, dz_ref, dg_ref):
        @pl.when(pl.program_id(0) == 0)
        def _():
            dg_ref[...] = jnp.zeros_like(dg_ref)
        _, vjp = jax.vjp(_dnpost_f, o_ref[...], z_ref[...], p_ref[2:3, :])
        do, dz, dg = vjp(dy_ref[...])
        do_ref[...] = do
        dz_ref[...] = dz
        dg_ref[...] += dg

    tok = pl.BlockSpec((t, B_V), lambda i: (i, 0))
    return pl.pallas_call(
        body, name="dn_post_bwd", grid=(S // t,),
        in_specs=[tok, pl.BlockSpec((t, B_V), lambda i: (i, 2)), pl.BlockSpec((8, LANE), lambda i: (0, 0)), tok],
        out_specs=[tok, tok, pl.BlockSpec((1, LANE), lambda i: (0, 0))],
        out_shape=[jax.ShapeDtypeStruct((S, B_V), f32), jax.ShapeDtypeStruct((S, B_V), f32),
                   jax.ShapeDtypeStruct((1, LANE), f32)],
        compiler_params=_cp(("arbitrary",)),
    )(o, proj, prm, dmix)


N_FF_BLK = D_FF // LANE
GU_SHARD = 2 * D_FF // 4


def _glu_f(gate, up, w, b):
    c = w[2:3] * gate + w[1:2] * shift_down(gate, 1) + w[0:1] * shift_down(gate, 2) + b
    return _silu(c) * up


def glu_fwd(gu, w, b, name):
    S = gu.shape[0]

    def body(g_ref, u_ref, w_ref, b_ref, o_ref):
        o_ref[...] = _glu_f(g_ref[...], u_ref[...], w_ref[...], b_ref[...]).astype(bf16)

    col = pl.BlockSpec((S, LANE), lambda j: (0, j))
    return pl.pallas_call(
        body, name=name, grid=(N_FF_BLK,),
        in_specs=[col, pl.BlockSpec((S, LANE), lambda j: (0, N_FF_BLK + j)), pl.BlockSpec((3, LANE), lambda j: (0, j)),
                  pl.BlockSpec((1, LANE), lambda j: (0, j))],
        out_specs=col, out_shape=jax.ShapeDtypeStruct((S, D_FF), bf16), compiler_params=_cp(("parallel",)),
    )(gu, gu, w, b.reshape(1, D_FF))


def glu_bwd(gu, w, b, dact, name):
    S = gu.shape[0]

    def body(g_ref, u_ref, w_ref, b_ref, d_ref, dg_ref, dw_ref, db_ref):
        _, vjp = jax.vjp(_glu_f, g_ref[...], u_ref[...], w_ref[...], b_ref[...])
        dg, du, dw, db = vjp(d_ref[...])
        dg_ref[0] = dg.astype(bf16)
        dg_ref[1] = du.astype(bf16)
        dw_ref[...] = dw
        db_ref[...] = db

    col = pl.BlockSpec((S, LANE), lambda j: (0, j))
    wsp = pl.BlockSpec((3, LANE), lambda j: (0, j))
    bsp = pl.BlockSpec((1, LANE), lambda j: (0, j))
    return pl.pallas_call(
        body, name=name, grid=(N_FF_BLK,),
        in_specs=[col, pl.BlockSpec((S, LANE), lambda j: (0, N_FF_BLK + j)), wsp, bsp, col],
        out_specs=[pl.BlockSpec((2, S, LANE), lambda j: (0, 0, j)), wsp, bsp],
        out_shape=[jax.ShapeDtypeStruct((2, S, D_FF), bf16), jax.ShapeDtypeStruct((3, D_FF), f32),
                   jax.ShapeDtypeStruct((1, D_FF), f32)],
        compiler_params=_cp(("parallel",)),
    )(gu, gu, w, b.reshape(1, D_FF), dact)


def gu_fwd(n2, wg, name):
    S = n2.shape[0]
    tm = min(512, S)

    def body(a_ref, w_ref, o_ref):
        o_ref[...] = _dg(a_ref[...], w_ref[...], 1, 0)

    return pl.pallas_call(
        body, name=name, grid=(4, S // tm),
        in_specs=[pl.BlockSpec((tm, D), lambda s, m: (m, 0)), pl.BlockSpec((None, D, GU_SHARD), lambda s, m: (s, 0, 0))],
        out_specs=pl.BlockSpec((tm, GU_SHARD), lambda s, m: (m, s)),
        out_shape=jax.ShapeDtypeStruct((S, 2 * D_FF), f32), compiler_params=_cp(("parallel", "parallel")),
    )(n2, wg)


def gu_bwd_x(dgu, wg, name):
    S = dgu.shape[1]
    tm = min(512, S)

    def body(d_ref, w_ref, o_ref):
        @pl.when(pl.program_id(1) == 0)
        def _():
            o_ref[...] = jnp.zeros_like(o_ref)
        o_ref[...] += _dg(d_ref[...], w_ref[...], 1, 1)

    return pl.pallas_call(
        body, name=name, grid=(S // tm, 4),
        in_specs=[pl.BlockSpec((None, tm, GU_SHARD), lambda m, s: (s // 2, m, s % 2)),
                  pl.BlockSpec((None, D, GU_SHARD), lambda m, s: (s, 0, 0))],
        out_specs=pl.BlockSpec((tm, D), lambda m, s: (m, 0)),
        out_shape=jax.ShapeDtypeStruct((S, D), f32), compiler_params=_cp(("parallel", "arbitrary")),
    )(dgu, wg)


def gu_bwd_w(n2, dgu, name):
    S = n2.shape[0]
    tm = min(512, S)
    nm = S // tm

    def body(a_ref, d_ref, o_ref, acc):
        @pl.when(pl.program_id(1) == 0)
        def _():
            acc[...] = jnp.zeros_like(acc)
        acc[...] += _dg(a_ref[...], d_ref[...], 0, 0)

        @pl.when(pl.program_id(1) == nm - 1)
        def _():
            o_ref[...] = acc[...].astype(bf16)

    return pl.pallas_call(
        body, name=name, grid=(4, nm),
        in_specs=[pl.BlockSpec((tm, D), lambda s, m: (m, 0)),
                  pl.BlockSpec((None, tm, GU_SHARD), lambda s, m: (s // 2, m, s % 2))],
        out_specs=pl.BlockSpec((None, D, GU_SHARD), lambda s, m: (s, 0, 0)),
        out_shape=jax.ShapeDtypeStruct((4, D, GU_SHARD), bf16),
        scratch_shapes=[pltpu.VMEM((D, GU_SHARD), f32)],
        compiler_params=_cp(("parallel", "arbitrary")),
    )(n2, dgu)


def _pair_cols(w):
    lead = w.shape[:-1]
    return w.reshape(lead + (2, 6, A_DH)).swapaxes(-3, -2).reshape(lead + (A_Q,))


def _unpair_cols(w):
    lead = w.shape[:-1]
    return w.reshape(lead + (6, 2, A_DH)).swapaxes(-3, -2).reshape(lead + (A_Q,))


def _lay_in_a(w):
    return jnp.concatenate([_pair_cols(w[:, :A_Q]), w[:, A_Q:]], axis=1)


def _unlay_in_a(w):
    return jnp.concatenate([_unpair_cols(w[:, :A_Q]), w[:, A_Q:]], axis=1)


def _lay_out_a(w):
    return jnp.concatenate([_pair_cols(w[:A_Q].T).T, w[A_Q:]], axis=0)


def _unlay_out_a(w):
    return jnp.concatenate([_unpair_cols(w[:A_Q].T).T, w[A_Q:]], axis=0)


def _lay_in_b(w):
    return jnp.concatenate([w[:, :2304], w[:, 2316:], w[:, 2304:2316],
                            jnp.zeros((w.shape[0], LANE - 12), w.dtype)], axis=1)


def _unlay_in_b(w):
    return jnp.concatenate([w[:, :2304], w[:, 2560:2572], w[:, 2304:2560]], axis=1)


def _chip_cols(w):
    return jnp.moveaxis(w.reshape(w.shape[0], 4, w.shape[1] // 4), 1, 0)


def _unchip_cols(w):
    return jnp.moveaxis(w, 0, 1).reshape(w.shape[1], 4 * w.shape[2])


def _local_step(x, mem, target, P):
    arrive = P.get("arrive", lambda key, after: None)
    sk =jnp.zeros((16, LANE), f32).at[:A_HEADS].set(jnp.broadcast_to(P["sinks"][:, None], (A_HEADS, LANE)))
    prm = jnp.zeros((8, LANE), f32).at[0, 6:12].set(P["a_log"]).at[1, 6:12].set(P["dt_bias"]).at[2].set(P["out_norm_g"])
    bias = bias_build(P["rel_bias"])
    saved = []
    h = x
    for i in range(2):
        n1 = rms_fwd(h, P["g_mix"][i], f"rms_mix{i}")
        arrive(("w_in", i), n1)
        kv = memkv_fwd(mem, P["g_mem"][i], P["w_mem"][i], f"memkv{i}")
        if i == 0:
            proj = mm_nn(n1, P["w_in_a"], name="proj_a")
            self_out = swa_fwd(proj, bias, sk)
            cross = xattn_fwd(proj, A_Q + 2 * LANE, kv, "xattn_a")
            extra = ()
        else:
            proj = mm_nn(n1, P["w_in_b"], name="proj_b")
            qkvn = dnprep_fwd(proj, P["conv_qkv"])
            chunked = dnc_fwd(qkvn, proj, prm)
            o, states = dns_fwd(chunked)
            self_out = dnpost_fwd(o, proj, prm)
            cross = xattn_fwd(proj, 2304, kv, "xattn_b")
            extra = (qkvn, chunked, states, o)
        mix = jnp.concatenate([self_out, cross], axis=1)
        arrive(("w_out", i), cross)
        h2 = mm_nn(mix, P["w_out"][i], res=h, name=f"out_proj{i}")
        n2 = rms_fwd(h2, P["g_ffn"][i], f"rms_ffn{i}")
        arrive(("w_gu", i), n2)
        gu = gu_fwd(n2, P["w_gu"][i], f"gate_up{i}")
        act = glu_fwd(gu, P["ffn_cw"][i], P["ffn_cb"][i], f"glu{i}")
        arrive(("w_down", i), act)
        h3 = mm_nn(act, P["w_down"][i], res=h2, name=f"down{i}")
        saved.append((h, n1, kv, proj, mix, h2, n2, gu, act, extra))
        h = h3

    loss, dh, dg_fin = loss_head(h, P["g_fin"], target)
    G = {"g_fin": dg_fin[0], "g_mix": [None, None], "g_mem": [None, None], "g_ffn": [None, None],
         "w_mem": [None, None], "w_out": [None, None], "w_gu": [None, None], "w_down": [None, None],
         "ffn_cw": [None, None], "ffn_cb": [None, None]}
    for i in (1, 0):
        hin, n1, kv, proj, mix, h2, n2, gu, act, extra = saved[i]
        dact = mm_nt(dh, P["w_down"][i], name=f"d_act{i}")
        G["w_down"][i] = mm_tn(act, dh, name=f"dw_down{i}")
        dgu, dcw, dcb = glu_bwd(gu, P["ffn_cw"][i], P["ffn_cb"][i], dact, f"glu_bwd{i}")
        G["ffn_cw"][i], G["ffn_cb"][i] = dcw, dcb[0]
        dn2 = gu_bwd_x(dgu, P["w_gu"][i], f"d_n2_{i}")
        G["w_gu"][i] = gu_bwd_w(n2, dgu, f"dw_gu{i}")
        dh2, dg = rms_bwd(h2, P["g_ffn"][i], dn2, dh, f"rms_ffn_bwd{i}")
        G["g_ffn"][i] = dg[0]
        dmix = mm_nt(dh2, P["w_out"][i], name=f"d_mix{i}")
        G["w_out"][i] = mm_tn(mix, dh2, name=f"dw_out{i}")
        if i == 0:
            dqkv, dbias, dsk = swa_bwd(proj, bias, sk, dmix)
            dxq, dkv = xattn_bwd(proj, A_Q + 2 * LANE, kv, dmix, "xattn_a_bwd")
            dproj = jnp.concatenate([dqkv, dxq], axis=1)
            G["sinks"] = dsk[:A_HEADS, 0]
            G["rel_bias"] = bias_grad(dbias)[:, :A_HEADS]
            w_in, gname = P["w_in_a"], "w_in_a"
        else:
            qkvn, chunked, states, o = extra
            do, dz, dgo = dnpost_bwd(o, proj, prm, dmix)
            dqkvn, dseg, dprm = dnc_bwd(qkvn, proj, prm, dns_bwd(chunked, states, do))
            draw, dconv = dnprep_bwd(proj, P["conv_qkv"], dqkvn)
            dxq, dkv = xattn_bwd(proj, 2304, kv, dmix, "xattn_b_bwd")
            dproj = jnp.concatenate([draw, dz, dxq, dseg], axis=1)
            G["conv_qkv"] = dconv
            G["a_log"], G["dt_bias"], G["out_norm_g"] = dprm[0, 6:12], dprm[1, 6:12], dgo[0]
            w_in, gname = P["w_in_b"], "w_in_b"
        dn1 = mm_nt(dproj, w_in, name=f"d_n1_{i}")
        G[gname] = mm_tn(n1, dproj, name=f"d{gname}")
        dh, dg = rms_bwd(hin, P["g_mix"][i], dn1, dh2, f"rms_mix_bwd{i}")
        G["g_mix"][i] = dg[0]
        dgm, dwm = memkv_bwd(mem, P["g_mem"][i], P["w_mem"][i], dkv, f"memkv_bwd{i}")
        G["g_mem"][i], G["w_mem"][i] = dgm[0], dwm
    return loss, dh, G


def _prepare(full, w_gu=None):
    return {
        "rel_bias": full["rel_bias"], "sinks": full["sinks_a"][0], "a_log": full["a_log_b"][0],
        "dt_bias": full["dt_bias_b"][0], "out_norm_g": full["out_norm_g_b"][0],
        "g_mix": full["norm_mix_g"], "g_mem": full["norm_mem_g"], "g_ffn": full["norm_ffn_g"],
        "g_fin": full["final_norm_g"], "conv_qkv": full["conv_qkv_b"][0],
        "ffn_cw": [full["ffn_conv_w"][0], full["ffn_conv_w"][1]],
        "ffn_cb": [full["ffn_conv_b"][0], full["ffn_conv_b"][1]],
        "w_mem": [full["w_mem_kv"][0], full["w_mem_kv"][1]],
        "w_out": [_lay_out_a(full["w_out"][0]), full["w_out"][1]],
        "w_in_a": _lay_in_a(full["w_in_a"][0]), "w_in_b": _lay_in_b(full["w_in_b"][0]),
        "w_gu": w_gu if w_gu is not None else [_chip_cols(full["w_gate_up"][0]), _chip_cols(full["w_gate_up"][1])],
        "w_down": [full["w_down"][0], full["w_down"][1]],
    }


def _grads_to_ref(G):
    return {
        "rel_bias": G["rel_bias"], "norm_mix_g": jnp.stack(G["g_mix"]), "norm_mem_g": jnp.stack(G["g_mem"]),
        "w_mem_kv": jnp.stack(G["w_mem"]),
        "w_out": jnp.stack([_unlay_out_a(G["w_out"][0]), G["w_out"][1]]),
        "w_in_a": _unlay_in_a(G["w_in_a"])[None], "sinks_a": G["sinks"][None],
        "w_in_b": _unlay_in_b(G["w_in_b"])[None], "conv_qkv_b": G["conv_qkv"][None],
        "a_log_b": G["a_log"][None], "dt_bias_b": G["dt_bias"][None], "out_norm_g_b": G["out_norm_g"][None],
        "norm_ffn_g": jnp.stack(G["g_ffn"]),
        "w_gate_up": jnp.stack([_unchip_cols(G["w_gu"][0]), _unchip_cols(G["w_gu"][1])]).astype(f32),
        "ffn_conv_w": jnp.stack(G["ffn_cw"]), "ffn_conv_b": jnp.stack(G["ffn_cb"]),
        "w_down": jnp.stack(G["w_down"]), "final_norm_g": G["g_fin"],
    }


ANY = pl.BlockSpec(memory_space=pl.ANY)


def _place():
    return lax.axis_index("x"), lax.axis_index("y"), lax.axis_index("c")


def chip_scatter(gs):
    n = len(gs)

    def body(*refs):
        ins, outs = refs[:n], refs[n:2 * n]
        ssem, rsem = refs[2 * n:]
        x, y, c = _place()
        me = 2 * x + y
        peers = [(1 - x, y), (x, 1 - y), (1 - x, 1 - y)]

        def remote(j, k, slot):
            px, py = peers[k]
            return pltpu.make_async_remote_copy(
                src_ref=ins[j].at[2 * px + py], dst_ref=outs[j].at[slot],
                send_sem=ssem.at[3 * j + k], recv_sem=rsem.at[3 * j + k],
                device_id=(px, py, c), device_id_type=MESH)

        sends = [remote(j, k, me) for j in range(n) for k in range(3)]
        for cp in sends:
            cp.start()
        for j in range(n):
            for k in range(3):
                px, py = peers[k]
                remote(j, k, 2 * px + py).wait_recv()
        for cp in sends:
            cp.wait_send()

    return pl.pallas_call(
        body, name="grad_scatter", in_specs=[ANY] * n, out_specs=[ANY] * n,
        out_shape=[jax.ShapeDtypeStruct(g.shape, g.dtype) for g in gs],
        scratch_shapes=[pltpu.SemaphoreType.DMA((3 * n,)), pltpu.SemaphoreType.DMA((3 * n,))],
    )(*gs)


def allreduce_small(buf):
    R = buf.shape[0]

    def body(b_ref, o_ref, recv, ssem, rsem):
        x, y, c = _place()
        me = 4 * x + 2 * y + c

        def peer(k):
            return (1 - x if k & 4 else x, 1 - y if k & 2 else y, 1 - c if k & 1 else c)

        def remote(k, slot):
            return pltpu.make_async_remote_copy(
                src_ref=b_ref, dst_ref=recv.at[slot], send_sem=ssem.at[k - 1], recv_sem=rsem.at[k - 1],
                device_id=peer(k), device_id_type=MESH)

        sends = [remote(k, me) for k in range(1, 8)]
        for cp in sends:
            cp.start()
        recv[me] = b_ref[...]
        for k in range(1, 8):
            px, py, pc = peer(k)
            remote(k, 4 * px + 2 * py + pc).wait_recv()
        for cp in sends:
            cp.wait_send()
        total = recv[0]
        for j in range(1, 8):
            total = total + recv[j]
        o_ref[...] = total

    return pl.pallas_call(
        body, name="small_allreduce",
        in_specs=[pl.BlockSpec(memory_space=pltpu.VMEM)], out_specs=pl.BlockSpec(memory_space=pltpu.VMEM),
        out_shape=jax.ShapeDtypeStruct(buf.shape, f32),
        scratch_shapes=[pltpu.VMEM((8, R, LANE), f32), pltpu.SemaphoreType.DMA((7,)), pltpu.SemaphoreType.DMA((7,))],
    )(buf)


def sum_slots(own, recv, chip, core, name):
    _, R, C = recv.shape
    tr = _row_tile(R)
    nt = R // tr

    def body(p_ref, a_ref, r_ref, o_ref):
        acc = jnp.zeros((tr, C), f32)
        for s in range(4):
            acc = acc + jnp.where(p_ref[0] == s, a_ref[s], r_ref[s]).astype(f32)
        o_ref[...] = acc

    slots = pl.BlockSpec((4, tr, C), lambda i, p_ref: (0, i, 0))
    return pl.pallas_call(
        body, name=name, out_shape=jax.ShapeDtypeStruct((2 * R, C), f32),
        grid_spec=pltpu.PrefetchScalarGridSpec(
            num_scalar_prefetch=1, grid=(nt,), in_specs=[slots, slots],
            out_specs=pl.BlockSpec((tr, C), lambda i, p_ref: (p_ref[1] * nt + i, 0))),
        compiler_params=_cp(("parallel",)),
    )(jnp.stack([chip, core]).astype(jnp.int32), own, recv)


def _half(ref, core, axis=0):
    half = ref.shape[axis] // 2
    idx = (slice(None),) * axis + (pl.ds(core * half, half),)
    return ref.at[idx]


IN_HBM = pl.BlockSpec(memory_space=pltpu.HBM)
IN_SEM = pl.BlockSpec(memory_space=pltpu.SEMAPHORE)
SIDE_EFFECT = pltpu.SideEffectType.DATAFLOW_SIDE_EFFECTING


def _gather_copy(buf, i, k, ssem, rsem, place, landing):
    x, y, c = place
    px, py = [(1 - x, y), (x, 1 - y), (1 - x, 1 - y)][k]
    me = 2 * x + y
    return pltpu.make_async_remote_copy(
        src_ref=buf.at[me], dst_ref=buf.at[me if landing == "theirs" else 2 * px + py],
        send_sem=ssem.at[3 * i + k], recv_sem=rsem.at[3 * i + k], device_id=(px, py, c), device_id_type=MESH)


def gather_start(groups):
    flat = [b for grp in groups for b in grp]
    n, ng = len(flat), len(groups)

    def body(*refs):
        bufs, sems = refs[:n], refs[n:n + 2 * ng]
        place = _place()
        j = 0
        for g, grp in enumerate(groups):
            for i in range(len(grp)):
                for k in range(3):
                    _gather_copy(bufs[j], i, k, sems[2 * g], sems[2 * g + 1], place, "theirs").start()
                j += 1

    sem_shapes = [pltpu.SemaphoreType.DMA((3 * len(grp),)) for grp in groups for _ in range(2)]
    out = pl.pallas_call(
        body, name="gather_start", in_specs=[IN_HBM] * n, out_specs=(*[IN_SEM] * (2 * ng), *[IN_HBM] * n),
        out_shape=(*sem_shapes, *[pltpu.HBM(b.shape, b.dtype) for b in flat]),
        input_output_aliases={i: 2 * ng + i for i in range(n)},
        compiler_params=pltpu.CompilerParams(has_side_effects=SIDE_EFFECT),
    )(*[pltpu.with_memory_space_constraint(b, pltpu.HBM) for b in flat])
    sems, bufs = out[:2 * ng], list(out[2 * ng:])
    flights, j = [], 0
    for g, grp in enumerate(groups):
        flights.append((bufs[j:j + len(grp)], sems[2 * g], sems[2 * g + 1]))
        j += len(grp)
    return flights


def gather_wait(flight, after, name):
    bufs, ssem, rsem = flight
    n = len(bufs)

    def body(*refs):
        place = _place()
        for i in range(n):
            for k in range(3):
                cp = _gather_copy(refs[i], i, k, refs[n], refs[n + 1], place, "mine")
                cp.wait_send()
                cp.wait_recv()

    return pl.pallas_call(
        body, name=name, in_specs=[IN_HBM] * n + [IN_SEM, IN_SEM, ANY], out_specs=[IN_HBM] * n,
        out_shape=[pltpu.HBM(b.shape, b.dtype) for b in bufs], input_output_aliases={i: i for i in range(n)},
        compiler_params=pltpu.CompilerParams(has_side_effects=SIDE_EFFECT),
    )(*bufs, ssem, rsem, after)


def pair_exchange(gbufs):
    n = len(gbufs)

    def body(*refs):
        ins, outs = refs[:n], refs[n:2 * n]
        ssem, rsem = refs[2 * n:]
        x, y, c = _place()
        cps = [pltpu.make_async_remote_copy(
            src_ref=_half(ins[j], 1 - c, axis=1), dst_ref=outs[j], send_sem=ssem.at[j], recv_sem=rsem.at[j],
            device_id=(x, y, 1 - c), device_id_type=MESH) for j in range(n)]
        for cp in cps:
            cp.start()
        for cp in cps:
            cp.wait()

    return pl.pallas_call(
        body, name="pair_exchange", in_specs=[ANY] * n, out_specs=[ANY] * n,
        out_shape=[jax.ShapeDtypeStruct((4, g.shape[1] // 2, g.shape[2]), g.dtype) for g in gbufs],
        scratch_shapes=[pltpu.SemaphoreType.DMA((n,)), pltpu.SemaphoreType.DMA((n,))],
    )(*gbufs)


def _row_tile(rows):
    t = 256
    while rows % t:
        t //= 2
    return t


def pair_sum(mine, theirs, core, name):
    _, R, C = mine.shape
    half = R // 2
    tr = _row_tile(half)
    nt = half // tr

    def body(c_ref, a_ref, b_ref, o_ref):
        o_ref[...] = (a_ref[...].astype(f32) + b_ref[...].astype(f32)).astype(bf16)

    return pl.pallas_call(
        body, name=name, out_shape=jax.ShapeDtypeStruct(theirs.shape, bf16),
        grid_spec=pltpu.PrefetchScalarGridSpec(
            num_scalar_prefetch=1, grid=(4, nt),
            in_specs=[pl.BlockSpec((None, tr, C), lambda s, i, c_ref: (s, c_ref[0] * nt + i, 0)),
                      pl.BlockSpec((None, tr, C), lambda s, i, c_ref: (s, i, 0))],
            out_specs=pl.BlockSpec((None, tr, C), lambda s, i, c_ref: (s, i, 0))),
        compiler_params=_cp(("parallel", "parallel")),
    )(jnp.reshape(core, (1,)).astype(jnp.int32), mine, theirs)


def final_exchange(fins):
    n = len(fins)

    def body(*refs):
        outs = refs[n:2 * n]
        ssem, rsem = refs[2 * n:]
        x, y, c = _place()
        cps = [pltpu.make_async_remote_copy(
            src_ref=_half(outs[j], c), dst_ref=_half(outs[j], c), send_sem=ssem.at[j], recv_sem=rsem.at[j],
            device_id=(x, y, 1 - c), device_id_type=MESH) for j in range(n)]
        for cp in cps:
            cp.start()
        for cp in cps:
            cp.wait()

    return pl.pallas_call(
        body, name="final_exchange", in_specs=[ANY] * n, out_specs=[ANY] * n,
        out_shape=[jax.ShapeDtypeStruct(f.shape, f.dtype) for f in fins],
        input_output_aliases={j: j for j in range(n)},
        scratch_shapes=[pltpu.SemaphoreType.DMA((n,)), pltpu.SemaphoreType.DMA((n,))],
    )(*fins)


def adamw_big(w, m, v, gs, row0, name):
    L, R, C = w.shape
    tr = _row_tile(math.gcd(R, row0) if row0 else R)
    b0 = row0 // tr

    def body(*refs):
        w_ref, m_ref, v_ref = refs[:3]
        g_refs = refs[3:3 + L]
        g_ref, d_ref, nm_ref, nv_ref = refs[3 + L:]
        g = g_refs[0][...]
        for l in range(1, L):
            g = jnp.where(pl.program_id(0) == l, g_refs[l][...], g)
        d, nm, nv = _adamw_math(w_ref[...], g, m_ref[...], v_ref[...])
        g_ref[...] = g
        d_ref[...] = d
        nm_ref[...] = nm
        nv_ref[...] = nv

    own = pl.BlockSpec((None, tr, C), lambda l, i: (l, i, 0))
    off = pl.BlockSpec((tr, C), lambda l, i: (b0 + i, 0))
    return pl.pallas_call(
        body, name=name, grid=(L, R // tr), in_specs=[own, own, own] + [off] * L, out_specs=[own] * 4,
        out_shape=[jax.ShapeDtypeStruct((L, R, C), f32)] * 4, compiler_params=_cp(("parallel", "parallel")),
    )(w, m, v, *gs)


def _adamw_math(w, g, m, v):
    m = B1 * m + (1.0 - B1) * g
    v = B2 * v + (1.0 - B2) * (g * g)
    m_hat = m / (1.0 - B1 ** STEP)
    v_hat = v / (1.0 - B2 ** STEP)
    delta = -LR * (m_hat / (jnp.sqrt(v_hat) + AEPS) + WD * w)
    return delta, m, v


def adamw_small(w, m, v, g):
    def body(w_ref, m_ref, v_ref, g_ref, d_ref, nm_ref, nv_ref):
        d, nm, nv = _adamw_math(w_ref[...], g_ref[...], m_ref[...], v_ref[...])
        d_ref[...] = d
        nm_ref[...] = nm
        nv_ref[...] = nv

    return pl.pallas_call(body, name="adamw_small", out_shape=[jax.ShapeDtypeStruct(w.shape, f32)] * 3)(w, m, v, g)


CONV =(("conv_qkv_b", 2), ("ffn_conv_w", 2))
SMALL = ("rel_bias", "norm_mix_g", "norm_mem_g", "sinks_a", "a_log_b", "dt_bias_b", "out_norm_g_b", "norm_ffn_g",
         "ffn_conv_b", "final_norm_g")
WEIGHTS = ("rel_bias", "norm_mix_g", "norm_mem_g", "w_mem_kv", "w_out", "w_in_a", "sinks_a", "w_in_b", "conv_qkv_b",
           "a_log_b", "dt_bias_b", "out_norm_g_b", "norm_ffn_g", "w_gate_up", "ffn_conv_w", "ffn_conv_b", "w_down",
           "final_norm_g")
ARGS = ("x", "mem") + WEIGHTS + ("loss_target",) + tuple("m_" + n for n in WEIGHTS) + tuple("v_" + n for n in WEIGHTS)


def _rows(a, width):
    flat = a.reshape(-1)
    pad = (-flat.shape[0]) % (8 * width)
    if pad:
        flat = jnp.concatenate([flat, jnp.zeros((pad,), a.dtype)])
    return flat.reshape(-1, width)


def _nrows(shape, width):
    return _pad_to(-(-math.prod(shape) // width), 8)


def _pack(arrs, width, total_rows, dtype):
    parts = [_rows(a.astype(dtype), width) for a in arrs]
    used = sum(p.shape[0] for p in parts)
    if total_rows > used:
        parts.append(jnp.zeros((total_rows - used, width), dtype))
    return jnp.concatenate(parts, axis=0)


def _unpack(buf, shapes, width):
    out, r = [], 0
    for s in shapes:
        n = _nrows(s, width)
        out.append(buf[r:r + n].reshape(-1)[:math.prod(s)].reshape(s))
        r += n
    return out


def _pad_to(n, mult):
    return -(-n // mult) * mult


def kernel(x, mem, rel_bias, norm_mix_g, norm_mem_g, w_mem_kv, w_out, w_in_a, sinks_a, w_in_b, conv_qkv_b, a_log_b, dt_bias_b, out_norm_g_b, norm_ffn_g, w_gate_up, ffn_conv_w, ffn_conv_b, w_down, final_norm_g, loss_target, m_rel_bias, m_norm_mix_g, m_norm_mem_g, m_w_mem_kv, m_w_out, m_w_in_a, m_sinks_a, m_w_in_b, m_conv_qkv_b, m_a_log_b, m_dt_bias_b, m_out_norm_g_b, m_norm_ffn_g, m_w_gate_up, m_ffn_conv_w, m_ffn_conv_b, m_w_down, m_final_norm_g, v_rel_bias, v_norm_mix_g, v_norm_mem_g, v_w_mem_kv, v_w_out, v_w_in_a, v_sinks_a, v_w_in_b, v_conv_qkv_b, v_a_log_b, v_dt_bias_b, v_out_norm_g_b, v_norm_ffn_g, v_w_gate_up, v_ffn_conv_w, v_ffn_conv_b, v_w_down, v_final_norm_g):
    A = dict(zip(ARGS, (x, mem, rel_bias, norm_mix_g, norm_mem_g, w_mem_kv, w_out, w_in_a, sinks_a, w_in_b, conv_qkv_b, a_log_b, dt_bias_b, out_norm_g_b, norm_ffn_g, w_gate_up, ffn_conv_w, ffn_conv_b, w_down, final_norm_g, loss_target, m_rel_bias, m_norm_mix_g, m_norm_mem_g, m_w_mem_kv, m_w_out, m_w_in_a, m_sinks_a, m_w_in_b, m_conv_qkv_b, m_a_log_b, m_dt_bias_b, m_out_norm_g_b, m_norm_ffn_g, m_w_gate_up, m_ffn_conv_w, m_ffn_conv_b, m_w_down, m_final_norm_g, v_rel_bias, v_norm_mix_g, v_norm_mem_g, v_w_mem_kv, v_w_out, v_w_in_a, v_sinks_a, v_w_in_b, v_conv_qkv_b, v_a_log_b, v_dt_bias_b, v_out_norm_g_b, v_norm_ffn_g, v_w_gate_up, v_ffn_conv_w, v_ffn_conv_b, v_w_down, v_final_norm_g)))
    chip = 2 * lax.axis_index("x") + lax.axis_index("y")
    core = lax.axis_index("c")
    n_down, n_out, n_mem = w_down.shape[1], w_out.shape[1], w_mem_kv.shape[1]

    def own_slot(shard):
        return lax.dynamic_update_index_in_dim(lax.empty((4,) + shard.shape, shard.dtype), shard, chip, 0)

    def bslot(w):
        return own_slot(w.astype(bf16))

    groups = {
        ("w_in", 0): [bslot(w_in_a[0]), bslot(w_mem_kv[0]), bslot(w_mem_kv[1]), own_slot(conv_qkv_b[0]),
                      own_slot(ffn_conv_w.reshape(6, -1))],
        ("w_out", 0): [bslot(w_out[0])], ("w_gu", 0): [bslot(w_gate_up[0])], ("w_down", 0): [bslot(w_down[0])],
        ("w_in", 1): [bslot(w_in_b[0])],
        ("w_out", 1): [bslot(w_out[1]), bslot(w_gate_up[1]), bslot(w_down[1])],
    }
    flights = dict(zip(groups, gather_start(list(groups.values()))))
    P = {"rel_bias": rel_bias, "sinks": sinks_a[0], "a_log": a_log_b[0], "dt_bias": dt_bias_b[0],
         "out_norm_g": out_norm_g_b[0], "g_mix": norm_mix_g, "g_mem": norm_mem_g, "g_ffn": norm_ffn_g,
         "g_fin": final_norm_g, "ffn_cb": [ffn_conv_b[0], ffn_conv_b[1]], "w_mem": [None, None], "w_out": [None, None],
         "w_gu": [None, None], "w_down": [None, None], "ffn_cw": [None, None]}

    def rows4(g):
        return g.reshape(4 * g.shape[1], g.shape[2])

    def arrive(key, after):
        if key not in flights:
            return
        got = gather_wait(flights.pop(key), after, "gather_wait_%s%d" % key)
        if key == ("w_in", 0):
            P["w_in_a"] = _lay_in_a(_unchip_cols(got[0]))
            P["w_mem"] = [rows4(got[1]), rows4(got[2])]
            P["conv_qkv"] = _unchip_cols(got[3])
            cw = _unchip_cols(got[4]).reshape(2, 3, D_FF)
            P["ffn_cw"] = [cw[0], cw[1]]
        elif key == ("w_in", 1):
            P["w_in_b"] = _lay_in_b(_unchip_cols(got[0]))
        elif key == ("w_out", 0):
            P["w_out"][0] = _lay_out_a(rows4(got[0]))
        elif key == ("w_gu", 0):
            P["w_gu"][0] = got[0]
        elif key == ("w_down", 0):
            P["w_down"][0] = rows4(got[0])
        else:
            P["w_out"][1], P["w_gu"][1], P["w_down"][1] = rows4(got[0]), got[1], rows4(got[2])

    P["arrive"] = arrive

    loss, dx, G = _local_step(x[0], mem[0], loss_target[0], P)
    gfull = _grads_to_ref(G)

    def chip_rows(g, n):
        return g.reshape(4, n, g.shape[-1])

    partial = [G["w_gu"][0], G["w_gu"][1],
               jnp.concatenate([chip_rows(G["w_down"][0], n_down), chip_rows(G["w_down"][1], n_down),
                                chip_rows(_unlay_out_a(G["w_out"][0]), n_out), chip_rows(G["w_out"][1], n_out)],
                               axis=1).astype(bf16),
               jnp.concatenate([chip_rows(G["w_mem"][0], n_mem), chip_rows(G["w_mem"][1], n_mem)], axis=1).astype(bf16),
               _chip_cols(_unlay_in_a(G["w_in_a"])).astype(bf16), _chip_cols(_unlay_in_b(G["w_in_b"])).astype(bf16)]
    theirs = pair_exchange(partial)
    names = ("gu0", "gu1", "rows", "mem", "in_a", "in_b")
    pair = [pair_sum(p, t, core, "pair_sum_" + nm) for p, t, nm in zip(partial, theirs, names)]
    arrived = chip_scatter(pair)
    halves = [sum_slots(p, r, chip, core, "sum_slots_" + nm) for p, r, nm in zip(pair, arrived, names)]
    f_gu0, f_gu1, f_rows, f_mem, f_ina, f_inb = final_exchange(halves)

    sm_shapes = [A[n].shape for n in SMALL] + [gfull[n].shape for n, _ in CONV] + [(LANE,)]
    sm_rows = _pad_to(sum(_nrows(s, LANE) for s in sm_shapes), 8)
    sbuf = _pack([gfull[n] for n in SMALL] + [gfull[n] for n, _ in CONV] + [loss[0]], LANE, sm_rows, f32)
    tot = _unpack(allreduce_small(sbuf), sm_shapes, LANE)
    gsmall = dict(zip(SMALL, tot[:len(SMALL)]))
    for (n, axis), t in zip(CONV, tot[len(SMALL):len(SMALL) + len(CONV)]):
        sh = A[n].shape[axis]
        gsmall[n] = lax.dynamic_slice_in_dim(t, chip * sh, sh, axis)
    loss_out = tot[-1][0]

    out = {}
    plan = (("w_gate_up", (2, D, GU_SHARD), [f_gu0, f_gu1], 0), ("w_down", (1, 2 * n_down, D), [f_rows], 0),
            ("w_out", (1, 2 * n_out, D), [f_rows], 2 * n_down), ("w_mem_kv", (1, 2 * n_mem, 2 * X_Q), [f_mem], 0),
            ("w_in_a", (1, D, IN_A // 4), [f_ina], 0), ("w_in_b", (1, D, IN_B // 4), [f_inb], 0))
    for n, shape3, gs, row0 in plan:
        res = adamw_big(A[n].reshape(shape3), A["m_" + n].reshape(shape3), A["v_" + n].reshape(shape3), gs, row0,
                        "adamw_" + n)
        for key, r in zip(("grad_", "delta_", "new_m_", "new_v_"), res):
            out[key + n] = r.reshape(A[n].shape)
    names = SMALL + tuple(n for n, _ in CONV)
    shapes = [A[n].shape for n in names]
    rows = _pad_to(sum(_nrows(s, LANE) for s in shapes), 8)
    packs = [_pack([src[n] for n in names], LANE, rows, f32)
             for src in ({n: A[n] for n in names}, {n: A["m_" + n] for n in names}, {n: A["v_" + n] for n in names}, gsmall)]
    res = adamw_small(*packs)
    for key, r in zip(("delta_", "new_m_", "new_v_"), res):
        for n, a in zip(names, _unpack(r, shapes, LANE)):
            out[key + n] = a
    for n in names:
        out["grad_" + n] = gsmall[n]
    return (loss_out, dx[None], *[out["grad_" + n] for n in WEIGHTS], *[out["delta_" + n] for n in WEIGHTS],
            *[out["new_m_" + n] for n in WEIGHTS], *[out["new_v_" + n] for n in WEIGHTS])
```

```python
import functools
import math

import numpy as np
import jax
import jax.numpy as jnp
from jax import lax
from jax.experimental import pallas as pl
from jax.experimental.pallas import tpu as pltpu

f32 = jnp.float32
bf16 = jnp.bfloat16
HI = lax.Precision.HIGHEST
MESH = pl.DeviceIdType.MESH

D = 1024
MEM_LEN = 256
EPS = 1e-6
A_HEADS, A_KV, A_DH = 12, 2, 64
A_Q = 768
BLK = 128
N_BUCKETS, MAX_DIST = 32, 128
B_QK, B_V, B_DH = 384, 768, 128
B_QKV = 1536
CHUNK = 64
X_Q = 256
D_FF = 2816
IN_A = 1280
IN_B = 2572
IN_B_PAD = 2688
LANE = 128
VMEM_LIMIT = 56 * 1024 * 1024

LR, B1, B2, AEPS, WD, STEP = 0.001, 0.9, 0.999, 1e-08, 0.01, 10


def _cp(sem=None):
    return pltpu.CompilerParams(dimension_semantics=sem, vmem_limit_bytes=VMEM_LIMIT)


def _dg(a, b, ca, cb, prec=None):
    return lax.dot_general(a, b, (((ca,), (cb,)), ((), ())), precision=prec, preferred_element_type=f32)


@jax.custom_vjp
def bdot(a, b):
    return _dg(a.astype(bf16), b.astype(bf16), 1, 0)


def _bdot_f(a, b):
    return bdot(a, b), (a, b)


def _bdot_b(res, g):
    a, b = res
    gb = g.astype(bf16)
    return _dg(gb, b.astype(bf16), 1, 1), _dg(a.astype(bf16), gb, 0, 0)


bdot.defvjp(_bdot_f, _bdot_b)


@jax.custom_vjp
def bdot_nt(a, b):
    return _dg(a.astype(bf16), b.astype(bf16), 1, 1)


def _bdot_nt_f(a, b):
    return bdot_nt(a, b), (a, b)


def _bdot_nt_b(res, g):
    a, b = res
    gb = g.astype(bf16)
    return _dg(gb, b.astype(bf16), 1, 0), _dg(gb, a.astype(bf16), 0, 0)


bdot_nt.defvjp(_bdot_nt_f, _bdot_nt_b)


def _shift_rows(x, s, down):
    n = x.shape[0]
    row = lax.broadcasted_iota(jnp.int32, x.shape, 0)
    if down:
        return jnp.where(row >= s, pltpu.roll(x, s, 0), 0.0)
    return jnp.where(row < n - s, pltpu.roll(x, n - s, 0), 0.0)


@functools.partial(jax.custom_vjp, nondiff_argnums=(1,))
def shift_down(x, s):
    return _shift_rows(x, s, True)


def _sd_f(x, s):
    return _shift_rows(x, s, True), None


def _sd_b(s, _, g):
    return (_shift_rows(g, s, False),)


shift_down.defvjp(_sd_f, _sd_b)


def _sigmoid(x):
    return 1.0 / (1.0 + jnp.exp(-x))


def _silu(x):
    return x * _sigmoid(x)


def _rms(x, g):
    return x * lax.rsqrt(jnp.mean(x * x, axis=-1, keepdims=True) + EPS) * g


def _tile(n, cap):
    u = n // LANE
    best = 1
    for d in range(1, u + 1):
        if u % d == 0 and d * LANE <= cap:
            best = d
    return best * LANE


def mm_nn(a, w, res=None, out_dtype=f32, name="mm_nn"):
    M, K = a.shape
    N = w.shape[1]
    tm, tn = min(512, M), _tile(N, 1024)

    def body(*refs):
        if res is None:
            a_ref, w_ref, o_ref = refs
            o_ref[...] = _dg(a_ref[...].astype(bf16), w_ref[...], 1, 0).astype(out_dtype)
        else:
            a_ref, w_ref, r_ref, o_ref = refs
            o_ref[...] = (r_ref[...] + _dg(a_ref[...].astype(bf16), w_ref[...], 1, 0)).astype(out_dtype)

    in_specs = [pl.BlockSpec((tm, K), lambda n, m: (m, 0)), pl.BlockSpec((K, tn), lambda n, m: (0, n))]
    args = [a, w]
    if res is not None:
        in_specs.append(pl.BlockSpec((tm, tn), lambda n, m: (m, n)))
        args.append(res)
    return pl.pallas_call(
        body, name=name, grid=(N // tn, M // tm), in_specs=in_specs,
        out_specs=pl.BlockSpec((tm, tn), lambda n, m: (m, n)),
        out_shape=jax.ShapeDtypeStruct((M, N), out_dtype),
        compiler_params=_cp(("parallel", "parallel")),
    )(*args)


def mm_nt(dy, w, name="mm_nt"):
    M, N = dy.shape
    K = w.shape[0]
    tm, tn = min(512, M), _tile(N, 1024)

    def body(dy_ref, w_ref, o_ref):
        @pl.when(pl.program_id(1) == 0)
        def _():
            o_ref[...] = jnp.zeros_like(o_ref)
        o_ref[...] += _dg(dy_ref[...].astype(bf16), w_ref[...], 1, 1)

    return pl.pallas_call(
        body, name=name, grid=(M // tm, N // tn),
        in_specs=[pl.BlockSpec((tm, tn), lambda m, n: (m, n)), pl.BlockSpec((K, tn), lambda m, n: (0, n))],
        out_specs=pl.BlockSpec((tm, K), lambda m, n: (m, 0)),
        out_shape=jax.ShapeDtypeStruct((M, K), f32),
        compiler_params=_cp(("parallel", "arbitrary")),
    )(dy, w)


def mm_tn(a, dy, name="mm_tn"):
    M, K = a.shape
    N = dy.shape[1]
    tm, tk, tn = min(512, M), _tile(K, 1408), _tile(N, 1024)

    def body(a_ref, dy_ref, o_ref):
        @pl.when(pl.program_id(2) == 0)
        def _():
            o_ref[...] = jnp.zeros_like(o_ref)
        o_ref[...] += _dg(a_ref[...].astype(bf16), dy_ref[...].astype(bf16), 0, 0)

    return pl.pallas_call(
        body, name=name, grid=(K // tk, N // tn, M // tm),
        in_specs=[pl.BlockSpec((tm, tk), lambda k, n, m: (m, k)), pl.BlockSpec((tm, tn), lambda k, n, m: (m, n))],
        out_specs=pl.BlockSpec((tk, tn), lambda k, n, m: (k, n)),
        out_shape=jax.ShapeDtypeStruct((K, N), f32),
        compiler_params=_cp(("parallel", "parallel", "arbitrary")),
    )(a, dy)


def rms_fwd(h, g, name):
    S = h.shape[0]
    t = min(512, S)

    def body(h_ref, g_ref, o_ref):
        o_ref[...] = _rms(h_ref[...], g_ref[...]).astype(bf16)

    return pl.pallas_call(
        body, name=name, grid=(S // t,),
        in_specs=[pl.BlockSpec((t, D), lambda i: (i, 0)), pl.BlockSpec((1, D), lambda i: (0, 0))],
        out_specs=pl.BlockSpec((t, D), lambda i: (i, 0)),
        out_shape=jax.ShapeDtypeStruct((S, D), bf16),
        compiler_params=_cp(("parallel",)),
    )(h, g.reshape(1, D))


def rms_bwd(h, g, dn, dres, name):
    S = h.shape[0]
    t = min(512, S)

    def body(h_ref, g_ref, dn_ref, dr_ref, dh_ref, dg_ref):
        @pl.when(pl.program_id(0) == 0)
        def _():
            dg_ref[...] = jnp.zeros_like(dg_ref)
        _, vjp = jax.vjp(_rms, h_ref[...], g_ref[...])
        dh, dg = vjp(dn_ref[...])
        dh_ref[...] = dr_ref[...] + dh
        dg_ref[...] += dg

    tok = pl.BlockSpec((t, D), lambda i: (i, 0))
    vec = pl.BlockSpec((1, D), lambda i: (0, 0))
    return pl.pallas_call(
        body, name=name, grid=(S // t,), in_specs=[tok, vec, tok, tok], out_specs=[tok, vec],
        out_shape=[jax.ShapeDtypeStruct((S, D), f32), jax.ShapeDtypeStruct((1, D), f32)],
        compiler_params=_cp(("arbitrary",)),
    )(h, g.reshape(1, D), dn, dres)


def loss_head(h, g, target):
    S = h.shape[0]
    t = min(512, S)

    def f(hh, gg, tt):
        err = _rms(hh, gg) - tt
        return 0.5 * jnp.sum(jnp.mean(err * err, axis=-1, keepdims=True), axis=0, keepdims=True)

    def body(h_ref, g_ref, t_ref, loss_ref, dh_ref, dg_ref):
        @pl.when(pl.program_id(0) == 0)
        def _():
            dg_ref[...] = jnp.zeros_like(dg_ref)
            loss_ref[...] = jnp.zeros_like(loss_ref)
        val, vjp = jax.vjp(lambda a, b: f(a, b, t_ref[...]), h_ref[...], g_ref[...])
        dh, dg = vjp(jnp.ones((1, 1), f32))
        dh_ref[...] = dh
        dg_ref[...] += dg
        loss_ref[...] += jnp.broadcast_to(val, loss_ref.shape)

    tok = pl.BlockSpec((t, D), lambda i: (i, 0))
    vec = pl.BlockSpec((1, D), lambda i: (0, 0))
    return pl.pallas_call(
        body, name="loss_head", grid=(S // t,), in_specs=[tok, vec, tok],
        out_specs=[pl.BlockSpec((1, LANE), lambda i: (0, 0)), tok, vec],
        out_shape=[jax.ShapeDtypeStruct((1, LANE), f32), jax.ShapeDtypeStruct((S, D), f32),
                   jax.ShapeDtypeStruct((1, D), f32)],
        compiler_params=_cp(("arbitrary",)),
    )(h, g.reshape(1, D), target)


def memkv_fwd(mem, g, w, name):
    def body(m_ref, g_ref, w_ref, o_ref):
        o_ref[...] = _dg(_rms(m_ref[...], g_ref[...]).astype(bf16), w_ref[...], 1, 0)

    return pl.pallas_call(
        body, name=name, out_shape=jax.ShapeDtypeStruct((MEM_LEN, 2 * X_Q), f32), compiler_params=_cp(),
    )(mem, g.reshape(1, D), w)


def memkv_bwd(mem, g, w, dkv, name):
    def body(m_ref, g_ref, w_ref, d_ref, dg_ref, dw_ref):
        n, vjp = jax.vjp(lambda gg: _rms(m_ref[...], gg), g_ref[...])
        db = d_ref[...].astype(bf16)
        dw_ref[...] = _dg(n.astype(bf16), db, 0, 0)
        dg_ref[...] = vjp(_dg(db, w_ref[...], 1, 1))[0]

    return pl.pallas_call(
        body, name=name,
        out_shape=[jax.ShapeDtypeStruct((1, D), f32), jax.ShapeDtypeStruct((D, 2 * X_Q), f32)],
        compiler_params=_cp(),
    )(mem, g.reshape(1, D), w, dkv)


def _xattn_f(xq, mk, mv):
    lane = lax.broadcasted_iota(jnp.int32, (1, X_Q), 1)
    out = jnp.zeros(xq.shape, f32)
    for hd in range(4):
        msk = (lane // 64 == hd).astype(f32)
        s = bdot_nt(xq * msk, mk) * (64 ** -0.5)
        m = lax.stop_gradient(jnp.max(s, axis=-1, keepdims=True))
        p = jnp.exp(s - m)
        p = p / jnp.sum(p, axis=-1, keepdims=True)
        out = out + bdot(p, mv * msk)
    return out


def xattn_fwd(proj, col, kv, name):
    S = proj.shape[0]
    t = min(512, S)
    cb = col // X_Q

    def body(q_ref, k_ref, v_ref, o_ref):
        o_ref[...] = _xattn_f(q_ref[...], k_ref[...], v_ref[...])

    return pl.pallas_call(
        body, name=name, grid=(S // t,),
        in_specs=[pl.BlockSpec((t, X_Q), lambda i: (i, cb)), pl.BlockSpec((MEM_LEN, X_Q), lambda i: (0, 0)),
                  pl.BlockSpec((MEM_LEN, X_Q), lambda i: (0, 1))],
        out_specs=pl.BlockSpec((t, X_Q), lambda i: (i, 0)),
        out_shape=jax.ShapeDtypeStruct((S, X_Q), f32),
        compiler_params=_cp(("parallel",)),
    )(proj, kv, kv)


def xattn_bwd(proj, col, kv, dmix, name):
    S = proj.shape[0]
    t = min(512, S)
    cb = col // X_Q

    def body(q_ref, k_ref, v_ref, do_ref, dq_ref, dk_ref, dv_ref):
        @pl.when(pl.program_id(0) == 0)
        def _():
            dk_ref[...] = jnp.zeros_like(dk_ref)
            dv_ref[...] = jnp.zeros_like(dv_ref)
        _, vjp = jax.vjp(_xattn_f, q_ref[...], k_ref[...], v_ref[...])
        dq, dk, dv = vjp(do_ref[...])
        dq_ref[...] = dq
        dk_ref[...] += dk
        dv_ref[...] += dv

    kvb = pl.BlockSpec((MEM_LEN, X_Q), lambda i: (0, 0))
    dq, dk, dv = pl.pallas_call(
        body, name=name, grid=(S // t,),
        in_specs=[pl.BlockSpec((t, X_Q), lambda i: (i, cb)), kvb,
                  pl.BlockSpec((MEM_LEN, X_Q), lambda i: (0, 1)), pl.BlockSpec((t, X_Q), lambda i: (i, 3))],
        out_specs=[pl.BlockSpec((t, X_Q), lambda i: (i, 0)), kvb, kvb],
        out_shape=[jax.ShapeDtypeStruct((S, X_Q), f32), jax.ShapeDtypeStruct((MEM_LEN, X_Q), f32),
                   jax.ShapeDtypeStruct((MEM_LEN, X_Q), f32)],
        compiler_params=_cp(("arbitrary",)),
    )(proj, kv, kv, dmix)
    return dq, jnp.concatenate([dk, dv], axis=1)


def _bucket_map():
    qi = np.arange(BLK)[:, None]
    kj = np.arange(2 * BLK)[None, :]
    n = np.maximum(BLK + qi - kj, 0)
    max_exact = N_BUCKETS // 2
    nf = np.maximum(n, 1).astype(np.float64)
    large = max_exact + (np.log(nf / max_exact) / math.log(MAX_DIST / max_exact)
                         * (N_BUCKETS - max_exact)).astype(np.int32)
    large = np.minimum(large, N_BUCKETS - 1)
    return np.where(n < max_exact, n, large).astype(np.int32)


def bias_build(rel_bias):
    def body(rb_ref, bk_ref, o_ref):
        bk = bk_ref[...]
        for h in range(A_HEADS):
            acc = jnp.zeros((BLK, 2 * BLK), f32)
            for b in range(N_BUCKETS):
                acc = jnp.where(bk == b, rb_ref[b, h], acc)
            o_ref[h] = acc

    return pl.pallas_call(
        body, name="bias_build",
        in_specs=[pl.BlockSpec(memory_space=pltpu.SMEM), pl.BlockSpec(memory_space=pltpu.VMEM)],
        out_specs=pl.BlockSpec(memory_space=pltpu.VMEM),
        out_shape=jax.ShapeDtypeStruct((A_HEADS, BLK, 2 * BLK), f32), compiler_params=_cp(),
    )(rel_bias, jnp.asarray(_bucket_map()))


def bias_grad(dbias):
    def body(d_ref, bk_ref, o_ref):
        bk = bk_ref[...]
        row = lax.broadcasted_iota(jnp.int32, (N_BUCKETS, LANE), 0)
        lane = lax.broadcasted_iota(jnp.int32, (N_BUCKETS, LANE), 1)
        acc = jnp.zeros((N_BUCKETS, LANE), f32)
        for h in range(A_HEADS):
            d = d_ref[h]
            for b in range(N_BUCKETS):
                s = jnp.sum(jnp.where(bk == b, d, 0.0), keepdims=True)
                acc = acc + jnp.where((row == b) & (lane == h), s, 0.0)
        o_ref[...] = acc

    return pl.pallas_call(
        body, name="bias_grad", out_shape=jax.ShapeDtypeStruct((N_BUCKETS, LANE), f32), compiler_params=_cp(),
    )(dbias, jnp.asarray(_bucket_map()))


def _swa_f(qb, kp, kc, vp, vc, bias, sk, first):
    kband = jnp.concatenate([kp, kc], axis=0)
    vband = jnp.concatenate([vp, vc], axis=0)
    qi = lax.broadcasted_iota(jnp.int32, (BLK, 2 * BLK), 0)
    kj = lax.broadcasted_iota(jnp.int32, (BLK, 2 * BLK), 1)
    rel = kj - qi
    ok = (rel >= 1) & (rel <= BLK) & ((kj >= BLK) | jnp.logical_not(first))
    lane = lax.broadcasted_iota(jnp.int32, (1, LANE), 1)
    lane_b = lax.broadcasted_iota(jnp.int32, (BLK, LANE), 1)
    outs = []
    for p in range(A_HEADS // 2):
        qp = qb[:, LANE * p:LANE * (p + 1)]
        acc = jnp.zeros((BLK, LANE), f32)
        for g in range(2):
            h = g * (A_HEADS // 2) + p
            msk = (lane // A_DH == g).astype(f32)
            s = bdot_nt(qp * msk, kband) * (A_DH ** -0.5) + bias[h]
            s = jnp.where(ok, s, -1e30)
            skb = jnp.broadcast_to(sk[h:h + 1, :], (BLK, LANE))
            sink = jnp.sum(jnp.where(lane_b == 0, skb, 0.0), axis=-1, keepdims=True)
            m = lax.stop_gradient(jnp.maximum(jnp.max(s, axis=-1, keepdims=True), sink))
            e = jnp.exp(s - m)
            prob = e / (jnp.sum(e, axis=-1, keepdims=True) + jnp.exp(sink - m))
            acc = acc + bdot(prob, vband) * msk
        outs.append(acc)
    return jnp.concatenate(outs, axis=1)


def _swa_specs(nb, rev):
    bi = (lambda i: nb - 1 - i) if rev else (lambda i: i)
    return [
        pl.BlockSpec((BLK, A_Q), lambda i: (bi(i), 0)),
        pl.BlockSpec((BLK, LANE), lambda i: (jnp.maximum(bi(i) - 1, 0), 6)),
        pl.BlockSpec((BLK, LANE), lambda i: (bi(i), 6)),
        pl.BlockSpec((BLK, LANE), lambda i: (jnp.maximum(bi(i) - 1, 0), 7)),
        pl.BlockSpec((BLK, LANE), lambda i: (bi(i), 7)),
        pl.BlockSpec((A_HEADS, BLK, 2 * BLK), lambda i: (0, 0, 0)),
        pl.BlockSpec((16, LANE), lambda i: (0, 0)),
    ]


def swa_fwd(proj, bias, sk):
    S = proj.shape[0]
    nb = S // BLK

    def body(q_ref, kp_ref, kc_ref, vp_ref, vc_ref, b_ref, s_ref, o_ref):
        o_ref[...] = _swa_f(q_ref[...], kp_ref[...], kc_ref[...], vp_ref[...], vc_ref[...], b_ref[...], s_ref[...],
                            pl.program_id(0) == 0)

    return pl.pallas_call(
        body, name="swa_fwd", grid=(nb,), in_specs=_swa_specs(nb, False),
        out_specs=pl.BlockSpec((BLK, A_Q), lambda i: (i, 0)),
        out_shape=jax.ShapeDtypeStruct((S, A_Q), f32), compiler_params=_cp(("parallel",)),
    )(proj, proj, proj, proj, proj, bias, sk)


def swa_bwd(proj, bias, sk, dmix):
    S = proj.shape[0]
    nb = S // BLK

    def body(q_ref, kp_ref, kc_ref, vp_ref, vc_ref, b_ref, s_ref, do_ref, dqkv_ref, db_ref, ds_ref, ck, cv):
        i = pl.program_id(0)

        @pl.when(i == 0)
        def _():
            db_ref[...] = jnp.zeros_like(db_ref)
            ds_ref[...] = jnp.zeros_like(ds_ref)
            ck[...] = jnp.zeros_like(ck)
            cv[...] = jnp.zeros_like(cv)
        first = i == nb - 1
        _, vjp = jax.vjp(lambda *a: _swa_f(*a, first), q_ref[...], kp_ref[...], kc_ref[...], vp_ref[...],
                         vc_ref[...], b_ref[...], s_ref[...])
        dq, dkp, dkc, dvp, dvc, db, ds = vjp(do_ref[...])
        dqkv_ref[...] = jnp.concatenate([dq, dkc + ck[...], dvc + cv[...]], axis=1)
        ck[...] = dkp
        cv[...] = dvp
        db_ref[...] += db
        ds_ref[...] += ds

    return pl.pallas_call(
        body, name="swa_bwd", grid=(nb,),
        in_specs=_swa_specs(nb, True) + [pl.BlockSpec((BLK, A_Q), lambda i: (nb - 1 - i, 0))],
        out_specs=[pl.BlockSpec((BLK, D), lambda i: (nb - 1 - i, 0)),
                   pl.BlockSpec((A_HEADS, BLK, 2 * BLK), lambda i: (0, 0, 0)),
                   pl.BlockSpec((16, LANE), lambda i: (0, 0))],
        out_shape=[jax.ShapeDtypeStruct((S, D), f32), jax.ShapeDtypeStruct((A_HEADS, BLK, 2 * BLK), f32),
                   jax.ShapeDtypeStruct((16, LANE), f32)],
        scratch_shapes=[pltpu.VMEM((BLK, LANE), f32), pltpu.VMEM((BLK, LANE), f32)],
        compiler_params=_cp(("arbitrary",)),
    )(proj, proj, proj, proj, proj, bias, sk, dmix)


def _dnprep_f(x, w, is_qk):
    c = (w[3:4] * x + w[2:3] * shift_down(x, 1) + w[1:2] * shift_down(x, 2) + w[0:1] * shift_down(x, 3))
    a = _silu(c)
    n = a * lax.rsqrt(jnp.sum(a * a, axis=-1, keepdims=True) + EPS)
    return jnp.where(is_qk, n, a)


def dnprep_fwd(proj, cw):
    S = proj.shape[0]
    nblk = B_QKV // LANE

    def body(x_ref, w_ref, o_ref):
        o_ref[...] = _dnprep_f(x_ref[...], w_ref[...], pl.program_id(0) < 2 * B_QK // LANE)

    return pl.pallas_call(
        body, name="dnprep_fwd", grid=(nblk,),
        in_specs=[pl.BlockSpec((S, LANE), lambda j: (0, j)), pl.BlockSpec((4, LANE), lambda j: (0, j))],
        out_specs=pl.BlockSpec((S, LANE), lambda j: (0, j)),
        out_shape=jax.ShapeDtypeStruct((S, B_QKV), f32), compiler_params=_cp(("parallel",)),
    )(proj, cw)


def dnprep_bwd(proj, cw, dqkvn):
    S = proj.shape[0]
    nblk = B_QKV // LANE

    def body(x_ref, w_ref, d_ref, dx_ref, dw_ref):
        is_qk = pl.program_id(0) < 2 * B_QK // LANE
        _, vjp = jax.vjp(lambda a, b: _dnprep_f(a, b, is_qk), x_ref[...], w_ref[...])
        dx, dw = vjp(d_ref[...])
        dx_ref[...] = dx
        dw_ref[...] = dw

    col = pl.BlockSpec((S, LANE), lambda j: (0, j))
    wsp = pl.BlockSpec((4, LANE), lambda j: (0, j))
    return pl.pallas_call(
        body, name="dnprep_bwd", grid=(nblk,), in_specs=[col, wsp, col], out_specs=[col, wsp],
        out_shape=[jax.ShapeDtypeStruct((S, B_QKV), f32), jax.ShapeDtypeStruct((4, B_QKV), f32)],
        compiler_params=_cp(("parallel",)),
    )(proj, cw, dqkvn)


def _hdot(a, b, ca=1, cb=0):
    return _dg(a, b, ca, cb, HI)


def _bdg(a, b, ca, cb):
    dn = (((ca,), (cb,)), ((0,), (0,)))
    ah, bh = a.astype(bf16), b.astype(bf16)
    al, bl = (a - ah.astype(f32)).astype(bf16), (b - bh.astype(f32)).astype(bf16)
    return (lax.dot_general(ah, bh, dn, preferred_element_type=f32)
            + lax.dot_general(ah, bl, dn, preferred_element_type=f32)
            + lax.dot_general(al, bh, dn, preferred_element_type=f32))


@jax.custom_vjp
def hbd(a, b):
    return _bdg(a, b, 2, 1)


@jax.custom_vjp
def hbd_nt(a, b):
    return _bdg(a, b, 2, 2)


@jax.custom_vjp
def hbd_tn(a, b):
    return _bdg(a, b, 1, 1)


hbd.defvjp(lambda a, b: (hbd(a, b), (a, b)), lambda r, g: (hbd_nt(g, r[1]), hbd_tn(r[0], g)))
hbd_nt.defvjp(lambda a, b: (hbd_nt(a, b), (a, b)), lambda r, g: (hbd(g, r[1]), hbd_tn(g, r[0])))
hbd_tn.defvjp(lambda a, b: (hbd_tn(a, b), (a, b)), lambda r, g: (hbd_nt(r[1], g), hbd(r[0], g)))


def _stack(xs):
    return jnp.concatenate([x[None] for x in xs], axis=0)


def _lane_col(x, j):
    lane = lax.broadcasted_iota(jnp.int32, (1, LANE), 1)
    return jnp.sum(jnp.where(lane == j, x, 0.0), axis=-1, keepdims=True)


def _dnc_f(q, k, v, seg, prm):
    C = CHUNK
    beta_all = _sigmoid(seg)
    xx = seg + prm[1:2]
    g_all = -jnp.exp(prm[0:1]) * (jnp.maximum(xx, 0.0) + jnp.log(1.0 + jnp.exp(-jnp.abs(xx))))
    r2 = lax.broadcasted_iota(jnp.int32, (C, C), 0)
    c2 = lax.broadcasted_iota(jnp.int32, (C, C), 1)
    gc_all = _hdot((r2 >= c2).astype(f32), g_all)
    beta = _stack([_lane_col(beta_all, h) for h in range(6)])
    gc = _stack([_lane_col(gc_all, 6 + h) for h in range(6)])
    r = lax.broadcasted_iota(jnp.int32, (1, C, C), 1)
    c = lax.broadcasted_iota(jnp.int32, (1, C, C), 2)
    incl = r >= c
    strict = r > c
    eye = (r == c).astype(f32)
    g_row = hbd(jnp.ones((6, C, C), f32), eye * gc)
    decay = jnp.where(incl, jnp.exp(jnp.where(incl, gc - g_row, 0.0)), 0.0)
    a_mat = beta * hbd_nt(k, k) * jnp.where(strict, decay, 0.0)
    eg = jnp.exp(gc)
    pw = -a_mat
    inv = eye + pw
    for _ in range(5):
        pw = hbd(pw, pw)
        inv = inv + hbd(inv, pw)
    u = hbd(inv, beta * v)
    w = hbd(inv, (beta * eg) * k)
    qc = q * (B_DH ** -0.5)
    attn = hbd_nt(qc, k) * decay
    last = (lax.broadcasted_iota(jnp.int32, (1, C, 1), 1) == C - 1).astype(f32)
    g_last = jnp.sum(gc * last, axis=1, keepdims=True)
    dc = jnp.broadcast_to(jnp.exp(g_last), (6, 1, LANE)).reshape(6, LANE)
    return u, w, qc * eg, k * jnp.exp(g_last - gc), attn, dc


def _dns_f(S0, u, w, qd, kt, attn, dcrows):
    dc = _lane_col(dcrows, 0).reshape(6, 1, 1)
    delta = u - hbd(w, S0)
    out = hbd(qd, S0) + hbd(attn, delta)
    return out, dc * S0 + hbd_tn(kt, delta)


def _dnpost_f(o, z, grow):
    outs = []
    for h in range(6):
        oh = o[:, LANE * h:LANE * (h + 1)]
        outs.append(oh * lax.rsqrt(jnp.mean(oh * oh, axis=-1, keepdims=True) + EPS) * grow
                    * _silu(z[:, LANE * h:LANE * (h + 1)]))
    return jnp.concatenate(outs, axis=1)


def _hs(h):
    return slice(LANE * h, LANE * (h + 1))


def _heads(ref, share):
    return _stack([ref[:, _hs(h // share)] for h in range(6)])


def _put_heads(ref, val):
    for h in range(6):
        ref[:, _hs(h)] = val[h]


def _dnc_in_specs():
    return [
        pl.BlockSpec((CHUNK, B_QK), lambda n: (n, 0)),
        pl.BlockSpec((CHUNK, B_QK), lambda n: (n, 1)),
        pl.BlockSpec((CHUNK, B_V), lambda n: (n, 1)),
        pl.BlockSpec((CHUNK, LANE), lambda n: (n, 20)),
        pl.BlockSpec((8, LANE), lambda n: (0, 0)),
    ]


def _dnc_out_specs(rev_nc=None):
    ci = (lambda n: n) if rev_nc is None else (lambda n: rev_nc - 1 - n)
    wide = pl.BlockSpec((CHUNK, B_V), lambda n: (ci(n), 0))
    return [wide, wide, wide, wide, pl.BlockSpec((1, 6, CHUNK, CHUNK), lambda n: (ci(n), 0, 0, 0)),
            pl.BlockSpec((1, 8, LANE), lambda n: (ci(n), 0, 0))]


def _dnc_shapes(S):
    nc = S // CHUNK
    wide = jax.ShapeDtypeStruct((S, B_V), f32)
    return [wide, wide, wide, wide, jax.ShapeDtypeStruct((nc, 6, CHUNK, CHUNK), f32),
            jax.ShapeDtypeStruct((nc, 8, LANE), f32)]


def dnc_fwd(qkvn, proj, prm):
    S = proj.shape[0]

    def body(q_ref, k_ref, v_ref, s_ref, p_ref, u_ref, w_ref, qd_ref, kt_ref, at_ref, dc_ref):
        u, w, qd, kt, attn, dc = _dnc_f(_heads(q_ref, 2), _heads(k_ref, 2), _heads(v_ref, 1), s_ref[...], p_ref[...])
        _put_heads(u_ref, u)
        _put_heads(w_ref, w)
        _put_heads(qd_ref, qd)
        _put_heads(kt_ref, kt)
        at_ref[0] = attn
        dc_ref[0] = jnp.concatenate([dc, jnp.zeros((2, LANE), f32)], axis=0)

    return pl.pallas_call(
        body, name="dn_chunk_fwd", grid=(S // CHUNK,), in_specs=_dnc_in_specs(), out_specs=_dnc_out_specs(),
        out_shape=_dnc_shapes(S), compiler_params=_cp(("parallel",)),
    )(qkvn, qkvn, qkvn, proj, prm)


def dnc_bwd(qkvn, proj, prm, cots):
    S = proj.shape[0]

    def body(q_ref, k_ref, v_ref, s_ref, p_ref, du_ref, dw_ref, dqd_ref, dkt_ref, dat_ref, ddc_ref,
             dx_ref, dseg_ref, dprm_ref):
        @pl.when(pl.program_id(0) == 0)
        def _():
            dprm_ref[...] = jnp.zeros_like(dprm_ref)
        _, vjp = jax.vjp(_dnc_f, _heads(q_ref, 2), _heads(k_ref, 2), _heads(v_ref, 1), s_ref[...], p_ref[...])
        dq, dk, dv, dseg, dprm = vjp((_heads(du_ref, 1), _heads(dw_ref, 1), _heads(dqd_ref, 1), _heads(dkt_ref, 1),
                                      dat_ref[0], ddc_ref[0, 0:6, :]))
        dx_ref[...] = jnp.concatenate([dq[0] + dq[1], dq[2] + dq[3], dq[4] + dq[5],
                                       dk[0] + dk[1], dk[2] + dk[3], dk[4] + dk[5]] + [dv[h] for h in range(6)], axis=1)
        dseg_ref[...] = dseg
        dprm_ref[...] += dprm

    return pl.pallas_call(
        body, name="dn_chunk_bwd", grid=(S // CHUNK,), in_specs=_dnc_in_specs() + _dnc_out_specs(),
        out_specs=[pl.BlockSpec((CHUNK, B_QKV), lambda n: (n, 0)), pl.BlockSpec((CHUNK, LANE), lambda n: (n, 0)),
                   pl.BlockSpec((8, LANE), lambda n: (0, 0))],
        out_shape=[jax.ShapeDtypeStruct((S, B_QKV), f32), jax.ShapeDtypeStruct((S, LANE), f32),
                   jax.ShapeDtypeStruct((8, LANE), f32)],
        compiler_params=_cp(("arbitrary",)),
    )(qkvn, qkvn, qkvn, proj, prm, *cots)


def dns_fwd(chunked):
    u = chunked[0]
    S = u.shape[0]
    nc = S // CHUNK

    def body(u_ref, w_ref, qd_ref, kt_ref, at_ref, dc_ref, o_ref, st_ref, st):
        @pl.when(pl.program_id(0) == 0)
        def _():
            st[...] = jnp.zeros_like(st)
        S0 = st[...]
        st_ref[0] = S0
        out, S1 = _dns_f(S0, _heads(u_ref, 1), _heads(w_ref, 1), _heads(qd_ref, 1), _heads(kt_ref, 1),
                         at_ref[0], dc_ref[0, 0:6, :])
        _put_heads(o_ref, out)
        st[...] = S1

    return pl.pallas_call(
        body, name="dn_scan_fwd", grid=(nc,), in_specs=_dnc_out_specs(),
        out_specs=[pl.BlockSpec((CHUNK, B_V), lambda n: (n, 0)),
                   pl.BlockSpec((1, 6, B_DH, B_DH), lambda n: (n, 0, 0, 0))],
        out_shape=[jax.ShapeDtypeStruct((S, B_V), f32), jax.ShapeDtypeStruct((nc, 6, B_DH, B_DH), f32)],
        scratch_shapes=[pltpu.VMEM((6, B_DH, B_DH), f32)],
        compiler_params=_cp(("arbitrary",)),
    )(*chunked)


def dns_bwd(chunked, states, do):
    S = do.shape[0]
    nc = S // CHUNK

    def body(u_ref, w_ref, qd_ref, kt_ref, at_ref, dc_ref, st_ref, do_ref,
             du_ref, dw_ref, dqd_ref, dkt_ref, dat_ref, ddc_ref, dst):
        @pl.when(pl.program_id(0) == 0)
        def _():
            dst[...] = jnp.zeros_like(dst)
        _, vjp = jax.vjp(_dns_f, st_ref[0], _heads(u_ref, 1), _heads(w_ref, 1), _heads(qd_ref, 1), _heads(kt_ref, 1),
                         at_ref[0], dc_ref[0, 0:6, :])
        dS0, du, dw, dqd, dkt, dat, ddc = vjp((_heads(do_ref, 1), dst[...]))
        dst[...] = dS0
        _put_heads(du_ref, du)
        _put_heads(dw_ref, dw)
        _put_heads(dqd_ref, dqd)
        _put_heads(dkt_ref, dkt)
        dat_ref[0] = dat
        ddc_ref[0] = jnp.concatenate([ddc, jnp.zeros((2, LANE), f32)], axis=0)

    return pl.pallas_call(
        body, name="dn_scan_bwd", grid=(nc,),
        in_specs=_dnc_out_specs(nc) + [pl.BlockSpec((1, 6, B_DH, B_DH), lambda n: (nc - 1 - n, 0, 0, 0)),
                                       pl.BlockSpec((CHUNK, B_V), lambda n: (nc - 1 - n, 0))],
        out_specs=_dnc_out_specs(nc), out_shape=_dnc_shapes(S),
        scratch_shapes=[pltpu.VMEM((6, B_DH, B_DH), f32)],
        compiler_params=_cp(("arbitrary",)),
    )(*chunked, states, do)


def dnpost_fwd(o, proj, prm):
    S = o.shape[0]
    t = min(512, S)

    def body(o_ref, z_ref, p_ref, y_ref):
        y_ref[...] = _dnpost_f(o_ref[...], z_ref[...], p_ref[2:3, :])

    tok = pl.BlockSpec((t, B_V), lambda i: (i, 0))
    return pl.pallas_call(
        body, name="dn_post_fwd", grid=(S // t,),
        in_specs=[tok, pl.BlockSpec((t, B_V), lambda i: (i, 2)), pl.BlockSpec((8, LANE), lambda i: (0, 0))],
        out_specs=tok, out_shape=jax.ShapeDtypeStruct((S, B_V), f32), compiler_params=_cp(("parallel",)),
    )(o, proj, prm)


def dnpost_bwd(o, proj, prm, dmix):
    S = o.shape[0]
    t = min(512, S)

    def body(o_ref, z_ref, p_ref, dy_ref, do_ref, dz_ref, dg_ref):
        @pl.when(pl.program_id(0) == 0)
        def _():
            dg_ref[...] = jnp.zeros_like(dg_ref)
        _, vjp = jax.vjp(_dnpost_f, o_ref[...], z_ref[...], p_ref[2:3, :])
        do, dz, dg = vjp(dy_ref[...])
        do_ref[...] = do
        dz_ref[...] = dz
        dg_ref[...] += dg

    tok = pl.BlockSpec((t, B_V), lambda i: (i, 0))
    return pl.pallas_call(
        body, name="dn_post_bwd", grid=(S // t,),
        in_specs=[tok, pl.BlockSpec((t, B_V), lambda i: (i, 2)), pl.BlockSpec((8, LANE), lambda i: (0, 0)), tok],
        out_specs=[tok, tok, pl.BlockSpec((1, LANE), lambda i: (0, 0))],
        out_shape=[jax.ShapeDtypeStruct((S, B_V), f32), jax.ShapeDtypeStruct((S, B_V), f32),
                   jax.ShapeDtypeStruct((1, LANE), f32)],
        compiler_params=_cp(("arbitrary",)),
    )(o, proj, prm, dmix)


N_FF_BLK = D_FF // LANE
GU_SHARD = 2 * D_FF // 4


def _glu_f(gate, up, w, b):
    c = w[2:3] * gate + w[1:2] * shift_down(gate, 1) + w[0:1] * shift_down(gate, 2) + b
    return _silu(c) * up


def glu_fwd(gu, w, b, name):
    S = gu.shape[0]

    def body(g_ref, u_ref, w_ref, b_ref, o_ref):
        o_ref[...] = _glu_f(g_ref[...], u_ref[...], w_ref[...], b_ref[...]).astype(bf16)

    col = pl.BlockSpec((S, LANE), lambda j: (0, j))
    return pl.pallas_call(
        body, name=name, grid=(N_FF_BLK,),
        in_specs=[col, pl.BlockSpec((S, LANE), lambda j: (0, N_FF_BLK + j)), pl.BlockSpec((3, LANE), lambda j: (0, j)),
                  pl.BlockSpec((1, LANE), lambda j: (0, j))],
        out_specs=col, out_shape=jax.ShapeDtypeStruct((S, D_FF), bf16), compiler_params=_cp(("parallel",)),
    )(gu, gu, w, b.reshape(1, D_FF))


def glu_bwd(gu, w, b, dact, name):
    S = gu.shape[0]

    def body(g_ref, u_ref, w_ref, b_ref, d_ref, dg_ref, dw_ref, db_ref):
        _, vjp = jax.vjp(_glu_f, g_ref[...], u_ref[...], w_ref[...], b_ref[...])
        dg, du, dw, db = vjp(d_ref[...])
        dg_ref[0] = dg.astype(bf16)
        dg_ref[1] = du.astype(bf16)
        dw_ref[...] = dw
        db_ref[...] = db

    col = pl.BlockSpec((S, LANE), lambda j: (0, j))
    wsp = pl.BlockSpec((3, LANE), lambda j: (0, j))
    bsp = pl.BlockSpec((1, LANE), lambda j: (0, j))
    return pl.pallas_call(
        body, name=name, grid=(N_FF_BLK,),
        in_specs=[col, pl.BlockSpec((S, LANE), lambda j: (0, N_FF_BLK + j)), wsp, bsp, col],
        out_specs=[pl.BlockSpec((2, S, LANE), lambda j: (0, 0, j)), wsp, bsp],
        out_shape=[jax.ShapeDtypeStruct((2, S, D_FF), bf16), jax.ShapeDtypeStruct((3, D_FF), f32),
                   jax.ShapeDtypeStruct((1, D_FF), f32)],
        compiler_params=_cp(("parallel",)),
    )(gu, gu, w, b.reshape(1, D_FF), dact)


def gu_fwd(n2, wg, name):
    S = n2.shape[0]
    tm = min(512, S)

    def body(a_ref, w_ref, o_ref):
        o_ref[...] = _dg(a_ref[...], w_ref[...], 1, 0)

    return pl.pallas_call(
        body, name=name, grid=(4, S // tm),
        in_specs=[pl.BlockSpec((tm, D), lambda s, m: (m, 0)), pl.BlockSpec((None, D, GU_SHARD), lambda s, m: (s, 0, 0))],
        out_specs=pl.BlockSpec((tm, GU_SHARD), lambda s, m: (m, s)),
        out_shape=jax.ShapeDtypeStruct((S, 2 * D_FF), f32), compiler_params=_cp(("parallel", "parallel")),
    )(n2, wg)


def gu_bwd_x(dgu, wg, name):
    S = dgu.shape[1]
    tm = min(512, S)

    def body(d_ref, w_ref, o_ref):
        @pl.when(pl.program_id(1) == 0)
        def _():
            o_ref[...] = jnp.zeros_like(o_ref)
        o_ref[...] += _dg(d_ref[...], w_ref[...], 1, 1)

    return pl.pallas_call(
        body, name=name, grid=(S // tm, 4),
        in_specs=[pl.BlockSpec((None, tm, GU_SHARD), lambda m, s: (s // 2, m, s % 2)),
                  pl.BlockSpec((None, D, GU_SHARD), lambda m, s: (s, 0, 0))],
        out_specs=pl.BlockSpec((tm, D), lambda m, s: (m, 0)),
        out_shape=jax.ShapeDtypeStruct((S, D), f32), compiler_params=_cp(("parallel", "arbitrary")),
    )(dgu, wg)


def gu_bwd_w(n2, dgu, name):
    S = n2.shape[0]
    tm = min(512, S)
    nm = S // tm

    def body(a_ref, d_ref, o_ref, acc):
        @pl.when(pl.program_id(1) == 0)
        def _():
            acc[...] = jnp.zeros_like(acc)
        acc[...] += _dg(a_ref[...], d_ref[...], 0, 0)

        @pl.when(pl.program_id(1) == nm - 1)
        def _():
            o_ref[...] = acc[...].astype(bf16)

    return pl.pallas_call(
        body, name=name, grid=(4, nm),
        in_specs=[pl.BlockSpec((tm, D), lambda s, m: (m, 0)),
                  pl.BlockSpec((None, tm, GU_SHARD), lambda s, m: (s // 2, m, s % 2))],
        out_specs=pl.BlockSpec((None, D, GU_SHARD), lambda s, m: (s, 0, 0)),
        out_shape=jax.ShapeDtypeStruct((4, D, GU_SHARD), bf16),
        scratch_shapes=[pltpu.VMEM((D, GU_SHARD), f32)],
        compiler_params=_cp(("parallel", "arbitrary")),
    )(n2, dgu)


def _pair_cols(w):
    lead = w.shape[:-1]
    return w.reshape(lead + (2, 6, A_DH)).swapaxes(-3, -2).reshape(lead + (A_Q,))


def _unpair_cols(w):
    lead = w.shape[:-1]
    return w.reshape(lead + (6, 2, A_DH)).swapaxes(-3, -2).reshape(lead + (A_Q,))


def _lay_in_a(w):
    return jnp.concatenate([_pair_cols(w[:, :A_Q]), w[:, A_Q:]], axis=1)


def _unlay_in_a(w):
    return jnp.concatenate([_unpair_cols(w[:, :A_Q]), w[:, A_Q:]], axis=1)


def _lay_out_a(w):
    return jnp.concatenate([_pair_cols(w[:A_Q].T).T, w[A_Q:]], axis=0)


def _unlay_out_a(w):
    return jnp.concatenate([_unpair_cols(w[:A_Q].T).T, w[A_Q:]], axis=0)


def _lay_in_b(w):
    return jnp.concatenate([w[:, :2304], w[:, 2316:], w[:, 2304:2316],
                            jnp.zeros((w.shape[0], LANE - 12), w.dtype)], axis=1)


def _unlay_in_b(w):
    return jnp.concatenate([w[:, :2304], w[:, 2560:2572], w[:, 2304:2560]], axis=1)


def _chip_cols(w):
    return jnp.moveaxis(w.reshape(w.shape[0], 4, w.shape[1] // 4), 1, 0)


def _unchip_cols(w):
    return jnp.moveaxis(w, 0, 1).reshape(w.shape[1], 4 * w.shape[2])


def _local_step(x, mem, target, P):
    arrive = P.get("arrive", lambda key, after: None)
    ready = P.get("ready", lambda key, grads, dep: dep)
    sk = jnp.zeros((16, LANE), f32).at[:A_HEADS].set(jnp.broadcast_to(P["sinks"][:, None], (A_HEADS, LANE)))
    prm = jnp.zeros((8, LANE), f32).at[0, 6:12].set(P["a_log"]).at[1, 6:12].set(P["dt_bias"]).at[2].set(P["out_norm_g"])
    bias = bias_build(P["rel_bias"])
    saved = []
    h = x
    for i in range(2):
        n1 = rms_fwd(h, P["g_mix"][i], f"rms_mix{i}")
        arrive(("w_in", i), n1)
        proj = mm_nn(n1, P["w_in_a"] if i == 0 else P["w_in_b"], name="proj_a" if i == 0 else "proj_b")
        arrive(("w_mem", i), proj)
        kv = memkv_fwd(mem, P["g_mem"][i], P["w_mem"][i], f"memkv{i}")
        if i == 0:
            self_out = swa_fwd(proj, bias, sk)
            cross = xattn_fwd(proj, A_Q + 2 * LANE, kv, "xattn_a")
            extra = ()
        else:
            qkvn = dnprep_fwd(proj, P["conv_qkv"])
            chunked = dnc_fwd(qkvn, proj, prm)
            o, states = dns_fwd(chunked)
            self_out = dnpost_fwd(o, proj, prm)
            cross = xattn_fwd(proj, 2304, kv, "xattn_b")
            extra = (qkvn, chunked, states, o)
        mix = jnp.concatenate([self_out, cross], axis=1)
        arrive(("w_out", i), cross)
        h2 = mm_nn(mix, P["w_out"][i], res=h, name=f"out_proj{i}")
        n2 = rms_fwd(h2, P["g_ffn"][i], f"rms_ffn{i}")
        arrive(("w_gu", i), n2)
        gu = gu_fwd(n2, P["w_gu"][i], f"gate_up{i}")
        act = glu_fwd(gu, P["ffn_cw"][i], P["ffn_cb"][i], f"glu{i}")
        arrive(("w_down", i), act)
        h3 = mm_nn(act, P["w_down"][i], res=h2, name=f"down{i}")
        saved.append((h, n1, kv, proj, mix, h2, n2, gu, act, extra))
        h = h3

    loss, dh, dg_fin = loss_head(h, P["g_fin"], target)
    G = {"g_fin": dg_fin[0], "g_mix": [None, None], "g_mem": [None, None], "g_ffn": [None, None],
         "w_mem": [None, None], "w_out": [None, None], "w_gu": [None, None], "w_down": [None, None],
         "ffn_cw": [None, None], "ffn_cb": [None, None]}
    for i in (1, 0):
        hin, n1, kv, proj, mix, h2, n2, gu, act, extra = saved[i]
        dact = mm_nt(dh, P["w_down"][i], name=f"d_act{i}")
        G["w_down"][i] = mm_tn(act, dh, name=f"dw_down{i}")
        dgu, dcw, dcb = glu_bwd(gu, P["ffn_cw"][i], P["ffn_cb"][i], dact, f"glu_bwd{i}")
        G["ffn_cw"][i], G["ffn_cb"][i] = dcw, dcb[0]
        dn2 = gu_bwd_x(dgu, P["w_gu"][i], f"d_n2_{i}")
        G["w_gu"][i] = gu_bwd_w(n2, dgu, f"dw_gu{i}")
        g_ffn = ready(("ffn", i), G, P["g_ffn"][i])
        dh2, dg = rms_bwd(h2, g_ffn, dn2, dh, f"rms_ffn_bwd{i}")
        G["g_ffn"][i] = dg[0]
        dmix = mm_nt(dh2, P["w_out"][i], name=f"d_mix{i}")
        G["w_out"][i] = mm_tn(mix, dh2, name=f"dw_out{i}")
        if i == 0:
            dqkv, dbias, dsk = swa_bwd(proj, bias, sk, dmix)
            dxq, dkv = xattn_bwd(proj, A_Q + 2 * LANE, kv, dmix, "xattn_a_bwd")
            dproj = jnp.concatenate([dqkv, dxq], axis=1)
            G["sinks"] = dsk[:A_HEADS, 0]
            G["rel_bias"] = bias_grad(dbias)[:, :A_HEADS]
            w_in, gname = P["w_in_a"], "w_in_a"
        else:
            qkvn, chunked, states, o = extra
            do, dz, dgo = dnpost_bwd(o, proj, prm, dmix)
            dqkvn, dseg, dprm = dnc_bwd(qkvn, proj, prm, dns_bwd(chunked, states, do))
            draw, dconv = dnprep_bwd(proj, P["conv_qkv"], dqkvn)
            dxq, dkv = xattn_bwd(proj, 2304, kv, dmix, "xattn_b_bwd")
            dproj = jnp.concatenate([draw, dz, dxq, dseg], axis=1)
            G["conv_qkv"] = dconv
            G["a_log"], G["dt_bias"], G["out_norm_g"] = dprm[0, 6:12], dprm[1, 6:12], dgo[0]
            w_in, gname = P["w_in_b"], "w_in_b"
        dn1 = mm_nt(dproj, w_in, name=f"d_n1_{i}")
        G[gname] = mm_tn(n1, dproj, name=f"d{gname}")
        dh, dg = rms_bwd(hin, P["g_mix"][i], dn1, dh2, f"rms_mix_bwd{i}")
        G["g_mix"][i] = dg[0]
        dgm, dwm = memkv_bwd(mem, P["g_mem"][i], P["w_mem"][i], dkv, f"memkv_bwd{i}")
        G["g_mem"][i], G["w_mem"][i] = dgm[0], dwm
        ready(("mix", i), G, None)
    return loss, dh, G


def _prepare(full, w_gu=None):
    return {
        "rel_bias": full["rel_bias"], "sinks": full["sinks_a"][0], "a_log": full["a_log_b"][0],
        "dt_bias": full["dt_bias_b"][0], "out_norm_g": full["out_norm_g_b"][0],
        "g_mix": full["norm_mix_g"], "g_mem": full["norm_mem_g"], "g_ffn": full["norm_ffn_g"],
        "g_fin": full["final_norm_g"], "conv_qkv": full["conv_qkv_b"][0],
        "ffn_cw": [full["ffn_conv_w"][0], full["ffn_conv_w"][1]],
        "ffn_cb": [full["ffn_conv_b"][0], full["ffn_conv_b"][1]],
        "w_mem": [full["w_mem_kv"][0], full["w_mem_kv"][1]],
        "w_out": [_lay_out_a(full["w_out"][0]), full["w_out"][1]],
        "w_in_a": _lay_in_a(full["w_in_a"][0]), "w_in_b": _lay_in_b(full["w_in_b"][0]),
        "w_gu": w_gu if w_gu is not None else [_chip_cols(full["w_gate_up"][0]), _chip_cols(full["w_gate_up"][1])],
        "w_down": [full["w_down"][0], full["w_down"][1]],
    }


def _grads_to_ref(G):
    return {
        "rel_bias": G["rel_bias"], "norm_mix_g": jnp.stack(G["g_mix"]), "norm_mem_g": jnp.stack(G["g_mem"]),
        "w_mem_kv": jnp.stack(G["w_mem"]),
        "w_out": jnp.stack([_unlay_out_a(G["w_out"][0]), G["w_out"][1]]),
        "w_in_a": _unlay_in_a(G["w_in_a"])[None], "sinks_a": G["sinks"][None],
        "w_in_b": _unlay_in_b(G["w_in_b"])[None], "conv_qkv_b": G["conv_qkv"][None],
        "a_log_b": G["a_log"][None], "dt_bias_b": G["dt_bias"][None], "out_norm_g_b": G["out_norm_g"][None],
        "norm_ffn_g": jnp.stack(G["g_ffn"]),
        "w_gate_up": jnp.stack([_unchip_cols(G["w_gu"][0]), _unchip_cols(G["w_gu"][1])]).astype(f32),
        "ffn_conv_w": jnp.stack(G["ffn_cw"]), "ffn_conv_b": jnp.stack(G["ffn_cb"]),
        "w_down": jnp.stack(G["w_down"]), "final_norm_g": G["g_fin"],
    }


ANY = pl.BlockSpec(memory_space=pl.ANY)


def _place():
    return lax.axis_index("x"), lax.axis_index("y"), lax.axis_index("c")


def chip_scatter(gs):
    n = len(gs)

    def body(*refs):
        ins, outs = refs[:n], refs[n:2 * n]
        ssem, rsem = refs[2 * n:]
        x, y, c = _place()
        me = 2 * x + y
        peers = [(1 - x, y), (x, 1 - y), (1 - x, 1 - y)]

        def remote(j, k, slot):
            px, py = peers[k]
            return pltpu.make_async_remote_copy(
                src_ref=ins[j].at[2 * px + py], dst_ref=outs[j].at[slot],
                send_sem=ssem.at[3 * j + k], recv_sem=rsem.at[3 * j + k],
                device_id=(px, py, c), device_id_type=MESH)

        sends = [remote(j, k, me) for j in range(n) for k in range(3)]
        for cp in sends:
            cp.start()
        for j in range(n):
            for k in range(3):
                px, py = peers[k]
                remote(j, k, 2 * px + py).wait_recv()
        for cp in sends:
            cp.wait_send()

    return pl.pallas_call(
        body, name="grad_scatter", in_specs=[ANY] * n, out_specs=[ANY] * n,
        out_shape=[jax.ShapeDtypeStruct(g.shape, g.dtype) for g in gs],
        scratch_shapes=[pltpu.SemaphoreType.DMA((3 * n,)), pltpu.SemaphoreType.DMA((3 * n,))],
    )(*gs)


def allreduce_small(buf):
    R = buf.shape[0]

    def body(b_ref, o_ref, recv, ssem, rsem):
        x, y, c = _place()
        me = 4 * x + 2 * y + c

        def peer(k):
            return (1 - x if k & 4 else x, 1 - y if k & 2 else y, 1 - c if k & 1 else c)

        def remote(k, slot):
            return pltpu.make_async_remote_copy(
                src_ref=b_ref, dst_ref=recv.at[slot], send_sem=ssem.at[k - 1], recv_sem=rsem.at[k - 1],
                device_id=peer(k), device_id_type=MESH)

        sends = [remote(k, me) for k in range(1, 8)]
        for cp in sends:
            cp.start()
        recv[me] = b_ref[...]
        for k in range(1, 8):
            px, py, pc = peer(k)
            remote(k, 4 * px + 2 * py + pc).wait_recv()
        for cp in sends:
            cp.wait_send()
        total = recv[0]
        for j in range(1, 8):
            total = total + recv[j]
        o_ref[...] = total

    return pl.pallas_call(
        body, name="small_allreduce",
        in_specs=[pl.BlockSpec(memory_space=pltpu.VMEM)], out_specs=pl.BlockSpec(memory_space=pltpu.VMEM),
        out_shape=jax.ShapeDtypeStruct(buf.shape, f32),
        scratch_shapes=[pltpu.VMEM((8, R, LANE), f32), pltpu.SemaphoreType.DMA((7,)), pltpu.SemaphoreType.DMA((7,))],
    )(buf)


def sum_slots(own, recv, chip, core, name):
    _, R, C = recv.shape
    tr = _row_tile(R, 256)
    nt = R // tr

    def body(p_ref, a_ref, r_ref, o_ref):
        acc = jnp.zeros((tr, C), f32)
        for s in range(4):
            acc = acc + jnp.where(p_ref[0] == s, a_ref[s], r_ref[s]).astype(f32)
        o_ref[...] = acc

    slots = pl.BlockSpec((4, tr, C), lambda i, p_ref: (0, i, 0))
    return pl.pallas_call(
        body, name=name, out_shape=jax.ShapeDtypeStruct((2 * R, C), f32),
        grid_spec=pltpu.PrefetchScalarGridSpec(
            num_scalar_prefetch=1, grid=(nt,), in_specs=[slots, slots],
            out_specs=pl.BlockSpec((tr, C), lambda i, p_ref: (p_ref[1] * nt + i, 0))),
        compiler_params=_cp(("parallel",)),
    )(jnp.stack([chip, core]).astype(jnp.int32), own, recv)


def _half(ref, core, axis=0):
    half = ref.shape[axis] // 2
    idx = (slice(None),) * axis + (pl.ds(core * half, half),)
    return ref.at[idx]


IN_HBM = pl.BlockSpec(memory_space=pltpu.HBM)
IN_SEM = pl.BlockSpec(memory_space=pltpu.SEMAPHORE)
SIDE_EFFECT = pltpu.SideEffectType.DATAFLOW_SIDE_EFFECTING


def _gather_copy(buf, i, k, ssem, rsem, place, landing):
    x, y, c = place
    px, py = [(1 - x, y), (x, 1 - y), (1 - x, 1 - y)][k]
    me = 2 * x + y
    return pltpu.make_async_remote_copy(
        src_ref=buf.at[me], dst_ref=buf.at[me if landing == "theirs" else 2 * px + py],
        send_sem=ssem.at[3 * i + k], recv_sem=rsem.at[3 * i + k], device_id=(px, py, c), device_id_type=MESH)


def gather_start(groups):
    flat = [b for grp in groups for b in grp]
    n, ng = len(flat), len(groups)

    def body(*refs):
        bufs, sems = refs[:n], refs[n:n + 2 * ng]
        place = _place()
        j = 0
        for g, grp in enumerate(groups):
            for i in range(len(grp)):
                for k in range(3):
                    _gather_copy(bufs[j], i, k, sems[2 * g], sems[2 * g + 1], place, "theirs").start()
                j += 1

    sem_shapes = [pltpu.SemaphoreType.DMA((3 * len(grp),)) for grp in groups for _ in range(2)]
    out = pl.pallas_call(
        body, name="gather_start", in_specs=[IN_HBM] * n, out_specs=(*[IN_SEM] * (2 * ng), *[IN_HBM] * n),
        out_shape=(*sem_shapes, *[pltpu.HBM(b.shape, b.dtype) for b in flat]),
        input_output_aliases={i: 2 * ng + i for i in range(n)},
        compiler_params=pltpu.CompilerParams(has_side_effects=SIDE_EFFECT),
    )(*[pltpu.with_memory_space_constraint(b, pltpu.HBM) for b in flat])
    sems, bufs = out[:2 * ng], list(out[2 * ng:])
    flights, j = [], 0
    for g, grp in enumerate(groups):
        flights.append((bufs[j:j + len(grp)], sems[2 * g], sems[2 * g + 1]))
        j += len(grp)
    return flights


def gather_wait(flight, after, name):
    bufs, ssem, rsem = flight
    n = len(bufs)

    def body(*refs):
        place = _place()
        for i in range(n):
            for k in range(3):
                cp = _gather_copy(refs[i], i, k, refs[n], refs[n + 1], place, "mine")
                cp.wait_send()
                cp.wait_recv()

    return pl.pallas_call(
        body, name=name, in_specs=[IN_HBM] * n + [IN_SEM, IN_SEM, ANY], out_specs=[IN_HBM] * n,
        out_shape=[pltpu.HBM(b.shape, b.dtype) for b in bufs], input_output_aliases={i: i for i in range(n)},
        compiler_params=pltpu.CompilerParams(has_side_effects=SIDE_EFFECT),
    )(*bufs, ssem, rsem, after)


def _scatter_copy(src, land, j, k, ssem, rsem, place, landing):
    x, y, c = place
    px, py = [(1 - x, y), (x, 1 - y), (1 - x, 1 - y)][k]
    return pltpu.make_async_remote_copy(
        src_ref=src.at[2 * px + py], dst_ref=land.at[2 * x + y if landing == "theirs" else 2 * px + py],
        send_sem=ssem.at[3 * j + k], recv_sem=rsem.at[3 * j + k], device_id=(px, py, c), device_id_type=MESH)


def scatter_start(srcs, name):
    n = len(srcs)
    lands = [lax.empty(g.shape, g.dtype) for g in srcs]

    def body(*refs):
        place = _place()
        for j in range(n):
            for k in range(3):
                _scatter_copy(refs[j], refs[n + j], j, k, refs[2 * n], refs[2 * n + 1], place, "theirs").start()

    sem = pltpu.SemaphoreType.DMA((3 * n,))
    hbm = [pltpu.with_memory_space_constraint(b, pltpu.HBM) for b in list(srcs) + lands]
    out = pl.pallas_call(
        body, name=name, in_specs=[IN_HBM] * (2 * n), out_specs=(IN_SEM, IN_SEM, *[IN_HBM] * (2 * n)),
        out_shape=(sem, sem, *[pltpu.HBM(b.shape, b.dtype) for b in hbm]),
        input_output_aliases={i: 2 + i for i in range(2 * n)},
        compiler_params=pltpu.CompilerParams(has_side_effects=SIDE_EFFECT),
    )(*hbm)
    return list(out[2:2 + n]), list(out[2 + n:]), out[0], out[1]


def scatter_wait(flight, after, name):
    srcs, lands, ssem, rsem = flight
    n = len(srcs)

    def body(*refs):
        place = _place()
        for j in range(n):
            for k in range(3):
                cp = _scatter_copy(refs[j], refs[n + j], j, k, refs[2 * n], refs[2 * n + 1], place, "mine")
                cp.wait_send()
                cp.wait_recv()

    out = pl.pallas_call(
        body, name=name, in_specs=[IN_HBM] * (2 * n) + [IN_SEM, IN_SEM, ANY], out_specs=[IN_HBM] * (2 * n),
        out_shape=[pltpu.HBM(b.shape, b.dtype) for b in list(srcs) + list(lands)],
        input_output_aliases={i: i for i in range(2 * n)},
        compiler_params=pltpu.CompilerParams(has_side_effects=SIDE_EFFECT),
    )(*srcs, *lands, ssem, rsem, after)
    return list(out[:n]), list(out[n:])


def pair_exchange(gbufs, name):
    n = len(gbufs)

    def body(*refs):
        ins, outs = refs[:n], refs[n:2 * n]
        ssem, rsem = refs[2 * n:]
        x, y, c = _place()
        cps = [pltpu.make_async_remote_copy(
            src_ref=_half(ins[j], 1 - c, axis=1), dst_ref=outs[j], send_sem=ssem.at[j], recv_sem=rsem.at[j],
            device_id=(x, y, 1 - c), device_id_type=MESH) for j in range(n)]
        for cp in cps:
            cp.start()
        for cp in cps:
            cp.wait()

    return pl.pallas_call(
        body, name=name, in_specs=[ANY] * n, out_specs=[ANY] * n,
        out_shape=[jax.ShapeDtypeStruct((4, g.shape[1] // 2, g.shape[2]), g.dtype) for g in gbufs],
        scratch_shapes=[pltpu.SemaphoreType.DMA((n,)), pltpu.SemaphoreType.DMA((n,))],
    )(*gbufs)


def _row_tile(rows, cap=512):
    return max(t for t in range(16, min(rows, cap) + 1, 16) if rows % t == 0)


def pair_sum(mine, theirs, core, name):
    _, R, C = mine.shape
    half = R // 2
    tr = _row_tile(half)
    nt = half // tr

    def body(c_ref, a_ref, b_ref, o_ref):
        o_ref[...] = (a_ref[...].astype(f32) + b_ref[...].astype(f32)).astype(bf16)

    return pl.pallas_call(
        body, name=name, out_shape=jax.ShapeDtypeStruct(theirs.shape, bf16),
        grid_spec=pltpu.PrefetchScalarGridSpec(
            num_scalar_prefetch=1, grid=(4, nt),
            in_specs=[pl.BlockSpec((None, tr, C), lambda s, i, c_ref: (s, c_ref[0] * nt + i, 0)),
                      pl.BlockSpec((None, tr, C), lambda s, i, c_ref: (s, i, 0))],
            out_specs=pl.BlockSpec((None, tr, C), lambda s, i, c_ref: (s, i, 0))),
        compiler_params=_cp(("parallel", "parallel")),
    )(jnp.reshape(core, (1,)).astype(jnp.int32), mine, theirs)


def final_exchange(fins):
    n = len(fins)

    def body(*refs):
        outs = refs[n:2 * n]
        ssem, rsem = refs[2 * n:]
        x, y, c = _place()
        cps = [pltpu.make_async_remote_copy(
            src_ref=_half(outs[j], c), dst_ref=_half(outs[j], c), send_sem=ssem.at[j], recv_sem=rsem.at[j],
            device_id=(x, y, 1 - c), device_id_type=MESH) for j in range(n)]
        for cp in cps:
            cp.start()
        for cp in cps:
            cp.wait()

    return pl.pallas_call(
        body, name="final_exchange", in_specs=[ANY] * n, out_specs=[ANY] * n,
        out_shape=[jax.ShapeDtypeStruct(f.shape, f.dtype) for f in fins],
        input_output_aliases={j: j for j in range(n)},
        scratch_shapes=[pltpu.SemaphoreType.DMA((n,)), pltpu.SemaphoreType.DMA((n,))],
    )(*fins)


def adamw_big(w, m, v, gs, row0, name):
    L, R, C = w.shape
    tr = _row_tile(math.gcd(R, row0) if row0 else R, max(16, 262144 // C // 16 * 16))
    b0 = row0 // tr

    def body(*refs):
        w_ref, m_ref, v_ref = refs[:3]
        g_refs = refs[3:3 + L]
        g_ref, d_ref, nm_ref, nv_ref = refs[3 + L:]
        g = g_refs[0][...]
        for l in range(1, L):
            g = jnp.where(pl.program_id(0) == l, g_refs[l][...], g)
        d, nm, nv = _adamw_math(w_ref[...], g, m_ref[...], v_ref[...])
        g_ref[...] = g
        d_ref[...] = d
        nm_ref[...] = nm
        nv_ref[...] = nv

    own = pl.BlockSpec((None, tr, C), lambda l, i: (l, i, 0))
    off = pl.BlockSpec((tr, C), lambda l, i: (b0 + i, 0))
    return pl.pallas_call(
        body, name=name, grid=(L, R // tr), in_specs=[own, own, own] + [off] * L, out_specs=[own] * 4,
        out_shape=[jax.ShapeDtypeStruct((L, R, C), f32)] * 4, compiler_params=_cp(("parallel", "parallel")),
    )(w, m, v, *gs)


def _adamw_math(w, g, m, v):
    m = B1 * m + (1.0 - B1) * g
    v = B2 * v + (1.0 - B2) * (g * g)
    m_hat = m / (1.0 - B1 ** STEP)
    v_hat = v / (1.0 - B2 ** STEP)
    delta = -LR * (m_hat / (jnp.sqrt(v_hat) + AEPS) + WD * w)
    return delta, m, v


def adamw_small(w, m, v, g):
    def body(w_ref, m_ref, v_ref, g_ref, d_ref, nm_ref, nv_ref):
        d, nm, nv = _adamw_math(w_ref[...], g_ref[...], m_ref[...], v_ref[...])
        d_ref[...] = d
        nm_ref[...] = nm
        nv_ref[...] = nv

    return pl.pallas_call(body, name="adamw_small", out_shape=[jax.ShapeDtypeStruct(w.shape, f32)] * 3)(w, m, v, g)


CONV =(("conv_qkv_b", 2), ("ffn_conv_w", 2))
SMALL = ("rel_bias", "norm_mix_g", "norm_mem_g", "sinks_a", "a_log_b", "dt_bias_b", "out_norm_g_b", "norm_ffn_g",
         "ffn_conv_b", "final_norm_g")
WEIGHTS = ("rel_bias", "norm_mix_g", "norm_mem_g", "w_mem_kv", "w_out", "w_in_a", "sinks_a", "w_in_b", "conv_qkv_b",
           "a_log_b", "dt_bias_b", "out_norm_g_b", "norm_ffn_g", "w_gate_up", "ffn_conv_w", "ffn_conv_b", "w_down",
           "final_norm_g")
ARGS = ("x", "mem") + WEIGHTS + ("loss_target",) + tuple("m_" + n for n in WEIGHTS) + tuple("v_" + n for n in WEIGHTS)


def _rows(a, width):
    flat = a.reshape(-1)
    pad = (-flat.shape[0]) % (8 * width)
    if pad:
        flat = jnp.concatenate([flat, jnp.zeros((pad,), a.dtype)])
    return flat.reshape(-1, width)


def _nrows(shape, width):
    return _pad_to(-(-math.prod(shape) // width), 8)


def _pack(arrs, width, total_rows, dtype):
    parts = [_rows(a.astype(dtype), width) for a in arrs]
    used = sum(p.shape[0] for p in parts)
    if total_rows > used:
        parts.append(jnp.zeros((total_rows - used, width), dtype))
    return jnp.concatenate(parts, axis=0)


def _unpack(buf, shapes, width):
    out, r = [], 0
    for s in shapes:
        n = _nrows(s, width)
        out.append(buf[r:r + n].reshape(-1)[:math.prod(s)].reshape(s))
        r += n
    return out


def _pad_to(n, mult):
    return -(-n // mult) * mult


def kernel(x, mem, rel_bias, norm_mix_g, norm_mem_g, w_mem_kv, w_out, w_in_a, sinks_a, w_in_b, conv_qkv_b, a_log_b, dt_bias_b, out_norm_g_b, norm_ffn_g, w_gate_up, ffn_conv_w, ffn_conv_b, w_down, final_norm_g, loss_target, m_rel_bias, m_norm_mix_g, m_norm_mem_g, m_w_mem_kv, m_w_out, m_w_in_a, m_sinks_a, m_w_in_b, m_conv_qkv_b, m_a_log_b, m_dt_bias_b, m_out_norm_g_b, m_norm_ffn_g, m_w_gate_up, m_ffn_conv_w, m_ffn_conv_b, m_w_down, m_final_norm_g, v_rel_bias, v_norm_mix_g, v_norm_mem_g, v_w_mem_kv, v_w_out, v_w_in_a, v_sinks_a, v_w_in_b, v_conv_qkv_b, v_a_log_b, v_dt_bias_b, v_out_norm_g_b, v_norm_ffn_g, v_w_gate_up, v_ffn_conv_w, v_ffn_conv_b, v_w_down, v_final_norm_g):
    A = dict(zip(ARGS, (x, mem, rel_bias, norm_mix_g, norm_mem_g, w_mem_kv, w_out, w_in_a, sinks_a, w_in_b, conv_qkv_b, a_log_b, dt_bias_b, out_norm_g_b, norm_ffn_g, w_gate_up, ffn_conv_w, ffn_conv_b, w_down, final_norm_g, loss_target, m_rel_bias, m_norm_mix_g, m_norm_mem_g, m_w_mem_kv, m_w_out, m_w_in_a, m_sinks_a, m_w_in_b, m_conv_qkv_b, m_a_log_b, m_dt_bias_b, m_out_norm_g_b, m_norm_ffn_g, m_w_gate_up, m_ffn_conv_w, m_ffn_conv_b, m_w_down, m_final_norm_g, v_rel_bias, v_norm_mix_g, v_norm_mem_g, v_w_mem_kv, v_w_out, v_w_in_a, v_sinks_a, v_w_in_b, v_conv_qkv_b, v_a_log_b, v_dt_bias_b, v_out_norm_g_b, v_norm_ffn_g, v_w_gate_up, v_ffn_conv_w, v_ffn_conv_b, v_w_down, v_final_norm_g)))
    chip = 2 * lax.axis_index("x") + lax.axis_index("y")
    core = lax.axis_index("c")
    n_down, n_out, n_mem = w_down.shape[1], w_out.shape[1], w_mem_kv.shape[1]

    def own_slot(shard):
        return lax.dynamic_update_index_in_dim(lax.empty((4,) + shard.shape, shard.dtype), shard, chip, 0)

    def bslot(w):
        return own_slot(w.astype(bf16))

    groups = {
        ("w_in", 0): [bslot(w_in_a[0])],
        ("w_mem", 0): [bslot(w_mem_kv[0]), bslot(w_mem_kv[1]), own_slot(conv_qkv_b[0]),
                       own_slot(ffn_conv_w.reshape(6, -1))],
        ("w_out", 0): [bslot(w_out[0])], ("w_gu", 0): [bslot(w_gate_up[0])], ("w_down", 0): [bslot(w_down[0])],
        ("w_in", 1): [bslot(w_in_b[0])],
        ("w_out", 1): [bslot(w_out[1])], ("w_gu", 1): [bslot(w_gate_up[1])], ("w_down", 1): [bslot(w_down[1])],
    }
    flights = dict(zip(groups, gather_start(list(groups.values()))))
    P = {"rel_bias": rel_bias, "sinks": sinks_a[0], "a_log": a_log_b[0], "dt_bias": dt_bias_b[0],
         "out_norm_g": out_norm_g_b[0], "g_mix": norm_mix_g, "g_mem": norm_mem_g, "g_ffn": norm_ffn_g,
         "g_fin": final_norm_g, "ffn_cb": [ffn_conv_b[0], ffn_conv_b[1]], "w_mem": [None, None], "w_out": [None, None],
         "w_gu": [None, None], "w_down": [None, None], "ffn_cw": [None, None]}

    def rows4(g):
        return g.reshape(4 * g.shape[1], g.shape[2])

    def arrive(key, after):
        if key not in flights:
            return
        got = gather_wait(flights.pop(key), after, "gather_wait_%s%d" % key)
        name, i = key
        if name == "w_in":
            P["w_in_a" if i == 0 else "w_in_b"] = (_lay_in_a if i == 0 else _lay_in_b)(_unchip_cols(got[0]))
        elif name == "w_mem":
            P["w_mem"] = [rows4(got[0]), rows4(got[1])]
            P["conv_qkv"] = _unchip_cols(got[2])
            cw = _unchip_cols(got[3]).reshape(2, 3, D_FF)
            P["ffn_cw"] = [cw[0], cw[1]]
        elif name == "w_out":
            P["w_out"][i] = _lay_out_a(rows4(got[0])) if i == 0 else rows4(got[0])
        elif name == "w_gu":
            P["w_gu"][i] = got[0]
        else:
            P["w_down"][i] = rows4(got[0])

    def chip_rows(g):
        return g.reshape(4, g.shape[0] // 4, g.shape[-1])

    sent = {}

    def ready(key, G, dep):
        kind, i = key
        tag = "%s%d" % key
        if kind == "ffn":
            names, partial = ("gu", "down"), [G["w_gu"][i], chip_rows(G["w_down"][i]).astype(bf16)]
        else:
            g_out = _unlay_out_a(G["w_out"][0]) if i == 0 else G["w_out"][1]
            g_in = _unlay_in_a(G["w_in_a"]) if i == 0 else _unlay_in_b(G["w_in_b"])
            names = ("out", "in", "mem")
            partial = [chip_rows(g_out).astype(bf16), _chip_cols(g_in).astype(bf16), chip_rows(G["w_mem"][i]).astype(bf16)]
        theirs = pair_exchange(partial, "pair_exchange_" + tag)
        pair = [pair_sum(p, t, core, "pair_sum_%s%d" % (nm, i)) for p, t, nm in zip(partial, theirs, names)]
        if key == ("mix", 0):
            sent[key] = (names, pair, chip_scatter(pair))
            return dep
        sent[key] = (names, scatter_start(pair, "scatter_start_" + tag))
        return dep

    P["arrive"], P["ready"] = arrive, ready

    loss, dx, G = _local_step(x[0], mem[0], loss_target[0], P)
    gfull = _grads_to_ref(G)

    fin = {}
    for key in (("ffn", 1), ("mix", 1), ("ffn", 0), ("mix", 0)):
        if key == ("mix", 0):
            names, pair, arrived = sent[key]
        else:
            names, flight = sent[key]
            pair, arrived = scatter_wait(flight, dx, "scatter_wait_%s%d" % key)
        for nm, p, r in zip(names, pair, arrived):
            fin[nm, key[1]] = sum_slots(p, r, chip, core, "sum_slots_%s%d" % (nm, key[1]))
    order = list(fin)
    done = dict(zip(order, final_exchange([fin[k] for k in order])))

    sm_shapes = [A[n].shape for n in SMALL] + [gfull[n].shape for n, _ in CONV] + [(LANE,)]
    sm_rows = _pad_to(sum(_nrows(s, LANE) for s in sm_shapes), 8)
    sbuf = _pack([gfull[n] for n in SMALL] + [gfull[n] for n, _ in CONV] + [loss[0]], LANE, sm_rows, f32)
    tot = _unpack(allreduce_small(sbuf), sm_shapes, LANE)
    gsmall = dict(zip(SMALL, tot[:len(SMALL)]))
    for (n, axis), t in zip(CONV, tot[len(SMALL):len(SMALL) + len(CONV)]):
        sh = A[n].shape[axis]
        gsmall[n] = lax.dynamic_slice_in_dim(t, chip * sh, sh, axis)
    loss_out = tot[-1][0]

    out = {}
    plan = (("w_gate_up", [done["gu", 0], done["gu", 1]]), ("w_down", [done["down", 0], done["down", 1]]),
            ("w_out", [done["out", 0], done["out", 1]]), ("w_mem_kv", [done["mem", 0], done["mem", 1]]),
            ("w_in_a", [done["in", 0]]), ("w_in_b", [done["in", 1]]))
    for n, gs in plan:
        shape3 = (len(gs),) + gs[0].shape
        res = adamw_big(A[n].reshape(shape3), A["m_" + n].reshape(shape3), A["v_" + n].reshape(shape3), gs, 0,
                        "adamw_" + n)
        for key, r in zip(("grad_", "delta_", "new_m_", "new_v_"), res):
            out[key + n] = r.reshape(A[n].shape)
    names = SMALL + tuple(n for n, _ in CONV)
    shapes = [A[n].shape for n in names]
    rows = _pad_to(sum(_nrows(s, LANE) for s in shapes), 8)
    packs = [_pack([src[n] for n in names], LANE, rows, f32)
             for src in ({n: A[n] for n in names}, {n: A["m_" + n] for n in names}, {n: A["v_" + n] for n in names}, gsmall)]
    res = adamw_small(*packs)
    for key, r in zip(("delta_", "new_m_", "new_v_"), res):
        for n, a in zip(names, _unpack(r, shapes, LANE)):
            out[key + n] = a
    for n in names:
        out["grad_" + n] = gsmall[n]
    return (loss_out, dx[None], *[out["grad_" + n] for n in WEIGHTS], *[out["delta_" + n] for n in WEIGHTS],
            *[out["new_m_" + n] for n in WEIGHTS], *[out["new_v_" + n] for n in WEIGHTS])
```

```python
import functools
import math

import numpy as np
import jax
import jax.numpy as jnp
from jax import lax
from jax.experimental import pallas as pl
from jax.experimental.pallas import tpu as pltpu

f32 = jnp.float32
bf16 = jnp.bfloat16
HI = lax.Precision.HIGHEST
MESH = pl.DeviceIdType.MESH

D = 1024
MEM_LEN = 256
EPS = 1e-6
A_HEADS, A_KV, A_DH = 12, 2, 64
A_Q = 768
BLK = 128
N_BUCKETS, MAX_DIST = 32, 128
B_QK, B_V, B_DH = 384, 768, 128
B_QKV = 1536
CHUNK = 64
X_Q = 256
D_FF = 2816
IN_A = 1280
IN_B = 2572
IN_B_PAD = 2688
LANE = 128
VMEM_LIMIT = 56 * 1024 * 1024

LR, B1, B2, AEPS, WD, STEP = 0.001, 0.9, 0.999, 1e-08, 0.01, 10


def _cp(sem=None):
    return pltpu.CompilerParams(dimension_semantics=sem, vmem_limit_bytes=VMEM_LIMIT)


def _dg(a, b, ca, cb, prec=None):
    return lax.dot_general(a, b, (((ca,), (cb,)), ((), ())), precision=prec, preferred_element_type=f32)


@jax.custom_vjp
def bdot(a, b):
    return _dg(a.astype(bf16), b.astype(bf16), 1, 0)


def _bdot_f(a, b):
    return bdot(a, b), (a, b)


def _bdot_b(res, g):
    a, b = res
    gb = g.astype(bf16)
    return _dg(gb, b.astype(bf16), 1, 1), _dg(a.astype(bf16), gb, 0, 0)


bdot.defvjp(_bdot_f, _bdot_b)


@jax.custom_vjp
def bdot_nt(a, b):
    return _dg(a.astype(bf16), b.astype(bf16), 1, 1)


def _bdot_nt_f(a, b):
    return bdot_nt(a, b), (a, b)


def _bdot_nt_b(res, g):
    a, b = res
    gb = g.astype(bf16)
    return _dg(gb, b.astype(bf16), 1, 0), _dg(gb, a.astype(bf16), 0, 0)


bdot_nt.defvjp(_bdot_nt_f, _bdot_nt_b)


def _shift_rows(x, s, down):
    n = x.shape[0]
    row = lax.broadcasted_iota(jnp.int32, x.shape, 0)
    if down:
        return jnp.where(row >= s, pltpu.roll(x, s, 0), 0.0)
    return jnp.where(row < n - s, pltpu.roll(x, n - s, 0), 0.0)


@functools.partial(jax.custom_vjp, nondiff_argnums=(1,))
def shift_down(x, s):
    return _shift_rows(x, s, True)


def _sd_f(x, s):
    return _shift_rows(x, s, True), None


def _sd_b(s, _, g):
    return (_shift_rows(g, s, False),)


shift_down.defvjp(_sd_f, _sd_b)


def _sigmoid(x):
    return 1.0 / (1.0 + jnp.exp(-x))


def _silu(x):
    return x * _sigmoid(x)


def _rms(x, g):
    return x * lax.rsqrt(jnp.mean(x * x, axis=-1, keepdims=True) + EPS) * g


def _tile(n, cap):
    u = n // LANE
    best = 1
    for d in range(1, u + 1):
        if u % d == 0 and d * LANE <= cap:
            best = d
    return best * LANE


def mm_nn(a, w, res=None, out_dtype=f32, name="mm_nn"):
    M, K = a.shape
    N = w.shape[1]
    tm, tn = min(512, M), _tile(N, 1024)

    def body(*refs):
        if res is None:
            a_ref, w_ref, o_ref = refs
            o_ref[...] = _dg(a_ref[...].astype(bf16), w_ref[...], 1, 0).astype(out_dtype)
        else:
            a_ref, w_ref, r_ref, o_ref = refs
            o_ref[...] = (r_ref[...] + _dg(a_ref[...].astype(bf16), w_ref[...], 1, 0)).astype(out_dtype)

    in_specs = [pl.BlockSpec((tm, K), lambda n, m: (m, 0)), pl.BlockSpec((K, tn), lambda n, m: (0, n))]
    args = [a, w]
    if res is not None:
        in_specs.append(pl.BlockSpec((tm, tn), lambda n, m: (m, n)))
        args.append(res)
    return pl.pallas_call(
        body, name=name, grid=(N // tn, M // tm), in_specs=in_specs,
        out_specs=pl.BlockSpec((tm, tn), lambda n, m: (m, n)),
        out_shape=jax.ShapeDtypeStruct((M, N), out_dtype),
        compiler_params=_cp(("parallel", "parallel")),
    )(*args)


def mm_nt(dy, w, name="mm_nt"):
    M, N = dy.shape
    K = w.shape[0]
    tm, tn = min(512, M), _tile(N, 1024)

    def body(dy_ref, w_ref, o_ref):
        @pl.when(pl.program_id(1) == 0)
        def _():
            o_ref[...] = jnp.zeros_like(o_ref)
        o_ref[...] += _dg(dy_ref[...].astype(bf16), w_ref[...], 1, 1)

    return pl.pallas_call(
        body, name=name, grid=(M // tm, N // tn),
        in_specs=[pl.BlockSpec((tm, tn), lambda m, n: (m, n)), pl.BlockSpec((K, tn), lambda m, n: (0, n))],
        out_specs=pl.BlockSpec((tm, K), lambda m, n: (m, 0)),
        out_shape=jax.ShapeDtypeStruct((M, K), f32),
        compiler_params=_cp(("parallel", "arbitrary")),
    )(dy, w)


def mm_tn(a, dy, name="mm_tn"):
    M, K = a.shape
    N = dy.shape[1]
    tm, tk, tn = min(512, M), _tile(K, 1408), _tile(N, 1024)

    def body(a_ref, dy_ref, o_ref):
        @pl.when(pl.program_id(2) == 0)
        def _():
            o_ref[...] = jnp.zeros_like(o_ref)
        o_ref[...] += _dg(a_ref[...].astype(bf16), dy_ref[...].astype(bf16), 0, 0)

    return pl.pallas_call(
        body, name=name, grid=(K // tk, N // tn, M // tm),
        in_specs=[pl.BlockSpec((tm, tk), lambda k, n, m: (m, k)), pl.BlockSpec((tm, tn), lambda k, n, m: (m, n))],
        out_specs=pl.BlockSpec((tk, tn), lambda k, n, m: (k, n)),
        out_shape=jax.ShapeDtypeStruct((K, N), f32),
        compiler_params=_cp(("parallel", "parallel", "arbitrary")),
    )(a, dy)


def rms_fwd(h, g, name):
    S = h.shape[0]
    t = min(512, S)

    def body(h_ref, g_ref, o_ref):
        o_ref[...] = _rms(h_ref[...], g_ref[...]).astype(bf16)

    return pl.pallas_call(
        body, name=name, grid=(S // t,),
        in_specs=[pl.BlockSpec((t, D), lambda i: (i, 0)), pl.BlockSpec((1, D), lambda i: (0, 0))],
        out_specs=pl.BlockSpec((t, D), lambda i: (i, 0)),
        out_shape=jax.ShapeDtypeStruct((S, D), bf16),
        compiler_params=_cp(("parallel",)),
    )(h, g.reshape(1, D))


def rms_bwd(h, g, dn, dres, name):
    S = h.shape[0]
    t = min(512, S)

    def body(h_ref, g_ref, dn_ref, dr_ref, dh_ref, dg_ref):
        @pl.when(pl.program_id(0) == 0)
        def _():
            dg_ref[...] = jnp.zeros_like(dg_ref)
        _, vjp = jax.vjp(_rms, h_ref[...], g_ref[...])
        dh, dg = vjp(dn_ref[...])
        dh_ref[...] = dr_ref[...] + dh
        dg_ref[...] += dg

    tok = pl.BlockSpec((t, D), lambda i: (i, 0))
    vec = pl.BlockSpec((1, D), lambda i: (0, 0))
    return pl.pallas_call(
        body, name=name, grid=(S // t,), in_specs=[tok, vec, tok, tok], out_specs=[tok, vec],
        out_shape=[jax.ShapeDtypeStruct((S, D), f32), jax.ShapeDtypeStruct((1, D), f32)],
        compiler_params=_cp(("arbitrary",)),
    )(h, g.reshape(1, D), dn, dres)


def loss_head(h, g, target):
    S = h.shape[0]
    t = min(512, S)

    def f(hh, gg, tt):
        err = _rms(hh, gg) - tt
        return 0.5 * jnp.sum(jnp.mean(err * err, axis=-1, keepdims=True), axis=0, keepdims=True)

    def body(h_ref, g_ref, t_ref, loss_ref, dh_ref, dg_ref):
        @pl.when(pl.program_id(0) == 0)
        def _():
            dg_ref[...] = jnp.zeros_like(dg_ref)
            loss_ref[...] = jnp.zeros_like(loss_ref)
        val, vjp = jax.vjp(lambda a, b: f(a, b, t_ref[...]), h_ref[...], g_ref[...])
        dh, dg = vjp(jnp.ones((1, 1), f32))
        dh_ref[...] = dh
        dg_ref[...] += dg
        loss_ref[...] += jnp.broadcast_to(val, loss_ref.shape)

    tok = pl.BlockSpec((t, D), lambda i: (i, 0))
    vec = pl.BlockSpec((1, D), lambda i: (0, 0))
    return pl.pallas_call(
        body, name="loss_head", grid=(S // t,), in_specs=[tok, vec, tok],
        out_specs=[pl.BlockSpec((1, LANE), lambda i: (0, 0)), tok, vec],
        out_shape=[jax.ShapeDtypeStruct((1, LANE), f32), jax.ShapeDtypeStruct((S, D), f32),
                   jax.ShapeDtypeStruct((1, D), f32)],
        compiler_params=_cp(("arbitrary",)),
    )(h, g.reshape(1, D), target)


def memkv_fwd(mem, g, w, name):
    def body(m_ref, g_ref, w_ref, o_ref):
        o_ref[...] = _dg(_rms(m_ref[...], g_ref[...]).astype(bf16), w_ref[...], 1, 0)

    return pl.pallas_call(
        body, name=name, out_shape=jax.ShapeDtypeStruct((MEM_LEN, 2 * X_Q), f32), compiler_params=_cp(),
    )(mem, g.reshape(1, D), w)


def memkv_bwd(mem, g, w, dkv, name):
    def body(m_ref, g_ref, w_ref, d_ref, dg_ref, dw_ref):
        n, vjp = jax.vjp(lambda gg: _rms(m_ref[...], gg), g_ref[...])
        db = d_ref[...].astype(bf16)
        dw_ref[...] = _dg(n.astype(bf16), db, 0, 0)
        dg_ref[...] = vjp(_dg(db, w_ref[...], 1, 1))[0]

    return pl.pallas_call(
        body, name=name,
        out_shape=[jax.ShapeDtypeStruct((1, D), f32), jax.ShapeDtypeStruct((D, 2 * X_Q), f32)],
        compiler_params=_cp(),
    )(mem, g.reshape(1, D), w, dkv)


def _xattn_f(xq, mk, mv):
    lane = lax.broadcasted_iota(jnp.int32, (1, X_Q), 1)
    out = jnp.zeros(xq.shape, f32)
    for hd in range(4):
        msk = (lane // 64 == hd).astype(f32)
        s = bdot_nt(xq * msk, mk) * (64 ** -0.5)
        m = lax.stop_gradient(jnp.max(s, axis=-1, keepdims=True))
        p = jnp.exp(s - m)
        p = p / jnp.sum(p, axis=-1, keepdims=True)
        out = out + bdot(p, mv * msk)
    return out


def xattn_fwd(proj, col, kv, name):
    S = proj.shape[0]
    t = min(512, S)
    cb = col // X_Q

    def body(q_ref, k_ref, v_ref, o_ref):
        o_ref[...] = _xattn_f(q_ref[...], k_ref[...], v_ref[...])

    return pl.pallas_call(
        body, name=name, grid=(S // t,),
        in_specs=[pl.BlockSpec((t, X_Q), lambda i: (i, cb)), pl.BlockSpec((MEM_LEN, X_Q), lambda i: (0, 0)),
                  pl.BlockSpec((MEM_LEN, X_Q), lambda i: (0, 1))],
        out_specs=pl.BlockSpec((t, X_Q), lambda i: (i, 0)),
        out_shape=jax.ShapeDtypeStruct((S, X_Q), f32),
        compiler_params=_cp(("parallel",)),
    )(proj, kv, kv)


def xattn_bwd(proj, col, kv, dmix, name):
    S = proj.shape[0]
    t = min(512, S)
    cb = col // X_Q

    def body(q_ref, k_ref, v_ref, do_ref, dq_ref, dk_ref, dv_ref):
        @pl.when(pl.program_id(0) == 0)
        def _():
            dk_ref[...] = jnp.zeros_like(dk_ref)
            dv_ref[...] = jnp.zeros_like(dv_ref)
        _, vjp = jax.vjp(_xattn_f, q_ref[...], k_ref[...], v_ref[...])
        dq, dk, dv = vjp(do_ref[...])
        dq_ref[...] = dq
        dk_ref[...] += dk
        dv_ref[...] += dv

    kvb = pl.BlockSpec((MEM_LEN, X_Q), lambda i: (0, 0))
    dq, dk, dv = pl.pallas_call(
        body, name=name, grid=(S // t,),
        in_specs=[pl.BlockSpec((t, X_Q), lambda i: (i, cb)), kvb,
                  pl.BlockSpec((MEM_LEN, X_Q), lambda i: (0, 1)), pl.BlockSpec((t, X_Q), lambda i: (i, 3))],
        out_specs=[pl.BlockSpec((t, X_Q), lambda i: (i, 0)), kvb, kvb],
        out_shape=[jax.ShapeDtypeStruct((S, X_Q), f32), jax.ShapeDtypeStruct((MEM_LEN, X_Q), f32),
                   jax.ShapeDtypeStruct((MEM_LEN, X_Q), f32)],
        compiler_params=_cp(("arbitrary",)),
    )(proj, kv, kv, dmix)
    return dq, jnp.concatenate([dk, dv], axis=1)


def _bucket_map():
    qi = np.arange(BLK)[:, None]
    kj = np.arange(2 * BLK)[None, :]
    n = np.maximum(BLK + qi - kj, 0)
    max_exact = N_BUCKETS // 2
    nf = np.maximum(n, 1).astype(np.float64)
    large = max_exact + (np.log(nf / max_exact) / math.log(MAX_DIST / max_exact)
                         * (N_BUCKETS - max_exact)).astype(np.int32)
    large = np.minimum(large, N_BUCKETS - 1)
    return np.where(n < max_exact, n, large).astype(np.int32)


def bias_build(rel_bias):
    def body(rb_ref, bk_ref, o_ref):
        bk = bk_ref[...]
        for h in range(A_HEADS):
            acc = jnp.zeros((BLK, 2 * BLK), f32)
            for b in range(N_BUCKETS):
                acc = jnp.where(bk == b, rb_ref[b, h], acc)
            o_ref[h] = acc

    return pl.pallas_call(
        body, name="bias_build",
        in_specs=[pl.BlockSpec(memory_space=pltpu.SMEM), pl.BlockSpec(memory_space=pltpu.VMEM)],
        out_specs=pl.BlockSpec(memory_space=pltpu.VMEM),
        out_shape=jax.ShapeDtypeStruct((A_HEADS, BLK, 2 * BLK), f32), compiler_params=_cp(),
    )(rel_bias, jnp.asarray(_bucket_map()))


def bias_grad(dbias):
    def body(d_ref, bk_ref, o_ref):
        bk = bk_ref[...]
        row = lax.broadcasted_iota(jnp.int32, (N_BUCKETS, LANE), 0)
        lane = lax.broadcasted_iota(jnp.int32, (N_BUCKETS, LANE), 1)
        acc = jnp.zeros((N_BUCKETS, LANE), f32)
        for h in range(A_HEADS):
            d = d_ref[h]
            for b in range(N_BUCKETS):
                s = jnp.sum(jnp.where(bk == b, d, 0.0), keepdims=True)
                acc = acc + jnp.where((row == b) & (lane == h), s, 0.0)
        o_ref[...] = acc

    return pl.pallas_call(
        body, name="bias_grad", out_shape=jax.ShapeDtypeStruct((N_BUCKETS, LANE), f32), compiler_params=_cp(),
    )(dbias, jnp.asarray(_bucket_map()))


def _swa_f(qb, kp, kc, vp, vc, bias, sk, first):
    kband = jnp.concatenate([kp, kc], axis=0)
    vband = jnp.concatenate([vp, vc], axis=0)
    qi = lax.broadcasted_iota(jnp.int32, (BLK, 2 * BLK), 0)
    kj = lax.broadcasted_iota(jnp.int32, (BLK, 2 * BLK), 1)
    rel = kj - qi
    ok = (rel >= 1) & (rel <= BLK) & ((kj >= BLK) | jnp.logical_not(first))
    lane = lax.broadcasted_iota(jnp.int32, (1, LANE), 1)
    lane_b = lax.broadcasted_iota(jnp.int32, (BLK, LANE), 1)
    outs = []
    for p in range(A_HEADS // 2):
        qp = qb[:, LANE * p:LANE * (p + 1)]
        acc = jnp.zeros((BLK, LANE), f32)
        for g in range(2):
            h = g * (A_HEADS // 2) + p
            msk = (lane // A_DH == g).astype(f32)
            s = bdot_nt(qp * msk, kband) * (A_DH ** -0.5) + bias[h]
            s = jnp.where(ok, s, -1e30)
            skb = jnp.broadcast_to(sk[h:h + 1, :], (BLK, LANE))
            sink = jnp.sum(jnp.where(lane_b == 0, skb, 0.0), axis=-1, keepdims=True)
            m = lax.stop_gradient(jnp.maximum(jnp.max(s, axis=-1, keepdims=True), sink))
            e = jnp.exp(s - m)
            prob = e / (jnp.sum(e, axis=-1, keepdims=True) + jnp.exp(sink - m))
            acc = acc + bdot(prob, vband) * msk
        outs.append(acc)
    return jnp.concatenate(outs, axis=1)


def _swa_specs(nb, rev):
    bi = (lambda i: nb - 1 - i) if rev else (lambda i: i)
    return [
        pl.BlockSpec((BLK, A_Q), lambda i: (bi(i), 0)),
        pl.BlockSpec((BLK, LANE), lambda i: (jnp.maximum(bi(i) - 1, 0), 6)),
        pl.BlockSpec((BLK, LANE), lambda i: (bi(i), 6)),
        pl.BlockSpec((BLK, LANE), lambda i: (jnp.maximum(bi(i) - 1, 0), 7)),
        pl.BlockSpec((BLK, LANE), lambda i: (bi(i), 7)),
        pl.BlockSpec((A_HEADS, BLK, 2 * BLK), lambda i: (0, 0, 0)),
        pl.BlockSpec((16, LANE), lambda i: (0, 0)),
    ]


def swa_fwd(proj, bias, sk):
    S = proj.shape[0]
    nb = S // BLK

    def body(q_ref, kp_ref, kc_ref, vp_ref, vc_ref, b_ref, s_ref, o_ref):
        o_ref[...] = _swa_f(q_ref[...], kp_ref[...], kc_ref[...], vp_ref[...], vc_ref[...], b_ref[...], s_ref[...],
                            pl.program_id(0) == 0)

    return pl.pallas_call(
        body, name="swa_fwd", grid=(nb,), in_specs=_swa_specs(nb, False),
        out_specs=pl.BlockSpec((BLK, A_Q), lambda i: (i, 0)),
        out_shape=jax.ShapeDtypeStruct((S, A_Q), f32), compiler_params=_cp(("parallel",)),
    )(proj, proj, proj, proj, proj, bias, sk)


def swa_bwd(proj, bias, sk, dmix):
    S = proj.shape[0]
    nb = S // BLK

    def body(q_ref, kp_ref, kc_ref, vp_ref, vc_ref, b_ref, s_ref, do_ref, dqkv_ref, db_ref, ds_ref, ck, cv):
        i = pl.program_id(0)

        @pl.when(i == 0)
        def _():
            db_ref[...] = jnp.zeros_like(db_ref)
            ds_ref[...] = jnp.zeros_like(ds_ref)
            ck[...] = jnp.zeros_like(ck)
            cv[...] = jnp.zeros_like(cv)
        first = i == nb - 1
        _, vjp = jax.vjp(lambda *a: _swa_f(*a, first), q_ref[...], kp_ref[...], kc_ref[...], vp_ref[...],
                         vc_ref[...], b_ref[...], s_ref[...])
        dq, dkp, dkc, dvp, dvc, db, ds = vjp(do_ref[...])
        dqkv_ref[...] = jnp.concatenate([dq, dkc + ck[...], dvc + cv[...]], axis=1)
        ck[...] = dkp
        cv[...] = dvp
        db_ref[...] += db
        ds_ref[...] += ds

    return pl.pallas_call(
        body, name="swa_bwd", grid=(nb,),
        in_specs=_swa_specs(nb, True) + [pl.BlockSpec((BLK, A_Q), lambda i: (nb - 1 - i, 0))],
        out_specs=[pl.BlockSpec((BLK, D), lambda i: (nb - 1 - i, 0)),
                   pl.BlockSpec((A_HEADS, BLK, 2 * BLK), lambda i: (0, 0, 0)),
                   pl.BlockSpec((16, LANE), lambda i: (0, 0))],
        out_shape=[jax.ShapeDtypeStruct((S, D), f32), jax.ShapeDtypeStruct((A_HEADS, BLK, 2 * BLK), f32),
                   jax.ShapeDtypeStruct((16, LANE), f32)],
        scratch_shapes=[pltpu.VMEM((BLK, LANE), f32), pltpu.VMEM((BLK, LANE), f32)],
        compiler_params=_cp(("arbitrary",)),
    )(proj, proj, proj, proj, proj, bias, sk, dmix)


def _dnprep_f(x, w, is_qk):
    c = (w[3:4] * x + w[2:3] * shift_down(x, 1) + w[1:2] * shift_down(x, 2) + w[0:1] * shift_down(x, 3))
    a = _silu(c)
    n = a * lax.rsqrt(jnp.sum(a * a, axis=-1, keepdims=True) + EPS)
    return jnp.where(is_qk, n, a)


def dnprep_fwd(proj, cw):
    S = proj.shape[0]
    nblk = B_QKV // LANE

    def body(x_ref, w_ref, o_ref):
        o_ref[...] = _dnprep_f(x_ref[...], w_ref[...], pl.program_id(0) < 2 * B_QK // LANE)

    return pl.pallas_call(
        body, name="dnprep_fwd", grid=(nblk,),
        in_specs=[pl.BlockSpec((S, LANE), lambda j: (0, j)), pl.BlockSpec((4, LANE), lambda j: (0, j))],
        out_specs=pl.BlockSpec((S, LANE), lambda j: (0, j)),
        out_shape=jax.ShapeDtypeStruct((S, B_QKV), f32), compiler_params=_cp(("parallel",)),
    )(proj, cw)


def dnprep_bwd(proj, cw, dqkvn):
    S = proj.shape[0]
    nblk = B_QKV // LANE

    def body(x_ref, w_ref, d_ref, dx_ref, dw_ref):
        is_qk = pl.program_id(0) < 2 * B_QK // LANE
        _, vjp = jax.vjp(lambda a, b: _dnprep_f(a, b, is_qk), x_ref[...], w_ref[...])
        dx, dw = vjp(d_ref[...])
        dx_ref[...] = dx
        dw_ref[...] = dw

    col = pl.BlockSpec((S, LANE), lambda j: (0, j))
    wsp = pl.BlockSpec((4, LANE), lambda j: (0, j))
    return pl.pallas_call(
        body, name="dnprep_bwd", grid=(nblk,), in_specs=[col, wsp, col], out_specs=[col, wsp],
        out_shape=[jax.ShapeDtypeStruct((S, B_QKV), f32), jax.ShapeDtypeStruct((4, B_QKV), f32)],
        compiler_params=_cp(("parallel",)),
    )(proj, cw, dqkvn)


def _hdot(a, b, ca=1, cb=0):
    return _dg(a, b, ca, cb, HI)


def _bdg(a, b, ca, cb):
    dn = (((ca,), (cb,)), ((0,), (0,)))
    ah, bh = a.astype(bf16), b.astype(bf16)
    al, bl = (a - ah.astype(f32)).astype(bf16), (b - bh.astype(f32)).astype(bf16)
    return (lax.dot_general(ah, bh, dn, preferred_element_type=f32)
            + lax.dot_general(ah, bl, dn, preferred_element_type=f32)
            + lax.dot_general(al, bh, dn, preferred_element_type=f32))


@jax.custom_vjp
def hbd(a, b):
    return _bdg(a, b, 2, 1)


@jax.custom_vjp
def hbd_nt(a, b):
    return _bdg(a, b, 2, 2)


@jax.custom_vjp
def hbd_tn(a, b):
    return _bdg(a, b, 1, 1)


hbd.defvjp(lambda a, b: (hbd(a, b), (a, b)), lambda r, g: (hbd_nt(g, r[1]), hbd_tn(r[0], g)))
hbd_nt.defvjp(lambda a, b: (hbd_nt(a, b), (a, b)), lambda r, g: (hbd(g, r[1]), hbd_tn(g, r[0])))
hbd_tn.defvjp(lambda a, b: (hbd_tn(a, b), (a, b)), lambda r, g: (hbd_nt(r[1], g), hbd(r[0], g)))


def _stack(xs):
    return jnp.concatenate([x[None] for x in xs], axis=0)


def _lane_col(x, j):
    lane = lax.broadcasted_iota(jnp.int32, (1, LANE), 1)
    return jnp.sum(jnp.where(lane == j, x, 0.0), axis=-1, keepdims=True)


def _dnc_f(q, k, v, seg, prm):
    C = CHUNK
    beta_all = _sigmoid(seg)
    xx = seg + prm[1:2]
    g_all = -jnp.exp(prm[0:1]) * (jnp.maximum(xx, 0.0) + jnp.log(1.0 + jnp.exp(-jnp.abs(xx))))
    r2 = lax.broadcasted_iota(jnp.int32, (C, C), 0)
    c2 = lax.broadcasted_iota(jnp.int32, (C, C), 1)
    gc_all = _hdot((r2 >= c2).astype(f32), g_all)
    beta = _stack([_lane_col(beta_all, h) for h in range(6)])
    gc = _stack([_lane_col(gc_all, 6 + h) for h in range(6)])
    r = lax.broadcasted_iota(jnp.int32, (1, C, C), 1)
    c = lax.broadcasted_iota(jnp.int32, (1, C, C), 2)
    incl = r >= c
    strict = r > c
    eye = (r == c).astype(f32)
    g_row = hbd(jnp.ones((6, C, C), f32), eye * gc)
    decay = jnp.where(incl, jnp.exp(jnp.where(incl, gc - g_row, 0.0)), 0.0)
    a_mat = beta * hbd_nt(k, k) * jnp.where(strict, decay, 0.0)
    eg = jnp.exp(gc)
    pw = -a_mat
    inv = eye + pw
    for _ in range(5):
        pw = hbd(pw, pw)
        inv = inv + hbd(inv, pw)
    u = hbd(inv, beta * v)
    w = hbd(inv, (beta * eg) * k)
    qc = q * (B_DH ** -0.5)
    attn = hbd_nt(qc, k) * decay
    last = (lax.broadcasted_iota(jnp.int32, (1, C, 1), 1) == C - 1).astype(f32)
    g_last = jnp.sum(gc * last, axis=1, keepdims=True)
    dc = jnp.broadcast_to(jnp.exp(g_last), (6, 1, LANE)).reshape(6, LANE)
    return u, w, qc * eg, k * jnp.exp(g_last - gc), attn, dc


def _dns_f(S0, u, w, qd, kt, attn, dcrows):
    dc = _lane_col(dcrows, 0).reshape(6, 1, 1)
    delta = u - hbd(w, S0)
    out = hbd(qd, S0) + hbd(attn, delta)
    return out, dc * S0 + hbd_tn(kt, delta)


def _dnpost_f(o, z, grow):
    outs = []
    for h in range(6):
        oh = o[:, LANE * h:LANE * (h + 1)]
        outs.append(oh * lax.rsqrt(jnp.mean(oh * oh, axis=-1, keepdims=True) + EPS) * grow
                    * _silu(z[:, LANE * h:LANE * (h + 1)]))
    return jnp.concatenate(outs, axis=1)


def _hs(h):
    return slice(LANE * h, LANE * (h + 1))


def _heads(ref, share):
    return _stack([ref[:, _hs(h // share)] for h in range(6)])


def _put_heads(ref, val):
    for h in range(6):
        ref[:, _hs(h)] = val[h]


def _dnc_in_specs():
    return [
        pl.BlockSpec((CHUNK, B_QK), lambda n: (n, 0)),
        pl.BlockSpec((CHUNK, B_QK), lambda n: (n, 1)),
        pl.BlockSpec((CHUNK, B_V), lambda n: (n, 1)),
        pl.BlockSpec((CHUNK, LANE), lambda n: (n, 20)),
        pl.BlockSpec((8, LANE), lambda n: (0, 0)),
    ]


def _dnc_out_specs(rev_nc=None):
    ci = (lambda n: n) if rev_nc is None else (lambda n: rev_nc - 1 - n)
    wide = pl.BlockSpec((CHUNK, B_V), lambda n: (ci(n), 0))
    return [wide, wide, wide, wide, pl.BlockSpec((1, 6, CHUNK, CHUNK), lambda n: (ci(n), 0, 0, 0)),
            pl.BlockSpec((1, 8, LANE), lambda n: (ci(n), 0, 0))]


def _dnc_shapes(S):
    nc = S // CHUNK
    wide = jax.ShapeDtypeStruct((S, B_V), f32)
    return [wide, wide, wide, wide, jax.ShapeDtypeStruct((nc, 6, CHUNK, CHUNK), f32),
            jax.ShapeDtypeStruct((nc, 8, LANE), f32)]


def dnc_fwd(qkvn, proj, prm):
    S = proj.shape[0]

    def body(q_ref, k_ref, v_ref, s_ref, p_ref, u_ref, w_ref, qd_ref, kt_ref, at_ref, dc_ref):
        u, w, qd, kt, attn, dc = _dnc_f(_heads(q_ref, 2), _heads(k_ref, 2), _heads(v_ref, 1), s_ref[...], p_ref[...])
        _put_heads(u_ref, u)
        _put_heads(w_ref, w)
        _put_heads(qd_ref, qd)
        _put_heads(kt_ref, kt)
        at_ref[0] = attn
        dc_ref[0] = jnp.concatenate([dc, jnp.zeros((2, LANE), f32)], axis=0)

    return pl.pallas_call(
        body, name="dn_chunk_fwd", grid=(S // CHUNK,), in_specs=_dnc_in_specs(), out_specs=_dnc_out_specs(),
        out_shape=_dnc_shapes(S), compiler_params=_cp(("parallel",)),
    )(qkvn, qkvn, qkvn, proj, prm)


def dnc_bwd(qkvn, proj, prm, cots):
    S = proj.shape[0]

    def body(q_ref, k_ref, v_ref, s_ref, p_ref, du_ref, dw_ref, dqd_ref, dkt_ref, dat_ref, ddc_ref,
             dx_ref, dseg_ref, dprm_ref):
        @pl.when(pl.program_id(0) == 0)
        def _():
            dprm_ref[...] = jnp.zeros_like(dprm_ref)
        _, vjp = jax.vjp(_dnc_f, _heads(q_ref, 2), _heads(k_ref, 2), _heads(v_ref, 1), s_ref[...], p_ref[...])
        dq, dk, dv, dseg, dprm = vjp((_heads(du_ref, 1), _heads(dw_ref, 1), _heads(dqd_ref, 1), _heads(dkt_ref, 1),
                                      dat_ref[0], ddc_ref[0, 0:6, :]))
        dx_ref[...] = jnp.concatenate([dq[0] + dq[1], dq[2] + dq[3], dq[4] + dq[5],
                                       dk[0] + dk[1], dk[2] + dk[3], dk[4] + dk[5]] + [dv[h] for h in range(6)], axis=1)
        dseg_ref[...] = dseg
        dprm_ref[...] += dprm

    return pl.pallas_call(
        body, name="dn_chunk_bwd", grid=(S // CHUNK,), in_specs=_dnc_in_specs() + _dnc_out_specs(),
        out_specs=[pl.BlockSpec((CHUNK, B_QKV), lambda n: (n, 0)), pl.BlockSpec((CHUNK, LANE), lambda n: (n, 0)),
                   pl.BlockSpec((8, LANE), lambda n: (0, 0))],
        out_shape=[jax.ShapeDtypeStruct((S, B_QKV), f32), jax.ShapeDtypeStruct((S, LANE), f32),
                   jax.ShapeDtypeStruct((8, LANE), f32)],
        compiler_params=_cp(("arbitrary",)),
    )(qkvn, qkvn, qkvn, proj, prm, *cots)


def dns_fwd(chunked):
    u = chunked[0]
    S = u.shape[0]
    nc = S // CHUNK

    def body(u_ref, w_ref, qd_ref, kt_ref, at_ref, dc_ref, o_ref, st_ref, st):
        @pl.when(pl.program_id(0) == 0)
        def _():
            st[...] = jnp.zeros_like(st)
        S0 = st[...]
        st_ref[0] = S0
        out, S1 = _dns_f(S0, _heads(u_ref, 1), _heads(w_ref, 1), _heads(qd_ref, 1), _heads(kt_ref, 1),
                         at_ref[0], dc_ref[0, 0:6, :])
        _put_heads(o_ref, out)
        st[...] = S1

    return pl.pallas_call(
        body, name="dn_scan_fwd", grid=(nc,), in_specs=_dnc_out_specs(),
        out_specs=[pl.BlockSpec((CHUNK, B_V), lambda n: (n, 0)),
                   pl.BlockSpec((1, 6, B_DH, B_DH), lambda n: (n, 0, 0, 0))],
        out_shape=[jax.ShapeDtypeStruct((S, B_V), f32), jax.ShapeDtypeStruct((nc, 6, B_DH, B_DH), f32)],
        scratch_shapes=[pltpu.VMEM((6, B_DH, B_DH), f32)],
        compiler_params=_cp(("arbitrary",)),
    )(*chunked)


def dns_bwd(chunked, states, do):
    S = do.shape[0]
    nc = S // CHUNK

    def body(u_ref, w_ref, qd_ref, kt_ref, at_ref, dc_ref, st_ref, do_ref,
             du_ref, dw_ref, dqd_ref, dkt_ref, dat_ref, ddc_ref, dst):
        @pl.when(pl.program_id(0) == 0)
        def _():
            dst[...] = jnp.zeros_like(dst)
        _, vjp = jax.vjp(_dns_f, st_ref[0], _heads(u_ref, 1), _heads(w_ref, 1), _heads(qd_ref, 1), _heads(kt_ref, 1),
                         at_ref[0], dc_ref[0, 0:6, :])
        dS0, du, dw, dqd, dkt, dat, ddc = vjp((_heads(do_ref, 1), dst[...]))
        dst[...] = dS0
        _put_heads(du_ref, du)
        _put_heads(dw_ref, dw)
        _put_heads(dqd_ref, dqd)
        _put_heads(dkt_ref, dkt)
        dat_ref[0] = dat
        ddc_ref[0] = jnp.concatenate([ddc, jnp.zeros((2, LANE), f32)], axis=0)

    return pl.pallas_call(
        body, name="dn_scan_bwd", grid=(nc,),
        in_specs=_dnc_out_specs(nc) + [pl.BlockSpec((1, 6, B_DH, B_DH), lambda n: (nc - 1 - n, 0, 0, 0)),
                                       pl.BlockSpec((CHUNK, B_V), lambda n: (nc - 1 - n, 0))],
        out_specs=_dnc_out_specs(nc), out_shape=_dnc_shapes(S),
        scratch_shapes=[pltpu.VMEM((6, B_DH, B_DH), f32)],
        compiler_params=_cp(("arbitrary",)),
    )(*chunked, states, do)


def dnpost_fwd(o, proj, prm):
    S = o.shape[0]
    t = min(512, S)

    def body(o_ref, z_ref, p_ref, y_ref):
        y_ref[...] = _dnpost_f(o_ref[...], z_ref[...], p_ref[2:3, :])

    tok = pl.BlockSpec((t, B_V), lambda i: (i, 0))
    return pl.pallas_call(
        body, name="dn_post_fwd", grid=(S // t,),
        in_specs=[tok, pl.BlockSpec((t, B_V), lambda i: (i, 2)), pl.BlockSpec((8, LANE), lambda i: (0, 0))],
        out_specs=tok, out_shape=jax.ShapeDtypeStruct((S, B_V), f32), compiler_params=_cp(("parallel",)),
    )(o, proj, prm)


def dnpost_bwd(o, proj, prm, dmix):
    S = o.shape[0]
    t = min(512, S)

    def body(o_ref, z_ref, p_ref, dy_ref, do_ref, dz_ref, dg_ref):
        @pl.when(pl.program_id(0) == 0)
        def _():
            dg_ref[...] = jnp.zeros_like(dg_ref)
        _, vjp = jax.vjp(_dnpost_f, o_ref[...], z_ref[...], p_ref[2:3, :])
        do, dz, dg = vjp(dy_ref[...])
        do_ref[...] = do
        dz_ref[...] = dz
        dg_ref[...] += dg

    tok = pl.BlockSpec((t, B_V), lambda i: (i, 0))
    return pl.pallas_call(
        body, name="dn_post_bwd", grid=(S // t,),
        in_specs=[tok, pl.BlockSpec((t, B_V), lambda i: (i, 2)), pl.BlockSpec((8, LANE), lambda i: (0, 0)), tok],
        out_specs=[tok, tok, pl.BlockSpec((1, LANE), lambda i: (0, 0))],
        out_shape=[jax.ShapeDtypeStruct((S, B_V), f32), jax.ShapeDtypeStruct((S, B_V), f32),
                   jax.ShapeDtypeStruct((1, LANE), f32)],
        compiler_params=_cp(("arbitrary",)),
    )(o, proj, prm, dmix)


N_FF_BLK = D_FF // LANE
GU_SHARD = 2 * D_FF // 4


def _glu_f(gate, up, w, b):
    c = w[2:3] * gate + w[1:2] * shift_down(gate, 1) + w[0:1] * shift_down(gate, 2) + b
    return _silu(c) * up


def glu_fwd(gu, w, b, name):
    S = gu.shape[0]

    def body(g_ref, u_ref, w_ref, b_ref, o_ref):
        o_ref[...] = _glu_f(g_ref[...], u_ref[...], w_ref[...], b_ref[...]).astype(bf16)

    col = pl.BlockSpec((S, LANE), lambda j: (0, j))
    return pl.pallas_call(
        body, name=name, grid=(N_FF_BLK,),
        in_specs=[col, pl.BlockSpec((S, LANE), lambda j: (0, N_FF_BLK + j)), pl.BlockSpec((3, LANE), lambda j: (0, j)),
                  pl.BlockSpec((1, LANE), lambda j: (0, j))],
        out_specs=col, out_shape=jax.ShapeDtypeStruct((S, D_FF), bf16), compiler_params=_cp(("parallel",)),
    )(gu, gu, w, b.reshape(1, D_FF))


def glu_bwd(gu, w, b, dact, name):
    S = gu.shape[0]

    def body(g_ref, u_ref, w_ref, b_ref, d_ref, dg_ref, dw_ref, db_ref):
        _, vjp = jax.vjp(_glu_f, g_ref[...], u_ref[...], w_ref[...], b_ref[...])
        dg, du, dw, db = vjp(d_ref[...])
        dg_ref[0] = dg.astype(bf16)
        dg_ref[1] = du.astype(bf16)
        dw_ref[...] = dw
        db_ref[...] = db

    col = pl.BlockSpec((S, LANE), lambda j: (0, j))
    wsp = pl.BlockSpec((3, LANE), lambda j: (0, j))
    bsp = pl.BlockSpec((1, LANE), lambda j: (0, j))
    return pl.pallas_call(
        body, name=name, grid=(N_FF_BLK,),
        in_specs=[col, pl.BlockSpec((S, LANE), lambda j: (0, N_FF_BLK + j)), wsp, bsp, col],
        out_specs=[pl.BlockSpec((2, S, LANE), lambda j: (0, 0, j)), wsp, bsp],
        out_shape=[jax.ShapeDtypeStruct((2, S, D_FF), bf16), jax.ShapeDtypeStruct((3, D_FF), f32),
                   jax.ShapeDtypeStruct((1, D_FF), f32)],
        compiler_params=_cp(("parallel",)),
    )(gu, gu, w, b.reshape(1, D_FF), dact)


def gu_fwd(n2, wg, name):
    S = n2.shape[0]
    tm = min(512, S)

    def body(a_ref, w_ref, o_ref):
        o_ref[...] = _dg(a_ref[...], w_ref[...], 1, 0)

    return pl.pallas_call(
        body, name=name, grid=(4, S // tm),
        in_specs=[pl.BlockSpec((tm, D), lambda s, m: (m, 0)), pl.BlockSpec((None, D, GU_SHARD), lambda s, m: (s, 0, 0))],
        out_specs=pl.BlockSpec((tm, GU_SHARD), lambda s, m: (m, s)),
        out_shape=jax.ShapeDtypeStruct((S, 2 * D_FF), f32), compiler_params=_cp(("parallel", "parallel")),
    )(n2, wg)


def gu_bwd_x(dgu, wg, name):
    S = dgu.shape[1]
    tm = min(512, S)

    def body(d_ref, w_ref, o_ref):
        @pl.when(pl.program_id(1) == 0)
        def _():
            o_ref[...] = jnp.zeros_like(o_ref)
        o_ref[...] += _dg(d_ref[...], w_ref[...], 1, 1)

    return pl.pallas_call(
        body, name=name, grid=(S // tm, 4),
        in_specs=[pl.BlockSpec((None, tm, GU_SHARD), lambda m, s: (s // 2, m, s % 2)),
                  pl.BlockSpec((None, D, GU_SHARD), lambda m, s: (s, 0, 0))],
        out_specs=pl.BlockSpec((tm, D), lambda m, s: (m, 0)),
        out_shape=jax.ShapeDtypeStruct((S, D), f32), compiler_params=_cp(("parallel", "arbitrary")),
    )(dgu, wg)


def gu_bwd_w(n2, dgu, name):
    S = n2.shape[0]
    tm = min(512, S)
    nm = S // tm

    def body(a_ref, d_ref, o_ref, acc):
        @pl.when(pl.program_id(1) == 0)
        def _():
            acc[...] = jnp.zeros_like(acc)
        acc[...] += _dg(a_ref[...], d_ref[...], 0, 0)

        @pl.when(pl.program_id(1) == nm - 1)
        def _():
            o_ref[...] = acc[...].astype(bf16)

    return pl.pallas_call(
        body, name=name, grid=(4, nm),
        in_specs=[pl.BlockSpec((tm, D), lambda s, m: (m, 0)),
                  pl.BlockSpec((None, tm, GU_SHARD), lambda s, m: (s // 2, m, s % 2))],
        out_specs=pl.BlockSpec((None, D, GU_SHARD), lambda s, m: (s, 0, 0)),
        out_shape=jax.ShapeDtypeStruct((4, D, GU_SHARD), bf16),
        scratch_shapes=[pltpu.VMEM((D, GU_SHARD), f32)],
        compiler_params=_cp(("parallel", "arbitrary")),
    )(n2, dgu)


def _pair_cols(w):
    lead = w.shape[:-1]
    return w.reshape(lead + (2, 6, A_DH)).swapaxes(-3, -2).reshape(lead + (A_Q,))


def _unpair_cols(w):
    lead = w.shape[:-1]
    return w.reshape(lead + (6, 2, A_DH)).swapaxes(-3, -2).reshape(lead + (A_Q,))


def _lay_in_a(w):
    return jnp.concatenate([_pair_cols(w[:, :A_Q]), w[:, A_Q:]], axis=1)


def _unlay_in_a(w):
    return jnp.concatenate([_unpair_cols(w[:, :A_Q]), w[:, A_Q:]], axis=1)


def _lay_out_a(w):
    return jnp.concatenate([_pair_cols(w[:A_Q].T).T, w[A_Q:]], axis=0)


def _unlay_out_a(w):
    return jnp.concatenate([_unpair_cols(w[:A_Q].T).T, w[A_Q:]], axis=0)


def _lay_in_b(w):
    return jnp.concatenate([w[:, :2304], w[:, 2316:], w[:, 2304:2316],
                            jnp.zeros((w.shape[0], LANE - 12), w.dtype)], axis=1)


def _unlay_in_b(w):
    return jnp.concatenate([w[:, :2304], w[:, 2560:2572], w[:, 2304:2560]], axis=1)


def _chip_cols(w):
    return jnp.moveaxis(w.reshape(w.shape[0], 4, w.shape[1] // 4), 1, 0)


def _unchip_cols(w):
    return jnp.moveaxis(w, 0, 1).reshape(w.shape[1], 4 * w.shape[2])


def _local_step(x, mem, target, P):
    arrive = P.get("arrive", lambda key, after: None)
    ready = P.get("ready", lambda key, grads, dep: dep)
    sk = jnp.zeros((16, LANE), f32).at[:A_HEADS].set(jnp.broadcast_to(P["sinks"][:, None], (A_HEADS, LANE)))
    prm = jnp.zeros((8, LANE), f32).at[0, 6:12].set(P["a_log"]).at[1, 6:12].set(P["dt_bias"]).at[2].set(P["out_norm_g"])
    bias = bias_build(P["rel_bias"])
    saved = []
    h = x
    for i in range(2):
        n1 = rms_fwd(h, P["g_mix"][i], f"rms_mix{i}")
        arrive(("w_in", i), n1)
        proj = mm_nn(n1, P["w_in_a"] if i == 0 else P["w_in_b"], name="proj_a" if i == 0 else "proj_b")
        arrive(("w_mem", i), proj)
        kv = memkv_fwd(mem, P["g_mem"][i], P["w_mem"][i], f"memkv{i}")
        if i == 0:
            self_out = swa_fwd(proj, bias, sk)
            cross = xattn_fwd(proj, A_Q + 2 * LANE, kv, "xattn_a")
            extra = ()
        else:
            qkvn = dnprep_fwd(proj, P["conv_qkv"])
            chunked = dnc_fwd(qkvn, proj, prm)
            o, states = dns_fwd(chunked)
            self_out = dnpost_fwd(o, proj, prm)
            cross = xattn_fwd(proj, 2304, kv, "xattn_b")
            extra = (qkvn, chunked, states, o)
        mix = jnp.concatenate([self_out, cross], axis=1)
        arrive(("w_out", i), cross)
        h2 = mm_nn(mix, P["w_out"][i], res=h, name=f"out_proj{i}")
        n2 = rms_fwd(h2, P["g_ffn"][i], f"rms_ffn{i}")
        arrive(("w_gu", i), n2)
        gu = gu_fwd(n2, P["w_gu"][i], f"gate_up{i}")
        act = glu_fwd(gu, P["ffn_cw"][i], P["ffn_cb"][i], f"glu{i}")
        arrive(("w_down", i), act)
        h3 = mm_nn(act, P["w_down"][i], res=h2, name=f"down{i}")
        saved.append((h, n1, kv, proj, mix, h2, n2, gu, act, extra))
        h = h3

    loss, dh, dg_fin = loss_head(h, P["g_fin"], target)
    G = {"g_fin": dg_fin[0], "g_mix": [None, None], "g_mem": [None, None], "g_ffn": [None, None],
         "w_mem": [None, None], "w_out": [None, None], "w_gu": [None, None], "w_down": [None, None],
         "ffn_cw": [None, None], "ffn_cb": [None, None]}
    for i in (1, 0):
        hin, n1, kv, proj, mix, h2, n2, gu, act, extra = saved[i]
        dact = mm_nt(dh, P["w_down"][i], name=f"d_act{i}")
        G["w_down"][i] = mm_tn(act, dh, name=f"dw_down{i}")
        dgu, dcw, dcb = glu_bwd(gu, P["ffn_cw"][i], P["ffn_cb"][i], dact, f"glu_bwd{i}")
        G["ffn_cw"][i], G["ffn_cb"][i] = dcw, dcb[0]
        dn2 = gu_bwd_x(dgu, P["w_gu"][i], f"d_n2_{i}")
        G["w_gu"][i] = gu_bwd_w(n2, dgu, f"dw_gu{i}")
        g_ffn = ready(("ffn", i), G, P["g_ffn"][i])
        dh2, dg = rms_bwd(h2, g_ffn, dn2, dh, f"rms_ffn_bwd{i}")
        G["g_ffn"][i] = dg[0]
        dmix = mm_nt(dh2, P["w_out"][i], name=f"d_mix{i}")
        G["w_out"][i] = mm_tn(mix, dh2, name=f"dw_out{i}")
        if i == 0:
            dqkv, dbias, dsk = swa_bwd(proj, bias, sk, dmix)
            dxq, dkv = xattn_bwd(proj, A_Q + 2 * LANE, kv, dmix, "xattn_a_bwd")
            dproj = jnp.concatenate([dqkv, dxq], axis=1)
            G["sinks"] = dsk[:A_HEADS, 0]
            G["rel_bias"] = bias_grad(dbias)[:, :A_HEADS]
            w_in, gname = P["w_in_a"], "w_in_a"
        else:
            qkvn, chunked, states, o = extra
            do, dz, dgo = dnpost_bwd(o, proj, prm, dmix)
            dqkvn, dseg, dprm = dnc_bwd(qkvn, proj, prm, dns_bwd(chunked, states, do))
            draw, dconv = dnprep_bwd(proj, P["conv_qkv"], dqkvn)
            dxq, dkv = xattn_bwd(proj, 2304, kv, dmix, "xattn_b_bwd")
            dproj = jnp.concatenate([draw, dz, dxq, dseg], axis=1)
            G["conv_qkv"] = dconv
            G["a_log"], G["dt_bias"], G["out_norm_g"] = dprm[0, 6:12], dprm[1, 6:12], dgo[0]
            w_in, gname = P["w_in_b"], "w_in_b"
        dn1 = mm_nt(dproj, w_in, name=f"d_n1_{i}")
        G[gname] = mm_tn(n1, dproj, name=f"d{gname}")
        dh, dg = rms_bwd(hin, P["g_mix"][i], dn1, dh2, f"rms_mix_bwd{i}")
        G["g_mix"][i] = dg[0]
        dgm, dwm = memkv_bwd(mem, P["g_mem"][i], P["w_mem"][i], dkv, f"memkv_bwd{i}")
        G["g_mem"][i], G["w_mem"][i] = dgm[0], dwm
        ready(("mix", i), G, None)
    return loss, dh, G


def _prepare(full, w_gu=None):
    return {
        "rel_bias": full["rel_bias"], "sinks": full["sinks_a"][0], "a_log": full["a_log_b"][0],
        "dt_bias": full["dt_bias_b"][0], "out_norm_g": full["out_norm_g_b"][0],
        "g_mix": full["norm_mix_g"], "g_mem": full["norm_mem_g"], "g_ffn": full["norm_ffn_g"],
        "g_fin": full["final_norm_g"], "conv_qkv": full["conv_qkv_b"][0],
        "ffn_cw": [full["ffn_conv_w"][0], full["ffn_conv_w"][1]],
        "ffn_cb": [full["ffn_conv_b"][0], full["ffn_conv_b"][1]],
        "w_mem": [full["w_mem_kv"][0], full["w_mem_kv"][1]],
        "w_out": [_lay_out_a(full["w_out"][0]), full["w_out"][1]],
        "w_in_a": _lay_in_a(full["w_in_a"][0]), "w_in_b": _lay_in_b(full["w_in_b"][0]),
        "w_gu": w_gu if w_gu is not None else [_chip_cols(full["w_gate_up"][0]), _chip_cols(full["w_gate_up"][1])],
        "w_down": [full["w_down"][0], full["w_down"][1]],
    }


def _grads_to_ref(G):
    return {
        "rel_bias": G["rel_bias"], "norm_mix_g": jnp.stack(G["g_mix"]), "norm_mem_g": jnp.stack(G["g_mem"]),
        "w_mem_kv": jnp.stack(G["w_mem"]),
        "w_out": jnp.stack([_unlay_out_a(G["w_out"][0]), G["w_out"][1]]),
        "w_in_a": _unlay_in_a(G["w_in_a"])[None], "sinks_a": G["sinks"][None],
        "w_in_b": _unlay_in_b(G["w_in_b"])[None], "conv_qkv_b": G["conv_qkv"][None],
        "a_log_b": G["a_log"][None], "dt_bias_b": G["dt_bias"][None], "out_norm_g_b": G["out_norm_g"][None],
        "norm_ffn_g": jnp.stack(G["g_ffn"]),
        "w_gate_up": jnp.stack([_unchip_cols(G["w_gu"][0]), _unchip_cols(G["w_gu"][1])]).astype(f32),
        "ffn_conv_w": jnp.stack(G["ffn_cw"]), "ffn_conv_b": jnp.stack(G["ffn_cb"]),
        "w_down": jnp.stack(G["w_down"]), "final_norm_g": G["g_fin"],
    }


ANY = pl.BlockSpec(memory_space=pl.ANY)


def _place():
    return lax.axis_index("x"), lax.axis_index("y"), lax.axis_index("c")


def chip_scatter(gs):
    n = len(gs)

    def body(*refs):
        ins, outs = refs[:n], refs[n:2 * n]
        ssem, rsem = refs[2 * n:]
        x, y, c = _place()
        me = 2 * x + y
        peers = [(1 - x, y), (x, 1 - y), (1 - x, 1 - y)]

        def remote(j, k, slot):
            px, py = peers[k]
            return pltpu.make_async_remote_copy(
                src_ref=ins[j].at[2 * px + py], dst_ref=outs[j].at[slot],
                send_sem=ssem.at[3 * j + k], recv_sem=rsem.at[3 * j + k],
                device_id=(px, py, c), device_id_type=MESH)

        sends = [remote(j, k, me) for j in range(n) for k in range(3)]
        for cp in sends:
            cp.start()
        for j in range(n):
            for k in range(3):
                px, py = peers[k]
                remote(j, k, 2 * px + py).wait_recv()
        for cp in sends:
            cp.wait_send()

    return pl.pallas_call(
        body, name="grad_scatter", in_specs=[ANY] * n, out_specs=[ANY] * n,
        out_shape=[jax.ShapeDtypeStruct(g.shape, g.dtype) for g in gs],
        scratch_shapes=[pltpu.SemaphoreType.DMA((3 * n,)), pltpu.SemaphoreType.DMA((3 * n,))],
    )(*gs)


def allreduce_small(buf):
    R = buf.shape[0]

    def body(b_ref, o_ref, recv, ssem, rsem):
        x, y, c = _place()
        me = 4 * x + 2 * y + c

        def peer(k):
            return (1 - x if k & 4 else x, 1 - y if k & 2 else y, 1 - c if k & 1 else c)

        def remote(k, slot):
            return pltpu.make_async_remote_copy(
                src_ref=b_ref, dst_ref=recv.at[slot], send_sem=ssem.at[k - 1], recv_sem=rsem.at[k - 1],
                device_id=peer(k), device_id_type=MESH)

        sends = [remote(k, me) for k in range(1, 8)]
        for cp in sends:
            cp.start()
        recv[me] = b_ref[...]
        for k in range(1, 8):
            px, py, pc = peer(k)
            remote(k, 4 * px + 2 * py + pc).wait_recv()
        for cp in sends:
            cp.wait_send()
        total = recv[0]
        for j in range(1, 8):
            total = total + recv[j]
        o_ref[...] = total

    return pl.pallas_call(
        body, name="small_allreduce",
        in_specs=[pl.BlockSpec(memory_space=pltpu.VMEM)], out_specs=pl.BlockSpec(memory_space=pltpu.VMEM),
        out_shape=jax.ShapeDtypeStruct(buf.shape, f32),
        scratch_shapes=[pltpu.VMEM((8, R, LANE), f32), pltpu.SemaphoreType.DMA((7,)), pltpu.SemaphoreType.DMA((7,))],
    )(buf)


def sum_slots(own, recv, chip, core, name):
    _, R, C = recv.shape
    tr = _row_tile(R, 256)
    nt = R // tr

    def body(p_ref, a_ref, r_ref, o_ref):
        acc = jnp.zeros((tr, C), f32)
        for s in range(4):
            acc = acc + jnp.where(p_ref[0] == s, a_ref[s], r_ref[s]).astype(f32)
        o_ref[...] = acc

    slots = pl.BlockSpec((4, tr, C), lambda i, p_ref: (0, i, 0))
    return pl.pallas_call(
        body, name=name, out_shape=jax.ShapeDtypeStruct((2 * R, C), f32),
        grid_spec=pltpu.PrefetchScalarGridSpec(
            num_scalar_prefetch=1, grid=(nt,), in_specs=[slots, slots],
            out_specs=pl.BlockSpec((tr, C), lambda i, p_ref: (p_ref[1] * nt + i, 0))),
        compiler_params=_cp(("parallel",)),
    )(jnp.stack([chip, core]).astype(jnp.int32), own, recv)


def _half(ref, core, axis=0):
    half = ref.shape[axis] // 2
    idx = (slice(None),) * axis + (pl.ds(core * half, half),)
    return ref.at[idx]


IN_HBM = pl.BlockSpec(memory_space=pltpu.HBM)
IN_SEM = pl.BlockSpec(memory_space=pltpu.SEMAPHORE)
SIDE_EFFECT = pltpu.SideEffectType.DATAFLOW_SIDE_EFFECTING


def _gather_copy(buf, i, k, ssem, rsem, place, landing):
    x, y, c = place
    px, py = [(1 - x, y), (x, 1 - y), (1 - x, 1 - y)][k]
    me = 2 * x + y
    return pltpu.make_async_remote_copy(
        src_ref=buf.at[me], dst_ref=buf.at[me if landing == "theirs" else 2 * px + py],
        send_sem=ssem.at[3 * i + k], recv_sem=rsem.at[3 * i + k], device_id=(px, py, c), device_id_type=MESH)


def gather_start(groups):
    flat = [b for grp in groups for b in grp]
    n, ng = len(flat), len(groups)

    def body(*refs):
        bufs, sems = refs[:n], refs[n:n + 2 * ng]
        place = _place()
        j = 0
        for g, grp in enumerate(groups):
            for i in range(len(grp)):
                for k in range(3):
                    _gather_copy(bufs[j], i, k, sems[2 * g], sems[2 * g + 1], place, "theirs").start()
                j += 1

    sem_shapes = [pltpu.SemaphoreType.DMA((3 * len(grp),)) for grp in groups for _ in range(2)]
    out = pl.pallas_call(
        body, name="gather_start", in_specs=[IN_HBM] * n, out_specs=(*[IN_SEM] * (2 * ng), *[IN_HBM] * n),
        out_shape=(*sem_shapes, *[pltpu.HBM(b.shape, b.dtype) for b in flat]),
        input_output_aliases={i: 2 * ng + i for i in range(n)},
        compiler_params=pltpu.CompilerParams(has_side_effects=SIDE_EFFECT),
    )(*[pltpu.with_memory_space_constraint(b, pltpu.HBM) for b in flat])
    sems, bufs = out[:2 * ng], list(out[2 * ng:])
    flights, j = [], 0
    for g, grp in enumerate(groups):
        flights.append((bufs[j:j + len(grp)], sems[2 * g], sems[2 * g + 1]))
        j += len(grp)
    return flights


def gather_wait(flight, after, name):
    bufs, ssem, rsem = flight
    n = len(bufs)

    def body(*refs):
        place = _place()
        for i in range(n):
            for k in range(3):
                cp = _gather_copy(refs[i], i, k, refs[n], refs[n + 1], place, "mine")
                cp.wait_send()
                cp.wait_recv()

    return pl.pallas_call(
        body, name=name, in_specs=[IN_HBM] * n + [IN_SEM, IN_SEM, ANY], out_specs=[IN_HBM] * n,
        out_shape=[pltpu.HBM(b.shape, b.dtype) for b in bufs], input_output_aliases={i: i for i in range(n)},
        compiler_params=pltpu.CompilerParams(has_side_effects=SIDE_EFFECT),
    )(*bufs, ssem, rsem, after)


def _scatter_copy(src, land, j, k, ssem, rsem, place, landing):
    x, y, c = place
    px, py = [(1 - x, y), (x, 1 - y), (1 - x, 1 - y)][k]
    return pltpu.make_async_remote_copy(
        src_ref=src.at[2 * px + py], dst_ref=land.at[2 * x + y if landing == "theirs" else 2 * px + py],
        send_sem=ssem.at[3 * j + k], recv_sem=rsem.at[3 * j + k], device_id=(px, py, c), device_id_type=MESH)


def scatter_start(srcs, name):
    n = len(srcs)
    lands = [lax.empty(g.shape, g.dtype) for g in srcs]

    def body(*refs):
        place = _place()
        for j in range(n):
            for k in range(3):
                _scatter_copy(refs[j], refs[n + j], j, k, refs[2 * n], refs[2 * n + 1], place, "theirs").start()
        refs[-1][...] = jnp.zeros_like(refs[-1])

    sem = pltpu.SemaphoreType.DMA((3 * n,))
    hbm = [pltpu.with_memory_space_constraint(b, pltpu.HBM) for b in list(srcs) + lands]
    out = pl.pallas_call(
        body, name=name, in_specs=[IN_HBM] * (2 * n),
        out_specs=(IN_SEM, IN_SEM, *[IN_HBM] * (2 * n), pl.BlockSpec(memory_space=pltpu.VMEM)),
        out_shape=(sem, sem, *[pltpu.HBM(b.shape, b.dtype) for b in hbm], jax.ShapeDtypeStruct((8, LANE), f32)),
        input_output_aliases={i: 2 + i for i in range(2 * n)},
        compiler_params=pltpu.CompilerParams(has_side_effects=SIDE_EFFECT),
    )(*hbm)
    return (list(out[2:2 + n]), list(out[2 + n:2 + 2 * n]), out[0], out[1]), out[-1]


def scatter_wait(flight, after, name):
    srcs, lands, ssem, rsem = flight
    n = len(srcs)

    def body(*refs):
        place = _place()
        for j in range(n):
            for k in range(3):
                cp = _scatter_copy(refs[j], refs[n + j], j, k, refs[2 * n], refs[2 * n + 1], place, "mine")
                cp.wait_send()
                cp.wait_recv()

    out = pl.pallas_call(
        body, name=name, in_specs=[IN_HBM] * (2 * n) + [IN_SEM, IN_SEM, ANY], out_specs=[IN_HBM] * (2 * n),
        out_shape=[pltpu.HBM(b.shape, b.dtype) for b in list(srcs) + list(lands)],
        input_output_aliases={i: i for i in range(2 * n)},
        compiler_params=pltpu.CompilerParams(has_side_effects=SIDE_EFFECT),
    )(*srcs, *lands, ssem, rsem, after)
    return list(out[:n]), list(out[n:])


def pair_exchange(gbufs, name):
    n = len(gbufs)

    def body(*refs):
        ins, outs = refs[:n], refs[n:2 * n]
        ssem, rsem = refs[2 * n:]
        x, y, c = _place()
        cps = [pltpu.make_async_remote_copy(
            src_ref=_half(ins[j], 1 - c, axis=1), dst_ref=outs[j], send_sem=ssem.at[j], recv_sem=rsem.at[j],
            device_id=(x, y, 1 - c), device_id_type=MESH) for j in range(n)]
        for cp in cps:
            cp.start()
        for cp in cps:
            cp.wait()

    return pl.pallas_call(
        body, name=name, in_specs=[ANY] * n, out_specs=[ANY] * n,
        out_shape=[jax.ShapeDtypeStruct((4, g.shape[1] // 2, g.shape[2]), g.dtype) for g in gbufs],
        scratch_shapes=[pltpu.SemaphoreType.DMA((n,)), pltpu.SemaphoreType.DMA((n,))],
    )(*gbufs)


def _row_tile(rows, cap=512):
    return max(t for t in range(16, min(rows, cap) + 1, 16) if rows % t == 0)


def pair_sum(mine, theirs, core, name):
    _, R, C = mine.shape
    half = R // 2
    tr = _row_tile(half)
    nt = half // tr

    def body(c_ref, a_ref, b_ref, o_ref):
        o_ref[...] = (a_ref[...].astype(f32) + b_ref[...].astype(f32)).astype(bf16)

    return pl.pallas_call(
        body, name=name, out_shape=jax.ShapeDtypeStruct(theirs.shape, bf16),
        grid_spec=pltpu.PrefetchScalarGridSpec(
            num_scalar_prefetch=1, grid=(4, nt),
            in_specs=[pl.BlockSpec((None, tr, C), lambda s, i, c_ref: (s, c_ref[0] * nt + i, 0)),
                      pl.BlockSpec((None, tr, C), lambda s, i, c_ref: (s, i, 0))],
            out_specs=pl.BlockSpec((None, tr, C), lambda s, i, c_ref: (s, i, 0))),
        compiler_params=_cp(("parallel", "parallel")),
    )(jnp.reshape(core, (1,)).astype(jnp.int32), mine, theirs)


def final_exchange(fins):
    n = len(fins)

    def body(*refs):
        outs = refs[n:2 * n]
        ssem, rsem = refs[2 * n:]
        x, y, c = _place()
        cps = [pltpu.make_async_remote_copy(
            src_ref=_half(outs[j], c), dst_ref=_half(outs[j], c), send_sem=ssem.at[j], recv_sem=rsem.at[j],
            device_id=(x, y, 1 - c), device_id_type=MESH) for j in range(n)]
        for cp in cps:
            cp.start()
        for cp in cps:
            cp.wait()

    return pl.pallas_call(
        body, name="final_exchange", in_specs=[ANY] * n, out_specs=[ANY] * n,
        out_shape=[jax.ShapeDtypeStruct(f.shape, f.dtype) for f in fins],
        input_output_aliases={j: j for j in range(n)},
        scratch_shapes=[pltpu.SemaphoreType.DMA((n,)), pltpu.SemaphoreType.DMA((n,))],
    )(*fins)


def adamw_big(w, m, v, gs, row0, name):
    L, R, C = w.shape
    tr = _row_tile(math.gcd(R, row0) if row0 else R, max(16, 262144 // C // 16 * 16))
    b0 = row0 // tr

    def body(*refs):
        w_ref, m_ref, v_ref = refs[:3]
        g_refs = refs[3:3 + L]
        g_ref, d_ref, nm_ref, nv_ref = refs[3 + L:]
        g = g_refs[0][...]
        for l in range(1, L):
            g = jnp.where(pl.program_id(0) == l, g_refs[l][...], g)
        d, nm, nv = _adamw_math(w_ref[...], g, m_ref[...], v_ref[...])
        g_ref[...] = g
        d_ref[...] = d
        nm_ref[...] = nm
        nv_ref[...] = nv

    own = pl.BlockSpec((None, tr, C), lambda l, i: (l, i, 0))
    off = pl.BlockSpec((tr, C), lambda l, i: (b0 + i, 0))
    return pl.pallas_call(
        body, name=name, grid=(L, R // tr), in_specs=[own, own, own] + [off] * L, out_specs=[own] * 4,
        out_shape=[jax.ShapeDtypeStruct((L, R, C), f32)] * 4, compiler_params=_cp(("parallel", "parallel")),
    )(w, m, v, *gs)


def _adamw_math(w, g, m, v):
    m = B1 * m + (1.0 - B1) * g
    v = B2 * v + (1.0 - B2) * (g * g)
    m_hat = m / (1.0 - B1 ** STEP)
    v_hat = v / (1.0 - B2 ** STEP)
    delta = -LR * (m_hat / (jnp.sqrt(v_hat) + AEPS) + WD * w)
    return delta, m, v


def adamw_small(w, m, v, g):
    def body(w_ref, m_ref, v_ref, g_ref, d_ref, nm_ref, nv_ref):
        d, nm, nv = _adamw_math(w_ref[...], g_ref[...], m_ref[...], v_ref[...])
        d_ref[...] = d
        nm_ref[...] = nm
        nv_ref[...] = nv

    return pl.pallas_call(body, name="adamw_small", out_shape=[jax.ShapeDtypeStruct(w.shape, f32)] * 3)(w, m, v, g)


CONV =(("conv_qkv_b", 2), ("ffn_conv_w", 2))
SMALL = ("rel_bias", "norm_mix_g", "norm_mem_g", "sinks_a", "a_log_b", "dt_bias_b", "out_norm_g_b", "norm_ffn_g",
         "ffn_conv_b", "final_norm_g")
WEIGHTS = ("rel_bias", "norm_mix_g", "norm_mem_g", "w_mem_kv", "w_out", "w_in_a", "sinks_a", "w_in_b", "conv_qkv_b",
           "a_log_b", "dt_bias_b", "out_norm_g_b", "norm_ffn_g", "w_gate_up", "ffn_conv_w", "ffn_conv_b", "w_down",
           "final_norm_g")
ARGS = ("x", "mem") + WEIGHTS + ("loss_target",) + tuple("m_" + n for n in WEIGHTS) + tuple("v_" + n for n in WEIGHTS)


def _rows(a, width):
    flat = a.reshape(-1)
    pad = (-flat.shape[0]) % (8 * width)
    if pad:
        flat = jnp.concatenate([flat, jnp.zeros((pad,), a.dtype)])
    return flat.reshape(-1, width)


def _nrows(shape, width):
    return _pad_to(-(-math.prod(shape) // width), 8)


def _pack(arrs, width, total_rows, dtype):
    parts = [_rows(a.astype(dtype), width) for a in arrs]
    used = sum(p.shape[0] for p in parts)
    if total_rows > used:
        parts.append(jnp.zeros((total_rows - used, width), dtype))
    return jnp.concatenate(parts, axis=0)


def _unpack(buf, shapes, width):
    out, r = [], 0
    for s in shapes:
        n = _nrows(s, width)
        out.append(buf[r:r + n].reshape(-1)[:math.prod(s)].reshape(s))
        r += n
    return out


def _pad_to(n, mult):
    return -(-n // mult) * mult


def kernel(x, mem, rel_bias, norm_mix_g, norm_mem_g, w_mem_kv, w_out, w_in_a, sinks_a, w_in_b, conv_qkv_b, a_log_b, dt_bias_b, out_norm_g_b, norm_ffn_g, w_gate_up, ffn_conv_w, ffn_conv_b, w_down, final_norm_g, loss_target, m_rel_bias, m_norm_mix_g, m_norm_mem_g, m_w_mem_kv, m_w_out, m_w_in_a, m_sinks_a, m_w_in_b, m_conv_qkv_b, m_a_log_b, m_dt_bias_b, m_out_norm_g_b, m_norm_ffn_g, m_w_gate_up, m_ffn_conv_w, m_ffn_conv_b, m_w_down, m_final_norm_g, v_rel_bias, v_norm_mix_g, v_norm_mem_g, v_w_mem_kv, v_w_out, v_w_in_a, v_sinks_a, v_w_in_b, v_conv_qkv_b, v_a_log_b, v_dt_bias_b, v_out_norm_g_b, v_norm_ffn_g, v_w_gate_up, v_ffn_conv_w, v_ffn_conv_b, v_w_down, v_final_norm_g):
    A = dict(zip(ARGS, (x, mem, rel_bias, norm_mix_g, norm_mem_g, w_mem_kv, w_out, w_in_a, sinks_a, w_in_b, conv_qkv_b, a_log_b, dt_bias_b, out_norm_g_b, norm_ffn_g, w_gate_up, ffn_conv_w, ffn_conv_b, w_down, final_norm_g, loss_target, m_rel_bias, m_norm_mix_g, m_norm_mem_g, m_w_mem_kv, m_w_out, m_w_in_a, m_sinks_a, m_w_in_b, m_conv_qkv_b, m_a_log_b, m_dt_bias_b, m_out_norm_g_b, m_norm_ffn_g, m_w_gate_up, m_ffn_conv_w, m_ffn_conv_b, m_w_down, m_final_norm_g, v_rel_bias, v_norm_mix_g, v_norm_mem_g, v_w_mem_kv, v_w_out, v_w_in_a, v_sinks_a, v_w_in_b, v_conv_qkv_b, v_a_log_b, v_dt_bias_b, v_out_norm_g_b, v_norm_ffn_g, v_w_gate_up, v_ffn_conv_w, v_ffn_conv_b, v_w_down, v_final_norm_g)))
    chip = 2 * lax.axis_index("x") + lax.axis_index("y")
    core = lax.axis_index("c")
    n_down, n_out, n_mem = w_down.shape[1], w_out.shape[1], w_mem_kv.shape[1]

    def own_slot(shard):
        return lax.dynamic_update_index_in_dim(lax.empty((4,) + shard.shape, shard.dtype), shard, chip, 0)

    def bslot(w):
        return own_slot(w.astype(bf16))

    groups = {
        ("w_in", 0): [bslot(w_in_a[0])],
        ("w_mem", 0): [bslot(w_mem_kv[0]), bslot(w_mem_kv[1]), own_slot(conv_qkv_b[0]),
                       own_slot(ffn_conv_w.reshape(6, -1))],
        ("w_out", 0): [bslot(w_out[0])], ("w_gu", 0): [bslot(w_gate_up[0])], ("w_down", 0): [bslot(w_down[0])],
        ("w_in", 1): [bslot(w_in_b[0])],
        ("w_out", 1): [bslot(w_out[1])], ("w_gu", 1): [bslot(w_gate_up[1])], ("w_down", 1): [bslot(w_down[1])],
    }
    flights = dict(zip(groups, gather_start(list(groups.values()))))
    P = {"rel_bias": rel_bias, "sinks": sinks_a[0], "a_log": a_log_b[0], "dt_bias": dt_bias_b[0],
         "out_norm_g": out_norm_g_b[0], "g_mix": norm_mix_g, "g_mem": norm_mem_g, "g_ffn": norm_ffn_g,
         "g_fin": final_norm_g, "ffn_cb": [ffn_conv_b[0], ffn_conv_b[1]], "w_mem": [None, None], "w_out": [None, None],
         "w_gu": [None, None], "w_down": [None, None], "ffn_cw": [None, None]}

    def rows4(g):
        return g.reshape(4 * g.shape[1], g.shape[2])

    def arrive(key, after):
        if key not in flights:
            return
        got = gather_wait(flights.pop(key), after, "gather_wait_%s%d" % key)
        name, i = key
        if name == "w_in":
            P["w_in_a" if i == 0 else "w_in_b"] = (_lay_in_a if i == 0 else _lay_in_b)(_unchip_cols(got[0]))
        elif name == "w_mem":
            P["w_mem"] = [rows4(got[0]), rows4(got[1])]
            P["conv_qkv"] = _unchip_cols(got[2])
            cw = _unchip_cols(got[3]).reshape(2, 3, D_FF)
            P["ffn_cw"] = [cw[0], cw[1]]
        elif name == "w_out":
            P["w_out"][i] = _lay_out_a(rows4(got[0])) if i == 0 else rows4(got[0])
        elif name == "w_gu":
            P["w_gu"][i] = got[0]
        else:
            P["w_down"][i] = rows4(got[0])

    def chip_rows(g):
        return g.reshape(4, g.shape[0] // 4, g.shape[-1])

    sent, started = {}, []

    def ready(key, G, dep):
        kind, i = key
        tag = "%s%d" % key
        if kind == "ffn":
            names, partial = ("gu", "down"), [G["w_gu"][i], chip_rows(G["w_down"][i]).astype(bf16)]
        else:
            g_out = _unlay_out_a(G["w_out"][0]) if i == 0 else G["w_out"][1]
            g_in = _unlay_in_a(G["w_in_a"]) if i == 0 else _unlay_in_b(G["w_in_b"])
            names = ("out", "in", "mem")
            partial = [chip_rows(g_out).astype(bf16), _chip_cols(g_in).astype(bf16), chip_rows(G["w_mem"][i]).astype(bf16)]
        theirs = pair_exchange(partial, "pair_exchange_" + tag)
        pair = [pair_sum(p, t, core, "pair_sum_%s%d" % (nm, i)) for p, t, nm in zip(partial, theirs, names)]
        if key == ("mix", 0):
            sent[key] = (names, pair, chip_scatter(pair))
            return dep
        flight, token = scatter_start(pair, "scatter_start_" + tag)
        sent[key] = (names, flight)
        started.append(token[0, 0])
        if dep is not None:
            while started:
                dep = dep + started.pop()
        return dep

    P["arrive"], P["ready"] = arrive, ready

    loss, dx, G = _local_step(x[0], mem[0], loss_target[0], P)
    gfull = _grads_to_ref(G)

    fin = {}
    for key in (("ffn", 1), ("mix", 1), ("ffn", 0), ("mix", 0)):
        if key == ("mix", 0):
            names, pair, arrived = sent[key]
        else:
            names, flight = sent[key]
            pair, arrived = scatter_wait(flight, dx, "scatter_wait_%s%d" % key)
        for nm, p, r in zip(names, pair, arrived):
            fin[nm, key[1]] = sum_slots(p, r, chip, core, "sum_slots_%s%d" % (nm, key[1]))
    order = list(fin)
    done = dict(zip(order, final_exchange([fin[k] for k in order])))

    sm_shapes = [A[n].shape for n in SMALL] + [gfull[n].shape for n, _ in CONV] + [(LANE,)]
    sm_rows = _pad_to(sum(_nrows(s, LANE) for s in sm_shapes), 8)
    sbuf = _pack([gfull[n] for n in SMALL] + [gfull[n] for n, _ in CONV] + [loss[0]], LANE, sm_rows, f32)
    tot = _unpack(allreduce_small(sbuf), sm_shapes, LANE)
    gsmall = dict(zip(SMALL, tot[:len(SMALL)]))
    for (n, axis), t in zip(CONV, tot[len(SMALL):len(SMALL) + len(CONV)]):
        sh = A[n].shape[axis]
        gsmall[n] = lax.dynamic_slice_in_dim(t, chip * sh, sh, axis)
    loss_out = tot[-1][0]

    out = {}
    plan = (("w_gate_up", [done["gu", 0], done["gu", 1]]), ("w_down", [done["down", 0], done["down", 1]]),
            ("w_out", [done["out", 0], done["out", 1]]), ("w_mem_kv", [done["mem", 0], done["mem", 1]]),
            ("w_in_a", [done["in", 0]]), ("w_in_b", [done["in", 1]]))
    for n, gs in plan:
        shape3 = (len(gs),) + gs[0].shape
        res = adamw_big(A[n].reshape(shape3), A["m_" + n].reshape(shape3), A["v_" + n].reshape(shape3), gs, 0,
                        "adamw_" + n)
        for key, r in zip(("grad_", "delta_", "new_m_", "new_v_"), res):
            out[key + n] = r.reshape(A[n].shape)
    names = SMALL + tuple(n for n, _ in CONV)
    shapes = [A[n].shape for n in names]
    rows = _pad_to(sum(_nrows(s, LANE) for s in shapes), 8)
    packs = [_pack([src[n] for n in names], LANE, rows, f32)
             for src in ({n: A[n] for n in names}, {n: A["m_" + n] for n in names}, {n: A["v_" + n] for n in names}, gsmall)]
    res = adamw_small(*packs)
    for key, r in zip(("delta_", "new_m_", "new_v_"), res):
        for n, a in zip(names, _unpack(r, shapes, LANE)):
            out[key + n] = a
    for n in names:
        out["grad_" + n] = gsmall[n]
    return (loss_out, dx[None], *[out["grad_" + n] for n in WEIGHTS], *[out["delta_" + n] for n in WEIGHTS],
            *[out["new_m_" + n] for n in WEIGHTS], *[out["new_v_" + n] for n in WEIGHTS])
```

```python
import functools
import math

import numpy as np
import jax
import jax.numpy as jnp
from jax import lax
from jax.experimental import pallas as pl
from jax.experimental.pallas import tpu as pltpu

f32 = jnp.float32
bf16 = jnp.bfloat16
HI = lax.Precision.HIGHEST
MESH = pl.DeviceIdType.MESH

D = 1024
MEM_LEN = 256
EPS = 1e-6
A_HEADS, A_KV, A_DH = 12, 2, 64
A_Q = 768
BLK = 128
N_BUCKETS, MAX_DIST = 32, 128
B_QK, B_V, B_DH = 384, 768, 128
B_QKV = 1536
CHUNK = 64
X_Q = 256
D_FF = 2816
IN_A = 1280
IN_B = 2572
IN_B_PAD = 2688
LANE = 128
VMEM_LIMIT = 56 * 1024 * 1024

LR, B1, B2, AEPS, WD, STEP = 0.001, 0.9, 0.999, 1e-08, 0.01, 10


def _cp(sem=None):
    return pltpu.CompilerParams(dimension_semantics=sem, vmem_limit_bytes=VMEM_LIMIT)


def _dg(a, b, ca, cb, prec=None):
    return lax.dot_general(a, b, (((ca,), (cb,)), ((), ())), precision=prec, preferred_element_type=f32)


@jax.custom_vjp
def bdot(a, b):
    return _dg(a.astype(bf16), b.astype(bf16), 1, 0)


def _bdot_f(a, b):
    return bdot(a, b), (a, b)


def _bdot_b(res, g):
    a, b = res
    gb = g.astype(bf16)
    return _dg(gb, b.astype(bf16), 1, 1), _dg(a.astype(bf16), gb, 0, 0)


bdot.defvjp(_bdot_f, _bdot_b)


@jax.custom_vjp
def bdot_nt(a, b):
    return _dg(a.astype(bf16), b.astype(bf16), 1, 1)


def _bdot_nt_f(a, b):
    return bdot_nt(a, b), (a, b)


def _bdot_nt_b(res, g):
    a, b = res
    gb = g.astype(bf16)
    return _dg(gb, b.astype(bf16), 1, 0), _dg(gb, a.astype(bf16), 0, 0)


bdot_nt.defvjp(_bdot_nt_f, _bdot_nt_b)


def _shift_rows(x, s, down):
    n = x.shape[0]
    row = lax.broadcasted_iota(jnp.int32, x.shape, 0)
    if down:
        return jnp.where(row >= s, pltpu.roll(x, s, 0), 0.0)
    return jnp.where(row < n - s, pltpu.roll(x, n - s, 0), 0.0)


@functools.partial(jax.custom_vjp, nondiff_argnums=(1,))
def shift_down(x, s):
    return _shift_rows(x, s, True)


def _sd_f(x, s):
    return _shift_rows(x, s, True), None


def _sd_b(s, _, g):
    return (_shift_rows(g, s, False),)


shift_down.defvjp(_sd_f, _sd_b)


def _sigmoid(x):
    return 1.0 / (1.0 + jnp.exp(-x))


def _silu(x):
    return x * _sigmoid(x)


def _rms(x, g):
    return x * lax.rsqrt(jnp.mean(x * x, axis=-1, keepdims=True) + EPS) * g


def _tile(n, cap):
    u = n // LANE
    best = 1
    for d in range(1, u + 1):
        if u % d == 0 and d * LANE <= cap:
            best = d
    return best * LANE


def mm_nn(a, w, res=None, out_dtype=f32, name="mm_nn"):
    M, K = a.shape
    N = w.shape[1]
    tm, tn = min(512, M), _tile(N, 1024)

    def body(*refs):
        if res is None:
            a_ref, w_ref, o_ref = refs
            o_ref[...] = _dg(a_ref[...].astype(bf16), w_ref[...], 1, 0).astype(out_dtype)
        else:
            a_ref, w_ref, r_ref, o_ref = refs
            o_ref[...] = (r_ref[...] + _dg(a_ref[...].astype(bf16), w_ref[...], 1, 0)).astype(out_dtype)

    in_specs = [pl.BlockSpec((tm, K), lambda n, m: (m, 0)), pl.BlockSpec((K, tn), lambda n, m: (0, n))]
    args = [a, w]
    if res is not None:
        in_specs.append(pl.BlockSpec((tm, tn), lambda n, m: (m, n)))
        args.append(res)
    return pl.pallas_call(
        body, name=name, grid=(N // tn, M // tm), in_specs=in_specs,
        out_specs=pl.BlockSpec((tm, tn), lambda n, m: (m, n)),
        out_shape=jax.ShapeDtypeStruct((M, N), out_dtype),
        compiler_params=_cp(("parallel", "parallel")),
    )(*args)


def mm_nt(dy, w, name="mm_nt"):
    M, N = dy.shape
    K = w.shape[0]
    tm, tn = min(512, M), _tile(N, 1024)

    def body(dy_ref, w_ref, o_ref):
        @pl.when(pl.program_id(1) == 0)
        def _():
            o_ref[...] = jnp.zeros_like(o_ref)
        o_ref[...] += _dg(dy_ref[...].astype(bf16), w_ref[...], 1, 1)

    return pl.pallas_call(
        body, name=name, grid=(M // tm, N // tn),
        in_specs=[pl.BlockSpec((tm, tn), lambda m, n: (m, n)), pl.BlockSpec((K, tn), lambda m, n: (0, n))],
        out_specs=pl.BlockSpec((tm, K), lambda m, n: (m, 0)),
        out_shape=jax.ShapeDtypeStruct((M, K), f32),
        compiler_params=_cp(("parallel", "arbitrary")),
    )(dy, w)


def mm_tn(a, dy, name="mm_tn"):
    M, K = a.shape
    N = dy.shape[1]
    tm, tk, tn = min(512, M), _tile(K, 1408), _tile(N, 1024)

    def body(a_ref, dy_ref, o_ref):
        @pl.when(pl.program_id(2) == 0)
        def _():
            o_ref[...] = jnp.zeros_like(o_ref)
        o_ref[...] += _dg(a_ref[...].astype(bf16), dy_ref[...].astype(bf16), 0, 0)

    return pl.pallas_call(
        body, name=name, grid=(K // tk, N // tn, M // tm),
        in_specs=[pl.BlockSpec((tm, tk), lambda k, n, m: (m, k)), pl.BlockSpec((tm, tn), lambda k, n, m: (m, n))],
        out_specs=pl.BlockSpec((tk, tn), lambda k, n, m: (k, n)),
        out_shape=jax.ShapeDtypeStruct((K, N), f32),
        compiler_params=_cp(("parallel", "parallel", "arbitrary")),
    )(a, dy)


def rms_fwd(h, g, name):
    S = h.shape[0]
    t = min(512, S)

    def body(h_ref, g_ref, o_ref):
        o_ref[...] = _rms(h_ref[...], g_ref[...]).astype(bf16)

    return pl.pallas_call(
        body, name=name, grid=(S // t,),
        in_specs=[pl.BlockSpec((t, D), lambda i: (i, 0)), pl.BlockSpec((1, D), lambda i: (0, 0))],
        out_specs=pl.BlockSpec((t, D), lambda i: (i, 0)),
        out_shape=jax.ShapeDtypeStruct((S, D), bf16),
        compiler_params=_cp(("parallel",)),
    )(h, g.reshape(1, D))


def rms_bwd(h, g, dn, dres, name):
    S = h.shape[0]
    t = min(512, S)

    def body(h_ref, g_ref, dn_ref, dr_ref, dh_ref, dg_ref):
        @pl.when(pl.program_id(0) == 0)
        def _():
            dg_ref[...] = jnp.zeros_like(dg_ref)
        _, vjp = jax.vjp(_rms, h_ref[...], g_ref[...])
        dh, dg = vjp(dn_ref[...])
        dh_ref[...] = dr_ref[...] + dh
        dg_ref[...] += dg

    tok = pl.BlockSpec((t, D), lambda i: (i, 0))
    vec = pl.BlockSpec((1, D), lambda i: (0, 0))
    return pl.pallas_call(
        body, name=name, grid=(S // t,), in_specs=[tok, vec, tok, tok], out_specs=[tok, vec],
        out_shape=[jax.ShapeDtypeStruct((S, D), f32), jax.ShapeDtypeStruct((1, D), f32)],
        compiler_params=_cp(("arbitrary",)),
    )(h, g.reshape(1, D), dn, dres)


def loss_head(h, g, target):
    S = h.shape[0]
    t = min(512, S)

    def f(hh, gg, tt):
        err = _rms(hh, gg) - tt
        return 0.5 * jnp.sum(jnp.mean(err * err, axis=-1, keepdims=True), axis=0, keepdims=True)

    def body(h_ref, g_ref, t_ref, loss_ref, dh_ref, dg_ref):
        @pl.when(pl.program_id(0) == 0)
        def _():
            dg_ref[...] = jnp.zeros_like(dg_ref)
            loss_ref[...] = jnp.zeros_like(loss_ref)
        val, vjp = jax.vjp(lambda a, b: f(a, b, t_ref[...]), h_ref[...], g_ref[...])
        dh, dg = vjp(jnp.ones((1, 1), f32))
        dh_ref[...] = dh
        dg_ref[...] += dg
        loss_ref[...] += jnp.broadcast_to(val, loss_ref.shape)

    tok = pl.BlockSpec((t, D), lambda i: (i, 0))
    vec = pl.BlockSpec((1, D), lambda i: (0, 0))
    return pl.pallas_call(
        body, name="loss_head", grid=(S // t,), in_specs=[tok, vec, tok],
        out_specs=[pl.BlockSpec((1, LANE), lambda i: (0, 0)), tok, vec],
        out_shape=[jax.ShapeDtypeStruct((1, LANE), f32), jax.ShapeDtypeStruct((S, D), f32),
                   jax.ShapeDtypeStruct((1, D), f32)],
        compiler_params=_cp(("arbitrary",)),
    )(h, g.reshape(1, D), target)


def memkv_fwd(mem, g, w, name):
    def body(m_ref, g_ref, w_ref, o_ref):
        o_ref[...] = _dg(_rms(m_ref[...], g_ref[...]).astype(bf16), w_ref[...], 1, 0)

    return pl.pallas_call(
        body, name=name, out_shape=jax.ShapeDtypeStruct((MEM_LEN, 2 * X_Q), f32), compiler_params=_cp(),
    )(mem, g.reshape(1, D), w)


def memkv_bwd(mem, g, w, dkv, name):
    def body(m_ref, g_ref, w_ref, d_ref, dg_ref, dw_ref):
        n, vjp = jax.vjp(lambda gg: _rms(m_ref[...], gg), g_ref[...])
        db = d_ref[...].astype(bf16)
        dw_ref[...] = _dg(n.astype(bf16), db, 0, 0)
        dg_ref[...] = vjp(_dg(db, w_ref[...], 1, 1))[0]

    return pl.pallas_call(
        body, name=name,
        out_shape=[jax.ShapeDtypeStruct((1, D), f32), jax.ShapeDtypeStruct((D, 2 * X_Q), f32)],
        compiler_params=_cp(),
    )(mem, g.reshape(1, D), w, dkv)


def _xattn_f(xq, mk, mv):
    lane = lax.broadcasted_iota(jnp.int32, (1, X_Q), 1)
    out = jnp.zeros(xq.shape, f32)
    for hd in range(4):
        msk = (lane // 64 == hd).astype(f32)
        s = bdot_nt(xq * msk, mk) * (64 ** -0.5)
        m = lax.stop_gradient(jnp.max(s, axis=-1, keepdims=True))
        p = jnp.exp(s - m)
        p = p / jnp.sum(p, axis=-1, keepdims=True)
        out = out + bdot(p, mv * msk)
    return out


def xattn_fwd(proj, col, kv, name):
    S = proj.shape[0]
    t = min(512, S)
    cb = col // X_Q

    def body(q_ref, k_ref, v_ref, o_ref):
        o_ref[...] = _xattn_f(q_ref[...], k_ref[...], v_ref[...])

    return pl.pallas_call(
        body, name=name, grid=(S // t,),
        in_specs=[pl.BlockSpec((t, X_Q), lambda i: (i, cb)), pl.BlockSpec((MEM_LEN, X_Q), lambda i: (0, 0)),
                  pl.BlockSpec((MEM_LEN, X_Q), lambda i: (0, 1))],
        out_specs=pl.BlockSpec((t, X_Q), lambda i: (i, 0)),
        out_shape=jax.ShapeDtypeStruct((S, X_Q), f32),
        compiler_params=_cp(("parallel",)),
    )(proj, kv, kv)


def xattn_bwd(proj, col, kv, dmix, name):
    S = proj.shape[0]
    t = min(512, S)
    cb = col // X_Q

    def body(q_ref, k_ref, v_ref, do_ref, dq_ref, dk_ref, dv_ref):
        @pl.when(pl.program_id(0) == 0)
        def _():
            dk_ref[...] = jnp.zeros_like(dk_ref)
            dv_ref[...] = jnp.zeros_like(dv_ref)
        _, vjp = jax.vjp(_xattn_f, q_ref[...], k_ref[...], v_ref[...])
        dq, dk, dv = vjp(do_ref[...])
        dq_ref[...] = dq
        dk_ref[...] += dk
        dv_ref[...] += dv

    kvb = pl.BlockSpec((MEM_LEN, X_Q), lambda i: (0, 0))
    dq, dk, dv = pl.pallas_call(
        body, name=name, grid=(S // t,),
        in_specs=[pl.BlockSpec((t, X_Q), lambda i: (i, cb)), kvb,
                  pl.BlockSpec((MEM_LEN, X_Q), lambda i: (0, 1)), pl.BlockSpec((t, X_Q), lambda i: (i, 3))],
        out_specs=[pl.BlockSpec((t, X_Q), lambda i: (i, 0)), kvb, kvb],
        out_shape=[jax.ShapeDtypeStruct((S, X_Q), f32), jax.ShapeDtypeStruct((MEM_LEN, X_Q), f32),
                   jax.ShapeDtypeStruct((MEM_LEN, X_Q), f32)],
        compiler_params=_cp(("arbitrary",)),
    )(proj, kv, kv, dmix)
    return dq, jnp.concatenate([dk, dv], axis=1)


def _bucket_map():
    qi = np.arange(BLK)[:, None]
    kj = np.arange(2 * BLK)[None, :]
    n = np.maximum(BLK + qi - kj, 0)
    max_exact = N_BUCKETS // 2
    nf = np.maximum(n, 1).astype(np.float64)
    large = max_exact + (np.log(nf / max_exact) / math.log(MAX_DIST / max_exact)
                         * (N_BUCKETS - max_exact)).astype(np.int32)
    large = np.minimum(large, N_BUCKETS - 1)
    return np.where(n < max_exact, n, large).astype(np.int32)


def bias_build(rel_bias):
    def body(rb_ref, bk_ref, o_ref):
        bk = bk_ref[...]
        for h in range(A_HEADS):
            acc = jnp.zeros((BLK, 2 * BLK), f32)
            for b in range(N_BUCKETS):
                acc = jnp.where(bk == b, rb_ref[b, h], acc)
            o_ref[h] = acc

    return pl.pallas_call(
        body, name="bias_build",
        in_specs=[pl.BlockSpec(memory_space=pltpu.SMEM), pl.BlockSpec(memory_space=pltpu.VMEM)],
        out_specs=pl.BlockSpec(memory_space=pltpu.VMEM),
        out_shape=jax.ShapeDtypeStruct((A_HEADS, BLK, 2 * BLK), f32), compiler_params=_cp(),
    )(rel_bias, jnp.asarray(_bucket_map()))


def bias_grad(dbias):
    def body(d_ref, bk_ref, o_ref):
        bk = bk_ref[...]
        row = lax.broadcasted_iota(jnp.int32, (N_BUCKETS, LANE), 0)
        lane = lax.broadcasted_iota(jnp.int32, (N_BUCKETS, LANE), 1)
        acc = jnp.zeros((N_BUCKETS, LANE), f32)
        for h in range(A_HEADS):
            d = d_ref[h]
            for b in range(N_BUCKETS):
                s = jnp.sum(jnp.where(bk == b, d, 0.0), keepdims=True)
                acc = acc + jnp.where((row == b) & (lane == h), s, 0.0)
        o_ref[...] = acc

    return pl.pallas_call(
        body, name="bias_grad", out_shape=jax.ShapeDtypeStruct((N_BUCKETS, LANE), f32), compiler_params=_cp(),
    )(dbias, jnp.asarray(_bucket_map()))


def _swa_f(qb, kp, kc, vp, vc, bias, sk, first):
    kband = jnp.concatenate([kp, kc], axis=0)
    vband = jnp.concatenate([vp, vc], axis=0)
    qi = lax.broadcasted_iota(jnp.int32, (BLK, 2 * BLK), 0)
    kj = lax.broadcasted_iota(jnp.int32, (BLK, 2 * BLK), 1)
    rel = kj - qi
    ok = (rel >= 1) & (rel <= BLK) & ((kj >= BLK) | jnp.logical_not(first))
    lane = lax.broadcasted_iota(jnp.int32, (1, LANE), 1)
    lane_b = lax.broadcasted_iota(jnp.int32, (BLK, LANE), 1)
    outs = []
    for p in range(A_HEADS // 2):
        qp = qb[:, LANE * p:LANE * (p + 1)]
        acc = jnp.zeros((BLK, LANE), f32)
        for g in range(2):
            h = g * (A_HEADS // 2) + p
            msk = (lane // A_DH == g).astype(f32)
            s = bdot_nt(qp * msk, kband) * (A_DH ** -0.5) + bias[h]
            s = jnp.where(ok, s, -1e30)
            skb = jnp.broadcast_to(sk[h:h + 1, :], (BLK, LANE))
            sink = jnp.sum(jnp.where(lane_b == 0, skb, 0.0), axis=-1, keepdims=True)
            m = lax.stop_gradient(jnp.maximum(jnp.max(s, axis=-1, keepdims=True), sink))
            e = jnp.exp(s - m)
            prob = e / (jnp.sum(e, axis=-1, keepdims=True) + jnp.exp(sink - m))
            acc = acc + bdot(prob, vband) * msk
        outs.append(acc)
    return jnp.concatenate(outs, axis=1)


def _swa_specs(nb, rev):
    bi = (lambda i: nb - 1 - i) if rev else (lambda i: i)
    return [
        pl.BlockSpec((BLK, A_Q), lambda i: (bi(i), 0)),
        pl.BlockSpec((BLK, LANE), lambda i: (jnp.maximum(bi(i) - 1, 0), 6)),
        pl.BlockSpec((BLK, LANE), lambda i: (bi(i), 6)),
        pl.BlockSpec((BLK, LANE), lambda i: (jnp.maximum(bi(i) - 1, 0), 7)),
        pl.BlockSpec((BLK, LANE), lambda i: (bi(i), 7)),
        pl.BlockSpec((A_HEADS, BLK, 2 * BLK), lambda i: (0, 0, 0)),
        pl.BlockSpec((16, LANE), lambda i: (0, 0)),
    ]


def swa_fwd(proj, bias, sk):
    S = proj.shape[0]
    nb = S // BLK

    def body(q_ref, kp_ref, kc_ref, vp_ref, vc_ref, b_ref, s_ref, o_ref):
        o_ref[...] = _swa_f(q_ref[...], kp_ref[...], kc_ref[...], vp_ref[...], vc_ref[...], b_ref[...], s_ref[...],
                            pl.program_id(0) == 0)

    return pl.pallas_call(
        body, name="swa_fwd", grid=(nb,), in_specs=_swa_specs(nb, False),
        out_specs=pl.BlockSpec((BLK, A_Q), lambda i: (i, 0)),
        out_shape=jax.ShapeDtypeStruct((S, A_Q), f32), compiler_params=_cp(("parallel",)),
    )(proj, proj, proj, proj, proj, bias, sk)


def swa_bwd(proj, bias, sk, dmix):
    S = proj.shape[0]
    nb = S // BLK

    def body(q_ref, kp_ref, kc_ref, vp_ref, vc_ref, b_ref, s_ref, do_ref, dqkv_ref, db_ref, ds_ref, ck, cv):
        i = pl.program_id(0)

        @pl.when(i == 0)
        def _():
            db_ref[...] = jnp.zeros_like(db_ref)
            ds_ref[...] = jnp.zeros_like(ds_ref)
            ck[...] = jnp.zeros_like(ck)
            cv[...] = jnp.zeros_like(cv)
        first = i == nb - 1
        _, vjp = jax.vjp(lambda *a: _swa_f(*a, first), q_ref[...], kp_ref[...], kc_ref[...], vp_ref[...],
                         vc_ref[...], b_ref[...], s_ref[...])
        dq, dkp, dkc, dvp, dvc, db, ds = vjp(do_ref[...])
        dqkv_ref[...] = jnp.concatenate([dq, dkc + ck[...], dvc + cv[...]], axis=1)
        ck[...] = dkp
        cv[...] = dvp
        db_ref[...] += db
        ds_ref[...] += ds

    return pl.pallas_call(
        body, name="swa_bwd", grid=(nb,),
        in_specs=_swa_specs(nb, True) + [pl.BlockSpec((BLK, A_Q), lambda i: (nb - 1 - i, 0))],
        out_specs=[pl.BlockSpec((BLK, D), lambda i: (nb - 1 - i, 0)),
                   pl.BlockSpec((A_HEADS, BLK, 2 * BLK), lambda i: (0, 0, 0)),
                   pl.BlockSpec((16, LANE), lambda i: (0, 0))],
        out_shape=[jax.ShapeDtypeStruct((S, D), f32), jax.ShapeDtypeStruct((A_HEADS, BLK, 2 * BLK), f32),
                   jax.ShapeDtypeStruct((16, LANE), f32)],
        scratch_shapes=[pltpu.VMEM((BLK, LANE), f32), pltpu.VMEM((BLK, LANE), f32)],
        compiler_params=_cp(("arbitrary",)),
    )(proj, proj, proj, proj, proj, bias, sk, dmix)


def _dnprep_f(x, w, is_qk):
    c = (w[3:4] * x + w[2:3] * shift_down(x, 1) + w[1:2] * shift_down(x, 2) + w[0:1] * shift_down(x, 3))
    a = _silu(c)
    n = a * lax.rsqrt(jnp.sum(a * a, axis=-1, keepdims=True) + EPS)
    return jnp.where(is_qk, n, a)


def dnprep_fwd(proj, cw):
    S = proj.shape[0]
    nblk = B_QKV // LANE

    def body(x_ref, w_ref, o_ref):
        o_ref[...] = _dnprep_f(x_ref[...], w_ref[...], pl.program_id(0) < 2 * B_QK // LANE)

    return pl.pallas_call(
        body, name="dnprep_fwd", grid=(nblk,),
        in_specs=[pl.BlockSpec((S, LANE), lambda j: (0, j)), pl.BlockSpec((4, LANE), lambda j: (0, j))],
        out_specs=pl.BlockSpec((S, LANE), lambda j: (0, j)),
        out_shape=jax.ShapeDtypeStruct((S, B_QKV), f32), compiler_params=_cp(("parallel",)),
    )(proj, cw)


def dnprep_bwd(proj, cw, dqkvn):
    S = proj.shape[0]
    nblk = B_QKV // LANE

    def body(x_ref, w_ref, d_ref, dx_ref, dw_ref):
        is_qk = pl.program_id(0) < 2 * B_QK // LANE
        _, vjp = jax.vjp(lambda a, b: _dnprep_f(a, b, is_qk), x_ref[...], w_ref[...])
        dx, dw = vjp(d_ref[...])
        dx_ref[...] = dx
        dw_ref[...] = dw

    col = pl.BlockSpec((S, LANE), lambda j: (0, j))
    wsp = pl.BlockSpec((4, LANE), lambda j: (0, j))
    return pl.pallas_call(
        body, name="dnprep_bwd", grid=(nblk,), in_specs=[col, wsp, col], out_specs=[col, wsp],
        out_shape=[jax.ShapeDtypeStruct((S, B_QKV), f32), jax.ShapeDtypeStruct((4, B_QKV), f32)],
        compiler_params=_cp(("parallel",)),
    )(proj, cw, dqkvn)


def _hdot(a, b, ca=1, cb=0):
    return _dg(a, b, ca, cb, HI)


def _bdg(a, b, ca, cb):
    dn = (((ca,), (cb,)), ((0,), (0,)))
    ah, bh = a.astype(bf16), b.astype(bf16)
    al, bl = (a - ah.astype(f32)).astype(bf16), (b - bh.astype(f32)).astype(bf16)
    return (lax.dot_general(ah, bh, dn, preferred_element_type=f32)
            + lax.dot_general(ah, bl, dn, preferred_element_type=f32)
            + lax.dot_general(al, bh, dn, preferred_element_type=f32))


@jax.custom_vjp
def hbd(a, b):
    return _bdg(a, b, 2, 1)


@jax.custom_vjp
def hbd_nt(a, b):
    return _bdg(a, b, 2, 2)


@jax.custom_vjp
def hbd_tn(a, b):
    return _bdg(a, b, 1, 1)


hbd.defvjp(lambda a, b: (hbd(a, b), (a, b)), lambda r, g: (hbd_nt(g, r[1]), hbd_tn(r[0], g)))
hbd_nt.defvjp(lambda a, b: (hbd_nt(a, b), (a, b)), lambda r, g: (hbd(g, r[1]), hbd_tn(g, r[0])))
hbd_tn.defvjp(lambda a, b: (hbd_tn(a, b), (a, b)), lambda r, g: (hbd_nt(r[1], g), hbd(r[0], g)))


def _stack(xs):
    return jnp.concatenate([x[None] for x in xs], axis=0)


def _lane_col(x, j):
    lane = lax.broadcasted_iota(jnp.int32, (1, LANE), 1)
    return jnp.sum(jnp.where(lane == j, x, 0.0), axis=-1, keepdims=True)


def _tri_inv(a_mat):
    r = lax.broadcasted_iota(jnp.int32, (1, CHUNK, CHUNK), 1)
    c = lax.broadcasted_iota(jnp.int32, (1, CHUNK, CHUNK), 2)
    pw = -a_mat
    inv = (r == c).astype(f32) + pw
    for _ in range(5):
        pw = hbd(pw, pw)
        inv = inv + hbd(inv, pw)
    return inv


@jax.custom_vjp
def _tri_inv_known(a_mat, inv):
    return inv


_tri_inv_known.defvjp(lambda a, inv: (inv, inv),
                      lambda inv, g: (-hbd_tn(inv, hbd_nt(g, inv)), jnp.zeros_like(inv)))


def _dnc_f(q, k, v, seg, prm, inverse=_tri_inv):
    C = CHUNK
    beta_all = _sigmoid(seg)
    xx = seg + prm[1:2]
    g_all = -jnp.exp(prm[0:1]) * (jnp.maximum(xx, 0.0) + jnp.log(1.0 + jnp.exp(-jnp.abs(xx))))
    r2 = lax.broadcasted_iota(jnp.int32, (C, C), 0)
    c2 = lax.broadcasted_iota(jnp.int32, (C, C), 1)
    gc_all = _hdot((r2 >= c2).astype(f32), g_all)
    beta = _stack([_lane_col(beta_all, h) for h in range(6)])
    gc = _stack([_lane_col(gc_all, 6 + h) for h in range(6)])
    r = lax.broadcasted_iota(jnp.int32, (1, C, C), 1)
    c = lax.broadcasted_iota(jnp.int32, (1, C, C), 2)
    incl = r >= c
    strict = r > c
    eye = (r == c).astype(f32)
    g_row = hbd(jnp.ones((6, C, C), f32), eye * gc)
    decay = jnp.where(incl, jnp.exp(jnp.where(incl, gc - g_row, 0.0)), 0.0)
    a_mat = beta * hbd_nt(k, k) * jnp.where(strict, decay, 0.0)
    eg = jnp.exp(gc)
    inv = inverse(a_mat)
    u = hbd(inv, beta * v)
    w = hbd(inv, (beta * eg) * k)
    qc = q * (B_DH ** -0.5)
    attn = hbd_nt(qc, k) * decay
    last = (lax.broadcasted_iota(jnp.int32, (1, C, 1), 1) == C - 1).astype(f32)
    g_last = jnp.sum(gc * last, axis=1, keepdims=True)
    dc = jnp.broadcast_to(jnp.exp(g_last), (6, 1, LANE)).reshape(6, LANE)
    return u, w, qc * eg, k * jnp.exp(g_last - gc), attn, dc, inv


def _dns_f(S0, u, w, qd, kt, attn, dcrows):
    dc = _lane_col(dcrows, 0).reshape(6, 1, 1)
    delta = u - hbd(w, S0)
    out = hbd(qd, S0) + hbd(attn, delta)
    return out, dc * S0 + hbd_tn(kt, delta)


def _dnpost_f(o, z, grow):
    outs = []
    for h in range(6):
        oh = o[:, LANE * h:LANE * (h + 1)]
        outs.append(oh * lax.rsqrt(jnp.mean(oh * oh, axis=-1, keepdims=True) + EPS) * grow
                    * _silu(z[:, LANE * h:LANE * (h + 1)]))
    return jnp.concatenate(outs, axis=1)


def _hs(h):
    return slice(LANE * h, LANE * (h + 1))


def _heads(ref, share):
    return _stack([ref[:, _hs(h // share)] for h in range(6)])


def _put_heads(ref, val):
    for h in range(6):
        ref[:, _hs(h)] = val[h]


def _dnc_in_specs():
    return [
        pl.BlockSpec((CHUNK, B_QK), lambda n: (n, 0)),
        pl.BlockSpec((CHUNK, B_QK), lambda n: (n, 1)),
        pl.BlockSpec((CHUNK, B_V), lambda n: (n, 1)),
        pl.BlockSpec((CHUNK, LANE), lambda n: (n, 20)),
        pl.BlockSpec((8, LANE), lambda n: (0, 0)),
    ]


def _dnc_out_specs(rev_nc=None):
    ci = (lambda n: n) if rev_nc is None else (lambda n: rev_nc - 1 - n)
    wide = pl.BlockSpec((CHUNK, B_V), lambda n: (ci(n), 0))
    return [wide, wide, wide, wide, pl.BlockSpec((1, 6, CHUNK, CHUNK), lambda n: (ci(n), 0, 0, 0)),
            pl.BlockSpec((1, 8, LANE), lambda n: (ci(n), 0, 0))]


def _dnc_shapes(S):
    nc = S // CHUNK
    wide = jax.ShapeDtypeStruct((S, B_V), f32)
    return [wide, wide, wide, wide, jax.ShapeDtypeStruct((nc, 6, CHUNK, CHUNK), f32),
            jax.ShapeDtypeStruct((nc, 8, LANE), f32)]


def dnc_fwd(qkvn, proj, prm):
    S = proj.shape[0]

    def body(q_ref, k_ref, v_ref, s_ref, p_ref, u_ref, w_ref, qd_ref, kt_ref, at_ref, dc_ref, inv_ref):
        u, w, qd, kt, attn, dc, inv = _dnc_f(_heads(q_ref, 2), _heads(k_ref, 2), _heads(v_ref, 1), s_ref[...],
                                             p_ref[...])
        inv_ref[0] = inv
        _put_heads(u_ref, u)
        _put_heads(w_ref, w)
        _put_heads(qd_ref, qd)
        _put_heads(kt_ref, kt)
        at_ref[0] = attn
        dc_ref[0] = jnp.concatenate([dc, jnp.zeros((2, LANE), f32)], axis=0)

    out = pl.pallas_call(
        body, name="dn_chunk_fwd", grid=(S // CHUNK,), in_specs=_dnc_in_specs(),
        out_specs=_dnc_out_specs() + [_dnc_out_specs()[4]], out_shape=_dnc_shapes(S) + [_dnc_shapes(S)[4]],
        compiler_params=_cp(("parallel",)),
    )(qkvn, qkvn, qkvn, proj, prm)
    return out[:6], out[6]


def dnc_bwd(qkvn, proj, prm, inv, cots):
    S = proj.shape[0]

    def body(q_ref, k_ref, v_ref, s_ref, p_ref, inv_ref, du_ref, dw_ref, dqd_ref, dkt_ref, dat_ref, ddc_ref,
             dx_ref, dseg_ref, dprm_ref):
        @pl.when(pl.program_id(0) == 0)
        def _():
            dprm_ref[...] = jnp.zeros_like(dprm_ref)
        known = functools.partial(_tri_inv_known, inv=inv_ref[0])
        _, vjp = jax.vjp(lambda *a: _dnc_f(*a, inverse=known)[:6], _heads(q_ref, 2), _heads(k_ref, 2),
                         _heads(v_ref, 1), s_ref[...], p_ref[...])
        dq, dk, dv, dseg, dprm = vjp((_heads(du_ref, 1), _heads(dw_ref, 1), _heads(dqd_ref, 1), _heads(dkt_ref, 1),
                                      dat_ref[0], ddc_ref[0, 0:6, :]))
        dx_ref[...] = jnp.concatenate([dq[0] + dq[1], dq[2] + dq[3], dq[4] + dq[5],
                                       dk[0] + dk[1], dk[2] + dk[3], dk[4] + dk[5]] + [dv[h] for h in range(6)], axis=1)
        dseg_ref[...] = dseg
        dprm_ref[...] += dprm

    return pl.pallas_call(
        body, name="dn_chunk_bwd", grid=(S // CHUNK,),
        in_specs=_dnc_in_specs() + [_dnc_out_specs()[4]] + _dnc_out_specs(),
        out_specs=[pl.BlockSpec((CHUNK, B_QKV), lambda n: (n, 0)), pl.BlockSpec((CHUNK, LANE), lambda n: (n, 0)),
                   pl.BlockSpec((8, LANE), lambda n: (0, 0))],
        out_shape=[jax.ShapeDtypeStruct((S, B_QKV), f32), jax.ShapeDtypeStruct((S, LANE), f32),
                   jax.ShapeDtypeStruct((8, LANE), f32)],
        compiler_params=_cp(("arbitrary",)),
    )(qkvn, qkvn, qkvn, proj, prm, inv, *cots)


def dns_fwd(chunked):
    u = chunked[0]
    S = u.shape[0]
    nc = S // CHUNK

    def body(u_ref, w_ref, qd_ref, kt_ref, at_ref, dc_ref, o_ref, st_ref, st):
        @pl.when(pl.program_id(0) == 0)
        def _():
            st[...] = jnp.zeros_like(st)
        S0 = st[...]
        st_ref[0] = S0
        out, S1 = _dns_f(S0, _heads(u_ref, 1), _heads(w_ref, 1), _heads(qd_ref, 1), _heads(kt_ref, 1),
                         at_ref[0], dc_ref[0, 0:6, :])
        _put_heads(o_ref, out)
        st[...] = S1

    return pl.pallas_call(
        body, name="dn_scan_fwd", grid=(nc,), in_specs=_dnc_out_specs(),
        out_specs=[pl.BlockSpec((CHUNK, B_V), lambda n: (n, 0)),
                   pl.BlockSpec((1, 6, B_DH, B_DH), lambda n: (n, 0, 0, 0))],
        out_shape=[jax.ShapeDtypeStruct((S, B_V), f32), jax.ShapeDtypeStruct((nc, 6, B_DH, B_DH), f32)],
        scratch_shapes=[pltpu.VMEM((6, B_DH, B_DH), f32)],
        compiler_params=_cp(("arbitrary",)),
    )(*chunked)


def dns_bwd(chunked, states, do):
    S = do.shape[0]
    nc = S // CHUNK

    def body(u_ref, w_ref, qd_ref, kt_ref, at_ref, dc_ref, st_ref, do_ref,
             du_ref, dw_ref, dqd_ref, dkt_ref, dat_ref, ddc_ref, dst):
        @pl.when(pl.program_id(0) == 0)
        def _():
            dst[...] = jnp.zeros_like(dst)
        _, vjp = jax.vjp(_dns_f, st_ref[0], _heads(u_ref, 1), _heads(w_ref, 1), _heads(qd_ref, 1), _heads(kt_ref, 1),
                         at_ref[0], dc_ref[0, 0:6, :])
        dS0, du, dw, dqd, dkt, dat, ddc = vjp((_heads(do_ref, 1), dst[...]))
        dst[...] = dS0
        _put_heads(du_ref, du)
        _put_heads(dw_ref, dw)
        _put_heads(dqd_ref, dqd)
        _put_heads(dkt_ref, dkt)
        dat_ref[0] = dat
        ddc_ref[0] = jnp.concatenate([ddc, jnp.zeros((2, LANE), f32)], axis=0)

    return pl.pallas_call(
        body, name="dn_scan_bwd", grid=(nc,),
        in_specs=_dnc_out_specs(nc) + [pl.BlockSpec((1, 6, B_DH, B_DH), lambda n: (nc - 1 - n, 0, 0, 0)),
                                       pl.BlockSpec((CHUNK, B_V), lambda n: (nc - 1 - n, 0))],
        out_specs=_dnc_out_specs(nc), out_shape=_dnc_shapes(S),
        scratch_shapes=[pltpu.VMEM((6, B_DH, B_DH), f32)],
        compiler_params=_cp(("arbitrary",)),
    )(*chunked, states, do)


def dnpost_fwd(o, proj, prm):
    S = o.shape[0]
    t = min(512, S)

    def body(o_ref, z_ref, p_ref, y_ref):
        y_ref[...] = _dnpost_f(o_ref[...], z_ref[...], p_ref[2:3, :])

    tok = pl.BlockSpec((t, B_V), lambda i: (i, 0))
    return pl.pallas_call(
        body, name="dn_post_fwd", grid=(S // t,),
        in_specs=[tok, pl.BlockSpec((t, B_V), lambda i: (i, 2)), pl.BlockSpec((8, LANE), lambda i: (0, 0))],
        out_specs=tok, out_shape=jax.ShapeDtypeStruct((S, B_V), f32), compiler_params=_cp(("parallel",)),
    )(o, proj, prm)


def dnpost_bwd(o, proj, prm, dmix):
    S = o.shape[0]
    t = min(512, S)

    def body(o_ref, z_ref, p_ref, dy_ref, do_ref, dz_ref, dg_ref):
        @pl.when(pl.program_id(0) == 0)
        def _():
            dg_ref[...] = jnp.zeros_like(dg_ref)
        _, vjp = jax.vjp(_dnpost_f, o_ref[...], z_ref[...], p_ref[2:3, :])
        do, dz, dg = vjp(dy_ref[...])
        do_ref[...] = do
        dz_ref[...] = dz
        dg_ref[...] += dg

    tok = pl.BlockSpec((t, B_V), lambda i: (i, 0))
    return pl.pallas_call(
        body, name="dn_post_bwd", grid=(S // t,),
        in_specs=[tok, pl.BlockSpec((t, B_V), lambda i: (i, 2)), pl.BlockSpec((8, LANE), lambda i: (0, 0)), tok],
        out_specs=[tok, tok, pl.BlockSpec((1, LANE), lambda i: (0, 0))],
        out_shape=[jax.ShapeDtypeStruct((S, B_V), f32), jax.ShapeDtypeStruct((S, B_V), f32),
                   jax.ShapeDtypeStruct((1, LANE), f32)],
        compiler_params=_cp(("arbitrary",)),
    )(o, proj, prm, dmix)


N_FF_BLK = D_FF // LANE
GU_SHARD = 2 * D_FF // 4


def _glu_f(gate, up, w, b):
    c = w[2:3] * gate + w[1:2] * shift_down(gate, 1) + w[0:1] * shift_down(gate, 2) + b
    return _silu(c) * up


def glu_fwd(gu, w, b, name):
    S = gu.shape[0]

    def body(g_ref, u_ref, w_ref, b_ref, o_ref):
        o_ref[...] = _glu_f(g_ref[...], u_ref[...], w_ref[...], b_ref[...]).astype(bf16)

    col = pl.BlockSpec((S, LANE), lambda j: (0, j))
    return pl.pallas_call(
        body, name=name, grid=(N_FF_BLK,),
        in_specs=[col, pl.BlockSpec((S, LANE), lambda j: (0, N_FF_BLK + j)), pl.BlockSpec((3, LANE), lambda j: (0, j)),
                  pl.BlockSpec((1, LANE), lambda j: (0, j))],
        out_specs=col, out_shape=jax.ShapeDtypeStruct((S, D_FF), bf16), compiler_params=_cp(("parallel",)),
    )(gu, gu, w, b.reshape(1, D_FF))


def glu_bwd(gu, w, b, dact, name):
    S = gu.shape[0]

    def body(g_ref, u_ref, w_ref, b_ref, d_ref, dg_ref, dw_ref, db_ref):
        _, vjp = jax.vjp(_glu_f, g_ref[...], u_ref[...], w_ref[...], b_ref[...])
        dg, du, dw, db = vjp(d_ref[...])
        dg_ref[0] = dg.astype(bf16)
        dg_ref[1] = du.astype(bf16)
        dw_ref[...] = dw
        db_ref[...] = db

    col = pl.BlockSpec((S, LANE), lambda j: (0, j))
    wsp = pl.BlockSpec((3, LANE), lambda j: (0, j))
    bsp = pl.BlockSpec((1, LANE), lambda j: (0, j))
    return pl.pallas_call(
        body, name=name, grid=(N_FF_BLK,),
        in_specs=[col, pl.BlockSpec((S, LANE), lambda j: (0, N_FF_BLK + j)), wsp, bsp, col],
        out_specs=[pl.BlockSpec((2, S, LANE), lambda j: (0, 0, j)), wsp, bsp],
        out_shape=[jax.ShapeDtypeStruct((2, S, D_FF), bf16), jax.ShapeDtypeStruct((3, D_FF), f32),
                   jax.ShapeDtypeStruct((1, D_FF), f32)],
        compiler_params=_cp(("parallel",)),
    )(gu, gu, w, b.reshape(1, D_FF), dact)


def gu_fwd(n2, wg, name):
    S = n2.shape[0]
    tm = min(512, S)

    def body(a_ref, w_ref, o_ref):
        o_ref[...] = _dg(a_ref[...], w_ref[...], 1, 0)

    return pl.pallas_call(
        body, name=name, grid=(4, S // tm),
        in_specs=[pl.BlockSpec((tm, D), lambda s, m: (m, 0)), pl.BlockSpec((None, D, GU_SHARD), lambda s, m: (s, 0, 0))],
        out_specs=pl.BlockSpec((tm, GU_SHARD), lambda s, m: (m, s)),
        out_shape=jax.ShapeDtypeStruct((S, 2 * D_FF), f32), compiler_params=_cp(("parallel", "parallel")),
    )(n2, wg)


def gu_bwd_x(dgu, wg, name):
    S = dgu.shape[1]
    tm = min(512, S)

    def body(d_ref, w_ref, o_ref):
        @pl.when(pl.program_id(1) == 0)
        def _():
            o_ref[...] = jnp.zeros_like(o_ref)
        o_ref[...] += _dg(d_ref[...], w_ref[...], 1, 1)

    return pl.pallas_call(
        body, name=name, grid=(S // tm, 4),
        in_specs=[pl.BlockSpec((None, tm, GU_SHARD), lambda m, s: (s // 2, m, s % 2)),
                  pl.BlockSpec((None, D, GU_SHARD), lambda m, s: (s, 0, 0))],
        out_specs=pl.BlockSpec((tm, D), lambda m, s: (m, 0)),
        out_shape=jax.ShapeDtypeStruct((S, D), f32), compiler_params=_cp(("parallel", "arbitrary")),
    )(dgu, wg)


def gu_bwd_w(n2, dgu, name):
    S = n2.shape[0]
    tm = min(512, S)
    nm = S // tm

    def body(a_ref, d_ref, o_ref, acc):
        @pl.when(pl.program_id(1) == 0)
        def _():
            acc[...] = jnp.zeros_like(acc)
        acc[...] += _dg(a_ref[...], d_ref[...], 0, 0)

        @pl.when(pl.program_id(1) == nm - 1)
        def _():
            o_ref[...] = acc[...].astype(bf16)

    return pl.pallas_call(
        body, name=name, grid=(4, nm),
        in_specs=[pl.BlockSpec((tm, D), lambda s, m: (m, 0)),
                  pl.BlockSpec((None, tm, GU_SHARD), lambda s, m: (s // 2, m, s % 2))],
        out_specs=pl.BlockSpec((None, D, GU_SHARD), lambda s, m: (s, 0, 0)),
        out_shape=jax.ShapeDtypeStruct((4, D, GU_SHARD), bf16),
        scratch_shapes=[pltpu.VMEM((D, GU_SHARD), f32)],
        compiler_params=_cp(("parallel", "arbitrary")),
    )(n2, dgu)


def _pair_cols(w):
    lead = w.shape[:-1]
    return w.reshape(lead + (2, 6, A_DH)).swapaxes(-3, -2).reshape(lead + (A_Q,))


def _unpair_cols(w):
    lead = w.shape[:-1]
    return w.reshape(lead + (6, 2, A_DH)).swapaxes(-3, -2).reshape(lead + (A_Q,))


def _lay_in_a(w):
    return jnp.concatenate([_pair_cols(w[:, :A_Q]), w[:, A_Q:]], axis=1)


def _unlay_in_a(w):
    return jnp.concatenate([_unpair_cols(w[:, :A_Q]), w[:, A_Q:]], axis=1)


def _lay_out_a(w):
    return jnp.concatenate([_pair_cols(w[:A_Q].T).T, w[A_Q:]], axis=0)


def _unlay_out_a(w):
    return jnp.concatenate([_unpair_cols(w[:A_Q].T).T, w[A_Q:]], axis=0)


def _lay_in_b(w):
    return jnp.concatenate([w[:, :2304], w[:, 2316:], w[:, 2304:2316],
                            jnp.zeros((w.shape[0], LANE - 12), w.dtype)], axis=1)


def _unlay_in_b(w):
    return jnp.concatenate([w[:, :2304], w[:, 2560:2572], w[:, 2304:2560]], axis=1)


def _chip_cols(w):
    return jnp.moveaxis(w.reshape(w.shape[0], 4, w.shape[1] // 4), 1, 0)


def _unchip_cols(w):
    return jnp.moveaxis(w, 0, 1).reshape(w.shape[1], 4 * w.shape[2])


def _local_step(x, mem, target, P):
    arrive = P.get("arrive", lambda key, after: None)
    ready = P.get("ready", lambda key, grads, dep: dep)
    sk = jnp.zeros((16, LANE), f32).at[:A_HEADS].set(jnp.broadcast_to(P["sinks"][:, None], (A_HEADS, LANE)))
    prm = jnp.zeros((8, LANE), f32).at[0, 6:12].set(P["a_log"]).at[1, 6:12].set(P["dt_bias"]).at[2].set(P["out_norm_g"])
    bias = bias_build(P["rel_bias"])
    saved = []
    h = x
    for i in range(2):
        n1 = rms_fwd(h, P["g_mix"][i], f"rms_mix{i}")
        arrive(("w_in", i), n1)
        proj = mm_nn(n1, P["w_in_a"] if i == 0 else P["w_in_b"], name="proj_a" if i == 0 else "proj_b")
        arrive(("w_mem", i), proj)
        kv = memkv_fwd(mem, P["g_mem"][i], P["w_mem"][i], f"memkv{i}")
        if i == 0:
            self_out = swa_fwd(proj, bias, sk)
            cross = xattn_fwd(proj, A_Q + 2 * LANE, kv, "xattn_a")
            extra = ()
        else:
            qkvn = dnprep_fwd(proj, P["conv_qkv"])
            chunked, inv = dnc_fwd(qkvn, proj, prm)
            o, states = dns_fwd(chunked)
            self_out = dnpost_fwd(o, proj, prm)
            cross = xattn_fwd(proj, 2304, kv, "xattn_b")
            extra = (qkvn, chunked, inv, states, o)
        mix = jnp.concatenate([self_out, cross], axis=1)
        arrive(("w_out", i), cross)
        h2 = mm_nn(mix, P["w_out"][i], res=h, name=f"out_proj{i}")
        n2 = rms_fwd(h2, P["g_ffn"][i], f"rms_ffn{i}")
        arrive(("w_gu", i), n2)
        gu = gu_fwd(n2, P["w_gu"][i], f"gate_up{i}")
        act = glu_fwd(gu, P["ffn_cw"][i], P["ffn_cb"][i], f"glu{i}")
        arrive(("w_down", i), act)
        h3 = mm_nn(act, P["w_down"][i], res=h2, name=f"down{i}")
        saved.append((h, n1, kv, proj, mix, h2, n2, gu, act, extra))
        h = h3

    loss, dh, dg_fin = loss_head(h, P["g_fin"], target)
    G = {"g_fin": dg_fin[0], "g_mix": [None, None], "g_mem": [None, None], "g_ffn": [None, None],
         "w_mem": [None, None], "w_out": [None, None], "w_gu": [None, None], "w_down": [None, None],
         "ffn_cw": [None, None], "ffn_cb": [None, None]}
    for i in (1, 0):
        hin, n1, kv, proj, mix, h2, n2, gu, act, extra = saved[i]
        dact = mm_nt(dh, P["w_down"][i], name=f"d_act{i}")
        G["w_down"][i] = mm_tn(act, dh, name=f"dw_down{i}")
        dgu, dcw, dcb = glu_bwd(gu, P["ffn_cw"][i], P["ffn_cb"][i], dact, f"glu_bwd{i}")
        G["ffn_cw"][i], G["ffn_cb"][i] = dcw, dcb[0]
        dn2 = gu_bwd_x(dgu, P["w_gu"][i], f"d_n2_{i}")
        G["w_gu"][i] = gu_bwd_w(n2, dgu, f"dw_gu{i}")
        g_ffn = ready(("ffn", i), G, P["g_ffn"][i])
        dh2, dg = rms_bwd(h2, g_ffn, dn2, dh, f"rms_ffn_bwd{i}")
        G["g_ffn"][i] = dg[0]
        dmix = mm_nt(dh2, P["w_out"][i], name=f"d_mix{i}")
        G["w_out"][i] = mm_tn(mix, dh2, name=f"dw_out{i}")
        if i == 0:
            dqkv, dbias, dsk = swa_bwd(proj, bias, sk, dmix)
            dxq, dkv = xattn_bwd(proj, A_Q + 2 * LANE, kv, dmix, "xattn_a_bwd")
            dproj = jnp.concatenate([dqkv, dxq], axis=1)
            G["sinks"] = dsk[:A_HEADS, 0]
            G["rel_bias"] = bias_grad(dbias)[:, :A_HEADS]
            w_in, gname = P["w_in_a"], "w_in_a"
        else:
            qkvn, chunked, inv, states, o = extra
            do, dz, dgo = dnpost_bwd(o, proj, prm, dmix)
            dqkvn, dseg, dprm = dnc_bwd(qkvn, proj, prm, inv, dns_bwd(chunked, states, do))
            draw, dconv = dnprep_bwd(proj, P["conv_qkv"], dqkvn)
            dxq, dkv = xattn_bwd(proj, 2304, kv, dmix, "xattn_b_bwd")
            dproj = jnp.concatenate([draw, dz, dxq, dseg], axis=1)
            G["conv_qkv"] = dconv
            G["a_log"], G["dt_bias"], G["out_norm_g"] = dprm[0, 6:12], dprm[1, 6:12], dgo[0]
            w_in, gname = P["w_in_b"], "w_in_b"
        dn1 = mm_nt(dproj, w_in, name=f"d_n1_{i}")
        G[gname] = mm_tn(n1, dproj, name=f"d{gname}")
        dh, dg = rms_bwd(hin, P["g_mix"][i], dn1, dh2, f"rms_mix_bwd{i}")
        G["g_mix"][i] = dg[0]
        dgm, dwm = memkv_bwd(mem, P["g_mem"][i], P["w_mem"][i], dkv, f"memkv_bwd{i}")
        G["g_mem"][i], G["w_mem"][i] = dgm[0], dwm
        ready(("mix", i), G, None)
    return loss, dh, G


def _prepare(full, w_gu=None):
    return {
        "rel_bias": full["rel_bias"], "sinks": full["sinks_a"][0], "a_log": full["a_log_b"][0],
        "dt_bias": full["dt_bias_b"][0], "out_norm_g": full["out_norm_g_b"][0],
        "g_mix": full["norm_mix_g"], "g_mem": full["norm_mem_g"], "g_ffn": full["norm_ffn_g"],
        "g_fin": full["final_norm_g"], "conv_qkv": full["conv_qkv_b"][0],
        "ffn_cw": [full["ffn_conv_w"][0], full["ffn_conv_w"][1]],
        "ffn_cb": [full["ffn_conv_b"][0], full["ffn_conv_b"][1]],
        "w_mem": [full["w_mem_kv"][0], full["w_mem_kv"][1]],
        "w_out": [_lay_out_a(full["w_out"][0]), full["w_out"][1]],
        "w_in_a": _lay_in_a(full["w_in_a"][0]), "w_in_b": _lay_in_b(full["w_in_b"][0]),
        "w_gu": w_gu if w_gu is not None else [_chip_cols(full["w_gate_up"][0]), _chip_cols(full["w_gate_up"][1])],
        "w_down": [full["w_down"][0], full["w_down"][1]],
    }


def _grads_to_ref(G):
    return {
        "rel_bias": G["rel_bias"], "norm_mix_g": jnp.stack(G["g_mix"]), "norm_mem_g": jnp.stack(G["g_mem"]),
        "w_mem_kv": jnp.stack(G["w_mem"]),
        "w_out": jnp.stack([_unlay_out_a(G["w_out"][0]), G["w_out"][1]]),
        "w_in_a": _unlay_in_a(G["w_in_a"])[None], "sinks_a": G["sinks"][None],
        "w_in_b": _unlay_in_b(G["w_in_b"])[None], "conv_qkv_b": G["conv_qkv"][None],
        "a_log_b": G["a_log"][None], "dt_bias_b": G["dt_bias"][None], "out_norm_g_b": G["out_norm_g"][None],
        "norm_ffn_g": jnp.stack(G["g_ffn"]),
        "w_gate_up": jnp.stack([_unchip_cols(G["w_gu"][0]), _unchip_cols(G["w_gu"][1])]).astype(f32),
        "ffn_conv_w": jnp.stack(G["ffn_cw"]), "ffn_conv_b": jnp.stack(G["ffn_cb"]),
        "w_down": jnp.stack(G["w_down"]), "final_norm_g": G["g_fin"],
    }


ANY = pl.BlockSpec(memory_space=pl.ANY)


def _place():
    return lax.axis_index("x"), lax.axis_index("y"), lax.axis_index("c")


def chip_scatter(gs):
    n = len(gs)

    def body(*refs):
        ins, outs = refs[:n], refs[n:2 * n]
        ssem, rsem = refs[2 * n:]
        x, y, c = _place()
        me = 2 * x + y
        peers = [(1 - x, y), (x, 1 - y), (1 - x, 1 - y)]

        def remote(j, k, slot):
            px, py = peers[k]
            return pltpu.make_async_remote_copy(
                src_ref=ins[j].at[2 * px + py], dst_ref=outs[j].at[slot],
                send_sem=ssem.at[3 * j + k], recv_sem=rsem.at[3 * j + k],
                device_id=(px, py, c), device_id_type=MESH)

        sends = [remote(j, k, me) for j in range(n) for k in range(3)]
        for cp in sends:
            cp.start()
        for j in range(n):
            for k in range(3):
                px, py = peers[k]
                remote(j, k, 2 * px + py).wait_recv()
        for cp in sends:
            cp.wait_send()

    return pl.pallas_call(
        body, name="grad_scatter", in_specs=[ANY] * n, out_specs=[ANY] * n,
        out_shape=[jax.ShapeDtypeStruct(g.shape, g.dtype) for g in gs],
        scratch_shapes=[pltpu.SemaphoreType.DMA((3 * n,)), pltpu.SemaphoreType.DMA((3 * n,))],
    )(*gs)


def allreduce_small(buf):
    R = buf.shape[0]

    def body(b_ref, o_ref, recv, ssem, rsem):
        x, y, c = _place()
        me = 4 * x + 2 * y + c

        def peer(k):
            return (1 - x if k & 4 else x, 1 - y if k & 2 else y, 1 - c if k & 1 else c)

        def remote(k, slot):
            return pltpu.make_async_remote_copy(
                src_ref=b_ref, dst_ref=recv.at[slot], send_sem=ssem.at[k - 1], recv_sem=rsem.at[k - 1],
                device_id=peer(k), device_id_type=MESH)

        sends = [remote(k, me) for k in range(1, 8)]
        for cp in sends:
            cp.start()
        recv[me] = b_ref[...]
        for k in range(1, 8):
            px, py, pc = peer(k)
            remote(k, 4 * px + 2 * py + pc).wait_recv()
        for cp in sends:
            cp.wait_send()
        total = recv[0]
        for j in range(1, 8):
            total = total + recv[j]
        o_ref[...] = total

    return pl.pallas_call(
        body, name="small_allreduce",
        in_specs=[pl.BlockSpec(memory_space=pltpu.VMEM)], out_specs=pl.BlockSpec(memory_space=pltpu.VMEM),
        out_shape=jax.ShapeDtypeStruct(buf.shape, f32),
        scratch_shapes=[pltpu.VMEM((8, R, LANE), f32), pltpu.SemaphoreType.DMA((7,)), pltpu.SemaphoreType.DMA((7,))],
    )(buf)


def sum_slots(own, recv, chip, core, name):
    _, R, C = recv.shape
    tr = _row_tile(R, 256)
    nt = R // tr

    def body(p_ref, a_ref, r_ref, o_ref):
        acc = jnp.zeros((tr, C), f32)
        for s in range(4):
            acc = acc + jnp.where(p_ref[0] == s, a_ref[s], r_ref[s]).astype(f32)
        o_ref[...] = acc

    slots = pl.BlockSpec((4, tr, C), lambda i, p_ref: (0, i, 0))
    return pl.pallas_call(
        body, name=name, out_shape=jax.ShapeDtypeStruct((2 * R, C), f32),
        grid_spec=pltpu.PrefetchScalarGridSpec(
            num_scalar_prefetch=1, grid=(nt,), in_specs=[slots, slots],
            out_specs=pl.BlockSpec((tr, C), lambda i, p_ref: (p_ref[1] * nt + i, 0))),
        compiler_params=_cp(("parallel",)),
    )(jnp.stack([chip, core]).astype(jnp.int32), own, recv)


def _half(ref, core, axis=0):
    half = ref.shape[axis] // 2
    idx = (slice(None),) * axis + (pl.ds(core * half, half),)
    return ref.at[idx]


IN_HBM = pl.BlockSpec(memory_space=pltpu.HBM)
IN_SEM = pl.BlockSpec(memory_space=pltpu.SEMAPHORE)
SIDE_EFFECT = pltpu.SideEffectType.DATAFLOW_SIDE_EFFECTING


def _gather_copy(buf, i, k, ssem, rsem, place, landing):
    x, y, c = place
    px, py = [(1 - x, y), (x, 1 - y), (1 - x, 1 - y)][k]
    me = 2 * x + y
    return pltpu.make_async_remote_copy(
        src_ref=buf.at[me], dst_ref=buf.at[me if landing == "theirs" else 2 * px + py],
        send_sem=ssem.at[3 * i + k], recv_sem=rsem.at[3 * i + k], device_id=(px, py, c), device_id_type=MESH)


def gather_start(groups):
    flat = [b for grp in groups for b in grp]
    n, ng = len(flat), len(groups)

    def body(*refs):
        bufs, sems = refs[:n], refs[n:n + 2 * ng]
        place = _place()
        j = 0
        for g, grp in enumerate(groups):
            for i in range(len(grp)):
                for k in range(3):
                    _gather_copy(bufs[j], i, k, sems[2 * g], sems[2 * g + 1], place, "theirs").start()
                j += 1

    sem_shapes = [pltpu.SemaphoreType.DMA((3 * len(grp),)) for grp in groups for _ in range(2)]
    out = pl.pallas_call(
        body, name="gather_start", in_specs=[IN_HBM] * n, out_specs=(*[IN_SEM] * (2 * ng), *[IN_HBM] * n),
        out_shape=(*sem_shapes, *[pltpu.HBM(b.shape, b.dtype) for b in flat]),
        input_output_aliases={i: 2 * ng + i for i in range(n)},
        compiler_params=pltpu.CompilerParams(has_side_effects=SIDE_EFFECT),
    )(*[pltpu.with_memory_space_constraint(b, pltpu.HBM) for b in flat])
    sems, bufs = out[:2 * ng], list(out[2 * ng:])
    flights, j = [], 0
    for g, grp in enumerate(groups):
        flights.append((bufs[j:j + len(grp)], sems[2 * g], sems[2 * g + 1]))
        j += len(grp)
    return flights


def gather_wait(flight, after, name):
    bufs, ssem, rsem = flight
    n = len(bufs)

    def body(*refs):
        place = _place()
        for i in range(n):
            for k in range(3):
                cp = _gather_copy(refs[i], i, k, refs[n], refs[n + 1], place, "mine")
                cp.wait_send()
                cp.wait_recv()

    return pl.pallas_call(
        body, name=name, in_specs=[IN_HBM] * n + [IN_SEM, IN_SEM, ANY], out_specs=[IN_HBM] * n,
        out_shape=[pltpu.HBM(b.shape, b.dtype) for b in bufs], input_output_aliases={i: i for i in range(n)},
        compiler_params=pltpu.CompilerParams(has_side_effects=SIDE_EFFECT),
    )(*bufs, ssem, rsem, after)


def _scatter_copy(src, land, j, k, ssem, rsem, place, landing):
    x, y, c = place
    px, py = [(1 - x, y), (x, 1 - y), (1 - x, 1 - y)][k]
    return pltpu.make_async_remote_copy(
        src_ref=src.at[2 * px + py], dst_ref=land.at[2 * x + y if landing == "theirs" else 2 * px + py],
        send_sem=ssem.at[3 * j + k], recv_sem=rsem.at[3 * j + k], device_id=(px, py, c), device_id_type=MESH)


def scatter_start(srcs, name):
    n = len(srcs)
    lands = [lax.empty(g.shape, g.dtype) for g in srcs]

    def body(*refs):
        place = _place()
        for j in range(n):
            for k in range(3):
                _scatter_copy(refs[j], refs[n + j], j, k, refs[2 * n], refs[2 * n + 1], place, "theirs").start()
        refs[-1][...] = jnp.zeros_like(refs[-1])

    sem = pltpu.SemaphoreType.DMA((3 * n,))
    hbm = [pltpu.with_memory_space_constraint(b, pltpu.HBM) for b in list(srcs) + lands]
    out = pl.pallas_call(
        body, name=name, in_specs=[IN_HBM] * (2 * n),
        out_specs=(IN_SEM, IN_SEM, *[IN_HBM] * (2 * n), pl.BlockSpec(memory_space=pltpu.VMEM)),
        out_shape=(sem, sem, *[pltpu.HBM(b.shape, b.dtype) for b in hbm], jax.ShapeDtypeStruct((8, LANE), f32)),
        input_output_aliases={i: 2 + i for i in range(2 * n)},
        compiler_params=pltpu.CompilerParams(has_side_effects=SIDE_EFFECT),
    )(*hbm)
    return (list(out[2:2 + n]), list(out[2 + n:2 + 2 * n]), out[0], out[1]), out[-1]


def scatter_wait(flight, after, name):
    srcs, lands, ssem, rsem = flight
    n = len(srcs)

    def body(*refs):
        place = _place()
        for j in range(n):
            for k in range(3):
                cp = _scatter_copy(refs[j], refs[n + j], j, k, refs[2 * n], refs[2 * n + 1], place, "mine")
                cp.wait_send()
                cp.wait_recv()

    out = pl.pallas_call(
        body, name=name, in_specs=[IN_HBM] * (2 * n) + [IN_SEM, IN_SEM, ANY], out_specs=[IN_HBM] * (2 * n),
        out_shape=[pltpu.HBM(b.shape, b.dtype) for b in list(srcs) + list(lands)],
        input_output_aliases={i: i for i in range(2 * n)},
        compiler_params=pltpu.CompilerParams(has_side_effects=SIDE_EFFECT),
    )(*srcs, *lands, ssem, rsem, after)
    return list(out[:n]), list(out[n:])


def pair_exchange(gbufs, name):
    n = len(gbufs)

    def body(*refs):
        ins, outs = refs[:n], refs[n:2 * n]
        ssem, rsem = refs[2 * n:]
        x, y, c = _place()
        cps = [pltpu.make_async_remote_copy(
            src_ref=_half(ins[j], 1 - c, axis=1), dst_ref=outs[j], send_sem=ssem.at[j], recv_sem=rsem.at[j],
            device_id=(x, y, 1 - c), device_id_type=MESH) for j in range(n)]
        for cp in cps:
            cp.start()
        for cp in cps:
            cp.wait()

    return pl.pallas_call(
        body, name=name, in_specs=[ANY] * n, out_specs=[ANY] * n,
        out_shape=[jax.ShapeDtypeStruct((4, g.shape[1] // 2, g.shape[2]), g.dtype) for g in gbufs],
        scratch_shapes=[pltpu.SemaphoreType.DMA((n,)), pltpu.SemaphoreType.DMA((n,))],
    )(*gbufs)


def _row_tile(rows, cap=512):
    return max(t for t in range(16, min(rows, cap) + 1, 16) if rows % t == 0)


def pair_sum(mine, theirs, core, name):
    _, R, C = mine.shape
    half = R // 2
    tr = _row_tile(half)
    nt = half // tr

    def body(c_ref, a_ref, b_ref, o_ref):
        o_ref[...] = (a_ref[...].astype(f32) + b_ref[...].astype(f32)).astype(bf16)

    return pl.pallas_call(
        body, name=name, out_shape=jax.ShapeDtypeStruct(theirs.shape, bf16),
        grid_spec=pltpu.PrefetchScalarGridSpec(
            num_scalar_prefetch=1, grid=(4, nt),
            in_specs=[pl.BlockSpec((None, tr, C), lambda s, i, c_ref: (s, c_ref[0] * nt + i, 0)),
                      pl.BlockSpec((None, tr, C), lambda s, i, c_ref: (s, i, 0))],
            out_specs=pl.BlockSpec((None, tr, C), lambda s, i, c_ref: (s, i, 0))),
        compiler_params=_cp(("parallel", "parallel")),
    )(jnp.reshape(core, (1,)).astype(jnp.int32), mine, theirs)


def final_exchange(fins):
    n = len(fins)

    def body(*refs):
        outs = refs[n:2 * n]
        ssem, rsem = refs[2 * n:]
        x, y, c = _place()
        cps = [pltpu.make_async_remote_copy(
            src_ref=_half(outs[j], c), dst_ref=_half(outs[j], c), send_sem=ssem.at[j], recv_sem=rsem.at[j],
            device_id=(x, y, 1 - c), device_id_type=MESH) for j in range(n)]
        for cp in cps:
            cp.start()
        for cp in cps:
            cp.wait()

    return pl.pallas_call(
        body, name="final_exchange", in_specs=[ANY] * n, out_specs=[ANY] * n,
        out_shape=[jax.ShapeDtypeStruct(f.shape, f.dtype) for f in fins],
        input_output_aliases={j: j for j in range(n)},
        scratch_shapes=[pltpu.SemaphoreType.DMA((n,)), pltpu.SemaphoreType.DMA((n,))],
    )(*fins)


def adamw_big(w, m, v, gs, row0, name):
    L, R, C = w.shape
    tr = _row_tile(math.gcd(R, row0) if row0 else R, max(16, 262144 // C // 16 * 16))
    b0 = row0 // tr

    def body(*refs):
        w_ref, m_ref, v_ref = refs[:3]
        g_refs = refs[3:3 + L]
        g_ref, d_ref, nm_ref, nv_ref = refs[3 + L:]
        g = g_refs[0][...]
        for l in range(1, L):
            g = jnp.where(pl.program_id(0) == l, g_refs[l][...], g)
        d, nm, nv = _adamw_math(w_ref[...], g, m_ref[...], v_ref[...])
        g_ref[...] = g
        d_ref[...] = d
        nm_ref[...] = nm
        nv_ref[...] = nv

    own = pl.BlockSpec((None, tr, C), lambda l, i: (l, i, 0))
    off = pl.BlockSpec((tr, C), lambda l, i: (b0 + i, 0))
    return pl.pallas_call(
        body, name=name, grid=(L, R // tr), in_specs=[own, own, own] + [off] * L, out_specs=[own] * 4,
        out_shape=[jax.ShapeDtypeStruct((L, R, C), f32)] * 4, compiler_params=_cp(("parallel", "parallel")),
    )(w, m, v, *gs)


def _adamw_math(w, g, m, v):
    m = B1 * m + (1.0 - B1) * g
    v = B2 * v + (1.0 - B2) * (g * g)
    m_hat = m / (1.0 - B1 ** STEP)
    v_hat = v / (1.0 - B2 ** STEP)
    delta = -LR * (m_hat / (jnp.sqrt(v_hat) + AEPS) + WD * w)
    return delta, m, v


def adamw_small(w, m, v, g):
    def body(w_ref, m_ref, v_ref, g_ref, d_ref, nm_ref, nv_ref):
        d, nm, nv = _adamw_math(w_ref[...], g_ref[...], m_ref[...], v_ref[...])
        d_ref[...] = d
        nm_ref[...] = nm
        nv_ref[...] = nv

    return pl.pallas_call(body, name="adamw_small", out_shape=[jax.ShapeDtypeStruct(w.shape, f32)] * 3)(w, m, v, g)


CONV =(("conv_qkv_b", 2), ("ffn_conv_w", 2))
SMALL = ("rel_bias", "norm_mix_g", "norm_mem_g", "sinks_a", "a_log_b", "dt_bias_b", "out_norm_g_b", "norm_ffn_g",
         "ffn_conv_b", "final_norm_g")
WEIGHTS = ("rel_bias", "norm_mix_g", "norm_mem_g", "w_mem_kv", "w_out", "w_in_a", "sinks_a", "w_in_b", "conv_qkv_b",
           "a_log_b", "dt_bias_b", "out_norm_g_b", "norm_ffn_g", "w_gate_up", "ffn_conv_w", "ffn_conv_b", "w_down",
           "final_norm_g")
ARGS = ("x", "mem") + WEIGHTS + ("loss_target",) + tuple("m_" + n for n in WEIGHTS) + tuple("v_" + n for n in WEIGHTS)


def _rows(a, width):
    flat = a.reshape(-1)
    pad = (-flat.shape[0]) % (8 * width)
    if pad:
        flat = jnp.concatenate([flat, jnp.zeros((pad,), a.dtype)])
    return flat.reshape(-1, width)


def _nrows(shape, width):
    return _pad_to(-(-math.prod(shape) // width), 8)


def _pack(arrs, width, total_rows, dtype):
    parts = [_rows(a.astype(dtype), width) for a in arrs]
    used = sum(p.shape[0] for p in parts)
    if total_rows > used:
        parts.append(jnp.zeros((total_rows - used, width), dtype))
    return jnp.concatenate(parts, axis=0)


def _unpack(buf, shapes, width):
    out, r = [], 0
    for s in shapes:
        n = _nrows(s, width)
        out.append(buf[r:r + n].reshape(-1)[:math.prod(s)].reshape(s))
        r += n
    return out


def _pad_to(n, mult):
    return -(-n // mult) * mult


def kernel(x, mem, rel_bias, norm_mix_g, norm_mem_g, w_mem_kv, w_out, w_in_a, sinks_a, w_in_b, conv_qkv_b, a_log_b, dt_bias_b, out_norm_g_b, norm_ffn_g, w_gate_up, ffn_conv_w, ffn_conv_b, w_down, final_norm_g, loss_target, m_rel_bias, m_norm_mix_g, m_norm_mem_g, m_w_mem_kv, m_w_out, m_w_in_a, m_sinks_a, m_w_in_b, m_conv_qkv_b, m_a_log_b, m_dt_bias_b, m_out_norm_g_b, m_norm_ffn_g, m_w_gate_up, m_ffn_conv_w, m_ffn_conv_b, m_w_down, m_final_norm_g, v_rel_bias, v_norm_mix_g, v_norm_mem_g, v_w_mem_kv, v_w_out, v_w_in_a, v_sinks_a, v_w_in_b, v_conv_qkv_b, v_a_log_b, v_dt_bias_b, v_out_norm_g_b, v_norm_ffn_g, v_w_gate_up, v_ffn_conv_w, v_ffn_conv_b, v_w_down, v_final_norm_g):
    A = dict(zip(ARGS, (x, mem, rel_bias, norm_mix_g, norm_mem_g, w_mem_kv, w_out, w_in_a, sinks_a, w_in_b, conv_qkv_b, a_log_b, dt_bias_b, out_norm_g_b, norm_ffn_g, w_gate_up, ffn_conv_w, ffn_conv_b, w_down, final_norm_g, loss_target, m_rel_bias, m_norm_mix_g, m_norm_mem_g, m_w_mem_kv, m_w_out, m_w_in_a, m_sinks_a, m_w_in_b, m_conv_qkv_b, m_a_log_b, m_dt_bias_b, m_out_norm_g_b, m_norm_ffn_g, m_w_gate_up, m_ffn_conv_w, m_ffn_conv_b, m_w_down, m_final_norm_g, v_rel_bias, v_norm_mix_g, v_norm_mem_g, v_w_mem_kv, v_w_out, v_w_in_a, v_sinks_a, v_w_in_b, v_conv_qkv_b, v_a_log_b, v_dt_bias_b, v_out_norm_g_b, v_norm_ffn_g, v_w_gate_up, v_ffn_conv_w, v_ffn_conv_b, v_w_down, v_final_norm_g)))
    chip = 2 * lax.axis_index("x") + lax.axis_index("y")
    core = lax.axis_index("c")
    n_down, n_out, n_mem = w_down.shape[1], w_out.shape[1], w_mem_kv.shape[1]

    def own_slot(shard):
        return lax.dynamic_update_index_in_dim(lax.empty((4,) + shard.shape, shard.dtype), shard, chip, 0)

    def bslot(w):
        return own_slot(w.astype(bf16))

    groups = {
        ("w_in", 0): [bslot(w_in_a[0])],
        ("w_mem", 0): [bslot(w_mem_kv[0]), bslot(w_mem_kv[1]), own_slot(conv_qkv_b[0]),
                       own_slot(ffn_conv_w.reshape(6, -1))],
        ("w_out", 0): [bslot(w_out[0])], ("w_gu", 0): [bslot(w_gate_up[0])], ("w_down", 0): [bslot(w_down[0])],
        ("w_in", 1): [bslot(w_in_b[0])],
        ("w_out", 1): [bslot(w_out[1])], ("w_gu", 1): [bslot(w_gate_up[1])], ("w_down", 1): [bslot(w_down[1])],
    }
    flights = dict(zip(groups, gather_start(list(groups.values()))))
    P = {"rel_bias": rel_bias, "sinks": sinks_a[0], "a_log": a_log_b[0], "dt_bias": dt_bias_b[0],
         "out_norm_g": out_norm_g_b[0], "g_mix": norm_mix_g, "g_mem": norm_mem_g, "g_ffn": norm_ffn_g,
         "g_fin": final_norm_g, "ffn_cb": [ffn_conv_b[0], ffn_conv_b[1]], "w_mem": [None, None], "w_out": [None, None],
         "w_gu": [None, None], "w_down": [None, None], "ffn_cw": [None, None]}

    def rows4(g):
        return g.reshape(4 * g.shape[1], g.shape[2])

    def arrive(key, after):
        if key not in flights:
            return
        got = gather_wait(flights.pop(key), after, "gather_wait_%s%d" % key)
        name, i = key
        if name == "w_in":
            P["w_in_a" if i == 0 else "w_in_b"] = (_lay_in_a if i == 0 else _lay_in_b)(_unchip_cols(got[0]))
        elif name == "w_mem":
            P["w_mem"] = [rows4(got[0]), rows4(got[1])]
            P["conv_qkv"] = _unchip_cols(got[2])
            cw = _unchip_cols(got[3]).reshape(2, 3, D_FF)
            P["ffn_cw"] = [cw[0], cw[1]]
        elif name == "w_out":
            P["w_out"][i] = _lay_out_a(rows4(got[0])) if i == 0 else rows4(got[0])
        elif name == "w_gu":
            P["w_gu"][i] = got[0]
        else:
            P["w_down"][i] = rows4(got[0])

    def chip_rows(g):
        return g.reshape(4, g.shape[0] // 4, g.shape[-1])

    sent, started = {}, []

    def ready(key, G, dep):
        kind, i = key
        tag = "%s%d" % key
        if kind == "ffn":
            names, partial = ("gu", "down"), [G["w_gu"][i], chip_rows(G["w_down"][i]).astype(bf16)]
        else:
            g_out = _unlay_out_a(G["w_out"][0]) if i == 0 else G["w_out"][1]
            g_in = _unlay_in_a(G["w_in_a"]) if i == 0 else _unlay_in_b(G["w_in_b"])
            names = ("out", "in", "mem")
            partial = [chip_rows(g_out).astype(bf16), _chip_cols(g_in).astype(bf16), chip_rows(G["w_mem"][i]).astype(bf16)]
        theirs = pair_exchange(partial, "pair_exchange_" + tag)
        pair = [pair_sum(p, t, core, "pair_sum_%s%d" % (nm, i)) for p, t, nm in zip(partial, theirs, names)]
        if key == ("mix", 0):
            sent[key] = (names, pair, chip_scatter(pair))
            return dep
        flight, token = scatter_start(pair, "scatter_start_" + tag)
        sent[key] = (names, flight)
        started.append(token[0, 0])
        if dep is not None:
            while started:
                dep = dep + started.pop()
        return dep

    P["arrive"], P["ready"] = arrive, ready

    loss, dx, G = _local_step(x[0], mem[0], loss_target[0], P)
    gfull = _grads_to_ref(G)

    fin = {}
    for key in (("ffn", 1), ("mix", 1), ("ffn", 0), ("mix", 0)):
        if key == ("mix", 0):
            names, pair, arrived = sent[key]
        else:
            names, flight = sent[key]
            pair, arrived = scatter_wait(flight, dx, "scatter_wait_%s%d" % key)
        for nm, p, r in zip(names, pair, arrived):
            fin[nm, key[1]] = sum_slots(p, r, chip, core, "sum_slots_%s%d" % (nm, key[1]))
    order = list(fin)
    done = dict(zip(order, final_exchange([fin[k] for k in order])))

    sm_shapes = [A[n].shape for n in SMALL] + [gfull[n].shape for n, _ in CONV] + [(LANE,)]
    sm_rows = _pad_to(sum(_nrows(s, LANE) for s in sm_shapes), 8)
    sbuf = _pack([gfull[n] for n in SMALL] + [gfull[n] for n, _ in CONV] + [loss[0]], LANE, sm_rows, f32)
    tot = _unpack(allreduce_small(sbuf), sm_shapes, LANE)
    gsmall = dict(zip(SMALL, tot[:len(SMALL)]))
    for (n, axis), t in zip(CONV, tot[len(SMALL):len(SMALL) + len(CONV)]):
        sh = A[n].shape[axis]
        gsmall[n] = lax.dynamic_slice_in_dim(t, chip * sh, sh, axis)
    loss_out = tot[-1][0]

    out = {}
    plan = (("w_gate_up", [done["gu", 0], done["gu", 1]]), ("w_down", [done["down", 0], done["down", 1]]),
            ("w_out", [done["out", 0], done["out", 1]]), ("w_mem_kv", [done["mem", 0], done["mem", 1]]),
            ("w_in_a", [done["in", 0]]), ("w_in_b", [done["in", 1]]))
    for n, gs in plan:
        shape3 = (len(gs),) + gs[0].shape
        res = adamw_big(A[n].reshape(shape3), A["m_" + n].reshape(shape3), A["v_" + n].reshape(shape3), gs, 0,
                        "adamw_" + n)
        for key, r in zip(("grad_", "delta_", "new_m_", "new_v_"), res):
            out[key + n] = r.reshape(A[n].shape)
    names = SMALL + tuple(n for n, _ in CONV)
    shapes = [A[n].shape for n in names]
    rows = _pad_to(sum(_nrows(s, LANE) for s in shapes), 8)
    packs = [_pack([src[n] for n in names], LANE, rows, f32)
             for src in ({n: A[n] for n in names}, {n: A["m_" + n] for n in names}, {n: A["v_" + n] for n in names}, gsmall)]
    res = adamw_small(*packs)
    for key, r in zip(("delta_", "new_m_", "new_v_"), res):
        for n, a in zip(names, _unpack(r, shapes, LANE)):
            out[key + n] = a
    for n in names:
        out["grad_" + n] = gsmall[n]
    return (loss_out, dx[None], *[out["grad_" + n] for n in WEIGHTS], *[out["delta_" + n] for n in WEIGHTS],
            *[out["new_m_" + n] for n in WEIGHTS], *[out["new_v_" + n] for n in WEIGHTS])
```

```python
import functools
import math

import numpy as np
import jax
import jax.numpy as jnp
from jax import lax
from jax.experimental import pallas as pl
from jax.experimental.pallas import tpu as pltpu

f32 = jnp.float32
bf16 = jnp.bfloat16
HI = lax.Precision.HIGHEST
MESH = pl.DeviceIdType.MESH

D = 1024
MEM_LEN = 256
EPS = 1e-6
A_HEADS, A_KV, A_DH = 12, 2, 64
A_Q = 768
BLK = 128
N_BUCKETS, MAX_DIST = 32, 128
B_QK, B_V, B_DH = 384, 768, 128
B_QKV = 1536
CHUNK = 64
X_Q = 256
D_FF = 2816
IN_A = 1280
IN_B = 2572
IN_B_PAD = 2688
LANE = 128
VMEM_LIMIT = 56 * 1024 * 1024

LR, B1, B2, AEPS, WD, STEP = 0.001, 0.9, 0.999, 1e-08, 0.01, 10


def _cp(sem=None):
    return pltpu.CompilerParams(dimension_semantics=sem, vmem_limit_bytes=VMEM_LIMIT)


def _dg(a, b, ca, cb, prec=None):
    return lax.dot_general(a, b, (((ca,), (cb,)), ((), ())), precision=prec, preferred_element_type=f32)


@jax.custom_vjp
def bdot(a, b):
    return _dg(a.astype(bf16), b.astype(bf16), 1, 0)


def _bdot_f(a, b):
    return bdot(a, b), (a, b)


def _bdot_b(res, g):
    a, b = res
    gb = g.astype(bf16)
    return _dg(gb, b.astype(bf16), 1, 1), _dg(a.astype(bf16), gb, 0, 0)


bdot.defvjp(_bdot_f, _bdot_b)


@jax.custom_vjp
def bdot_nt(a, b):
    return _dg(a.astype(bf16), b.astype(bf16), 1, 1)


def _bdot_nt_f(a, b):
    return bdot_nt(a, b), (a, b)


def _bdot_nt_b(res, g):
    a, b = res
    gb = g.astype(bf16)
    return _dg(gb, b.astype(bf16), 1, 0), _dg(gb, a.astype(bf16), 0, 0)


bdot_nt.defvjp(_bdot_nt_f, _bdot_nt_b)


def _shift_rows(x, s, down):
    n = x.shape[0]
    row = lax.broadcasted_iota(jnp.int32, x.shape, 0)
    if down:
        return jnp.where(row >= s, pltpu.roll(x, s, 0), 0.0)
    return jnp.where(row < n - s, pltpu.roll(x, n - s, 0), 0.0)


@functools.partial(jax.custom_vjp, nondiff_argnums=(1,))
def shift_down(x, s):
    return _shift_rows(x, s, True)


def _sd_f(x, s):
    return _shift_rows(x, s, True), None


def _sd_b(s, _, g):
    return (_shift_rows(g, s, False),)


shift_down.defvjp(_sd_f, _sd_b)


def _sigmoid(x):
    return 1.0 / (1.0 + jnp.exp(-x))


def _silu(x):
    return x * _sigmoid(x)


def _rms(x, g):
    return x * lax.rsqrt(jnp.mean(x * x, axis=-1, keepdims=True) + EPS) * g


def _tile(n, cap):
    u = n // LANE
    best = 1
    for d in range(1, u + 1):
        if u % d == 0 and d * LANE <= cap:
            best = d
    return best * LANE


def mm_nn(a, w, res=None, out_dtype=f32, name="mm_nn"):
    M, K = a.shape
    N = w.shape[1]
    tm, tn = min(512, M), _tile(N, 1024)

    def body(*refs):
        if res is None:
            a_ref, w_ref, o_ref = refs
            o_ref[...] = _dg(a_ref[...].astype(bf16), w_ref[...], 1, 0).astype(out_dtype)
        else:
            a_ref, w_ref, r_ref, o_ref = refs
            o_ref[...] = (r_ref[...] + _dg(a_ref[...].astype(bf16), w_ref[...], 1, 0)).astype(out_dtype)

    in_specs = [pl.BlockSpec((tm, K), lambda n, m: (m, 0)), pl.BlockSpec((K, tn), lambda n, m: (0, n))]
    args = [a, w]
    if res is not None:
        in_specs.append(pl.BlockSpec((tm, tn), lambda n, m: (m, n)))
        args.append(res)
    return pl.pallas_call(
        body, name=name, grid=(N // tn, M // tm), in_specs=in_specs,
        out_specs=pl.BlockSpec((tm, tn), lambda n, m: (m, n)),
        out_shape=jax.ShapeDtypeStruct((M, N), out_dtype),
        compiler_params=_cp(("parallel", "parallel")),
    )(*args)


def mm_nt(dy, w, name="mm_nt"):
    M, N = dy.shape
    K = w.shape[0]
    tm, tn = min(512, M), _tile(N, 1024)

    def body(dy_ref, w_ref, o_ref):
        @pl.when(pl.program_id(1) == 0)
        def _():
            o_ref[...] = jnp.zeros_like(o_ref)
        o_ref[...] += _dg(dy_ref[...].astype(bf16), w_ref[...], 1, 1)

    return pl.pallas_call(
        body, name=name, grid=(M // tm, N // tn),
        in_specs=[pl.BlockSpec((tm, tn), lambda m, n: (m, n)), pl.BlockSpec((K, tn), lambda m, n: (0, n))],
        out_specs=pl.BlockSpec((tm, K), lambda m, n: (m, 0)),
        out_shape=jax.ShapeDtypeStruct((M, K), f32),
        compiler_params=_cp(("parallel", "arbitrary")),
    )(dy, w)


def mm_tn(a, dy, name="mm_tn"):
    M, K = a.shape
    N = dy.shape[1]
    tm, tk, tn = min(512, M), _tile(K, 1408), _tile(N, 1024)

    def body(a_ref, dy_ref, o_ref):
        @pl.when(pl.program_id(2) == 0)
        def _():
            o_ref[...] = jnp.zeros_like(o_ref)
        o_ref[...] += _dg(a_ref[...].astype(bf16), dy_ref[...].astype(bf16), 0, 0)

    return pl.pallas_call(
        body, name=name, grid=(K // tk, N // tn, M // tm),
        in_specs=[pl.BlockSpec((tm, tk), lambda k, n, m: (m, k)), pl.BlockSpec((tm, tn), lambda k, n, m: (m, n))],
        out_specs=pl.BlockSpec((tk, tn), lambda k, n, m: (k, n)),
        out_shape=jax.ShapeDtypeStruct((K, N), f32),
        compiler_params=_cp(("parallel", "parallel", "arbitrary")),
    )(a, dy)


def rms_fwd(h, g, name):
    S = h.shape[0]
    t = min(512, S)

    def body(h_ref, g_ref, o_ref):
        o_ref[...] = _rms(h_ref[...], g_ref[...]).astype(bf16)

    return pl.pallas_call(
        body, name=name, grid=(S // t,),
        in_specs=[pl.BlockSpec((t, D), lambda i: (i, 0)), pl.BlockSpec((1, D), lambda i: (0, 0))],
        out_specs=pl.BlockSpec((t, D), lambda i: (i, 0)),
        out_shape=jax.ShapeDtypeStruct((S, D), bf16),
        compiler_params=_cp(("parallel",)),
    )(h, g.reshape(1, D))


def rms_bwd(h, g, dn, dres, name):
    S = h.shape[0]
    t = min(512, S)

    def body(h_ref, g_ref, dn_ref, dr_ref, dh_ref, dg_ref):
        @pl.when(pl.program_id(0) == 0)
        def _():
            dg_ref[...] = jnp.zeros_like(dg_ref)
        _, vjp = jax.vjp(_rms, h_ref[...], g_ref[...])
        dh, dg = vjp(dn_ref[...])
        dh_ref[...] = dr_ref[...] + dh
        dg_ref[...] += dg

    tok = pl.BlockSpec((t, D), lambda i: (i, 0))
    vec = pl.BlockSpec((1, D), lambda i: (0, 0))
    return pl.pallas_call(
        body, name=name, grid=(S // t,), in_specs=[tok, vec, tok, tok], out_specs=[tok, vec],
        out_shape=[jax.ShapeDtypeStruct((S, D), f32), jax.ShapeDtypeStruct((1, D), f32)],
        compiler_params=_cp(("arbitrary",)),
    )(h, g.reshape(1, D), dn, dres)


def loss_head(h, g, target):
    S = h.shape[0]
    t = min(512, S)

    def f(hh, gg, tt):
        err = _rms(hh, gg) - tt
        return 0.5 * jnp.sum(jnp.mean(err * err, axis=-1, keepdims=True), axis=0, keepdims=True)

    def body(h_ref, g_ref, t_ref, loss_ref, dh_ref, dg_ref):
        @pl.when(pl.program_id(0) == 0)
        def _():
            dg_ref[...] = jnp.zeros_like(dg_ref)
            loss_ref[...] = jnp.zeros_like(loss_ref)
        val, vjp = jax.vjp(lambda a, b: f(a, b, t_ref[...]), h_ref[...], g_ref[...])
        dh, dg = vjp(jnp.ones((1, 1), f32))
        dh_ref[...] = dh
        dg_ref[...] += dg
        loss_ref[...] += jnp.broadcast_to(val, loss_ref.shape)

    tok = pl.BlockSpec((t, D), lambda i: (i, 0))
    vec = pl.BlockSpec((1, D), lambda i: (0, 0))
    return pl.pallas_call(
        body, name="loss_head", grid=(S // t,), in_specs=[tok, vec, tok],
        out_specs=[pl.BlockSpec((1, LANE), lambda i: (0, 0)), tok, vec],
        out_shape=[jax.ShapeDtypeStruct((1, LANE), f32), jax.ShapeDtypeStruct((S, D), f32),
                   jax.ShapeDtypeStruct((1, D), f32)],
        compiler_params=_cp(("arbitrary",)),
    )(h, g.reshape(1, D), target)


def memkv_fwd(mem, g, w, name):
    def body(m_ref, g_ref, w_ref, o_ref):
        o_ref[...] = _dg(_rms(m_ref[...], g_ref[...]).astype(bf16), w_ref[...], 1, 0)

    return pl.pallas_call(
        body, name=name, out_shape=jax.ShapeDtypeStruct((MEM_LEN, 2 * X_Q), f32), compiler_params=_cp(),
    )(mem, g.reshape(1, D), w)


def memkv_bwd(mem, g, w, dkv, name):
    def body(m_ref, g_ref, w_ref, d_ref, dg_ref, dw_ref):
        n, vjp = jax.vjp(lambda gg: _rms(m_ref[...], gg), g_ref[...])
        db = d_ref[...].astype(bf16)
        dw_ref[...] = _dg(n.astype(bf16), db, 0, 0)
        dg_ref[...] = vjp(_dg(db, w_ref[...], 1, 1))[0]

    return pl.pallas_call(
        body, name=name,
        out_shape=[jax.ShapeDtypeStruct((1, D), f32), jax.ShapeDtypeStruct((D, 2 * X_Q), f32)],
        compiler_params=_cp(),
    )(mem, g.reshape(1, D), w, dkv)


def _xattn_f(xq, mk, mv):
    lane = lax.broadcasted_iota(jnp.int32, (1, X_Q), 1)
    out = jnp.zeros(xq.shape, f32)
    for hd in range(4):
        msk = (lane // 64 == hd).astype(f32)
        s = bdot_nt(xq * msk, mk) * (64 ** -0.5)
        m = lax.stop_gradient(jnp.max(s, axis=-1, keepdims=True))
        p = jnp.exp(s - m)
        p = p / jnp.sum(p, axis=-1, keepdims=True)
        out = out + bdot(p, mv * msk)
    return out


def xattn_fwd(proj, col, kv, name):
    S = proj.shape[0]
    t = min(512, S)
    cb = col // X_Q

    def body(q_ref, k_ref, v_ref, o_ref):
        o_ref[...] = _xattn_f(q_ref[...], k_ref[...], v_ref[...])

    return pl.pallas_call(
        body, name=name, grid=(S // t,),
        in_specs=[pl.BlockSpec((t, X_Q), lambda i: (i, cb)), pl.BlockSpec((MEM_LEN, X_Q), lambda i: (0, 0)),
                  pl.BlockSpec((MEM_LEN, X_Q), lambda i: (0, 1))],
        out_specs=pl.BlockSpec((t, X_Q), lambda i: (i, 0)),
        out_shape=jax.ShapeDtypeStruct((S, X_Q), f32),
        compiler_params=_cp(("parallel",)),
    )(proj, kv, kv)


def xattn_bwd(proj, col, kv, dmix, name):
    S = proj.shape[0]
    t = min(512, S)
    cb = col // X_Q

    def body(q_ref, k_ref, v_ref, do_ref, dq_ref, dk_ref, dv_ref):
        @pl.when(pl.program_id(0) == 0)
        def _():
            dk_ref[...] = jnp.zeros_like(dk_ref)
            dv_ref[...] = jnp.zeros_like(dv_ref)
        _, vjp = jax.vjp(_xattn_f, q_ref[...], k_ref[...], v_ref[...])
        dq, dk, dv = vjp(do_ref[...])
        dq_ref[...] = dq
        dk_ref[...] += dk
        dv_ref[...] += dv

    kvb = pl.BlockSpec((MEM_LEN, X_Q), lambda i: (0, 0))
    dq, dk, dv = pl.pallas_call(
        body, name=name, grid=(S // t,),
        in_specs=[pl.BlockSpec((t, X_Q), lambda i: (i, cb)), kvb,
                  pl.BlockSpec((MEM_LEN, X_Q), lambda i: (0, 1)), pl.BlockSpec((t, X_Q), lambda i: (i, 3))],
        out_specs=[pl.BlockSpec((t, X_Q), lambda i: (i, 0)), kvb, kvb],
        out_shape=[jax.ShapeDtypeStruct((S, X_Q), f32), jax.ShapeDtypeStruct((MEM_LEN, X_Q), f32),
                   jax.ShapeDtypeStruct((MEM_LEN, X_Q), f32)],
        compiler_params=_cp(("arbitrary",)),
    )(proj, kv, kv, dmix)
    return dq, jnp.concatenate([dk, dv], axis=1)


def _bucket_map():
    qi = np.arange(BLK)[:, None]
    kj = np.arange(2 * BLK)[None, :]
    n = np.maximum(BLK + qi - kj, 0)
    max_exact = N_BUCKETS // 2
    nf = np.maximum(n, 1).astype(np.float64)
    large = max_exact + (np.log(nf / max_exact) / math.log(MAX_DIST / max_exact)
                         * (N_BUCKETS - max_exact)).astype(np.int32)
    large = np.minimum(large, N_BUCKETS - 1)
    return np.where(n < max_exact, n, large).astype(np.int32)


def bias_build(rel_bias):
    def body(rb_ref, bk_ref, o_ref):
        bk = bk_ref[...]
        for h in range(A_HEADS):
            acc = jnp.zeros((BLK, 2 * BLK), f32)
            for b in range(N_BUCKETS):
                acc = jnp.where(bk == b, rb_ref[b, h], acc)
            o_ref[h] = acc

    return pl.pallas_call(
        body, name="bias_build",
        in_specs=[pl.BlockSpec(memory_space=pltpu.SMEM), pl.BlockSpec(memory_space=pltpu.VMEM)],
        out_specs=pl.BlockSpec(memory_space=pltpu.VMEM),
        out_shape=jax.ShapeDtypeStruct((A_HEADS, BLK, 2 * BLK), f32), compiler_params=_cp(),
    )(rel_bias, jnp.asarray(_bucket_map()))


def bias_grad(dbias):
    def body(d_ref, bk_ref, o_ref):
        bk = bk_ref[...]
        row = lax.broadcasted_iota(jnp.int32, (N_BUCKETS, LANE), 0)
        lane = lax.broadcasted_iota(jnp.int32, (N_BUCKETS, LANE), 1)
        acc = jnp.zeros((N_BUCKETS, LANE), f32)
        for h in range(A_HEADS):
            d = d_ref[h]
            for b in range(N_BUCKETS):
                s = jnp.sum(jnp.where(bk == b, d, 0.0), keepdims=True)
                acc = acc + jnp.where((row == b) & (lane == h), s, 0.0)
        o_ref[...] = acc

    return pl.pallas_call(
        body, name="bias_grad", out_shape=jax.ShapeDtypeStruct((N_BUCKETS, LANE), f32), compiler_params=_cp(),
    )(dbias, jnp.asarray(_bucket_map()))


def _swa_f(qb, kp, kc, vp, vc, bias, sk, first):
    kband = jnp.concatenate([kp, kc], axis=0)
    vband = jnp.concatenate([vp, vc], axis=0)
    qi = lax.broadcasted_iota(jnp.int32, (BLK, 2 * BLK), 0)
    kj = lax.broadcasted_iota(jnp.int32, (BLK, 2 * BLK), 1)
    rel = kj - qi
    ok = (rel >= 1) & (rel <= BLK) & ((kj >= BLK) | jnp.logical_not(first))
    lane = lax.broadcasted_iota(jnp.int32, (1, LANE), 1)
    lane_b = lax.broadcasted_iota(jnp.int32, (BLK, LANE), 1)
    outs = []
    for p in range(A_HEADS // 2):
        qp = qb[:, LANE * p:LANE * (p + 1)]
        acc = jnp.zeros((BLK, LANE), f32)
        for g in range(2):
            h = g * (A_HEADS // 2) + p
            msk = (lane // A_DH == g).astype(f32)
            s = bdot_nt(qp * msk, kband) * (A_DH ** -0.5) + bias[h]
            s = jnp.where(ok, s, -1e30)
            skb = jnp.broadcast_to(sk[h:h + 1, :], (BLK, LANE))
            sink = jnp.sum(jnp.where(lane_b == 0, skb, 0.0), axis=-1, keepdims=True)
            m = lax.stop_gradient(jnp.maximum(jnp.max(s, axis=-1, keepdims=True), sink))
            e = jnp.exp(s - m)
            prob = e / (jnp.sum(e, axis=-1, keepdims=True) + jnp.exp(sink - m))
            acc = acc + bdot(prob, vband) * msk
        outs.append(acc)
    return jnp.concatenate(outs, axis=1)


def _swa_specs(nb, rev):
    bi = (lambda i: nb - 1 - i) if rev else (lambda i: i)
    return [
        pl.BlockSpec((BLK, A_Q), lambda i: (bi(i), 0)),
        pl.BlockSpec((BLK, LANE), lambda i: (jnp.maximum(bi(i) - 1, 0), 6)),
        pl.BlockSpec((BLK, LANE), lambda i: (bi(i), 6)),
        pl.BlockSpec((BLK, LANE), lambda i: (jnp.maximum(bi(i) - 1, 0), 7)),
        pl.BlockSpec((BLK, LANE), lambda i: (bi(i), 7)),
        pl.BlockSpec((A_HEADS, BLK, 2 * BLK), lambda i: (0, 0, 0)),
        pl.BlockSpec((16, LANE), lambda i: (0, 0)),
    ]


def swa_fwd(proj, bias, sk):
    S = proj.shape[0]
    nb = S // BLK

    def body(q_ref, kp_ref, kc_ref, vp_ref, vc_ref, b_ref, s_ref, o_ref):
        o_ref[...] = _swa_f(q_ref[...], kp_ref[...], kc_ref[...], vp_ref[...], vc_ref[...], b_ref[...], s_ref[...],
                            pl.program_id(0) == 0)

    return pl.pallas_call(
        body, name="swa_fwd", grid=(nb,), in_specs=_swa_specs(nb, False),
        out_specs=pl.BlockSpec((BLK, A_Q), lambda i: (i, 0)),
        out_shape=jax.ShapeDtypeStruct((S, A_Q), f32), compiler_params=_cp(("parallel",)),
    )(proj, proj, proj, proj, proj, bias, sk)


def swa_bwd(proj, bias, sk, dmix):
    S = proj.shape[0]
    nb = S // BLK

    def body(q_ref, kp_ref, kc_ref, vp_ref, vc_ref, b_ref, s_ref, do_ref, dqkv_ref, db_ref, ds_ref, ck, cv):
        i = pl.program_id(0)

        @pl.when(i == 0)
        def _():
            db_ref[...] = jnp.zeros_like(db_ref)
            ds_ref[...] = jnp.zeros_like(ds_ref)
            ck[...] = jnp.zeros_like(ck)
            cv[...] = jnp.zeros_like(cv)
        first = i == nb - 1
        _, vjp = jax.vjp(lambda *a: _swa_f(*a, first), q_ref[...], kp_ref[...], kc_ref[...], vp_ref[...],
                         vc_ref[...], b_ref[...], s_ref[...])
        dq, dkp, dkc, dvp, dvc, db, ds = vjp(do_ref[...])
        dqkv_ref[...] = jnp.concatenate([dq, dkc + ck[...], dvc + cv[...]], axis=1)
        ck[...] = dkp
        cv[...] = dvp
        db_ref[...] += db
        ds_ref[...] += ds

    return pl.pallas_call(
        body, name="swa_bwd", grid=(nb,),
        in_specs=_swa_specs(nb, True) + [pl.BlockSpec((BLK, A_Q), lambda i: (nb - 1 - i, 0))],
        out_specs=[pl.BlockSpec((BLK, D), lambda i: (nb - 1 - i, 0)),
                   pl.BlockSpec((A_HEADS, BLK, 2 * BLK), lambda i: (0, 0, 0)),
                   pl.BlockSpec((16, LANE), lambda i: (0, 0))],
        out_shape=[jax.ShapeDtypeStruct((S, D), f32), jax.ShapeDtypeStruct((A_HEADS, BLK, 2 * BLK), f32),
                   jax.ShapeDtypeStruct((16, LANE), f32)],
        scratch_shapes=[pltpu.VMEM((BLK, LANE), f32), pltpu.VMEM((BLK, LANE), f32)],
        compiler_params=_cp(("arbitrary",)),
    )(proj, proj, proj, proj, proj, bias, sk, dmix)


def _dnprep_f(x, w, is_qk):
    c = (w[3:4] * x + w[2:3] * shift_down(x, 1) + w[1:2] * shift_down(x, 2) + w[0:1] * shift_down(x, 3))
    a = _silu(c)
    n = a * lax.rsqrt(jnp.sum(a * a, axis=-1, keepdims=True) + EPS)
    return jnp.where(is_qk, n, a)


def dnprep_fwd(proj, cw):
    S = proj.shape[0]
    nblk = B_QKV // LANE

    def body(x_ref, w_ref, o_ref):
        o_ref[...] = _dnprep_f(x_ref[...], w_ref[...], pl.program_id(0) < 2 * B_QK // LANE)

    return pl.pallas_call(
        body, name="dnprep_fwd", grid=(nblk,),
        in_specs=[pl.BlockSpec((S, LANE), lambda j: (0, j)), pl.BlockSpec((4, LANE), lambda j: (0, j))],
        out_specs=pl.BlockSpec((S, LANE), lambda j: (0, j)),
        out_shape=jax.ShapeDtypeStruct((S, B_QKV), f32), compiler_params=_cp(("parallel",)),
    )(proj, cw)


def dnprep_bwd(proj, cw, dqkvn):
    S = proj.shape[0]
    nblk = B_QKV // LANE

    def body(x_ref, w_ref, d_ref, dx_ref, dw_ref):
        is_qk = pl.program_id(0) < 2 * B_QK // LANE
        _, vjp = jax.vjp(lambda a, b: _dnprep_f(a, b, is_qk), x_ref[...], w_ref[...])
        dx, dw = vjp(d_ref[...])
        dx_ref[...] = dx
        dw_ref[...] = dw

    col = pl.BlockSpec((S, LANE), lambda j: (0, j))
    wsp = pl.BlockSpec((4, LANE), lambda j: (0, j))
    return pl.pallas_call(
        body, name="dnprep_bwd", grid=(nblk,), in_specs=[col, wsp, col], out_specs=[col, wsp],
        out_shape=[jax.ShapeDtypeStruct((S, B_QKV), f32), jax.ShapeDtypeStruct((4, B_QKV), f32)],
        compiler_params=_cp(("parallel",)),
    )(proj, cw, dqkvn)


def _hdot(a, b, ca=1, cb=0):
    return _dg(a, b, ca, cb, HI)


def _bdg(a, b, ca, cb):
    dn = (((ca,), (cb,)), ((0,), (0,)))
    ah, bh = a.astype(bf16), b.astype(bf16)
    al, bl = (a - ah.astype(f32)).astype(bf16), (b - bh.astype(f32)).astype(bf16)
    return (lax.dot_general(ah, bh, dn, preferred_element_type=f32)
            + lax.dot_general(ah, bl, dn, preferred_element_type=f32)
            + lax.dot_general(al, bh, dn, preferred_element_type=f32))


@jax.custom_vjp
def hbd(a, b):
    return _bdg(a, b, 2, 1)


@jax.custom_vjp
def hbd_nt(a, b):
    return _bdg(a, b, 2, 2)


@jax.custom_vjp
def hbd_tn(a, b):
    return _bdg(a, b, 1, 1)


hbd.defvjp(lambda a, b: (hbd(a, b), (a, b)), lambda r, g: (hbd_nt(g, r[1]), hbd_tn(r[0], g)))
hbd_nt.defvjp(lambda a, b: (hbd_nt(a, b), (a, b)), lambda r, g: (hbd(g, r[1]), hbd_tn(g, r[0])))
hbd_tn.defvjp(lambda a, b: (hbd_tn(a, b), (a, b)), lambda r, g: (hbd_nt(r[1], g), hbd(r[0], g)))


def _stack(xs):
    return jnp.concatenate([x[None] for x in xs], axis=0)


def _lane_col(x, j):
    lane = lax.broadcasted_iota(jnp.int32, (1, LANE), 1)
    return jnp.sum(jnp.where(lane == j, x, 0.0), axis=-1, keepdims=True)


def _tri_inv(a_mat):
    r = lax.broadcasted_iota(jnp.int32, (1, CHUNK, CHUNK), 1)
    c = lax.broadcasted_iota(jnp.int32, (1, CHUNK, CHUNK), 2)
    pw = -a_mat
    inv = (r == c).astype(f32) + pw
    for _ in range(5):
        pw = hbd(pw, pw)
        inv = inv + hbd(inv, pw)
    return inv


@jax.custom_vjp
def _tri_inv_known(a_mat, inv):
    return inv


_tri_inv_known.defvjp(lambda a, inv: (inv, inv),
                      lambda inv, g: (-hbd_tn(inv, hbd_nt(g, inv)), jnp.zeros_like(inv)))


def _dnc_f(q, k, v, seg, prm, inverse=_tri_inv):
    C = CHUNK
    beta_all = _sigmoid(seg)
    xx = seg + prm[1:2]
    g_all = -jnp.exp(prm[0:1]) * (jnp.maximum(xx, 0.0) + jnp.log(1.0 + jnp.exp(-jnp.abs(xx))))
    r2 = lax.broadcasted_iota(jnp.int32, (C, C), 0)
    c2 = lax.broadcasted_iota(jnp.int32, (C, C), 1)
    gc_all = _hdot((r2 >= c2).astype(f32), g_all)
    beta = _stack([_lane_col(beta_all, h) for h in range(6)])
    gc = _stack([_lane_col(gc_all, 6 + h) for h in range(6)])
    r = lax.broadcasted_iota(jnp.int32, (1, C, C), 1)
    c = lax.broadcasted_iota(jnp.int32, (1, C, C), 2)
    incl = r >= c
    strict = r > c
    eye = (r == c).astype(f32)
    g_row = hbd(jnp.ones((6, C, C), f32), eye * gc)
    decay = jnp.where(incl, jnp.exp(jnp.where(incl, gc - g_row, 0.0)), 0.0)
    a_mat = beta * hbd_nt(k, k) * jnp.where(strict, decay, 0.0)
    eg = jnp.exp(gc)
    inv = inverse(a_mat)
    u = hbd(inv, beta * v)
    w = hbd(inv, (beta * eg) * k)
    qc = q * (B_DH ** -0.5)
    attn = hbd_nt(qc, k) * decay
    last = (lax.broadcasted_iota(jnp.int32, (1, C, 1), 1) == C - 1).astype(f32)
    g_last = jnp.sum(gc * last, axis=1, keepdims=True)
    dc = jnp.broadcast_to(jnp.exp(g_last), (6, 1, LANE)).reshape(6, LANE)
    return u, w, qc * eg, k * jnp.exp(g_last - gc), attn, dc, inv


def _dns_f(S0, u, w, qd, kt, attn, dcrows):
    dc = _lane_col(dcrows, 0).reshape(6, 1, 1)
    delta = u - hbd(w, S0)
    out = hbd(qd, S0) + hbd(attn, delta)
    return out, dc * S0 + hbd_tn(kt, delta)


def _dnpost_f(o, z, grow):
    outs = []
    for h in range(6):
        oh = o[:, LANE * h:LANE * (h + 1)]
        outs.append(oh * lax.rsqrt(jnp.mean(oh * oh, axis=-1, keepdims=True) + EPS) * grow
                    * _silu(z[:, LANE * h:LANE * (h + 1)]))
    return jnp.concatenate(outs, axis=1)


def _hs(h):
    return slice(LANE * h, LANE * (h + 1))


def _heads(ref, share):
    return _stack([ref[:, _hs(h // share)] for h in range(6)])


def _put_heads(ref, val):
    for h in range(6):
        ref[:, _hs(h)] = val[h]


def _dnc_in_specs():
    return [
        pl.BlockSpec((CHUNK, B_QK), lambda n: (n, 0)),
        pl.BlockSpec((CHUNK, B_QK), lambda n: (n, 1)),
        pl.BlockSpec((CHUNK, B_V), lambda n: (n, 1)),
        pl.BlockSpec((CHUNK, LANE), lambda n: (n, 20)),
        pl.BlockSpec((8, LANE), lambda n: (0, 0)),
    ]


def _dnc_out_specs(rev_nc=None):
    ci = (lambda n: n) if rev_nc is None else (lambda n: rev_nc - 1 - n)
    wide = pl.BlockSpec((CHUNK, B_V), lambda n: (ci(n), 0))
    return [wide, wide, wide, wide, pl.BlockSpec((1, 6, CHUNK, CHUNK), lambda n: (ci(n), 0, 0, 0)),
            pl.BlockSpec((1, 8, LANE), lambda n: (ci(n), 0, 0))]


def _dnc_shapes(S):
    nc = S // CHUNK
    wide = jax.ShapeDtypeStruct((S, B_V), f32)
    return [wide, wide, wide, wide, jax.ShapeDtypeStruct((nc, 6, CHUNK, CHUNK), f32),
            jax.ShapeDtypeStruct((nc, 8, LANE), f32)]


def dnc_fwd(qkvn, proj, prm):
    S = proj.shape[0]

    def body(q_ref, k_ref, v_ref, s_ref, p_ref, u_ref, w_ref, qd_ref, kt_ref, at_ref, dc_ref, inv_ref):
        u, w, qd, kt, attn, dc, inv = _dnc_f(_heads(q_ref, 2), _heads(k_ref, 2), _heads(v_ref, 1), s_ref[...],
                                             p_ref[...])
        inv_ref[0] = inv
        _put_heads(u_ref, u)
        _put_heads(w_ref, w)
        _put_heads(qd_ref, qd)
        _put_heads(kt_ref, kt)
        at_ref[0] = attn
        dc_ref[0] = jnp.concatenate([dc, jnp.zeros((2, LANE), f32)], axis=0)

    out = pl.pallas_call(
        body, name="dn_chunk_fwd", grid=(S // CHUNK,), in_specs=_dnc_in_specs(),
        out_specs=_dnc_out_specs() + [_dnc_out_specs()[4]], out_shape=_dnc_shapes(S) + [_dnc_shapes(S)[4]],
        compiler_params=_cp(("parallel",)),
    )(qkvn, qkvn, qkvn, proj, prm)
    return out[:6], out[6]


def dnc_bwd(qkvn, proj, prm, inv, cots):
    S = proj.shape[0]

    def body(q_ref, k_ref, v_ref, s_ref, p_ref, inv_ref, du_ref, dw_ref, dqd_ref, dkt_ref, dat_ref, ddc_ref,
             dx_ref, dseg_ref, dprm_ref):
        @pl.when(pl.program_id(0) == 0)
        def _():
            dprm_ref[...] = jnp.zeros_like(dprm_ref)
        known = functools.partial(_tri_inv_known, inv=inv_ref[0])
        _, vjp = jax.vjp(lambda *a: _dnc_f(*a, inverse=known)[:6], _heads(q_ref, 2), _heads(k_ref, 2),
                         _heads(v_ref, 1), s_ref[...], p_ref[...])
        dq, dk, dv, dseg, dprm = vjp((_heads(du_ref, 1), _heads(dw_ref, 1), _heads(dqd_ref, 1), _heads(dkt_ref, 1),
                                      dat_ref[0], ddc_ref[0, 0:6, :]))
        dx_ref[...] = jnp.concatenate([dq[0] + dq[1], dq[2] + dq[3], dq[4] + dq[5],
                                       dk[0] + dk[1], dk[2] + dk[3], dk[4] + dk[5]] + [dv[h] for h in range(6)], axis=1)
        dseg_ref[...] = dseg
        dprm_ref[...] += dprm

    return pl.pallas_call(
        body, name="dn_chunk_bwd", grid=(S // CHUNK,),
        in_specs=_dnc_in_specs() + [_dnc_out_specs()[4]] + _dnc_out_specs(),
        out_specs=[pl.BlockSpec((CHUNK, B_QKV), lambda n: (n, 0)), pl.BlockSpec((CHUNK, LANE), lambda n: (n, 0)),
                   pl.BlockSpec((8, LANE), lambda n: (0, 0))],
        out_shape=[jax.ShapeDtypeStruct((S, B_QKV), f32), jax.ShapeDtypeStruct((S, LANE), f32),
                   jax.ShapeDtypeStruct((8, LANE), f32)],
        compiler_params=_cp(("arbitrary",)),
    )(qkvn, qkvn, qkvn, proj, prm, inv, *cots)


def dns_fwd(chunked):
    u = chunked[0]
    S = u.shape[0]
    nc = S // CHUNK

    def body(u_ref, w_ref, qd_ref, kt_ref, at_ref, dc_ref, o_ref, st_ref, st):
        @pl.when(pl.program_id(0) == 0)
        def _():
            st[...] = jnp.zeros_like(st)
        S0 = st[...]
        st_ref[0] = S0
        out, S1 = _dns_f(S0, _heads(u_ref, 1), _heads(w_ref, 1), _heads(qd_ref, 1), _heads(kt_ref, 1),
                         at_ref[0], dc_ref[0, 0:6, :])
        _put_heads(o_ref, out)
        st[...] = S1

    return pl.pallas_call(
        body, name="dn_scan_fwd", grid=(nc,), in_specs=_dnc_out_specs(),
        out_specs=[pl.BlockSpec((CHUNK, B_V), lambda n: (n, 0)),
                   pl.BlockSpec((1, 6, B_DH, B_DH), lambda n: (n, 0, 0, 0))],
        out_shape=[jax.ShapeDtypeStruct((S, B_V), f32), jax.ShapeDtypeStruct((nc, 6, B_DH, B_DH), f32)],
        scratch_shapes=[pltpu.VMEM((6, B_DH, B_DH), f32)],
        compiler_params=_cp(("arbitrary",)),
    )(*chunked)


def dns_bwd(chunked, states, do):
    S = do.shape[0]
    nc = S // CHUNK

    def body(u_ref, w_ref, qd_ref, kt_ref, at_ref, dc_ref, st_ref, do_ref,
             du_ref, dw_ref, dqd_ref, dkt_ref, dat_ref, ddc_ref, dst):
        @pl.when(pl.program_id(0) == 0)
        def _():
            dst[...] = jnp.zeros_like(dst)
        _, vjp = jax.vjp(_dns_f, st_ref[0], _heads(u_ref, 1), _heads(w_ref, 1), _heads(qd_ref, 1), _heads(kt_ref, 1),
                         at_ref[0], dc_ref[0, 0:6, :])
        dS0, du, dw, dqd, dkt, dat, ddc = vjp((_heads(do_ref, 1), dst[...]))
        dst[...] = dS0
        _put_heads(du_ref, du)
        _put_heads(dw_ref, dw)
        _put_heads(dqd_ref, dqd)
        _put_heads(dkt_ref, dkt)
        dat_ref[0] = dat
        ddc_ref[0] = jnp.concatenate([ddc, jnp.zeros((2, LANE), f32)], axis=0)

    return pl.pallas_call(
        body, name="dn_scan_bwd", grid=(nc,),
        in_specs=_dnc_out_specs(nc) + [pl.BlockSpec((1, 6, B_DH, B_DH), lambda n: (nc - 1 - n, 0, 0, 0)),
                                       pl.BlockSpec((CHUNK, B_V), lambda n: (nc - 1 - n, 0))],
        out_specs=_dnc_out_specs(nc), out_shape=_dnc_shapes(S),
        scratch_shapes=[pltpu.VMEM((6, B_DH, B_DH), f32)],
        compiler_params=_cp(("arbitrary",)),
    )(*chunked, states, do)


def dnpost_fwd(o, proj, prm):
    S = o.shape[0]
    t = min(512, S)

    def body(o_ref, z_ref, p_ref, y_ref):
        y_ref[...] = _dnpost_f(o_ref[...], z_ref[...], p_ref[2:3, :])

    tok = pl.BlockSpec((t, B_V), lambda i: (i, 0))
    return pl.pallas_call(
        body, name="dn_post_fwd", grid=(S // t,),
        in_specs=[tok, pl.BlockSpec((t, B_V), lambda i: (i, 2)), pl.BlockSpec((8, LANE), lambda i: (0, 0))],
        out_specs=tok, out_shape=jax.ShapeDtypeStruct((S, B_V), f32), compiler_params=_cp(("parallel",)),
    )(o, proj, prm)


def dnpost_bwd(o, proj, prm, dmix):
    S = o.shape[0]
    t = min(512, S)

    def body(o_ref, z_ref, p_ref, dy_ref, do_ref, dz_ref, dg_ref):
        @pl.when(pl.program_id(0) == 0)
        def _():
            dg_ref[...] = jnp.zeros_like(dg_ref)
        _, vjp = jax.vjp(_dnpost_f, o_ref[...], z_ref[...], p_ref[2:3, :])
        do, dz, dg = vjp(dy_ref[...])
        do_ref[...] = do
        dz_ref[...] = dz
        dg_ref[...] += dg

    tok = pl.BlockSpec((t, B_V), lambda i: (i, 0))
    return pl.pallas_call(
        body, name="dn_post_bwd", grid=(S // t,),
        in_specs=[tok, pl.BlockSpec((t, B_V), lambda i: (i, 2)), pl.BlockSpec((8, LANE), lambda i: (0, 0)), tok],
        out_specs=[tok, tok, pl.BlockSpec((1, LANE), lambda i: (0, 0))],
        out_shape=[jax.ShapeDtypeStruct((S, B_V), f32), jax.ShapeDtypeStruct((S, B_V), f32),
                   jax.ShapeDtypeStruct((1, LANE), f32)],
        compiler_params=_cp(("arbitrary",)),
    )(o, proj, prm, dmix)


N_FF_BLK = D_FF // LANE
GU_SHARD = 2 * D_FF // 4


GLU_ROWS = 256
HALO = 8


def _glu_f(gext, up, w, b):
    c = w[2:3] * gext + w[1:2] * shift_down(gext, 1) + w[0:1] * shift_down(gext, 2) + b
    return _silu(c)[HALO:] * up


def _glu_gext(g_ref, r0, first):
    if first:
        return jnp.concatenate([jnp.zeros((HALO, LANE), f32), g_ref[0:GLU_ROWS, :]], axis=0)
    return g_ref[pl.ds(r0 - HALO, GLU_ROWS + HALO), :]


def glu_fwd(gu, w, b, name):
    S = gu.shape[0]
    T = GLU_ROWS

    def body(g_ref, u_ref, w_ref, b_ref, o_ref):
        wv, bv = w_ref[...], b_ref[...]

        def tile(r0, first):
            act = _glu_f(_glu_gext(g_ref, r0, first), u_ref[pl.ds(r0, T), :], wv, bv)
            o_ref[pl.ds(r0, T), :] = act.astype(bf16)

        tile(0, True)

        @pl.loop(1, S // T)
        def _(t):
            tile(pl.multiple_of(t * T, T), False)

    col = pl.BlockSpec((S, LANE), lambda j: (0, j))
    return pl.pallas_call(
        body, name=name, grid=(N_FF_BLK,),
        in_specs=[col, pl.BlockSpec((S, LANE), lambda j: (0, N_FF_BLK + j)), pl.BlockSpec((3, LANE), lambda j: (0, j)),
                  pl.BlockSpec((1, LANE), lambda j: (0, j))],
        out_specs=col, out_shape=jax.ShapeDtypeStruct((S, D_FF), bf16), compiler_params=_cp(("parallel",)),
    )(gu, gu, w, b.reshape(1, D_FF))


def glu_bwd(gu, w, b, dact, name):
    S = gu.shape[0]
    T = GLU_ROWS

    def body(g_ref, u_ref, w_ref, b_ref, d_ref, dg_ref, dw_ref, db_ref, acc):
        wv, bv = w_ref[...], b_ref[...]

        def tile(r0, first):
            _, vjp = jax.vjp(_glu_f, _glu_gext(g_ref, r0, first), u_ref[pl.ds(r0, T), :], wv, bv)
            dgx, du, dw, db = vjp(d_ref[pl.ds(r0, T), :])
            acc[pl.ds(r0, T), :] = dgx[HALO:]
            if not first:
                acc[pl.ds(r0 - HALO, HALO), :] += dgx[:HALO]
            dg_ref[1, pl.ds(r0, T), :] = du.astype(bf16)
            return dw, db

        dw0, db0 = tile(0, True)
        dw_ref[...] = dw0
        db_ref[...] = db0

        @pl.loop(1, S // T)
        def _(t):
            dw, db = tile(pl.multiple_of(t * T, T), False)
            dw_ref[...] += dw
            db_ref[...] += db

        dg_ref[0] = acc[...].astype(bf16)

    col = pl.BlockSpec((S, LANE), lambda j: (0, j))
    wsp = pl.BlockSpec((3, LANE), lambda j: (0, j))
    bsp = pl.BlockSpec((1, LANE), lambda j: (0, j))
    return pl.pallas_call(
        body, name=name, grid=(N_FF_BLK,),
        in_specs=[col, pl.BlockSpec((S, LANE), lambda j: (0, N_FF_BLK + j)), wsp, bsp, col],
        out_specs=[pl.BlockSpec((2, S, LANE), lambda j: (0, 0, j)), wsp, bsp],
        out_shape=[jax.ShapeDtypeStruct((2, S, D_FF), bf16), jax.ShapeDtypeStruct((3, D_FF), f32),
                   jax.ShapeDtypeStruct((1, D_FF), f32)],
        scratch_shapes=[pltpu.VMEM((S, LANE), f32)],
        compiler_params=_cp(("parallel",)),
    )(gu, gu, w, b.reshape(1, D_FF), dact)


def gu_fwd(n2, wg, name):
    S = n2.shape[0]
    tm = min(512, S)

    def body(a_ref, w_ref, o_ref):
        o_ref[...] = _dg(a_ref[...], w_ref[...], 1, 0)

    return pl.pallas_call(
        body, name=name, grid=(4, S // tm),
        in_specs=[pl.BlockSpec((tm, D), lambda s, m: (m, 0)), pl.BlockSpec((None, D, GU_SHARD), lambda s, m: (s, 0, 0))],
        out_specs=pl.BlockSpec((tm, GU_SHARD), lambda s, m: (m, s)),
        out_shape=jax.ShapeDtypeStruct((S, 2 * D_FF), f32), compiler_params=_cp(("parallel", "parallel")),
    )(n2, wg)


def gu_bwd_x(dgu, wg, name):
    S = dgu.shape[1]
    tm = min(512, S)

    def body(d_ref, w_ref, o_ref):
        @pl.when(pl.program_id(1) == 0)
        def _():
            o_ref[...] = jnp.zeros_like(o_ref)
        o_ref[...] += _dg(d_ref[...], w_ref[...], 1, 1)

    return pl.pallas_call(
        body, name=name, grid=(S // tm, 4),
        in_specs=[pl.BlockSpec((None, tm, GU_SHARD), lambda m, s: (s // 2, m, s % 2)),
                  pl.BlockSpec((None, D, GU_SHARD), lambda m, s: (s, 0, 0))],
        out_specs=pl.BlockSpec((tm, D), lambda m, s: (m, 0)),
        out_shape=jax.ShapeDtypeStruct((S, D), f32), compiler_params=_cp(("parallel", "arbitrary")),
    )(dgu, wg)


def gu_bwd_w(n2, dgu, name):
    S = n2.shape[0]
    tm = min(512, S)
    nm = S // tm

    def body(a_ref, d_ref, o_ref, acc):
        @pl.when(pl.program_id(1) == 0)
        def _():
            acc[...] = jnp.zeros_like(acc)
        acc[...] += _dg(a_ref[...], d_ref[...], 0, 0)

        @pl.when(pl.program_id(1) == nm - 1)
        def _():
            o_ref[...] = acc[...].astype(bf16)

    return pl.pallas_call(
        body, name=name, grid=(4, nm),
        in_specs=[pl.BlockSpec((tm, D), lambda s, m: (m, 0)),
                  pl.BlockSpec((None, tm, GU_SHARD), lambda s, m: (s // 2, m, s % 2))],
        out_specs=pl.BlockSpec((None, D, GU_SHARD), lambda s, m: (s, 0, 0)),
        out_shape=jax.ShapeDtypeStruct((4, D, GU_SHARD), bf16),
        scratch_shapes=[pltpu.VMEM((D, GU_SHARD), f32)],
        compiler_params=_cp(("parallel", "arbitrary")),
    )(n2, dgu)


def _pair_cols(w):
    lead = w.shape[:-1]
    return w.reshape(lead + (2, 6, A_DH)).swapaxes(-3, -2).reshape(lead + (A_Q,))


def _unpair_cols(w):
    lead = w.shape[:-1]
    return w.reshape(lead + (6, 2, A_DH)).swapaxes(-3, -2).reshape(lead + (A_Q,))


def _lay_in_a(w):
    return jnp.concatenate([_pair_cols(w[:, :A_Q]), w[:, A_Q:]], axis=1)


def _unlay_in_a(w):
    return jnp.concatenate([_unpair_cols(w[:, :A_Q]), w[:, A_Q:]], axis=1)


def _lay_out_a(w):
    return jnp.concatenate([_pair_cols(w[:A_Q].T).T, w[A_Q:]], axis=0)


def _unlay_out_a(w):
    return jnp.concatenate([_unpair_cols(w[:A_Q].T).T, w[A_Q:]], axis=0)


def _lay_in_b(w):
    return jnp.concatenate([w[:, :2304], w[:, 2316:], w[:, 2304:2316],
                            jnp.zeros((w.shape[0], LANE - 12), w.dtype)], axis=1)


def _unlay_in_b(w):
    return jnp.concatenate([w[:, :2304], w[:, 2560:2572], w[:, 2304:2560]], axis=1)


def _chip_cols(w):
    return jnp.moveaxis(w.reshape(w.shape[0], 4, w.shape[1] // 4), 1, 0)


def _unchip_cols(w):
    return jnp.moveaxis(w, 0, 1).reshape(w.shape[1], 4 * w.shape[2])


def _local_step(x, mem, target, P):
    arrive = P.get("arrive", lambda key, after: None)
    ready = P.get("ready", lambda key, grads, dep: dep)
    sk = jnp.zeros((16, LANE), f32).at[:A_HEADS].set(jnp.broadcast_to(P["sinks"][:, None], (A_HEADS, LANE)))
    prm = jnp.zeros((8, LANE), f32).at[0, 6:12].set(P["a_log"]).at[1, 6:12].set(P["dt_bias"]).at[2].set(P["out_norm_g"])
    bias = bias_build(P["rel_bias"])
    saved = []
    h = x
    for i in range(2):
        n1 = rms_fwd(h, P["g_mix"][i], f"rms_mix{i}")
        arrive(("w_in", i), n1)
        proj = mm_nn(n1, P["w_in_a"] if i == 0 else P["w_in_b"], name="proj_a" if i == 0 else "proj_b")
        arrive(("w_mem", i), proj)
        kv = memkv_fwd(mem, P["g_mem"][i], P["w_mem"][i], f"memkv{i}")
        if i == 0:
            self_out = swa_fwd(proj, bias, sk)
            cross = xattn_fwd(proj, A_Q + 2 * LANE, kv, "xattn_a")
            extra = ()
        else:
            qkvn = dnprep_fwd(proj, P["conv_qkv"])
            chunked, inv = dnc_fwd(qkvn, proj, prm)
            o, states = dns_fwd(chunked)
            self_out = dnpost_fwd(o, proj, prm)
            cross = xattn_fwd(proj, 2304, kv, "xattn_b")
            extra = (qkvn, chunked, inv, states, o)
        mix = jnp.concatenate([self_out, cross], axis=1)
        arrive(("w_out", i), cross)
        h2 = mm_nn(mix, P["w_out"][i], res=h, name=f"out_proj{i}")
        n2 = rms_fwd(h2, P["g_ffn"][i], f"rms_ffn{i}")
        arrive(("w_gu", i), n2)
        gu = gu_fwd(n2, P["w_gu"][i], f"gate_up{i}")
        act = glu_fwd(gu, P["ffn_cw"][i], P["ffn_cb"][i], f"glu{i}")
        arrive(("w_down", i), act)
        h3 = mm_nn(act, P["w_down"][i], res=h2, name=f"down{i}")
        saved.append((h, n1, kv, proj, mix, h2, n2, gu, act, extra))
        h = h3

    loss, dh, dg_fin = loss_head(h, P["g_fin"], target)
    G = {"g_fin": dg_fin[0], "g_mix": [None, None], "g_mem": [None, None], "g_ffn": [None, None],
         "w_mem": [None, None], "w_out": [None, None], "w_gu": [None, None], "w_down": [None, None],
         "ffn_cw": [None, None], "ffn_cb": [None, None]}
    for i in (1, 0):
        hin, n1, kv, proj, mix, h2, n2, gu, act, extra = saved[i]
        dact = mm_nt(dh, P["w_down"][i], name=f"d_act{i}")
        G["w_down"][i] = mm_tn(act, dh, name=f"dw_down{i}")
        dgu, dcw, dcb = glu_bwd(gu, P["ffn_cw"][i], P["ffn_cb"][i], dact, f"glu_bwd{i}")
        G["ffn_cw"][i], G["ffn_cb"][i] = dcw, dcb[0]
        dn2 = gu_bwd_x(dgu, P["w_gu"][i], f"d_n2_{i}")
        G["w_gu"][i] = gu_bwd_w(n2, dgu, f"dw_gu{i}")
        g_ffn = ready(("ffn", i), G, P["g_ffn"][i])
        dh2, dg = rms_bwd(h2, g_ffn, dn2, dh, f"rms_ffn_bwd{i}")
        G["g_ffn"][i] = dg[0]
        dmix = mm_nt(dh2, P["w_out"][i], name=f"d_mix{i}")
        G["w_out"][i] = mm_tn(mix, dh2, name=f"dw_out{i}")
        if i == 0:
            dqkv, dbias, dsk = swa_bwd(proj, bias, sk, dmix)
            dxq, dkv = xattn_bwd(proj, A_Q + 2 * LANE, kv, dmix, "xattn_a_bwd")
            dproj = jnp.concatenate([dqkv, dxq], axis=1)
            G["sinks"] = dsk[:A_HEADS, 0]
            G["rel_bias"] = bias_grad(dbias)[:, :A_HEADS]
            w_in, gname = P["w_in_a"], "w_in_a"
        else:
            qkvn, chunked, inv, states, o = extra
            do, dz, dgo = dnpost_bwd(o, proj, prm, dmix)
            dqkvn, dseg, dprm = dnc_bwd(qkvn, proj, prm, inv, dns_bwd(chunked, states, do))
            draw, dconv = dnprep_bwd(proj, P["conv_qkv"], dqkvn)
            dxq, dkv = xattn_bwd(proj, 2304, kv, dmix, "xattn_b_bwd")
            dproj = jnp.concatenate([draw, dz, dxq, dseg], axis=1)
            G["conv_qkv"] = dconv
            G["a_log"], G["dt_bias"], G["out_norm_g"] = dprm[0, 6:12], dprm[1, 6:12], dgo[0]
            w_in, gname = P["w_in_b"], "w_in_b"
        dn1 = mm_nt(dproj, w_in, name=f"d_n1_{i}")
        G[gname] = mm_tn(n1, dproj, name=f"d{gname}")
        dh, dg = rms_bwd(hin, P["g_mix"][i], dn1, dh2, f"rms_mix_bwd{i}")
        G["g_mix"][i] = dg[0]
        dgm, dwm = memkv_bwd(mem, P["g_mem"][i], P["w_mem"][i], dkv, f"memkv_bwd{i}")
        G["g_mem"][i], G["w_mem"][i] = dgm[0], dwm
        ready(("mix", i), G, None)
    return loss, dh, G


def _prepare(full, w_gu=None):
    return {
        "rel_bias": full["rel_bias"], "sinks": full["sinks_a"][0], "a_log": full["a_log_b"][0],
        "dt_bias": full["dt_bias_b"][0], "out_norm_g": full["out_norm_g_b"][0],
        "g_mix": full["norm_mix_g"], "g_mem": full["norm_mem_g"], "g_ffn": full["norm_ffn_g"],
        "g_fin": full["final_norm_g"], "conv_qkv": full["conv_qkv_b"][0],
        "ffn_cw": [full["ffn_conv_w"][0], full["ffn_conv_w"][1]],
        "ffn_cb": [full["ffn_conv_b"][0], full["ffn_conv_b"][1]],
        "w_mem": [full["w_mem_kv"][0], full["w_mem_kv"][1]],
        "w_out": [_lay_out_a(full["w_out"][0]), full["w_out"][1]],
        "w_in_a": _lay_in_a(full["w_in_a"][0]), "w_in_b": _lay_in_b(full["w_in_b"][0]),
        "w_gu": w_gu if w_gu is not None else [_chip_cols(full["w_gate_up"][0]), _chip_cols(full["w_gate_up"][1])],
        "w_down": [full["w_down"][0], full["w_down"][1]],
    }


def _grads_to_ref(G):
    return {
        "rel_bias": G["rel_bias"], "norm_mix_g": jnp.stack(G["g_mix"]), "norm_mem_g": jnp.stack(G["g_mem"]),
        "w_mem_kv": jnp.stack(G["w_mem"]),
        "w_out": jnp.stack([_unlay_out_a(G["w_out"][0]), G["w_out"][1]]),
        "w_in_a": _unlay_in_a(G["w_in_a"])[None], "sinks_a": G["sinks"][None],
        "w_in_b": _unlay_in_b(G["w_in_b"])[None], "conv_qkv_b": G["conv_qkv"][None],
        "a_log_b": G["a_log"][None], "dt_bias_b": G["dt_bias"][None], "out_norm_g_b": G["out_norm_g"][None],
        "norm_ffn_g": jnp.stack(G["g_ffn"]),
        "w_gate_up": jnp.stack([_unchip_cols(G["w_gu"][0]), _unchip_cols(G["w_gu"][1])]).astype(f32),
        "ffn_conv_w": jnp.stack(G["ffn_cw"]), "ffn_conv_b": jnp.stack(G["ffn_cb"]),
        "w_down": jnp.stack(G["w_down"]), "final_norm_g": G["g_fin"],
    }


ANY = pl.BlockSpec(memory_space=pl.ANY)


def _place():
    return lax.axis_index("x"), lax.axis_index("y"), lax.axis_index("c")


def chip_scatter(gs):
    n = len(gs)

    def body(*refs):
        ins, outs = refs[:n], refs[n:2 * n]
        ssem, rsem = refs[2 * n:]
        x, y, c = _place()
        me = 2 * x + y
        peers = [(1 - x, y), (x, 1 - y), (1 - x, 1 - y)]

        def remote(j, k, slot):
            px, py = peers[k]
            return pltpu.make_async_remote_copy(
                src_ref=ins[j].at[2 * px + py], dst_ref=outs[j].at[slot],
                send_sem=ssem.at[3 * j + k], recv_sem=rsem.at[3 * j + k],
                device_id=(px, py, c), device_id_type=MESH)

        sends = [remote(j, k, me) for j in range(n) for k in range(3)]
        for cp in sends:
            cp.start()
        for j in range(n):
            for k in range(3):
                px, py = peers[k]
                remote(j, k, 2 * px + py).wait_recv()
        for cp in sends:
            cp.wait_send()

    return pl.pallas_call(
        body, name="grad_scatter", in_specs=[ANY] * n, out_specs=[ANY] * n,
        out_shape=[jax.ShapeDtypeStruct(g.shape, g.dtype) for g in gs],
        scratch_shapes=[pltpu.SemaphoreType.DMA((3 * n,)), pltpu.SemaphoreType.DMA((3 * n,))],
    )(*gs)


def allreduce_small(buf):
    R = buf.shape[0]

    def body(b_ref, o_ref, recv, ssem, rsem):
        x, y, c = _place()
        me = 4 * x + 2 * y + c

        def peer(k):
            return (1 - x if k & 4 else x, 1 - y if k & 2 else y, 1 - c if k & 1 else c)

        def remote(k, slot):
            return pltpu.make_async_remote_copy(
                src_ref=b_ref, dst_ref=recv.at[slot], send_sem=ssem.at[k - 1], recv_sem=rsem.at[k - 1],
                device_id=peer(k), device_id_type=MESH)

        sends = [remote(k, me) for k in range(1, 8)]
        for cp in sends:
            cp.start()
        recv[me] = b_ref[...]
        for k in range(1, 8):
            px, py, pc = peer(k)
            remote(k, 4 * px + 2 * py + pc).wait_recv()
        for cp in sends:
            cp.wait_send()
        total = recv[0]
        for j in range(1, 8):
            total = total + recv[j]
        o_ref[...] = total

    return pl.pallas_call(
        body, name="small_allreduce",
        in_specs=[pl.BlockSpec(memory_space=pltpu.VMEM)], out_specs=pl.BlockSpec(memory_space=pltpu.VMEM),
        out_shape=jax.ShapeDtypeStruct(buf.shape, f32),
        scratch_shapes=[pltpu.VMEM((8, R, LANE), f32), pltpu.SemaphoreType.DMA((7,)), pltpu.SemaphoreType.DMA((7,))],
    )(buf)


def sum_slots(own, recv, chip, core, name):
    _, R, C = recv.shape
    tr = _row_tile(R, 256)
    nt = R // tr

    def body(p_ref, a_ref, r_ref, o_ref):
        acc = jnp.zeros((tr, C), f32)
        for s in range(4):
            acc = acc + jnp.where(p_ref[0] == s, a_ref[s], r_ref[s]).astype(f32)
        o_ref[...] = acc

    slots = pl.BlockSpec((4, tr, C), lambda i, p_ref: (0, i, 0))
    return pl.pallas_call(
        body, name=name, out_shape=jax.ShapeDtypeStruct((2 * R, C), f32),
        grid_spec=pltpu.PrefetchScalarGridSpec(
            num_scalar_prefetch=1, grid=(nt,), in_specs=[slots, slots],
            out_specs=pl.BlockSpec((tr, C), lambda i, p_ref: (p_ref[1] * nt + i, 0))),
        compiler_params=_cp(("parallel",)),
    )(jnp.stack([chip, core]).astype(jnp.int32), own, recv)


def _half(ref, core, axis=0):
    half = ref.shape[axis] // 2
    idx = (slice(None),) * axis + (pl.ds(core * half, half),)
    return ref.at[idx]


IN_HBM = pl.BlockSpec(memory_space=pltpu.HBM)
IN_SEM = pl.BlockSpec(memory_space=pltpu.SEMAPHORE)
SIDE_EFFECT = pltpu.SideEffectType.DATAFLOW_SIDE_EFFECTING


def _gather_copy(buf, i, k, ssem, rsem, place, landing):
    x, y, c = place
    px, py = [(1 - x, y), (x, 1 - y), (1 - x, 1 - y)][k]
    me = 2 * x + y
    return pltpu.make_async_remote_copy(
        src_ref=buf.at[me], dst_ref=buf.at[me if landing == "theirs" else 2 * px + py],
        send_sem=ssem.at[3 * i + k], recv_sem=rsem.at[3 * i + k], device_id=(px, py, c), device_id_type=MESH)


def gather_start(groups):
    flat = [b for grp in groups for b in grp]
    n, ng = len(flat), len(groups)

    def body(*refs):
        bufs, sems = refs[:n], refs[n:n + 2 * ng]
        place = _place()
        j = 0
        for g, grp in enumerate(groups):
            for i in range(len(grp)):
                for k in range(3):
                    _gather_copy(bufs[j], i, k, sems[2 * g], sems[2 * g + 1], place, "theirs").start()
                j += 1

    sem_shapes = [pltpu.SemaphoreType.DMA((3 * len(grp),)) for grp in groups for _ in range(2)]
    out = pl.pallas_call(
        body, name="gather_start", in_specs=[IN_HBM] * n, out_specs=(*[IN_SEM] * (2 * ng), *[IN_HBM] * n),
        out_shape=(*sem_shapes, *[pltpu.HBM(b.shape, b.dtype) for b in flat]),
        input_output_aliases={i: 2 * ng + i for i in range(n)},
        compiler_params=pltpu.CompilerParams(has_side_effects=SIDE_EFFECT),
    )(*[pltpu.with_memory_space_constraint(b, pltpu.HBM) for b in flat])
    sems, bufs = out[:2 * ng], list(out[2 * ng:])
    flights, j = [], 0
    for g, grp in enumerate(groups):
        flights.append((bufs[j:j + len(grp)], sems[2 * g], sems[2 * g + 1]))
        j += len(grp)
    return flights


def gather_wait(flight, after, name):
    bufs, ssem, rsem = flight
    n = len(bufs)

    def body(*refs):
        place = _place()
        for i in range(n):
            for k in range(3):
                cp = _gather_copy(refs[i], i, k, refs[n], refs[n + 1], place, "mine")
                cp.wait_send()
                cp.wait_recv()

    return pl.pallas_call(
        body, name=name, in_specs=[IN_HBM] * n + [IN_SEM, IN_SEM, ANY], out_specs=[IN_HBM] * n,
        out_shape=[pltpu.HBM(b.shape, b.dtype) for b in bufs], input_output_aliases={i: i for i in range(n)},
        compiler_params=pltpu.CompilerParams(has_side_effects=SIDE_EFFECT),
    )(*bufs, ssem, rsem, after)


def _scatter_copy(src, land, j, k, ssem, rsem, place, landing):
    x, y, c = place
    px, py = [(1 - x, y), (x, 1 - y), (1 - x, 1 - y)][k]
    return pltpu.make_async_remote_copy(
        src_ref=src.at[2 * px + py], dst_ref=land.at[2 * x + y if landing == "theirs" else 2 * px + py],
        send_sem=ssem.at[3 * j + k], recv_sem=rsem.at[3 * j + k], device_id=(px, py, c), device_id_type=MESH)


def scatter_start(srcs, name):
    n = len(srcs)
    lands = [lax.empty(g.shape, g.dtype) for g in srcs]

    def body(*refs):
        place = _place()
        for j in range(n):
            for k in range(3):
                _scatter_copy(refs[j], refs[n + j], j, k, refs[2 * n], refs[2 * n + 1], place, "theirs").start()
        refs[-1][...] = jnp.zeros_like(refs[-1])

    sem = pltpu.SemaphoreType.DMA((3 * n,))
    hbm = [pltpu.with_memory_space_constraint(b, pltpu.HBM) for b in list(srcs) + lands]
    out = pl.pallas_call(
        body, name=name, in_specs=[IN_HBM] * (2 * n),
        out_specs=(IN_SEM, IN_SEM, *[IN_HBM] * (2 * n), pl.BlockSpec(memory_space=pltpu.VMEM)),
        out_shape=(sem, sem, *[pltpu.HBM(b.shape, b.dtype) for b in hbm], jax.ShapeDtypeStruct((8, LANE), f32)),
        input_output_aliases={i: 2 + i for i in range(2 * n)},
        compiler_params=pltpu.CompilerParams(has_side_effects=SIDE_EFFECT),
    )(*hbm)
    return (list(out[2:2 + n]), list(out[2 + n:2 + 2 * n]), out[0], out[1]), out[-1]


def scatter_wait(flight, after, name):
    srcs, lands, ssem, rsem = flight
    n = len(srcs)

    def body(*refs):
        place = _place()
        for j in range(n):
            for k in range(3):
                cp = _scatter_copy(refs[j], refs[n + j], j, k, refs[2 * n], refs[2 * n + 1], place, "mine")
                cp.wait_send()
                cp.wait_recv()

    out = pl.pallas_call(
        body, name=name, in_specs=[IN_HBM] * (2 * n) + [IN_SEM, IN_SEM, ANY], out_specs=[IN_HBM] * (2 * n),
        out_shape=[pltpu.HBM(b.shape, b.dtype) for b in list(srcs) + list(lands)],
        input_output_aliases={i: i for i in range(2 * n)},
        compiler_params=pltpu.CompilerParams(has_side_effects=SIDE_EFFECT),
    )(*srcs, *lands, ssem, rsem, after)
    return list(out[:n]), list(out[n:])


def pair_exchange(gbufs, name):
    n = len(gbufs)

    def body(*refs):
        ins, outs = refs[:n], refs[n:2 * n]
        ssem, rsem = refs[2 * n:]
        x, y, c = _place()
        cps = [pltpu.make_async_remote_copy(
            src_ref=_half(ins[j], 1 - c, axis=1), dst_ref=outs[j], send_sem=ssem.at[j], recv_sem=rsem.at[j],
            device_id=(x, y, 1 - c), device_id_type=MESH) for j in range(n)]
        for cp in cps:
            cp.start()
        for cp in cps:
            cp.wait()

    return pl.pallas_call(
        body, name=name, in_specs=[ANY] * n, out_specs=[ANY] * n,
        out_shape=[jax.ShapeDtypeStruct((4, g.shape[1] // 2, g.shape[2]), g.dtype) for g in gbufs],
        scratch_shapes=[pltpu.SemaphoreType.DMA((n,)), pltpu.SemaphoreType.DMA((n,))],
    )(*gbufs)


def _row_tile(rows, cap=512):
    return max(t for t in range(16, min(rows, cap) + 1, 16) if rows % t == 0)


def pair_sum(mine, theirs, core, name):
    _, R, C = mine.shape
    half = R // 2
    tr = _row_tile(half)
    nt = half // tr

    def body(c_ref, a_ref, b_ref, o_ref):
        o_ref[...] = (a_ref[...].astype(f32) + b_ref[...].astype(f32)).astype(bf16)

    return pl.pallas_call(
        body, name=name, out_shape=jax.ShapeDtypeStruct(theirs.shape, bf16),
        grid_spec=pltpu.PrefetchScalarGridSpec(
            num_scalar_prefetch=1, grid=(4, nt),
            in_specs=[pl.BlockSpec((None, tr, C), lambda s, i, c_ref: (s, c_ref[0] * nt + i, 0)),
                      pl.BlockSpec((None, tr, C), lambda s, i, c_ref: (s, i, 0))],
            out_specs=pl.BlockSpec((None, tr, C), lambda s, i, c_ref: (s, i, 0))),
        compiler_params=_cp(("parallel", "parallel")),
    )(jnp.reshape(core, (1,)).astype(jnp.int32), mine, theirs)


def final_exchange(fins):
    n = len(fins)

    def body(*refs):
        outs = refs[n:2 * n]
        ssem, rsem = refs[2 * n:]
        x, y, c = _place()
        cps = [pltpu.make_async_remote_copy(
            src_ref=_half(outs[j], c), dst_ref=_half(outs[j], c), send_sem=ssem.at[j], recv_sem=rsem.at[j],
            device_id=(x, y, 1 - c), device_id_type=MESH) for j in range(n)]
        for cp in cps:
            cp.start()
        for cp in cps:
            cp.wait()

    return pl.pallas_call(
        body, name="final_exchange", in_specs=[ANY] * n, out_specs=[ANY] * n,
        out_shape=[jax.ShapeDtypeStruct(f.shape, f.dtype) for f in fins],
        input_output_aliases={j: j for j in range(n)},
        scratch_shapes=[pltpu.SemaphoreType.DMA((n,)), pltpu.SemaphoreType.DMA((n,))],
    )(*fins)


def adamw_big(w, m, v, gs, row0, name):
    L, R, C = w.shape
    tr = _row_tile(math.gcd(R, row0) if row0 else R, max(16, 262144 // C // 16 * 16))
    b0 = row0 // tr

    def body(*refs):
        w_ref, m_ref, v_ref = refs[:3]
        g_refs = refs[3:3 + L]
        g_ref, d_ref, nm_ref, nv_ref = refs[3 + L:]
        g = g_refs[0][...]
        for l in range(1, L):
            g = jnp.where(pl.program_id(0) == l, g_refs[l][...], g)
        d, nm, nv = _adamw_math(w_ref[...], g, m_ref[...], v_ref[...])
        g_ref[...] = g
        d_ref[...] = d
        nm_ref[...] = nm
        nv_ref[...] = nv

    own = pl.BlockSpec((None, tr, C), lambda l, i: (l, i, 0))
    off = pl.BlockSpec((tr, C), lambda l, i: (b0 + i, 0))
    return pl.pallas_call(
        body, name=name, grid=(L, R // tr), in_specs=[own, own, own] + [off] * L, out_specs=[own] * 4,
        out_shape=[jax.ShapeDtypeStruct((L, R, C), f32)] * 4, compiler_params=_cp(("parallel", "parallel")),
    )(w, m, v, *gs)


def _adamw_math(w, g, m, v):
    m = B1 * m + (1.0 - B1) * g
    v = B2 * v + (1.0 - B2) * (g * g)
    m_hat = m / (1.0 - B1 ** STEP)
    v_hat = v / (1.0 - B2 ** STEP)
    delta = -LR * (m_hat / (jnp.sqrt(v_hat) + AEPS) + WD * w)
    return delta, m, v


def adamw_small(w, m, v, g):
    def body(w_ref, m_ref, v_ref, g_ref, d_ref, nm_ref, nv_ref):
        d, nm, nv = _adamw_math(w_ref[...], g_ref[...], m_ref[...], v_ref[...])
        d_ref[...] = d
        nm_ref[...] = nm
        nv_ref[...] = nv

    return pl.pallas_call(body, name="adamw_small", out_shape=[jax.ShapeDtypeStruct(w.shape, f32)] * 3)(w, m, v, g)


CONV =(("conv_qkv_b", 2), ("ffn_conv_w", 2))
SMALL = ("rel_bias", "norm_mix_g", "norm_mem_g", "sinks_a", "a_log_b", "dt_bias_b", "out_norm_g_b", "norm_ffn_g",
         "ffn_conv_b", "final_norm_g")
WEIGHTS = ("rel_bias", "norm_mix_g", "norm_mem_g", "w_mem_kv", "w_out", "w_in_a", "sinks_a", "w_in_b", "conv_qkv_b",
           "a_log_b", "dt_bias_b", "out_norm_g_b", "norm_ffn_g", "w_gate_up", "ffn_conv_w", "ffn_conv_b", "w_down",
           "final_norm_g")
ARGS = ("x", "mem") + WEIGHTS + ("loss_target",) + tuple("m_" + n for n in WEIGHTS) + tuple("v_" + n for n in WEIGHTS)


def _rows(a, width):
    flat = a.reshape(-1)
    pad = (-flat.shape[0]) % (8 * width)
    if pad:
        flat = jnp.concatenate([flat, jnp.zeros((pad,), a.dtype)])
    return flat.reshape(-1, width)


def _nrows(shape, width):
    return _pad_to(-(-math.prod(shape) // width), 8)


def _pack(arrs, width, total_rows, dtype):
    parts = [_rows(a.astype(dtype), width) for a in arrs]
    used = sum(p.shape[0] for p in parts)
    if total_rows > used:
        parts.append(jnp.zeros((total_rows - used, width), dtype))
    return jnp.concatenate(parts, axis=0)


def _unpack(buf, shapes, width):
    out, r = [], 0
    for s in shapes:
        n = _nrows(s, width)
        out.append(buf[r:r + n].reshape(-1)[:math.prod(s)].reshape(s))
        r += n
    return out


def _pad_to(n, mult):
    return -(-n // mult) * mult


def kernel(x, mem, rel_bias, norm_mix_g, norm_mem_g, w_mem_kv, w_out, w_in_a, sinks_a, w_in_b, conv_qkv_b, a_log_b, dt_bias_b, out_norm_g_b, norm_ffn_g, w_gate_up, ffn_conv_w, ffn_conv_b, w_down, final_norm_g, loss_target, m_rel_bias, m_norm_mix_g, m_norm_mem_g, m_w_mem_kv, m_w_out, m_w_in_a, m_sinks_a, m_w_in_b, m_conv_qkv_b, m_a_log_b, m_dt_bias_b, m_out_norm_g_b, m_norm_ffn_g, m_w_gate_up, m_ffn_conv_w, m_ffn_conv_b, m_w_down, m_final_norm_g, v_rel_bias, v_norm_mix_g, v_norm_mem_g, v_w_mem_kv, v_w_out, v_w_in_a, v_sinks_a, v_w_in_b, v_conv_qkv_b, v_a_log_b, v_dt_bias_b, v_out_norm_g_b, v_norm_ffn_g, v_w_gate_up, v_ffn_conv_w, v_ffn_conv_b, v_w_down, v_final_norm_g):
    A = dict(zip(ARGS, (x, mem, rel_bias, norm_mix_g, norm_mem_g, w_mem_kv, w_out, w_in_a, sinks_a, w_in_b, conv_qkv_b, a_log_b, dt_bias_b, out_norm_g_b, norm_ffn_g, w_gate_up, ffn_conv_w, ffn_conv_b, w_down, final_norm_g, loss_target, m_rel_bias, m_norm_mix_g, m_norm_mem_g, m_w_mem_kv, m_w_out, m_w_in_a, m_sinks_a, m_w_in_b, m_conv_qkv_b, m_a_log_b, m_dt_bias_b, m_out_norm_g_b, m_norm_ffn_g, m_w_gate_up, m_ffn_conv_w, m_ffn_conv_b, m_w_down, m_final_norm_g, v_rel_bias, v_norm_mix_g, v_norm_mem_g, v_w_mem_kv, v_w_out, v_w_in_a, v_sinks_a, v_w_in_b, v_conv_qkv_b, v_a_log_b, v_dt_bias_b, v_out_norm_g_b, v_norm_ffn_g, v_w_gate_up, v_ffn_conv_w, v_ffn_conv_b, v_w_down, v_final_norm_g)))
    chip = 2 * lax.axis_index("x") + lax.axis_index("y")
    core = lax.axis_index("c")
    n_down, n_out, n_mem = w_down.shape[1], w_out.shape[1], w_mem_kv.shape[1]

    def own_slot(shard):
        return lax.dynamic_update_index_in_dim(lax.empty((4,) + shard.shape, shard.dtype), shard, chip, 0)

    def bslot(w):
        return own_slot(w.astype(bf16))

    groups = {
        ("w_in", 0): [bslot(w_in_a[0])],
        ("w_mem", 0): [bslot(w_mem_kv[0]), bslot(w_mem_kv[1]), own_slot(conv_qkv_b[0]),
                       own_slot(ffn_conv_w.reshape(6, -1))],
        ("w_out", 0): [bslot(w_out[0])], ("w_gu", 0): [bslot(w_gate_up[0])], ("w_down", 0): [bslot(w_down[0])],
        ("w_in", 1): [bslot(w_in_b[0])],
        ("w_out", 1): [bslot(w_out[1])], ("w_gu", 1): [bslot(w_gate_up[1])], ("w_down", 1): [bslot(w_down[1])],
    }
    flights = dict(zip(groups, gather_start(list(groups.values()))))
    P = {"rel_bias": rel_bias, "sinks": sinks_a[0], "a_log": a_log_b[0], "dt_bias": dt_bias_b[0],
         "out_norm_g": out_norm_g_b[0], "g_mix": norm_mix_g, "g_mem": norm_mem_g, "g_ffn": norm_ffn_g,
         "g_fin": final_norm_g, "ffn_cb": [ffn_conv_b[0], ffn_conv_b[1]], "w_mem": [None, None], "w_out": [None, None],
         "w_gu": [None, None], "w_down": [None, None], "ffn_cw": [None, None]}

    def rows4(g):
        return g.reshape(4 * g.shape[1], g.shape[2])

    def arrive(key, after):
        if key not in flights:
            return
        got = gather_wait(flights.pop(key), after, "gather_wait_%s%d" % key)
        name, i = key
        if name == "w_in":
            P["w_in_a" if i == 0 else "w_in_b"] = (_lay_in_a if i == 0 else _lay_in_b)(_unchip_cols(got[0]))
        elif name == "w_mem":
            P["w_mem"] = [rows4(got[0]), rows4(got[1])]
            P["conv_qkv"] = _unchip_cols(got[2])
            cw = _unchip_cols(got[3]).reshape(2, 3, D_FF)
            P["ffn_cw"] = [cw[0], cw[1]]
        elif name == "w_out":
            P["w_out"][i] = _lay_out_a(rows4(got[0])) if i == 0 else rows4(got[0])
        elif name == "w_gu":
            P["w_gu"][i] = got[0]
        else:
            P["w_down"][i] = rows4(got[0])

    def chip_rows(g):
        return g.reshape(4, g.shape[0] // 4, g.shape[-1])

    sent, started = {}, []

    def ready(key, G, dep):
        kind, i = key
        tag = "%s%d" % key
        if kind == "ffn":
            names, partial = ("gu", "down"), [G["w_gu"][i], chip_rows(G["w_down"][i]).astype(bf16)]
        else:
            g_out = _unlay_out_a(G["w_out"][0]) if i == 0 else G["w_out"][1]
            g_in = _unlay_in_a(G["w_in_a"]) if i == 0 else _unlay_in_b(G["w_in_b"])
            names = ("out", "in", "mem")
            partial = [chip_rows(g_out).astype(bf16), _chip_cols(g_in).astype(bf16), chip_rows(G["w_mem"][i]).astype(bf16)]
        theirs = pair_exchange(partial, "pair_exchange_" + tag)
        pair = [pair_sum(p, t, core, "pair_sum_%s%d" % (nm, i)) for p, t, nm in zip(partial, theirs, names)]
        if key == ("mix", 0):
            sent[key] = (names, pair, chip_scatter(pair))
            return dep
        flight, token = scatter_start(pair, "scatter_start_" + tag)
        sent[key] = (names, flight)
        started.append(token[0, 0])
        if dep is not None:
            while started:
                dep = dep + started.pop()
        return dep

    P["arrive"], P["ready"] = arrive, ready

    loss, dx, G = _local_step(x[0], mem[0], loss_target[0], P)
    gfull = _grads_to_ref(G)

    fin = {}
    for key in (("ffn", 1), ("mix", 1), ("ffn", 0), ("mix", 0)):
        if key == ("mix", 0):
            names, pair, arrived = sent[key]
        else:
            names, flight = sent[key]
            pair, arrived = scatter_wait(flight, dx, "scatter_wait_%s%d" % key)
        for nm, p, r in zip(names, pair, arrived):
            fin[nm, key[1]] = sum_slots(p, r, chip, core, "sum_slots_%s%d" % (nm, key[1]))
    order = list(fin)
    done = dict(zip(order, final_exchange([fin[k] for k in order])))

    sm_shapes = [A[n].shape for n in SMALL] + [gfull[n].shape for n, _ in CONV] + [(LANE,)]
    sm_rows = _pad_to(sum(_nrows(s, LANE) for s in sm_shapes), 8)
    sbuf = _pack([gfull[n] for n in SMALL] + [gfull[n] for n, _ in CONV] + [loss[0]], LANE, sm_rows, f32)
    tot = _unpack(allreduce_small(sbuf), sm_shapes, LANE)
    gsmall = dict(zip(SMALL, tot[:len(SMALL)]))
    for (n, axis), t in zip(CONV, tot[len(SMALL):len(SMALL) + len(CONV)]):
        sh = A[n].shape[axis]
        gsmall[n] = lax.dynamic_slice_in_dim(t, chip * sh, sh, axis)
    loss_out = tot[-1][0]

    out = {}
    plan = (("w_gate_up", [done["gu", 0], done["gu", 1]]), ("w_down", [done["down", 0], done["down", 1]]),
            ("w_out", [done["out", 0], done["out", 1]]), ("w_mem_kv", [done["mem", 0], done["mem", 1]]),
            ("w_in_a", [done["in", 0]]), ("w_in_b", [done["in", 1]]))
    for n, gs in plan:
        shape3 = (len(gs),) + gs[0].shape
        res = adamw_big(A[n].reshape(shape3), A["m_" + n].reshape(shape3), A["v_" + n].reshape(shape3), gs, 0,
                        "adamw_" + n)
        for key, r in zip(("grad_", "delta_", "new_m_", "new_v_"), res):
            out[key + n] = r.reshape(A[n].shape)
    names = SMALL + tuple(n for n, _ in CONV)
    shapes = [A[n].shape for n in names]
    rows = _pad_to(sum(_nrows(s, LANE) for s in shapes), 8)
    packs = [_pack([src[n] for n in names], LANE, rows, f32)
             for src in ({n: A[n] for n in names}, {n: A["m_" + n] for n in names}, {n: A["v_" + n] for n in names}, gsmall)]
    res = adamw_small(*packs)
    for key, r in zip(("delta_", "new_m_", "new_v_"), res):
        for n, a in zip(names, _unpack(r, shapes, LANE)):
            out[key + n] = a
    for n in names:
        out["grad_" + n] = gsmall[n]
    return (loss_out, dx[None], *[out["grad_" + n] for n in WEIGHTS], *[out["delta_" + n] for n in WEIGHTS],
            *[out["new_m_" + n] for n in WEIGHTS], *[out["new_v_" + n] for n in WEIGHTS])
```

```python
import functools
import math

import numpy as np
import jax
import jax.numpy as jnp
from jax import lax
from jax.experimental import pallas as pl
from jax.experimental.pallas import tpu as pltpu

f32 = jnp.float32
bf16 = jnp.bfloat16
HI = lax.Precision.HIGHEST
MESH = pl.DeviceIdType.MESH

D = 1024
MEM_LEN = 256
EPS = 1e-6
A_HEADS, A_KV, A_DH = 12, 2, 64
A_Q = 768
BLK = 128
N_BUCKETS, MAX_DIST = 32, 128
B_QK, B_V, B_DH = 384, 768, 128
B_QKV = 1536
CHUNK = 64
X_Q = 256
D_FF = 2816
IN_A = 1280
IN_B = 2572
IN_B_PAD = 2688
LANE = 128
VMEM_LIMIT = 56 * 1024 * 1024
MM_ROWS = 1024

LR, B1, B2, AEPS, WD, STEP = 0.001, 0.9, 0.999, 1e-08, 0.01, 10


def _cp(sem=None):
    return pltpu.CompilerParams(dimension_semantics=sem, vmem_limit_bytes=VMEM_LIMIT)


def _dg(a, b, ca, cb, prec=None):
    return lax.dot_general(a, b, (((ca,), (cb,)), ((), ())), precision=prec, preferred_element_type=f32)


@jax.custom_vjp
def bdot(a, b):
    return _dg(a.astype(bf16), b.astype(bf16), 1, 0)


def _bdot_f(a, b):
    return bdot(a, b), (a, b)


def _bdot_b(res, g):
    a, b = res
    gb = g.astype(bf16)
    return _dg(gb, b.astype(bf16), 1, 1), _dg(a.astype(bf16), gb, 0, 0)


bdot.defvjp(_bdot_f, _bdot_b)


@jax.custom_vjp
def bdot_nt(a, b):
    return _dg(a.astype(bf16), b.astype(bf16), 1, 1)


def _bdot_nt_f(a, b):
    return bdot_nt(a, b), (a, b)


def _bdot_nt_b(res, g):
    a, b = res
    gb = g.astype(bf16)
    return _dg(gb, b.astype(bf16), 1, 0), _dg(gb, a.astype(bf16), 0, 0)


bdot_nt.defvjp(_bdot_nt_f, _bdot_nt_b)


def _shift_rows(x, s, down):
    n = x.shape[0]
    row = lax.broadcasted_iota(jnp.int32, x.shape, 0)
    if down:
        return jnp.where(row >= s, pltpu.roll(x, s, 0), 0.0)
    return jnp.where(row < n - s, pltpu.roll(x, n - s, 0), 0.0)


@functools.partial(jax.custom_vjp, nondiff_argnums=(1,))
def shift_down(x, s):
    return _shift_rows(x, s, True)


def _sd_f(x, s):
    return _shift_rows(x, s, True), None


def _sd_b(s, _, g):
    return (_shift_rows(g, s, False),)


shift_down.defvjp(_sd_f, _sd_b)


def _sigmoid(x):
    return 1.0 / (1.0 + jnp.exp(-x))


def _silu(x):
    return x * _sigmoid(x)


def _rms(x, g):
    return x * lax.rsqrt(jnp.mean(x * x, axis=-1, keepdims=True) + EPS) * g


def _tile(n, cap):
    u = n // LANE
    best = 1
    for d in range(1, u + 1):
        if u % d == 0 and d * LANE <= cap:
            best = d
    return best * LANE


def mm_nn(a, w, res=None, out_dtype=f32, name="mm_nn"):
    M, K = a.shape
    N = w.shape[1]
    tm, tn = min(MM_ROWS, M), _tile(N, 1024)

    def body(*refs):
        if res is None:
            a_ref, w_ref, o_ref = refs
            o_ref[...] = _dg(a_ref[...].astype(bf16), w_ref[...], 1, 0).astype(out_dtype)
        else:
            a_ref, w_ref, r_ref, o_ref = refs
            o_ref[...] = (r_ref[...] + _dg(a_ref[...].astype(bf16), w_ref[...], 1, 0)).astype(out_dtype)

    in_specs = [pl.BlockSpec((tm, K), lambda n, m: (m, 0)), pl.BlockSpec((K, tn), lambda n, m: (0, n))]
    args = [a, w]
    if res is not None:
        in_specs.append(pl.BlockSpec((tm, tn), lambda n, m: (m, n)))
        args.append(res)
    return pl.pallas_call(
        body, name=name, grid=(N // tn, M // tm), in_specs=in_specs,
        out_specs=pl.BlockSpec((tm, tn), lambda n, m: (m, n)),
        out_shape=jax.ShapeDtypeStruct((M, N), out_dtype),
        compiler_params=_cp(("parallel", "parallel")),
    )(*args)


def mm_nt(dy, w, name="mm_nt"):
    M, N = dy.shape
    K = w.shape[0]
    tm, tn = min(MM_ROWS, M), _tile(N, 1024)

    def body(dy_ref, w_ref, o_ref):
        @pl.when(pl.program_id(1) == 0)
        def _():
            o_ref[...] = jnp.zeros_like(o_ref)
        o_ref[...] += _dg(dy_ref[...].astype(bf16), w_ref[...], 1, 1)

    return pl.pallas_call(
        body, name=name, grid=(M // tm, N // tn),
        in_specs=[pl.BlockSpec((tm, tn), lambda m, n: (m, n)), pl.BlockSpec((K, tn), lambda m, n: (0, n))],
        out_specs=pl.BlockSpec((tm, K), lambda m, n: (m, 0)),
        out_shape=jax.ShapeDtypeStruct((M, K), f32),
        compiler_params=_cp(("parallel", "arbitrary")),
    )(dy, w)


def mm_tn(a, dy, name="mm_tn"):
    M, K = a.shape
    N = dy.shape[1]
    tm, tk, tn = min(512, M), _tile(K, 1408), _tile(N, 1024)

    def body(a_ref, dy_ref, o_ref):
        @pl.when(pl.program_id(2) == 0)
        def _():
            o_ref[...] = jnp.zeros_like(o_ref)
        o_ref[...] += _dg(a_ref[...].astype(bf16), dy_ref[...].astype(bf16), 0, 0)

    return pl.pallas_call(
        body, name=name, grid=(K // tk, N // tn, M // tm),
        in_specs=[pl.BlockSpec((tm, tk), lambda k, n, m: (m, k)), pl.BlockSpec((tm, tn), lambda k, n, m: (m, n))],
        out_specs=pl.BlockSpec((tk, tn), lambda k, n, m: (k, n)),
        out_shape=jax.ShapeDtypeStruct((K, N), f32),
        compiler_params=_cp(("parallel", "parallel", "arbitrary")),
    )(a, dy)


def rms_fwd(h, g, name):
    S = h.shape[0]
    t = min(512, S)

    def body(h_ref, g_ref, o_ref):
        o_ref[...] = _rms(h_ref[...], g_ref[...]).astype(bf16)

    return pl.pallas_call(
        body, name=name, grid=(S // t,),
        in_specs=[pl.BlockSpec((t, D), lambda i: (i, 0)), pl.BlockSpec((1, D), lambda i: (0, 0))],
        out_specs=pl.BlockSpec((t, D), lambda i: (i, 0)),
        out_shape=jax.ShapeDtypeStruct((S, D), bf16),
        compiler_params=_cp(("parallel",)),
    )(h, g.reshape(1, D))


def rms_bwd(h, g, dn, dres, name):
    S = h.shape[0]
    t = min(512, S)

    def body(h_ref, g_ref, dn_ref, dr_ref, dh_ref, dg_ref):
        @pl.when(pl.program_id(0) == 0)
        def _():
            dg_ref[...] = jnp.zeros_like(dg_ref)
        _, vjp = jax.vjp(_rms, h_ref[...], g_ref[...])
        dh, dg = vjp(dn_ref[...])
        dh_ref[...] = dr_ref[...] + dh
        dg_ref[...] += dg

    tok = pl.BlockSpec((t, D), lambda i: (i, 0))
    vec = pl.BlockSpec((1, D), lambda i: (0, 0))
    return pl.pallas_call(
        body, name=name, grid=(S // t,), in_specs=[tok, vec, tok, tok], out_specs=[tok, vec],
        out_shape=[jax.ShapeDtypeStruct((S, D), f32), jax.ShapeDtypeStruct((1, D), f32)],
        compiler_params=_cp(("arbitrary",)),
    )(h, g.reshape(1, D), dn, dres)


def loss_head(h, g, target):
    S = h.shape[0]
    t = min(512, S)

    def f(hh, gg, tt):
        err = _rms(hh, gg) - tt
        return 0.5 * jnp.sum(jnp.mean(err * err, axis=-1, keepdims=True), axis=0, keepdims=True)

    def body(h_ref, g_ref, t_ref, loss_ref, dh_ref, dg_ref):
        @pl.when(pl.program_id(0) == 0)
        def _():
            dg_ref[...] = jnp.zeros_like(dg_ref)
            loss_ref[...] = jnp.zeros_like(loss_ref)
        val, vjp = jax.vjp(lambda a, b: f(a, b, t_ref[...]), h_ref[...], g_ref[...])
        dh, dg = vjp(jnp.ones((1, 1), f32))
        dh_ref[...] = dh
        dg_ref[...] += dg
        loss_ref[...] += jnp.broadcast_to(val, loss_ref.shape)

    tok = pl.BlockSpec((t, D), lambda i: (i, 0))
    vec = pl.BlockSpec((1, D), lambda i: (0, 0))
    return pl.pallas_call(
        body, name="loss_head", grid=(S // t,), in_specs=[tok, vec, tok],
        out_specs=[pl.BlockSpec((1, LANE), lambda i: (0, 0)), tok, vec],
        out_shape=[jax.ShapeDtypeStruct((1, LANE), f32), jax.ShapeDtypeStruct((S, D), f32),
                   jax.ShapeDtypeStruct((1, D), f32)],
        compiler_params=_cp(("arbitrary",)),
    )(h, g.reshape(1, D), target)


def memkv_fwd(mem, g, w, name):
    def body(m_ref, g_ref, w_ref, o_ref):
        o_ref[...] = _dg(_rms(m_ref[...], g_ref[...]).astype(bf16), w_ref[...], 1, 0)

    return pl.pallas_call(
        body, name=name, out_shape=jax.ShapeDtypeStruct((MEM_LEN, 2 * X_Q), f32), compiler_params=_cp(),
    )(mem, g.reshape(1, D), w)


def memkv_bwd(mem, g, w, dkv, name):
    def body(m_ref, g_ref, w_ref, d_ref, dg_ref, dw_ref):
        n, vjp = jax.vjp(lambda gg: _rms(m_ref[...], gg), g_ref[...])
        db = d_ref[...].astype(bf16)
        dw_ref[...] = _dg(n.astype(bf16), db, 0, 0)
        dg_ref[...] = vjp(_dg(db, w_ref[...], 1, 1))[0]

    return pl.pallas_call(
        body, name=name,
        out_shape=[jax.ShapeDtypeStruct((1, D), f32), jax.ShapeDtypeStruct((D, 2 * X_Q), f32)],
        compiler_params=_cp(),
    )(mem, g.reshape(1, D), w, dkv)


def _xattn_f(xq, mk, mv):
    lane = lax.broadcasted_iota(jnp.int32, (1, X_Q), 1)
    out = jnp.zeros(xq.shape, f32)
    for hd in range(4):
        msk = (lane // 64 == hd).astype(f32)
        s = bdot_nt(xq * msk, mk) * (64 ** -0.5)
        m = lax.stop_gradient(jnp.max(s, axis=-1, keepdims=True))
        p = jnp.exp(s - m)
        p = p / jnp.sum(p, axis=-1, keepdims=True)
        out = out + bdot(p, mv * msk)
    return out


def xattn_fwd(proj, col, kv, name):
    S = proj.shape[0]
    t = min(512, S)
    cb = col // X_Q

    def body(q_ref, k_ref, v_ref, o_ref):
        o_ref[...] = _xattn_f(q_ref[...], k_ref[...], v_ref[...])

    return pl.pallas_call(
        body, name=name, grid=(S // t,),
        in_specs=[pl.BlockSpec((t, X_Q), lambda i: (i, cb)), pl.BlockSpec((MEM_LEN, X_Q), lambda i: (0, 0)),
                  pl.BlockSpec((MEM_LEN, X_Q), lambda i: (0, 1))],
        out_specs=pl.BlockSpec((t, X_Q), lambda i: (i, 0)),
        out_shape=jax.ShapeDtypeStruct((S, X_Q), f32),
        compiler_params=_cp(("parallel",)),
    )(proj, kv, kv)


def xattn_bwd(proj, col, kv, dmix, name):
    S = proj.shape[0]
    t = min(512, S)
    cb = col // X_Q

    def body(q_ref, k_ref, v_ref, do_ref, dq_ref, dk_ref, dv_ref):
        @pl.when(pl.program_id(0) == 0)
        def _():
            dk_ref[...] = jnp.zeros_like(dk_ref)
            dv_ref[...] = jnp.zeros_like(dv_ref)
        _, vjp = jax.vjp(_xattn_f, q_ref[...], k_ref[...], v_ref[...])
        dq, dk, dv = vjp(do_ref[...])
        dq_ref[...] = dq
        dk_ref[...] += dk
        dv_ref[...] += dv

    kvb = pl.BlockSpec((MEM_LEN, X_Q), lambda i: (0, 0))
    dq, dk, dv = pl.pallas_call(
        body, name=name, grid=(S // t,),
        in_specs=[pl.BlockSpec((t, X_Q), lambda i: (i, cb)), kvb,
                  pl.BlockSpec((MEM_LEN, X_Q), lambda i: (0, 1)), pl.BlockSpec((t, X_Q), lambda i: (i, 3))],
        out_specs=[pl.BlockSpec((t, X_Q), lambda i: (i, 0)), kvb, kvb],
        out_shape=[jax.ShapeDtypeStruct((S, X_Q), f32), jax.ShapeDtypeStruct((MEM_LEN, X_Q), f32),
                   jax.ShapeDtypeStruct((MEM_LEN, X_Q), f32)],
        compiler_params=_cp(("arbitrary",)),
    )(proj, kv, kv, dmix)
    return dq, jnp.concatenate([dk, dv], axis=1)


def _bucket_map():
    qi = np.arange(BLK)[:, None]
    kj = np.arange(2 * BLK)[None, :]
    n = np.maximum(BLK + qi - kj, 0)
    max_exact = N_BUCKETS // 2
    nf = np.maximum(n, 1).astype(np.float64)
    large = max_exact + (np.log(nf / max_exact) / math.log(MAX_DIST / max_exact)
                         * (N_BUCKETS - max_exact)).astype(np.int32)
    large = np.minimum(large, N_BUCKETS - 1)
    return np.where(n < max_exact, n, large).astype(np.int32)


def bias_build(rel_bias):
    def body(rb_ref, bk_ref, o_ref):
        bk = bk_ref[...]
        for h in range(A_HEADS):
            acc = jnp.zeros((BLK, 2 * BLK), f32)
            for b in range(N_BUCKETS):
                acc = jnp.where(bk == b, rb_ref[b, h], acc)
            o_ref[h] = acc

    return pl.pallas_call(
        body, name="bias_build",
        in_specs=[pl.BlockSpec(memory_space=pltpu.SMEM), pl.BlockSpec(memory_space=pltpu.VMEM)],
        out_specs=pl.BlockSpec(memory_space=pltpu.VMEM),
        out_shape=jax.ShapeDtypeStruct((A_HEADS, BLK, 2 * BLK), f32), compiler_params=_cp(),
    )(rel_bias, jnp.asarray(_bucket_map()))


def bias_grad(dbias):
    def body(d_ref, bk_ref, o_ref):
        bk = bk_ref[...]
        row = lax.broadcasted_iota(jnp.int32, (N_BUCKETS, LANE), 0)
        lane = lax.broadcasted_iota(jnp.int32, (N_BUCKETS, LANE), 1)
        acc = jnp.zeros((N_BUCKETS, LANE), f32)
        for h in range(A_HEADS):
            d = d_ref[h]
            for b in range(N_BUCKETS):
                s = jnp.sum(jnp.where(bk == b, d, 0.0), keepdims=True)
                acc = acc + jnp.where((row == b) & (lane == h), s, 0.0)
        o_ref[...] = acc

    return pl.pallas_call(
        body, name="bias_grad", out_shape=jax.ShapeDtypeStruct((N_BUCKETS, LANE), f32), compiler_params=_cp(),
    )(dbias, jnp.asarray(_bucket_map()))


def _swa_f(qb, kp, kc, vp, vc, bias, sk, first):
    kband = jnp.concatenate([kp, kc], axis=0)
    vband = jnp.concatenate([vp, vc], axis=0)
    qi = lax.broadcasted_iota(jnp.int32, (BLK, 2 * BLK), 0)
    kj = lax.broadcasted_iota(jnp.int32, (BLK, 2 * BLK), 1)
    rel = kj - qi
    ok = (rel >= 1) & (rel <= BLK) & ((kj >= BLK) | jnp.logical_not(first))
    lane = lax.broadcasted_iota(jnp.int32, (1, LANE), 1)
    lane_b = lax.broadcasted_iota(jnp.int32, (BLK, LANE), 1)
    outs = []
    for p in range(A_HEADS // 2):
        qp = qb[:, LANE * p:LANE * (p + 1)]
        acc = jnp.zeros((BLK, LANE), f32)
        for g in range(2):
            h = g * (A_HEADS // 2) + p
            msk = (lane // A_DH == g).astype(f32)
            s = bdot_nt(qp * msk, kband) * (A_DH ** -0.5) + bias[h]
            s = jnp.where(ok, s, -1e30)
            skb = jnp.broadcast_to(sk[h:h + 1, :], (BLK, LANE))
            sink = jnp.sum(jnp.where(lane_b == 0, skb, 0.0), axis=-1, keepdims=True)
            m = lax.stop_gradient(jnp.maximum(jnp.max(s, axis=-1, keepdims=True), sink))
            e = jnp.exp(s - m)
            prob = e / (jnp.sum(e, axis=-1, keepdims=True) + jnp.exp(sink - m))
            acc = acc + bdot(prob, vband) * msk
        outs.append(acc)
    return jnp.concatenate(outs, axis=1)


def _swa_specs(nb, rev):
    bi = (lambda i: nb - 1 - i) if rev else (lambda i: i)
    return [
        pl.BlockSpec((BLK, A_Q), lambda i: (bi(i), 0)),
        pl.BlockSpec((BLK, LANE), lambda i: (jnp.maximum(bi(i) - 1, 0), 6)),
        pl.BlockSpec((BLK, LANE), lambda i: (bi(i), 6)),
        pl.BlockSpec((BLK, LANE), lambda i: (jnp.maximum(bi(i) - 1, 0), 7)),
        pl.BlockSpec((BLK, LANE), lambda i: (bi(i), 7)),
        pl.BlockSpec((A_HEADS, BLK, 2 * BLK), lambda i: (0, 0, 0)),
        pl.BlockSpec((16, LANE), lambda i: (0, 0)),
    ]


def swa_fwd(proj, bias, sk):
    S = proj.shape[0]
    nb = S // BLK

    def body(q_ref, kp_ref, kc_ref, vp_ref, vc_ref, b_ref, s_ref, o_ref):
        o_ref[...] = _swa_f(q_ref[...], kp_ref[...], kc_ref[...], vp_ref[...], vc_ref[...], b_ref[...], s_ref[...],
                            pl.program_id(0) == 0)

    return pl.pallas_call(
        body, name="swa_fwd", grid=(nb,), in_specs=_swa_specs(nb, False),
        out_specs=pl.BlockSpec((BLK, A_Q), lambda i: (i, 0)),
        out_shape=jax.ShapeDtypeStruct((S, A_Q), f32), compiler_params=_cp(("parallel",)),
    )(proj, proj, proj, proj, proj, bias, sk)


def swa_bwd(proj, bias, sk, dmix):
    S = proj.shape[0]
    nb = S // BLK

    def body(q_ref, kp_ref, kc_ref, vp_ref, vc_ref, b_ref, s_ref, do_ref, dqkv_ref, db_ref, ds_ref, ck, cv):
        i = pl.program_id(0)

        @pl.when(i == 0)
        def _():
            db_ref[...] = jnp.zeros_like(db_ref)
            ds_ref[...] = jnp.zeros_like(ds_ref)
            ck[...] = jnp.zeros_like(ck)
            cv[...] = jnp.zeros_like(cv)
        first = i == nb - 1
        _, vjp = jax.vjp(lambda *a: _swa_f(*a, first), q_ref[...], kp_ref[...], kc_ref[...], vp_ref[...],
                         vc_ref[...], b_ref[...], s_ref[...])
        dq, dkp, dkc, dvp, dvc, db, ds = vjp(do_ref[...])
        dqkv_ref[...] = jnp.concatenate([dq, dkc + ck[...], dvc + cv[...]], axis=1)
        ck[...] = dkp
        cv[...] = dvp
        db_ref[...] += db
        ds_ref[...] += ds

    return pl.pallas_call(
        body, name="swa_bwd", grid=(nb,),
        in_specs=_swa_specs(nb, True) + [pl.BlockSpec((BLK, A_Q), lambda i: (nb - 1 - i, 0))],
        out_specs=[pl.BlockSpec((BLK, D), lambda i: (nb - 1 - i, 0)),
                   pl.BlockSpec((A_HEADS, BLK, 2 * BLK), lambda i: (0, 0, 0)),
                   pl.BlockSpec((16, LANE), lambda i: (0, 0))],
        out_shape=[jax.ShapeDtypeStruct((S, D), f32), jax.ShapeDtypeStruct((A_HEADS, BLK, 2 * BLK), f32),
                   jax.ShapeDtypeStruct((16, LANE), f32)],
        scratch_shapes=[pltpu.VMEM((BLK, LANE), f32), pltpu.VMEM((BLK, LANE), f32)],
        compiler_params=_cp(("arbitrary",)),
    )(proj, proj, proj, proj, proj, bias, sk, dmix)


def _dnprep_f(x, w, is_qk):
    c = (w[3:4] * x + w[2:3] * shift_down(x, 1) + w[1:2] * shift_down(x, 2) + w[0:1] * shift_down(x, 3))
    a = _silu(c)
    n = a * lax.rsqrt(jnp.sum(a * a, axis=-1, keepdims=True) + EPS)
    return jnp.where(is_qk, n, a)


def dnprep_fwd(proj, cw):
    S = proj.shape[0]
    nblk = B_QKV // LANE

    def body(x_ref, w_ref, o_ref):
        o_ref[...] = _dnprep_f(x_ref[...], w_ref[...], pl.program_id(0) < 2 * B_QK // LANE)

    return pl.pallas_call(
        body, name="dnprep_fwd", grid=(nblk,),
        in_specs=[pl.BlockSpec((S, LANE), lambda j: (0, j)), pl.BlockSpec((4, LANE), lambda j: (0, j))],
        out_specs=pl.BlockSpec((S, LANE), lambda j: (0, j)),
        out_shape=jax.ShapeDtypeStruct((S, B_QKV), f32), compiler_params=_cp(("parallel",)),
    )(proj, cw)


def dnprep_bwd(proj, cw, dqkvn):
    S = proj.shape[0]
    nblk = B_QKV // LANE

    def body(x_ref, w_ref, d_ref, dx_ref, dw_ref):
        is_qk = pl.program_id(0) < 2 * B_QK // LANE
        _, vjp = jax.vjp(lambda a, b: _dnprep_f(a, b, is_qk), x_ref[...], w_ref[...])
        dx, dw = vjp(d_ref[...])
        dx_ref[...] = dx
        dw_ref[...] = dw

    col = pl.BlockSpec((S, LANE), lambda j: (0, j))
    wsp = pl.BlockSpec((4, LANE), lambda j: (0, j))
    return pl.pallas_call(
        body, name="dnprep_bwd", grid=(nblk,), in_specs=[col, wsp, col], out_specs=[col, wsp],
        out_shape=[jax.ShapeDtypeStruct((S, B_QKV), f32), jax.ShapeDtypeStruct((4, B_QKV), f32)],
        compiler_params=_cp(("parallel",)),
    )(proj, cw, dqkvn)


def _hdot(a, b, ca=1, cb=0):
    return _dg(a, b, ca, cb, HI)


def _bdg(a, b, ca, cb):
    dn = (((ca,), (cb,)), ((0,), (0,)))
    ah, bh = a.astype(bf16), b.astype(bf16)
    al, bl = (a - ah.astype(f32)).astype(bf16), (b - bh.astype(f32)).astype(bf16)
    return (lax.dot_general(ah, bh, dn, preferred_element_type=f32)
            + lax.dot_general(ah, bl, dn, preferred_element_type=f32)
            + lax.dot_general(al, bh, dn, preferred_element_type=f32))


@jax.custom_vjp
def hbd(a, b):
    return _bdg(a, b, 2, 1)


@jax.custom_vjp
def hbd_nt(a, b):
    return _bdg(a, b, 2, 2)


@jax.custom_vjp
def hbd_tn(a, b):
    return _bdg(a, b, 1, 1)


hbd.defvjp(lambda a, b: (hbd(a, b), (a, b)), lambda r, g: (hbd_nt(g, r[1]), hbd_tn(r[0], g)))
hbd_nt.defvjp(lambda a, b: (hbd_nt(a, b), (a, b)), lambda r, g: (hbd(g, r[1]), hbd_tn(g, r[0])))
hbd_tn.defvjp(lambda a, b: (hbd_tn(a, b), (a, b)), lambda r, g: (hbd_nt(r[1], g), hbd(r[0], g)))


def _stack(xs):
    return jnp.concatenate([x[None] for x in xs], axis=0)


def _lane_col(x, j):
    lane = lax.broadcasted_iota(jnp.int32, (1, LANE), 1)
    return jnp.sum(jnp.where(lane == j, x, 0.0), axis=-1, keepdims=True)


def _tri_inv(a_mat):
    r = lax.broadcasted_iota(jnp.int32, (1, CHUNK, CHUNK), 1)
    c = lax.broadcasted_iota(jnp.int32, (1, CHUNK, CHUNK), 2)
    pw = -a_mat
    inv = (r == c).astype(f32) + pw
    for _ in range(5):
        pw = hbd(pw, pw)
        inv = inv + hbd(inv, pw)
    return inv


@jax.custom_vjp
def _tri_inv_known(a_mat, inv):
    return inv


_tri_inv_known.defvjp(lambda a, inv: (inv, inv),
                      lambda inv, g: (-hbd_tn(inv, hbd_nt(g, inv)), jnp.zeros_like(inv)))


def _dnc_f(q, k, v, seg, prm, inverse=_tri_inv):
    C = CHUNK
    B = q.shape[0]
    rows = seg.shape[0]
    beta_all = _sigmoid(seg)
    xx = seg + prm[1:2]
    g_all = -jnp.exp(prm[0:1]) * (jnp.maximum(xx, 0.0) + jnp.log(1.0 + jnp.exp(-jnp.abs(xx))))
    r2 = lax.broadcasted_iota(jnp.int32, (rows, rows), 0)
    c2 = lax.broadcasted_iota(jnp.int32, (rows, rows), 1)
    within = (r2 >= c2) & (r2 // C == c2 // C)
    gc_all = _hdot(within.astype(f32), g_all)
    beta = _stack([_lane_col(beta_all[C * j:C * (j + 1)], h) for j in range(rows // C) for h in range(6)])
    gc = _stack([_lane_col(gc_all[C * j:C * (j + 1)], 6 + h) for j in range(rows // C) for h in range(6)])
    r = lax.broadcasted_iota(jnp.int32, (1, C, C), 1)
    c = lax.broadcasted_iota(jnp.int32, (1, C, C), 2)
    incl = r >= c
    strict = r > c
    eye = (r == c).astype(f32)
    g_row = hbd(jnp.ones((B, C, C), f32), eye * gc)
    decay = jnp.where(incl, jnp.exp(jnp.where(incl, gc - g_row, 0.0)), 0.0)
    a_mat = beta * hbd_nt(k, k) * jnp.where(strict, decay, 0.0)
    eg = jnp.exp(gc)
    inv = inverse(a_mat)
    u = hbd(inv, beta * v)
    w = hbd(inv, (beta * eg) * k)
    qc = q * (B_DH ** -0.5)
    attn = hbd_nt(qc, k) * decay
    last = (lax.broadcasted_iota(jnp.int32, (1, C, 1), 1) == C - 1).astype(f32)
    g_last = jnp.sum(gc * last, axis=1, keepdims=True)
    dc = jnp.broadcast_to(jnp.exp(g_last), (B, 1, LANE)).reshape(B, LANE)
    return u, w, qc * eg, k * jnp.exp(g_last - gc), attn, dc, inv


def _dns_f(S0, u, w, qd, kt, attn, dcrows):
    dc = _lane_col(dcrows, 0).reshape(6, 1, 1)
    delta = u - hbd(w, S0)
    out = hbd(qd, S0) + hbd(attn, delta)
    return out, dc * S0 + hbd_tn(kt, delta)


def _dnpost_f(o, z, grow):
    outs = []
    for h in range(6):
        oh = o[:, LANE * h:LANE * (h + 1)]
        outs.append(oh * lax.rsqrt(jnp.mean(oh * oh, axis=-1, keepdims=True) + EPS) * grow
                    * _silu(z[:, LANE * h:LANE * (h + 1)]))
    return jnp.concatenate(outs, axis=1)


def _hs(h):
    return slice(LANE * h, LANE * (h + 1))


DN_CHUNKS = 2


def _heads(ref, share):
    return _stack([ref[CHUNK * j:CHUNK * (j + 1), _hs(h // share)]
                   for j in range(ref.shape[0] // CHUNK) for h in range(6)])


def _put_heads(ref, val):
    for j in range(ref.shape[0] // CHUNK):
        for h in range(6):
            ref[CHUNK * j:CHUNK * (j + 1), _hs(h)] = val[6 * j + h]


def _dnc_in_specs():
    rows = CHUNK * DN_CHUNKS
    return [
        pl.BlockSpec((rows, B_QK), lambda n: (n, 0)),
        pl.BlockSpec((rows, B_QK), lambda n: (n, 1)),
        pl.BlockSpec((rows, B_V), lambda n: (n, 1)),
        pl.BlockSpec((rows, LANE), lambda n: (n, 20)),
        pl.BlockSpec((8, LANE), lambda n: (0, 0)),
    ]


def _dnc_out_specs(rev_nc=None, chunks=1):
    ci = (lambda n: n) if rev_nc is None else (lambda n: rev_nc - 1 - n)
    wide = pl.BlockSpec((CHUNK * chunks, B_V), lambda n: (ci(n), 0))
    return [wide, wide, wide, wide, pl.BlockSpec((chunks, 6, CHUNK, CHUNK), lambda n: (ci(n), 0, 0, 0)),
            pl.BlockSpec((chunks, 8, LANE), lambda n: (ci(n), 0, 0))]


def _dc_rows(dc):
    pad = jnp.zeros((2, LANE), f32)
    return _stack([jnp.concatenate([dc[6 * j:6 * (j + 1)], pad], axis=0) for j in range(dc.shape[0] // 6)])


def _dnc_shapes(S):
    nc = S // CHUNK
    wide = jax.ShapeDtypeStruct((S, B_V), f32)
    return [wide, wide, wide, wide, jax.ShapeDtypeStruct((nc, 6, CHUNK, CHUNK), f32),
            jax.ShapeDtypeStruct((nc, 8, LANE), f32)]


def dnc_fwd(qkvn, proj, prm):
    S = proj.shape[0]

    def body(q_ref, k_ref, v_ref, s_ref, p_ref, u_ref, w_ref, qd_ref, kt_ref, at_ref, dc_ref, inv_ref):
        u, w, qd, kt, attn, dc, inv = _dnc_f(_heads(q_ref, 2), _heads(k_ref, 2), _heads(v_ref, 1), s_ref[...],
                                             p_ref[...])
        inv_ref[...] = inv.reshape(inv_ref.shape)
        _put_heads(u_ref, u)
        _put_heads(w_ref, w)
        _put_heads(qd_ref, qd)
        _put_heads(kt_ref, kt)
        at_ref[...] = attn.reshape(at_ref.shape)
        dc_ref[...] = _dc_rows(dc)

    outs = _dnc_out_specs(chunks=DN_CHUNKS)
    out = pl.pallas_call(
        body, name="dn_chunk_fwd", grid=(S // (CHUNK * DN_CHUNKS),), in_specs=_dnc_in_specs(),
        out_specs=outs + [outs[4]], out_shape=_dnc_shapes(S) + [_dnc_shapes(S)[4]],
        compiler_params=_cp(("parallel",)),
    )(qkvn, qkvn, qkvn, proj, prm)
    return out[:6], out[6]


def dnc_bwd(qkvn, proj, prm, inv, cots):
    S = proj.shape[0]

    def body(q_ref, k_ref, v_ref, s_ref, p_ref, inv_ref, du_ref, dw_ref, dqd_ref, dkt_ref, dat_ref, ddc_ref,
             dx_ref, dseg_ref, dprm_ref):
        @pl.when(pl.program_id(0) == 0)
        def _():
            dprm_ref[...] = jnp.zeros_like(dprm_ref)
        nb = 6 * DN_CHUNKS
        known = functools.partial(_tri_inv_known, inv=inv_ref[...].reshape(nb, CHUNK, CHUNK))
        _, vjp = jax.vjp(lambda *a: _dnc_f(*a, inverse=known)[:6], _heads(q_ref, 2), _heads(k_ref, 2),
                         _heads(v_ref, 1), s_ref[...], p_ref[...])
        ddc = jnp.concatenate([ddc_ref[j, 0:6, :] for j in range(DN_CHUNKS)], axis=0)
        dq, dk, dv, dseg, dprm = vjp((_heads(du_ref, 1), _heads(dw_ref, 1), _heads(dqd_ref, 1), _heads(dkt_ref, 1),
                                      dat_ref[...].reshape(nb, CHUNK, CHUNK), ddc))
        for j in range(DN_CHUNKS):
            o = 6 * j
            dx_ref[CHUNK * j:CHUNK * (j + 1), :] = jnp.concatenate(
                [dq[o] + dq[o + 1], dq[o + 2] + dq[o + 3], dq[o + 4] + dq[o + 5],
                 dk[o] + dk[o + 1], dk[o + 2] + dk[o + 3], dk[o + 4] + dk[o + 5]] + [dv[o + h] for h in range(6)], axis=1)
        dseg_ref[...] = dseg
        dprm_ref[...] += dprm

    rows = CHUNK * DN_CHUNKS
    outs = _dnc_out_specs(chunks=DN_CHUNKS)
    return pl.pallas_call(
        body, name="dn_chunk_bwd", grid=(S // rows,),
        in_specs=_dnc_in_specs() + [outs[4]] + outs,
        out_specs=[pl.BlockSpec((rows, B_QKV), lambda n: (n, 0)), pl.BlockSpec((rows, LANE), lambda n: (n, 0)),
                   pl.BlockSpec((8, LANE), lambda n: (0, 0))],
        out_shape=[jax.ShapeDtypeStruct((S, B_QKV), f32), jax.ShapeDtypeStruct((S, LANE), f32),
                   jax.ShapeDtypeStruct((8, LANE), f32)],
        compiler_params=_cp(("arbitrary",)),
    )(qkvn, qkvn, qkvn, proj, prm, inv, *cots)


def dns_fwd(chunked):
    u = chunked[0]
    S = u.shape[0]
    nc = S // CHUNK

    def body(u_ref, w_ref, qd_ref, kt_ref, at_ref, dc_ref, o_ref, st_ref, st):
        @pl.when(pl.program_id(0) == 0)
        def _():
            st[...] = jnp.zeros_like(st)
        S0 = st[...]
        st_ref[0] = S0
        out, S1 = _dns_f(S0, _heads(u_ref, 1), _heads(w_ref, 1), _heads(qd_ref, 1), _heads(kt_ref, 1),
                         at_ref[0], dc_ref[0, 0:6, :])
        _put_heads(o_ref, out)
        st[...] = S1

    return pl.pallas_call(
        body, name="dn_scan_fwd", grid=(nc,), in_specs=_dnc_out_specs(),
        out_specs=[pl.BlockSpec((CHUNK, B_V), lambda n: (n, 0)),
                   pl.BlockSpec((1, 6, B_DH, B_DH), lambda n: (n, 0, 0, 0))],
        out_shape=[jax.ShapeDtypeStruct((S, B_V), f32), jax.ShapeDtypeStruct((nc, 6, B_DH, B_DH), f32)],
        scratch_shapes=[pltpu.VMEM((6, B_DH, B_DH), f32)],
        compiler_params=_cp(("arbitrary",)),
    )(*chunked)


def dns_bwd(chunked, states, do):
    S = do.shape[0]
    nc = S // CHUNK

    def body(u_ref, w_ref, qd_ref, kt_ref, at_ref, dc_ref, st_ref, do_ref,
             du_ref, dw_ref, dqd_ref, dkt_ref, dat_ref, ddc_ref, dst):
        @pl.when(pl.program_id(0) == 0)
        def _():
            dst[...] = jnp.zeros_like(dst)
        _, vjp = jax.vjp(_dns_f, st_ref[0], _heads(u_ref, 1), _heads(w_ref, 1), _heads(qd_ref, 1), _heads(kt_ref, 1),
                         at_ref[0], dc_ref[0, 0:6, :])
        dS0, du, dw, dqd, dkt, dat, ddc = vjp((_heads(do_ref, 1), dst[...]))
        dst[...] = dS0
        _put_heads(du_ref, du)
        _put_heads(dw_ref, dw)
        _put_heads(dqd_ref, dqd)
        _put_heads(dkt_ref, dkt)
        dat_ref[0] = dat
        ddc_ref[0] = jnp.concatenate([ddc, jnp.zeros((2, LANE), f32)], axis=0)

    return pl.pallas_call(
        body, name="dn_scan_bwd", grid=(nc,),
        in_specs=_dnc_out_specs(nc) + [pl.BlockSpec((1, 6, B_DH, B_DH), lambda n: (nc - 1 - n, 0, 0, 0)),
                                       pl.BlockSpec((CHUNK, B_V), lambda n: (nc - 1 - n, 0))],
        out_specs=_dnc_out_specs(nc), out_shape=_dnc_shapes(S),
        scratch_shapes=[pltpu.VMEM((6, B_DH, B_DH), f32)],
        compiler_params=_cp(("arbitrary",)),
    )(*chunked, states, do)


def dnpost_fwd(o, proj, prm):
    S = o.shape[0]
    t = min(512, S)

    def body(o_ref, z_ref, p_ref, y_ref):
        y_ref[...] = _dnpost_f(o_ref[...], z_ref[...], p_ref[2:3, :])

    tok = pl.BlockSpec((t, B_V), lambda i: (i, 0))
    return pl.pallas_call(
        body, name="dn_post_fwd", grid=(S // t,),
        in_specs=[tok, pl.BlockSpec((t, B_V), lambda i: (i, 2)), pl.BlockSpec((8, LANE), lambda i: (0, 0))],
        out_specs=tok, out_shape=jax.ShapeDtypeStruct((S, B_V), f32), compiler_params=_cp(("parallel",)),
    )(o, proj, prm)


def dnpost_bwd(o, proj, prm, dmix):
    S = o.shape[0]
    t = min(512, S)

    def body(o_ref, z_ref, p_ref, dy_ref, do_ref, dz_ref, dg_ref):
        @pl.when(pl.program_id(0) == 0)
        def _():
            dg_ref[...] = jnp.zeros_like(dg_ref)
        _, vjp = jax.vjp(_dnpost_f, o_ref[...], z_ref[...], p_ref[2:3, :])
        do, dz, dg = vjp(dy_ref[...])
        do_ref[...] = do
        dz_ref[...] = dz
        dg_ref[...] += dg

    tok = pl.BlockSpec((t, B_V), lambda i: (i, 0))
    return pl.pallas_call(
        body, name="dn_post_bwd", grid=(S // t,),
        in_specs=[tok, pl.BlockSpec((t, B_V), lambda i: (i, 2)), pl.BlockSpec((8, LANE), lambda i: (0, 0)), tok],
        out_specs=[tok, tok, pl.BlockSpec((1, LANE), lambda i: (0, 0))],
        out_shape=[jax.ShapeDtypeStruct((S, B_V), f32), jax.ShapeDtypeStruct((S, B_V), f32),
                   jax.ShapeDtypeStruct((1, LANE), f32)],
        compiler_params=_cp(("arbitrary",)),
    )(o, proj, prm, dmix)


N_FF_BLK = D_FF // LANE
GU_SHARD = 2 * D_FF // 4


GLU_ROWS = 256
HALO = 8


def _glu_f(gext, up, w, b):
    c = w[2:3] * gext + w[1:2] * shift_down(gext, 1) + w[0:1] * shift_down(gext, 2) + b
    return _silu(c)[HALO:] * up


def _glu_gext(g_ref, r0, first):
    if first:
        return jnp.concatenate([jnp.zeros((HALO, LANE), f32), g_ref[0:GLU_ROWS, :]], axis=0)
    return g_ref[pl.ds(r0 - HALO, GLU_ROWS + HALO), :]


def glu_fwd(gu, w, b, name):
    S = gu.shape[0]
    T = GLU_ROWS

    def body(g_ref, u_ref, w_ref, b_ref, o_ref):
        wv, bv = w_ref[...], b_ref[...]

        def tile(r0, first):
            act = _glu_f(_glu_gext(g_ref, r0, first), u_ref[pl.ds(r0, T), :], wv, bv)
            o_ref[pl.ds(r0, T), :] = act.astype(bf16)

        tile(0, True)

        @pl.loop(1, S // T)
        def _(t):
            tile(pl.multiple_of(t * T, T), False)

    col = pl.BlockSpec((S, LANE), lambda j: (0, j))
    return pl.pallas_call(
        body, name=name, grid=(N_FF_BLK,),
        in_specs=[col, pl.BlockSpec((S, LANE), lambda j: (0, N_FF_BLK + j)), pl.BlockSpec((3, LANE), lambda j: (0, j)),
                  pl.BlockSpec((1, LANE), lambda j: (0, j))],
        out_specs=col, out_shape=jax.ShapeDtypeStruct((S, D_FF), bf16), compiler_params=_cp(("parallel",)),
    )(gu, gu, w, b.reshape(1, D_FF))


def glu_bwd(gu, w, b, dact, name):
    S = gu.shape[0]
    T = GLU_ROWS

    def body(g_ref, u_ref, w_ref, b_ref, d_ref, dg_ref, dw_ref, db_ref, acc):
        wv, bv = w_ref[...], b_ref[...]

        def tile(r0, first):
            _, vjp = jax.vjp(_glu_f, _glu_gext(g_ref, r0, first), u_ref[pl.ds(r0, T), :], wv, bv)
            dgx, du, dw, db = vjp(d_ref[pl.ds(r0, T), :])
            acc[pl.ds(r0, T), :] = dgx[HALO:]
            if not first:
                acc[pl.ds(r0 - HALO, HALO), :] += dgx[:HALO]
            dg_ref[1, pl.ds(r0, T), :] = du.astype(bf16)
            return dw, db

        dw0, db0 = tile(0, True)
        dw_ref[...] = dw0
        db_ref[...] = db0

        @pl.loop(1, S // T)
        def _(t):
            dw, db = tile(pl.multiple_of(t * T, T), False)
            dw_ref[...] += dw
            db_ref[...] += db

        dg_ref[0] = acc[...].astype(bf16)

    col = pl.BlockSpec((S, LANE), lambda j: (0, j))
    wsp = pl.BlockSpec((3, LANE), lambda j: (0, j))
    bsp = pl.BlockSpec((1, LANE), lambda j: (0, j))
    return pl.pallas_call(
        body, name=name, grid=(N_FF_BLK,),
        in_specs=[col, pl.BlockSpec((S, LANE), lambda j: (0, N_FF_BLK + j)), wsp, bsp, col],
        out_specs=[pl.BlockSpec((2, S, LANE), lambda j: (0, 0, j)), wsp, bsp],
        out_shape=[jax.ShapeDtypeStruct((2, S, D_FF), bf16), jax.ShapeDtypeStruct((3, D_FF), f32),
                   jax.ShapeDtypeStruct((1, D_FF), f32)],
        scratch_shapes=[pltpu.VMEM((S, LANE), f32)],
        compiler_params=_cp(("parallel",)),
    )(gu, gu, w, b.reshape(1, D_FF), dact)


def gu_fwd(n2, wg, name):
    S = n2.shape[0]
    tm = min(MM_ROWS, S)

    def body(a_ref, w_ref, o_ref):
        o_ref[...] = _dg(a_ref[...], w_ref[...], 1, 0)

    return pl.pallas_call(
        body, name=name, grid=(4, S // tm),
        in_specs=[pl.BlockSpec((tm, D), lambda s, m: (m, 0)), pl.BlockSpec((None, D, GU_SHARD), lambda s, m: (s, 0, 0))],
        out_specs=pl.BlockSpec((tm, GU_SHARD), lambda s, m: (m, s)),
        out_shape=jax.ShapeDtypeStruct((S, 2 * D_FF), f32), compiler_params=_cp(("parallel", "parallel")),
    )(n2, wg)


def gu_bwd_x(dgu, wg, name):
    S = dgu.shape[1]
    tm = min(MM_ROWS, S)

    def body(d_ref, w_ref, o_ref):
        @pl.when(pl.program_id(1) == 0)
        def _():
            o_ref[...] = jnp.zeros_like(o_ref)
        o_ref[...] += _dg(d_ref[...], w_ref[...], 1, 1)

    return pl.pallas_call(
        body, name=name, grid=(S // tm, 4),
        in_specs=[pl.BlockSpec((None, tm, GU_SHARD), lambda m, s: (s // 2, m, s % 2)),
                  pl.BlockSpec((None, D, GU_SHARD), lambda m, s: (s, 0, 0))],
        out_specs=pl.BlockSpec((tm, D), lambda m, s: (m, 0)),
        out_shape=jax.ShapeDtypeStruct((S, D), f32), compiler_params=_cp(("parallel", "arbitrary")),
    )(dgu, wg)


def gu_bwd_w(n2, dgu, name):
    S = n2.shape[0]
    tm = min(512, S)
    nm = S // tm

    def body(a_ref, d_ref, o_ref, acc):
        @pl.when(pl.program_id(1) == 0)
        def _():
            acc[...] = jnp.zeros_like(acc)
        acc[...] += _dg(a_ref[...], d_ref[...], 0, 0)

        @pl.when(pl.program_id(1) == nm - 1)
        def _():
            o_ref[...] = acc[...].astype(bf16)

    return pl.pallas_call(
        body, name=name, grid=(4, nm),
        in_specs=[pl.BlockSpec((tm, D), lambda s, m: (m, 0)),
                  pl.BlockSpec((None, tm, GU_SHARD), lambda s, m: (s // 2, m, s % 2))],
        out_specs=pl.BlockSpec((None, D, GU_SHARD), lambda s, m: (s, 0, 0)),
        out_shape=jax.ShapeDtypeStruct((4, D, GU_SHARD), bf16),
        scratch_shapes=[pltpu.VMEM((D, GU_SHARD), f32)],
        compiler_params=_cp(("parallel", "arbitrary")),
    )(n2, dgu)


def _pair_cols(w):
    lead = w.shape[:-1]
    return w.reshape(lead + (2, 6, A_DH)).swapaxes(-3, -2).reshape(lead + (A_Q,))


def _unpair_cols(w):
    lead = w.shape[:-1]
    return w.reshape(lead + (6, 2, A_DH)).swapaxes(-3, -2).reshape(lead + (A_Q,))


def _lay_in_a(w):
    return jnp.concatenate([_pair_cols(w[:, :A_Q]), w[:, A_Q:]], axis=1)


def _unlay_in_a(w):
    return jnp.concatenate([_unpair_cols(w[:, :A_Q]), w[:, A_Q:]], axis=1)


def _lay_out_a(w):
    return jnp.concatenate([_pair_cols(w[:A_Q].T).T, w[A_Q:]], axis=0)


def _unlay_out_a(w):
    return jnp.concatenate([_unpair_cols(w[:A_Q].T).T, w[A_Q:]], axis=0)


def _lay_in_b(w):
    return jnp.concatenate([w[:, :2304], w[:, 2316:], w[:, 2304:2316],
                            jnp.zeros((w.shape[0], LANE - 12), w.dtype)], axis=1)


def _unlay_in_b(w):
    return jnp.concatenate([w[:, :2304], w[:, 2560:2572], w[:, 2304:2560]], axis=1)


def _chip_cols(w):
    return jnp.moveaxis(w.reshape(w.shape[0], 4, w.shape[1] // 4), 1, 0)


def _unchip_cols(w):
    return jnp.moveaxis(w, 0, 1).reshape(w.shape[1], 4 * w.shape[2])


def _local_step(x, mem, target, P):
    arrive = P.get("arrive", lambda key, after: None)
    ready = P.get("ready", lambda key, grads, dep: dep)
    sk = jnp.zeros((16, LANE), f32).at[:A_HEADS].set(jnp.broadcast_to(P["sinks"][:, None], (A_HEADS, LANE)))
    prm = jnp.zeros((8, LANE), f32).at[0, 6:12].set(P["a_log"]).at[1, 6:12].set(P["dt_bias"]).at[2].set(P["out_norm_g"])
    bias = bias_build(P["rel_bias"])
    saved = []
    h = x
    for i in range(2):
        n1 = rms_fwd(h, P["g_mix"][i], f"rms_mix{i}")
        arrive(("w_in", i), n1)
        proj = mm_nn(n1, P["w_in_a"] if i == 0 else P["w_in_b"], name="proj_a" if i == 0 else "proj_b")
        arrive(("w_mem", i), proj)
        kv = memkv_fwd(mem, P["g_mem"][i], P["w_mem"][i], f"memkv{i}")
        if i == 0:
            self_out = swa_fwd(proj, bias, sk)
            cross = xattn_fwd(proj, A_Q + 2 * LANE, kv, "xattn_a")
            extra = ()
        else:
            qkvn = dnprep_fwd(proj, P["conv_qkv"])
            chunked, inv = dnc_fwd(qkvn, proj, prm)
            o, states = dns_fwd(chunked)
            self_out = dnpost_fwd(o, proj, prm)
            cross = xattn_fwd(proj, 2304, kv, "xattn_b")
            extra = (qkvn, chunked, inv, states, o)
        mix = jnp.concatenate([self_out, cross], axis=1)
        arrive(("w_out", i), cross)
        h2 = mm_nn(mix, P["w_out"][i], res=h, name=f"out_proj{i}")
        n2 = rms_fwd(h2, P["g_ffn"][i], f"rms_ffn{i}")
        arrive(("w_gu", i), n2)
        gu = gu_fwd(n2, P["w_gu"][i], f"gate_up{i}")
        act = glu_fwd(gu, P["ffn_cw"][i], P["ffn_cb"][i], f"glu{i}")
        arrive(("w_down", i), act)
        h3 = mm_nn(act, P["w_down"][i], res=h2, name=f"down{i}")
        saved.append((h, n1, kv, proj, mix, h2, n2, gu, act, extra))
        h = h3

    loss, dh, dg_fin = loss_head(h, P["g_fin"], target)
    G = {"g_fin": dg_fin[0], "g_mix": [None, None], "g_mem": [None, None], "g_ffn": [None, None],
         "w_mem": [None, None], "w_out": [None, None], "w_gu": [None, None], "w_down": [None, None],
         "ffn_cw": [None, None], "ffn_cb": [None, None]}
    for i in (1, 0):
        hin, n1, kv, proj, mix, h2, n2, gu, act, extra = saved[i]
        dact = mm_nt(dh, P["w_down"][i], name=f"d_act{i}")
        G["w_down"][i] = mm_tn(act, dh, name=f"dw_down{i}")
        dgu, dcw, dcb = glu_bwd(gu, P["ffn_cw"][i], P["ffn_cb"][i], dact, f"glu_bwd{i}")
        G["ffn_cw"][i], G["ffn_cb"][i] = dcw, dcb[0]
        dn2 = gu_bwd_x(dgu, P["w_gu"][i], f"d_n2_{i}")
        G["w_gu"][i] = gu_bwd_w(n2, dgu, f"dw_gu{i}")
        g_ffn = ready(("ffn", i), G, P["g_ffn"][i])
        dh2, dg = rms_bwd(h2, g_ffn, dn2, dh, f"rms_ffn_bwd{i}")
        G["g_ffn"][i] = dg[0]
        dmix = mm_nt(dh2, P["w_out"][i], name=f"d_mix{i}")
        G["w_out"][i] = mm_tn(mix, dh2, name=f"dw_out{i}")
        if i == 0:
            dqkv, dbias, dsk = swa_bwd(proj, bias, sk, dmix)
            dxq, dkv = xattn_bwd(proj, A_Q + 2 * LANE, kv, dmix, "xattn_a_bwd")
            dproj = jnp.concatenate([dqkv, dxq], axis=1)
            G["sinks"] = dsk[:A_HEADS, 0]
            G["rel_bias"] = bias_grad(dbias)[:, :A_HEADS]
            w_in, gname = P["w_in_a"], "w_in_a"
        else:
            qkvn, chunked, inv, states, o = extra
            do, dz, dgo = dnpost_bwd(o, proj, prm, dmix)
            dqkvn, dseg, dprm = dnc_bwd(qkvn, proj, prm, inv, dns_bwd(chunked, states, do))
            draw, dconv = dnprep_bwd(proj, P["conv_qkv"], dqkvn)
            dxq, dkv = xattn_bwd(proj, 2304, kv, dmix, "xattn_b_bwd")
            dproj = jnp.concatenate([draw, dz, dxq, dseg], axis=1)
            G["conv_qkv"] = dconv
            G["a_log"], G["dt_bias"], G["out_norm_g"] = dprm[0, 6:12], dprm[1, 6:12], dgo[0]
            w_in, gname = P["w_in_b"], "w_in_b"
        dn1 = mm_nt(dproj, w_in, name=f"d_n1_{i}")
        G[gname] = mm_tn(n1, dproj, name=f"d{gname}")
        dh, dg = rms_bwd(hin, P["g_mix"][i], dn1, dh2, f"rms_mix_bwd{i}")
        G["g_mix"][i] = dg[0]
        dgm, dwm = memkv_bwd(mem, P["g_mem"][i], P["w_mem"][i], dkv, f"memkv_bwd{i}")
        G["g_mem"][i], G["w_mem"][i] = dgm[0], dwm
        ready(("mix", i), G, None)
    return loss, dh, G


def _prepare(full, w_gu=None):
    return {
        "rel_bias": full["rel_bias"], "sinks": full["sinks_a"][0], "a_log": full["a_log_b"][0],
        "dt_bias": full["dt_bias_b"][0], "out_norm_g": full["out_norm_g_b"][0],
        "g_mix": full["norm_mix_g"], "g_mem": full["norm_mem_g"], "g_ffn": full["norm_ffn_g"],
        "g_fin": full["final_norm_g"], "conv_qkv": full["conv_qkv_b"][0],
        "ffn_cw": [full["ffn_conv_w"][0], full["ffn_conv_w"][1]],
        "ffn_cb": [full["ffn_conv_b"][0], full["ffn_conv_b"][1]],
        "w_mem": [full["w_mem_kv"][0], full["w_mem_kv"][1]],
        "w_out": [_lay_out_a(full["w_out"][0]), full["w_out"][1]],
        "w_in_a": _lay_in_a(full["w_in_a"][0]), "w_in_b": _lay_in_b(full["w_in_b"][0]),
        "w_gu": w_gu if w_gu is not None else [_chip_cols(full["w_gate_up"][0]), _chip_cols(full["w_gate_up"][1])],
        "w_down": [full["w_down"][0], full["w_down"][1]],
    }


def _grads_to_ref(G):
    return {
        "rel_bias": G["rel_bias"], "norm_mix_g": jnp.stack(G["g_mix"]), "norm_mem_g": jnp.stack(G["g_mem"]),
        "w_mem_kv": jnp.stack(G["w_mem"]),
        "w_out": jnp.stack([_unlay_out_a(G["w_out"][0]), G["w_out"][1]]),
        "w_in_a": _unlay_in_a(G["w_in_a"])[None], "sinks_a": G["sinks"][None],
        "w_in_b": _unlay_in_b(G["w_in_b"])[None], "conv_qkv_b": G["conv_qkv"][None],
        "a_log_b": G["a_log"][None], "dt_bias_b": G["dt_bias"][None], "out_norm_g_b": G["out_norm_g"][None],
        "norm_ffn_g": jnp.stack(G["g_ffn"]),
        "w_gate_up": jnp.stack([_unchip_cols(G["w_gu"][0]), _unchip_cols(G["w_gu"][1])]).astype(f32),
        "ffn_conv_w": jnp.stack(G["ffn_cw"]), "ffn_conv_b": jnp.stack(G["ffn_cb"]),
        "w_down": jnp.stack(G["w_down"]), "final_norm_g": G["g_fin"],
    }


ANY = pl.BlockSpec(memory_space=pl.ANY)


def _place():
    return lax.axis_index("x"), lax.axis_index("y"), lax.axis_index("c")


def chip_scatter(gs):
    n = len(gs)

    def body(*refs):
        ins, outs = refs[:n], refs[n:2 * n]
        ssem, rsem = refs[2 * n:]
        x, y, c = _place()
        me = 2 * x + y
        peers = [(1 - x, y), (x, 1 - y), (1 - x, 1 - y)]

        def remote(j, k, slot):
            px, py = peers[k]
            return pltpu.make_async_remote_copy(
                src_ref=ins[j].at[2 * px + py], dst_ref=outs[j].at[slot],
                send_sem=ssem.at[3 * j + k], recv_sem=rsem.at[3 * j + k],
                device_id=(px, py, c), device_id_type=MESH)

        sends = [remote(j, k, me) for j in range(n) for k in range(3)]
        for cp in sends:
            cp.start()
        for j in range(n):
            for k in range(3):
                px, py = peers[k]
                remote(j, k, 2 * px + py).wait_recv()
        for cp in sends:
            cp.wait_send()

    return pl.pallas_call(
        body, name="grad_scatter", in_specs=[ANY] * n, out_specs=[ANY] * n,
        out_shape=[jax.ShapeDtypeStruct(g.shape, g.dtype) for g in gs],
        scratch_shapes=[pltpu.SemaphoreType.DMA((3 * n,)), pltpu.SemaphoreType.DMA((3 * n,))],
    )(*gs)


def allreduce_small(buf):
    R = buf.shape[0]

    def body(b_ref, o_ref, recv, ssem, rsem):
        x, y, c = _place()
        me = 4 * x + 2 * y + c

        def peer(k):
            return (1 - x if k & 4 else x, 1 - y if k & 2 else y, 1 - c if k & 1 else c)

        def remote(k, slot):
            return pltpu.make_async_remote_copy(
                src_ref=b_ref, dst_ref=recv.at[slot], send_sem=ssem.at[k - 1], recv_sem=rsem.at[k - 1],
                device_id=peer(k), device_id_type=MESH)

        sends = [remote(k, me) for k in range(1, 8)]
        for cp in sends:
            cp.start()
        recv[me] = b_ref[...]
        for k in range(1, 8):
            px, py, pc = peer(k)
            remote(k, 4 * px + 2 * py + pc).wait_recv()
        for cp in sends:
            cp.wait_send()
        total = recv[0]
        for j in range(1, 8):
            total = total + recv[j]
        o_ref[...] = total

    return pl.pallas_call(
        body, name="small_allreduce",
        in_specs=[pl.BlockSpec(memory_space=pltpu.VMEM)], out_specs=pl.BlockSpec(memory_space=pltpu.VMEM),
        out_shape=jax.ShapeDtypeStruct(buf.shape, f32),
        scratch_shapes=[pltpu.VMEM((8, R, LANE), f32), pltpu.SemaphoreType.DMA((7,)), pltpu.SemaphoreType.DMA((7,))],
    )(buf)


def sum_slots(own, recv, chip, core, name):
    _, R, C = recv.shape
    tr = _row_tile(R, 256)
    nt = R // tr

    def body(p_ref, a_ref, r_ref, o_ref):
        acc = jnp.zeros((tr, C), f32)
        for s in range(4):
            acc = acc + jnp.where(p_ref[0] == s, a_ref[s], r_ref[s]).astype(f32)
        o_ref[...] = acc

    slots = pl.BlockSpec((4, tr, C), lambda i, p_ref: (0, i, 0))
    return pl.pallas_call(
        body, name=name, out_shape=jax.ShapeDtypeStruct((2 * R, C), f32),
        grid_spec=pltpu.PrefetchScalarGridSpec(
            num_scalar_prefetch=1, grid=(nt,), in_specs=[slots, slots],
            out_specs=pl.BlockSpec((tr, C), lambda i, p_ref: (p_ref[1] * nt + i, 0))),
        compiler_params=_cp(("parallel",)),
    )(jnp.stack([chip, core]).astype(jnp.int32), own, recv)


def _half(ref, core, axis=0):
    half = ref.shape[axis] // 2
    idx = (slice(None),) * axis + (pl.ds(core * half, half),)
    return ref.at[idx]


IN_HBM = pl.BlockSpec(memory_space=pltpu.HBM)
IN_SEM = pl.BlockSpec(memory_space=pltpu.SEMAPHORE)
SIDE_EFFECT = pltpu.SideEffectType.DATAFLOW_SIDE_EFFECTING


def _gather_copy(buf, i, k, ssem, rsem, place, landing):
    x, y, c = place
    px, py = [(1 - x, y), (x, 1 - y), (1 - x, 1 - y)][k]
    me = 2 * x + y
    return pltpu.make_async_remote_copy(
        src_ref=buf.at[me], dst_ref=buf.at[me if landing == "theirs" else 2 * px + py],
        send_sem=ssem.at[3 * i + k], recv_sem=rsem.at[3 * i + k], device_id=(px, py, c), device_id_type=MESH)


def gather_start(groups):
    flat = [b for grp in groups for b in grp]
    n, ng = len(flat), len(groups)

    def body(*refs):
        bufs, sems = refs[:n], refs[n:n + 2 * ng]
        place = _place()
        j = 0
        for g, grp in enumerate(groups):
            for i in range(len(grp)):
                for k in range(3):
                    _gather_copy(bufs[j], i, k, sems[2 * g], sems[2 * g + 1], place, "theirs").start()
                j += 1

    sem_shapes = [pltpu.SemaphoreType.DMA((3 * len(grp),)) for grp in groups for _ in range(2)]
    out = pl.pallas_call(
        body, name="gather_start", in_specs=[IN_HBM] * n, out_specs=(*[IN_SEM] * (2 * ng), *[IN_HBM] * n),
        out_shape=(*sem_shapes, *[pltpu.HBM(b.shape, b.dtype) for b in flat]),
        input_output_aliases={i: 2 * ng + i for i in range(n)},
        compiler_params=pltpu.CompilerParams(has_side_effects=SIDE_EFFECT),
    )(*[pltpu.with_memory_space_constraint(b, pltpu.HBM) for b in flat])
    sems, bufs = out[:2 * ng], list(out[2 * ng:])
    flights, j = [], 0
    for g, grp in enumerate(groups):
        flights.append((bufs[j:j + len(grp)], sems[2 * g], sems[2 * g + 1]))
        j += len(grp)
    return flights


def gather_wait(flight, after, name):
    bufs, ssem, rsem = flight
    n = len(bufs)

    def body(*refs):
        place = _place()
        for i in range(n):
            for k in range(3):
                cp = _gather_copy(refs[i], i, k, refs[n], refs[n + 1], place, "mine")
                cp.wait_send()
                cp.wait_recv()

    return pl.pallas_call(
        body, name=name, in_specs=[IN_HBM] * n + [IN_SEM, IN_SEM, ANY], out_specs=[IN_HBM] * n,
        out_shape=[pltpu.HBM(b.shape, b.dtype) for b in bufs], input_output_aliases={i: i for i in range(n)},
        compiler_params=pltpu.CompilerParams(has_side_effects=SIDE_EFFECT),
    )(*bufs, ssem, rsem, after)


def _scatter_copy(src, land, j, k, ssem, rsem, place, landing):
    x, y, c = place
    px, py = [(1 - x, y), (x, 1 - y), (1 - x, 1 - y)][k]
    return pltpu.make_async_remote_copy(
        src_ref=src.at[2 * px + py], dst_ref=land.at[2 * x + y if landing == "theirs" else 2 * px + py],
        send_sem=ssem.at[3 * j + k], recv_sem=rsem.at[3 * j + k], device_id=(px, py, c), device_id_type=MESH)


def scatter_start(srcs, name):
    n = len(srcs)
    lands = [lax.empty(g.shape, g.dtype) for g in srcs]

    def body(*refs):
        place = _place()
        for j in range(n):
            for k in range(3):
                _scatter_copy(refs[j], refs[n + j], j, k, refs[2 * n], refs[2 * n + 1], place, "theirs").start()
        refs[-1][...] = jnp.zeros_like(refs[-1])

    sem = pltpu.SemaphoreType.DMA((3 * n,))
    hbm = [pltpu.with_memory_space_constraint(b, pltpu.HBM) for b in list(srcs) + lands]
    out = pl.pallas_call(
        body, name=name, in_specs=[IN_HBM] * (2 * n),
        out_specs=(IN_SEM, IN_SEM, *[IN_HBM] * (2 * n), pl.BlockSpec(memory_space=pltpu.VMEM)),
        out_shape=(sem, sem, *[pltpu.HBM(b.shape, b.dtype) for b in hbm], jax.ShapeDtypeStruct((8, LANE), f32)),
        input_output_aliases={i: 2 + i for i in range(2 * n)},
        compiler_params=pltpu.CompilerParams(has_side_effects=SIDE_EFFECT),
    )(*hbm)
    return (list(out[2:2 + n]), list(out[2 + n:2 + 2 * n]), out[0], out[1]), out[-1]


def scatter_wait(flight, after, name):
    srcs, lands, ssem, rsem = flight
    n = len(srcs)

    def body(*refs):
        place = _place()
        for j in range(n):
            for k in range(3):
                cp = _scatter_copy(refs[j], refs[n + j], j, k, refs[2 * n], refs[2 * n + 1], place, "mine")
                cp.wait_send()
                cp.wait_recv()

    out = pl.pallas_call(
        body, name=name, in_specs=[IN_HBM] * (2 * n) + [IN_SEM, IN_SEM, ANY], out_specs=[IN_HBM] * (2 * n),
        out_shape=[pltpu.HBM(b.shape, b.dtype) for b in list(srcs) + list(lands)],
        input_output_aliases={i: i for i in range(2 * n)},
        compiler_params=pltpu.CompilerParams(has_side_effects=SIDE_EFFECT),
    )(*srcs, *lands, ssem, rsem, after)
    return list(out[:n]), list(out[n:])


def pair_exchange(gbufs, name):
    n = len(gbufs)

    def body(*refs):
        ins, outs = refs[:n], refs[n:2 * n]
        ssem, rsem = refs[2 * n:]
        x, y, c = _place()
        cps = [pltpu.make_async_remote_copy(
            src_ref=_half(ins[j], 1 - c, axis=1), dst_ref=outs[j], send_sem=ssem.at[j], recv_sem=rsem.at[j],
            device_id=(x, y, 1 - c), device_id_type=MESH) for j in range(n)]
        for cp in cps:
            cp.start()
        for cp in cps:
            cp.wait()

    return pl.pallas_call(
        body, name=name, in_specs=[ANY] * n, out_specs=[ANY] * n,
        out_shape=[jax.ShapeDtypeStruct((4, g.shape[1] // 2, g.shape[2]), g.dtype) for g in gbufs],
        scratch_shapes=[pltpu.SemaphoreType.DMA((n,)), pltpu.SemaphoreType.DMA((n,))],
    )(*gbufs)


def _row_tile(rows, cap=512):
    return max(t for t in range(16, min(rows, cap) + 1, 16) if rows % t == 0)


def pair_sum(mine, theirs, core, name):
    _, R, C = mine.shape
    half = R // 2
    tr = _row_tile(half)
    nt = half // tr

    def body(c_ref, a_ref, b_ref, o_ref):
        o_ref[...] = (a_ref[...].astype(f32) + b_ref[...].astype(f32)).astype(bf16)

    return pl.pallas_call(
        body, name=name, out_shape=jax.ShapeDtypeStruct(theirs.shape, bf16),
        grid_spec=pltpu.PrefetchScalarGridSpec(
            num_scalar_prefetch=1, grid=(4, nt),
            in_specs=[pl.BlockSpec((None, tr, C), lambda s, i, c_ref: (s, c_ref[0] * nt + i, 0)),
                      pl.BlockSpec((None, tr, C), lambda s, i, c_ref: (s, i, 0))],
            out_specs=pl.BlockSpec((None, tr, C), lambda s, i, c_ref: (s, i, 0))),
        compiler_params=_cp(("parallel", "parallel")),
    )(jnp.reshape(core, (1,)).astype(jnp.int32), mine, theirs)


def final_exchange(fins):
    n = len(fins)

    def body(*refs):
        outs = refs[n:2 * n]
        ssem, rsem = refs[2 * n:]
        x, y, c = _place()
        cps = [pltpu.make_async_remote_copy(
            src_ref=_half(outs[j], c), dst_ref=_half(outs[j], c), send_sem=ssem.at[j], recv_sem=rsem.at[j],
            device_id=(x, y, 1 - c), device_id_type=MESH) for j in range(n)]
        for cp in cps:
            cp.start()
        for cp in cps:
            cp.wait()

    return pl.pallas_call(
        body, name="final_exchange", in_specs=[ANY] * n, out_specs=[ANY] * n,
        out_shape=[jax.ShapeDtypeStruct(f.shape, f.dtype) for f in fins],
        input_output_aliases={j: j for j in range(n)},
        scratch_shapes=[pltpu.SemaphoreType.DMA((n,)), pltpu.SemaphoreType.DMA((n,))],
    )(*fins)


def adamw_big(w, m, v, gs, row0, name):
    L, R, C = w.shape
    tr = _row_tile(math.gcd(R, row0) if row0 else R, max(16, 262144 // C // 16 * 16))
    b0 = row0 // tr

    def body(*refs):
        w_ref, m_ref, v_ref = refs[:3]
        g_refs = refs[3:3 + L]
        g_ref, d_ref, nm_ref, nv_ref = refs[3 + L:]
        g = g_refs[0][...]
        for l in range(1, L):
            g = jnp.where(pl.program_id(0) == l, g_refs[l][...], g)
        d, nm, nv = _adamw_math(w_ref[...], g, m_ref[...], v_ref[...])
        g_ref[...] = g
        d_ref[...] = d
        nm_ref[...] = nm
        nv_ref[...] = nv

    own = pl.BlockSpec((None, tr, C), lambda l, i: (l, i, 0))
    off = pl.BlockSpec((tr, C), lambda l, i: (b0 + i, 0))
    return pl.pallas_call(
        body, name=name, grid=(L, R // tr), in_specs=[own, own, own] + [off] * L, out_specs=[own] * 4,
        out_shape=[jax.ShapeDtypeStruct((L, R, C), f32)] * 4, compiler_params=_cp(("parallel", "parallel")),
    )(w, m, v, *gs)


def _adamw_math(w, g, m, v):
    m = B1 * m + (1.0 - B1) * g
    v = B2 * v + (1.0 - B2) * (g * g)
    m_hat = m / (1.0 - B1 ** STEP)
    v_hat = v / (1.0 - B2 ** STEP)
    delta = -LR * (m_hat / (jnp.sqrt(v_hat) + AEPS) + WD * w)
    return delta, m, v


def adamw_small(w, m, v, g):
    def body(w_ref, m_ref, v_ref, g_ref, d_ref, nm_ref, nv_ref):
        d, nm, nv = _adamw_math(w_ref[...], g_ref[...], m_ref[...], v_ref[...])
        d_ref[...] = d
        nm_ref[...] = nm
        nv_ref[...] = nv

    return pl.pallas_call(body, name="adamw_small", out_shape=[jax.ShapeDtypeStruct(w.shape, f32)] * 3)(w, m, v, g)


CONV =(("conv_qkv_b", 2), ("ffn_conv_w", 2))
SMALL = ("rel_bias", "norm_mix_g", "norm_mem_g", "sinks_a", "a_log_b", "dt_bias_b", "out_norm_g_b", "norm_ffn_g",
         "ffn_conv_b", "final_norm_g")
WEIGHTS = ("rel_bias", "norm_mix_g", "norm_mem_g", "w_mem_kv", "w_out", "w_in_a", "sinks_a", "w_in_b", "conv_qkv_b",
           "a_log_b", "dt_bias_b", "out_norm_g_b", "norm_ffn_g", "w_gate_up", "ffn_conv_w", "ffn_conv_b", "w_down",
           "final_norm_g")
ARGS = ("x", "mem") + WEIGHTS + ("loss_target",) + tuple("m_" + n for n in WEIGHTS) + tuple("v_" + n for n in WEIGHTS)


def _rows(a, width):
    flat = a.reshape(-1)
    pad = (-flat.shape[0]) % (8 * width)
    if pad:
        flat = jnp.concatenate([flat, jnp.zeros((pad,), a.dtype)])
    return flat.reshape(-1, width)


def _nrows(shape, width):
    return _pad_to(-(-math.prod(shape) // width), 8)


def _pack(arrs, width, total_rows, dtype):
    parts = [_rows(a.astype(dtype), width) for a in arrs]
    used = sum(p.shape[0] for p in parts)
    if total_rows > used:
        parts.append(jnp.zeros((total_rows - used, width), dtype))
    return jnp.concatenate(parts, axis=0)


def _unpack(buf, shapes, width):
    out, r = [], 0
    for s in shapes:
        n = _nrows(s, width)
        out.append(buf[r:r + n].reshape(-1)[:math.prod(s)].reshape(s))
        r += n
    return out


def _pad_to(n, mult):
    return -(-n // mult) * mult


def kernel(x, mem, rel_bias, norm_mix_g, norm_mem_g, w_mem_kv, w_out, w_in_a, sinks_a, w_in_b, conv_qkv_b, a_log_b, dt_bias_b, out_norm_g_b, norm_ffn_g, w_gate_up, ffn_conv_w, ffn_conv_b, w_down, final_norm_g, loss_target, m_rel_bias, m_norm_mix_g, m_norm_mem_g, m_w_mem_kv, m_w_out, m_w_in_a, m_sinks_a, m_w_in_b, m_conv_qkv_b, m_a_log_b, m_dt_bias_b, m_out_norm_g_b, m_norm_ffn_g, m_w_gate_up, m_ffn_conv_w, m_ffn_conv_b, m_w_down, m_final_norm_g, v_rel_bias, v_norm_mix_g, v_norm_mem_g, v_w_mem_kv, v_w_out, v_w_in_a, v_sinks_a, v_w_in_b, v_conv_qkv_b, v_a_log_b, v_dt_bias_b, v_out_norm_g_b, v_norm_ffn_g, v_w_gate_up, v_ffn_conv_w, v_ffn_conv_b, v_w_down, v_final_norm_g):
    A = dict(zip(ARGS, (x, mem, rel_bias, norm_mix_g, norm_mem_g, w_mem_kv, w_out, w_in_a, sinks_a, w_in_b, conv_qkv_b, a_log_b, dt_bias_b, out_norm_g_b, norm_ffn_g, w_gate_up, ffn_conv_w, ffn_conv_b, w_down, final_norm_g, loss_target, m_rel_bias, m_norm_mix_g, m_norm_mem_g, m_w_mem_kv, m_w_out, m_w_in_a, m_sinks_a, m_w_in_b, m_conv_qkv_b, m_a_log_b, m_dt_bias_b, m_out_norm_g_b, m_norm_ffn_g, m_w_gate_up, m_ffn_conv_w, m_ffn_conv_b, m_w_down, m_final_norm_g, v_rel_bias, v_norm_mix_g, v_norm_mem_g, v_w_mem_kv, v_w_out, v_w_in_a, v_sinks_a, v_w_in_b, v_conv_qkv_b, v_a_log_b, v_dt_bias_b, v_out_norm_g_b, v_norm_ffn_g, v_w_gate_up, v_ffn_conv_w, v_ffn_conv_b, v_w_down, v_final_norm_g)))
    chip = 2 * lax.axis_index("x") + lax.axis_index("y")
    core = lax.axis_index("c")
    n_down, n_out, n_mem = w_down.shape[1], w_out.shape[1], w_mem_kv.shape[1]

    def own_slot(shard):
        return lax.dynamic_update_index_in_dim(lax.empty((4,) + shard.shape, shard.dtype), shard, chip, 0)

    def bslot(w):
        return own_slot(w.astype(bf16))

    groups = {
        ("w_in", 0): [bslot(w_in_a[0])],
        ("w_mem", 0): [bslot(w_mem_kv[0]), bslot(w_mem_kv[1]), own_slot(conv_qkv_b[0]),
                       own_slot(ffn_conv_w.reshape(6, -1))],
        ("w_out", 0): [bslot(w_out[0])], ("w_gu", 0): [bslot(w_gate_up[0])], ("w_down", 0): [bslot(w_down[0])],
        ("w_in", 1): [bslot(w_in_b[0])],
        ("w_out", 1): [bslot(w_out[1])], ("w_gu", 1): [bslot(w_gate_up[1])], ("w_down", 1): [bslot(w_down[1])],
    }
    flights = dict(zip(groups, gather_start(list(groups.values()))))
    P = {"rel_bias": rel_bias, "sinks": sinks_a[0], "a_log": a_log_b[0], "dt_bias": dt_bias_b[0],
         "out_norm_g": out_norm_g_b[0], "g_mix": norm_mix_g, "g_mem": norm_mem_g, "g_ffn": norm_ffn_g,
         "g_fin": final_norm_g, "ffn_cb": [ffn_conv_b[0], ffn_conv_b[1]], "w_mem": [None, None], "w_out": [None, None],
         "w_gu": [None, None], "w_down": [None, None], "ffn_cw": [None, None]}

    def rows4(g):
        return g.reshape(4 * g.shape[1], g.shape[2])

    def arrive(key, after):
        if key not in flights:
            return
        got = gather_wait(flights.pop(key), after, "gather_wait_%s%d" % key)
        name, i = key
        if name == "w_in":
            P["w_in_a" if i == 0 else "w_in_b"] = (_lay_in_a if i == 0 else _lay_in_b)(_unchip_cols(got[0]))
        elif name == "w_mem":
            P["w_mem"] = [rows4(got[0]), rows4(got[1])]
            P["conv_qkv"] = _unchip_cols(got[2])
            cw = _unchip_cols(got[3]).reshape(2, 3, D_FF)
            P["ffn_cw"] = [cw[0], cw[1]]
        elif name == "w_out":
            P["w_out"][i] = _lay_out_a(rows4(got[0])) if i == 0 else rows4(got[0])
        elif name == "w_gu":
            P["w_gu"][i] = got[0]
        else:
            P["w_down"][i] = rows4(got[0])

    def chip_rows(g):
        return g.reshape(4, g.shape[0] // 4, g.shape[-1])

    sent, started = {}, []

    def ready(key, G, dep):
        kind, i = key
        tag = "%s%d" % key
        if kind == "ffn":
            names, partial = ("gu", "down"), [G["w_gu"][i], chip_rows(G["w_down"][i]).astype(bf16)]
        else:
            g_out = _unlay_out_a(G["w_out"][0]) if i == 0 else G["w_out"][1]
            g_in = _unlay_in_a(G["w_in_a"]) if i == 0 else _unlay_in_b(G["w_in_b"])
            names = ("out", "in", "mem")
            partial = [chip_rows(g_out).astype(bf16), _chip_cols(g_in).astype(bf16), chip_rows(G["w_mem"][i]).astype(bf16)]
        theirs = pair_exchange(partial, "pair_exchange_" + tag)
        pair = [pair_sum(p, t, core, "pair_sum_%s%d" % (nm, i)) for p, t, nm in zip(partial, theirs, names)]
        if key == ("mix", 0):
            sent[key] = (names, pair, chip_scatter(pair))
            return dep
        flight, token = scatter_start(pair, "scatter_start_" + tag)
        sent[key] = (names, flight)
        started.append(token[0, 0])
        if dep is not None:
            while started:
                dep = dep + started.pop()
        return dep

    P["arrive"], P["ready"] = arrive, ready

    loss, dx, G = _local_step(x[0], mem[0], loss_target[0], P)
    gfull = _grads_to_ref(G)

    fin = {}
    for key in (("ffn", 1), ("mix", 1), ("ffn", 0), ("mix", 0)):
        if key == ("mix", 0):
            names, pair, arrived = sent[key]
        else:
            names, flight = sent[key]
            pair, arrived = scatter_wait(flight, dx, "scatter_wait_%s%d" % key)
        for nm, p, r in zip(names, pair, arrived):
            fin[nm, key[1]] = sum_slots(p, r, chip, core, "sum_slots_%s%d" % (nm, key[1]))
    order = list(fin)
    done = dict(zip(order, final_exchange([fin[k] for k in order])))

    sm_shapes = [A[n].shape for n in SMALL] + [gfull[n].shape for n, _ in CONV] + [(LANE,)]
    sm_rows = _pad_to(sum(_nrows(s, LANE) for s in sm_shapes), 8)
    sbuf = _pack([gfull[n] for n in SMALL] + [gfull[n] for n, _ in CONV] + [loss[0]], LANE, sm_rows, f32)
    tot = _unpack(allreduce_small(sbuf), sm_shapes, LANE)
    gsmall = dict(zip(SMALL, tot[:len(SMALL)]))
    for (n, axis), t in zip(CONV, tot[len(SMALL):len(SMALL) + len(CONV)]):
        sh = A[n].shape[axis]
        gsmall[n] = lax.dynamic_slice_in_dim(t, chip * sh, sh, axis)
    loss_out = tot[-1][0]

    out = {}
    plan = (("w_gate_up", [done["gu", 0], done["gu", 1]]), ("w_down", [done["down", 0], done["down", 1]]),
            ("w_out", [done["out", 0], done["out", 1]]), ("w_mem_kv", [done["mem", 0], done["mem", 1]]),
            ("w_in_a", [done["in", 0]]), ("w_in_b", [done["in", 1]]))
    for n, gs in plan:
        shape3 = (len(gs),) + gs[0].shape
        res = adamw_big(A[n].reshape(shape3), A["m_" + n].reshape(shape3), A["v_" + n].reshape(shape3), gs, 0,
                        "adamw_" + n)
        for key, r in zip(("grad_", "delta_", "new_m_", "new_v_"), res):
            out[key + n] = r.reshape(A[n].shape)
    names = SMALL + tuple(n for n, _ in CONV)
    shapes = [A[n].shape for n in names]
    rows = _pad_to(sum(_nrows(s, LANE) for s in shapes), 8)
    packs = [_pack([src[n] for n in names], LANE, rows, f32)
             for src in ({n: A[n] for n in names}, {n: A["m_" + n] for n in names}, {n: A["v_" + n] for n in names}, gsmall)]
    res = adamw_small(*packs)
    for key, r in zip(("delta_", "new_m_", "new_v_"), res):
        for n, a in zip(names, _unpack(r, shapes, LANE)):
            out[key + n] = a
    for n in names:
        out["grad_" + n] = gsmall[n]
    return (loss_out, dx[None], *[out["grad_" + n] for n in WEIGHTS], *[out["delta_" + n] for n in WEIGHTS],
            *[out["new_m_" + n] for n in WEIGHTS], *[out["new_v_" + n] for n in WEIGHTS])
```

```python
import functools
import math

import numpy as np
import jax
import jax.numpy as jnp
from jax import lax
from jax.experimental import pallas as pl
from jax.experimental.pallas import tpu as pltpu

f32 = jnp.float32
bf16 = jnp.bfloat16
HI = lax.Precision.HIGHEST
MESH = pl.DeviceIdType.MESH

D = 1024
MEM_LEN = 256
EPS = 1e-6
A_HEADS, A_KV, A_DH = 12, 2, 64
A_Q = 768
BLK = 128
N_BUCKETS, MAX_DIST = 32, 128
B_QK, B_V, B_DH = 384, 768, 128
B_QKV = 1536
CHUNK = 64
X_Q = 256
D_FF = 2816
IN_A = 1280
IN_B = 2572
IN_B_PAD = 2688
LANE = 128
VMEM_LIMIT = 56 * 1024 * 1024
MM_ROWS = 1024

LR, B1, B2, AEPS, WD, STEP = 0.001, 0.9, 0.999, 1e-08, 0.01, 10


def _cp(sem=None):
    return pltpu.CompilerParams(dimension_semantics=sem, vmem_limit_bytes=VMEM_LIMIT)


def _dg(a, b, ca, cb, prec=None):
    return lax.dot_general(a, b, (((ca,), (cb,)), ((), ())), precision=prec, preferred_element_type=f32)


@jax.custom_vjp
def bdot(a, b):
    return _dg(a.astype(bf16), b.astype(bf16), 1, 0)


def _bdot_f(a, b):
    return bdot(a, b), (a, b)


def _bdot_b(res, g):
    a, b = res
    gb = g.astype(bf16)
    return _dg(gb, b.astype(bf16), 1, 1), _dg(a.astype(bf16), gb, 0, 0)


bdot.defvjp(_bdot_f, _bdot_b)


@jax.custom_vjp
def bdot_nt(a, b):
    return _dg(a.astype(bf16), b.astype(bf16), 1, 1)


def _bdot_nt_f(a, b):
    return bdot_nt(a, b), (a, b)


def _bdot_nt_b(res, g):
    a, b = res
    gb = g.astype(bf16)
    return _dg(gb, b.astype(bf16), 1, 0), _dg(gb, a.astype(bf16), 0, 0)


bdot_nt.defvjp(_bdot_nt_f, _bdot_nt_b)


def _shift_rows(x, s, down):
    n = x.shape[0]
    row = lax.broadcasted_iota(jnp.int32, x.shape, 0)
    if down:
        return jnp.where(row >= s, pltpu.roll(x, s, 0), 0.0)
    return jnp.where(row < n - s, pltpu.roll(x, n - s, 0), 0.0)


@functools.partial(jax.custom_vjp, nondiff_argnums=(1,))
def shift_down(x, s):
    return _shift_rows(x, s, True)


def _sd_f(x, s):
    return _shift_rows(x, s, True), None


def _sd_b(s, _, g):
    return (_shift_rows(g, s, False),)


shift_down.defvjp(_sd_f, _sd_b)


def _sigmoid(x):
    return 1.0 / (1.0 + jnp.exp(-x))


def _silu(x):
    return x * _sigmoid(x)


def _rms(x, g):
    return x * lax.rsqrt(jnp.mean(x * x, axis=-1, keepdims=True) + EPS) * g


def _tile(n, cap):
    u = n // LANE
    best = 1
    for d in range(1, u + 1):
        if u % d == 0 and d * LANE <= cap:
            best = d
    return best * LANE


def mm_nn(a, w, res=None, out_dtype=f32, name="mm_nn"):
    M, K = a.shape
    N = w.shape[1]
    tm, tn = min(MM_ROWS, M), _tile(N, 1024)

    def body(*refs):
        if res is None:
            a_ref, w_ref, o_ref = refs
            o_ref[...] = _dg(a_ref[...].astype(bf16), w_ref[...], 1, 0).astype(out_dtype)
        else:
            a_ref, w_ref, r_ref, o_ref = refs
            o_ref[...] = (r_ref[...] + _dg(a_ref[...].astype(bf16), w_ref[...], 1, 0)).astype(out_dtype)

    in_specs = [pl.BlockSpec((tm, K), lambda n, m: (m, 0)), pl.BlockSpec((K, tn), lambda n, m: (0, n))]
    args = [a, w]
    if res is not None:
        in_specs.append(pl.BlockSpec((tm, tn), lambda n, m: (m, n)))
        args.append(res)
    return pl.pallas_call(
        body, name=name, grid=(N // tn, M // tm), in_specs=in_specs,
        out_specs=pl.BlockSpec((tm, tn), lambda n, m: (m, n)),
        out_shape=jax.ShapeDtypeStruct((M, N), out_dtype),
        compiler_params=_cp(("parallel", "parallel")),
    )(*args)


def mm_nt(dy, w, name="mm_nt"):
    M, N = dy.shape
    K = w.shape[0]
    tm, tn = min(MM_ROWS, M), _tile(N, 1024)

    def body(dy_ref, w_ref, o_ref):
        @pl.when(pl.program_id(1) == 0)
        def _():
            o_ref[...] = jnp.zeros_like(o_ref)
        o_ref[...] += _dg(dy_ref[...].astype(bf16), w_ref[...], 1, 1)

    return pl.pallas_call(
        body, name=name, grid=(M // tm, N // tn),
        in_specs=[pl.BlockSpec((tm, tn), lambda m, n: (m, n)), pl.BlockSpec((K, tn), lambda m, n: (0, n))],
        out_specs=pl.BlockSpec((tm, K), lambda m, n: (m, 0)),
        out_shape=jax.ShapeDtypeStruct((M, K), f32),
        compiler_params=_cp(("parallel", "arbitrary")),
    )(dy, w)


def mm_tn(a, dy, name="mm_tn"):
    M, K = a.shape
    N = dy.shape[1]
    tm, tk, tn = min(512, M), _tile(K, 1408), _tile(N, 1024)

    def body(a_ref, dy_ref, o_ref):
        @pl.when(pl.program_id(2) == 0)
        def _():
            o_ref[...] = jnp.zeros_like(o_ref)
        o_ref[...] += _dg(a_ref[...].astype(bf16), dy_ref[...].astype(bf16), 0, 0)

    return pl.pallas_call(
        body, name=name, grid=(K // tk, N // tn, M // tm),
        in_specs=[pl.BlockSpec((tm, tk), lambda k, n, m: (m, k)), pl.BlockSpec((tm, tn), lambda k, n, m: (m, n))],
        out_specs=pl.BlockSpec((tk, tn), lambda k, n, m: (k, n)),
        out_shape=jax.ShapeDtypeStruct((K, N), f32),
        compiler_params=_cp(("parallel", "parallel", "arbitrary")),
    )(a, dy)


def rms_fwd(h, g, name):
    S = h.shape[0]
    t = min(512, S)

    def body(h_ref, g_ref, o_ref):
        o_ref[...] = _rms(h_ref[...], g_ref[...]).astype(bf16)

    return pl.pallas_call(
        body, name=name, grid=(S // t,),
        in_specs=[pl.BlockSpec((t, D), lambda i: (i, 0)), pl.BlockSpec((1, D), lambda i: (0, 0))],
        out_specs=pl.BlockSpec((t, D), lambda i: (i, 0)),
        out_shape=jax.ShapeDtypeStruct((S, D), bf16),
        compiler_params=_cp(("parallel",)),
    )(h, g.reshape(1, D))


def rms_bwd(h, g, dn, dres, name):
    S = h.shape[0]
    t = min(512, S)

    def body(h_ref, g_ref, dn_ref, dr_ref, dh_ref, dg_ref):
        @pl.when(pl.program_id(0) == 0)
        def _():
            dg_ref[...] = jnp.zeros_like(dg_ref)
        _, vjp = jax.vjp(_rms, h_ref[...], g_ref[...])
        dh, dg = vjp(dn_ref[...])
        dh_ref[...] = dr_ref[...] + dh
        dg_ref[...] += dg

    tok = pl.BlockSpec((t, D), lambda i: (i, 0))
    vec = pl.BlockSpec((1, D), lambda i: (0, 0))
    return pl.pallas_call(
        body, name=name, grid=(S // t,), in_specs=[tok, vec, tok, tok], out_specs=[tok, vec],
        out_shape=[jax.ShapeDtypeStruct((S, D), f32), jax.ShapeDtypeStruct((1, D), f32)],
        compiler_params=_cp(("arbitrary",)),
    )(h, g.reshape(1, D), dn, dres)


def loss_head(h, g, target):
    S = h.shape[0]
    t = min(512, S)

    def f(hh, gg, tt):
        err = _rms(hh, gg) - tt
        return 0.5 * jnp.sum(jnp.mean(err * err, axis=-1, keepdims=True), axis=0, keepdims=True)

    def body(h_ref, g_ref, t_ref, loss_ref, dh_ref, dg_ref):
        @pl.when(pl.program_id(0) == 0)
        def _():
            dg_ref[...] = jnp.zeros_like(dg_ref)
            loss_ref[...] = jnp.zeros_like(loss_ref)
        val, vjp = jax.vjp(lambda a, b: f(a, b, t_ref[...]), h_ref[...], g_ref[...])
        dh, dg = vjp(jnp.ones((1, 1), f32))
        dh_ref[...] = dh
        dg_ref[...] += dg
        loss_ref[...] += jnp.broadcast_to(val, loss_ref.shape)

    tok = pl.BlockSpec((t, D), lambda i: (i, 0))
    vec = pl.BlockSpec((1, D), lambda i: (0, 0))
    return pl.pallas_call(
        body, name="loss_head", grid=(S // t,), in_specs=[tok, vec, tok],
        out_specs=[pl.BlockSpec((1, LANE), lambda i: (0, 0)), tok, vec],
        out_shape=[jax.ShapeDtypeStruct((1, LANE), f32), jax.ShapeDtypeStruct((S, D), f32),
                   jax.ShapeDtypeStruct((1, D), f32)],
        compiler_params=_cp(("arbitrary",)),
    )(h, g.reshape(1, D), target)


def memkv_fwd(mem, g, w, name):
    def body(m_ref, g_ref, w_ref, o_ref):
        o_ref[...] = _dg(_rms(m_ref[...], g_ref[...]).astype(bf16), w_ref[...], 1, 0)

    return pl.pallas_call(
        body, name=name, out_shape=jax.ShapeDtypeStruct((MEM_LEN, 2 * X_Q), f32), compiler_params=_cp(),
    )(mem, g.reshape(1, D), w)


def memkv_bwd(mem, g, w, dkv, name):
    def body(m_ref, g_ref, w_ref, d_ref, dg_ref, dw_ref):
        n, vjp = jax.vjp(lambda gg: _rms(m_ref[...], gg), g_ref[...])
        db = d_ref[...].astype(bf16)
        dw_ref[...] = _dg(n.astype(bf16), db, 0, 0)
        dg_ref[...] = vjp(_dg(db, w_ref[...], 1, 1))[0]

    return pl.pallas_call(
        body, name=name,
        out_shape=[jax.ShapeDtypeStruct((1, D), f32), jax.ShapeDtypeStruct((D, 2 * X_Q), f32)],
        compiler_params=_cp(),
    )(mem, g.reshape(1, D), w, dkv)


def _xattn_f(xq, mk, mv):
    lane = lax.broadcasted_iota(jnp.int32, (1, X_Q), 1)
    out = jnp.zeros(xq.shape, f32)
    for hd in range(4):
        msk = (lane // 64 == hd).astype(f32)
        s = bdot_nt(xq * msk, mk) * (64 ** -0.5)
        m = lax.stop_gradient(jnp.max(s, axis=-1, keepdims=True))
        p = jnp.exp(s - m)
        p = p / jnp.sum(p, axis=-1, keepdims=True)
        out = out + bdot(p, mv * msk)
    return out


def xattn_fwd(proj, col, kv, name):
    S = proj.shape[0]
    t = min(512, S)
    cb = col // X_Q

    def body(q_ref, k_ref, v_ref, o_ref):
        o_ref[...] = _xattn_f(q_ref[...], k_ref[...], v_ref[...])

    return pl.pallas_call(
        body, name=name, grid=(S // t,),
        in_specs=[pl.BlockSpec((t, X_Q), lambda i: (i, cb)), pl.BlockSpec((MEM_LEN, X_Q), lambda i: (0, 0)),
                  pl.BlockSpec((MEM_LEN, X_Q), lambda i: (0, 1))],
        out_specs=pl.BlockSpec((t, X_Q), lambda i: (i, 0)),
        out_shape=jax.ShapeDtypeStruct((S, X_Q), f32),
        compiler_params=_cp(("parallel",)),
    )(proj, kv, kv)


def xattn_bwd(proj, col, kv, dmix, name):
    S = proj.shape[0]
    t = min(512, S)
    cb = col // X_Q

    def body(q_ref, k_ref, v_ref, do_ref, dq_ref, dk_ref, dv_ref):
        @pl.when(pl.program_id(0) == 0)
        def _():
            dk_ref[...] = jnp.zeros_like(dk_ref)
            dv_ref[...] = jnp.zeros_like(dv_ref)
        _, vjp = jax.vjp(_xattn_f, q_ref[...], k_ref[...], v_ref[...])
        dq, dk, dv = vjp(do_ref[...])
        dq_ref[...] = dq
        dk_ref[...] += dk
        dv_ref[...] += dv

    kvb = pl.BlockSpec((MEM_LEN, X_Q), lambda i: (0, 0))
    dq, dk, dv = pl.pallas_call(
        body, name=name, grid=(S // t,),
        in_specs=[pl.BlockSpec((t, X_Q), lambda i: (i, cb)), kvb,
                  pl.BlockSpec((MEM_LEN, X_Q), lambda i: (0, 1)), pl.BlockSpec((t, X_Q), lambda i: (i, 3))],
        out_specs=[pl.BlockSpec((t, X_Q), lambda i: (i, 0)), kvb, kvb],
        out_shape=[jax.ShapeDtypeStruct((S, X_Q), f32), jax.ShapeDtypeStruct((MEM_LEN, X_Q), f32),
                   jax.ShapeDtypeStruct((MEM_LEN, X_Q), f32)],
        compiler_params=_cp(("arbitrary",)),
    )(proj, kv, kv, dmix)
    return dq, jnp.concatenate([dk, dv], axis=1)


def _bucket_map():
    qi = np.arange(BLK)[:, None]
    kj = np.arange(2 * BLK)[None, :]
    n = np.maximum(BLK + qi - kj, 0)
    max_exact = N_BUCKETS // 2
    nf = np.maximum(n, 1).astype(np.float64)
    large = max_exact + (np.log(nf / max_exact) / math.log(MAX_DIST / max_exact)
                         * (N_BUCKETS - max_exact)).astype(np.int32)
    large = np.minimum(large, N_BUCKETS - 1)
    return np.where(n < max_exact, n, large).astype(np.int32)


def bias_build(rel_bias):
    def body(rb_ref, bk_ref, o_ref):
        bk = bk_ref[...]
        for h in range(A_HEADS):
            acc = jnp.zeros((BLK, 2 * BLK), f32)
            for b in range(N_BUCKETS):
                acc = jnp.where(bk == b, rb_ref[b, h], acc)
            o_ref[h] = acc

    return pl.pallas_call(
        body, name="bias_build",
        in_specs=[pl.BlockSpec(memory_space=pltpu.SMEM), pl.BlockSpec(memory_space=pltpu.VMEM)],
        out_specs=pl.BlockSpec(memory_space=pltpu.VMEM),
        out_shape=jax.ShapeDtypeStruct((A_HEADS, BLK, 2 * BLK), f32), compiler_params=_cp(),
    )(rel_bias, jnp.asarray(_bucket_map()))


def bias_grad(dbias):
    def body(d_ref, bk_ref, o_ref):
        bk = bk_ref[...]
        row = lax.broadcasted_iota(jnp.int32, (N_BUCKETS, LANE), 0)
        lane = lax.broadcasted_iota(jnp.int32, (N_BUCKETS, LANE), 1)
        acc = jnp.zeros((N_BUCKETS, LANE), f32)
        for h in range(A_HEADS):
            d = d_ref[h]
            for b in range(N_BUCKETS):
                s = jnp.sum(jnp.where(bk == b, d, 0.0), keepdims=True)
                acc = acc + jnp.where((row == b) & (lane == h), s, 0.0)
        o_ref[...] = acc

    return pl.pallas_call(
        body, name="bias_grad", out_shape=jax.ShapeDtypeStruct((N_BUCKETS, LANE), f32), compiler_params=_cp(),
    )(dbias, jnp.asarray(_bucket_map()))


def _swa_f(qb, kp, kc, vp, vc, bias, sk, first):
    kband = jnp.concatenate([kp, kc], axis=0)
    vband = jnp.concatenate([vp, vc], axis=0)
    qi = lax.broadcasted_iota(jnp.int32, (BLK, 2 * BLK), 0)
    kj = lax.broadcasted_iota(jnp.int32, (BLK, 2 * BLK), 1)
    rel = kj - qi
    ok = (rel >= 1) & (rel <= BLK) & ((kj >= BLK) | jnp.logical_not(first))
    lane = lax.broadcasted_iota(jnp.int32, (1, LANE), 1)
    lane_b = lax.broadcasted_iota(jnp.int32, (BLK, LANE), 1)
    outs = []
    for p in range(A_HEADS // 2):
        qp = qb[:, LANE * p:LANE * (p + 1)]
        acc = jnp.zeros((BLK, LANE), f32)
        for g in range(2):
            h = g * (A_HEADS // 2) + p
            msk = (lane // A_DH == g).astype(f32)
            s = bdot_nt(qp * msk, kband) * (A_DH ** -0.5) + bias[h]
            s = jnp.where(ok, s, -1e30)
            skb = jnp.broadcast_to(sk[h:h + 1, :], (BLK, LANE))
            sink = jnp.sum(jnp.where(lane_b == 0, skb, 0.0), axis=-1, keepdims=True)
            m = lax.stop_gradient(jnp.maximum(jnp.max(s, axis=-1, keepdims=True), sink))
            e = jnp.exp(s - m)
            prob = e / (jnp.sum(e, axis=-1, keepdims=True) + jnp.exp(sink - m))
            acc = acc + bdot(prob, vband) * msk
        outs.append(acc)
    return jnp.concatenate(outs, axis=1)


def _swa_specs(nb, rev):
    bi = (lambda i: nb - 1 - i) if rev else (lambda i: i)
    return [
        pl.BlockSpec((BLK, A_Q), lambda i: (bi(i), 0)),
        pl.BlockSpec((BLK, LANE), lambda i: (jnp.maximum(bi(i) - 1, 0), 6)),
        pl.BlockSpec((BLK, LANE), lambda i: (bi(i), 6)),
        pl.BlockSpec((BLK, LANE), lambda i: (jnp.maximum(bi(i) - 1, 0), 7)),
        pl.BlockSpec((BLK, LANE), lambda i: (bi(i), 7)),
        pl.BlockSpec((A_HEADS, BLK, 2 * BLK), lambda i: (0, 0, 0)),
        pl.BlockSpec((16, LANE), lambda i: (0, 0)),
    ]


def swa_fwd(proj, bias, sk):
    S = proj.shape[0]
    nb = S // BLK

    def body(q_ref, kp_ref, kc_ref, vp_ref, vc_ref, b_ref, s_ref, o_ref):
        o_ref[...] = _swa_f(q_ref[...], kp_ref[...], kc_ref[...], vp_ref[...], vc_ref[...], b_ref[...], s_ref[...],
                            pl.program_id(0) == 0)

    return pl.pallas_call(
        body, name="swa_fwd", grid=(nb,), in_specs=_swa_specs(nb, False),
        out_specs=pl.BlockSpec((BLK, A_Q), lambda i: (i, 0)),
        out_shape=jax.ShapeDtypeStruct((S, A_Q), f32), compiler_params=_cp(("parallel",)),
    )(proj, proj, proj, proj, proj, bias, sk)


def swa_bwd(proj, bias, sk, dmix):
    S = proj.shape[0]
    nb = S // BLK

    def body(q_ref, kp_ref, kc_ref, vp_ref, vc_ref, b_ref, s_ref, do_ref, dqkv_ref, db_ref, ds_ref, ck, cv):
        i = pl.program_id(0)

        @pl.when(i == 0)
        def _():
            db_ref[...] = jnp.zeros_like(db_ref)
            ds_ref[...] = jnp.zeros_like(ds_ref)
            ck[...] = jnp.zeros_like(ck)
            cv[...] = jnp.zeros_like(cv)
        first = i == nb - 1
        _, vjp = jax.vjp(lambda *a: _swa_f(*a, first), q_ref[...], kp_ref[...], kc_ref[...], vp_ref[...],
                         vc_ref[...], b_ref[...], s_ref[...])
        dq, dkp, dkc, dvp, dvc, db, ds = vjp(do_ref[...])
        dqkv_ref[...] = jnp.concatenate([dq, dkc + ck[...], dvc + cv[...]], axis=1)
        ck[...] = dkp
        cv[...] = dvp
        db_ref[...] += db
        ds_ref[...] += ds

    return pl.pallas_call(
        body, name="swa_bwd", grid=(nb,),
        in_specs=_swa_specs(nb, True) + [pl.BlockSpec((BLK, A_Q), lambda i: (nb - 1 - i, 0))],
        out_specs=[pl.BlockSpec((BLK, D), lambda i: (nb - 1 - i, 0)),
                   pl.BlockSpec((A_HEADS, BLK, 2 * BLK), lambda i: (0, 0, 0)),
                   pl.BlockSpec((16, LANE), lambda i: (0, 0))],
        out_shape=[jax.ShapeDtypeStruct((S, D), f32), jax.ShapeDtypeStruct((A_HEADS, BLK, 2 * BLK), f32),
                   jax.ShapeDtypeStruct((16, LANE), f32)],
        scratch_shapes=[pltpu.VMEM((BLK, LANE), f32), pltpu.VMEM((BLK, LANE), f32)],
        compiler_params=_cp(("arbitrary",)),
    )(proj, proj, proj, proj, proj, bias, sk, dmix)


def _dnprep_f(xext, w, is_qk):
    c = (w[3:4] * xext + w[2:3] * shift_down(xext, 1) + w[1:2] * shift_down(xext, 2) + w[0:1] * shift_down(xext, 3))
    a = _silu(c)[HALO:]
    n = a * lax.rsqrt(jnp.sum(a * a, axis=-1, keepdims=True) + EPS)
    return jnp.where(is_qk, n, a)


def dnprep_fwd(proj, cw):
    S = proj.shape[0]
    nblk = B_QKV // LANE
    T = GLU_ROWS

    def body(x_ref, w_ref, o_ref):
        is_qk = pl.program_id(0) < 2 * B_QK // LANE
        wv = w_ref[...]

        def tile(r0, first):
            o_ref[pl.ds(r0, T), :] = _dnprep_f(_glu_gext(x_ref, r0, first), wv, is_qk)

        tile(0, True)

        @pl.loop(1, S // T)
        def _(t):
            tile(pl.multiple_of(t * T, T), False)

    return pl.pallas_call(
        body, name="dnprep_fwd", grid=(nblk,),
        in_specs=[pl.BlockSpec((S, LANE), lambda j: (0, j)), pl.BlockSpec((4, LANE), lambda j: (0, j))],
        out_specs=pl.BlockSpec((S, LANE), lambda j: (0, j)),
        out_shape=jax.ShapeDtypeStruct((S, B_QKV), f32), compiler_params=_cp(("parallel",)),
    )(proj, cw)


def dnprep_bwd(proj, cw, dqkvn):
    S = proj.shape[0]
    nblk = B_QKV // LANE

    T = GLU_ROWS

    def body(x_ref, w_ref, d_ref, dx_ref, dw_ref):
        is_qk = pl.program_id(0) < 2 * B_QK // LANE
        wv = w_ref[...]

        def tile(r0, first):
            _, vjp = jax.vjp(lambda a, b: _dnprep_f(a, b, is_qk), _glu_gext(x_ref, r0, first), wv)
            dx, dw = vjp(d_ref[pl.ds(r0, T), :])
            dx_ref[pl.ds(r0, T), :] = dx[HALO:]
            if not first:
                dx_ref[pl.ds(r0 - HALO, HALO), :] += dx[:HALO]
            return dw

        dw_ref[...] = tile(0, True)

        @pl.loop(1, S // T)
        def _(t):
            dw_ref[...] += tile(pl.multiple_of(t * T, T), False)

    col = pl.BlockSpec((S, LANE), lambda j: (0, j))
    wsp = pl.BlockSpec((4, LANE), lambda j: (0, j))
    return pl.pallas_call(
        body, name="dnprep_bwd", grid=(nblk,), in_specs=[col, wsp, col], out_specs=[col, wsp],
        out_shape=[jax.ShapeDtypeStruct((S, B_QKV), f32), jax.ShapeDtypeStruct((4, B_QKV), f32)],
        compiler_params=_cp(("parallel",)),
    )(proj, cw, dqkvn)


def _hdot(a, b, ca=1, cb=0):
    return _dg(a, b, ca, cb, HI)


def _bdg(a, b, ca, cb):
    dn = (((ca,), (cb,)), ((0,), (0,)))
    ah, bh = a.astype(bf16), b.astype(bf16)
    al, bl = (a - ah.astype(f32)).astype(bf16), (b - bh.astype(f32)).astype(bf16)
    return (lax.dot_general(ah, bh, dn, preferred_element_type=f32)
            + lax.dot_general(ah, bl, dn, preferred_element_type=f32)
            + lax.dot_general(al, bh, dn, preferred_element_type=f32))


@jax.custom_vjp
def hbd(a, b):
    return _bdg(a, b, 2, 1)


@jax.custom_vjp
def hbd_nt(a, b):
    return _bdg(a, b, 2, 2)


@jax.custom_vjp
def hbd_tn(a, b):
    return _bdg(a, b, 1, 1)


hbd.defvjp(lambda a, b: (hbd(a, b), (a, b)), lambda r, g: (hbd_nt(g, r[1]), hbd_tn(r[0], g)))
hbd_nt.defvjp(lambda a, b: (hbd_nt(a, b), (a, b)), lambda r, g: (hbd(g, r[1]), hbd_tn(g, r[0])))
hbd_tn.defvjp(lambda a, b: (hbd_tn(a, b), (a, b)), lambda r, g: (hbd_nt(r[1], g), hbd(r[0], g)))


def _stack(xs):
    return jnp.concatenate([x[None] for x in xs], axis=0)


def _lane_col(x, j):
    lane = lax.broadcasted_iota(jnp.int32, (1, LANE), 1)
    return jnp.sum(jnp.where(lane == j, x, 0.0), axis=-1, keepdims=True)


def _tri_inv(a_mat):
    r = lax.broadcasted_iota(jnp.int32, (1, CHUNK, CHUNK), 1)
    c = lax.broadcasted_iota(jnp.int32, (1, CHUNK, CHUNK), 2)
    pw = -a_mat
    inv = (r == c).astype(f32) + pw
    for _ in range(5):
        pw = hbd(pw, pw)
        inv = inv + hbd(inv, pw)
    return inv


@jax.custom_vjp
def _tri_inv_known(a_mat, inv):
    return inv


_tri_inv_known.defvjp(lambda a, inv: (inv, inv),
                      lambda inv, g: (-hbd_tn(inv, hbd_nt(g, inv)), jnp.zeros_like(inv)))


def _dnc_f(q, k, v, seg, prm, inverse=_tri_inv):
    C = CHUNK
    B = q.shape[0]
    rows = seg.shape[0]
    beta_all = _sigmoid(seg)
    xx = seg + prm[1:2]
    g_all = -jnp.exp(prm[0:1]) * (jnp.maximum(xx, 0.0) + jnp.log(1.0 + jnp.exp(-jnp.abs(xx))))
    r2 = lax.broadcasted_iota(jnp.int32, (rows, rows), 0)
    c2 = lax.broadcasted_iota(jnp.int32, (rows, rows), 1)
    within = (r2 >= c2) & (r2 // C == c2 // C)
    gc_all = _hdot(within.astype(f32), g_all)
    beta = _stack([_lane_col(beta_all[C * j:C * (j + 1)], h) for j in range(rows // C) for h in range(6)])
    gc = _stack([_lane_col(gc_all[C * j:C * (j + 1)], 6 + h) for j in range(rows // C) for h in range(6)])
    r = lax.broadcasted_iota(jnp.int32, (1, C, C), 1)
    c = lax.broadcasted_iota(jnp.int32, (1, C, C), 2)
    incl = r >= c
    strict = r > c
    eye = (r == c).astype(f32)
    g_row = hbd(jnp.ones((B, C, C), f32), eye * gc)
    decay = jnp.where(incl, jnp.exp(jnp.where(incl, gc - g_row, 0.0)), 0.0)
    a_mat = beta * hbd_nt(k, k) * jnp.where(strict, decay, 0.0)
    eg = jnp.exp(gc)
    inv = inverse(a_mat)
    u = hbd(inv, beta * v)
    w = hbd(inv, (beta * eg) * k)
    qc = q * (B_DH ** -0.5)
    attn = hbd_nt(qc, k) * decay
    last = (lax.broadcasted_iota(jnp.int32, (1, C, 1), 1) == C - 1).astype(f32)
    g_last = jnp.sum(gc * last, axis=1, keepdims=True)
    dc = jnp.broadcast_to(jnp.exp(g_last), (B, 1, LANE)).reshape(B, LANE)
    return u, w, qc * eg, k * jnp.exp(g_last - gc), attn, dc, inv


def _b1(a, b, ca, cb):
    return lax.dot_general(a.astype(bf16), b.astype(bf16), (((ca,), (cb,)), ((0,), (0,))), preferred_element_type=f32)


@jax.custom_vjp
def sbd(a, b):
    return _b1(a, b, 2, 1)


@jax.custom_vjp
def sbd_nt(a, b):
    return _b1(a, b, 2, 2)


@jax.custom_vjp
def sbd_tn(a, b):
    return _b1(a, b, 1, 1)


sbd.defvjp(lambda a, b: (sbd(a, b), (a, b)), lambda r, g: (sbd_nt(g, r[1]), sbd_tn(r[0], g)))
sbd_nt.defvjp(lambda a, b: (sbd_nt(a, b), (a, b)), lambda r, g: (sbd(g, r[1]), sbd_tn(g, r[0])))
sbd_tn.defvjp(lambda a, b: (sbd_tn(a, b), (a, b)), lambda r, g: (sbd_nt(r[1], g), sbd(r[0], g)))


def _dns_f(S0, u, w, qd, kt, attn, dcrows):
    dc = _lane_col(dcrows, 0).reshape(6, 1, 1)
    delta = u - sbd(w, S0)
    out = sbd(qd, S0) + sbd(attn, delta)
    return out, dc * S0 + sbd_tn(kt, delta)


def _dnpost_f(o, z, grow):
    outs = []
    for h in range(6):
        oh = o[:, LANE * h:LANE * (h + 1)]
        outs.append(oh * lax.rsqrt(jnp.mean(oh * oh, axis=-1, keepdims=True) + EPS) * grow
                    * _silu(z[:, LANE * h:LANE * (h + 1)]))
    return jnp.concatenate(outs, axis=1)


def _hs(h):
    return slice(LANE * h, LANE * (h + 1))


DN_CHUNKS = 4


def _heads(ref, share):
    return _stack([ref[CHUNK * j:CHUNK * (j + 1), _hs(h // share)]
                   for j in range(ref.shape[0] // CHUNK) for h in range(6)])


def _put_heads(ref, val):
    for j in range(ref.shape[0] // CHUNK):
        for h in range(6):
            ref[CHUNK * j:CHUNK * (j + 1), _hs(h)] = val[6 * j + h]


def _dnc_in_specs():
    rows = CHUNK * DN_CHUNKS
    return [
        pl.BlockSpec((rows, B_QK), lambda n: (n, 0)),
        pl.BlockSpec((rows, B_QK), lambda n: (n, 1)),
        pl.BlockSpec((rows, B_V), lambda n: (n, 1)),
        pl.BlockSpec((rows, LANE), lambda n: (n, 20)),
        pl.BlockSpec((8, LANE), lambda n: (0, 0)),
    ]


def _dnc_out_specs(rev_nc=None, chunks=1):
    ci = (lambda n: n) if rev_nc is None else (lambda n: rev_nc - 1 - n)
    wide = pl.BlockSpec((CHUNK * chunks, B_V), lambda n: (ci(n), 0))
    return [wide, wide, wide, wide, pl.BlockSpec((chunks, 6, CHUNK, CHUNK), lambda n: (ci(n), 0, 0, 0)),
            pl.BlockSpec((chunks, 8, LANE), lambda n: (ci(n), 0, 0))]


def _dc_rows(dc):
    pad = jnp.zeros((2, LANE), f32)
    return _stack([jnp.concatenate([dc[6 * j:6 * (j + 1)], pad], axis=0) for j in range(dc.shape[0] // 6)])


def _dnc_shapes(S):
    nc = S // CHUNK
    wide = jax.ShapeDtypeStruct((S, B_V), f32)
    return [wide, wide, wide, wide, jax.ShapeDtypeStruct((nc, 6, CHUNK, CHUNK), f32),
            jax.ShapeDtypeStruct((nc, 8, LANE), f32)]


def dnc_fwd(qkvn, proj, prm):
    S = proj.shape[0]

    def body(q_ref, k_ref, v_ref, s_ref, p_ref, u_ref, w_ref, qd_ref, kt_ref, at_ref, dc_ref, inv_ref):
        u, w, qd, kt, attn, dc, inv = _dnc_f(_heads(q_ref, 2), _heads(k_ref, 2), _heads(v_ref, 1), s_ref[...],
                                             p_ref[...])
        inv_ref[...] = inv.reshape(inv_ref.shape)
        _put_heads(u_ref, u)
        _put_heads(w_ref, w)
        _put_heads(qd_ref, qd)
        _put_heads(kt_ref, kt)
        at_ref[...] = attn.reshape(at_ref.shape)
        dc_ref[...] = _dc_rows(dc)

    outs = _dnc_out_specs(chunks=DN_CHUNKS)
    out = pl.pallas_call(
        body, name="dn_chunk_fwd", grid=(S // (CHUNK * DN_CHUNKS),), in_specs=_dnc_in_specs(),
        out_specs=outs + [outs[4]], out_shape=_dnc_shapes(S) + [_dnc_shapes(S)[4]],
        compiler_params=_cp(("parallel",)),
    )(qkvn, qkvn, qkvn, proj, prm)
    return out[:6], out[6]


def dnc_bwd(qkvn, proj, prm, inv, cots):
    S = proj.shape[0]

    def body(q_ref, k_ref, v_ref, s_ref, p_ref, inv_ref, du_ref, dw_ref, dqd_ref, dkt_ref, dat_ref, ddc_ref,
             dx_ref, dseg_ref, dprm_ref):
        @pl.when(pl.program_id(0) == 0)
        def _():
            dprm_ref[...] = jnp.zeros_like(dprm_ref)
        nb = 6 * DN_CHUNKS
        known = functools.partial(_tri_inv_known, inv=inv_ref[...].reshape(nb, CHUNK, CHUNK))
        _, vjp = jax.vjp(lambda *a: _dnc_f(*a, inverse=known)[:6], _heads(q_ref, 2), _heads(k_ref, 2),
                         _heads(v_ref, 1), s_ref[...], p_ref[...])
        ddc = jnp.concatenate([ddc_ref[j, 0:6, :] for j in range(DN_CHUNKS)], axis=0)
        dq, dk, dv, dseg, dprm = vjp((_heads(du_ref, 1), _heads(dw_ref, 1), _heads(dqd_ref, 1), _heads(dkt_ref, 1),
                                      dat_ref[...].reshape(nb, CHUNK, CHUNK), ddc))
        for j in range(DN_CHUNKS):
            o = 6 * j
            dx_ref[CHUNK * j:CHUNK * (j + 1), :] = jnp.concatenate(
                [dq[o] + dq[o + 1], dq[o + 2] + dq[o + 3], dq[o + 4] + dq[o + 5],
                 dk[o] + dk[o + 1], dk[o + 2] + dk[o + 3], dk[o + 4] + dk[o + 5]] + [dv[o + h] for h in range(6)], axis=1)
        dseg_ref[...] = dseg
        dprm_ref[...] += dprm

    rows = CHUNK * DN_CHUNKS
    outs = _dnc_out_specs(chunks=DN_CHUNKS)
    return pl.pallas_call(
        body, name="dn_chunk_bwd", grid=(S // rows,),
        in_specs=_dnc_in_specs() + [outs[4]] + outs,
        out_specs=[pl.BlockSpec((rows, B_QKV), lambda n: (n, 0)), pl.BlockSpec((rows, LANE), lambda n: (n, 0)),
                   pl.BlockSpec((8, LANE), lambda n: (0, 0))],
        out_shape=[jax.ShapeDtypeStruct((S, B_QKV), f32), jax.ShapeDtypeStruct((S, LANE), f32),
                   jax.ShapeDtypeStruct((8, LANE), f32)],
        compiler_params=_cp(("arbitrary",)),
    )(qkvn, qkvn, qkvn, proj, prm, inv, *cots)


def dns_fwd(chunked):
    u = chunked[0]
    S = u.shape[0]
    nc = S // CHUNK

    def body(u_ref, w_ref, qd_ref, kt_ref, at_ref, dc_ref, o_ref, st_ref, st):
        @pl.when(pl.program_id(0) == 0)
        def _():
            st[...] = jnp.zeros_like(st)
        S0 = st[...]
        st_ref[0] = S0
        out, S1 = _dns_f(S0, _heads(u_ref, 1), _heads(w_ref, 1), _heads(qd_ref, 1), _heads(kt_ref, 1),
                         at_ref[0], dc_ref[0, 0:6, :])
        _put_heads(o_ref, out)
        st[...] = S1

    return pl.pallas_call(
        body, name="dn_scan_fwd", grid=(nc,), in_specs=_dnc_out_specs(),
        out_specs=[pl.BlockSpec((CHUNK, B_V), lambda n: (n, 0)),
                   pl.BlockSpec((1, 6, B_DH, B_DH), lambda n: (n, 0, 0, 0))],
        out_shape=[jax.ShapeDtypeStruct((S, B_V), f32), jax.ShapeDtypeStruct((nc, 6, B_DH, B_DH), f32)],
        scratch_shapes=[pltpu.VMEM((6, B_DH, B_DH), f32)],
        compiler_params=_cp(("arbitrary",)),
    )(*chunked)


def dns_bwd(chunked, states, do):
    S = do.shape[0]
    nc = S // CHUNK

    def body(u_ref, w_ref, qd_ref, kt_ref, at_ref, dc_ref, st_ref, do_ref,
             du_ref, dw_ref, dqd_ref, dkt_ref, dat_ref, ddc_ref, dst):
        @pl.when(pl.program_id(0) == 0)
        def _():
            dst[...] = jnp.zeros_like(dst)
        _, vjp = jax.vjp(_dns_f, st_ref[0], _heads(u_ref, 1), _heads(w_ref, 1), _heads(qd_ref, 1), _heads(kt_ref, 1),
                         at_ref[0], dc_ref[0, 0:6, :])
        dS0, du, dw, dqd, dkt, dat, ddc = vjp((_heads(do_ref, 1), dst[...]))
        dst[...] = dS0
        _put_heads(du_ref, du)
        _put_heads(dw_ref, dw)
        _put_heads(dqd_ref, dqd)
        _put_heads(dkt_ref, dkt)
        dat_ref[0] = dat
        ddc_ref[0] = jnp.concatenate([ddc, jnp.zeros((2, LANE), f32)], axis=0)

    return pl.pallas_call(
        body, name="dn_scan_bwd", grid=(nc,),
        in_specs=_dnc_out_specs(nc) + [pl.BlockSpec((1, 6, B_DH, B_DH), lambda n: (nc - 1 - n, 0, 0, 0)),
                                       pl.BlockSpec((CHUNK, B_V), lambda n: (nc - 1 - n, 0))],
        out_specs=_dnc_out_specs(nc), out_shape=_dnc_shapes(S),
        scratch_shapes=[pltpu.VMEM((6, B_DH, B_DH), f32)],
        compiler_params=_cp(("arbitrary",)),
    )(*chunked, states, do)


def dnpost_fwd(o, proj, prm):
    S = o.shape[0]
    t = min(512, S)

    def body(o_ref, z_ref, p_ref, y_ref):
        y_ref[...] = _dnpost_f(o_ref[...], z_ref[...], p_ref[2:3, :])

    tok = pl.BlockSpec((t, B_V), lambda i: (i, 0))
    return pl.pallas_call(
        body, name="dn_post_fwd", grid=(S // t,),
        in_specs=[tok, pl.BlockSpec((t, B_V), lambda i: (i, 2)), pl.BlockSpec((8, LANE), lambda i: (0, 0))],
        out_specs=tok, out_shape=jax.ShapeDtypeStruct((S, B_V), f32), compiler_params=_cp(("parallel",)),
    )(o, proj, prm)


def dnpost_bwd(o, proj, prm, dmix):
    S = o.shape[0]
    t = min(512, S)

    def body(o_ref, z_ref, p_ref, dy_ref, do_ref, dz_ref, dg_ref):
        @pl.when(pl.program_id(0) == 0)
        def _():
            dg_ref[...] = jnp.zeros_like(dg_ref)
        _, vjp = jax.vjp(_dnpost_f, o_ref[...], z_ref[...], p_ref[2:3, :])
        do, dz, dg = vjp(dy_ref[...])
        do_ref[...] = do
        dz_ref[...] = dz
        dg_ref[...] += dg

    tok = pl.BlockSpec((t, B_V), lambda i: (i, 0))
    return pl.pallas_call(
        body, name="dn_post_bwd", grid=(S // t,),
        in_specs=[tok, pl.BlockSpec((t, B_V), lambda i: (i, 2)), pl.BlockSpec((8, LANE), lambda i: (0, 0)), tok],
        out_specs=[tok, tok, pl.BlockSpec((1, LANE), lambda i: (0, 0))],
        out_shape=[jax.ShapeDtypeStruct((S, B_V), f32), jax.ShapeDtypeStruct((S, B_V), f32),
                   jax.ShapeDtypeStruct((1, LANE), f32)],
        compiler_params=_cp(("arbitrary",)),
    )(o, proj, prm, dmix)


N_FF_BLK = D_FF // LANE
GU_SHARD = 2 * D_FF // 4


GLU_ROWS = 256
HALO = 8


def _glu_f(gext, up, w, b):
    c = w[2:3] * gext + w[1:2] * shift_down(gext, 1) + w[0:1] * shift_down(gext, 2) + b
    return _silu(c)[HALO:] * up


def _glu_gext(g_ref, r0, first):
    if first:
        return jnp.concatenate([jnp.zeros((HALO, LANE), f32), g_ref[0:GLU_ROWS, :]], axis=0)
    return g_ref[pl.ds(r0 - HALO, GLU_ROWS + HALO), :]


def glu_fwd(gu, w, b, name):
    S = gu.shape[0]
    T = GLU_ROWS

    def body(g_ref, u_ref, w_ref, b_ref, o_ref):
        wv, bv = w_ref[...], b_ref[...]

        def tile(r0, first):
            act = _glu_f(_glu_gext(g_ref, r0, first), u_ref[pl.ds(r0, T), :], wv, bv)
            o_ref[pl.ds(r0, T), :] = act.astype(bf16)

        tile(0, True)

        @pl.loop(1, S // T)
        def _(t):
            tile(pl.multiple_of(t * T, T), False)

    col = pl.BlockSpec((S, LANE), lambda j: (0, j))
    return pl.pallas_call(
        body, name=name, grid=(N_FF_BLK,),
        in_specs=[col, pl.BlockSpec((S, LANE), lambda j: (0, N_FF_BLK + j)), pl.BlockSpec((3, LANE), lambda j: (0, j)),
                  pl.BlockSpec((1, LANE), lambda j: (0, j))],
        out_specs=col, out_shape=jax.ShapeDtypeStruct((S, D_FF), bf16), compiler_params=_cp(("parallel",)),
    )(gu, gu, w, b.reshape(1, D_FF))


def glu_bwd(gu, w, b, dact, name):
    S = gu.shape[0]
    T = GLU_ROWS

    def body(g_ref, u_ref, w_ref, b_ref, d_ref, dg_ref, dw_ref, db_ref, acc):
        wv, bv = w_ref[...], b_ref[...]

        def tile(r0, first):
            _, vjp = jax.vjp(_glu_f, _glu_gext(g_ref, r0, first), u_ref[pl.ds(r0, T), :], wv, bv)
            dgx, du, dw, db = vjp(d_ref[pl.ds(r0, T), :])
            acc[pl.ds(r0, T), :] = dgx[HALO:]
            if not first:
                acc[pl.ds(r0 - HALO, HALO), :] += dgx[:HALO]
            dg_ref[1, pl.ds(r0, T), :] = du.astype(bf16)
            return dw, db

        dw0, db0 = tile(0, True)
        dw_ref[...] = dw0
        db_ref[...] = db0

        @pl.loop(1, S // T)
        def _(t):
            dw, db = tile(pl.multiple_of(t * T, T), False)
            dw_ref[...] += dw
            db_ref[...] += db

        dg_ref[0] = acc[...].astype(bf16)

    col = pl.BlockSpec((S, LANE), lambda j: (0, j))
    wsp = pl.BlockSpec((3, LANE), lambda j: (0, j))
    bsp = pl.BlockSpec((1, LANE), lambda j: (0, j))
    return pl.pallas_call(
        body, name=name, grid=(N_FF_BLK,),
        in_specs=[col, pl.BlockSpec((S, LANE), lambda j: (0, N_FF_BLK + j)), wsp, bsp, col],
        out_specs=[pl.BlockSpec((2, S, LANE), lambda j: (0, 0, j)), wsp, bsp],
        out_shape=[jax.ShapeDtypeStruct((2, S, D_FF), bf16), jax.ShapeDtypeStruct((3, D_FF), f32),
                   jax.ShapeDtypeStruct((1, D_FF), f32)],
        scratch_shapes=[pltpu.VMEM((S, LANE), f32)],
        compiler_params=_cp(("parallel",)),
    )(gu, gu, w, b.reshape(1, D_FF), dact)


def gu_fwd(n2, wg, name):
    S = n2.shape[0]
    tm = min(MM_ROWS, S)

    def body(a_ref, w_ref, o_ref):
        o_ref[...] = _dg(a_ref[...], w_ref[...], 1, 0)

    return pl.pallas_call(
        body, name=name, grid=(4, S // tm),
        in_specs=[pl.BlockSpec((tm, D), lambda s, m: (m, 0)), pl.BlockSpec((None, D, GU_SHARD), lambda s, m: (s, 0, 0))],
        out_specs=pl.BlockSpec((tm, GU_SHARD), lambda s, m: (m, s)),
        out_shape=jax.ShapeDtypeStruct((S, 2 * D_FF), f32), compiler_params=_cp(("parallel", "parallel")),
    )(n2, wg)


def gu_bwd_x(dgu, wg, name):
    S = dgu.shape[1]
    tm = min(MM_ROWS, S)

    def body(d_ref, w_ref, o_ref):
        @pl.when(pl.program_id(1) == 0)
        def _():
            o_ref[...] = jnp.zeros_like(o_ref)
        o_ref[...] += _dg(d_ref[...], w_ref[...], 1, 1)

    return pl.pallas_call(
        body, name=name, grid=(S // tm, 4),
        in_specs=[pl.BlockSpec((None, tm, GU_SHARD), lambda m, s: (s // 2, m, s % 2)),
                  pl.BlockSpec((None, D, GU_SHARD), lambda m, s: (s, 0, 0))],
        out_specs=pl.BlockSpec((tm, D), lambda m, s: (m, 0)),
        out_shape=jax.ShapeDtypeStruct((S, D), f32), compiler_params=_cp(("parallel", "arbitrary")),
    )(dgu, wg)


def gu_bwd_w(n2, dgu, name):
    S = n2.shape[0]
    tm = min(512, S)
    nm = S // tm

    def body(a_ref, d_ref, o_ref, acc):
        @pl.when(pl.program_id(1) == 0)
        def _():
            acc[...] = jnp.zeros_like(acc)
        acc[...] += _dg(a_ref[...], d_ref[...], 0, 0)

        @pl.when(pl.program_id(1) == nm - 1)
        def _():
            o_ref[...] = acc[...].astype(bf16)

    return pl.pallas_call(
        body, name=name, grid=(4, nm),
        in_specs=[pl.BlockSpec((tm, D), lambda s, m: (m, 0)),
                  pl.BlockSpec((None, tm, GU_SHARD), lambda s, m: (s // 2, m, s % 2))],
        out_specs=pl.BlockSpec((None, D, GU_SHARD), lambda s, m: (s, 0, 0)),
        out_shape=jax.ShapeDtypeStruct((4, D, GU_SHARD), bf16),
        scratch_shapes=[pltpu.VMEM((D, GU_SHARD), f32)],
        compiler_params=_cp(("parallel", "arbitrary")),
    )(n2, dgu)


def _pair_cols(w):
    lead = w.shape[:-1]
    return w.reshape(lead + (2, 6, A_DH)).swapaxes(-3, -2).reshape(lead + (A_Q,))


def _unpair_cols(w):
    lead = w.shape[:-1]
    return w.reshape(lead + (6, 2, A_DH)).swapaxes(-3, -2).reshape(lead + (A_Q,))


def _lay_in_a(w):
    return jnp.concatenate([_pair_cols(w[:, :A_Q]), w[:, A_Q:]], axis=1)


def _unlay_in_a(w):
    return jnp.concatenate([_unpair_cols(w[:, :A_Q]), w[:, A_Q:]], axis=1)


def _lay_out_a(w):
    return jnp.concatenate([_pair_cols(w[:A_Q].T).T, w[A_Q:]], axis=0)


def _unlay_out_a(w):
    return jnp.concatenate([_unpair_cols(w[:A_Q].T).T, w[A_Q:]], axis=0)


def _lay_in_b(w):
    return jnp.concatenate([w[:, :2304], w[:, 2316:], w[:, 2304:2316],
                            jnp.zeros((w.shape[0], LANE - 12), w.dtype)], axis=1)


def _unlay_in_b(w):
    return jnp.concatenate([w[:, :2304], w[:, 2560:2572], w[:, 2304:2560]], axis=1)


def _chip_cols(w):
    return jnp.moveaxis(w.reshape(w.shape[0], 4, w.shape[1] // 4), 1, 0)


def _unchip_cols(w):
    return jnp.moveaxis(w, 0, 1).reshape(w.shape[1], 4 * w.shape[2])


def _local_step(x, mem, target, P):
    arrive = P.get("arrive", lambda key, after: None)
    ready = P.get("ready", lambda key, grads, dep: dep)
    sk = jnp.zeros((16, LANE), f32).at[:A_HEADS].set(jnp.broadcast_to(P["sinks"][:, None], (A_HEADS, LANE)))
    prm = jnp.zeros((8, LANE), f32).at[0, 6:12].set(P["a_log"]).at[1, 6:12].set(P["dt_bias"]).at[2].set(P["out_norm_g"])
    bias = bias_build(P["rel_bias"])
    saved = []
    h = x
    for i in range(2):
        n1 = rms_fwd(h, P["g_mix"][i], f"rms_mix{i}")
        arrive(("w_in", i), n1)
        proj = mm_nn(n1, P["w_in_a"] if i == 0 else P["w_in_b"], name="proj_a" if i == 0 else "proj_b")
        arrive(("w_mem", i), proj)
        kv = memkv_fwd(mem, P["g_mem"][i], P["w_mem"][i], f"memkv{i}")
        if i == 0:
            self_out = swa_fwd(proj, bias, sk)
            cross = xattn_fwd(proj, A_Q + 2 * LANE, kv, "xattn_a")
            extra = ()
        else:
            qkvn = dnprep_fwd(proj, P["conv_qkv"])
            chunked, inv = dnc_fwd(qkvn, proj, prm)
            o, states = dns_fwd(chunked)
            self_out = dnpost_fwd(o, proj, prm)
            cross = xattn_fwd(proj, 2304, kv, "xattn_b")
            extra = (qkvn, chunked, inv, states, o)
        mix = jnp.concatenate([self_out, cross], axis=1)
        arrive(("w_out", i), cross)
        h2 = mm_nn(mix, P["w_out"][i], res=h, name=f"out_proj{i}")
        n2 = rms_fwd(h2, P["g_ffn"][i], f"rms_ffn{i}")
        arrive(("w_gu", i), n2)
        gu = gu_fwd(n2, P["w_gu"][i], f"gate_up{i}")
        act = glu_fwd(gu, P["ffn_cw"][i], P["ffn_cb"][i], f"glu{i}")
        arrive(("w_down", i), act)
        h3 = mm_nn(act, P["w_down"][i], res=h2, name=f"down{i}")
        saved.append((h, n1, kv, proj, mix, h2, n2, gu, act, extra))
        h = h3

    loss, dh, dg_fin = loss_head(h, P["g_fin"], target)
    G = {"g_fin": dg_fin[0], "g_mix": [None, None], "g_mem": [None, None], "g_ffn": [None, None],
         "w_mem": [None, None], "w_out": [None, None], "w_gu": [None, None], "w_down": [None, None],
         "ffn_cw": [None, None], "ffn_cb": [None, None]}
    for i in (1, 0):
        hin, n1, kv, proj, mix, h2, n2, gu, act, extra = saved[i]
        dact = mm_nt(dh, P["w_down"][i], name=f"d_act{i}")
        G["w_down"][i] = mm_tn(act, dh, name=f"dw_down{i}")
        dgu, dcw, dcb = glu_bwd(gu, P["ffn_cw"][i], P["ffn_cb"][i], dact, f"glu_bwd{i}")
        G["ffn_cw"][i], G["ffn_cb"][i] = dcw, dcb[0]
        dn2 = gu_bwd_x(dgu, P["w_gu"][i], f"d_n2_{i}")
        G["w_gu"][i] = gu_bwd_w(n2, dgu, f"dw_gu{i}")
        g_ffn = ready(("ffn", i), G, P["g_ffn"][i])
        dh2, dg = rms_bwd(h2, g_ffn, dn2, dh, f"rms_ffn_bwd{i}")
        G["g_ffn"][i] = dg[0]
        dmix = mm_nt(dh2, P["w_out"][i], name=f"d_mix{i}")
        G["w_out"][i] = mm_tn(mix, dh2, name=f"dw_out{i}")
        if i == 0:
            dqkv, dbias, dsk = swa_bwd(proj, bias, sk, dmix)
            dxq, dkv = xattn_bwd(proj, A_Q + 2 * LANE, kv, dmix, "xattn_a_bwd")
            dproj = jnp.concatenate([dqkv, dxq], axis=1)
            G["sinks"] = dsk[:A_HEADS, 0]
            G["rel_bias"] = bias_grad(dbias)[:, :A_HEADS]
            w_in, gname = P["w_in_a"], "w_in_a"
        else:
            qkvn, chunked, inv, states, o = extra
            do, dz, dgo = dnpost_bwd(o, proj, prm, dmix)
            dqkvn, dseg, dprm = dnc_bwd(qkvn, proj, prm, inv, dns_bwd(chunked, states, do))
            draw, dconv = dnprep_bwd(proj, P["conv_qkv"], dqkvn)
            dxq, dkv = xattn_bwd(proj, 2304, kv, dmix, "xattn_b_bwd")
            dproj = jnp.concatenate([draw, dz, dxq, dseg], axis=1)
            G["conv_qkv"] = dconv
            G["a_log"], G["dt_bias"], G["out_norm_g"] = dprm[0, 6:12], dprm[1, 6:12], dgo[0]
            w_in, gname = P["w_in_b"], "w_in_b"
        dn1 = mm_nt(dproj, w_in, name=f"d_n1_{i}")
        G[gname] = mm_tn(n1, dproj, name=f"d{gname}")
        dh, dg = rms_bwd(hin, P["g_mix"][i], dn1, dh2, f"rms_mix_bwd{i}")
        G["g_mix"][i] = dg[0]
        dgm, dwm = memkv_bwd(mem, P["g_mem"][i], P["w_mem"][i], dkv, f"memkv_bwd{i}")
        G["g_mem"][i], G["w_mem"][i] = dgm[0], dwm
        ready(("mix", i), G, None)
    return loss, dh, G


def _prepare(full, w_gu=None):
    return {
        "rel_bias": full["rel_bias"], "sinks": full["sinks_a"][0], "a_log": full["a_log_b"][0],
        "dt_bias": full["dt_bias_b"][0], "out_norm_g": full["out_norm_g_b"][0],
        "g_mix": full["norm_mix_g"], "g_mem": full["norm_mem_g"], "g_ffn": full["norm_ffn_g"],
        "g_fin": full["final_norm_g"], "conv_qkv": full["conv_qkv_b"][0],
        "ffn_cw": [full["ffn_conv_w"][0], full["ffn_conv_w"][1]],
        "ffn_cb": [full["ffn_conv_b"][0], full["ffn_conv_b"][1]],
        "w_mem": [full["w_mem_kv"][0], full["w_mem_kv"][1]],
        "w_out": [_lay_out_a(full["w_out"][0]), full["w_out"][1]],
        "w_in_a": _lay_in_a(full["w_in_a"][0]), "w_in_b": _lay_in_b(full["w_in_b"][0]),
        "w_gu": w_gu if w_gu is not None else [_chip_cols(full["w_gate_up"][0]), _chip_cols(full["w_gate_up"][1])],
        "w_down": [full["w_down"][0], full["w_down"][1]],
    }


def _grads_to_ref(G):
    return {
        "rel_bias": G["rel_bias"], "norm_mix_g": jnp.stack(G["g_mix"]), "norm_mem_g": jnp.stack(G["g_mem"]),
        "w_mem_kv": jnp.stack(G["w_mem"]),
        "w_out": jnp.stack([_unlay_out_a(G["w_out"][0]), G["w_out"][1]]),
        "w_in_a": _unlay_in_a(G["w_in_a"])[None], "sinks_a": G["sinks"][None],
        "w_in_b": _unlay_in_b(G["w_in_b"])[None], "conv_qkv_b": G["conv_qkv"][None],
        "a_log_b": G["a_log"][None], "dt_bias_b": G["dt_bias"][None], "out_norm_g_b": G["out_norm_g"][None],
        "norm_ffn_g": jnp.stack(G["g_ffn"]),
        "w_gate_up": jnp.stack([_unchip_cols(G["w_gu"][0]), _unchip_cols(G["w_gu"][1])]).astype(f32),
        "ffn_conv_w": jnp.stack(G["ffn_cw"]), "ffn_conv_b": jnp.stack(G["ffn_cb"]),
        "w_down": jnp.stack(G["w_down"]), "final_norm_g": G["g_fin"],
    }


ANY = pl.BlockSpec(memory_space=pl.ANY)


def _place():
    return lax.axis_index("x"), lax.axis_index("y"), lax.axis_index("c")


def chip_scatter(gs):
    n = len(gs)

    def body(*refs):
        ins, outs = refs[:n], refs[n:2 * n]
        ssem, rsem = refs[2 * n:]
        x, y, c = _place()
        me = 2 * x + y
        peers = [(1 - x, y), (x, 1 - y), (1 - x, 1 - y)]

        def remote(j, k, slot):
            px, py = peers[k]
            return pltpu.make_async_remote_copy(
                src_ref=ins[j].at[2 * px + py], dst_ref=outs[j].at[slot],
                send_sem=ssem.at[3 * j + k], recv_sem=rsem.at[3 * j + k],
                device_id=(px, py, c), device_id_type=MESH)

        sends = [remote(j, k, me) for j in range(n) for k in range(3)]
        for cp in sends:
            cp.start()
        for j in range(n):
            for k in range(3):
                px, py = peers[k]
                remote(j, k, 2 * px + py).wait_recv()
        for cp in sends:
            cp.wait_send()

    return pl.pallas_call(
        body, name="grad_scatter", in_specs=[ANY] * n, out_specs=[ANY] * n,
        out_shape=[jax.ShapeDtypeStruct(g.shape, g.dtype) for g in gs],
        scratch_shapes=[pltpu.SemaphoreType.DMA((3 * n,)), pltpu.SemaphoreType.DMA((3 * n,))],
    )(*gs)


def allreduce_small(buf):
    R = buf.shape[0]

    def body(b_ref, o_ref, recv, ssem, rsem):
        x, y, c = _place()
        me = 4 * x + 2 * y + c

        def peer(k):
            return (1 - x if k & 4 else x, 1 - y if k & 2 else y, 1 - c if k & 1 else c)

        def remote(k, slot):
            return pltpu.make_async_remote_copy(
                src_ref=b_ref, dst_ref=recv.at[slot], send_sem=ssem.at[k - 1], recv_sem=rsem.at[k - 1],
                device_id=peer(k), device_id_type=MESH)

        sends = [remote(k, me) for k in range(1, 8)]
        for cp in sends:
            cp.start()
        recv[me] = b_ref[...]
        for k in range(1, 8):
            px, py, pc = peer(k)
            remote(k, 4 * px + 2 * py + pc).wait_recv()
        for cp in sends:
            cp.wait_send()
        total = recv[0]
        for j in range(1, 8):
            total = total + recv[j]
        o_ref[...] = total

    return pl.pallas_call(
        body, name="small_allreduce",
        in_specs=[pl.BlockSpec(memory_space=pltpu.VMEM)], out_specs=pl.BlockSpec(memory_space=pltpu.VMEM),
        out_shape=jax.ShapeDtypeStruct(buf.shape, f32),
        scratch_shapes=[pltpu.VMEM((8, R, LANE), f32), pltpu.SemaphoreType.DMA((7,)), pltpu.SemaphoreType.DMA((7,))],
    )(buf)


def sum_slots(own, recv, chip, core, name):
    _, R, C = recv.shape
    tr = _row_tile(R, 256)
    nt = R // tr

    def body(p_ref, a_ref, r_ref, o_ref):
        acc = jnp.zeros((tr, C), f32)
        for s in range(4):
            acc = acc + jnp.where(p_ref[0] == s, a_ref[s], r_ref[s]).astype(f32)
        o_ref[...] = acc

    slots = pl.BlockSpec((4, tr, C), lambda i, p_ref: (0, i, 0))
    return pl.pallas_call(
        body, name=name, out_shape=jax.ShapeDtypeStruct((2 * R, C), f32),
        grid_spec=pltpu.PrefetchScalarGridSpec(
            num_scalar_prefetch=1, grid=(nt,), in_specs=[slots, slots],
            out_specs=pl.BlockSpec((tr, C), lambda i, p_ref: (p_ref[1] * nt + i, 0))),
        compiler_params=_cp(("parallel",)),
    )(jnp.stack([chip, core]).astype(jnp.int32), own, recv)


def _half(ref, core, axis=0):
    half = ref.shape[axis] // 2
    idx = (slice(None),) * axis + (pl.ds(core * half, half),)
    return ref.at[idx]


IN_HBM = pl.BlockSpec(memory_space=pltpu.HBM)
IN_SEM = pl.BlockSpec(memory_space=pltpu.SEMAPHORE)
SIDE_EFFECT = pltpu.SideEffectType.DATAFLOW_SIDE_EFFECTING


def _gather_copy(buf, i, k, ssem, rsem, place, landing):
    x, y, c = place
    px, py = [(1 - x, y), (x, 1 - y), (1 - x, 1 - y)][k]
    me = 2 * x + y
    return pltpu.make_async_remote_copy(
        src_ref=buf.at[me], dst_ref=buf.at[me if landing == "theirs" else 2 * px + py],
        send_sem=ssem.at[3 * i + k], recv_sem=rsem.at[3 * i + k], device_id=(px, py, c), device_id_type=MESH)


def gather_start(groups):
    flat = [b for grp in groups for b in grp]
    n, ng = len(flat), len(groups)

    def body(*refs):
        bufs, sems = refs[:n], refs[n:n + 2 * ng]
        place = _place()
        j = 0
        for g, grp in enumerate(groups):
            for i in range(len(grp)):
                for k in range(3):
                    _gather_copy(bufs[j], i, k, sems[2 * g], sems[2 * g + 1], place, "theirs").start()
                j += 1

    sem_shapes = [pltpu.SemaphoreType.DMA((3 * len(grp),)) for grp in groups for _ in range(2)]
    out = pl.pallas_call(
        body, name="gather_start", in_specs=[IN_HBM] * n, out_specs=(*[IN_SEM] * (2 * ng), *[IN_HBM] * n),
        out_shape=(*sem_shapes, *[pltpu.HBM(b.shape, b.dtype) for b in flat]),
        input_output_aliases={i: 2 * ng + i for i in range(n)},
        compiler_params=pltpu.CompilerParams(has_side_effects=SIDE_EFFECT),
    )(*[pltpu.with_memory_space_constraint(b, pltpu.HBM) for b in flat])
    sems, bufs = out[:2 * ng], list(out[2 * ng:])
    flights, j = [], 0
    for g, grp in enumerate(groups):
        flights.append((bufs[j:j + len(grp)], sems[2 * g], sems[2 * g + 1]))
        j += len(grp)
    return flights


def gather_wait(flight, after, name):
    bufs, ssem, rsem = flight
    n = len(bufs)

    def body(*refs):
        place = _place()
        for i in range(n):
            for k in range(3):
                cp = _gather_copy(refs[i], i, k, refs[n], refs[n + 1], place, "mine")
                cp.wait_send()
                cp.wait_recv()

    return pl.pallas_call(
        body, name=name, in_specs=[IN_HBM] * n + [IN_SEM, IN_SEM, ANY], out_specs=[IN_HBM] * n,
        out_shape=[pltpu.HBM(b.shape, b.dtype) for b in bufs], input_output_aliases={i: i for i in range(n)},
        compiler_params=pltpu.CompilerParams(has_side_effects=SIDE_EFFECT),
    )(*bufs, ssem, rsem, after)


def _scatter_copy(src, land, j, k, ssem, rsem, place, landing):
    x, y, c = place
    px, py = [(1 - x, y), (x, 1 - y), (1 - x, 1 - y)][k]
    return pltpu.make_async_remote_copy(
        src_ref=src.at[2 * px + py], dst_ref=land.at[2 * x + y if landing == "theirs" else 2 * px + py],
        send_sem=ssem.at[3 * j + k], recv_sem=rsem.at[3 * j + k], device_id=(px, py, c), device_id_type=MESH)


def scatter_start(srcs, name):
    n = len(srcs)
    lands = [lax.empty(g.shape, g.dtype) for g in srcs]

    def body(*refs):
        place = _place()
        for j in range(n):
            for k in range(3):
                _scatter_copy(refs[j], refs[n + j], j, k, refs[2 * n], refs[2 * n + 1], place, "theirs").start()
        refs[-1][...] = jnp.zeros_like(refs[-1])

    sem = pltpu.SemaphoreType.DMA((3 * n,))
    hbm = [pltpu.with_memory_space_constraint(b, pltpu.HBM) for b in list(srcs) + lands]
    out = pl.pallas_call(
        body, name=name, in_specs=[IN_HBM] * (2 * n),
        out_specs=(IN_SEM, IN_SEM, *[IN_HBM] * (2 * n), pl.BlockSpec(memory_space=pltpu.VMEM)),
        out_shape=(sem, sem, *[pltpu.HBM(b.shape, b.dtype) for b in hbm], jax.ShapeDtypeStruct((8, LANE), f32)),
        input_output_aliases={i: 2 + i for i in range(2 * n)},
        compiler_params=pltpu.CompilerParams(has_side_effects=SIDE_EFFECT),
    )(*hbm)
    return (list(out[2:2 + n]), list(out[2 + n:2 + 2 * n]), out[0], out[1]), out[-1]


def scatter_wait(flight, after, name):
    srcs, lands, ssem, rsem = flight
    n = len(srcs)

    def body(*refs):
        place = _place()
        for j in range(n):
            for k in range(3):
                cp = _scatter_copy(refs[j], refs[n + j], j, k, refs[2 * n], refs[2 * n + 1], place, "mine")
                cp.wait_send()
                cp.wait_recv()

    out = pl.pallas_call(
        body, name=name, in_specs=[IN_HBM] * (2 * n) + [IN_SEM, IN_SEM, ANY], out_specs=[IN_HBM] * (2 * n),
        out_shape=[pltpu.HBM(b.shape, b.dtype) for b in list(srcs) + list(lands)],
        input_output_aliases={i: i for i in range(2 * n)},
        compiler_params=pltpu.CompilerParams(has_side_effects=SIDE_EFFECT),
    )(*srcs, *lands, ssem, rsem, after)
    return list(out[:n]), list(out[n:])


def pair_exchange(gbufs, name):
    n = len(gbufs)

    def body(*refs):
        ins, outs = refs[:n], refs[n:2 * n]
        ssem, rsem = refs[2 * n:]
        x, y, c = _place()
        cps = [pltpu.make_async_remote_copy(
            src_ref=_half(ins[j], 1 - c, axis=1), dst_ref=outs[j], send_sem=ssem.at[j], recv_sem=rsem.at[j],
            device_id=(x, y, 1 - c), device_id_type=MESH) for j in range(n)]
        for cp in cps:
            cp.start()
        for cp in cps:
            cp.wait()

    return pl.pallas_call(
        body, name=name, in_specs=[ANY] * n, out_specs=[ANY] * n,
        out_shape=[jax.ShapeDtypeStruct((4, g.shape[1] // 2, g.shape[2]), g.dtype) for g in gbufs],
        scratch_shapes=[pltpu.SemaphoreType.DMA((n,)), pltpu.SemaphoreType.DMA((n,))],
    )(*gbufs)


def _row_tile(rows, cap=512):
    return max(t for t in range(16, min(rows, cap) + 1, 16) if rows % t == 0)


def pair_sum(mine, theirs, core, name):
    _, R, C = mine.shape
    half = R // 2
    tr = _row_tile(half)
    nt = half // tr

    def body(c_ref, a_ref, b_ref, o_ref):
        o_ref[...] = (a_ref[...].astype(f32) + b_ref[...].astype(f32)).astype(bf16)

    return pl.pallas_call(
        body, name=name, out_shape=jax.ShapeDtypeStruct(theirs.shape, bf16),
        grid_spec=pltpu.PrefetchScalarGridSpec(
            num_scalar_prefetch=1, grid=(4, nt),
            in_specs=[pl.BlockSpec((None, tr, C), lambda s, i, c_ref: (s, c_ref[0] * nt + i, 0)),
                      pl.BlockSpec((None, tr, C), lambda s, i, c_ref: (s, i, 0))],
            out_specs=pl.BlockSpec((None, tr, C), lambda s, i, c_ref: (s, i, 0))),
        compiler_params=_cp(("parallel", "parallel")),
    )(jnp.reshape(core, (1,)).astype(jnp.int32), mine, theirs)


def final_exchange(fins):
    n = len(fins)

    def body(*refs):
        outs = refs[n:2 * n]
        ssem, rsem = refs[2 * n:]
        x, y, c = _place()
        cps = [pltpu.make_async_remote_copy(
            src_ref=_half(outs[j], c), dst_ref=_half(outs[j], c), send_sem=ssem.at[j], recv_sem=rsem.at[j],
            device_id=(x, y, 1 - c), device_id_type=MESH) for j in range(n)]
        for cp in cps:
            cp.start()
        for cp in cps:
            cp.wait()

    return pl.pallas_call(
        body, name="final_exchange", in_specs=[ANY] * n, out_specs=[ANY] * n,
        out_shape=[jax.ShapeDtypeStruct(f.shape, f.dtype) for f in fins],
        input_output_aliases={j: j for j in range(n)},
        scratch_shapes=[pltpu.SemaphoreType.DMA((n,)), pltpu.SemaphoreType.DMA((n,))],
    )(*fins)


def adamw_big(w, m, v, gs, row0, name):
    L, R, C = w.shape
    tr = _row_tile(math.gcd(R, row0) if row0 else R, max(16, 262144 // C // 16 * 16))
    b0 = row0 // tr

    def body(*refs):
        w_ref, m_ref, v_ref = refs[:3]
        g_refs = refs[3:3 + L]
        g_ref, d_ref, nm_ref, nv_ref = refs[3 + L:]
        g = g_refs[0][...]
        for l in range(1, L):
            g = jnp.where(pl.program_id(0) == l, g_refs[l][...], g)
        d, nm, nv = _adamw_math(w_ref[...], g, m_ref[...], v_ref[...])
        g_ref[...] = g
        d_ref[...] = d
        nm_ref[...] = nm
        nv_ref[...] = nv

    own = pl.BlockSpec((None, tr, C), lambda l, i: (l, i, 0))
    off = pl.BlockSpec((tr, C), lambda l, i: (b0 + i, 0))
    return pl.pallas_call(
        body, name=name, grid=(L, R // tr), in_specs=[own, own, own] + [off] * L, out_specs=[own] * 4,
        out_shape=[jax.ShapeDtypeStruct((L, R, C), f32)] * 4, compiler_params=_cp(("parallel", "parallel")),
    )(w, m, v, *gs)


def _adamw_math(w, g, m, v):
    m = B1 * m + (1.0 - B1) * g
    v = B2 * v + (1.0 - B2) * (g * g)
    m_hat = m / (1.0 - B1 ** STEP)
    v_hat = v / (1.0 - B2 ** STEP)
    delta = -LR * (m_hat / (jnp.sqrt(v_hat) + AEPS) + WD * w)
    return delta, m, v


def adamw_small(w, m, v, g):
    def body(w_ref, m_ref, v_ref, g_ref, d_ref, nm_ref, nv_ref):
        d, nm, nv = _adamw_math(w_ref[...], g_ref[...], m_ref[...], v_ref[...])
        d_ref[...] = d
        nm_ref[...] = nm
        nv_ref[...] = nv

    return pl.pallas_call(body, name="adamw_small", out_shape=[jax.ShapeDtypeStruct(w.shape, f32)] * 3)(w, m, v, g)


CONV =(("conv_qkv_b", 2), ("ffn_conv_w", 2))
SMALL = ("rel_bias", "norm_mix_g", "norm_mem_g", "sinks_a", "a_log_b", "dt_bias_b", "out_norm_g_b", "norm_ffn_g",
         "ffn_conv_b", "final_norm_g")
WEIGHTS = ("rel_bias", "norm_mix_g", "norm_mem_g", "w_mem_kv", "w_out", "w_in_a", "sinks_a", "w_in_b", "conv_qkv_b",
           "a_log_b", "dt_bias_b", "out_norm_g_b", "norm_ffn_g", "w_gate_up", "ffn_conv_w", "ffn_conv_b", "w_down",
           "final_norm_g")
ARGS = ("x", "mem") + WEIGHTS + ("loss_target",) + tuple("m_" + n for n in WEIGHTS) + tuple("v_" + n for n in WEIGHTS)


def _rows(a, width):
    flat = a.reshape(-1)
    pad = (-flat.shape[0]) % (8 * width)
    if pad:
        flat = jnp.concatenate([flat, jnp.zeros((pad,), a.dtype)])
    return flat.reshape(-1, width)


def _nrows(shape, width):
    return _pad_to(-(-math.prod(shape) // width), 8)


def _pack(arrs, width, total_rows, dtype):
    parts = [_rows(a.astype(dtype), width) for a in arrs]
    used = sum(p.shape[0] for p in parts)
    if total_rows > used:
        parts.append(jnp.zeros((total_rows - used, width), dtype))
    return jnp.concatenate(parts, axis=0)


def _unpack(buf, shapes, width):
    out, r = [], 0
    for s in shapes:
        n = _nrows(s, width)
        out.append(buf[r:r + n].reshape(-1)[:math.prod(s)].reshape(s))
        r += n
    return out


def _pad_to(n, mult):
    return -(-n // mult) * mult


def kernel(x, mem, rel_bias, norm_mix_g, norm_mem_g, w_mem_kv, w_out, w_in_a, sinks_a, w_in_b, conv_qkv_b, a_log_b, dt_bias_b, out_norm_g_b, norm_ffn_g, w_gate_up, ffn_conv_w, ffn_conv_b, w_down, final_norm_g, loss_target, m_rel_bias, m_norm_mix_g, m_norm_mem_g, m_w_mem_kv, m_w_out, m_w_in_a, m_sinks_a, m_w_in_b, m_conv_qkv_b, m_a_log_b, m_dt_bias_b, m_out_norm_g_b, m_norm_ffn_g, m_w_gate_up, m_ffn_conv_w, m_ffn_conv_b, m_w_down, m_final_norm_g, v_rel_bias, v_norm_mix_g, v_norm_mem_g, v_w_mem_kv, v_w_out, v_w_in_a, v_sinks_a, v_w_in_b, v_conv_qkv_b, v_a_log_b, v_dt_bias_b, v_out_norm_g_b, v_norm_ffn_g, v_w_gate_up, v_ffn_conv_w, v_ffn_conv_b, v_w_down, v_final_norm_g):
    A = dict(zip(ARGS, (x, mem, rel_bias, norm_mix_g, norm_mem_g, w_mem_kv, w_out, w_in_a, sinks_a, w_in_b, conv_qkv_b, a_log_b, dt_bias_b, out_norm_g_b, norm_ffn_g, w_gate_up, ffn_conv_w, ffn_conv_b, w_down, final_norm_g, loss_target, m_rel_bias, m_norm_mix_g, m_norm_mem_g, m_w_mem_kv, m_w_out, m_w_in_a, m_sinks_a, m_w_in_b, m_conv_qkv_b, m_a_log_b, m_dt_bias_b, m_out_norm_g_b, m_norm_ffn_g, m_w_gate_up, m_ffn_conv_w, m_ffn_conv_b, m_w_down, m_final_norm_g, v_rel_bias, v_norm_mix_g, v_norm_mem_g, v_w_mem_kv, v_w_out, v_w_in_a, v_sinks_a, v_w_in_b, v_conv_qkv_b, v_a_log_b, v_dt_bias_b, v_out_norm_g_b, v_norm_ffn_g, v_w_gate_up, v_ffn_conv_w, v_ffn_conv_b, v_w_down, v_final_norm_g)))
    chip = 2 * lax.axis_index("x") + lax.axis_index("y")
    core = lax.axis_index("c")
    n_down, n_out, n_mem = w_down.shape[1], w_out.shape[1], w_mem_kv.shape[1]

    def own_slot(shard):
        return lax.dynamic_update_index_in_dim(lax.empty((4,) + shard.shape, shard.dtype), shard, chip, 0)

    def bslot(w):
        return own_slot(w.astype(bf16))

    groups = {
        ("w_in", 0): [bslot(w_in_a[0])],
        ("w_mem", 0): [bslot(w_mem_kv[0]), bslot(w_mem_kv[1]), own_slot(conv_qkv_b[0]),
                       own_slot(ffn_conv_w.reshape(6, -1))],
        ("w_out", 0): [bslot(w_out[0])], ("w_gu", 0): [bslot(w_gate_up[0])], ("w_down", 0): [bslot(w_down[0])],
        ("w_in", 1): [bslot(w_in_b[0])],
        ("w_out", 1): [bslot(w_out[1])], ("w_gu", 1): [bslot(w_gate_up[1])], ("w_down", 1): [bslot(w_down[1])],
    }
    flights = dict(zip(groups, gather_start(list(groups.values()))))
    P = {"rel_bias": rel_bias, "sinks": sinks_a[0], "a_log": a_log_b[0], "dt_bias": dt_bias_b[0],
         "out_norm_g": out_norm_g_b[0], "g_mix": norm_mix_g, "g_mem": norm_mem_g, "g_ffn": norm_ffn_g,
         "g_fin": final_norm_g, "ffn_cb": [ffn_conv_b[0], ffn_conv_b[1]], "w_mem": [None, None], "w_out": [None, None],
         "w_gu": [None, None], "w_down": [None, None], "ffn_cw": [None, None]}

    def rows4(g):
        return g.reshape(4 * g.shape[1], g.shape[2])

    def arrive(key, after):
        if key not in flights:
            return
        got = gather_wait(flights.pop(key), after, "gather_wait_%s%d" % key)
        name, i = key
        if name == "w_in":
            P["w_in_a" if i == 0 else "w_in_b"] = (_lay_in_a if i == 0 else _lay_in_b)(_unchip_cols(got[0]))
        elif name == "w_mem":
            P["w_mem"] = [rows4(got[0]), rows4(got[1])]
            P["conv_qkv"] = _unchip_cols(got[2])
            cw = _unchip_cols(got[3]).reshape(2, 3, D_FF)
            P["ffn_cw"] = [cw[0], cw[1]]
        elif name == "w_out":
            P["w_out"][i] = _lay_out_a(rows4(got[0])) if i == 0 else rows4(got[0])
        elif name == "w_gu":
            P["w_gu"][i] = got[0]
        else:
            P["w_down"][i] = rows4(got[0])

    def chip_rows(g):
        return g.reshape(4, g.shape[0] // 4, g.shape[-1])

    sent, started = {}, []

    def ready(key, G, dep):
        kind, i = key
        tag = "%s%d" % key
        if kind == "ffn":
            names, partial = ("gu", "down"), [G["w_gu"][i], chip_rows(G["w_down"][i]).astype(bf16)]
        else:
            g_out = _unlay_out_a(G["w_out"][0]) if i == 0 else G["w_out"][1]
            g_in = _unlay_in_a(G["w_in_a"]) if i == 0 else _unlay_in_b(G["w_in_b"])
            names = ("out", "in", "mem")
            partial = [chip_rows(g_out).astype(bf16), _chip_cols(g_in).astype(bf16), chip_rows(G["w_mem"][i]).astype(bf16)]
        theirs = pair_exchange(partial, "pair_exchange_" + tag)
        pair = [pair_sum(p, t, core, "pair_sum_%s%d" % (nm, i)) for p, t, nm in zip(partial, theirs, names)]
        if key == ("mix", 0):
            sent[key] = (names, pair, chip_scatter(pair))
            return dep
        flight, token = scatter_start(pair, "scatter_start_" + tag)
        sent[key] = (names, flight)
        started.append(token[0, 0])
        if dep is not None:
            while started:
                dep = dep + started.pop()
        return dep

    P["arrive"], P["ready"] = arrive, ready

    loss, dx, G = _local_step(x[0], mem[0], loss_target[0], P)
    gfull = _grads_to_ref(G)

    fin = {}
    for key in (("ffn", 1), ("mix", 1), ("ffn", 0), ("mix", 0)):
        if key == ("mix", 0):
            names, pair, arrived = sent[key]
        else:
            names, flight = sent[key]
            pair, arrived = scatter_wait(flight, dx, "scatter_wait_%s%d" % key)
        for nm, p, r in zip(names, pair, arrived):
            fin[nm, key[1]] = sum_slots(p, r, chip, core, "sum_slots_%s%d" % (nm, key[1]))
    order = list(fin)
    done = dict(zip(order, final_exchange([fin[k] for k in order])))

    sm_shapes = [A[n].shape for n in SMALL] + [gfull[n].shape for n, _ in CONV] + [(LANE,)]
    sm_rows = _pad_to(sum(_nrows(s, LANE) for s in sm_shapes), 8)
    sbuf = _pack([gfull[n] for n in SMALL] + [gfull[n] for n, _ in CONV] + [loss[0]], LANE, sm_rows, f32)
    tot = _unpack(allreduce_small(sbuf), sm_shapes, LANE)
    gsmall = dict(zip(SMALL, tot[:len(SMALL)]))
    for (n, axis), t in zip(CONV, tot[len(SMALL):len(SMALL) + len(CONV)]):
        sh = A[n].shape[axis]
        gsmall[n] = lax.dynamic_slice_in_dim(t, chip * sh, sh, axis)
    loss_out = tot[-1][0]

    out = {}
    plan = (("w_gate_up", [done["gu", 0], done["gu", 1]]), ("w_down", [done["down", 0], done["down", 1]]),
            ("w_out", [done["out", 0], done["out", 1]]), ("w_mem_kv", [done["mem", 0], done["mem", 1]]),
            ("w_in_a", [done["in", 0]]), ("w_in_b", [done["in", 1]]))
    for n, gs in plan:
        shape3 = (len(gs),) + gs[0].shape
        res = adamw_big(A[n].reshape(shape3), A["m_" + n].reshape(shape3), A["v_" + n].reshape(shape3), gs, 0,
                        "adamw_" + n)
        for key, r in zip(("grad_", "delta_", "new_m_", "new_v_"), res):
            out[key + n] = r.reshape(A[n].shape)
    names = SMALL + tuple(n for n, _ in CONV)
    shapes = [A[n].shape for n in names]
    rows = _pad_to(sum(_nrows(s, LANE) for s in shapes), 8)
    packs = [_pack([src[n] for n in names], LANE, rows, f32)
             for src in ({n: A[n] for n in names}, {n: A["m_" + n] for n in names}, {n: A["v_" + n] for n in names}, gsmall)]
    res = adamw_small(*packs)
    for key, r in zip(("delta_", "new_m_", "new_v_"), res):
        for n, a in zip(names, _unpack(r, shapes, LANE)):
            out[key + n] = a
    for n in names:
        out["grad_" + n] = gsmall[n]
    return (loss_out, dx[None], *[out["grad_" + n] for n in WEIGHTS], *[out["delta_" + n] for n in WEIGHTS],
            *[out["new_m_" + n] for n in WEIGHTS], *[out["new_v_" + n] for n in WEIGHTS])
```

```python
import functools
import math

import numpy as np
import jax
import jax.numpy as jnp
from jax import lax
from jax.experimental import pallas as pl
from jax.experimental.pallas import tpu as pltpu

f32 = jnp.float32
bf16 = jnp.bfloat16
HI = lax.Precision.HIGHEST
MESH = pl.DeviceIdType.MESH

D = 1024
MEM_LEN = 256
EPS = 1e-6
A_HEADS, A_KV, A_DH = 12, 2, 64
A_Q = 768
BLK = 128
N_BUCKETS, MAX_DIST = 32, 128
B_QK, B_V, B_DH = 384, 768, 128
B_QKV = 1536
CHUNK = 64
X_Q = 256
D_FF = 2816
IN_A = 1280
IN_B = 2572
IN_B_PAD = 2688
LANE = 128
VMEM_LIMIT = 56 * 1024 * 1024
MM_ROWS = 1024

LR, B1, B2, AEPS, WD, STEP = 0.001, 0.9, 0.999, 1e-08, 0.01, 10


def _cp(sem=None):
    return pltpu.CompilerParams(dimension_semantics=sem, vmem_limit_bytes=VMEM_LIMIT)


def _dg(a, b, ca, cb, prec=None):
    return lax.dot_general(a, b, (((ca,), (cb,)), ((), ())), precision=prec, preferred_element_type=f32)


@jax.custom_vjp
def bdot(a, b):
    return _dg(a.astype(bf16), b.astype(bf16), 1, 0)


def _bdot_f(a, b):
    return bdot(a, b), (a, b)


def _bdot_b(res, g):
    a, b = res
    gb = g.astype(bf16)
    return _dg(gb, b.astype(bf16), 1, 1), _dg(a.astype(bf16), gb, 0, 0)


bdot.defvjp(_bdot_f, _bdot_b)


@jax.custom_vjp
def bdot_nt(a, b):
    return _dg(a.astype(bf16), b.astype(bf16), 1, 1)


def _bdot_nt_f(a, b):
    return bdot_nt(a, b), (a, b)


def _bdot_nt_b(res, g):
    a, b = res
    gb = g.astype(bf16)
    return _dg(gb, b.astype(bf16), 1, 0), _dg(gb, a.astype(bf16), 0, 0)


bdot_nt.defvjp(_bdot_nt_f, _bdot_nt_b)


def _shift_rows(x, s, down):
    n = x.shape[0]
    row = lax.broadcasted_iota(jnp.int32, x.shape, 0)
    if down:
        return jnp.where(row >= s, pltpu.roll(x, s, 0), 0.0)
    return jnp.where(row < n - s, pltpu.roll(x, n - s, 0), 0.0)


@functools.partial(jax.custom_vjp, nondiff_argnums=(1,))
def shift_down(x, s):
    return _shift_rows(x, s, True)


def _sd_f(x, s):
    return _shift_rows(x, s, True), None


def _sd_b(s, _, g):
    return (_shift_rows(g, s, False),)


shift_down.defvjp(_sd_f, _sd_b)


def _sigmoid(x):
    return 1.0 / (1.0 + jnp.exp(-x))


def _silu(x):
    return x * _sigmoid(x)


def _rms(x, g):
    return x * lax.rsqrt(jnp.mean(x * x, axis=-1, keepdims=True) + EPS) * g


def _tile(n, cap):
    u = n // LANE
    best = 1
    for d in range(1, u + 1):
        if u % d == 0 and d * LANE <= cap:
            best = d
    return best * LANE


def mm_nn(a, w, res=None, out_dtype=f32, name="mm_nn"):
    M, K = a.shape
    N = w.shape[1]
    tm, tn = min(MM_ROWS, M), _tile(N, 1024)

    def body(*refs):
        if res is None:
            a_ref, w_ref, o_ref = refs
            o_ref[...] = _dg(a_ref[...].astype(bf16), w_ref[...], 1, 0).astype(out_dtype)
        else:
            a_ref, w_ref, r_ref, o_ref = refs
            o_ref[...] = (r_ref[...] + _dg(a_ref[...].astype(bf16), w_ref[...], 1, 0)).astype(out_dtype)

    in_specs = [pl.BlockSpec((tm, K), lambda n, m: (m, 0)), pl.BlockSpec((K, tn), lambda n, m: (0, n))]
    args = [a, w]
    if res is not None:
        in_specs.append(pl.BlockSpec((tm, tn), lambda n, m: (m, n)))
        args.append(res)
    return pl.pallas_call(
        body, name=name, grid=(N // tn, M // tm), in_specs=in_specs,
        out_specs=pl.BlockSpec((tm, tn), lambda n, m: (m, n)),
        out_shape=jax.ShapeDtypeStruct((M, N), out_dtype),
        compiler_params=_cp(("parallel", "parallel")),
    )(*args)


def mm_nt(dy, w, out_dtype=f32, name="mm_nt"):
    M, N = dy.shape
    K = w.shape[0]
    tm, tn = min(MM_ROWS, M), _tile(N, 1024)
    assert out_dtype == f32 or tn == N

    def body(dy_ref, w_ref, o_ref):
        part = _dg(dy_ref[...].astype(bf16), w_ref[...], 1, 1)
        if tn == N:
            o_ref[...] = part.astype(out_dtype)
        else:
            @pl.when(pl.program_id(1) == 0)
            def _():
                o_ref[...] = jnp.zeros_like(o_ref)
            o_ref[...] += part

    return pl.pallas_call(
        body, name=name, grid=(M // tm, N // tn),
        in_specs=[pl.BlockSpec((tm, tn), lambda m, n: (m, n)), pl.BlockSpec((K, tn), lambda m, n: (0, n))],
        out_specs=pl.BlockSpec((tm, K), lambda m, n: (m, 0)),
        out_shape=jax.ShapeDtypeStruct((M, K), out_dtype),
        compiler_params=_cp(("parallel", "arbitrary")),
    )(dy, w)


def mm_tn(a, dy, name="mm_tn"):
    M, K = a.shape
    N = dy.shape[1]
    tm, tk, tn = min(MM_ROWS, M), _tile(K, 1408), _tile(N, 1024)

    def body(a_ref, dy_ref, o_ref):
        @pl.when(pl.program_id(2) == 0)
        def _():
            o_ref[...] = jnp.zeros_like(o_ref)
        o_ref[...] += _dg(a_ref[...].astype(bf16), dy_ref[...].astype(bf16), 0, 0)

    return pl.pallas_call(
        body, name=name, grid=(K // tk, N // tn, M // tm),
        in_specs=[pl.BlockSpec((tm, tk), lambda k, n, m: (m, k)), pl.BlockSpec((tm, tn), lambda k, n, m: (m, n))],
        out_specs=pl.BlockSpec((tk, tn), lambda k, n, m: (k, n)),
        out_shape=jax.ShapeDtypeStruct((K, N), f32),
        compiler_params=_cp(("parallel", "parallel", "arbitrary")),
    )(a, dy)


def rms_fwd(h, g, name):
    S = h.shape[0]
    t = min(512, S)

    def body(h_ref, g_ref, o_ref):
        o_ref[...] = _rms(h_ref[...], g_ref[...]).astype(bf16)

    return pl.pallas_call(
        body, name=name, grid=(S // t,),
        in_specs=[pl.BlockSpec((t, D), lambda i: (i, 0)), pl.BlockSpec((1, D), lambda i: (0, 0))],
        out_specs=pl.BlockSpec((t, D), lambda i: (i, 0)),
        out_shape=jax.ShapeDtypeStruct((S, D), bf16),
        compiler_params=_cp(("parallel",)),
    )(h, g.reshape(1, D))


def rms_bwd(h, g, dn, dres, name):
    S = h.shape[0]
    t = min(512, S)

    def body(h_ref, g_ref, dn_ref, dr_ref, dh_ref, dg_ref):
        @pl.when(pl.program_id(0) == 0)
        def _():
            dg_ref[...] = jnp.zeros_like(dg_ref)
        _, vjp = jax.vjp(_rms, h_ref[...], g_ref[...])
        dh, dg = vjp(dn_ref[...])
        dh_ref[...] = dr_ref[...] + dh
        dg_ref[...] += dg

    tok = pl.BlockSpec((t, D), lambda i: (i, 0))
    vec = pl.BlockSpec((1, D), lambda i: (0, 0))
    return pl.pallas_call(
        body, name=name, grid=(S // t,), in_specs=[tok, vec, tok, tok], out_specs=[tok, vec],
        out_shape=[jax.ShapeDtypeStruct((S, D), f32), jax.ShapeDtypeStruct((1, D), f32)],
        compiler_params=_cp(("arbitrary",)),
    )(h, g.reshape(1, D), dn, dres)


def loss_head(h, g, target):
    S = h.shape[0]
    t = min(512, S)

    def f(hh, gg, tt):
        err = _rms(hh, gg) - tt
        return 0.5 * jnp.sum(jnp.mean(err * err, axis=-1, keepdims=True), axis=0, keepdims=True)

    def body(h_ref, g_ref, t_ref, loss_ref, dh_ref, dg_ref):
        @pl.when(pl.program_id(0) == 0)
        def _():
            dg_ref[...] = jnp.zeros_like(dg_ref)
            loss_ref[...] = jnp.zeros_like(loss_ref)
        val, vjp = jax.vjp(lambda a, b: f(a, b, t_ref[...]), h_ref[...], g_ref[...])
        dh, dg = vjp(jnp.ones((1, 1), f32))
        dh_ref[...] = dh
        dg_ref[...] += dg
        loss_ref[...] += jnp.broadcast_to(val, loss_ref.shape)

    tok = pl.BlockSpec((t, D), lambda i: (i, 0))
    vec = pl.BlockSpec((1, D), lambda i: (0, 0))
    return pl.pallas_call(
        body, name="loss_head", grid=(S // t,), in_specs=[tok, vec, tok],
        out_specs=[pl.BlockSpec((1, LANE), lambda i: (0, 0)), tok, vec],
        out_shape=[jax.ShapeDtypeStruct((1, LANE), f32), jax.ShapeDtypeStruct((S, D), f32),
                   jax.ShapeDtypeStruct((1, D), f32)],
        compiler_params=_cp(("arbitrary",)),
    )(h, g.reshape(1, D), target)


def memkv_fwd(mem, g, w, name):
    def body(m_ref, g_ref, w_ref, o_ref):
        o_ref[...] = _dg(_rms(m_ref[...], g_ref[...]).astype(bf16), w_ref[...], 1, 0)

    return pl.pallas_call(
        body, name=name, out_shape=jax.ShapeDtypeStruct((MEM_LEN, 2 * X_Q), f32), compiler_params=_cp(),
    )(mem, g.reshape(1, D), w)


def memkv_bwd(mem, g, w, dkv, name):
    def body(m_ref, g_ref, w_ref, d_ref, dg_ref, dw_ref):
        n, vjp = jax.vjp(lambda gg: _rms(m_ref[...], gg), g_ref[...])
        db = d_ref[...].astype(bf16)
        dw_ref[...] = _dg(n.astype(bf16), db, 0, 0)
        dg_ref[...] = vjp(_dg(db, w_ref[...], 1, 1))[0]

    return pl.pallas_call(
        body, name=name,
        out_shape=[jax.ShapeDtypeStruct((1, D), f32), jax.ShapeDtypeStruct((D, 2 * X_Q), f32)],
        compiler_params=_cp(),
    )(mem, g.reshape(1, D), w, dkv)


def _xattn_f(xq, mk, mv):
    lane = lax.broadcasted_iota(jnp.int32, (1, X_Q), 1)
    out = jnp.zeros(xq.shape, f32)
    for hd in range(4):
        msk = (lane // 64 == hd).astype(f32)
        s = bdot_nt(xq * msk, mk) * (64 ** -0.5)
        m = lax.stop_gradient(jnp.max(s, axis=-1, keepdims=True))
        p = jnp.exp(s - m)
        p = p / jnp.sum(p, axis=-1, keepdims=True)
        out = out + bdot(p, mv * msk)
    return out


def xattn_fwd(proj, col, kv, name):
    S = proj.shape[0]
    t = min(512, S)
    cb = col // X_Q

    def body(q_ref, k_ref, v_ref, o_ref):
        o_ref[...] = _xattn_f(q_ref[...], k_ref[...], v_ref[...]).astype(bf16)

    return pl.pallas_call(
        body, name=name, grid=(S // t,),
        in_specs=[pl.BlockSpec((t, X_Q), lambda i: (i, cb)), pl.BlockSpec((MEM_LEN, X_Q), lambda i: (0, 0)),
                  pl.BlockSpec((MEM_LEN, X_Q), lambda i: (0, 1))],
        out_specs=pl.BlockSpec((t, X_Q), lambda i: (i, 0)),
        out_shape=jax.ShapeDtypeStruct((S, X_Q), bf16),
        compiler_params=_cp(("parallel",)),
    )(proj, kv, kv)


def xattn_bwd(proj, col, kv, dmix, name):
    S = proj.shape[0]
    t = min(512, S)
    cb = col // X_Q

    def body(q_ref, k_ref, v_ref, do_ref, dq_ref, dk_ref, dv_ref):
        @pl.when(pl.program_id(0) == 0)
        def _():
            dk_ref[...] = jnp.zeros_like(dk_ref)
            dv_ref[...] = jnp.zeros_like(dv_ref)
        _, vjp = jax.vjp(_xattn_f, q_ref[...], k_ref[...], v_ref[...])
        dq, dk, dv = vjp(do_ref[...])
        dq_ref[...] = dq.astype(bf16)
        dk_ref[...] += dk
        dv_ref[...] += dv

    kvb = pl.BlockSpec((MEM_LEN, X_Q), lambda i: (0, 0))
    dq, dk, dv = pl.pallas_call(
        body, name=name, grid=(S // t,),
        in_specs=[pl.BlockSpec((t, X_Q), lambda i: (i, cb)), kvb,
                  pl.BlockSpec((MEM_LEN, X_Q), lambda i: (0, 1)), pl.BlockSpec((t, X_Q), lambda i: (i, 3))],
        out_specs=[pl.BlockSpec((t, X_Q), lambda i: (i, 0)), kvb, kvb],
        out_shape=[jax.ShapeDtypeStruct((S, X_Q), bf16), jax.ShapeDtypeStruct((MEM_LEN, X_Q), f32),
                   jax.ShapeDtypeStruct((MEM_LEN, X_Q), f32)],
        compiler_params=_cp(("arbitrary",)),
    )(proj, kv, kv, dmix)
    return dq, jnp.concatenate([dk, dv], axis=1)


def _bucket_map():
    qi = np.arange(BLK)[:, None]
    kj = np.arange(2 * BLK)[None, :]
    n = np.maximum(BLK + qi - kj, 0)
    max_exact = N_BUCKETS // 2
    nf = np.maximum(n, 1).astype(np.float64)
    large = max_exact + (np.log(nf / max_exact) / math.log(MAX_DIST / max_exact)
                         * (N_BUCKETS - max_exact)).astype(np.int32)
    large = np.minimum(large, N_BUCKETS - 1)
    return np.where(n < max_exact, n, large).astype(np.int32)


def bias_build(rel_bias):
    def body(rb_ref, bk_ref, o_ref):
        bk = bk_ref[...]
        for h in range(A_HEADS):
            acc = jnp.zeros((BLK, 2 * BLK), f32)
            for b in range(N_BUCKETS):
                acc = jnp.where(bk == b, rb_ref[b, h], acc)
            o_ref[h] = acc

    return pl.pallas_call(
        body, name="bias_build",
        in_specs=[pl.BlockSpec(memory_space=pltpu.SMEM), pl.BlockSpec(memory_space=pltpu.VMEM)],
        out_specs=pl.BlockSpec(memory_space=pltpu.VMEM),
        out_shape=jax.ShapeDtypeStruct((A_HEADS, BLK, 2 * BLK), f32), compiler_params=_cp(),
    )(rel_bias, jnp.asarray(_bucket_map()))


def bias_grad(dbias):
    def body(d_ref, bk_ref, o_ref):
        bk = bk_ref[...]
        row = lax.broadcasted_iota(jnp.int32, (N_BUCKETS, LANE), 0)
        lane = lax.broadcasted_iota(jnp.int32, (N_BUCKETS, LANE), 1)
        acc = jnp.zeros((N_BUCKETS, LANE), f32)
        for h in range(A_HEADS):
            d = d_ref[h]
            for b in range(N_BUCKETS):
                s = jnp.sum(jnp.where(bk == b, d, 0.0), keepdims=True)
                acc = acc + jnp.where((row == b) & (lane == h), s, 0.0)
        o_ref[...] = acc

    return pl.pallas_call(
        body, name="bias_grad", out_shape=jax.ShapeDtypeStruct((N_BUCKETS, LANE), f32), compiler_params=_cp(),
    )(dbias, jnp.asarray(_bucket_map()))


def _swa_f(qb, kp, kc, vp, vc, bias, sk, first):
    kband = jnp.concatenate([kp, kc], axis=0)
    vband = jnp.concatenate([vp, vc], axis=0)
    qi = lax.broadcasted_iota(jnp.int32, (BLK, 2 * BLK), 0)
    kj = lax.broadcasted_iota(jnp.int32, (BLK, 2 * BLK), 1)
    rel = kj - qi
    ok = (rel >= 1) & (rel <= BLK) & ((kj >= BLK) | jnp.logical_not(first))
    lane = lax.broadcasted_iota(jnp.int32, (1, LANE), 1)
    lane_b = lax.broadcasted_iota(jnp.int32, (BLK, LANE), 1)
    outs = []
    for p in range(A_HEADS // 2):
        qp = qb[:, LANE * p:LANE * (p + 1)]
        acc = jnp.zeros((BLK, LANE), f32)
        for g in range(2):
            h = g * (A_HEADS // 2) + p
            msk = (lane // A_DH == g).astype(f32)
            s = bdot_nt(qp * msk, kband) * (A_DH ** -0.5) + bias[h]
            s = jnp.where(ok, s, -1e30)
            skb = jnp.broadcast_to(sk[h:h + 1, :], (BLK, LANE))
            sink = jnp.sum(jnp.where(lane_b == 0, skb, 0.0), axis=-1, keepdims=True)
            m = lax.stop_gradient(jnp.maximum(jnp.max(s, axis=-1, keepdims=True), sink))
            e = jnp.exp(s - m)
            prob = e / (jnp.sum(e, axis=-1, keepdims=True) + jnp.exp(sink - m))
            acc = acc + bdot(prob, vband) * msk
        outs.append(acc)
    return jnp.concatenate(outs, axis=1)


def _swa_specs(nb, rev):
    bi = (lambda i: nb - 1 - i) if rev else (lambda i: i)
    return [
        pl.BlockSpec((BLK, A_Q), lambda i: (bi(i), 0)),
        pl.BlockSpec((BLK, LANE), lambda i: (jnp.maximum(bi(i) - 1, 0), 6)),
        pl.BlockSpec((BLK, LANE), lambda i: (bi(i), 6)),
        pl.BlockSpec((BLK, LANE), lambda i: (jnp.maximum(bi(i) - 1, 0), 7)),
        pl.BlockSpec((BLK, LANE), lambda i: (bi(i), 7)),
        pl.BlockSpec((A_HEADS, BLK, 2 * BLK), lambda i: (0, 0, 0)),
        pl.BlockSpec((16, LANE), lambda i: (0, 0)),
    ]


def swa_fwd(proj, bias, sk):
    S = proj.shape[0]
    nb = S // BLK

    def body(q_ref, kp_ref, kc_ref, vp_ref, vc_ref, b_ref, s_ref, o_ref):
        o_ref[...] = _swa_f(q_ref[...], kp_ref[...], kc_ref[...], vp_ref[...], vc_ref[...], b_ref[...], s_ref[...],
                            pl.program_id(0) == 0).astype(bf16)

    return pl.pallas_call(
        body, name="swa_fwd", grid=(nb,), in_specs=_swa_specs(nb, False),
        out_specs=pl.BlockSpec((BLK, A_Q), lambda i: (i, 0)),
        out_shape=jax.ShapeDtypeStruct((S, A_Q), bf16), compiler_params=_cp(("parallel",)),
    )(proj, proj, proj, proj, proj, bias, sk)


def swa_bwd(proj, bias, sk, dmix):
    S = proj.shape[0]
    nb = S // BLK

    def body(q_ref, kp_ref, kc_ref, vp_ref, vc_ref, b_ref, s_ref, do_ref, dqkv_ref, db_ref, ds_ref, ck, cv):
        i = pl.program_id(0)

        @pl.when(i == 0)
        def _():
            db_ref[...] = jnp.zeros_like(db_ref)
            ds_ref[...] = jnp.zeros_like(ds_ref)
            ck[...] = jnp.zeros_like(ck)
            cv[...] = jnp.zeros_like(cv)
        first = i == nb - 1
        _, vjp = jax.vjp(lambda *a: _swa_f(*a, first), q_ref[...], kp_ref[...], kc_ref[...], vp_ref[...],
                         vc_ref[...], b_ref[...], s_ref[...])
        dq, dkp, dkc, dvp, dvc, db, ds = vjp(do_ref[...])
        dqkv_ref[...] = jnp.concatenate([dq, dkc + ck[...], dvc + cv[...]], axis=1).astype(bf16)
        ck[...] = dkp
        cv[...] = dvp
        db_ref[...] += db
        ds_ref[...] += ds

    return pl.pallas_call(
        body, name="swa_bwd", grid=(nb,),
        in_specs=_swa_specs(nb, True) + [pl.BlockSpec((BLK, A_Q), lambda i: (nb - 1 - i, 0))],
        out_specs=[pl.BlockSpec((BLK, D), lambda i: (nb - 1 - i, 0)),
                   pl.BlockSpec((A_HEADS, BLK, 2 * BLK), lambda i: (0, 0, 0)),
                   pl.BlockSpec((16, LANE), lambda i: (0, 0))],
        out_shape=[jax.ShapeDtypeStruct((S, D), bf16), jax.ShapeDtypeStruct((A_HEADS, BLK, 2 * BLK), f32),
                   jax.ShapeDtypeStruct((16, LANE), f32)],
        scratch_shapes=[pltpu.VMEM((BLK, LANE), f32), pltpu.VMEM((BLK, LANE), f32)],
        compiler_params=_cp(("arbitrary",)),
    )(proj, proj, proj, proj, proj, bias, sk, dmix)


def _dnprep_f(xext, w, is_qk):
    c = (w[3:4] * xext + w[2:3] * shift_down(xext, 1) + w[1:2] * shift_down(xext, 2) + w[0:1] * shift_down(xext, 3))
    a = _silu(c)[HALO:]
    n = a * lax.rsqrt(jnp.sum(a * a, axis=-1, keepdims=True) + EPS)
    return jnp.where(is_qk, n, a)


def dnprep_fwd(proj, cw):
    S = proj.shape[0]
    nblk = B_QKV // LANE
    T = S

    def body(x_ref, w_ref, o_ref):
        is_qk = pl.program_id(0) < 2 * B_QK // LANE
        wv = w_ref[...]

        def tile(r0, first):
            o_ref[pl.ds(r0, T), :] = _dnprep_f(_glu_gext(x_ref, r0, first, T), wv, is_qk)

        tile(0, True)

    return pl.pallas_call(
        body, name="dnprep_fwd", grid=(nblk,),
        in_specs=[pl.BlockSpec((S, LANE), lambda j: (0, j)), pl.BlockSpec((4, LANE), lambda j: (0, j))],
        out_specs=pl.BlockSpec((S, LANE), lambda j: (0, j)),
        out_shape=jax.ShapeDtypeStruct((S, B_QKV), f32), compiler_params=_cp(("parallel",)),
    )(proj, cw)


def dnprep_bwd(proj, cw, dqkvn):
    S = proj.shape[0]
    nblk = B_QKV // LANE

    T = S

    def body(x_ref, w_ref, d_ref, dx_ref, dw_ref):
        is_qk = pl.program_id(0) < 2 * B_QK // LANE
        wv = w_ref[...]

        def tile(r0, first):
            _, vjp = jax.vjp(lambda a, b: _dnprep_f(a, b, is_qk), _glu_gext(x_ref, r0, first, T), wv)
            dx, dw = vjp(d_ref[pl.ds(r0, T), :])
            dx_ref[pl.ds(r0, T), :] = dx[HALO:].astype(bf16)
            if not first:
                dx_ref[pl.ds(r0 - HALO, HALO), :] += dx[:HALO]
            return dw

        dw_ref[...] = tile(0, True)

    col = pl.BlockSpec((S, LANE), lambda j: (0, j))
    wsp = pl.BlockSpec((4, LANE), lambda j: (0, j))
    return pl.pallas_call(
        body, name="dnprep_bwd", grid=(nblk,), in_specs=[col, wsp, col], out_specs=[col, wsp],
        out_shape=[jax.ShapeDtypeStruct((S, B_QKV), bf16), jax.ShapeDtypeStruct((4, B_QKV), f32)],
        compiler_params=_cp(("parallel",)),
    )(proj, cw, dqkvn)


def _hdot(a, b, ca=1, cb=0):
    return _dg(a, b, ca, cb, HI)


def _bdg(a, b, ca, cb):
    dn = (((ca,), (cb,)), ((0,), (0,)))
    ah, bh = a.astype(bf16), b.astype(bf16)
    al, bl = (a - ah.astype(f32)).astype(bf16), (b - bh.astype(f32)).astype(bf16)
    return (lax.dot_general(ah, bh, dn, preferred_element_type=f32)
            + lax.dot_general(ah, bl, dn, preferred_element_type=f32)
            + lax.dot_general(al, bh, dn, preferred_element_type=f32))


@jax.custom_vjp
def hbd(a, b):
    return _bdg(a, b, 2, 1)


@jax.custom_vjp
def hbd_nt(a, b):
    return _bdg(a, b, 2, 2)


@jax.custom_vjp
def hbd_tn(a, b):
    return _bdg(a, b, 1, 1)


hbd.defvjp(lambda a, b: (hbd(a, b), (a, b)), lambda r, g: (hbd_nt(g, r[1]), hbd_tn(r[0], g)))
hbd_nt.defvjp(lambda a, b: (hbd_nt(a, b), (a, b)), lambda r, g: (hbd(g, r[1]), hbd_tn(g, r[0])))
hbd_tn.defvjp(lambda a, b: (hbd_tn(a, b), (a, b)), lambda r, g: (hbd_nt(r[1], g), hbd(r[0], g)))


def _stack(xs):
    return jnp.concatenate([x[None] for x in xs], axis=0)


def _lane_col(x, j):
    lane = lax.broadcasted_iota(jnp.int32, (1, LANE), 1)
    return jnp.sum(jnp.where(lane == j, x, 0.0), axis=-1, keepdims=True)


def _tri_inv(a_mat):
    r = lax.broadcasted_iota(jnp.int32, (1, CHUNK, CHUNK), 1)
    c = lax.broadcasted_iota(jnp.int32, (1, CHUNK, CHUNK), 2)
    pw = -a_mat
    inv = (r == c).astype(f32) + pw
    for _ in range(5):
        pw = hbd(pw, pw)
        inv = inv + hbd(inv, pw)
    return inv


@jax.custom_vjp
def _tri_inv_known(a_mat, inv):
    return inv


_tri_inv_known.defvjp(lambda a, inv: (inv, inv),
                      lambda inv, g: (-hbd_tn(inv, hbd_nt(g, inv)), jnp.zeros_like(inv)))


def _dnc_f(q, k, v, seg, prm, inverse=_tri_inv):
    C = CHUNK
    B = q.shape[0]
    rows = seg.shape[0]
    beta_all = _sigmoid(seg)
    xx = seg + prm[1:2]
    g_all = -jnp.exp(prm[0:1]) * (jnp.maximum(xx, 0.0) + jnp.log(1.0 + jnp.exp(-jnp.abs(xx))))
    r2 = lax.broadcasted_iota(jnp.int32, (rows, rows), 0)
    c2 = lax.broadcasted_iota(jnp.int32, (rows, rows), 1)
    within = (r2 >= c2) & (r2 // C == c2 // C)
    gc_all = _hdot(within.astype(f32), g_all)
    beta = _stack([_lane_col(beta_all[C * j:C * (j + 1)], h) for j in range(rows // C) for h in range(6)])
    gc = _stack([_lane_col(gc_all[C * j:C * (j + 1)], 6 + h) for j in range(rows // C) for h in range(6)])
    r = lax.broadcasted_iota(jnp.int32, (1, C, C), 1)
    c = lax.broadcasted_iota(jnp.int32, (1, C, C), 2)
    incl = r >= c
    strict = r > c
    eye = (r == c).astype(f32)
    g_row = hbd(jnp.ones((B, C, C), f32), eye * gc)
    decay = jnp.where(incl, jnp.exp(jnp.where(incl, gc - g_row, 0.0)), 0.0)
    a_mat = beta * hbd_nt(k, k) * jnp.where(strict, decay, 0.0)
    eg = jnp.exp(gc)
    inv = inverse(a_mat)
    u = hbd(inv, beta * v)
    w = hbd(inv, (beta * eg) * k)
    qc = q * (B_DH ** -0.5)
    attn = hbd_nt(qc, k) * decay
    last = (lax.broadcasted_iota(jnp.int32, (1, C, 1), 1) == C - 1).astype(f32)
    g_last = jnp.sum(gc * last, axis=1, keepdims=True)
    dc = jnp.broadcast_to(jnp.exp(g_last), (B, 1, LANE)).reshape(B, LANE)
    return u, w, qc * eg, k * jnp.exp(g_last - gc), attn, dc, inv


def _b1(a, b, ca, cb):
    return lax.dot_general(a.astype(bf16), b.astype(bf16), (((ca,), (cb,)), ((0,), (0,))), preferred_element_type=f32)


@jax.custom_vjp
def sbd(a, b):
    return _b1(a, b, 2, 1)


@jax.custom_vjp
def sbd_nt(a, b):
    return _b1(a, b, 2, 2)


@jax.custom_vjp
def sbd_tn(a, b):
    return _b1(a, b, 1, 1)


sbd.defvjp(lambda a, b: (sbd(a, b), (a, b)), lambda r, g: (sbd_nt(g, r[1]), sbd_tn(r[0], g)))
sbd_nt.defvjp(lambda a, b: (sbd_nt(a, b), (a, b)), lambda r, g: (sbd(g, r[1]), sbd_tn(g, r[0])))
sbd_tn.defvjp(lambda a, b: (sbd_tn(a, b), (a, b)), lambda r, g: (sbd_nt(r[1], g), sbd(r[0], g)))


def _dns_f(S0, u, w, qd, kt, attn, dcrows):
    dc = _lane_col(dcrows, 0).reshape(6, 1, 1)
    delta = u - sbd(w, S0)
    out = sbd(qd, S0) + sbd(attn, delta)
    return out, dc * S0 + sbd_tn(kt, delta)


def _dnpost_f(o, z, grow):
    outs = []
    for h in range(6):
        oh = o[:, LANE * h:LANE * (h + 1)]
        outs.append(oh * lax.rsqrt(jnp.mean(oh * oh, axis=-1, keepdims=True) + EPS) * grow
                    * _silu(z[:, LANE * h:LANE * (h + 1)]))
    return jnp.concatenate(outs, axis=1)


def _hs(h):
    return slice(LANE * h, LANE * (h + 1))


DN_CHUNKS = 4


def _heads(ref, share):
    return _stack([ref[CHUNK * j:CHUNK * (j + 1), _hs(h // share)]
                   for j in range(ref.shape[0] // CHUNK) for h in range(6)])


def _put_heads(ref, val):
    for j in range(ref.shape[0] // CHUNK):
        for h in range(6):
            ref[CHUNK * j:CHUNK * (j + 1), _hs(h)] = val[6 * j + h]


def _dnc_in_specs():
    rows = CHUNK * DN_CHUNKS
    return [
        pl.BlockSpec((rows, B_QK), lambda n: (n, 0)),
        pl.BlockSpec((rows, B_QK), lambda n: (n, 1)),
        pl.BlockSpec((rows, B_V), lambda n: (n, 1)),
        pl.BlockSpec((rows, LANE), lambda n: (n, 20)),
        pl.BlockSpec((8, LANE), lambda n: (0, 0)),
    ]


def _dnc_out_specs(rev_nc=None, chunks=1):
    ci = (lambda n: n) if rev_nc is None else (lambda n: rev_nc - 1 - n)
    wide = pl.BlockSpec((CHUNK * chunks, B_V), lambda n: (ci(n), 0))
    return [wide, wide, wide, wide, pl.BlockSpec((chunks, 6, CHUNK, CHUNK), lambda n: (ci(n), 0, 0, 0)),
            pl.BlockSpec((chunks, 8, LANE), lambda n: (ci(n), 0, 0))]


def _dc_rows(dc):
    pad = jnp.zeros((2, LANE), f32)
    return _stack([jnp.concatenate([dc[6 * j:6 * (j + 1)], pad], axis=0) for j in range(dc.shape[0] // 6)])


def _dnc_shapes(S):
    nc = S // CHUNK
    wide = jax.ShapeDtypeStruct((S, B_V), f32)
    return [wide, wide, wide, wide, jax.ShapeDtypeStruct((nc, 6, CHUNK, CHUNK), f32),
            jax.ShapeDtypeStruct((nc, 8, LANE), f32)]


def dnc_fwd(qkvn, proj, prm):
    S = proj.shape[0]

    def body(q_ref, k_ref, v_ref, s_ref, p_ref, u_ref, w_ref, qd_ref, kt_ref, at_ref, dc_ref, inv_ref):
        u, w, qd, kt, attn, dc, inv = _dnc_f(_heads(q_ref, 2), _heads(k_ref, 2), _heads(v_ref, 1), s_ref[...],
                                             p_ref[...])
        inv_ref[...] = inv.reshape(inv_ref.shape)
        _put_heads(u_ref, u)
        _put_heads(w_ref, w)
        _put_heads(qd_ref, qd)
        _put_heads(kt_ref, kt)
        at_ref[...] = attn.reshape(at_ref.shape)
        dc_ref[...] = _dc_rows(dc)

    outs = _dnc_out_specs(chunks=DN_CHUNKS)
    out = pl.pallas_call(
        body, name="dn_chunk_fwd", grid=(S // (CHUNK * DN_CHUNKS),), in_specs=_dnc_in_specs(),
        out_specs=outs + [outs[4]], out_shape=_dnc_shapes(S) + [_dnc_shapes(S)[4]],
        compiler_params=_cp(("parallel",)),
    )(qkvn, qkvn, qkvn, proj, prm)
    return out[:6], out[6]


def dnc_bwd(qkvn, proj, prm, inv, cots):
    S = proj.shape[0]

    def body(q_ref, k_ref, v_ref, s_ref, p_ref, inv_ref, du_ref, dw_ref, dqd_ref, dkt_ref, dat_ref, ddc_ref,
             dx_ref, dseg_ref, dprm_ref):
        @pl.when(pl.program_id(0) == 0)
        def _():
            dprm_ref[...] = jnp.zeros_like(dprm_ref)
        nb = 6 * DN_CHUNKS
        known = functools.partial(_tri_inv_known, inv=inv_ref[...].reshape(nb, CHUNK, CHUNK))
        _, vjp = jax.vjp(lambda *a: _dnc_f(*a, inverse=known)[:6], _heads(q_ref, 2), _heads(k_ref, 2),
                         _heads(v_ref, 1), s_ref[...], p_ref[...])
        ddc = jnp.concatenate([ddc_ref[j, 0:6, :] for j in range(DN_CHUNKS)], axis=0)
        dq, dk, dv, dseg, dprm = vjp((_heads(du_ref, 1), _heads(dw_ref, 1), _heads(dqd_ref, 1), _heads(dkt_ref, 1),
                                      dat_ref[...].reshape(nb, CHUNK, CHUNK), ddc))
        for j in range(DN_CHUNKS):
            o = 6 * j
            dx_ref[CHUNK * j:CHUNK * (j + 1), :] = jnp.concatenate(
                [dq[o] + dq[o + 1], dq[o + 2] + dq[o + 3], dq[o + 4] + dq[o + 5],
                 dk[o] + dk[o + 1], dk[o + 2] + dk[o + 3], dk[o + 4] + dk[o + 5]] + [dv[o + h] for h in range(6)], axis=1)
        dseg_ref[...] = dseg.astype(bf16)
        dprm_ref[...] += dprm

    rows = CHUNK * DN_CHUNKS
    outs = _dnc_out_specs(chunks=DN_CHUNKS)
    return pl.pallas_call(
        body, name="dn_chunk_bwd", grid=(S // rows,),
        in_specs=_dnc_in_specs() + [outs[4]] + outs,
        out_specs=[pl.BlockSpec((rows, B_QKV), lambda n: (n, 0)), pl.BlockSpec((rows, LANE), lambda n: (n, 0)),
                   pl.BlockSpec((8, LANE), lambda n: (0, 0))],
        out_shape=[jax.ShapeDtypeStruct((S, B_QKV), f32), jax.ShapeDtypeStruct((S, LANE), bf16),
                   jax.ShapeDtypeStruct((8, LANE), f32)],
        compiler_params=_cp(("arbitrary",)),
    )(qkvn, qkvn, qkvn, proj, prm, inv, *cots)


def dns_fwd(chunked):
    u = chunked[0]
    S = u.shape[0]
    nc = S // CHUNK

    def body(u_ref, w_ref, qd_ref, kt_ref, at_ref, dc_ref, o_ref, st_ref, st):
        @pl.when(pl.program_id(0) == 0)
        def _():
            st[...] = jnp.zeros_like(st)
        S0 = st[...]
        st_ref[0] = S0
        out, S1 = _dns_f(S0, _heads(u_ref, 1), _heads(w_ref, 1), _heads(qd_ref, 1), _heads(kt_ref, 1),
                         at_ref[0], dc_ref[0, 0:6, :])
        _put_heads(o_ref, out)
        st[...] = S1

    return pl.pallas_call(
        body, name="dn_scan_fwd", grid=(nc,), in_specs=_dnc_out_specs(),
        out_specs=[pl.BlockSpec((CHUNK, B_V), lambda n: (n, 0)),
                   pl.BlockSpec((1, 6, B_DH, B_DH), lambda n: (n, 0, 0, 0))],
        out_shape=[jax.ShapeDtypeStruct((S, B_V), f32), jax.ShapeDtypeStruct((nc, 6, B_DH, B_DH), f32)],
        scratch_shapes=[pltpu.VMEM((6, B_DH, B_DH), f32)],
        compiler_params=_cp(("arbitrary",)),
    )(*chunked)


def dns_bwd(chunked, states, do):
    S = do.shape[0]
    nc = S // CHUNK

    def body(u_ref, w_ref, qd_ref, kt_ref, at_ref, dc_ref, st_ref, do_ref,
             du_ref, dw_ref, dqd_ref, dkt_ref, dat_ref, ddc_ref, dst):
        @pl.when(pl.program_id(0) == 0)
        def _():
            dst[...] = jnp.zeros_like(dst)
        _, vjp = jax.vjp(_dns_f, st_ref[0], _heads(u_ref, 1), _heads(w_ref, 1), _heads(qd_ref, 1), _heads(kt_ref, 1),
                         at_ref[0], dc_ref[0, 0:6, :])
        dS0, du, dw, dqd, dkt, dat, ddc = vjp((_heads(do_ref, 1), dst[...]))
        dst[...] = dS0
        _put_heads(du_ref, du)
        _put_heads(dw_ref, dw)
        _put_heads(dqd_ref, dqd)
        _put_heads(dkt_ref, dkt)
        dat_ref[0] = dat
        ddc_ref[0] = jnp.concatenate([ddc, jnp.zeros((2, LANE), f32)], axis=0)

    return pl.pallas_call(
        body, name="dn_scan_bwd", grid=(nc,),
        in_specs=_dnc_out_specs(nc) + [pl.BlockSpec((1, 6, B_DH, B_DH), lambda n: (nc - 1 - n, 0, 0, 0)),
                                       pl.BlockSpec((CHUNK, B_V), lambda n: (nc - 1 - n, 0))],
        out_specs=_dnc_out_specs(nc), out_shape=_dnc_shapes(S),
        scratch_shapes=[pltpu.VMEM((6, B_DH, B_DH), f32)],
        compiler_params=_cp(("arbitrary",)),
    )(*chunked, states, do)


def dnpost_fwd(o, proj, prm):
    S = o.shape[0]
    t = min(512, S)

    def body(o_ref, z_ref, p_ref, y_ref):
        y_ref[...] = _dnpost_f(o_ref[...], z_ref[...], p_ref[2:3, :]).astype(bf16)

    tok = pl.BlockSpec((t, B_V), lambda i: (i, 0))
    return pl.pallas_call(
        body, name="dn_post_fwd", grid=(S // t,),
        in_specs=[tok, pl.BlockSpec((t, B_V), lambda i: (i, 2)), pl.BlockSpec((8, LANE), lambda i: (0, 0))],
        out_specs=tok, out_shape=jax.ShapeDtypeStruct((S, B_V), bf16), compiler_params=_cp(("parallel",)),
    )(o, proj, prm)


def dnpost_bwd(o, proj, prm, dmix):
    S = o.shape[0]
    t = min(512, S)

    def body(o_ref, z_ref, p_ref, dy_ref, do_ref, dz_ref, dg_ref):
        @pl.when(pl.program_id(0) == 0)
        def _():
            dg_ref[...] = jnp.zeros_like(dg_ref)
        _, vjp = jax.vjp(_dnpost_f, o_ref[...], z_ref[...], p_ref[2:3, :])
        do, dz, dg = vjp(dy_ref[...])
        do_ref[...] = do
        dz_ref[...] = dz.astype(bf16)
        dg_ref[...] += dg

    tok = pl.BlockSpec((t, B_V), lambda i: (i, 0))
    return pl.pallas_call(
        body, name="dn_post_bwd", grid=(S // t,),
        in_specs=[tok, pl.BlockSpec((t, B_V), lambda i: (i, 2)), pl.BlockSpec((8, LANE), lambda i: (0, 0)), tok],
        out_specs=[tok, tok, pl.BlockSpec((1, LANE), lambda i: (0, 0))],
        out_shape=[jax.ShapeDtypeStruct((S, B_V), f32), jax.ShapeDtypeStruct((S, B_V), bf16),
                   jax.ShapeDtypeStruct((1, LANE), f32)],
        compiler_params=_cp(("arbitrary",)),
    )(o, proj, prm, dmix)


N_FF_BLK = D_FF // LANE
GU_SHARD = 2 * D_FF // 4


GLU_ROWS = 256
HALO = 16


def _glu_f(gext, up, w, b):
    c = w[2:3] * gext + w[1:2] * shift_down(gext, 1) + w[0:1] * shift_down(gext, 2) + b
    return _silu(c)[HALO:] * up


def _glu_gext(g_ref, r0, first, T=GLU_ROWS):
    if first:
        return jnp.concatenate([jnp.zeros((HALO, LANE), f32), g_ref[0:T, :].astype(f32)], axis=0)
    return g_ref[pl.ds(r0 - HALO, T + HALO), :].astype(f32)


def glu_fwd(gu, w, b, name):
    S = gu.shape[0]
    T = GLU_ROWS

    def body(g_ref, u_ref, w_ref, b_ref, o_ref):
        wv, bv = w_ref[...], b_ref[...]

        def tile(r0, first):
            act = _glu_f(_glu_gext(g_ref, r0, first), u_ref[pl.ds(r0, T), :].astype(f32), wv, bv)
            o_ref[pl.ds(r0, T), :] = act.astype(bf16)

        tile(0, True)

        @pl.loop(1, S // T)
        def _(t):
            tile(pl.multiple_of(t * T, T), False)

    col = pl.BlockSpec((S, LANE), lambda j: (0, j))
    return pl.pallas_call(
        body, name=name, grid=(N_FF_BLK,),
        in_specs=[col, pl.BlockSpec((S, LANE), lambda j: (0, N_FF_BLK + j)), pl.BlockSpec((3, LANE), lambda j: (0, j)),
                  pl.BlockSpec((1, LANE), lambda j: (0, j))],
        out_specs=col, out_shape=jax.ShapeDtypeStruct((S, D_FF), bf16), compiler_params=_cp(("parallel",)),
    )(gu, gu, w, b.reshape(1, D_FF))


def glu_bwd(gu, w, b, dact, name):
    S = gu.shape[0]
    T = GLU_ROWS

    def body(g_ref, u_ref, w_ref, b_ref, d_ref, dg_ref, dw_ref, db_ref, acc):
        wv, bv = w_ref[...], b_ref[...]

        def tile(r0, first):
            _, vjp = jax.vjp(_glu_f, _glu_gext(g_ref, r0, first), u_ref[pl.ds(r0, T), :].astype(f32), wv, bv)
            dgx, du, dw, db = vjp(d_ref[pl.ds(r0, T), :].astype(f32))
            acc[pl.ds(r0, T), :] = dgx[HALO:]
            if not first:
                acc[pl.ds(r0 - HALO, HALO), :] += dgx[:HALO]
            dg_ref[1, pl.ds(r0, T), :] = du.astype(bf16)
            return dw, db

        dw0, db0 = tile(0, True)
        dw_ref[...] = dw0
        db_ref[...] = db0

        @pl.loop(1, S // T)
        def _(t):
            dw, db = tile(pl.multiple_of(t * T, T), False)
            dw_ref[...] += dw
            db_ref[...] += db

        dg_ref[0] = acc[...].astype(bf16)

    col = pl.BlockSpec((S, LANE), lambda j: (0, j))
    wsp = pl.BlockSpec((3, LANE), lambda j: (0, j))
    bsp = pl.BlockSpec((1, LANE), lambda j: (0, j))
    return pl.pallas_call(
        body, name=name, grid=(N_FF_BLK,),
        in_specs=[col, pl.BlockSpec((S, LANE), lambda j: (0, N_FF_BLK + j)), wsp, bsp, col],
        out_specs=[pl.BlockSpec((2, S, LANE), lambda j: (0, 0, j)), wsp, bsp],
        out_shape=[jax.ShapeDtypeStruct((2, S, D_FF), bf16), jax.ShapeDtypeStruct((3, D_FF), f32),
                   jax.ShapeDtypeStruct((1, D_FF), f32)],
        scratch_shapes=[pltpu.VMEM((S, LANE), f32)],
        compiler_params=_cp(("parallel",)),
    )(gu, gu, w, b.reshape(1, D_FF), dact)


def gu_fwd(n2, wg, name):
    S = n2.shape[0]
    tm = min(MM_ROWS, S)

    def body(a_ref, w_ref, o_ref):
        o_ref[...] = _dg(a_ref[...], w_ref[...], 1, 0).astype(bf16)

    return pl.pallas_call(
        body, name=name, grid=(4, S // tm),
        in_specs=[pl.BlockSpec((tm, D), lambda s, m: (m, 0)), pl.BlockSpec((None, D, GU_SHARD), lambda s, m: (s, 0, 0))],
        out_specs=pl.BlockSpec((tm, GU_SHARD), lambda s, m: (m, s)),
        out_shape=jax.ShapeDtypeStruct((S, 2 * D_FF), bf16), compiler_params=_cp(("parallel", "parallel")),
    )(n2, wg)


def gu_bwd_x(dgu, wg, name):
    S = dgu.shape[1]
    tm = min(MM_ROWS, S)

    def body(d_ref, w_ref, o_ref):
        @pl.when(pl.program_id(1) == 0)
        def _():
            o_ref[...] = jnp.zeros_like(o_ref)
        o_ref[...] += _dg(d_ref[...], w_ref[...], 1, 1)

    return pl.pallas_call(
        body, name=name, grid=(S // tm, 4),
        in_specs=[pl.BlockSpec((None, tm, GU_SHARD), lambda m, s: (s // 2, m, s % 2)),
                  pl.BlockSpec((None, D, GU_SHARD), lambda m, s: (s, 0, 0))],
        out_specs=pl.BlockSpec((tm, D), lambda m, s: (m, 0)),
        out_shape=jax.ShapeDtypeStruct((S, D), f32), compiler_params=_cp(("parallel", "arbitrary")),
    )(dgu, wg)


def gu_bwd_w(n2, dgu, name):
    S = n2.shape[0]
    tm = min(MM_ROWS, S)
    nm = S // tm

    def body(a_ref, d_ref, o_ref, acc):
        @pl.when(pl.program_id(1) == 0)
        def _():
            acc[...] = jnp.zeros_like(acc)
        acc[...] += _dg(a_ref[...], d_ref[...], 0, 0)

        @pl.when(pl.program_id(1) == nm - 1)
        def _():
            o_ref[...] = acc[...].astype(bf16)

    return pl.pallas_call(
        body, name=name, grid=(4, nm),
        in_specs=[pl.BlockSpec((tm, D), lambda s, m: (m, 0)),
                  pl.BlockSpec((None, tm, GU_SHARD), lambda s, m: (s // 2, m, s % 2))],
        out_specs=pl.BlockSpec((None, D, GU_SHARD), lambda s, m: (s, 0, 0)),
        out_shape=jax.ShapeDtypeStruct((4, D, GU_SHARD), bf16),
        scratch_shapes=[pltpu.VMEM((D, GU_SHARD), f32)],
        compiler_params=_cp(("parallel", "arbitrary")),
    )(n2, dgu)


def _pair_cols(w):
    lead = w.shape[:-1]
    return w.reshape(lead + (2, 6, A_DH)).swapaxes(-3, -2).reshape(lead + (A_Q,))


def _unpair_cols(w):
    lead = w.shape[:-1]
    return w.reshape(lead + (6, 2, A_DH)).swapaxes(-3, -2).reshape(lead + (A_Q,))


def _lay_in_a(w):
    return jnp.concatenate([_pair_cols(w[:, :A_Q]), w[:, A_Q:]], axis=1)


def _unlay_in_a(w):
    return jnp.concatenate([_unpair_cols(w[:, :A_Q]), w[:, A_Q:]], axis=1)


def _lay_out_a(w):
    return jnp.concatenate([_pair_cols(w[:A_Q].T).T, w[A_Q:]], axis=0)


def _unlay_out_a(w):
    return jnp.concatenate([_unpair_cols(w[:A_Q].T).T, w[A_Q:]], axis=0)


def _lay_in_b(w):
    return jnp.concatenate([w[:, :2304], w[:, 2316:], w[:, 2304:2316],
                            jnp.zeros((w.shape[0], LANE - 12), w.dtype)], axis=1)


def _unlay_in_b(w):
    return jnp.concatenate([w[:, :2304], w[:, 2560:2572], w[:, 2304:2560]], axis=1)


def _chip_cols(w):
    return jnp.moveaxis(w.reshape(w.shape[0], 4, w.shape[1] // 4), 1, 0)


def _unchip_cols(w):
    return jnp.moveaxis(w, 0, 1).reshape(w.shape[1], 4 * w.shape[2])


def _local_step(x, mem, target, P):
    arrive = P.get("arrive", lambda key, after: None)
    ready = P.get("ready", lambda key, grads, dep: dep)
    sk = jnp.zeros((16, LANE), f32).at[:A_HEADS].set(jnp.broadcast_to(P["sinks"][:, None], (A_HEADS, LANE)))
    prm = jnp.zeros((8, LANE), f32).at[0, 6:12].set(P["a_log"]).at[1, 6:12].set(P["dt_bias"]).at[2].set(P["out_norm_g"])
    bias = bias_build(P["rel_bias"])
    saved = []
    h = x
    for i in range(2):
        n1 = rms_fwd(h, P["g_mix"][i], f"rms_mix{i}")
        arrive(("w_in", i), n1)
        proj = mm_nn(n1, P["w_in_a"] if i == 0 else P["w_in_b"], name="proj_a" if i == 0 else "proj_b")
        arrive(("w_mem", i), proj)
        kv = memkv_fwd(mem, P["g_mem"][i], P["w_mem"][i], f"memkv{i}")
        if i == 0:
            self_out = swa_fwd(proj, bias, sk)
            cross = xattn_fwd(proj, A_Q + 2 * LANE, kv, "xattn_a")
            extra = ()
        else:
            qkvn = dnprep_fwd(proj, P["conv_qkv"])
            chunked, inv = dnc_fwd(qkvn, proj, prm)
            o, states = dns_fwd(chunked)
            self_out = dnpost_fwd(o, proj, prm)
            cross = xattn_fwd(proj, 2304, kv, "xattn_b")
            extra = (qkvn, chunked, inv, states, o)
        mix = jnp.concatenate([self_out, cross], axis=1)
        arrive(("w_out", i), cross)
        h2 = mm_nn(mix, P["w_out"][i], res=h, name=f"out_proj{i}")
        n2 = rms_fwd(h2, P["g_ffn"][i], f"rms_ffn{i}")
        arrive(("w_gu", i), n2)
        gu = gu_fwd(n2, P["w_gu"][i], f"gate_up{i}")
        act = glu_fwd(gu, P["ffn_cw"][i], P["ffn_cb"][i], f"glu{i}")
        arrive(("w_down", i), act)
        h3 = mm_nn(act, P["w_down"][i], res=h2, name=f"down{i}")
        saved.append((h, n1, kv, proj, mix, h2, n2, gu, act, extra))
        h = h3

    loss, dh, dg_fin = loss_head(h, P["g_fin"], target)
    G = {"g_fin": dg_fin[0], "g_mix": [None, None], "g_mem": [None, None], "g_ffn": [None, None],
         "w_mem": [None, None], "w_out": [None, None], "w_gu": [None, None], "w_down": [None, None],
         "ffn_cw": [None, None], "ffn_cb": [None, None]}
    for i in (1, 0):
        hin, n1, kv, proj, mix, h2, n2, gu, act, extra = saved[i]
        dact = mm_nt(dh, P["w_down"][i], out_dtype=bf16, name=f"d_act{i}")
        G["w_down"][i] = mm_tn(act, dh, name=f"dw_down{i}")
        dgu, dcw, dcb = glu_bwd(gu, P["ffn_cw"][i], P["ffn_cb"][i], dact, f"glu_bwd{i}")
        G["ffn_cw"][i], G["ffn_cb"][i] = dcw, dcb[0]
        dn2 = gu_bwd_x(dgu, P["w_gu"][i], f"d_n2_{i}")
        G["w_gu"][i] = gu_bwd_w(n2, dgu, f"dw_gu{i}")
        g_ffn = ready(("ffn", i), G, P["g_ffn"][i])
        dh2, dg = rms_bwd(h2, g_ffn, dn2, dh, f"rms_ffn_bwd{i}")
        G["g_ffn"][i] = dg[0]
        dmix = mm_nt(dh2, P["w_out"][i], name=f"d_mix{i}")
        G["w_out"][i] = mm_tn(mix, dh2, name=f"dw_out{i}")
        if i == 0:
            dqkv, dbias, dsk = swa_bwd(proj, bias, sk, dmix)
            dxq, dkv = xattn_bwd(proj, A_Q + 2 * LANE, kv, dmix, "xattn_a_bwd")
            dproj = jnp.concatenate([dqkv, dxq], axis=1)
            G["sinks"] = dsk[:A_HEADS, 0]
            G["rel_bias"] = bias_grad(dbias)[:, :A_HEADS]
            w_in, gname = P["w_in_a"], "w_in_a"
        else:
            qkvn, chunked, inv, states, o = extra
            do, dz, dgo = dnpost_bwd(o, proj, prm, dmix)
            dqkvn, dseg, dprm = dnc_bwd(qkvn, proj, prm, inv, dns_bwd(chunked, states, do))
            draw, dconv = dnprep_bwd(proj, P["conv_qkv"], dqkvn)
            dxq, dkv = xattn_bwd(proj, 2304, kv, dmix, "xattn_b_bwd")
            dproj = jnp.concatenate([draw, dz, dxq, dseg], axis=1)
            G["conv_qkv"] = dconv
            G["a_log"], G["dt_bias"], G["out_norm_g"] = dprm[0, 6:12], dprm[1, 6:12], dgo[0]
            w_in, gname = P["w_in_b"], "w_in_b"
        dn1 = mm_nt(dproj, w_in, name=f"d_n1_{i}")
        G[gname] = mm_tn(n1, dproj, name=f"d{gname}")
        dh, dg = rms_bwd(hin, P["g_mix"][i], dn1, dh2, f"rms_mix_bwd{i}")
        G["g_mix"][i] = dg[0]
        dgm, dwm = memkv_bwd(mem, P["g_mem"][i], P["w_mem"][i], dkv, f"memkv_bwd{i}")
        G["g_mem"][i], G["w_mem"][i] = dgm[0], dwm
        ready(("mix", i), G, None)
    return loss, dh, G


def _prepare(full, w_gu=None):
    return {
        "rel_bias": full["rel_bias"], "sinks": full["sinks_a"][0], "a_log": full["a_log_b"][0],
        "dt_bias": full["dt_bias_b"][0], "out_norm_g": full["out_norm_g_b"][0],
        "g_mix": full["norm_mix_g"], "g_mem": full["norm_mem_g"], "g_ffn": full["norm_ffn_g"],
        "g_fin": full["final_norm_g"], "conv_qkv": full["conv_qkv_b"][0],
        "ffn_cw": [full["ffn_conv_w"][0], full["ffn_conv_w"][1]],
        "ffn_cb": [full["ffn_conv_b"][0], full["ffn_conv_b"][1]],
        "w_mem": [full["w_mem_kv"][0], full["w_mem_kv"][1]],
        "w_out": [_lay_out_a(full["w_out"][0]), full["w_out"][1]],
        "w_in_a": _lay_in_a(full["w_in_a"][0]), "w_in_b": _lay_in_b(full["w_in_b"][0]),
        "w_gu": w_gu if w_gu is not None else [_chip_cols(full["w_gate_up"][0]), _chip_cols(full["w_gate_up"][1])],
        "w_down": [full["w_down"][0], full["w_down"][1]],
    }


def _grads_to_ref(G):
    return {
        "rel_bias": G["rel_bias"], "norm_mix_g": jnp.stack(G["g_mix"]), "norm_mem_g": jnp.stack(G["g_mem"]),
        "w_mem_kv": jnp.stack(G["w_mem"]),
        "w_out": jnp.stack([_unlay_out_a(G["w_out"][0]), G["w_out"][1]]),
        "w_in_a": _unlay_in_a(G["w_in_a"])[None], "sinks_a": G["sinks"][None],
        "w_in_b": _unlay_in_b(G["w_in_b"])[None], "conv_qkv_b": G["conv_qkv"][None],
        "a_log_b": G["a_log"][None], "dt_bias_b": G["dt_bias"][None], "out_norm_g_b": G["out_norm_g"][None],
        "norm_ffn_g": jnp.stack(G["g_ffn"]),
        "w_gate_up": jnp.stack([_unchip_cols(G["w_gu"][0]), _unchip_cols(G["w_gu"][1])]).astype(f32),
        "ffn_conv_w": jnp.stack(G["ffn_cw"]), "ffn_conv_b": jnp.stack(G["ffn_cb"]),
        "w_down": jnp.stack(G["w_down"]), "final_norm_g": G["g_fin"],
    }


ANY = pl.BlockSpec(memory_space=pl.ANY)


def _place():
    return lax.axis_index("x"), lax.axis_index("y"), lax.axis_index("c")


def chip_scatter(gs):
    n = len(gs)

    def body(*refs):
        ins, outs = refs[:n], refs[n:2 * n]
        ssem, rsem = refs[2 * n:]
        x, y, c = _place()
        me = 2 * x + y
        peers = [(1 - x, y), (x, 1 - y), (1 - x, 1 - y)]

        def remote(j, k, slot):
            px, py = peers[k]
            return pltpu.make_async_remote_copy(
                src_ref=ins[j].at[2 * px + py], dst_ref=outs[j].at[slot],
                send_sem=ssem.at[3 * j + k], recv_sem=rsem.at[3 * j + k],
                device_id=(px, py, c), device_id_type=MESH)

        sends = [remote(j, k, me) for j in range(n) for k in range(3)]
        for cp in sends:
            cp.start()
        for j in range(n):
            for k in range(3):
                px, py = peers[k]
                remote(j, k, 2 * px + py).wait_recv()
        for cp in sends:
            cp.wait_send()

    return pl.pallas_call(
        body, name="grad_scatter", in_specs=[ANY] * n, out_specs=[ANY] * n,
        out_shape=[jax.ShapeDtypeStruct(g.shape, g.dtype) for g in gs],
        scratch_shapes=[pltpu.SemaphoreType.DMA((3 * n,)), pltpu.SemaphoreType.DMA((3 * n,))],
    )(*gs)


def allreduce_small(buf):
    R = buf.shape[0]

    def body(b_ref, o_ref, recv, ssem, rsem):
        x, y, c = _place()
        me = 4 * x + 2 * y + c

        def peer(k):
            return (1 - x if k & 4 else x, 1 - y if k & 2 else y, 1 - c if k & 1 else c)

        def remote(k, slot):
            return pltpu.make_async_remote_copy(
                src_ref=b_ref, dst_ref=recv.at[slot], send_sem=ssem.at[k - 1], recv_sem=rsem.at[k - 1],
                device_id=peer(k), device_id_type=MESH)

        sends = [remote(k, me) for k in range(1, 8)]
        for cp in sends:
            cp.start()
        recv[me] = b_ref[...]
        for k in range(1, 8):
            px, py, pc = peer(k)
            remote(k, 4 * px + 2 * py + pc).wait_recv()
        for cp in sends:
            cp.wait_send()
        total = recv[0]
        for j in range(1, 8):
            total = total + recv[j]
        o_ref[...] = total

    return pl.pallas_call(
        body, name="small_allreduce",
        in_specs=[pl.BlockSpec(memory_space=pltpu.VMEM)], out_specs=pl.BlockSpec(memory_space=pltpu.VMEM),
        out_shape=jax.ShapeDtypeStruct(buf.shape, f32),
        scratch_shapes=[pltpu.VMEM((8, R, LANE), f32), pltpu.SemaphoreType.DMA((7,)), pltpu.SemaphoreType.DMA((7,))],
    )(buf)


def sum_slots(own, recv, chip, core, name):
    _, R, C = recv.shape
    tr = _row_tile(R, 256)
    nt = R // tr

    def body(p_ref, a_ref, r_ref, o_ref):
        acc = jnp.zeros((tr, C), f32)
        for s in range(4):
            acc = acc + jnp.where(p_ref[0] == s, a_ref[s], r_ref[s]).astype(f32)
        o_ref[...] = acc

    slots = pl.BlockSpec((4, tr, C), lambda i, p_ref: (0, i, 0))
    return pl.pallas_call(
        body, name=name, out_shape=jax.ShapeDtypeStruct((2 * R, C), f32),
        grid_spec=pltpu.PrefetchScalarGridSpec(
            num_scalar_prefetch=1, grid=(nt,), in_specs=[slots, slots],
            out_specs=pl.BlockSpec((tr, C), lambda i, p_ref: (p_ref[1] * nt + i, 0))),
        compiler_params=_cp(("parallel",)),
    )(jnp.stack([chip, core]).astype(jnp.int32), own, recv)


def _half(ref, core, axis=0):
    half = ref.shape[axis] // 2
    idx = (slice(None),) * axis + (pl.ds(core * half, half),)
    return ref.at[idx]


IN_HBM = pl.BlockSpec(memory_space=pltpu.HBM)
IN_SEM = pl.BlockSpec(memory_space=pltpu.SEMAPHORE)
SIDE_EFFECT = pltpu.SideEffectType.DATAFLOW_SIDE_EFFECTING


def _gather_copy(buf, i, k, ssem, rsem, place, landing):
    x, y, c = place
    px, py = [(1 - x, y), (x, 1 - y), (1 - x, 1 - y)][k]
    me = 2 * x + y
    return pltpu.make_async_remote_copy(
        src_ref=buf.at[me], dst_ref=buf.at[me if landing == "theirs" else 2 * px + py],
        send_sem=ssem.at[3 * i + k], recv_sem=rsem.at[3 * i + k], device_id=(px, py, c), device_id_type=MESH)


def gather_start(groups):
    flat = [b for grp in groups for b in grp]
    n, ng = len(flat), len(groups)

    def body(*refs):
        bufs, sems = refs[:n], refs[n:n + 2 * ng]
        place = _place()
        j = 0
        for g, grp in enumerate(groups):
            for i in range(len(grp)):
                for k in range(3):
                    _gather_copy(bufs[j], i, k, sems[2 * g], sems[2 * g + 1], place, "theirs").start()
                j += 1

    sem_shapes = [pltpu.SemaphoreType.DMA((3 * len(grp),)) for grp in groups for _ in range(2)]
    out = pl.pallas_call(
        body, name="gather_start", in_specs=[IN_HBM] * n, out_specs=(*[IN_SEM] * (2 * ng), *[IN_HBM] * n),
        out_shape=(*sem_shapes, *[pltpu.HBM(b.shape, b.dtype) for b in flat]),
        input_output_aliases={i: 2 * ng + i for i in range(n)},
        compiler_params=pltpu.CompilerParams(has_side_effects=SIDE_EFFECT),
    )(*[pltpu.with_memory_space_constraint(b, pltpu.HBM) for b in flat])
    sems, bufs = out[:2 * ng], list(out[2 * ng:])
    flights, j = [], 0
    for g, grp in enumerate(groups):
        flights.append((bufs[j:j + len(grp)], sems[2 * g], sems[2 * g + 1]))
        j += len(grp)
    return flights


def gather_wait(flight, after, name):
    bufs, ssem, rsem = flight
    n = len(bufs)

    def body(*refs):
        place = _place()
        for i in range(n):
            for k in range(3):
                cp = _gather_copy(refs[i], i, k, refs[n], refs[n + 1], place, "mine")
                cp.wait_send()
                cp.wait_recv()

    return pl.pallas_call(
        body, name=name, in_specs=[IN_HBM] * n + [IN_SEM, IN_SEM, ANY], out_specs=[IN_HBM] * n,
        out_shape=[pltpu.HBM(b.shape, b.dtype) for b in bufs], input_output_aliases={i: i for i in range(n)},
        compiler_params=pltpu.CompilerParams(has_side_effects=SIDE_EFFECT),
    )(*bufs, ssem, rsem, after)


def _scatter_copy(src, land, j, k, ssem, rsem, place, landing):
    x, y, c = place
    px, py = [(1 - x, y), (x, 1 - y), (1 - x, 1 - y)][k]
    return pltpu.make_async_remote_copy(
        src_ref=src.at[2 * px + py], dst_ref=land.at[2 * x + y if landing == "theirs" else 2 * px + py],
        send_sem=ssem.at[3 * j + k], recv_sem=rsem.at[3 * j + k], device_id=(px, py, c), device_id_type=MESH)


def scatter_start(srcs, name):
    n = len(srcs)
    lands = [lax.empty(g.shape, g.dtype) for g in srcs]

    def body(*refs):
        place = _place()
        for j in range(n):
            for k in range(3):
                _scatter_copy(refs[j], refs[n + j], j, k, refs[2 * n], refs[2 * n + 1], place, "theirs").start()
        refs[-1][...] = jnp.zeros_like(refs[-1])

    sem = pltpu.SemaphoreType.DMA((3 * n,))
    hbm = [pltpu.with_memory_space_constraint(b, pltpu.HBM) for b in list(srcs) + lands]
    out = pl.pallas_call(
        body, name=name, in_specs=[IN_HBM] * (2 * n),
        out_specs=(IN_SEM, IN_SEM, *[IN_HBM] * (2 * n), pl.BlockSpec(memory_space=pltpu.VMEM)),
        out_shape=(sem, sem, *[pltpu.HBM(b.shape, b.dtype) for b in hbm], jax.ShapeDtypeStruct((8, LANE), f32)),
        input_output_aliases={i: 2 + i for i in range(2 * n)},
        compiler_params=pltpu.CompilerParams(has_side_effects=SIDE_EFFECT),
    )(*hbm)
    return (list(out[2:2 + n]), list(out[2 + n:2 + 2 * n]), out[0], out[1]), out[-1]


def scatter_wait(flight, after, name):
    srcs, lands, ssem, rsem = flight
    n = len(srcs)

    def body(*refs):
        place = _place()
        for j in range(n):
            for k in range(3):
                cp = _scatter_copy(refs[j], refs[n + j], j, k, refs[2 * n], refs[2 * n + 1], place, "mine")
                cp.wait_send()
                cp.wait_recv()

    out = pl.pallas_call(
        body, name=name, in_specs=[IN_HBM] * (2 * n) + [IN_SEM, IN_SEM, ANY], out_specs=[IN_HBM] * (2 * n),
        out_shape=[pltpu.HBM(b.shape, b.dtype) for b in list(srcs) + list(lands)],
        input_output_aliases={i: i for i in range(2 * n)},
        compiler_params=pltpu.CompilerParams(has_side_effects=SIDE_EFFECT),
    )(*srcs, *lands, ssem, rsem, after)
    return list(out[:n]), list(out[n:])


def pair_exchange(gbufs, name):
    n = len(gbufs)

    def body(*refs):
        ins, outs = refs[:n], refs[n:2 * n]
        ssem, rsem = refs[2 * n:]
        x, y, c = _place()
        cps = [pltpu.make_async_remote_copy(
            src_ref=_half(ins[j], 1 - c, axis=1), dst_ref=outs[j], send_sem=ssem.at[j], recv_sem=rsem.at[j],
            device_id=(x, y, 1 - c), device_id_type=MESH) for j in range(n)]
        for cp in cps:
            cp.start()
        for cp in cps:
            cp.wait()

    return pl.pallas_call(
        body, name=name, in_specs=[ANY] * n, out_specs=[ANY] * n,
        out_shape=[jax.ShapeDtypeStruct((4, g.shape[1] // 2, g.shape[2]), g.dtype) for g in gbufs],
        scratch_shapes=[pltpu.SemaphoreType.DMA((n,)), pltpu.SemaphoreType.DMA((n,))],
    )(*gbufs)


def _row_tile(rows, cap=512):
    return max(t for t in range(16, min(rows, cap) + 1, 16) if rows % t == 0)


def pair_sum(mine, theirs, core, name):
    _, R, C = mine.shape
    half = R // 2
    tr = _row_tile(half)
    nt = half // tr

    def body(c_ref, a_ref, b_ref, o_ref):
        o_ref[...] = (a_ref[...].astype(f32) + b_ref[...].astype(f32)).astype(bf16)

    return pl.pallas_call(
        body, name=name, out_shape=jax.ShapeDtypeStruct(theirs.shape, bf16),
        grid_spec=pltpu.PrefetchScalarGridSpec(
            num_scalar_prefetch=1, grid=(4, nt),
            in_specs=[pl.BlockSpec((None, tr, C), lambda s, i, c_ref: (s, c_ref[0] * nt + i, 0)),
                      pl.BlockSpec((None, tr, C), lambda s, i, c_ref: (s, i, 0))],
            out_specs=pl.BlockSpec((None, tr, C), lambda s, i, c_ref: (s, i, 0))),
        compiler_params=_cp(("parallel", "parallel")),
    )(jnp.reshape(core, (1,)).astype(jnp.int32), mine, theirs)


def final_exchange(fins):
    n = len(fins)

    def body(*refs):
        outs = refs[n:2 * n]
        ssem, rsem = refs[2 * n:]
        x, y, c = _place()
        cps = [pltpu.make_async_remote_copy(
            src_ref=_half(outs[j], c), dst_ref=_half(outs[j], c), send_sem=ssem.at[j], recv_sem=rsem.at[j],
            device_id=(x, y, 1 - c), device_id_type=MESH) for j in range(n)]
        for cp in cps:
            cp.start()
        for cp in cps:
            cp.wait()

    return pl.pallas_call(
        body, name="final_exchange", in_specs=[ANY] * n, out_specs=[ANY] * n,
        out_shape=[jax.ShapeDtypeStruct(f.shape, f.dtype) for f in fins],
        input_output_aliases={j: j for j in range(n)},
        scratch_shapes=[pltpu.SemaphoreType.DMA((n,)), pltpu.SemaphoreType.DMA((n,))],
    )(*fins)


def adamw_big(w, m, v, gs, row0, name):
    L, R, C = w.shape
    tr = _row_tile(math.gcd(R, row0) if row0 else R, max(16, 262144 // C // 16 * 16))
    b0 = row0 // tr

    def body(*refs):
        w_ref, m_ref, v_ref = refs[:3]
        g_refs = refs[3:3 + L]
        g_ref, d_ref, nm_ref, nv_ref = refs[3 + L:]
        g = g_refs[0][...]
        for l in range(1, L):
            g = jnp.where(pl.program_id(0) == l, g_refs[l][...], g)
        d, nm, nv = _adamw_math(w_ref[...], g, m_ref[...], v_ref[...])
        g_ref[...] = g
        d_ref[...] = d
        nm_ref[...] = nm
        nv_ref[...] = nv

    own = pl.BlockSpec((None, tr, C), lambda l, i: (l, i, 0))
    off = pl.BlockSpec((tr, C), lambda l, i: (b0 + i, 0))
    return pl.pallas_call(
        body, name=name, grid=(L, R // tr), in_specs=[own, own, own] + [off] * L, out_specs=[own] * 4,
        out_shape=[jax.ShapeDtypeStruct((L, R, C), f32)] * 4, compiler_params=_cp(("parallel", "parallel")),
    )(w, m, v, *gs)


def _adamw_math(w, g, m, v):
    m = B1 * m + (1.0 - B1) * g
    v = B2 * v + (1.0 - B2) * (g * g)
    m_hat = m / (1.0 - B1 ** STEP)
    v_hat = v / (1.0 - B2 ** STEP)
    delta = -LR * (m_hat / (jnp.sqrt(v_hat) + AEPS) + WD * w)
    return delta, m, v


def adamw_small(w, m, v, g):
    def body(w_ref, m_ref, v_ref, g_ref, d_ref, nm_ref, nv_ref):
        d, nm, nv = _adamw_math(w_ref[...], g_ref[...], m_ref[...], v_ref[...])
        d_ref[...] = d
        nm_ref[...] = nm
        nv_ref[...] = nv

    return pl.pallas_call(body, name="adamw_small", out_shape=[jax.ShapeDtypeStruct(w.shape, f32)] * 3)(w, m, v, g)


CONV =(("conv_qkv_b", 2), ("ffn_conv_w", 2))
SMALL = ("rel_bias", "norm_mix_g", "norm_mem_g", "sinks_a", "a_log_b", "dt_bias_b", "out_norm_g_b", "norm_ffn_g",
         "ffn_conv_b", "final_norm_g")
WEIGHTS = ("rel_bias", "norm_mix_g", "norm_mem_g", "w_mem_kv", "w_out", "w_in_a", "sinks_a", "w_in_b", "conv_qkv_b",
           "a_log_b", "dt_bias_b", "out_norm_g_b", "norm_ffn_g", "w_gate_up", "ffn_conv_w", "ffn_conv_b", "w_down",
           "final_norm_g")
ARGS = ("x", "mem") + WEIGHTS + ("loss_target",) + tuple("m_" + n for n in WEIGHTS) + tuple("v_" + n for n in WEIGHTS)


def _rows(a, width):
    flat = a.reshape(-1)
    pad = (-flat.shape[0]) % (8 * width)
    if pad:
        flat = jnp.concatenate([flat, jnp.zeros((pad,), a.dtype)])
    return flat.reshape(-1, width)


def _nrows(shape, width):
    return _pad_to(-(-math.prod(shape) // width), 8)


def _pack(arrs, width, total_rows, dtype):
    parts = [_rows(a.astype(dtype), width) for a in arrs]
    used = sum(p.shape[0] for p in parts)
    if total_rows > used:
        parts.append(jnp.zeros((total_rows - used, width), dtype))
    return jnp.concatenate(parts, axis=0)


def _unpack(buf, shapes, width):
    out, r = [], 0
    for s in shapes:
        n = _nrows(s, width)
        out.append(buf[r:r + n].reshape(-1)[:math.prod(s)].reshape(s))
        r += n
    return out


def _pad_to(n, mult):
    return -(-n // mult) * mult


def kernel(x, mem, rel_bias, norm_mix_g, norm_mem_g, w_mem_kv, w_out, w_in_a, sinks_a, w_in_b, conv_qkv_b, a_log_b, dt_bias_b, out_norm_g_b, norm_ffn_g, w_gate_up, ffn_conv_w, ffn_conv_b, w_down, final_norm_g, loss_target, m_rel_bias, m_norm_mix_g, m_norm_mem_g, m_w_mem_kv, m_w_out, m_w_in_a, m_sinks_a, m_w_in_b, m_conv_qkv_b, m_a_log_b, m_dt_bias_b, m_out_norm_g_b, m_norm_ffn_g, m_w_gate_up, m_ffn_conv_w, m_ffn_conv_b, m_w_down, m_final_norm_g, v_rel_bias, v_norm_mix_g, v_norm_mem_g, v_w_mem_kv, v_w_out, v_w_in_a, v_sinks_a, v_w_in_b, v_conv_qkv_b, v_a_log_b, v_dt_bias_b, v_out_norm_g_b, v_norm_ffn_g, v_w_gate_up, v_ffn_conv_w, v_ffn_conv_b, v_w_down, v_final_norm_g):
    A = dict(zip(ARGS, (x, mem, rel_bias, norm_mix_g, norm_mem_g, w_mem_kv, w_out, w_in_a, sinks_a, w_in_b, conv_qkv_b, a_log_b, dt_bias_b, out_norm_g_b, norm_ffn_g, w_gate_up, ffn_conv_w, ffn_conv_b, w_down, final_norm_g, loss_target, m_rel_bias, m_norm_mix_g, m_norm_mem_g, m_w_mem_kv, m_w_out, m_w_in_a, m_sinks_a, m_w_in_b, m_conv_qkv_b, m_a_log_b, m_dt_bias_b, m_out_norm_g_b, m_norm_ffn_g, m_w_gate_up, m_ffn_conv_w, m_ffn_conv_b, m_w_down, m_final_norm_g, v_rel_bias, v_norm_mix_g, v_norm_mem_g, v_w_mem_kv, v_w_out, v_w_in_a, v_sinks_a, v_w_in_b, v_conv_qkv_b, v_a_log_b, v_dt_bias_b, v_out_norm_g_b, v_norm_ffn_g, v_w_gate_up, v_ffn_conv_w, v_ffn_conv_b, v_w_down, v_final_norm_g)))
    chip = 2 * lax.axis_index("x") + lax.axis_index("y")
    core = lax.axis_index("c")
    n_down, n_out, n_mem = w_down.shape[1], w_out.shape[1], w_mem_kv.shape[1]

    def own_slot(shard):
        return lax.dynamic_update_index_in_dim(lax.empty((4,) + shard.shape, shard.dtype), shard, chip, 0)

    def bslot(w):
        return own_slot(w.astype(bf16))

    groups = {
        ("w_in", 0): [bslot(w_in_a[0])],
        ("w_mem", 0): [bslot(w_mem_kv[0]), bslot(w_mem_kv[1]), own_slot(conv_qkv_b[0]),
                       own_slot(ffn_conv_w.reshape(6, -1))],
        ("w_out", 0): [bslot(w_out[0])], ("w_gu", 0): [bslot(w_gate_up[0])], ("w_down", 0): [bslot(w_down[0])],
        ("w_in", 1): [bslot(w_in_b[0])],
        ("w_out", 1): [bslot(w_out[1])], ("w_gu", 1): [bslot(w_gate_up[1])], ("w_down", 1): [bslot(w_down[1])],
    }
    flights = dict(zip(groups, gather_start(list(groups.values()))))
    P = {"rel_bias": rel_bias, "sinks": sinks_a[0], "a_log": a_log_b[0], "dt_bias": dt_bias_b[0],
         "out_norm_g": out_norm_g_b[0], "g_mix": norm_mix_g, "g_mem": norm_mem_g, "g_ffn": norm_ffn_g,
         "g_fin": final_norm_g, "ffn_cb": [ffn_conv_b[0], ffn_conv_b[1]], "w_mem": [None, None], "w_out": [None, None],
         "w_gu": [None, None], "w_down": [None, None], "ffn_cw": [None, None]}

    def rows4(g):
        return g.reshape(4 * g.shape[1], g.shape[2])

    def arrive(key, after):
        if key not in flights:
            return
        got = gather_wait(flights.pop(key), after, "gather_wait_%s%d" % key)
        name, i = key
        if name == "w_in":
            P["w_in_a" if i == 0 else "w_in_b"] = (_lay_in_a if i == 0 else _lay_in_b)(_unchip_cols(got[0]))
        elif name == "w_mem":
            P["w_mem"] = [rows4(got[0]), rows4(got[1])]
            P["conv_qkv"] = _unchip_cols(got[2])
            cw = _unchip_cols(got[3]).reshape(2, 3, D_FF)
            P["ffn_cw"] = [cw[0], cw[1]]
        elif name == "w_out":
            P["w_out"][i] = _lay_out_a(rows4(got[0])) if i == 0 else rows4(got[0])
        elif name == "w_gu":
            P["w_gu"][i] = got[0]
        else:
            P["w_down"][i] = rows4(got[0])

    def chip_rows(g):
        return g.reshape(4, g.shape[0] // 4, g.shape[-1])

    sent, started = {}, []

    def ready(key, G, dep):
        kind, i = key
        tag = "%s%d" % key
        if kind == "ffn":
            names, partial = ("gu", "down"), [G["w_gu"][i], chip_rows(G["w_down"][i]).astype(bf16)]
        else:
            g_out = _unlay_out_a(G["w_out"][0]) if i == 0 else G["w_out"][1]
            g_in = _unlay_in_a(G["w_in_a"]) if i == 0 else _unlay_in_b(G["w_in_b"])
            names = ("out", "in", "mem")
            partial = [chip_rows(g_out).astype(bf16), _chip_cols(g_in).astype(bf16), chip_rows(G["w_mem"][i]).astype(bf16)]
        theirs = pair_exchange(partial, "pair_exchange_" + tag)
        pair = [pair_sum(p, t, core, "pair_sum_%s%d" % (nm, i)) for p, t, nm in zip(partial, theirs, names)]
        if key == ("mix", 0):
            sent[key] = (names, pair, chip_scatter(pair))
            return dep
        flight, token = scatter_start(pair, "scatter_start_" + tag)
        sent[key] = (names, flight)
        started.append(token[0, 0])
        if dep is not None:
            while started:
                dep = dep + started.pop()
        return dep

    P["arrive"], P["ready"] = arrive, ready

    loss, dx, G = _local_step(x[0], mem[0], loss_target[0], P)
    gfull = _grads_to_ref(G)

    fin = {}
    for key in (("ffn", 1), ("mix", 1), ("ffn", 0), ("mix", 0)):
        if key == ("mix", 0):
            names, pair, arrived = sent[key]
        else:
            names, flight = sent[key]
            pair, arrived = scatter_wait(flight, dx, "scatter_wait_%s%d" % key)
        for nm, p, r in zip(names, pair, arrived):
            fin[nm, key[1]] = sum_slots(p, r, chip, core, "sum_slots_%s%d" % (nm, key[1]))
    order = list(fin)
    done = dict(zip(order, final_exchange([fin[k] for k in order])))

    sm_shapes = [A[n].shape for n in SMALL] + [gfull[n].shape for n, _ in CONV] + [(LANE,)]
    sm_rows = _pad_to(sum(_nrows(s, LANE) for s in sm_shapes), 8)
    sbuf = _pack([gfull[n] for n in SMALL] + [gfull[n] for n, _ in CONV] + [loss[0]], LANE, sm_rows, f32)
    tot = _unpack(allreduce_small(sbuf), sm_shapes, LANE)
    gsmall = dict(zip(SMALL, tot[:len(SMALL)]))
    for (n, axis), t in zip(CONV, tot[len(SMALL):len(SMALL) + len(CONV)]):
        sh = A[n].shape[axis]
        gsmall[n] = lax.dynamic_slice_in_dim(t, chip * sh, sh, axis)
    loss_out = tot[-1][0]

    out = {}
    plan = (("w_gate_up", [done["gu", 0], done["gu", 1]]), ("w_down", [done["down", 0], done["down", 1]]),
            ("w_out", [done["out", 0], done["out", 1]]), ("w_mem_kv", [done["mem", 0], done["mem", 1]]),
            ("w_in_a", [done["in", 0]]), ("w_in_b", [done["in", 1]]))
    for n, gs in plan:
        shape3 = (len(gs),) + gs[0].shape
        res = adamw_big(A[n].reshape(shape3), A["m_" + n].reshape(shape3), A["v_" + n].reshape(shape3), gs, 0,
                        "adamw_" + n)
        for key, r in zip(("grad_", "delta_", "new_m_", "new_v_"), res):
            out[key + n] = r.reshape(A[n].shape)
    names = SMALL + tuple(n for n, _ in CONV)
    shapes = [A[n].shape for n in names]
    rows = _pad_to(sum(_nrows(s, LANE) for s in shapes), 8)
    packs = [_pack([src[n] for n in names], LANE, rows, f32)
             for src in ({n: A[n] for n in names}, {n: A["m_" + n] for n in names}, {n: A["v_" + n] for n in names}, gsmall)]
    res = adamw_small(*packs)
    for key, r in zip(("delta_", "new_m_", "new_v_"), res):
        for n, a in zip(names, _unpack(r, shapes, LANE)):
            out[key + n] = a
    for n in names:
        out["grad_" + n] = gsmall[n]
    return (loss_out, dx[None], *[out["grad_" + n] for n in WEIGHTS], *[out["delta_" + n] for n in WEIGHTS],
            *[out["new_m_" + n] for n in WEIGHTS], *[out["new_v_" + n] for n in WEIGHTS])
```

```python
import functools
import math

import numpy as np
import jax
import jax.numpy as jnp
from jax import lax
from jax.experimental import pallas as pl
from jax.experimental.pallas import tpu as pltpu

f32 = jnp.float32
bf16 = jnp.bfloat16
HI = lax.Precision.HIGHEST
MESH = pl.DeviceIdType.MESH

D = 1024
MEM_LEN = 256
EPS = 1e-6
A_HEADS, A_KV, A_DH = 12, 2, 64
A_Q = 768
BLK = 128
N_BUCKETS, MAX_DIST = 32, 128
B_QK, B_V, B_DH = 384, 768, 128
B_QKV = 1536
CHUNK = 64
X_Q = 256
D_FF = 2816
IN_A = 1280
IN_B = 2572
IN_B_PAD = 2688
LANE = 128
VMEM_LIMIT = 56 * 1024 * 1024
MM_ROWS = 1024

LR, B1, B2, AEPS, WD, STEP = 0.001, 0.9, 0.999, 1e-08, 0.01, 10


def _cp(sem=None):
    return pltpu.CompilerParams(dimension_semantics=sem, vmem_limit_bytes=VMEM_LIMIT)


def _dg(a, b, ca, cb, prec=None):
    return lax.dot_general(a, b, (((ca,), (cb,)), ((), ())), precision=prec, preferred_element_type=f32)


@jax.custom_vjp
def bdot(a, b):
    return _dg(a.astype(bf16), b.astype(bf16), 1, 0)


def _bdot_f(a, b):
    return bdot(a, b), (a, b)


def _bdot_b(res, g):
    a, b = res
    gb = g.astype(bf16)
    return _dg(gb, b.astype(bf16), 1, 1), _dg(a.astype(bf16), gb, 0, 0)


bdot.defvjp(_bdot_f, _bdot_b)


@jax.custom_vjp
def bdot_nt(a, b):
    return _dg(a.astype(bf16), b.astype(bf16), 1, 1)


def _bdot_nt_f(a, b):
    return bdot_nt(a, b), (a, b)


def _bdot_nt_b(res, g):
    a, b = res
    gb = g.astype(bf16)
    return _dg(gb, b.astype(bf16), 1, 0), _dg(gb, a.astype(bf16), 0, 0)


bdot_nt.defvjp(_bdot_nt_f, _bdot_nt_b)


def _shift_rows(x, s, down):
    n = x.shape[0]
    row = lax.broadcasted_iota(jnp.int32, x.shape, 0)
    if down:
        return jnp.where(row >= s, pltpu.roll(x, s, 0), 0.0)
    return jnp.where(row < n - s, pltpu.roll(x, n - s, 0), 0.0)


@functools.partial(jax.custom_vjp, nondiff_argnums=(1,))
def shift_down(x, s):
    return _shift_rows(x, s, True)


def _sd_f(x, s):
    return _shift_rows(x, s, True), None


def _sd_b(s, _, g):
    return (_shift_rows(g, s, False),)


shift_down.defvjp(_sd_f, _sd_b)


def _sigmoid(x):
    return 1.0 / (1.0 + jnp.exp(-x))


def _silu(x):
    return x * _sigmoid(x)


def _rms(x, g):
    return x * lax.rsqrt(jnp.mean(x * x, axis=-1, keepdims=True) + EPS) * g


def _tile(n, cap):
    u = n // LANE
    best = 1
    for d in range(1, u + 1):
        if u % d == 0 and d * LANE <= cap:
            best = d
    return best * LANE


def mm_nn(a, w, res=None, out_dtype=f32, name="mm_nn"):
    M, K = a.shape
    N = w.shape[1]
    tm, tn = min(MM_ROWS, M), _tile(N, 1024)

    def body(*refs):
        if res is None:
            a_ref, w_ref, o_ref = refs
            o_ref[...] = _dg(a_ref[...].astype(bf16), w_ref[...], 1, 0).astype(out_dtype)
        else:
            a_ref, w_ref, r_ref, o_ref = refs
            o_ref[...] = (r_ref[...] + _dg(a_ref[...].astype(bf16), w_ref[...], 1, 0)).astype(out_dtype)

    in_specs = [pl.BlockSpec((tm, K), lambda n, m: (m, 0)), pl.BlockSpec((K, tn), lambda n, m: (0, n))]
    args = [a, w]
    if res is not None:
        in_specs.append(pl.BlockSpec((tm, tn), lambda n, m: (m, n)))
        args.append(res)
    return pl.pallas_call(
        body, name=name, grid=(N // tn, M // tm), in_specs=in_specs,
        out_specs=pl.BlockSpec((tm, tn), lambda n, m: (m, n)),
        out_shape=jax.ShapeDtypeStruct((M, N), out_dtype),
        compiler_params=_cp(("parallel", "parallel")),
    )(*args)


def mm_nt(dy, w, out_dtype=f32, name="mm_nt"):
    M, N = dy.shape
    K = w.shape[0]
    tm, tn = min(MM_ROWS, M), _tile(N, 1024)
    assert out_dtype == f32 or tn == N

    def body(dy_ref, w_ref, o_ref):
        part = _dg(dy_ref[...].astype(bf16), w_ref[...], 1, 1)
        if tn == N:
            o_ref[...] = part.astype(out_dtype)
        else:
            @pl.when(pl.program_id(1) == 0)
            def _():
                o_ref[...] = jnp.zeros_like(o_ref)
            o_ref[...] += part

    return pl.pallas_call(
        body, name=name, grid=(M // tm, N // tn),
        in_specs=[pl.BlockSpec((tm, tn), lambda m, n: (m, n)), pl.BlockSpec((K, tn), lambda m, n: (0, n))],
        out_specs=pl.BlockSpec((tm, K), lambda m, n: (m, 0)),
        out_shape=jax.ShapeDtypeStruct((M, K), out_dtype),
        compiler_params=_cp(("parallel", "arbitrary")),
    )(dy, w)


def mm_tn(a, dy, name="mm_tn"):
    M, K = a.shape
    N = dy.shape[1]
    tm, tk, tn = min(MM_ROWS, M), _tile(K, 1408), _tile(N, 1024)

    def body(a_ref, dy_ref, o_ref):
        @pl.when(pl.program_id(2) == 0)
        def _():
            o_ref[...] = jnp.zeros_like(o_ref)
        o_ref[...] += _dg(a_ref[...].astype(bf16), dy_ref[...].astype(bf16), 0, 0)

    return pl.pallas_call(
        body, name=name, grid=(K // tk, N // tn, M // tm),
        in_specs=[pl.BlockSpec((tm, tk), lambda k, n, m: (m, k)), pl.BlockSpec((tm, tn), lambda k, n, m: (m, n))],
        out_specs=pl.BlockSpec((tk, tn), lambda k, n, m: (k, n)),
        out_shape=jax.ShapeDtypeStruct((K, N), f32),
        compiler_params=_cp(("parallel", "parallel", "arbitrary")),
    )(a, dy)


def rms_fwd(h, g, name):
    S = h.shape[0]
    t = min(512, S)

    def body(h_ref, g_ref, o_ref):
        o_ref[...] = _rms(h_ref[...], g_ref[...]).astype(bf16)

    return pl.pallas_call(
        body, name=name, grid=(S // t,),
        in_specs=[pl.BlockSpec((t, D), lambda i: (i, 0)), pl.BlockSpec((1, D), lambda i: (0, 0))],
        out_specs=pl.BlockSpec((t, D), lambda i: (i, 0)),
        out_shape=jax.ShapeDtypeStruct((S, D), bf16),
        compiler_params=_cp(("parallel",)),
    )(h, g.reshape(1, D))


def rms_bwd(h, g, dn, dres, name):
    S = h.shape[0]
    t = min(512, S)

    def body(h_ref, g_ref, dn_ref, dr_ref, dh_ref, dg_ref):
        @pl.when(pl.program_id(0) == 0)
        def _():
            dg_ref[...] = jnp.zeros_like(dg_ref)
        _, vjp = jax.vjp(_rms, h_ref[...], g_ref[...])
        dh, dg = vjp(dn_ref[...])
        dh_ref[...] = dr_ref[...] + dh
        dg_ref[...] += dg

    tok = pl.BlockSpec((t, D), lambda i: (i, 0))
    vec = pl.BlockSpec((1, D), lambda i: (0, 0))
    return pl.pallas_call(
        body, name=name, grid=(S // t,), in_specs=[tok, vec, tok, tok], out_specs=[tok, vec],
        out_shape=[jax.ShapeDtypeStruct((S, D), f32), jax.ShapeDtypeStruct((1, D), f32)],
        compiler_params=_cp(("arbitrary",)),
    )(h, g.reshape(1, D), dn, dres)


def loss_head(h, g, target):
    S = h.shape[0]
    t = min(512, S)

    def f(hh, gg, tt):
        err = _rms(hh, gg) - tt
        return 0.5 * jnp.sum(jnp.mean(err * err, axis=-1, keepdims=True), axis=0, keepdims=True)

    def body(h_ref, g_ref, t_ref, loss_ref, dh_ref, dg_ref):
        @pl.when(pl.program_id(0) == 0)
        def _():
            dg_ref[...] = jnp.zeros_like(dg_ref)
            loss_ref[...] = jnp.zeros_like(loss_ref)
        val, vjp = jax.vjp(lambda a, b: f(a, b, t_ref[...]), h_ref[...], g_ref[...])
        dh, dg = vjp(jnp.ones((1, 1), f32))
        dh_ref[...] = dh
        dg_ref[...] += dg
        loss_ref[...] += jnp.broadcast_to(val, loss_ref.shape)

    tok = pl.BlockSpec((t, D), lambda i: (i, 0))
    vec = pl.BlockSpec((1, D), lambda i: (0, 0))
    return pl.pallas_call(
        body, name="loss_head", grid=(S // t,), in_specs=[tok, vec, tok],
        out_specs=[pl.BlockSpec((1, LANE), lambda i: (0, 0)), tok, vec],
        out_shape=[jax.ShapeDtypeStruct((1, LANE), f32), jax.ShapeDtypeStruct((S, D), f32),
                   jax.ShapeDtypeStruct((1, D), f32)],
        compiler_params=_cp(("arbitrary",)),
    )(h, g.reshape(1, D), target)


def memkv_fwd(mem, g, w, name):
    def body(m_ref, g_ref, w_ref, o_ref):
        o_ref[...] = _dg(_rms(m_ref[...], g_ref[...]).astype(bf16), w_ref[...], 1, 0)

    return pl.pallas_call(
        body, name=name, out_shape=jax.ShapeDtypeStruct((MEM_LEN, 2 * X_Q), f32), compiler_params=_cp(),
    )(mem, g.reshape(1, D), w)


def memkv_bwd(mem, g, w, dkv, name):
    def body(m_ref, g_ref, w_ref, d_ref, dg_ref, dw_ref):
        n, vjp = jax.vjp(lambda gg: _rms(m_ref[...], gg), g_ref[...])
        db = d_ref[...].astype(bf16)
        dw_ref[...] = _dg(n.astype(bf16), db, 0, 0)
        dg_ref[...] = vjp(_dg(db, w_ref[...], 1, 1))[0]

    return pl.pallas_call(
        body, name=name,
        out_shape=[jax.ShapeDtypeStruct((1, D), f32), jax.ShapeDtypeStruct((D, 2 * X_Q), f32)],
        compiler_params=_cp(),
    )(mem, g.reshape(1, D), w, dkv)


def _xattn_f(xq, mk, mv):
    lane = lax.broadcasted_iota(jnp.int32, (1, X_Q), 1)
    out = jnp.zeros(xq.shape, f32)
    for hd in range(4):
        msk = (lane // 64 == hd).astype(f32)
        s = bdot_nt(xq * msk, mk) * (64 ** -0.5)
        m = lax.stop_gradient(jnp.max(s, axis=-1, keepdims=True))
        p = jnp.exp(s - m)
        p = p / jnp.sum(p, axis=-1, keepdims=True)
        out = out + bdot(p, mv * msk)
    return out


def xattn_fwd(proj, col, kv, name):
    S = proj.shape[0]
    t = min(512, S)
    cb = col // X_Q

    def body(q_ref, k_ref, v_ref, o_ref):
        o_ref[...] = _xattn_f(q_ref[...], k_ref[...], v_ref[...]).astype(bf16)

    return pl.pallas_call(
        body, name=name, grid=(S // t,),
        in_specs=[pl.BlockSpec((t, X_Q), lambda i: (i, cb)), pl.BlockSpec((MEM_LEN, X_Q), lambda i: (0, 0)),
                  pl.BlockSpec((MEM_LEN, X_Q), lambda i: (0, 1))],
        out_specs=pl.BlockSpec((t, X_Q), lambda i: (i, 0)),
        out_shape=jax.ShapeDtypeStruct((S, X_Q), bf16),
        compiler_params=_cp(("parallel",)),
    )(proj, kv, kv)


def xattn_bwd(proj, col, kv, dmix, name):
    S = proj.shape[0]
    t = min(512, S)
    cb = col // X_Q

    def body(q_ref, k_ref, v_ref, do_ref, dq_ref, dk_ref, dv_ref):
        @pl.when(pl.program_id(0) == 0)
        def _():
            dk_ref[...] = jnp.zeros_like(dk_ref)
            dv_ref[...] = jnp.zeros_like(dv_ref)
        _, vjp = jax.vjp(_xattn_f, q_ref[...], k_ref[...], v_ref[...])
        dq, dk, dv = vjp(do_ref[...])
        dq_ref[...] = dq.astype(bf16)
        dk_ref[...] += dk
        dv_ref[...] += dv

    kvb = pl.BlockSpec((MEM_LEN, X_Q), lambda i: (0, 0))
    dq, dk, dv = pl.pallas_call(
        body, name=name, grid=(S // t,),
        in_specs=[pl.BlockSpec((t, X_Q), lambda i: (i, cb)), kvb,
                  pl.BlockSpec((MEM_LEN, X_Q), lambda i: (0, 1)), pl.BlockSpec((t, X_Q), lambda i: (i, 3))],
        out_specs=[pl.BlockSpec((t, X_Q), lambda i: (i, 0)), kvb, kvb],
        out_shape=[jax.ShapeDtypeStruct((S, X_Q), bf16), jax.ShapeDtypeStruct((MEM_LEN, X_Q), f32),
                   jax.ShapeDtypeStruct((MEM_LEN, X_Q), f32)],
        compiler_params=_cp(("arbitrary",)),
    )(proj, kv, kv, dmix)
    return dq, jnp.concatenate([dk, dv], axis=1)


def _bucket_map():
    qi = np.arange(BLK)[:, None]
    kj = np.arange(2 * BLK)[None, :]
    n = np.maximum(BLK + qi - kj, 0)
    max_exact = N_BUCKETS // 2
    nf = np.maximum(n, 1).astype(np.float64)
    large = max_exact + (np.log(nf / max_exact) / math.log(MAX_DIST / max_exact)
                         * (N_BUCKETS - max_exact)).astype(np.int32)
    large = np.minimum(large, N_BUCKETS - 1)
    return np.where(n < max_exact, n, large).astype(np.int32)


def bias_build(rel_bias):
    def body(rb_ref, bk_ref, o_ref):
        bk = bk_ref[...]
        for h in range(A_HEADS):
            acc = jnp.zeros((BLK, 2 * BLK), f32)
            for b in range(N_BUCKETS):
                acc = jnp.where(bk == b, rb_ref[b, h], acc)
            o_ref[h] = acc

    return pl.pallas_call(
        body, name="bias_build",
        in_specs=[pl.BlockSpec(memory_space=pltpu.SMEM), pl.BlockSpec(memory_space=pltpu.VMEM)],
        out_specs=pl.BlockSpec(memory_space=pltpu.VMEM),
        out_shape=jax.ShapeDtypeStruct((A_HEADS, BLK, 2 * BLK), f32), compiler_params=_cp(),
    )(rel_bias, jnp.asarray(_bucket_map()))


def bias_grad(dbias):
    def body(d_ref, bk_ref, o_ref):
        bk = bk_ref[...]
        row = lax.broadcasted_iota(jnp.int32, (N_BUCKETS, LANE), 0)
        lane = lax.broadcasted_iota(jnp.int32, (N_BUCKETS, LANE), 1)
        acc = jnp.zeros((N_BUCKETS, LANE), f32)
        for h in range(A_HEADS):
            d = d_ref[h]
            for b in range(N_BUCKETS):
                s = jnp.sum(jnp.where(bk == b, d, 0.0), keepdims=True)
                acc = acc + jnp.where((row == b) & (lane == h), s, 0.0)
        o_ref[...] = acc

    return pl.pallas_call(
        body, name="bias_grad", out_shape=jax.ShapeDtypeStruct((N_BUCKETS, LANE), f32), compiler_params=_cp(),
    )(dbias, jnp.asarray(_bucket_map()))


def _swa_f(qb, kp, kc, vp, vc, bias, sk, first):
    kband = jnp.concatenate([kp, kc], axis=0)
    vband = jnp.concatenate([vp, vc], axis=0)
    qi = lax.broadcasted_iota(jnp.int32, (BLK, 2 * BLK), 0)
    kj = lax.broadcasted_iota(jnp.int32, (BLK, 2 * BLK), 1)
    rel = kj - qi
    ok = (rel >= 1) & (rel <= BLK) & ((kj >= BLK) | jnp.logical_not(first))
    lane = lax.broadcasted_iota(jnp.int32, (1, LANE), 1)
    lane_b = lax.broadcasted_iota(jnp.int32, (BLK, LANE), 1)
    outs = []
    for p in range(A_HEADS // 2):
        qp = qb[:, LANE * p:LANE * (p + 1)]
        acc = jnp.zeros((BLK, LANE), f32)
        for g in range(2):
            h = g * (A_HEADS // 2) + p
            msk = (lane // A_DH == g).astype(f32)
            s = bdot_nt(qp * msk, kband) * (A_DH ** -0.5) + bias[h]
            s = jnp.where(ok, s, -1e30)
            skb = jnp.broadcast_to(sk[h:h + 1, :], (BLK, LANE))
            sink = jnp.sum(jnp.where(lane_b == 0, skb, 0.0), axis=-1, keepdims=True)
            m = lax.stop_gradient(jnp.maximum(jnp.max(s, axis=-1, keepdims=True), sink))
            e = jnp.exp(s - m)
            prob = e / (jnp.sum(e, axis=-1, keepdims=True) + jnp.exp(sink - m))
            acc = acc + bdot(prob, vband) * msk
        outs.append(acc)
    return jnp.concatenate(outs, axis=1)


def _swa_specs(nb, rev):
    bi = (lambda i: nb - 1 - i) if rev else (lambda i: i)
    return [
        pl.BlockSpec((BLK, A_Q), lambda i: (bi(i), 0)),
        pl.BlockSpec((BLK, LANE), lambda i: (jnp.maximum(bi(i) - 1, 0), 6)),
        pl.BlockSpec((BLK, LANE), lambda i: (bi(i), 6)),
        pl.BlockSpec((BLK, LANE), lambda i: (jnp.maximum(bi(i) - 1, 0), 7)),
        pl.BlockSpec((BLK, LANE), lambda i: (bi(i), 7)),
        pl.BlockSpec((A_HEADS, BLK, 2 * BLK), lambda i: (0, 0, 0)),
        pl.BlockSpec((16, LANE), lambda i: (0, 0)),
    ]


def swa_fwd(proj, bias, sk):
    S = proj.shape[0]
    nb = S // BLK

    def body(q_ref, kp_ref, kc_ref, vp_ref, vc_ref, b_ref, s_ref, o_ref):
        o_ref[...] = _swa_f(q_ref[...], kp_ref[...], kc_ref[...], vp_ref[...], vc_ref[...], b_ref[...], s_ref[...],
                            pl.program_id(0) == 0).astype(bf16)

    return pl.pallas_call(
        body, name="swa_fwd", grid=(nb,), in_specs=_swa_specs(nb, False),
        out_specs=pl.BlockSpec((BLK, A_Q), lambda i: (i, 0)),
        out_shape=jax.ShapeDtypeStruct((S, A_Q), bf16), compiler_params=_cp(("parallel",)),
    )(proj, proj, proj, proj, proj, bias, sk)


def swa_bwd(proj, bias, sk, dmix):
    S = proj.shape[0]
    nb = S // BLK

    def body(q_ref, kp_ref, kc_ref, vp_ref, vc_ref, b_ref, s_ref, do_ref, dqkv_ref, db_ref, ds_ref, ck, cv):
        i = pl.program_id(0)

        @pl.when(i == 0)
        def _():
            db_ref[...] = jnp.zeros_like(db_ref)
            ds_ref[...] = jnp.zeros_like(ds_ref)
            ck[...] = jnp.zeros_like(ck)
            cv[...] = jnp.zeros_like(cv)
        first = i == nb - 1
        _, vjp = jax.vjp(lambda *a: _swa_f(*a, first), q_ref[...], kp_ref[...], kc_ref[...], vp_ref[...],
                         vc_ref[...], b_ref[...], s_ref[...])
        dq, dkp, dkc, dvp, dvc, db, ds = vjp(do_ref[...])
        dqkv_ref[...] = jnp.concatenate([dq, dkc + ck[...], dvc + cv[...]], axis=1).astype(bf16)
        ck[...] = dkp
        cv[...] = dvp
        db_ref[...] += db
        ds_ref[...] += ds

    return pl.pallas_call(
        body, name="swa_bwd", grid=(nb,),
        in_specs=_swa_specs(nb, True) + [pl.BlockSpec((BLK, A_Q), lambda i: (nb - 1 - i, 0))],
        out_specs=[pl.BlockSpec((BLK, D), lambda i: (nb - 1 - i, 0)),
                   pl.BlockSpec((A_HEADS, BLK, 2 * BLK), lambda i: (0, 0, 0)),
                   pl.BlockSpec((16, LANE), lambda i: (0, 0))],
        out_shape=[jax.ShapeDtypeStruct((S, D), bf16), jax.ShapeDtypeStruct((A_HEADS, BLK, 2 * BLK), f32),
                   jax.ShapeDtypeStruct((16, LANE), f32)],
        scratch_shapes=[pltpu.VMEM((BLK, LANE), f32), pltpu.VMEM((BLK, LANE), f32)],
        compiler_params=_cp(("arbitrary",)),
    )(proj, proj, proj, proj, proj, bias, sk, dmix)


def _dnprep_f(xext, w, is_qk):
    c = (w[3:4] * xext + w[2:3] * shift_down(xext, 1) + w[1:2] * shift_down(xext, 2) + w[0:1] * shift_down(xext, 3))
    a = _silu(c)[HALO:]
    n = a * lax.rsqrt(jnp.sum(a * a, axis=-1, keepdims=True) + EPS)
    return jnp.where(is_qk, n, a)


def dnprep_fwd(proj, cw):
    S = proj.shape[0]
    nblk = B_QKV // LANE
    T = S

    def body(x_ref, w_ref, o_ref):
        is_qk = pl.program_id(0) < 2 * B_QK // LANE
        wv = w_ref[...]

        def tile(r0, first):
            o_ref[pl.ds(r0, T), :] = _dnprep_f(_glu_gext(x_ref, r0, first, T), wv, is_qk)

        tile(0, True)

    return pl.pallas_call(
        body, name="dnprep_fwd", grid=(nblk,),
        in_specs=[pl.BlockSpec((S, LANE), lambda j: (0, j)), pl.BlockSpec((4, LANE), lambda j: (0, j))],
        out_specs=pl.BlockSpec((S, LANE), lambda j: (0, j)),
        out_shape=jax.ShapeDtypeStruct((S, B_QKV), f32), compiler_params=_cp(("parallel",)),
    )(proj, cw)


def dnprep_bwd(proj, cw, dqkvn):
    S = proj.shape[0]
    nblk = B_QKV // LANE

    T = S

    def body(x_ref, w_ref, d_ref, dx_ref, dw_ref):
        is_qk = pl.program_id(0) < 2 * B_QK // LANE
        wv = w_ref[...]

        def tile(r0, first):
            _, vjp = jax.vjp(lambda a, b: _dnprep_f(a, b, is_qk), _glu_gext(x_ref, r0, first, T), wv)
            dx, dw = vjp(d_ref[pl.ds(r0, T), :])
            dx_ref[pl.ds(r0, T), :] = dx[HALO:].astype(bf16)
            if not first:
                dx_ref[pl.ds(r0 - HALO, HALO), :] += dx[:HALO]
            return dw

        dw_ref[...] = tile(0, True)

    col = pl.BlockSpec((S, LANE), lambda j: (0, j))
    wsp = pl.BlockSpec((4, LANE), lambda j: (0, j))
    return pl.pallas_call(
        body, name="dnprep_bwd", grid=(nblk,), in_specs=[col, wsp, col], out_specs=[col, wsp],
        out_shape=[jax.ShapeDtypeStruct((S, B_QKV), bf16), jax.ShapeDtypeStruct((4, B_QKV), f32)],
        compiler_params=_cp(("parallel",)),
    )(proj, cw, dqkvn)


def _hdot(a, b, ca=1, cb=0):
    return _dg(a, b, ca, cb, HI)


def _bdg(a, b, ca, cb):
    dn = (((ca,), (cb,)), ((0,), (0,)))
    ah, bh = a.astype(bf16), b.astype(bf16)
    al, bl = (a - ah.astype(f32)).astype(bf16), (b - bh.astype(f32)).astype(bf16)
    return (lax.dot_general(ah, bh, dn, preferred_element_type=f32)
            + lax.dot_general(ah, bl, dn, preferred_element_type=f32)
            + lax.dot_general(al, bh, dn, preferred_element_type=f32))


@jax.custom_vjp
def hbd(a, b):
    return _bdg(a, b, 2, 1)


@jax.custom_vjp
def hbd_nt(a, b):
    return _bdg(a, b, 2, 2)


@jax.custom_vjp
def hbd_tn(a, b):
    return _bdg(a, b, 1, 1)


hbd.defvjp(lambda a, b: (hbd(a, b), (a, b)), lambda r, g: (hbd_nt(g, r[1]), hbd_tn(r[0], g)))
hbd_nt.defvjp(lambda a, b: (hbd_nt(a, b), (a, b)), lambda r, g: (hbd(g, r[1]), hbd_tn(g, r[0])))
hbd_tn.defvjp(lambda a, b: (hbd_tn(a, b), (a, b)), lambda r, g: (hbd_nt(r[1], g), hbd(r[0], g)))


def _stack(xs):
    return jnp.concatenate([x[None] for x in xs], axis=0)


def _lane_col(x, j):
    lane = lax.broadcasted_iota(jnp.int32, (1, LANE), 1)
    return jnp.sum(jnp.where(lane == j, x, 0.0), axis=-1, keepdims=True)


def _tri_inv(a_mat):
    r = lax.broadcasted_iota(jnp.int32, (1, CHUNK, CHUNK), 1)
    c = lax.broadcasted_iota(jnp.int32, (1, CHUNK, CHUNK), 2)
    pw = -a_mat
    inv = (r == c).astype(f32) + pw
    for _ in range(5):
        pw = hbd(pw, pw)
        inv = inv + hbd(inv, pw)
    return inv


@jax.custom_vjp
def _tri_inv_known(a_mat, inv):
    return inv


_tri_inv_known.defvjp(lambda a, inv: (inv, inv),
                      lambda inv, g: (-hbd_tn(inv, hbd_nt(g, inv)), jnp.zeros_like(inv)))


def _dnc_f(q, k, v, seg, prm, inverse=_tri_inv):
    C = CHUNK
    B = q.shape[0]
    rows = seg.shape[0]
    beta_all = _sigmoid(seg)
    xx = seg + prm[1:2]
    g_all = -jnp.exp(prm[0:1]) * (jnp.maximum(xx, 0.0) + jnp.log(1.0 + jnp.exp(-jnp.abs(xx))))
    r2 = lax.broadcasted_iota(jnp.int32, (rows, rows), 0)
    c2 = lax.broadcasted_iota(jnp.int32, (rows, rows), 1)
    within = (r2 >= c2) & (r2 // C == c2 // C)
    gc_all = _hdot(within.astype(f32), g_all)
    beta = _stack([_lane_col(beta_all[C * j:C * (j + 1)], h) for j in range(rows // C) for h in range(6)])
    gc = _stack([_lane_col(gc_all[C * j:C * (j + 1)], 6 + h) for j in range(rows // C) for h in range(6)])
    r = lax.broadcasted_iota(jnp.int32, (1, C, C), 1)
    c = lax.broadcasted_iota(jnp.int32, (1, C, C), 2)
    incl = r >= c
    strict = r > c
    eye = (r == c).astype(f32)
    g_row = hbd(jnp.ones((B, C, C), f32), eye * gc)
    decay = jnp.where(incl, jnp.exp(jnp.where(incl, gc - g_row, 0.0)), 0.0)
    a_mat = beta * hbd_nt(k, k) * jnp.where(strict, decay, 0.0)
    eg = jnp.exp(gc)
    inv = inverse(a_mat)
    u = hbd(inv, beta * v)
    w = hbd(inv, (beta * eg) * k)
    qc = q * (B_DH ** -0.5)
    attn = hbd_nt(qc, k) * decay
    last = (lax.broadcasted_iota(jnp.int32, (1, C, 1), 1) == C - 1).astype(f32)
    g_last = jnp.sum(gc * last, axis=1, keepdims=True)
    dc = jnp.broadcast_to(jnp.exp(g_last), (B, 1, LANE)).reshape(B, LANE)
    return u, w, qc * eg, k * jnp.exp(g_last - gc), attn, dc, inv


def _b1(a, b, ca, cb):
    return lax.dot_general(a.astype(bf16), b.astype(bf16), (((ca,), (cb,)), ((0,), (0,))), preferred_element_type=f32)


@jax.custom_vjp
def sbd(a, b):
    return _b1(a, b, 2, 1)


@jax.custom_vjp
def sbd_nt(a, b):
    return _b1(a, b, 2, 2)


@jax.custom_vjp
def sbd_tn(a, b):
    return _b1(a, b, 1, 1)


sbd.defvjp(lambda a, b: (sbd(a, b), (a, b)), lambda r, g: (sbd_nt(g, r[1]), sbd_tn(r[0], g)))
sbd_nt.defvjp(lambda a, b: (sbd_nt(a, b), (a, b)), lambda r, g: (sbd(g, r[1]), sbd_tn(g, r[0])))
sbd_tn.defvjp(lambda a, b: (sbd_tn(a, b), (a, b)), lambda r, g: (sbd_nt(r[1], g), sbd(r[0], g)))


def _dns_f(S0, u, w, qd, kt, attn, dcrows):
    dc = _lane_col(dcrows, 0).reshape(6, 1, 1)
    delta = u - sbd(w, S0)
    out = sbd(qd, S0) + sbd(attn, delta)
    return out, dc * S0 + sbd_tn(kt, delta)


def _dnpost_f(o, z, grow):
    outs = []
    for h in range(6):
        oh = o[:, LANE * h:LANE * (h + 1)]
        outs.append(oh * lax.rsqrt(jnp.mean(oh * oh, axis=-1, keepdims=True) + EPS) * grow
                    * _silu(z[:, LANE * h:LANE * (h + 1)]))
    return jnp.concatenate(outs, axis=1)


def _hs(h):
    return slice(LANE * h, LANE * (h + 1))


DN_CHUNKS = 4


def _heads(ref, share):
    return _stack([ref[CHUNK * j:CHUNK * (j + 1), _hs(h // share)]
                   for j in range(ref.shape[0] // CHUNK) for h in range(6)])


def _put_heads(ref, val):
    for j in range(ref.shape[0] // CHUNK):
        for h in range(6):
            ref[CHUNK * j:CHUNK * (j + 1), _hs(h)] = val[6 * j + h]


def _dnc_in_specs():
    rows = CHUNK * DN_CHUNKS
    return [
        pl.BlockSpec((rows, B_QK), lambda n: (n, 0)),
        pl.BlockSpec((rows, B_QK), lambda n: (n, 1)),
        pl.BlockSpec((rows, B_V), lambda n: (n, 1)),
        pl.BlockSpec((rows, LANE), lambda n: (n, 20)),
        pl.BlockSpec((8, LANE), lambda n: (0, 0)),
    ]


def _dnc_out_specs(rev_nc=None, chunks=1):
    ci = (lambda n: n) if rev_nc is None else (lambda n: rev_nc - 1 - n)
    wide = pl.BlockSpec((CHUNK * chunks, B_V), lambda n: (ci(n), 0))
    return [wide, wide, wide, wide, pl.BlockSpec((chunks, 6, CHUNK, CHUNK), lambda n: (ci(n), 0, 0, 0)),
            pl.BlockSpec((chunks, 8, LANE), lambda n: (ci(n), 0, 0))]


def _dc_rows(dc):
    pad = jnp.zeros((2, LANE), f32)
    return _stack([jnp.concatenate([dc[6 * j:6 * (j + 1)], pad], axis=0) for j in range(dc.shape[0] // 6)])


def _dnc_shapes(S):
    nc = S // CHUNK
    wide = jax.ShapeDtypeStruct((S, B_V), f32)
    return [wide, wide, wide, wide, jax.ShapeDtypeStruct((nc, 6, CHUNK, CHUNK), f32),
            jax.ShapeDtypeStruct((nc, 8, LANE), f32)]


def dnc_fwd(qkvn, proj, prm):
    S = proj.shape[0]

    def body(q_ref, k_ref, v_ref, s_ref, p_ref, u_ref, w_ref, qd_ref, kt_ref, at_ref, dc_ref, inv_ref):
        u, w, qd, kt, attn, dc, inv = _dnc_f(_heads(q_ref, 2), _heads(k_ref, 2), _heads(v_ref, 1), s_ref[...],
                                             p_ref[...])
        inv_ref[...] = inv.reshape(inv_ref.shape)
        _put_heads(u_ref, u)
        _put_heads(w_ref, w)
        _put_heads(qd_ref, qd)
        _put_heads(kt_ref, kt)
        at_ref[...] = attn.reshape(at_ref.shape)
        dc_ref[...] = _dc_rows(dc)

    outs = _dnc_out_specs(chunks=DN_CHUNKS)
    out = pl.pallas_call(
        body, name="dn_chunk_fwd", grid=(S // (CHUNK * DN_CHUNKS),), in_specs=_dnc_in_specs(),
        out_specs=outs + [outs[4]], out_shape=_dnc_shapes(S) + [_dnc_shapes(S)[4]],
        compiler_params=_cp(("parallel",)),
    )(qkvn, qkvn, qkvn, proj, prm)
    return out[:6], out[6]


def dnc_bwd(qkvn, proj, prm, inv, cots):
    S = proj.shape[0]

    def body(q_ref, k_ref, v_ref, s_ref, p_ref, inv_ref, du_ref, dw_ref, dqd_ref, dkt_ref, dat_ref, ddc_ref,
             dx_ref, dseg_ref, dprm_ref):
        @pl.when(pl.program_id(0) == 0)
        def _():
            dprm_ref[...] = jnp.zeros_like(dprm_ref)
        nb = 6 * DN_CHUNKS
        known = functools.partial(_tri_inv_known, inv=inv_ref[...].reshape(nb, CHUNK, CHUNK))
        _, vjp = jax.vjp(lambda *a: _dnc_f(*a, inverse=known)[:6], _heads(q_ref, 2), _heads(k_ref, 2),
                         _heads(v_ref, 1), s_ref[...], p_ref[...])
        ddc = jnp.concatenate([ddc_ref[j, 0:6, :] for j in range(DN_CHUNKS)], axis=0)
        dq, dk, dv, dseg, dprm = vjp((_heads(du_ref, 1), _heads(dw_ref, 1), _heads(dqd_ref, 1), _heads(dkt_ref, 1),
                                      dat_ref[...].reshape(nb, CHUNK, CHUNK), ddc))
        for j in range(DN_CHUNKS):
            o = 6 * j
            dx_ref[CHUNK * j:CHUNK * (j + 1), :] = jnp.concatenate(
                [dq[o] + dq[o + 1], dq[o + 2] + dq[o + 3], dq[o + 4] + dq[o + 5],
                 dk[o] + dk[o + 1], dk[o + 2] + dk[o + 3], dk[o + 4] + dk[o + 5]] + [dv[o + h] for h in range(6)], axis=1)
        dseg_ref[...] = dseg.astype(bf16)
        dprm_ref[...] += dprm

    rows = CHUNK * DN_CHUNKS
    outs = _dnc_out_specs(chunks=DN_CHUNKS)
    return pl.pallas_call(
        body, name="dn_chunk_bwd", grid=(S // rows,),
        in_specs=_dnc_in_specs() + [outs[4]] + outs,
        out_specs=[pl.BlockSpec((rows, B_QKV), lambda n: (n, 0)), pl.BlockSpec((rows, LANE), lambda n: (n, 0)),
                   pl.BlockSpec((8, LANE), lambda n: (0, 0))],
        out_shape=[jax.ShapeDtypeStruct((S, B_QKV), f32), jax.ShapeDtypeStruct((S, LANE), bf16),
                   jax.ShapeDtypeStruct((8, LANE), f32)],
        compiler_params=_cp(("arbitrary",)),
    )(qkvn, qkvn, qkvn, proj, prm, inv, *cots)


def dns_fwd(chunked):
    u = chunked[0]
    S = u.shape[0]
    nc = S // CHUNK

    def body(u_ref, w_ref, qd_ref, kt_ref, at_ref, dc_ref, o_ref, st_ref, st):
        @pl.when(pl.program_id(0) == 0)
        def _():
            st[...] = jnp.zeros_like(st)
        S0 = st[...]
        st_ref[0] = S0
        out, S1 = _dns_f(S0, _heads(u_ref, 1), _heads(w_ref, 1), _heads(qd_ref, 1), _heads(kt_ref, 1),
                         at_ref[0], dc_ref[0, 0:6, :])
        _put_heads(o_ref, out)
        st[...] = S1

    return pl.pallas_call(
        body, name="dn_scan_fwd", grid=(nc,), in_specs=_dnc_out_specs(),
        out_specs=[pl.BlockSpec((CHUNK, B_V), lambda n: (n, 0)),
                   pl.BlockSpec((1, 6, B_DH, B_DH), lambda n: (n, 0, 0, 0))],
        out_shape=[jax.ShapeDtypeStruct((S, B_V), f32), jax.ShapeDtypeStruct((nc, 6, B_DH, B_DH), f32)],
        scratch_shapes=[pltpu.VMEM((6, B_DH, B_DH), f32)],
        compiler_params=_cp(("arbitrary",)),
    )(*chunked)


def dns_bwd(chunked, states, do):
    S = do.shape[0]
    nc = S // CHUNK

    def body(u_ref, w_ref, qd_ref, kt_ref, at_ref, dc_ref, st_ref, do_ref,
             du_ref, dw_ref, dqd_ref, dkt_ref, dat_ref, ddc_ref, dst):
        @pl.when(pl.program_id(0) == 0)
        def _():
            dst[...] = jnp.zeros_like(dst)
        _, vjp = jax.vjp(_dns_f, st_ref[0], _heads(u_ref, 1), _heads(w_ref, 1), _heads(qd_ref, 1), _heads(kt_ref, 1),
                         at_ref[0], dc_ref[0, 0:6, :])
        dS0, du, dw, dqd, dkt, dat, ddc = vjp((_heads(do_ref, 1), dst[...]))
        dst[...] = dS0
        _put_heads(du_ref, du)
        _put_heads(dw_ref, dw)
        _put_heads(dqd_ref, dqd)
        _put_heads(dkt_ref, dkt)
        dat_ref[0] = dat
        ddc_ref[0] = jnp.concatenate([ddc, jnp.zeros((2, LANE), f32)], axis=0)

    return pl.pallas_call(
        body, name="dn_scan_bwd", grid=(nc,),
        in_specs=_dnc_out_specs(nc) + [pl.BlockSpec((1, 6, B_DH, B_DH), lambda n: (nc - 1 - n, 0, 0, 0)),
                                       pl.BlockSpec((CHUNK, B_V), lambda n: (nc - 1 - n, 0))],
        out_specs=_dnc_out_specs(nc), out_shape=_dnc_shapes(S),
        scratch_shapes=[pltpu.VMEM((6, B_DH, B_DH), f32)],
        compiler_params=_cp(("arbitrary",)),
    )(*chunked, states, do)


def dnpost_fwd(o, proj, prm):
    S = o.shape[0]
    t = min(512, S)

    def body(o_ref, z_ref, p_ref, y_ref):
        y_ref[...] = _dnpost_f(o_ref[...], z_ref[...], p_ref[2:3, :]).astype(bf16)

    tok = pl.BlockSpec((t, B_V), lambda i: (i, 0))
    return pl.pallas_call(
        body, name="dn_post_fwd", grid=(S // t,),
        in_specs=[tok, pl.BlockSpec((t, B_V), lambda i: (i, 2)), pl.BlockSpec((8, LANE), lambda i: (0, 0))],
        out_specs=tok, out_shape=jax.ShapeDtypeStruct((S, B_V), bf16), compiler_params=_cp(("parallel",)),
    )(o, proj, prm)


def dnpost_bwd(o, proj, prm, dmix):
    S = o.shape[0]
    t = min(512, S)

    def body(o_ref, z_ref, p_ref, dy_ref, do_ref, dz_ref, dg_ref):
        @pl.when(pl.program_id(0) == 0)
        def _():
            dg_ref[...] = jnp.zeros_like(dg_ref)
        _, vjp = jax.vjp(_dnpost_f, o_ref[...], z_ref[...], p_ref[2:3, :])
        do, dz, dg = vjp(dy_ref[...])
        do_ref[...] = do
        dz_ref[...] = dz.astype(bf16)
        dg_ref[...] += dg

    tok = pl.BlockSpec((t, B_V), lambda i: (i, 0))
    return pl.pallas_call(
        body, name="dn_post_bwd", grid=(S // t,),
        in_specs=[tok, pl.BlockSpec((t, B_V), lambda i: (i, 2)), pl.BlockSpec((8, LANE), lambda i: (0, 0)), tok],
        out_specs=[tok, tok, pl.BlockSpec((1, LANE), lambda i: (0, 0))],
        out_shape=[jax.ShapeDtypeStruct((S, B_V), f32), jax.ShapeDtypeStruct((S, B_V), bf16),
                   jax.ShapeDtypeStruct((1, LANE), f32)],
        compiler_params=_cp(("arbitrary",)),
    )(o, proj, prm, dmix)


N_FF_BLK = D_FF // LANE
GU_SHARD = 2 * D_FF // 4


GLU_ROWS = 256
HALO = 16


def _glu_f(gext, up, w, b):
    c = w[2:3] * gext + w[1:2] * shift_down(gext, 1) + w[0:1] * shift_down(gext, 2) + b
    return _silu(c)[HALO:] * up


def _glu_gext(g_ref, r0, first, T=GLU_ROWS):
    if first:
        return jnp.concatenate([jnp.zeros((HALO, LANE), f32), g_ref[0:T, :].astype(f32)], axis=0)
    return g_ref[pl.ds(r0 - HALO, T + HALO), :].astype(f32)


def glu_fwd(gu, w, b, name):
    S = gu.shape[0]
    T = min(GLU_ROWS, S // 2)

    def body(g_ref, u_ref, w_ref, b_ref, o_ref):
        wv, bv = w_ref[...], b_ref[...]

        def tile(r0, first):
            act = _glu_f(_glu_gext(g_ref, r0, first, T), u_ref[pl.ds(r0, T), :].astype(f32), wv, bv)
            o_ref[pl.ds(r0, T), :] = act.astype(bf16)

        tile(0, True)

        @pl.loop(1, S // T)
        def _(t):
            tile(pl.multiple_of(t * T, T), False)

    col = pl.BlockSpec((S, LANE), lambda j: (0, j))
    return pl.pallas_call(
        body, name=name, grid=(N_FF_BLK,),
        in_specs=[col, pl.BlockSpec((S, LANE), lambda j: (0, N_FF_BLK + j)), pl.BlockSpec((3, LANE), lambda j: (0, j)),
                  pl.BlockSpec((1, LANE), lambda j: (0, j))],
        out_specs=col, out_shape=jax.ShapeDtypeStruct((S, D_FF), bf16), compiler_params=_cp(("parallel",)),
    )(gu, gu, w, b.reshape(1, D_FF))


def glu_bwd(gu, w, b, dact, name):
    S = gu.shape[0]
    T = min(GLU_ROWS, S // 2)

    def body(g_ref, u_ref, w_ref, b_ref, d_ref, dg_ref, dw_ref, db_ref, acc):
        wv, bv = w_ref[...], b_ref[...]

        def tile(r0, first):
            _, vjp = jax.vjp(_glu_f, _glu_gext(g_ref, r0, first, T), u_ref[pl.ds(r0, T), :].astype(f32), wv, bv)
            dgx, du, dw, db = vjp(d_ref[pl.ds(r0, T), :].astype(f32))
            acc[pl.ds(r0, T), :] = dgx[HALO:]
            if not first:
                acc[pl.ds(r0 - HALO, HALO), :] += dgx[:HALO]
            dg_ref[1, pl.ds(r0, T), :] = du.astype(bf16)
            return dw, db

        dw0, db0 = tile(0, True)
        dw_ref[...] = dw0
        db_ref[...] = db0

        @pl.loop(1, S // T)
        def _(t):
            dw, db = tile(pl.multiple_of(t * T, T), False)
            dw_ref[...] += dw
            db_ref[...] += db

        dg_ref[0] = acc[...].astype(bf16)

    col = pl.BlockSpec((S, LANE), lambda j: (0, j))
    wsp = pl.BlockSpec((3, LANE), lambda j: (0, j))
    bsp = pl.BlockSpec((1, LANE), lambda j: (0, j))
    return pl.pallas_call(
        body, name=name, grid=(N_FF_BLK,),
        in_specs=[col, pl.BlockSpec((S, LANE), lambda j: (0, N_FF_BLK + j)), wsp, bsp, col],
        out_specs=[pl.BlockSpec((2, S, LANE), lambda j: (0, 0, j)), wsp, bsp],
        out_shape=[jax.ShapeDtypeStruct((2, S, D_FF), bf16), jax.ShapeDtypeStruct((3, D_FF), f32),
                   jax.ShapeDtypeStruct((1, D_FF), f32)],
        scratch_shapes=[pltpu.VMEM((S, LANE), f32)],
        compiler_params=_cp(("parallel",)),
    )(gu, gu, w, b.reshape(1, D_FF), dact)


def gu_fwd(n2, wg, name):
    S = n2.shape[0]
    tm = min(MM_ROWS, S)

    def body(a_ref, w_ref, o_ref):
        o_ref[...] = _dg(a_ref[...], w_ref[...], 1, 0).astype(bf16)

    return pl.pallas_call(
        body, name=name, grid=(4, S // tm),
        in_specs=[pl.BlockSpec((tm, D), lambda s, m: (m, 0)), pl.BlockSpec((None, D, GU_SHARD), lambda s, m: (s, 0, 0))],
        out_specs=pl.BlockSpec((tm, GU_SHARD), lambda s, m: (m, s)),
        out_shape=jax.ShapeDtypeStruct((S, 2 * D_FF), bf16), compiler_params=_cp(("parallel", "parallel")),
    )(n2, wg)


def gu_bwd_x(dgu, wg, name):
    S = dgu.shape[1]
    tm = min(MM_ROWS, S)

    def body(d_ref, w_ref, o_ref):
        @pl.when(pl.program_id(1) == 0)
        def _():
            o_ref[...] = jnp.zeros_like(o_ref)
        o_ref[...] += _dg(d_ref[...], w_ref[...], 1, 1)

    return pl.pallas_call(
        body, name=name, grid=(S // tm, 4),
        in_specs=[pl.BlockSpec((None, tm, GU_SHARD), lambda m, s: (s // 2, m, s % 2)),
                  pl.BlockSpec((None, D, GU_SHARD), lambda m, s: (s, 0, 0))],
        out_specs=pl.BlockSpec((tm, D), lambda m, s: (m, 0)),
        out_shape=jax.ShapeDtypeStruct((S, D), f32), compiler_params=_cp(("parallel", "arbitrary")),
    )(dgu, wg)


def gu_bwd_w(n2, dgu, name):
    S = n2.shape[0]
    tm = min(MM_ROWS, S)
    nm = S // tm

    def body(a_ref, d_ref, o_ref, acc):
        @pl.when(pl.program_id(1) == 0)
        def _():
            acc[...] = jnp.zeros_like(acc)
        acc[...] += _dg(a_ref[...], d_ref[...], 0, 0)

        @pl.when(pl.program_id(1) == nm - 1)
        def _():
            o_ref[...] = acc[...].astype(bf16)

    return pl.pallas_call(
        body, name=name, grid=(4, nm),
        in_specs=[pl.BlockSpec((tm, D), lambda s, m: (m, 0)),
                  pl.BlockSpec((None, tm, GU_SHARD), lambda s, m: (s // 2, m, s % 2))],
        out_specs=pl.BlockSpec((None, D, GU_SHARD), lambda s, m: (s, 0, 0)),
        out_shape=jax.ShapeDtypeStruct((4, D, GU_SHARD), bf16),
        scratch_shapes=[pltpu.VMEM((D, GU_SHARD), f32)],
        compiler_params=_cp(("parallel", "arbitrary")),
    )(n2, dgu)


def _pair_cols(w):
    lead = w.shape[:-1]
    return w.reshape(lead + (2, 6, A_DH)).swapaxes(-3, -2).reshape(lead + (A_Q,))


def _unpair_cols(w):
    lead = w.shape[:-1]
    return w.reshape(lead + (6, 2, A_DH)).swapaxes(-3, -2).reshape(lead + (A_Q,))


def _lay_in_a(w):
    return jnp.concatenate([_pair_cols(w[:, :A_Q]), w[:, A_Q:]], axis=1)


def _unlay_in_a(w):
    return jnp.concatenate([_unpair_cols(w[:, :A_Q]), w[:, A_Q:]], axis=1)


def _lay_out_a(w):
    return jnp.concatenate([_pair_cols(w[:A_Q].T).T, w[A_Q:]], axis=0)


def _unlay_out_a(w):
    return jnp.concatenate([_unpair_cols(w[:A_Q].T).T, w[A_Q:]], axis=0)


def _lay_in_b(w):
    return jnp.concatenate([w[:, :2304], w[:, 2316:], w[:, 2304:2316],
                            jnp.zeros((w.shape[0], LANE - 12), w.dtype)], axis=1)


def _unlay_in_b(w):
    return jnp.concatenate([w[:, :2304], w[:, 2560:2572], w[:, 2304:2560]], axis=1)


def _chip_cols(w):
    return jnp.moveaxis(w.reshape(w.shape[0], 4, w.shape[1] // 4), 1, 0)


def _unchip_cols(w):
    return jnp.moveaxis(w, 0, 1).reshape(w.shape[1], 4 * w.shape[2])


def _local_step(x, mem, target, P):
    arrive = P.get("arrive", lambda key, after: None)
    ready = P.get("ready", lambda key, grads, dep: dep)
    sk = jnp.zeros((16, LANE), f32).at[:A_HEADS].set(jnp.broadcast_to(P["sinks"][:, None], (A_HEADS, LANE)))
    prm = jnp.zeros((8, LANE), f32).at[0, 6:12].set(P["a_log"]).at[1, 6:12].set(P["dt_bias"]).at[2].set(P["out_norm_g"])
    bias = bias_build(P["rel_bias"])
    saved = []
    h = x
    for i in range(2):
        n1 = rms_fwd(h, P["g_mix"][i], f"rms_mix{i}")
        arrive(("w_in", i), n1)
        proj = mm_nn(n1, P["w_in_a"] if i == 0 else P["w_in_b"], name="proj_a" if i == 0 else "proj_b")
        arrive(("w_mem", i), proj)
        kv = memkv_fwd(mem, P["g_mem"][i], P["w_mem"][i], f"memkv{i}")
        if i == 0:
            self_out = swa_fwd(proj, bias, sk)
            cross = xattn_fwd(proj, A_Q + 2 * LANE, kv, "xattn_a")
            extra = ()
        else:
            qkvn = dnprep_fwd(proj, P["conv_qkv"])
            chunked, inv = dnc_fwd(qkvn, proj, prm)
            o, states = dns_fwd(chunked)
            self_out = dnpost_fwd(o, proj, prm)
            cross = xattn_fwd(proj, 2304, kv, "xattn_b")
            extra = (qkvn, chunked, inv, states, o)
        mix = jnp.concatenate([self_out, cross], axis=1)
        arrive(("w_out", i), cross)
        h2 = mm_nn(mix, P["w_out"][i], res=h, name=f"out_proj{i}")
        n2 = rms_fwd(h2, P["g_ffn"][i], f"rms_ffn{i}")
        arrive(("w_gu", i), n2)
        gu = gu_fwd(n2, P["w_gu"][i], f"gate_up{i}")
        act = glu_fwd(gu, P["ffn_cw"][i], P["ffn_cb"][i], f"glu{i}")
        arrive(("w_down", i), act)
        h3 = mm_nn(act, P["w_down"][i], res=h2, name=f"down{i}")
        saved.append((h, n1, kv, proj, mix, h2, n2, gu, act, extra))
        h = h3

    loss, dh, dg_fin = loss_head(h, P["g_fin"], target)
    G = {"g_fin": dg_fin[0], "g_mix": [None, None], "g_mem": [None, None], "g_ffn": [None, None],
         "w_mem": [None, None], "w_out": [None, None], "w_gu": [None, None], "w_down": [None, None],
         "ffn_cw": [None, None], "ffn_cb": [None, None]}
    for i in (1, 0):
        hin, n1, kv, proj, mix, h2, n2, gu, act, extra = saved[i]
        dact = mm_nt(dh, P["w_down"][i], out_dtype=bf16, name=f"d_act{i}")
        G["w_down"][i] = mm_tn(act, dh, name=f"dw_down{i}")
        dgu, dcw, dcb = glu_bwd(gu, P["ffn_cw"][i], P["ffn_cb"][i], dact, f"glu_bwd{i}")
        G["ffn_cw"][i], G["ffn_cb"][i] = dcw, dcb[0]
        dn2 = gu_bwd_x(dgu, P["w_gu"][i], f"d_n2_{i}")
        G["w_gu"][i] = gu_bwd_w(n2, dgu, f"dw_gu{i}")
        g_ffn = ready(("ffn", i), G, P["g_ffn"][i])
        dh2, dg = rms_bwd(h2, g_ffn, dn2, dh, f"rms_ffn_bwd{i}")
        G["g_ffn"][i] = dg[0]
        dmix = mm_nt(dh2, P["w_out"][i], name=f"d_mix{i}")
        G["w_out"][i] = mm_tn(mix, dh2, name=f"dw_out{i}")
        if i == 0:
            dqkv, dbias, dsk = swa_bwd(proj, bias, sk, dmix)
            dxq, dkv = xattn_bwd(proj, A_Q + 2 * LANE, kv, dmix, "xattn_a_bwd")
            dproj = jnp.concatenate([dqkv, dxq], axis=1)
            G["sinks"] = dsk[:A_HEADS, 0]
            G["rel_bias"] = bias_grad(dbias)[:, :A_HEADS]
            w_in, gname = P["w_in_a"], "w_in_a"
        else:
            qkvn, chunked, inv, states, o = extra
            do, dz, dgo = dnpost_bwd(o, proj, prm, dmix)
            dqkvn, dseg, dprm = dnc_bwd(qkvn, proj, prm, inv, dns_bwd(chunked, states, do))
            draw, dconv = dnprep_bwd(proj, P["conv_qkv"], dqkvn)
            dxq, dkv = xattn_bwd(proj, 2304, kv, dmix, "xattn_b_bwd")
            dproj = jnp.concatenate([draw, dz, dxq, dseg], axis=1)
            G["conv_qkv"] = dconv
            G["a_log"], G["dt_bias"], G["out_norm_g"] = dprm[0, 6:12], dprm[1, 6:12], dgo[0]
            w_in, gname = P["w_in_b"], "w_in_b"
        dn1 = mm_nt(dproj, w_in, name=f"d_n1_{i}")
        G[gname] = mm_tn(n1, dproj, name=f"d{gname}")
        dh, dg = rms_bwd(hin, P["g_mix"][i], dn1, dh2, f"rms_mix_bwd{i}")
        G["g_mix"][i] = dg[0]
        dgm, dwm = memkv_bwd(mem, P["g_mem"][i], P["w_mem"][i], dkv, f"memkv_bwd{i}")
        G["g_mem"][i], G["w_mem"][i] = dgm[0], dwm
        ready(("mix", i), G, None)
    return loss, dh, G


def _prepare(full, w_gu=None):
    return {
        "rel_bias": full["rel_bias"], "sinks": full["sinks_a"][0], "a_log": full["a_log_b"][0],
        "dt_bias": full["dt_bias_b"][0], "out_norm_g": full["out_norm_g_b"][0],
        "g_mix": full["norm_mix_g"], "g_mem": full["norm_mem_g"], "g_ffn": full["norm_ffn_g"],
        "g_fin": full["final_norm_g"], "conv_qkv": full["conv_qkv_b"][0],
        "ffn_cw": [full["ffn_conv_w"][0], full["ffn_conv_w"][1]],
        "ffn_cb": [full["ffn_conv_b"][0], full["ffn_conv_b"][1]],
        "w_mem": [full["w_mem_kv"][0], full["w_mem_kv"][1]],
        "w_out": [_lay_out_a(full["w_out"][0]), full["w_out"][1]],
        "w_in_a": _lay_in_a(full["w_in_a"][0]), "w_in_b": _lay_in_b(full["w_in_b"][0]),
        "w_gu": w_gu if w_gu is not None else [_chip_cols(full["w_gate_up"][0]), _chip_cols(full["w_gate_up"][1])],
        "w_down": [full["w_down"][0], full["w_down"][1]],
    }


def _grads_to_ref(G):
    return {
        "rel_bias": G["rel_bias"], "norm_mix_g": jnp.stack(G["g_mix"]), "norm_mem_g": jnp.stack(G["g_mem"]),
        "w_mem_kv": jnp.stack(G["w_mem"]),
        "w_out": jnp.stack([_unlay_out_a(G["w_out"][0]), G["w_out"][1]]),
        "w_in_a": _unlay_in_a(G["w_in_a"])[None], "sinks_a": G["sinks"][None],
        "w_in_b": _unlay_in_b(G["w_in_b"])[None], "conv_qkv_b": G["conv_qkv"][None],
        "a_log_b": G["a_log"][None], "dt_bias_b": G["dt_bias"][None], "out_norm_g_b": G["out_norm_g"][None],
        "norm_ffn_g": jnp.stack(G["g_ffn"]),
        "w_gate_up": jnp.stack([_unchip_cols(G["w_gu"][0]), _unchip_cols(G["w_gu"][1])]).astype(f32),
        "ffn_conv_w": jnp.stack(G["ffn_cw"]), "ffn_conv_b": jnp.stack(G["ffn_cb"]),
        "w_down": jnp.stack(G["w_down"]), "final_norm_g": G["g_fin"],
    }


ANY = pl.BlockSpec(memory_space=pl.ANY)


def _place():
    return lax.axis_index("x"), lax.axis_index("y"), lax.axis_index("c")


def chip_scatter(gs):
    n = len(gs)

    def body(*refs):
        ins, outs = refs[:n], refs[n:2 * n]
        ssem, rsem = refs[2 * n:]
        x, y, c = _place()
        me = 2 * x + y
        peers = [(1 - x, y), (x, 1 - y), (1 - x, 1 - y)]

        def remote(j, k, slot):
            px, py = peers[k]
            return pltpu.make_async_remote_copy(
                src_ref=ins[j].at[2 * px + py], dst_ref=outs[j].at[slot],
                send_sem=ssem.at[3 * j + k], recv_sem=rsem.at[3 * j + k],
                device_id=(px, py, c), device_id_type=MESH)

        sends = [remote(j, k, me) for j in range(n) for k in range(3)]
        for cp in sends:
            cp.start()
        for j in range(n):
            for k in range(3):
                px, py = peers[k]
                remote(j, k, 2 * px + py).wait_recv()
        for cp in sends:
            cp.wait_send()

    return pl.pallas_call(
        body, name="grad_scatter", in_specs=[ANY] * n, out_specs=[ANY] * n,
        out_shape=[jax.ShapeDtypeStruct(g.shape, g.dtype) for g in gs],
        scratch_shapes=[pltpu.SemaphoreType.DMA((3 * n,)), pltpu.SemaphoreType.DMA((3 * n,))],
    )(*gs)


def allreduce_small(buf):
    R = buf.shape[0]

    def body(b_ref, o_ref, recv, ssem, rsem):
        x, y, c = _place()
        me = 4 * x + 2 * y + c

        def peer(k):
            return (1 - x if k & 4 else x, 1 - y if k & 2 else y, 1 - c if k & 1 else c)

        def remote(k, slot):
            return pltpu.make_async_remote_copy(
                src_ref=b_ref, dst_ref=recv.at[slot], send_sem=ssem.at[k - 1], recv_sem=rsem.at[k - 1],
                device_id=peer(k), device_id_type=MESH)

        sends = [remote(k, me) for k in range(1, 8)]
        for cp in sends:
            cp.start()
        recv[me] = b_ref[...]
        for k in range(1, 8):
            px, py, pc = peer(k)
            remote(k, 4 * px + 2 * py + pc).wait_recv()
        for cp in sends:
            cp.wait_send()
        total = recv[0]
        for j in range(1, 8):
            total = total + recv[j]
        o_ref[...] = total

    return pl.pallas_call(
        body, name="small_allreduce",
        in_specs=[pl.BlockSpec(memory_space=pltpu.VMEM)], out_specs=pl.BlockSpec(memory_space=pltpu.VMEM),
        out_shape=jax.ShapeDtypeStruct(buf.shape, f32),
        scratch_shapes=[pltpu.VMEM((8, R, LANE), f32), pltpu.SemaphoreType.DMA((7,)), pltpu.SemaphoreType.DMA((7,))],
    )(buf)


def sum_slots(own, recv, chip, core, name):
    _, R, C = recv.shape
    tr = _row_tile(R, 256)
    nt = R // tr

    def body(p_ref, a_ref, r_ref, o_ref):
        acc = jnp.zeros((tr, C), f32)
        for s in range(4):
            acc = acc + jnp.where(p_ref[0] == s, a_ref[s], r_ref[s]).astype(f32)
        o_ref[...] = acc

    slots = pl.BlockSpec((4, tr, C), lambda i, p_ref: (0, i, 0))
    return pl.pallas_call(
        body, name=name, out_shape=jax.ShapeDtypeStruct((2 * R, C), f32),
        grid_spec=pltpu.PrefetchScalarGridSpec(
            num_scalar_prefetch=1, grid=(nt,), in_specs=[slots, slots],
            out_specs=pl.BlockSpec((tr, C), lambda i, p_ref: (p_ref[1] * nt + i, 0))),
        compiler_params=_cp(("parallel",)),
    )(jnp.stack([chip, core]).astype(jnp.int32), own, recv)


def _half(ref, core, axis=0):
    half = ref.shape[axis] // 2
    idx = (slice(None),) * axis + (pl.ds(core * half, half),)
    return ref.at[idx]


IN_HBM = pl.BlockSpec(memory_space=pltpu.HBM)
IN_SEM = pl.BlockSpec(memory_space=pltpu.SEMAPHORE)
SIDE_EFFECT = pltpu.SideEffectType.DATAFLOW_SIDE_EFFECTING


def slot_cast(w, layer, chip, name):
    _, R, C = w.shape
    tr = _row_tile(R)

    def body(p_ref, w_ref, o_ref):
        o_ref[...] = w_ref[...].astype(bf16)

    return pl.pallas_call(
        body, name=name, out_shape=jax.ShapeDtypeStruct((4, R, C), bf16),
        grid_spec=pltpu.PrefetchScalarGridSpec(
            num_scalar_prefetch=1, grid=(R // tr,),
            in_specs=[pl.BlockSpec((None, tr, C), lambda i, p_ref: (layer, i, 0))],
            out_specs=pl.BlockSpec((None, tr, C), lambda i, p_ref: (p_ref[0], i, 0))),
        compiler_params=_cp(("parallel",)),
    )(jnp.reshape(chip, (1,)).astype(jnp.int32), w)


def _gather_copy(buf, i, k, ssem, rsem, place, landing):
    x, y, c = place
    px, py = [(1 - x, y), (x, 1 - y), (1 - x, 1 - y)][k]
    me = 2 * x + y
    return pltpu.make_async_remote_copy(
        src_ref=buf.at[me], dst_ref=buf.at[me if landing == "theirs" else 2 * px + py],
        send_sem=ssem.at[3 * i + k], recv_sem=rsem.at[3 * i + k], device_id=(px, py, c), device_id_type=MESH)


def gather_start(groups):
    flat = [b for grp in groups for b in grp]
    n, ng = len(flat), len(groups)

    def body(*refs):
        bufs, sems = refs[:n], refs[n:n + 2 * ng]
        place = _place()
        j = 0
        for g, grp in enumerate(groups):
            for i in range(len(grp)):
                for k in range(3):
                    _gather_copy(bufs[j], i, k, sems[2 * g], sems[2 * g + 1], place, "theirs").start()
                j += 1

    sem_shapes = [pltpu.SemaphoreType.DMA((3 * len(grp),)) for grp in groups for _ in range(2)]
    out = pl.pallas_call(
        body, name="gather_start", in_specs=[IN_HBM] * n, out_specs=(*[IN_SEM] * (2 * ng), *[IN_HBM] * n),
        out_shape=(*sem_shapes, *[pltpu.HBM(b.shape, b.dtype) for b in flat]),
        input_output_aliases={i: 2 * ng + i for i in range(n)},
        compiler_params=pltpu.CompilerParams(has_side_effects=SIDE_EFFECT),
    )(*[pltpu.with_memory_space_constraint(b, pltpu.HBM) for b in flat])
    sems, bufs = out[:2 * ng], list(out[2 * ng:])
    flights, j = [], 0
    for g, grp in enumerate(groups):
        flights.append((bufs[j:j + len(grp)], sems[2 * g], sems[2 * g + 1]))
        j += len(grp)
    return flights


def gather_wait(flight, after, name):
    bufs, ssem, rsem = flight
    n = len(bufs)

    def body(*refs):
        place = _place()
        for i in range(n):
            for k in range(3):
                cp = _gather_copy(refs[i], i, k, refs[n], refs[n + 1], place, "mine")
                cp.wait_send()
                cp.wait_recv()

    return pl.pallas_call(
        body, name=name, in_specs=[IN_HBM] * n + [IN_SEM, IN_SEM, ANY], out_specs=[IN_HBM] * n,
        out_shape=[pltpu.HBM(b.shape, b.dtype) for b in bufs], input_output_aliases={i: i for i in range(n)},
        compiler_params=pltpu.CompilerParams(has_side_effects=SIDE_EFFECT),
    )(*bufs, ssem, rsem, after)


def _scatter_copy(src, land, j, k, ssem, rsem, place, landing):
    x, y, c = place
    px, py = [(1 - x, y), (x, 1 - y), (1 - x, 1 - y)][k]
    return pltpu.make_async_remote_copy(
        src_ref=src.at[2 * px + py], dst_ref=land.at[2 * x + y if landing == "theirs" else 2 * px + py],
        send_sem=ssem.at[3 * j + k], recv_sem=rsem.at[3 * j + k], device_id=(px, py, c), device_id_type=MESH)


def scatter_start(srcs, name):
    n = len(srcs)
    lands = [lax.empty(g.shape, g.dtype) for g in srcs]

    def body(*refs):
        place = _place()
        for j in range(n):
            for k in range(3):
                _scatter_copy(refs[j], refs[n + j], j, k, refs[2 * n], refs[2 * n + 1], place, "theirs").start()
        refs[-1][...] = jnp.zeros_like(refs[-1])

    sem = pltpu.SemaphoreType.DMA((3 * n,))
    hbm = [pltpu.with_memory_space_constraint(b, pltpu.HBM) for b in list(srcs) + lands]
    out = pl.pallas_call(
        body, name=name, in_specs=[IN_HBM] * (2 * n),
        out_specs=(IN_SEM, IN_SEM, *[IN_HBM] * (2 * n), pl.BlockSpec(memory_space=pltpu.VMEM)),
        out_shape=(sem, sem, *[pltpu.HBM(b.shape, b.dtype) for b in hbm], jax.ShapeDtypeStruct((8, LANE), f32)),
        input_output_aliases={i: 2 + i for i in range(2 * n)},
        compiler_params=pltpu.CompilerParams(has_side_effects=SIDE_EFFECT),
    )(*hbm)
    return (list(out[2:2 + n]), list(out[2 + n:2 + 2 * n]), out[0], out[1]), out[-1]


def scatter_wait(flight, after, name):
    srcs, lands, ssem, rsem = flight
    n = len(srcs)

    def body(*refs):
        place = _place()
        for j in range(n):
            for k in range(3):
                cp = _scatter_copy(refs[j], refs[n + j], j, k, refs[2 * n], refs[2 * n + 1], place, "mine")
                cp.wait_send()
                cp.wait_recv()

    out = pl.pallas_call(
        body, name=name, in_specs=[IN_HBM] * (2 * n) + [IN_SEM, IN_SEM, ANY], out_specs=[IN_HBM] * (2 * n),
        out_shape=[pltpu.HBM(b.shape, b.dtype) for b in list(srcs) + list(lands)],
        input_output_aliases={i: i for i in range(2 * n)},
        compiler_params=pltpu.CompilerParams(has_side_effects=SIDE_EFFECT),
    )(*srcs, *lands, ssem, rsem, after)
    return list(out[:n]), list(out[n:])


def pair_exchange(gbufs, name):
    n = len(gbufs)

    def body(*refs):
        ins, outs = refs[:n], refs[n:2 * n]
        ssem, rsem = refs[2 * n:]
        x, y, c = _place()
        cps = [pltpu.make_async_remote_copy(
            src_ref=_half(ins[j], 1 - c, axis=1), dst_ref=outs[j], send_sem=ssem.at[j], recv_sem=rsem.at[j],
            device_id=(x, y, 1 - c), device_id_type=MESH) for j in range(n)]
        for cp in cps:
            cp.start()
        for cp in cps:
            cp.wait()

    return pl.pallas_call(
        body, name=name, in_specs=[ANY] * n, out_specs=[ANY] * n,
        out_shape=[jax.ShapeDtypeStruct((4, g.shape[1] // 2, g.shape[2]), g.dtype) for g in gbufs],
        scratch_shapes=[pltpu.SemaphoreType.DMA((n,)), pltpu.SemaphoreType.DMA((n,))],
    )(*gbufs)


def _row_tile(rows, cap=512):
    return max(t for t in range(16, min(rows, cap) + 1, 16) if rows % t == 0)


def pair_sum(mine, theirs, core, name):
    _, R, C = mine.shape
    half = R // 2
    tr = _row_tile(half)
    nt = half // tr

    def body(c_ref, a_ref, b_ref, o_ref):
        o_ref[...] = (a_ref[...].astype(f32) + b_ref[...].astype(f32)).astype(bf16)

    return pl.pallas_call(
        body, name=name, out_shape=jax.ShapeDtypeStruct(theirs.shape, bf16),
        grid_spec=pltpu.PrefetchScalarGridSpec(
            num_scalar_prefetch=1, grid=(4, nt),
            in_specs=[pl.BlockSpec((None, tr, C), lambda s, i, c_ref: (s, c_ref[0] * nt + i, 0)),
                      pl.BlockSpec((None, tr, C), lambda s, i, c_ref: (s, i, 0))],
            out_specs=pl.BlockSpec((None, tr, C), lambda s, i, c_ref: (s, i, 0))),
        compiler_params=_cp(("parallel", "parallel")),
    )(jnp.reshape(core, (1,)).astype(jnp.int32), mine, theirs)


def final_exchange(fins):
    n = len(fins)

    def body(*refs):
        outs = refs[n:2 * n]
        ssem, rsem = refs[2 * n:]
        x, y, c = _place()
        cps = [pltpu.make_async_remote_copy(
            src_ref=_half(outs[j], c), dst_ref=_half(outs[j], c), send_sem=ssem.at[j], recv_sem=rsem.at[j],
            device_id=(x, y, 1 - c), device_id_type=MESH) for j in range(n)]
        for cp in cps:
            cp.start()
        for cp in cps:
            cp.wait()

    return pl.pallas_call(
        body, name="final_exchange", in_specs=[ANY] * n, out_specs=[ANY] * n,
        out_shape=[jax.ShapeDtypeStruct(f.shape, f.dtype) for f in fins],
        input_output_aliases={j: j for j in range(n)},
        scratch_shapes=[pltpu.SemaphoreType.DMA((n,)), pltpu.SemaphoreType.DMA((n,))],
    )(*fins)


def adamw_big(w, m, v, gs, row0, name):
    L, R, C = w.shape
    tr = _row_tile(math.gcd(R, row0) if row0 else R, max(16, 262144 // C // 16 * 16))
    b0 = row0 // tr

    def body(*refs):
        w_ref, m_ref, v_ref = refs[:3]
        g_refs = refs[3:3 + L]
        g_ref, d_ref, nm_ref, nv_ref = refs[3 + L:]
        g = g_refs[0][...]
        for l in range(1, L):
            g = jnp.where(pl.program_id(0) == l, g_refs[l][...], g)
        d, nm, nv = _adamw_math(w_ref[...], g, m_ref[...], v_ref[...])
        g_ref[...] = g
        d_ref[...] = d
        nm_ref[...] = nm
        nv_ref[...] = nv

    own = pl.BlockSpec((None, tr, C), lambda l, i: (l, i, 0))
    off = pl.BlockSpec((tr, C), lambda l, i: (b0 + i, 0))
    return pl.pallas_call(
        body, name=name, grid=(L, R // tr), in_specs=[own, own, own] + [off] * L, out_specs=[own] * 4,
        out_shape=[jax.ShapeDtypeStruct((L, R, C), f32)] * 4, compiler_params=_cp(("parallel", "parallel")),
    )(w, m, v, *gs)


def _adamw_math(w, g, m, v):
    m = B1 * m + (1.0 - B1) * g
    v = B2 * v + (1.0 - B2) * (g * g)
    m_hat = m / (1.0 - B1 ** STEP)
    v_hat = v / (1.0 - B2 ** STEP)
    delta = -LR * (m_hat / (jnp.sqrt(v_hat) + AEPS) + WD * w)
    return delta, m, v


def adamw_small(w, m, v, g):
    def body(w_ref, m_ref, v_ref, g_ref, d_ref, nm_ref, nv_ref):
        d, nm, nv = _adamw_math(w_ref[...], g_ref[...], m_ref[...], v_ref[...])
        d_ref[...] = d
        nm_ref[...] = nm
        nv_ref[...] = nv

    return pl.pallas_call(body, name="adamw_small", out_shape=[jax.ShapeDtypeStruct(w.shape, f32)] * 3)(w, m, v, g)


CONV =(("conv_qkv_b", 2), ("ffn_conv_w", 2))
SMALL = ("rel_bias", "norm_mix_g", "norm_mem_g", "sinks_a", "a_log_b", "dt_bias_b", "out_norm_g_b", "norm_ffn_g",
         "ffn_conv_b", "final_norm_g")
WEIGHTS = ("rel_bias", "norm_mix_g", "norm_mem_g", "w_mem_kv", "w_out", "w_in_a", "sinks_a", "w_in_b", "conv_qkv_b",
           "a_log_b", "dt_bias_b", "out_norm_g_b", "norm_ffn_g", "w_gate_up", "ffn_conv_w", "ffn_conv_b", "w_down",
           "final_norm_g")
ARGS = ("x", "mem") + WEIGHTS + ("loss_target",) + tuple("m_" + n for n in WEIGHTS) + tuple("v_" + n for n in WEIGHTS)


def _rows(a, width):
    flat = a.reshape(-1)
    pad = (-flat.shape[0]) % (8 * width)
    if pad:
        flat = jnp.concatenate([flat, jnp.zeros((pad,), a.dtype)])
    return flat.reshape(-1, width)


def _nrows(shape, width):
    return _pad_to(-(-math.prod(shape) // width), 8)


def _pack(arrs, width, total_rows, dtype):
    parts = [_rows(a.astype(dtype), width) for a in arrs]
    used = sum(p.shape[0] for p in parts)
    if total_rows > used:
        parts.append(jnp.zeros((total_rows - used, width), dtype))
    return jnp.concatenate(parts, axis=0)


def _unpack(buf, shapes, width):
    out, r = [], 0
    for s in shapes:
        n = _nrows(s, width)
        out.append(buf[r:r + n].reshape(-1)[:math.prod(s)].reshape(s))
        r += n
    return out


def _pad_to(n, mult):
    return -(-n // mult) * mult


def kernel(x, mem, rel_bias, norm_mix_g, norm_mem_g, w_mem_kv, w_out, w_in_a, sinks_a, w_in_b, conv_qkv_b, a_log_b, dt_bias_b, out_norm_g_b, norm_ffn_g, w_gate_up, ffn_conv_w, ffn_conv_b, w_down, final_norm_g, loss_target, m_rel_bias, m_norm_mix_g, m_norm_mem_g, m_w_mem_kv, m_w_out, m_w_in_a, m_sinks_a, m_w_in_b, m_conv_qkv_b, m_a_log_b, m_dt_bias_b, m_out_norm_g_b, m_norm_ffn_g, m_w_gate_up, m_ffn_conv_w, m_ffn_conv_b, m_w_down, m_final_norm_g, v_rel_bias, v_norm_mix_g, v_norm_mem_g, v_w_mem_kv, v_w_out, v_w_in_a, v_sinks_a, v_w_in_b, v_conv_qkv_b, v_a_log_b, v_dt_bias_b, v_out_norm_g_b, v_norm_ffn_g, v_w_gate_up, v_ffn_conv_w, v_ffn_conv_b, v_w_down, v_final_norm_g):
    A = dict(zip(ARGS, (x, mem, rel_bias, norm_mix_g, norm_mem_g, w_mem_kv, w_out, w_in_a, sinks_a, w_in_b, conv_qkv_b, a_log_b, dt_bias_b, out_norm_g_b, norm_ffn_g, w_gate_up, ffn_conv_w, ffn_conv_b, w_down, final_norm_g, loss_target, m_rel_bias, m_norm_mix_g, m_norm_mem_g, m_w_mem_kv, m_w_out, m_w_in_a, m_sinks_a, m_w_in_b, m_conv_qkv_b, m_a_log_b, m_dt_bias_b, m_out_norm_g_b, m_norm_ffn_g, m_w_gate_up, m_ffn_conv_w, m_ffn_conv_b, m_w_down, m_final_norm_g, v_rel_bias, v_norm_mix_g, v_norm_mem_g, v_w_mem_kv, v_w_out, v_w_in_a, v_sinks_a, v_w_in_b, v_conv_qkv_b, v_a_log_b, v_dt_bias_b, v_out_norm_g_b, v_norm_ffn_g, v_w_gate_up, v_ffn_conv_w, v_ffn_conv_b, v_w_down, v_final_norm_g)))
    chip = 2 * lax.axis_index("x") + lax.axis_index("y")
    core = lax.axis_index("c")
    n_down, n_out, n_mem = w_down.shape[1], w_out.shape[1], w_mem_kv.shape[1]

    def own_slot(shard):
        return lax.dynamic_update_index_in_dim(lax.empty((4,) + shard.shape, shard.dtype), shard, chip, 0)

    def bslot(w, i, tag):
        return slot_cast(w, i, chip, "slot_cast_" + tag)

    groups = {
        ("w_in", 0): [bslot(w_in_a, 0, "in_a")],
        ("w_mem", 0): [bslot(w_mem_kv, 0, "mem0"), bslot(w_mem_kv, 1, "mem1"), own_slot(conv_qkv_b[0]),
                       own_slot(ffn_conv_w.reshape(6, -1))],
        ("w_out", 0): [bslot(w_out, 0, "out0")], ("w_gu", 0): [bslot(w_gate_up, 0, "gu0")],
        ("w_down", 0): [bslot(w_down, 0, "down0")],
        ("w_in", 1): [bslot(w_in_b, 0, "in_b")],
        ("w_out", 1): [bslot(w_out, 1, "out1")], ("w_gu", 1): [bslot(w_gate_up, 1, "gu1")],
        ("w_down", 1): [bslot(w_down, 1, "down1")],
    }
    flights = dict(zip(groups, gather_start(list(groups.values()))))
    P = {"rel_bias": rel_bias, "sinks": sinks_a[0], "a_log": a_log_b[0], "dt_bias": dt_bias_b[0],
         "out_norm_g": out_norm_g_b[0], "g_mix": norm_mix_g, "g_mem": norm_mem_g, "g_ffn": norm_ffn_g,
         "g_fin": final_norm_g, "ffn_cb": [ffn_conv_b[0], ffn_conv_b[1]], "w_mem": [None, None], "w_out": [None, None],
         "w_gu": [None, None], "w_down": [None, None], "ffn_cw": [None, None]}

    def rows4(g):
        return g.reshape(4 * g.shape[1], g.shape[2])

    def arrive(key, after):
        if key not in flights:
            return
        got = gather_wait(flights.pop(key), after, "gather_wait_%s%d" % key)
        name, i = key
        if name == "w_in":
            P["w_in_a" if i == 0 else "w_in_b"] = (_lay_in_a if i == 0 else _lay_in_b)(_unchip_cols(got[0]))
        elif name == "w_mem":
            P["w_mem"] = [rows4(got[0]), rows4(got[1])]
            P["conv_qkv"] = _unchip_cols(got[2])
            cw = _unchip_cols(got[3]).reshape(2, 3, D_FF)
            P["ffn_cw"] = [cw[0], cw[1]]
        elif name == "w_out":
            P["w_out"][i] = _lay_out_a(rows4(got[0])) if i == 0 else rows4(got[0])
        elif name == "w_gu":
            P["w_gu"][i] = got[0]
        else:
            P["w_down"][i] = rows4(got[0])

    def chip_rows(g):
        return g.reshape(4, g.shape[0] // 4, g.shape[-1])

    sent, started = {}, []

    def ready(key, G, dep):
        kind, i = key
        tag = "%s%d" % key
        if kind == "ffn":
            names, partial = ("gu", "down"), [G["w_gu"][i], chip_rows(G["w_down"][i]).astype(bf16)]
        else:
            g_out = _unlay_out_a(G["w_out"][0]) if i == 0 else G["w_out"][1]
            g_in = _unlay_in_a(G["w_in_a"]) if i == 0 else _unlay_in_b(G["w_in_b"])
            names = ("out", "in", "mem")
            partial = [chip_rows(g_out).astype(bf16), _chip_cols(g_in).astype(bf16), chip_rows(G["w_mem"][i]).astype(bf16)]
        theirs = pair_exchange(partial, "pair_exchange_" + tag)
        pair = [pair_sum(p, t, core, "pair_sum_%s%d" % (nm, i)) for p, t, nm in zip(partial, theirs, names)]
        if key == ("mix", 0):
            sent[key] = (names, pair, chip_scatter(pair))
            return dep
        flight, token = scatter_start(pair, "scatter_start_" + tag)
        sent[key] = (names, flight)
        started.append(token[0, 0])
        if dep is not None:
            while started:
                dep = dep + started.pop()
        return dep

    P["arrive"], P["ready"] = arrive, ready

    loss, dx, G = _local_step(x[0], mem[0], loss_target[0], P)
    gfull = _grads_to_ref(G)

    fin = {}
    for key in (("ffn", 1), ("mix", 1), ("ffn", 0), ("mix", 0)):
        if key == ("mix", 0):
            names, pair, arrived = sent[key]
        else:
            names, flight = sent[key]
            pair, arrived = scatter_wait(flight, dx, "scatter_wait_%s%d" % key)
        for nm, p, r in zip(names, pair, arrived):
            fin[nm, key[1]] = sum_slots(p, r, chip, core, "sum_slots_%s%d" % (nm, key[1]))
    order = list(fin)
    done = dict(zip(order, final_exchange([fin[k] for k in order])))

    sm_shapes = [A[n].shape for n in SMALL] + [gfull[n].shape for n, _ in CONV] + [(LANE,)]
    sm_rows = _pad_to(sum(_nrows(s, LANE) for s in sm_shapes), 8)
    sbuf = _pack([gfull[n] for n in SMALL] + [gfull[n] for n, _ in CONV] + [loss[0]], LANE, sm_rows, f32)
    tot = _unpack(allreduce_small(sbuf), sm_shapes, LANE)
    gsmall = dict(zip(SMALL, tot[:len(SMALL)]))
    for (n, axis), t in zip(CONV, tot[len(SMALL):len(SMALL) + len(CONV)]):
        sh = A[n].shape[axis]
        gsmall[n] = lax.dynamic_slice_in_dim(t, chip * sh, sh, axis)
    loss_out = tot[-1][0]

    out = {}
    plan = (("w_gate_up", [done["gu", 0], done["gu", 1]]), ("w_down", [done["down", 0], done["down", 1]]),
            ("w_out", [done["out", 0], done["out", 1]]), ("w_mem_kv", [done["mem", 0], done["mem", 1]]),
            ("w_in_a", [done["in", 0]]), ("w_in_b", [done["in", 1]]))
    for n, gs in plan:
        shape3 = (len(gs),) + gs[0].shape
        res = adamw_big(A[n].reshape(shape3), A["m_" + n].reshape(shape3), A["v_" + n].reshape(shape3), gs, 0,
                        "adamw_" + n)
        for key, r in zip(("grad_", "delta_", "new_m_", "new_v_"), res):
            out[key + n] = r.reshape(A[n].shape)
    names = SMALL + tuple(n for n, _ in CONV)
    shapes = [A[n].shape for n in names]
    rows = _pad_to(sum(_nrows(s, LANE) for s in shapes), 8)
    packs = [_pack([src[n] for n in names], LANE, rows, f32)
             for src in ({n: A[n] for n in names}, {n: A["m_" + n] for n in names}, {n: A["v_" + n] for n in names}, gsmall)]
    res = adamw_small(*packs)
    for key, r in zip(("delta_", "new_m_", "new_v_"), res):
        for n, a in zip(names, _unpack(r, shapes, LANE)):
            out[key + n] = a
    for n in names:
        out["grad_" + n] = gsmall[n]
    return (loss_out, dx[None], *[out["grad_" + n] for n in WEIGHTS], *[out["delta_" + n] for n in WEIGHTS],
            *[out["new_m_" + n] for n in WEIGHTS], *[out["new_v_" + n] for n in WEIGHTS])
```

```python
import functools
import math

import numpy as np
import jax
import jax.numpy as jnp
from jax import lax
from jax.experimental import pallas as pl
from jax.experimental.pallas import tpu as pltpu

f32 = jnp.float32
bf16 = jnp.bfloat16
HI = lax.Precision.HIGHEST
MESH = pl.DeviceIdType.MESH

D = 1024
MEM_LEN = 256
EPS = 1e-6
A_HEADS, A_KV, A_DH = 12, 2, 64
A_Q = 768
BLK = 128
N_BUCKETS, MAX_DIST = 32, 128
B_QK, B_V, B_DH = 384, 768, 128
B_QKV = 1536
CHUNK = 64
X_Q = 256
D_FF = 2816
IN_A = 1280
IN_B = 2572
IN_B_PAD = 2688
LANE = 128
VMEM_LIMIT = 56 * 1024 * 1024
MM_ROWS = 1024

LR, B1, B2, AEPS, WD, STEP = 0.001, 0.9, 0.999, 1e-08, 0.01, 10


def _cp(sem=None):
    return pltpu.CompilerParams(dimension_semantics=sem, vmem_limit_bytes=VMEM_LIMIT)


def _dg(a, b, ca, cb, prec=None):
    return lax.dot_general(a, b, (((ca,), (cb,)), ((), ())), precision=prec, preferred_element_type=f32)


@jax.custom_vjp
def bdot(a, b):
    return _dg(a.astype(bf16), b.astype(bf16), 1, 0)


def _bdot_f(a, b):
    return bdot(a, b), (a, b)


def _bdot_b(res, g):
    a, b = res
    gb = g.astype(bf16)
    return _dg(gb, b.astype(bf16), 1, 1), _dg(a.astype(bf16), gb, 0, 0)


bdot.defvjp(_bdot_f, _bdot_b)


@jax.custom_vjp
def bdot_nt(a, b):
    return _dg(a.astype(bf16), b.astype(bf16), 1, 1)


def _bdot_nt_f(a, b):
    return bdot_nt(a, b), (a, b)


def _bdot_nt_b(res, g):
    a, b = res
    gb = g.astype(bf16)
    return _dg(gb, b.astype(bf16), 1, 0), _dg(gb, a.astype(bf16), 0, 0)


bdot_nt.defvjp(_bdot_nt_f, _bdot_nt_b)


def _shift_rows(x, s, down):
    n = x.shape[0]
    row = lax.broadcasted_iota(jnp.int32, x.shape, 0)
    if down:
        return jnp.where(row >= s, pltpu.roll(x, s, 0), 0.0)
    return jnp.where(row < n - s, pltpu.roll(x, n - s, 0), 0.0)


@functools.partial(jax.custom_vjp, nondiff_argnums=(1,))
def shift_down(x, s):
    return _shift_rows(x, s, True)


def _sd_f(x, s):
    return _shift_rows(x, s, True), None


def _sd_b(s, _, g):
    return (_shift_rows(g, s, False),)


shift_down.defvjp(_sd_f, _sd_b)


def _sigmoid(x):
    return 1.0 / (1.0 + jnp.exp(-x))


def _silu(x):
    return x * _sigmoid(x)


def _rms(x, g):
    return x * lax.rsqrt(jnp.mean(x * x, axis=-1, keepdims=True) + EPS) * g


def _tile(n, cap):
    u = n // LANE
    best = 1
    for d in range(1, u + 1):
        if u % d == 0 and d * LANE <= cap:
            best = d
    return best * LANE


def mm_nn(a, w, res=None, out_dtype=f32, name="mm_nn"):
    M, K = a.shape
    N = w.shape[1]
    tm, tn = min(MM_ROWS, M), _tile(N, 1024)

    def body(*refs):
        if res is None:
            a_ref, w_ref, o_ref = refs
            o_ref[...] = _dg(a_ref[...].astype(bf16), w_ref[...], 1, 0).astype(out_dtype)
        else:
            a_ref, w_ref, r_ref, o_ref = refs
            o_ref[...] = (r_ref[...] + _dg(a_ref[...].astype(bf16), w_ref[...], 1, 0)).astype(out_dtype)

    in_specs = [pl.BlockSpec((tm, K), lambda n, m: (m, 0)), pl.BlockSpec((K, tn), lambda n, m: (0, n))]
    args = [a, w]
    if res is not None:
        in_specs.append(pl.BlockSpec((tm, tn), lambda n, m: (m, n)))
        args.append(res)
    return pl.pallas_call(
        body, name=name, grid=(N // tn, M // tm), in_specs=in_specs,
        out_specs=pl.BlockSpec((tm, tn), lambda n, m: (m, n)),
        out_shape=jax.ShapeDtypeStruct((M, N), out_dtype),
        compiler_params=_cp(("parallel", "parallel")),
    )(*args)


def mm_nt(dy, w, out_dtype=f32, name="mm_nt"):
    M, N = dy.shape
    K = w.shape[0]
    tm, tn = min(MM_ROWS, M), _tile(N, 1024)
    assert out_dtype == f32 or tn == N

    def body(dy_ref, w_ref, o_ref):
        part = _dg(dy_ref[...].astype(bf16), w_ref[...], 1, 1)
        if tn == N:
            o_ref[...] = part.astype(out_dtype)
        else:
            @pl.when(pl.program_id(1) == 0)
            def _():
                o_ref[...] = jnp.zeros_like(o_ref)
            o_ref[...] += part

    return pl.pallas_call(
        body, name=name, grid=(M // tm, N // tn),
        in_specs=[pl.BlockSpec((tm, tn), lambda m, n: (m, n)), pl.BlockSpec((K, tn), lambda m, n: (0, n))],
        out_specs=pl.BlockSpec((tm, K), lambda m, n: (m, 0)),
        out_shape=jax.ShapeDtypeStruct((M, K), out_dtype),
        compiler_params=_cp(("parallel", "arbitrary")),
    )(dy, w)


NORM_ROWS = 512


def _acc_then_norm_bwd(part, steps, h_ref, g_ref, r_ref, o_ref, dg_ref):
    k = pl.program_id(1)

    @pl.when((pl.program_id(0) == 0) & (k == 0))
    def _():
        dg_ref[...] = jnp.zeros_like(dg_ref)

    @pl.when(k == 0)
    def _():
        o_ref[...] = part

    @pl.when(k > 0)
    def _():
        o_ref[...] += part

    @pl.when(k == steps - 1)
    def _():
        _, vjp = jax.vjp(_rms, h_ref[...], g_ref[...])
        dh, dg = vjp(o_ref[...])
        o_ref[...] = r_ref[...] + dh
        dg_ref[...] += dg


def mm_nt_norm(dy, w, norm, name):
    M, N = dy.shape
    tm, tn = min(NORM_ROWS, M), _tile(N, 1024)

    def body(dy_ref, w_ref, h_ref, g_ref, r_ref, o_ref, dg_ref):
        _acc_then_norm_bwd(_dg(dy_ref[...].astype(bf16), w_ref[...], 1, 1), N // tn, h_ref, g_ref, r_ref, o_ref, dg_ref)

    tok = pl.BlockSpec((tm, D), lambda m, n: (m, 0))
    vec = pl.BlockSpec((1, D), lambda m, n: (0, 0))
    return pl.pallas_call(
        body, name=name, grid=(M // tm, N // tn),
        in_specs=[pl.BlockSpec((tm, tn), lambda m, n: (m, n)), pl.BlockSpec((D, tn), lambda m, n: (0, n)), tok, vec, tok],
        out_specs=[tok, vec],
        out_shape=[jax.ShapeDtypeStruct((M, D), f32), jax.ShapeDtypeStruct((1, D), f32)],
        compiler_params=_cp(("arbitrary", "arbitrary")),
    )(dy, w, norm[0], norm[1].reshape(1, D), norm[2])


def mm_tn(a, dy, name="mm_tn"):
    M, K = a.shape
    N = dy.shape[1]
    tm, tk, tn = min(MM_ROWS, M), _tile(K, 1408), _tile(N, 1024)

    def body(a_ref, dy_ref, o_ref):
        @pl.when(pl.program_id(2) == 0)
        def _():
            o_ref[...] = jnp.zeros_like(o_ref)
        o_ref[...] += _dg(a_ref[...].astype(bf16), dy_ref[...].astype(bf16), 0, 0)

    return pl.pallas_call(
        body, name=name, grid=(K // tk, N // tn, M // tm),
        in_specs=[pl.BlockSpec((tm, tk), lambda k, n, m: (m, k)), pl.BlockSpec((tm, tn), lambda k, n, m: (m, n))],
        out_specs=pl.BlockSpec((tk, tn), lambda k, n, m: (k, n)),
        out_shape=jax.ShapeDtypeStruct((K, N), f32),
        compiler_params=_cp(("parallel", "parallel", "arbitrary")),
    )(a, dy)


def rms_fwd(h, g, name):
    S = h.shape[0]
    t = min(512, S)

    def body(h_ref, g_ref, o_ref):
        o_ref[...] = _rms(h_ref[...], g_ref[...]).astype(bf16)

    return pl.pallas_call(
        body, name=name, grid=(S // t,),
        in_specs=[pl.BlockSpec((t, D), lambda i: (i, 0)), pl.BlockSpec((1, D), lambda i: (0, 0))],
        out_specs=pl.BlockSpec((t, D), lambda i: (i, 0)),
        out_shape=jax.ShapeDtypeStruct((S, D), bf16),
        compiler_params=_cp(("parallel",)),
    )(h, g.reshape(1, D))


def loss_head(h, g, target):
    S = h.shape[0]
    t = min(512, S)

    def f(hh, gg, tt):
        err = _rms(hh, gg) - tt
        return 0.5 * jnp.sum(jnp.mean(err * err, axis=-1, keepdims=True), axis=0, keepdims=True)

    def body(h_ref, g_ref, t_ref, loss_ref, dh_ref, dg_ref):
        @pl.when(pl.program_id(0) == 0)
        def _():
            dg_ref[...] = jnp.zeros_like(dg_ref)
            loss_ref[...] = jnp.zeros_like(loss_ref)
        val, vjp = jax.vjp(lambda a, b: f(a, b, t_ref[...]), h_ref[...], g_ref[...])
        dh, dg = vjp(jnp.ones((1, 1), f32))
        dh_ref[...] = dh
        dg_ref[...] += dg
        loss_ref[...] += jnp.broadcast_to(val, loss_ref.shape)

    tok = pl.BlockSpec((t, D), lambda i: (i, 0))
    vec = pl.BlockSpec((1, D), lambda i: (0, 0))
    return pl.pallas_call(
        body, name="loss_head", grid=(S // t,), in_specs=[tok, vec, tok],
        out_specs=[pl.BlockSpec((1, LANE), lambda i: (0, 0)), tok, vec],
        out_shape=[jax.ShapeDtypeStruct((1, LANE), f32), jax.ShapeDtypeStruct((S, D), f32),
                   jax.ShapeDtypeStruct((1, D), f32)],
        compiler_params=_cp(("arbitrary",)),
    )(h, g.reshape(1, D), target)


def memkv_fwd(mem, g, w, name):
    def body(m_ref, g_ref, w_ref, o_ref):
        o_ref[...] = _dg(_rms(m_ref[...], g_ref[...]).astype(bf16), w_ref[...], 1, 0)

    return pl.pallas_call(
        body, name=name, out_shape=jax.ShapeDtypeStruct((MEM_LEN, 2 * X_Q), f32), compiler_params=_cp(),
    )(mem, g.reshape(1, D), w)


def memkv_bwd(mem, g, w, dkv, name):
    def body(m_ref, g_ref, w_ref, d_ref, dg_ref, dw_ref):
        n, vjp = jax.vjp(lambda gg: _rms(m_ref[...], gg), g_ref[...])
        db = d_ref[...].astype(bf16)
        dw_ref[...] = _dg(n.astype(bf16), db, 0, 0)
        dg_ref[...] = vjp(_dg(db, w_ref[...], 1, 1))[0]

    return pl.pallas_call(
        body, name=name,
        out_shape=[jax.ShapeDtypeStruct((1, D), f32), jax.ShapeDtypeStruct((D, 2 * X_Q), f32)],
        compiler_params=_cp(),
    )(mem, g.reshape(1, D), w, dkv)


def _xattn_f(xq, mk, mv):
    lane = lax.broadcasted_iota(jnp.int32, (1, X_Q), 1)
    out = jnp.zeros(xq.shape, f32)
    for hd in range(4):
        msk = (lane // 64 == hd).astype(f32)
        s = bdot_nt(xq * msk, mk) * (64 ** -0.5)
        m = lax.stop_gradient(jnp.max(s, axis=-1, keepdims=True))
        p = jnp.exp(s - m)
        p = p / jnp.sum(p, axis=-1, keepdims=True)
        out = out + bdot(p, mv * msk)
    return out


def xattn_fwd(proj, col, kv, name):
    S = proj.shape[0]
    t = min(512, S)
    cb = col // X_Q

    def body(q_ref, k_ref, v_ref, o_ref):
        o_ref[...] = _xattn_f(q_ref[...], k_ref[...], v_ref[...]).astype(bf16)

    return pl.pallas_call(
        body, name=name, grid=(S // t,),
        in_specs=[pl.BlockSpec((t, X_Q), lambda i: (i, cb)), pl.BlockSpec((MEM_LEN, X_Q), lambda i: (0, 0)),
                  pl.BlockSpec((MEM_LEN, X_Q), lambda i: (0, 1))],
        out_specs=pl.BlockSpec((t, X_Q), lambda i: (i, 0)),
        out_shape=jax.ShapeDtypeStruct((S, X_Q), bf16),
        compiler_params=_cp(("parallel",)),
    )(proj, kv, kv)


def xattn_bwd(proj, col, kv, dmix, name):
    S = proj.shape[0]
    t = min(512, S)
    cb = col // X_Q

    def body(q_ref, k_ref, v_ref, do_ref, dq_ref, dk_ref, dv_ref):
        @pl.when(pl.program_id(0) == 0)
        def _():
            dk_ref[...] = jnp.zeros_like(dk_ref)
            dv_ref[...] = jnp.zeros_like(dv_ref)
        _, vjp = jax.vjp(_xattn_f, q_ref[...], k_ref[...], v_ref[...])
        dq, dk, dv = vjp(do_ref[...])
        dq_ref[...] = dq.astype(bf16)
        dk_ref[...] += dk
        dv_ref[...] += dv

    kvb = pl.BlockSpec((MEM_LEN, X_Q), lambda i: (0, 0))
    dq, dk, dv = pl.pallas_call(
        body, name=name, grid=(S // t,),
        in_specs=[pl.BlockSpec((t, X_Q), lambda i: (i, cb)), kvb,
                  pl.BlockSpec((MEM_LEN, X_Q), lambda i: (0, 1)), pl.BlockSpec((t, X_Q), lambda i: (i, 3))],
        out_specs=[pl.BlockSpec((t, X_Q), lambda i: (i, 0)), kvb, kvb],
        out_shape=[jax.ShapeDtypeStruct((S, X_Q), bf16), jax.ShapeDtypeStruct((MEM_LEN, X_Q), f32),
                   jax.ShapeDtypeStruct((MEM_LEN, X_Q), f32)],
        compiler_params=_cp(("arbitrary",)),
    )(proj, kv, kv, dmix)
    return dq, jnp.concatenate([dk, dv], axis=1)


def _bucket_map():
    qi = np.arange(BLK)[:, None]
    kj = np.arange(2 * BLK)[None, :]
    n = np.maximum(BLK + qi - kj, 0)
    max_exact = N_BUCKETS // 2
    nf = np.maximum(n, 1).astype(np.float64)
    large = max_exact + (np.log(nf / max_exact) / math.log(MAX_DIST / max_exact)
                         * (N_BUCKETS - max_exact)).astype(np.int32)
    large = np.minimum(large, N_BUCKETS - 1)
    return np.where(n < max_exact, n, large).astype(np.int32)


def bias_build(rel_bias):
    def body(rb_ref, bk_ref, o_ref):
        bk = bk_ref[...]
        for h in range(A_HEADS):
            acc = jnp.zeros((BLK, 2 * BLK), f32)
            for b in range(N_BUCKETS):
                acc = jnp.where(bk == b, rb_ref[b, h], acc)
            o_ref[h] = acc

    return pl.pallas_call(
        body, name="bias_build",
        in_specs=[pl.BlockSpec(memory_space=pltpu.SMEM), pl.BlockSpec(memory_space=pltpu.VMEM)],
        out_specs=pl.BlockSpec(memory_space=pltpu.VMEM),
        out_shape=jax.ShapeDtypeStruct((A_HEADS, BLK, 2 * BLK), f32), compiler_params=_cp(),
    )(rel_bias, jnp.asarray(_bucket_map()))


def bias_grad(dbias):
    def body(d_ref, bk_ref, o_ref):
        bk = bk_ref[...]
        row = lax.broadcasted_iota(jnp.int32, (N_BUCKETS, LANE), 0)
        lane = lax.broadcasted_iota(jnp.int32, (N_BUCKETS, LANE), 1)
        acc = jnp.zeros((N_BUCKETS, LANE), f32)
        for h in range(A_HEADS):
            d = d_ref[h]
            for b in range(N_BUCKETS):
                s = jnp.sum(jnp.where(bk == b, d, 0.0), keepdims=True)
                acc = acc + jnp.where((row == b) & (lane == h), s, 0.0)
        o_ref[...] = acc

    return pl.pallas_call(
        body, name="bias_grad", out_shape=jax.ShapeDtypeStruct((N_BUCKETS, LANE), f32), compiler_params=_cp(),
    )(dbias, jnp.asarray(_bucket_map()))


def _swa_f(qb, kp, kc, vp, vc, bias, sk, first):
    kband = jnp.concatenate([kp, kc], axis=0)
    vband = jnp.concatenate([vp, vc], axis=0)
    qi = lax.broadcasted_iota(jnp.int32, (BLK, 2 * BLK), 0)
    kj = lax.broadcasted_iota(jnp.int32, (BLK, 2 * BLK), 1)
    rel = kj - qi
    ok = (rel >= 1) & (rel <= BLK) & ((kj >= BLK) | jnp.logical_not(first))
    lane = lax.broadcasted_iota(jnp.int32, (1, LANE), 1)
    lane_b = lax.broadcasted_iota(jnp.int32, (BLK, LANE), 1)
    outs = []
    for p in range(A_HEADS // 2):
        qp = qb[:, LANE * p:LANE * (p + 1)]
        acc = jnp.zeros((BLK, LANE), f32)
        for g in range(2):
            h = g * (A_HEADS // 2) + p
            msk = (lane // A_DH == g).astype(f32)
            s = bdot_nt(qp * msk, kband) * (A_DH ** -0.5) + bias[h]
            s = jnp.where(ok, s, -1e30)
            skb = jnp.broadcast_to(sk[h:h + 1, :], (BLK, LANE))
            sink = jnp.sum(jnp.where(lane_b == 0, skb, 0.0), axis=-1, keepdims=True)
            m = lax.stop_gradient(jnp.maximum(jnp.max(s, axis=-1, keepdims=True), sink))
            e = jnp.exp(s - m)
            prob = e / (jnp.sum(e, axis=-1, keepdims=True) + jnp.exp(sink - m))
            acc = acc + bdot(prob, vband) * msk
        outs.append(acc)
    return jnp.concatenate(outs, axis=1)


def _swa_specs(nb, rev):
    bi = (lambda i: nb - 1 - i) if rev else (lambda i: i)
    return [
        pl.BlockSpec((BLK, A_Q), lambda i: (bi(i), 0)),
        pl.BlockSpec((BLK, LANE), lambda i: (jnp.maximum(bi(i) - 1, 0), 6)),
        pl.BlockSpec((BLK, LANE), lambda i: (bi(i), 6)),
        pl.BlockSpec((BLK, LANE), lambda i: (jnp.maximum(bi(i) - 1, 0), 7)),
        pl.BlockSpec((BLK, LANE), lambda i: (bi(i), 7)),
        pl.BlockSpec((A_HEADS, BLK, 2 * BLK), lambda i: (0, 0, 0)),
        pl.BlockSpec((16, LANE), lambda i: (0, 0)),
    ]


def swa_fwd(proj, bias, sk):
    S = proj.shape[0]
    nb = S // BLK

    def body(q_ref, kp_ref, kc_ref, vp_ref, vc_ref, b_ref, s_ref, o_ref):
        o_ref[...] = _swa_f(q_ref[...], kp_ref[...], kc_ref[...], vp_ref[...], vc_ref[...], b_ref[...], s_ref[...],
                            pl.program_id(0) == 0).astype(bf16)

    return pl.pallas_call(
        body, name="swa_fwd", grid=(nb,), in_specs=_swa_specs(nb, False),
        out_specs=pl.BlockSpec((BLK, A_Q), lambda i: (i, 0)),
        out_shape=jax.ShapeDtypeStruct((S, A_Q), bf16), compiler_params=_cp(("parallel",)),
    )(proj, proj, proj, proj, proj, bias, sk)


def swa_bwd(proj, bias, sk, dmix):
    S = proj.shape[0]
    nb = S // BLK

    def body(q_ref, kp_ref, kc_ref, vp_ref, vc_ref, b_ref, s_ref, do_ref, dqkv_ref, db_ref, ds_ref, ck, cv):
        i = pl.program_id(0)

        @pl.when(i == 0)
        def _():
            db_ref[...] = jnp.zeros_like(db_ref)
            ds_ref[...] = jnp.zeros_like(ds_ref)
            ck[...] = jnp.zeros_like(ck)
            cv[...] = jnp.zeros_like(cv)
        first = i == nb - 1
        _, vjp = jax.vjp(lambda *a: _swa_f(*a, first), q_ref[...], kp_ref[...], kc_ref[...], vp_ref[...],
                         vc_ref[...], b_ref[...], s_ref[...])
        dq, dkp, dkc, dvp, dvc, db, ds = vjp(do_ref[...])
        dqkv_ref[...] = jnp.concatenate([dq, dkc + ck[...], dvc + cv[...]], axis=1).astype(bf16)
        ck[...] = dkp
        cv[...] = dvp
        db_ref[...] += db
        ds_ref[...] += ds

    return pl.pallas_call(
        body, name="swa_bwd", grid=(nb,),
        in_specs=_swa_specs(nb, True) + [pl.BlockSpec((BLK, A_Q), lambda i: (nb - 1 - i, 0))],
        out_specs=[pl.BlockSpec((BLK, D), lambda i: (nb - 1 - i, 0)),
                   pl.BlockSpec((A_HEADS, BLK, 2 * BLK), lambda i: (0, 0, 0)),
                   pl.BlockSpec((16, LANE), lambda i: (0, 0))],
        out_shape=[jax.ShapeDtypeStruct((S, D), bf16), jax.ShapeDtypeStruct((A_HEADS, BLK, 2 * BLK), f32),
                   jax.ShapeDtypeStruct((16, LANE), f32)],
        scratch_shapes=[pltpu.VMEM((BLK, LANE), f32), pltpu.VMEM((BLK, LANE), f32)],
        compiler_params=_cp(("arbitrary",)),
    )(proj, proj, proj, proj, proj, bias, sk, dmix)


def _dnprep_f(xext, w, is_qk):
    c = (w[3:4] * xext + w[2:3] * shift_down(xext, 1) + w[1:2] * shift_down(xext, 2) + w[0:1] * shift_down(xext, 3))
    a = _silu(c)[HALO:]
    n = a * lax.rsqrt(jnp.sum(a * a, axis=-1, keepdims=True) + EPS)
    return jnp.where(is_qk, n, a)


def dnprep_fwd(proj, cw):
    S = proj.shape[0]
    nblk = B_QKV // LANE
    T = S

    def body(x_ref, w_ref, o_ref):
        is_qk = pl.program_id(0) < 2 * B_QK // LANE
        wv = w_ref[...]

        def tile(r0, first):
            o_ref[pl.ds(r0, T), :] = _dnprep_f(_glu_gext(x_ref, r0, first, T), wv, is_qk)

        tile(0, True)

    return pl.pallas_call(
        body, name="dnprep_fwd", grid=(nblk,),
        in_specs=[pl.BlockSpec((S, LANE), lambda j: (0, j)), pl.BlockSpec((4, LANE), lambda j: (0, j))],
        out_specs=pl.BlockSpec((S, LANE), lambda j: (0, j)),
        out_shape=jax.ShapeDtypeStruct((S, B_QKV), f32), compiler_params=_cp(("parallel",)),
    )(proj, cw)


def dnprep_bwd(proj, cw, dqkvn):
    S = proj.shape[0]
    nblk = B_QKV // LANE

    T = S

    def body(x_ref, w_ref, d_ref, dx_ref, dw_ref):
        is_qk = pl.program_id(0) < 2 * B_QK // LANE
        wv = w_ref[...]

        def tile(r0, first):
            _, vjp = jax.vjp(lambda a, b: _dnprep_f(a, b, is_qk), _glu_gext(x_ref, r0, first, T), wv)
            dx, dw = vjp(d_ref[pl.ds(r0, T), :])
            dx_ref[pl.ds(r0, T), :] = dx[HALO:].astype(bf16)
            if not first:
                dx_ref[pl.ds(r0 - HALO, HALO), :] += dx[:HALO]
            return dw

        dw_ref[...] = tile(0, True)

    col = pl.BlockSpec((S, LANE), lambda j: (0, j))
    wsp = pl.BlockSpec((4, LANE), lambda j: (0, j))
    return pl.pallas_call(
        body, name="dnprep_bwd", grid=(nblk,), in_specs=[col, wsp, col], out_specs=[col, wsp],
        out_shape=[jax.ShapeDtypeStruct((S, B_QKV), bf16), jax.ShapeDtypeStruct((4, B_QKV), f32)],
        compiler_params=_cp(("parallel",)),
    )(proj, cw, dqkvn)


def _hdot(a, b, ca=1, cb=0):
    return _dg(a, b, ca, cb, HI)


def _bdg(a, b, ca, cb):
    dn = (((ca,), (cb,)), ((0,), (0,)))
    ah, bh = a.astype(bf16), b.astype(bf16)
    al, bl = (a - ah.astype(f32)).astype(bf16), (b - bh.astype(f32)).astype(bf16)
    return (lax.dot_general(ah, bh, dn, preferred_element_type=f32)
            + lax.dot_general(ah, bl, dn, preferred_element_type=f32)
            + lax.dot_general(al, bh, dn, preferred_element_type=f32))


@jax.custom_vjp
def hbd(a, b):
    return _bdg(a, b, 2, 1)


@jax.custom_vjp
def hbd_nt(a, b):
    return _bdg(a, b, 2, 2)


@jax.custom_vjp
def hbd_tn(a, b):
    return _bdg(a, b, 1, 1)


hbd.defvjp(lambda a, b: (hbd(a, b), (a, b)), lambda r, g: (hbd_nt(g, r[1]), hbd_tn(r[0], g)))
hbd_nt.defvjp(lambda a, b: (hbd_nt(a, b), (a, b)), lambda r, g: (hbd(g, r[1]), hbd_tn(g, r[0])))
hbd_tn.defvjp(lambda a, b: (hbd_tn(a, b), (a, b)), lambda r, g: (hbd_nt(r[1], g), hbd(r[0], g)))


def _stack(xs):
    return jnp.concatenate([x[None] for x in xs], axis=0)


def _lane_col(x, j):
    lane = lax.broadcasted_iota(jnp.int32, (1, LANE), 1)
    return jnp.sum(jnp.where(lane == j, x, 0.0), axis=-1, keepdims=True)


def _tri_inv(a_mat):
    r = lax.broadcasted_iota(jnp.int32, (1, CHUNK, CHUNK), 1)
    c = lax.broadcasted_iota(jnp.int32, (1, CHUNK, CHUNK), 2)
    pw = -a_mat
    inv = (r == c).astype(f32) + pw
    for _ in range(5):
        pw = hbd(pw, pw)
        inv = inv + hbd(inv, pw)
    return inv


@jax.custom_vjp
def _tri_inv_known(a_mat, inv):
    return inv


_tri_inv_known.defvjp(lambda a, inv: (inv, inv),
                      lambda inv, g: (-hbd_tn(inv, hbd_nt(g, inv)), jnp.zeros_like(inv)))


def _dnc_f(q, k, v, seg, prm, inverse=_tri_inv):
    C = CHUNK
    B = q.shape[0]
    rows = seg.shape[0]
    beta_all = _sigmoid(seg)
    xx = seg + prm[1:2]
    g_all = -jnp.exp(prm[0:1]) * (jnp.maximum(xx, 0.0) + jnp.log(1.0 + jnp.exp(-jnp.abs(xx))))
    r2 = lax.broadcasted_iota(jnp.int32, (rows, rows), 0)
    c2 = lax.broadcasted_iota(jnp.int32, (rows, rows), 1)
    within = (r2 >= c2) & (r2 // C == c2 // C)
    gc_all = _hdot(within.astype(f32), g_all)
    beta = _stack([_lane_col(beta_all[C * j:C * (j + 1)], h) for j in range(rows // C) for h in range(6)])
    gc = _stack([_lane_col(gc_all[C * j:C * (j + 1)], 6 + h) for j in range(rows // C) for h in range(6)])
    r = lax.broadcasted_iota(jnp.int32, (1, C, C), 1)
    c = lax.broadcasted_iota(jnp.int32, (1, C, C), 2)
    incl = r >= c
    strict = r > c
    eye = (r == c).astype(f32)
    g_row = hbd(jnp.ones((B, C, C), f32), eye * gc)
    decay = jnp.where(incl, jnp.exp(jnp.where(incl, gc - g_row, 0.0)), 0.0)
    a_mat = beta * hbd_nt(k, k) * jnp.where(strict, decay, 0.0)
    eg = jnp.exp(gc)
    inv = inverse(a_mat)
    u = hbd(inv, beta * v)
    w = hbd(inv, (beta * eg) * k)
    qc = q * (B_DH ** -0.5)
    attn = hbd_nt(qc, k) * decay
    last = (lax.broadcasted_iota(jnp.int32, (1, C, 1), 1) == C - 1).astype(f32)
    g_last = jnp.sum(gc * last, axis=1, keepdims=True)
    dc = jnp.broadcast_to(jnp.exp(g_last), (B, 1, LANE)).reshape(B, LANE)
    return u, w, qc * eg, k * jnp.exp(g_last - gc), attn, dc, inv


def _b1(a, b, ca, cb):
    return lax.dot_general(a.astype(bf16), b.astype(bf16), (((ca,), (cb,)), ((0,), (0,))), preferred_element_type=f32)


@jax.custom_vjp
def sbd(a, b):
    return _b1(a, b, 2, 1)


@jax.custom_vjp
def sbd_nt(a, b):
    return _b1(a, b, 2, 2)


@jax.custom_vjp
def sbd_tn(a, b):
    return _b1(a, b, 1, 1)


sbd.defvjp(lambda a, b: (sbd(a, b), (a, b)), lambda r, g: (sbd_nt(g, r[1]), sbd_tn(r[0], g)))
sbd_nt.defvjp(lambda a, b: (sbd_nt(a, b), (a, b)), lambda r, g: (sbd(g, r[1]), sbd_tn(g, r[0])))
sbd_tn.defvjp(lambda a, b: (sbd_tn(a, b), (a, b)), lambda r, g: (sbd_nt(r[1], g), sbd(r[0], g)))


def _dns_f(S0, u, w, qd, kt, attn, dcrows):
    dc = _lane_col(dcrows, 0).reshape(6, 1, 1)
    delta = u - sbd(w, S0)
    out = sbd(qd, S0) + sbd(attn, delta)
    return out, dc * S0 + sbd_tn(kt, delta)


def _dnpost_f(o, z, grow):
    outs = []
    for h in range(6):
        oh = o[:, LANE * h:LANE * (h + 1)]
        outs.append(oh * lax.rsqrt(jnp.mean(oh * oh, axis=-1, keepdims=True) + EPS) * grow
                    * _silu(z[:, LANE * h:LANE * (h + 1)]))
    return jnp.concatenate(outs, axis=1)


def _hs(h):
    return slice(LANE * h, LANE * (h + 1))


DN_CHUNKS = 4


def _heads(ref, share):
    return _stack([ref[CHUNK * j:CHUNK * (j + 1), _hs(h // share)]
                   for j in range(ref.shape[0] // CHUNK) for h in range(6)])


def _put_heads(ref, val):
    for j in range(ref.shape[0] // CHUNK):
        for h in range(6):
            ref[CHUNK * j:CHUNK * (j + 1), _hs(h)] = val[6 * j + h]


def _dnc_in_specs():
    rows = CHUNK * DN_CHUNKS
    return [
        pl.BlockSpec((rows, B_QK), lambda n: (n, 0)),
        pl.BlockSpec((rows, B_QK), lambda n: (n, 1)),
        pl.BlockSpec((rows, B_V), lambda n: (n, 1)),
        pl.BlockSpec((rows, LANE), lambda n: (n, 20)),
        pl.BlockSpec((8, LANE), lambda n: (0, 0)),
    ]


def _dnc_out_specs(rev_nc=None, chunks=1):
    ci = (lambda n: n) if rev_nc is None else (lambda n: rev_nc - 1 - n)
    wide = pl.BlockSpec((CHUNK * chunks, B_V), lambda n: (ci(n), 0))
    return [wide, wide, wide, wide, pl.BlockSpec((chunks, 6, CHUNK, CHUNK), lambda n: (ci(n), 0, 0, 0)),
            pl.BlockSpec((chunks, 8, LANE), lambda n: (ci(n), 0, 0))]


def _dc_rows(dc):
    pad = jnp.zeros((2, LANE), f32)
    return _stack([jnp.concatenate([dc[6 * j:6 * (j + 1)], pad], axis=0) for j in range(dc.shape[0] // 6)])


def _dnc_shapes(S):
    nc = S // CHUNK
    wide = jax.ShapeDtypeStruct((S, B_V), f32)
    return [wide, wide, wide, wide, jax.ShapeDtypeStruct((nc, 6, CHUNK, CHUNK), f32),
            jax.ShapeDtypeStruct((nc, 8, LANE), f32)]


def dnc_fwd(qkvn, proj, prm):
    S = proj.shape[0]

    def body(q_ref, k_ref, v_ref, s_ref, p_ref, u_ref, w_ref, qd_ref, kt_ref, at_ref, dc_ref, inv_ref):
        u, w, qd, kt, attn, dc, inv = _dnc_f(_heads(q_ref, 2), _heads(k_ref, 2), _heads(v_ref, 1), s_ref[...],
                                             p_ref[...])
        inv_ref[...] = inv.reshape(inv_ref.shape)
        _put_heads(u_ref, u)
        _put_heads(w_ref, w)
        _put_heads(qd_ref, qd)
        _put_heads(kt_ref, kt)
        at_ref[...] = attn.reshape(at_ref.shape)
        dc_ref[...] = _dc_rows(dc)

    outs = _dnc_out_specs(chunks=DN_CHUNKS)
    out = pl.pallas_call(
        body, name="dn_chunk_fwd", grid=(S // (CHUNK * DN_CHUNKS),), in_specs=_dnc_in_specs(),
        out_specs=outs + [outs[4]], out_shape=_dnc_shapes(S) + [_dnc_shapes(S)[4]],
        compiler_params=_cp(("parallel",)),
    )(qkvn, qkvn, qkvn, proj, prm)
    return out[:6], out[6]


def dnc_bwd(qkvn, proj, prm, inv, cots):
    S = proj.shape[0]

    def body(q_ref, k_ref, v_ref, s_ref, p_ref, inv_ref, du_ref, dw_ref, dqd_ref, dkt_ref, dat_ref, ddc_ref,
             dx_ref, dseg_ref, dprm_ref):
        @pl.when(pl.program_id(0) == 0)
        def _():
            dprm_ref[...] = jnp.zeros_like(dprm_ref)
        nb = 6 * DN_CHUNKS
        known = functools.partial(_tri_inv_known, inv=inv_ref[...].reshape(nb, CHUNK, CHUNK))
        _, vjp = jax.vjp(lambda *a: _dnc_f(*a, inverse=known)[:6], _heads(q_ref, 2), _heads(k_ref, 2),
                         _heads(v_ref, 1), s_ref[...], p_ref[...])
        ddc = jnp.concatenate([ddc_ref[j, 0:6, :] for j in range(DN_CHUNKS)], axis=0)
        dq, dk, dv, dseg, dprm = vjp((_heads(du_ref, 1), _heads(dw_ref, 1), _heads(dqd_ref, 1), _heads(dkt_ref, 1),
                                      dat_ref[...].reshape(nb, CHUNK, CHUNK), ddc))
        for j in range(DN_CHUNKS):
            o = 6 * j
            dx_ref[CHUNK * j:CHUNK * (j + 1), :] = jnp.concatenate(
                [dq[o] + dq[o + 1], dq[o + 2] + dq[o + 3], dq[o + 4] + dq[o + 5],
                 dk[o] + dk[o + 1], dk[o + 2] + dk[o + 3], dk[o + 4] + dk[o + 5]] + [dv[o + h] for h in range(6)], axis=1)
        dseg_ref[...] = dseg.astype(bf16)
        dprm_ref[...] += dprm

    rows = CHUNK * DN_CHUNKS
    outs = _dnc_out_specs(chunks=DN_CHUNKS)
    return pl.pallas_call(
        body, name="dn_chunk_bwd", grid=(S // rows,),
        in_specs=_dnc_in_specs() + [outs[4]] + outs,
        out_specs=[pl.BlockSpec((rows, B_QKV), lambda n: (n, 0)), pl.BlockSpec((rows, LANE), lambda n: (n, 0)),
                   pl.BlockSpec((8, LANE), lambda n: (0, 0))],
        out_shape=[jax.ShapeDtypeStruct((S, B_QKV), f32), jax.ShapeDtypeStruct((S, LANE), bf16),
                   jax.ShapeDtypeStruct((8, LANE), f32)],
        compiler_params=_cp(("arbitrary",)),
    )(qkvn, qkvn, qkvn, proj, prm, inv, *cots)


def dns_fwd(chunked):
    u = chunked[0]
    S = u.shape[0]
    nc = S // CHUNK

    def body(u_ref, w_ref, qd_ref, kt_ref, at_ref, dc_ref, o_ref, st_ref, st):
        @pl.when(pl.program_id(0) == 0)
        def _():
            st[...] = jnp.zeros_like(st)
        S0 = st[...]
        st_ref[0] = S0
        out, S1 = _dns_f(S0, _heads(u_ref, 1), _heads(w_ref, 1), _heads(qd_ref, 1), _heads(kt_ref, 1),
                         at_ref[0], dc_ref[0, 0:6, :])
        _put_heads(o_ref, out)
        st[...] = S1

    return pl.pallas_call(
        body, name="dn_scan_fwd", grid=(nc,), in_specs=_dnc_out_specs(),
        out_specs=[pl.BlockSpec((CHUNK, B_V), lambda n: (n, 0)),
                   pl.BlockSpec((1, 6, B_DH, B_DH), lambda n: (n, 0, 0, 0))],
        out_shape=[jax.ShapeDtypeStruct((S, B_V), f32), jax.ShapeDtypeStruct((nc, 6, B_DH, B_DH), f32)],
        scratch_shapes=[pltpu.VMEM((6, B_DH, B_DH), f32)],
        compiler_params=_cp(("arbitrary",)),
    )(*chunked)


def dns_bwd(chunked, states, do):
    S = do.shape[0]
    nc = S // CHUNK

    def body(u_ref, w_ref, qd_ref, kt_ref, at_ref, dc_ref, st_ref, do_ref,
             du_ref, dw_ref, dqd_ref, dkt_ref, dat_ref, ddc_ref, dst):
        @pl.when(pl.program_id(0) == 0)
        def _():
            dst[...] = jnp.zeros_like(dst)
        _, vjp = jax.vjp(_dns_f, st_ref[0], _heads(u_ref, 1), _heads(w_ref, 1), _heads(qd_ref, 1), _heads(kt_ref, 1),
                         at_ref[0], dc_ref[0, 0:6, :])
        dS0, du, dw, dqd, dkt, dat, ddc = vjp((_heads(do_ref, 1), dst[...]))
        dst[...] = dS0
        _put_heads(du_ref, du)
        _put_heads(dw_ref, dw)
        _put_heads(dqd_ref, dqd)
        _put_heads(dkt_ref, dkt)
        dat_ref[0] = dat
        ddc_ref[0] = jnp.concatenate([ddc, jnp.zeros((2, LANE), f32)], axis=0)

    return pl.pallas_call(
        body, name="dn_scan_bwd", grid=(nc,),
        in_specs=_dnc_out_specs(nc) + [pl.BlockSpec((1, 6, B_DH, B_DH), lambda n: (nc - 1 - n, 0, 0, 0)),
                                       pl.BlockSpec((CHUNK, B_V), lambda n: (nc - 1 - n, 0))],
        out_specs=_dnc_out_specs(nc), out_shape=_dnc_shapes(S),
        scratch_shapes=[pltpu.VMEM((6, B_DH, B_DH), f32)],
        compiler_params=_cp(("arbitrary",)),
    )(*chunked, states, do)


def dnpost_fwd(o, proj, prm):
    S = o.shape[0]
    t = min(512, S)

    def body(o_ref, z_ref, p_ref, y_ref):
        y_ref[...] = _dnpost_f(o_ref[...], z_ref[...], p_ref[2:3, :]).astype(bf16)

    tok = pl.BlockSpec((t, B_V), lambda i: (i, 0))
    return pl.pallas_call(
        body, name="dn_post_fwd", grid=(S // t,),
        in_specs=[tok, pl.BlockSpec((t, B_V), lambda i: (i, 2)), pl.BlockSpec((8, LANE), lambda i: (0, 0))],
        out_specs=tok, out_shape=jax.ShapeDtypeStruct((S, B_V), bf16), compiler_params=_cp(("parallel",)),
    )(o, proj, prm)


def dnpost_bwd(o, proj, prm, dmix):
    S = o.shape[0]
    t = min(512, S)

    def body(o_ref, z_ref, p_ref, dy_ref, do_ref, dz_ref, dg_ref):
        @pl.when(pl.program_id(0) == 0)
        def _():
            dg_ref[...] = jnp.zeros_like(dg_ref)
        _, vjp = jax.vjp(_dnpost_f, o_ref[...], z_ref[...], p_ref[2:3, :])
        do, dz, dg = vjp(dy_ref[...])
        do_ref[...] = do
        dz_ref[...] = dz.astype(bf16)
        dg_ref[...] += dg

    tok = pl.BlockSpec((t, B_V), lambda i: (i, 0))
    return pl.pallas_call(
        body, name="dn_post_bwd", grid=(S // t,),
        in_specs=[tok, pl.BlockSpec((t, B_V), lambda i: (i, 2)), pl.BlockSpec((8, LANE), lambda i: (0, 0)), tok],
        out_specs=[tok, tok, pl.BlockSpec((1, LANE), lambda i: (0, 0))],
        out_shape=[jax.ShapeDtypeStruct((S, B_V), f32), jax.ShapeDtypeStruct((S, B_V), bf16),
                   jax.ShapeDtypeStruct((1, LANE), f32)],
        compiler_params=_cp(("arbitrary",)),
    )(o, proj, prm, dmix)


N_FF_BLK = D_FF // LANE
GU_SHARD = 2 * D_FF // 4


GLU_ROWS = 256
HALO = 16


def _glu_f(gext, up, w, b):
    c = w[2:3] * gext + w[1:2] * shift_down(gext, 1) + w[0:1] * shift_down(gext, 2) + b
    return _silu(c)[HALO:] * up


def _glu_gext(g_ref, r0, first, T=GLU_ROWS):
    if first:
        return jnp.concatenate([jnp.zeros((HALO, LANE), f32), g_ref[0:T, :].astype(f32)], axis=0)
    return g_ref[pl.ds(r0 - HALO, T + HALO), :].astype(f32)


def glu_fwd(gu, w, b, name):
    S = gu.shape[0]
    T = min(GLU_ROWS, S // 2)

    def body(g_ref, u_ref, w_ref, b_ref, o_ref):
        wv, bv = w_ref[...], b_ref[...]

        def tile(r0, first):
            act = _glu_f(_glu_gext(g_ref, r0, first, T), u_ref[pl.ds(r0, T), :].astype(f32), wv, bv)
            o_ref[pl.ds(r0, T), :] = act.astype(bf16)

        tile(0, True)

        @pl.loop(1, S // T)
        def _(t):
            tile(pl.multiple_of(t * T, T), False)

    col = pl.BlockSpec((S, LANE), lambda j: (0, j))
    return pl.pallas_call(
        body, name=name, grid=(N_FF_BLK,),
        in_specs=[col, pl.BlockSpec((S, LANE), lambda j: (0, N_FF_BLK + j)), pl.BlockSpec((3, LANE), lambda j: (0, j)),
                  pl.BlockSpec((1, LANE), lambda j: (0, j))],
        out_specs=col, out_shape=jax.ShapeDtypeStruct((S, D_FF), bf16), compiler_params=_cp(("parallel",)),
    )(gu, gu, w, b.reshape(1, D_FF))


def glu_bwd(gu, w, b, dact, name):
    S = gu.shape[0]
    T = min(GLU_ROWS, S // 2)

    def body(g_ref, u_ref, w_ref, b_ref, d_ref, dg_ref, dw_ref, db_ref, acc):
        wv, bv = w_ref[...], b_ref[...]

        def tile(r0, first):
            _, vjp = jax.vjp(_glu_f, _glu_gext(g_ref, r0, first, T), u_ref[pl.ds(r0, T), :].astype(f32), wv, bv)
            dgx, du, dw, db = vjp(d_ref[pl.ds(r0, T), :].astype(f32))
            acc[pl.ds(r0, T), :] = dgx[HALO:]
            if not first:
                acc[pl.ds(r0 - HALO, HALO), :] += dgx[:HALO]
            dg_ref[1, pl.ds(r0, T), :] = du.astype(bf16)
            return dw, db

        dw0, db0 = tile(0, True)
        dw_ref[...] = dw0
        db_ref[...] = db0

        @pl.loop(1, S // T)
        def _(t):
            dw, db = tile(pl.multiple_of(t * T, T), False)
            dw_ref[...] += dw
            db_ref[...] += db

        dg_ref[0] = acc[...].astype(bf16)

    col = pl.BlockSpec((S, LANE), lambda j: (0, j))
    wsp = pl.BlockSpec((3, LANE), lambda j: (0, j))
    bsp = pl.BlockSpec((1, LANE), lambda j: (0, j))
    return pl.pallas_call(
        body, name=name, grid=(N_FF_BLK,),
        in_specs=[col, pl.BlockSpec((S, LANE), lambda j: (0, N_FF_BLK + j)), wsp, bsp, col],
        out_specs=[pl.BlockSpec((2, S, LANE), lambda j: (0, 0, j)), wsp, bsp],
        out_shape=[jax.ShapeDtypeStruct((2, S, D_FF), bf16), jax.ShapeDtypeStruct((3, D_FF), f32),
                   jax.ShapeDtypeStruct((1, D_FF), f32)],
        scratch_shapes=[pltpu.VMEM((S, LANE), f32)],
        compiler_params=_cp(("parallel",)),
    )(gu, gu, w, b.reshape(1, D_FF), dact)


def gu_fwd(n2, wg, name):
    S = n2.shape[0]
    tm = min(MM_ROWS, S)

    def body(a_ref, w_ref, o_ref):
        o_ref[...] = _dg(a_ref[...], w_ref[...], 1, 0).astype(bf16)

    return pl.pallas_call(
        body, name=name, grid=(4, S // tm),
        in_specs=[pl.BlockSpec((tm, D), lambda s, m: (m, 0)), pl.BlockSpec((None, D, GU_SHARD), lambda s, m: (s, 0, 0))],
        out_specs=pl.BlockSpec((tm, GU_SHARD), lambda s, m: (m, s)),
        out_shape=jax.ShapeDtypeStruct((S, 2 * D_FF), bf16), compiler_params=_cp(("parallel", "parallel")),
    )(n2, wg)


def gu_bwd_x(dgu, wg, norm, name):
    S = dgu.shape[1]
    tm = min(NORM_ROWS, S)

    def body(d_ref, w_ref, h_ref, g_ref, r_ref, o_ref, dg_ref):
        _acc_then_norm_bwd(_dg(d_ref[...], w_ref[...], 1, 1), 4, h_ref, g_ref, r_ref, o_ref, dg_ref)

    tok = pl.BlockSpec((tm, D), lambda m, s: (m, 0))
    vec = pl.BlockSpec((1, D), lambda m, s: (0, 0))
    return pl.pallas_call(
        body, name=name, grid=(S // tm, 4),
        in_specs=[pl.BlockSpec((None, tm, GU_SHARD), lambda m, s: (s // 2, m, s % 2)),
                  pl.BlockSpec((None, D, GU_SHARD), lambda m, s: (s, 0, 0)), tok, vec, tok],
        out_specs=[tok, vec],
        out_shape=[jax.ShapeDtypeStruct((S, D), f32), jax.ShapeDtypeStruct((1, D), f32)],
        compiler_params=_cp(("arbitrary", "arbitrary")),
    )(dgu, wg, norm[0], norm[1].reshape(1, D), norm[2])


def gu_bwd_w(n2, dgu, name):
    S = n2.shape[0]
    tm = min(MM_ROWS, S)
    nm = S // tm

    def body(a_ref, d_ref, o_ref, acc):
        @pl.when(pl.program_id(1) == 0)
        def _():
            acc[...] = jnp.zeros_like(acc)
        acc[...] += _dg(a_ref[...], d_ref[...], 0, 0)

        @pl.when(pl.program_id(1) == nm - 1)
        def _():
            o_ref[...] = acc[...].astype(bf16)

    return pl.pallas_call(
        body, name=name, grid=(4, nm),
        in_specs=[pl.BlockSpec((tm, D), lambda s, m: (m, 0)),
                  pl.BlockSpec((None, tm, GU_SHARD), lambda s, m: (s // 2, m, s % 2))],
        out_specs=pl.BlockSpec((None, D, GU_SHARD), lambda s, m: (s, 0, 0)),
        out_shape=jax.ShapeDtypeStruct((4, D, GU_SHARD), bf16),
        scratch_shapes=[pltpu.VMEM((D, GU_SHARD), f32)],
        compiler_params=_cp(("parallel", "arbitrary")),
    )(n2, dgu)


def _pair_cols(w):
    lead = w.shape[:-1]
    return w.reshape(lead + (2, 6, A_DH)).swapaxes(-3, -2).reshape(lead + (A_Q,))


def _unpair_cols(w):
    lead = w.shape[:-1]
    return w.reshape(lead + (6, 2, A_DH)).swapaxes(-3, -2).reshape(lead + (A_Q,))


def _lay_in_a(w):
    return jnp.concatenate([_pair_cols(w[:, :A_Q]), w[:, A_Q:]], axis=1)


def _unlay_in_a(w):
    return jnp.concatenate([_unpair_cols(w[:, :A_Q]), w[:, A_Q:]], axis=1)


def _lay_out_a(w):
    return jnp.concatenate([_pair_cols(w[:A_Q].T).T, w[A_Q:]], axis=0)


def _unlay_out_a(w):
    return jnp.concatenate([_unpair_cols(w[:A_Q].T).T, w[A_Q:]], axis=0)


def _lay_in_b(w):
    return jnp.concatenate([w[:, :2304], w[:, 2316:], w[:, 2304:2316],
                            jnp.zeros((w.shape[0], LANE - 12), w.dtype)], axis=1)


def _unlay_in_b(w):
    return jnp.concatenate([w[:, :2304], w[:, 2560:2572], w[:, 2304:2560]], axis=1)


def _chip_cols(w):
    return jnp.moveaxis(w.reshape(w.shape[0], 4, w.shape[1] // 4), 1, 0)


def _unchip_cols(w):
    return jnp.moveaxis(w, 0, 1).reshape(w.shape[1], 4 * w.shape[2])


def _local_step(x, mem, target, P):
    arrive = P.get("arrive", lambda key, after: None)
    ready = P.get("ready", lambda key, grads, dep: dep)
    sk = jnp.zeros((16, LANE), f32).at[:A_HEADS].set(jnp.broadcast_to(P["sinks"][:, None], (A_HEADS, LANE)))
    prm = jnp.zeros((8, LANE), f32).at[0, 6:12].set(P["a_log"]).at[1, 6:12].set(P["dt_bias"]).at[2].set(P["out_norm_g"])
    bias = bias_build(P["rel_bias"])
    saved = []
    h = x
    for i in range(2):
        n1 = rms_fwd(h, P["g_mix"][i], f"rms_mix{i}")
        arrive(("w_in", i), n1)
        proj = mm_nn(n1, P["w_in_a"] if i == 0 else P["w_in_b"], name="proj_a" if i == 0 else "proj_b")
        arrive(("w_mem", i), proj)
        kv = memkv_fwd(mem, P["g_mem"][i], P["w_mem"][i], f"memkv{i}")
        if i == 0:
            self_out = swa_fwd(proj, bias, sk)
            cross = xattn_fwd(proj, A_Q + 2 * LANE, kv, "xattn_a")
            extra = ()
        else:
            qkvn = dnprep_fwd(proj, P["conv_qkv"])
            chunked, inv = dnc_fwd(qkvn, proj, prm)
            o, states = dns_fwd(chunked)
            self_out = dnpost_fwd(o, proj, prm)
            cross = xattn_fwd(proj, 2304, kv, "xattn_b")
            extra = (qkvn, chunked, inv, states, o)
        mix = jnp.concatenate([self_out, cross], axis=1)
        arrive(("w_out", i), cross)
        h2 = mm_nn(mix, P["w_out"][i], res=h, name=f"out_proj{i}")
        n2 = rms_fwd(h2, P["g_ffn"][i], f"rms_ffn{i}")
        arrive(("w_gu", i), n2)
        gu = gu_fwd(n2, P["w_gu"][i], f"gate_up{i}")
        act = glu_fwd(gu, P["ffn_cw"][i], P["ffn_cb"][i], f"glu{i}")
        arrive(("w_down", i), act)
        h3 = mm_nn(act, P["w_down"][i], res=h2, name=f"down{i}")
        saved.append((h, n1, kv, proj, mix, h2, n2, gu, act, extra))
        h = h3

    loss, dh, dg_fin = loss_head(h, P["g_fin"], target)
    G = {"g_fin": dg_fin[0], "g_mix": [None, None], "g_mem": [None, None], "g_ffn": [None, None],
         "w_mem": [None, None], "w_out": [None, None], "w_gu": [None, None], "w_down": [None, None],
         "ffn_cw": [None, None], "ffn_cb": [None, None]}
    for i in (1, 0):
        hin, n1, kv, proj, mix, h2, n2, gu, act, extra = saved[i]
        dact = mm_nt(dh, P["w_down"][i], out_dtype=bf16, name=f"d_act{i}")
        G["w_down"][i] = mm_tn(act, dh, name=f"dw_down{i}")
        dgu, dcw, dcb = glu_bwd(gu, P["ffn_cw"][i], P["ffn_cb"][i], dact, f"glu_bwd{i}")
        G["ffn_cw"][i], G["ffn_cb"][i] = dcw, dcb[0]
        G["w_gu"][i] = gu_bwd_w(n2, dgu, f"dw_gu{i}")
        g_ffn = ready(("ffn", i), G, P["g_ffn"][i])
        dh2, dg = gu_bwd_x(dgu, P["w_gu"][i], (h2, g_ffn, dh), f"d_n2_{i}")
        G["g_ffn"][i] = dg[0]
        dmix = mm_nt(dh2, P["w_out"][i], name=f"d_mix{i}")
        G["w_out"][i] = mm_tn(mix, dh2, name=f"dw_out{i}")
        if i == 0:
            dqkv, dbias, dsk = swa_bwd(proj, bias, sk, dmix)
            dxq, dkv = xattn_bwd(proj, A_Q + 2 * LANE, kv, dmix, "xattn_a_bwd")
            dproj = jnp.concatenate([dqkv, dxq], axis=1)
            G["sinks"] = dsk[:A_HEADS, 0]
            G["rel_bias"] = bias_grad(dbias)[:, :A_HEADS]
            w_in, gname = P["w_in_a"], "w_in_a"
        else:
            qkvn, chunked, inv, states, o = extra
            do, dz, dgo = dnpost_bwd(o, proj, prm, dmix)
            dqkvn, dseg, dprm = dnc_bwd(qkvn, proj, prm, inv, dns_bwd(chunked, states, do))
            draw, dconv = dnprep_bwd(proj, P["conv_qkv"], dqkvn)
            dxq, dkv = xattn_bwd(proj, 2304, kv, dmix, "xattn_b_bwd")
            dproj = jnp.concatenate([draw, dz, dxq, dseg], axis=1)
            G["conv_qkv"] = dconv
            G["a_log"], G["dt_bias"], G["out_norm_g"] = dprm[0, 6:12], dprm[1, 6:12], dgo[0]
            w_in, gname = P["w_in_b"], "w_in_b"
        G[gname] = mm_tn(n1, dproj, name=f"d{gname}")
        dh, dg = mm_nt_norm(dproj, w_in, (hin, P["g_mix"][i], dh2), f"d_n1_{i}")
        G["g_mix"][i] = dg[0]
        dgm, dwm = memkv_bwd(mem, P["g_mem"][i], P["w_mem"][i], dkv, f"memkv_bwd{i}")
        G["g_mem"][i], G["w_mem"][i] = dgm[0], dwm
        ready(("mix", i), G, None)
    return loss, dh, G


def _prepare(full, w_gu=None):
    return {
        "rel_bias": full["rel_bias"], "sinks": full["sinks_a"][0], "a_log": full["a_log_b"][0],
        "dt_bias": full["dt_bias_b"][0], "out_norm_g": full["out_norm_g_b"][0],
        "g_mix": full["norm_mix_g"], "g_mem": full["norm_mem_g"], "g_ffn": full["norm_ffn_g"],
        "g_fin": full["final_norm_g"], "conv_qkv": full["conv_qkv_b"][0],
        "ffn_cw": [full["ffn_conv_w"][0], full["ffn_conv_w"][1]],
        "ffn_cb": [full["ffn_conv_b"][0], full["ffn_conv_b"][1]],
        "w_mem": [full["w_mem_kv"][0], full["w_mem_kv"][1]],
        "w_out": [_lay_out_a(full["w_out"][0]), full["w_out"][1]],
        "w_in_a": _lay_in_a(full["w_in_a"][0]), "w_in_b": _lay_in_b(full["w_in_b"][0]),
        "w_gu": w_gu if w_gu is not None else [_chip_cols(full["w_gate_up"][0]), _chip_cols(full["w_gate_up"][1])],
        "w_down": [full["w_down"][0], full["w_down"][1]],
    }


def _grads_to_ref(G):
    return {
        "rel_bias": G["rel_bias"], "norm_mix_g": jnp.stack(G["g_mix"]), "norm_mem_g": jnp.stack(G["g_mem"]),
        "w_mem_kv": jnp.stack(G["w_mem"]),
        "w_out": jnp.stack([_unlay_out_a(G["w_out"][0]), G["w_out"][1]]),
        "w_in_a": _unlay_in_a(G["w_in_a"])[None], "sinks_a": G["sinks"][None],
        "w_in_b": _unlay_in_b(G["w_in_b"])[None], "conv_qkv_b": G["conv_qkv"][None],
        "a_log_b": G["a_log"][None], "dt_bias_b": G["dt_bias"][None], "out_norm_g_b": G["out_norm_g"][None],
        "norm_ffn_g": jnp.stack(G["g_ffn"]),
        "w_gate_up": jnp.stack([_unchip_cols(G["w_gu"][0]), _unchip_cols(G["w_gu"][1])]).astype(f32),
        "ffn_conv_w": jnp.stack(G["ffn_cw"]), "ffn_conv_b": jnp.stack(G["ffn_cb"]),
        "w_down": jnp.stack(G["w_down"]), "final_norm_g": G["g_fin"],
    }


ANY = pl.BlockSpec(memory_space=pl.ANY)


def _place():
    return lax.axis_index("x"), lax.axis_index("y"), lax.axis_index("c")


def chip_scatter(gs):
    n = len(gs)

    def body(*refs):
        ins, outs = refs[:n], refs[n:2 * n]
        ssem, rsem = refs[2 * n:]
        x, y, c = _place()
        me = 2 * x + y
        peers = [(1 - x, y), (x, 1 - y), (1 - x, 1 - y)]

        def remote(j, k, slot):
            px, py = peers[k]
            return pltpu.make_async_remote_copy(
                src_ref=ins[j].at[2 * px + py], dst_ref=outs[j].at[slot],
                send_sem=ssem.at[3 * j + k], recv_sem=rsem.at[3 * j + k],
                device_id=(px, py, c), device_id_type=MESH)

        sends = [remote(j, k, me) for j in range(n) for k in range(3)]
        for cp in sends:
            cp.start()
        for j in range(n):
            for k in range(3):
                px, py = peers[k]
                remote(j, k, 2 * px + py).wait_recv()
        for cp in sends:
            cp.wait_send()

    return pl.pallas_call(
        body, name="grad_scatter", in_specs=[ANY] * n, out_specs=[ANY] * n,
        out_shape=[jax.ShapeDtypeStruct(g.shape, g.dtype) for g in gs],
        scratch_shapes=[pltpu.SemaphoreType.DMA((3 * n,)), pltpu.SemaphoreType.DMA((3 * n,))],
    )(*gs)


def allreduce_small(buf):
    R = buf.shape[0]

    def body(b_ref, o_ref, recv, ssem, rsem):
        x, y, c = _place()
        me = 4 * x + 2 * y + c

        def peer(k):
            return (1 - x if k & 4 else x, 1 - y if k & 2 else y, 1 - c if k & 1 else c)

        def remote(k, slot):
            return pltpu.make_async_remote_copy(
                src_ref=b_ref, dst_ref=recv.at[slot], send_sem=ssem.at[k - 1], recv_sem=rsem.at[k - 1],
                device_id=peer(k), device_id_type=MESH)

        sends = [remote(k, me) for k in range(1, 8)]
        for cp in sends:
            cp.start()
        recv[me] = b_ref[...]
        for k in range(1, 8):
            px, py, pc = peer(k)
            remote(k, 4 * px + 2 * py + pc).wait_recv()
        for cp in sends:
            cp.wait_send()
        total = recv[0]
        for j in range(1, 8):
            total = total + recv[j]
        o_ref[...] = total

    return pl.pallas_call(
        body, name="small_allreduce",
        in_specs=[pl.BlockSpec(memory_space=pltpu.VMEM)], out_specs=pl.BlockSpec(memory_space=pltpu.VMEM),
        out_shape=jax.ShapeDtypeStruct(buf.shape, f32),
        scratch_shapes=[pltpu.VMEM((8, R, LANE), f32), pltpu.SemaphoreType.DMA((7,)), pltpu.SemaphoreType.DMA((7,))],
    )(buf)


def sum_slots(own, recv, chip, core, name):
    _, R, C = recv.shape
    tr = _row_tile(R, 256)
    nt = R // tr

    def body(p_ref, a_ref, r_ref, o_ref):
        acc = jnp.zeros((tr, C), f32)
        for s in range(4):
            acc = acc + jnp.where(p_ref[0] == s, a_ref[s], r_ref[s]).astype(f32)
        o_ref[...] = acc

    slots = pl.BlockSpec((4, tr, C), lambda i, p_ref: (0, i, 0))
    return pl.pallas_call(
        body, name=name, out_shape=jax.ShapeDtypeStruct((2 * R, C), f32),
        grid_spec=pltpu.PrefetchScalarGridSpec(
            num_scalar_prefetch=1, grid=(nt,), in_specs=[slots, slots],
            out_specs=pl.BlockSpec((tr, C), lambda i, p_ref: (p_ref[1] * nt + i, 0))),
        compiler_params=_cp(("parallel",)),
    )(jnp.stack([chip, core]).astype(jnp.int32), own, recv)


def _half(ref, core, axis=0):
    half = ref.shape[axis] // 2
    idx = (slice(None),) * axis + (pl.ds(core * half, half),)
    return ref.at[idx]


IN_HBM = pl.BlockSpec(memory_space=pltpu.HBM)
IN_SEM = pl.BlockSpec(memory_space=pltpu.SEMAPHORE)
SIDE_EFFECT = pltpu.SideEffectType.DATAFLOW_SIDE_EFFECTING


def _gather_copy(buf, i, k, ssem, rsem, place, landing):
    x, y, c = place
    px, py = [(1 - x, y), (x, 1 - y), (1 - x, 1 - y)][k]
    me = 2 * x + y
    return pltpu.make_async_remote_copy(
        src_ref=buf.at[me], dst_ref=buf.at[me if landing == "theirs" else 2 * px + py],
        send_sem=ssem.at[3 * i + k], recv_sem=rsem.at[3 * i + k], device_id=(px, py, c), device_id_type=MESH)


def gather_start(groups):
    flat = [b for grp in groups for b in grp]
    n, ng = len(flat), len(groups)

    def body(*refs):
        bufs, sems = refs[:n], refs[n:n + 2 * ng]
        place = _place()
        j = 0
        for g, grp in enumerate(groups):
            for i in range(len(grp)):
                for k in range(3):
                    _gather_copy(bufs[j], i, k, sems[2 * g], sems[2 * g + 1], place, "theirs").start()
                j += 1

    sem_shapes = [pltpu.SemaphoreType.DMA((3 * len(grp),)) for grp in groups for _ in range(2)]
    out = pl.pallas_call(
        body, name="gather_start", in_specs=[IN_HBM] * n, out_specs=(*[IN_SEM] * (2 * ng), *[IN_HBM] * n),
        out_shape=(*sem_shapes, *[pltpu.HBM(b.shape, b.dtype) for b in flat]),
        input_output_aliases={i: 2 * ng + i for i in range(n)},
        compiler_params=pltpu.CompilerParams(has_side_effects=SIDE_EFFECT),
    )(*[pltpu.with_memory_space_constraint(b, pltpu.HBM) for b in flat])
    sems, bufs = out[:2 * ng], list(out[2 * ng:])
    flights, j = [], 0
    for g, grp in enumerate(groups):
        flights.append((bufs[j:j + len(grp)], sems[2 * g], sems[2 * g + 1]))
        j += len(grp)
    return flights


def gather_wait(flight, after, name):
    bufs, ssem, rsem = flight
    n = len(bufs)

    def body(*refs):
        place = _place()
        for i in range(n):
            for k in range(3):
                cp = _gather_copy(refs[i], i, k, refs[n], refs[n + 1], place, "mine")
                cp.wait_send()
                cp.wait_recv()

    return pl.pallas_call(
        body, name=name, in_specs=[IN_HBM] * n + [IN_SEM, IN_SEM, ANY], out_specs=[IN_HBM] * n,
        out_shape=[pltpu.HBM(b.shape, b.dtype) for b in bufs], input_output_aliases={i: i for i in range(n)},
        compiler_params=pltpu.CompilerParams(has_side_effects=SIDE_EFFECT),
    )(*bufs, ssem, rsem, after)


def _scatter_copy(src, land, j, k, ssem, rsem, place, landing):
    x, y, c = place
    px, py = [(1 - x, y), (x, 1 - y), (1 - x, 1 - y)][k]
    return pltpu.make_async_remote_copy(
        src_ref=src.at[2 * px + py], dst_ref=land.at[2 * x + y if landing == "theirs" else 2 * px + py],
        send_sem=ssem.at[3 * j + k], recv_sem=rsem.at[3 * j + k], device_id=(px, py, c), device_id_type=MESH)


def scatter_start(srcs, name):
    n = len(srcs)
    lands = [lax.empty(g.shape, g.dtype) for g in srcs]

    def body(*refs):
        place = _place()
        for j in range(n):
            for k in range(3):
                _scatter_copy(refs[j], refs[n + j], j, k, refs[2 * n], refs[2 * n + 1], place, "theirs").start()
        refs[-1][...] = jnp.zeros_like(refs[-1])

    sem = pltpu.SemaphoreType.DMA((3 * n,))
    hbm = [pltpu.with_memory_space_constraint(b, pltpu.HBM) for b in list(srcs) + lands]
    out = pl.pallas_call(
        body, name=name, in_specs=[IN_HBM] * (2 * n),
        out_specs=(IN_SEM, IN_SEM, *[IN_HBM] * (2 * n), pl.BlockSpec(memory_space=pltpu.VMEM)),
        out_shape=(sem, sem, *[pltpu.HBM(b.shape, b.dtype) for b in hbm], jax.ShapeDtypeStruct((8, LANE), f32)),
        input_output_aliases={i: 2 + i for i in range(2 * n)},
        compiler_params=pltpu.CompilerParams(has_side_effects=SIDE_EFFECT),
    )(*hbm)
    return (list(out[2:2 + n]), list(out[2 + n:2 + 2 * n]), out[0], out[1]), out[-1]


def scatter_wait(flight, after, name):
    srcs, lands, ssem, rsem = flight
    n = len(srcs)

    def body(*refs):
        place = _place()
        for j in range(n):
            for k in range(3):
                cp = _scatter_copy(refs[j], refs[n + j], j, k, refs[2 * n], refs[2 * n + 1], place, "mine")
                cp.wait_send()
                cp.wait_recv()

    out = pl.pallas_call(
        body, name=name, in_specs=[IN_HBM] * (2 * n) + [IN_SEM, IN_SEM, ANY], out_specs=[IN_HBM] * (2 * n),
        out_shape=[pltpu.HBM(b.shape, b.dtype) for b in list(srcs) + list(lands)],
        input_output_aliases={i: i for i in range(2 * n)},
        compiler_params=pltpu.CompilerParams(has_side_effects=SIDE_EFFECT),
    )(*srcs, *lands, ssem, rsem, after)
    return list(out[:n]), list(out[n:])


def pair_exchange(gbufs, name):
    n = len(gbufs)

    def body(*refs):
        ins, outs = refs[:n], refs[n:2 * n]
        ssem, rsem = refs[2 * n:]
        x, y, c = _place()
        cps = [pltpu.make_async_remote_copy(
            src_ref=_half(ins[j], 1 - c, axis=1), dst_ref=outs[j], send_sem=ssem.at[j], recv_sem=rsem.at[j],
            device_id=(x, y, 1 - c), device_id_type=MESH) for j in range(n)]
        for cp in cps:
            cp.start()
        for cp in cps:
            cp.wait()

    return pl.pallas_call(
        body, name=name, in_specs=[ANY] * n, out_specs=[ANY] * n,
        out_shape=[jax.ShapeDtypeStruct((4, g.shape[1] // 2, g.shape[2]), g.dtype) for g in gbufs],
        scratch_shapes=[pltpu.SemaphoreType.DMA((n,)), pltpu.SemaphoreType.DMA((n,))],
    )(*gbufs)


def _row_tile(rows, cap=512):
    return max(t for t in range(16, min(rows, cap) + 1, 16) if rows % t == 0)


def pair_sum(mine, theirs, core, name):
    _, R, C = mine.shape
    half = R // 2
    tr = _row_tile(half)
    nt = half // tr

    def body(c_ref, a_ref, b_ref, o_ref):
        o_ref[...] = (a_ref[...].astype(f32) + b_ref[...].astype(f32)).astype(bf16)

    return pl.pallas_call(
        body, name=name, out_shape=jax.ShapeDtypeStruct(theirs.shape, bf16),
        grid_spec=pltpu.PrefetchScalarGridSpec(
            num_scalar_prefetch=1, grid=(4, nt),
            in_specs=[pl.BlockSpec((None, tr, C), lambda s, i, c_ref: (s, c_ref[0] * nt + i, 0)),
                      pl.BlockSpec((None, tr, C), lambda s, i, c_ref: (s, i, 0))],
            out_specs=pl.BlockSpec((None, tr, C), lambda s, i, c_ref: (s, i, 0))),
        compiler_params=_cp(("parallel", "parallel")),
    )(jnp.reshape(core, (1,)).astype(jnp.int32), mine, theirs)


def final_exchange(fins):
    n = len(fins)

    def body(*refs):
        outs = refs[n:2 * n]
        ssem, rsem = refs[2 * n:]
        x, y, c = _place()
        cps = [pltpu.make_async_remote_copy(
            src_ref=_half(outs[j], c), dst_ref=_half(outs[j], c), send_sem=ssem.at[j], recv_sem=rsem.at[j],
            device_id=(x, y, 1 - c), device_id_type=MESH) for j in range(n)]
        for cp in cps:
            cp.start()
        for cp in cps:
            cp.wait()

    return pl.pallas_call(
        body, name="final_exchange", in_specs=[ANY] * n, out_specs=[ANY] * n,
        out_shape=[jax.ShapeDtypeStruct(f.shape, f.dtype) for f in fins],
        input_output_aliases={j: j for j in range(n)},
        scratch_shapes=[pltpu.SemaphoreType.DMA((n,)), pltpu.SemaphoreType.DMA((n,))],
    )(*fins)


def adamw_big(w, m, v, gs, row0, name):
    L, R, C = w.shape
    tr = _row_tile(math.gcd(R, row0) if row0 else R, max(16, 262144 // C // 16 * 16))
    b0 = row0 // tr

    def body(*refs):
        w_ref, m_ref, v_ref = refs[:3]
        g_refs = refs[3:3 + L]
        g_ref, d_ref, nm_ref, nv_ref = refs[3 + L:]
        g = g_refs[0][...]
        for l in range(1, L):
            g = jnp.where(pl.program_id(0) == l, g_refs[l][...], g)
        d, nm, nv = _adamw_math(w_ref[...], g, m_ref[...], v_ref[...])
        g_ref[...] = g
        d_ref[...] = d
        nm_ref[...] = nm
        nv_ref[...] = nv

    own = pl.BlockSpec((None, tr, C), lambda l, i: (l, i, 0))
    off = pl.BlockSpec((tr, C), lambda l, i: (b0 + i, 0))
    return pl.pallas_call(
        body, name=name, grid=(L, R // tr), in_specs=[own, own, own] + [off] * L, out_specs=[own] * 4,
        out_shape=[jax.ShapeDtypeStruct((L, R, C), f32)] * 4, compiler_params=_cp(("parallel", "parallel")),
    )(w, m, v, *gs)


def _adamw_math(w, g, m, v):
    m = B1 * m + (1.0 - B1) * g
    v = B2 * v + (1.0 - B2) * (g * g)
    m_hat = m / (1.0 - B1 ** STEP)
    v_hat = v / (1.0 - B2 ** STEP)
    delta = -LR * (m_hat / (jnp.sqrt(v_hat) + AEPS) + WD * w)
    return delta, m, v


def adamw_small(w, m, v, g):
    def body(w_ref, m_ref, v_ref, g_ref, d_ref, nm_ref, nv_ref):
        d, nm, nv = _adamw_math(w_ref[...], g_ref[...], m_ref[...], v_ref[...])
        d_ref[...] = d
        nm_ref[...] = nm
        nv_ref[...] = nv

    return pl.pallas_call(body, name="adamw_small", out_shape=[jax.ShapeDtypeStruct(w.shape, f32)] * 3)(w, m, v, g)


CONV =(("conv_qkv_b", 2), ("ffn_conv_w", 2))
SMALL = ("rel_bias", "norm_mix_g", "norm_mem_g", "sinks_a", "a_log_b", "dt_bias_b", "out_norm_g_b", "norm_ffn_g",
         "ffn_conv_b", "final_norm_g")
WEIGHTS = ("rel_bias", "norm_mix_g", "norm_mem_g", "w_mem_kv", "w_out", "w_in_a", "sinks_a", "w_in_b", "conv_qkv_b",
           "a_log_b", "dt_bias_b", "out_norm_g_b", "norm_ffn_g", "w_gate_up", "ffn_conv_w", "ffn_conv_b", "w_down",
           "final_norm_g")
ARGS = ("x", "mem") + WEIGHTS + ("loss_target",) + tuple("m_" + n for n in WEIGHTS) + tuple("v_" + n for n in WEIGHTS)


def _rows(a, width):
    flat = a.reshape(-1)
    pad = (-flat.shape[0]) % (8 * width)
    if pad:
        flat = jnp.concatenate([flat, jnp.zeros((pad,), a.dtype)])
    return flat.reshape(-1, width)


def _nrows(shape, width):
    return _pad_to(-(-math.prod(shape) // width), 8)


def _pack(arrs, width, total_rows, dtype):
    parts = [_rows(a.astype(dtype), width) for a in arrs]
    used = sum(p.shape[0] for p in parts)
    if total_rows > used:
        parts.append(jnp.zeros((total_rows - used, width), dtype))
    return jnp.concatenate(parts, axis=0)


def _unpack(buf, shapes, width):
    out, r = [], 0
    for s in shapes:
        n = _nrows(s, width)
        out.append(buf[r:r + n].reshape(-1)[:math.prod(s)].reshape(s))
        r += n
    return out


def _pad_to(n, mult):
    return -(-n // mult) * mult


def kernel(x, mem, rel_bias, norm_mix_g, norm_mem_g, w_mem_kv, w_out, w_in_a, sinks_a, w_in_b, conv_qkv_b, a_log_b, dt_bias_b, out_norm_g_b, norm_ffn_g, w_gate_up, ffn_conv_w, ffn_conv_b, w_down, final_norm_g, loss_target, m_rel_bias, m_norm_mix_g, m_norm_mem_g, m_w_mem_kv, m_w_out, m_w_in_a, m_sinks_a, m_w_in_b, m_conv_qkv_b, m_a_log_b, m_dt_bias_b, m_out_norm_g_b, m_norm_ffn_g, m_w_gate_up, m_ffn_conv_w, m_ffn_conv_b, m_w_down, m_final_norm_g, v_rel_bias, v_norm_mix_g, v_norm_mem_g, v_w_mem_kv, v_w_out, v_w_in_a, v_sinks_a, v_w_in_b, v_conv_qkv_b, v_a_log_b, v_dt_bias_b, v_out_norm_g_b, v_norm_ffn_g, v_w_gate_up, v_ffn_conv_w, v_ffn_conv_b, v_w_down, v_final_norm_g):
    A = dict(zip(ARGS, (x, mem, rel_bias, norm_mix_g, norm_mem_g, w_mem_kv, w_out, w_in_a, sinks_a, w_in_b, conv_qkv_b, a_log_b, dt_bias_b, out_norm_g_b, norm_ffn_g, w_gate_up, ffn_conv_w, ffn_conv_b, w_down, final_norm_g, loss_target, m_rel_bias, m_norm_mix_g, m_norm_mem_g, m_w_mem_kv, m_w_out, m_w_in_a, m_sinks_a, m_w_in_b, m_conv_qkv_b, m_a_log_b, m_dt_bias_b, m_out_norm_g_b, m_norm_ffn_g, m_w_gate_up, m_ffn_conv_w, m_ffn_conv_b, m_w_down, m_final_norm_g, v_rel_bias, v_norm_mix_g, v_norm_mem_g, v_w_mem_kv, v_w_out, v_w_in_a, v_sinks_a, v_w_in_b, v_conv_qkv_b, v_a_log_b, v_dt_bias_b, v_out_norm_g_b, v_norm_ffn_g, v_w_gate_up, v_ffn_conv_w, v_ffn_conv_b, v_w_down, v_final_norm_g)))
    chip = 2 * lax.axis_index("x") + lax.axis_index("y")
    core = lax.axis_index("c")
    n_down, n_out, n_mem = w_down.shape[1], w_out.shape[1], w_mem_kv.shape[1]

    def own_slot(shard):
        return lax.dynamic_update_index_in_dim(lax.empty((4,) + shard.shape, shard.dtype), shard, chip, 0)

    def bslot(w):
        return own_slot(w.astype(bf16))

    groups = {
        ("w_in", 0): [bslot(w_in_a[0])],
        ("w_mem", 0): [bslot(w_mem_kv[0]), bslot(w_mem_kv[1]), own_slot(conv_qkv_b[0]),
                       own_slot(ffn_conv_w.reshape(6, -1))],
        ("w_out", 0): [bslot(w_out[0])], ("w_gu", 0): [bslot(w_gate_up[0])], ("w_down", 0): [bslot(w_down[0])],
        ("w_in", 1): [bslot(w_in_b[0])],
        ("w_out", 1): [bslot(w_out[1])], ("w_gu", 1): [bslot(w_gate_up[1])], ("w_down", 1): [bslot(w_down[1])],
    }
    flights = dict(zip(groups, gather_start(list(groups.values()))))
    P = {"rel_bias": rel_bias, "sinks": sinks_a[0], "a_log": a_log_b[0], "dt_bias": dt_bias_b[0],
         "out_norm_g": out_norm_g_b[0], "g_mix": norm_mix_g, "g_mem": norm_mem_g, "g_ffn": norm_ffn_g,
         "g_fin": final_norm_g, "ffn_cb": [ffn_conv_b[0], ffn_conv_b[1]], "w_mem": [None, None], "w_out": [None, None],
         "w_gu": [None, None], "w_down": [None, None], "ffn_cw": [None, None]}

    def rows4(g):
        return g.reshape(4 * g.shape[1], g.shape[2])

    def arrive(key, after):
        if key not in flights:
            return
        got = gather_wait(flights.pop(key), after, "gather_wait_%s%d" % key)
        name, i = key
        if name == "w_in":
            P["w_in_a" if i == 0 else "w_in_b"] = (_lay_in_a if i == 0 else _lay_in_b)(_unchip_cols(got[0]))
        elif name == "w_mem":
            P["w_mem"] = [rows4(got[0]), rows4(got[1])]
            P["conv_qkv"] = _unchip_cols(got[2])
            cw = _unchip_cols(got[3]).reshape(2, 3, D_FF)
            P["ffn_cw"] = [cw[0], cw[1]]
        elif name == "w_out":
            P["w_out"][i] = _lay_out_a(rows4(got[0])) if i == 0 else rows4(got[0])
        elif name == "w_gu":
            P["w_gu"][i] = got[0]
        else:
            P["w_down"][i] = rows4(got[0])

    def chip_rows(g):
        return g.reshape(4, g.shape[0] // 4, g.shape[-1])

    sent, started = {}, []

    def ready(key, G, dep):
        kind, i = key
        tag = "%s%d" % key
        if kind == "ffn":
            names, partial = ("gu", "down"), [G["w_gu"][i], chip_rows(G["w_down"][i]).astype(bf16)]
        else:
            g_out = _unlay_out_a(G["w_out"][0]) if i == 0 else G["w_out"][1]
            g_in = _unlay_in_a(G["w_in_a"]) if i == 0 else _unlay_in_b(G["w_in_b"])
            names = ("out", "in", "mem")
            partial = [chip_rows(g_out).astype(bf16), _chip_cols(g_in).astype(bf16), chip_rows(G["w_mem"][i]).astype(bf16)]
        theirs = pair_exchange(partial, "pair_exchange_" + tag)
        pair = [pair_sum(p, t, core, "pair_sum_%s%d" % (nm, i)) for p, t, nm in zip(partial, theirs, names)]
        if key == ("mix", 0):
            sent[key] = (names, pair, chip_scatter(pair))
            return dep
        flight, token = scatter_start(pair, "scatter_start_" + tag)
        sent[key] = (names, flight)
        started.append(token[0, 0])
        if dep is not None:
            while started:
                dep = dep + started.pop()
        return dep

    P["arrive"], P["ready"] = arrive, ready

    loss, dx, G = _local_step(x[0], mem[0], loss_target[0], P)
    gfull = _grads_to_ref(G)

    fin = {}
    for key in (("ffn", 1), ("mix", 1), ("ffn", 0), ("mix", 0)):
        if key == ("mix", 0):
            names, pair, arrived = sent[key]
        else:
            names, flight = sent[key]
            pair, arrived = scatter_wait(flight, dx, "scatter_wait_%s%d" % key)
        for nm, p, r in zip(names, pair, arrived):
            fin[nm, key[1]] = sum_slots(p, r, chip, core, "sum_slots_%s%d" % (nm, key[1]))
    order = list(fin)
    done = dict(zip(order, final_exchange([fin[k] for k in order])))

    sm_shapes = [A[n].shape for n in SMALL] + [gfull[n].shape for n, _ in CONV] + [(LANE,)]
    sm_rows = _pad_to(sum(_nrows(s, LANE) for s in sm_shapes), 8)
    sbuf = _pack([gfull[n] for n in SMALL] + [gfull[n] for n, _ in CONV] + [loss[0]], LANE, sm_rows, f32)
    tot = _unpack(allreduce_small(sbuf), sm_shapes, LANE)
    gsmall = dict(zip(SMALL, tot[:len(SMALL)]))
    for (n, axis), t in zip(CONV, tot[len(SMALL):len(SMALL) + len(CONV)]):
        sh = A[n].shape[axis]
        gsmall[n] = lax.dynamic_slice_in_dim(t, chip * sh, sh, axis)
    loss_out = tot[-1][0]

    out = {}
    plan = (("w_gate_up", [done["gu", 0], done["gu", 1]]), ("w_down", [done["down", 0], done["down", 1]]),
            ("w_out", [done["out", 0], done["out", 1]]), ("w_mem_kv", [done["mem", 0], done["mem", 1]]),
            ("w_in_a", [done["in", 0]]), ("w_in_b", [done["in", 1]]))
    for n, gs in plan:
        shape3 = (len(gs),) + gs[0].shape
        res = adamw_big(A[n].reshape(shape3), A["m_" + n].reshape(shape3), A["v_" + n].reshape(shape3), gs, 0,
                        "adamw_" + n)
        for key, r in zip(("grad_", "delta_", "new_m_", "new_v_"), res):
            out[key + n] = r.reshape(A[n].shape)
    names = SMALL + tuple(n for n, _ in CONV)
    shapes = [A[n].shape for n in names]
    rows = _pad_to(sum(_nrows(s, LANE) for s in shapes), 8)
    packs = [_pack([src[n] for n in names], LANE, rows, f32)
             for src in ({n: A[n] for n in names}, {n: A["m_" + n] for n in names}, {n: A["v_" + n] for n in names}, gsmall)]
    res = adamw_small(*packs)
    for key, r in zip(("delta_", "new_m_", "new_v_"), res):
        for n, a in zip(names, _unpack(r, shapes, LANE)):
            out[key + n] = a
    for n in names:
        out["grad_" + n] = gsmall[n]
    return (loss_out, dx[None], *[out["grad_" + n] for n in WEIGHTS], *[out["delta_" + n] for n in WEIGHTS],
            *[out["new_m_" + n] for n in WEIGHTS], *[out["new_v_" + n] for n in WEIGHTS])
```

```python
import functools
import math

import numpy as np
import jax
import jax.numpy as jnp
from jax import lax
from jax.experimental import pallas as pl
from jax.experimental.pallas import tpu as pltpu

f32 = jnp.float32
bf16 = jnp.bfloat16
HI = lax.Precision.HIGHEST
MESH = pl.DeviceIdType.MESH

D = 1024
MEM_LEN = 256
EPS = 1e-6
A_HEADS, A_KV, A_DH = 12, 2, 64
A_Q = 768
BLK = 128
N_BUCKETS, MAX_DIST = 32, 128
B_QK, B_V, B_DH = 384, 768, 128
B_QKV = 1536
CHUNK = 64
X_Q = 256
D_FF = 2816
IN_A = 1280
IN_B = 2572
IN_B_PAD = 2688
LANE = 128
VMEM_LIMIT = 56 * 1024 * 1024
MM_ROWS = 1024

LR, B1, B2, AEPS, WD, STEP = 0.001, 0.9, 0.999, 1e-08, 0.01, 10


def _cp(sem=None):
    return pltpu.CompilerParams(dimension_semantics=sem, vmem_limit_bytes=VMEM_LIMIT)


def _dg(a, b, ca, cb, prec=None):
    return lax.dot_general(a, b, (((ca,), (cb,)), ((), ())), precision=prec, preferred_element_type=f32)


@jax.custom_vjp
def bdot(a, b):
    return _dg(a.astype(bf16), b.astype(bf16), 1, 0)


def _bdot_f(a, b):
    return bdot(a, b), (a, b)


def _bdot_b(res, g):
    a, b = res
    gb = g.astype(bf16)
    return _dg(gb, b.astype(bf16), 1, 1), _dg(a.astype(bf16), gb, 0, 0)


bdot.defvjp(_bdot_f, _bdot_b)


@jax.custom_vjp
def bdot_nt(a, b):
    return _dg(a.astype(bf16), b.astype(bf16), 1, 1)


def _bdot_nt_f(a, b):
    return bdot_nt(a, b), (a, b)


def _bdot_nt_b(res, g):
    a, b = res
    gb = g.astype(bf16)
    return _dg(gb, b.astype(bf16), 1, 0), _dg(gb, a.astype(bf16), 0, 0)


bdot_nt.defvjp(_bdot_nt_f, _bdot_nt_b)


def _shift_rows(x, s, down):
    n = x.shape[0]
    row = lax.broadcasted_iota(jnp.int32, x.shape, 0)
    if down:
        return jnp.where(row >= s, pltpu.roll(x, s, 0), 0.0)
    return jnp.where(row < n - s, pltpu.roll(x, n - s, 0), 0.0)


@functools.partial(jax.custom_vjp, nondiff_argnums=(1,))
def shift_down(x, s):
    return _shift_rows(x, s, True)


def _sd_f(x, s):
    return _shift_rows(x, s, True), None


def _sd_b(s, _, g):
    return (_shift_rows(g, s, False),)


shift_down.defvjp(_sd_f, _sd_b)


def _sigmoid(x):
    return 1.0 / (1.0 + jnp.exp(-x))


def _silu(x):
    return x * _sigmoid(x)


def _rms(x, g):
    return x * lax.rsqrt(jnp.mean(x * x, axis=-1, keepdims=True) + EPS) * g


def _tile(n, cap):
    u = n // LANE
    best = 1
    for d in range(1, u + 1):
        if u % d == 0 and d * LANE <= cap:
            best = d
    return best * LANE


def mm_nn(a, w, res=None, out_dtype=f32, name="mm_nn"):
    M, K = a.shape
    N = w.shape[1]
    tm, tn = min(MM_ROWS, M), _tile(N, 1024)

    def body(*refs):
        if res is None:
            a_ref, w_ref, o_ref = refs
            o_ref[...] = _dg(a_ref[...].astype(bf16), w_ref[...], 1, 0).astype(out_dtype)
        else:
            a_ref, w_ref, r_ref, o_ref = refs
            o_ref[...] = (r_ref[...] + _dg(a_ref[...].astype(bf16), w_ref[...], 1, 0)).astype(out_dtype)

    in_specs = [pl.BlockSpec((tm, K), lambda n, m: (m, 0)), pl.BlockSpec((K, tn), lambda n, m: (0, n))]
    args = [a, w]
    if res is not None:
        in_specs.append(pl.BlockSpec((tm, tn), lambda n, m: (m, n)))
        args.append(res)
    return pl.pallas_call(
        body, name=name, grid=(N // tn, M // tm), in_specs=in_specs,
        out_specs=pl.BlockSpec((tm, tn), lambda n, m: (m, n)),
        out_shape=jax.ShapeDtypeStruct((M, N), out_dtype),
        compiler_params=_cp(("parallel", "parallel")),
    )(*args)


def mm_nt(dy, w, out_dtype=f32, name="mm_nt"):
    M, N = dy.shape
    K = w.shape[0]
    tm, tn = min(MM_ROWS, M), _tile(N, 1024)
    assert out_dtype == f32 or tn == N

    def body(dy_ref, w_ref, o_ref):
        part = _dg(dy_ref[...].astype(bf16), w_ref[...], 1, 1)
        if tn == N:
            o_ref[...] = part.astype(out_dtype)
        else:
            @pl.when(pl.program_id(1) == 0)
            def _():
                o_ref[...] = jnp.zeros_like(o_ref)
            o_ref[...] += part

    return pl.pallas_call(
        body, name=name, grid=(M // tm, N // tn),
        in_specs=[pl.BlockSpec((tm, tn), lambda m, n: (m, n)), pl.BlockSpec((K, tn), lambda m, n: (0, n))],
        out_specs=pl.BlockSpec((tm, K), lambda m, n: (m, 0)),
        out_shape=jax.ShapeDtypeStruct((M, K), out_dtype),
        compiler_params=_cp(("parallel", "arbitrary")),
    )(dy, w)


NORM_ROWS = 1024


def _acc_then_norm_bwd(part, steps, h_ref, g_ref, r_ref, o_ref, dg_ref):
    k = pl.program_id(1)

    @pl.when((pl.program_id(0) == 0) & (k == 0))
    def _():
        dg_ref[...] = jnp.zeros_like(dg_ref)

    @pl.when(k == 0)
    def _():
        o_ref[...] = part

    @pl.when(k > 0)
    def _():
        o_ref[...] += part

    @pl.when(k == steps - 1)
    def _():
        _, vjp = jax.vjp(_rms, h_ref[...], g_ref[...])
        dh, dg = vjp(o_ref[...])
        o_ref[...] = r_ref[...] + dh
        dg_ref[...] += dg


def mm_nt_norm(dy, w, norm, name):
    M, N = dy.shape
    tm, tn = min(NORM_ROWS, M), _tile(N, 1024)

    def body(dy_ref, w_ref, h_ref, g_ref, r_ref, o_ref, dg_ref):
        _acc_then_norm_bwd(_dg(dy_ref[...].astype(bf16), w_ref[...], 1, 1), N // tn, h_ref, g_ref, r_ref, o_ref, dg_ref)

    tok = pl.BlockSpec((tm, D), lambda m, n: (m, 0))
    vec = pl.BlockSpec((1, D), lambda m, n: (0, 0))
    return pl.pallas_call(
        body, name=name, grid=(M // tm, N // tn),
        in_specs=[pl.BlockSpec((tm, tn), lambda m, n: (m, n)), pl.BlockSpec((D, tn), lambda m, n: (0, n)), tok, vec, tok],
        out_specs=[tok, vec],
        out_shape=[jax.ShapeDtypeStruct((M, D), f32), jax.ShapeDtypeStruct((1, D), f32)],
        compiler_params=_cp(("arbitrary", "arbitrary")),
    )(dy, w, norm[0], norm[1].reshape(1, D), norm[2])


def mm_tn(a, dy, name="mm_tn"):
    M, K = a.shape
    N = dy.shape[1]
    tm, tk, tn = min(MM_ROWS, M), _tile(K, 1408), _tile(N, 1024)

    def body(a_ref, dy_ref, o_ref):
        @pl.when(pl.program_id(2) == 0)
        def _():
            o_ref[...] = jnp.zeros_like(o_ref)
        o_ref[...] += _dg(a_ref[...].astype(bf16), dy_ref[...].astype(bf16), 0, 0)

    return pl.pallas_call(
        body, name=name, grid=(K // tk, N // tn, M // tm),
        in_specs=[pl.BlockSpec((tm, tk), lambda k, n, m: (m, k)), pl.BlockSpec((tm, tn), lambda k, n, m: (m, n))],
        out_specs=pl.BlockSpec((tk, tn), lambda k, n, m: (k, n)),
        out_shape=jax.ShapeDtypeStruct((K, N), f32),
        compiler_params=_cp(("parallel", "parallel", "arbitrary")),
    )(a, dy)


def rms_fwd(h, g, name):
    S = h.shape[0]
    t = min(512, S)

    def body(h_ref, g_ref, o_ref):
        o_ref[...] = _rms(h_ref[...], g_ref[...]).astype(bf16)

    return pl.pallas_call(
        body, name=name, grid=(S // t,),
        in_specs=[pl.BlockSpec((t, D), lambda i: (i, 0)), pl.BlockSpec((1, D), lambda i: (0, 0))],
        out_specs=pl.BlockSpec((t, D), lambda i: (i, 0)),
        out_shape=jax.ShapeDtypeStruct((S, D), bf16),
        compiler_params=_cp(("parallel",)),
    )(h, g.reshape(1, D))


def loss_head(h, g, target):
    S = h.shape[0]
    t = min(512, S)

    def f(hh, gg, tt):
        err = _rms(hh, gg) - tt
        return 0.5 * jnp.sum(jnp.mean(err * err, axis=-1, keepdims=True), axis=0, keepdims=True)

    def body(h_ref, g_ref, t_ref, loss_ref, dh_ref, dg_ref):
        @pl.when(pl.program_id(0) == 0)
        def _():
            dg_ref[...] = jnp.zeros_like(dg_ref)
            loss_ref[...] = jnp.zeros_like(loss_ref)
        val, vjp = jax.vjp(lambda a, b: f(a, b, t_ref[...]), h_ref[...], g_ref[...])
        dh, dg = vjp(jnp.ones((1, 1), f32))
        dh_ref[...] = dh
        dg_ref[...] += dg
        loss_ref[...] += jnp.broadcast_to(val, loss_ref.shape)

    tok = pl.BlockSpec((t, D), lambda i: (i, 0))
    vec = pl.BlockSpec((1, D), lambda i: (0, 0))
    return pl.pallas_call(
        body, name="loss_head", grid=(S // t,), in_specs=[tok, vec, tok],
        out_specs=[pl.BlockSpec((1, LANE), lambda i: (0, 0)), tok, vec],
        out_shape=[jax.ShapeDtypeStruct((1, LANE), f32), jax.ShapeDtypeStruct((S, D), f32),
                   jax.ShapeDtypeStruct((1, D), f32)],
        compiler_params=_cp(("arbitrary",)),
    )(h, g.reshape(1, D), target)


def memkv_fwd(mem, g, w, name):
    def body(m_ref, g_ref, w_ref, o_ref):
        o_ref[...] = _dg(_rms(m_ref[...], g_ref[...]).astype(bf16), w_ref[...], 1, 0)

    return pl.pallas_call(
        body, name=name, out_shape=jax.ShapeDtypeStruct((MEM_LEN, 2 * X_Q), f32), compiler_params=_cp(),
    )(mem, g.reshape(1, D), w)


def memkv_bwd(mem, g, w, dkv, name):
    def body(m_ref, g_ref, w_ref, d_ref, dg_ref, dw_ref):
        n, vjp = jax.vjp(lambda gg: _rms(m_ref[...], gg), g_ref[...])
        db = d_ref[...].astype(bf16)
        dw_ref[...] = _dg(n.astype(bf16), db, 0, 0)
        dg_ref[...] = vjp(_dg(db, w_ref[...], 1, 1))[0]

    return pl.pallas_call(
        body, name=name,
        out_shape=[jax.ShapeDtypeStruct((1, D), f32), jax.ShapeDtypeStruct((D, 2 * X_Q), f32)],
        compiler_params=_cp(),
    )(mem, g.reshape(1, D), w, dkv)


def _xattn_f(xq, mk, mv):
    lane = lax.broadcasted_iota(jnp.int32, (1, X_Q), 1)
    out = jnp.zeros(xq.shape, f32)
    for hd in range(4):
        msk = (lane // 64 == hd).astype(f32)
        s = bdot_nt(xq * msk, mk) * (64 ** -0.5)
        m = lax.stop_gradient(jnp.max(s, axis=-1, keepdims=True))
        p = jnp.exp(s - m)
        p = p / jnp.sum(p, axis=-1, keepdims=True)
        out = out + bdot(p, mv * msk)
    return out


def xattn_fwd(proj, col, kv, name):
    S = proj.shape[0]
    t = min(512, S)
    cb = col // X_Q

    def body(q_ref, k_ref, v_ref, o_ref):
        o_ref[...] = _xattn_f(q_ref[...], k_ref[...], v_ref[...]).astype(bf16)

    return pl.pallas_call(
        body, name=name, grid=(S // t,),
        in_specs=[pl.BlockSpec((t, X_Q), lambda i: (i, cb)), pl.BlockSpec((MEM_LEN, X_Q), lambda i: (0, 0)),
                  pl.BlockSpec((MEM_LEN, X_Q), lambda i: (0, 1))],
        out_specs=pl.BlockSpec((t, X_Q), lambda i: (i, 0)),
        out_shape=jax.ShapeDtypeStruct((S, X_Q), bf16),
        compiler_params=_cp(("parallel",)),
    )(proj, kv, kv)


def xattn_bwd(proj, col, kv, dmix, name):
    S = proj.shape[0]
    t = min(512, S)
    cb = col // X_Q

    def body(q_ref, k_ref, v_ref, do_ref, dq_ref, dk_ref, dv_ref):
        @pl.when(pl.program_id(0) == 0)
        def _():
            dk_ref[...] = jnp.zeros_like(dk_ref)
            dv_ref[...] = jnp.zeros_like(dv_ref)
        _, vjp = jax.vjp(_xattn_f, q_ref[...], k_ref[...], v_ref[...])
        dq, dk, dv = vjp(do_ref[...])
        dq_ref[...] = dq.astype(bf16)
        dk_ref[...] += dk
        dv_ref[...] += dv

    kvb = pl.BlockSpec((MEM_LEN, X_Q), lambda i: (0, 0))
    dq, dk, dv = pl.pallas_call(
        body, name=name, grid=(S // t,),
        in_specs=[pl.BlockSpec((t, X_Q), lambda i: (i, cb)), kvb,
                  pl.BlockSpec((MEM_LEN, X_Q), lambda i: (0, 1)), pl.BlockSpec((t, X_Q), lambda i: (i, 3))],
        out_specs=[pl.BlockSpec((t, X_Q), lambda i: (i, 0)), kvb, kvb],
        out_shape=[jax.ShapeDtypeStruct((S, X_Q), bf16), jax.ShapeDtypeStruct((MEM_LEN, X_Q), f32),
                   jax.ShapeDtypeStruct((MEM_LEN, X_Q), f32)],
        compiler_params=_cp(("arbitrary",)),
    )(proj, kv, kv, dmix)
    return dq, jnp.concatenate([dk, dv], axis=1)


def _bucket_map():
    qi = np.arange(BLK)[:, None]
    kj = np.arange(2 * BLK)[None, :]
    n = np.maximum(BLK + qi - kj, 0)
    max_exact = N_BUCKETS // 2
    nf = np.maximum(n, 1).astype(np.float64)
    large = max_exact + (np.log(nf / max_exact) / math.log(MAX_DIST / max_exact)
                         * (N_BUCKETS - max_exact)).astype(np.int32)
    large = np.minimum(large, N_BUCKETS - 1)
    return np.where(n < max_exact, n, large).astype(np.int32)


def bias_build(rel_bias):
    def body(rb_ref, bk_ref, o_ref):
        bk = bk_ref[...]
        for h in range(A_HEADS):
            acc = jnp.zeros((BLK, 2 * BLK), f32)
            for b in range(N_BUCKETS):
                acc = jnp.where(bk == b, rb_ref[b, h], acc)
            o_ref[h] = acc

    return pl.pallas_call(
        body, name="bias_build",
        in_specs=[pl.BlockSpec(memory_space=pltpu.SMEM), pl.BlockSpec(memory_space=pltpu.VMEM)],
        out_specs=pl.BlockSpec(memory_space=pltpu.VMEM),
        out_shape=jax.ShapeDtypeStruct((A_HEADS, BLK, 2 * BLK), f32), compiler_params=_cp(),
    )(rel_bias, jnp.asarray(_bucket_map()))


def bias_grad(dbias):
    def body(d_ref, bk_ref, o_ref):
        bk = bk_ref[...]
        row = lax.broadcasted_iota(jnp.int32, (N_BUCKETS, LANE), 0)
        lane = lax.broadcasted_iota(jnp.int32, (N_BUCKETS, LANE), 1)
        acc = jnp.zeros((N_BUCKETS, LANE), f32)
        for h in range(A_HEADS):
            d = d_ref[h]
            for b in range(N_BUCKETS):
                s = jnp.sum(jnp.where(bk == b, d, 0.0), keepdims=True)
                acc = acc + jnp.where((row == b) & (lane == h), s, 0.0)
        o_ref[...] = acc

    return pl.pallas_call(
        body, name="bias_grad", out_shape=jax.ShapeDtypeStruct((N_BUCKETS, LANE), f32), compiler_params=_cp(),
    )(dbias, jnp.asarray(_bucket_map()))


def _swa_f(qb, kp, kc, vp, vc, bias, sk, first):
    kband = jnp.concatenate([kp, kc], axis=0)
    vband = jnp.concatenate([vp, vc], axis=0)
    qi = lax.broadcasted_iota(jnp.int32, (BLK, 2 * BLK), 0)
    kj = lax.broadcasted_iota(jnp.int32, (BLK, 2 * BLK), 1)
    rel = kj - qi
    ok = (rel >= 1) & (rel <= BLK) & ((kj >= BLK) | jnp.logical_not(first))
    lane = lax.broadcasted_iota(jnp.int32, (1, LANE), 1)
    lane_b = lax.broadcasted_iota(jnp.int32, (BLK, LANE), 1)
    outs = []
    for p in range(A_HEADS // 2):
        qp = qb[:, LANE * p:LANE * (p + 1)]
        acc = jnp.zeros((BLK, LANE), f32)
        for g in range(2):
            h = g * (A_HEADS // 2) + p
            msk = (lane // A_DH == g).astype(f32)
            s = bdot_nt(qp * msk, kband) * (A_DH ** -0.5) + bias[h]
            s = jnp.where(ok, s, -1e30)
            skb = jnp.broadcast_to(sk[h:h + 1, :], (BLK, LANE))
            sink = jnp.sum(jnp.where(lane_b == 0, skb, 0.0), axis=-1, keepdims=True)
            m = lax.stop_gradient(jnp.maximum(jnp.max(s, axis=-1, keepdims=True), sink))
            e = jnp.exp(s - m)
            prob = e / (jnp.sum(e, axis=-1, keepdims=True) + jnp.exp(sink - m))
            acc = acc + bdot(prob, vband) * msk
        outs.append(acc)
    return jnp.concatenate(outs, axis=1)


def _swa_specs(nb, rev):
    bi = (lambda i: nb - 1 - i) if rev else (lambda i: i)
    return [
        pl.BlockSpec((BLK, A_Q), lambda i: (bi(i), 0)),
        pl.BlockSpec((BLK, LANE), lambda i: (jnp.maximum(bi(i) - 1, 0), 6)),
        pl.BlockSpec((BLK, LANE), lambda i: (bi(i), 6)),
        pl.BlockSpec((BLK, LANE), lambda i: (jnp.maximum(bi(i) - 1, 0), 7)),
        pl.BlockSpec((BLK, LANE), lambda i: (bi(i), 7)),
        pl.BlockSpec((A_HEADS, BLK, 2 * BLK), lambda i: (0, 0, 0)),
        pl.BlockSpec((16, LANE), lambda i: (0, 0)),
    ]


def swa_fwd(proj, bias, sk):
    S = proj.shape[0]
    nb = S // BLK

    def body(q_ref, kp_ref, kc_ref, vp_ref, vc_ref, b_ref, s_ref, o_ref):
        o_ref[...] = _swa_f(q_ref[...], kp_ref[...], kc_ref[...], vp_ref[...], vc_ref[...], b_ref[...], s_ref[...],
                            pl.program_id(0) == 0).astype(bf16)

    return pl.pallas_call(
        body, name="swa_fwd", grid=(nb,), in_specs=_swa_specs(nb, False),
        out_specs=pl.BlockSpec((BLK, A_Q), lambda i: (i, 0)),
        out_shape=jax.ShapeDtypeStruct((S, A_Q), bf16), compiler_params=_cp(("parallel",)),
    )(proj, proj, proj, proj, proj, bias, sk)


def swa_bwd(proj, bias, sk, dmix):
    S = proj.shape[0]
    nb = S // BLK

    def body(q_ref, kp_ref, kc_ref, vp_ref, vc_ref, b_ref, s_ref, do_ref, dqkv_ref, db_ref, ds_ref, ck, cv):
        i = pl.program_id(0)

        @pl.when(i == 0)
        def _():
            db_ref[...] = jnp.zeros_like(db_ref)
            ds_ref[...] = jnp.zeros_like(ds_ref)
            ck[...] = jnp.zeros_like(ck)
            cv[...] = jnp.zeros_like(cv)
        first = i == nb - 1
        _, vjp = jax.vjp(lambda *a: _swa_f(*a, first), q_ref[...], kp_ref[...], kc_ref[...], vp_ref[...],
                         vc_ref[...], b_ref[...], s_ref[...])
        dq, dkp, dkc, dvp, dvc, db, ds = vjp(do_ref[...])
        dqkv_ref[...] = jnp.concatenate([dq, dkc + ck[...], dvc + cv[...]], axis=1).astype(bf16)
        ck[...] = dkp
        cv[...] = dvp
        db_ref[...] += db
        ds_ref[...] += ds

    return pl.pallas_call(
        body, name="swa_bwd", grid=(nb,),
        in_specs=_swa_specs(nb, True) + [pl.BlockSpec((BLK, A_Q), lambda i: (nb - 1 - i, 0))],
        out_specs=[pl.BlockSpec((BLK, D), lambda i: (nb - 1 - i, 0)),
                   pl.BlockSpec((A_HEADS, BLK, 2 * BLK), lambda i: (0, 0, 0)),
                   pl.BlockSpec((16, LANE), lambda i: (0, 0))],
        out_shape=[jax.ShapeDtypeStruct((S, D), bf16), jax.ShapeDtypeStruct((A_HEADS, BLK, 2 * BLK), f32),
                   jax.ShapeDtypeStruct((16, LANE), f32)],
        scratch_shapes=[pltpu.VMEM((BLK, LANE), f32), pltpu.VMEM((BLK, LANE), f32)],
        compiler_params=_cp(("arbitrary",)),
    )(proj, proj, proj, proj, proj, bias, sk, dmix)


def _dnprep_f(xext, w, is_qk):
    c = (w[3:4] * xext + w[2:3] * shift_down(xext, 1) + w[1:2] * shift_down(xext, 2) + w[0:1] * shift_down(xext, 3))
    a = _silu(c)[HALO:]
    n = a * lax.rsqrt(jnp.sum(a * a, axis=-1, keepdims=True) + EPS)
    return jnp.where(is_qk, n, a)


def dnprep_fwd(proj, cw):
    S = proj.shape[0]
    nblk = B_QKV // LANE
    T = S

    def body(x_ref, w_ref, o_ref):
        is_qk = pl.program_id(0) < 2 * B_QK // LANE
        wv = w_ref[...]

        def tile(r0, first):
            o_ref[pl.ds(r0, T), :] = _dnprep_f(_glu_gext(x_ref, r0, first, T), wv, is_qk)

        tile(0, True)

    return pl.pallas_call(
        body, name="dnprep_fwd", grid=(nblk,),
        in_specs=[pl.BlockSpec((S, LANE), lambda j: (0, j)), pl.BlockSpec((4, LANE), lambda j: (0, j))],
        out_specs=pl.BlockSpec((S, LANE), lambda j: (0, j)),
        out_shape=jax.ShapeDtypeStruct((S, B_QKV), f32), compiler_params=_cp(("parallel",)),
    )(proj, cw)


def dnprep_bwd(proj, cw, dqkvn):
    S = proj.shape[0]
    nblk = B_QKV // LANE

    T = S

    def body(x_ref, w_ref, d_ref, dx_ref, dw_ref):
        is_qk = pl.program_id(0) < 2 * B_QK // LANE
        wv = w_ref[...]

        def tile(r0, first):
            _, vjp = jax.vjp(lambda a, b: _dnprep_f(a, b, is_qk), _glu_gext(x_ref, r0, first, T), wv)
            dx, dw = vjp(d_ref[pl.ds(r0, T), :])
            dx_ref[pl.ds(r0, T), :] = dx[HALO:].astype(bf16)
            if not first:
                dx_ref[pl.ds(r0 - HALO, HALO), :] += dx[:HALO]
            return dw

        dw_ref[...] = tile(0, True)

    col = pl.BlockSpec((S, LANE), lambda j: (0, j))
    wsp = pl.BlockSpec((4, LANE), lambda j: (0, j))
    return pl.pallas_call(
        body, name="dnprep_bwd", grid=(nblk,), in_specs=[col, wsp, col], out_specs=[col, wsp],
        out_shape=[jax.ShapeDtypeStruct((S, B_QKV), bf16), jax.ShapeDtypeStruct((4, B_QKV), f32)],
        compiler_params=_cp(("parallel",)),
    )(proj, cw, dqkvn)


def _hdot(a, b, ca=1, cb=0):
    return _dg(a, b, ca, cb, HI)


def _bdg(a, b, ca, cb):
    dn = (((ca,), (cb,)), ((0,), (0,)))
    ah, bh = a.astype(bf16), b.astype(bf16)
    al, bl = (a - ah.astype(f32)).astype(bf16), (b - bh.astype(f32)).astype(bf16)
    return (lax.dot_general(ah, bh, dn, preferred_element_type=f32)
            + lax.dot_general(ah, bl, dn, preferred_element_type=f32)
            + lax.dot_general(al, bh, dn, preferred_element_type=f32))


@jax.custom_vjp
def hbd(a, b):
    return _bdg(a, b, 2, 1)


@jax.custom_vjp
def hbd_nt(a, b):
    return _bdg(a, b, 2, 2)


@jax.custom_vjp
def hbd_tn(a, b):
    return _bdg(a, b, 1, 1)


hbd.defvjp(lambda a, b: (hbd(a, b), (a, b)), lambda r, g: (hbd_nt(g, r[1]), hbd_tn(r[0], g)))
hbd_nt.defvjp(lambda a, b: (hbd_nt(a, b), (a, b)), lambda r, g: (hbd(g, r[1]), hbd_tn(g, r[0])))
hbd_tn.defvjp(lambda a, b: (hbd_tn(a, b), (a, b)), lambda r, g: (hbd_nt(r[1], g), hbd(r[0], g)))


def _stack(xs):
    return jnp.concatenate([x[None] for x in xs], axis=0)


def _lane_col(x, j):
    lane = lax.broadcasted_iota(jnp.int32, (1, LANE), 1)
    return jnp.sum(jnp.where(lane == j, x, 0.0), axis=-1, keepdims=True)


def _tri_inv(a_mat):
    r = lax.broadcasted_iota(jnp.int32, (1, CHUNK, CHUNK), 1)
    c = lax.broadcasted_iota(jnp.int32, (1, CHUNK, CHUNK), 2)
    pw = -a_mat
    inv = (r == c).astype(f32) + pw
    for _ in range(5):
        pw = hbd(pw, pw)
        inv = inv + hbd(inv, pw)
    return inv


@jax.custom_vjp
def _tri_inv_known(a_mat, inv):
    return inv


_tri_inv_known.defvjp(lambda a, inv: (inv, inv),
                      lambda inv, g: (-hbd_tn(inv, hbd_nt(g, inv)), jnp.zeros_like(inv)))


def _dnc_f(q, k, v, seg, prm, inverse=_tri_inv):
    C = CHUNK
    B = q.shape[0]
    rows = seg.shape[0]
    beta_all = _sigmoid(seg)
    xx = seg + prm[1:2]
    g_all = -jnp.exp(prm[0:1]) * (jnp.maximum(xx, 0.0) + jnp.log(1.0 + jnp.exp(-jnp.abs(xx))))
    r2 = lax.broadcasted_iota(jnp.int32, (rows, rows), 0)
    c2 = lax.broadcasted_iota(jnp.int32, (rows, rows), 1)
    within = (r2 >= c2) & (r2 // C == c2 // C)
    gc_all = _hdot(within.astype(f32), g_all)
    beta = _stack([_lane_col(beta_all[C * j:C * (j + 1)], h) for j in range(rows // C) for h in range(6)])
    gc = _stack([_lane_col(gc_all[C * j:C * (j + 1)], 6 + h) for j in range(rows // C) for h in range(6)])
    r = lax.broadcasted_iota(jnp.int32, (1, C, C), 1)
    c = lax.broadcasted_iota(jnp.int32, (1, C, C), 2)
    incl = r >= c
    strict = r > c
    gct = [gc_all[C * j:C * (j + 1)].T for j in range(rows // C)]
    g_row = _stack([jnp.broadcast_to(gct[j][6 + h:7 + h, :], (C, C))
                    for j in range(rows // C) for h in range(6)])
    decay = jnp.where(incl, jnp.exp(jnp.where(incl, gc - g_row, 0.0)), 0.0)
    a_mat = beta * sbd_nt(k, k) * jnp.where(strict, decay, 0.0)
    eg = jnp.exp(gc)
    inv = inverse(a_mat)
    u = hbd(inv, beta * v)
    w = hbd(inv, (beta * eg) * k)
    qc = q * (B_DH ** -0.5)
    attn = sbd_nt(qc, k) * decay
    last = (lax.broadcasted_iota(jnp.int32, (1, C, 1), 1) == C - 1).astype(f32)
    g_last = jnp.sum(gc * last, axis=1, keepdims=True)
    dc = jnp.broadcast_to(jnp.exp(g_last), (B, 1, LANE)).reshape(B, LANE)
    return u, w, qc * eg, k * jnp.exp(g_last - gc), attn, dc, inv


def _b1(a, b, ca, cb):
    return lax.dot_general(a.astype(bf16), b.astype(bf16), (((ca,), (cb,)), ((0,), (0,))), preferred_element_type=f32)


@jax.custom_vjp
def sbd(a, b):
    return _b1(a, b, 2, 1)


@jax.custom_vjp
def sbd_nt(a, b):
    return _b1(a, b, 2, 2)


@jax.custom_vjp
def sbd_tn(a, b):
    return _b1(a, b, 1, 1)


sbd.defvjp(lambda a, b: (sbd(a, b), (a, b)), lambda r, g: (sbd_nt(g, r[1]), sbd_tn(r[0], g)))
sbd_nt.defvjp(lambda a, b: (sbd_nt(a, b), (a, b)), lambda r, g: (sbd(g, r[1]), sbd_tn(g, r[0])))
sbd_tn.defvjp(lambda a, b: (sbd_tn(a, b), (a, b)), lambda r, g: (sbd_nt(r[1], g), sbd(r[0], g)))


def _dns_f(S0, u, w, qd, kt, attn, dcrows):
    dc = _lane_col(dcrows, 0).reshape(6, 1, 1)
    delta = u - sbd(w, S0)
    out = sbd(qd, S0) + sbd(attn, delta)
    return out, dc * S0 + sbd_tn(kt, delta)


def _dnpost_f(o, z, grow):
    outs = []
    for h in range(6):
        oh = o[:, LANE * h:LANE * (h + 1)]
        outs.append(oh * lax.rsqrt(jnp.mean(oh * oh, axis=-1, keepdims=True) + EPS) * grow
                    * _silu(z[:, LANE * h:LANE * (h + 1)]))
    return jnp.concatenate(outs, axis=1)


def _hs(h):
    return slice(LANE * h, LANE * (h + 1))


DN_CHUNKS = 4


def _heads(ref, share):
    return _stack([ref[CHUNK * j:CHUNK * (j + 1), _hs(h // share)]
                   for j in range(ref.shape[0] // CHUNK) for h in range(6)])


def _put_heads(ref, val):
    for j in range(ref.shape[0] // CHUNK):
        for h in range(6):
            ref[CHUNK * j:CHUNK * (j + 1), _hs(h)] = val[6 * j + h]


def _dnc_in_specs():
    rows = CHUNK * DN_CHUNKS
    return [
        pl.BlockSpec((rows, B_QK), lambda n: (n, 0)),
        pl.BlockSpec((rows, B_QK), lambda n: (n, 1)),
        pl.BlockSpec((rows, B_V), lambda n: (n, 1)),
        pl.BlockSpec((rows, LANE), lambda n: (n, 20)),
        pl.BlockSpec((8, LANE), lambda n: (0, 0)),
    ]


def _dnc_out_specs(rev_nc=None, chunks=1):
    ci = (lambda n: n) if rev_nc is None else (lambda n: rev_nc - 1 - n)
    wide = pl.BlockSpec((CHUNK * chunks, B_V), lambda n: (ci(n), 0))
    return [wide, wide, wide, wide, pl.BlockSpec((chunks, 6, CHUNK, CHUNK), lambda n: (ci(n), 0, 0, 0)),
            pl.BlockSpec((chunks, 8, LANE), lambda n: (ci(n), 0, 0))]


def _dc_rows(dc):
    pad = jnp.zeros((2, LANE), f32)
    return _stack([jnp.concatenate([dc[6 * j:6 * (j + 1)], pad], axis=0) for j in range(dc.shape[0] // 6)])


def _dnc_shapes(S):
    nc = S // CHUNK
    wide = jax.ShapeDtypeStruct((S, B_V), f32)
    return [wide, wide, wide, wide, jax.ShapeDtypeStruct((nc, 6, CHUNK, CHUNK), f32),
            jax.ShapeDtypeStruct((nc, 8, LANE), f32)]


def dnc_fwd(qkvn, proj, prm):
    S = proj.shape[0]

    def body(q_ref, k_ref, v_ref, s_ref, p_ref, u_ref, w_ref, qd_ref, kt_ref, at_ref, dc_ref, inv_ref):
        u, w, qd, kt, attn, dc, inv = _dnc_f(_heads(q_ref, 2), _heads(k_ref, 2), _heads(v_ref, 1), s_ref[...],
                                             p_ref[...])
        inv_ref[...] = inv.reshape(inv_ref.shape)
        _put_heads(u_ref, u)
        _put_heads(w_ref, w)
        _put_heads(qd_ref, qd)
        _put_heads(kt_ref, kt)
        at_ref[...] = attn.reshape(at_ref.shape)
        dc_ref[...] = _dc_rows(dc)

    outs = _dnc_out_specs(chunks=DN_CHUNKS)
    out = pl.pallas_call(
        body, name="dn_chunk_fwd", grid=(S // (CHUNK * DN_CHUNKS),), in_specs=_dnc_in_specs(),
        out_specs=outs + [outs[4]], out_shape=_dnc_shapes(S) + [_dnc_shapes(S)[4]],
        compiler_params=_cp(("parallel",)),
    )(qkvn, qkvn, qkvn, proj, prm)
    return out[:6], out[6]


def dnc_bwd(qkvn, proj, prm, inv, cots):
    S = proj.shape[0]

    def body(q_ref, k_ref, v_ref, s_ref, p_ref, inv_ref, du_ref, dw_ref, dqd_ref, dkt_ref, dat_ref, ddc_ref,
             dx_ref, dseg_ref, dprm_ref):
        @pl.when(pl.program_id(0) == 0)
        def _():
            dprm_ref[...] = jnp.zeros_like(dprm_ref)
        nb = 6 * DN_CHUNKS
        known = functools.partial(_tri_inv_known, inv=inv_ref[...].reshape(nb, CHUNK, CHUNK))
        _, vjp = jax.vjp(lambda *a: _dnc_f(*a, inverse=known)[:6], _heads(q_ref, 2), _heads(k_ref, 2),
                         _heads(v_ref, 1), s_ref[...], p_ref[...])
        ddc = jnp.concatenate([ddc_ref[j, 0:6, :] for j in range(DN_CHUNKS)], axis=0)
        dq, dk, dv, dseg, dprm = vjp((_heads(du_ref, 1), _heads(dw_ref, 1), _heads(dqd_ref, 1), _heads(dkt_ref, 1),
                                      dat_ref[...].reshape(nb, CHUNK, CHUNK), ddc))
        for j in range(DN_CHUNKS):
            o = 6 * j
            dx_ref[CHUNK * j:CHUNK * (j + 1), :] = jnp.concatenate(
                [dq[o] + dq[o + 1], dq[o + 2] + dq[o + 3], dq[o + 4] + dq[o + 5],
                 dk[o] + dk[o + 1], dk[o + 2] + dk[o + 3], dk[o + 4] + dk[o + 5]] + [dv[o + h] for h in range(6)], axis=1)
        dseg_ref[...] = dseg.astype(bf16)
        dprm_ref[...] += dprm

    rows = CHUNK * DN_CHUNKS
    outs = _dnc_out_specs(chunks=DN_CHUNKS)
    return pl.pallas_call(
        body, name="dn_chunk_bwd", grid=(S // rows,),
        in_specs=_dnc_in_specs() + [outs[4]] + outs,
        out_specs=[pl.BlockSpec((rows, B_QKV), lambda n: (n, 0)), pl.BlockSpec((rows, LANE), lambda n: (n, 0)),
                   pl.BlockSpec((8, LANE), lambda n: (0, 0))],
        out_shape=[jax.ShapeDtypeStruct((S, B_QKV), f32), jax.ShapeDtypeStruct((S, LANE), bf16),
                   jax.ShapeDtypeStruct((8, LANE), f32)],
        compiler_params=_cp(("arbitrary",)),
    )(qkvn, qkvn, qkvn, proj, prm, inv, *cots)


def dns_fwd(chunked):
    u = chunked[0]
    S = u.shape[0]
    nc = S // CHUNK

    def body(u_ref, w_ref, qd_ref, kt_ref, at_ref, dc_ref, o_ref, st_ref, st):
        @pl.when(pl.program_id(0) == 0)
        def _():
            st[...] = jnp.zeros_like(st)
        S0 = st[...]
        st_ref[0] = S0
        out, S1 = _dns_f(S0, _heads(u_ref, 1), _heads(w_ref, 1), _heads(qd_ref, 1), _heads(kt_ref, 1),
                         at_ref[0], dc_ref[0, 0:6, :])
        _put_heads(o_ref, out)
        st[...] = S1

    return pl.pallas_call(
        body, name="dn_scan_fwd", grid=(nc,), in_specs=_dnc_out_specs(),
        out_specs=[pl.BlockSpec((CHUNK, B_V), lambda n: (n, 0)),
                   pl.BlockSpec((1, 6, B_DH, B_DH), lambda n: (n, 0, 0, 0))],
        out_shape=[jax.ShapeDtypeStruct((S, B_V), f32), jax.ShapeDtypeStruct((nc, 6, B_DH, B_DH), f32)],
        scratch_shapes=[pltpu.VMEM((6, B_DH, B_DH), f32)],
        compiler_params=_cp(("arbitrary",)),
    )(*chunked)


def dns_bwd(chunked, states, do):
    S = do.shape[0]
    nc = S // CHUNK

    def body(u_ref, w_ref, qd_ref, kt_ref, at_ref, dc_ref, st_ref, do_ref,
             du_ref, dw_ref, dqd_ref, dkt_ref, dat_ref, ddc_ref, dst):
        @pl.when(pl.program_id(0) == 0)
        def _():
            dst[...] = jnp.zeros_like(dst)
        _, vjp = jax.vjp(_dns_f, st_ref[0], _heads(u_ref, 1), _heads(w_ref, 1), _heads(qd_ref, 1), _heads(kt_ref, 1),
                         at_ref[0], dc_ref[0, 0:6, :])
        dS0, du, dw, dqd, dkt, dat, ddc = vjp((_heads(do_ref, 1), dst[...]))
        dst[...] = dS0
        _put_heads(du_ref, du)
        _put_heads(dw_ref, dw)
        _put_heads(dqd_ref, dqd)
        _put_heads(dkt_ref, dkt)
        dat_ref[0] = dat
        ddc_ref[0] = jnp.concatenate([ddc, jnp.zeros((2, LANE), f32)], axis=0)

    return pl.pallas_call(
        body, name="dn_scan_bwd", grid=(nc,),
        in_specs=_dnc_out_specs(nc) + [pl.BlockSpec((1, 6, B_DH, B_DH), lambda n: (nc - 1 - n, 0, 0, 0)),
                                       pl.BlockSpec((CHUNK, B_V), lambda n: (nc - 1 - n, 0))],
        out_specs=_dnc_out_specs(nc), out_shape=_dnc_shapes(S),
        scratch_shapes=[pltpu.VMEM((6, B_DH, B_DH), f32)],
        compiler_params=_cp(("arbitrary",)),
    )(*chunked, states, do)


def dnpost_fwd(o, proj, prm):
    S = o.shape[0]
    t = min(512, S)

    def body(o_ref, z_ref, p_ref, y_ref):
        y_ref[...] = _dnpost_f(o_ref[...], z_ref[...], p_ref[2:3, :]).astype(bf16)

    tok = pl.BlockSpec((t, B_V), lambda i: (i, 0))
    return pl.pallas_call(
        body, name="dn_post_fwd", grid=(S // t,),
        in_specs=[tok, pl.BlockSpec((t, B_V), lambda i: (i, 2)), pl.BlockSpec((8, LANE), lambda i: (0, 0))],
        out_specs=tok, out_shape=jax.ShapeDtypeStruct((S, B_V), bf16), compiler_params=_cp(("parallel",)),
    )(o, proj, prm)


def dnpost_bwd(o, proj, prm, dmix):
    S = o.shape[0]
    t = min(512, S)

    def body(o_ref, z_ref, p_ref, dy_ref, do_ref, dz_ref, dg_ref):
        @pl.when(pl.program_id(0) == 0)
        def _():
            dg_ref[...] = jnp.zeros_like(dg_ref)
        _, vjp = jax.vjp(_dnpost_f, o_ref[...], z_ref[...], p_ref[2:3, :])
        do, dz, dg = vjp(dy_ref[...])
        do_ref[...] = do
        dz_ref[...] = dz.astype(bf16)
        dg_ref[...] += dg

    tok = pl.BlockSpec((t, B_V), lambda i: (i, 0))
    return pl.pallas_call(
        body, name="dn_post_bwd", grid=(S // t,),
        in_specs=[tok, pl.BlockSpec((t, B_V), lambda i: (i, 2)), pl.BlockSpec((8, LANE), lambda i: (0, 0)), tok],
        out_specs=[tok, tok, pl.BlockSpec((1, LANE), lambda i: (0, 0))],
        out_shape=[jax.ShapeDtypeStruct((S, B_V), f32), jax.ShapeDtypeStruct((S, B_V), bf16),
                   jax.ShapeDtypeStruct((1, LANE), f32)],
        compiler_params=_cp(("arbitrary",)),
    )(o, proj, prm, dmix)


N_FF_BLK = D_FF // LANE
GU_SHARD = 2 * D_FF // 4


GLU_ROWS = 256
HALO = 16


def _glu_f(gext, up, w, b):
    c = w[2:3] * gext + w[1:2] * shift_down(gext, 1) + w[0:1] * shift_down(gext, 2) + b
    return _silu(c)[HALO:] * up


def _glu_gext(g_ref, r0, first, T=GLU_ROWS):
    if first:
        return jnp.concatenate([jnp.zeros((HALO, LANE), f32), g_ref[0:T, :].astype(f32)], axis=0)
    return g_ref[pl.ds(r0 - HALO, T + HALO), :].astype(f32)


def glu_fwd(gu, w, b, name):
    S = gu.shape[0]
    T = min(GLU_ROWS, S // 2)

    def body(g_ref, u_ref, w_ref, b_ref, o_ref):
        wv, bv = w_ref[...], b_ref[...]

        def tile(r0, first):
            act = _glu_f(_glu_gext(g_ref, r0, first, T), u_ref[pl.ds(r0, T), :].astype(f32), wv, bv)
            o_ref[pl.ds(r0, T), :] = act.astype(bf16)

        tile(0, True)

        @pl.loop(1, S // T)
        def _(t):
            tile(pl.multiple_of(t * T, T), False)

    col = pl.BlockSpec((S, LANE), lambda j: (0, j))
    return pl.pallas_call(
        body, name=name, grid=(N_FF_BLK,),
        in_specs=[col, pl.BlockSpec((S, LANE), lambda j: (0, N_FF_BLK + j)), pl.BlockSpec((3, LANE), lambda j: (0, j)),
                  pl.BlockSpec((1, LANE), lambda j: (0, j))],
        out_specs=col, out_shape=jax.ShapeDtypeStruct((S, D_FF), bf16), compiler_params=_cp(("parallel",)),
    )(gu, gu, w, b.reshape(1, D_FF))


def glu_bwd(gu, w, b, dact, name):
    S = gu.shape[0]
    T = min(GLU_ROWS, S // 2)

    def body(g_ref, u_ref, w_ref, b_ref, d_ref, dg_ref, dw_ref, db_ref, acc):
        wv, bv = w_ref[...], b_ref[...]

        def tile(r0, first):
            _, vjp = jax.vjp(_glu_f, _glu_gext(g_ref, r0, first, T), u_ref[pl.ds(r0, T), :].astype(f32), wv, bv)
            dgx, du, dw, db = vjp(d_ref[pl.ds(r0, T), :].astype(f32))
            acc[pl.ds(r0, T), :] = dgx[HALO:]
            if not first:
                acc[pl.ds(r0 - HALO, HALO), :] += dgx[:HALO]
            dg_ref[1, pl.ds(r0, T), :] = du.astype(bf16)
            return dw, db

        dw0, db0 = tile(0, True)
        dw_ref[...] = dw0
        db_ref[...] = db0

        @pl.loop(1, S // T)
        def _(t):
            dw, db = tile(pl.multiple_of(t * T, T), False)
            dw_ref[...] += dw
            db_ref[...] += db

        dg_ref[0] = acc[...].astype(bf16)

    col = pl.BlockSpec((S, LANE), lambda j: (0, j))
    wsp = pl.BlockSpec((3, LANE), lambda j: (0, j))
    bsp = pl.BlockSpec((1, LANE), lambda j: (0, j))
    return pl.pallas_call(
        body, name=name, grid=(N_FF_BLK,),
        in_specs=[col, pl.BlockSpec((S, LANE), lambda j: (0, N_FF_BLK + j)), wsp, bsp, col],
        out_specs=[pl.BlockSpec((2, S, LANE), lambda j: (0, 0, j)), wsp, bsp],
        out_shape=[jax.ShapeDtypeStruct((2, S, D_FF), bf16), jax.ShapeDtypeStruct((3, D_FF), f32),
                   jax.ShapeDtypeStruct((1, D_FF), f32)],
        scratch_shapes=[pltpu.VMEM((S, LANE), f32)],
        compiler_params=_cp(("parallel",)),
    )(gu, gu, w, b.reshape(1, D_FF), dact)


def gu_fwd(n2, wg, name):
    S = n2.shape[0]
    tm = min(MM_ROWS, S)

    def body(a_ref, w_ref, o_ref):
        o_ref[...] = _dg(a_ref[...], w_ref[...], 1, 0).astype(bf16)

    return pl.pallas_call(
        body, name=name, grid=(4, S // tm),
        in_specs=[pl.BlockSpec((tm, D), lambda s, m: (m, 0)), pl.BlockSpec((None, D, GU_SHARD), lambda s, m: (s, 0, 0))],
        out_specs=pl.BlockSpec((tm, GU_SHARD), lambda s, m: (m, s)),
        out_shape=jax.ShapeDtypeStruct((S, 2 * D_FF), bf16), compiler_params=_cp(("parallel", "parallel")),
    )(n2, wg)


def gu_bwd_x(dgu, wg, norm, name):
    S = dgu.shape[1]
    tm = min(NORM_ROWS, S)

    def body(d_ref, w_ref, h_ref, g_ref, r_ref, o_ref, dg_ref):
        _acc_then_norm_bwd(_dg(d_ref[...], w_ref[...], 1, 1), 4, h_ref, g_ref, r_ref, o_ref, dg_ref)

    tok = pl.BlockSpec((tm, D), lambda m, s: (m, 0))
    vec = pl.BlockSpec((1, D), lambda m, s: (0, 0))
    return pl.pallas_call(
        body, name=name, grid=(S // tm, 4),
        in_specs=[pl.BlockSpec((None, tm, GU_SHARD), lambda m, s: (s // 2, m, s % 2)),
                  pl.BlockSpec((None, D, GU_SHARD), lambda m, s: (s, 0, 0)), tok, vec, tok],
        out_specs=[tok, vec],
        out_shape=[jax.ShapeDtypeStruct((S, D), f32), jax.ShapeDtypeStruct((1, D), f32)],
        compiler_params=_cp(("arbitrary", "arbitrary")),
    )(dgu, wg, norm[0], norm[1].reshape(1, D), norm[2])


def gu_bwd_w(n2, dgu, name):
    S = n2.shape[0]
    tm = min(MM_ROWS, S)
    nm = S // tm

    def body(a_ref, d_ref, o_ref, acc):
        @pl.when(pl.program_id(1) == 0)
        def _():
            acc[...] = jnp.zeros_like(acc)
        acc[...] += _dg(a_ref[...], d_ref[...], 0, 0)

        @pl.when(pl.program_id(1) == nm - 1)
        def _():
            o_ref[...] = acc[...].astype(bf16)

    return pl.pallas_call(
        body, name=name, grid=(4, nm),
        in_specs=[pl.BlockSpec((tm, D), lambda s, m: (m, 0)),
                  pl.BlockSpec((None, tm, GU_SHARD), lambda s, m: (s // 2, m, s % 2))],
        out_specs=pl.BlockSpec((None, D, GU_SHARD), lambda s, m: (s, 0, 0)),
        out_shape=jax.ShapeDtypeStruct((4, D, GU_SHARD), bf16),
        scratch_shapes=[pltpu.VMEM((D, GU_SHARD), f32)],
        compiler_params=_cp(("parallel", "arbitrary")),
    )(n2, dgu)


def _pair_cols(w):
    lead = w.shape[:-1]
    return w.reshape(lead + (2, 6, A_DH)).swapaxes(-3, -2).reshape(lead + (A_Q,))


def _unpair_cols(w):
    lead = w.shape[:-1]
    return w.reshape(lead + (6, 2, A_DH)).swapaxes(-3, -2).reshape(lead + (A_Q,))


def _lay_in_a(w):
    return jnp.concatenate([_pair_cols(w[:, :A_Q]), w[:, A_Q:]], axis=1)


def _unlay_in_a(w):
    return jnp.concatenate([_unpair_cols(w[:, :A_Q]), w[:, A_Q:]], axis=1)


def _lay_out_a(w):
    return jnp.concatenate([_pair_cols(w[:A_Q].T).T, w[A_Q:]], axis=0)


def _unlay_out_a(w):
    return jnp.concatenate([_unpair_cols(w[:A_Q].T).T, w[A_Q:]], axis=0)


def _lay_in_b(w):
    return jnp.concatenate([w[:, :2304], w[:, 2316:], w[:, 2304:2316],
                            jnp.zeros((w.shape[0], LANE - 12), w.dtype)], axis=1)


def _unlay_in_b(w):
    return jnp.concatenate([w[:, :2304], w[:, 2560:2572], w[:, 2304:2560]], axis=1)


def _chip_cols(w):
    return jnp.moveaxis(w.reshape(w.shape[0], 4, w.shape[1] // 4), 1, 0)


def _unchip_cols(w):
    return jnp.moveaxis(w, 0, 1).reshape(w.shape[1], 4 * w.shape[2])


def _local_step(x, mem, target, P):
    arrive = P.get("arrive", lambda key, after: None)
    ready = P.get("ready", lambda key, grads, dep: dep)
    sk = jnp.zeros((16, LANE), f32).at[:A_HEADS].set(jnp.broadcast_to(P["sinks"][:, None], (A_HEADS, LANE)))
    prm = jnp.zeros((8, LANE), f32).at[0, 6:12].set(P["a_log"]).at[1, 6:12].set(P["dt_bias"]).at[2].set(P["out_norm_g"])
    bias = bias_build(P["rel_bias"])
    saved = []
    h = x
    for i in range(2):
        n1 = rms_fwd(h, P["g_mix"][i], f"rms_mix{i}")
        arrive(("w_in", i), n1)
        proj = mm_nn(n1, P["w_in_a"] if i == 0 else P["w_in_b"], name="proj_a" if i == 0 else "proj_b")
        arrive(("w_mem", i), proj)
        kv = memkv_fwd(mem, P["g_mem"][i], P["w_mem"][i], f"memkv{i}")
        if i == 0:
            self_out = swa_fwd(proj, bias, sk)
            cross = xattn_fwd(proj, A_Q + 2 * LANE, kv, "xattn_a")
            extra = ()
        else:
            qkvn = dnprep_fwd(proj, P["conv_qkv"])
            chunked, inv = dnc_fwd(qkvn, proj, prm)
            o, states = dns_fwd(chunked)
            self_out = dnpost_fwd(o, proj, prm)
            cross = xattn_fwd(proj, 2304, kv, "xattn_b")
            extra = (qkvn, chunked, inv, states, o)
        mix = jnp.concatenate([self_out, cross], axis=1)
        arrive(("w_out", i), cross)
        h2 = mm_nn(mix, P["w_out"][i], res=h, name=f"out_proj{i}")
        n2 = rms_fwd(h2, P["g_ffn"][i], f"rms_ffn{i}")
        arrive(("w_gu", i), n2)
        gu = gu_fwd(n2, P["w_gu"][i], f"gate_up{i}")
        act = glu_fwd(gu, P["ffn_cw"][i], P["ffn_cb"][i], f"glu{i}")
        arrive(("w_down", i), act)
        h3 = mm_nn(act, P["w_down"][i], res=h2, name=f"down{i}")
        saved.append((h, n1, kv, proj, mix, h2, n2, gu, act, extra))
        h = h3

    loss, dh, dg_fin = loss_head(h, P["g_fin"], target)
    G = {"g_fin": dg_fin[0], "g_mix": [None, None], "g_mem": [None, None], "g_ffn": [None, None],
         "w_mem": [None, None], "w_out": [None, None], "w_gu": [None, None], "w_down": [None, None],
         "ffn_cw": [None, None], "ffn_cb": [None, None]}
    for i in (1, 0):
        hin, n1, kv, proj, mix, h2, n2, gu, act, extra = saved[i]
        dact = mm_nt(dh, P["w_down"][i], out_dtype=bf16, name=f"d_act{i}")
        G["w_down"][i] = mm_tn(act, dh, name=f"dw_down{i}")
        dgu, dcw, dcb = glu_bwd(gu, P["ffn_cw"][i], P["ffn_cb"][i], dact, f"glu_bwd{i}")
        G["ffn_cw"][i], G["ffn_cb"][i] = dcw, dcb[0]
        G["w_gu"][i] = gu_bwd_w(n2, dgu, f"dw_gu{i}")
        g_ffn = ready(("ffn", i), G, P["g_ffn"][i])
        dh2, dg = gu_bwd_x(dgu, P["w_gu"][i], (h2, g_ffn, dh), f"d_n2_{i}")
        G["g_ffn"][i] = dg[0]
        dmix = mm_nt(dh2, P["w_out"][i], name=f"d_mix{i}")
        G["w_out"][i] = mm_tn(mix, dh2, name=f"dw_out{i}")
        if i == 0:
            dqkv, dbias, dsk = swa_bwd(proj, bias, sk, dmix)
            dxq, dkv = xattn_bwd(proj, A_Q + 2 * LANE, kv, dmix, "xattn_a_bwd")
            dproj = jnp.concatenate([dqkv, dxq], axis=1)
            G["sinks"] = dsk[:A_HEADS, 0]
            G["rel_bias"] = bias_grad(dbias)[:, :A_HEADS]
            w_in, gname = P["w_in_a"], "w_in_a"
        else:
            qkvn, chunked, inv, states, o = extra
            do, dz, dgo = dnpost_bwd(o, proj, prm, dmix)
            dqkvn, dseg, dprm = dnc_bwd(qkvn, proj, prm, inv, dns_bwd(chunked, states, do))
            draw, dconv = dnprep_bwd(proj, P["conv_qkv"], dqkvn)
            dxq, dkv = xattn_bwd(proj, 2304, kv, dmix, "xattn_b_bwd")
            dproj = jnp.concatenate([draw, dz, dxq, dseg], axis=1)
            G["conv_qkv"] = dconv
            G["a_log"], G["dt_bias"], G["out_norm_g"] = dprm[0, 6:12], dprm[1, 6:12], dgo[0]
            w_in, gname = P["w_in_b"], "w_in_b"
        G[gname] = mm_tn(n1, dproj, name=f"d{gname}")
        dh, dg = mm_nt_norm(dproj, w_in, (hin, P["g_mix"][i], dh2), f"d_n1_{i}")
        G["g_mix"][i] = dg[0]
        dgm, dwm = memkv_bwd(mem, P["g_mem"][i], P["w_mem"][i], dkv, f"memkv_bwd{i}")
        G["g_mem"][i], G["w_mem"][i] = dgm[0], dwm
        ready(("mix", i), G, None)
    return loss, dh, G


def _prepare(full, w_gu=None):
    return {
        "rel_bias": full["rel_bias"], "sinks": full["sinks_a"][0], "a_log": full["a_log_b"][0],
        "dt_bias": full["dt_bias_b"][0], "out_norm_g": full["out_norm_g_b"][0],
        "g_mix": full["norm_mix_g"], "g_mem": full["norm_mem_g"], "g_ffn": full["norm_ffn_g"],
        "g_fin": full["final_norm_g"], "conv_qkv": full["conv_qkv_b"][0],
        "ffn_cw": [full["ffn_conv_w"][0], full["ffn_conv_w"][1]],
        "ffn_cb": [full["ffn_conv_b"][0], full["ffn_conv_b"][1]],
        "w_mem": [full["w_mem_kv"][0], full["w_mem_kv"][1]],
        "w_out": [_lay_out_a(full["w_out"][0]), full["w_out"][1]],
        "w_in_a": _lay_in_a(full["w_in_a"][0]), "w_in_b": _lay_in_b(full["w_in_b"][0]),
        "w_gu": w_gu if w_gu is not None else [_chip_cols(full["w_gate_up"][0]), _chip_cols(full["w_gate_up"][1])],
        "w_down": [full["w_down"][0], full["w_down"][1]],
    }


def _grads_to_ref(G):
    return {
        "rel_bias": G["rel_bias"], "norm_mix_g": jnp.stack(G["g_mix"]), "norm_mem_g": jnp.stack(G["g_mem"]),
        "w_mem_kv": jnp.stack(G["w_mem"]),
        "w_out": jnp.stack([_unlay_out_a(G["w_out"][0]), G["w_out"][1]]),
        "w_in_a": _unlay_in_a(G["w_in_a"])[None], "sinks_a": G["sinks"][None],
        "w_in_b": _unlay_in_b(G["w_in_b"])[None], "conv_qkv_b": G["conv_qkv"][None],
        "a_log_b": G["a_log"][None], "dt_bias_b": G["dt_bias"][None], "out_norm_g_b": G["out_norm_g"][None],
        "norm_ffn_g": jnp.stack(G["g_ffn"]),
        "w_gate_up": jnp.stack([_unchip_cols(G["w_gu"][0]), _unchip_cols(G["w_gu"][1])]).astype(f32),
        "ffn_conv_w": jnp.stack(G["ffn_cw"]), "ffn_conv_b": jnp.stack(G["ffn_cb"]),
        "w_down": jnp.stack(G["w_down"]), "final_norm_g": G["g_fin"],
    }


ANY = pl.BlockSpec(memory_space=pl.ANY)


def _place():
    return lax.axis_index("x"), lax.axis_index("y"), lax.axis_index("c")


def chip_scatter(gs):
    n = len(gs)

    def body(*refs):
        ins, outs = refs[:n], refs[n:2 * n]
        ssem, rsem = refs[2 * n:]
        x, y, c = _place()
        me = 2 * x + y
        peers = [(1 - x, y), (x, 1 - y), (1 - x, 1 - y)]

        def remote(j, k, slot):
            px, py = peers[k]
            return pltpu.make_async_remote_copy(
                src_ref=ins[j].at[2 * px + py], dst_ref=outs[j].at[slot],
                send_sem=ssem.at[3 * j + k], recv_sem=rsem.at[3 * j + k],
                device_id=(px, py, c), device_id_type=MESH)

        sends = [remote(j, k, me) for j in range(n) for k in range(3)]
        for cp in sends:
            cp.start()
        for j in range(n):
            for k in range(3):
                px, py = peers[k]
                remote(j, k, 2 * px + py).wait_recv()
        for cp in sends:
            cp.wait_send()

    return pl.pallas_call(
        body, name="grad_scatter", in_specs=[ANY] * n, out_specs=[ANY] * n,
        out_shape=[jax.ShapeDtypeStruct(g.shape, g.dtype) for g in gs],
        scratch_shapes=[pltpu.SemaphoreType.DMA((3 * n,)), pltpu.SemaphoreType.DMA((3 * n,))],
    )(*gs)


def allreduce_small(buf):
    R = buf.shape[0]

    def body(b_ref, o_ref, recv, ssem, rsem):
        x, y, c = _place()
        me = 4 * x + 2 * y + c

        def peer(k):
            return (1 - x if k & 4 else x, 1 - y if k & 2 else y, 1 - c if k & 1 else c)

        def remote(k, slot):
            return pltpu.make_async_remote_copy(
                src_ref=b_ref, dst_ref=recv.at[slot], send_sem=ssem.at[k - 1], recv_sem=rsem.at[k - 1],
                device_id=peer(k), device_id_type=MESH)

        sends = [remote(k, me) for k in range(1, 8)]
        for cp in sends:
            cp.start()
        recv[me] = b_ref[...]
        for k in range(1, 8):
            px, py, pc = peer(k)
            remote(k, 4 * px + 2 * py + pc).wait_recv()
        for cp in sends:
            cp.wait_send()
        total = recv[0]
        for j in range(1, 8):
            total = total + recv[j]
        o_ref[...] = total

    return pl.pallas_call(
        body, name="small_allreduce",
        in_specs=[pl.BlockSpec(memory_space=pltpu.VMEM)], out_specs=pl.BlockSpec(memory_space=pltpu.VMEM),
        out_shape=jax.ShapeDtypeStruct(buf.shape, f32),
        scratch_shapes=[pltpu.VMEM((8, R, LANE), f32), pltpu.SemaphoreType.DMA((7,)), pltpu.SemaphoreType.DMA((7,))],
    )(buf)


def sum_slots(own, recv, chip, core, name):
    _, R, C = recv.shape
    tr = _row_tile(R, 256)
    nt = R // tr

    def body(p_ref, a_ref, r_ref, o_ref):
        acc = jnp.zeros((tr, C), f32)
        for s in range(4):
            acc = acc + jnp.where(p_ref[0] == s, a_ref[s], r_ref[s]).astype(f32)
        o_ref[...] = acc

    slots = pl.BlockSpec((4, tr, C), lambda i, p_ref: (0, i, 0))
    return pl.pallas_call(
        body, name=name, out_shape=jax.ShapeDtypeStruct((2 * R, C), f32),
        grid_spec=pltpu.PrefetchScalarGridSpec(
            num_scalar_prefetch=1, grid=(nt,), in_specs=[slots, slots],
            out_specs=pl.BlockSpec((tr, C), lambda i, p_ref: (p_ref[1] * nt + i, 0))),
        compiler_params=_cp(("parallel",)),
    )(jnp.stack([chip, core]).astype(jnp.int32), own, recv)


def _half(ref, core, axis=0):
    half = ref.shape[axis] // 2
    idx = (slice(None),) * axis + (pl.ds(core * half, half),)
    return ref.at[idx]


IN_HBM = pl.BlockSpec(memory_space=pltpu.HBM)
IN_SEM = pl.BlockSpec(memory_space=pltpu.SEMAPHORE)
SIDE_EFFECT = pltpu.SideEffectType.DATAFLOW_SIDE_EFFECTING


def _gather_copy(buf, i, k, ssem, rsem, place, landing):
    x, y, c = place
    px, py = [(1 - x, y), (x, 1 - y), (1 - x, 1 - y)][k]
    me = 2 * x + y
    return pltpu.make_async_remote_copy(
        src_ref=buf.at[me], dst_ref=buf.at[me if landing == "theirs" else 2 * px + py],
        send_sem=ssem.at[3 * i + k], recv_sem=rsem.at[3 * i + k], device_id=(px, py, c), device_id_type=MESH)


def gather_start(groups):
    flat = [b for grp in groups for b in grp]
    n, ng = len(flat), len(groups)

    def body(*refs):
        bufs, sems = refs[:n], refs[n:n + 2 * ng]
        place = _place()
        j = 0
        for g, grp in enumerate(groups):
            for i in range(len(grp)):
                for k in range(3):
                    _gather_copy(bufs[j], i, k, sems[2 * g], sems[2 * g + 1], place, "theirs").start()
                j += 1

    sem_shapes = [pltpu.SemaphoreType.DMA((3 * len(grp),)) for grp in groups for _ in range(2)]
    out = pl.pallas_call(
        body, name="gather_start", in_specs=[IN_HBM] * n, out_specs=(*[IN_SEM] * (2 * ng), *[IN_HBM] * n),
        out_shape=(*sem_shapes, *[pltpu.HBM(b.shape, b.dtype) for b in flat]),
        input_output_aliases={i: 2 * ng + i for i in range(n)},
        compiler_params=pltpu.CompilerParams(has_side_effects=SIDE_EFFECT),
    )(*[pltpu.with_memory_space_constraint(b, pltpu.HBM) for b in flat])
    sems, bufs = out[:2 * ng], list(out[2 * ng:])
    flights, j = [], 0
    for g, grp in enumerate(groups):
        flights.append((bufs[j:j + len(grp)], sems[2 * g], sems[2 * g + 1]))
        j += len(grp)
    return flights


def gather_wait(flight, after, name):
    bufs, ssem, rsem = flight
    n = len(bufs)

    def body(*refs):
        place = _place()
        for i in range(n):
            for k in range(3):
                cp = _gather_copy(refs[i], i, k, refs[n], refs[n + 1], place, "mine")
                cp.wait_send()
                cp.wait_recv()

    return pl.pallas_call(
        body, name=name, in_specs=[IN_HBM] * n + [IN_SEM, IN_SEM, ANY], out_specs=[IN_HBM] * n,
        out_shape=[pltpu.HBM(b.shape, b.dtype) for b in bufs], input_output_aliases={i: i for i in range(n)},
        compiler_params=pltpu.CompilerParams(has_side_effects=SIDE_EFFECT),
    )(*bufs, ssem, rsem, after)


def _scatter_copy(src, land, j, k, ssem, rsem, place, landing):
    x, y, c = place
    px, py = [(1 - x, y), (x, 1 - y), (1 - x, 1 - y)][k]
    return pltpu.make_async_remote_copy(
        src_ref=src.at[2 * px + py], dst_ref=land.at[2 * x + y if landing == "theirs" else 2 * px + py],
        send_sem=ssem.at[3 * j + k], recv_sem=rsem.at[3 * j + k], device_id=(px, py, c), device_id_type=MESH)


def scatter_start(srcs, name):
    n = len(srcs)
    lands = [lax.empty(g.shape, g.dtype) for g in srcs]

    def body(*refs):
        place = _place()
        for j in range(n):
            for k in range(3):
                _scatter_copy(refs[j], refs[n + j], j, k, refs[2 * n], refs[2 * n + 1], place, "theirs").start()
        refs[-1][...] = jnp.zeros_like(refs[-1])

    sem = pltpu.SemaphoreType.DMA((3 * n,))
    hbm = [pltpu.with_memory_space_constraint(b, pltpu.HBM) for b in list(srcs) + lands]
    out = pl.pallas_call(
        body, name=name, in_specs=[IN_HBM] * (2 * n),
        out_specs=(IN_SEM, IN_SEM, *[IN_HBM] * (2 * n), pl.BlockSpec(memory_space=pltpu.VMEM)),
        out_shape=(sem, sem, *[pltpu.HBM(b.shape, b.dtype) for b in hbm], jax.ShapeDtypeStruct((8, LANE), f32)),
        input_output_aliases={i: 2 + i for i in range(2 * n)},
        compiler_params=pltpu.CompilerParams(has_side_effects=SIDE_EFFECT),
    )(*hbm)
    return (list(out[2:2 + n]), list(out[2 + n:2 + 2 * n]), out[0], out[1]), out[-1]


def scatter_wait(flight, after, name):
    srcs, lands, ssem, rsem = flight
    n = len(srcs)

    def body(*refs):
        place = _place()
        for j in range(n):
            for k in range(3):
                cp = _scatter_copy(refs[j], refs[n + j], j, k, refs[2 * n], refs[2 * n + 1], place, "mine")
                cp.wait_send()
                cp.wait_recv()

    out = pl.pallas_call(
        body, name=name, in_specs=[IN_HBM] * (2 * n) + [IN_SEM, IN_SEM, ANY], out_specs=[IN_HBM] * (2 * n),
        out_shape=[pltpu.HBM(b.shape, b.dtype) for b in list(srcs) + list(lands)],
        input_output_aliases={i: i for i in range(2 * n)},
        compiler_params=pltpu.CompilerParams(has_side_effects=SIDE_EFFECT),
    )(*srcs, *lands, ssem, rsem, after)
    return list(out[:n]), list(out[n:])


def pair_exchange(gbufs, name):
    n = len(gbufs)

    def body(*refs):
        ins, outs = refs[:n], refs[n:2 * n]
        ssem, rsem = refs[2 * n:]
        x, y, c = _place()
        cps = [pltpu.make_async_remote_copy(
            src_ref=_half(ins[j], 1 - c, axis=1), dst_ref=outs[j], send_sem=ssem.at[j], recv_sem=rsem.at[j],
            device_id=(x, y, 1 - c), device_id_type=MESH) for j in range(n)]
        for cp in cps:
            cp.start()
        for cp in cps:
            cp.wait()

    return pl.pallas_call(
        body, name=name, in_specs=[ANY] * n, out_specs=[ANY] * n,
        out_shape=[jax.ShapeDtypeStruct((4, g.shape[1] // 2, g.shape[2]), g.dtype) for g in gbufs],
        scratch_shapes=[pltpu.SemaphoreType.DMA((n,)), pltpu.SemaphoreType.DMA((n,))],
    )(*gbufs)


def _row_tile(rows, cap=512):
    return max(t for t in range(16, min(rows, cap) + 1, 16) if rows % t == 0)


def pair_sum(mine, theirs, core, name):
    _, R, C = mine.shape
    half = R // 2
    tr = _row_tile(half)
    nt = half // tr

    def body(c_ref, a_ref, b_ref, o_ref):
        o_ref[...] = (a_ref[...].astype(f32) + b_ref[...].astype(f32)).astype(bf16)

    return pl.pallas_call(
        body, name=name, out_shape=jax.ShapeDtypeStruct(theirs.shape, bf16),
        grid_spec=pltpu.PrefetchScalarGridSpec(
            num_scalar_prefetch=1, grid=(4, nt),
            in_specs=[pl.BlockSpec((None, tr, C), lambda s, i, c_ref: (s, c_ref[0] * nt + i, 0)),
                      pl.BlockSpec((None, tr, C), lambda s, i, c_ref: (s, i, 0))],
            out_specs=pl.BlockSpec((None, tr, C), lambda s, i, c_ref: (s, i, 0))),
        compiler_params=_cp(("parallel", "parallel")),
    )(jnp.reshape(core, (1,)).astype(jnp.int32), mine, theirs)


def final_exchange(fins):
    n = len(fins)

    def body(*refs):
        outs = refs[n:2 * n]
        ssem, rsem = refs[2 * n:]
        x, y, c = _place()
        cps = [pltpu.make_async_remote_copy(
            src_ref=_half(outs[j], c), dst_ref=_half(outs[j], c), send_sem=ssem.at[j], recv_sem=rsem.at[j],
            device_id=(x, y, 1 - c), device_id_type=MESH) for j in range(n)]
        for cp in cps:
            cp.start()
        for cp in cps:
            cp.wait()

    return pl.pallas_call(
        body, name="final_exchange", in_specs=[ANY] * n, out_specs=[ANY] * n,
        out_shape=[jax.ShapeDtypeStruct(f.shape, f.dtype) for f in fins],
        input_output_aliases={j: j for j in range(n)},
        scratch_shapes=[pltpu.SemaphoreType.DMA((n,)), pltpu.SemaphoreType.DMA((n,))],
    )(*fins)


def adamw_big(w, m, v, gs, row0, name):
    L, R, C = w.shape
    tr = _row_tile(math.gcd(R, row0) if row0 else R, max(16, 262144 // C // 16 * 16))
    b0 = row0 // tr

    def body(*refs):
        w_ref, m_ref, v_ref = refs[:3]
        g_refs = refs[3:3 + L]
        g_ref, d_ref, nm_ref, nv_ref = refs[3 + L:]
        g = g_refs[0][...]
        for l in range(1, L):
            g = jnp.where(pl.program_id(0) == l, g_refs[l][...], g)
        d, nm, nv = _adamw_math(w_ref[...], g, m_ref[...], v_ref[...])
        g_ref[...] = g
        d_ref[...] = d
        nm_ref[...] = nm
        nv_ref[...] = nv

    own = pl.BlockSpec((None, tr, C), lambda l, i: (l, i, 0))
    off = pl.BlockSpec((tr, C), lambda l, i: (b0 + i, 0))
    return pl.pallas_call(
        body, name=name, grid=(L, R // tr), in_specs=[own, own, own] + [off] * L, out_specs=[own] * 4,
        out_shape=[jax.ShapeDtypeStruct((L, R, C), f32)] * 4, compiler_params=_cp(("parallel", "parallel")),
    )(w, m, v, *gs)


def _adamw_math(w, g, m, v):
    m = B1 * m + (1.0 - B1) * g
    v = B2 * v + (1.0 - B2) * (g * g)
    m_hat = m / (1.0 - B1 ** STEP)
    v_hat = v / (1.0 - B2 ** STEP)
    delta = -LR * (m_hat / (jnp.sqrt(v_hat) + AEPS) + WD * w)
    return delta, m, v


def adamw_small(w, m, v, g):
    def body(w_ref, m_ref, v_ref, g_ref, d_ref, nm_ref, nv_ref):
        d, nm, nv = _adamw_math(w_ref[...], g_ref[...], m_ref[...], v_ref[...])
        d_ref[...] = d
        nm_ref[...] = nm
        nv_ref[...] = nv

    return pl.pallas_call(body, name="adamw_small", out_shape=[jax.ShapeDtypeStruct(w.shape, f32)] * 3)(w, m, v, g)


CONV =(("conv_qkv_b", 2), ("ffn_conv_w", 2))
SMALL = ("rel_bias", "norm_mix_g", "norm_mem_g", "sinks_a", "a_log_b", "dt_bias_b", "out_norm_g_b", "norm_ffn_g",
         "ffn_conv_b", "final_norm_g")
WEIGHTS = ("rel_bias", "norm_mix_g", "norm_mem_g", "w_mem_kv", "w_out", "w_in_a", "sinks_a", "w_in_b", "conv_qkv_b",
           "a_log_b", "dt_bias_b", "out_norm_g_b", "norm_ffn_g", "w_gate_up", "ffn_conv_w", "ffn_conv_b", "w_down",
           "final_norm_g")
ARGS = ("x", "mem") + WEIGHTS + ("loss_target",) + tuple("m_" + n for n in WEIGHTS) + tuple("v_" + n for n in WEIGHTS)


def _rows(a, width):
    flat = a.reshape(-1)
    pad = (-flat.shape[0]) % (8 * width)
    if pad:
        flat = jnp.concatenate([flat, jnp.zeros((pad,), a.dtype)])
    return flat.reshape(-1, width)


def _nrows(shape, width):
    return _pad_to(-(-math.prod(shape) // width), 8)


def _pack(arrs, width, total_rows, dtype):
    parts = [_rows(a.astype(dtype), width) for a in arrs]
    used = sum(p.shape[0] for p in parts)
    if total_rows > used:
        parts.append(jnp.zeros((total_rows - used, width), dtype))
    return jnp.concatenate(parts, axis=0)


def _unpack(buf, shapes, width):
    out, r = [], 0
    for s in shapes:
        n = _nrows(s, width)
        out.append(buf[r:r + n].reshape(-1)[:math.prod(s)].reshape(s))
        r += n
    return out


def _pad_to(n, mult):
    return -(-n // mult) * mult


def kernel(x, mem, rel_bias, norm_mix_g, norm_mem_g, w_mem_kv, w_out, w_in_a, sinks_a, w_in_b, conv_qkv_b, a_log_b, dt_bias_b, out_norm_g_b, norm_ffn_g, w_gate_up, ffn_conv_w, ffn_conv_b, w_down, final_norm_g, loss_target, m_rel_bias, m_norm_mix_g, m_norm_mem_g, m_w_mem_kv, m_w_out, m_w_in_a, m_sinks_a, m_w_in_b, m_conv_qkv_b, m_a_log_b, m_dt_bias_b, m_out_norm_g_b, m_norm_ffn_g, m_w_gate_up, m_ffn_conv_w, m_ffn_conv_b, m_w_down, m_final_norm_g, v_rel_bias, v_norm_mix_g, v_norm_mem_g, v_w_mem_kv, v_w_out, v_w_in_a, v_sinks_a, v_w_in_b, v_conv_qkv_b, v_a_log_b, v_dt_bias_b, v_out_norm_g_b, v_norm_ffn_g, v_w_gate_up, v_ffn_conv_w, v_ffn_conv_b, v_w_down, v_final_norm_g):
    A = dict(zip(ARGS, (x, mem, rel_bias, norm_mix_g, norm_mem_g, w_mem_kv, w_out, w_in_a, sinks_a, w_in_b, conv_qkv_b, a_log_b, dt_bias_b, out_norm_g_b, norm_ffn_g, w_gate_up, ffn_conv_w, ffn_conv_b, w_down, final_norm_g, loss_target, m_rel_bias, m_norm_mix_g, m_norm_mem_g, m_w_mem_kv, m_w_out, m_w_in_a, m_sinks_a, m_w_in_b, m_conv_qkv_b, m_a_log_b, m_dt_bias_b, m_out_norm_g_b, m_norm_ffn_g, m_w_gate_up, m_ffn_conv_w, m_ffn_conv_b, m_w_down, m_final_norm_g, v_rel_bias, v_norm_mix_g, v_norm_mem_g, v_w_mem_kv, v_w_out, v_w_in_a, v_sinks_a, v_w_in_b, v_conv_qkv_b, v_a_log_b, v_dt_bias_b, v_out_norm_g_b, v_norm_ffn_g, v_w_gate_up, v_ffn_conv_w, v_ffn_conv_b, v_w_down, v_final_norm_g)))
    chip = 2 * lax.axis_index("x") + lax.axis_index("y")
    core = lax.axis_index("c")

    def own_slot(shard):
        return lax.dynamic_update_index_in_dim(lax.empty((4,) + shard.shape, shard.dtype), shard, chip, 0)

    def bslot(w):
        return own_slot(w.astype(bf16))

    groups = {
        ("w_in", 0): [bslot(w_in_a[0])],
        ("w_mem", 0): [bslot(w_mem_kv[0]), bslot(w_mem_kv[1]), own_slot(conv_qkv_b[0]),
                       own_slot(ffn_conv_w.reshape(6, -1))],
        ("w_out", 0): [bslot(w_out[0])], ("w_gu", 0): [bslot(w_gate_up[0])], ("w_down", 0): [bslot(w_down[0])],
        ("w_in", 1): [bslot(w_in_b[0])],
        ("w_out", 1): [bslot(w_out[1])], ("w_gu", 1): [bslot(w_gate_up[1])], ("w_down", 1): [bslot(w_down[1])],
    }
    flights = dict(zip(groups, gather_start(list(groups.values()))))
    P = {"rel_bias": rel_bias, "sinks": sinks_a[0], "a_log": a_log_b[0], "dt_bias": dt_bias_b[0],
         "out_norm_g": out_norm_g_b[0], "g_mix": norm_mix_g, "g_mem": norm_mem_g, "g_ffn": norm_ffn_g,
         "g_fin": final_norm_g, "ffn_cb": [ffn_conv_b[0], ffn_conv_b[1]], "w_mem": [None, None], "w_out": [None, None],
         "w_gu": [None, None], "w_down": [None, None], "ffn_cw": [None, None]}

    def rows4(g):
        return g.reshape(4 * g.shape[1], g.shape[2])

    def arrive(key, after):
        if key not in flights:
            return
        got = gather_wait(flights.pop(key), after, "gather_wait_%s%d" % key)
        name, i = key
        if name == "w_in":
            P["w_in_a" if i == 0 else "w_in_b"] = (_lay_in_a if i == 0 else _lay_in_b)(_unchip_cols(got[0]))
        elif name == "w_mem":
            P["w_mem"] = [rows4(got[0]), rows4(got[1])]
            P["conv_qkv"] = _unchip_cols(got[2])
            cw = _unchip_cols(got[3]).reshape(2, 3, D_FF)
            P["ffn_cw"] = [cw[0], cw[1]]
        elif name == "w_out":
            P["w_out"][i] = _lay_out_a(rows4(got[0])) if i == 0 else rows4(got[0])
        elif name == "w_gu":
            P["w_gu"][i] = got[0]
        else:
            P["w_down"][i] = rows4(got[0])

    def chip_rows(g):
        return g.reshape(4, g.shape[0] // 4, g.shape[-1])

    sent, started = {}, []

    def ready(key, G, dep):
        kind, i = key
        tag = "%s%d" % key
        if kind == "ffn":
            names, partial = ("gu", "down"), [G["w_gu"][i], chip_rows(G["w_down"][i]).astype(bf16)]
        else:
            g_out = _unlay_out_a(G["w_out"][0]) if i == 0 else G["w_out"][1]
            g_in = _unlay_in_a(G["w_in_a"]) if i == 0 else _unlay_in_b(G["w_in_b"])
            names = ("out", "in", "mem")
            partial = [chip_rows(g_out).astype(bf16), _chip_cols(g_in).astype(bf16), chip_rows(G["w_mem"][i]).astype(bf16)]
        theirs = pair_exchange(partial, "pair_exchange_" + tag)
        pair = [pair_sum(p, t, core, "pair_sum_%s%d" % (nm, i)) for p, t, nm in zip(partial, theirs, names)]
        if key == ("mix", 0):
            sent[key] = (names, pair, chip_scatter(pair))
            return dep
        flight, token = scatter_start(pair, "scatter_start_" + tag)
        sent[key] = (names, flight)
        started.append(token[0, 0])
        if dep is not None:
            while started:
                dep = dep + started.pop()
        return dep

    P["arrive"], P["ready"] = arrive, ready

    loss, dx, G = _local_step(x[0], mem[0], loss_target[0], P)
    gfull = _grads_to_ref(G)

    fin = {}
    for key in (("ffn", 1), ("mix", 1), ("ffn", 0), ("mix", 0)):
        if key == ("mix", 0):
            names, pair, arrived = sent[key]
        else:
            names, flight = sent[key]
            pair, arrived = scatter_wait(flight, dx, "scatter_wait_%s%d" % key)
        for nm, p, r in zip(names, pair, arrived):
            fin[nm, key[1]] = sum_slots(p, r, chip, core, "sum_slots_%s%d" % (nm, key[1]))
    order = list(fin)
    done = dict(zip(order, final_exchange([fin[k] for k in order])))

    sm_shapes = [A[n].shape for n in SMALL] + [gfull[n].shape for n, _ in CONV] + [(LANE,)]
    sm_rows = _pad_to(sum(_nrows(s, LANE) for s in sm_shapes), 8)
    sbuf = _pack([gfull[n] for n in SMALL] + [gfull[n] for n, _ in CONV] + [loss[0]], LANE, sm_rows, f32)
    tot = _unpack(allreduce_small(sbuf), sm_shapes, LANE)
    gsmall = dict(zip(SMALL, tot[:len(SMALL)]))
    for (n, axis), t in zip(CONV, tot[len(SMALL):len(SMALL) + len(CONV)]):
        sh = A[n].shape[axis]
        gsmall[n] = lax.dynamic_slice_in_dim(t, chip * sh, sh, axis)
    loss_out = tot[-1][0]

    out = {}
    plan = (("w_gate_up", [done["gu", 0], done["gu", 1]]), ("w_down", [done["down", 0], done["down", 1]]),
            ("w_out", [done["out", 0], done["out", 1]]), ("w_mem_kv", [done["mem", 0], done["mem", 1]]),
            ("w_in_a", [done["in", 0]]), ("w_in_b", [done["in", 1]]))
    for n, gs in plan:
        shape3 = (len(gs),) + gs[0].shape
        res = adamw_big(A[n].reshape(shape3), A["m_" + n].reshape(shape3), A["v_" + n].reshape(shape3), gs, 0,
                        "adamw_" + n)
        for key, r in zip(("grad_", "delta_", "new_m_", "new_v_"), res):
            out[key + n] = r.reshape(A[n].shape)
    names = SMALL + tuple(n for n, _ in CONV)
    shapes = [A[n].shape for n in names]
    rows = _pad_to(sum(_nrows(s, LANE) for s in shapes), 8)
    packs = [_pack([src[n] for n in names], LANE, rows, f32)
             for src in ({n: A[n] for n in names}, {n: A["m_" + n] for n in names}, {n: A["v_" + n] for n in names}, gsmall)]
    res = adamw_small(*packs)
    for key, r in zip(("delta_", "new_m_", "new_v_"), res):
        for n, a in zip(names, _unpack(r, shapes, LANE)):
            out[key + n] = a
    for n in names:
        out["grad_" + n] = gsmall[n]
    return (loss_out, dx[None], *[out["grad_" + n] for n in WEIGHTS], *[out["delta_" + n] for n in WEIGHTS],
            *[out["new_m_" + n] for n in WEIGHTS], *[out["new_v_" + n] for n in WEIGHTS])
```

```python
import functools
import math

import numpy as np
import jax
import jax.numpy as jnp
from jax import lax
from jax.experimental import pallas as pl
from jax.experimental.pallas import tpu as pltpu

f32 = jnp.float32
bf16 = jnp.bfloat16
HI = lax.Precision.HIGHEST
MESH = pl.DeviceIdType.MESH

D = 1024
MEM_LEN = 256
EPS = 1e-6
A_HEADS, A_KV, A_DH = 12, 2, 64
A_Q = 768
BLK = 128
N_BUCKETS, MAX_DIST = 32, 128
B_QK, B_V, B_DH = 384, 768, 128
B_QKV = 1536
CHUNK = 64
X_Q = 256
D_FF = 2816
IN_A = 1280
IN_B = 2572
IN_B_PAD = 2688
LANE = 128
VMEM_LIMIT = 56 * 1024 * 1024
MM_ROWS = 1024

LR, B1, B2, AEPS, WD, STEP = 0.001, 0.9, 0.999, 1e-08, 0.01, 10


def _cp(sem=None):
    return pltpu.CompilerParams(dimension_semantics=sem, vmem_limit_bytes=VMEM_LIMIT)


def _dg(a, b, ca, cb, prec=None):
    return lax.dot_general(a, b, (((ca,), (cb,)), ((), ())), precision=prec, preferred_element_type=f32)


@jax.custom_vjp
def bdot(a, b):
    return _dg(a.astype(bf16), b.astype(bf16), 1, 0)


def _bdot_f(a, b):
    return bdot(a, b), (a, b)


def _bdot_b(res, g):
    a, b = res
    gb = g.astype(bf16)
    return _dg(gb, b.astype(bf16), 1, 1), _dg(a.astype(bf16), gb, 0, 0)


bdot.defvjp(_bdot_f, _bdot_b)


@jax.custom_vjp
def bdot_nt(a, b):
    return _dg(a.astype(bf16), b.astype(bf16), 1, 1)


def _bdot_nt_f(a, b):
    return bdot_nt(a, b), (a, b)


def _bdot_nt_b(res, g):
    a, b = res
    gb = g.astype(bf16)
    return _dg(gb, b.astype(bf16), 1, 0), _dg(gb, a.astype(bf16), 0, 0)


bdot_nt.defvjp(_bdot_nt_f, _bdot_nt_b)


def _shift_rows(x, s, down):
    n = x.shape[0]
    row = lax.broadcasted_iota(jnp.int32, x.shape, 0)
    if down:
        return jnp.where(row >= s, pltpu.roll(x, s, 0), 0.0)
    return jnp.where(row < n - s, pltpu.roll(x, n - s, 0), 0.0)


@functools.partial(jax.custom_vjp, nondiff_argnums=(1,))
def shift_down(x, s):
    return _shift_rows(x, s, True)


def _sd_f(x, s):
    return _shift_rows(x, s, True), None


def _sd_b(s, _, g):
    return (_shift_rows(g, s, False),)


shift_down.defvjp(_sd_f, _sd_b)


def _sigmoid(x):
    return 1.0 / (1.0 + jnp.exp(-x))


def _silu(x):
    return x * _sigmoid(x)


def _rms(x, g):
    return x * lax.rsqrt(jnp.mean(x * x, axis=-1, keepdims=True) + EPS) * g


def _tile(n, cap):
    u = n // LANE
    best = 1
    for d in range(1, u + 1):
        if u % d == 0 and d * LANE <= cap:
            best = d
    return best * LANE


def mm_nn(a, w, res=None, out_dtype=f32, name="mm_nn"):
    M, K = a.shape
    N = w.shape[1]
    tm, tn = min(MM_ROWS, M), _tile(N, 1024)

    def body(*refs):
        if res is None:
            a_ref, w_ref, o_ref = refs
            o_ref[...] = _dg(a_ref[...].astype(bf16), w_ref[...], 1, 0).astype(out_dtype)
        else:
            a_ref, w_ref, r_ref, o_ref = refs
            o_ref[...] = (r_ref[...] + _dg(a_ref[...].astype(bf16), w_ref[...], 1, 0)).astype(out_dtype)

    in_specs = [pl.BlockSpec((tm, K), lambda n, m: (m, 0)), pl.BlockSpec((K, tn), lambda n, m: (0, n))]
    args = [a, w]
    if res is not None:
        in_specs.append(pl.BlockSpec((tm, tn), lambda n, m: (m, n)))
        args.append(res)
    return pl.pallas_call(
        body, name=name, grid=(N // tn, M // tm), in_specs=in_specs,
        out_specs=pl.BlockSpec((tm, tn), lambda n, m: (m, n)),
        out_shape=jax.ShapeDtypeStruct((M, N), out_dtype),
        compiler_params=_cp(("parallel", "parallel")),
    )(*args)


def mm_res_norm(a, w, res, g, name):
    M, K = a.shape
    tm = min(MM_ROWS, M)

    def body(a_ref, w_ref, r_ref, g_ref, o_ref, n_ref):
        h = r_ref[...] + _dg(a_ref[...].astype(bf16), w_ref[...], 1, 0)
        o_ref[...] = h
        n_ref[...] = _rms(h, g_ref[...]).astype(bf16)

    tok = pl.BlockSpec((tm, D), lambda m: (m, 0))
    return pl.pallas_call(
        body, name=name, grid=(M // tm,),
        in_specs=[pl.BlockSpec((tm, K), lambda m: (m, 0)), pl.BlockSpec((K, D), lambda m: (0, 0)), tok,
                  pl.BlockSpec((1, D), lambda m: (0, 0))],
        out_specs=[tok, tok],
        out_shape=[jax.ShapeDtypeStruct((M, D), f32), jax.ShapeDtypeStruct((M, D), bf16)],
        compiler_params=_cp(("parallel",)),
    )(a, w, res, g.reshape(1, D))


def mm_nt(dy, w, out_dtype=f32, name="mm_nt"):
    M, N = dy.shape
    K = w.shape[0]
    tm, tn = min(MM_ROWS, M), _tile(N, 1024)
    assert out_dtype == f32 or tn == N

    def body(dy_ref, w_ref, o_ref):
        part = _dg(dy_ref[...].astype(bf16), w_ref[...], 1, 1)
        if tn == N:
            o_ref[...] = part.astype(out_dtype)
        else:
            @pl.when(pl.program_id(1) == 0)
            def _():
                o_ref[...] = jnp.zeros_like(o_ref)
            o_ref[...] += part

    return pl.pallas_call(
        body, name=name, grid=(M // tm, N // tn),
        in_specs=[pl.BlockSpec((tm, tn), lambda m, n: (m, n)), pl.BlockSpec((K, tn), lambda m, n: (0, n))],
        out_specs=pl.BlockSpec((tm, K), lambda m, n: (m, 0)),
        out_shape=jax.ShapeDtypeStruct((M, K), out_dtype),
        compiler_params=_cp(("parallel", "arbitrary")),
    )(dy, w)


NORM_ROWS = 1024


def _acc_then_norm_bwd(part, steps, h_ref, g_ref, r_ref, o_ref, dg_ref):
    k = pl.program_id(1)

    @pl.when((pl.program_id(0) == 0) & (k == 0))
    def _():
        dg_ref[...] = jnp.zeros_like(dg_ref)

    @pl.when(k == 0)
    def _():
        o_ref[...] = part

    @pl.when(k > 0)
    def _():
        o_ref[...] += part

    @pl.when(k == steps - 1)
    def _():
        _, vjp = jax.vjp(_rms, h_ref[...], g_ref[...])
        dh, dg = vjp(o_ref[...])
        o_ref[...] = r_ref[...] + dh
        dg_ref[...] += dg


def mm_nt_norm(dy, w, norm, name):
    M, N = dy.shape
    tm, tn = min(NORM_ROWS, M), _tile(N, 1024)

    def body(dy_ref, w_ref, h_ref, g_ref, r_ref, o_ref, dg_ref):
        _acc_then_norm_bwd(_dg(dy_ref[...].astype(bf16), w_ref[...], 1, 1), N // tn, h_ref, g_ref, r_ref, o_ref, dg_ref)

    tok = pl.BlockSpec((tm, D), lambda m, n: (m, 0))
    vec = pl.BlockSpec((1, D), lambda m, n: (0, 0))
    return pl.pallas_call(
        body, name=name, grid=(M // tm, N // tn),
        in_specs=[pl.BlockSpec((tm, tn), lambda m, n: (m, n)), pl.BlockSpec((D, tn), lambda m, n: (0, n)), tok, vec, tok],
        out_specs=[tok, vec],
        out_shape=[jax.ShapeDtypeStruct((M, D), f32), jax.ShapeDtypeStruct((1, D), f32)],
        compiler_params=_cp(("arbitrary", "arbitrary")),
    )(dy, w, norm[0], norm[1].reshape(1, D), norm[2])


def mm_tn(a, dy, name="mm_tn"):
    M, K = a.shape
    N = dy.shape[1]
    tm, tk, tn = min(MM_ROWS, M), _tile(K, 1408), _tile(N, 1024)

    def body(a_ref, dy_ref, o_ref):
        @pl.when(pl.program_id(2) == 0)
        def _():
            o_ref[...] = jnp.zeros_like(o_ref)
        o_ref[...] += _dg(a_ref[...].astype(bf16), dy_ref[...].astype(bf16), 0, 0)

    return pl.pallas_call(
        body, name=name, grid=(K // tk, N // tn, M // tm),
        in_specs=[pl.BlockSpec((tm, tk), lambda k, n, m: (m, k)), pl.BlockSpec((tm, tn), lambda k, n, m: (m, n))],
        out_specs=pl.BlockSpec((tk, tn), lambda k, n, m: (k, n)),
        out_shape=jax.ShapeDtypeStruct((K, N), f32),
        compiler_params=_cp(("parallel", "parallel", "arbitrary")),
    )(a, dy)


def rms_fwd(h, g, name):
    S = h.shape[0]
    t = min(512, S)

    def body(h_ref, g_ref, o_ref):
        o_ref[...] = _rms(h_ref[...], g_ref[...]).astype(bf16)

    return pl.pallas_call(
        body, name=name, grid=(S // t,),
        in_specs=[pl.BlockSpec((t, D), lambda i: (i, 0)), pl.BlockSpec((1, D), lambda i: (0, 0))],
        out_specs=pl.BlockSpec((t, D), lambda i: (i, 0)),
        out_shape=jax.ShapeDtypeStruct((S, D), bf16),
        compiler_params=_cp(("parallel",)),
    )(h, g.reshape(1, D))


def loss_head(h, g, target):
    S = h.shape[0]
    t = min(512, S)

    def f(hh, gg, tt):
        err = _rms(hh, gg) - tt
        return 0.5 * jnp.sum(jnp.mean(err * err, axis=-1, keepdims=True), axis=0, keepdims=True)

    def body(h_ref, g_ref, t_ref, loss_ref, dh_ref, dg_ref):
        @pl.when(pl.program_id(0) == 0)
        def _():
            dg_ref[...] = jnp.zeros_like(dg_ref)
            loss_ref[...] = jnp.zeros_like(loss_ref)
        val, vjp = jax.vjp(lambda a, b: f(a, b, t_ref[...]), h_ref[...], g_ref[...])
        dh, dg = vjp(jnp.ones((1, 1), f32))
        dh_ref[...] = dh
        dg_ref[...] += dg
        loss_ref[...] += jnp.broadcast_to(val, loss_ref.shape)

    tok = pl.BlockSpec((t, D), lambda i: (i, 0))
    vec = pl.BlockSpec((1, D), lambda i: (0, 0))
    return pl.pallas_call(
        body, name="loss_head", grid=(S // t,), in_specs=[tok, vec, tok],
        out_specs=[pl.BlockSpec((1, LANE), lambda i: (0, 0)), tok, vec],
        out_shape=[jax.ShapeDtypeStruct((1, LANE), f32), jax.ShapeDtypeStruct((S, D), f32),
                   jax.ShapeDtypeStruct((1, D), f32)],
        compiler_params=_cp(("arbitrary",)),
    )(h, g.reshape(1, D), target)


def memkv_fwd(mem, g, w, name):
    def body(m_ref, g_ref, w_ref, o_ref):
        o_ref[...] = _dg(_rms(m_ref[...], g_ref[...]).astype(bf16), w_ref[...], 1, 0)

    return pl.pallas_call(
        body, name=name, out_shape=jax.ShapeDtypeStruct((MEM_LEN, 2 * X_Q), f32), compiler_params=_cp(),
    )(mem, g.reshape(1, D), w)


def memkv_bwd(mem, g, w, dkv, name):
    def body(m_ref, g_ref, w_ref, d_ref, dg_ref, dw_ref):
        n, vjp = jax.vjp(lambda gg: _rms(m_ref[...], gg), g_ref[...])
        db = d_ref[...].astype(bf16)
        dw_ref[...] = _dg(n.astype(bf16), db, 0, 0)
        dg_ref[...] = vjp(_dg(db, w_ref[...], 1, 1))[0]

    return pl.pallas_call(
        body, name=name,
        out_shape=[jax.ShapeDtypeStruct((1, D), f32), jax.ShapeDtypeStruct((D, 2 * X_Q), f32)],
        compiler_params=_cp(),
    )(mem, g.reshape(1, D), w, dkv)


def _xattn_f(xq, mk, mv):
    lane = lax.broadcasted_iota(jnp.int32, (1, X_Q), 1)
    out = jnp.zeros(xq.shape, f32)
    for hd in range(4):
        msk = (lane // 64 == hd).astype(f32)
        s = bdot_nt(xq * msk, mk) * (64 ** -0.5)
        m = lax.stop_gradient(jnp.max(s, axis=-1, keepdims=True))
        p = jnp.exp(s - m)
        p = p / jnp.sum(p, axis=-1, keepdims=True)
        out = out + bdot(p, mv * msk)
    return out


def xattn_fwd(proj, col, kv, name):
    S = proj.shape[0]
    t = min(512, S)
    cb = col // X_Q

    def body(q_ref, k_ref, v_ref, o_ref):
        o_ref[...] = _xattn_f(q_ref[...], k_ref[...], v_ref[...]).astype(bf16)

    return pl.pallas_call(
        body, name=name, grid=(S // t,),
        in_specs=[pl.BlockSpec((t, X_Q), lambda i: (i, cb)), pl.BlockSpec((MEM_LEN, X_Q), lambda i: (0, 0)),
                  pl.BlockSpec((MEM_LEN, X_Q), lambda i: (0, 1))],
        out_specs=pl.BlockSpec((t, X_Q), lambda i: (i, 0)),
        out_shape=jax.ShapeDtypeStruct((S, X_Q), bf16),
        compiler_params=_cp(("parallel",)),
    )(proj, kv, kv)


def xattn_bwd(proj, col, kv, dmix, name):
    S = proj.shape[0]
    t = min(512, S)
    cb = col // X_Q

    def body(q_ref, k_ref, v_ref, do_ref, dq_ref, dk_ref, dv_ref):
        @pl.when(pl.program_id(0) == 0)
        def _():
            dk_ref[...] = jnp.zeros_like(dk_ref)
            dv_ref[...] = jnp.zeros_like(dv_ref)
        _, vjp = jax.vjp(_xattn_f, q_ref[...], k_ref[...], v_ref[...])
        dq, dk, dv = vjp(do_ref[...])
        dq_ref[...] = dq.astype(bf16)
        dk_ref[...] += dk
        dv_ref[...] += dv

    kvb = pl.BlockSpec((MEM_LEN, X_Q), lambda i: (0, 0))
    dq, dk, dv = pl.pallas_call(
        body, name=name, grid=(S // t,),
        in_specs=[pl.BlockSpec((t, X_Q), lambda i: (i, cb)), kvb,
                  pl.BlockSpec((MEM_LEN, X_Q), lambda i: (0, 1)), pl.BlockSpec((t, X_Q), lambda i: (i, 3))],
        out_specs=[pl.BlockSpec((t, X_Q), lambda i: (i, 0)), kvb, kvb],
        out_shape=[jax.ShapeDtypeStruct((S, X_Q), bf16), jax.ShapeDtypeStruct((MEM_LEN, X_Q), f32),
                   jax.ShapeDtypeStruct((MEM_LEN, X_Q), f32)],
        compiler_params=_cp(("arbitrary",)),
    )(proj, kv, kv, dmix)
    return dq, jnp.concatenate([dk, dv], axis=1)


def _bucket_map():
    qi = np.arange(BLK)[:, None]
    kj = np.arange(2 * BLK)[None, :]
    n = np.maximum(BLK + qi - kj, 0)
    max_exact = N_BUCKETS // 2
    nf = np.maximum(n, 1).astype(np.float64)
    large = max_exact + (np.log(nf / max_exact) / math.log(MAX_DIST / max_exact)
                         * (N_BUCKETS - max_exact)).astype(np.int32)
    large = np.minimum(large, N_BUCKETS - 1)
    return np.where(n < max_exact, n, large).astype(np.int32)


def bias_build(rel_bias):
    def body(rb_ref, bk_ref, o_ref):
        bk = bk_ref[...]
        for h in range(A_HEADS):
            acc = jnp.zeros((BLK, 2 * BLK), f32)
            for b in range(N_BUCKETS):
                acc = jnp.where(bk == b, rb_ref[b, h], acc)
            o_ref[h] = acc

    return pl.pallas_call(
        body, name="bias_build",
        in_specs=[pl.BlockSpec(memory_space=pltpu.SMEM), pl.BlockSpec(memory_space=pltpu.VMEM)],
        out_specs=pl.BlockSpec(memory_space=pltpu.VMEM),
        out_shape=jax.ShapeDtypeStruct((A_HEADS, BLK, 2 * BLK), f32), compiler_params=_cp(),
    )(rel_bias, jnp.asarray(_bucket_map()))


def bias_grad(dbias):
    def body(d_ref, bk_ref, o_ref):
        bk = bk_ref[...]
        row = lax.broadcasted_iota(jnp.int32, (N_BUCKETS, LANE), 0)
        lane = lax.broadcasted_iota(jnp.int32, (N_BUCKETS, LANE), 1)
        acc = jnp.zeros((N_BUCKETS, LANE), f32)
        for h in range(A_HEADS):
            d = d_ref[h]
            for b in range(N_BUCKETS):
                s = jnp.sum(jnp.where(bk == b, d, 0.0), keepdims=True)
                acc = acc + jnp.where((row == b) & (lane == h), s, 0.0)
        o_ref[...] = acc

    return pl.pallas_call(
        body, name="bias_grad", out_shape=jax.ShapeDtypeStruct((N_BUCKETS, LANE), f32), compiler_params=_cp(),
    )(dbias, jnp.asarray(_bucket_map()))


def _swa_f(qb, kp, kc, vp, vc, bias, sk, first):
    kband = jnp.concatenate([kp, kc], axis=0)
    vband = jnp.concatenate([vp, vc], axis=0)
    qi = lax.broadcasted_iota(jnp.int32, (BLK, 2 * BLK), 0)
    kj = lax.broadcasted_iota(jnp.int32, (BLK, 2 * BLK), 1)
    rel = kj - qi
    ok = (rel >= 1) & (rel <= BLK) & ((kj >= BLK) | jnp.logical_not(first))
    lane = lax.broadcasted_iota(jnp.int32, (1, LANE), 1)
    lane_b = lax.broadcasted_iota(jnp.int32, (BLK, LANE), 1)
    outs = []
    for p in range(A_HEADS // 2):
        qp = qb[:, LANE * p:LANE * (p + 1)]
        acc = jnp.zeros((BLK, LANE), f32)
        for g in range(2):
            h = g * (A_HEADS // 2) + p
            msk = (lane // A_DH == g).astype(f32)
            s = bdot_nt(qp * msk, kband) * (A_DH ** -0.5) + bias[h]
            s = jnp.where(ok, s, -1e30)
            skb = jnp.broadcast_to(sk[h:h + 1, :], (BLK, LANE))
            sink = jnp.sum(jnp.where(lane_b == 0, skb, 0.0), axis=-1, keepdims=True)
            m = lax.stop_gradient(jnp.maximum(jnp.max(s, axis=-1, keepdims=True), sink))
            e = jnp.exp(s - m)
            prob = e / (jnp.sum(e, axis=-1, keepdims=True) + jnp.exp(sink - m))
            acc = acc + bdot(prob, vband) * msk
        outs.append(acc)
    return jnp.concatenate(outs, axis=1)


def _swa_specs(nb, rev):
    bi = (lambda i: nb - 1 - i) if rev else (lambda i: i)
    return [
        pl.BlockSpec((BLK, A_Q), lambda i: (bi(i), 0)),
        pl.BlockSpec((BLK, LANE), lambda i: (jnp.maximum(bi(i) - 1, 0), 6)),
        pl.BlockSpec((BLK, LANE), lambda i: (bi(i), 6)),
        pl.BlockSpec((BLK, LANE), lambda i: (jnp.maximum(bi(i) - 1, 0), 7)),
        pl.BlockSpec((BLK, LANE), lambda i: (bi(i), 7)),
        pl.BlockSpec((A_HEADS, BLK, 2 * BLK), lambda i: (0, 0, 0)),
        pl.BlockSpec((16, LANE), lambda i: (0, 0)),
    ]


def swa_fwd(proj, bias, sk):
    S = proj.shape[0]
    nb = S // BLK

    def body(q_ref, kp_ref, kc_ref, vp_ref, vc_ref, b_ref, s_ref, o_ref):
        o_ref[...] = _swa_f(q_ref[...], kp_ref[...], kc_ref[...], vp_ref[...], vc_ref[...], b_ref[...], s_ref[...],
                            pl.program_id(0) == 0).astype(bf16)

    return pl.pallas_call(
        body, name="swa_fwd", grid=(nb,), in_specs=_swa_specs(nb, False),
        out_specs=pl.BlockSpec((BLK, A_Q), lambda i: (i, 0)),
        out_shape=jax.ShapeDtypeStruct((S, A_Q), bf16), compiler_params=_cp(("parallel",)),
    )(proj, proj, proj, proj, proj, bias, sk)


def swa_bwd(proj, bias, sk, dmix):
    S = proj.shape[0]
    nb = S // BLK

    def body(q_ref, kp_ref, kc_ref, vp_ref, vc_ref, b_ref, s_ref, do_ref, dqkv_ref, db_ref, ds_ref, ck, cv):
        i = pl.program_id(0)

        @pl.when(i == 0)
        def _():
            db_ref[...] = jnp.zeros_like(db_ref)
            ds_ref[...] = jnp.zeros_like(ds_ref)
            ck[...] = jnp.zeros_like(ck)
            cv[...] = jnp.zeros_like(cv)
        first = i == nb - 1
        _, vjp = jax.vjp(lambda *a: _swa_f(*a, first), q_ref[...], kp_ref[...], kc_ref[...], vp_ref[...],
                         vc_ref[...], b_ref[...], s_ref[...])
        dq, dkp, dkc, dvp, dvc, db, ds = vjp(do_ref[...])
        dqkv_ref[...] = jnp.concatenate([dq, dkc + ck[...], dvc + cv[...]], axis=1).astype(bf16)
        ck[...] = dkp
        cv[...] = dvp
        db_ref[...] += db
        ds_ref[...] += ds

    return pl.pallas_call(
        body, name="swa_bwd", grid=(nb,),
        in_specs=_swa_specs(nb, True) + [pl.BlockSpec((BLK, A_Q), lambda i: (nb - 1 - i, 0))],
        out_specs=[pl.BlockSpec((BLK, D), lambda i: (nb - 1 - i, 0)),
                   pl.BlockSpec((A_HEADS, BLK, 2 * BLK), lambda i: (0, 0, 0)),
                   pl.BlockSpec((16, LANE), lambda i: (0, 0))],
        out_shape=[jax.ShapeDtypeStruct((S, D), bf16), jax.ShapeDtypeStruct((A_HEADS, BLK, 2 * BLK), f32),
                   jax.ShapeDtypeStruct((16, LANE), f32)],
        scratch_shapes=[pltpu.VMEM((BLK, LANE), f32), pltpu.VMEM((BLK, LANE), f32)],
        compiler_params=_cp(("arbitrary",)),
    )(proj, proj, proj, proj, proj, bias, sk, dmix)


def _dnprep_f(xext, w, is_qk):
    c = (w[3:4] * xext + w[2:3] * shift_down(xext, 1) + w[1:2] * shift_down(xext, 2) + w[0:1] * shift_down(xext, 3))
    a = _silu(c)[HALO:]
    n = a * lax.rsqrt(jnp.sum(a * a, axis=-1, keepdims=True) + EPS)
    return jnp.where(is_qk, n, a)


def dnprep_fwd(proj, cw):
    S = proj.shape[0]
    nblk = B_QKV // LANE
    T = S

    def body(x_ref, w_ref, o_ref):
        is_qk = pl.program_id(0) < 2 * B_QK // LANE
        wv = w_ref[...]

        def tile(r0, first):
            o_ref[pl.ds(r0, T), :] = _dnprep_f(_glu_gext(x_ref, r0, first, T), wv, is_qk)

        tile(0, True)

    return pl.pallas_call(
        body, name="dnprep_fwd", grid=(nblk,),
        in_specs=[pl.BlockSpec((S, LANE), lambda j: (0, j)), pl.BlockSpec((4, LANE), lambda j: (0, j))],
        out_specs=pl.BlockSpec((S, LANE), lambda j: (0, j)),
        out_shape=jax.ShapeDtypeStruct((S, B_QKV), f32), compiler_params=_cp(("parallel",)),
    )(proj, cw)


def dnprep_bwd(proj, cw, dqkvn):
    S = proj.shape[0]
    nblk = B_QKV // LANE

    T = S

    def body(x_ref, w_ref, d_ref, dx_ref, dw_ref):
        is_qk = pl.program_id(0) < 2 * B_QK // LANE
        wv = w_ref[...]

        def tile(r0, first):
            _, vjp = jax.vjp(lambda a, b: _dnprep_f(a, b, is_qk), _glu_gext(x_ref, r0, first, T), wv)
            dx, dw = vjp(d_ref[pl.ds(r0, T), :])
            dx_ref[pl.ds(r0, T), :] = dx[HALO:].astype(bf16)
            if not first:
                dx_ref[pl.ds(r0 - HALO, HALO), :] += dx[:HALO]
            return dw

        dw_ref[...] = tile(0, True)

    col = pl.BlockSpec((S, LANE), lambda j: (0, j))
    wsp = pl.BlockSpec((4, LANE), lambda j: (0, j))
    return pl.pallas_call(
        body, name="dnprep_bwd", grid=(nblk,), in_specs=[col, wsp, col], out_specs=[col, wsp],
        out_shape=[jax.ShapeDtypeStruct((S, B_QKV), bf16), jax.ShapeDtypeStruct((4, B_QKV), f32)],
        compiler_params=_cp(("parallel",)),
    )(proj, cw, dqkvn)


def _hdot(a, b, ca=1, cb=0):
    return _dg(a, b, ca, cb, HI)


def _bdg(a, b, ca, cb):
    dn = (((ca,), (cb,)), ((0,), (0,)))
    ah, bh = a.astype(bf16), b.astype(bf16)
    al, bl = (a - ah.astype(f32)).astype(bf16), (b - bh.astype(f32)).astype(bf16)
    return (lax.dot_general(ah, bh, dn, preferred_element_type=f32)
            + lax.dot_general(ah, bl, dn, preferred_element_type=f32)
            + lax.dot_general(al, bh, dn, preferred_element_type=f32))


@jax.custom_vjp
def hbd(a, b):
    return _bdg(a, b, 2, 1)


@jax.custom_vjp
def hbd_nt(a, b):
    return _bdg(a, b, 2, 2)


@jax.custom_vjp
def hbd_tn(a, b):
    return _bdg(a, b, 1, 1)


hbd.defvjp(lambda a, b: (hbd(a, b), (a, b)), lambda r, g: (hbd_nt(g, r[1]), hbd_tn(r[0], g)))
hbd_nt.defvjp(lambda a, b: (hbd_nt(a, b), (a, b)), lambda r, g: (hbd(g, r[1]), hbd_tn(g, r[0])))
hbd_tn.defvjp(lambda a, b: (hbd_tn(a, b), (a, b)), lambda r, g: (hbd_nt(r[1], g), hbd(r[0], g)))


def _stack(xs):
    return jnp.concatenate([x[None] for x in xs], axis=0)


def _lane_col(x, j):
    lane = lax.broadcasted_iota(jnp.int32, (1, LANE), 1)
    return jnp.sum(jnp.where(lane == j, x, 0.0), axis=-1, keepdims=True)


def _tri_inv(a_mat):
    r = lax.broadcasted_iota(jnp.int32, (1, CHUNK, CHUNK), 1)
    c = lax.broadcasted_iota(jnp.int32, (1, CHUNK, CHUNK), 2)
    pw = -a_mat
    inv = (r == c).astype(f32) + pw
    for _ in range(5):
        pw = hbd(pw, pw)
        inv = inv + hbd(inv, pw)
    return inv


@jax.custom_vjp
def _tri_inv_known(a_mat, inv):
    return inv


_tri_inv_known.defvjp(lambda a, inv: (inv, inv),
                      lambda inv, g: (-hbd_tn(inv, hbd_nt(g, inv)), jnp.zeros_like(inv)))


def _dnc_f(q, k, v, seg, prm, inverse=_tri_inv):
    C = CHUNK
    B = q.shape[0]
    rows = seg.shape[0]
    beta_all = _sigmoid(seg)
    xx = seg + prm[1:2]
    g_all = -jnp.exp(prm[0:1]) * (jnp.maximum(xx, 0.0) + jnp.log(1.0 + jnp.exp(-jnp.abs(xx))))
    r2 = lax.broadcasted_iota(jnp.int32, (rows, rows), 0)
    c2 = lax.broadcasted_iota(jnp.int32, (rows, rows), 1)
    within = (r2 >= c2) & (r2 // C == c2 // C)
    gc_all = _hdot(within.astype(f32), g_all)
    beta = _stack([_lane_col(beta_all[C * j:C * (j + 1)], h) for j in range(rows // C) for h in range(6)])
    gc = _stack([_lane_col(gc_all[C * j:C * (j + 1)], 6 + h) for j in range(rows // C) for h in range(6)])
    r = lax.broadcasted_iota(jnp.int32, (1, C, C), 1)
    c = lax.broadcasted_iota(jnp.int32, (1, C, C), 2)
    incl = r >= c
    strict = r > c
    gct = [gc_all[C * j:C * (j + 1)].T for j in range(rows // C)]
    g_row = _stack([jnp.broadcast_to(gct[j][6 + h:7 + h, :], (C, C))
                    for j in range(rows // C) for h in range(6)])
    decay = jnp.where(incl, jnp.exp(jnp.where(incl, gc - g_row, 0.0)), 0.0)
    a_mat = beta * sbd_nt(k, k) * jnp.where(strict, decay, 0.0)
    eg = jnp.exp(gc)
    inv = inverse(a_mat)
    u = hbd(inv, beta * v)
    w = hbd(inv, (beta * eg) * k)
    qc = q * (B_DH ** -0.5)
    attn = sbd_nt(qc, k) * decay
    last = (lax.broadcasted_iota(jnp.int32, (1, C, 1), 1) == C - 1).astype(f32)
    g_last = jnp.sum(gc * last, axis=1, keepdims=True)
    dc = jnp.broadcast_to(jnp.exp(g_last), (B, 1, LANE)).reshape(B, LANE)
    return u, w, qc * eg, k * jnp.exp(g_last - gc), attn, dc, inv


def _b1(a, b, ca, cb):
    return lax.dot_general(a.astype(bf16), b.astype(bf16), (((ca,), (cb,)), ((0,), (0,))), preferred_element_type=f32)


@jax.custom_vjp
def sbd(a, b):
    return _b1(a, b, 2, 1)


@jax.custom_vjp
def sbd_nt(a, b):
    return _b1(a, b, 2, 2)


@jax.custom_vjp
def sbd_tn(a, b):
    return _b1(a, b, 1, 1)


sbd.defvjp(lambda a, b: (sbd(a, b), (a, b)), lambda r, g: (sbd_nt(g, r[1]), sbd_tn(r[0], g)))
sbd_nt.defvjp(lambda a, b: (sbd_nt(a, b), (a, b)), lambda r, g: (sbd(g, r[1]), sbd_tn(g, r[0])))
sbd_tn.defvjp(lambda a, b: (sbd_tn(a, b), (a, b)), lambda r, g: (sbd_nt(r[1], g), sbd(r[0], g)))


def _dns_f(S0, u, w, qd, kt, attn, dcrows):
    dc = _lane_col(dcrows, 0).reshape(6, 1, 1)
    delta = u - sbd(w, S0)
    out = sbd(qd, S0) + sbd(attn, delta)
    return out, dc * S0 + sbd_tn(kt, delta)


def _dnpost_f(o, z, grow):
    outs = []
    for h in range(6):
        oh = o[:, LANE * h:LANE * (h + 1)]
        outs.append(oh * lax.rsqrt(jnp.mean(oh * oh, axis=-1, keepdims=True) + EPS) * grow
                    * _silu(z[:, LANE * h:LANE * (h + 1)]))
    return jnp.concatenate(outs, axis=1)


def _hs(h):
    return slice(LANE * h, LANE * (h + 1))


DN_CHUNKS = 4


def _heads(ref, share):
    return _stack([ref[CHUNK * j:CHUNK * (j + 1), _hs(h // share)]
                   for j in range(ref.shape[0] // CHUNK) for h in range(6)])


def _put_heads(ref, val):
    for j in range(ref.shape[0] // CHUNK):
        for h in range(6):
            ref[CHUNK * j:CHUNK * (j + 1), _hs(h)] = val[6 * j + h]


def _dnc_in_specs():
    rows = CHUNK * DN_CHUNKS
    return [
        pl.BlockSpec((rows, B_QK), lambda n: (n, 0)),
        pl.BlockSpec((rows, B_QK), lambda n: (n, 1)),
        pl.BlockSpec((rows, B_V), lambda n: (n, 1)),
        pl.BlockSpec((rows, LANE), lambda n: (n, 20)),
        pl.BlockSpec((8, LANE), lambda n: (0, 0)),
    ]


def _dnc_out_specs(rev_nc=None, chunks=1):
    ci = (lambda n: n) if rev_nc is None else (lambda n: rev_nc - 1 - n)
    wide = pl.BlockSpec((CHUNK * chunks, B_V), lambda n: (ci(n), 0))
    return [wide, wide, wide, wide, pl.BlockSpec((chunks, 6, CHUNK, CHUNK), lambda n: (ci(n), 0, 0, 0)),
            pl.BlockSpec((chunks, 8, LANE), lambda n: (ci(n), 0, 0))]


def _dc_rows(dc):
    pad = jnp.zeros((2, LANE), f32)
    return _stack([jnp.concatenate([dc[6 * j:6 * (j + 1)], pad], axis=0) for j in range(dc.shape[0] // 6)])


def _dnc_shapes(S):
    nc = S // CHUNK
    wide = jax.ShapeDtypeStruct((S, B_V), f32)
    return [wide, wide, wide, wide, jax.ShapeDtypeStruct((nc, 6, CHUNK, CHUNK), f32),
            jax.ShapeDtypeStruct((nc, 8, LANE), f32)]


def dnc_fwd(qkvn, proj, prm):
    S = proj.shape[0]

    def body(q_ref, k_ref, v_ref, s_ref, p_ref, u_ref, w_ref, qd_ref, kt_ref, at_ref, dc_ref, inv_ref):
        u, w, qd, kt, attn, dc, inv = _dnc_f(_heads(q_ref, 2), _heads(k_ref, 2), _heads(v_ref, 1), s_ref[...],
                                             p_ref[...])
        inv_ref[...] = inv.reshape(inv_ref.shape)
        _put_heads(u_ref, u)
        _put_heads(w_ref, w)
        _put_heads(qd_ref, qd)
        _put_heads(kt_ref, kt)
        at_ref[...] = attn.reshape(at_ref.shape)
        dc_ref[...] = _dc_rows(dc)

    outs = _dnc_out_specs(chunks=DN_CHUNKS)
    out = pl.pallas_call(
        body, name="dn_chunk_fwd", grid=(S // (CHUNK * DN_CHUNKS),), in_specs=_dnc_in_specs(),
        out_specs=outs + [outs[4]], out_shape=_dnc_shapes(S) + [_dnc_shapes(S)[4]],
        compiler_params=_cp(("parallel",)),
    )(qkvn, qkvn, qkvn, proj, prm)
    return out[:6], out[6]


def dnc_bwd(qkvn, proj, prm, inv, cots):
    S = proj.shape[0]

    def body(q_ref, k_ref, v_ref, s_ref, p_ref, inv_ref, du_ref, dw_ref, dqd_ref, dkt_ref, dat_ref, ddc_ref,
             dx_ref, dseg_ref, dprm_ref):
        @pl.when(pl.program_id(0) == 0)
        def _():
            dprm_ref[...] = jnp.zeros_like(dprm_ref)
        nb = 6 * DN_CHUNKS
        known = functools.partial(_tri_inv_known, inv=inv_ref[...].reshape(nb, CHUNK, CHUNK))
        _, vjp = jax.vjp(lambda *a: _dnc_f(*a, inverse=known)[:6], _heads(q_ref, 2), _heads(k_ref, 2),
                         _heads(v_ref, 1), s_ref[...], p_ref[...])
        ddc = jnp.concatenate([ddc_ref[j, 0:6, :] for j in range(DN_CHUNKS)], axis=0)
        dq, dk, dv, dseg, dprm = vjp((_heads(du_ref, 1), _heads(dw_ref, 1), _heads(dqd_ref, 1), _heads(dkt_ref, 1),
                                      dat_ref[...].reshape(nb, CHUNK, CHUNK), ddc))
        for j in range(DN_CHUNKS):
            o = 6 * j
            dx_ref[CHUNK * j:CHUNK * (j + 1), :] = jnp.concatenate(
                [dq[o] + dq[o + 1], dq[o + 2] + dq[o + 3], dq[o + 4] + dq[o + 5],
                 dk[o] + dk[o + 1], dk[o + 2] + dk[o + 3], dk[o + 4] + dk[o + 5]] + [dv[o + h] for h in range(6)], axis=1)
        dseg_ref[...] = dseg.astype(bf16)
        dprm_ref[...] += dprm

    rows = CHUNK * DN_CHUNKS
    outs = _dnc_out_specs(chunks=DN_CHUNKS)
    return pl.pallas_call(
        body, name="dn_chunk_bwd", grid=(S // rows,),
        in_specs=_dnc_in_specs() + [outs[4]] + outs,
        out_specs=[pl.BlockSpec((rows, B_QKV), lambda n: (n, 0)), pl.BlockSpec((rows, LANE), lambda n: (n, 0)),
                   pl.BlockSpec((8, LANE), lambda n: (0, 0))],
        out_shape=[jax.ShapeDtypeStruct((S, B_QKV), f32), jax.ShapeDtypeStruct((S, LANE), bf16),
                   jax.ShapeDtypeStruct((8, LANE), f32)],
        compiler_params=_cp(("arbitrary",)),
    )(qkvn, qkvn, qkvn, proj, prm, inv, *cots)


def dns_fwd(chunked):
    u = chunked[0]
    S = u.shape[0]
    nc = S // CHUNK

    def body(u_ref, w_ref, qd_ref, kt_ref, at_ref, dc_ref, o_ref, st_ref, st):
        @pl.when(pl.program_id(0) == 0)
        def _():
            st[...] = jnp.zeros_like(st)
        S0 = st[...]
        st_ref[0] = S0
        out, S1 = _dns_f(S0, _heads(u_ref, 1), _heads(w_ref, 1), _heads(qd_ref, 1), _heads(kt_ref, 1),
                         at_ref[0], dc_ref[0, 0:6, :])
        _put_heads(o_ref, out)
        st[...] = S1

    return pl.pallas_call(
        body, name="dn_scan_fwd", grid=(nc,), in_specs=_dnc_out_specs(),
        out_specs=[pl.BlockSpec((CHUNK, B_V), lambda n: (n, 0)),
                   pl.BlockSpec((1, 6, B_DH, B_DH), lambda n: (n, 0, 0, 0))],
        out_shape=[jax.ShapeDtypeStruct((S, B_V), f32), jax.ShapeDtypeStruct((nc, 6, B_DH, B_DH), f32)],
        scratch_shapes=[pltpu.VMEM((6, B_DH, B_DH), f32)],
        compiler_params=_cp(("arbitrary",)),
    )(*chunked)


def dns_bwd(chunked, states, do):
    S = do.shape[0]
    nc = S // CHUNK

    def body(u_ref, w_ref, qd_ref, kt_ref, at_ref, dc_ref, st_ref, do_ref,
             du_ref, dw_ref, dqd_ref, dkt_ref, dat_ref, ddc_ref, dst):
        @pl.when(pl.program_id(0) == 0)
        def _():
            dst[...] = jnp.zeros_like(dst)
        _, vjp = jax.vjp(_dns_f, st_ref[0], _heads(u_ref, 1), _heads(w_ref, 1), _heads(qd_ref, 1), _heads(kt_ref, 1),
                         at_ref[0], dc_ref[0, 0:6, :])
        dS0, du, dw, dqd, dkt, dat, ddc = vjp((_heads(do_ref, 1), dst[...]))
        dst[...] = dS0
        _put_heads(du_ref, du)
        _put_heads(dw_ref, dw)
        _put_heads(dqd_ref, dqd)
        _put_heads(dkt_ref, dkt)
        dat_ref[0] = dat
        ddc_ref[0] = jnp.concatenate([ddc, jnp.zeros((2, LANE), f32)], axis=0)

    return pl.pallas_call(
        body, name="dn_scan_bwd", grid=(nc,),
        in_specs=_dnc_out_specs(nc) + [pl.BlockSpec((1, 6, B_DH, B_DH), lambda n: (nc - 1 - n, 0, 0, 0)),
                                       pl.BlockSpec((CHUNK, B_V), lambda n: (nc - 1 - n, 0))],
        out_specs=_dnc_out_specs(nc), out_shape=_dnc_shapes(S),
        scratch_shapes=[pltpu.VMEM((6, B_DH, B_DH), f32)],
        compiler_params=_cp(("arbitrary",)),
    )(*chunked, states, do)


def dnpost_fwd(o, proj, prm):
    S = o.shape[0]
    t = min(512, S)

    def body(o_ref, z_ref, p_ref, y_ref):
        y_ref[...] = _dnpost_f(o_ref[...], z_ref[...], p_ref[2:3, :]).astype(bf16)

    tok = pl.BlockSpec((t, B_V), lambda i: (i, 0))
    return pl.pallas_call(
        body, name="dn_post_fwd", grid=(S // t,),
        in_specs=[tok, pl.BlockSpec((t, B_V), lambda i: (i, 2)), pl.BlockSpec((8, LANE), lambda i: (0, 0))],
        out_specs=tok, out_shape=jax.ShapeDtypeStruct((S, B_V), bf16), compiler_params=_cp(("parallel",)),
    )(o, proj, prm)


def dnpost_bwd(o, proj, prm, dmix):
    S = o.shape[0]
    t = min(512, S)

    def body(o_ref, z_ref, p_ref, dy_ref, do_ref, dz_ref, dg_ref):
        @pl.when(pl.program_id(0) == 0)
        def _():
            dg_ref[...] = jnp.zeros_like(dg_ref)
        _, vjp = jax.vjp(_dnpost_f, o_ref[...], z_ref[...], p_ref[2:3, :])
        do, dz, dg = vjp(dy_ref[...])
        do_ref[...] = do
        dz_ref[...] = dz.astype(bf16)
        dg_ref[...] += dg

    tok = pl.BlockSpec((t, B_V), lambda i: (i, 0))
    return pl.pallas_call(
        body, name="dn_post_bwd", grid=(S // t,),
        in_specs=[tok, pl.BlockSpec((t, B_V), lambda i: (i, 2)), pl.BlockSpec((8, LANE), lambda i: (0, 0)), tok],
        out_specs=[tok, tok, pl.BlockSpec((1, LANE), lambda i: (0, 0))],
        out_shape=[jax.ShapeDtypeStruct((S, B_V), f32), jax.ShapeDtypeStruct((S, B_V), bf16),
                   jax.ShapeDtypeStruct((1, LANE), f32)],
        compiler_params=_cp(("arbitrary",)),
    )(o, proj, prm, dmix)


N_FF_BLK = D_FF // LANE
GU_SHARD = 2 * D_FF // 4


GLU_ROWS = 256
HALO = 16


def _glu_f(gext, up, w, b):
    c = w[2:3] * gext + w[1:2] * shift_down(gext, 1) + w[0:1] * shift_down(gext, 2) + b
    return _silu(c)[HALO:] * up


def _glu_gext(g_ref, r0, first, T=GLU_ROWS):
    if first:
        return jnp.concatenate([jnp.zeros((HALO, LANE), f32), g_ref[0:T, :].astype(f32)], axis=0)
    return g_ref[pl.ds(r0 - HALO, T + HALO), :].astype(f32)


def glu_fwd(gu, w, b, name):
    S = gu.shape[0]
    T = min(GLU_ROWS, S // 2)

    def body(g_ref, u_ref, w_ref, b_ref, o_ref):
        wv, bv = w_ref[...], b_ref[...]

        def tile(r0, first):
            act = _glu_f(_glu_gext(g_ref, r0, first, T), u_ref[pl.ds(r0, T), :].astype(f32), wv, bv)
            o_ref[pl.ds(r0, T), :] = act.astype(bf16)

        tile(0, True)

        @pl.loop(1, S // T)
        def _(t):
            tile(pl.multiple_of(t * T, T), False)

    col = pl.BlockSpec((S, LANE), lambda j: (0, j))
    return pl.pallas_call(
        body, name=name, grid=(N_FF_BLK,),
        in_specs=[col, pl.BlockSpec((S, LANE), lambda j: (0, N_FF_BLK + j)), pl.BlockSpec((3, LANE), lambda j: (0, j)),
                  pl.BlockSpec((1, LANE), lambda j: (0, j))],
        out_specs=col, out_shape=jax.ShapeDtypeStruct((S, D_FF), bf16), compiler_params=_cp(("parallel",)),
    )(gu, gu, w, b.reshape(1, D_FF))


def glu_bwd(gu, w, b, dact, name):
    S = gu.shape[0]
    T = min(GLU_ROWS, S // 2)

    def body(g_ref, u_ref, w_ref, b_ref, d_ref, dg_ref, dw_ref, db_ref, acc):
        wv, bv = w_ref[...], b_ref[...]

        def tile(r0, first):
            _, vjp = jax.vjp(_glu_f, _glu_gext(g_ref, r0, first, T), u_ref[pl.ds(r0, T), :].astype(f32), wv, bv)
            dgx, du, dw, db = vjp(d_ref[pl.ds(r0, T), :].astype(f32))
            acc[pl.ds(r0, T), :] = dgx[HALO:]
            if not first:
                acc[pl.ds(r0 - HALO, HALO), :] += dgx[:HALO]
            dg_ref[1, pl.ds(r0, T), :] = du.astype(bf16)
            return dw, db

        dw0, db0 = tile(0, True)
        dw_ref[...] = dw0
        db_ref[...] = db0

        @pl.loop(1, S // T)
        def _(t):
            dw, db = tile(pl.multiple_of(t * T, T), False)
            dw_ref[...] += dw
            db_ref[...] += db

        dg_ref[0] = acc[...].astype(bf16)

    col = pl.BlockSpec((S, LANE), lambda j: (0, j))
    wsp = pl.BlockSpec((3, LANE), lambda j: (0, j))
    bsp = pl.BlockSpec((1, LANE), lambda j: (0, j))
    return pl.pallas_call(
        body, name=name, grid=(N_FF_BLK,),
        in_specs=[col, pl.BlockSpec((S, LANE), lambda j: (0, N_FF_BLK + j)), wsp, bsp, col],
        out_specs=[pl.BlockSpec((2, S, LANE), lambda j: (0, 0, j)), wsp, bsp],
        out_shape=[jax.ShapeDtypeStruct((2, S, D_FF), bf16), jax.ShapeDtypeStruct((3, D_FF), f32),
                   jax.ShapeDtypeStruct((1, D_FF), f32)],
        scratch_shapes=[pltpu.VMEM((S, LANE), f32)],
        compiler_params=_cp(("parallel",)),
    )(gu, gu, w, b.reshape(1, D_FF), dact)


def gu_fwd(n2, wg, name):
    S = n2.shape[0]
    tm = min(MM_ROWS, S)

    def body(a_ref, w_ref, o_ref):
        o_ref[...] = _dg(a_ref[...], w_ref[...], 1, 0).astype(bf16)

    return pl.pallas_call(
        body, name=name, grid=(4, S // tm),
        in_specs=[pl.BlockSpec((tm, D), lambda s, m: (m, 0)), pl.BlockSpec((None, D, GU_SHARD), lambda s, m: (s, 0, 0))],
        out_specs=pl.BlockSpec((tm, GU_SHARD), lambda s, m: (m, s)),
        out_shape=jax.ShapeDtypeStruct((S, 2 * D_FF), bf16), compiler_params=_cp(("parallel", "parallel")),
    )(n2, wg)


def gu_bwd_x(dgu, wg, norm, name):
    S = dgu.shape[1]
    tm = min(NORM_ROWS, S)

    def body(d_ref, w_ref, h_ref, g_ref, r_ref, o_ref, dg_ref):
        _acc_then_norm_bwd(_dg(d_ref[...], w_ref[...], 1, 1), 4, h_ref, g_ref, r_ref, o_ref, dg_ref)

    tok = pl.BlockSpec((tm, D), lambda m, s: (m, 0))
    vec = pl.BlockSpec((1, D), lambda m, s: (0, 0))
    return pl.pallas_call(
        body, name=name, grid=(S // tm, 4),
        in_specs=[pl.BlockSpec((None, tm, GU_SHARD), lambda m, s: (s // 2, m, s % 2)),
                  pl.BlockSpec((None, D, GU_SHARD), lambda m, s: (s, 0, 0)), tok, vec, tok],
        out_specs=[tok, vec],
        out_shape=[jax.ShapeDtypeStruct((S, D), f32), jax.ShapeDtypeStruct((1, D), f32)],
        compiler_params=_cp(("arbitrary", "arbitrary")),
    )(dgu, wg, norm[0], norm[1].reshape(1, D), norm[2])


def gu_bwd_w(n2, dgu, name):
    S = n2.shape[0]
    tm = min(MM_ROWS, S)
    nm = S // tm

    def body(a_ref, d_ref, o_ref, acc):
        @pl.when(pl.program_id(1) == 0)
        def _():
            acc[...] = jnp.zeros_like(acc)
        acc[...] += _dg(a_ref[...], d_ref[...], 0, 0)

        @pl.when(pl.program_id(1) == nm - 1)
        def _():
            o_ref[...] = acc[...].astype(bf16)

    return pl.pallas_call(
        body, name=name, grid=(4, nm),
        in_specs=[pl.BlockSpec((tm, D), lambda s, m: (m, 0)),
                  pl.BlockSpec((None, tm, GU_SHARD), lambda s, m: (s // 2, m, s % 2))],
        out_specs=pl.BlockSpec((None, D, GU_SHARD), lambda s, m: (s, 0, 0)),
        out_shape=jax.ShapeDtypeStruct((4, D, GU_SHARD), bf16),
        scratch_shapes=[pltpu.VMEM((D, GU_SHARD), f32)],
        compiler_params=_cp(("parallel", "arbitrary")),
    )(n2, dgu)


def _pair_cols(w):
    lead = w.shape[:-1]
    return w.reshape(lead + (2, 6, A_DH)).swapaxes(-3, -2).reshape(lead + (A_Q,))


def _unpair_cols(w):
    lead = w.shape[:-1]
    return w.reshape(lead + (6, 2, A_DH)).swapaxes(-3, -2).reshape(lead + (A_Q,))


def _lay_in_a(w):
    return jnp.concatenate([_pair_cols(w[:, :A_Q]), w[:, A_Q:]], axis=1)


def _unlay_in_a(w):
    return jnp.concatenate([_unpair_cols(w[:, :A_Q]), w[:, A_Q:]], axis=1)


def _lay_out_a(w):
    return jnp.concatenate([_pair_cols(w[:A_Q].T).T, w[A_Q:]], axis=0)


def _unlay_out_a(w):
    return jnp.concatenate([_unpair_cols(w[:A_Q].T).T, w[A_Q:]], axis=0)


def _lay_in_b(w):
    return jnp.concatenate([w[:, :2304], w[:, 2316:], w[:, 2304:2316],
                            jnp.zeros((w.shape[0], LANE - 12), w.dtype)], axis=1)


def _unlay_in_b(w):
    return jnp.concatenate([w[:, :2304], w[:, 2560:2572], w[:, 2304:2560]], axis=1)


def _chip_cols(w):
    return jnp.moveaxis(w.reshape(w.shape[0], 4, w.shape[1] // 4), 1, 0)


def _unchip_cols(w):
    return jnp.moveaxis(w, 0, 1).reshape(w.shape[1], 4 * w.shape[2])


def _local_step(x, mem, target, P):
    arrive = P.get("arrive", lambda key, after: None)
    ready = P.get("ready", lambda key, grads, dep: dep)
    sk = jnp.zeros((16, LANE), f32).at[:A_HEADS].set(jnp.broadcast_to(P["sinks"][:, None], (A_HEADS, LANE)))
    prm = jnp.zeros((8, LANE), f32).at[0, 6:12].set(P["a_log"]).at[1, 6:12].set(P["dt_bias"]).at[2].set(P["out_norm_g"])
    bias = bias_build(P["rel_bias"])
    saved = []
    h = x
    n1 = rms_fwd(h, P["g_mix"][0], "rms_mix0")
    for i in range(2):
        arrive(("w_in", i), n1)
        proj = mm_nn(n1, P["w_in_a"] if i == 0 else P["w_in_b"], name="proj_a" if i == 0 else "proj_b")
        arrive(("w_mem", i), proj)
        kv = memkv_fwd(mem, P["g_mem"][i], P["w_mem"][i], f"memkv{i}")
        if i == 0:
            self_out = swa_fwd(proj, bias, sk)
            cross = xattn_fwd(proj, A_Q + 2 * LANE, kv, "xattn_a")
            extra = ()
        else:
            qkvn = dnprep_fwd(proj, P["conv_qkv"])
            chunked, inv = dnc_fwd(qkvn, proj, prm)
            o, states = dns_fwd(chunked)
            self_out = dnpost_fwd(o, proj, prm)
            cross = xattn_fwd(proj, 2304, kv, "xattn_b")
            extra = (qkvn, chunked, inv, states, o)
        mix = jnp.concatenate([self_out, cross], axis=1)
        arrive(("w_out", i), cross)
        h2, n2 = mm_res_norm(mix, P["w_out"][i], h, P["g_ffn"][i], f"out_proj{i}")
        arrive(("w_gu", i), n2)
        gu = gu_fwd(n2, P["w_gu"][i], f"gate_up{i}")
        act = glu_fwd(gu, P["ffn_cw"][i], P["ffn_cb"][i], f"glu{i}")
        arrive(("w_down", i), act)
        saved.append((h, n1, kv, proj, mix, h2, n2, gu, act, extra))
        if i == 0:
            h, n1 = mm_res_norm(act, P["w_down"][i], h2, P["g_mix"][1], f"down{i}")
        else:
            h = mm_nn(act, P["w_down"][i], res=h2, name=f"down{i}")

    loss, dh, dg_fin = loss_head(h, P["g_fin"], target)
    G = {"g_fin": dg_fin[0], "g_mix": [None, None], "g_mem": [None, None], "g_ffn": [None, None],
         "w_mem": [None, None], "w_out": [None, None], "w_gu": [None, None], "w_down": [None, None],
         "ffn_cw": [None, None], "ffn_cb": [None, None]}
    for i in (1, 0):
        hin, n1, kv, proj, mix, h2, n2, gu, act, extra = saved[i]
        dact = mm_nt(dh, P["w_down"][i], out_dtype=bf16, name=f"d_act{i}")
        G["w_down"][i] = mm_tn(act, dh, name=f"dw_down{i}")
        dgu, dcw, dcb = glu_bwd(gu, P["ffn_cw"][i], P["ffn_cb"][i], dact, f"glu_bwd{i}")
        G["ffn_cw"][i], G["ffn_cb"][i] = dcw, dcb[0]
        G["w_gu"][i] = gu_bwd_w(n2, dgu, f"dw_gu{i}")
        g_ffn = ready(("ffn", i), G, P["g_ffn"][i])
        dh2, dg = gu_bwd_x(dgu, P["w_gu"][i], (h2, g_ffn, dh), f"d_n2_{i}")
        G["g_ffn"][i] = dg[0]
        dmix = mm_nt(dh2, P["w_out"][i], name=f"d_mix{i}")
        G["w_out"][i] = mm_tn(mix, dh2, name=f"dw_out{i}")
        if i == 0:
            dqkv, dbias, dsk = swa_bwd(proj, bias, sk, dmix)
            dxq, dkv = xattn_bwd(proj, A_Q + 2 * LANE, kv, dmix, "xattn_a_bwd")
            dproj = jnp.concatenate([dqkv, dxq], axis=1)
            G["sinks"] = dsk[:A_HEADS, 0]
            G["rel_bias"] = bias_grad(dbias)[:, :A_HEADS]
            w_in, gname = P["w_in_a"], "w_in_a"
        else:
            qkvn, chunked, inv, states, o = extra
            do, dz, dgo = dnpost_bwd(o, proj, prm, dmix)
            dqkvn, dseg, dprm = dnc_bwd(qkvn, proj, prm, inv, dns_bwd(chunked, states, do))
            draw, dconv = dnprep_bwd(proj, P["conv_qkv"], dqkvn)
            dxq, dkv = xattn_bwd(proj, 2304, kv, dmix, "xattn_b_bwd")
            dproj = jnp.concatenate([draw, dz, dxq, dseg], axis=1)
            G["conv_qkv"] = dconv
            G["a_log"], G["dt_bias"], G["out_norm_g"] = dprm[0, 6:12], dprm[1, 6:12], dgo[0]
            w_in, gname = P["w_in_b"], "w_in_b"
        G[gname] = mm_tn(n1, dproj, name=f"d{gname}")
        dh, dg = mm_nt_norm(dproj, w_in, (hin, P["g_mix"][i], dh2), f"d_n1_{i}")
        G["g_mix"][i] = dg[0]
        dgm, dwm = memkv_bwd(mem, P["g_mem"][i], P["w_mem"][i], dkv, f"memkv_bwd{i}")
        G["g_mem"][i], G["w_mem"][i] = dgm[0], dwm
        ready(("mix", i), G, None)
    return loss, dh, G


def _prepare(full, w_gu=None):
    return {
        "rel_bias": full["rel_bias"], "sinks": full["sinks_a"][0], "a_log": full["a_log_b"][0],
        "dt_bias": full["dt_bias_b"][0], "out_norm_g": full["out_norm_g_b"][0],
        "g_mix": full["norm_mix_g"], "g_mem": full["norm_mem_g"], "g_ffn": full["norm_ffn_g"],
        "g_fin": full["final_norm_g"], "conv_qkv": full["conv_qkv_b"][0],
        "ffn_cw": [full["ffn_conv_w"][0], full["ffn_conv_w"][1]],
        "ffn_cb": [full["ffn_conv_b"][0], full["ffn_conv_b"][1]],
        "w_mem": [full["w_mem_kv"][0], full["w_mem_kv"][1]],
        "w_out": [_lay_out_a(full["w_out"][0]), full["w_out"][1]],
        "w_in_a": _lay_in_a(full["w_in_a"][0]), "w_in_b": _lay_in_b(full["w_in_b"][0]),
        "w_gu": w_gu if w_gu is not None else [_chip_cols(full["w_gate_up"][0]), _chip_cols(full["w_gate_up"][1])],
        "w_down": [full["w_down"][0], full["w_down"][1]],
    }


def _grads_to_ref(G):
    return {
        "rel_bias": G["rel_bias"], "norm_mix_g": jnp.stack(G["g_mix"]), "norm_mem_g": jnp.stack(G["g_mem"]),
        "w_mem_kv": jnp.stack(G["w_mem"]),
        "w_out": jnp.stack([_unlay_out_a(G["w_out"][0]), G["w_out"][1]]),
        "w_in_a": _unlay_in_a(G["w_in_a"])[None], "sinks_a": G["sinks"][None],
        "w_in_b": _unlay_in_b(G["w_in_b"])[None], "conv_qkv_b": G["conv_qkv"][None],
        "a_log_b": G["a_log"][None], "dt_bias_b": G["dt_bias"][None], "out_norm_g_b": G["out_norm_g"][None],
        "norm_ffn_g": jnp.stack(G["g_ffn"]),
        "w_gate_up": jnp.stack([_unchip_cols(G["w_gu"][0]), _unchip_cols(G["w_gu"][1])]).astype(f32),
        "ffn_conv_w": jnp.stack(G["ffn_cw"]), "ffn_conv_b": jnp.stack(G["ffn_cb"]),
        "w_down": jnp.stack(G["w_down"]), "final_norm_g": G["g_fin"],
    }


ANY = pl.BlockSpec(memory_space=pl.ANY)


def _place():
    return lax.axis_index("x"), lax.axis_index("y"), lax.axis_index("c")


def chip_scatter(gs):
    n = len(gs)

    def body(*refs):
        ins, outs = refs[:n], refs[n:2 * n]
        ssem, rsem = refs[2 * n:]
        x, y, c = _place()
        me = 2 * x + y
        peers = [(1 - x, y), (x, 1 - y), (1 - x, 1 - y)]

        def remote(j, k, slot):
            px, py = peers[k]
            return pltpu.make_async_remote_copy(
                src_ref=ins[j].at[2 * px + py], dst_ref=outs[j].at[slot],
                send_sem=ssem.at[3 * j + k], recv_sem=rsem.at[3 * j + k],
                device_id=(px, py, c), device_id_type=MESH)

        sends = [remote(j, k, me) for j in range(n) for k in range(3)]
        for cp in sends:
            cp.start()
        for j in range(n):
            for k in range(3):
                px, py = peers[k]
                remote(j, k, 2 * px + py).wait_recv()
        for cp in sends:
            cp.wait_send()

    return pl.pallas_call(
        body, name="grad_scatter", in_specs=[ANY] * n, out_specs=[ANY] * n,
        out_shape=[jax.ShapeDtypeStruct(g.shape, g.dtype) for g in gs],
        scratch_shapes=[pltpu.SemaphoreType.DMA((3 * n,)), pltpu.SemaphoreType.DMA((3 * n,))],
    )(*gs)


def allreduce_small(buf):
    R = buf.shape[0]

    def body(b_ref, o_ref, recv, ssem, rsem):
        x, y, c = _place()
        me = 4 * x + 2 * y + c

        def peer(k):
            return (1 - x if k & 4 else x, 1 - y if k & 2 else y, 1 - c if k & 1 else c)

        def remote(k, slot):
            return pltpu.make_async_remote_copy(
                src_ref=b_ref, dst_ref=recv.at[slot], send_sem=ssem.at[k - 1], recv_sem=rsem.at[k - 1],
                device_id=peer(k), device_id_type=MESH)

        sends = [remote(k, me) for k in range(1, 8)]
        for cp in sends:
            cp.start()
        recv[me] = b_ref[...]
        for k in range(1, 8):
            px, py, pc = peer(k)
            remote(k, 4 * px + 2 * py + pc).wait_recv()
        for cp in sends:
            cp.wait_send()
        total = recv[0]
        for j in range(1, 8):
            total = total + recv[j]
        o_ref[...] = total

    return pl.pallas_call(
        body, name="small_allreduce",
        in_specs=[pl.BlockSpec(memory_space=pltpu.VMEM)], out_specs=pl.BlockSpec(memory_space=pltpu.VMEM),
        out_shape=jax.ShapeDtypeStruct(buf.shape, f32),
        scratch_shapes=[pltpu.VMEM((8, R, LANE), f32), pltpu.SemaphoreType.DMA((7,)), pltpu.SemaphoreType.DMA((7,))],
    )(buf)


def sum_slots(own, recv, chip, core, name):
    _, R, C = recv.shape
    tr = _row_tile(R, 256)
    nt = R // tr

    def body(p_ref, a_ref, r_ref, o_ref):
        acc = jnp.zeros((tr, C), f32)
        for s in range(4):
            acc = acc + jnp.where(p_ref[0] == s, a_ref[s], r_ref[s]).astype(f32)
        o_ref[...] = acc

    slots = pl.BlockSpec((4, tr, C), lambda i, p_ref: (0, i, 0))
    return pl.pallas_call(
        body, name=name, out_shape=jax.ShapeDtypeStruct((2 * R, C), f32),
        grid_spec=pltpu.PrefetchScalarGridSpec(
            num_scalar_prefetch=1, grid=(nt,), in_specs=[slots, slots],
            out_specs=pl.BlockSpec((tr, C), lambda i, p_ref: (p_ref[1] * nt + i, 0))),
        compiler_params=_cp(("parallel",)),
    )(jnp.stack([chip, core]).astype(jnp.int32), own, recv)


def _half(ref, core, axis=0):
    half = ref.shape[axis] // 2
    idx = (slice(None),) * axis + (pl.ds(core * half, half),)
    return ref.at[idx]


IN_HBM = pl.BlockSpec(memory_space=pltpu.HBM)
IN_SEM = pl.BlockSpec(memory_space=pltpu.SEMAPHORE)
SIDE_EFFECT = pltpu.SideEffectType.DATAFLOW_SIDE_EFFECTING


def _gather_copy(buf, i, k, ssem, rsem, place, landing):
    x, y, c = place
    px, py = [(1 - x, y), (x, 1 - y), (1 - x, 1 - y)][k]
    me = 2 * x + y
    return pltpu.make_async_remote_copy(
        src_ref=buf.at[me], dst_ref=buf.at[me if landing == "theirs" else 2 * px + py],
        send_sem=ssem.at[3 * i + k], recv_sem=rsem.at[3 * i + k], device_id=(px, py, c), device_id_type=MESH)


def gather_start(groups):
    flat = [b for grp in groups for b in grp]
    n, ng = len(flat), len(groups)

    def body(*refs):
        bufs, sems = refs[:n], refs[n:n + 2 * ng]
        place = _place()
        j = 0
        for g, grp in enumerate(groups):
            for i in range(len(grp)):
                for k in range(3):
                    _gather_copy(bufs[j], i, k, sems[2 * g], sems[2 * g + 1], place, "theirs").start()
                j += 1

    sem_shapes = [pltpu.SemaphoreType.DMA((3 * len(grp),)) for grp in groups for _ in range(2)]
    out = pl.pallas_call(
        body, name="gather_start", in_specs=[IN_HBM] * n, out_specs=(*[IN_SEM] * (2 * ng), *[IN_HBM] * n),
        out_shape=(*sem_shapes, *[pltpu.HBM(b.shape, b.dtype) for b in flat]),
        input_output_aliases={i: 2 * ng + i for i in range(n)},
        compiler_params=pltpu.CompilerParams(has_side_effects=SIDE_EFFECT),
    )(*[pltpu.with_memory_space_constraint(b, pltpu.HBM) for b in flat])
    sems, bufs = out[:2 * ng], list(out[2 * ng:])
    flights, j = [], 0
    for g, grp in enumerate(groups):
        flights.append((bufs[j:j + len(grp)], sems[2 * g], sems[2 * g + 1]))
        j += len(grp)
    return flights


def gather_wait(flight, after, name):
    bufs, ssem, rsem = flight
    n = len(bufs)

    def body(*refs):
        place = _place()
        for i in range(n):
            for k in range(3):
                cp = _gather_copy(refs[i], i, k, refs[n], refs[n + 1], place, "mine")
                cp.wait_send()
                cp.wait_recv()

    return pl.pallas_call(
        body, name=name, in_specs=[IN_HBM] * n + [IN_SEM, IN_SEM, ANY], out_specs=[IN_HBM] * n,
        out_shape=[pltpu.HBM(b.shape, b.dtype) for b in bufs], input_output_aliases={i: i for i in range(n)},
        compiler_params=pltpu.CompilerParams(has_side_effects=SIDE_EFFECT),
    )(*bufs, ssem, rsem, after)


def _scatter_copy(src, land, j, k, ssem, rsem, place, landing):
    x, y, c = place
    px, py = [(1 - x, y), (x, 1 - y), (1 - x, 1 - y)][k]
    return pltpu.make_async_remote_copy(
        src_ref=src.at[2 * px + py], dst_ref=land.at[2 * x + y if landing == "theirs" else 2 * px + py],
        send_sem=ssem.at[3 * j + k], recv_sem=rsem.at[3 * j + k], device_id=(px, py, c), device_id_type=MESH)


def scatter_start(srcs, name):
    n = len(srcs)
    lands = [lax.empty(g.shape, g.dtype) for g in srcs]

    def body(*refs):
        place = _place()
        for j in range(n):
            for k in range(3):
                _scatter_copy(refs[j], refs[n + j], j, k, refs[2 * n], refs[2 * n + 1], place, "theirs").start()
        refs[-1][...] = jnp.zeros_like(refs[-1])

    sem = pltpu.SemaphoreType.DMA((3 * n,))
    hbm = [pltpu.with_memory_space_constraint(b, pltpu.HBM) for b in list(srcs) + lands]
    out = pl.pallas_call(
        body, name=name, in_specs=[IN_HBM] * (2 * n),
        out_specs=(IN_SEM, IN_SEM, *[IN_HBM] * (2 * n), pl.BlockSpec(memory_space=pltpu.VMEM)),
        out_shape=(sem, sem, *[pltpu.HBM(b.shape, b.dtype) for b in hbm], jax.ShapeDtypeStruct((8, LANE), f32)),
        input_output_aliases={i: 2 + i for i in range(2 * n)},
        compiler_params=pltpu.CompilerParams(has_side_effects=SIDE_EFFECT),
    )(*hbm)
    return (list(out[2:2 + n]), list(out[2 + n:2 + 2 * n]), out[0], out[1]), out[-1]


def scatter_wait(flight, after, name):
    srcs, lands, ssem, rsem = flight
    n = len(srcs)

    def body(*refs):
        place = _place()
        for j in range(n):
            for k in range(3):
                cp = _scatter_copy(refs[j], refs[n + j], j, k, refs[2 * n], refs[2 * n + 1], place, "mine")
                cp.wait_send()
                cp.wait_recv()

    out = pl.pallas_call(
        body, name=name, in_specs=[IN_HBM] * (2 * n) + [IN_SEM, IN_SEM, ANY], out_specs=[IN_HBM] * (2 * n),
        out_shape=[pltpu.HBM(b.shape, b.dtype) for b in list(srcs) + list(lands)],
        input_output_aliases={i: i for i in range(2 * n)},
        compiler_params=pltpu.CompilerParams(has_side_effects=SIDE_EFFECT),
    )(*srcs, *lands, ssem, rsem, after)
    return list(out[:n]), list(out[n:])


def pair_exchange(gbufs, name):
    n = len(gbufs)

    def body(*refs):
        ins, outs = refs[:n], refs[n:2 * n]
        ssem, rsem = refs[2 * n:]
        x, y, c = _place()
        cps = [pltpu.make_async_remote_copy(
            src_ref=_half(ins[j], 1 - c, axis=1), dst_ref=outs[j], send_sem=ssem.at[j], recv_sem=rsem.at[j],
            device_id=(x, y, 1 - c), device_id_type=MESH) for j in range(n)]
        for cp in cps:
            cp.start()
        for cp in cps:
            cp.wait()

    return pl.pallas_call(
        body, name=name, in_specs=[ANY] * n, out_specs=[ANY] * n,
        out_shape=[jax.ShapeDtypeStruct((4, g.shape[1] // 2, g.shape[2]), g.dtype) for g in gbufs],
        scratch_shapes=[pltpu.SemaphoreType.DMA((n,)), pltpu.SemaphoreType.DMA((n,))],
    )(*gbufs)


def _row_tile(rows, cap=512):
    return max(t for t in range(16, min(rows, cap) + 1, 16) if rows % t == 0)


def pair_sum(mine, theirs, core, name):
    _, R, C = mine.shape
    half = R // 2
    tr = _row_tile(half)
    nt = half // tr

    def body(c_ref, a_ref, b_ref, o_ref):
        o_ref[...] = (a_ref[...].astype(f32) + b_ref[...].astype(f32)).astype(bf16)

    return pl.pallas_call(
        body, name=name, out_shape=jax.ShapeDtypeStruct(theirs.shape, bf16),
        grid_spec=pltpu.PrefetchScalarGridSpec(
            num_scalar_prefetch=1, grid=(4, nt),
            in_specs=[pl.BlockSpec((None, tr, C), lambda s, i, c_ref: (s, c_ref[0] * nt + i, 0)),
                      pl.BlockSpec((None, tr, C), lambda s, i, c_ref: (s, i, 0))],
            out_specs=pl.BlockSpec((None, tr, C), lambda s, i, c_ref: (s, i, 0))),
        compiler_params=_cp(("parallel", "parallel")),
    )(jnp.reshape(core, (1,)).astype(jnp.int32), mine, theirs)


def final_exchange(fins):
    n = len(fins)

    def body(*refs):
        outs = refs[n:2 * n]
        ssem, rsem = refs[2 * n:]
        x, y, c = _place()
        cps = [pltpu.make_async_remote_copy(
            src_ref=_half(outs[j], c), dst_ref=_half(outs[j], c), send_sem=ssem.at[j], recv_sem=rsem.at[j],
            device_id=(x, y, 1 - c), device_id_type=MESH) for j in range(n)]
        for cp in cps:
            cp.start()
        for cp in cps:
            cp.wait()

    return pl.pallas_call(
        body, name="final_exchange", in_specs=[ANY] * n, out_specs=[ANY] * n,
        out_shape=[jax.ShapeDtypeStruct(f.shape, f.dtype) for f in fins],
        input_output_aliases={j: j for j in range(n)},
        scratch_shapes=[pltpu.SemaphoreType.DMA((n,)), pltpu.SemaphoreType.DMA((n,))],
    )(*fins)


def adamw_big(w, m, v, gs, row0, name):
    L, R, C = w.shape
    tr = _row_tile(math.gcd(R, row0) if row0 else R, max(16, 262144 // C // 16 * 16))
    b0 = row0 // tr

    def body(*refs):
        w_ref, m_ref, v_ref = refs[:3]
        g_refs = refs[3:3 + L]
        g_ref, d_ref, nm_ref, nv_ref = refs[3 + L:]
        g = g_refs[0][...]
        for l in range(1, L):
            g = jnp.where(pl.program_id(0) == l, g_refs[l][...], g)
        d, nm, nv = _adamw_math(w_ref[...], g, m_ref[...], v_ref[...])
        g_ref[...] = g
        d_ref[...] = d
        nm_ref[...] = nm
        nv_ref[...] = nv

    own = pl.BlockSpec((None, tr, C), lambda l, i: (l, i, 0))
    off = pl.BlockSpec((tr, C), lambda l, i: (b0 + i, 0))
    return pl.pallas_call(
        body, name=name, grid=(L, R // tr), in_specs=[own, own, own] + [off] * L, out_specs=[own] * 4,
        out_shape=[jax.ShapeDtypeStruct((L, R, C), f32)] * 4, compiler_params=_cp(("parallel", "parallel")),
    )(w, m, v, *gs)


def _adamw_math(w, g, m, v):
    m = B1 * m + (1.0 - B1) * g
    v = B2 * v + (1.0 - B2) * (g * g)
    m_hat = m / (1.0 - B1 ** STEP)
    v_hat = v / (1.0 - B2 ** STEP)
    delta = -LR * (m_hat / (jnp.sqrt(v_hat) + AEPS) + WD * w)
    return delta, m, v


def adamw_small(w, m, v, g):
    def body(w_ref, m_ref, v_ref, g_ref, d_ref, nm_ref, nv_ref):
        d, nm, nv = _adamw_math(w_ref[...], g_ref[...], m_ref[...], v_ref[...])
        d_ref[...] = d
        nm_ref[...] = nm
        nv_ref[...] = nv

    return pl.pallas_call(body, name="adamw_small", out_shape=[jax.ShapeDtypeStruct(w.shape, f32)] * 3)(w, m, v, g)


CONV =(("conv_qkv_b", 2), ("ffn_conv_w", 2))
SMALL = ("rel_bias", "norm_mix_g", "norm_mem_g", "sinks_a", "a_log_b", "dt_bias_b", "out_norm_g_b", "norm_ffn_g",
         "ffn_conv_b", "final_norm_g")
WEIGHTS = ("rel_bias", "norm_mix_g", "norm_mem_g", "w_mem_kv", "w_out", "w_in_a", "sinks_a", "w_in_b", "conv_qkv_b",
           "a_log_b", "dt_bias_b", "out_norm_g_b", "norm_ffn_g", "w_gate_up", "ffn_conv_w", "ffn_conv_b", "w_down",
           "final_norm_g")
ARGS = ("x", "mem") + WEIGHTS + ("loss_target",) + tuple("m_" + n for n in WEIGHTS) + tuple("v_" + n for n in WEIGHTS)


def _rows(a, width):
    flat = a.reshape(-1)
    pad = (-flat.shape[0]) % (8 * width)
    if pad:
        flat = jnp.concatenate([flat, jnp.zeros((pad,), a.dtype)])
    return flat.reshape(-1, width)


def _nrows(shape, width):
    return _pad_to(-(-math.prod(shape) // width), 8)


def _pack(arrs, width, total_rows, dtype):
    parts = [_rows(a.astype(dtype), width) for a in arrs]
    used = sum(p.shape[0] for p in parts)
    if total_rows > used:
        parts.append(jnp.zeros((total_rows - used, width), dtype))
    return jnp.concatenate(parts, axis=0)


def _unpack(buf, shapes, width):
    out, r = [], 0
    for s in shapes:
        n = _nrows(s, width)
        out.append(buf[r:r + n].reshape(-1)[:math.prod(s)].reshape(s))
        r += n
    return out


def _pad_to(n, mult):
    return -(-n // mult) * mult


def kernel(x, mem, rel_bias, norm_mix_g, norm_mem_g, w_mem_kv, w_out, w_in_a, sinks_a, w_in_b, conv_qkv_b, a_log_b, dt_bias_b, out_norm_g_b, norm_ffn_g, w_gate_up, ffn_conv_w, ffn_conv_b, w_down, final_norm_g, loss_target, m_rel_bias, m_norm_mix_g, m_norm_mem_g, m_w_mem_kv, m_w_out, m_w_in_a, m_sinks_a, m_w_in_b, m_conv_qkv_b, m_a_log_b, m_dt_bias_b, m_out_norm_g_b, m_norm_ffn_g, m_w_gate_up, m_ffn_conv_w, m_ffn_conv_b, m_w_down, m_final_norm_g, v_rel_bias, v_norm_mix_g, v_norm_mem_g, v_w_mem_kv, v_w_out, v_w_in_a, v_sinks_a, v_w_in_b, v_conv_qkv_b, v_a_log_b, v_dt_bias_b, v_out_norm_g_b, v_norm_ffn_g, v_w_gate_up, v_ffn_conv_w, v_ffn_conv_b, v_w_down, v_final_norm_g):
    A = dict(zip(ARGS, (x, mem, rel_bias, norm_mix_g, norm_mem_g, w_mem_kv, w_out, w_in_a, sinks_a, w_in_b, conv_qkv_b, a_log_b, dt_bias_b, out_norm_g_b, norm_ffn_g, w_gate_up, ffn_conv_w, ffn_conv_b, w_down, final_norm_g, loss_target, m_rel_bias, m_norm_mix_g, m_norm_mem_g, m_w_mem_kv, m_w_out, m_w_in_a, m_sinks_a, m_w_in_b, m_conv_qkv_b, m_a_log_b, m_dt_bias_b, m_out_norm_g_b, m_norm_ffn_g, m_w_gate_up, m_ffn_conv_w, m_ffn_conv_b, m_w_down, m_final_norm_g, v_rel_bias, v_norm_mix_g, v_norm_mem_g, v_w_mem_kv, v_w_out, v_w_in_a, v_sinks_a, v_w_in_b, v_conv_qkv_b, v_a_log_b, v_dt_bias_b, v_out_norm_g_b, v_norm_ffn_g, v_w_gate_up, v_ffn_conv_w, v_ffn_conv_b, v_w_down, v_final_norm_g)))
    chip = 2 * lax.axis_index("x") + lax.axis_index("y")
    core = lax.axis_index("c")

    def own_slot(shard):
        return lax.dynamic_update_index_in_dim(lax.empty((4,) + shard.shape, shard.dtype), shard, chip, 0)

    def bslot(w):
        return own_slot(w.astype(bf16))

    groups = {
        ("w_in", 0): [bslot(w_in_a[0])],
        ("w_mem", 0): [bslot(w_mem_kv[0]), bslot(w_mem_kv[1]), own_slot(conv_qkv_b[0]),
                       own_slot(ffn_conv_w.reshape(6, -1))],
        ("w_out", 0): [bslot(w_out[0])], ("w_gu", 0): [bslot(w_gate_up[0])], ("w_down", 0): [bslot(w_down[0])],
        ("w_in", 1): [bslot(w_in_b[0])],
        ("w_out", 1): [bslot(w_out[1])], ("w_gu", 1): [bslot(w_gate_up[1])], ("w_down", 1): [bslot(w_down[1])],
    }
    flights = dict(zip(groups, gather_start(list(groups.values()))))
    P = {"rel_bias": rel_bias, "sinks": sinks_a[0], "a_log": a_log_b[0], "dt_bias": dt_bias_b[0],
         "out_norm_g": out_norm_g_b[0], "g_mix": norm_mix_g, "g_mem": norm_mem_g, "g_ffn": norm_ffn_g,
         "g_fin": final_norm_g, "ffn_cb": [ffn_conv_b[0], ffn_conv_b[1]], "w_mem": [None, None], "w_out": [None, None],
         "w_gu": [None, None], "w_down": [None, None], "ffn_cw": [None, None]}

    def rows4(g):
        return g.reshape(4 * g.shape[1], g.shape[2])

    def arrive(key, after):
        if key not in flights:
            return
        got = gather_wait(flights.pop(key), after, "gather_wait_%s%d" % key)
        name, i = key
        if name == "w_in":
            P["w_in_a" if i == 0 else "w_in_b"] = (_lay_in_a if i == 0 else _lay_in_b)(_unchip_cols(got[0]))
        elif name == "w_mem":
            P["w_mem"] = [rows4(got[0]), rows4(got[1])]
            P["conv_qkv"] = _unchip_cols(got[2])
            cw = _unchip_cols(got[3]).reshape(2, 3, D_FF)
            P["ffn_cw"] = [cw[0], cw[1]]
        elif name == "w_out":
            P["w_out"][i] = _lay_out_a(rows4(got[0])) if i == 0 else rows4(got[0])
        elif name == "w_gu":
            P["w_gu"][i] = got[0]
        else:
            P["w_down"][i] = rows4(got[0])

    def chip_rows(g):
        return g.reshape(4, g.shape[0] // 4, g.shape[-1])

    sent, started = {}, []

    def ready(key, G, dep):
        kind, i = key
        tag = "%s%d" % key
        if kind == "ffn":
            names, partial = ("gu", "down"), [G["w_gu"][i], chip_rows(G["w_down"][i]).astype(bf16)]
        else:
            g_out = _unlay_out_a(G["w_out"][0]) if i == 0 else G["w_out"][1]
            g_in = _unlay_in_a(G["w_in_a"]) if i == 0 else _unlay_in_b(G["w_in_b"])
            names = ("out", "in", "mem")
            partial = [chip_rows(g_out).astype(bf16), _chip_cols(g_in).astype(bf16), chip_rows(G["w_mem"][i]).astype(bf16)]
        theirs = pair_exchange(partial, "pair_exchange_" + tag)
        pair = [pair_sum(p, t, core, "pair_sum_%s%d" % (nm, i)) for p, t, nm in zip(partial, theirs, names)]
        if key == ("mix", 0):
            sent[key] = (names, pair, chip_scatter(pair))
            return dep
        flight, token = scatter_start(pair, "scatter_start_" + tag)
        sent[key] = (names, flight)
        started.append(token[0, 0])
        if dep is not None:
            while started:
                dep = dep + started.pop()
        return dep

    P["arrive"], P["ready"] = arrive, ready

    loss, dx, G = _local_step(x[0], mem[0], loss_target[0], P)
    gfull = _grads_to_ref(G)

    fin = {}
    for key in (("ffn", 1), ("mix", 1), ("ffn", 0), ("mix", 0)):
        if key == ("mix", 0):
            names, pair, arrived = sent[key]
        else:
            names, flight = sent[key]
            pair, arrived = scatter_wait(flight, dx, "scatter_wait_%s%d" % key)
        for nm, p, r in zip(names, pair, arrived):
            fin[nm, key[1]] = sum_slots(p, r, chip, core, "sum_slots_%s%d" % (nm, key[1]))
    order = list(fin)
    done = dict(zip(order, final_exchange([fin[k] for k in order])))

    sm_shapes = [A[n].shape for n in SMALL] + [gfull[n].shape for n, _ in CONV] + [(LANE,)]
    sm_rows = _pad_to(sum(_nrows(s, LANE) for s in sm_shapes), 8)
    sbuf = _pack([gfull[n] for n in SMALL] + [gfull[n] for n, _ in CONV] + [loss[0]], LANE, sm_rows, f32)
    tot = _unpack(allreduce_small(sbuf), sm_shapes, LANE)
    gsmall = dict(zip(SMALL, tot[:len(SMALL)]))
    for (n, axis), t in zip(CONV, tot[len(SMALL):len(SMALL) + len(CONV)]):
        sh = A[n].shape[axis]
        gsmall[n] = lax.dynamic_slice_in_dim(t, chip * sh, sh, axis)
    loss_out = tot[-1][0]

    out = {}
    plan = (("w_gate_up", [done["gu", 0], done["gu", 1]]), ("w_down", [done["down", 0], done["down", 1]]),
            ("w_out", [done["out", 0], done["out", 1]]), ("w_mem_kv", [done["mem", 0], done["mem", 1]]),
            ("w_in_a", [done["in", 0]]), ("w_in_b", [done["in", 1]]))
    for n, gs in plan:
        shape3 = (len(gs),) + gs[0].shape
        res = adamw_big(A[n].reshape(shape3), A["m_" + n].reshape(shape3), A["v_" + n].reshape(shape3), gs, 0,
                        "adamw_" + n)
        for key, r in zip(("grad_", "delta_", "new_m_", "new_v_"), res):
            out[key + n] = r.reshape(A[n].shape)
    names = SMALL + tuple(n for n, _ in CONV)
    shapes = [A[n].shape for n in names]
    rows = _pad_to(sum(_nrows(s, LANE) for s in shapes), 8)
    packs = [_pack([src[n] for n in names], LANE, rows, f32)
             for src in ({n: A[n] for n in names}, {n: A["m_" + n] for n in names}, {n: A["v_" + n] for n in names}, gsmall)]
    res = adamw_small(*packs)
    for key, r in zip(("delta_", "new_m_", "new_v_"), res):
        for n, a in zip(names, _unpack(r, shapes, LANE)):
            out[key + n] = a
    for n in names:
        out["grad_" + n] = gsmall[n]
    return (loss_out, dx[None], *[out["grad_" + n] for n in WEIGHTS], *[out["delta_" + n] for n in WEIGHTS],
            *[out["new_m_" + n] for n in WEIGHTS], *[out["new_v_" + n] for n in WEIGHTS])
```

```python
import functools
import math

import numpy as np
import jax
import jax.numpy as jnp
from jax import lax
from jax.experimental import pallas as pl
from jax.experimental.pallas import tpu as pltpu

f32 = jnp.float32
bf16 = jnp.bfloat16
HI = lax.Precision.HIGHEST
MESH = pl.DeviceIdType.MESH

D = 1024
MEM_LEN = 256
EPS = 1e-6
A_HEADS, A_KV, A_DH = 12, 2, 64
A_Q = 768
BLK = 128
N_BUCKETS, MAX_DIST = 32, 128
B_QK, B_V, B_DH = 384, 768, 128
B_QKV = 1536
CHUNK = 64
X_Q = 256
D_FF = 2816
IN_A = 1280
IN_B = 2572
IN_B_PAD = 2688
LANE = 128
VMEM_LIMIT = 56 * 1024 * 1024
MM_ROWS = 1024

LR, B1, B2, AEPS, WD, STEP = 0.001, 0.9, 0.999, 1e-08, 0.01, 10


def _cp(sem=None):
    return pltpu.CompilerParams(dimension_semantics=sem, vmem_limit_bytes=VMEM_LIMIT)


def _dg(a, b, ca, cb, prec=None):
    return lax.dot_general(a, b, (((ca,), (cb,)), ((), ())), precision=prec, preferred_element_type=f32)


@jax.custom_vjp
def bdot(a, b):
    return _dg(a.astype(bf16), b.astype(bf16), 1, 0)


def _bdot_f(a, b):
    return bdot(a, b), (a, b)


def _bdot_b(res, g):
    a, b = res
    gb = g.astype(bf16)
    return _dg(gb, b.astype(bf16), 1, 1), _dg(a.astype(bf16), gb, 0, 0)


bdot.defvjp(_bdot_f, _bdot_b)


@jax.custom_vjp
def bdot_nt(a, b):
    return _dg(a.astype(bf16), b.astype(bf16), 1, 1)


def _bdot_nt_f(a, b):
    return bdot_nt(a, b), (a, b)


def _bdot_nt_b(res, g):
    a, b = res
    gb = g.astype(bf16)
    return _dg(gb, b.astype(bf16), 1, 0), _dg(gb, a.astype(bf16), 0, 0)


bdot_nt.defvjp(_bdot_nt_f, _bdot_nt_b)


def _shift_rows(x, s, down):
    n = x.shape[0]
    row = lax.broadcasted_iota(jnp.int32, x.shape, 0)
    if down:
        return jnp.where(row >= s, pltpu.roll(x, s, 0), 0.0)
    return jnp.where(row < n - s, pltpu.roll(x, n - s, 0), 0.0)


@functools.partial(jax.custom_vjp, nondiff_argnums=(1,))
def shift_down(x, s):
    return _shift_rows(x, s, True)


def _sd_f(x, s):
    return _shift_rows(x, s, True), None


def _sd_b(s, _, g):
    return (_shift_rows(g, s, False),)


shift_down.defvjp(_sd_f, _sd_b)


def _sigmoid(x):
    return 1.0 / (1.0 + jnp.exp(-x))


def _silu(x):
    return x * _sigmoid(x)


def _rms(x, g):
    return x * lax.rsqrt(jnp.mean(x * x, axis=-1, keepdims=True) + EPS) * g


def _tile(n, cap):
    u = n // LANE
    best = 1
    for d in range(1, u + 1):
        if u % d == 0 and d * LANE <= cap:
            best = d
    return best * LANE


def mm_nn(a, w, res=None, out_dtype=f32, name="mm_nn"):
    M, K = a.shape
    N = w.shape[1]
    tm, tn = min(MM_ROWS, M), _tile(N, 1024)

    def body(*refs):
        if res is None:
            a_ref, w_ref, o_ref = refs
            o_ref[...] = _dg(a_ref[...].astype(bf16), w_ref[...], 1, 0).astype(out_dtype)
        else:
            a_ref, w_ref, r_ref, o_ref = refs
            o_ref[...] = (r_ref[...] + _dg(a_ref[...].astype(bf16), w_ref[...], 1, 0)).astype(out_dtype)

    in_specs = [pl.BlockSpec((tm, K), lambda n, m: (m, 0)), pl.BlockSpec((K, tn), lambda n, m: (0, n))]
    args = [a, w]
    if res is not None:
        in_specs.append(pl.BlockSpec((tm, tn), lambda n, m: (m, n)))
        args.append(res)
    return pl.pallas_call(
        body, name=name, grid=(N // tn, M // tm), in_specs=in_specs,
        out_specs=pl.BlockSpec((tm, tn), lambda n, m: (m, n)),
        out_shape=jax.ShapeDtypeStruct((M, N), out_dtype),
        compiler_params=_cp(("parallel", "parallel")),
    )(*args)


def mm_res_norm(a, w, res, g, name):
    pieces = a if isinstance(a, tuple) else (a,)
    na = len(pieces)
    M, K = pieces[0].shape[0], sum(p.shape[1] for p in pieces)
    tm = min(MM_ROWS, M)

    def body(*refs):
        w_ref, r_ref, g_ref, o_ref, n_ref = refs[na:]
        h = r_ref[...] + _dg(_cols(refs[:na]).astype(bf16), w_ref[...], 1, 0)
        o_ref[...] = h
        n_ref[...] = _rms(h, g_ref[...]).astype(bf16)

    tok = pl.BlockSpec((tm, D), lambda m: (m, 0))
    return pl.pallas_call(
        body, name=name, grid=(M // tm,),
        in_specs=[pl.BlockSpec((tm, p.shape[1]), lambda m: (m, 0)) for p in pieces]
        + [pl.BlockSpec((K, D), lambda m: (0, 0)), tok, pl.BlockSpec((1, D), lambda m: (0, 0))],
        out_specs=[tok, tok],
        out_shape=[jax.ShapeDtypeStruct((M, D), f32), jax.ShapeDtypeStruct((M, D), bf16)],
        compiler_params=_cp(("parallel",)),
    )(*pieces, w, res, g.reshape(1, D))


def mm_nt(dy, w, out_dtype=f32, name="mm_nt"):
    M, N = dy.shape
    K = w.shape[0]
    tm, tn = min(MM_ROWS, M), _tile(N, 1024)
    assert out_dtype == f32 or tn == N

    def body(dy_ref, w_ref, o_ref):
        part = _dg(dy_ref[...].astype(bf16), w_ref[...], 1, 1)
        if tn == N:
            o_ref[...] = part.astype(out_dtype)
        else:
            @pl.when(pl.program_id(1) == 0)
            def _():
                o_ref[...] = jnp.zeros_like(o_ref)
            o_ref[...] += part

    return pl.pallas_call(
        body, name=name, grid=(M // tm, N // tn),
        in_specs=[pl.BlockSpec((tm, tn), lambda m, n: (m, n)), pl.BlockSpec((K, tn), lambda m, n: (0, n))],
        out_specs=pl.BlockSpec((tm, K), lambda m, n: (m, 0)),
        out_shape=jax.ShapeDtypeStruct((M, K), out_dtype),
        compiler_params=_cp(("parallel", "arbitrary")),
    )(dy, w)


NORM_ROWS = 1024


def _acc_then_norm_bwd(part, steps, h_ref, g_ref, r_ref, o_ref, dg_ref):
    k = pl.program_id(1)

    @pl.when((pl.program_id(0) == 0) & (k == 0))
    def _():
        dg_ref[...] = jnp.zeros_like(dg_ref)

    @pl.when(k == 0)
    def _():
        o_ref[...] = part

    @pl.when(k > 0)
    def _():
        o_ref[...] += part

    @pl.when(k == steps - 1)
    def _():
        _, vjp = jax.vjp(_rms, h_ref[...], g_ref[...])
        dh, dg = vjp(o_ref[...])
        o_ref[...] = r_ref[...] + dh
        dg_ref[...] += dg


def mm_nt_norm(dy, w, norm, name):
    pieces = dy if isinstance(dy, tuple) else (dy,)
    nd = len(pieces)
    M, N = pieces[0].shape[0], sum(p.shape[1] for p in pieces)
    tm, tn = (min(NORM_ROWS, M), _tile(N, 1024)) if nd == 1 else (min(512, M), N)

    def body(*refs):
        w_ref, h_ref, g_ref, r_ref, o_ref, dg_ref = refs[nd:]
        _acc_then_norm_bwd(_dg(_cols(refs[:nd]).astype(bf16), w_ref[...], 1, 1), N // tn, h_ref, g_ref, r_ref, o_ref,
                           dg_ref)

    tok = pl.BlockSpec((tm, D), lambda m, n: (m, 0))
    vec = pl.BlockSpec((1, D), lambda m, n: (0, 0))
    return pl.pallas_call(
        body, name=name, grid=(M // tm, N // tn),
        in_specs=[pl.BlockSpec((tm, tn if nd == 1 else p.shape[1]), lambda m, n: (m, n)) for p in pieces]
        + [pl.BlockSpec((D, tn), lambda m, n: (0, n)), tok, vec, tok],
        out_specs=[tok, vec],
        out_shape=[jax.ShapeDtypeStruct((M, D), f32), jax.ShapeDtypeStruct((1, D), f32)],
        compiler_params=_cp(("arbitrary", "arbitrary")),
    )(*pieces, w, norm[0], norm[1].reshape(1, D), norm[2])


def _cols(refs):
    return refs[0][...] if len(refs) == 1 else jnp.concatenate([r[...] for r in refs], axis=1)


def mm_tn(a, dy, name="mm_tn"):
    pieces = a if isinstance(a, tuple) else (a,)
    dpieces = dy if isinstance(dy, tuple) else (dy,)
    na, nd = len(pieces), len(dpieces)
    M, K = pieces[0].shape[0], sum(p.shape[1] for p in pieces)
    N = sum(p.shape[1] for p in dpieces)
    tm = min(MM_ROWS, M)
    tk = _tile(K, 1408) if na == 1 else K
    tn = _tile(N, 1024) if nd == 1 else N

    def body(*refs):
        o_ref = refs[-1]

        @pl.when(pl.program_id(2) == 0)
        def _():
            o_ref[...] = jnp.zeros_like(o_ref)
        o_ref[...] += _dg(_cols(refs[:na]).astype(bf16), _cols(refs[na:na + nd]).astype(bf16), 0, 0)

    a_specs = [pl.BlockSpec((tm, tk if na == 1 else p.shape[1]), lambda k, n, m: (m, k)) for p in pieces]
    d_specs = [pl.BlockSpec((tm, tn if nd == 1 else p.shape[1]), lambda k, n, m: (m, n)) for p in dpieces]
    return pl.pallas_call(
        body, name=name, grid=(K // tk, N // tn, M // tm), in_specs=a_specs + d_specs,
        out_specs=pl.BlockSpec((tk, tn), lambda k, n, m: (k, n)),
        out_shape=jax.ShapeDtypeStruct((K, N), f32),
        compiler_params=_cp(("parallel", "parallel", "arbitrary")),
    )(*pieces, *dpieces)


def rms_fwd(h, g, name):
    S = h.shape[0]
    t = min(512, S)

    def body(h_ref, g_ref, o_ref):
        o_ref[...] = _rms(h_ref[...], g_ref[...]).astype(bf16)

    return pl.pallas_call(
        body, name=name, grid=(S // t,),
        in_specs=[pl.BlockSpec((t, D), lambda i: (i, 0)), pl.BlockSpec((1, D), lambda i: (0, 0))],
        out_specs=pl.BlockSpec((t, D), lambda i: (i, 0)),
        out_shape=jax.ShapeDtypeStruct((S, D), bf16),
        compiler_params=_cp(("parallel",)),
    )(h, g.reshape(1, D))


def loss_head(h, g, target):
    S = h.shape[0]
    t = min(512, S)

    def f(hh, gg, tt):
        err = _rms(hh, gg) - tt
        return 0.5 * jnp.sum(jnp.mean(err * err, axis=-1, keepdims=True), axis=0, keepdims=True)

    def body(h_ref, g_ref, t_ref, loss_ref, dh_ref, dg_ref):
        @pl.when(pl.program_id(0) == 0)
        def _():
            dg_ref[...] = jnp.zeros_like(dg_ref)
            loss_ref[...] = jnp.zeros_like(loss_ref)
        val, vjp = jax.vjp(lambda a, b: f(a, b, t_ref[...]), h_ref[...], g_ref[...])
        dh, dg = vjp(jnp.ones((1, 1), f32))
        dh_ref[...] = dh
        dg_ref[...] += dg
        loss_ref[...] += jnp.broadcast_to(val, loss_ref.shape)

    tok = pl.BlockSpec((t, D), lambda i: (i, 0))
    vec = pl.BlockSpec((1, D), lambda i: (0, 0))
    return pl.pallas_call(
        body, name="loss_head", grid=(S // t,), in_specs=[tok, vec, tok],
        out_specs=[pl.BlockSpec((1, LANE), lambda i: (0, 0)), tok, vec],
        out_shape=[jax.ShapeDtypeStruct((1, LANE), f32), jax.ShapeDtypeStruct((S, D), f32),
                   jax.ShapeDtypeStruct((1, D), f32)],
        compiler_params=_cp(("arbitrary",)),
    )(h, g.reshape(1, D), target)


def memkv_fwd(mem, g, w, name):
    def body(m_ref, g_ref, w_ref, o_ref):
        o_ref[...] = _dg(_rms(m_ref[...], g_ref[...]).astype(bf16), w_ref[...], 1, 0)

    return pl.pallas_call(
        body, name=name, out_shape=jax.ShapeDtypeStruct((MEM_LEN, 2 * X_Q), f32), compiler_params=_cp(),
    )(mem, g.reshape(1, D), w)


def memkv_bwd(mem, g, w, dkv, name):
    def body(m_ref, g_ref, w_ref, d_ref, dg_ref, dw_ref):
        n, vjp = jax.vjp(lambda gg: _rms(m_ref[...], gg), g_ref[...])
        db = d_ref[...].astype(bf16)
        dw_ref[...] = _dg(n.astype(bf16), db, 0, 0)
        dg_ref[...] = vjp(_dg(db, w_ref[...], 1, 1))[0]

    return pl.pallas_call(
        body, name=name,
        out_shape=[jax.ShapeDtypeStruct((1, D), f32), jax.ShapeDtypeStruct((D, 2 * X_Q), f32)],
        compiler_params=_cp(),
    )(mem, g.reshape(1, D), w, dkv)


def _xattn_f(xq, mk, mv):
    lane = lax.broadcasted_iota(jnp.int32, (1, X_Q), 1)
    out = jnp.zeros(xq.shape, f32)
    for hd in range(4):
        msk = (lane // 64 == hd).astype(f32)
        s = bdot_nt(xq * msk, mk) * (64 ** -0.5)
        m = lax.stop_gradient(jnp.max(s, axis=-1, keepdims=True))
        p = jnp.exp(s - m)
        p = p / jnp.sum(p, axis=-1, keepdims=True)
        out = out + bdot(p, mv * msk)
    return out


def xattn_fwd(proj, col, kv, name):
    S = proj.shape[0]
    t = min(512, S)
    cb = col // X_Q

    def body(q_ref, k_ref, v_ref, o_ref):
        o_ref[...] = _xattn_f(q_ref[...], k_ref[...], v_ref[...]).astype(bf16)

    return pl.pallas_call(
        body, name=name, grid=(S // t,),
        in_specs=[pl.BlockSpec((t, X_Q), lambda i: (i, cb)), pl.BlockSpec((MEM_LEN, X_Q), lambda i: (0, 0)),
                  pl.BlockSpec((MEM_LEN, X_Q), lambda i: (0, 1))],
        out_specs=pl.BlockSpec((t, X_Q), lambda i: (i, 0)),
        out_shape=jax.ShapeDtypeStruct((S, X_Q), bf16),
        compiler_params=_cp(("parallel",)),
    )(proj, kv, kv)


def xattn_bwd(proj, col, kv, dmix, name):
    S = proj.shape[0]
    t = min(512, S)
    cb = col // X_Q

    def body(q_ref, k_ref, v_ref, do_ref, dq_ref, dk_ref, dv_ref):
        @pl.when(pl.program_id(0) == 0)
        def _():
            dk_ref[...] = jnp.zeros_like(dk_ref)
            dv_ref[...] = jnp.zeros_like(dv_ref)
        _, vjp = jax.vjp(_xattn_f, q_ref[...], k_ref[...], v_ref[...])
        dq, dk, dv = vjp(do_ref[...])
        dq_ref[...] = dq.astype(bf16)
        dk_ref[...] += dk
        dv_ref[...] += dv

    kvb = pl.BlockSpec((MEM_LEN, X_Q), lambda i: (0, 0))
    dq, dk, dv = pl.pallas_call(
        body, name=name, grid=(S // t,),
        in_specs=[pl.BlockSpec((t, X_Q), lambda i: (i, cb)), kvb,
                  pl.BlockSpec((MEM_LEN, X_Q), lambda i: (0, 1)), pl.BlockSpec((t, X_Q), lambda i: (i, 3))],
        out_specs=[pl.BlockSpec((t, X_Q), lambda i: (i, 0)), kvb, kvb],
        out_shape=[jax.ShapeDtypeStruct((S, X_Q), bf16), jax.ShapeDtypeStruct((MEM_LEN, X_Q), f32),
                   jax.ShapeDtypeStruct((MEM_LEN, X_Q), f32)],
        compiler_params=_cp(("arbitrary",)),
    )(proj, kv, kv, dmix)
    return dq, jnp.concatenate([dk, dv], axis=1)


def _bucket_map():
    qi = np.arange(BLK)[:, None]
    kj = np.arange(2 * BLK)[None, :]
    n = np.maximum(BLK + qi - kj, 0)
    max_exact = N_BUCKETS // 2
    nf = np.maximum(n, 1).astype(np.float64)
    large = max_exact + (np.log(nf / max_exact) / math.log(MAX_DIST / max_exact)
                         * (N_BUCKETS - max_exact)).astype(np.int32)
    large = np.minimum(large, N_BUCKETS - 1)
    return np.where(n < max_exact, n, large).astype(np.int32)


def bias_build(rel_bias):
    def body(rb_ref, bk_ref, o_ref):
        bk = bk_ref[...]
        for h in range(A_HEADS):
            acc = jnp.zeros((BLK, 2 * BLK), f32)
            for b in range(N_BUCKETS):
                acc = jnp.where(bk == b, rb_ref[b, h], acc)
            o_ref[h] = acc

    return pl.pallas_call(
        body, name="bias_build",
        in_specs=[pl.BlockSpec(memory_space=pltpu.SMEM), pl.BlockSpec(memory_space=pltpu.VMEM)],
        out_specs=pl.BlockSpec(memory_space=pltpu.VMEM),
        out_shape=jax.ShapeDtypeStruct((A_HEADS, BLK, 2 * BLK), f32), compiler_params=_cp(),
    )(rel_bias, jnp.asarray(_bucket_map()))


def bias_grad(dbias):
    def body(d_ref, bk_ref, o_ref):
        bk = bk_ref[...]
        row = lax.broadcasted_iota(jnp.int32, (N_BUCKETS, LANE), 0)
        lane = lax.broadcasted_iota(jnp.int32, (N_BUCKETS, LANE), 1)
        acc = jnp.zeros((N_BUCKETS, LANE), f32)
        for h in range(A_HEADS):
            d = d_ref[h]
            for b in range(N_BUCKETS):
                s = jnp.sum(jnp.where(bk == b, d, 0.0), keepdims=True)
                acc = acc + jnp.where((row == b) & (lane == h), s, 0.0)
        o_ref[...] = acc

    return pl.pallas_call(
        body, name="bias_grad", out_shape=jax.ShapeDtypeStruct((N_BUCKETS, LANE), f32), compiler_params=_cp(),
    )(dbias, jnp.asarray(_bucket_map()))


def _swa_f(qb, kp, kc, vp, vc, bias, sk, first):
    kband = jnp.concatenate([kp, kc], axis=0)
    vband = jnp.concatenate([vp, vc], axis=0)
    qi = lax.broadcasted_iota(jnp.int32, (BLK, 2 * BLK), 0)
    kj = lax.broadcasted_iota(jnp.int32, (BLK, 2 * BLK), 1)
    rel = kj - qi
    ok = (rel >= 1) & (rel <= BLK) & ((kj >= BLK) | jnp.logical_not(first))
    lane = lax.broadcasted_iota(jnp.int32, (1, LANE), 1)
    lane_b = lax.broadcasted_iota(jnp.int32, (BLK, LANE), 1)
    outs = []
    for p in range(A_HEADS // 2):
        qp = qb[:, LANE * p:LANE * (p + 1)]
        acc = jnp.zeros((BLK, LANE), f32)
        for g in range(2):
            h = g * (A_HEADS // 2) + p
            msk = (lane // A_DH == g).astype(f32)
            s = bdot_nt(qp * msk, kband) * (A_DH ** -0.5) + bias[h]
            s = jnp.where(ok, s, -1e30)
            skb = jnp.broadcast_to(sk[h:h + 1, :], (BLK, LANE))
            sink = jnp.sum(jnp.where(lane_b == 0, skb, 0.0), axis=-1, keepdims=True)
            m = lax.stop_gradient(jnp.maximum(jnp.max(s, axis=-1, keepdims=True), sink))
            e = jnp.exp(s - m)
            prob = e / (jnp.sum(e, axis=-1, keepdims=True) + jnp.exp(sink - m))
            acc = acc + bdot(prob, vband) * msk
        outs.append(acc)
    return jnp.concatenate(outs, axis=1)


def _swa_specs(nb, rev):
    bi = (lambda i: nb - 1 - i) if rev else (lambda i: i)
    return [
        pl.BlockSpec((BLK, A_Q), lambda i: (bi(i), 0)),
        pl.BlockSpec((BLK, LANE), lambda i: (jnp.maximum(bi(i) - 1, 0), 6)),
        pl.BlockSpec((BLK, LANE), lambda i: (bi(i), 6)),
        pl.BlockSpec((BLK, LANE), lambda i: (jnp.maximum(bi(i) - 1, 0), 7)),
        pl.BlockSpec((BLK, LANE), lambda i: (bi(i), 7)),
        pl.BlockSpec((A_HEADS, BLK, 2 * BLK), lambda i: (0, 0, 0)),
        pl.BlockSpec((16, LANE), lambda i: (0, 0)),
    ]


def swa_fwd(proj, bias, sk):
    S = proj.shape[0]
    nb = S // BLK

    def body(q_ref, kp_ref, kc_ref, vp_ref, vc_ref, b_ref, s_ref, o_ref):
        o_ref[...] = _swa_f(q_ref[...], kp_ref[...], kc_ref[...], vp_ref[...], vc_ref[...], b_ref[...], s_ref[...],
                            pl.program_id(0) == 0).astype(bf16)

    return pl.pallas_call(
        body, name="swa_fwd", grid=(nb,), in_specs=_swa_specs(nb, False),
        out_specs=pl.BlockSpec((BLK, A_Q), lambda i: (i, 0)),
        out_shape=jax.ShapeDtypeStruct((S, A_Q), bf16), compiler_params=_cp(("parallel",)),
    )(proj, proj, proj, proj, proj, bias, sk)


def swa_bwd(proj, bias, sk, dmix):
    S = proj.shape[0]
    nb = S // BLK

    def body(q_ref, kp_ref, kc_ref, vp_ref, vc_ref, b_ref, s_ref, do_ref, dqkv_ref, db_ref, ds_ref, ck, cv):
        i = pl.program_id(0)

        @pl.when(i == 0)
        def _():
            db_ref[...] = jnp.zeros_like(db_ref)
            ds_ref[...] = jnp.zeros_like(ds_ref)
            ck[...] = jnp.zeros_like(ck)
            cv[...] = jnp.zeros_like(cv)
        first = i == nb - 1
        _, vjp = jax.vjp(lambda *a: _swa_f(*a, first), q_ref[...], kp_ref[...], kc_ref[...], vp_ref[...],
                         vc_ref[...], b_ref[...], s_ref[...])
        dq, dkp, dkc, dvp, dvc, db, ds = vjp(do_ref[...])
        dqkv_ref[...] = jnp.concatenate([dq, dkc + ck[...], dvc + cv[...]], axis=1).astype(bf16)
        ck[...] = dkp
        cv[...] = dvp
        db_ref[...] += db
        ds_ref[...] += ds

    return pl.pallas_call(
        body, name="swa_bwd", grid=(nb,),
        in_specs=_swa_specs(nb, True) + [pl.BlockSpec((BLK, A_Q), lambda i: (nb - 1 - i, 0))],
        out_specs=[pl.BlockSpec((BLK, D), lambda i: (nb - 1 - i, 0)),
                   pl.BlockSpec((A_HEADS, BLK, 2 * BLK), lambda i: (0, 0, 0)),
                   pl.BlockSpec((16, LANE), lambda i: (0, 0))],
        out_shape=[jax.ShapeDtypeStruct((S, D), bf16), jax.ShapeDtypeStruct((A_HEADS, BLK, 2 * BLK), f32),
                   jax.ShapeDtypeStruct((16, LANE), f32)],
        scratch_shapes=[pltpu.VMEM((BLK, LANE), f32), pltpu.VMEM((BLK, LANE), f32)],
        compiler_params=_cp(("arbitrary",)),
    )(proj, proj, proj, proj, proj, bias, sk, dmix)


def _dnprep_f(xext, w, is_qk):
    c = (w[3:4] * xext + w[2:3] * shift_down(xext, 1) + w[1:2] * shift_down(xext, 2) + w[0:1] * shift_down(xext, 3))
    a = _silu(c)[HALO:]
    n = a * lax.rsqrt(jnp.sum(a * a, axis=-1, keepdims=True) + EPS)
    return jnp.where(is_qk, n, a)


def dnprep_fwd(proj, cw):
    S = proj.shape[0]
    nblk = B_QKV // LANE
    T = S

    def body(x_ref, w_ref, o_ref):
        is_qk = pl.program_id(0) < 2 * B_QK // LANE
        wv = w_ref[...]

        def tile(r0, first):
            o_ref[pl.ds(r0, T), :] = _dnprep_f(_glu_gext(x_ref, r0, first, T), wv, is_qk)

        tile(0, True)

    return pl.pallas_call(
        body, name="dnprep_fwd", grid=(nblk,),
        in_specs=[pl.BlockSpec((S, LANE), lambda j: (0, j)), pl.BlockSpec((4, LANE), lambda j: (0, j))],
        out_specs=pl.BlockSpec((S, LANE), lambda j: (0, j)),
        out_shape=jax.ShapeDtypeStruct((S, B_QKV), f32), compiler_params=_cp(("parallel",)),
    )(proj, cw)


def dnprep_bwd(proj, cw, dqkvn):
    S = proj.shape[0]
    nblk = B_QKV // LANE

    T = S

    def body(x_ref, w_ref, d_ref, dx_ref, dw_ref):
        is_qk = pl.program_id(0) < 2 * B_QK // LANE
        wv = w_ref[...]

        def tile(r0, first):
            _, vjp = jax.vjp(lambda a, b: _dnprep_f(a, b, is_qk), _glu_gext(x_ref, r0, first, T), wv)
            dx, dw = vjp(d_ref[pl.ds(r0, T), :])
            dx_ref[pl.ds(r0, T), :] = dx[HALO:].astype(bf16)
            if not first:
                dx_ref[pl.ds(r0 - HALO, HALO), :] += dx[:HALO]
            return dw

        dw_ref[...] = tile(0, True)

    col = pl.BlockSpec((S, LANE), lambda j: (0, j))
    wsp = pl.BlockSpec((4, LANE), lambda j: (0, j))
    return pl.pallas_call(
        body, name="dnprep_bwd", grid=(nblk,), in_specs=[col, wsp, col], out_specs=[col, wsp],
        out_shape=[jax.ShapeDtypeStruct((S, B_QKV), bf16), jax.ShapeDtypeStruct((4, B_QKV), f32)],
        compiler_params=_cp(("parallel",)),
    )(proj, cw, dqkvn)


def _hdot(a, b, ca=1, cb=0):
    return _dg(a, b, ca, cb, HI)


def _bdg(a, b, ca, cb):
    dn = (((ca,), (cb,)), ((0,), (0,)))
    ah, bh = a.astype(bf16), b.astype(bf16)
    al, bl = (a - ah.astype(f32)).astype(bf16), (b - bh.astype(f32)).astype(bf16)
    return (lax.dot_general(ah, bh, dn, preferred_element_type=f32)
            + lax.dot_general(ah, bl, dn, preferred_element_type=f32)
            + lax.dot_general(al, bh, dn, preferred_element_type=f32))


@jax.custom_vjp
def hbd(a, b):
    return _bdg(a, b, 2, 1)


@jax.custom_vjp
def hbd_nt(a, b):
    return _bdg(a, b, 2, 2)


@jax.custom_vjp
def hbd_tn(a, b):
    return _bdg(a, b, 1, 1)


hbd.defvjp(lambda a, b: (hbd(a, b), (a, b)), lambda r, g: (hbd_nt(g, r[1]), hbd_tn(r[0], g)))
hbd_nt.defvjp(lambda a, b: (hbd_nt(a, b), (a, b)), lambda r, g: (hbd(g, r[1]), hbd_tn(g, r[0])))
hbd_tn.defvjp(lambda a, b: (hbd_tn(a, b), (a, b)), lambda r, g: (hbd_nt(r[1], g), hbd(r[0], g)))


def _stack(xs):
    return jnp.concatenate([x[None] for x in xs], axis=0)


def _lane_col(x, j):
    lane = lax.broadcasted_iota(jnp.int32, (1, LANE), 1)
    return jnp.sum(jnp.where(lane == j, x, 0.0), axis=-1, keepdims=True)


def _tri_inv(a_mat):
    r = lax.broadcasted_iota(jnp.int32, (1, CHUNK, CHUNK), 1)
    c = lax.broadcasted_iota(jnp.int32, (1, CHUNK, CHUNK), 2)
    pw = -a_mat
    inv = (r == c).astype(f32) + pw
    for _ in range(5):
        pw = hbd(pw, pw)
        inv = inv + hbd(inv, pw)
    return inv


@jax.custom_vjp
def _tri_inv_known(a_mat, inv):
    return inv


_tri_inv_known.defvjp(lambda a, inv: (inv, inv),
                      lambda inv, g: (-hbd_tn(inv, hbd_nt(g, inv)), jnp.zeros_like(inv)))


def _dnc_f(q, k, v, seg, prm, inverse=_tri_inv):
    C = CHUNK
    B = q.shape[0]
    rows = seg.shape[0]
    beta_all = _sigmoid(seg)
    xx = seg + prm[1:2]
    g_all = -jnp.exp(prm[0:1]) * (jnp.maximum(xx, 0.0) + jnp.log(1.0 + jnp.exp(-jnp.abs(xx))))
    r2 = lax.broadcasted_iota(jnp.int32, (rows, rows), 0)
    c2 = lax.broadcasted_iota(jnp.int32, (rows, rows), 1)
    within = (r2 >= c2) & (r2 // C == c2 // C)
    gc_all = _hdot(within.astype(f32), g_all)
    beta = _stack([_lane_col(beta_all[C * j:C * (j + 1)], h) for j in range(rows // C) for h in range(6)])
    gc = _stack([_lane_col(gc_all[C * j:C * (j + 1)], 6 + h) for j in range(rows // C) for h in range(6)])
    r = lax.broadcasted_iota(jnp.int32, (1, C, C), 1)
    c = lax.broadcasted_iota(jnp.int32, (1, C, C), 2)
    incl = r >= c
    strict = r > c
    gct = [gc_all[C * j:C * (j + 1)].T for j in range(rows // C)]
    g_row = _stack([jnp.broadcast_to(gct[j][6 + h:7 + h, :], (C, C))
                    for j in range(rows // C) for h in range(6)])
    decay = jnp.where(incl, jnp.exp(jnp.where(incl, gc - g_row, 0.0)), 0.0)
    a_mat = beta * sbd_nt(k, k) * jnp.where(strict, decay, 0.0)
    eg = jnp.exp(gc)
    inv = inverse(a_mat)
    u = hbd(inv, beta * v)
    w = hbd(inv, (beta * eg) * k)
    qc = q * (B_DH ** -0.5)
    attn = sbd_nt(qc, k) * decay
    last = (lax.broadcasted_iota(jnp.int32, (1, C, 1), 1) == C - 1).astype(f32)
    g_last = jnp.sum(gc * last, axis=1, keepdims=True)
    dc = jnp.broadcast_to(jnp.exp(g_last), (B, 1, LANE)).reshape(B, LANE)
    return u, w, qc * eg, k * jnp.exp(g_last - gc), attn, dc, inv


def _b1(a, b, ca, cb):
    return lax.dot_general(a.astype(bf16), b.astype(bf16), (((ca,), (cb,)), ((0,), (0,))), preferred_element_type=f32)


@jax.custom_vjp
def sbd(a, b):
    return _b1(a, b, 2, 1)


@jax.custom_vjp
def sbd_nt(a, b):
    return _b1(a, b, 2, 2)


@jax.custom_vjp
def sbd_tn(a, b):
    return _b1(a, b, 1, 1)


sbd.defvjp(lambda a, b: (sbd(a, b), (a, b)), lambda r, g: (sbd_nt(g, r[1]), sbd_tn(r[0], g)))
sbd_nt.defvjp(lambda a, b: (sbd_nt(a, b), (a, b)), lambda r, g: (sbd(g, r[1]), sbd_tn(g, r[0])))
sbd_tn.defvjp(lambda a, b: (sbd_tn(a, b), (a, b)), lambda r, g: (sbd_nt(r[1], g), sbd(r[0], g)))


def _dns_f(S0, u, w, qd, kt, attn, dcrows):
    dc = _lane_col(dcrows, 0).reshape(6, 1, 1)
    delta = u - sbd(w, S0)
    out = sbd(qd, S0) + sbd(attn, delta)
    return out, dc * S0 + sbd_tn(kt, delta)


def _dnpost_f(o, z, grow):
    outs = []
    for h in range(6):
        oh = o[:, LANE * h:LANE * (h + 1)]
        outs.append(oh * lax.rsqrt(jnp.mean(oh * oh, axis=-1, keepdims=True) + EPS) * grow
                    * _silu(z[:, LANE * h:LANE * (h + 1)]))
    return jnp.concatenate(outs, axis=1)


def _hs(h):
    return slice(LANE * h, LANE * (h + 1))


DN_CHUNKS = 4


def _heads(ref, share):
    return _stack([ref[CHUNK * j:CHUNK * (j + 1), _hs(h // share)]
                   for j in range(ref.shape[0] // CHUNK) for h in range(6)])


def _put_heads(ref, val):
    for j in range(ref.shape[0] // CHUNK):
        for h in range(6):
            ref[CHUNK * j:CHUNK * (j + 1), _hs(h)] = val[6 * j + h]


def _dnc_in_specs():
    rows = CHUNK * DN_CHUNKS
    return [
        pl.BlockSpec((rows, B_QK), lambda n: (n, 0)),
        pl.BlockSpec((rows, B_QK), lambda n: (n, 1)),
        pl.BlockSpec((rows, B_V), lambda n: (n, 1)),
        pl.BlockSpec((rows, LANE), lambda n: (n, 20)),
        pl.BlockSpec((8, LANE), lambda n: (0, 0)),
    ]


def _dnc_out_specs(rev_nc=None, chunks=1):
    ci = (lambda n: n) if rev_nc is None else (lambda n: rev_nc - 1 - n)
    wide = pl.BlockSpec((CHUNK * chunks, B_V), lambda n: (ci(n), 0))
    return [wide, wide, wide, wide, pl.BlockSpec((chunks, 6, CHUNK, CHUNK), lambda n: (ci(n), 0, 0, 0)),
            pl.BlockSpec((chunks, 8, LANE), lambda n: (ci(n), 0, 0))]


def _dc_rows(dc):
    pad = jnp.zeros((2, LANE), f32)
    return _stack([jnp.concatenate([dc[6 * j:6 * (j + 1)], pad], axis=0) for j in range(dc.shape[0] // 6)])


def _dnc_shapes(S):
    nc = S // CHUNK
    wide = jax.ShapeDtypeStruct((S, B_V), f32)
    return [wide, wide, wide, wide, jax.ShapeDtypeStruct((nc, 6, CHUNK, CHUNK), f32),
            jax.ShapeDtypeStruct((nc, 8, LANE), f32)]


def dnc_fwd(qkvn, proj, prm):
    S = proj.shape[0]

    def body(q_ref, k_ref, v_ref, s_ref, p_ref, u_ref, w_ref, qd_ref, kt_ref, at_ref, dc_ref, inv_ref):
        u, w, qd, kt, attn, dc, inv = _dnc_f(_heads(q_ref, 2), _heads(k_ref, 2), _heads(v_ref, 1), s_ref[...],
                                             p_ref[...])
        inv_ref[...] = inv.reshape(inv_ref.shape)
        _put_heads(u_ref, u)
        _put_heads(w_ref, w)
        _put_heads(qd_ref, qd)
        _put_heads(kt_ref, kt)
        at_ref[...] = attn.reshape(at_ref.shape)
        dc_ref[...] = _dc_rows(dc)

    outs = _dnc_out_specs(chunks=DN_CHUNKS)
    out = pl.pallas_call(
        body, name="dn_chunk_fwd", grid=(S // (CHUNK * DN_CHUNKS),), in_specs=_dnc_in_specs(),
        out_specs=outs + [outs[4]], out_shape=_dnc_shapes(S) + [_dnc_shapes(S)[4]],
        compiler_params=_cp(("parallel",)),
    )(qkvn, qkvn, qkvn, proj, prm)
    return out[:6], out[6]


def dnc_bwd(qkvn, proj, prm, inv, cots):
    S = proj.shape[0]

    def body(q_ref, k_ref, v_ref, s_ref, p_ref, inv_ref, du_ref, dw_ref, dqd_ref, dkt_ref, dat_ref, ddc_ref,
             dx_ref, dseg_ref, dprm_ref):
        @pl.when(pl.program_id(0) == 0)
        def _():
            dprm_ref[...] = jnp.zeros_like(dprm_ref)
        nb = 6 * DN_CHUNKS
        known = functools.partial(_tri_inv_known, inv=inv_ref[...].reshape(nb, CHUNK, CHUNK))
        _, vjp = jax.vjp(lambda *a: _dnc_f(*a, inverse=known)[:6], _heads(q_ref, 2), _heads(k_ref, 2),
                         _heads(v_ref, 1), s_ref[...], p_ref[...])
        ddc = jnp.concatenate([ddc_ref[j, 0:6, :] for j in range(DN_CHUNKS)], axis=0)
        dq, dk, dv, dseg, dprm = vjp((_heads(du_ref, 1), _heads(dw_ref, 1), _heads(dqd_ref, 1), _heads(dkt_ref, 1),
                                      dat_ref[...].reshape(nb, CHUNK, CHUNK), ddc))
        for j in range(DN_CHUNKS):
            o = 6 * j
            dx_ref[CHUNK * j:CHUNK * (j + 1), :] = jnp.concatenate(
                [dq[o] + dq[o + 1], dq[o + 2] + dq[o + 3], dq[o + 4] + dq[o + 5],
                 dk[o] + dk[o + 1], dk[o + 2] + dk[o + 3], dk[o + 4] + dk[o + 5]] + [dv[o + h] for h in range(6)], axis=1)
        dseg_ref[...] = dseg.astype(bf16)
        dprm_ref[...] += dprm

    rows = CHUNK * DN_CHUNKS
    outs = _dnc_out_specs(chunks=DN_CHUNKS)
    return pl.pallas_call(
        body, name="dn_chunk_bwd", grid=(S // rows,),
        in_specs=_dnc_in_specs() + [outs[4]] + outs,
        out_specs=[pl.BlockSpec((rows, B_QKV), lambda n: (n, 0)), pl.BlockSpec((rows, LANE), lambda n: (n, 0)),
                   pl.BlockSpec((8, LANE), lambda n: (0, 0))],
        out_shape=[jax.ShapeDtypeStruct((S, B_QKV), f32), jax.ShapeDtypeStruct((S, LANE), bf16),
                   jax.ShapeDtypeStruct((8, LANE), f32)],
        compiler_params=_cp(("arbitrary",)),
    )(qkvn, qkvn, qkvn, proj, prm, inv, *cots)


def dns_fwd(chunked):
    u = chunked[0]
    S = u.shape[0]
    nc = S // CHUNK

    def body(u_ref, w_ref, qd_ref, kt_ref, at_ref, dc_ref, o_ref, st_ref, st):
        @pl.when(pl.program_id(0) == 0)
        def _():
            st[...] = jnp.zeros_like(st)
        S0 = st[...]
        st_ref[0] = S0
        out, S1 = _dns_f(S0, _heads(u_ref, 1), _heads(w_ref, 1), _heads(qd_ref, 1), _heads(kt_ref, 1),
                         at_ref[0], dc_ref[0, 0:6, :])
        _put_heads(o_ref, out)
        st[...] = S1

    return pl.pallas_call(
        body, name="dn_scan_fwd", grid=(nc,), in_specs=_dnc_out_specs(),
        out_specs=[pl.BlockSpec((CHUNK, B_V), lambda n: (n, 0)),
                   pl.BlockSpec((1, 6, B_DH, B_DH), lambda n: (n, 0, 0, 0))],
        out_shape=[jax.ShapeDtypeStruct((S, B_V), f32), jax.ShapeDtypeStruct((nc, 6, B_DH, B_DH), f32)],
        scratch_shapes=[pltpu.VMEM((6, B_DH, B_DH), f32)],
        compiler_params=_cp(("arbitrary",)),
    )(*chunked)


def dns_bwd(chunked, states, do):
    S = do.shape[0]
    nc = S // CHUNK

    def body(u_ref, w_ref, qd_ref, kt_ref, at_ref, dc_ref, st_ref, do_ref,
             du_ref, dw_ref, dqd_ref, dkt_ref, dat_ref, ddc_ref, dst):
        @pl.when(pl.program_id(0) == 0)
        def _():
            dst[...] = jnp.zeros_like(dst)
        _, vjp = jax.vjp(_dns_f, st_ref[0], _heads(u_ref, 1), _heads(w_ref, 1), _heads(qd_ref, 1), _heads(kt_ref, 1),
                         at_ref[0], dc_ref[0, 0:6, :])
        dS0, du, dw, dqd, dkt, dat, ddc = vjp((_heads(do_ref, 1), dst[...]))
        dst[...] = dS0
        _put_heads(du_ref, du)
        _put_heads(dw_ref, dw)
        _put_heads(dqd_ref, dqd)
        _put_heads(dkt_ref, dkt)
        dat_ref[0] = dat
        ddc_ref[0] = jnp.concatenate([ddc, jnp.zeros((2, LANE), f32)], axis=0)

    return pl.pallas_call(
        body, name="dn_scan_bwd", grid=(nc,),
        in_specs=_dnc_out_specs(nc) + [pl.BlockSpec((1, 6, B_DH, B_DH), lambda n: (nc - 1 - n, 0, 0, 0)),
                                       pl.BlockSpec((CHUNK, B_V), lambda n: (nc - 1 - n, 0))],
        out_specs=_dnc_out_specs(nc), out_shape=_dnc_shapes(S),
        scratch_shapes=[pltpu.VMEM((6, B_DH, B_DH), f32)],
        compiler_params=_cp(("arbitrary",)),
    )(*chunked, states, do)


def dnpost_fwd(o, proj, prm):
    S = o.shape[0]
    t = min(512, S)

    def body(o_ref, z_ref, p_ref, y_ref):
        y_ref[...] = _dnpost_f(o_ref[...], z_ref[...], p_ref[2:3, :]).astype(bf16)

    tok = pl.BlockSpec((t, B_V), lambda i: (i, 0))
    return pl.pallas_call(
        body, name="dn_post_fwd", grid=(S // t,),
        in_specs=[tok, pl.BlockSpec((t, B_V), lambda i: (i, 2)), pl.BlockSpec((8, LANE), lambda i: (0, 0))],
        out_specs=tok, out_shape=jax.ShapeDtypeStruct((S, B_V), bf16), compiler_params=_cp(("parallel",)),
    )(o, proj, prm)


def dnpost_bwd(o, proj, prm, dmix):
    S = o.shape[0]
    t = min(512, S)

    def body(o_ref, z_ref, p_ref, dy_ref, do_ref, dz_ref, dg_ref):
        @pl.when(pl.program_id(0) == 0)
        def _():
            dg_ref[...] = jnp.zeros_like(dg_ref)
        _, vjp = jax.vjp(_dnpost_f, o_ref[...], z_ref[...], p_ref[2:3, :])
        do, dz, dg = vjp(dy_ref[...])
        do_ref[...] = do
        dz_ref[...] = dz.astype(bf16)
        dg_ref[...] += dg

    tok = pl.BlockSpec((t, B_V), lambda i: (i, 0))
    return pl.pallas_call(
        body, name="dn_post_bwd", grid=(S // t,),
        in_specs=[tok, pl.BlockSpec((t, B_V), lambda i: (i, 2)), pl.BlockSpec((8, LANE), lambda i: (0, 0)), tok],
        out_specs=[tok, tok, pl.BlockSpec((1, LANE), lambda i: (0, 0))],
        out_shape=[jax.ShapeDtypeStruct((S, B_V), f32), jax.ShapeDtypeStruct((S, B_V), bf16),
                   jax.ShapeDtypeStruct((1, LANE), f32)],
        compiler_params=_cp(("arbitrary",)),
    )(o, proj, prm, dmix)


N_FF_BLK = D_FF // LANE
GU_SHARD = 2 * D_FF // 4


GLU_ROWS = 256
HALO = 16


def _glu_f(gext, up, w, b):
    c = w[2:3] * gext + w[1:2] * shift_down(gext, 1) + w[0:1] * shift_down(gext, 2) + b
    return _silu(c)[HALO:] * up


def _glu_gext(g_ref, r0, first, T=GLU_ROWS):
    if first:
        return jnp.concatenate([jnp.zeros((HALO, LANE), f32), g_ref[0:T, :].astype(f32)], axis=0)
    return g_ref[pl.ds(r0 - HALO, T + HALO), :].astype(f32)


def glu_fwd(gu, w, b, name):
    S = gu.shape[0]
    T = min(GLU_ROWS, S // 2)

    def body(g_ref, u_ref, w_ref, b_ref, o_ref):
        wv, bv = w_ref[...], b_ref[...]

        def tile(r0, first):
            act = _glu_f(_glu_gext(g_ref, r0, first, T), u_ref[pl.ds(r0, T), :].astype(f32), wv, bv)
            o_ref[pl.ds(r0, T), :] = act.astype(bf16)

        tile(0, True)

        @pl.loop(1, S // T)
        def _(t):
            tile(pl.multiple_of(t * T, T), False)

    col = pl.BlockSpec((S, LANE), lambda j: (0, j))
    return pl.pallas_call(
        body, name=name, grid=(N_FF_BLK,),
        in_specs=[col, pl.BlockSpec((S, LANE), lambda j: (0, N_FF_BLK + j)), pl.BlockSpec((3, LANE), lambda j: (0, j)),
                  pl.BlockSpec((1, LANE), lambda j: (0, j))],
        out_specs=col, out_shape=jax.ShapeDtypeStruct((S, D_FF), bf16), compiler_params=_cp(("parallel",)),
    )(gu, gu, w, b.reshape(1, D_FF))


def glu_bwd(gu, w, b, dact, name):
    S = gu.shape[0]
    T = min(GLU_ROWS, S // 2)

    def body(g_ref, u_ref, w_ref, b_ref, d_ref, dg_ref, dw_ref, db_ref, acc):
        wv, bv = w_ref[...], b_ref[...]

        def tile(r0, first):
            _, vjp = jax.vjp(_glu_f, _glu_gext(g_ref, r0, first, T), u_ref[pl.ds(r0, T), :].astype(f32), wv, bv)
            dgx, du, dw, db = vjp(d_ref[pl.ds(r0, T), :].astype(f32))
            acc[pl.ds(r0, T), :] = dgx[HALO:]
            if not first:
                acc[pl.ds(r0 - HALO, HALO), :] += dgx[:HALO]
            dg_ref[1, pl.ds(r0, T), :] = du.astype(bf16)
            return dw, db

        dw0, db0 = tile(0, True)
        dw_ref[...] = dw0
        db_ref[...] = db0

        @pl.loop(1, S // T)
        def _(t):
            dw, db = tile(pl.multiple_of(t * T, T), False)
            dw_ref[...] += dw
            db_ref[...] += db

        dg_ref[0] = acc[...].astype(bf16)

    col = pl.BlockSpec((S, LANE), lambda j: (0, j))
    wsp = pl.BlockSpec((3, LANE), lambda j: (0, j))
    bsp = pl.BlockSpec((1, LANE), lambda j: (0, j))
    return pl.pallas_call(
        body, name=name, grid=(N_FF_BLK,),
        in_specs=[col, pl.BlockSpec((S, LANE), lambda j: (0, N_FF_BLK + j)), wsp, bsp, col],
        out_specs=[pl.BlockSpec((2, S, LANE), lambda j: (0, 0, j)), wsp, bsp],
        out_shape=[jax.ShapeDtypeStruct((2, S, D_FF), bf16), jax.ShapeDtypeStruct((3, D_FF), f32),
                   jax.ShapeDtypeStruct((1, D_FF), f32)],
        scratch_shapes=[pltpu.VMEM((S, LANE), f32)],
        compiler_params=_cp(("parallel",)),
    )(gu, gu, w, b.reshape(1, D_FF), dact)


def gu_fwd(n2, wg, name):
    S = n2.shape[0]
    tm = min(MM_ROWS, S)

    def body(a_ref, w_ref, o_ref):
        o_ref[...] = _dg(a_ref[...], w_ref[...], 1, 0).astype(bf16)

    return pl.pallas_call(
        body, name=name, grid=(4, S // tm),
        in_specs=[pl.BlockSpec((tm, D), lambda s, m: (m, 0)), pl.BlockSpec((None, D, GU_SHARD), lambda s, m: (s, 0, 0))],
        out_specs=pl.BlockSpec((tm, GU_SHARD), lambda s, m: (m, s)),
        out_shape=jax.ShapeDtypeStruct((S, 2 * D_FF), bf16), compiler_params=_cp(("parallel", "parallel")),
    )(n2, wg)


def gu_bwd_x(dgu, wg, norm, name):
    S = dgu.shape[1]
    tm = min(NORM_ROWS, S)

    def body(d_ref, w_ref, h_ref, g_ref, r_ref, o_ref, dg_ref):
        _acc_then_norm_bwd(_dg(d_ref[...], w_ref[...], 1, 1), 4, h_ref, g_ref, r_ref, o_ref, dg_ref)

    tok = pl.BlockSpec((tm, D), lambda m, s: (m, 0))
    vec = pl.BlockSpec((1, D), lambda m, s: (0, 0))
    return pl.pallas_call(
        body, name=name, grid=(S // tm, 4),
        in_specs=[pl.BlockSpec((None, tm, GU_SHARD), lambda m, s: (s // 2, m, s % 2)),
                  pl.BlockSpec((None, D, GU_SHARD), lambda m, s: (s, 0, 0)), tok, vec, tok],
        out_specs=[tok, vec],
        out_shape=[jax.ShapeDtypeStruct((S, D), f32), jax.ShapeDtypeStruct((1, D), f32)],
        compiler_params=_cp(("arbitrary", "arbitrary")),
    )(dgu, wg, norm[0], norm[1].reshape(1, D), norm[2])


def gu_bwd_w(n2, dgu, name):
    S = n2.shape[0]
    tm = min(MM_ROWS, S)
    nm = S // tm

    def body(a_ref, d_ref, o_ref, acc):
        @pl.when(pl.program_id(1) == 0)
        def _():
            acc[...] = jnp.zeros_like(acc)
        acc[...] += _dg(a_ref[...], d_ref[...], 0, 0)

        @pl.when(pl.program_id(1) == nm - 1)
        def _():
            o_ref[...] = acc[...].astype(bf16)

    return pl.pallas_call(
        body, name=name, grid=(4, nm),
        in_specs=[pl.BlockSpec((tm, D), lambda s, m: (m, 0)),
                  pl.BlockSpec((None, tm, GU_SHARD), lambda s, m: (s // 2, m, s % 2))],
        out_specs=pl.BlockSpec((None, D, GU_SHARD), lambda s, m: (s, 0, 0)),
        out_shape=jax.ShapeDtypeStruct((4, D, GU_SHARD), bf16),
        scratch_shapes=[pltpu.VMEM((D, GU_SHARD), f32)],
        compiler_params=_cp(("parallel", "arbitrary")),
    )(n2, dgu)


def _pair_cols(w):
    lead = w.shape[:-1]
    return w.reshape(lead + (2, 6, A_DH)).swapaxes(-3, -2).reshape(lead + (A_Q,))


def _unpair_cols(w):
    lead = w.shape[:-1]
    return w.reshape(lead + (6, 2, A_DH)).swapaxes(-3, -2).reshape(lead + (A_Q,))


def _lay_in_a(w):
    return jnp.concatenate([_pair_cols(w[:, :A_Q]), w[:, A_Q:]], axis=1)


def _unlay_in_a(w):
    return jnp.concatenate([_unpair_cols(w[:, :A_Q]), w[:, A_Q:]], axis=1)


def _lay_out_a(w):
    return jnp.concatenate([_pair_cols(w[:A_Q].T).T, w[A_Q:]], axis=0)


def _unlay_out_a(w):
    return jnp.concatenate([_unpair_cols(w[:A_Q].T).T, w[A_Q:]], axis=0)


def _lay_in_b(w):
    return jnp.concatenate([w[:, :2304], w[:, 2316:], w[:, 2304:2316],
                            jnp.zeros((w.shape[0], LANE - 12), w.dtype)], axis=1)


def _unlay_in_b(w):
    return jnp.concatenate([w[:, :2304], w[:, 2560:2572], w[:, 2304:2560]], axis=1)


def _chip_cols(w):
    return jnp.moveaxis(w.reshape(w.shape[0], 4, w.shape[1] // 4), 1, 0)


def _unchip_cols(w):
    return jnp.moveaxis(w, 0, 1).reshape(w.shape[1], 4 * w.shape[2])


def _local_step(x, mem, target, P):
    arrive = P.get("arrive", lambda key, after: None)
    ready = P.get("ready", lambda key, grads, dep: dep)
    sk = jnp.zeros((16, LANE), f32).at[:A_HEADS].set(jnp.broadcast_to(P["sinks"][:, None], (A_HEADS, LANE)))
    prm = jnp.zeros((8, LANE), f32).at[0, 6:12].set(P["a_log"]).at[1, 6:12].set(P["dt_bias"]).at[2].set(P["out_norm_g"])
    bias = bias_build(P["rel_bias"])
    saved = []
    h = x
    n1 = rms_fwd(h, P["g_mix"][0], "rms_mix0")
    for i in range(2):
        arrive(("w_in", i), n1)
        proj = mm_nn(n1, P["w_in_a"] if i == 0 else P["w_in_b"], name="proj_a" if i == 0 else "proj_b")
        arrive(("w_mem", i), proj)
        kv = memkv_fwd(mem, P["g_mem"][i], P["w_mem"][i], f"memkv{i}")
        if i == 0:
            self_out = swa_fwd(proj, bias, sk)
            cross = xattn_fwd(proj, A_Q + 2 * LANE, kv, "xattn_a")
            extra = ()
        else:
            qkvn = dnprep_fwd(proj, P["conv_qkv"])
            chunked, inv = dnc_fwd(qkvn, proj, prm)
            o, states = dns_fwd(chunked)
            self_out = dnpost_fwd(o, proj, prm)
            cross = xattn_fwd(proj, 2304, kv, "xattn_b")
            extra = (qkvn, chunked, inv, states, o)
        mix = (self_out, cross)
        arrive(("w_out", i), cross)
        h2, n2 = mm_res_norm(mix, P["w_out"][i], h, P["g_ffn"][i], f"out_proj{i}")
        arrive(("w_gu", i), n2)
        gu = gu_fwd(n2, P["w_gu"][i], f"gate_up{i}")
        act = glu_fwd(gu, P["ffn_cw"][i], P["ffn_cb"][i], f"glu{i}")
        arrive(("w_down", i), act)
        saved.append((h, n1, kv, proj, mix, h2, n2, gu, act, extra))
        if i == 0:
            h, n1 = mm_res_norm(act, P["w_down"][i], h2, P["g_mix"][1], f"down{i}")
        else:
            h = mm_nn(act, P["w_down"][i], res=h2, name=f"down{i}")

    loss, dh, dg_fin = loss_head(h, P["g_fin"], target)
    G = {"g_fin": dg_fin[0], "g_mix": [None, None], "g_mem": [None, None], "g_ffn": [None, None],
         "w_mem": [None, None], "w_out": [None, None], "w_gu": [None, None], "w_down": [None, None],
         "ffn_cw": [None, None], "ffn_cb": [None, None]}
    for i in (1, 0):
        hin, n1, kv, proj, mix, h2, n2, gu, act, extra = saved[i]
        dact = mm_nt(dh, P["w_down"][i], out_dtype=bf16, name=f"d_act{i}")
        G["w_down"][i] = mm_tn(act, dh, name=f"dw_down{i}")
        dgu, dcw, dcb = glu_bwd(gu, P["ffn_cw"][i], P["ffn_cb"][i], dact, f"glu_bwd{i}")
        G["ffn_cw"][i], G["ffn_cb"][i] = dcw, dcb[0]
        G["w_gu"][i] = gu_bwd_w(n2, dgu, f"dw_gu{i}")
        g_ffn = ready(("ffn", i), G, P["g_ffn"][i])
        dh2, dg = gu_bwd_x(dgu, P["w_gu"][i], (h2, g_ffn, dh), f"d_n2_{i}")
        G["g_ffn"][i] = dg[0]
        dmix = mm_nt(dh2, P["w_out"][i], name=f"d_mix{i}")
        G["w_out"][i] = mm_tn(mix, dh2, name=f"dw_out{i}")
        if i == 0:
            dqkv, dbias, dsk = swa_bwd(proj, bias, sk, dmix)
            dxq, dkv = xattn_bwd(proj, A_Q + 2 * LANE, kv, dmix, "xattn_a_bwd")
            dproj = (dqkv, dxq)
            G["sinks"] = dsk[:A_HEADS, 0]
            G["rel_bias"] = bias_grad(dbias)[:, :A_HEADS]
            w_in, gname = P["w_in_a"], "w_in_a"
        else:
            qkvn, chunked, inv, states, o = extra
            do, dz, dgo = dnpost_bwd(o, proj, prm, dmix)
            dqkvn, dseg, dprm = dnc_bwd(qkvn, proj, prm, inv, dns_bwd(chunked, states, do))
            draw, dconv = dnprep_bwd(proj, P["conv_qkv"], dqkvn)
            dxq, dkv = xattn_bwd(proj, 2304, kv, dmix, "xattn_b_bwd")
            dproj = (draw, dz, dxq, dseg)
            G["conv_qkv"] = dconv
            G["a_log"], G["dt_bias"], G["out_norm_g"] = dprm[0, 6:12], dprm[1, 6:12], dgo[0]
            w_in, gname = P["w_in_b"], "w_in_b"
        G[gname] = mm_tn(n1, dproj, name=f"d{gname}")
        dh, dg = mm_nt_norm(dproj, w_in, (hin, P["g_mix"][i], dh2), f"d_n1_{i}")
        G["g_mix"][i] = dg[0]
        dgm, dwm = memkv_bwd(mem, P["g_mem"][i], P["w_mem"][i], dkv, f"memkv_bwd{i}")
        G["g_mem"][i], G["w_mem"][i] = dgm[0], dwm
        ready(("mix", i), G, None)
    return loss, dh, G


def _prepare(full, w_gu=None):
    return {
        "rel_bias": full["rel_bias"], "sinks": full["sinks_a"][0], "a_log": full["a_log_b"][0],
        "dt_bias": full["dt_bias_b"][0], "out_norm_g": full["out_norm_g_b"][0],
        "g_mix": full["norm_mix_g"], "g_mem": full["norm_mem_g"], "g_ffn": full["norm_ffn_g"],
        "g_fin": full["final_norm_g"], "conv_qkv": full["conv_qkv_b"][0],
        "ffn_cw": [full["ffn_conv_w"][0], full["ffn_conv_w"][1]],
        "ffn_cb": [full["ffn_conv_b"][0], full["ffn_conv_b"][1]],
        "w_mem": [full["w_mem_kv"][0], full["w_mem_kv"][1]],
        "w_out": [_lay_out_a(full["w_out"][0]), full["w_out"][1]],
        "w_in_a": _lay_in_a(full["w_in_a"][0]), "w_in_b": _lay_in_b(full["w_in_b"][0]),
        "w_gu": w_gu if w_gu is not None else [_chip_cols(full["w_gate_up"][0]), _chip_cols(full["w_gate_up"][1])],
        "w_down": [full["w_down"][0], full["w_down"][1]],
    }


def _grads_to_ref(G):
    return {
        "rel_bias": G["rel_bias"], "norm_mix_g": jnp.stack(G["g_mix"]), "norm_mem_g": jnp.stack(G["g_mem"]),
        "w_mem_kv": jnp.stack(G["w_mem"]),
        "w_out": jnp.stack([_unlay_out_a(G["w_out"][0]), G["w_out"][1]]),
        "w_in_a": _unlay_in_a(G["w_in_a"])[None], "sinks_a": G["sinks"][None],
        "w_in_b": _unlay_in_b(G["w_in_b"])[None], "conv_qkv_b": G["conv_qkv"][None],
        "a_log_b": G["a_log"][None], "dt_bias_b": G["dt_bias"][None], "out_norm_g_b": G["out_norm_g"][None],
        "norm_ffn_g": jnp.stack(G["g_ffn"]),
        "w_gate_up": jnp.stack([_unchip_cols(G["w_gu"][0]), _unchip_cols(G["w_gu"][1])]).astype(f32),
        "ffn_conv_w": jnp.stack(G["ffn_cw"]), "ffn_conv_b": jnp.stack(G["ffn_cb"]),
        "w_down": jnp.stack(G["w_down"]), "final_norm_g": G["g_fin"],
    }


ANY = pl.BlockSpec(memory_space=pl.ANY)


def _place():
    return lax.axis_index("x"), lax.axis_index("y"), lax.axis_index("c")


def chip_scatter(gs):
    n = len(gs)

    def body(*refs):
        ins, outs = refs[:n], refs[n:2 * n]
        ssem, rsem = refs[2 * n:]
        x, y, c = _place()
        me = 2 * x + y
        peers = [(1 - x, y), (x, 1 - y), (1 - x, 1 - y)]

        def remote(j, k, slot):
            px, py = peers[k]
            return pltpu.make_async_remote_copy(
                src_ref=ins[j].at[2 * px + py], dst_ref=outs[j].at[slot],
                send_sem=ssem.at[3 * j + k], recv_sem=rsem.at[3 * j + k],
                device_id=(px, py, c), device_id_type=MESH)

        sends = [remote(j, k, me) for j in range(n) for k in range(3)]
        for cp in sends:
            cp.start()
        for j in range(n):
            for k in range(3):
                px, py = peers[k]
                remote(j, k, 2 * px + py).wait_recv()
        for cp in sends:
            cp.wait_send()

    return pl.pallas_call(
        body, name="grad_scatter", in_specs=[ANY] * n, out_specs=[ANY] * n,
        out_shape=[jax.ShapeDtypeStruct(g.shape, g.dtype) for g in gs],
        scratch_shapes=[pltpu.SemaphoreType.DMA((3 * n,)), pltpu.SemaphoreType.DMA((3 * n,))],
    )(*gs)


def allreduce_small(buf):
    R = buf.shape[0]

    def body(b_ref, o_ref, recv, ssem, rsem):
        x, y, c = _place()
        me = 4 * x + 2 * y + c

        def peer(k):
            return (1 - x if k & 4 else x, 1 - y if k & 2 else y, 1 - c if k & 1 else c)

        def remote(k, slot):
            return pltpu.make_async_remote_copy(
                src_ref=b_ref, dst_ref=recv.at[slot], send_sem=ssem.at[k - 1], recv_sem=rsem.at[k - 1],
                device_id=peer(k), device_id_type=MESH)

        sends = [remote(k, me) for k in range(1, 8)]
        for cp in sends:
            cp.start()
        recv[me] = b_ref[...]
        for k in range(1, 8):
            px, py, pc = peer(k)
            remote(k, 4 * px + 2 * py + pc).wait_recv()
        for cp in sends:
            cp.wait_send()
        total = recv[0]
        for j in range(1, 8):
            total = total + recv[j]
        o_ref[...] = total

    return pl.pallas_call(
        body, name="small_allreduce",
        in_specs=[pl.BlockSpec(memory_space=pltpu.VMEM)], out_specs=pl.BlockSpec(memory_space=pltpu.VMEM),
        out_shape=jax.ShapeDtypeStruct(buf.shape, f32),
        scratch_shapes=[pltpu.VMEM((8, R, LANE), f32), pltpu.SemaphoreType.DMA((7,)), pltpu.SemaphoreType.DMA((7,))],
    )(buf)


def sum_slots(own, recv, chip, core, name):
    _, R, C = recv.shape
    tr = _row_tile(R, 256)
    nt = R // tr

    def body(p_ref, a_ref, r_ref, o_ref):
        acc = jnp.zeros((tr, C), f32)
        for s in range(4):
            acc = acc + jnp.where(p_ref[0] == s, a_ref[s], r_ref[s]).astype(f32)
        o_ref[...] = acc

    slots = pl.BlockSpec((4, tr, C), lambda i, p_ref: (0, i, 0))
    return pl.pallas_call(
        body, name=name, out_shape=jax.ShapeDtypeStruct((2 * R, C), f32),
        grid_spec=pltpu.PrefetchScalarGridSpec(
            num_scalar_prefetch=1, grid=(nt,), in_specs=[slots, slots],
            out_specs=pl.BlockSpec((tr, C), lambda i, p_ref: (p_ref[1] * nt + i, 0))),
        compiler_params=_cp(("parallel",)),
    )(jnp.stack([chip, core]).astype(jnp.int32), own, recv)


def _half(ref, core, axis=0):
    half = ref.shape[axis] // 2
    idx = (slice(None),) * axis + (pl.ds(core * half, half),)
    return ref.at[idx]


IN_HBM = pl.BlockSpec(memory_space=pltpu.HBM)
IN_SEM = pl.BlockSpec(memory_space=pltpu.SEMAPHORE)
SIDE_EFFECT = pltpu.SideEffectType.DATAFLOW_SIDE_EFFECTING


def _gather_copy(buf, i, k, ssem, rsem, place, landing):
    x, y, c = place
    px, py = [(1 - x, y), (x, 1 - y), (1 - x, 1 - y)][k]
    me = 2 * x + y
    return pltpu.make_async_remote_copy(
        src_ref=buf.at[me], dst_ref=buf.at[me if landing == "theirs" else 2 * px + py],
        send_sem=ssem.at[3 * i + k], recv_sem=rsem.at[3 * i + k], device_id=(px, py, c), device_id_type=MESH)


def gather_start(groups):
    flat = [b for grp in groups for b in grp]
    n, ng = len(flat), len(groups)

    def body(*refs):
        bufs, sems = refs[:n], refs[n:n + 2 * ng]
        place = _place()
        j = 0
        for g, grp in enumerate(groups):
            for i in range(len(grp)):
                for k in range(3):
                    _gather_copy(bufs[j], i, k, sems[2 * g], sems[2 * g + 1], place, "theirs").start()
                j += 1

    sem_shapes = [pltpu.SemaphoreType.DMA((3 * len(grp),)) for grp in groups for _ in range(2)]
    out = pl.pallas_call(
        body, name="gather_start", in_specs=[IN_HBM] * n, out_specs=(*[IN_SEM] * (2 * ng), *[IN_HBM] * n),
        out_shape=(*sem_shapes, *[pltpu.HBM(b.shape, b.dtype) for b in flat]),
        input_output_aliases={i: 2 * ng + i for i in range(n)},
        compiler_params=pltpu.CompilerParams(has_side_effects=SIDE_EFFECT),
    )(*[pltpu.with_memory_space_constraint(b, pltpu.HBM) for b in flat])
    sems, bufs = out[:2 * ng], list(out[2 * ng:])
    flights, j = [], 0
    for g, grp in enumerate(groups):
        flights.append((bufs[j:j + len(grp)], sems[2 * g], sems[2 * g + 1]))
        j += len(grp)
    return flights


def gather_wait(flight, after, name):
    bufs, ssem, rsem = flight
    n = len(bufs)

    def body(*refs):
        place = _place()
        for i in range(n):
            for k in range(3):
                cp = _gather_copy(refs[i], i, k, refs[n], refs[n + 1], place, "mine")
                cp.wait_send()
                cp.wait_recv()

    return pl.pallas_call(
        body, name=name, in_specs=[IN_HBM] * n + [IN_SEM, IN_SEM, ANY], out_specs=[IN_HBM] * n,
        out_shape=[pltpu.HBM(b.shape, b.dtype) for b in bufs], input_output_aliases={i: i for i in range(n)},
        compiler_params=pltpu.CompilerParams(has_side_effects=SIDE_EFFECT),
    )(*bufs, ssem, rsem, after)


def _scatter_copy(src, land, j, k, ssem, rsem, place, landing):
    x, y, c = place
    px, py = [(1 - x, y), (x, 1 - y), (1 - x, 1 - y)][k]
    return pltpu.make_async_remote_copy(
        src_ref=src.at[2 * px + py], dst_ref=land.at[2 * x + y if landing == "theirs" else 2 * px + py],
        send_sem=ssem.at[3 * j + k], recv_sem=rsem.at[3 * j + k], device_id=(px, py, c), device_id_type=MESH)


def scatter_start(srcs, name):
    n = len(srcs)
    lands = [lax.empty(g.shape, g.dtype) for g in srcs]

    def body(*refs):
        place = _place()
        for j in range(n):
            for k in range(3):
                _scatter_copy(refs[j], refs[n + j], j, k, refs[2 * n], refs[2 * n + 1], place, "theirs").start()
        refs[-1][...] = jnp.zeros_like(refs[-1])

    sem = pltpu.SemaphoreType.DMA((3 * n,))
    hbm = [pltpu.with_memory_space_constraint(b, pltpu.HBM) for b in list(srcs) + lands]
    out = pl.pallas_call(
        body, name=name, in_specs=[IN_HBM] * (2 * n),
        out_specs=(IN_SEM, IN_SEM, *[IN_HBM] * (2 * n), pl.BlockSpec(memory_space=pltpu.VMEM)),
        out_shape=(sem, sem, *[pltpu.HBM(b.shape, b.dtype) for b in hbm], jax.ShapeDtypeStruct((8, LANE), f32)),
        input_output_aliases={i: 2 + i for i in range(2 * n)},
        compiler_params=pltpu.CompilerParams(has_side_effects=SIDE_EFFECT),
    )(*hbm)
    return (list(out[2:2 + n]), list(out[2 + n:2 + 2 * n]), out[0], out[1]), out[-1]


def scatter_wait(flight, after, name):
    srcs, lands, ssem, rsem = flight
    n = len(srcs)

    def body(*refs):
        place = _place()
        for j in range(n):
            for k in range(3):
                cp = _scatter_copy(refs[j], refs[n + j], j, k, refs[2 * n], refs[2 * n + 1], place, "mine")
                cp.wait_send()
                cp.wait_recv()

    out = pl.pallas_call(
        body, name=name, in_specs=[IN_HBM] * (2 * n) + [IN_SEM, IN_SEM, ANY], out_specs=[IN_HBM] * (2 * n),
        out_shape=[pltpu.HBM(b.shape, b.dtype) for b in list(srcs) + list(lands)],
        input_output_aliases={i: i for i in range(2 * n)},
        compiler_params=pltpu.CompilerParams(has_side_effects=SIDE_EFFECT),
    )(*srcs, *lands, ssem, rsem, after)
    return list(out[:n]), list(out[n:])


def pair_exchange(gbufs, name):
    n = len(gbufs)

    def body(*refs):
        ins, outs = refs[:n], refs[n:2 * n]
        ssem, rsem = refs[2 * n:]
        x, y, c = _place()
        cps = [pltpu.make_async_remote_copy(
            src_ref=_half(ins[j], 1 - c, axis=1), dst_ref=outs[j], send_sem=ssem.at[j], recv_sem=rsem.at[j],
            device_id=(x, y, 1 - c), device_id_type=MESH) for j in range(n)]
        for cp in cps:
            cp.start()
        for cp in cps:
            cp.wait()

    return pl.pallas_call(
        body, name=name, in_specs=[ANY] * n, out_specs=[ANY] * n,
        out_shape=[jax.ShapeDtypeStruct((4, g.shape[1] // 2, g.shape[2]), g.dtype) for g in gbufs],
        scratch_shapes=[pltpu.SemaphoreType.DMA((n,)), pltpu.SemaphoreType.DMA((n,))],
    )(*gbufs)


def _row_tile(rows, cap=512):
    return max(t for t in range(16, min(rows, cap) + 1, 16) if rows % t == 0)


def pair_sum(mine, theirs, core, name):
    _, R, C = mine.shape
    half = R // 2
    tr = _row_tile(half)
    nt = half // tr

    def body(c_ref, a_ref, b_ref, o_ref):
        o_ref[...] = (a_ref[...].astype(f32) + b_ref[...].astype(f32)).astype(bf16)

    return pl.pallas_call(
        body, name=name, out_shape=jax.ShapeDtypeStruct(theirs.shape, bf16),
        grid_spec=pltpu.PrefetchScalarGridSpec(
            num_scalar_prefetch=1, grid=(4, nt),
            in_specs=[pl.BlockSpec((None, tr, C), lambda s, i, c_ref: (s, c_ref[0] * nt + i, 0)),
                      pl.BlockSpec((None, tr, C), lambda s, i, c_ref: (s, i, 0))],
            out_specs=pl.BlockSpec((None, tr, C), lambda s, i, c_ref: (s, i, 0))),
        compiler_params=_cp(("parallel", "parallel")),
    )(jnp.reshape(core, (1,)).astype(jnp.int32), mine, theirs)


def final_exchange(fins):
    n = len(fins)

    def body(*refs):
        outs = refs[n:2 * n]
        ssem, rsem = refs[2 * n:]
        x, y, c = _place()
        cps = [pltpu.make_async_remote_copy(
            src_ref=_half(outs[j], c), dst_ref=_half(outs[j], c), send_sem=ssem.at[j], recv_sem=rsem.at[j],
            device_id=(x, y, 1 - c), device_id_type=MESH) for j in range(n)]
        for cp in cps:
            cp.start()
        for cp in cps:
            cp.wait()

    return pl.pallas_call(
        body, name="final_exchange", in_specs=[ANY] * n, out_specs=[ANY] * n,
        out_shape=[jax.ShapeDtypeStruct(f.shape, f.dtype) for f in fins],
        input_output_aliases={j: j for j in range(n)},
        scratch_shapes=[pltpu.SemaphoreType.DMA((n,)), pltpu.SemaphoreType.DMA((n,))],
    )(*fins)


def adamw_big(w, m, v, gs, row0, name):
    L, R, C = w.shape
    tr = _row_tile(math.gcd(R, row0) if row0 else R, max(16, 262144 // C // 16 * 16))
    b0 = row0 // tr

    def body(*refs):
        w_ref, m_ref, v_ref = refs[:3]
        g_refs = refs[3:3 + L]
        g_ref, d_ref, nm_ref, nv_ref = refs[3 + L:]
        g = g_refs[0][...]
        for l in range(1, L):
            g = jnp.where(pl.program_id(0) == l, g_refs[l][...], g)
        d, nm, nv = _adamw_math(w_ref[...], g, m_ref[...], v_ref[...])
        g_ref[...] = g
        d_ref[...] = d
        nm_ref[...] = nm
        nv_ref[...] = nv

    own = pl.BlockSpec((None, tr, C), lambda l, i: (l, i, 0))
    off = pl.BlockSpec((tr, C), lambda l, i: (b0 + i, 0))
    return pl.pallas_call(
        body, name=name, grid=(L, R // tr), in_specs=[own, own, own] + [off] * L, out_specs=[own] * 4,
        out_shape=[jax.ShapeDtypeStruct((L, R, C), f32)] * 4, compiler_params=_cp(("parallel", "parallel")),
    )(w, m, v, *gs)


def _adamw_math(w, g, m, v):
    m = B1 * m + (1.0 - B1) * g
    v = B2 * v + (1.0 - B2) * (g * g)
    m_hat = m / (1.0 - B1 ** STEP)
    v_hat = v / (1.0 - B2 ** STEP)
    delta = -LR * (m_hat / (jnp.sqrt(v_hat) + AEPS) + WD * w)
    return delta, m, v


def adamw_small(w, m, v, g):
    def body(w_ref, m_ref, v_ref, g_ref, d_ref, nm_ref, nv_ref):
        d, nm, nv = _adamw_math(w_ref[...], g_ref[...], m_ref[...], v_ref[...])
        d_ref[...] = d
        nm_ref[...] = nm
        nv_ref[...] = nv

    return pl.pallas_call(body, name="adamw_small", out_shape=[jax.ShapeDtypeStruct(w.shape, f32)] * 3)(w, m, v, g)


CONV =(("conv_qkv_b", 2), ("ffn_conv_w", 2))
SMALL = ("rel_bias", "norm_mix_g", "norm_mem_g", "sinks_a", "a_log_b", "dt_bias_b", "out_norm_g_b", "norm_ffn_g",
         "ffn_conv_b", "final_norm_g")
WEIGHTS = ("rel_bias", "norm_mix_g", "norm_mem_g", "w_mem_kv", "w_out", "w_in_a", "sinks_a", "w_in_b", "conv_qkv_b",
           "a_log_b", "dt_bias_b", "out_norm_g_b", "norm_ffn_g", "w_gate_up", "ffn_conv_w", "ffn_conv_b", "w_down",
           "final_norm_g")
ARGS = ("x", "mem") + WEIGHTS + ("loss_target",) + tuple("m_" + n for n in WEIGHTS) + tuple("v_" + n for n in WEIGHTS)


def _rows(a, width):
    flat = a.reshape(-1)
    pad = (-flat.shape[0]) % (8 * width)
    if pad:
        flat = jnp.concatenate([flat, jnp.zeros((pad,), a.dtype)])
    return flat.reshape(-1, width)


def _nrows(shape, width):
    return _pad_to(-(-math.prod(shape) // width), 8)


def _pack(arrs, width, total_rows, dtype):
    parts = [_rows(a.astype(dtype), width) for a in arrs]
    used = sum(p.shape[0] for p in parts)
    if total_rows > used:
        parts.append(jnp.zeros((total_rows - used, width), dtype))
    return jnp.concatenate(parts, axis=0)


def _unpack(buf, shapes, width):
    out, r = [], 0
    for s in shapes:
        n = _nrows(s, width)
        out.append(buf[r:r + n].reshape(-1)[:math.prod(s)].reshape(s))
        r += n
    return out


def _pad_to(n, mult):
    return -(-n // mult) * mult


def kernel(x, mem, rel_bias, norm_mix_g, norm_mem_g, w_mem_kv, w_out, w_in_a, sinks_a, w_in_b, conv_qkv_b, a_log_b, dt_bias_b, out_norm_g_b, norm_ffn_g, w_gate_up, ffn_conv_w, ffn_conv_b, w_down, final_norm_g, loss_target, m_rel_bias, m_norm_mix_g, m_norm_mem_g, m_w_mem_kv, m_w_out, m_w_in_a, m_sinks_a, m_w_in_b, m_conv_qkv_b, m_a_log_b, m_dt_bias_b, m_out_norm_g_b, m_norm_ffn_g, m_w_gate_up, m_ffn_conv_w, m_ffn_conv_b, m_w_down, m_final_norm_g, v_rel_bias, v_norm_mix_g, v_norm_mem_g, v_w_mem_kv, v_w_out, v_w_in_a, v_sinks_a, v_w_in_b, v_conv_qkv_b, v_a_log_b, v_dt_bias_b, v_out_norm_g_b, v_norm_ffn_g, v_w_gate_up, v_ffn_conv_w, v_ffn_conv_b, v_w_down, v_final_norm_g):
    A = dict(zip(ARGS, (x, mem, rel_bias, norm_mix_g, norm_mem_g, w_mem_kv, w_out, w_in_a, sinks_a, w_in_b, conv_qkv_b, a_log_b, dt_bias_b, out_norm_g_b, norm_ffn_g, w_gate_up, ffn_conv_w, ffn_conv_b, w_down, final_norm_g, loss_target, m_rel_bias, m_norm_mix_g, m_norm_mem_g, m_w_mem_kv, m_w_out, m_w_in_a, m_sinks_a, m_w_in_b, m_conv_qkv_b, m_a_log_b, m_dt_bias_b, m_out_norm_g_b, m_norm_ffn_g, m_w_gate_up, m_ffn_conv_w, m_ffn_conv_b, m_w_down, m_final_norm_g, v_rel_bias, v_norm_mix_g, v_norm_mem_g, v_w_mem_kv, v_w_out, v_w_in_a, v_sinks_a, v_w_in_b, v_conv_qkv_b, v_a_log_b, v_dt_bias_b, v_out_norm_g_b, v_norm_ffn_g, v_w_gate_up, v_ffn_conv_w, v_ffn_conv_b, v_w_down, v_final_norm_g)))
    chip = 2 * lax.axis_index("x") + lax.axis_index("y")
    core = lax.axis_index("c")

    def own_slot(shard):
        return lax.dynamic_update_index_in_dim(lax.empty((4,) + shard.shape, shard.dtype), shard, chip, 0)

    def bslot(w):
        return own_slot(w.astype(bf16))

    groups = {
        ("w_in", 0): [bslot(w_in_a[0])],
        ("w_mem", 0): [bslot(w_mem_kv[0]), bslot(w_mem_kv[1]), own_slot(conv_qkv_b[0]),
                       own_slot(ffn_conv_w.reshape(6, -1))],
        ("w_out", 0): [bslot(w_out[0])], ("w_gu", 0): [bslot(w_gate_up[0])], ("w_down", 0): [bslot(w_down[0])],
        ("w_in", 1): [bslot(w_in_b[0])],
        ("w_out", 1): [bslot(w_out[1])], ("w_gu", 1): [bslot(w_gate_up[1])], ("w_down", 1): [bslot(w_down[1])],
    }
    flights = dict(zip(groups, gather_start(list(groups.values()))))
    P = {"rel_bias": rel_bias, "sinks": sinks_a[0], "a_log": a_log_b[0], "dt_bias": dt_bias_b[0],
         "out_norm_g": out_norm_g_b[0], "g_mix": norm_mix_g, "g_mem": norm_mem_g, "g_ffn": norm_ffn_g,
         "g_fin": final_norm_g, "ffn_cb": [ffn_conv_b[0], ffn_conv_b[1]], "w_mem": [None, None], "w_out": [None, None],
         "w_gu": [None, None], "w_down": [None, None], "ffn_cw": [None, None]}

    def rows4(g):
        return g.reshape(4 * g.shape[1], g.shape[2])

    def arrive(key, after):
        if key not in flights:
            return
        got = gather_wait(flights.pop(key), after, "gather_wait_%s%d" % key)
        name, i = key
        if name == "w_in":
            P["w_in_a" if i == 0 else "w_in_b"] = (_lay_in_a if i == 0 else _lay_in_b)(_unchip_cols(got[0]))
        elif name == "w_mem":
            P["w_mem"] = [rows4(got[0]), rows4(got[1])]
            P["conv_qkv"] = _unchip_cols(got[2])
            cw = _unchip_cols(got[3]).reshape(2, 3, D_FF)
            P["ffn_cw"] = [cw[0], cw[1]]
        elif name == "w_out":
            P["w_out"][i] = _lay_out_a(rows4(got[0])) if i == 0 else rows4(got[0])
        elif name == "w_gu":
            P["w_gu"][i] = got[0]
        else:
            P["w_down"][i] = rows4(got[0])

    def chip_rows(g):
        return g.reshape(4, g.shape[0] // 4, g.shape[-1])

    sent, started = {}, []

    def ready(key, G, dep):
        kind, i = key
        tag = "%s%d" % key
        if kind == "ffn":
            names, partial = ("gu", "down"), [G["w_gu"][i], chip_rows(G["w_down"][i]).astype(bf16)]
        else:
            g_out = _unlay_out_a(G["w_out"][0]) if i == 0 else G["w_out"][1]
            g_in = _unlay_in_a(G["w_in_a"]) if i == 0 else _unlay_in_b(G["w_in_b"])
            names = ("out", "in", "mem")
            partial = [chip_rows(g_out).astype(bf16), _chip_cols(g_in).astype(bf16), chip_rows(G["w_mem"][i]).astype(bf16)]
        theirs = pair_exchange(partial, "pair_exchange_" + tag)
        pair = [pair_sum(p, t, core, "pair_sum_%s%d" % (nm, i)) for p, t, nm in zip(partial, theirs, names)]
        if key == ("mix", 0):
            sent[key] = (names, pair, chip_scatter(pair))
            return dep
        flight, token = scatter_start(pair, "scatter_start_" + tag)
        sent[key] = (names, flight)
        started.append(token[0, 0])
        if dep is not None:
            while started:
                dep = dep + started.pop()
        return dep

    P["arrive"], P["ready"] = arrive, ready

    loss, dx, G = _local_step(x[0], mem[0], loss_target[0], P)
    gfull = _grads_to_ref(G)

    fin = {}
    for key in (("ffn", 1), ("mix", 1), ("ffn", 0), ("mix", 0)):
        if key == ("mix", 0):
            names, pair, arrived = sent[key]
        else:
            names, flight = sent[key]
            pair, arrived = scatter_wait(flight, dx, "scatter_wait_%s%d" % key)
        for nm, p, r in zip(names, pair, arrived):
            fin[nm, key[1]] = sum_slots(p, r, chip, core, "sum_slots_%s%d" % (nm, key[1]))
    order = list(fin)
    done = dict(zip(order, final_exchange([fin[k] for k in order])))

    sm_shapes = [A[n].shape for n in SMALL] + [gfull[n].shape for n, _ in CONV] + [(LANE,)]
    sm_rows = _pad_to(sum(_nrows(s, LANE) for s in sm_shapes), 8)
    sbuf = _pack([gfull[n] for n in SMALL] + [gfull[n] for n, _ in CONV] + [loss[0]], LANE, sm_rows, f32)
    tot = _unpack(allreduce_small(sbuf), sm_shapes, LANE)
    gsmall = dict(zip(SMALL, tot[:len(SMALL)]))
    for (n, axis), t in zip(CONV, tot[len(SMALL):len(SMALL) + len(CONV)]):
        sh = A[n].shape[axis]
        gsmall[n] = lax.dynamic_slice_in_dim(t, chip * sh, sh, axis)
    loss_out = tot[-1][0]

    out = {}
    plan = (("w_gate_up", [done["gu", 0], done["gu", 1]]), ("w_down", [done["down", 0], done["down", 1]]),
            ("w_out", [done["out", 0], done["out", 1]]), ("w_mem_kv", [done["mem", 0], done["mem", 1]]),
            ("w_in_a", [done["in", 0]]), ("w_in_b", [done["in", 1]]))
    for n, gs in plan:
        shape3 = (len(gs),) + gs[0].shape
        res = adamw_big(A[n].reshape(shape3), A["m_" + n].reshape(shape3), A["v_" + n].reshape(shape3), gs, 0,
                        "adamw_" + n)
        for key, r in zip(("grad_", "delta_", "new_m_", "new_v_"), res):
            out[key + n] = r.reshape(A[n].shape)
    names = SMALL + tuple(n for n, _ in CONV)
    shapes = [A[n].shape for n in names]
    rows = _pad_to(sum(_nrows(s, LANE) for s in shapes), 8)
    packs = [_pack([src[n] for n in names], LANE, rows, f32)
             for src in ({n: A[n] for n in names}, {n: A["m_" + n] for n in names}, {n: A["v_" + n] for n in names}, gsmall)]
    res = adamw_small(*packs)
    for key, r in zip(("delta_", "new_m_", "new_v_"), res):
        for n, a in zip(names, _unpack(r, shapes, LANE)):
            out[key + n] = a
    for n in names:
        out["grad_" + n] = gsmall[n]
    return (loss_out, dx[None], *[out["grad_" + n] for n in WEIGHTS], *[out["delta_" + n] for n in WEIGHTS],
            *[out["new_m_" + n] for n in WEIGHTS], *[out["new_v_" + n] for n in WEIGHTS])
```

```python
import functools
import math

import numpy as np
import jax
import jax.numpy as jnp
from jax import lax
from jax.experimental import pallas as pl
from jax.experimental.pallas import tpu as pltpu

f32 = jnp.float32
bf16 = jnp.bfloat16
HI = lax.Precision.HIGHEST
MESH = pl.DeviceIdType.MESH

D = 1024
MEM_LEN = 256
EPS = 1e-6
A_HEADS, A_KV, A_DH = 12, 2, 64
A_Q = 768
BLK = 128
N_BUCKETS, MAX_DIST = 32, 128
B_QK, B_V, B_DH = 384, 768, 128
B_QKV = 1536
CHUNK = 64
X_Q = 256
D_FF = 2816
LANE = 128
VMEM_LIMIT = 56 * 1024 * 1024
MM_ROWS = 1024

LR, B1, B2, AEPS, WD, STEP = 0.001, 0.9, 0.999, 1e-08, 0.01, 10


def _cp(sem=None):
    return pltpu.CompilerParams(dimension_semantics=sem, vmem_limit_bytes=VMEM_LIMIT)


def _dg(a, b, ca, cb, prec=None):
    return lax.dot_general(a, b, (((ca,), (cb,)), ((), ())), precision=prec, preferred_element_type=f32)


@jax.custom_vjp
def bdot(a, b):
    return _dg(a.astype(bf16), b.astype(bf16), 1, 0)


def _bdot_f(a, b):
    return bdot(a, b), (a, b)


def _bdot_b(res, g):
    a, b = res
    gb = g.astype(bf16)
    return _dg(gb, b.astype(bf16), 1, 1), _dg(a.astype(bf16), gb, 0, 0)


bdot.defvjp(_bdot_f, _bdot_b)


@jax.custom_vjp
def bdot_nt(a, b):
    return _dg(a.astype(bf16), b.astype(bf16), 1, 1)


def _bdot_nt_f(a, b):
    return bdot_nt(a, b), (a, b)


def _bdot_nt_b(res, g):
    a, b = res
    gb = g.astype(bf16)
    return _dg(gb, b.astype(bf16), 1, 0), _dg(gb, a.astype(bf16), 0, 0)


bdot_nt.defvjp(_bdot_nt_f, _bdot_nt_b)


def _shift_rows(x, s, down):
    n = x.shape[0]
    row = lax.broadcasted_iota(jnp.int32, x.shape, 0)
    if down:
        return jnp.where(row >= s, pltpu.roll(x, s, 0), 0.0)
    return jnp.where(row < n - s, pltpu.roll(x, n - s, 0), 0.0)


@functools.partial(jax.custom_vjp, nondiff_argnums=(1,))
def shift_down(x, s):
    return _shift_rows(x, s, True)


def _sd_f(x, s):
    return _shift_rows(x, s, True), None


def _sd_b(s, _, g):
    return (_shift_rows(g, s, False),)


shift_down.defvjp(_sd_f, _sd_b)


def _sigmoid(x):
    return 1.0 / (1.0 + jnp.exp(-x))


def _silu(x):
    return x * _sigmoid(x)


def _rms(x, g):
    return x * lax.rsqrt(jnp.mean(x * x, axis=-1, keepdims=True) + EPS) * g


def _tile(n, cap):
    u = n // LANE
    best = 1
    for d in range(1, u + 1):
        if u % d == 0 and d * LANE <= cap:
            best = d
    return best * LANE


def mm_nn(a, w, res=None, out_dtype=f32, name="mm_nn"):
    M, K = a.shape
    N = w.shape[1]
    tm, tn = min(MM_ROWS, M), _tile(N, 1024)

    def body(*refs):
        if res is None:
            a_ref, w_ref, o_ref = refs
            o_ref[...] = _dg(a_ref[...].astype(bf16), w_ref[...], 1, 0).astype(out_dtype)
        else:
            a_ref, w_ref, r_ref, o_ref = refs
            o_ref[...] = (r_ref[...] + _dg(a_ref[...].astype(bf16), w_ref[...], 1, 0)).astype(out_dtype)

    in_specs = [pl.BlockSpec((tm, K), lambda n, m: (m, 0)), pl.BlockSpec((K, tn), lambda n, m: (0, n))]
    args = [a, w]
    if res is not None:
        in_specs.append(pl.BlockSpec((tm, tn), lambda n, m: (m, n)))
        args.append(res)
    return pl.pallas_call(
        body, name=name, grid=(N // tn, M // tm), in_specs=in_specs,
        out_specs=pl.BlockSpec((tm, tn), lambda n, m: (m, n)),
        out_shape=jax.ShapeDtypeStruct((M, N), out_dtype),
        compiler_params=_cp(("parallel", "parallel")),
    )(*args)


def mm_res_norm(a, w, res, g, name):
    pieces = a if isinstance(a, tuple) else (a,)
    na = len(pieces)
    M, K = pieces[0].shape[0], sum(p.shape[1] for p in pieces)
    tm = min(MM_ROWS, M)

    def body(*refs):
        w_ref, r_ref, g_ref, o_ref, n_ref = refs[na:]
        h = r_ref[...] + _dg(_cols(refs[:na]).astype(bf16), w_ref[...], 1, 0)
        o_ref[...] = h
        n_ref[...] = _rms(h, g_ref[...]).astype(bf16)

    tok = pl.BlockSpec((tm, D), lambda m: (m, 0))
    return pl.pallas_call(
        body, name=name, grid=(M // tm,),
        in_specs=[pl.BlockSpec((tm, p.shape[1]), lambda m: (m, 0)) for p in pieces]
        + [pl.BlockSpec((K, D), lambda m: (0, 0)), tok, pl.BlockSpec((1, D), lambda m: (0, 0))],
        out_specs=[tok, tok],
        out_shape=[jax.ShapeDtypeStruct((M, D), f32), jax.ShapeDtypeStruct((M, D), bf16)],
        compiler_params=_cp(("parallel",)),
    )(*pieces, w, res, g.reshape(1, D))


def mm_nt(dy, w, out_dtype=f32, name="mm_nt"):
    M, N = dy.shape
    K = w.shape[0]
    tm, tn = min(MM_ROWS, M), _tile(N, 1024)
    assert out_dtype == f32 or tn == N

    def body(dy_ref, w_ref, o_ref):
        part = _dg(dy_ref[...].astype(bf16), w_ref[...], 1, 1)
        if tn == N:
            o_ref[...] = part.astype(out_dtype)
        else:
            @pl.when(pl.program_id(1) == 0)
            def _():
                o_ref[...] = jnp.zeros_like(o_ref)
            o_ref[...] += part

    return pl.pallas_call(
        body, name=name, grid=(M // tm, N // tn),
        in_specs=[pl.BlockSpec((tm, tn), lambda m, n: (m, n)), pl.BlockSpec((K, tn), lambda m, n: (0, n))],
        out_specs=pl.BlockSpec((tm, K), lambda m, n: (m, 0)),
        out_shape=jax.ShapeDtypeStruct((M, K), out_dtype),
        compiler_params=_cp(("parallel", "arbitrary")),
    )(dy, w)


NORM_ROWS = 1024


def _acc_then_norm_bwd(part, steps, h_ref, g_ref, r_ref, o_ref, dg_ref):
    k = pl.program_id(1)

    @pl.when((pl.program_id(0) == 0) & (k == 0))
    def _():
        dg_ref[...] = jnp.zeros_like(dg_ref)

    @pl.when(k == 0)
    def _():
        o_ref[...] = part

    @pl.when(k > 0)
    def _():
        o_ref[...] += part

    @pl.when(k == steps - 1)
    def _():
        _, vjp = jax.vjp(_rms, h_ref[...], g_ref[...])
        dh, dg = vjp(o_ref[...])
        o_ref[...] = r_ref[...] + dh
        dg_ref[...] += dg


def mm_nt_norm(dy, w, norm, name):
    pieces = dy if isinstance(dy, tuple) else (dy,)
    nd = len(pieces)
    M, N = pieces[0].shape[0], sum(p.shape[1] for p in pieces)
    tm, tn = (min(NORM_ROWS, M), _tile(N, 1024)) if nd == 1 else (min(512, M), N)

    def body(*refs):
        w_ref, h_ref, g_ref, r_ref, o_ref, dg_ref = refs[nd:]
        _acc_then_norm_bwd(_dg(_cols(refs[:nd]).astype(bf16), w_ref[...], 1, 1), N // tn, h_ref, g_ref, r_ref, o_ref,
                           dg_ref)

    tok = pl.BlockSpec((tm, D), lambda m, n: (m, 0))
    vec = pl.BlockSpec((1, D), lambda m, n: (0, 0))
    return pl.pallas_call(
        body, name=name, grid=(M // tm, N // tn),
        in_specs=[pl.BlockSpec((tm, tn if nd == 1 else p.shape[1]), lambda m, n: (m, n)) for p in pieces]
        + [pl.BlockSpec((D, tn), lambda m, n: (0, n)), tok, vec, tok],
        out_specs=[tok, vec],
        out_shape=[jax.ShapeDtypeStruct((M, D), f32), jax.ShapeDtypeStruct((1, D), f32)],
        compiler_params=_cp(("arbitrary", "arbitrary")),
    )(*pieces, w, norm[0], norm[1].reshape(1, D), norm[2])


def _cols(refs):
    return refs[0][...] if len(refs) == 1 else jnp.concatenate([r[...] for r in refs], axis=1)


def mm_tn(a, dy, name="mm_tn"):
    pieces = a if isinstance(a, tuple) else (a,)
    dpieces = dy if isinstance(dy, tuple) else (dy,)
    na, nd = len(pieces), len(dpieces)
    M, K = pieces[0].shape[0], sum(p.shape[1] for p in pieces)
    N = sum(p.shape[1] for p in dpieces)
    tm = min(MM_ROWS, M)
    tk = _tile(K, 1408) if na == 1 else K
    tn = _tile(N, 1024) if nd == 1 else N

    def body(*refs):
        o_ref = refs[-1]

        @pl.when(pl.program_id(2) == 0)
        def _():
            o_ref[...] = jnp.zeros_like(o_ref)
        o_ref[...] += _dg(_cols(refs[:na]).astype(bf16), _cols(refs[na:na + nd]).astype(bf16), 0, 0)

    a_specs = [pl.BlockSpec((tm, tk if na == 1 else p.shape[1]), lambda k, n, m: (m, k)) for p in pieces]
    d_specs = [pl.BlockSpec((tm, tn if nd == 1 else p.shape[1]), lambda k, n, m: (m, n)) for p in dpieces]
    return pl.pallas_call(
        body, name=name, grid=(K // tk, N // tn, M // tm), in_specs=a_specs + d_specs,
        out_specs=pl.BlockSpec((tk, tn), lambda k, n, m: (k, n)),
        out_shape=jax.ShapeDtypeStruct((K, N), f32),
        compiler_params=_cp(("parallel", "parallel", "arbitrary")),
    )(*pieces, *dpieces)


def rms_fwd(h, g, name):
    S = h.shape[0]
    t = min(512, S)

    def body(h_ref, g_ref, o_ref):
        o_ref[...] = _rms(h_ref[...], g_ref[...]).astype(bf16)

    return pl.pallas_call(
        body, name=name, grid=(S // t,),
        in_specs=[pl.BlockSpec((t, D), lambda i: (i, 0)), pl.BlockSpec((1, D), lambda i: (0, 0))],
        out_specs=pl.BlockSpec((t, D), lambda i: (i, 0)),
        out_shape=jax.ShapeDtypeStruct((S, D), bf16),
        compiler_params=_cp(("parallel",)),
    )(h, g.reshape(1, D))


def loss_head(h, g, target):
    S = h.shape[0]
    t = min(512, S)

    def f(hh, gg, tt):
        err = _rms(hh, gg) - tt
        return 0.5 * jnp.sum(jnp.mean(err * err, axis=-1, keepdims=True), axis=0, keepdims=True)

    def body(h_ref, g_ref, t_ref, loss_ref, dh_ref, dg_ref):
        @pl.when(pl.program_id(0) == 0)
        def _():
            dg_ref[...] = jnp.zeros_like(dg_ref)
            loss_ref[...] = jnp.zeros_like(loss_ref)
        val, vjp = jax.vjp(lambda a, b: f(a, b, t_ref[...]), h_ref[...], g_ref[...])
        dh, dg = vjp(jnp.ones((1, 1), f32))
        dh_ref[...] = dh
        dg_ref[...] += dg
        loss_ref[...] += jnp.broadcast_to(val, loss_ref.shape)

    tok = pl.BlockSpec((t, D), lambda i: (i, 0))
    vec = pl.BlockSpec((1, D), lambda i: (0, 0))
    return pl.pallas_call(
        body, name="loss_head", grid=(S // t,), in_specs=[tok, vec, tok],
        out_specs=[pl.BlockSpec((1, LANE), lambda i: (0, 0)), tok, vec],
        out_shape=[jax.ShapeDtypeStruct((1, LANE), f32), jax.ShapeDtypeStruct((S, D), f32),
                   jax.ShapeDtypeStruct((1, D), f32)],
        compiler_params=_cp(("arbitrary",)),
    )(h, g.reshape(1, D), target)


def memkv_fwd(mem, g, w, name):
    def body(m_ref, g_ref, w_ref, o_ref):
        o_ref[...] = _dg(_rms(m_ref[...], g_ref[...]).astype(bf16), w_ref[...], 1, 0)

    return pl.pallas_call(
        body, name=name, out_shape=jax.ShapeDtypeStruct((MEM_LEN, 2 * X_Q), f32), compiler_params=_cp(),
    )(mem, g.reshape(1, D), w)


def memkv_bwd(mem, g, w, dkv, name):
    def body(m_ref, g_ref, w_ref, d_ref, dg_ref, dw_ref):
        n, vjp = jax.vjp(lambda gg: _rms(m_ref[...], gg), g_ref[...])
        db = d_ref[...].astype(bf16)
        dw_ref[...] = _dg(n.astype(bf16), db, 0, 0)
        dg_ref[...] = vjp(_dg(db, w_ref[...], 1, 1))[0]

    return pl.pallas_call(
        body, name=name,
        out_shape=[jax.ShapeDtypeStruct((1, D), f32), jax.ShapeDtypeStruct((D, 2 * X_Q), f32)],
        compiler_params=_cp(),
    )(mem, g.reshape(1, D), w, dkv)


def _xattn_f(xq, mk, mv):
    lane = lax.broadcasted_iota(jnp.int32, (1, X_Q), 1)
    out = jnp.zeros(xq.shape, f32)
    for hd in range(4):
        msk = (lane // 64 == hd).astype(f32)
        s = bdot_nt(xq * msk, mk) * (64 ** -0.5)
        m = lax.stop_gradient(jnp.max(s, axis=-1, keepdims=True))
        p = jnp.exp(s - m)
        p = p / jnp.sum(p, axis=-1, keepdims=True)
        out = out + bdot(p, mv * msk)
    return out


def xattn_fwd(proj, col, kv, name):
    S = proj.shape[0]
    t = min(512, S)
    cb = col // X_Q

    def body(q_ref, k_ref, v_ref, o_ref):
        o_ref[...] = _xattn_f(q_ref[...], k_ref[...], v_ref[...]).astype(bf16)

    return pl.pallas_call(
        body, name=name, grid=(S // t,),
        in_specs=[pl.BlockSpec((t, X_Q), lambda i: (i, cb)), pl.BlockSpec((MEM_LEN, X_Q), lambda i: (0, 0)),
                  pl.BlockSpec((MEM_LEN, X_Q), lambda i: (0, 1))],
        out_specs=pl.BlockSpec((t, X_Q), lambda i: (i, 0)),
        out_shape=jax.ShapeDtypeStruct((S, X_Q), bf16),
        compiler_params=_cp(("parallel",)),
    )(proj, kv, kv)


def xattn_bwd(proj, col, kv, dmix, name):
    S = proj.shape[0]
    t = min(512, S)
    cb = col // X_Q

    def body(q_ref, k_ref, v_ref, do_ref, dq_ref, dk_ref, dv_ref):
        @pl.when(pl.program_id(0) == 0)
        def _():
            dk_ref[...] = jnp.zeros_like(dk_ref)
            dv_ref[...] = jnp.zeros_like(dv_ref)
        _, vjp = jax.vjp(_xattn_f, q_ref[...], k_ref[...], v_ref[...])
        dq, dk, dv = vjp(do_ref[...])
        dq_ref[...] = dq.astype(bf16)
        dk_ref[...] += dk
        dv_ref[...] += dv

    kvb = pl.BlockSpec((MEM_LEN, X_Q), lambda i: (0, 0))
    dq, dk, dv = pl.pallas_call(
        body, name=name, grid=(S // t,),
        in_specs=[pl.BlockSpec((t, X_Q), lambda i: (i, cb)), kvb,
                  pl.BlockSpec((MEM_LEN, X_Q), lambda i: (0, 1)), pl.BlockSpec((t, X_Q), lambda i: (i, 3))],
        out_specs=[pl.BlockSpec((t, X_Q), lambda i: (i, 0)), kvb, kvb],
        out_shape=[jax.ShapeDtypeStruct((S, X_Q), bf16), jax.ShapeDtypeStruct((MEM_LEN, X_Q), f32),
                   jax.ShapeDtypeStruct((MEM_LEN, X_Q), f32)],
        compiler_params=_cp(("arbitrary",)),
    )(proj, kv, kv, dmix)
    return dq, jnp.concatenate([dk, dv], axis=1)


def _bucket_map():
    qi = np.arange(BLK)[:, None]
    kj = np.arange(2 * BLK)[None, :]
    n = np.maximum(BLK + qi - kj, 0)
    max_exact = N_BUCKETS // 2
    nf = np.maximum(n, 1).astype(np.float64)
    large = max_exact + (np.log(nf / max_exact) / math.log(MAX_DIST / max_exact)
                         * (N_BUCKETS - max_exact)).astype(np.int32)
    large = np.minimum(large, N_BUCKETS - 1)
    return np.where(n < max_exact, n, large).astype(np.int32)


def bias_build(rel_bias):
    def body(rb_ref, bk_ref, o_ref):
        bk = bk_ref[...]
        for h in range(A_HEADS):
            acc = jnp.zeros((BLK, 2 * BLK), f32)
            for b in range(N_BUCKETS):
                acc = jnp.where(bk == b, rb_ref[b, h], acc)
            o_ref[h] = acc

    return pl.pallas_call(
        body, name="bias_build",
        in_specs=[pl.BlockSpec(memory_space=pltpu.SMEM), pl.BlockSpec(memory_space=pltpu.VMEM)],
        out_specs=pl.BlockSpec(memory_space=pltpu.VMEM),
        out_shape=jax.ShapeDtypeStruct((A_HEADS, BLK, 2 * BLK), f32), compiler_params=_cp(),
    )(rel_bias, jnp.asarray(_bucket_map()))


def bias_grad(dbias):
    def body(d_ref, bk_ref, o_ref):
        bk = bk_ref[...]
        row = lax.broadcasted_iota(jnp.int32, (N_BUCKETS, LANE), 0)
        lane = lax.broadcasted_iota(jnp.int32, (N_BUCKETS, LANE), 1)
        acc = jnp.zeros((N_BUCKETS, LANE), f32)
        for h in range(A_HEADS):
            d = d_ref[h]
            for b in range(N_BUCKETS):
                s = jnp.sum(jnp.where(bk == b, d, 0.0), keepdims=True)
                acc = acc + jnp.where((row == b) & (lane == h), s, 0.0)
        o_ref[...] = acc

    return pl.pallas_call(
        body, name="bias_grad", out_shape=jax.ShapeDtypeStruct((N_BUCKETS, LANE), f32), compiler_params=_cp(),
    )(dbias, jnp.asarray(_bucket_map()))


def _swa_f(qb, kp, kc, vp, vc, bias, sk, first):
    kband = jnp.concatenate([kp, kc], axis=0)
    vband = jnp.concatenate([vp, vc], axis=0)
    qi = lax.broadcasted_iota(jnp.int32, (BLK, 2 * BLK), 0)
    kj = lax.broadcasted_iota(jnp.int32, (BLK, 2 * BLK), 1)
    rel = kj - qi
    ok = (rel >= 1) & (rel <= BLK) & ((kj >= BLK) | jnp.logical_not(first))
    lane = lax.broadcasted_iota(jnp.int32, (1, LANE), 1)
    lane_b = lax.broadcasted_iota(jnp.int32, (BLK, LANE), 1)
    outs = []
    for p in range(A_HEADS // 2):
        qp = qb[:, LANE * p:LANE * (p + 1)]
        acc = jnp.zeros((BLK, LANE), f32)
        for g in range(2):
            h = g * (A_HEADS // 2) + p
            msk = (lane // A_DH == g).astype(f32)
            s = bdot_nt(qp * msk, kband) * (A_DH ** -0.5) + bias[h]
            s = jnp.where(ok, s, -1e30)
            skb = jnp.broadcast_to(sk[h:h + 1, :], (BLK, LANE))
            sink = jnp.sum(jnp.where(lane_b == 0, skb, 0.0), axis=-1, keepdims=True)
            m = lax.stop_gradient(jnp.maximum(jnp.max(s, axis=-1, keepdims=True), sink))
            e = jnp.exp(s - m)
            prob = e / (jnp.sum(e, axis=-1, keepdims=True) + jnp.exp(sink - m))
            acc = acc + bdot(prob, vband) * msk
        outs.append(acc)
    return jnp.concatenate(outs, axis=1)


def _swa_specs(nb, rev):
    bi = (lambda i: nb - 1 - i) if rev else (lambda i: i)
    return [
        pl.BlockSpec((BLK, A_Q), lambda i: (bi(i), 0)),
        pl.BlockSpec((BLK, LANE), lambda i: (jnp.maximum(bi(i) - 1, 0), 6)),
        pl.BlockSpec((BLK, LANE), lambda i: (bi(i), 6)),
        pl.BlockSpec((BLK, LANE), lambda i: (jnp.maximum(bi(i) - 1, 0), 7)),
        pl.BlockSpec((BLK, LANE), lambda i: (bi(i), 7)),
        pl.BlockSpec((A_HEADS, BLK, 2 * BLK), lambda i: (0, 0, 0)),
        pl.BlockSpec((16, LANE), lambda i: (0, 0)),
    ]


def swa_fwd(proj, bias, sk):
    S = proj.shape[0]
    nb = S // BLK

    def body(q_ref, kp_ref, kc_ref, vp_ref, vc_ref, b_ref, s_ref, o_ref):
        o_ref[...] = _swa_f(q_ref[...], kp_ref[...], kc_ref[...], vp_ref[...], vc_ref[...], b_ref[...], s_ref[...],
                            pl.program_id(0) == 0).astype(bf16)

    return pl.pallas_call(
        body, name="swa_fwd", grid=(nb,), in_specs=_swa_specs(nb, False),
        out_specs=pl.BlockSpec((BLK, A_Q), lambda i: (i, 0)),
        out_shape=jax.ShapeDtypeStruct((S, A_Q), bf16), compiler_params=_cp(("parallel",)),
    )(proj, proj, proj, proj, proj, bias, sk)


def swa_bwd(proj, bias, sk, dmix):
    S = proj.shape[0]
    nb = S // BLK

    def body(q_ref, kp_ref, kc_ref, vp_ref, vc_ref, b_ref, s_ref, do_ref, dqkv_ref, db_ref, ds_ref, ck, cv):
        i = pl.program_id(0)

        @pl.when(i == 0)
        def _():
            db_ref[...] = jnp.zeros_like(db_ref)
            ds_ref[...] = jnp.zeros_like(ds_ref)
            ck[...] = jnp.zeros_like(ck)
            cv[...] = jnp.zeros_like(cv)
        first = i == nb - 1
        _, vjp = jax.vjp(lambda *a: _swa_f(*a, first), q_ref[...], kp_ref[...], kc_ref[...], vp_ref[...],
                         vc_ref[...], b_ref[...], s_ref[...])
        dq, dkp, dkc, dvp, dvc, db, ds = vjp(do_ref[...])
        dqkv_ref[...] = jnp.concatenate([dq, dkc + ck[...], dvc + cv[...]], axis=1).astype(bf16)
        ck[...] = dkp
        cv[...] = dvp
        db_ref[...] += db
        ds_ref[...] += ds

    return pl.pallas_call(
        body, name="swa_bwd", grid=(nb,),
        in_specs=_swa_specs(nb, True) + [pl.BlockSpec((BLK, A_Q), lambda i: (nb - 1 - i, 0))],
        out_specs=[pl.BlockSpec((BLK, D), lambda i: (nb - 1 - i, 0)),
                   pl.BlockSpec((A_HEADS, BLK, 2 * BLK), lambda i: (0, 0, 0)),
                   pl.BlockSpec((16, LANE), lambda i: (0, 0))],
        out_shape=[jax.ShapeDtypeStruct((S, D), bf16), jax.ShapeDtypeStruct((A_HEADS, BLK, 2 * BLK), f32),
                   jax.ShapeDtypeStruct((16, LANE), f32)],
        scratch_shapes=[pltpu.VMEM((BLK, LANE), f32), pltpu.VMEM((BLK, LANE), f32)],
        compiler_params=_cp(("arbitrary",)),
    )(proj, proj, proj, proj, proj, bias, sk, dmix)


def _dnprep_f(xext, w, is_qk):
    c = (w[3:4] * xext + w[2:3] * shift_down(xext, 1) + w[1:2] * shift_down(xext, 2) + w[0:1] * shift_down(xext, 3))
    a = _silu(c)[HALO:]
    n = a * lax.rsqrt(jnp.sum(a * a, axis=-1, keepdims=True) + EPS)
    return jnp.where(is_qk, n, a)


def dnprep_fwd(proj, cw):
    S = proj.shape[0]
    nblk = B_QKV // LANE
    T = S

    def body(x_ref, w_ref, o_ref):
        is_qk = pl.program_id(0) < 2 * B_QK // LANE
        wv = w_ref[...]

        def tile(r0, first):
            o_ref[pl.ds(r0, T), :] = _dnprep_f(_glu_gext(x_ref, r0, first, T), wv, is_qk)

        tile(0, True)

    return pl.pallas_call(
        body, name="dnprep_fwd", grid=(nblk,),
        in_specs=[pl.BlockSpec((S, LANE), lambda j: (0, j)), pl.BlockSpec((4, LANE), lambda j: (0, j))],
        out_specs=pl.BlockSpec((S, LANE), lambda j: (0, j)),
        out_shape=jax.ShapeDtypeStruct((S, B_QKV), f32), compiler_params=_cp(("parallel",)),
    )(proj, cw)


def dnprep_bwd(proj, cw, dqkvn):
    S = proj.shape[0]
    nblk = B_QKV // LANE

    T = S

    def body(x_ref, w_ref, d_ref, dx_ref, dw_ref):
        is_qk = pl.program_id(0) < 2 * B_QK // LANE
        wv = w_ref[...]

        def tile(r0, first):
            _, vjp = jax.vjp(lambda a, b: _dnprep_f(a, b, is_qk), _glu_gext(x_ref, r0, first, T), wv)
            dx, dw = vjp(d_ref[pl.ds(r0, T), :])
            dx_ref[pl.ds(r0, T), :] = dx[HALO:].astype(bf16)
            if not first:
                dx_ref[pl.ds(r0 - HALO, HALO), :] += dx[:HALO]
            return dw

        dw_ref[...] = tile(0, True)

    col = pl.BlockSpec((S, LANE), lambda j: (0, j))
    wsp = pl.BlockSpec((4, LANE), lambda j: (0, j))
    return pl.pallas_call(
        body, name="dnprep_bwd", grid=(nblk,), in_specs=[col, wsp, col], out_specs=[col, wsp],
        out_shape=[jax.ShapeDtypeStruct((S, B_QKV), bf16), jax.ShapeDtypeStruct((4, B_QKV), f32)],
        compiler_params=_cp(("parallel",)),
    )(proj, cw, dqkvn)


def _hdot(a, b, ca=1, cb=0):
    return _dg(a, b, ca, cb, HI)


def _bdg(a, b, ca, cb):
    dn = (((ca,), (cb,)), ((0,), (0,)))
    ah, bh = a.astype(bf16), b.astype(bf16)
    al, bl = (a - ah.astype(f32)).astype(bf16), (b - bh.astype(f32)).astype(bf16)
    return (lax.dot_general(ah, bh, dn, preferred_element_type=f32)
            + lax.dot_general(ah, bl, dn, preferred_element_type=f32)
            + lax.dot_general(al, bh, dn, preferred_element_type=f32))


@jax.custom_vjp
def hbd(a, b):
    return _bdg(a, b, 2, 1)


@jax.custom_vjp
def hbd_nt(a, b):
    return _bdg(a, b, 2, 2)


@jax.custom_vjp
def hbd_tn(a, b):
    return _bdg(a, b, 1, 1)


hbd.defvjp(lambda a, b: (hbd(a, b), (a, b)), lambda r, g: (hbd_nt(g, r[1]), hbd_tn(r[0], g)))
hbd_nt.defvjp(lambda a, b: (hbd_nt(a, b), (a, b)), lambda r, g: (hbd(g, r[1]), hbd_tn(g, r[0])))
hbd_tn.defvjp(lambda a, b: (hbd_tn(a, b), (a, b)), lambda r, g: (hbd_nt(r[1], g), hbd(r[0], g)))


def _stack(xs):
    return jnp.concatenate([x[None] for x in xs], axis=0)


def _lane_col(x, j):
    lane = lax.broadcasted_iota(jnp.int32, (1, LANE), 1)
    return jnp.sum(jnp.where(lane == j, x, 0.0), axis=-1, keepdims=True)


def _tri_inv(a_mat):
    r = lax.broadcasted_iota(jnp.int32, (1, CHUNK, CHUNK), 1)
    c = lax.broadcasted_iota(jnp.int32, (1, CHUNK, CHUNK), 2)
    pw = -a_mat
    inv = (r == c).astype(f32) + pw
    for _ in range(5):
        pw = hbd(pw, pw)
        inv = inv + hbd(inv, pw)
    return inv


@jax.custom_vjp
def _tri_inv_known(a_mat, inv):
    return inv


_tri_inv_known.defvjp(lambda a, inv: (inv, inv),
                      lambda inv, g: (-hbd_tn(inv, hbd_nt(g, inv)), jnp.zeros_like(inv)))


def _dnc_f(q, k, v, seg, prm, inverse=_tri_inv):
    C = CHUNK
    B = q.shape[0]
    rows = seg.shape[0]
    beta_all = _sigmoid(seg)
    xx = seg + prm[1:2]
    g_all = -jnp.exp(prm[0:1]) * (jnp.maximum(xx, 0.0) + jnp.log(1.0 + jnp.exp(-jnp.abs(xx))))
    r2 = lax.broadcasted_iota(jnp.int32, (rows, rows), 0)
    c2 = lax.broadcasted_iota(jnp.int32, (rows, rows), 1)
    within = (r2 >= c2) & (r2 // C == c2 // C)
    gc_all = _hdot(within.astype(f32), g_all)
    beta = _stack([_lane_col(beta_all[C * j:C * (j + 1)], h) for j in range(rows // C) for h in range(6)])
    gc = _stack([_lane_col(gc_all[C * j:C * (j + 1)], 6 + h) for j in range(rows // C) for h in range(6)])
    r = lax.broadcasted_iota(jnp.int32, (1, C, C), 1)
    c = lax.broadcasted_iota(jnp.int32, (1, C, C), 2)
    incl = r >= c
    strict = r > c
    gct = [gc_all[C * j:C * (j + 1)].T for j in range(rows // C)]
    g_row = _stack([jnp.broadcast_to(gct[j][6 + h:7 + h, :], (C, C))
                    for j in range(rows // C) for h in range(6)])
    decay = jnp.where(incl, jnp.exp(jnp.where(incl, gc - g_row, 0.0)), 0.0)
    a_mat = beta * sbd_nt(k, k) * jnp.where(strict, decay, 0.0)
    eg = jnp.exp(gc)
    inv = inverse(a_mat)
    u = hbd(inv, beta * v)
    w = hbd(inv, (beta * eg) * k)
    qc = q * (B_DH ** -0.5)
    attn = sbd_nt(qc, k) * decay
    last = (lax.broadcasted_iota(jnp.int32, (1, C, 1), 1) == C - 1).astype(f32)
    g_last = jnp.sum(gc * last, axis=1, keepdims=True)
    dc = jnp.broadcast_to(jnp.exp(g_last), (B, 1, LANE)).reshape(B, LANE)
    return u, w, qc * eg, k * jnp.exp(g_last - gc), attn, dc, inv


def _b1(a, b, ca, cb):
    return lax.dot_general(a.astype(bf16), b.astype(bf16), (((ca,), (cb,)), ((0,), (0,))), preferred_element_type=f32)


@jax.custom_vjp
def sbd(a, b):
    return _b1(a, b, 2, 1)


@jax.custom_vjp
def sbd_nt(a, b):
    return _b1(a, b, 2, 2)


@jax.custom_vjp
def sbd_tn(a, b):
    return _b1(a, b, 1, 1)


sbd.defvjp(lambda a, b: (sbd(a, b), (a, b)), lambda r, g: (sbd_nt(g, r[1]), sbd_tn(r[0], g)))
sbd_nt.defvjp(lambda a, b: (sbd_nt(a, b), (a, b)), lambda r, g: (sbd(g, r[1]), sbd_tn(g, r[0])))
sbd_tn.defvjp(lambda a, b: (sbd_tn(a, b), (a, b)), lambda r, g: (sbd_nt(r[1], g), sbd(r[0], g)))


def _dns_f(S0, u, w, qd, kt, attn, dcrows):
    dc = _lane_col(dcrows, 0).reshape(6, 1, 1)
    delta = u - sbd(w, S0)
    out = sbd(qd, S0) + sbd(attn, delta)
    return out, dc * S0 + sbd_tn(kt, delta)


def _dnpost_f(o, z, grow):
    outs = []
    for h in range(6):
        oh = o[:, LANE * h:LANE * (h + 1)]
        outs.append(oh * lax.rsqrt(jnp.mean(oh * oh, axis=-1, keepdims=True) + EPS) * grow
                    * _silu(z[:, LANE * h:LANE * (h + 1)]))
    return jnp.concatenate(outs, axis=1)


def _hs(h):
    return slice(LANE * h, LANE * (h + 1))


DN_CHUNKS = 4


def _heads(ref, share):
    return _stack([ref[CHUNK * j:CHUNK * (j + 1), _hs(h // share)]
                   for j in range(ref.shape[0] // CHUNK) for h in range(6)])


def _put_heads(ref, val):
    for j in range(ref.shape[0] // CHUNK):
        for h in range(6):
            ref[CHUNK * j:CHUNK * (j + 1), _hs(h)] = val[6 * j + h]


def _dnc_in_specs():
    rows = CHUNK * DN_CHUNKS
    return [
        pl.BlockSpec((rows, B_QK), lambda n: (n, 0)),
        pl.BlockSpec((rows, B_QK), lambda n: (n, 1)),
        pl.BlockSpec((rows, B_V), lambda n: (n, 1)),
        pl.BlockSpec((rows, LANE), lambda n: (n, 20)),
        pl.BlockSpec((8, LANE), lambda n: (0, 0)),
    ]


def _dnc_out_specs(rev_nc=None, chunks=1):
    ci = (lambda n: n) if rev_nc is None else (lambda n: rev_nc - 1 - n)
    wide = pl.BlockSpec((CHUNK * chunks, B_V), lambda n: (ci(n), 0))
    return [wide, wide, wide, wide, pl.BlockSpec((chunks, 6, CHUNK, CHUNK), lambda n: (ci(n), 0, 0, 0)),
            pl.BlockSpec((chunks, 8, LANE), lambda n: (ci(n), 0, 0))]


def _dc_rows(dc):
    pad = jnp.zeros((2, LANE), f32)
    return _stack([jnp.concatenate([dc[6 * j:6 * (j + 1)], pad], axis=0) for j in range(dc.shape[0] // 6)])


def _dnc_shapes(S):
    nc = S // CHUNK
    wide = jax.ShapeDtypeStruct((S, B_V), f32)
    return [wide, wide, wide, wide, jax.ShapeDtypeStruct((nc, 6, CHUNK, CHUNK), f32),
            jax.ShapeDtypeStruct((nc, 8, LANE), f32)]


def dnc_fwd(qkvn, proj, prm):
    S = proj.shape[0]

    def body(q_ref, k_ref, v_ref, s_ref, p_ref, u_ref, w_ref, qd_ref, kt_ref, at_ref, dc_ref, inv_ref):
        u, w, qd, kt, attn, dc, inv = _dnc_f(_heads(q_ref, 2), _heads(k_ref, 2), _heads(v_ref, 1), s_ref[...],
                                             p_ref[...])
        inv_ref[...] = inv.reshape(inv_ref.shape)
        _put_heads(u_ref, u)
        _put_heads(w_ref, w)
        _put_heads(qd_ref, qd)
        _put_heads(kt_ref, kt)
        at_ref[...] = attn.reshape(at_ref.shape)
        dc_ref[...] = _dc_rows(dc)

    outs = _dnc_out_specs(chunks=DN_CHUNKS)
    out = pl.pallas_call(
        body, name="dn_chunk_fwd", grid=(S // (CHUNK * DN_CHUNKS),), in_specs=_dnc_in_specs(),
        out_specs=outs + [outs[4]], out_shape=_dnc_shapes(S) + [_dnc_shapes(S)[4]],
        compiler_params=_cp(("parallel",)),
    )(qkvn, qkvn, qkvn, proj, prm)
    return out[:6], out[6]


def dnc_bwd(qkvn, proj, prm, inv, cots):
    S = proj.shape[0]

    def body(q_ref, k_ref, v_ref, s_ref, p_ref, inv_ref, du_ref, dw_ref, dqd_ref, dkt_ref, dat_ref, ddc_ref,
             dx_ref, dseg_ref, dprm_ref):
        @pl.when(pl.program_id(0) == 0)
        def _():
            dprm_ref[...] = jnp.zeros_like(dprm_ref)
        nb = 6 * DN_CHUNKS
        known = functools.partial(_tri_inv_known, inv=inv_ref[...].reshape(nb, CHUNK, CHUNK))
        _, vjp = jax.vjp(lambda *a: _dnc_f(*a, inverse=known)[:6], _heads(q_ref, 2), _heads(k_ref, 2),
                         _heads(v_ref, 1), s_ref[...], p_ref[...])
        ddc = jnp.concatenate([ddc_ref[j, 0:6, :] for j in range(DN_CHUNKS)], axis=0)
        dq, dk, dv, dseg, dprm = vjp((_heads(du_ref, 1), _heads(dw_ref, 1), _heads(dqd_ref, 1), _heads(dkt_ref, 1),
                                      dat_ref[...].reshape(nb, CHUNK, CHUNK), ddc))
        for j in range(DN_CHUNKS):
            o = 6 * j
            dx_ref[CHUNK * j:CHUNK * (j + 1), :] = jnp.concatenate(
                [dq[o] + dq[o + 1], dq[o + 2] + dq[o + 3], dq[o + 4] + dq[o + 5],
                 dk[o] + dk[o + 1], dk[o + 2] + dk[o + 3], dk[o + 4] + dk[o + 5]] + [dv[o + h] for h in range(6)], axis=1)
        dseg_ref[...] = dseg.astype(bf16)
        dprm_ref[...] += dprm

    rows = CHUNK * DN_CHUNKS
    outs = _dnc_out_specs(chunks=DN_CHUNKS)
    return pl.pallas_call(
        body, name="dn_chunk_bwd", grid=(S // rows,),
        in_specs=_dnc_in_specs() + [outs[4]] + outs,
        out_specs=[pl.BlockSpec((rows, B_QKV), lambda n: (n, 0)), pl.BlockSpec((rows, LANE), lambda n: (n, 0)),
                   pl.BlockSpec((8, LANE), lambda n: (0, 0))],
        out_shape=[jax.ShapeDtypeStruct((S, B_QKV), f32), jax.ShapeDtypeStruct((S, LANE), bf16),
                   jax.ShapeDtypeStruct((8, LANE), f32)],
        compiler_params=_cp(("arbitrary",)),
    )(qkvn, qkvn, qkvn, proj, prm, inv, *cots)


def dns_fwd(chunked):
    u = chunked[0]
    S = u.shape[0]
    nc = S // CHUNK

    def body(u_ref, w_ref, qd_ref, kt_ref, at_ref, dc_ref, o_ref, st_ref, st):
        @pl.when(pl.program_id(0) == 0)
        def _():
            st[...] = jnp.zeros_like(st)
        S0 = st[...]
        st_ref[0] = S0
        out, S1 = _dns_f(S0, _heads(u_ref, 1), _heads(w_ref, 1), _heads(qd_ref, 1), _heads(kt_ref, 1),
                         at_ref[0], dc_ref[0, 0:6, :])
        _put_heads(o_ref, out)
        st[...] = S1

    return pl.pallas_call(
        body, name="dn_scan_fwd", grid=(nc,), in_specs=_dnc_out_specs(),
        out_specs=[pl.BlockSpec((CHUNK, B_V), lambda n: (n, 0)),
                   pl.BlockSpec((1, 6, B_DH, B_DH), lambda n: (n, 0, 0, 0))],
        out_shape=[jax.ShapeDtypeStruct((S, B_V), f32), jax.ShapeDtypeStruct((nc, 6, B_DH, B_DH), f32)],
        scratch_shapes=[pltpu.VMEM((6, B_DH, B_DH), f32)],
        compiler_params=_cp(("arbitrary",)),
    )(*chunked)


def dns_bwd(chunked, states, do):
    S = do.shape[0]
    nc = S // CHUNK

    def body(u_ref, w_ref, qd_ref, kt_ref, at_ref, dc_ref, st_ref, do_ref,
             du_ref, dw_ref, dqd_ref, dkt_ref, dat_ref, ddc_ref, dst):
        @pl.when(pl.program_id(0) == 0)
        def _():
            dst[...] = jnp.zeros_like(dst)
        _, vjp = jax.vjp(_dns_f, st_ref[0], _heads(u_ref, 1), _heads(w_ref, 1), _heads(qd_ref, 1), _heads(kt_ref, 1),
                         at_ref[0], dc_ref[0, 0:6, :])
        dS0, du, dw, dqd, dkt, dat, ddc = vjp((_heads(do_ref, 1), dst[...]))
        dst[...] = dS0
        _put_heads(du_ref, du)
        _put_heads(dw_ref, dw)
        _put_heads(dqd_ref, dqd)
        _put_heads(dkt_ref, dkt)
        dat_ref[0] = dat
        ddc_ref[0] = jnp.concatenate([ddc, jnp.zeros((2, LANE), f32)], axis=0)

    return pl.pallas_call(
        body, name="dn_scan_bwd", grid=(nc,),
        in_specs=_dnc_out_specs(nc) + [pl.BlockSpec((1, 6, B_DH, B_DH), lambda n: (nc - 1 - n, 0, 0, 0)),
                                       pl.BlockSpec((CHUNK, B_V), lambda n: (nc - 1 - n, 0))],
        out_specs=_dnc_out_specs(nc), out_shape=_dnc_shapes(S),
        scratch_shapes=[pltpu.VMEM((6, B_DH, B_DH), f32)],
        compiler_params=_cp(("arbitrary",)),
    )(*chunked, states, do)


def dnpost_fwd(o, proj, prm):
    S = o.shape[0]
    t = min(512, S)

    def body(o_ref, z_ref, p_ref, y_ref):
        y_ref[...] = _dnpost_f(o_ref[...], z_ref[...], p_ref[2:3, :]).astype(bf16)

    tok = pl.BlockSpec((t, B_V), lambda i: (i, 0))
    return pl.pallas_call(
        body, name="dn_post_fwd", grid=(S // t,),
        in_specs=[tok, pl.BlockSpec((t, B_V), lambda i: (i, 2)), pl.BlockSpec((8, LANE), lambda i: (0, 0))],
        out_specs=tok, out_shape=jax.ShapeDtypeStruct((S, B_V), bf16), compiler_params=_cp(("parallel",)),
    )(o, proj, prm)


def dnpost_bwd(o, proj, prm, dmix):
    S = o.shape[0]
    t = min(512, S)

    def body(o_ref, z_ref, p_ref, dy_ref, do_ref, dz_ref, dg_ref):
        @pl.when(pl.program_id(0) == 0)
        def _():
            dg_ref[...] = jnp.zeros_like(dg_ref)
        _, vjp = jax.vjp(_dnpost_f, o_ref[...], z_ref[...], p_ref[2:3, :])
        do, dz, dg = vjp(dy_ref[...])
        do_ref[...] = do
        dz_ref[...] = dz.astype(bf16)
        dg_ref[...] += dg

    tok = pl.BlockSpec((t, B_V), lambda i: (i, 0))
    return pl.pallas_call(
        body, name="dn_post_bwd", grid=(S // t,),
        in_specs=[tok, pl.BlockSpec((t, B_V), lambda i: (i, 2)), pl.BlockSpec((8, LANE), lambda i: (0, 0)), tok],
        out_specs=[tok, tok, pl.BlockSpec((1, LANE), lambda i: (0, 0))],
        out_shape=[jax.ShapeDtypeStruct((S, B_V), f32), jax.ShapeDtypeStruct((S, B_V), bf16),
                   jax.ShapeDtypeStruct((1, LANE), f32)],
        compiler_params=_cp(("arbitrary",)),
    )(o, proj, prm, dmix)


N_FF_BLK = D_FF // LANE
GU_SHARD = 2 * D_FF // 4


GLU_ROWS = 256
HALO = 16


def _glu_f(gext, up, w, b):
    c = w[2:3] * gext + w[1:2] * shift_down(gext, 1) + w[0:1] * shift_down(gext, 2) + b
    return _silu(c)[HALO:] * up


def _glu_gext(g_ref, r0, first, T=GLU_ROWS):
    if first:
        return jnp.concatenate([jnp.zeros((HALO, LANE), f32), g_ref[0:T, :].astype(f32)], axis=0)
    return g_ref[pl.ds(r0 - HALO, T + HALO), :].astype(f32)


def glu_fwd(gu, w, b, name):
    S = gu.shape[0]
    T = min(GLU_ROWS, S // 2)

    def body(g_ref, u_ref, w_ref, b_ref, o_ref):
        wv, bv = w_ref[...], b_ref[...]

        def tile(r0, first):
            act = _glu_f(_glu_gext(g_ref, r0, first, T), u_ref[pl.ds(r0, T), :].astype(f32), wv, bv)
            o_ref[pl.ds(r0, T), :] = act.astype(bf16)

        tile(0, True)

        @pl.loop(1, S // T)
        def _(t):
            tile(pl.multiple_of(t * T, T), False)

    col = pl.BlockSpec((S, LANE), lambda j: (0, j))
    return pl.pallas_call(
        body, name=name, grid=(N_FF_BLK,),
        in_specs=[col, pl.BlockSpec((S, LANE), lambda j: (0, N_FF_BLK + j)), pl.BlockSpec((3, LANE), lambda j: (0, j)),
                  pl.BlockSpec((1, LANE), lambda j: (0, j))],
        out_specs=col, out_shape=jax.ShapeDtypeStruct((S, D_FF), bf16), compiler_params=_cp(("parallel",)),
    )(gu, gu, w, b.reshape(1, D_FF))


def glu_bwd(gu, w, b, dact, name):
    S = gu.shape[0]
    T = min(GLU_ROWS, S // 2)

    def body(g_ref, u_ref, w_ref, b_ref, d_ref, dg_ref, dw_ref, db_ref, acc):
        wv, bv = w_ref[...], b_ref[...]

        def tile(r0, first):
            _, vjp = jax.vjp(_glu_f, _glu_gext(g_ref, r0, first, T), u_ref[pl.ds(r0, T), :].astype(f32), wv, bv)
            dgx, du, dw, db = vjp(d_ref[pl.ds(r0, T), :].astype(f32))
            acc[pl.ds(r0, T), :] = dgx[HALO:]
            if not first:
                acc[pl.ds(r0 - HALO, HALO), :] += dgx[:HALO]
            dg_ref[1, pl.ds(r0, T), :] = du.astype(bf16)
            return dw, db

        dw0, db0 = tile(0, True)
        dw_ref[...] = dw0
        db_ref[...] = db0

        @pl.loop(1, S // T)
        def _(t):
            dw, db = tile(pl.multiple_of(t * T, T), False)
            dw_ref[...] += dw
            db_ref[...] += db

        dg_ref[0] = acc[...].astype(bf16)

    col = pl.BlockSpec((S, LANE), lambda j: (0, j))
    wsp = pl.BlockSpec((3, LANE), lambda j: (0, j))
    bsp = pl.BlockSpec((1, LANE), lambda j: (0, j))
    return pl.pallas_call(
        body, name=name, grid=(N_FF_BLK,),
        in_specs=[col, pl.BlockSpec((S, LANE), lambda j: (0, N_FF_BLK + j)), wsp, bsp, col],
        out_specs=[pl.BlockSpec((2, S, LANE), lambda j: (0, 0, j)), wsp, bsp],
        out_shape=[jax.ShapeDtypeStruct((2, S, D_FF), bf16), jax.ShapeDtypeStruct((3, D_FF), f32),
                   jax.ShapeDtypeStruct((1, D_FF), f32)],
        scratch_shapes=[pltpu.VMEM((S, LANE), f32)],
        compiler_params=_cp(("parallel",)),
    )(gu, gu, w, b.reshape(1, D_FF), dact)


def gu_fwd(n2, wg, name):
    S = n2.shape[0]
    tm = min(MM_ROWS, S)

    def body(a_ref, w_ref, o_ref):
        o_ref[...] = _dg(a_ref[...], w_ref[...], 1, 0).astype(bf16)

    return pl.pallas_call(
        body, name=name, grid=(4, S // tm),
        in_specs=[pl.BlockSpec((tm, D), lambda s, m: (m, 0)), pl.BlockSpec((None, D, GU_SHARD), lambda s, m: (s, 0, 0))],
        out_specs=pl.BlockSpec((tm, GU_SHARD), lambda s, m: (m, s)),
        out_shape=jax.ShapeDtypeStruct((S, 2 * D_FF), bf16), compiler_params=_cp(("parallel", "parallel")),
    )(n2, wg)


def gu_bwd_x(dgu, wg, norm, name):
    S = dgu.shape[1]
    tm = min(NORM_ROWS, S)

    def body(d_ref, w_ref, h_ref, g_ref, r_ref, o_ref, dg_ref):
        _acc_then_norm_bwd(_dg(d_ref[...], w_ref[...], 1, 1), 4, h_ref, g_ref, r_ref, o_ref, dg_ref)

    tok = pl.BlockSpec((tm, D), lambda m, s: (m, 0))
    vec = pl.BlockSpec((1, D), lambda m, s: (0, 0))
    return pl.pallas_call(
        body, name=name, grid=(S // tm, 4),
        in_specs=[pl.BlockSpec((None, tm, GU_SHARD), lambda m, s: (s // 2, m, s % 2)),
                  pl.BlockSpec((None, D, GU_SHARD), lambda m, s: (s, 0, 0)), tok, vec, tok],
        out_specs=[tok, vec],
        out_shape=[jax.ShapeDtypeStruct((S, D), f32), jax.ShapeDtypeStruct((1, D), f32)],
        compiler_params=_cp(("arbitrary", "arbitrary")),
    )(dgu, wg, norm[0], norm[1].reshape(1, D), norm[2])


def gu_bwd_w(n2, dgu, name):
    S = n2.shape[0]
    tm = min(MM_ROWS, S)
    nm = S // tm

    def body(a_ref, d_ref, o_ref, acc):
        @pl.when(pl.program_id(1) == 0)
        def _():
            acc[...] = jnp.zeros_like(acc)
        acc[...] += _dg(a_ref[...], d_ref[...], 0, 0)

        @pl.when(pl.program_id(1) == nm - 1)
        def _():
            o_ref[...] = acc[...].astype(bf16)

    return pl.pallas_call(
        body, name=name, grid=(4, nm),
        in_specs=[pl.BlockSpec((tm, D), lambda s, m: (m, 0)),
                  pl.BlockSpec((None, tm, GU_SHARD), lambda s, m: (s // 2, m, s % 2))],
        out_specs=pl.BlockSpec((None, D, GU_SHARD), lambda s, m: (s, 0, 0)),
        out_shape=jax.ShapeDtypeStruct((4, D, GU_SHARD), bf16),
        scratch_shapes=[pltpu.VMEM((D, GU_SHARD), f32)],
        compiler_params=_cp(("parallel", "arbitrary")),
    )(n2, dgu)


def _pair_cols(w):
    lead = w.shape[:-1]
    return w.reshape(lead + (2, 6, A_DH)).swapaxes(-3, -2).reshape(lead + (A_Q,))


def _unpair_cols(w):
    lead = w.shape[:-1]
    return w.reshape(lead + (6, 2, A_DH)).swapaxes(-3, -2).reshape(lead + (A_Q,))


def _lay_in_a(w):
    return jnp.concatenate([_pair_cols(w[:, :A_Q]), w[:, A_Q:]], axis=1)


def _unlay_in_a(w):
    return jnp.concatenate([_unpair_cols(w[:, :A_Q]), w[:, A_Q:]], axis=1)


def _lay_out_a(w):
    return jnp.concatenate([_pair_cols(w[:A_Q].T).T, w[A_Q:]], axis=0)


def _unlay_out_a(w):
    return jnp.concatenate([_unpair_cols(w[:A_Q].T).T, w[A_Q:]], axis=0)


def _lay_in_b(w):
    return jnp.concatenate([w[:, :2304], w[:, 2316:], w[:, 2304:2316],
                            jnp.zeros((w.shape[0], LANE - 12), w.dtype)], axis=1)


def _unlay_in_b(w):
    return jnp.concatenate([w[:, :2304], w[:, 2560:2572], w[:, 2304:2560]], axis=1)


def _chip_cols(w):
    return jnp.moveaxis(w.reshape(w.shape[0], 4, w.shape[1] // 4), 1, 0)


def _unchip_cols(w):
    return jnp.moveaxis(w, 0, 1).reshape(w.shape[1], 4 * w.shape[2])


def _local_step(x, mem, target, P):
    arrive = P.get("arrive", lambda key, after: None)
    ready = P.get("ready", lambda key, grads, dep: dep)
    sk = jnp.zeros((16, LANE), f32).at[:A_HEADS].set(jnp.broadcast_to(P["sinks"][:, None], (A_HEADS, LANE)))
    prm = jnp.zeros((8, LANE), f32).at[0, 6:12].set(P["a_log"]).at[1, 6:12].set(P["dt_bias"]).at[2].set(P["out_norm_g"])
    bias = bias_build(P["rel_bias"])
    saved = []
    h = x
    n1 = rms_fwd(h, P["g_mix"][0], "rms_mix0")
    for i in range(2):
        arrive(("w_in", i), n1)
        proj = mm_nn(n1, P["w_in_a"] if i == 0 else P["w_in_b"], name="proj_a" if i == 0 else "proj_b")
        arrive(("w_mem", i), proj)
        kv = memkv_fwd(mem, P["g_mem"][i], P["w_mem"][i], f"memkv{i}")
        if i == 0:
            self_out = swa_fwd(proj, bias, sk)
            cross = xattn_fwd(proj, A_Q + 2 * LANE, kv, "xattn_a")
            extra = ()
        else:
            qkvn = dnprep_fwd(proj, P["conv_qkv"])
            chunked, inv = dnc_fwd(qkvn, proj, prm)
            o, states = dns_fwd(chunked)
            self_out = dnpost_fwd(o, proj, prm)
            cross = xattn_fwd(proj, 2304, kv, "xattn_b")
            extra = (qkvn, chunked, inv, states, o)
        mix = (self_out, cross)
        arrive(("w_out", i), cross)
        h2, n2 = mm_res_norm(mix, P["w_out"][i], h, P["g_ffn"][i], f"out_proj{i}")
        arrive(("w_gu", i), n2)
        gu = gu_fwd(n2, P["w_gu"][i], f"gate_up{i}")
        act = glu_fwd(gu, P["ffn_cw"][i], P["ffn_cb"][i], f"glu{i}")
        arrive(("w_down", i), act)
        saved.append((h, n1, kv, proj, mix, h2, n2, gu, act, extra))
        if i == 0:
            h, n1 = mm_res_norm(act, P["w_down"][i], h2, P["g_mix"][1], f"down{i}")
        else:
            h = mm_nn(act, P["w_down"][i], res=h2, name=f"down{i}")

    loss, dh, dg_fin = loss_head(h, P["g_fin"], target)
    G = {"g_fin": dg_fin[0], "g_mix": [None, None], "g_mem": [None, None], "g_ffn": [None, None],
         "w_mem": [None, None], "w_out": [None, None], "w_gu": [None, None], "w_down": [None, None],
         "ffn_cw": [None, None], "ffn_cb": [None, None]}
    for i in (1, 0):
        hin, n1, kv, proj, mix, h2, n2, gu, act, extra = saved[i]
        dact = mm_nt(dh, P["w_down"][i], out_dtype=bf16, name=f"d_act{i}")
        G["w_down"][i] = mm_tn(act, dh, name=f"dw_down{i}")
        dgu, dcw, dcb = glu_bwd(gu, P["ffn_cw"][i], P["ffn_cb"][i], dact, f"glu_bwd{i}")
        G["ffn_cw"][i], G["ffn_cb"][i] = dcw, dcb[0]
        G["w_gu"][i] = gu_bwd_w(n2, dgu, f"dw_gu{i}")
        g_ffn = ready(("ffn", i), G, P["g_ffn"][i])
        dh2, dg = gu_bwd_x(dgu, P["w_gu"][i], (h2, g_ffn, dh), f"d_n2_{i}")
        G["g_ffn"][i] = dg[0]
        dmix = mm_nt(dh2, P["w_out"][i], name=f"d_mix{i}")
        G["w_out"][i] = mm_tn(mix, dh2, name=f"dw_out{i}")
        if i == 0:
            dqkv, dbias, dsk = swa_bwd(proj, bias, sk, dmix)
            dxq, dkv = xattn_bwd(proj, A_Q + 2 * LANE, kv, dmix, "xattn_a_bwd")
            dproj = (dqkv, dxq)
            G["sinks"] = dsk[:A_HEADS, 0]
            G["rel_bias"] = bias_grad(dbias)[:, :A_HEADS]
            w_in, gname = P["w_in_a"], "w_in_a"
        else:
            qkvn, chunked, inv, states, o = extra
            do, dz, dgo = dnpost_bwd(o, proj, prm, dmix)
            dqkvn, dseg, dprm = dnc_bwd(qkvn, proj, prm, inv, dns_bwd(chunked, states, do))
            draw, dconv = dnprep_bwd(proj, P["conv_qkv"], dqkvn)
            dxq, dkv = xattn_bwd(proj, 2304, kv, dmix, "xattn_b_bwd")
            dproj = (draw, dz, dxq, dseg)
            G["conv_qkv"] = dconv
            G["a_log"], G["dt_bias"], G["out_norm_g"] = dprm[0, 6:12], dprm[1, 6:12], dgo[0]
            w_in, gname = P["w_in_b"], "w_in_b"
        G[gname] = mm_tn(n1, dproj, name=f"d{gname}")
        dh, dg = mm_nt_norm(dproj, w_in, (hin, P["g_mix"][i], dh2), f"d_n1_{i}")
        G["g_mix"][i] = dg[0]
        dgm, dwm = memkv_bwd(mem, P["g_mem"][i], P["w_mem"][i], dkv, f"memkv_bwd{i}")
        G["g_mem"][i], G["w_mem"][i] = dgm[0], dwm
        ready(("mix", i), G, None)
    return loss, dh, G


def _grads_to_ref(G):
    return {
        "rel_bias": G["rel_bias"], "norm_mix_g": jnp.stack(G["g_mix"]), "norm_mem_g": jnp.stack(G["g_mem"]),
        "w_mem_kv": jnp.stack(G["w_mem"]),
        "w_out": jnp.stack([_unlay_out_a(G["w_out"][0]), G["w_out"][1]]),
        "w_in_a": _unlay_in_a(G["w_in_a"])[None], "sinks_a": G["sinks"][None],
        "w_in_b": _unlay_in_b(G["w_in_b"])[None], "conv_qkv_b": G["conv_qkv"][None],
        "a_log_b": G["a_log"][None], "dt_bias_b": G["dt_bias"][None], "out_norm_g_b": G["out_norm_g"][None],
        "norm_ffn_g": jnp.stack(G["g_ffn"]),
        "w_gate_up": jnp.stack([_unchip_cols(G["w_gu"][0]), _unchip_cols(G["w_gu"][1])]).astype(f32),
        "ffn_conv_w": jnp.stack(G["ffn_cw"]), "ffn_conv_b": jnp.stack(G["ffn_cb"]),
        "w_down": jnp.stack(G["w_down"]), "final_norm_g": G["g_fin"],
    }


ANY = pl.BlockSpec(memory_space=pl.ANY)


def _place():
    return lax.axis_index("x"), lax.axis_index("y"), lax.axis_index("c")


def allreduce_small(buf):
    R = buf.shape[0]

    def body(b_ref, o_ref, recv, ssem, rsem):
        x, y, c = _place()
        me = 4 * x + 2 * y + c

        def peer(k):
            return (1 - x if k & 4 else x, 1 - y if k & 2 else y, 1 - c if k & 1 else c)

        def remote(k, slot):
            return pltpu.make_async_remote_copy(
                src_ref=b_ref, dst_ref=recv.at[slot], send_sem=ssem.at[k - 1], recv_sem=rsem.at[k - 1],
                device_id=peer(k), device_id_type=MESH)

        sends = [remote(k, me) for k in range(1, 8)]
        for cp in sends:
            cp.start()
        recv[me] = b_ref[...]
        for k in range(1, 8):
            px, py, pc = peer(k)
            remote(k, 4 * px + 2 * py + pc).wait_recv()
        for cp in sends:
            cp.wait_send()
        total = recv[0]
        for j in range(1, 8):
            total = total + recv[j]
        o_ref[...] = total

    return pl.pallas_call(
        body, name="small_allreduce",
        in_specs=[pl.BlockSpec(memory_space=pltpu.VMEM)], out_specs=pl.BlockSpec(memory_space=pltpu.VMEM),
        out_shape=jax.ShapeDtypeStruct(buf.shape, f32),
        scratch_shapes=[pltpu.VMEM((8, R, LANE), f32), pltpu.SemaphoreType.DMA((7,)), pltpu.SemaphoreType.DMA((7,))],
    )(buf)


def sum_slots(own, recv, chip, core, name):
    _, R, C = recv.shape
    tr = _row_tile(R, 256)
    nt = R // tr

    def body(p_ref, a_ref, r_ref, o_ref):
        acc = jnp.zeros((tr, C), f32)
        for s in range(4):
            acc = acc + jnp.where(p_ref[0] == s, a_ref[s], r_ref[s]).astype(f32)
        o_ref[...] = acc

    slots = pl.BlockSpec((4, tr, C), lambda i, p_ref: (0, i, 0))
    return pl.pallas_call(
        body, name=name, out_shape=jax.ShapeDtypeStruct((2 * R, C), f32),
        grid_spec=pltpu.PrefetchScalarGridSpec(
            num_scalar_prefetch=1, grid=(nt,), in_specs=[slots, slots],
            out_specs=pl.BlockSpec((tr, C), lambda i, p_ref: (p_ref[1] * nt + i, 0))),
        compiler_params=_cp(("parallel",)),
    )(jnp.stack([chip, core]).astype(jnp.int32), own, recv)


def _half(ref, core, axis=0):
    half = ref.shape[axis] // 2
    idx = (slice(None),) * axis + (pl.ds(core * half, half),)
    return ref.at[idx]


IN_HBM = pl.BlockSpec(memory_space=pltpu.HBM)
IN_SEM = pl.BlockSpec(memory_space=pltpu.SEMAPHORE)
SIDE_EFFECT = pltpu.SideEffectType.DATAFLOW_SIDE_EFFECTING


def _gather_copy(buf, i, k, ssem, rsem, place, landing):
    x, y, c = place
    px, py = [(1 - x, y), (x, 1 - y), (1 - x, 1 - y)][k]
    me = 2 * x + y
    return pltpu.make_async_remote_copy(
        src_ref=buf.at[me], dst_ref=buf.at[me if landing == "theirs" else 2 * px + py],
        send_sem=ssem.at[3 * i + k], recv_sem=rsem.at[3 * i + k], device_id=(px, py, c), device_id_type=MESH)


def gather_start(groups):
    flat = [b for grp in groups for b in grp]
    n, ng = len(flat), len(groups)

    def body(*refs):
        bufs, sems = refs[:n], refs[n:n + 2 * ng]
        place = _place()
        j = 0
        for g, grp in enumerate(groups):
            for i in range(len(grp)):
                for k in range(3):
                    _gather_copy(bufs[j], i, k, sems[2 * g], sems[2 * g + 1], place, "theirs").start()
                j += 1

    sem_shapes = [pltpu.SemaphoreType.DMA((3 * len(grp),)) for grp in groups for _ in range(2)]
    out = pl.pallas_call(
        body, name="gather_start", in_specs=[IN_HBM] * n, out_specs=(*[IN_SEM] * (2 * ng), *[IN_HBM] * n),
        out_shape=(*sem_shapes, *[pltpu.HBM(b.shape, b.dtype) for b in flat]),
        input_output_aliases={i: 2 * ng + i for i in range(n)},
        compiler_params=pltpu.CompilerParams(has_side_effects=SIDE_EFFECT),
    )(*[pltpu.with_memory_space_constraint(b, pltpu.HBM) for b in flat])
    sems, bufs = out[:2 * ng], list(out[2 * ng:])
    flights, j = [], 0
    for g, grp in enumerate(groups):
        flights.append((bufs[j:j + len(grp)], sems[2 * g], sems[2 * g + 1]))
        j += len(grp)
    return flights


def gather_wait(flight, after, name):
    bufs, ssem, rsem = flight
    n = len(bufs)

    def body(*refs):
        place = _place()
        for i in range(n):
            for k in range(3):
                cp = _gather_copy(refs[i], i, k, refs[n], refs[n + 1], place, "mine")
                cp.wait_send()
                cp.wait_recv()

    return pl.pallas_call(
        body, name=name, in_specs=[IN_HBM] * n + [IN_SEM, IN_SEM, ANY], out_specs=[IN_HBM] * n,
        out_shape=[pltpu.HBM(b.shape, b.dtype) for b in bufs], input_output_aliases={i: i for i in range(n)},
        compiler_params=pltpu.CompilerParams(has_side_effects=SIDE_EFFECT),
    )(*bufs, ssem, rsem, after)


def _scatter_copy(src, land, j, k, ssem, rsem, place, landing):
    x, y, c = place
    px, py = [(1 - x, y), (x, 1 - y), (1 - x, 1 - y)][k]
    return pltpu.make_async_remote_copy(
        src_ref=src.at[2 * px + py], dst_ref=land.at[2 * x + y if landing == "theirs" else 2 * px + py],
        send_sem=ssem.at[3 * j + k], recv_sem=rsem.at[3 * j + k], device_id=(px, py, c), device_id_type=MESH)


def scatter_start(srcs, name):
    n = len(srcs)
    lands = [lax.empty(g.shape, g.dtype) for g in srcs]

    def body(*refs):
        place = _place()
        for j in range(n):
            for k in range(3):
                _scatter_copy(refs[j], refs[n + j], j, k, refs[2 * n], refs[2 * n + 1], place, "theirs").start()
        refs[-1][...] = jnp.zeros_like(refs[-1])

    sem = pltpu.SemaphoreType.DMA((3 * n,))
    hbm = [pltpu.with_memory_space_constraint(b, pltpu.HBM) for b in list(srcs) + lands]
    out = pl.pallas_call(
        body, name=name, in_specs=[IN_HBM] * (2 * n),
        out_specs=(IN_SEM, IN_SEM, *[IN_HBM] * (2 * n), pl.BlockSpec(memory_space=pltpu.VMEM)),
        out_shape=(sem, sem, *[pltpu.HBM(b.shape, b.dtype) for b in hbm], jax.ShapeDtypeStruct((8, LANE), f32)),
        input_output_aliases={i: 2 + i for i in range(2 * n)},
        compiler_params=pltpu.CompilerParams(has_side_effects=SIDE_EFFECT),
    )(*hbm)
    return (list(out[2:2 + n]), list(out[2 + n:2 + 2 * n]), out[0], out[1]), out[-1]


def scatter_wait(flight, after, name):
    srcs, lands, ssem, rsem = flight
    n = len(srcs)

    def body(*refs):
        place = _place()
        for j in range(n):
            for k in range(3):
                cp = _scatter_copy(refs[j], refs[n + j], j, k, refs[2 * n], refs[2 * n + 1], place, "mine")
                cp.wait_send()
                cp.wait_recv()

    out = pl.pallas_call(
        body, name=name, in_specs=[IN_HBM] * (2 * n) + [IN_SEM, IN_SEM, ANY], out_specs=[IN_HBM] * (2 * n),
        out_shape=[pltpu.HBM(b.shape, b.dtype) for b in list(srcs) + list(lands)],
        input_output_aliases={i: i for i in range(2 * n)},
        compiler_params=pltpu.CompilerParams(has_side_effects=SIDE_EFFECT),
    )(*srcs, *lands, ssem, rsem, after)
    return list(out[:n]), list(out[n:])


def pair_exchange(gbufs, name):
    n = len(gbufs)

    def body(*refs):
        ins, outs = refs[:n], refs[n:2 * n]
        ssem, rsem = refs[2 * n:]
        x, y, c = _place()
        cps = [pltpu.make_async_remote_copy(
            src_ref=_half(ins[j], 1 - c, axis=1), dst_ref=outs[j], send_sem=ssem.at[j], recv_sem=rsem.at[j],
            device_id=(x, y, 1 - c), device_id_type=MESH) for j in range(n)]
        for cp in cps:
            cp.start()
        for cp in cps:
            cp.wait()

    return pl.pallas_call(
        body, name=name, in_specs=[ANY] * n, out_specs=[ANY] * n,
        out_shape=[jax.ShapeDtypeStruct((4, g.shape[1] // 2, g.shape[2]), g.dtype) for g in gbufs],
        scratch_shapes=[pltpu.SemaphoreType.DMA((n,)), pltpu.SemaphoreType.DMA((n,))],
    )(*gbufs)


def _row_tile(rows, cap=512):
    return max(t for t in range(16, min(rows, cap) + 1, 16) if rows % t == 0)


def pair_sum(mine, theirs, core, name):
    _, R, C = mine.shape
    half = R // 2
    tr = _row_tile(half)
    nt = half // tr

    def body(c_ref, a_ref, b_ref, o_ref):
        o_ref[...] = (a_ref[...].astype(f32) + b_ref[...].astype(f32)).astype(bf16)

    return pl.pallas_call(
        body, name=name, out_shape=jax.ShapeDtypeStruct(theirs.shape, bf16),
        grid_spec=pltpu.PrefetchScalarGridSpec(
            num_scalar_prefetch=1, grid=(4, nt),
            in_specs=[pl.BlockSpec((None, tr, C), lambda s, i, c_ref: (s, c_ref[0] * nt + i, 0)),
                      pl.BlockSpec((None, tr, C), lambda s, i, c_ref: (s, i, 0))],
            out_specs=pl.BlockSpec((None, tr, C), lambda s, i, c_ref: (s, i, 0))),
        compiler_params=_cp(("parallel", "parallel")),
    )(jnp.reshape(core, (1,)).astype(jnp.int32), mine, theirs)


def final_exchange(fins):
    n = len(fins)

    def body(*refs):
        outs = refs[n:2 * n]
        ssem, rsem = refs[2 * n:]
        x, y, c = _place()
        cps = [pltpu.make_async_remote_copy(
            src_ref=_half(outs[j], c), dst_ref=_half(outs[j], c), send_sem=ssem.at[j], recv_sem=rsem.at[j],
            device_id=(x, y, 1 - c), device_id_type=MESH) for j in range(n)]
        for cp in cps:
            cp.start()
        for cp in cps:
            cp.wait()

    return pl.pallas_call(
        body, name="final_exchange", in_specs=[ANY] * n, out_specs=[ANY] * n,
        out_shape=[jax.ShapeDtypeStruct(f.shape, f.dtype) for f in fins],
        input_output_aliases={j: j for j in range(n)},
        scratch_shapes=[pltpu.SemaphoreType.DMA((n,)), pltpu.SemaphoreType.DMA((n,))],
    )(*fins)


def adamw_big(w, m, v, gs, row0, name):
    L, R, C = w.shape
    tr = _row_tile(math.gcd(R, row0) if row0 else R, max(16, 262144 // C // 16 * 16))
    b0 = row0 // tr

    def body(*refs):
        w_ref, m_ref, v_ref = refs[:3]
        g_refs = refs[3:3 + L]
        g_ref, d_ref, nm_ref, nv_ref = refs[3 + L:]
        g = g_refs[0][...]
        for l in range(1, L):
            g = jnp.where(pl.program_id(0) == l, g_refs[l][...], g)
        d, nm, nv = _adamw_math(w_ref[...], g, m_ref[...], v_ref[...])
        g_ref[...] = g
        d_ref[...] = d
        nm_ref[...] = nm
        nv_ref[...] = nv

    own = pl.BlockSpec((None, tr, C), lambda l, i: (l, i, 0))
    off = pl.BlockSpec((tr, C), lambda l, i: (b0 + i, 0))
    return pl.pallas_call(
        body, name=name, grid=(L, R // tr), in_specs=[own, own, own] + [off] * L, out_specs=[own] * 4,
        out_shape=[jax.ShapeDtypeStruct((L, R, C), f32)] * 4, compiler_params=_cp(("parallel", "parallel")),
    )(w, m, v, *gs)


def _adamw_math(w, g, m, v):
    m = B1 * m + (1.0 - B1) * g
    v = B2 * v + (1.0 - B2) * (g * g)
    m_hat = m / (1.0 - B1 ** STEP)
    v_hat = v / (1.0 - B2 ** STEP)
    delta = -LR * (m_hat / (jnp.sqrt(v_hat) + AEPS) + WD * w)
    return delta, m, v


def adamw_small(w, m, v, g):
    def body(w_ref, m_ref, v_ref, g_ref, d_ref, nm_ref, nv_ref):
        d, nm, nv = _adamw_math(w_ref[...], g_ref[...], m_ref[...], v_ref[...])
        d_ref[...] = d
        nm_ref[...] = nm
        nv_ref[...] = nv

    return pl.pallas_call(body, name="adamw_small", out_shape=[jax.ShapeDtypeStruct(w.shape, f32)] * 3)(w, m, v, g)


CONV =(("conv_qkv_b", 2), ("ffn_conv_w", 2))
SMALL = ("rel_bias", "norm_mix_g", "norm_mem_g", "sinks_a", "a_log_b", "dt_bias_b", "out_norm_g_b", "norm_ffn_g",
         "ffn_conv_b", "final_norm_g")
WEIGHTS = ("rel_bias", "norm_mix_g", "norm_mem_g", "w_mem_kv", "w_out", "w_in_a", "sinks_a", "w_in_b", "conv_qkv_b",
           "a_log_b", "dt_bias_b", "out_norm_g_b", "norm_ffn_g", "w_gate_up", "ffn_conv_w", "ffn_conv_b", "w_down",
           "final_norm_g")
ARGS = ("x", "mem") + WEIGHTS + ("loss_target",) + tuple("m_" + n for n in WEIGHTS) + tuple("v_" + n for n in WEIGHTS)


def _rows(a, width):
    flat = a.reshape(-1)
    pad = (-flat.shape[0]) % (8 * width)
    if pad:
        flat = jnp.concatenate([flat, jnp.zeros((pad,), a.dtype)])
    return flat.reshape(-1, width)


def _nrows(shape, width):
    return _pad_to(-(-math.prod(shape) // width), 8)


def _pack(arrs, width, total_rows, dtype):
    parts = [_rows(a.astype(dtype), width) for a in arrs]
    used = sum(p.shape[0] for p in parts)
    if total_rows > used:
        parts.append(jnp.zeros((total_rows - used, width), dtype))
    return jnp.concatenate(parts, axis=0)


def _unpack(buf, shapes, width):
    out, r = [], 0
    for s in shapes:
        n = _nrows(s, width)
        out.append(buf[r:r + n].reshape(-1)[:math.prod(s)].reshape(s))
        r += n
    return out


def _pad_to(n, mult):
    return -(-n // mult) * mult


def kernel(x, mem, rel_bias, norm_mix_g, norm_mem_g, w_mem_kv, w_out, w_in_a, sinks_a, w_in_b, conv_qkv_b, a_log_b, dt_bias_b, out_norm_g_b, norm_ffn_g, w_gate_up, ffn_conv_w, ffn_conv_b, w_down, final_norm_g, loss_target, m_rel_bias, m_norm_mix_g, m_norm_mem_g, m_w_mem_kv, m_w_out, m_w_in_a, m_sinks_a, m_w_in_b, m_conv_qkv_b, m_a_log_b, m_dt_bias_b, m_out_norm_g_b, m_norm_ffn_g, m_w_gate_up, m_ffn_conv_w, m_ffn_conv_b, m_w_down, m_final_norm_g, v_rel_bias, v_norm_mix_g, v_norm_mem_g, v_w_mem_kv, v_w_out, v_w_in_a, v_sinks_a, v_w_in_b, v_conv_qkv_b, v_a_log_b, v_dt_bias_b, v_out_norm_g_b, v_norm_ffn_g, v_w_gate_up, v_ffn_conv_w, v_ffn_conv_b, v_w_down, v_final_norm_g):
    A = dict(zip(ARGS, (x, mem, rel_bias, norm_mix_g, norm_mem_g, w_mem_kv, w_out, w_in_a, sinks_a, w_in_b, conv_qkv_b, a_log_b, dt_bias_b, out_norm_g_b, norm_ffn_g, w_gate_up, ffn_conv_w, ffn_conv_b, w_down, final_norm_g, loss_target, m_rel_bias, m_norm_mix_g, m_norm_mem_g, m_w_mem_kv, m_w_out, m_w_in_a, m_sinks_a, m_w_in_b, m_conv_qkv_b, m_a_log_b, m_dt_bias_b, m_out_norm_g_b, m_norm_ffn_g, m_w_gate_up, m_ffn_conv_w, m_ffn_conv_b, m_w_down, m_final_norm_g, v_rel_bias, v_norm_mix_g, v_norm_mem_g, v_w_mem_kv, v_w_out, v_w_in_a, v_sinks_a, v_w_in_b, v_conv_qkv_b, v_a_log_b, v_dt_bias_b, v_out_norm_g_b, v_norm_ffn_g, v_w_gate_up, v_ffn_conv_w, v_ffn_conv_b, v_w_down, v_final_norm_g)))
    chip = 2 * lax.axis_index("x") + lax.axis_index("y")
    core = lax.axis_index("c")

    def own_slot(shard):
        return lax.dynamic_update_index_in_dim(lax.empty((4,) + shard.shape, shard.dtype), shard, chip, 0)

    def bslot(w):
        return own_slot(w.astype(bf16))

    groups = {
        ("w_in", 0): [bslot(w_in_a[0])],
        ("w_mem", 0): [bslot(w_mem_kv[0]), bslot(w_mem_kv[1]), own_slot(conv_qkv_b[0]),
                       own_slot(ffn_conv_w.reshape(6, -1))],
        ("w_out", 0): [bslot(w_out[0])], ("w_gu", 0): [bslot(w_gate_up[0])], ("w_down", 0): [bslot(w_down[0])],
        ("w_in", 1): [bslot(w_in_b[0])],
        ("w_out", 1): [bslot(w_out[1])], ("w_gu", 1): [bslot(w_gate_up[1])], ("w_down", 1): [bslot(w_down[1])],
    }
    flights = dict(zip(groups, gather_start(list(groups.values()))))
    P = {"rel_bias": rel_bias, "sinks": sinks_a[0], "a_log": a_log_b[0], "dt_bias": dt_bias_b[0],
         "out_norm_g": out_norm_g_b[0], "g_mix": norm_mix_g, "g_mem": norm_mem_g, "g_ffn": norm_ffn_g,
         "g_fin": final_norm_g, "ffn_cb": [ffn_conv_b[0], ffn_conv_b[1]], "w_mem": [None, None], "w_out": [None, None],
         "w_gu": [None, None], "w_down": [None, None], "ffn_cw": [None, None]}

    def rows4(g):
        return g.reshape(4 * g.shape[1], g.shape[2])

    def arrive(key, after):
        if key not in flights:
            return
        got = gather_wait(flights.pop(key), after, "gather_wait_%s%d" % key)
        name, i = key
        if name == "w_in":
            P["w_in_a" if i == 0 else "w_in_b"] = (_lay_in_a if i == 0 else _lay_in_b)(_unchip_cols(got[0]))
        elif name == "w_mem":
            P["w_mem"] = [rows4(got[0]), rows4(got[1])]
            P["conv_qkv"] = _unchip_cols(got[2])
            cw = _unchip_cols(got[3]).reshape(2, 3, D_FF)
            P["ffn_cw"] = [cw[0], cw[1]]
        elif name == "w_out":
            P["w_out"][i] = _lay_out_a(rows4(got[0])) if i == 0 else rows4(got[0])
        elif name == "w_gu":
            P["w_gu"][i] = got[0]
        else:
            P["w_down"][i] = rows4(got[0])

    def chip_rows(g):
        return g.reshape(4, g.shape[0] // 4, g.shape[-1])

    sent, started = {}, []

    def ready(key, G, dep):
        kind, i = key
        tag = "%s%d" % key
        if kind == "ffn":
            names, partial = ("gu", "down"), [G["w_gu"][i], chip_rows(G["w_down"][i]).astype(bf16)]
        else:
            g_out = _unlay_out_a(G["w_out"][0]) if i == 0 else G["w_out"][1]
            g_in = _unlay_in_a(G["w_in_a"]) if i == 0 else _unlay_in_b(G["w_in_b"])
            names = ("out", "in", "mem")
            partial = [chip_rows(g_out).astype(bf16), _chip_cols(g_in).astype(bf16), chip_rows(G["w_mem"][i]).astype(bf16)]
        theirs = pair_exchange(partial, "pair_exchange_" + tag)
        pair = [pair_sum(p, t, core, "pair_sum_%s%d" % (nm, i)) for p, t, nm in zip(partial, theirs, names)]
        flight, token = scatter_start(pair, "scatter_start_" + tag)
        sent[key] = (names, flight, token)
        started.append(token[0, 0])
        if dep is not None:
            while started:
                dep = dep + started.pop()
        return dep

    P["arrive"], P["ready"] = arrive, ready

    loss, dx, G = _local_step(x[0], mem[0], loss_target[0], P)
    gfull = _grads_to_ref(G)

    fin, after = {}, sent["mix", 0][2]
    for key in (("ffn", 1), ("mix", 1), ("ffn", 0), ("mix", 0)):
        names, flight, _ = sent[key]
        pair, arrived = scatter_wait(flight, after, "scatter_wait_%s%d" % key)
        for nm, p, r in zip(names, pair, arrived):
            after = fin[nm, key[1]] = sum_slots(p, r, chip, core, "sum_slots_%s%d" % (nm, key[1]))
    order = list(fin)
    done = dict(zip(order, final_exchange([fin[k] for k in order])))

    sm_shapes = [A[n].shape for n in SMALL] + [gfull[n].shape for n, _ in CONV] + [(LANE,)]
    sm_rows = _pad_to(sum(_nrows(s, LANE) for s in sm_shapes), 8)
    sbuf = _pack([gfull[n] for n in SMALL] + [gfull[n] for n, _ in CONV] + [loss[0]], LANE, sm_rows, f32)
    tot = _unpack(allreduce_small(sbuf), sm_shapes, LANE)
    gsmall = dict(zip(SMALL, tot[:len(SMALL)]))
    for (n, axis), t in zip(CONV, tot[len(SMALL):len(SMALL) + len(CONV)]):
        sh = A[n].shape[axis]
        gsmall[n] = lax.dynamic_slice_in_dim(t, chip * sh, sh, axis)
    loss_out = tot[-1][0]

    out = {}
    plan = (("w_gate_up", [done["gu", 0], done["gu", 1]]), ("w_down", [done["down", 0], done["down", 1]]),
            ("w_out", [done["out", 0], done["out", 1]]), ("w_mem_kv", [done["mem", 0], done["mem", 1]]),
            ("w_in_a", [done["in", 0]]), ("w_in_b", [done["in", 1]]))
    for n, gs in plan:
        shape3 = (len(gs),) + gs[0].shape
        res = adamw_big(A[n].reshape(shape3), A["m_" + n].reshape(shape3), A["v_" + n].reshape(shape3), gs, 0,
                        "adamw_" + n)
        for key, r in zip(("grad_", "delta_", "new_m_", "new_v_"), res):
            out[key + n] = r.reshape(A[n].shape)
    names = SMALL + tuple(n for n, _ in CONV)
    shapes = [A[n].shape for n in names]
    rows = _pad_to(sum(_nrows(s, LANE) for s in shapes), 8)
    packs = [_pack([src[n] for n in names], LANE, rows, f32)
             for src in ({n: A[n] for n in names}, {n: A["m_" + n] for n in names}, {n: A["v_" + n] for n in names}, gsmall)]
    res = adamw_small(*packs)
    for key, r in zip(("delta_", "new_m_", "new_v_"), res):
        for n, a in zip(names, _unpack(r, shapes, LANE)):
            out[key + n] = a
    for n in names:
        out["grad_" + n] = gsmall[n]
    return (loss_out, dx[None], *[out["grad_" + n] for n in WEIGHTS], *[out["delta_" + n] for n in WEIGHTS],
            *[out["new_m_" + n] for n in WEIGHTS], *[out["new_v_" + n] for n in WEIGHTS])
```

```python
import functools
import math

import numpy as np
import jax
import jax.numpy as jnp
from jax import lax
from jax.experimental import pallas as pl
from jax.experimental.pallas import tpu as pltpu

f32 = jnp.float32
bf16 = jnp.bfloat16
HI = lax.Precision.HIGHEST
MESH = pl.DeviceIdType.MESH

D = 1024
MEM_LEN = 256
EPS = 1e-6
A_HEADS, A_KV, A_DH = 12, 2, 64
A_Q = 768
BLK = 128
N_BUCKETS, MAX_DIST = 32, 128
B_QK, B_V, B_DH = 384, 768, 128
B_QKV = 1536
CHUNK = 64
X_Q = 256
D_FF = 2816
LANE = 128
VMEM_LIMIT = 56 * 1024 * 1024
MM_ROWS = 1024

LR, B1, B2, AEPS, WD, STEP = 0.001, 0.9, 0.999, 1e-08, 0.01, 10


def _cp(sem=None):
    return pltpu.CompilerParams(dimension_semantics=sem, vmem_limit_bytes=VMEM_LIMIT)


def _dg(a, b, ca, cb, prec=None):
    return lax.dot_general(a, b, (((ca,), (cb,)), ((), ())), precision=prec, preferred_element_type=f32)


@jax.custom_vjp
def bdot(a, b):
    return _dg(a.astype(bf16), b.astype(bf16), 1, 0)


def _bdot_f(a, b):
    return bdot(a, b), (a, b)


def _bdot_b(res, g):
    a, b = res
    gb = g.astype(bf16)
    return _dg(gb, b.astype(bf16), 1, 1), _dg(a.astype(bf16), gb, 0, 0)


bdot.defvjp(_bdot_f, _bdot_b)


@jax.custom_vjp
def bdot_nt(a, b):
    return _dg(a.astype(bf16), b.astype(bf16), 1, 1)


def _bdot_nt_f(a, b):
    return bdot_nt(a, b), (a, b)


def _bdot_nt_b(res, g):
    a, b = res
    gb = g.astype(bf16)
    return _dg(gb, b.astype(bf16), 1, 0), _dg(gb, a.astype(bf16), 0, 0)


bdot_nt.defvjp(_bdot_nt_f, _bdot_nt_b)


def _shift_rows(x, s, down):
    n = x.shape[0]
    row = lax.broadcasted_iota(jnp.int32, x.shape, 0)
    if down:
        return jnp.where(row >= s, pltpu.roll(x, s, 0), 0.0)
    return jnp.where(row < n - s, pltpu.roll(x, n - s, 0), 0.0)


@functools.partial(jax.custom_vjp, nondiff_argnums=(1,))
def shift_down(x, s):
    return _shift_rows(x, s, True)


def _sd_f(x, s):
    return _shift_rows(x, s, True), None


def _sd_b(s, _, g):
    return (_shift_rows(g, s, False),)


shift_down.defvjp(_sd_f, _sd_b)


def _sigmoid(x):
    return 1.0 / (1.0 + jnp.exp(-x))


def _silu(x):
    return x * _sigmoid(x)


def _rms(x, g):
    return x * lax.rsqrt(jnp.mean(x * x, axis=-1, keepdims=True) + EPS) * g


def _tile(n, cap):
    u = n // LANE
    best = 1
    for d in range(1, u + 1):
        if u % d == 0 and d * LANE <= cap:
            best = d
    return best * LANE


def mm_nn(a, w, res=None, out_dtype=f32, name="mm_nn"):
    M, K = a.shape
    N = w.shape[1]
    tm, tn = min(MM_ROWS, M), _tile(N, 1024)

    def body(*refs):
        if res is None:
            a_ref, w_ref, o_ref = refs
            o_ref[...] = _dg(a_ref[...].astype(bf16), w_ref[...], 1, 0).astype(out_dtype)
        else:
            a_ref, w_ref, r_ref, o_ref = refs
            o_ref[...] = (r_ref[...] + _dg(a_ref[...].astype(bf16), w_ref[...], 1, 0)).astype(out_dtype)

    in_specs = [pl.BlockSpec((tm, K), lambda n, m: (m, 0)), pl.BlockSpec((K, tn), lambda n, m: (0, n))]
    args = [a, w]
    if res is not None:
        in_specs.append(pl.BlockSpec((tm, tn), lambda n, m: (m, n)))
        args.append(res)
    return pl.pallas_call(
        body, name=name, grid=(N // tn, M // tm), in_specs=in_specs,
        out_specs=pl.BlockSpec((tm, tn), lambda n, m: (m, n)),
        out_shape=jax.ShapeDtypeStruct((M, N), out_dtype),
        compiler_params=_cp(("parallel", "parallel")),
    )(*args)


def mm_res_norm(a, w, res, g, name):
    pieces = a if isinstance(a, tuple) else (a,)
    na = len(pieces)
    M, K = pieces[0].shape[0], sum(p.shape[1] for p in pieces)
    tm = min(MM_ROWS, M)

    def body(*refs):
        w_ref, r_ref, g_ref, o_ref, n_ref = refs[na:]
        h = r_ref[...] + _dg(_cols(refs[:na]).astype(bf16), w_ref[...], 1, 0)
        o_ref[...] = h
        n_ref[...] = _rms(h, g_ref[...]).astype(bf16)

    tok = pl.BlockSpec((tm, D), lambda m: (m, 0))
    return pl.pallas_call(
        body, name=name, grid=(M // tm,),
        in_specs=[pl.BlockSpec((tm, p.shape[1]), lambda m: (m, 0)) for p in pieces]
        + [pl.BlockSpec((K, D), lambda m: (0, 0)), tok, pl.BlockSpec((1, D), lambda m: (0, 0))],
        out_specs=[tok, tok],
        out_shape=[jax.ShapeDtypeStruct((M, D), f32), jax.ShapeDtypeStruct((M, D), bf16)],
        compiler_params=_cp(("parallel",)),
    )(*pieces, w, res, g.reshape(1, D))


def mm_nt(dy, w, out_dtype=f32, name="mm_nt"):
    M, N = dy.shape
    K = w.shape[0]
    tm, tn = min(MM_ROWS, M), _tile(N, 1024)
    assert out_dtype == f32 or tn == N

    def body(dy_ref, w_ref, o_ref):
        part = _dg(dy_ref[...].astype(bf16), w_ref[...], 1, 1)
        if tn == N:
            o_ref[...] = part.astype(out_dtype)
        else:
            @pl.when(pl.program_id(1) == 0)
            def _():
                o_ref[...] = jnp.zeros_like(o_ref)
            o_ref[...] += part

    return pl.pallas_call(
        body, name=name, grid=(M // tm, N // tn),
        in_specs=[pl.BlockSpec((tm, tn), lambda m, n: (m, n)), pl.BlockSpec((K, tn), lambda m, n: (0, n))],
        out_specs=pl.BlockSpec((tm, K), lambda m, n: (m, 0)),
        out_shape=jax.ShapeDtypeStruct((M, K), out_dtype),
        compiler_params=_cp(("parallel", "arbitrary")),
    )(dy, w)


NORM_ROWS = 1024


def _acc_then_norm_bwd(part, steps, h_ref, g_ref, r_ref, o_ref, dg_ref):
    k = pl.program_id(1)

    @pl.when((pl.program_id(0) == 0) & (k == 0))
    def _():
        dg_ref[...] = jnp.zeros_like(dg_ref)

    @pl.when(k == 0)
    def _():
        o_ref[...] = part

    @pl.when(k > 0)
    def _():
        o_ref[...] += part

    @pl.when(k == steps - 1)
    def _():
        _, vjp = jax.vjp(_rms, h_ref[...], g_ref[...])
        dh, dg = vjp(o_ref[...])
        o_ref[...] = r_ref[...] + dh
        dg_ref[...] += dg


def mm_nt_norm(dy, w, norm, name):
    pieces = dy if isinstance(dy, tuple) else (dy,)
    nd = len(pieces)
    M, N = pieces[0].shape[0], sum(p.shape[1] for p in pieces)
    tm, tn = (min(NORM_ROWS, M), _tile(N, 1024)) if nd == 1 else (min(512, M), N)

    def body(*refs):
        w_ref, h_ref, g_ref, r_ref, o_ref, dg_ref = refs[nd:]
        _acc_then_norm_bwd(_dg(_cols(refs[:nd]).astype(bf16), w_ref[...], 1, 1), N // tn, h_ref, g_ref, r_ref, o_ref,
                           dg_ref)

    tok = pl.BlockSpec((tm, D), lambda m, n: (m, 0))
    vec = pl.BlockSpec((1, D), lambda m, n: (0, 0))
    return pl.pallas_call(
        body, name=name, grid=(M // tm, N // tn),
        in_specs=[pl.BlockSpec((tm, tn if nd == 1 else p.shape[1]), lambda m, n: (m, n)) for p in pieces]
        + [pl.BlockSpec((D, tn), lambda m, n: (0, n)), tok, vec, tok],
        out_specs=[tok, vec],
        out_shape=[jax.ShapeDtypeStruct((M, D), f32), jax.ShapeDtypeStruct((1, D), f32)],
        compiler_params=_cp(("arbitrary", "arbitrary")),
    )(*pieces, w, norm[0], norm[1].reshape(1, D), norm[2])


def _cols(refs):
    return refs[0][...] if len(refs) == 1 else jnp.concatenate([r[...] for r in refs], axis=1)


def mm_tn(a, dy, name="mm_tn"):
    pieces = a if isinstance(a, tuple) else (a,)
    dpieces = dy if isinstance(dy, tuple) else (dy,)
    na, nd = len(pieces), len(dpieces)
    M, K = pieces[0].shape[0], sum(p.shape[1] for p in pieces)
    N = sum(p.shape[1] for p in dpieces)
    tm = min(MM_ROWS, M)
    tk = _tile(K, 1408) if na == 1 else K
    tn = _tile(N, 1024) if nd == 1 else N

    def body(*refs):
        o_ref = refs[-1]

        @pl.when(pl.program_id(2) == 0)
        def _():
            o_ref[...] = jnp.zeros_like(o_ref)
        o_ref[...] += _dg(_cols(refs[:na]).astype(bf16), _cols(refs[na:na + nd]).astype(bf16), 0, 0)

    a_specs = [pl.BlockSpec((tm, tk if na == 1 else p.shape[1]), lambda k, n, m: (m, k)) for p in pieces]
    d_specs = [pl.BlockSpec((tm, tn if nd == 1 else p.shape[1]), lambda k, n, m: (m, n)) for p in dpieces]
    return pl.pallas_call(
        body, name=name, grid=(K // tk, N // tn, M // tm), in_specs=a_specs + d_specs,
        out_specs=pl.BlockSpec((tk, tn), lambda k, n, m: (k, n)),
        out_shape=jax.ShapeDtypeStruct((K, N), f32),
        compiler_params=_cp(("parallel", "parallel", "arbitrary")),
    )(*pieces, *dpieces)


def rms_fwd(h, g, name):
    S = h.shape[0]
    t = min(512, S)

    def body(h_ref, g_ref, o_ref):
        o_ref[...] = _rms(h_ref[...], g_ref[...]).astype(bf16)

    return pl.pallas_call(
        body, name=name, grid=(S // t,),
        in_specs=[pl.BlockSpec((t, D), lambda i: (i, 0)), pl.BlockSpec((1, D), lambda i: (0, 0))],
        out_specs=pl.BlockSpec((t, D), lambda i: (i, 0)),
        out_shape=jax.ShapeDtypeStruct((S, D), bf16),
        compiler_params=_cp(("parallel",)),
    )(h, g.reshape(1, D))


def loss_head(h, g, target):
    S = h.shape[0]
    t = min(512, S)

    def f(hh, gg, tt):
        err = _rms(hh, gg) - tt
        return 0.5 * jnp.sum(jnp.mean(err * err, axis=-1, keepdims=True), axis=0, keepdims=True)

    def body(h_ref, g_ref, t_ref, loss_ref, dh_ref, dg_ref):
        @pl.when(pl.program_id(0) == 0)
        def _():
            dg_ref[...] = jnp.zeros_like(dg_ref)
            loss_ref[...] = jnp.zeros_like(loss_ref)
        val, vjp = jax.vjp(lambda a, b: f(a, b, t_ref[...]), h_ref[...], g_ref[...])
        dh, dg = vjp(jnp.ones((1, 1), f32))
        dh_ref[...] = dh
        dg_ref[...] += dg
        loss_ref[...] += jnp.broadcast_to(val, loss_ref.shape)

    tok = pl.BlockSpec((t, D), lambda i: (i, 0))
    vec = pl.BlockSpec((1, D), lambda i: (0, 0))
    return pl.pallas_call(
        body, name="loss_head", grid=(S // t,), in_specs=[tok, vec, tok],
        out_specs=[pl.BlockSpec((1, LANE), lambda i: (0, 0)), tok, vec],
        out_shape=[jax.ShapeDtypeStruct((1, LANE), f32), jax.ShapeDtypeStruct((S, D), f32),
                   jax.ShapeDtypeStruct((1, D), f32)],
        compiler_params=_cp(("arbitrary",)),
    )(h, g.reshape(1, D), target)


def memkv_fwd(mem, g, w, name):
    def body(m_ref, g_ref, w_ref, o_ref):
        o_ref[...] = _dg(_rms(m_ref[...], g_ref[...]).astype(bf16), w_ref[...], 1, 0)

    return pl.pallas_call(
        body, name=name, out_shape=jax.ShapeDtypeStruct((MEM_LEN, 2 * X_Q), f32), compiler_params=_cp(),
    )(mem, g.reshape(1, D), w)


def memkv_bwd(mem, g, w, dkv, name):
    def body(m_ref, g_ref, w_ref, d_ref, dg_ref, dw_ref):
        n, vjp = jax.vjp(lambda gg: _rms(m_ref[...], gg), g_ref[...])
        db = d_ref[...].astype(bf16)
        dw_ref[...] = _dg(n.astype(bf16), db, 0, 0)
        dg_ref[...] = vjp(_dg(db, w_ref[...], 1, 1))[0]

    return pl.pallas_call(
        body, name=name,
        out_shape=[jax.ShapeDtypeStruct((1, D), f32), jax.ShapeDtypeStruct((D, 2 * X_Q), f32)],
        compiler_params=_cp(),
    )(mem, g.reshape(1, D), w, dkv)


def _xattn_f(xq, mk, mv):
    lane = lax.broadcasted_iota(jnp.int32, (1, X_Q), 1)
    out = jnp.zeros(xq.shape, f32)
    for hd in range(4):
        msk = (lane // 64 == hd).astype(f32)
        s = bdot_nt(xq * msk, mk) * (64 ** -0.5)
        m = lax.stop_gradient(jnp.max(s, axis=-1, keepdims=True))
        p = jnp.exp(s - m)
        p = p / jnp.sum(p, axis=-1, keepdims=True)
        out = out + bdot(p, mv * msk)
    return out


def xattn_fwd(proj, col, kv, name):
    S = proj.shape[0]
    t = min(512, S)
    cb = col // X_Q

    def body(q_ref, k_ref, v_ref, o_ref):
        o_ref[...] = _xattn_f(q_ref[...], k_ref[...], v_ref[...]).astype(bf16)

    return pl.pallas_call(
        body, name=name, grid=(S // t,),
        in_specs=[pl.BlockSpec((t, X_Q), lambda i: (i, cb)), pl.BlockSpec((MEM_LEN, X_Q), lambda i: (0, 0)),
                  pl.BlockSpec((MEM_LEN, X_Q), lambda i: (0, 1))],
        out_specs=pl.BlockSpec((t, X_Q), lambda i: (i, 0)),
        out_shape=jax.ShapeDtypeStruct((S, X_Q), bf16),
        compiler_params=_cp(("parallel",)),
    )(proj, kv, kv)


def xattn_bwd(proj, col, kv, dmix, name):
    S = proj.shape[0]
    t = min(512, S)
    cb = col // X_Q

    def body(q_ref, k_ref, v_ref, do_ref, dq_ref, dk_ref, dv_ref):
        @pl.when(pl.program_id(0) == 0)
        def _():
            dk_ref[...] = jnp.zeros_like(dk_ref)
            dv_ref[...] = jnp.zeros_like(dv_ref)
        _, vjp = jax.vjp(_xattn_f, q_ref[...], k_ref[...], v_ref[...])
        dq, dk, dv = vjp(do_ref[...])
        dq_ref[...] = dq.astype(bf16)
        dk_ref[...] += dk
        dv_ref[...] += dv

    kvb = pl.BlockSpec((MEM_LEN, X_Q), lambda i: (0, 0))
    dq, dk, dv = pl.pallas_call(
        body, name=name, grid=(S // t,),
        in_specs=[pl.BlockSpec((t, X_Q), lambda i: (i, cb)), kvb,
                  pl.BlockSpec((MEM_LEN, X_Q), lambda i: (0, 1)), pl.BlockSpec((t, X_Q), lambda i: (i, 3))],
        out_specs=[pl.BlockSpec((t, X_Q), lambda i: (i, 0)), kvb, kvb],
        out_shape=[jax.ShapeDtypeStruct((S, X_Q), bf16), jax.ShapeDtypeStruct((MEM_LEN, X_Q), f32),
                   jax.ShapeDtypeStruct((MEM_LEN, X_Q), f32)],
        compiler_params=_cp(("arbitrary",)),
    )(proj, kv, kv, dmix)
    return dq, jnp.concatenate([dk, dv], axis=1)


def _bucket_map():
    qi = np.arange(BLK)[:, None]
    kj = np.arange(2 * BLK)[None, :]
    n = np.maximum(BLK + qi - kj, 0)
    max_exact = N_BUCKETS // 2
    nf = np.maximum(n, 1).astype(np.float64)
    large = max_exact + (np.log(nf / max_exact) / math.log(MAX_DIST / max_exact)
                         * (N_BUCKETS - max_exact)).astype(np.int32)
    large = np.minimum(large, N_BUCKETS - 1)
    return np.where(n < max_exact, n, large).astype(np.int32)


def bias_build(rel_bias):
    def body(rb_ref, bk_ref, o_ref):
        bk = bk_ref[...]
        for h in range(A_HEADS):
            acc = jnp.zeros((BLK, 2 * BLK), f32)
            for b in range(N_BUCKETS):
                acc = jnp.where(bk == b, rb_ref[b, h], acc)
            o_ref[h] = acc

    return pl.pallas_call(
        body, name="bias_build",
        in_specs=[pl.BlockSpec(memory_space=pltpu.SMEM), pl.BlockSpec(memory_space=pltpu.VMEM)],
        out_specs=pl.BlockSpec(memory_space=pltpu.VMEM),
        out_shape=jax.ShapeDtypeStruct((A_HEADS, BLK, 2 * BLK), f32), compiler_params=_cp(),
    )(rel_bias, jnp.asarray(_bucket_map()))


def bias_grad(dbias):
    def body(d_ref, bk_ref, o_ref):
        bk = bk_ref[...]
        row = lax.broadcasted_iota(jnp.int32, (N_BUCKETS, LANE), 0)
        lane = lax.broadcasted_iota(jnp.int32, (N_BUCKETS, LANE), 1)
        acc = jnp.zeros((N_BUCKETS, LANE), f32)
        for h in range(A_HEADS):
            d = d_ref[h]
            for b in range(N_BUCKETS):
                s = jnp.sum(jnp.where(bk == b, d, 0.0), keepdims=True)
                acc = acc + jnp.where((row == b) & (lane == h), s, 0.0)
        o_ref[...] = acc

    return pl.pallas_call(
        body, name="bias_grad", out_shape=jax.ShapeDtypeStruct((N_BUCKETS, LANE), f32), compiler_params=_cp(),
    )(dbias, jnp.asarray(_bucket_map()))


def _swa_f(qb, kp, kc, vp, vc, bias, sk, first):
    kband = jnp.concatenate([kp, kc], axis=0)
    vband = jnp.concatenate([vp, vc], axis=0)
    qi = lax.broadcasted_iota(jnp.int32, (BLK, 2 * BLK), 0)
    kj = lax.broadcasted_iota(jnp.int32, (BLK, 2 * BLK), 1)
    rel = kj - qi
    ok = (rel >= 1) & (rel <= BLK) & ((kj >= BLK) | jnp.logical_not(first))
    lane = lax.broadcasted_iota(jnp.int32, (1, LANE), 1)
    lane_b = lax.broadcasted_iota(jnp.int32, (BLK, LANE), 1)
    outs = []
    for p in range(A_HEADS // 2):
        qp = qb[:, LANE * p:LANE * (p + 1)]
        acc = jnp.zeros((BLK, LANE), f32)
        for g in range(2):
            h = g * (A_HEADS // 2) + p
            msk = (lane // A_DH == g).astype(f32)
            s = bdot_nt(qp * msk, kband) * (A_DH ** -0.5) + bias[h]
            s = jnp.where(ok, s, -1e30)
            skb = jnp.broadcast_to(sk[h:h + 1, :], (BLK, LANE))
            sink = jnp.sum(jnp.where(lane_b == 0, skb, 0.0), axis=-1, keepdims=True)
            m = lax.stop_gradient(jnp.maximum(jnp.max(s, axis=-1, keepdims=True), sink))
            e = jnp.exp(s - m)
            prob = e / (jnp.sum(e, axis=-1, keepdims=True) + jnp.exp(sink - m))
            acc = acc + bdot(prob, vband) * msk
        outs.append(acc)
    return jnp.concatenate(outs, axis=1)


def _swa_specs(nb, rev):
    bi = (lambda i: nb - 1 - i) if rev else (lambda i: i)
    return [
        pl.BlockSpec((BLK, A_Q), lambda i: (bi(i), 0)),
        pl.BlockSpec((BLK, LANE), lambda i: (jnp.maximum(bi(i) - 1, 0), 6)),
        pl.BlockSpec((BLK, LANE), lambda i: (bi(i), 6)),
        pl.BlockSpec((BLK, LANE), lambda i: (jnp.maximum(bi(i) - 1, 0), 7)),
        pl.BlockSpec((BLK, LANE), lambda i: (bi(i), 7)),
        pl.BlockSpec((A_HEADS, BLK, 2 * BLK), lambda i: (0, 0, 0)),
        pl.BlockSpec((16, LANE), lambda i: (0, 0)),
    ]


def swa_fwd(proj, bias, sk):
    S = proj.shape[0]
    nb = S // BLK

    def body(q_ref, kp_ref, kc_ref, vp_ref, vc_ref, b_ref, s_ref, o_ref):
        o_ref[...] = _swa_f(q_ref[...], kp_ref[...], kc_ref[...], vp_ref[...], vc_ref[...], b_ref[...], s_ref[...],
                            pl.program_id(0) == 0).astype(bf16)

    return pl.pallas_call(
        body, name="swa_fwd", grid=(nb,), in_specs=_swa_specs(nb, False),
        out_specs=pl.BlockSpec((BLK, A_Q), lambda i: (i, 0)),
        out_shape=jax.ShapeDtypeStruct((S, A_Q), bf16), compiler_params=_cp(("parallel",)),
    )(proj, proj, proj, proj, proj, bias, sk)


def swa_bwd(proj, bias, sk, dmix):
    S = proj.shape[0]
    nb = S // BLK

    def body(q_ref, kp_ref, kc_ref, vp_ref, vc_ref, b_ref, s_ref, do_ref, dqkv_ref, db_ref, ds_ref, ck, cv):
        i = pl.program_id(0)

        @pl.when(i == 0)
        def _():
            db_ref[...] = jnp.zeros_like(db_ref)
            ds_ref[...] = jnp.zeros_like(ds_ref)
            ck[...] = jnp.zeros_like(ck)
            cv[...] = jnp.zeros_like(cv)
        first = i == nb - 1
        _, vjp = jax.vjp(lambda *a: _swa_f(*a, first), q_ref[...], kp_ref[...], kc_ref[...], vp_ref[...],
                         vc_ref[...], b_ref[...], s_ref[...])
        dq, dkp, dkc, dvp, dvc, db, ds = vjp(do_ref[...])
        dqkv_ref[...] = jnp.concatenate([dq, dkc + ck[...], dvc + cv[...]], axis=1).astype(bf16)
        ck[...] = dkp
        cv[...] = dvp
        db_ref[...] += db
        ds_ref[...] += ds

    return pl.pallas_call(
        body, name="swa_bwd", grid=(nb,),
        in_specs=_swa_specs(nb, True) + [pl.BlockSpec((BLK, A_Q), lambda i: (nb - 1 - i, 0))],
        out_specs=[pl.BlockSpec((BLK, D), lambda i: (nb - 1 - i, 0)),
                   pl.BlockSpec((A_HEADS, BLK, 2 * BLK), lambda i: (0, 0, 0)),
                   pl.BlockSpec((16, LANE), lambda i: (0, 0))],
        out_shape=[jax.ShapeDtypeStruct((S, D), bf16), jax.ShapeDtypeStruct((A_HEADS, BLK, 2 * BLK), f32),
                   jax.ShapeDtypeStruct((16, LANE), f32)],
        scratch_shapes=[pltpu.VMEM((BLK, LANE), f32), pltpu.VMEM((BLK, LANE), f32)],
        compiler_params=_cp(("arbitrary",)),
    )(proj, proj, proj, proj, proj, bias, sk, dmix)


def _dnprep_f(xext, w, is_qk):
    c = (w[3:4] * xext + w[2:3] * shift_down(xext, 1) + w[1:2] * shift_down(xext, 2) + w[0:1] * shift_down(xext, 3))
    a = _silu(c)[HALO:]
    n = a * lax.rsqrt(jnp.sum(a * a, axis=-1, keepdims=True) + EPS)
    return jnp.where(is_qk, n, a)


def dnprep_fwd(proj, cw):
    S = proj.shape[0]
    nblk = B_QKV // LANE
    T = S

    def body(x_ref, w_ref, o_ref):
        is_qk = pl.program_id(0) < 2 * B_QK // LANE
        wv = w_ref[...]

        def tile(r0, first):
            o_ref[pl.ds(r0, T), :] = _dnprep_f(_glu_gext(x_ref, r0, first, T), wv, is_qk)

        tile(0, True)

    return pl.pallas_call(
        body, name="dnprep_fwd", grid=(nblk,),
        in_specs=[pl.BlockSpec((S, LANE), lambda j: (0, j)), pl.BlockSpec((4, LANE), lambda j: (0, j))],
        out_specs=pl.BlockSpec((S, LANE), lambda j: (0, j)),
        out_shape=jax.ShapeDtypeStruct((S, B_QKV), f32), compiler_params=_cp(("parallel",)),
    )(proj, cw)


def dnprep_bwd(proj, cw, dqkvn):
    S = proj.shape[0]
    nblk = B_QKV // LANE

    T = S

    def body(x_ref, w_ref, d_ref, dx_ref, dw_ref):
        is_qk = pl.program_id(0) < 2 * B_QK // LANE
        wv = w_ref[...]

        def tile(r0, first):
            _, vjp = jax.vjp(lambda a, b: _dnprep_f(a, b, is_qk), _glu_gext(x_ref, r0, first, T), wv)
            dx, dw = vjp(d_ref[pl.ds(r0, T), :])
            dx_ref[pl.ds(r0, T), :] = dx[HALO:].astype(bf16)
            if not first:
                dx_ref[pl.ds(r0 - HALO, HALO), :] += dx[:HALO]
            return dw

        dw_ref[...] = tile(0, True)

    col = pl.BlockSpec((S, LANE), lambda j: (0, j))
    wsp = pl.BlockSpec((4, LANE), lambda j: (0, j))
    return pl.pallas_call(
        body, name="dnprep_bwd", grid=(nblk,), in_specs=[col, wsp, col], out_specs=[col, wsp],
        out_shape=[jax.ShapeDtypeStruct((S, B_QKV), bf16), jax.ShapeDtypeStruct((4, B_QKV), f32)],
        compiler_params=_cp(("parallel",)),
    )(proj, cw, dqkvn)


def _hdot(a, b, ca=1, cb=0):
    return _dg(a, b, ca, cb, HI)


def _bdg(a, b, ca, cb):
    dn = (((ca,), (cb,)), ((0,), (0,)))
    ah, bh = a.astype(bf16), b.astype(bf16)
    al, bl = (a - ah.astype(f32)).astype(bf16), (b - bh.astype(f32)).astype(bf16)
    return (lax.dot_general(ah, bh, dn, preferred_element_type=f32)
            + lax.dot_general(ah, bl, dn, preferred_element_type=f32)
            + lax.dot_general(al, bh, dn, preferred_element_type=f32))


@jax.custom_vjp
def hbd(a, b):
    return _bdg(a, b, 2, 1)


@jax.custom_vjp
def hbd_nt(a, b):
    return _bdg(a, b, 2, 2)


@jax.custom_vjp
def hbd_tn(a, b):
    return _bdg(a, b, 1, 1)


hbd.defvjp(lambda a, b: (hbd(a, b), (a, b)), lambda r, g: (hbd_nt(g, r[1]), hbd_tn(r[0], g)))
hbd_nt.defvjp(lambda a, b: (hbd_nt(a, b), (a, b)), lambda r, g: (hbd(g, r[1]), hbd_tn(g, r[0])))
hbd_tn.defvjp(lambda a, b: (hbd_tn(a, b), (a, b)), lambda r, g: (hbd_nt(r[1], g), hbd(r[0], g)))


def _stack(xs):
    return jnp.concatenate([x[None] for x in xs], axis=0)


def _lane_col(x, j):
    lane = lax.broadcasted_iota(jnp.int32, (1, LANE), 1)
    return jnp.sum(jnp.where(lane == j, x, 0.0), axis=-1, keepdims=True)


def _tri_inv(a_mat):
    r = lax.broadcasted_iota(jnp.int32, (1, CHUNK, CHUNK), 1)
    c = lax.broadcasted_iota(jnp.int32, (1, CHUNK, CHUNK), 2)
    pw = -a_mat
    inv = (r == c).astype(f32) + pw
    for _ in range(5):
        pw = hbd(pw, pw)
        inv = inv + hbd(inv, pw)
    return inv


@jax.custom_vjp
def _tri_inv_known(a_mat, inv):
    return inv


_tri_inv_known.defvjp(lambda a, inv: (inv, inv),
                      lambda inv, g: (-hbd_tn(inv, hbd_nt(g, inv)), jnp.zeros_like(inv)))


def _dnc_f(q, k, v, seg, prm, inverse=_tri_inv):
    C = CHUNK
    B = q.shape[0]
    rows = seg.shape[0]
    beta_all = _sigmoid(seg)
    xx = seg + prm[1:2]
    g_all = -jnp.exp(prm[0:1]) * (jnp.maximum(xx, 0.0) + jnp.log(1.0 + jnp.exp(-jnp.abs(xx))))
    r2 = lax.broadcasted_iota(jnp.int32, (rows, rows), 0)
    c2 = lax.broadcasted_iota(jnp.int32, (rows, rows), 1)
    within = (r2 >= c2) & (r2 // C == c2 // C)
    gc_all = _hdot(within.astype(f32), g_all)
    beta = _stack([_lane_col(beta_all[C * j:C * (j + 1)], h) for j in range(rows // C) for h in range(6)])
    gc = _stack([_lane_col(gc_all[C * j:C * (j + 1)], 6 + h) for j in range(rows // C) for h in range(6)])
    r = lax.broadcasted_iota(jnp.int32, (1, C, C), 1)
    c = lax.broadcasted_iota(jnp.int32, (1, C, C), 2)
    incl = r >= c
    strict = r > c
    gct = [gc_all[C * j:C * (j + 1)].T for j in range(rows // C)]
    g_row = _stack([jnp.broadcast_to(gct[j][6 + h:7 + h, :], (C, C))
                    for j in range(rows // C) for h in range(6)])
    decay = jnp.where(incl, jnp.exp(jnp.where(incl, gc - g_row, 0.0)), 0.0)
    a_mat = beta * sbd_nt(k, k) * jnp.where(strict, decay, 0.0)
    eg = jnp.exp(gc)
    inv = inverse(a_mat)
    u = hbd(inv, beta * v)
    w = hbd(inv, (beta * eg) * k)
    qc = q * (B_DH ** -0.5)
    attn = sbd_nt(qc, k) * decay
    last = (lax.broadcasted_iota(jnp.int32, (1, C, 1), 1) == C - 1).astype(f32)
    g_last = jnp.sum(gc * last, axis=1, keepdims=True)
    dc = jnp.broadcast_to(jnp.exp(g_last), (B, 1, LANE)).reshape(B, LANE)
    return u, w, qc * eg, k * jnp.exp(g_last - gc), attn, dc, inv


def _b1(a, b, ca, cb):
    return lax.dot_general(a.astype(bf16), b.astype(bf16), (((ca,), (cb,)), ((0,), (0,))), preferred_element_type=f32)


@jax.custom_vjp
def sbd(a, b):
    return _b1(a, b, 2, 1)


@jax.custom_vjp
def sbd_nt(a, b):
    return _b1(a, b, 2, 2)


@jax.custom_vjp
def sbd_tn(a, b):
    return _b1(a, b, 1, 1)


sbd.defvjp(lambda a, b: (sbd(a, b), (a, b)), lambda r, g: (sbd_nt(g, r[1]), sbd_tn(r[0], g)))
sbd_nt.defvjp(lambda a, b: (sbd_nt(a, b), (a, b)), lambda r, g: (sbd(g, r[1]), sbd_tn(g, r[0])))
sbd_tn.defvjp(lambda a, b: (sbd_tn(a, b), (a, b)), lambda r, g: (sbd_nt(r[1], g), sbd(r[0], g)))


def _dns_f(S0, u, w, qd, kt, attn, dcrows):
    dc = _lane_col(dcrows, 0).reshape(6, 1, 1)
    delta = u - sbd(w, S0)
    out = sbd(qd, S0) + sbd(attn, delta)
    return out, dc * S0 + sbd_tn(kt, delta)


def _dnpost_f(o, z, grow):
    outs = []
    for h in range(6):
        oh = o[:, LANE * h:LANE * (h + 1)]
        outs.append(oh * lax.rsqrt(jnp.mean(oh * oh, axis=-1, keepdims=True) + EPS) * grow
                    * _silu(z[:, LANE * h:LANE * (h + 1)]))
    return jnp.concatenate(outs, axis=1)


def _hs(h):
    return slice(LANE * h, LANE * (h + 1))


DN_CHUNKS = 4


def _heads(ref, share):
    return _stack([ref[CHUNK * j:CHUNK * (j + 1), _hs(h // share)]
                   for j in range(ref.shape[0] // CHUNK) for h in range(6)])


def _put_heads(ref, val):
    for j in range(ref.shape[0] // CHUNK):
        for h in range(6):
            ref[CHUNK * j:CHUNK * (j + 1), _hs(h)] = val[6 * j + h]


def _dnc_in_specs():
    rows = CHUNK * DN_CHUNKS
    return [
        pl.BlockSpec((rows, B_QK), lambda n: (n, 0)),
        pl.BlockSpec((rows, B_QK), lambda n: (n, 1)),
        pl.BlockSpec((rows, B_V), lambda n: (n, 1)),
        pl.BlockSpec((rows, LANE), lambda n: (n, 20)),
        pl.BlockSpec((8, LANE), lambda n: (0, 0)),
    ]


def _dnc_out_specs(rev_nc=None, chunks=1):
    ci = (lambda n: n) if rev_nc is None else (lambda n: rev_nc - 1 - n)
    wide = pl.BlockSpec((CHUNK * chunks, B_V), lambda n: (ci(n), 0))
    return [wide, wide, wide, wide, pl.BlockSpec((chunks, 6, CHUNK, CHUNK), lambda n: (ci(n), 0, 0, 0)),
            pl.BlockSpec((chunks, 8, LANE), lambda n: (ci(n), 0, 0))]


def _dc_rows(dc):
    pad = jnp.zeros((2, LANE), f32)
    return _stack([jnp.concatenate([dc[6 * j:6 * (j + 1)], pad], axis=0) for j in range(dc.shape[0] // 6)])


def _dnc_shapes(S):
    nc = S // CHUNK
    wide = jax.ShapeDtypeStruct((S, B_V), f32)
    return [wide, wide, wide, wide, jax.ShapeDtypeStruct((nc, 6, CHUNK, CHUNK), f32),
            jax.ShapeDtypeStruct((nc, 8, LANE), f32)]


def dnc_fwd(qkvn, proj, prm):
    S = proj.shape[0]

    def body(q_ref, k_ref, v_ref, s_ref, p_ref, u_ref, w_ref, qd_ref, kt_ref, at_ref, dc_ref, inv_ref):
        u, w, qd, kt, attn, dc, inv = _dnc_f(_heads(q_ref, 2), _heads(k_ref, 2), _heads(v_ref, 1), s_ref[...],
                                             p_ref[...])
        inv_ref[...] = inv.reshape(inv_ref.shape)
        _put_heads(u_ref, u)
        _put_heads(w_ref, w)
        _put_heads(qd_ref, qd)
        _put_heads(kt_ref, kt)
        at_ref[...] = attn.reshape(at_ref.shape)
        dc_ref[...] = _dc_rows(dc)

    outs = _dnc_out_specs(chunks=DN_CHUNKS)
    out = pl.pallas_call(
        body, name="dn_chunk_fwd", grid=(S // (CHUNK * DN_CHUNKS),), in_specs=_dnc_in_specs(),
        out_specs=outs + [outs[4]], out_shape=_dnc_shapes(S) + [_dnc_shapes(S)[4]],
        compiler_params=_cp(("parallel",)),
    )(qkvn, qkvn, qkvn, proj, prm)
    return out[:6], out[6]


def dnc_bwd(qkvn, proj, prm, inv, cots):
    S = proj.shape[0]

    def body(q_ref, k_ref, v_ref, s_ref, p_ref, inv_ref, du_ref, dw_ref, dqd_ref, dkt_ref, dat_ref, ddc_ref,
             dx_ref, dseg_ref, dprm_ref):
        @pl.when(pl.program_id(0) == 0)
        def _():
            dprm_ref[...] = jnp.zeros_like(dprm_ref)
        nb = 6 * DN_CHUNKS
        known = functools.partial(_tri_inv_known, inv=inv_ref[...].reshape(nb, CHUNK, CHUNK))
        _, vjp = jax.vjp(lambda *a: _dnc_f(*a, inverse=known)[:6], _heads(q_ref, 2), _heads(k_ref, 2),
                         _heads(v_ref, 1), s_ref[...], p_ref[...])
        ddc = jnp.concatenate([ddc_ref[j, 0:6, :] for j in range(DN_CHUNKS)], axis=0)
        dq, dk, dv, dseg, dprm = vjp((_heads(du_ref, 1), _heads(dw_ref, 1), _heads(dqd_ref, 1), _heads(dkt_ref, 1),
                                      dat_ref[...].reshape(nb, CHUNK, CHUNK), ddc))
        for j in range(DN_CHUNKS):
            o = 6 * j
            dx_ref[CHUNK * j:CHUNK * (j + 1), :] = jnp.concatenate(
                [dq[o] + dq[o + 1], dq[o + 2] + dq[o + 3], dq[o + 4] + dq[o + 5],
                 dk[o] + dk[o + 1], dk[o + 2] + dk[o + 3], dk[o + 4] + dk[o + 5]] + [dv[o + h] for h in range(6)], axis=1)
        dseg_ref[...] = dseg.astype(bf16)
        dprm_ref[...] += dprm

    rows = CHUNK * DN_CHUNKS
    outs = _dnc_out_specs(chunks=DN_CHUNKS)
    return pl.pallas_call(
        body, name="dn_chunk_bwd", grid=(S // rows,),
        in_specs=_dnc_in_specs() + [outs[4]] + outs,
        out_specs=[pl.BlockSpec((rows, B_QKV), lambda n: (n, 0)), pl.BlockSpec((rows, LANE), lambda n: (n, 0)),
                   pl.BlockSpec((8, LANE), lambda n: (0, 0))],
        out_shape=[jax.ShapeDtypeStruct((S, B_QKV), f32), jax.ShapeDtypeStruct((S, LANE), bf16),
                   jax.ShapeDtypeStruct((8, LANE), f32)],
        compiler_params=_cp(("arbitrary",)),
    )(qkvn, qkvn, qkvn, proj, prm, inv, *cots)


def dns_fwd(chunked):
    u = chunked[0]
    S = u.shape[0]
    nc = S // CHUNK

    def body(u_ref, w_ref, qd_ref, kt_ref, at_ref, dc_ref, o_ref, st_ref, st):
        @pl.when(pl.program_id(0) == 0)
        def _():
            st[...] = jnp.zeros_like(st)
        S0 = st[...]
        st_ref[0] = S0
        out, S1 = _dns_f(S0, _heads(u_ref, 1), _heads(w_ref, 1), _heads(qd_ref, 1), _heads(kt_ref, 1),
                         at_ref[0], dc_ref[0, 0:6, :])
        _put_heads(o_ref, out)
        st[...] = S1

    return pl.pallas_call(
        body, name="dn_scan_fwd", grid=(nc,), in_specs=_dnc_out_specs(),
        out_specs=[pl.BlockSpec((CHUNK, B_V), lambda n: (n, 0)),
                   pl.BlockSpec((1, 6, B_DH, B_DH), lambda n: (n, 0, 0, 0))],
        out_shape=[jax.ShapeDtypeStruct((S, B_V), f32), jax.ShapeDtypeStruct((nc, 6, B_DH, B_DH), f32)],
        scratch_shapes=[pltpu.VMEM((6, B_DH, B_DH), f32)],
        compiler_params=_cp(("arbitrary",)),
    )(*chunked)


def dns_bwd(chunked, states, do):
    S = do.shape[0]
    nc = S // CHUNK

    def body(u_ref, w_ref, qd_ref, kt_ref, at_ref, dc_ref, st_ref, do_ref,
             du_ref, dw_ref, dqd_ref, dkt_ref, dat_ref, ddc_ref, dst):
        @pl.when(pl.program_id(0) == 0)
        def _():
            dst[...] = jnp.zeros_like(dst)
        _, vjp = jax.vjp(_dns_f, st_ref[0], _heads(u_ref, 1), _heads(w_ref, 1), _heads(qd_ref, 1), _heads(kt_ref, 1),
                         at_ref[0], dc_ref[0, 0:6, :])
        dS0, du, dw, dqd, dkt, dat, ddc = vjp((_heads(do_ref, 1), dst[...]))
        dst[...] = dS0
        _put_heads(du_ref, du)
        _put_heads(dw_ref, dw)
        _put_heads(dqd_ref, dqd)
        _put_heads(dkt_ref, dkt)
        dat_ref[0] = dat
        ddc_ref[0] = jnp.concatenate([ddc, jnp.zeros((2, LANE), f32)], axis=0)

    return pl.pallas_call(
        body, name="dn_scan_bwd", grid=(nc,),
        in_specs=_dnc_out_specs(nc) + [pl.BlockSpec((1, 6, B_DH, B_DH), lambda n: (nc - 1 - n, 0, 0, 0)),
                                       pl.BlockSpec((CHUNK, B_V), lambda n: (nc - 1 - n, 0))],
        out_specs=_dnc_out_specs(nc), out_shape=_dnc_shapes(S),
        scratch_shapes=[pltpu.VMEM((6, B_DH, B_DH), f32)],
        compiler_params=_cp(("arbitrary",)),
    )(*chunked, states, do)


def dnpost_fwd(o, proj, prm):
    S = o.shape[0]
    t = min(512, S)

    def body(o_ref, z_ref, p_ref, y_ref):
        y_ref[...] = _dnpost_f(o_ref[...], z_ref[...], p_ref[2:3, :]).astype(bf16)

    tok = pl.BlockSpec((t, B_V), lambda i: (i, 0))
    return pl.pallas_call(
        body, name="dn_post_fwd", grid=(S // t,),
        in_specs=[tok, pl.BlockSpec((t, B_V), lambda i: (i, 2)), pl.BlockSpec((8, LANE), lambda i: (0, 0))],
        out_specs=tok, out_shape=jax.ShapeDtypeStruct((S, B_V), bf16), compiler_params=_cp(("parallel",)),
    )(o, proj, prm)


def dnpost_bwd(o, proj, prm, dmix):
    S = o.shape[0]
    t = min(512, S)

    def body(o_ref, z_ref, p_ref, dy_ref, do_ref, dz_ref, dg_ref):
        @pl.when(pl.program_id(0) == 0)
        def _():
            dg_ref[...] = jnp.zeros_like(dg_ref)
        _, vjp = jax.vjp(_dnpost_f, o_ref[...], z_ref[...], p_ref[2:3, :])
        do, dz, dg = vjp(dy_ref[...])
        do_ref[...] = do
        dz_ref[...] = dz.astype(bf16)
        dg_ref[...] += dg

    tok = pl.BlockSpec((t, B_V), lambda i: (i, 0))
    return pl.pallas_call(
        body, name="dn_post_bwd", grid=(S // t,),
        in_specs=[tok, pl.BlockSpec((t, B_V), lambda i: (i, 2)), pl.BlockSpec((8, LANE), lambda i: (0, 0)), tok],
        out_specs=[tok, tok, pl.BlockSpec((1, LANE), lambda i: (0, 0))],
        out_shape=[jax.ShapeDtypeStruct((S, B_V), f32), jax.ShapeDtypeStruct((S, B_V), bf16),
                   jax.ShapeDtypeStruct((1, LANE), f32)],
        compiler_params=_cp(("arbitrary",)),
    )(o, proj, prm, dmix)


N_FF_BLK = D_FF // LANE
GU_SHARD = 2 * D_FF // 4


GLU_ROWS = 256
HALO = 16


def _glu_conv(gext, w, b):
    return (w[2:3] * gext + w[1:2] * shift_down(gext, 1) + w[0:1] * shift_down(gext, 2) + b)[HALO:]


def _glu_gate(c, up):
    return _silu(c) * up


def _glu_gext(g_ref, r0, first, T=GLU_ROWS):
    if first:
        return jnp.concatenate([jnp.zeros((HALO, LANE), f32), g_ref[0:T, :].astype(f32)], axis=0)
    return g_ref[pl.ds(r0 - HALO, T + HALO), :].astype(f32)


def glu_fwd(gu, w, b, name):
    S = gu.shape[0]
    T = min(GLU_ROWS, S // 2)

    def body(g_ref, u_ref, w_ref, b_ref, o_ref, c_ref):
        wv, bv = w_ref[...], b_ref[...]

        def tile(r0, first):
            c = _glu_conv(_glu_gext(g_ref, r0, first, T), wv, bv)
            c_ref[pl.ds(r0, T), :] = c.astype(bf16)
            o_ref[pl.ds(r0, T), :] = _glu_gate(c, u_ref[pl.ds(r0, T), :].astype(f32)).astype(bf16)

        tile(0, True)

        @pl.loop(1, S // T)
        def _(t):
            tile(pl.multiple_of(t * T, T), False)

    col = pl.BlockSpec((S, LANE), lambda j: (0, j))
    return pl.pallas_call(
        body, name=name, grid=(N_FF_BLK,),
        in_specs=[col, pl.BlockSpec((S, LANE), lambda j: (0, N_FF_BLK + j)), pl.BlockSpec((3, LANE), lambda j: (0, j)),
                  pl.BlockSpec((1, LANE), lambda j: (0, j))],
        out_specs=[col, col], out_shape=[jax.ShapeDtypeStruct((S, D_FF), bf16)] * 2,
        compiler_params=_cp(("parallel",)),
    )(gu, gu, w, b.reshape(1, D_FF))


def glu_bwd(gu, c, w, b, dact, name):
    S = gu.shape[0]
    T = min(GLU_ROWS, S // 2)

    def body(g_ref, u_ref, c_ref, w_ref, b_ref, d_ref, dg_ref, dw_ref, db_ref, acc):
        wv, bv = w_ref[...], b_ref[...]

        def tile(r0, first):
            rows = pl.ds(r0, T)
            _, vjp_gate = jax.vjp(_glu_gate, c_ref[rows, :].astype(f32), u_ref[rows, :].astype(f32))
            dc, du = vjp_gate(d_ref[rows, :].astype(f32))
            _, vjp_conv = jax.vjp(_glu_conv, _glu_gext(g_ref, r0, first, T), wv, bv)
            dgx, dw, db = vjp_conv(dc)
            acc[pl.ds(r0, T), :] = dgx[HALO:]
            if not first:
                acc[pl.ds(r0 - HALO, HALO), :] += dgx[:HALO]
            dg_ref[1, pl.ds(r0, T), :] = du.astype(bf16)
            return dw, db

        dw0, db0 = tile(0, True)
        dw_ref[...] = dw0
        db_ref[...] = db0

        @pl.loop(1, S // T)
        def _(t):
            dw, db = tile(pl.multiple_of(t * T, T), False)
            dw_ref[...] += dw
            db_ref[...] += db

        dg_ref[0] = acc[...].astype(bf16)

    col = pl.BlockSpec((S, LANE), lambda j: (0, j))
    wsp = pl.BlockSpec((3, LANE), lambda j: (0, j))
    bsp = pl.BlockSpec((1, LANE), lambda j: (0, j))
    return pl.pallas_call(
        body, name=name, grid=(N_FF_BLK,),
        in_specs=[col, pl.BlockSpec((S, LANE), lambda j: (0, N_FF_BLK + j)), col, wsp, bsp, col],
        out_specs=[pl.BlockSpec((2, S, LANE), lambda j: (0, 0, j)), wsp, bsp],
        out_shape=[jax.ShapeDtypeStruct((2, S, D_FF), bf16), jax.ShapeDtypeStruct((3, D_FF), f32),
                   jax.ShapeDtypeStruct((1, D_FF), f32)],
        scratch_shapes=[pltpu.VMEM((S, LANE), f32)],
        compiler_params=_cp(("parallel",)),
    )(gu, gu, c, w, b.reshape(1, D_FF), dact)


def gu_fwd(n2, wg, name):
    S = n2.shape[0]
    tm = min(MM_ROWS, S)

    def body(a_ref, w_ref, o_ref):
        o_ref[...] = _dg(a_ref[...], w_ref[...], 1, 0).astype(bf16)

    return pl.pallas_call(
        body, name=name, grid=(4, S // tm),
        in_specs=[pl.BlockSpec((tm, D), lambda s, m: (m, 0)), pl.BlockSpec((None, D, GU_SHARD), lambda s, m: (s, 0, 0))],
        out_specs=pl.BlockSpec((tm, GU_SHARD), lambda s, m: (m, s)),
        out_shape=jax.ShapeDtypeStruct((S, 2 * D_FF), bf16), compiler_params=_cp(("parallel", "parallel")),
    )(n2, wg)


def gu_bwd_x(dgu, wg, norm, name):
    S = dgu.shape[1]
    tm = min(NORM_ROWS, S)

    def body(d_ref, w_ref, h_ref, g_ref, r_ref, o_ref, dg_ref):
        _acc_then_norm_bwd(_dg(d_ref[...], w_ref[...], 1, 1), 4, h_ref, g_ref, r_ref, o_ref, dg_ref)

    tok = pl.BlockSpec((tm, D), lambda m, s: (m, 0))
    vec = pl.BlockSpec((1, D), lambda m, s: (0, 0))
    return pl.pallas_call(
        body, name=name, grid=(S // tm, 4),
        in_specs=[pl.BlockSpec((None, tm, GU_SHARD), lambda m, s: (s // 2, m, s % 2)),
                  pl.BlockSpec((None, D, GU_SHARD), lambda m, s: (s, 0, 0)), tok, vec, tok],
        out_specs=[tok, vec],
        out_shape=[jax.ShapeDtypeStruct((S, D), f32), jax.ShapeDtypeStruct((1, D), f32)],
        compiler_params=_cp(("arbitrary", "arbitrary")),
    )(dgu, wg, norm[0], norm[1].reshape(1, D), norm[2])


def gu_bwd_w(n2, dgu, name):
    S = n2.shape[0]
    tm = min(MM_ROWS, S)
    nm = S // tm

    def body(a_ref, d_ref, o_ref, acc):
        @pl.when(pl.program_id(1) == 0)
        def _():
            acc[...] = jnp.zeros_like(acc)
        acc[...] += _dg(a_ref[...], d_ref[...], 0, 0)

        @pl.when(pl.program_id(1) == nm - 1)
        def _():
            o_ref[...] = acc[...].astype(bf16)

    return pl.pallas_call(
        body, name=name, grid=(4, nm),
        in_specs=[pl.BlockSpec((tm, D), lambda s, m: (m, 0)),
                  pl.BlockSpec((None, tm, GU_SHARD), lambda s, m: (s // 2, m, s % 2))],
        out_specs=pl.BlockSpec((None, D, GU_SHARD), lambda s, m: (s, 0, 0)),
        out_shape=jax.ShapeDtypeStruct((4, D, GU_SHARD), bf16),
        scratch_shapes=[pltpu.VMEM((D, GU_SHARD), f32)],
        compiler_params=_cp(("parallel", "arbitrary")),
    )(n2, dgu)


def _pair_cols(w):
    lead = w.shape[:-1]
    return w.reshape(lead + (2, 6, A_DH)).swapaxes(-3, -2).reshape(lead + (A_Q,))


def _unpair_cols(w):
    lead = w.shape[:-1]
    return w.reshape(lead + (6, 2, A_DH)).swapaxes(-3, -2).reshape(lead + (A_Q,))


def _lay_in_a(w):
    return jnp.concatenate([_pair_cols(w[:, :A_Q]), w[:, A_Q:]], axis=1)


def _unlay_in_a(w):
    return jnp.concatenate([_unpair_cols(w[:, :A_Q]), w[:, A_Q:]], axis=1)


def _lay_out_a(w):
    return jnp.concatenate([_pair_cols(w[:A_Q].T).T, w[A_Q:]], axis=0)


def _unlay_out_a(w):
    return jnp.concatenate([_unpair_cols(w[:A_Q].T).T, w[A_Q:]], axis=0)


def _lay_in_b(w):
    return jnp.concatenate([w[:, :2304], w[:, 2316:], w[:, 2304:2316],
                            jnp.zeros((w.shape[0], LANE - 12), w.dtype)], axis=1)


def _unlay_in_b(w):
    return jnp.concatenate([w[:, :2304], w[:, 2560:2572], w[:, 2304:2560]], axis=1)


def _chip_cols(w):
    return jnp.moveaxis(w.reshape(w.shape[0], 4, w.shape[1] // 4), 1, 0)


def _unchip_cols(w):
    return jnp.moveaxis(w, 0, 1).reshape(w.shape[1], 4 * w.shape[2])


def _local_step(x, mem, target, P):
    arrive = P.get("arrive", lambda key, after: None)
    ready = P.get("ready", lambda key, grads, dep: dep)
    sk = jnp.zeros((16, LANE), f32).at[:A_HEADS].set(jnp.broadcast_to(P["sinks"][:, None], (A_HEADS, LANE)))
    prm = jnp.zeros((8, LANE), f32).at[0, 6:12].set(P["a_log"]).at[1, 6:12].set(P["dt_bias"]).at[2].set(P["out_norm_g"])
    bias = bias_build(P["rel_bias"])
    saved = []
    h = x
    n1 = rms_fwd(h, P["g_mix"][0], "rms_mix0")
    for i in range(2):
        arrive(("w_in", i), n1)
        proj = mm_nn(n1, P["w_in_a"] if i == 0 else P["w_in_b"], name="proj_a" if i == 0 else "proj_b")
        arrive(("w_mem", i), proj)
        kv = memkv_fwd(mem, P["g_mem"][i], P["w_mem"][i], f"memkv{i}")
        if i == 0:
            self_out = swa_fwd(proj, bias, sk)
            cross = xattn_fwd(proj, A_Q + 2 * LANE, kv, "xattn_a")
            extra = ()
        else:
            qkvn = dnprep_fwd(proj, P["conv_qkv"])
            chunked, inv = dnc_fwd(qkvn, proj, prm)
            o, states = dns_fwd(chunked)
            self_out = dnpost_fwd(o, proj, prm)
            cross = xattn_fwd(proj, 2304, kv, "xattn_b")
            extra = (qkvn, chunked, inv, states, o)
        mix = (self_out, cross)
        arrive(("w_out", i), cross)
        h2, n2 = mm_res_norm(mix, P["w_out"][i], h, P["g_ffn"][i], f"out_proj{i}")
        arrive(("w_gu", i), n2)
        gu = gu_fwd(n2, P["w_gu"][i], f"gate_up{i}")
        act, pre = glu_fwd(gu, P["ffn_cw"][i], P["ffn_cb"][i], f"glu{i}")
        arrive(("w_down", i), act)
        saved.append((h, n1, kv, proj, mix, h2, n2, gu, pre, act, extra))
        if i == 0:
            h, n1 = mm_res_norm(act, P["w_down"][i], h2, P["g_mix"][1], f"down{i}")
        else:
            h = mm_nn(act, P["w_down"][i], res=h2, name=f"down{i}")

    loss, dh, dg_fin = loss_head(h, P["g_fin"], target)
    G = {"g_fin": dg_fin[0], "g_mix": [None, None], "g_mem": [None, None], "g_ffn": [None, None],
         "w_mem": [None, None], "w_out": [None, None], "w_gu": [None, None], "w_down": [None, None],
         "ffn_cw": [None, None], "ffn_cb": [None, None]}
    for i in (1, 0):
        hin, n1, kv, proj, mix, h2, n2, gu, pre, act, extra = saved[i]
        dact = mm_nt(dh, P["w_down"][i], out_dtype=bf16, name=f"d_act{i}")
        G["w_down"][i] = mm_tn(act, dh, name=f"dw_down{i}")
        dgu, dcw, dcb = glu_bwd(gu, pre, P["ffn_cw"][i], P["ffn_cb"][i], dact, f"glu_bwd{i}")
        G["ffn_cw"][i], G["ffn_cb"][i] = dcw, dcb[0]
        G["w_gu"][i] = gu_bwd_w(n2, dgu, f"dw_gu{i}")
        g_ffn = ready(("ffn", i), G, P["g_ffn"][i])
        dh2, dg = gu_bwd_x(dgu, P["w_gu"][i], (h2, g_ffn, dh), f"d_n2_{i}")
        G["g_ffn"][i] = dg[0]
        dmix = mm_nt(dh2, P["w_out"][i], name=f"d_mix{i}")
        G["w_out"][i] = mm_tn(mix, dh2, name=f"dw_out{i}")
        if i == 0:
            dqkv, dbias, dsk = swa_bwd(proj, bias, sk, dmix)
            dxq, dkv = xattn_bwd(proj, A_Q + 2 * LANE, kv, dmix, "xattn_a_bwd")
            dproj = (dqkv, dxq)
            G["sinks"] = dsk[:A_HEADS, 0]
            G["rel_bias"] = bias_grad(dbias)[:, :A_HEADS]
            w_in, gname = P["w_in_a"], "w_in_a"
        else:
            qkvn, chunked, inv, states, o = extra
            do, dz, dgo = dnpost_bwd(o, proj, prm, dmix)
            dqkvn, dseg, dprm = dnc_bwd(qkvn, proj, prm, inv, dns_bwd(chunked, states, do))
            draw, dconv = dnprep_bwd(proj, P["conv_qkv"], dqkvn)
            dxq, dkv = xattn_bwd(proj, 2304, kv, dmix, "xattn_b_bwd")
            dproj = (draw, dz, dxq, dseg)
            G["conv_qkv"] = dconv
            G["a_log"], G["dt_bias"], G["out_norm_g"] = dprm[0, 6:12], dprm[1, 6:12], dgo[0]
            w_in, gname = P["w_in_b"], "w_in_b"
        G[gname] = mm_tn(n1, dproj, name=f"d{gname}")
        dh, dg = mm_nt_norm(dproj, w_in, (hin, P["g_mix"][i], dh2), f"d_n1_{i}")
        G["g_mix"][i] = dg[0]
        dgm, dwm = memkv_bwd(mem, P["g_mem"][i], P["w_mem"][i], dkv, f"memkv_bwd{i}")
        G["g_mem"][i], G["w_mem"][i] = dgm[0], dwm
        ready(("mix", i), G, None)
    return loss, dh, G


def _grads_to_ref(G):
    return {
        "rel_bias": G["rel_bias"], "norm_mix_g": jnp.stack(G["g_mix"]), "norm_mem_g": jnp.stack(G["g_mem"]),
        "w_mem_kv": jnp.stack(G["w_mem"]),
        "w_out": jnp.stack([_unlay_out_a(G["w_out"][0]), G["w_out"][1]]),
        "w_in_a": _unlay_in_a(G["w_in_a"])[None], "sinks_a": G["sinks"][None],
        "w_in_b": _unlay_in_b(G["w_in_b"])[None], "conv_qkv_b": G["conv_qkv"][None],
        "a_log_b": G["a_log"][None], "dt_bias_b": G["dt_bias"][None], "out_norm_g_b": G["out_norm_g"][None],
        "norm_ffn_g": jnp.stack(G["g_ffn"]),
        "w_gate_up": jnp.stack([_unchip_cols(G["w_gu"][0]), _unchip_cols(G["w_gu"][1])]).astype(f32),
        "ffn_conv_w": jnp.stack(G["ffn_cw"]), "ffn_conv_b": jnp.stack(G["ffn_cb"]),
        "w_down": jnp.stack(G["w_down"]), "final_norm_g": G["g_fin"],
    }


ANY = pl.BlockSpec(memory_space=pl.ANY)


def _place():
    return lax.axis_index("x"), lax.axis_index("y"), lax.axis_index("c")


def allreduce_small(buf):
    R = buf.shape[0]

    def body(b_ref, o_ref, recv, ssem, rsem):
        x, y, c = _place()
        me = 4 * x + 2 * y + c

        def peer(k):
            return (1 - x if k & 4 else x, 1 - y if k & 2 else y, 1 - c if k & 1 else c)

        def remote(k, slot):
            return pltpu.make_async_remote_copy(
                src_ref=b_ref, dst_ref=recv.at[slot], send_sem=ssem.at[k - 1], recv_sem=rsem.at[k - 1],
                device_id=peer(k), device_id_type=MESH)

        sends = [remote(k, me) for k in range(1, 8)]
        for cp in sends:
            cp.start()
        recv[me] = b_ref[...]
        for k in range(1, 8):
            px, py, pc = peer(k)
            remote(k, 4 * px + 2 * py + pc).wait_recv()
        for cp in sends:
            cp.wait_send()
        total = recv[0]
        for j in range(1, 8):
            total = total + recv[j]
        o_ref[...] = total

    return pl.pallas_call(
        body, name="small_allreduce",
        in_specs=[pl.BlockSpec(memory_space=pltpu.VMEM)], out_specs=pl.BlockSpec(memory_space=pltpu.VMEM),
        out_shape=jax.ShapeDtypeStruct(buf.shape, f32),
        scratch_shapes=[pltpu.VMEM((8, R, LANE), f32), pltpu.SemaphoreType.DMA((7,)), pltpu.SemaphoreType.DMA((7,))],
    )(buf)


def sum_slots(own, recv, chip, core, name):
    _, R, C = recv.shape
    tr = _row_tile(R, 256)
    nt = R // tr

    def body(p_ref, a_ref, r_ref, o_ref):
        acc = jnp.zeros((tr, C), f32)
        for s in range(4):
            acc = acc + jnp.where(p_ref[0] == s, a_ref[s], r_ref[s]).astype(f32)
        o_ref[...] = acc

    slots = pl.BlockSpec((4, tr, C), lambda i, p_ref: (0, i, 0))
    return pl.pallas_call(
        body, name=name, out_shape=jax.ShapeDtypeStruct((2 * R, C), f32),
        grid_spec=pltpu.PrefetchScalarGridSpec(
            num_scalar_prefetch=1, grid=(nt,), in_specs=[slots, slots],
            out_specs=pl.BlockSpec((tr, C), lambda i, p_ref: (p_ref[1] * nt + i, 0))),
        compiler_params=_cp(("parallel",)),
    )(jnp.stack([chip, core]).astype(jnp.int32), own, recv)


def _half(ref, core, axis=0):
    half = ref.shape[axis] // 2
    idx = (slice(None),) * axis + (pl.ds(core * half, half),)
    return ref.at[idx]


IN_HBM = pl.BlockSpec(memory_space=pltpu.HBM)
IN_SEM = pl.BlockSpec(memory_space=pltpu.SEMAPHORE)
SIDE_EFFECT = pltpu.SideEffectType.DATAFLOW_SIDE_EFFECTING


def _gather_copy(buf, i, k, ssem, rsem, place, landing):
    x, y, c = place
    px, py = [(1 - x, y), (x, 1 - y), (1 - x, 1 - y)][k]
    me = 2 * x + y
    return pltpu.make_async_remote_copy(
        src_ref=buf.at[me], dst_ref=buf.at[me if landing == "theirs" else 2 * px + py],
        send_sem=ssem.at[3 * i + k], recv_sem=rsem.at[3 * i + k], device_id=(px, py, c), device_id_type=MESH)


def gather_start(groups):
    flat = [b for grp in groups for b in grp]
    n, ng = len(flat), len(groups)

    def body(*refs):
        bufs, sems = refs[:n], refs[n:n + 2 * ng]
        place = _place()
        j = 0
        for g, grp in enumerate(groups):
            for i in range(len(grp)):
                for k in range(3):
                    _gather_copy(bufs[j], i, k, sems[2 * g], sems[2 * g + 1], place, "theirs").start()
                j += 1

    sem_shapes = [pltpu.SemaphoreType.DMA((3 * len(grp),)) for grp in groups for _ in range(2)]
    out = pl.pallas_call(
        body, name="gather_start", in_specs=[IN_HBM] * n, out_specs=(*[IN_SEM] * (2 * ng), *[IN_HBM] * n),
        out_shape=(*sem_shapes, *[pltpu.HBM(b.shape, b.dtype) for b in flat]),
        input_output_aliases={i: 2 * ng + i for i in range(n)},
        compiler_params=pltpu.CompilerParams(has_side_effects=SIDE_EFFECT),
    )(*[pltpu.with_memory_space_constraint(b, pltpu.HBM) for b in flat])
    sems, bufs = out[:2 * ng], list(out[2 * ng:])
    flights, j = [], 0
    for g, grp in enumerate(groups):
        flights.append((bufs[j:j + len(grp)], sems[2 * g], sems[2 * g + 1]))
        j += len(grp)
    return flights


def gather_wait(flight, after, name):
    bufs, ssem, rsem = flight
    n = len(bufs)

    def body(*refs):
        place = _place()
        for i in range(n):
            for k in range(3):
                cp = _gather_copy(refs[i], i, k, refs[n], refs[n + 1], place, "mine")
                cp.wait_send()
                cp.wait_recv()

    return pl.pallas_call(
        body, name=name, in_specs=[IN_HBM] * n + [IN_SEM, IN_SEM, ANY], out_specs=[IN_HBM] * n,
        out_shape=[pltpu.HBM(b.shape, b.dtype) for b in bufs], input_output_aliases={i: i for i in range(n)},
        compiler_params=pltpu.CompilerParams(has_side_effects=SIDE_EFFECT),
    )(*bufs, ssem, rsem, after)


def _scatter_copy(src, land, j, k, ssem, rsem, place, landing):
    x, y, c = place
    px, py = [(1 - x, y), (x, 1 - y), (1 - x, 1 - y)][k]
    return pltpu.make_async_remote_copy(
        src_ref=src.at[2 * px + py], dst_ref=land.at[2 * x + y if landing == "theirs" else 2 * px + py],
        send_sem=ssem.at[3 * j + k], recv_sem=rsem.at[3 * j + k], device_id=(px, py, c), device_id_type=MESH)


def scatter_start(srcs, name):
    n = len(srcs)
    lands = [lax.empty(g.shape, g.dtype) for g in srcs]

    def body(*refs):
        place = _place()
        for j in range(n):
            for k in range(3):
                _scatter_copy(refs[j], refs[n + j], j, k, refs[2 * n], refs[2 * n + 1], place, "theirs").start()
        refs[-1][...] = jnp.zeros_like(refs[-1])

    sem = pltpu.SemaphoreType.DMA((3 * n,))
    hbm = [pltpu.with_memory_space_constraint(b, pltpu.HBM) for b in list(srcs) + lands]
    out = pl.pallas_call(
        body, name=name, in_specs=[IN_HBM] * (2 * n),
        out_specs=(IN_SEM, IN_SEM, *[IN_HBM] * (2 * n), pl.BlockSpec(memory_space=pltpu.VMEM)),
        out_shape=(sem, sem, *[pltpu.HBM(b.shape, b.dtype) for b in hbm], jax.ShapeDtypeStruct((8, LANE), f32)),
        input_output_aliases={i: 2 + i for i in range(2 * n)},
        compiler_params=pltpu.CompilerParams(has_side_effects=SIDE_EFFECT),
    )(*hbm)
    return (list(out[2:2 + n]), list(out[2 + n:2 + 2 * n]), out[0], out[1]), out[-1]


def scatter_wait(flight, after, name):
    srcs, lands, ssem, rsem = flight
    n = len(srcs)

    def body(*refs):
        place = _place()
        for j in range(n):
            for k in range(3):
                cp = _scatter_copy(refs[j], refs[n + j], j, k, refs[2 * n], refs[2 * n + 1], place, "mine")
                cp.wait_send()
                cp.wait_recv()

    out = pl.pallas_call(
        body, name=name, in_specs=[IN_HBM] * (2 * n) + [IN_SEM, IN_SEM, ANY], out_specs=[IN_HBM] * (2 * n),
        out_shape=[pltpu.HBM(b.shape, b.dtype) for b in list(srcs) + list(lands)],
        input_output_aliases={i: i for i in range(2 * n)},
        compiler_params=pltpu.CompilerParams(has_side_effects=SIDE_EFFECT),
    )(*srcs, *lands, ssem, rsem, after)
    return list(out[:n]), list(out[n:])


def pair_exchange(gbufs, name):
    n = len(gbufs)

    def body(*refs):
        ins, outs = refs[:n], refs[n:2 * n]
        ssem, rsem = refs[2 * n:]
        x, y, c = _place()
        cps = [pltpu.make_async_remote_copy(
            src_ref=_half(ins[j], 1 - c, axis=1), dst_ref=outs[j], send_sem=ssem.at[j], recv_sem=rsem.at[j],
            device_id=(x, y, 1 - c), device_id_type=MESH) for j in range(n)]
        for cp in cps:
            cp.start()
        for cp in cps:
            cp.wait()

    return pl.pallas_call(
        body, name=name, in_specs=[ANY] * n, out_specs=[ANY] * n,
        out_shape=[jax.ShapeDtypeStruct((4, g.shape[1] // 2, g.shape[2]), g.dtype) for g in gbufs],
        scratch_shapes=[pltpu.SemaphoreType.DMA((n,)), pltpu.SemaphoreType.DMA((n,))],
    )(*gbufs)


def _row_tile(rows, cap=512):
    return max(t for t in range(16, min(rows, cap) + 1, 16) if rows % t == 0)


def pair_sum(mine, theirs, core, name):
    _, R, C = mine.shape
    half = R // 2
    tr = _row_tile(half)
    nt = half // tr

    def body(c_ref, a_ref, b_ref, o_ref):
        o_ref[...] = (a_ref[...].astype(f32) + b_ref[...].astype(f32)).astype(bf16)

    return pl.pallas_call(
        body, name=name, out_shape=jax.ShapeDtypeStruct(theirs.shape, bf16),
        grid_spec=pltpu.PrefetchScalarGridSpec(
            num_scalar_prefetch=1, grid=(4, nt),
            in_specs=[pl.BlockSpec((None, tr, C), lambda s, i, c_ref: (s, c_ref[0] * nt + i, 0)),
                      pl.BlockSpec((None, tr, C), lambda s, i, c_ref: (s, i, 0))],
            out_specs=pl.BlockSpec((None, tr, C), lambda s, i, c_ref: (s, i, 0))),
        compiler_params=_cp(("parallel", "parallel")),
    )(jnp.reshape(core, (1,)).astype(jnp.int32), mine, theirs)


def final_exchange(fins):
    n = len(fins)

    def body(*refs):
        outs = refs[n:2 * n]
        ssem, rsem = refs[2 * n:]
        x, y, c = _place()
        cps = [pltpu.make_async_remote_copy(
            src_ref=_half(outs[j], c), dst_ref=_half(outs[j], c), send_sem=ssem.at[j], recv_sem=rsem.at[j],
            device_id=(x, y, 1 - c), device_id_type=MESH) for j in range(n)]
        for cp in cps:
            cp.start()
        for cp in cps:
            cp.wait()

    return pl.pallas_call(
        body, name="final_exchange", in_specs=[ANY] * n, out_specs=[ANY] * n,
        out_shape=[jax.ShapeDtypeStruct(f.shape, f.dtype) for f in fins],
        input_output_aliases={j: j for j in range(n)},
        scratch_shapes=[pltpu.SemaphoreType.DMA((n,)), pltpu.SemaphoreType.DMA((n,))],
    )(*fins)


def adamw_big(w, m, v, gs, row0, name):
    L, R, C = w.shape
    tr = _row_tile(math.gcd(R, row0) if row0 else R, max(16, 262144 // C // 16 * 16))
    b0 = row0 // tr

    def body(*refs):
        w_ref, m_ref, v_ref = refs[:3]
        g_refs = refs[3:3 + L]
        g_ref, d_ref, nm_ref, nv_ref = refs[3 + L:]
        g = g_refs[0][...]
        for l in range(1, L):
            g = jnp.where(pl.program_id(0) == l, g_refs[l][...], g)
        d, nm, nv = _adamw_math(w_ref[...], g, m_ref[...], v_ref[...])
        g_ref[...] = g
        d_ref[...] = d
        nm_ref[...] = nm
        nv_ref[...] = nv

    own = pl.BlockSpec((None, tr, C), lambda l, i: (l, i, 0))
    off = pl.BlockSpec((tr, C), lambda l, i: (b0 + i, 0))
    return pl.pallas_call(
        body, name=name, grid=(L, R // tr), in_specs=[own, own, own] + [off] * L, out_specs=[own] * 4,
        out_shape=[jax.ShapeDtypeStruct((L, R, C), f32)] * 4, compiler_params=_cp(("parallel", "parallel")),
    )(w, m, v, *gs)


def _adamw_math(w, g, m, v):
    m = B1 * m + (1.0 - B1) * g
    v = B2 * v + (1.0 - B2) * (g * g)
    m_hat = m / (1.0 - B1 ** STEP)
    v_hat = v / (1.0 - B2 ** STEP)
    delta = -LR * (m_hat / (jnp.sqrt(v_hat) + AEPS) + WD * w)
    return delta, m, v


def adamw_small(w, m, v, g):
    def body(w_ref, m_ref, v_ref, g_ref, d_ref, nm_ref, nv_ref):
        d, nm, nv = _adamw_math(w_ref[...], g_ref[...], m_ref[...], v_ref[...])
        d_ref[...] = d
        nm_ref[...] = nm
        nv_ref[...] = nv

    return pl.pallas_call(body, name="adamw_small", out_shape=[jax.ShapeDtypeStruct(w.shape, f32)] * 3)(w, m, v, g)


CONV =(("conv_qkv_b", 2), ("ffn_conv_w", 2))
SMALL = ("rel_bias", "norm_mix_g", "norm_mem_g", "sinks_a", "a_log_b", "dt_bias_b", "out_norm_g_b", "norm_ffn_g",
         "ffn_conv_b", "final_norm_g")
WEIGHTS = ("rel_bias", "norm_mix_g", "norm_mem_g", "w_mem_kv", "w_out", "w_in_a", "sinks_a", "w_in_b", "conv_qkv_b",
           "a_log_b", "dt_bias_b", "out_norm_g_b", "norm_ffn_g", "w_gate_up", "ffn_conv_w", "ffn_conv_b", "w_down",
           "final_norm_g")
ARGS = ("x", "mem") + WEIGHTS + ("loss_target",) + tuple("m_" + n for n in WEIGHTS) + tuple("v_" + n for n in WEIGHTS)


def _rows(a, width):
    flat = a.reshape(-1)
    pad = (-flat.shape[0]) % (8 * width)
    if pad:
        flat = jnp.concatenate([flat, jnp.zeros((pad,), a.dtype)])
    return flat.reshape(-1, width)


def _nrows(shape, width):
    return _pad_to(-(-math.prod(shape) // width), 8)


def _pack(arrs, width, total_rows, dtype):
    parts = [_rows(a.astype(dtype), width) for a in arrs]
    used = sum(p.shape[0] for p in parts)
    if total_rows > used:
        parts.append(jnp.zeros((total_rows - used, width), dtype))
    return jnp.concatenate(parts, axis=0)


def _unpack(buf, shapes, width):
    out, r = [], 0
    for s in shapes:
        n = _nrows(s, width)
        out.append(buf[r:r + n].reshape(-1)[:math.prod(s)].reshape(s))
        r += n
    return out


def _pad_to(n, mult):
    return -(-n // mult) * mult


def kernel(x, mem, rel_bias, norm_mix_g, norm_mem_g, w_mem_kv, w_out, w_in_a, sinks_a, w_in_b, conv_qkv_b, a_log_b, dt_bias_b, out_norm_g_b, norm_ffn_g, w_gate_up, ffn_conv_w, ffn_conv_b, w_down, final_norm_g, loss_target, m_rel_bias, m_norm_mix_g, m_norm_mem_g, m_w_mem_kv, m_w_out, m_w_in_a, m_sinks_a, m_w_in_b, m_conv_qkv_b, m_a_log_b, m_dt_bias_b, m_out_norm_g_b, m_norm_ffn_g, m_w_gate_up, m_ffn_conv_w, m_ffn_conv_b, m_w_down, m_final_norm_g, v_rel_bias, v_norm_mix_g, v_norm_mem_g, v_w_mem_kv, v_w_out, v_w_in_a, v_sinks_a, v_w_in_b, v_conv_qkv_b, v_a_log_b, v_dt_bias_b, v_out_norm_g_b, v_norm_ffn_g, v_w_gate_up, v_ffn_conv_w, v_ffn_conv_b, v_w_down, v_final_norm_g):
    A = dict(zip(ARGS, (x, mem, rel_bias, norm_mix_g, norm_mem_g, w_mem_kv, w_out, w_in_a, sinks_a, w_in_b, conv_qkv_b, a_log_b, dt_bias_b, out_norm_g_b, norm_ffn_g, w_gate_up, ffn_conv_w, ffn_conv_b, w_down, final_norm_g, loss_target, m_rel_bias, m_norm_mix_g, m_norm_mem_g, m_w_mem_kv, m_w_out, m_w_in_a, m_sinks_a, m_w_in_b, m_conv_qkv_b, m_a_log_b, m_dt_bias_b, m_out_norm_g_b, m_norm_ffn_g, m_w_gate_up, m_ffn_conv_w, m_ffn_conv_b, m_w_down, m_final_norm_g, v_rel_bias, v_norm_mix_g, v_norm_mem_g, v_w_mem_kv, v_w_out, v_w_in_a, v_sinks_a, v_w_in_b, v_conv_qkv_b, v_a_log_b, v_dt_bias_b, v_out_norm_g_b, v_norm_ffn_g, v_w_gate_up, v_ffn_conv_w, v_ffn_conv_b, v_w_down, v_final_norm_g)))
    chip = 2 * lax.axis_index("x") + lax.axis_index("y")
    core = lax.axis_index("c")

    def own_slot(shard):
        return lax.dynamic_update_index_in_dim(lax.empty((4,) + shard.shape, shard.dtype), shard, chip, 0)

    def bslot(w):
        return own_slot(w.astype(bf16))

    groups = {
        ("w_in", 0): [bslot(w_in_a[0])],
        ("w_mem", 0): [bslot(w_mem_kv[0]), bslot(w_mem_kv[1]), own_slot(conv_qkv_b[0]),
                       own_slot(ffn_conv_w.reshape(6, -1))],
        ("w_out", 0): [bslot(w_out[0])], ("w_gu", 0): [bslot(w_gate_up[0])], ("w_down", 0): [bslot(w_down[0])],
        ("w_in", 1): [bslot(w_in_b[0])],
        ("w_out", 1): [bslot(w_out[1])], ("w_gu", 1): [bslot(w_gate_up[1])], ("w_down", 1): [bslot(w_down[1])],
    }
    flights = dict(zip(groups, gather_start(list(groups.values()))))
    P = {"rel_bias": rel_bias, "sinks": sinks_a[0], "a_log": a_log_b[0], "dt_bias": dt_bias_b[0],
         "out_norm_g": out_norm_g_b[0], "g_mix": norm_mix_g, "g_mem": norm_mem_g, "g_ffn": norm_ffn_g,
         "g_fin": final_norm_g, "ffn_cb": [ffn_conv_b[0], ffn_conv_b[1]], "w_mem": [None, None], "w_out": [None, None],
         "w_gu": [None, None], "w_down": [None, None], "ffn_cw": [None, None]}

    def rows4(g):
        return g.reshape(4 * g.shape[1], g.shape[2])

    def arrive(key, after):
        if key not in flights:
            return
        got = gather_wait(flights.pop(key), after, "gather_wait_%s%d" % key)
        name, i = key
        if name == "w_in":
            P["w_in_a" if i == 0 else "w_in_b"] = (_lay_in_a if i == 0 else _lay_in_b)(_unchip_cols(got[0]))
        elif name == "w_mem":
            P["w_mem"] = [rows4(got[0]), rows4(got[1])]
            P["conv_qkv"] = _unchip_cols(got[2])
            cw = _unchip_cols(got[3]).reshape(2, 3, D_FF)
            P["ffn_cw"] = [cw[0], cw[1]]
        elif name == "w_out":
            P["w_out"][i] = _lay_out_a(rows4(got[0])) if i == 0 else rows4(got[0])
        elif name == "w_gu":
            P["w_gu"][i] = got[0]
        else:
            P["w_down"][i] = rows4(got[0])

    def chip_rows(g):
        return g.reshape(4, g.shape[0] // 4, g.shape[-1])

    sent, started = {}, []

    def ready(key, G, dep):
        kind, i = key
        tag = "%s%d" % key
        if kind == "ffn":
            names, partial = ("gu", "down"), [G["w_gu"][i], chip_rows(G["w_down"][i]).astype(bf16)]
        else:
            g_out = _unlay_out_a(G["w_out"][0]) if i == 0 else G["w_out"][1]
            g_in = _unlay_in_a(G["w_in_a"]) if i == 0 else _unlay_in_b(G["w_in_b"])
            names = ("out", "in", "mem")
            partial = [chip_rows(g_out).astype(bf16), _chip_cols(g_in).astype(bf16), chip_rows(G["w_mem"][i]).astype(bf16)]
        theirs = pair_exchange(partial, "pair_exchange_" + tag)
        pair = [pair_sum(p, t, core, "pair_sum_%s%d" % (nm, i)) for p, t, nm in zip(partial, theirs, names)]
        flight, token = scatter_start(pair, "scatter_start_" + tag)
        sent[key] = (names, flight, token)
        started.append(token[0, 0])
        if dep is not None:
            while started:
                dep = dep + started.pop()
        return dep

    P["arrive"], P["ready"] = arrive, ready

    loss, dx, G = _local_step(x[0], mem[0], loss_target[0], P)
    gfull = _grads_to_ref(G)

    fin, after = {}, sent["mix", 0][2]
    for key in (("ffn", 1), ("mix", 1), ("ffn", 0), ("mix", 0)):
        names, flight, _ = sent[key]
        pair, arrived = scatter_wait(flight, after, "scatter_wait_%s%d" % key)
        for nm, p, r in zip(names, pair, arrived):
            after = fin[nm, key[1]] = sum_slots(p, r, chip, core, "sum_slots_%s%d" % (nm, key[1]))
    order = list(fin)
    done = dict(zip(order, final_exchange([fin[k] for k in order])))

    sm_shapes = [A[n].shape for n in SMALL] + [gfull[n].shape for n, _ in CONV] + [(LANE,)]
    sm_rows = _pad_to(sum(_nrows(s, LANE) for s in sm_shapes), 8)
    sbuf = _pack([gfull[n] for n in SMALL] + [gfull[n] for n, _ in CONV] + [loss[0]], LANE, sm_rows, f32)
    tot = _unpack(allreduce_small(sbuf), sm_shapes, LANE)
    gsmall = dict(zip(SMALL, tot[:len(SMALL)]))
    for (n, axis), t in zip(CONV, tot[len(SMALL):len(SMALL) + len(CONV)]):
        sh = A[n].shape[axis]
        gsmall[n] = lax.dynamic_slice_in_dim(t, chip * sh, sh, axis)
    loss_out = tot[-1][0]

    out = {}
    plan = (("w_gate_up", [done["gu", 0], done["gu", 1]]), ("w_down", [done["down", 0], done["down", 1]]),
            ("w_out", [done["out", 0], done["out", 1]]), ("w_mem_kv", [done["mem", 0], done["mem", 1]]),
            ("w_in_a", [done["in", 0]]), ("w_in_b", [done["in", 1]]))
    for n, gs in plan:
        shape3 = (len(gs),) + gs[0].shape
        res = adamw_big(A[n].reshape(shape3), A["m_" + n].reshape(shape3), A["v_" + n].reshape(shape3), gs, 0,
                        "adamw_" + n)
        for key, r in zip(("grad_", "delta_", "new_m_", "new_v_"), res):
            out[key + n] = r.reshape(A[n].shape)
    names = SMALL + tuple(n for n, _ in CONV)
    shapes = [A[n].shape for n in names]
    rows = _pad_to(sum(_nrows(s, LANE) for s in shapes), 8)
    packs = [_pack([src[n] for n in names], LANE, rows, f32)
             for src in ({n: A[n] for n in names}, {n: A["m_" + n] for n in names}, {n: A["v_" + n] for n in names}, gsmall)]
    res = adamw_small(*packs)
    for key, r in zip(("delta_", "new_m_", "new_v_"), res):
        for n, a in zip(names, _unpack(r, shapes, LANE)):
            out[key + n] = a
    for n in names:
        out["grad_" + n] = gsmall[n]
    return (loss_out, dx[None], *[out["grad_" + n] for n in WEIGHTS], *[out["delta_" + n] for n in WEIGHTS],
            *[out["new_m_" + n] for n in WEIGHTS], *[out["new_v_" + n] for n in WEIGHTS])
```

```python
import functools
import math

import numpy as np
import jax
import jax.numpy as jnp
from jax import lax
from jax.experimental import pallas as pl
from jax.experimental.pallas import tpu as pltpu

f32 = jnp.float32
bf16 = jnp.bfloat16
HI = lax.Precision.HIGHEST
MESH = pl.DeviceIdType.MESH

D = 1024
MEM_LEN = 256
EPS = 1e-6
A_HEADS, A_KV, A_DH = 12, 2, 64
A_Q = 768
BLK = 128
N_BUCKETS, MAX_DIST = 32, 128
B_QK, B_V, B_DH = 384, 768, 128
B_QKV = 1536
CHUNK = 64
X_Q = 256
D_FF = 2816
LANE = 128
VMEM_LIMIT = 56 * 1024 * 1024
MM_ROWS = 1024

LR, B1, B2, AEPS, WD, STEP = 0.001, 0.9, 0.999, 1e-08, 0.01, 10


def _cp(sem=None):
    return pltpu.CompilerParams(dimension_semantics=sem, vmem_limit_bytes=VMEM_LIMIT)


def _dg(a, b, ca, cb, prec=None):
    return lax.dot_general(a, b, (((ca,), (cb,)), ((), ())), precision=prec, preferred_element_type=f32)


@jax.custom_vjp
def bdot(a, b):
    return _dg(a.astype(bf16), b.astype(bf16), 1, 0)


def _bdot_f(a, b):
    return bdot(a, b), (a, b)


def _bdot_b(res, g):
    a, b = res
    gb = g.astype(bf16)
    return _dg(gb, b.astype(bf16), 1, 1), _dg(a.astype(bf16), gb, 0, 0)


bdot.defvjp(_bdot_f, _bdot_b)


@jax.custom_vjp
def bdot_nt(a, b):
    return _dg(a.astype(bf16), b.astype(bf16), 1, 1)


def _bdot_nt_f(a, b):
    return bdot_nt(a, b), (a, b)


def _bdot_nt_b(res, g):
    a, b = res
    gb = g.astype(bf16)
    return _dg(gb, b.astype(bf16), 1, 0), _dg(gb, a.astype(bf16), 0, 0)


bdot_nt.defvjp(_bdot_nt_f, _bdot_nt_b)


@functools.partial(jax.custom_vjp, nondiff_argnums=(1,))
def shift_down(x, s):
    return pltpu.roll(x, s, 0)


def _sd_f(x, s):
    return pltpu.roll(x, s, 0), None


def _sd_b(s, _, g):
    return (pltpu.roll(g, g.shape[0] - s, 0),)


shift_down.defvjp(_sd_f, _sd_b)


def _sigmoid(x):
    return 1.0 / (1.0 + jnp.exp(-x))


def _silu(x):
    return x * _sigmoid(x)


def _rms(x, g):
    return x * lax.rsqrt(jnp.mean(x * x, axis=-1, keepdims=True) + EPS) * g


def _tile(n, cap):
    u = n // LANE
    best = 1
    for d in range(1, u + 1):
        if u % d == 0 and d * LANE <= cap:
            best = d
    return best * LANE


def mm_nn(a, w, res=None, out_dtype=f32, name="mm_nn"):
    M, K = a.shape
    N = w.shape[1]
    tm, tn = min(MM_ROWS, M), _tile(N, 1024)

    def body(*refs):
        if res is None:
            a_ref, w_ref, o_ref = refs
            o_ref[...] = _dg(a_ref[...].astype(bf16), w_ref[...], 1, 0).astype(out_dtype)
        else:
            a_ref, w_ref, r_ref, o_ref = refs
            o_ref[...] = (r_ref[...] + _dg(a_ref[...].astype(bf16), w_ref[...], 1, 0)).astype(out_dtype)

    in_specs = [pl.BlockSpec((tm, K), lambda n, m: (m, 0)), pl.BlockSpec((K, tn), lambda n, m: (0, n))]
    args = [a, w]
    if res is not None:
        in_specs.append(pl.BlockSpec((tm, tn), lambda n, m: (m, n)))
        args.append(res)
    return pl.pallas_call(
        body, name=name, grid=(N // tn, M // tm), in_specs=in_specs,
        out_specs=pl.BlockSpec((tm, tn), lambda n, m: (m, n)),
        out_shape=jax.ShapeDtypeStruct((M, N), out_dtype),
        compiler_params=_cp(("parallel", "parallel")),
    )(*args)


def mm_res_norm(a, w, res, g, name):
    pieces = a if isinstance(a, tuple) else (a,)
    na = len(pieces)
    M, K = pieces[0].shape[0], sum(p.shape[1] for p in pieces)
    tm = min(MM_ROWS, M)

    def body(*refs):
        w_ref, r_ref, g_ref, o_ref, n_ref = refs[na:]
        h = r_ref[...] + _dg(_cols(refs[:na]).astype(bf16), w_ref[...], 1, 0)
        o_ref[...] = h
        n_ref[...] = _rms(h, g_ref[...]).astype(bf16)

    tok = pl.BlockSpec((tm, D), lambda m: (m, 0))
    return pl.pallas_call(
        body, name=name, grid=(M // tm,),
        in_specs=[pl.BlockSpec((tm, p.shape[1]), lambda m: (m, 0)) for p in pieces]
        + [pl.BlockSpec((K, D), lambda m: (0, 0)), tok, pl.BlockSpec((1, D), lambda m: (0, 0))],
        out_specs=[tok, tok],
        out_shape=[jax.ShapeDtypeStruct((M, D), f32), jax.ShapeDtypeStruct((M, D), bf16)],
        compiler_params=_cp(("parallel",)),
    )(*pieces, w, res, g.reshape(1, D))


def mm_nt(dy, w, out_dtype=f32, name="mm_nt"):
    M, N = dy.shape
    K = w.shape[0]
    tm, tn = min(MM_ROWS, M), _tile(N, 1024)
    assert out_dtype == f32 or tn == N

    def body(dy_ref, w_ref, o_ref):
        part = _dg(dy_ref[...].astype(bf16), w_ref[...], 1, 1)
        if tn == N:
            o_ref[...] = part.astype(out_dtype)
        else:
            @pl.when(pl.program_id(1) == 0)
            def _():
                o_ref[...] = jnp.zeros_like(o_ref)
            o_ref[...] += part

    return pl.pallas_call(
        body, name=name, grid=(M // tm, N // tn),
        in_specs=[pl.BlockSpec((tm, tn), lambda m, n: (m, n)), pl.BlockSpec((K, tn), lambda m, n: (0, n))],
        out_specs=pl.BlockSpec((tm, K), lambda m, n: (m, 0)),
        out_shape=jax.ShapeDtypeStruct((M, K), out_dtype),
        compiler_params=_cp(("parallel", "arbitrary")),
    )(dy, w)


NORM_ROWS = 1024


def _acc_then_norm_bwd(part, steps, h_ref, g_ref, r_ref, o_ref, dg_ref):
    k = pl.program_id(1)

    @pl.when((pl.program_id(0) == 0) & (k == 0))
    def _():
        dg_ref[...] = jnp.zeros_like(dg_ref)

    @pl.when(k == 0)
    def _():
        o_ref[...] = part

    @pl.when(k > 0)
    def _():
        o_ref[...] += part

    @pl.when(k == steps - 1)
    def _():
        _, vjp = jax.vjp(_rms, h_ref[...], g_ref[...])
        dh, dg = vjp(o_ref[...])
        o_ref[...] = r_ref[...] + dh
        dg_ref[...] += dg


def mm_nt_norm(dy, w, norm, name):
    pieces = dy if isinstance(dy, tuple) else (dy,)
    nd = len(pieces)
    M, N = pieces[0].shape[0], sum(p.shape[1] for p in pieces)
    tm, tn = (min(NORM_ROWS, M), _tile(N, 1024)) if nd == 1 else (min(512, M), N)

    def body(*refs):
        w_ref, h_ref, g_ref, r_ref, o_ref, dg_ref = refs[nd:]
        _acc_then_norm_bwd(_dg(_cols(refs[:nd]).astype(bf16), w_ref[...], 1, 1), N // tn, h_ref, g_ref, r_ref, o_ref,
                           dg_ref)

    tok = pl.BlockSpec((tm, D), lambda m, n: (m, 0))
    vec = pl.BlockSpec((1, D), lambda m, n: (0, 0))
    return pl.pallas_call(
        body, name=name, grid=(M // tm, N // tn),
        in_specs=[pl.BlockSpec((tm, tn if nd == 1 else p.shape[1]), lambda m, n: (m, n)) for p in pieces]
        + [pl.BlockSpec((D, tn), lambda m, n: (0, n)), tok, vec, tok],
        out_specs=[tok, vec],
        out_shape=[jax.ShapeDtypeStruct((M, D), f32), jax.ShapeDtypeStruct((1, D), f32)],
        compiler_params=_cp(("arbitrary", "arbitrary")),
    )(*pieces, w, norm[0], norm[1].reshape(1, D), norm[2])


def _cols(refs):
    return refs[0][...] if len(refs) == 1 else jnp.concatenate([r[...] for r in refs], axis=1)


def mm_tn(a, dy, name="mm_tn"):
    pieces = a if isinstance(a, tuple) else (a,)
    dpieces = dy if isinstance(dy, tuple) else (dy,)
    na, nd = len(pieces), len(dpieces)
    M, K = pieces[0].shape[0], sum(p.shape[1] for p in pieces)
    N = sum(p.shape[1] for p in dpieces)
    tm = min(MM_ROWS, M)
    tk = _tile(K, 1408) if na == 1 else K
    tn = _tile(N, 1024) if nd == 1 else N

    def body(*refs):
        o_ref = refs[-1]

        @pl.when(pl.program_id(2) == 0)
        def _():
            o_ref[...] = jnp.zeros_like(o_ref)
        o_ref[...] += _dg(_cols(refs[:na]).astype(bf16), _cols(refs[na:na + nd]).astype(bf16), 0, 0)

    a_specs = [pl.BlockSpec((tm, tk if na == 1 else p.shape[1]), lambda k, n, m: (m, k)) for p in pieces]
    d_specs = [pl.BlockSpec((tm, tn if nd == 1 else p.shape[1]), lambda k, n, m: (m, n)) for p in dpieces]
    return pl.pallas_call(
        body, name=name, grid=(K // tk, N // tn, M // tm), in_specs=a_specs + d_specs,
        out_specs=pl.BlockSpec((tk, tn), lambda k, n, m: (k, n)),
        out_shape=jax.ShapeDtypeStruct((K, N), f32),
        compiler_params=_cp(("parallel", "parallel", "arbitrary")),
    )(*pieces, *dpieces)


def rms_fwd(h, g, name):
    S = h.shape[0]
    t = min(512, S)

    def body(h_ref, g_ref, o_ref):
        o_ref[...] = _rms(h_ref[...], g_ref[...]).astype(bf16)

    return pl.pallas_call(
        body, name=name, grid=(S // t,),
        in_specs=[pl.BlockSpec((t, D), lambda i: (i, 0)), pl.BlockSpec((1, D), lambda i: (0, 0))],
        out_specs=pl.BlockSpec((t, D), lambda i: (i, 0)),
        out_shape=jax.ShapeDtypeStruct((S, D), bf16),
        compiler_params=_cp(("parallel",)),
    )(h, g.reshape(1, D))


def loss_head(h, g, target):
    S = h.shape[0]
    t = min(512, S)

    def f(hh, gg, tt):
        err = _rms(hh, gg) - tt
        return 0.5 * jnp.sum(jnp.mean(err * err, axis=-1, keepdims=True), axis=0, keepdims=True)

    def body(h_ref, g_ref, t_ref, loss_ref, dh_ref, dg_ref):
        @pl.when(pl.program_id(0) == 0)
        def _():
            dg_ref[...] = jnp.zeros_like(dg_ref)
            loss_ref[...] = jnp.zeros_like(loss_ref)
        val, vjp = jax.vjp(lambda a, b: f(a, b, t_ref[...]), h_ref[...], g_ref[...])
        dh, dg = vjp(jnp.ones((1, 1), f32))
        dh_ref[...] = dh
        dg_ref[...] += dg
        loss_ref[...] += jnp.broadcast_to(val, loss_ref.shape)

    tok = pl.BlockSpec((t, D), lambda i: (i, 0))
    vec = pl.BlockSpec((1, D), lambda i: (0, 0))
    return pl.pallas_call(
        body, name="loss_head", grid=(S // t,), in_specs=[tok, vec, tok],
        out_specs=[pl.BlockSpec((1, LANE), lambda i: (0, 0)), tok, vec],
        out_shape=[jax.ShapeDtypeStruct((1, LANE), f32), jax.ShapeDtypeStruct((S, D), f32),
                   jax.ShapeDtypeStruct((1, D), f32)],
        compiler_params=_cp(("arbitrary",)),
    )(h, g.reshape(1, D), target)


def memkv_fwd(mem, g, w, name):
    def body(m_ref, g_ref, w_ref, o_ref):
        o_ref[...] = _dg(_rms(m_ref[...], g_ref[...]).astype(bf16), w_ref[...], 1, 0)

    return pl.pallas_call(
        body, name=name, out_shape=jax.ShapeDtypeStruct((MEM_LEN, 2 * X_Q), f32), compiler_params=_cp(),
    )(mem, g.reshape(1, D), w)


def memkv_bwd(mem, g, w, dkv, name):
    def body(m_ref, g_ref, w_ref, d_ref, dg_ref, dw_ref):
        n, vjp = jax.vjp(lambda gg: _rms(m_ref[...], gg), g_ref[...])
        db = d_ref[...].astype(bf16)
        dw_ref[...] = _dg(n.astype(bf16), db, 0, 0)
        dg_ref[...] = vjp(_dg(db, w_ref[...], 1, 1))[0]

    return pl.pallas_call(
        body, name=name,
        out_shape=[jax.ShapeDtypeStruct((1, D), f32), jax.ShapeDtypeStruct((D, 2 * X_Q), f32)],
        compiler_params=_cp(),
    )(mem, g.reshape(1, D), w, dkv)


def _xattn_f(xq, mk, mv):
    lane = lax.broadcasted_iota(jnp.int32, (1, X_Q), 1)
    out = jnp.zeros(xq.shape, f32)
    for hd in range(4):
        msk = (lane // 64 == hd).astype(f32)
        s = bdot_nt(xq * msk, mk) * (64 ** -0.5)
        m = lax.stop_gradient(jnp.max(s, axis=-1, keepdims=True))
        p = jnp.exp(s - m)
        p = p / jnp.sum(p, axis=-1, keepdims=True)
        out = out + bdot(p, mv * msk)
    return out


def xattn_fwd(proj, col, kv, name):
    S = proj.shape[0]
    t = min(512, S)
    cb = col // X_Q

    def body(q_ref, k_ref, v_ref, o_ref):
        o_ref[...] = _xattn_f(q_ref[...], k_ref[...], v_ref[...]).astype(bf16)

    return pl.pallas_call(
        body, name=name, grid=(S // t,),
        in_specs=[pl.BlockSpec((t, X_Q), lambda i: (i, cb)), pl.BlockSpec((MEM_LEN, X_Q), lambda i: (0, 0)),
                  pl.BlockSpec((MEM_LEN, X_Q), lambda i: (0, 1))],
        out_specs=pl.BlockSpec((t, X_Q), lambda i: (i, 0)),
        out_shape=jax.ShapeDtypeStruct((S, X_Q), bf16),
        compiler_params=_cp(("parallel",)),
    )(proj, kv, kv)


def xattn_bwd(proj, col, kv, dmix, name):
    S = proj.shape[0]
    t = min(512, S)
    cb = col // X_Q

    def body(q_ref, k_ref, v_ref, do_ref, dq_ref, dk_ref, dv_ref):
        @pl.when(pl.program_id(0) == 0)
        def _():
            dk_ref[...] = jnp.zeros_like(dk_ref)
            dv_ref[...] = jnp.zeros_like(dv_ref)
        _, vjp = jax.vjp(_xattn_f, q_ref[...], k_ref[...], v_ref[...])
        dq, dk, dv = vjp(do_ref[...])
        dq_ref[...] = dq.astype(bf16)
        dk_ref[...] += dk
        dv_ref[...] += dv

    kvb = pl.BlockSpec((MEM_LEN, X_Q), lambda i: (0, 0))
    dq, dk, dv = pl.pallas_call(
        body, name=name, grid=(S // t,),
        in_specs=[pl.BlockSpec((t, X_Q), lambda i: (i, cb)), kvb,
                  pl.BlockSpec((MEM_LEN, X_Q), lambda i: (0, 1)), pl.BlockSpec((t, X_Q), lambda i: (i, 3))],
        out_specs=[pl.BlockSpec((t, X_Q), lambda i: (i, 0)), kvb, kvb],
        out_shape=[jax.ShapeDtypeStruct((S, X_Q), bf16), jax.ShapeDtypeStruct((MEM_LEN, X_Q), f32),
                   jax.ShapeDtypeStruct((MEM_LEN, X_Q), f32)],
        compiler_params=_cp(("arbitrary",)),
    )(proj, kv, kv, dmix)
    return dq, jnp.concatenate([dk, dv], axis=1)


def _bucket_map():
    qi = np.arange(BLK)[:, None]
    kj = np.arange(2 * BLK)[None, :]
    n = np.maximum(BLK + qi - kj, 0)
    max_exact = N_BUCKETS // 2
    nf = np.maximum(n, 1).astype(np.float64)
    large = max_exact + (np.log(nf / max_exact) / math.log(MAX_DIST / max_exact)
                         * (N_BUCKETS - max_exact)).astype(np.int32)
    large = np.minimum(large, N_BUCKETS - 1)
    return np.where(n < max_exact, n, large).astype(np.int32)


def bias_build(rel_bias):
    def body(rb_ref, bk_ref, o_ref):
        bk = bk_ref[...]
        for h in range(A_HEADS):
            acc = jnp.zeros((BLK, 2 * BLK), f32)
            for b in range(N_BUCKETS):
                acc = jnp.where(bk == b, rb_ref[b, h], acc)
            o_ref[h] = acc

    return pl.pallas_call(
        body, name="bias_build",
        in_specs=[pl.BlockSpec(memory_space=pltpu.SMEM), pl.BlockSpec(memory_space=pltpu.VMEM)],
        out_specs=pl.BlockSpec(memory_space=pltpu.VMEM),
        out_shape=jax.ShapeDtypeStruct((A_HEADS, BLK, 2 * BLK), f32), compiler_params=_cp(),
    )(rel_bias, jnp.asarray(_bucket_map()))


def bias_grad(dbias):
    def body(d_ref, bk_ref, o_ref):
        bk = bk_ref[...]
        row = lax.broadcasted_iota(jnp.int32, (N_BUCKETS, LANE), 0)
        lane = lax.broadcasted_iota(jnp.int32, (N_BUCKETS, LANE), 1)
        acc = jnp.zeros((N_BUCKETS, LANE), f32)
        for h in range(A_HEADS):
            d = d_ref[h]
            for b in range(N_BUCKETS):
                s = jnp.sum(jnp.where(bk == b, d, 0.0), keepdims=True)
                acc = acc + jnp.where((row == b) & (lane == h), s, 0.0)
        o_ref[...] = acc

    return pl.pallas_call(
        body, name="bias_grad", out_shape=jax.ShapeDtypeStruct((N_BUCKETS, LANE), f32), compiler_params=_cp(),
    )(dbias, jnp.asarray(_bucket_map()))


def _swa_f(qb, kp, kc, vp, vc, bias, sk, first):
    kband = jnp.concatenate([kp, kc], axis=0)
    vband = jnp.concatenate([vp, vc], axis=0)
    qi = lax.broadcasted_iota(jnp.int32, (BLK, 2 * BLK), 0)
    kj = lax.broadcasted_iota(jnp.int32, (BLK, 2 * BLK), 1)
    rel = kj - qi
    ok = (rel >= 1) & (rel <= BLK) & ((kj >= BLK) | jnp.logical_not(first))
    lane = lax.broadcasted_iota(jnp.int32, (1, LANE), 1)
    lane_b = lax.broadcasted_iota(jnp.int32, (BLK, LANE), 1)
    outs = []
    for p in range(A_HEADS // 2):
        qp = qb[:, LANE * p:LANE * (p + 1)]
        acc = jnp.zeros((BLK, LANE), f32)
        for g in range(2):
            h = g * (A_HEADS // 2) + p
            msk = (lane // A_DH == g).astype(f32)
            s = bdot_nt(qp * msk, kband) * (A_DH ** -0.5) + bias[h]
            s = jnp.where(ok, s, -1e30)
            skb = jnp.broadcast_to(sk[h:h + 1, :], (BLK, LANE))
            sink = jnp.sum(jnp.where(lane_b == 0, skb, 0.0), axis=-1, keepdims=True)
            m = lax.stop_gradient(jnp.maximum(jnp.max(s, axis=-1, keepdims=True), sink))
            e = jnp.exp(s - m)
            prob = e / (jnp.sum(e, axis=-1, keepdims=True) + jnp.exp(sink - m))
            acc = acc + bdot(prob, vband) * msk
        outs.append(acc)
    return jnp.concatenate(outs, axis=1)


def _swa_specs(nb, rev):
    bi = (lambda i: nb - 1 - i) if rev else (lambda i: i)
    return [
        pl.BlockSpec((BLK, A_Q), lambda i: (bi(i), 0)),
        pl.BlockSpec((BLK, LANE), lambda i: (jnp.maximum(bi(i) - 1, 0), 6)),
        pl.BlockSpec((BLK, LANE), lambda i: (bi(i), 6)),
        pl.BlockSpec((BLK, LANE), lambda i: (jnp.maximum(bi(i) - 1, 0), 7)),
        pl.BlockSpec((BLK, LANE), lambda i: (bi(i), 7)),
        pl.BlockSpec((A_HEADS, BLK, 2 * BLK), lambda i: (0, 0, 0)),
        pl.BlockSpec((16, LANE), lambda i: (0, 0)),
    ]


def swa_fwd(proj, bias, sk):
    S = proj.shape[0]
    nb = S // BLK

    def body(q_ref, kp_ref, kc_ref, vp_ref, vc_ref, b_ref, s_ref, o_ref):
        o_ref[...] = _swa_f(q_ref[...], kp_ref[...], kc_ref[...], vp_ref[...], vc_ref[...], b_ref[...], s_ref[...],
                            pl.program_id(0) == 0).astype(bf16)

    return pl.pallas_call(
        body, name="swa_fwd", grid=(nb,), in_specs=_swa_specs(nb, False),
        out_specs=pl.BlockSpec((BLK, A_Q), lambda i: (i, 0)),
        out_shape=jax.ShapeDtypeStruct((S, A_Q), bf16), compiler_params=_cp(("parallel",)),
    )(proj, proj, proj, proj, proj, bias, sk)


def swa_bwd(proj, bias, sk, dmix):
    S = proj.shape[0]
    nb = S // BLK

    def body(q_ref, kp_ref, kc_ref, vp_ref, vc_ref, b_ref, s_ref, do_ref, dqkv_ref, db_ref, ds_ref, ck, cv):
        i = pl.program_id(0)

        @pl.when(i == 0)
        def _():
            db_ref[...] = jnp.zeros_like(db_ref)
            ds_ref[...] = jnp.zeros_like(ds_ref)
            ck[...] = jnp.zeros_like(ck)
            cv[...] = jnp.zeros_like(cv)
        first = i == nb - 1
        _, vjp = jax.vjp(lambda *a: _swa_f(*a, first), q_ref[...], kp_ref[...], kc_ref[...], vp_ref[...],
                         vc_ref[...], b_ref[...], s_ref[...])
        dq, dkp, dkc, dvp, dvc, db, ds = vjp(do_ref[...])
        dqkv_ref[...] = jnp.concatenate([dq, dkc + ck[...], dvc + cv[...]], axis=1).astype(bf16)
        ck[...] = dkp
        cv[...] = dvp
        db_ref[...] += db
        ds_ref[...] += ds

    return pl.pallas_call(
        body, name="swa_bwd", grid=(nb,),
        in_specs=_swa_specs(nb, True) + [pl.BlockSpec((BLK, A_Q), lambda i: (nb - 1 - i, 0))],
        out_specs=[pl.BlockSpec((BLK, D), lambda i: (nb - 1 - i, 0)),
                   pl.BlockSpec((A_HEADS, BLK, 2 * BLK), lambda i: (0, 0, 0)),
                   pl.BlockSpec((16, LANE), lambda i: (0, 0))],
        out_shape=[jax.ShapeDtypeStruct((S, D), bf16), jax.ShapeDtypeStruct((A_HEADS, BLK, 2 * BLK), f32),
                   jax.ShapeDtypeStruct((16, LANE), f32)],
        scratch_shapes=[pltpu.VMEM((BLK, LANE), f32), pltpu.VMEM((BLK, LANE), f32)],
        compiler_params=_cp(("arbitrary",)),
    )(proj, proj, proj, proj, proj, bias, sk, dmix)


def _dnprep_f(xext, w, is_qk):
    c = (w[3:4] * xext + w[2:3] * shift_down(xext, 1) + w[1:2] * shift_down(xext, 2) + w[0:1] * shift_down(xext, 3))
    a = _silu(c)[HALO:]
    n = a * lax.rsqrt(jnp.sum(a * a, axis=-1, keepdims=True) + EPS)
    return jnp.where(is_qk, n, a)


def dnprep_fwd(proj, cw):
    S = proj.shape[0]
    nblk = B_QKV // LANE
    T = S

    def body(x_ref, w_ref, o_ref):
        is_qk = pl.program_id(0) < 2 * B_QK // LANE
        wv = w_ref[...]

        def tile(r0, first):
            o_ref[pl.ds(r0, T), :] = _dnprep_f(_glu_gext(x_ref, r0, first, T), wv, is_qk)

        tile(0, True)

    return pl.pallas_call(
        body, name="dnprep_fwd", grid=(nblk,),
        in_specs=[pl.BlockSpec((S, LANE), lambda j: (0, j)), pl.BlockSpec((4, LANE), lambda j: (0, j))],
        out_specs=pl.BlockSpec((S, LANE), lambda j: (0, j)),
        out_shape=jax.ShapeDtypeStruct((S, B_QKV), f32), compiler_params=_cp(("parallel",)),
    )(proj, cw)


def dnprep_bwd(proj, cw, dqkvn):
    S = proj.shape[0]
    nblk = B_QKV // LANE

    T = S

    def body(x_ref, w_ref, d_ref, dx_ref, dw_ref):
        is_qk = pl.program_id(0) < 2 * B_QK // LANE
        wv = w_ref[...]

        def tile(r0, first):
            _, vjp = jax.vjp(lambda a, b: _dnprep_f(a, b, is_qk), _glu_gext(x_ref, r0, first, T), wv)
            dx, dw = vjp(d_ref[pl.ds(r0, T), :])
            dx_ref[pl.ds(r0, T), :] = dx[HALO:].astype(bf16)
            if not first:
                dx_ref[pl.ds(r0 - HALO, HALO), :] += dx[:HALO]
            return dw

        dw_ref[...] = tile(0, True)

    col = pl.BlockSpec((S, LANE), lambda j: (0, j))
    wsp = pl.BlockSpec((4, LANE), lambda j: (0, j))
    return pl.pallas_call(
        body, name="dnprep_bwd", grid=(nblk,), in_specs=[col, wsp, col], out_specs=[col, wsp],
        out_shape=[jax.ShapeDtypeStruct((S, B_QKV), bf16), jax.ShapeDtypeStruct((4, B_QKV), f32)],
        compiler_params=_cp(("parallel",)),
    )(proj, cw, dqkvn)


def _hdot(a, b, ca=1, cb=0):
    return _dg(a, b, ca, cb, HI)


def _bdg(a, b, ca, cb):
    dn = (((ca,), (cb,)), ((0,), (0,)))
    ah, bh = a.astype(bf16), b.astype(bf16)
    al, bl = (a - ah.astype(f32)).astype(bf16), (b - bh.astype(f32)).astype(bf16)
    return (lax.dot_general(ah, bh, dn, preferred_element_type=f32)
            + lax.dot_general(ah, bl, dn, preferred_element_type=f32)
            + lax.dot_general(al, bh, dn, preferred_element_type=f32))


@jax.custom_vjp
def hbd(a, b):
    return _bdg(a, b, 2, 1)


@jax.custom_vjp
def hbd_nt(a, b):
    return _bdg(a, b, 2, 2)


@jax.custom_vjp
def hbd_tn(a, b):
    return _bdg(a, b, 1, 1)


hbd.defvjp(lambda a, b: (hbd(a, b), (a, b)), lambda r, g: (hbd_nt(g, r[1]), hbd_tn(r[0], g)))
hbd_nt.defvjp(lambda a, b: (hbd_nt(a, b), (a, b)), lambda r, g: (hbd(g, r[1]), hbd_tn(g, r[0])))
hbd_tn.defvjp(lambda a, b: (hbd_tn(a, b), (a, b)), lambda r, g: (hbd_nt(r[1], g), hbd(r[0], g)))


def _stack(xs):
    return jnp.concatenate([x[None] for x in xs], axis=0)


def _lane_col(x, j):
    lane = lax.broadcasted_iota(jnp.int32, (1, LANE), 1)
    return jnp.sum(jnp.where(lane == j, x, 0.0), axis=-1, keepdims=True)


def _tri_inv(a_mat):
    r = lax.broadcasted_iota(jnp.int32, (1, CHUNK, CHUNK), 1)
    c = lax.broadcasted_iota(jnp.int32, (1, CHUNK, CHUNK), 2)
    pw = -a_mat
    inv = (r == c).astype(f32) + pw
    for _ in range(5):
        pw = hbd(pw, pw)
        inv = inv + hbd(inv, pw)
    return inv


@jax.custom_vjp
def _tri_inv_known(a_mat, inv):
    return inv


_tri_inv_known.defvjp(lambda a, inv: (inv, inv),
                      lambda inv, g: (-hbd_tn(inv, hbd_nt(g, inv)), jnp.zeros_like(inv)))


def _dnc_f(q, k, v, seg, prm, inverse=_tri_inv):
    C = CHUNK
    B = q.shape[0]
    rows = seg.shape[0]
    beta_all = _sigmoid(seg)
    xx = seg + prm[1:2]
    g_all = -jnp.exp(prm[0:1]) * (jnp.maximum(xx, 0.0) + jnp.log(1.0 + jnp.exp(-jnp.abs(xx))))
    r2 = lax.broadcasted_iota(jnp.int32, (rows, rows), 0)
    c2 = lax.broadcasted_iota(jnp.int32, (rows, rows), 1)
    within = (r2 >= c2) & (r2 // C == c2 // C)
    gc_all = _hdot(within.astype(f32), g_all)
    beta = _stack([_lane_col(beta_all[C * j:C * (j + 1)], h) for j in range(rows // C) for h in range(6)])
    gc = _stack([_lane_col(gc_all[C * j:C * (j + 1)], 6 + h) for j in range(rows // C) for h in range(6)])
    r = lax.broadcasted_iota(jnp.int32, (1, C, C), 1)
    c = lax.broadcasted_iota(jnp.int32, (1, C, C), 2)
    incl = r >= c
    strict = r > c
    gct = [gc_all[C * j:C * (j + 1)].T for j in range(rows // C)]
    g_row = _stack([jnp.broadcast_to(gct[j][6 + h:7 + h, :], (C, C))
                    for j in range(rows // C) for h in range(6)])
    decay = jnp.where(incl, jnp.exp(jnp.where(incl, gc - g_row, 0.0)), 0.0)
    a_mat = beta * sbd_nt(k, k) * jnp.where(strict, decay, 0.0)
    eg = jnp.exp(gc)
    inv = inverse(a_mat)
    u = hbd(inv, beta * v)
    w = hbd(inv, (beta * eg) * k)
    qc = q * (B_DH ** -0.5)
    attn = sbd_nt(qc, k) * decay
    last = (lax.broadcasted_iota(jnp.int32, (1, C, 1), 1) == C - 1).astype(f32)
    g_last = jnp.sum(gc * last, axis=1, keepdims=True)
    dc = jnp.broadcast_to(jnp.exp(g_last), (B, 1, LANE)).reshape(B, LANE)
    return u, w, qc * eg, k * jnp.exp(g_last - gc), attn, dc, inv


def _b1(a, b, ca, cb):
    return lax.dot_general(a.astype(bf16), b.astype(bf16), (((ca,), (cb,)), ((0,), (0,))), preferred_element_type=f32)


@jax.custom_vjp
def sbd(a, b):
    return _b1(a, b, 2, 1)


@jax.custom_vjp
def sbd_nt(a, b):
    return _b1(a, b, 2, 2)


@jax.custom_vjp
def sbd_tn(a, b):
    return _b1(a, b, 1, 1)


sbd.defvjp(lambda a, b: (sbd(a, b), (a, b)), lambda r, g: (sbd_nt(g, r[1]), sbd_tn(r[0], g)))
sbd_nt.defvjp(lambda a, b: (sbd_nt(a, b), (a, b)), lambda r, g: (sbd(g, r[1]), sbd_tn(g, r[0])))
sbd_tn.defvjp(lambda a, b: (sbd_tn(a, b), (a, b)), lambda r, g: (sbd_nt(r[1], g), sbd(r[0], g)))


def _dns_f(S0, u, w, qd, kt, attn, dcrows):
    dc = _lane_col(dcrows, 0).reshape(6, 1, 1)
    delta = u - sbd(w, S0)
    out = sbd(qd, S0) + sbd(attn, delta)
    return out, dc * S0 + sbd_tn(kt, delta)


def _dnpost_f(o, z, grow):
    outs = []
    for h in range(6):
        oh = o[:, LANE * h:LANE * (h + 1)]
        outs.append(oh * lax.rsqrt(jnp.mean(oh * oh, axis=-1, keepdims=True) + EPS) * grow
                    * _silu(z[:, LANE * h:LANE * (h + 1)]))
    return jnp.concatenate(outs, axis=1)


def _hs(h):
    return slice(LANE * h, LANE * (h + 1))


DN_CHUNKS = 4


def _heads(ref, share):
    return _stack([ref[CHUNK * j:CHUNK * (j + 1), _hs(h // share)]
                   for j in range(ref.shape[0] // CHUNK) for h in range(6)])


def _put_heads(ref, val):
    for j in range(ref.shape[0] // CHUNK):
        for h in range(6):
            ref[CHUNK * j:CHUNK * (j + 1), _hs(h)] = val[6 * j + h]


def _dnc_in_specs():
    rows = CHUNK * DN_CHUNKS
    return [
        pl.BlockSpec((rows, B_QK), lambda n: (n, 0)),
        pl.BlockSpec((rows, B_QK), lambda n: (n, 1)),
        pl.BlockSpec((rows, B_V), lambda n: (n, 1)),
        pl.BlockSpec((rows, LANE), lambda n: (n, 20)),
        pl.BlockSpec((8, LANE), lambda n: (0, 0)),
    ]


def _dnc_out_specs(rev_nc=None, chunks=1):
    ci = (lambda n: n) if rev_nc is None else (lambda n: rev_nc - 1 - n)
    wide = pl.BlockSpec((CHUNK * chunks, B_V), lambda n: (ci(n), 0))
    return [wide, wide, wide, wide, pl.BlockSpec((chunks, 6, CHUNK, CHUNK), lambda n: (ci(n), 0, 0, 0)),
            pl.BlockSpec((chunks, 8, LANE), lambda n: (ci(n), 0, 0))]


def _dc_rows(dc):
    pad = jnp.zeros((2, LANE), f32)
    return _stack([jnp.concatenate([dc[6 * j:6 * (j + 1)], pad], axis=0) for j in range(dc.shape[0] // 6)])


def _dnc_shapes(S):
    nc = S // CHUNK
    wide = jax.ShapeDtypeStruct((S, B_V), f32)
    return [wide, wide, wide, wide, jax.ShapeDtypeStruct((nc, 6, CHUNK, CHUNK), f32),
            jax.ShapeDtypeStruct((nc, 8, LANE), f32)]


def dnc_fwd(qkvn, proj, prm):
    S = proj.shape[0]

    def body(q_ref, k_ref, v_ref, s_ref, p_ref, u_ref, w_ref, qd_ref, kt_ref, at_ref, dc_ref, inv_ref):
        u, w, qd, kt, attn, dc, inv = _dnc_f(_heads(q_ref, 2), _heads(k_ref, 2), _heads(v_ref, 1), s_ref[...],
                                             p_ref[...])
        inv_ref[...] = inv.reshape(inv_ref.shape)
        _put_heads(u_ref, u)
        _put_heads(w_ref, w)
        _put_heads(qd_ref, qd)
        _put_heads(kt_ref, kt)
        at_ref[...] = attn.reshape(at_ref.shape)
        dc_ref[...] = _dc_rows(dc)

    outs = _dnc_out_specs(chunks=DN_CHUNKS)
    out = pl.pallas_call(
        body, name="dn_chunk_fwd", grid=(S // (CHUNK * DN_CHUNKS),), in_specs=_dnc_in_specs(),
        out_specs=outs + [outs[4]], out_shape=_dnc_shapes(S) + [_dnc_shapes(S)[4]],
        compiler_params=_cp(("parallel",)),
    )(qkvn, qkvn, qkvn, proj, prm)
    return out[:6], out[6]


def dnc_bwd(qkvn, proj, prm, inv, cots):
    S = proj.shape[0]

    def body(q_ref, k_ref, v_ref, s_ref, p_ref, inv_ref, du_ref, dw_ref, dqd_ref, dkt_ref, dat_ref, ddc_ref,
             dx_ref, dseg_ref, dprm_ref):
        @pl.when(pl.program_id(0) == 0)
        def _():
            dprm_ref[...] = jnp.zeros_like(dprm_ref)
        nb = 6 * DN_CHUNKS
        known = functools.partial(_tri_inv_known, inv=inv_ref[...].reshape(nb, CHUNK, CHUNK))
        _, vjp = jax.vjp(lambda *a: _dnc_f(*a, inverse=known)[:6], _heads(q_ref, 2), _heads(k_ref, 2),
                         _heads(v_ref, 1), s_ref[...], p_ref[...])
        ddc = jnp.concatenate([ddc_ref[j, 0:6, :] for j in range(DN_CHUNKS)], axis=0)
        dq, dk, dv, dseg, dprm = vjp((_heads(du_ref, 1), _heads(dw_ref, 1), _heads(dqd_ref, 1), _heads(dkt_ref, 1),
                                      dat_ref[...].reshape(nb, CHUNK, CHUNK), ddc))
        for j in range(DN_CHUNKS):
            o = 6 * j
            dx_ref[CHUNK * j:CHUNK * (j + 1), :] = jnp.concatenate(
                [dq[o] + dq[o + 1], dq[o + 2] + dq[o + 3], dq[o + 4] + dq[o + 5],
                 dk[o] + dk[o + 1], dk[o + 2] + dk[o + 3], dk[o + 4] + dk[o + 5]] + [dv[o + h] for h in range(6)], axis=1)
        dseg_ref[...] = dseg.astype(bf16)
        dprm_ref[...] += dprm

    rows = CHUNK * DN_CHUNKS
    outs = _dnc_out_specs(chunks=DN_CHUNKS)
    return pl.pallas_call(
        body, name="dn_chunk_bwd", grid=(S // rows,),
        in_specs=_dnc_in_specs() + [outs[4]] + outs,
        out_specs=[pl.BlockSpec((rows, B_QKV), lambda n: (n, 0)), pl.BlockSpec((rows, LANE), lambda n: (n, 0)),
                   pl.BlockSpec((8, LANE), lambda n: (0, 0))],
        out_shape=[jax.ShapeDtypeStruct((S, B_QKV), f32), jax.ShapeDtypeStruct((S, LANE), bf16),
                   jax.ShapeDtypeStruct((8, LANE), f32)],
        compiler_params=_cp(("arbitrary",)),
    )(qkvn, qkvn, qkvn, proj, prm, inv, *cots)


def dns_fwd(chunked):
    u = chunked[0]
    S = u.shape[0]
    nc = S // CHUNK

    def body(u_ref, w_ref, qd_ref, kt_ref, at_ref, dc_ref, o_ref, st_ref, st):
        @pl.when(pl.program_id(0) == 0)
        def _():
            st[...] = jnp.zeros_like(st)
        S0 = st[...]
        st_ref[0] = S0
        out, S1 = _dns_f(S0, _heads(u_ref, 1), _heads(w_ref, 1), _heads(qd_ref, 1), _heads(kt_ref, 1),
                         at_ref[0], dc_ref[0, 0:6, :])
        _put_heads(o_ref, out)
        st[...] = S1

    return pl.pallas_call(
        body, name="dn_scan_fwd", grid=(nc,), in_specs=_dnc_out_specs(),
        out_specs=[pl.BlockSpec((CHUNK, B_V), lambda n: (n, 0)),
                   pl.BlockSpec((1, 6, B_DH, B_DH), lambda n: (n, 0, 0, 0))],
        out_shape=[jax.ShapeDtypeStruct((S, B_V), f32), jax.ShapeDtypeStruct((nc, 6, B_DH, B_DH), f32)],
        scratch_shapes=[pltpu.VMEM((6, B_DH, B_DH), f32)],
        compiler_params=_cp(("arbitrary",)),
    )(*chunked)


def dns_bwd(chunked, states, do):
    S = do.shape[0]
    nc = S // CHUNK

    def body(u_ref, w_ref, qd_ref, kt_ref, at_ref, dc_ref, st_ref, do_ref,
             du_ref, dw_ref, dqd_ref, dkt_ref, dat_ref, ddc_ref, dst):
        @pl.when(pl.program_id(0) == 0)
        def _():
            dst[...] = jnp.zeros_like(dst)
        _, vjp = jax.vjp(_dns_f, st_ref[0], _heads(u_ref, 1), _heads(w_ref, 1), _heads(qd_ref, 1), _heads(kt_ref, 1),
                         at_ref[0], dc_ref[0, 0:6, :])
        dS0, du, dw, dqd, dkt, dat, ddc = vjp((_heads(do_ref, 1), dst[...]))
        dst[...] = dS0
        _put_heads(du_ref, du)
        _put_heads(dw_ref, dw)
        _put_heads(dqd_ref, dqd)
        _put_heads(dkt_ref, dkt)
        dat_ref[0] = dat
        ddc_ref[0] = jnp.concatenate([ddc, jnp.zeros((2, LANE), f32)], axis=0)

    return pl.pallas_call(
        body, name="dn_scan_bwd", grid=(nc,),
        in_specs=_dnc_out_specs(nc) + [pl.BlockSpec((1, 6, B_DH, B_DH), lambda n: (nc - 1 - n, 0, 0, 0)),
                                       pl.BlockSpec((CHUNK, B_V), lambda n: (nc - 1 - n, 0))],
        out_specs=_dnc_out_specs(nc), out_shape=_dnc_shapes(S),
        scratch_shapes=[pltpu.VMEM((6, B_DH, B_DH), f32)],
        compiler_params=_cp(("arbitrary",)),
    )(*chunked, states, do)


def dnpost_fwd(o, proj, prm):
    S = o.shape[0]
    t = min(512, S)

    def body(o_ref, z_ref, p_ref, y_ref):
        y_ref[...] = _dnpost_f(o_ref[...], z_ref[...], p_ref[2:3, :]).astype(bf16)

    tok = pl.BlockSpec((t, B_V), lambda i: (i, 0))
    return pl.pallas_call(
        body, name="dn_post_fwd", grid=(S // t,),
        in_specs=[tok, pl.BlockSpec((t, B_V), lambda i: (i, 2)), pl.BlockSpec((8, LANE), lambda i: (0, 0))],
        out_specs=tok, out_shape=jax.ShapeDtypeStruct((S, B_V), bf16), compiler_params=_cp(("parallel",)),
    )(o, proj, prm)


def dnpost_bwd(o, proj, prm, dmix):
    S = o.shape[0]
    t = min(512, S)

    def body(o_ref, z_ref, p_ref, dy_ref, do_ref, dz_ref, dg_ref):
        @pl.when(pl.program_id(0) == 0)
        def _():
            dg_ref[...] = jnp.zeros_like(dg_ref)
        _, vjp = jax.vjp(_dnpost_f, o_ref[...], z_ref[...], p_ref[2:3, :])
        do, dz, dg = vjp(dy_ref[...])
        do_ref[...] = do
        dz_ref[...] = dz.astype(bf16)
        dg_ref[...] += dg

    tok = pl.BlockSpec((t, B_V), lambda i: (i, 0))
    return pl.pallas_call(
        body, name="dn_post_bwd", grid=(S // t,),
        in_specs=[tok, pl.BlockSpec((t, B_V), lambda i: (i, 2)), pl.BlockSpec((8, LANE), lambda i: (0, 0)), tok],
        out_specs=[tok, tok, pl.BlockSpec((1, LANE), lambda i: (0, 0))],
        out_shape=[jax.ShapeDtypeStruct((S, B_V), f32), jax.ShapeDtypeStruct((S, B_V), bf16),
                   jax.ShapeDtypeStruct((1, LANE), f32)],
        compiler_params=_cp(("arbitrary",)),
    )(o, proj, prm, dmix)


N_FF_BLK = D_FF // LANE
GU_SHARD = 2 * D_FF // 4


GLU_ROWS = 256
HALO = 16


def _glu_conv(gext, w, b):
    return (w[2:3] * gext + w[1:2] * shift_down(gext, 1) + w[0:1] * shift_down(gext, 2) + b)[HALO:]


def _glu_gate(c, up):
    return _silu(c) * up


def _glu_gext(g_ref, r0, first, T=GLU_ROWS):
    if first:
        return jnp.concatenate([jnp.zeros((HALO, LANE), f32), g_ref[0:T, :].astype(f32)], axis=0)
    return g_ref[pl.ds(r0 - HALO, T + HALO), :].astype(f32)


def glu_fwd(gu, w, b, name):
    S = gu.shape[0]
    T = min(GLU_ROWS, S // 2)

    def body(g_ref, u_ref, w_ref, b_ref, o_ref, c_ref):
        wv, bv = w_ref[...], b_ref[...]

        def tile(r0, first):
            c = _glu_conv(_glu_gext(g_ref, r0, first, T), wv, bv)
            c_ref[pl.ds(r0, T), :] = c.astype(bf16)
            o_ref[pl.ds(r0, T), :] = _glu_gate(c, u_ref[pl.ds(r0, T), :].astype(f32)).astype(bf16)

        tile(0, True)

        @pl.loop(1, S // T)
        def _(t):
            tile(pl.multiple_of(t * T, T), False)

    col = pl.BlockSpec((S, LANE), lambda j: (0, j))
    return pl.pallas_call(
        body, name=name, grid=(N_FF_BLK,),
        in_specs=[col, pl.BlockSpec((S, LANE), lambda j: (0, N_FF_BLK + j)), pl.BlockSpec((3, LANE), lambda j: (0, j)),
                  pl.BlockSpec((1, LANE), lambda j: (0, j))],
        out_specs=[col, col], out_shape=[jax.ShapeDtypeStruct((S, D_FF), bf16)] * 2,
        compiler_params=_cp(("parallel",)),
    )(gu, gu, w, b.reshape(1, D_FF))


def glu_bwd(gu, c, w, b, dact, name):
    S = gu.shape[0]
    T = min(GLU_ROWS, S // 2)

    def body(g_ref, u_ref, c_ref, w_ref, b_ref, d_ref, dg_ref, dw_ref, db_ref, acc):
        wv, bv = w_ref[...], b_ref[...]

        def tile(r0, first):
            rows = pl.ds(r0, T)
            _, vjp_gate = jax.vjp(_glu_gate, c_ref[rows, :].astype(f32), u_ref[rows, :].astype(f32))
            dc, du = vjp_gate(d_ref[rows, :].astype(f32))
            _, vjp_conv = jax.vjp(_glu_conv, _glu_gext(g_ref, r0, first, T), wv, bv)
            dgx, dw, db = vjp_conv(dc)
            acc[pl.ds(r0, T), :] = dgx[HALO:]
            if not first:
                acc[pl.ds(r0 - HALO, HALO), :] += dgx[:HALO]
            dg_ref[1, pl.ds(r0, T), :] = du.astype(bf16)
            return dw, db

        dw0, db0 = tile(0, True)
        dw_ref[...] = dw0
        db_ref[...] = db0

        @pl.loop(1, S // T)
        def _(t):
            dw, db = tile(pl.multiple_of(t * T, T), False)
            dw_ref[...] += dw
            db_ref[...] += db

        dg_ref[0] = acc[...].astype(bf16)

    col = pl.BlockSpec((S, LANE), lambda j: (0, j))
    wsp = pl.BlockSpec((3, LANE), lambda j: (0, j))
    bsp = pl.BlockSpec((1, LANE), lambda j: (0, j))
    return pl.pallas_call(
        body, name=name, grid=(N_FF_BLK,),
        in_specs=[col, pl.BlockSpec((S, LANE), lambda j: (0, N_FF_BLK + j)), col, wsp, bsp, col],
        out_specs=[pl.BlockSpec((2, S, LANE), lambda j: (0, 0, j)), wsp, bsp],
        out_shape=[jax.ShapeDtypeStruct((2, S, D_FF), bf16), jax.ShapeDtypeStruct((3, D_FF), f32),
                   jax.ShapeDtypeStruct((1, D_FF), f32)],
        scratch_shapes=[pltpu.VMEM((S, LANE), f32)],
        compiler_params=_cp(("parallel",)),
    )(gu, gu, c, w, b.reshape(1, D_FF), dact)


def gu_fwd(n2, wg, name):
    S = n2.shape[0]
    tm = min(MM_ROWS, S)

    def body(a_ref, w_ref, o_ref):
        o_ref[...] = _dg(a_ref[...], w_ref[...], 1, 0).astype(bf16)

    return pl.pallas_call(
        body, name=name, grid=(4, S // tm),
        in_specs=[pl.BlockSpec((tm, D), lambda s, m: (m, 0)), pl.BlockSpec((None, D, GU_SHARD), lambda s, m: (s, 0, 0))],
        out_specs=pl.BlockSpec((tm, GU_SHARD), lambda s, m: (m, s)),
        out_shape=jax.ShapeDtypeStruct((S, 2 * D_FF), bf16), compiler_params=_cp(("parallel", "parallel")),
    )(n2, wg)


def gu_bwd_x(dgu, wg, norm, name):
    S = dgu.shape[1]
    tm = min(NORM_ROWS, S)

    def body(d_ref, w_ref, h_ref, g_ref, r_ref, o_ref, dg_ref):
        _acc_then_norm_bwd(_dg(d_ref[...], w_ref[...], 1, 1), 4, h_ref, g_ref, r_ref, o_ref, dg_ref)

    tok = pl.BlockSpec((tm, D), lambda m, s: (m, 0))
    vec = pl.BlockSpec((1, D), lambda m, s: (0, 0))
    return pl.pallas_call(
        body, name=name, grid=(S // tm, 4),
        in_specs=[pl.BlockSpec((None, tm, GU_SHARD), lambda m, s: (s // 2, m, s % 2)),
                  pl.BlockSpec((None, D, GU_SHARD), lambda m, s: (s, 0, 0)), tok, vec, tok],
        out_specs=[tok, vec],
        out_shape=[jax.ShapeDtypeStruct((S, D), f32), jax.ShapeDtypeStruct((1, D), f32)],
        compiler_params=_cp(("arbitrary", "arbitrary")),
    )(dgu, wg, norm[0], norm[1].reshape(1, D), norm[2])


def gu_bwd_w(n2, dgu, name):
    S = n2.shape[0]
    tm = min(MM_ROWS, S)
    nm = S // tm

    def body(a_ref, d_ref, o_ref, acc):
        @pl.when(pl.program_id(1) == 0)
        def _():
            acc[...] = jnp.zeros_like(acc)
        acc[...] += _dg(a_ref[...], d_ref[...], 0, 0)

        @pl.when(pl.program_id(1) == nm - 1)
        def _():
            o_ref[...] = acc[...].astype(bf16)

    return pl.pallas_call(
        body, name=name, grid=(4, nm),
        in_specs=[pl.BlockSpec((tm, D), lambda s, m: (m, 0)),
                  pl.BlockSpec((None, tm, GU_SHARD), lambda s, m: (s // 2, m, s % 2))],
        out_specs=pl.BlockSpec((None, D, GU_SHARD), lambda s, m: (s, 0, 0)),
        out_shape=jax.ShapeDtypeStruct((4, D, GU_SHARD), bf16),
        scratch_shapes=[pltpu.VMEM((D, GU_SHARD), f32)],
        compiler_params=_cp(("parallel", "arbitrary")),
    )(n2, dgu)


def _pair_cols(w):
    lead = w.shape[:-1]
    return w.reshape(lead + (2, 6, A_DH)).swapaxes(-3, -2).reshape(lead + (A_Q,))


def _unpair_cols(w):
    lead = w.shape[:-1]
    return w.reshape(lead + (6, 2, A_DH)).swapaxes(-3, -2).reshape(lead + (A_Q,))


def _lay_in_a(w):
    return jnp.concatenate([_pair_cols(w[:, :A_Q]), w[:, A_Q:]], axis=1)


def _unlay_in_a(w):
    return jnp.concatenate([_unpair_cols(w[:, :A_Q]), w[:, A_Q:]], axis=1)


def _lay_out_a(w):
    return jnp.concatenate([_pair_cols(w[:A_Q].T).T, w[A_Q:]], axis=0)


def _unlay_out_a(w):
    return jnp.concatenate([_unpair_cols(w[:A_Q].T).T, w[A_Q:]], axis=0)


def _lay_in_b(w):
    return jnp.concatenate([w[:, :2304], w[:, 2316:], w[:, 2304:2316],
                            jnp.zeros((w.shape[0], LANE - 12), w.dtype)], axis=1)


def _unlay_in_b(w):
    return jnp.concatenate([w[:, :2304], w[:, 2560:2572], w[:, 2304:2560]], axis=1)


def _chip_cols(w):
    return jnp.moveaxis(w.reshape(w.shape[0], 4, w.shape[1] // 4), 1, 0)


def _unchip_cols(w):
    return jnp.moveaxis(w, 0, 1).reshape(w.shape[1], 4 * w.shape[2])


def _local_step(x, mem, target, P):
    arrive = P.get("arrive", lambda key, after: None)
    ready = P.get("ready", lambda key, grads, dep: dep)
    sk = jnp.zeros((16, LANE), f32).at[:A_HEADS].set(jnp.broadcast_to(P["sinks"][:, None], (A_HEADS, LANE)))
    prm = jnp.zeros((8, LANE), f32).at[0, 6:12].set(P["a_log"]).at[1, 6:12].set(P["dt_bias"]).at[2].set(P["out_norm_g"])
    bias = bias_build(P["rel_bias"])
    saved = []
    h = x
    n1 = rms_fwd(h, P["g_mix"][0], "rms_mix0")
    for i in range(2):
        arrive(("w_in", i), n1)
        proj = mm_nn(n1, P["w_in_a"] if i == 0 else P["w_in_b"], name="proj_a" if i == 0 else "proj_b")
        arrive(("w_mem", i), proj)
        kv = memkv_fwd(mem, P["g_mem"][i], P["w_mem"][i], f"memkv{i}")
        if i == 0:
            self_out = swa_fwd(proj, bias, sk)
            cross = xattn_fwd(proj, A_Q + 2 * LANE, kv, "xattn_a")
            extra = ()
        else:
            qkvn = dnprep_fwd(proj, P["conv_qkv"])
            chunked, inv = dnc_fwd(qkvn, proj, prm)
            o, states = dns_fwd(chunked)
            self_out = dnpost_fwd(o, proj, prm)
            cross = xattn_fwd(proj, 2304, kv, "xattn_b")
            extra = (qkvn, chunked, inv, states, o)
        mix = (self_out, cross)
        arrive(("w_out", i), cross)
        h2, n2 = mm_res_norm(mix, P["w_out"][i], h, P["g_ffn"][i], f"out_proj{i}")
        arrive(("w_gu", i), n2)
        gu = gu_fwd(n2, P["w_gu"][i], f"gate_up{i}")
        act, pre = glu_fwd(gu, P["ffn_cw"][i], P["ffn_cb"][i], f"glu{i}")
        arrive(("w_down", i), act)
        saved.append((h, n1, kv, proj, mix, h2, n2, gu, pre, act, extra))
        if i == 0:
            h, n1 = mm_res_norm(act, P["w_down"][i], h2, P["g_mix"][1], f"down{i}")
        else:
            h = mm_nn(act, P["w_down"][i], res=h2, name=f"down{i}")

    loss, dh, dg_fin = loss_head(h, P["g_fin"], target)
    G = {"g_fin": dg_fin[0], "g_mix": [None, None], "g_mem": [None, None], "g_ffn": [None, None],
         "w_mem": [None, None], "w_out": [None, None], "w_gu": [None, None], "w_down": [None, None],
         "ffn_cw": [None, None], "ffn_cb": [None, None]}
    for i in (1, 0):
        hin, n1, kv, proj, mix, h2, n2, gu, pre, act, extra = saved[i]
        dact = mm_nt(dh, P["w_down"][i], out_dtype=bf16, name=f"d_act{i}")
        G["w_down"][i] = mm_tn(act, dh, name=f"dw_down{i}")
        dgu, dcw, dcb = glu_bwd(gu, pre, P["ffn_cw"][i], P["ffn_cb"][i], dact, f"glu_bwd{i}")
        G["ffn_cw"][i], G["ffn_cb"][i] = dcw, dcb[0]
        G["w_gu"][i] = gu_bwd_w(n2, dgu, f"dw_gu{i}")
        g_ffn = ready(("ffn", i), G, P["g_ffn"][i])
        dh2, dg = gu_bwd_x(dgu, P["w_gu"][i], (h2, g_ffn, dh), f"d_n2_{i}")
        G["g_ffn"][i] = dg[0]
        dmix = mm_nt(dh2, P["w_out"][i], name=f"d_mix{i}")
        G["w_out"][i] = mm_tn(mix, dh2, name=f"dw_out{i}")
        if i == 0:
            dqkv, dbias, dsk = swa_bwd(proj, bias, sk, dmix)
            dxq, dkv = xattn_bwd(proj, A_Q + 2 * LANE, kv, dmix, "xattn_a_bwd")
            dproj = (dqkv, dxq)
            G["sinks"] = dsk[:A_HEADS, 0]
            G["rel_bias"] = bias_grad(dbias)[:, :A_HEADS]
            w_in, gname = P["w_in_a"], "w_in_a"
        else:
            qkvn, chunked, inv, states, o = extra
            do, dz, dgo = dnpost_bwd(o, proj, prm, dmix)
            dqkvn, dseg, dprm = dnc_bwd(qkvn, proj, prm, inv, dns_bwd(chunked, states, do))
            draw, dconv = dnprep_bwd(proj, P["conv_qkv"], dqkvn)
            dxq, dkv = xattn_bwd(proj, 2304, kv, dmix, "xattn_b_bwd")
            dproj = (draw, dz, dxq, dseg)
            G["conv_qkv"] = dconv
            G["a_log"], G["dt_bias"], G["out_norm_g"] = dprm[0, 6:12], dprm[1, 6:12], dgo[0]
            w_in, gname = P["w_in_b"], "w_in_b"
        G[gname] = mm_tn(n1, dproj, name=f"d{gname}")
        dh, dg = mm_nt_norm(dproj, w_in, (hin, P["g_mix"][i], dh2), f"d_n1_{i}")
        G["g_mix"][i] = dg[0]
        dgm, dwm = memkv_bwd(mem, P["g_mem"][i], P["w_mem"][i], dkv, f"memkv_bwd{i}")
        G["g_mem"][i], G["w_mem"][i] = dgm[0], dwm
        ready(("mix", i), G, None)
    return loss, dh, G


def _grads_to_ref(G):
    return {
        "rel_bias": G["rel_bias"], "norm_mix_g": jnp.stack(G["g_mix"]), "norm_mem_g": jnp.stack(G["g_mem"]),
        "w_mem_kv": jnp.stack(G["w_mem"]),
        "w_out": jnp.stack([_unlay_out_a(G["w_out"][0]), G["w_out"][1]]),
        "w_in_a": _unlay_in_a(G["w_in_a"])[None], "sinks_a": G["sinks"][None],
        "w_in_b": _unlay_in_b(G["w_in_b"])[None], "conv_qkv_b": G["conv_qkv"][None],
        "a_log_b": G["a_log"][None], "dt_bias_b": G["dt_bias"][None], "out_norm_g_b": G["out_norm_g"][None],
        "norm_ffn_g": jnp.stack(G["g_ffn"]),
        "w_gate_up": jnp.stack([_unchip_cols(G["w_gu"][0]), _unchip_cols(G["w_gu"][1])]).astype(f32),
        "ffn_conv_w": jnp.stack(G["ffn_cw"]), "ffn_conv_b": jnp.stack(G["ffn_cb"]),
        "w_down": jnp.stack(G["w_down"]), "final_norm_g": G["g_fin"],
    }


ANY = pl.BlockSpec(memory_space=pl.ANY)


def _place():
    return lax.axis_index("x"), lax.axis_index("y"), lax.axis_index("c")


def allreduce_small(buf):
    R = buf.shape[0]

    def body(b_ref, o_ref, recv, ssem, rsem):
        x, y, c = _place()
        me = 4 * x + 2 * y + c

        def peer(k):
            return (1 - x if k & 4 else x, 1 - y if k & 2 else y, 1 - c if k & 1 else c)

        def remote(k, slot):
            return pltpu.make_async_remote_copy(
                src_ref=b_ref, dst_ref=recv.at[slot], send_sem=ssem.at[k - 1], recv_sem=rsem.at[k - 1],
                device_id=peer(k), device_id_type=MESH)

        sends = [remote(k, me) for k in range(1, 8)]
        for cp in sends:
            cp.start()
        recv[me] = b_ref[...]
        for k in range(1, 8):
            px, py, pc = peer(k)
            remote(k, 4 * px + 2 * py + pc).wait_recv()
        for cp in sends:
            cp.wait_send()
        total = recv[0]
        for j in range(1, 8):
            total = total + recv[j]
        o_ref[...] = total

    return pl.pallas_call(
        body, name="small_allreduce",
        in_specs=[pl.BlockSpec(memory_space=pltpu.VMEM)], out_specs=pl.BlockSpec(memory_space=pltpu.VMEM),
        out_shape=jax.ShapeDtypeStruct(buf.shape, f32),
        scratch_shapes=[pltpu.VMEM((8, R, LANE), f32), pltpu.SemaphoreType.DMA((7,)), pltpu.SemaphoreType.DMA((7,))],
    )(buf)


def sum_slots(own, recv, chip, core, name):
    _, R, C = recv.shape
    tr = _row_tile(R, 256)
    nt = R // tr

    def body(p_ref, a_ref, r_ref, o_ref):
        acc = jnp.zeros((tr, C), f32)
        for s in range(4):
            acc = acc + jnp.where(p_ref[0] == s, a_ref[s], r_ref[s]).astype(f32)
        o_ref[...] = acc

    slots = pl.BlockSpec((4, tr, C), lambda i, p_ref: (0, i, 0))
    return pl.pallas_call(
        body, name=name, out_shape=jax.ShapeDtypeStruct((2 * R, C), f32),
        grid_spec=pltpu.PrefetchScalarGridSpec(
            num_scalar_prefetch=1, grid=(nt,), in_specs=[slots, slots],
            out_specs=pl.BlockSpec((tr, C), lambda i, p_ref: (p_ref[1] * nt + i, 0))),
        compiler_params=_cp(("parallel",)),
    )(jnp.stack([chip, core]).astype(jnp.int32), own, recv)


def _half(ref, core, axis=0):
    half = ref.shape[axis] // 2
    idx = (slice(None),) * axis + (pl.ds(core * half, half),)
    return ref.at[idx]


IN_HBM = pl.BlockSpec(memory_space=pltpu.HBM)
IN_SEM = pl.BlockSpec(memory_space=pltpu.SEMAPHORE)
SIDE_EFFECT = pltpu.SideEffectType.DATAFLOW_SIDE_EFFECTING


def _gather_copy(buf, i, k, ssem, rsem, place, landing):
    x, y, c = place
    px, py = [(1 - x, y), (x, 1 - y), (1 - x, 1 - y)][k]
    me = 2 * x + y
    return pltpu.make_async_remote_copy(
        src_ref=buf.at[me], dst_ref=buf.at[me if landing == "theirs" else 2 * px + py],
        send_sem=ssem.at[3 * i + k], recv_sem=rsem.at[3 * i + k], device_id=(px, py, c), device_id_type=MESH)


def gather_start(groups):
    flat = [b for grp in groups for b in grp]
    n, ng = len(flat), len(groups)

    def body(*refs):
        bufs, sems = refs[:n], refs[n:n + 2 * ng]
        place = _place()
        j = 0
        for g, grp in enumerate(groups):
            for i in range(len(grp)):
                for k in range(3):
                    _gather_copy(bufs[j], i, k, sems[2 * g], sems[2 * g + 1], place, "theirs").start()
                j += 1

    sem_shapes = [pltpu.SemaphoreType.DMA((3 * len(grp),)) for grp in groups for _ in range(2)]
    out = pl.pallas_call(
        body, name="gather_start", in_specs=[IN_HBM] * n, out_specs=(*[IN_SEM] * (2 * ng), *[IN_HBM] * n),
        out_shape=(*sem_shapes, *[pltpu.HBM(b.shape, b.dtype) for b in flat]),
        input_output_aliases={i: 2 * ng + i for i in range(n)},
        compiler_params=pltpu.CompilerParams(has_side_effects=SIDE_EFFECT),
    )(*[pltpu.with_memory_space_constraint(b, pltpu.HBM) for b in flat])
    sems, bufs = out[:2 * ng], list(out[2 * ng:])
    flights, j = [], 0
    for g, grp in enumerate(groups):
        flights.append((bufs[j:j + len(grp)], sems[2 * g], sems[2 * g + 1]))
        j += len(grp)
    return flights


def gather_wait(flight, after, name):
    bufs, ssem, rsem = flight
    n = len(bufs)

    def body(*refs):
        place = _place()
        for i in range(n):
            for k in range(3):
                cp = _gather_copy(refs[i], i, k, refs[n], refs[n + 1], place, "mine")
                cp.wait_send()
                cp.wait_recv()

    return pl.pallas_call(
        body, name=name, in_specs=[IN_HBM] * n + [IN_SEM, IN_SEM, ANY], out_specs=[IN_HBM] * n,
        out_shape=[pltpu.HBM(b.shape, b.dtype) for b in bufs], input_output_aliases={i: i for i in range(n)},
        compiler_params=pltpu.CompilerParams(has_side_effects=SIDE_EFFECT),
    )(*bufs, ssem, rsem, after)


def _scatter_copy(src, land, j, k, ssem, rsem, place, landing):
    x, y, c = place
    px, py = [(1 - x, y), (x, 1 - y), (1 - x, 1 - y)][k]
    return pltpu.make_async_remote_copy(
        src_ref=src.at[2 * px + py], dst_ref=land.at[2 * x + y if landing == "theirs" else 2 * px + py],
        send_sem=ssem.at[3 * j + k], recv_sem=rsem.at[3 * j + k], device_id=(px, py, c), device_id_type=MESH)


def scatter_start(srcs, name):
    n = len(srcs)
    lands = [lax.empty(g.shape, g.dtype) for g in srcs]

    def body(*refs):
        place = _place()
        for j in range(n):
            for k in range(3):
                _scatter_copy(refs[j], refs[n + j], j, k, refs[2 * n], refs[2 * n + 1], place, "theirs").start()
        refs[-1][...] = jnp.zeros_like(refs[-1])

    sem = pltpu.SemaphoreType.DMA((3 * n,))
    hbm = [pltpu.with_memory_space_constraint(b, pltpu.HBM) for b in list(srcs) + lands]
    out = pl.pallas_call(
        body, name=name, in_specs=[IN_HBM] * (2 * n),
        out_specs=(IN_SEM, IN_SEM, *[IN_HBM] * (2 * n), pl.BlockSpec(memory_space=pltpu.VMEM)),
        out_shape=(sem, sem, *[pltpu.HBM(b.shape, b.dtype) for b in hbm], jax.ShapeDtypeStruct((8, LANE), f32)),
        input_output_aliases={i: 2 + i for i in range(2 * n)},
        compiler_params=pltpu.CompilerParams(has_side_effects=SIDE_EFFECT),
    )(*hbm)
    return (list(out[2:2 + n]), list(out[2 + n:2 + 2 * n]), out[0], out[1]), out[-1]


def scatter_wait(flight, after, name):
    srcs, lands, ssem, rsem = flight
    n = len(srcs)

    def body(*refs):
        place = _place()
        for j in range(n):
            for k in range(3):
                cp = _scatter_copy(refs[j], refs[n + j], j, k, refs[2 * n], refs[2 * n + 1], place, "mine")
                cp.wait_send()
                cp.wait_recv()

    out = pl.pallas_call(
        body, name=name, in_specs=[IN_HBM] * (2 * n) + [IN_SEM, IN_SEM, ANY], out_specs=[IN_HBM] * (2 * n),
        out_shape=[pltpu.HBM(b.shape, b.dtype) for b in list(srcs) + list(lands)],
        input_output_aliases={i: i for i in range(2 * n)},
        compiler_params=pltpu.CompilerParams(has_side_effects=SIDE_EFFECT),
    )(*srcs, *lands, ssem, rsem, after)
    return list(out[:n]), list(out[n:])


def pair_exchange(gbufs, name):
    n = len(gbufs)

    def body(*refs):
        ins, outs = refs[:n], refs[n:2 * n]
        ssem, rsem = refs[2 * n:]
        x, y, c = _place()
        cps = [pltpu.make_async_remote_copy(
            src_ref=_half(ins[j], 1 - c, axis=1), dst_ref=outs[j], send_sem=ssem.at[j], recv_sem=rsem.at[j],
            device_id=(x, y, 1 - c), device_id_type=MESH) for j in range(n)]
        for cp in cps:
            cp.start()
        for cp in cps:
            cp.wait()

    return pl.pallas_call(
        body, name=name, in_specs=[ANY] * n, out_specs=[ANY] * n,
        out_shape=[jax.ShapeDtypeStruct((4, g.shape[1] // 2, g.shape[2]), g.dtype) for g in gbufs],
        scratch_shapes=[pltpu.SemaphoreType.DMA((n,)), pltpu.SemaphoreType.DMA((n,))],
    )(*gbufs)


def _row_tile(rows, cap=512):
    return max(t for t in range(16, min(rows, cap) + 1, 16) if rows % t == 0)


def pair_sum(mine, theirs, core, name):
    _, R, C = mine.shape
    half = R // 2
    tr = _row_tile(half)
    nt = half // tr

    def body(c_ref, a_ref, b_ref, o_ref):
        o_ref[...] = (a_ref[...].astype(f32) + b_ref[...].astype(f32)).astype(bf16)

    return pl.pallas_call(
        body, name=name, out_shape=jax.ShapeDtypeStruct(theirs.shape, bf16),
        grid_spec=pltpu.PrefetchScalarGridSpec(
            num_scalar_prefetch=1, grid=(4, nt),
            in_specs=[pl.BlockSpec((None, tr, C), lambda s, i, c_ref: (s, c_ref[0] * nt + i, 0)),
                      pl.BlockSpec((None, tr, C), lambda s, i, c_ref: (s, i, 0))],
            out_specs=pl.BlockSpec((None, tr, C), lambda s, i, c_ref: (s, i, 0))),
        compiler_params=_cp(("parallel", "parallel")),
    )(jnp.reshape(core, (1,)).astype(jnp.int32), mine, theirs)


def final_exchange(fins):
    n = len(fins)

    def body(*refs):
        outs = refs[n:2 * n]
        ssem, rsem = refs[2 * n:]
        x, y, c = _place()
        cps = [pltpu.make_async_remote_copy(
            src_ref=_half(outs[j], c), dst_ref=_half(outs[j], c), send_sem=ssem.at[j], recv_sem=rsem.at[j],
            device_id=(x, y, 1 - c), device_id_type=MESH) for j in range(n)]
        for cp in cps:
            cp.start()
        for cp in cps:
            cp.wait()

    return pl.pallas_call(
        body, name="final_exchange", in_specs=[ANY] * n, out_specs=[ANY] * n,
        out_shape=[jax.ShapeDtypeStruct(f.shape, f.dtype) for f in fins],
        input_output_aliases={j: j for j in range(n)},
        scratch_shapes=[pltpu.SemaphoreType.DMA((n,)), pltpu.SemaphoreType.DMA((n,))],
    )(*fins)


def adamw_big(w, m, v, gs, row0, name):
    L, R, C = w.shape
    tr = _row_tile(math.gcd(R, row0) if row0 else R, max(16, 262144 // C // 16 * 16))
    b0 = row0 // tr

    def body(*refs):
        w_ref, m_ref, v_ref = refs[:3]
        g_refs = refs[3:3 + L]
        g_ref, d_ref, nm_ref, nv_ref = refs[3 + L:]
        g = g_refs[0][...]
        for l in range(1, L):
            g = jnp.where(pl.program_id(0) == l, g_refs[l][...], g)
        d, nm, nv = _adamw_math(w_ref[...], g, m_ref[...], v_ref[...])
        g_ref[...] = g
        d_ref[...] = d
        nm_ref[...] = nm
        nv_ref[...] = nv

    own = pl.BlockSpec((None, tr, C), lambda l, i: (l, i, 0))
    off = pl.BlockSpec((tr, C), lambda l, i: (b0 + i, 0))
    return pl.pallas_call(
        body, name=name, grid=(L, R // tr), in_specs=[own, own, own] + [off] * L, out_specs=[own] * 4,
        out_shape=[jax.ShapeDtypeStruct((L, R, C), f32)] * 4, compiler_params=_cp(("parallel", "parallel")),
    )(w, m, v, *gs)


def _adamw_math(w, g, m, v):
    m = B1 * m + (1.0 - B1) * g
    v = B2 * v + (1.0 - B2) * (g * g)
    m_hat = m / (1.0 - B1 ** STEP)
    v_hat = v / (1.0 - B2 ** STEP)
    delta = -LR * (m_hat / (jnp.sqrt(v_hat) + AEPS) + WD * w)
    return delta, m, v


def adamw_small(w, m, v, g):
    def body(w_ref, m_ref, v_ref, g_ref, d_ref, nm_ref, nv_ref):
        d, nm, nv = _adamw_math(w_ref[...], g_ref[...], m_ref[...], v_ref[...])
        d_ref[...] = d
        nm_ref[...] = nm
        nv_ref[...] = nv

    return pl.pallas_call(body, name="adamw_small", out_shape=[jax.ShapeDtypeStruct(w.shape, f32)] * 3)(w, m, v, g)


CONV =(("conv_qkv_b", 2), ("ffn_conv_w", 2))
SMALL = ("rel_bias", "norm_mix_g", "norm_mem_g", "sinks_a", "a_log_b", "dt_bias_b", "out_norm_g_b", "norm_ffn_g",
         "ffn_conv_b", "final_norm_g")
WEIGHTS = ("rel_bias", "norm_mix_g", "norm_mem_g", "w_mem_kv", "w_out", "w_in_a", "sinks_a", "w_in_b", "conv_qkv_b",
           "a_log_b", "dt_bias_b", "out_norm_g_b", "norm_ffn_g", "w_gate_up", "ffn_conv_w", "ffn_conv_b", "w_down",
           "final_norm_g")
ARGS = ("x", "mem") + WEIGHTS + ("loss_target",) + tuple("m_" + n for n in WEIGHTS) + tuple("v_" + n for n in WEIGHTS)


def _rows(a, width):
    flat = a.reshape(-1)
    pad = (-flat.shape[0]) % (8 * width)
    if pad:
        flat = jnp.concatenate([flat, jnp.zeros((pad,), a.dtype)])
    return flat.reshape(-1, width)


def _nrows(shape, width):
    return _pad_to(-(-math.prod(shape) // width), 8)


def _pack(arrs, width, total_rows, dtype):
    parts = [_rows(a.astype(dtype), width) for a in arrs]
    used = sum(p.shape[0] for p in parts)
    if total_rows > used:
        parts.append(jnp.zeros((total_rows - used, width), dtype))
    return jnp.concatenate(parts, axis=0)


def _unpack(buf, shapes, width):
    out, r = [], 0
    for s in shapes:
        n = _nrows(s, width)
        out.append(buf[r:r + n].reshape(-1)[:math.prod(s)].reshape(s))
        r += n
    return out


def _pad_to(n, mult):
    return -(-n // mult) * mult


def kernel(x, mem, rel_bias, norm_mix_g, norm_mem_g, w_mem_kv, w_out, w_in_a, sinks_a, w_in_b, conv_qkv_b, a_log_b, dt_bias_b, out_norm_g_b, norm_ffn_g, w_gate_up, ffn_conv_w, ffn_conv_b, w_down, final_norm_g, loss_target, m_rel_bias, m_norm_mix_g, m_norm_mem_g, m_w_mem_kv, m_w_out, m_w_in_a, m_sinks_a, m_w_in_b, m_conv_qkv_b, m_a_log_b, m_dt_bias_b, m_out_norm_g_b, m_norm_ffn_g, m_w_gate_up, m_ffn_conv_w, m_ffn_conv_b, m_w_down, m_final_norm_g, v_rel_bias, v_norm_mix_g, v_norm_mem_g, v_w_mem_kv, v_w_out, v_w_in_a, v_sinks_a, v_w_in_b, v_conv_qkv_b, v_a_log_b, v_dt_bias_b, v_out_norm_g_b, v_norm_ffn_g, v_w_gate_up, v_ffn_conv_w, v_ffn_conv_b, v_w_down, v_final_norm_g):
    A = dict(zip(ARGS, (x, mem, rel_bias, norm_mix_g, norm_mem_g, w_mem_kv, w_out, w_in_a, sinks_a, w_in_b, conv_qkv_b, a_log_b, dt_bias_b, out_norm_g_b, norm_ffn_g, w_gate_up, ffn_conv_w, ffn_conv_b, w_down, final_norm_g, loss_target, m_rel_bias, m_norm_mix_g, m_norm_mem_g, m_w_mem_kv, m_w_out, m_w_in_a, m_sinks_a, m_w_in_b, m_conv_qkv_b, m_a_log_b, m_dt_bias_b, m_out_norm_g_b, m_norm_ffn_g, m_w_gate_up, m_ffn_conv_w, m_ffn_conv_b, m_w_down, m_final_norm_g, v_rel_bias, v_norm_mix_g, v_norm_mem_g, v_w_mem_kv, v_w_out, v_w_in_a, v_sinks_a, v_w_in_b, v_conv_qkv_b, v_a_log_b, v_dt_bias_b, v_out_norm_g_b, v_norm_ffn_g, v_w_gate_up, v_ffn_conv_w, v_ffn_conv_b, v_w_down, v_final_norm_g)))
    chip = 2 * lax.axis_index("x") + lax.axis_index("y")
    core = lax.axis_index("c")

    def own_slot(shard):
        return lax.dynamic_update_index_in_dim(lax.empty((4,) + shard.shape, shard.dtype), shard, chip, 0)

    def bslot(w):
        return own_slot(w.astype(bf16))

    groups = {
        ("w_in", 0): [bslot(w_in_a[0])],
        ("w_mem", 0): [bslot(w_mem_kv[0]), bslot(w_mem_kv[1]), own_slot(conv_qkv_b[0]),
                       own_slot(ffn_conv_w.reshape(6, -1))],
        ("w_out", 0): [bslot(w_out[0])], ("w_gu", 0): [bslot(w_gate_up[0])], ("w_down", 0): [bslot(w_down[0])],
        ("w_in", 1): [bslot(w_in_b[0])],
        ("w_out", 1): [bslot(w_out[1])], ("w_gu", 1): [bslot(w_gate_up[1])], ("w_down", 1): [bslot(w_down[1])],
    }
    flights = dict(zip(groups, gather_start(list(groups.values()))))
    P = {"rel_bias": rel_bias, "sinks": sinks_a[0], "a_log": a_log_b[0], "dt_bias": dt_bias_b[0],
         "out_norm_g": out_norm_g_b[0], "g_mix": norm_mix_g, "g_mem": norm_mem_g, "g_ffn": norm_ffn_g,
         "g_fin": final_norm_g, "ffn_cb": [ffn_conv_b[0], ffn_conv_b[1]], "w_mem": [None, None], "w_out": [None, None],
         "w_gu": [None, None], "w_down": [None, None], "ffn_cw": [None, None]}

    def rows4(g):
        return g.reshape(4 * g.shape[1], g.shape[2])

    def arrive(key, after):
        if key not in flights:
            return
        got = gather_wait(flights.pop(key), after, "gather_wait_%s%d" % key)
        name, i = key
        if name == "w_in":
            P["w_in_a" if i == 0 else "w_in_b"] = (_lay_in_a if i == 0 else _lay_in_b)(_unchip_cols(got[0]))
        elif name == "w_mem":
            P["w_mem"] = [rows4(got[0]), rows4(got[1])]
            P["conv_qkv"] = _unchip_cols(got[2])
            cw = _unchip_cols(got[3]).reshape(2, 3, D_FF)
            P["ffn_cw"] = [cw[0], cw[1]]
        elif name == "w_out":
            P["w_out"][i] = _lay_out_a(rows4(got[0])) if i == 0 else rows4(got[0])
        elif name == "w_gu":
            P["w_gu"][i] = got[0]
        else:
            P["w_down"][i] = rows4(got[0])

    def chip_rows(g):
        return g.reshape(4, g.shape[0] // 4, g.shape[-1])

    sent, started = {}, []

    def ready(key, G, dep):
        kind, i = key
        tag = "%s%d" % key
        if kind == "ffn":
            names, partial = ("gu", "down"), [G["w_gu"][i], chip_rows(G["w_down"][i]).astype(bf16)]
        else:
            g_out = _unlay_out_a(G["w_out"][0]) if i == 0 else G["w_out"][1]
            g_in = _unlay_in_a(G["w_in_a"]) if i == 0 else _unlay_in_b(G["w_in_b"])
            names = ("out", "in", "mem")
            partial = [chip_rows(g_out).astype(bf16), _chip_cols(g_in).astype(bf16), chip_rows(G["w_mem"][i]).astype(bf16)]
        theirs = pair_exchange(partial, "pair_exchange_" + tag)
        pair = [pair_sum(p, t, core, "pair_sum_%s%d" % (nm, i)) for p, t, nm in zip(partial, theirs, names)]
        flight, token = scatter_start(pair, "scatter_start_" + tag)
        sent[key] = (names, flight, token)
        started.append(token[0, 0])
        if dep is not None:
            while started:
                dep = dep + started.pop()
        return dep

    P["arrive"], P["ready"] = arrive, ready

    loss, dx, G = _local_step(x[0], mem[0], loss_target[0], P)
    gfull = _grads_to_ref(G)

    fin, after = {}, sent["mix", 0][2]
    for key in (("ffn", 1), ("mix", 1), ("ffn", 0), ("mix", 0)):
        names, flight, _ = sent[key]
        pair, arrived = scatter_wait(flight, after, "scatter_wait_%s%d" % key)
        for nm, p, r in zip(names, pair, arrived):
            after = fin[nm, key[1]] = sum_slots(p, r, chip, core, "sum_slots_%s%d" % (nm, key[1]))
    order = list(fin)
    done = dict(zip(order, final_exchange([fin[k] for k in order])))

    sm_shapes = [A[n].shape for n in SMALL] + [gfull[n].shape for n, _ in CONV] + [(LANE,)]
    sm_rows = _pad_to(sum(_nrows(s, LANE) for s in sm_shapes), 8)
    sbuf = _pack([gfull[n] for n in SMALL] + [gfull[n] for n, _ in CONV] + [loss[0]], LANE, sm_rows, f32)
    tot = _unpack(allreduce_small(sbuf), sm_shapes, LANE)
    gsmall = dict(zip(SMALL, tot[:len(SMALL)]))
    for (n, axis), t in zip(CONV, tot[len(SMALL):len(SMALL) + len(CONV)]):
        sh = A[n].shape[axis]
        gsmall[n] = lax.dynamic_slice_in_dim(t, chip * sh, sh, axis)
    loss_out = tot[-1][0]

    out = {}
    plan = (("w_gate_up", [done["gu", 0], done["gu", 1]]), ("w_down", [done["down", 0], done["down", 1]]),
            ("w_out", [done["out", 0], done["out", 1]]), ("w_mem_kv", [done["mem", 0], done["mem", 1]]),
            ("w_in_a", [done["in", 0]]), ("w_in_b", [done["in", 1]]))
    for n, gs in plan:
        shape3 = (len(gs),) + gs[0].shape
        res = adamw_big(A[n].reshape(shape3), A["m_" + n].reshape(shape3), A["v_" + n].reshape(shape3), gs, 0,
                        "adamw_" + n)
        for key, r in zip(("grad_", "delta_", "new_m_", "new_v_"), res):
            out[key + n] = r.reshape(A[n].shape)
    names = SMALL + tuple(n for n, _ in CONV)
    shapes = [A[n].shape for n in names]
    rows = _pad_to(sum(_nrows(s, LANE) for s in shapes), 8)
    packs = [_pack([src[n] for n in names], LANE, rows, f32)
             for src in ({n: A[n] for n in names}, {n: A["m_" + n] for n in names}, {n: A["v_" + n] for n in names}, gsmall)]
    res = adamw_small(*packs)
    for key, r in zip(("delta_", "new_m_", "new_v_"), res):
        for n, a in zip(names, _unpack(r, shapes, LANE)):
            out[key + n] = a
    for n in names:
        out["grad_" + n] = gsmall[n]
    return (loss_out, dx[None], *[out["grad_" + n] for n in WEIGHTS], *[out["delta_" + n] for n in WEIGHTS],
            *[out["new_m_" + n] for n in WEIGHTS], *[out["new_v_" + n] for n in WEIGHTS])
```

```python
import functools
import math

import numpy as np
import jax
import jax.numpy as jnp
from jax import lax
from jax.experimental import pallas as pl
from jax.experimental.pallas import tpu as pltpu

f32 = jnp.float32
bf16 = jnp.bfloat16
HI = lax.Precision.HIGHEST
MESH = pl.DeviceIdType.MESH

D = 1024
MEM_LEN = 256
EPS = 1e-6
A_HEADS, A_KV, A_DH = 12, 2, 64
A_Q = 768
BLK = 128
N_BUCKETS, MAX_DIST = 32, 128
B_QK, B_V, B_DH = 384, 768, 128
B_QKV = 1536
CHUNK = 64
X_Q = 256
D_FF = 2816
LANE = 128
VMEM_LIMIT = 56 * 1024 * 1024
MM_ROWS = 1024

LR, B1, B2, AEPS, WD, STEP = 0.001, 0.9, 0.999, 1e-08, 0.01, 10


def _cp(sem=None):
    return pltpu.CompilerParams(dimension_semantics=sem, vmem_limit_bytes=VMEM_LIMIT)


def _dg(a, b, ca, cb, prec=None):
    return lax.dot_general(a, b, (((ca,), (cb,)), ((), ())), precision=prec, preferred_element_type=f32)


@jax.custom_vjp
def bdot(a, b):
    return _dg(a.astype(bf16), b.astype(bf16), 1, 0)


def _bdot_f(a, b):
    return bdot(a, b), (a, b)


def _bdot_b(res, g):
    a, b = res
    gb = g.astype(bf16)
    return _dg(gb, b.astype(bf16), 1, 1), _dg(a.astype(bf16), gb, 0, 0)


bdot.defvjp(_bdot_f, _bdot_b)


@jax.custom_vjp
def bdot_nt(a, b):
    return _dg(a.astype(bf16), b.astype(bf16), 1, 1)


def _bdot_nt_f(a, b):
    return bdot_nt(a, b), (a, b)


def _bdot_nt_b(res, g):
    a, b = res
    gb = g.astype(bf16)
    return _dg(gb, b.astype(bf16), 1, 0), _dg(gb, a.astype(bf16), 0, 0)


bdot_nt.defvjp(_bdot_nt_f, _bdot_nt_b)


def _shift_rows(x, s, down):
    n = x.shape[0]
    row = lax.broadcasted_iota(jnp.int32, x.shape, 0)
    if down:
        return jnp.where(row >= s, pltpu.roll(x, s, 0), 0.0)
    return jnp.where(row < n - s, pltpu.roll(x, n - s, 0), 0.0)


@functools.partial(jax.custom_vjp, nondiff_argnums=(1,))
def shift_down(x, s):
    return _shift_rows(x, s, True)


def _sd_f(x, s):
    return _shift_rows(x, s, True), None


def _sd_b(s, _, g):
    return (_shift_rows(g, s, False),)


shift_down.defvjp(_sd_f, _sd_b)


def _sigmoid(x):
    return 1.0 / (1.0 + jnp.exp(-x))


def _silu(x):
    return x * _sigmoid(x)


def _rms(x, g):
    return x * lax.rsqrt(jnp.mean(x * x, axis=-1, keepdims=True) + EPS) * g


def _tile(n, cap):
    u = n // LANE
    best = 1
    for d in range(1, u + 1):
        if u % d == 0 and d * LANE <= cap:
            best = d
    return best * LANE


def mm_nn(a, w, res=None, out_dtype=f32, name="mm_nn"):
    M, K = a.shape
    N = w.shape[1]
    tm, tn = min(MM_ROWS, M), _tile(N, 1024)

    def body(*refs):
        if res is None:
            a_ref, w_ref, o_ref = refs
            o_ref[...] = _dg(a_ref[...].astype(bf16), w_ref[...], 1, 0).astype(out_dtype)
        else:
            a_ref, w_ref, r_ref, o_ref = refs
            o_ref[...] = (r_ref[...] + _dg(a_ref[...].astype(bf16), w_ref[...], 1, 0)).astype(out_dtype)

    in_specs = [pl.BlockSpec((tm, K), lambda n, m: (m, 0)), pl.BlockSpec((K, tn), lambda n, m: (0, n))]
    args = [a, w]
    if res is not None:
        in_specs.append(pl.BlockSpec((tm, tn), lambda n, m: (m, n)))
        args.append(res)
    return pl.pallas_call(
        body, name=name, grid=(N // tn, M // tm), in_specs=in_specs,
        out_specs=pl.BlockSpec((tm, tn), lambda n, m: (m, n)),
        out_shape=jax.ShapeDtypeStruct((M, N), out_dtype),
        compiler_params=_cp(("parallel", "parallel")),
    )(*args)


def mm_res_norm(a, w, res, g, name):
    pieces = a if isinstance(a, tuple) else (a,)
    na = len(pieces)
    M, K = pieces[0].shape[0], sum(p.shape[1] for p in pieces)
    tm = min(MM_ROWS, M)

    def body(*refs):
        w_ref, r_ref, g_ref, o_ref, n_ref = refs[na:]
        h = r_ref[...] + _dg(_cols(refs[:na]).astype(bf16), w_ref[...], 1, 0)
        o_ref[...] = h
        n_ref[...] = _rms(h, g_ref[...]).astype(bf16)

    tok = pl.BlockSpec((tm, D), lambda m: (m, 0))
    return pl.pallas_call(
        body, name=name, grid=(M // tm,),
        in_specs=[pl.BlockSpec((tm, p.shape[1]), lambda m: (m, 0)) for p in pieces]
        + [pl.BlockSpec((K, D), lambda m: (0, 0)), tok, pl.BlockSpec((1, D), lambda m: (0, 0))],
        out_specs=[tok, tok],
        out_shape=[jax.ShapeDtypeStruct((M, D), f32), jax.ShapeDtypeStruct((M, D), bf16)],
        compiler_params=_cp(("parallel",)),
    )(*pieces, w, res, g.reshape(1, D))


def mm_nt(dy, w, out_dtype=f32, name="mm_nt"):
    M, N = dy.shape
    K = w.shape[0]
    tm, tn = min(MM_ROWS, M), _tile(N, 1024)
    assert out_dtype == f32 or tn == N

    def body(dy_ref, w_ref, o_ref):
        part = _dg(dy_ref[...].astype(bf16), w_ref[...], 1, 1)
        if tn == N:
            o_ref[...] = part.astype(out_dtype)
        else:
            @pl.when(pl.program_id(1) == 0)
            def _():
                o_ref[...] = jnp.zeros_like(o_ref)
            o_ref[...] += part

    return pl.pallas_call(
        body, name=name, grid=(M // tm, N // tn),
        in_specs=[pl.BlockSpec((tm, tn), lambda m, n: (m, n)), pl.BlockSpec((K, tn), lambda m, n: (0, n))],
        out_specs=pl.BlockSpec((tm, K), lambda m, n: (m, 0)),
        out_shape=jax.ShapeDtypeStruct((M, K), out_dtype),
        compiler_params=_cp(("parallel", "arbitrary")),
    )(dy, w)


NORM_ROWS = 1024


def _acc_then_norm_bwd(part, steps, h_ref, g_ref, r_ref, o_ref, dg_ref):
    k = pl.program_id(1)

    @pl.when((pl.program_id(0) == 0) & (k == 0))
    def _():
        dg_ref[...] = jnp.zeros_like(dg_ref)

    @pl.when(k == 0)
    def _():
        o_ref[...] = part

    @pl.when(k > 0)
    def _():
        o_ref[...] += part

    @pl.when(k == steps - 1)
    def _():
        _, vjp = jax.vjp(_rms, h_ref[...], g_ref[...])
        dh, dg = vjp(o_ref[...])
        o_ref[...] = r_ref[...] + dh
        dg_ref[...] += dg


def mm_nt_norm(dy, w, norm, name):
    pieces = dy if isinstance(dy, tuple) else (dy,)
    nd = len(pieces)
    M, N = pieces[0].shape[0], sum(p.shape[1] for p in pieces)
    tm, tn = (min(NORM_ROWS, M), _tile(N, 1024)) if nd == 1 else (min(512, M), N)

    def body(*refs):
        w_ref, h_ref, g_ref, r_ref, o_ref, dg_ref = refs[nd:]
        _acc_then_norm_bwd(_dg(_cols(refs[:nd]).astype(bf16), w_ref[...], 1, 1), N // tn, h_ref, g_ref, r_ref, o_ref,
                           dg_ref)

    tok = pl.BlockSpec((tm, D), lambda m, n: (m, 0))
    vec = pl.BlockSpec((1, D), lambda m, n: (0, 0))
    return pl.pallas_call(
        body, name=name, grid=(M // tm, N // tn),
        in_specs=[pl.BlockSpec((tm, tn if nd == 1 else p.shape[1]), lambda m, n: (m, n)) for p in pieces]
        + [pl.BlockSpec((D, tn), lambda m, n: (0, n)), tok, vec, tok],
        out_specs=[tok, vec],
        out_shape=[jax.ShapeDtypeStruct((M, D), f32), jax.ShapeDtypeStruct((1, D), f32)],
        compiler_params=_cp(("arbitrary", "arbitrary")),
    )(*pieces, w, norm[0], norm[1].reshape(1, D), norm[2])


def _cols(refs):
    return refs[0][...] if len(refs) == 1 else jnp.concatenate([r[...] for r in refs], axis=1)


def mm_tn(a, dy, name="mm_tn"):
    pieces = a if isinstance(a, tuple) else (a,)
    dpieces = dy if isinstance(dy, tuple) else (dy,)
    na, nd = len(pieces), len(dpieces)
    M, K = pieces[0].shape[0], sum(p.shape[1] for p in pieces)
    N = sum(p.shape[1] for p in dpieces)
    tm = min(MM_ROWS, M)
    tk = _tile(K, 1408) if na == 1 else K
    tn = _tile(N, 1024) if nd == 1 else N

    def body(*refs):
        o_ref = refs[-1]

        @pl.when(pl.program_id(2) == 0)
        def _():
            o_ref[...] = jnp.zeros_like(o_ref)
        o_ref[...] += _dg(_cols(refs[:na]).astype(bf16), _cols(refs[na:na + nd]).astype(bf16), 0, 0)

    a_specs = [pl.BlockSpec((tm, tk if na == 1 else p.shape[1]), lambda k, n, m: (m, k)) for p in pieces]
    d_specs = [pl.BlockSpec((tm, tn if nd == 1 else p.shape[1]), lambda k, n, m: (m, n)) for p in dpieces]
    return pl.pallas_call(
        body, name=name, grid=(K // tk, N // tn, M // tm), in_specs=a_specs + d_specs,
        out_specs=pl.BlockSpec((tk, tn), lambda k, n, m: (k, n)),
        out_shape=jax.ShapeDtypeStruct((K, N), f32),
        compiler_params=_cp(("parallel", "parallel", "arbitrary")),
    )(*pieces, *dpieces)


def rms_fwd(h, g, name):
    S = h.shape[0]
    t = min(512, S)

    def body(h_ref, g_ref, o_ref):
        o_ref[...] = _rms(h_ref[...], g_ref[...]).astype(bf16)

    return pl.pallas_call(
        body, name=name, grid=(S // t,),
        in_specs=[pl.BlockSpec((t, D), lambda i: (i, 0)), pl.BlockSpec((1, D), lambda i: (0, 0))],
        out_specs=pl.BlockSpec((t, D), lambda i: (i, 0)),
        out_shape=jax.ShapeDtypeStruct((S, D), bf16),
        compiler_params=_cp(("parallel",)),
    )(h, g.reshape(1, D))


def loss_head(h, g, target):
    S = h.shape[0]
    t = min(512, S)

    def f(hh, gg, tt):
        err = _rms(hh, gg) - tt
        return 0.5 * jnp.sum(jnp.mean(err * err, axis=-1, keepdims=True), axis=0, keepdims=True)

    def body(h_ref, g_ref, t_ref, loss_ref, dh_ref, dg_ref):
        @pl.when(pl.program_id(0) == 0)
        def _():
            dg_ref[...] = jnp.zeros_like(dg_ref)
            loss_ref[...] = jnp.zeros_like(loss_ref)
        val, vjp = jax.vjp(lambda a, b: f(a, b, t_ref[...]), h_ref[...], g_ref[...])
        dh, dg = vjp(jnp.ones((1, 1), f32))
        dh_ref[...] = dh
        dg_ref[...] += dg
        loss_ref[...] += jnp.broadcast_to(val, loss_ref.shape)

    tok = pl.BlockSpec((t, D), lambda i: (i, 0))
    vec = pl.BlockSpec((1, D), lambda i: (0, 0))
    return pl.pallas_call(
        body, name="loss_head", grid=(S // t,), in_specs=[tok, vec, tok],
        out_specs=[pl.BlockSpec((1, LANE), lambda i: (0, 0)), tok, vec],
        out_shape=[jax.ShapeDtypeStruct((1, LANE), f32), jax.ShapeDtypeStruct((S, D), f32),
                   jax.ShapeDtypeStruct((1, D), f32)],
        compiler_params=_cp(("arbitrary",)),
    )(h, g.reshape(1, D), target)


def memkv_fwd(mem, g, w, name):
    def body(m_ref, g_ref, w_ref, o_ref):
        o_ref[...] = _dg(_rms(m_ref[...], g_ref[...]).astype(bf16), w_ref[...], 1, 0)

    return pl.pallas_call(
        body, name=name, out_shape=jax.ShapeDtypeStruct((MEM_LEN, 2 * X_Q), f32), compiler_params=_cp(),
    )(mem, g.reshape(1, D), w)


def memkv_bwd(mem, g, w, dkv, name):
    def body(m_ref, g_ref, w_ref, d_ref, dg_ref, dw_ref):
        n, vjp = jax.vjp(lambda gg: _rms(m_ref[...], gg), g_ref[...])
        db = d_ref[...].astype(bf16)
        dw_ref[...] = _dg(n.astype(bf16), db, 0, 0)
        dg_ref[...] = vjp(_dg(db, w_ref[...], 1, 1))[0]

    return pl.pallas_call(
        body, name=name,
        out_shape=[jax.ShapeDtypeStruct((1, D), f32), jax.ShapeDtypeStruct((D, 2 * X_Q), f32)],
        compiler_params=_cp(),
    )(mem, g.reshape(1, D), w, dkv)


def _xattn_f(xq, mk, mv):
    lane = lax.broadcasted_iota(jnp.int32, (1, X_Q), 1)
    out = jnp.zeros(xq.shape, f32)
    for hd in range(4):
        msk = (lane // 64 == hd).astype(f32)
        s = bdot_nt(xq * msk, mk) * (64 ** -0.5)
        m = lax.stop_gradient(jnp.max(s, axis=-1, keepdims=True))
        p = jnp.exp(s - m)
        p = p / jnp.sum(p, axis=-1, keepdims=True)
        out = out + bdot(p, mv * msk)
    return out


def xattn_fwd(proj, col, kv, name):
    S = proj.shape[0]
    t = min(512, S)
    cb = col // X_Q

    def body(q_ref, k_ref, v_ref, o_ref):
        o_ref[...] = _xattn_f(q_ref[...], k_ref[...], v_ref[...]).astype(bf16)

    return pl.pallas_call(
        body, name=name, grid=(S // t,),
        in_specs=[pl.BlockSpec((t, X_Q), lambda i: (i, cb)), pl.BlockSpec((MEM_LEN, X_Q), lambda i: (0, 0)),
                  pl.BlockSpec((MEM_LEN, X_Q), lambda i: (0, 1))],
        out_specs=pl.BlockSpec((t, X_Q), lambda i: (i, 0)),
        out_shape=jax.ShapeDtypeStruct((S, X_Q), bf16),
        compiler_params=_cp(("parallel",)),
    )(proj, kv, kv)


def xattn_bwd(proj, col, kv, dmix, name):
    S = proj.shape[0]
    t = min(512, S)
    cb = col // X_Q

    def body(q_ref, k_ref, v_ref, do_ref, dq_ref, dk_ref, dv_ref):
        @pl.when(pl.program_id(0) == 0)
        def _():
            dk_ref[...] = jnp.zeros_like(dk_ref)
            dv_ref[...] = jnp.zeros_like(dv_ref)
        _, vjp = jax.vjp(_xattn_f, q_ref[...], k_ref[...], v_ref[...])
        dq, dk, dv = vjp(do_ref[...])
        dq_ref[...] = dq.astype(bf16)
        dk_ref[...] += dk
        dv_ref[...] += dv

    kvb = pl.BlockSpec((MEM_LEN, X_Q), lambda i: (0, 0))
    dq, dk, dv = pl.pallas_call(
        body, name=name, grid=(S // t,),
        in_specs=[pl.BlockSpec((t, X_Q), lambda i: (i, cb)), kvb,
                  pl.BlockSpec((MEM_LEN, X_Q), lambda i: (0, 1)), pl.BlockSpec((t, X_Q), lambda i: (i, 3))],
        out_specs=[pl.BlockSpec((t, X_Q), lambda i: (i, 0)), kvb, kvb],
        out_shape=[jax.ShapeDtypeStruct((S, X_Q), bf16), jax.ShapeDtypeStruct((MEM_LEN, X_Q), f32),
                   jax.ShapeDtypeStruct((MEM_LEN, X_Q), f32)],
        compiler_params=_cp(("arbitrary",)),
    )(proj, kv, kv, dmix)
    return dq, jnp.concatenate([dk, dv], axis=1)


def _bucket_map():
    qi = np.arange(BLK)[:, None]
    kj = np.arange(2 * BLK)[None, :]
    n = np.maximum(BLK + qi - kj, 0)
    max_exact = N_BUCKETS // 2
    nf = np.maximum(n, 1).astype(np.float64)
    large = max_exact + (np.log(nf / max_exact) / math.log(MAX_DIST / max_exact)
                         * (N_BUCKETS - max_exact)).astype(np.int32)
    large = np.minimum(large, N_BUCKETS - 1)
    return np.where(n < max_exact, n, large).astype(np.int32)


def bias_build(rel_bias):
    def body(rb_ref, bk_ref, o_ref):
        bk = bk_ref[...]
        for h in range(A_HEADS):
            acc = jnp.zeros((BLK, 2 * BLK), f32)
            for b in range(N_BUCKETS):
                acc = jnp.where(bk == b, rb_ref[b, h], acc)
            o_ref[h] = acc

    return pl.pallas_call(
        body, name="bias_build",
        in_specs=[pl.BlockSpec(memory_space=pltpu.SMEM), pl.BlockSpec(memory_space=pltpu.VMEM)],
        out_specs=pl.BlockSpec(memory_space=pltpu.VMEM),
        out_shape=jax.ShapeDtypeStruct((A_HEADS, BLK, 2 * BLK), f32), compiler_params=_cp(),
    )(rel_bias, jnp.asarray(_bucket_map()))


def bias_grad(dbias):
    def body(d_ref, bk_ref, o_ref):
        bk = bk_ref[...]
        row = lax.broadcasted_iota(jnp.int32, (N_BUCKETS, LANE), 0)
        lane = lax.broadcasted_iota(jnp.int32, (N_BUCKETS, LANE), 1)
        acc = jnp.zeros((N_BUCKETS, LANE), f32)
        for h in range(A_HEADS):
            d = d_ref[h]
            for b in range(N_BUCKETS):
                s = jnp.sum(jnp.where(bk == b, d, 0.0), keepdims=True)
                acc = acc + jnp.where((row == b) & (lane == h), s, 0.0)
        o_ref[...] = acc

    return pl.pallas_call(
        body, name="bias_grad", out_shape=jax.ShapeDtypeStruct((N_BUCKETS, LANE), f32), compiler_params=_cp(),
    )(dbias, jnp.asarray(_bucket_map()))


def _swa_f(qb, kp, kc, vp, vc, bias, sk, first):
    kband = jnp.concatenate([kp, kc], axis=0)
    vband = jnp.concatenate([vp, vc], axis=0)
    qi = lax.broadcasted_iota(jnp.int32, (BLK, 2 * BLK), 0)
    kj = lax.broadcasted_iota(jnp.int32, (BLK, 2 * BLK), 1)
    rel = kj - qi
    ok = (rel >= 1) & (rel <= BLK) & ((kj >= BLK) | jnp.logical_not(first))
    lane = lax.broadcasted_iota(jnp.int32, (1, LANE), 1)
    lane_b = lax.broadcasted_iota(jnp.int32, (BLK, LANE), 1)
    outs = []
    for p in range(A_HEADS // 2):
        qp = qb[:, LANE * p:LANE * (p + 1)]
        acc = jnp.zeros((BLK, LANE), f32)
        for g in range(2):
            h = g * (A_HEADS // 2) + p
            msk = (lane // A_DH == g).astype(f32)
            s = bdot_nt(qp * msk, kband) * (A_DH ** -0.5) + bias[h]
            s = jnp.where(ok, s, -1e30)
            skb = jnp.broadcast_to(sk[h:h + 1, :], (BLK, LANE))
            sink = jnp.sum(jnp.where(lane_b == 0, skb, 0.0), axis=-1, keepdims=True)
            m = lax.stop_gradient(jnp.maximum(jnp.max(s, axis=-1, keepdims=True), sink))
            e = jnp.exp(s - m)
            prob = e / (jnp.sum(e, axis=-1, keepdims=True) + jnp.exp(sink - m))
            acc = acc + bdot(prob, vband) * msk
        outs.append(acc)
    return jnp.concatenate(outs, axis=1)


def _swa_specs(nb, rev):
    bi = (lambda i: nb - 1 - i) if rev else (lambda i: i)
    return [
        pl.BlockSpec((BLK, A_Q), lambda i: (bi(i), 0)),
        pl.BlockSpec((BLK, LANE), lambda i: (jnp.maximum(bi(i) - 1, 0), 6)),
        pl.BlockSpec((BLK, LANE), lambda i: (bi(i), 6)),
        pl.BlockSpec((BLK, LANE), lambda i: (jnp.maximum(bi(i) - 1, 0), 7)),
        pl.BlockSpec((BLK, LANE), lambda i: (bi(i), 7)),
        pl.BlockSpec((A_HEADS, BLK, 2 * BLK), lambda i: (0, 0, 0)),
        pl.BlockSpec((16, LANE), lambda i: (0, 0)),
    ]


def swa_fwd(proj, bias, sk):
    S = proj.shape[0]
    nb = S // BLK

    def body(q_ref, kp_ref, kc_ref, vp_ref, vc_ref, b_ref, s_ref, o_ref):
        o_ref[...] = _swa_f(q_ref[...], kp_ref[...], kc_ref[...], vp_ref[...], vc_ref[...], b_ref[...], s_ref[...],
                            pl.program_id(0) == 0).astype(bf16)

    return pl.pallas_call(
        body, name="swa_fwd", grid=(nb,), in_specs=_swa_specs(nb, False),
        out_specs=pl.BlockSpec((BLK, A_Q), lambda i: (i, 0)),
        out_shape=jax.ShapeDtypeStruct((S, A_Q), bf16), compiler_params=_cp(("parallel",)),
    )(proj, proj, proj, proj, proj, bias, sk)


def swa_bwd(proj, bias, sk, dmix):
    S = proj.shape[0]
    nb = S // BLK

    def body(q_ref, kp_ref, kc_ref, vp_ref, vc_ref, b_ref, s_ref, do_ref, dqkv_ref, db_ref, ds_ref, ck, cv):
        i = pl.program_id(0)

        @pl.when(i == 0)
        def _():
            db_ref[...] = jnp.zeros_like(db_ref)
            ds_ref[...] = jnp.zeros_like(ds_ref)
            ck[...] = jnp.zeros_like(ck)
            cv[...] = jnp.zeros_like(cv)
        first = i == nb - 1
        _, vjp = jax.vjp(lambda *a: _swa_f(*a, first), q_ref[...], kp_ref[...], kc_ref[...], vp_ref[...],
                         vc_ref[...], b_ref[...], s_ref[...])
        dq, dkp, dkc, dvp, dvc, db, ds = vjp(do_ref[...])
        dqkv_ref[...] = jnp.concatenate([dq, dkc + ck[...], dvc + cv[...]], axis=1).astype(bf16)
        ck[...] = dkp
        cv[...] = dvp
        db_ref[...] += db
        ds_ref[...] += ds

    return pl.pallas_call(
        body, name="swa_bwd", grid=(nb,),
        in_specs=_swa_specs(nb, True) + [pl.BlockSpec((BLK, A_Q), lambda i: (nb - 1 - i, 0))],
        out_specs=[pl.BlockSpec((BLK, D), lambda i: (nb - 1 - i, 0)),
                   pl.BlockSpec((A_HEADS, BLK, 2 * BLK), lambda i: (0, 0, 0)),
                   pl.BlockSpec((16, LANE), lambda i: (0, 0))],
        out_shape=[jax.ShapeDtypeStruct((S, D), bf16), jax.ShapeDtypeStruct((A_HEADS, BLK, 2 * BLK), f32),
                   jax.ShapeDtypeStruct((16, LANE), f32)],
        scratch_shapes=[pltpu.VMEM((BLK, LANE), f32), pltpu.VMEM((BLK, LANE), f32)],
        compiler_params=_cp(("arbitrary",)),
    )(proj, proj, proj, proj, proj, bias, sk, dmix)


def _dnprep_f(xext, w, is_qk):
    c = (w[3:4] * xext + w[2:3] * shift_down(xext, 1) + w[1:2] * shift_down(xext, 2) + w[0:1] * shift_down(xext, 3))
    a = _silu(c)[HALO:]
    n = a * lax.rsqrt(jnp.sum(a * a, axis=-1, keepdims=True) + EPS)
    return jnp.where(is_qk, n, a)


def dnprep_fwd(proj, cw):
    S = proj.shape[0]
    nblk = B_QKV // LANE
    T = S

    def body(x_ref, w_ref, o_ref):
        is_qk = pl.program_id(0) < 2 * B_QK // LANE
        wv = w_ref[...]

        def tile(r0, first):
            o_ref[pl.ds(r0, T), :] = _dnprep_f(_glu_gext(x_ref, r0, first, T), wv, is_qk)

        tile(0, True)

    return pl.pallas_call(
        body, name="dnprep_fwd", grid=(nblk,),
        in_specs=[pl.BlockSpec((S, LANE), lambda j: (0, j)), pl.BlockSpec((4, LANE), lambda j: (0, j))],
        out_specs=pl.BlockSpec((S, LANE), lambda j: (0, j)),
        out_shape=jax.ShapeDtypeStruct((S, B_QKV), f32), compiler_params=_cp(("parallel",)),
    )(proj, cw)


def dnprep_bwd(proj, cw, dqkvn):
    S = proj.shape[0]
    nblk = B_QKV // LANE

    T = S

    def body(x_ref, w_ref, d_ref, dx_ref, dw_ref):
        is_qk = pl.program_id(0) < 2 * B_QK // LANE
        wv = w_ref[...]

        def tile(r0, first):
            _, vjp = jax.vjp(lambda a, b: _dnprep_f(a, b, is_qk), _glu_gext(x_ref, r0, first, T), wv)
            dx, dw = vjp(d_ref[pl.ds(r0, T), :])
            dx_ref[pl.ds(r0, T), :] = dx[HALO:].astype(bf16)
            if not first:
                dx_ref[pl.ds(r0 - HALO, HALO), :] += dx[:HALO]
            return dw

        dw_ref[...] = tile(0, True)

    col = pl.BlockSpec((S, LANE), lambda j: (0, j))
    wsp = pl.BlockSpec((4, LANE), lambda j: (0, j))
    return pl.pallas_call(
        body, name="dnprep_bwd", grid=(nblk,), in_specs=[col, wsp, col], out_specs=[col, wsp],
        out_shape=[jax.ShapeDtypeStruct((S, B_QKV), bf16), jax.ShapeDtypeStruct((4, B_QKV), f32)],
        compiler_params=_cp(("parallel",)),
    )(proj, cw, dqkvn)


def _hdot(a, b, ca=1, cb=0):
    return _dg(a, b, ca, cb, HI)


def _bdg(a, b, ca, cb):
    dn = (((ca,), (cb,)), ((0,), (0,)))
    ah, bh = a.astype(bf16), b.astype(bf16)
    al, bl = (a - ah.astype(f32)).astype(bf16), (b - bh.astype(f32)).astype(bf16)
    return (lax.dot_general(ah, bh, dn, preferred_element_type=f32)
            + lax.dot_general(ah, bl, dn, preferred_element_type=f32)
            + lax.dot_general(al, bh, dn, preferred_element_type=f32))


@jax.custom_vjp
def hbd(a, b):
    return _bdg(a, b, 2, 1)


@jax.custom_vjp
def hbd_nt(a, b):
    return _bdg(a, b, 2, 2)


@jax.custom_vjp
def hbd_tn(a, b):
    return _bdg(a, b, 1, 1)


hbd.defvjp(lambda a, b: (hbd(a, b), (a, b)), lambda r, g: (hbd_nt(g, r[1]), hbd_tn(r[0], g)))
hbd_nt.defvjp(lambda a, b: (hbd_nt(a, b), (a, b)), lambda r, g: (hbd(g, r[1]), hbd_tn(g, r[0])))
hbd_tn.defvjp(lambda a, b: (hbd_tn(a, b), (a, b)), lambda r, g: (hbd_nt(r[1], g), hbd(r[0], g)))


def _stack(xs):
    return jnp.concatenate([x[None] for x in xs], axis=0)


def _lane_col(x, j):
    lane = lax.broadcasted_iota(jnp.int32, (1, LANE), 1)
    return jnp.sum(jnp.where(lane == j, x, 0.0), axis=-1, keepdims=True)


def _tri_inv(a_mat):
    r = lax.broadcasted_iota(jnp.int32, (1, CHUNK, CHUNK), 1)
    c = lax.broadcasted_iota(jnp.int32, (1, CHUNK, CHUNK), 2)
    pw = -a_mat
    inv = (r == c).astype(f32) + pw
    for _ in range(5):
        pw = hbd(pw, pw)
        inv = inv + hbd(inv, pw)
    return inv


@jax.custom_vjp
def _tri_inv_known(a_mat, inv):
    return inv


_tri_inv_known.defvjp(lambda a, inv: (inv, inv),
                      lambda inv, g: (-hbd_tn(inv, hbd_nt(g, inv)), jnp.zeros_like(inv)))


def _dnc_f(q, k, v, seg, prm, inverse=_tri_inv):
    C = CHUNK
    B = q.shape[0]
    rows = seg.shape[0]
    beta_all = _sigmoid(seg)
    xx = seg + prm[1:2]
    g_all = -jnp.exp(prm[0:1]) * (jnp.maximum(xx, 0.0) + jnp.log(1.0 + jnp.exp(-jnp.abs(xx))))
    r2 = lax.broadcasted_iota(jnp.int32, (rows, rows), 0)
    c2 = lax.broadcasted_iota(jnp.int32, (rows, rows), 1)
    within = (r2 >= c2) & (r2 // C == c2 // C)
    gc_all = _hdot(within.astype(f32), g_all)
    beta = _stack([_lane_col(beta_all[C * j:C * (j + 1)], h) for j in range(rows // C) for h in range(6)])
    gc = _stack([_lane_col(gc_all[C * j:C * (j + 1)], 6 + h) for j in range(rows // C) for h in range(6)])
    r = lax.broadcasted_iota(jnp.int32, (1, C, C), 1)
    c = lax.broadcasted_iota(jnp.int32, (1, C, C), 2)
    incl = r >= c
    strict = r > c
    gct = [gc_all[C * j:C * (j + 1)].T for j in range(rows // C)]
    g_row = _stack([jnp.broadcast_to(gct[j][6 + h:7 + h, :], (C, C))
                    for j in range(rows // C) for h in range(6)])
    decay = jnp.where(incl, jnp.exp(jnp.where(incl, gc - g_row, 0.0)), 0.0)
    a_mat = beta * sbd_nt(k, k) * jnp.where(strict, decay, 0.0)
    eg = jnp.exp(gc)
    inv = inverse(a_mat)
    u = hbd(inv, beta * v)
    w = hbd(inv, (beta * eg) * k)
    qc = q * (B_DH ** -0.5)
    attn = sbd_nt(qc, k) * decay
    last = (lax.broadcasted_iota(jnp.int32, (1, C, 1), 1) == C - 1).astype(f32)
    g_last = jnp.sum(gc * last, axis=1, keepdims=True)
    dc = jnp.broadcast_to(jnp.exp(g_last), (B, 1, LANE)).reshape(B, LANE)
    return u, w, qc * eg, k * jnp.exp(g_last - gc), attn, dc, inv


def _b1(a, b, ca, cb):
    return lax.dot_general(a.astype(bf16), b.astype(bf16), (((ca,), (cb,)), ((0,), (0,))), preferred_element_type=f32)


@jax.custom_vjp
def sbd(a, b):
    return _b1(a, b, 2, 1)


@jax.custom_vjp
def sbd_nt(a, b):
    return _b1(a, b, 2, 2)


@jax.custom_vjp
def sbd_tn(a, b):
    return _b1(a, b, 1, 1)


sbd.defvjp(lambda a, b: (sbd(a, b), (a, b)), lambda r, g: (sbd_nt(g, r[1]), sbd_tn(r[0], g)))
sbd_nt.defvjp(lambda a, b: (sbd_nt(a, b), (a, b)), lambda r, g: (sbd(g, r[1]), sbd_tn(g, r[0])))
sbd_tn.defvjp(lambda a, b: (sbd_tn(a, b), (a, b)), lambda r, g: (sbd_nt(r[1], g), sbd(r[0], g)))


def _dns_f(S0, u, w, qd, kt, attn, dcrows):
    dc = _lane_col(dcrows, 0).reshape(6, 1, 1)
    delta = u - sbd(w, S0)
    out = sbd(qd, S0) + sbd(attn, delta)
    return out, dc * S0 + sbd_tn(kt, delta)


def _dnpost_f(o, z, grow):
    outs = []
    for h in range(6):
        oh = o[:, LANE * h:LANE * (h + 1)]
        outs.append(oh * lax.rsqrt(jnp.mean(oh * oh, axis=-1, keepdims=True) + EPS) * grow
                    * _silu(z[:, LANE * h:LANE * (h + 1)]))
    return jnp.concatenate(outs, axis=1)


def _hs(h):
    return slice(LANE * h, LANE * (h + 1))


DN_CHUNKS = 4


def _heads(ref, share):
    return _stack([ref[CHUNK * j:CHUNK * (j + 1), _hs(h // share)]
                   for j in range(ref.shape[0] // CHUNK) for h in range(6)])


def _put_heads(ref, val):
    for j in range(ref.shape[0] // CHUNK):
        for h in range(6):
            ref[CHUNK * j:CHUNK * (j + 1), _hs(h)] = val[6 * j + h].astype(ref.dtype)


def _dnc_in_specs():
    rows = CHUNK * DN_CHUNKS
    return [
        pl.BlockSpec((rows, B_QK), lambda n: (n, 0)),
        pl.BlockSpec((rows, B_QK), lambda n: (n, 1)),
        pl.BlockSpec((rows, B_V), lambda n: (n, 1)),
        pl.BlockSpec((rows, LANE), lambda n: (n, 20)),
        pl.BlockSpec((8, LANE), lambda n: (0, 0)),
    ]


def _dnc_out_specs(rev_nc=None, chunks=1):
    ci = (lambda n: n) if rev_nc is None else (lambda n: rev_nc - 1 - n)
    wide = pl.BlockSpec((CHUNK * chunks, B_V), lambda n: (ci(n), 0))
    return [wide, wide, wide, wide, pl.BlockSpec((chunks, 6, CHUNK, CHUNK), lambda n: (ci(n), 0, 0, 0)),
            pl.BlockSpec((chunks, 8, LANE), lambda n: (ci(n), 0, 0))]


def _dc_rows(dc):
    pad = jnp.zeros((2, LANE), f32)
    return _stack([jnp.concatenate([dc[6 * j:6 * (j + 1)], pad], axis=0) for j in range(dc.shape[0] // 6)])


def _dnc_shapes(S, mm=f32):
    nc = S // CHUNK
    return [jax.ShapeDtypeStruct((S, B_V), f32)] + [jax.ShapeDtypeStruct((S, B_V), mm)] * 3 + [
        jax.ShapeDtypeStruct((nc, 6, CHUNK, CHUNK), mm), jax.ShapeDtypeStruct((nc, 8, LANE), f32)]


def dnc_fwd(qkvn, proj, prm):
    S = proj.shape[0]

    def body(q_ref, k_ref, v_ref, s_ref, p_ref, u_ref, w_ref, qd_ref, kt_ref, at_ref, dc_ref, inv_ref):
        u, w, qd, kt, attn, dc, inv = _dnc_f(_heads(q_ref, 2), _heads(k_ref, 2), _heads(v_ref, 1), s_ref[...],
                                             p_ref[...])
        inv_ref[...] = inv.reshape(inv_ref.shape)
        _put_heads(u_ref, u)
        _put_heads(w_ref, w)
        _put_heads(qd_ref, qd)
        _put_heads(kt_ref, kt)
        at_ref[...] = attn.reshape(at_ref.shape).astype(at_ref.dtype)
        dc_ref[...] = _dc_rows(dc)

    outs = _dnc_out_specs(chunks=DN_CHUNKS)
    out = pl.pallas_call(
        body, name="dn_chunk_fwd", grid=(S // (CHUNK * DN_CHUNKS),), in_specs=_dnc_in_specs(),
        out_specs=outs + [outs[4]], out_shape=_dnc_shapes(S, bf16) + [_dnc_shapes(S)[4]],
        compiler_params=_cp(("parallel",)),
    )(qkvn, qkvn, qkvn, proj, prm)
    return out[:6], out[6]


def dnc_bwd(qkvn, proj, prm, inv, cots):
    S = proj.shape[0]

    def body(q_ref, k_ref, v_ref, s_ref, p_ref, inv_ref, du_ref, dw_ref, dqd_ref, dkt_ref, dat_ref, ddc_ref,
             dx_ref, dseg_ref, dprm_ref):
        @pl.when(pl.program_id(0) == 0)
        def _():
            dprm_ref[...] = jnp.zeros_like(dprm_ref)
        nb = 6 * DN_CHUNKS
        known = functools.partial(_tri_inv_known, inv=inv_ref[...].reshape(nb, CHUNK, CHUNK))
        _, vjp = jax.vjp(lambda *a: _dnc_f(*a, inverse=known)[:6], _heads(q_ref, 2), _heads(k_ref, 2),
                         _heads(v_ref, 1), s_ref[...], p_ref[...])
        ddc = jnp.concatenate([ddc_ref[j, 0:6, :] for j in range(DN_CHUNKS)], axis=0)
        dq, dk, dv, dseg, dprm = vjp((_heads(du_ref, 1), _heads(dw_ref, 1), _heads(dqd_ref, 1), _heads(dkt_ref, 1),
                                      dat_ref[...].reshape(nb, CHUNK, CHUNK), ddc))
        for j in range(DN_CHUNKS):
            o = 6 * j
            dx_ref[CHUNK * j:CHUNK * (j + 1), :] = jnp.concatenate(
                [dq[o] + dq[o + 1], dq[o + 2] + dq[o + 3], dq[o + 4] + dq[o + 5],
                 dk[o] + dk[o + 1], dk[o + 2] + dk[o + 3], dk[o + 4] + dk[o + 5]] + [dv[o + h] for h in range(6)], axis=1)
        dseg_ref[...] = dseg.astype(bf16)
        dprm_ref[...] += dprm

    rows = CHUNK * DN_CHUNKS
    outs = _dnc_out_specs(chunks=DN_CHUNKS)
    return pl.pallas_call(
        body, name="dn_chunk_bwd", grid=(S // rows,),
        in_specs=_dnc_in_specs() + [outs[4]] + outs,
        out_specs=[pl.BlockSpec((rows, B_QKV), lambda n: (n, 0)), pl.BlockSpec((rows, LANE), lambda n: (n, 0)),
                   pl.BlockSpec((8, LANE), lambda n: (0, 0))],
        out_shape=[jax.ShapeDtypeStruct((S, B_QKV), f32), jax.ShapeDtypeStruct((S, LANE), bf16),
                   jax.ShapeDtypeStruct((8, LANE), f32)],
        compiler_params=_cp(("arbitrary",)),
    )(qkvn, qkvn, qkvn, proj, prm, inv, *cots)


def dns_fwd(chunked):
    u = chunked[0]
    S = u.shape[0]
    nc = S // CHUNK

    def body(u_ref, w_ref, qd_ref, kt_ref, at_ref, dc_ref, o_ref, st_ref, st):
        @pl.when(pl.program_id(0) == 0)
        def _():
            st[...] = jnp.zeros_like(st)
        S0 = st[...]
        st_ref[0] = S0
        out, S1 = _dns_f(S0, _heads(u_ref, 1), _heads(w_ref, 1), _heads(qd_ref, 1), _heads(kt_ref, 1),
                         at_ref[0], dc_ref[0, 0:6, :])
        _put_heads(o_ref, out)
        st[...] = S1

    return pl.pallas_call(
        body, name="dn_scan_fwd", grid=(nc,), in_specs=_dnc_out_specs(),
        out_specs=[pl.BlockSpec((CHUNK, B_V), lambda n: (n, 0)),
                   pl.BlockSpec((1, 6, B_DH, B_DH), lambda n: (n, 0, 0, 0))],
        out_shape=[jax.ShapeDtypeStruct((S, B_V), f32), jax.ShapeDtypeStruct((nc, 6, B_DH, B_DH), f32)],
        scratch_shapes=[pltpu.VMEM((6, B_DH, B_DH), f32)],
        compiler_params=_cp(("arbitrary",)),
    )(*chunked)


def dns_bwd(chunked, states, do):
    S = do.shape[0]
    nc = S // CHUNK

    def body(u_ref, w_ref, qd_ref, kt_ref, at_ref, dc_ref, st_ref, do_ref,
             du_ref, dw_ref, dqd_ref, dkt_ref, dat_ref, ddc_ref, dst):
        @pl.when(pl.program_id(0) == 0)
        def _():
            dst[...] = jnp.zeros_like(dst)
        _, vjp = jax.vjp(_dns_f, st_ref[0], _heads(u_ref, 1), _heads(w_ref, 1).astype(f32),
                         _heads(qd_ref, 1).astype(f32), _heads(kt_ref, 1).astype(f32), at_ref[0].astype(f32),
                         dc_ref[0, 0:6, :])
        dS0, du, dw, dqd, dkt, dat, ddc = vjp((_heads(do_ref, 1), dst[...]))
        dst[...] = dS0
        _put_heads(du_ref, du)
        _put_heads(dw_ref, dw)
        _put_heads(dqd_ref, dqd)
        _put_heads(dkt_ref, dkt)
        dat_ref[0] = dat
        ddc_ref[0] = jnp.concatenate([ddc, jnp.zeros((2, LANE), f32)], axis=0)

    return pl.pallas_call(
        body, name="dn_scan_bwd", grid=(nc,),
        in_specs=_dnc_out_specs(nc) + [pl.BlockSpec((1, 6, B_DH, B_DH), lambda n: (nc - 1 - n, 0, 0, 0)),
                                       pl.BlockSpec((CHUNK, B_V), lambda n: (nc - 1 - n, 0))],
        out_specs=_dnc_out_specs(nc), out_shape=_dnc_shapes(S),
        scratch_shapes=[pltpu.VMEM((6, B_DH, B_DH), f32)],
        compiler_params=_cp(("arbitrary",)),
    )(*chunked, states, do)


def dnpost_fwd(o, proj, prm):
    S = o.shape[0]
    t = min(512, S)

    def body(o_ref, z_ref, p_ref, y_ref):
        y_ref[...] = _dnpost_f(o_ref[...], z_ref[...], p_ref[2:3, :]).astype(bf16)

    tok = pl.BlockSpec((t, B_V), lambda i: (i, 0))
    return pl.pallas_call(
        body, name="dn_post_fwd", grid=(S // t,),
        in_specs=[tok, pl.BlockSpec((t, B_V), lambda i: (i, 2)), pl.BlockSpec((8, LANE), lambda i: (0, 0))],
        out_specs=tok, out_shape=jax.ShapeDtypeStruct((S, B_V), bf16), compiler_params=_cp(("parallel",)),
    )(o, proj, prm)


def dnpost_bwd(o, proj, prm, dmix):
    S = o.shape[0]
    t = min(512, S)

    def body(o_ref, z_ref, p_ref, dy_ref, do_ref, dz_ref, dg_ref):
        @pl.when(pl.program_id(0) == 0)
        def _():
            dg_ref[...] = jnp.zeros_like(dg_ref)
        _, vjp = jax.vjp(_dnpost_f, o_ref[...], z_ref[...], p_ref[2:3, :])
        do, dz, dg = vjp(dy_ref[...])
        do_ref[...] = do
        dz_ref[...] = dz.astype(bf16)
        dg_ref[...] += dg

    tok = pl.BlockSpec((t, B_V), lambda i: (i, 0))
    return pl.pallas_call(
        body, name="dn_post_bwd", grid=(S // t,),
        in_specs=[tok, pl.BlockSpec((t, B_V), lambda i: (i, 2)), pl.BlockSpec((8, LANE), lambda i: (0, 0)), tok],
        out_specs=[tok, tok, pl.BlockSpec((1, LANE), lambda i: (0, 0))],
        out_shape=[jax.ShapeDtypeStruct((S, B_V), f32), jax.ShapeDtypeStruct((S, B_V), bf16),
                   jax.ShapeDtypeStruct((1, LANE), f32)],
        compiler_params=_cp(("arbitrary",)),
    )(o, proj, prm, dmix)


N_FF_BLK = D_FF // LANE
GU_SHARD = 2 * D_FF // 4


GLU_ROWS = 256
HALO = 16


def _glu_conv(gext, w, b):
    return (w[2:3] * gext + w[1:2] * shift_down(gext, 1) + w[0:1] * shift_down(gext, 2) + b)[HALO:]


def _glu_gate(c, up):
    return _silu(c) * up


def _glu_gext(g_ref, r0, first, T=GLU_ROWS):
    if first:
        return jnp.concatenate([jnp.zeros((HALO, LANE), f32), g_ref[0:T, :].astype(f32)], axis=0)
    return g_ref[pl.ds(r0 - HALO, T + HALO), :].astype(f32)


def glu_fwd(gu, w, b, name):
    S = gu.shape[0]
    T = min(GLU_ROWS, S // 2)

    def body(g_ref, u_ref, w_ref, b_ref, o_ref, c_ref):
        wv, bv = w_ref[...], b_ref[...]

        def tile(r0, first):
            c = _glu_conv(_glu_gext(g_ref, r0, first, T), wv, bv)
            c_ref[pl.ds(r0, T), :] = c.astype(bf16)
            o_ref[pl.ds(r0, T), :] = _glu_gate(c, u_ref[pl.ds(r0, T), :].astype(f32)).astype(bf16)

        tile(0, True)

        @pl.loop(1, S // T)
        def _(t):
            tile(pl.multiple_of(t * T, T), False)

    col = pl.BlockSpec((S, LANE), lambda j: (0, j))
    return pl.pallas_call(
        body, name=name, grid=(N_FF_BLK,),
        in_specs=[col, pl.BlockSpec((S, LANE), lambda j: (0, N_FF_BLK + j)), pl.BlockSpec((3, LANE), lambda j: (0, j)),
                  pl.BlockSpec((1, LANE), lambda j: (0, j))],
        out_specs=[col, col], out_shape=[jax.ShapeDtypeStruct((S, D_FF), bf16)] * 2,
        compiler_params=_cp(("parallel",)),
    )(gu, gu, w, b.reshape(1, D_FF))


def glu_bwd(gu, c, w, b, dact, name):
    S = gu.shape[0]
    T = min(GLU_ROWS, S // 2)

    def body(g_ref, u_ref, c_ref, w_ref, b_ref, d_ref, dg_ref, dw_ref, db_ref, acc):
        wv, bv = w_ref[...], b_ref[...]

        def tile(r0, first):
            rows = pl.ds(r0, T)
            _, vjp_gate = jax.vjp(_glu_gate, c_ref[rows, :].astype(f32), u_ref[rows, :].astype(f32))
            dc, du = vjp_gate(d_ref[rows, :].astype(f32))
            _, vjp_conv = jax.vjp(_glu_conv, _glu_gext(g_ref, r0, first, T), wv, bv)
            dgx, dw, db = vjp_conv(dc)
            acc[pl.ds(r0, T), :] = dgx[HALO:]
            if not first:
                acc[pl.ds(r0 - HALO, HALO), :] += dgx[:HALO]
            dg_ref[1, pl.ds(r0, T), :] = du.astype(bf16)
            return dw, db

        dw0, db0 = tile(0, True)
        dw_ref[...] = dw0
        db_ref[...] = db0

        @pl.loop(1, S // T)
        def _(t):
            dw, db = tile(pl.multiple_of(t * T, T), False)
            dw_ref[...] += dw
            db_ref[...] += db

        dg_ref[0] = acc[...].astype(bf16)

    col = pl.BlockSpec((S, LANE), lambda j: (0, j))
    wsp = pl.BlockSpec((3, LANE), lambda j: (0, j))
    bsp = pl.BlockSpec((1, LANE), lambda j: (0, j))
    return pl.pallas_call(
        body, name=name, grid=(N_FF_BLK,),
        in_specs=[col, pl.BlockSpec((S, LANE), lambda j: (0, N_FF_BLK + j)), col, wsp, bsp, col],
        out_specs=[pl.BlockSpec((2, S, LANE), lambda j: (0, 0, j)), wsp, bsp],
        out_shape=[jax.ShapeDtypeStruct((2, S, D_FF), bf16), jax.ShapeDtypeStruct((3, D_FF), f32),
                   jax.ShapeDtypeStruct((1, D_FF), f32)],
        scratch_shapes=[pltpu.VMEM((S, LANE), f32)],
        compiler_params=_cp(("parallel",)),
    )(gu, gu, c, w, b.reshape(1, D_FF), dact)


def gu_fwd(n2, wg, name):
    S = n2.shape[0]
    tm = min(MM_ROWS, S)

    def body(a_ref, w_ref, o_ref):
        o_ref[...] = _dg(a_ref[...], w_ref[...], 1, 0).astype(bf16)

    return pl.pallas_call(
        body, name=name, grid=(4, S // tm),
        in_specs=[pl.BlockSpec((tm, D), lambda s, m: (m, 0)), pl.BlockSpec((None, D, GU_SHARD), lambda s, m: (s, 0, 0))],
        out_specs=pl.BlockSpec((tm, GU_SHARD), lambda s, m: (m, s)),
        out_shape=jax.ShapeDtypeStruct((S, 2 * D_FF), bf16), compiler_params=_cp(("parallel", "parallel")),
    )(n2, wg)


def gu_bwd_x(dgu, wg, norm, name):
    S = dgu.shape[1]
    tm = min(NORM_ROWS, S)

    def body(d_ref, w_ref, h_ref, g_ref, r_ref, o_ref, dg_ref):
        _acc_then_norm_bwd(_dg(d_ref[...], w_ref[...], 1, 1), 4, h_ref, g_ref, r_ref, o_ref, dg_ref)

    tok = pl.BlockSpec((tm, D), lambda m, s: (m, 0))
    vec = pl.BlockSpec((1, D), lambda m, s: (0, 0))
    return pl.pallas_call(
        body, name=name, grid=(S // tm, 4),
        in_specs=[pl.BlockSpec((None, tm, GU_SHARD), lambda m, s: (s // 2, m, s % 2)),
                  pl.BlockSpec((None, D, GU_SHARD), lambda m, s: (s, 0, 0)), tok, vec, tok],
        out_specs=[tok, vec],
        out_shape=[jax.ShapeDtypeStruct((S, D), f32), jax.ShapeDtypeStruct((1, D), f32)],
        compiler_params=_cp(("arbitrary", "arbitrary")),
    )(dgu, wg, norm[0], norm[1].reshape(1, D), norm[2])


def gu_bwd_w(n2, dgu, name):
    S = n2.shape[0]
    tm = min(MM_ROWS, S)
    nm = S // tm

    def body(a_ref, d_ref, o_ref, acc):
        @pl.when(pl.program_id(1) == 0)
        def _():
            acc[...] = jnp.zeros_like(acc)
        acc[...] += _dg(a_ref[...], d_ref[...], 0, 0)

        @pl.when(pl.program_id(1) == nm - 1)
        def _():
            o_ref[...] = acc[...].astype(bf16)

    return pl.pallas_call(
        body, name=name, grid=(4, nm),
        in_specs=[pl.BlockSpec((tm, D), lambda s, m: (m, 0)),
                  pl.BlockSpec((None, tm, GU_SHARD), lambda s, m: (s // 2, m, s % 2))],
        out_specs=pl.BlockSpec((None, D, GU_SHARD), lambda s, m: (s, 0, 0)),
        out_shape=jax.ShapeDtypeStruct((4, D, GU_SHARD), bf16),
        scratch_shapes=[pltpu.VMEM((D, GU_SHARD), f32)],
        compiler_params=_cp(("parallel", "arbitrary")),
    )(n2, dgu)


def _pair_cols(w):
    lead = w.shape[:-1]
    return w.reshape(lead + (2, 6, A_DH)).swapaxes(-3, -2).reshape(lead + (A_Q,))


def _unpair_cols(w):
    lead = w.shape[:-1]
    return w.reshape(lead + (6, 2, A_DH)).swapaxes(-3, -2).reshape(lead + (A_Q,))


def _lay_in_a(w):
    return jnp.concatenate([_pair_cols(w[:, :A_Q]), w[:, A_Q:]], axis=1)


def _unlay_in_a(w):
    return jnp.concatenate([_unpair_cols(w[:, :A_Q]), w[:, A_Q:]], axis=1)


def _lay_out_a(w):
    return jnp.concatenate([_pair_cols(w[:A_Q].T).T, w[A_Q:]], axis=0)


def _unlay_out_a(w):
    return jnp.concatenate([_unpair_cols(w[:A_Q].T).T, w[A_Q:]], axis=0)


def _lay_in_b(w):
    return jnp.concatenate([w[:, :2304], w[:, 2316:], w[:, 2304:2316],
                            jnp.zeros((w.shape[0], LANE - 12), w.dtype)], axis=1)


def _unlay_in_b(w):
    return jnp.concatenate([w[:, :2304], w[:, 2560:2572], w[:, 2304:2560]], axis=1)


def _chip_cols(w):
    return jnp.moveaxis(w.reshape(w.shape[0], 4, w.shape[1] // 4), 1, 0)


def _unchip_cols(w):
    return jnp.moveaxis(w, 0, 1).reshape(w.shape[1], 4 * w.shape[2])


def _local_step(x, mem, target, P):
    arrive = P.get("arrive", lambda key, after: None)
    ready = P.get("ready", lambda key, grads, dep: dep)
    sk = jnp.zeros((16, LANE), f32).at[:A_HEADS].set(jnp.broadcast_to(P["sinks"][:, None], (A_HEADS, LANE)))
    prm = jnp.zeros((8, LANE), f32).at[0, 6:12].set(P["a_log"]).at[1, 6:12].set(P["dt_bias"]).at[2].set(P["out_norm_g"])
    bias = bias_build(P["rel_bias"])
    saved = []
    h = x
    n1 = rms_fwd(h, P["g_mix"][0], "rms_mix0")
    for i in range(2):
        arrive(("w_in", i), n1)
        proj = mm_nn(n1, P["w_in_a"] if i == 0 else P["w_in_b"], name="proj_a" if i == 0 else "proj_b")
        arrive(("w_mem", i), proj)
        kv = memkv_fwd(mem, P["g_mem"][i], P["w_mem"][i], f"memkv{i}")
        if i == 0:
            self_out = swa_fwd(proj, bias, sk)
            cross = xattn_fwd(proj, A_Q + 2 * LANE, kv, "xattn_a")
            extra = ()
        else:
            qkvn = dnprep_fwd(proj, P["conv_qkv"])
            chunked, inv = dnc_fwd(qkvn, proj, prm)
            o, states = dns_fwd(chunked)
            self_out = dnpost_fwd(o, proj, prm)
            cross = xattn_fwd(proj, 2304, kv, "xattn_b")
            extra = (qkvn, chunked, inv, states, o)
        mix = (self_out, cross)
        arrive(("w_out", i), cross)
        h2, n2 = mm_res_norm(mix, P["w_out"][i], h, P["g_ffn"][i], f"out_proj{i}")
        arrive(("w_gu", i), n2)
        gu = gu_fwd(n2, P["w_gu"][i], f"gate_up{i}")
        act, pre = glu_fwd(gu, P["ffn_cw"][i], P["ffn_cb"][i], f"glu{i}")
        arrive(("w_down", i), act)
        saved.append((h, n1, kv, proj, mix, h2, n2, gu, pre, act, extra))
        if i == 0:
            h, n1 = mm_res_norm(act, P["w_down"][i], h2, P["g_mix"][1], f"down{i}")
        else:
            h = mm_nn(act, P["w_down"][i], res=h2, name=f"down{i}")

    loss, dh, dg_fin = loss_head(h, P["g_fin"], target)
    G = {"g_fin": dg_fin[0], "g_mix": [None, None], "g_mem": [None, None], "g_ffn": [None, None],
         "w_mem": [None, None], "w_out": [None, None], "w_gu": [None, None], "w_down": [None, None],
         "ffn_cw": [None, None], "ffn_cb": [None, None]}
    for i in (1, 0):
        hin, n1, kv, proj, mix, h2, n2, gu, pre, act, extra = saved[i]
        dact = mm_nt(dh, P["w_down"][i], out_dtype=bf16, name=f"d_act{i}")
        G["w_down"][i] = mm_tn(act, dh, name=f"dw_down{i}")
        dgu, dcw, dcb = glu_bwd(gu, pre, P["ffn_cw"][i], P["ffn_cb"][i], dact, f"glu_bwd{i}")
        G["ffn_cw"][i], G["ffn_cb"][i] = dcw, dcb[0]
        G["w_gu"][i] = gu_bwd_w(n2, dgu, f"dw_gu{i}")
        g_ffn = ready(("ffn", i), G, P["g_ffn"][i])
        dh2, dg = gu_bwd_x(dgu, P["w_gu"][i], (h2, g_ffn, dh), f"d_n2_{i}")
        G["g_ffn"][i] = dg[0]
        dmix = mm_nt(dh2, P["w_out"][i], name=f"d_mix{i}")
        G["w_out"][i] = mm_tn(mix, dh2, name=f"dw_out{i}")
        if i == 0:
            dqkv, dbias, dsk = swa_bwd(proj, bias, sk, dmix)
            dxq, dkv = xattn_bwd(proj, A_Q + 2 * LANE, kv, dmix, "xattn_a_bwd")
            dproj = (dqkv, dxq)
            G["sinks"] = dsk[:A_HEADS, 0]
            G["rel_bias"] = bias_grad(dbias)[:, :A_HEADS]
            w_in, gname = P["w_in_a"], "w_in_a"
        else:
            qkvn, chunked, inv, states, o = extra
            do, dz, dgo = dnpost_bwd(o, proj, prm, dmix)
            dqkvn, dseg, dprm = dnc_bwd(qkvn, proj, prm, inv, dns_bwd(chunked, states, do))
            draw, dconv = dnprep_bwd(proj, P["conv_qkv"], dqkvn)
            dxq, dkv = xattn_bwd(proj, 2304, kv, dmix, "xattn_b_bwd")
            dproj = (draw, dz, dxq, dseg)
            G["conv_qkv"] = dconv
            G["a_log"], G["dt_bias"], G["out_norm_g"] = dprm[0, 6:12], dprm[1, 6:12], dgo[0]
            w_in, gname = P["w_in_b"], "w_in_b"
        G[gname] = mm_tn(n1, dproj, name=f"d{gname}")
        dh, dg = mm_nt_norm(dproj, w_in, (hin, P["g_mix"][i], dh2), f"d_n1_{i}")
        G["g_mix"][i] = dg[0]
        dgm, dwm = memkv_bwd(mem, P["g_mem"][i], P["w_mem"][i], dkv, f"memkv_bwd{i}")
        G["g_mem"][i], G["w_mem"][i] = dgm[0], dwm
        ready(("mix", i), G, None)
    return loss, dh, G


def _grads_to_ref(G):
    return {
        "rel_bias": G["rel_bias"], "norm_mix_g": jnp.stack(G["g_mix"]), "norm_mem_g": jnp.stack(G["g_mem"]),
        "w_mem_kv": jnp.stack(G["w_mem"]),
        "w_out": jnp.stack([_unlay_out_a(G["w_out"][0]), G["w_out"][1]]),
        "w_in_a": _unlay_in_a(G["w_in_a"])[None], "sinks_a": G["sinks"][None],
        "w_in_b": _unlay_in_b(G["w_in_b"])[None], "conv_qkv_b": G["conv_qkv"][None],
        "a_log_b": G["a_log"][None], "dt_bias_b": G["dt_bias"][None], "out_norm_g_b": G["out_norm_g"][None],
        "norm_ffn_g": jnp.stack(G["g_ffn"]),
        "w_gate_up": jnp.stack([_unchip_cols(G["w_gu"][0]), _unchip_cols(G["w_gu"][1])]).astype(f32),
        "ffn_conv_w": jnp.stack(G["ffn_cw"]), "ffn_conv_b": jnp.stack(G["ffn_cb"]),
        "w_down": jnp.stack(G["w_down"]), "final_norm_g": G["g_fin"],
    }


ANY = pl.BlockSpec(memory_space=pl.ANY)


def _place():
    return lax.axis_index("x"), lax.axis_index("y"), lax.axis_index("c")


def allreduce_small(buf):
    R = buf.shape[0]

    def body(b_ref, o_ref, recv, ssem, rsem):
        x, y, c = _place()
        me = 4 * x + 2 * y + c

        def peer(k):
            return (1 - x if k & 4 else x, 1 - y if k & 2 else y, 1 - c if k & 1 else c)

        def remote(k, slot):
            return pltpu.make_async_remote_copy(
                src_ref=b_ref, dst_ref=recv.at[slot], send_sem=ssem.at[k - 1], recv_sem=rsem.at[k - 1],
                device_id=peer(k), device_id_type=MESH)

        sends = [remote(k, me) for k in range(1, 8)]
        for cp in sends:
            cp.start()
        recv[me] = b_ref[...]
        for k in range(1, 8):
            px, py, pc = peer(k)
            remote(k, 4 * px + 2 * py + pc).wait_recv()
        for cp in sends:
            cp.wait_send()
        total = recv[0]
        for j in range(1, 8):
            total = total + recv[j]
        o_ref[...] = total

    return pl.pallas_call(
        body, name="small_allreduce",
        in_specs=[pl.BlockSpec(memory_space=pltpu.VMEM)], out_specs=pl.BlockSpec(memory_space=pltpu.VMEM),
        out_shape=jax.ShapeDtypeStruct(buf.shape, f32),
        scratch_shapes=[pltpu.VMEM((8, R, LANE), f32), pltpu.SemaphoreType.DMA((7,)), pltpu.SemaphoreType.DMA((7,))],
    )(buf)


def sum_slots(own, recv, chip, core, name):
    _, R, C = recv.shape
    tr = _row_tile(R, 256)
    nt = R // tr

    def body(p_ref, a_ref, r_ref, o_ref):
        acc = jnp.zeros((tr, C), f32)
        for s in range(4):
            acc = acc + jnp.where(p_ref[0] == s, a_ref[s], r_ref[s]).astype(f32)
        o_ref[...] = acc

    slots = pl.BlockSpec((4, tr, C), lambda i, p_ref: (0, i, 0))
    return pl.pallas_call(
        body, name=name, out_shape=jax.ShapeDtypeStruct((2 * R, C), f32),
        grid_spec=pltpu.PrefetchScalarGridSpec(
            num_scalar_prefetch=1, grid=(nt,), in_specs=[slots, slots],
            out_specs=pl.BlockSpec((tr, C), lambda i, p_ref: (p_ref[1] * nt + i, 0))),
        compiler_params=_cp(("parallel",)),
    )(jnp.stack([chip, core]).astype(jnp.int32), own, recv)


def _half(ref, core, axis=0):
    half = ref.shape[axis] // 2
    idx = (slice(None),) * axis + (pl.ds(core * half, half),)
    return ref.at[idx]


IN_HBM = pl.BlockSpec(memory_space=pltpu.HBM)
IN_SEM = pl.BlockSpec(memory_space=pltpu.SEMAPHORE)
SIDE_EFFECT = pltpu.SideEffectType.DATAFLOW_SIDE_EFFECTING


def _gather_copy(buf, i, k, ssem, rsem, place, landing):
    x, y, c = place
    px, py = [(1 - x, y), (x, 1 - y), (1 - x, 1 - y)][k]
    me = 2 * x + y
    return pltpu.make_async_remote_copy(
        src_ref=buf.at[me], dst_ref=buf.at[me if landing == "theirs" else 2 * px + py],
        send_sem=ssem.at[3 * i + k], recv_sem=rsem.at[3 * i + k], device_id=(px, py, c), device_id_type=MESH)


def gather_start(groups):
    flat = [b for grp in groups for b in grp]
    n, ng = len(flat), len(groups)

    def body(*refs):
        bufs, sems = refs[:n], refs[n:n + 2 * ng]
        place = _place()
        j = 0
        for g, grp in enumerate(groups):
            for i in range(len(grp)):
                for k in range(3):
                    _gather_copy(bufs[j], i, k, sems[2 * g], sems[2 * g + 1], place, "theirs").start()
                j += 1

    sem_shapes = [pltpu.SemaphoreType.DMA((3 * len(grp),)) for grp in groups for _ in range(2)]
    out = pl.pallas_call(
        body, name="gather_start", in_specs=[IN_HBM] * n, out_specs=(*[IN_SEM] * (2 * ng), *[IN_HBM] * n),
        out_shape=(*sem_shapes, *[pltpu.HBM(b.shape, b.dtype) for b in flat]),
        input_output_aliases={i: 2 * ng + i for i in range(n)},
        compiler_params=pltpu.CompilerParams(has_side_effects=SIDE_EFFECT),
    )(*[pltpu.with_memory_space_constraint(b, pltpu.HBM) for b in flat])
    sems, bufs = out[:2 * ng], list(out[2 * ng:])
    flights, j = [], 0
    for g, grp in enumerate(groups):
        flights.append((bufs[j:j + len(grp)], sems[2 * g], sems[2 * g + 1]))
        j += len(grp)
    return flights


def gather_wait(flight, after, name):
    bufs, ssem, rsem = flight
    n = len(bufs)

    def body(*refs):
        place = _place()
        for i in range(n):
            for k in range(3):
                cp = _gather_copy(refs[i], i, k, refs[n], refs[n + 1], place, "mine")
                cp.wait_send()
                cp.wait_recv()

    return pl.pallas_call(
        body, name=name, in_specs=[IN_HBM] * n + [IN_SEM, IN_SEM, ANY], out_specs=[IN_HBM] * n,
        out_shape=[pltpu.HBM(b.shape, b.dtype) for b in bufs], input_output_aliases={i: i for i in range(n)},
        compiler_params=pltpu.CompilerParams(has_side_effects=SIDE_EFFECT),
    )(*bufs, ssem, rsem, after)


def _scatter_copy(src, land, j, k, ssem, rsem, place, landing):
    x, y, c = place
    px, py = [(1 - x, y), (x, 1 - y), (1 - x, 1 - y)][k]
    return pltpu.make_async_remote_copy(
        src_ref=src.at[2 * px + py], dst_ref=land.at[2 * x + y if landing == "theirs" else 2 * px + py],
        send_sem=ssem.at[3 * j + k], recv_sem=rsem.at[3 * j + k], device_id=(px, py, c), device_id_type=MESH)


def scatter_start(srcs, name):
    n = len(srcs)
    lands = [lax.empty(g.shape, g.dtype) for g in srcs]

    def body(*refs):
        place = _place()
        for j in range(n):
            for k in range(3):
                _scatter_copy(refs[j], refs[n + j], j, k, refs[2 * n], refs[2 * n + 1], place, "theirs").start()
        refs[-1][...] = jnp.zeros_like(refs[-1])

    sem = pltpu.SemaphoreType.DMA((3 * n,))
    hbm = [pltpu.with_memory_space_constraint(b, pltpu.HBM) for b in list(srcs) + lands]
    out = pl.pallas_call(
        body, name=name, in_specs=[IN_HBM] * (2 * n),
        out_specs=(IN_SEM, IN_SEM, *[IN_HBM] * (2 * n), pl.BlockSpec(memory_space=pltpu.VMEM)),
        out_shape=(sem, sem, *[pltpu.HBM(b.shape, b.dtype) for b in hbm], jax.ShapeDtypeStruct((8, LANE), f32)),
        input_output_aliases={i: 2 + i for i in range(2 * n)},
        compiler_params=pltpu.CompilerParams(has_side_effects=SIDE_EFFECT),
    )(*hbm)
    return (list(out[2:2 + n]), list(out[2 + n:2 + 2 * n]), out[0], out[1]), out[-1]


def scatter_wait(flight, after, name):
    srcs, lands, ssem, rsem = flight
    n = len(srcs)

    def body(*refs):
        place = _place()
        for j in range(n):
            for k in range(3):
                cp = _scatter_copy(refs[j], refs[n + j], j, k, refs[2 * n], refs[2 * n + 1], place, "mine")
                cp.wait_send()
                cp.wait_recv()

    out = pl.pallas_call(
        body, name=name, in_specs=[IN_HBM] * (2 * n) + [IN_SEM, IN_SEM, ANY], out_specs=[IN_HBM] * (2 * n),
        out_shape=[pltpu.HBM(b.shape, b.dtype) for b in list(srcs) + list(lands)],
        input_output_aliases={i: i for i in range(2 * n)},
        compiler_params=pltpu.CompilerParams(has_side_effects=SIDE_EFFECT),
    )(*srcs, *lands, ssem, rsem, after)
    return list(out[:n]), list(out[n:])


def pair_exchange(gbufs, name):
    n = len(gbufs)

    def body(*refs):
        ins, outs = refs[:n], refs[n:2 * n]
        ssem, rsem = refs[2 * n:]
        x, y, c = _place()
        cps = [pltpu.make_async_remote_copy(
            src_ref=_half(ins[j], 1 - c, axis=1), dst_ref=outs[j], send_sem=ssem.at[j], recv_sem=rsem.at[j],
            device_id=(x, y, 1 - c), device_id_type=MESH) for j in range(n)]
        for cp in cps:
            cp.start()
        for cp in cps:
            cp.wait()

    return pl.pallas_call(
        body, name=name, in_specs=[ANY] * n, out_specs=[ANY] * n,
        out_shape=[jax.ShapeDtypeStruct((4, g.shape[1] // 2, g.shape[2]), g.dtype) for g in gbufs],
        scratch_shapes=[pltpu.SemaphoreType.DMA((n,)), pltpu.SemaphoreType.DMA((n,))],
    )(*gbufs)


def _row_tile(rows, cap=512):
    return max(t for t in range(16, min(rows, cap) + 1, 16) if rows % t == 0)


def pair_sum(mine, theirs, core, name):
    _, R, C = mine.shape
    half = R // 2
    tr = _row_tile(half)
    nt = half // tr

    def body(c_ref, a_ref, b_ref, o_ref):
        o_ref[...] = (a_ref[...].astype(f32) + b_ref[...].astype(f32)).astype(bf16)

    return pl.pallas_call(
        body, name=name, out_shape=jax.ShapeDtypeStruct(theirs.shape, bf16),
        grid_spec=pltpu.PrefetchScalarGridSpec(
            num_scalar_prefetch=1, grid=(4, nt),
            in_specs=[pl.BlockSpec((None, tr, C), lambda s, i, c_ref: (s, c_ref[0] * nt + i, 0)),
                      pl.BlockSpec((None, tr, C), lambda s, i, c_ref: (s, i, 0))],
            out_specs=pl.BlockSpec((None, tr, C), lambda s, i, c_ref: (s, i, 0))),
        compiler_params=_cp(("parallel", "parallel")),
    )(jnp.reshape(core, (1,)).astype(jnp.int32), mine, theirs)


def final_exchange(fins):
    n = len(fins)

    def body(*refs):
        outs = refs[n:2 * n]
        ssem, rsem = refs[2 * n:]
        x, y, c = _place()
        cps = [pltpu.make_async_remote_copy(
            src_ref=_half(outs[j], c), dst_ref=_half(outs[j], c), send_sem=ssem.at[j], recv_sem=rsem.at[j],
            device_id=(x, y, 1 - c), device_id_type=MESH) for j in range(n)]
        for cp in cps:
            cp.start()
        for cp in cps:
            cp.wait()

    return pl.pallas_call(
        body, name="final_exchange", in_specs=[ANY] * n, out_specs=[ANY] * n,
        out_shape=[jax.ShapeDtypeStruct(f.shape, f.dtype) for f in fins],
        input_output_aliases={j: j for j in range(n)},
        scratch_shapes=[pltpu.SemaphoreType.DMA((n,)), pltpu.SemaphoreType.DMA((n,))],
    )(*fins)


def adamw_big(w, m, v, gs, row0, name):
    L, R, C = w.shape
    tr = _row_tile(math.gcd(R, row0) if row0 else R, max(16, 262144 // C // 16 * 16))
    b0 = row0 // tr

    def body(*refs):
        w_ref, m_ref, v_ref = refs[:3]
        g_refs = refs[3:3 + L]
        g_ref, d_ref, nm_ref, nv_ref = refs[3 + L:]
        g = g_refs[0][...]
        for l in range(1, L):
            g = jnp.where(pl.program_id(0) == l, g_refs[l][...], g)
        d, nm, nv = _adamw_math(w_ref[...], g, m_ref[...], v_ref[...])
        g_ref[...] = g
        d_ref[...] = d
        nm_ref[...] = nm
        nv_ref[...] = nv

    own = pl.BlockSpec((None, tr, C), lambda l, i: (l, i, 0))
    off = pl.BlockSpec((tr, C), lambda l, i: (b0 + i, 0))
    return pl.pallas_call(
        body, name=name, grid=(L, R // tr), in_specs=[own, own, own] + [off] * L, out_specs=[own] * 4,
        out_shape=[jax.ShapeDtypeStruct((L, R, C), f32)] * 4, compiler_params=_cp(("parallel", "parallel")),
    )(w, m, v, *gs)


def _adamw_math(w, g, m, v):
    m = B1 * m + (1.0 - B1) * g
    v = B2 * v + (1.0 - B2) * (g * g)
    m_hat = m / (1.0 - B1 ** STEP)
    v_hat = v / (1.0 - B2 ** STEP)
    delta = -LR * (m_hat / (jnp.sqrt(v_hat) + AEPS) + WD * w)
    return delta, m, v


def adamw_small(w, m, v, g):
    def body(w_ref, m_ref, v_ref, g_ref, d_ref, nm_ref, nv_ref):
        d, nm, nv = _adamw_math(w_ref[...], g_ref[...], m_ref[...], v_ref[...])
        d_ref[...] = d
        nm_ref[...] = nm
        nv_ref[...] = nv

    return pl.pallas_call(body, name="adamw_small", out_shape=[jax.ShapeDtypeStruct(w.shape, f32)] * 3)(w, m, v, g)


CONV =(("conv_qkv_b", 2), ("ffn_conv_w", 2))
SMALL = ("rel_bias", "norm_mix_g", "norm_mem_g", "sinks_a", "a_log_b", "dt_bias_b", "out_norm_g_b", "norm_ffn_g",
         "ffn_conv_b", "final_norm_g")
WEIGHTS = ("rel_bias", "norm_mix_g", "norm_mem_g", "w_mem_kv", "w_out", "w_in_a", "sinks_a", "w_in_b", "conv_qkv_b",
           "a_log_b", "dt_bias_b", "out_norm_g_b", "norm_ffn_g", "w_gate_up", "ffn_conv_w", "ffn_conv_b", "w_down",
           "final_norm_g")
ARGS = ("x", "mem") + WEIGHTS + ("loss_target",) + tuple("m_" + n for n in WEIGHTS) + tuple("v_" + n for n in WEIGHTS)


def _rows(a, width):
    flat = a.reshape(-1)
    pad = (-flat.shape[0]) % (8 * width)
    if pad:
        flat = jnp.concatenate([flat, jnp.zeros((pad,), a.dtype)])
    return flat.reshape(-1, width)


def _nrows(shape, width):
    return _pad_to(-(-math.prod(shape) // width), 8)


def _pack(arrs, width, total_rows, dtype):
    parts = [_rows(a.astype(dtype), width) for a in arrs]
    used = sum(p.shape[0] for p in parts)
    if total_rows > used:
        parts.append(jnp.zeros((total_rows - used, width), dtype))
    return jnp.concatenate(parts, axis=0)


def _unpack(buf, shapes, width):
    out, r = [], 0
    for s in shapes:
        n = _nrows(s, width)
        out.append(buf[r:r + n].reshape(-1)[:math.prod(s)].reshape(s))
        r += n
    return out


def _pad_to(n, mult):
    return -(-n // mult) * mult


def kernel(x, mem, rel_bias, norm_mix_g, norm_mem_g, w_mem_kv, w_out, w_in_a, sinks_a, w_in_b, conv_qkv_b, a_log_b, dt_bias_b, out_norm_g_b, norm_ffn_g, w_gate_up, ffn_conv_w, ffn_conv_b, w_down, final_norm_g, loss_target, m_rel_bias, m_norm_mix_g, m_norm_mem_g, m_w_mem_kv, m_w_out, m_w_in_a, m_sinks_a, m_w_in_b, m_conv_qkv_b, m_a_log_b, m_dt_bias_b, m_out_norm_g_b, m_norm_ffn_g, m_w_gate_up, m_ffn_conv_w, m_ffn_conv_b, m_w_down, m_final_norm_g, v_rel_bias, v_norm_mix_g, v_norm_mem_g, v_w_mem_kv, v_w_out, v_w_in_a, v_sinks_a, v_w_in_b, v_conv_qkv_b, v_a_log_b, v_dt_bias_b, v_out_norm_g_b, v_norm_ffn_g, v_w_gate_up, v_ffn_conv_w, v_ffn_conv_b, v_w_down, v_final_norm_g):
    A = dict(zip(ARGS, (x, mem, rel_bias, norm_mix_g, norm_mem_g, w_mem_kv, w_out, w_in_a, sinks_a, w_in_b, conv_qkv_b, a_log_b, dt_bias_b, out_norm_g_b, norm_ffn_g, w_gate_up, ffn_conv_w, ffn_conv_b, w_down, final_norm_g, loss_target, m_rel_bias, m_norm_mix_g, m_norm_mem_g, m_w_mem_kv, m_w_out, m_w_in_a, m_sinks_a, m_w_in_b, m_conv_qkv_b, m_a_log_b, m_dt_bias_b, m_out_norm_g_b, m_norm_ffn_g, m_w_gate_up, m_ffn_conv_w, m_ffn_conv_b, m_w_down, m_final_norm_g, v_rel_bias, v_norm_mix_g, v_norm_mem_g, v_w_mem_kv, v_w_out, v_w_in_a, v_sinks_a, v_w_in_b, v_conv_qkv_b, v_a_log_b, v_dt_bias_b, v_out_norm_g_b, v_norm_ffn_g, v_w_gate_up, v_ffn_conv_w, v_ffn_conv_b, v_w_down, v_final_norm_g)))
    chip = 2 * lax.axis_index("x") + lax.axis_index("y")
    core = lax.axis_index("c")

    def own_slot(shard):
        return lax.dynamic_update_index_in_dim(lax.empty((4,) + shard.shape, shard.dtype), shard, chip, 0)

    def bslot(w):
        return own_slot(w.astype(bf16))

    groups = {
        ("w_in", 0): [bslot(w_in_a[0])],
        ("w_mem", 0): [bslot(w_mem_kv[0]), bslot(w_mem_kv[1]), own_slot(conv_qkv_b[0]),
                       own_slot(ffn_conv_w.reshape(6, -1))],
        ("w_out", 0): [bslot(w_out[0])], ("w_gu", 0): [bslot(w_gate_up[0])], ("w_down", 0): [bslot(w_down[0])],
        ("w_in", 1): [bslot(w_in_b[0])],
        ("w_out", 1): [bslot(w_out[1])], ("w_gu", 1): [bslot(w_gate_up[1])], ("w_down", 1): [bslot(w_down[1])],
    }
    flights = dict(zip(groups, gather_start(list(groups.values()))))
    P = {"rel_bias": rel_bias, "sinks": sinks_a[0], "a_log": a_log_b[0], "dt_bias": dt_bias_b[0],
         "out_norm_g": out_norm_g_b[0], "g_mix": norm_mix_g, "g_mem": norm_mem_g, "g_ffn": norm_ffn_g,
         "g_fin": final_norm_g, "ffn_cb": [ffn_conv_b[0], ffn_conv_b[1]], "w_mem": [None, None], "w_out": [None, None],
         "w_gu": [None, None], "w_down": [None, None], "ffn_cw": [None, None]}

    def rows4(g):
        return g.reshape(4 * g.shape[1], g.shape[2])

    def arrive(key, after):
        if key not in flights:
            return
        got = gather_wait(flights.pop(key), after, "gather_wait_%s%d" % key)
        name, i = key
        if name == "w_in":
            P["w_in_a" if i == 0 else "w_in_b"] = (_lay_in_a if i == 0 else _lay_in_b)(_unchip_cols(got[0]))
        elif name == "w_mem":
            P["w_mem"] = [rows4(got[0]), rows4(got[1])]
            P["conv_qkv"] = _unchip_cols(got[2])
            cw = _unchip_cols(got[3]).reshape(2, 3, D_FF)
            P["ffn_cw"] = [cw[0], cw[1]]
        elif name == "w_out":
            P["w_out"][i] = _lay_out_a(rows4(got[0])) if i == 0 else rows4(got[0])
        elif name == "w_gu":
            P["w_gu"][i] = got[0]
        else:
            P["w_down"][i] = rows4(got[0])

    def chip_rows(g):
        return g.reshape(4, g.shape[0] // 4, g.shape[-1])

    sent, started = {}, []

    def ready(key, G, dep):
        kind, i = key
        tag = "%s%d" % key
        if kind == "ffn":
            names, partial = ("gu", "down"), [G["w_gu"][i], chip_rows(G["w_down"][i]).astype(bf16)]
        else:
            g_out = _unlay_out_a(G["w_out"][0]) if i == 0 else G["w_out"][1]
            g_in = _unlay_in_a(G["w_in_a"]) if i == 0 else _unlay_in_b(G["w_in_b"])
            names = ("out", "in", "mem")
            partial = [chip_rows(g_out).astype(bf16), _chip_cols(g_in).astype(bf16), chip_rows(G["w_mem"][i]).astype(bf16)]
        theirs = pair_exchange(partial, "pair_exchange_" + tag)
        pair = [pair_sum(p, t, core, "pair_sum_%s%d" % (nm, i)) for p, t, nm in zip(partial, theirs, names)]
        flight, token = scatter_start(pair, "scatter_start_" + tag)
        sent[key] = (names, flight, token)
        started.append(token[0, 0])
        if dep is not None:
            while started:
                dep = dep + started.pop()
        return dep

    P["arrive"], P["ready"] = arrive, ready

    loss, dx, G = _local_step(x[0], mem[0], loss_target[0], P)
    gfull = _grads_to_ref(G)

    fin, after = {}, sent["mix", 0][2]
    for key in (("ffn", 1), ("mix", 1), ("ffn", 0), ("mix", 0)):
        names, flight, _ = sent[key]
        pair, arrived = scatter_wait(flight, after, "scatter_wait_%s%d" % key)
        for nm, p, r in zip(names, pair, arrived):
            after = fin[nm, key[1]] = sum_slots(p, r, chip, core, "sum_slots_%s%d" % (nm, key[1]))
    order = list(fin)
    done = dict(zip(order, final_exchange([fin[k] for k in order])))

    sm_shapes = [A[n].shape for n in SMALL] + [gfull[n].shape for n, _ in CONV] + [(LANE,)]
    sm_rows = _pad_to(sum(_nrows(s, LANE) for s in sm_shapes), 8)
    sbuf = _pack([gfull[n] for n in SMALL] + [gfull[n] for n, _ in CONV] + [loss[0]], LANE, sm_rows, f32)
    tot = _unpack(allreduce_small(sbuf), sm_shapes, LANE)
    gsmall = dict(zip(SMALL, tot[:len(SMALL)]))
    for (n, axis), t in zip(CONV, tot[len(SMALL):len(SMALL) + len(CONV)]):
        sh = A[n].shape[axis]
        gsmall[n] = lax.dynamic_slice_in_dim(t, chip * sh, sh, axis)
    loss_out = tot[-1][0]

    out = {}
    plan = (("w_gate_up", [done["gu", 0], done["gu", 1]]), ("w_down", [done["down", 0], done["down", 1]]),
            ("w_out", [done["out", 0], done["out", 1]]), ("w_mem_kv", [done["mem", 0], done["mem", 1]]),
            ("w_in_a", [done["in", 0]]), ("w_in_b", [done["in", 1]]))
    for n, gs in plan:
        shape3 = (len(gs),) + gs[0].shape
        res = adamw_big(A[n].reshape(shape3), A["m_" + n].reshape(shape3), A["v_" + n].reshape(shape3), gs, 0,
                        "adamw_" + n)
        for key, r in zip(("grad_", "delta_", "new_m_", "new_v_"), res):
            out[key + n] = r.reshape(A[n].shape)
    names = SMALL + tuple(n for n, _ in CONV)
    shapes = [A[n].shape for n in names]
    rows = _pad_to(sum(_nrows(s, LANE) for s in shapes), 8)
    packs = [_pack([src[n] for n in names], LANE, rows, f32)
             for src in ({n: A[n] for n in names}, {n: A["m_" + n] for n in names}, {n: A["v_" + n] for n in names}, gsmall)]
    res = adamw_small(*packs)
    for key, r in zip(("delta_", "new_m_", "new_v_"), res):
        for n, a in zip(names, _unpack(r, shapes, LANE)):
            out[key + n] = a
    for n in names:
        out["grad_" + n] = gsmall[n]
    return (loss_out, dx[None], *[out["grad_" + n] for n in WEIGHTS], *[out["delta_" + n] for n in WEIGHTS],
            *[out["new_m_" + n] for n in WEIGHTS], *[out["new_v_" + n] for n in WEIGHTS])
```

```python
import functools
import math

import numpy as np
import jax
import jax.numpy as jnp
from jax import lax
from jax.experimental import pallas as pl
from jax.experimental.pallas import tpu as pltpu

f32 = jnp.float32
bf16 = jnp.bfloat16
HI = lax.Precision.HIGHEST
MESH = pl.DeviceIdType.MESH

D = 1024
MEM_LEN = 256
EPS = 1e-6
A_HEADS, A_KV, A_DH = 12, 2, 64
A_Q = 768
BLK = 128
N_BUCKETS, MAX_DIST = 32, 128
B_QK, B_V, B_DH = 384, 768, 128
B_QKV = 1536
CHUNK = 64
X_Q = 256
D_FF = 2816
LANE = 128
VMEM_LIMIT = 56 * 1024 * 1024
MM_ROWS = 1024

LR, B1, B2, AEPS, WD, STEP = 0.001, 0.9, 0.999, 1e-08, 0.01, 10


def _cp(sem=None):
    return pltpu.CompilerParams(dimension_semantics=sem, vmem_limit_bytes=VMEM_LIMIT)


def _dg(a, b, ca, cb, prec=None):
    return lax.dot_general(a, b, (((ca,), (cb,)), ((), ())), precision=prec, preferred_element_type=f32)


@jax.custom_vjp
def bdot(a, b):
    return _dg(a.astype(bf16), b.astype(bf16), 1, 0)


def _bdot_f(a, b):
    return bdot(a, b), (a, b)


def _bdot_b(res, g):
    a, b = res
    gb = g.astype(bf16)
    return _dg(gb, b.astype(bf16), 1, 1), _dg(a.astype(bf16), gb, 0, 0)


bdot.defvjp(_bdot_f, _bdot_b)


@jax.custom_vjp
def bdot_nt(a, b):
    return _dg(a.astype(bf16), b.astype(bf16), 1, 1)


def _bdot_nt_f(a, b):
    return bdot_nt(a, b), (a, b)


def _bdot_nt_b(res, g):
    a, b = res
    gb = g.astype(bf16)
    return _dg(gb, b.astype(bf16), 1, 0), _dg(gb, a.astype(bf16), 0, 0)


bdot_nt.defvjp(_bdot_nt_f, _bdot_nt_b)


def _shift_rows(x, s, down):
    n = x.shape[0]
    row = lax.broadcasted_iota(jnp.int32, x.shape, 0)
    if down:
        return jnp.where(row >= s, pltpu.roll(x, s, 0), 0.0)
    return jnp.where(row < n - s, pltpu.roll(x, n - s, 0), 0.0)


@functools.partial(jax.custom_vjp, nondiff_argnums=(1,))
def shift_down(x, s):
    return _shift_rows(x, s, True)


def _sd_f(x, s):
    return _shift_rows(x, s, True), None


def _sd_b(s, _, g):
    return (_shift_rows(g, s, False),)


shift_down.defvjp(_sd_f, _sd_b)


def _sigmoid(x):
    return 1.0 / (1.0 + jnp.exp(-x))


def _silu(x):
    return x * _sigmoid(x)


def _rms(x, g):
    return x * lax.rsqrt(jnp.mean(x * x, axis=-1, keepdims=True) + EPS) * g


def _tile(n, cap):
    u = n // LANE
    best = 1
    for d in range(1, u + 1):
        if u % d == 0 and d * LANE <= cap:
            best = d
    return best * LANE


def mm_nn(a, w, res=None, out_dtype=f32, name="mm_nn"):
    M, K = a.shape
    N = w.shape[1]
    tm, tn = min(MM_ROWS, M), _tile(N, 1024)

    def body(*refs):
        if res is None:
            a_ref, w_ref, o_ref = refs
            o_ref[...] = _dg(a_ref[...].astype(bf16), w_ref[...], 1, 0).astype(out_dtype)
        else:
            a_ref, w_ref, r_ref, o_ref = refs
            o_ref[...] = (r_ref[...] + _dg(a_ref[...].astype(bf16), w_ref[...], 1, 0)).astype(out_dtype)

    in_specs = [pl.BlockSpec((tm, K), lambda n, m: (m, 0)), pl.BlockSpec((K, tn), lambda n, m: (0, n))]
    args = [a, w]
    if res is not None:
        in_specs.append(pl.BlockSpec((tm, tn), lambda n, m: (m, n)))
        args.append(res)
    return pl.pallas_call(
        body, name=name, grid=(N // tn, M // tm), in_specs=in_specs,
        out_specs=pl.BlockSpec((tm, tn), lambda n, m: (m, n)),
        out_shape=jax.ShapeDtypeStruct((M, N), out_dtype),
        compiler_params=_cp(("parallel", "parallel")),
    )(*args)


def mm_res_norm(a, w, res, g, name):
    pieces = a if isinstance(a, tuple) else (a,)
    na = len(pieces)
    M, K = pieces[0].shape[0], sum(p.shape[1] for p in pieces)
    tm = min(MM_ROWS, M)

    def body(*refs):
        w_ref, r_ref, g_ref, o_ref, n_ref = refs[na:]
        h = r_ref[...] + _dg(_cols(refs[:na]).astype(bf16), w_ref[...], 1, 0)
        o_ref[...] = h
        n_ref[...] = _rms(h, g_ref[...]).astype(bf16)

    tok = pl.BlockSpec((tm, D), lambda m: (m, 0))
    return pl.pallas_call(
        body, name=name, grid=(M // tm,),
        in_specs=[pl.BlockSpec((tm, p.shape[1]), lambda m: (m, 0)) for p in pieces]
        + [pl.BlockSpec((K, D), lambda m: (0, 0)), tok, pl.BlockSpec((1, D), lambda m: (0, 0))],
        out_specs=[tok, tok],
        out_shape=[jax.ShapeDtypeStruct((M, D), f32), jax.ShapeDtypeStruct((M, D), bf16)],
        compiler_params=_cp(("parallel",)),
    )(*pieces, w, res, g.reshape(1, D))


def mm_nt(dy, w, out_dtype=f32, name="mm_nt"):
    M, N = dy.shape
    K = w.shape[0]
    tm, tn = min(MM_ROWS, M), _tile(N, 1024)
    assert out_dtype == f32 or tn == N

    def body(dy_ref, w_ref, o_ref):
        part = _dg(dy_ref[...].astype(bf16), w_ref[...], 1, 1)
        if tn == N:
            o_ref[...] = part.astype(out_dtype)
        else:
            @pl.when(pl.program_id(1) == 0)
            def _():
                o_ref[...] = jnp.zeros_like(o_ref)
            o_ref[...] += part

    return pl.pallas_call(
        body, name=name, grid=(M // tm, N // tn),
        in_specs=[pl.BlockSpec((tm, tn), lambda m, n: (m, n)), pl.BlockSpec((K, tn), lambda m, n: (0, n))],
        out_specs=pl.BlockSpec((tm, K), lambda m, n: (m, 0)),
        out_shape=jax.ShapeDtypeStruct((M, K), out_dtype),
        compiler_params=_cp(("parallel", "arbitrary")),
    )(dy, w)


NORM_ROWS = 1024


def _acc_then_norm_bwd(part, steps, h_ref, g_ref, r_ref, o_ref, dg_ref):
    k = pl.program_id(1)

    @pl.when((pl.program_id(0) == 0) & (k == 0))
    def _():
        dg_ref[...] = jnp.zeros_like(dg_ref)

    @pl.when(k == 0)
    def _():
        o_ref[...] = part

    @pl.when(k > 0)
    def _():
        o_ref[...] += part

    @pl.when(k == steps - 1)
    def _():
        _, vjp = jax.vjp(_rms, h_ref[...], g_ref[...])
        dh, dg = vjp(o_ref[...])
        o_ref[...] = r_ref[...] + dh
        dg_ref[...] += dg


def mm_nt_norm(dy, w, norm, name):
    pieces = dy if isinstance(dy, tuple) else (dy,)
    nd = len(pieces)
    M, N = pieces[0].shape[0], sum(p.shape[1] for p in pieces)
    tm, tn = (min(NORM_ROWS, M), _tile(N, 1024)) if nd == 1 else (min(512, M), N)

    def body(*refs):
        w_ref, h_ref, g_ref, r_ref, o_ref, dg_ref = refs[nd:]
        _acc_then_norm_bwd(_dg(_cols(refs[:nd]).astype(bf16), w_ref[...], 1, 1), N // tn, h_ref, g_ref, r_ref, o_ref,
                           dg_ref)

    tok = pl.BlockSpec((tm, D), lambda m, n: (m, 0))
    vec = pl.BlockSpec((1, D), lambda m, n: (0, 0))
    return pl.pallas_call(
        body, name=name, grid=(M // tm, N // tn),
        in_specs=[pl.BlockSpec((tm, tn if nd == 1 else p.shape[1]), lambda m, n: (m, n)) for p in pieces]
        + [pl.BlockSpec((D, tn), lambda m, n: (0, n)), tok, vec, tok],
        out_specs=[tok, vec],
        out_shape=[jax.ShapeDtypeStruct((M, D), f32), jax.ShapeDtypeStruct((1, D), f32)],
        compiler_params=_cp(("arbitrary", "arbitrary")),
    )(*pieces, w, norm[0], norm[1].reshape(1, D), norm[2])


def _cols(refs):
    return refs[0][...] if len(refs) == 1 else jnp.concatenate([r[...] for r in refs], axis=1)


def mm_tn(a, dy, name="mm_tn"):
    pieces = a if isinstance(a, tuple) else (a,)
    dpieces = dy if isinstance(dy, tuple) else (dy,)
    na, nd = len(pieces), len(dpieces)
    M, K = pieces[0].shape[0], sum(p.shape[1] for p in pieces)
    N = sum(p.shape[1] for p in dpieces)
    tm = min(MM_ROWS, M)
    tk = _tile(K, 1408) if na == 1 else K
    tn = _tile(N, 1024) if nd == 1 else N

    def body(*refs):
        o_ref = refs[-1]

        @pl.when(pl.program_id(2) == 0)
        def _():
            o_ref[...] = jnp.zeros_like(o_ref)
        o_ref[...] += _dg(_cols(refs[:na]).astype(bf16), _cols(refs[na:na + nd]).astype(bf16), 0, 0)

    a_specs = [pl.BlockSpec((tm, tk if na == 1 else p.shape[1]), lambda k, n, m: (m, k)) for p in pieces]
    d_specs = [pl.BlockSpec((tm, tn if nd == 1 else p.shape[1]), lambda k, n, m: (m, n)) for p in dpieces]
    return pl.pallas_call(
        body, name=name, grid=(K // tk, N // tn, M // tm), in_specs=a_specs + d_specs,
        out_specs=pl.BlockSpec((tk, tn), lambda k, n, m: (k, n)),
        out_shape=jax.ShapeDtypeStruct((K, N), f32),
        compiler_params=_cp(("parallel", "parallel", "arbitrary")),
    )(*pieces, *dpieces)


def rms_fwd(h, g, name):
    S = h.shape[0]
    t = min(512, S)

    def body(h_ref, g_ref, o_ref):
        o_ref[...] = _rms(h_ref[...], g_ref[...]).astype(bf16)

    return pl.pallas_call(
        body, name=name, grid=(S // t,),
        in_specs=[pl.BlockSpec((t, D), lambda i: (i, 0)), pl.BlockSpec((1, D), lambda i: (0, 0))],
        out_specs=pl.BlockSpec((t, D), lambda i: (i, 0)),
        out_shape=jax.ShapeDtypeStruct((S, D), bf16),
        compiler_params=_cp(("parallel",)),
    )(h, g.reshape(1, D))


def loss_head(h, g, target):
    S = h.shape[0]
    t = min(512, S)

    def f(hh, gg, tt):
        err = _rms(hh, gg) - tt
        return 0.5 * jnp.sum(jnp.mean(err * err, axis=-1, keepdims=True), axis=0, keepdims=True)

    def body(h_ref, g_ref, t_ref, loss_ref, dh_ref, dg_ref):
        @pl.when(pl.program_id(0) == 0)
        def _():
            dg_ref[...] = jnp.zeros_like(dg_ref)
            loss_ref[...] = jnp.zeros_like(loss_ref)
        val, vjp = jax.vjp(lambda a, b: f(a, b, t_ref[...]), h_ref[...], g_ref[...])
        dh, dg = vjp(jnp.ones((1, 1), f32))
        dh_ref[...] = dh
        dg_ref[...] += dg
        loss_ref[...] += jnp.broadcast_to(val, loss_ref.shape)

    tok = pl.BlockSpec((t, D), lambda i: (i, 0))
    vec = pl.BlockSpec((1, D), lambda i: (0, 0))
    return pl.pallas_call(
        body, name="loss_head", grid=(S // t,), in_specs=[tok, vec, tok],
        out_specs=[pl.BlockSpec((1, LANE), lambda i: (0, 0)), tok, vec],
        out_shape=[jax.ShapeDtypeStruct((1, LANE), f32), jax.ShapeDtypeStruct((S, D), f32),
                   jax.ShapeDtypeStruct((1, D), f32)],
        compiler_params=_cp(("arbitrary",)),
    )(h, g.reshape(1, D), target)


def memkv_fwd(mem, g, w, name):
    def body(m_ref, g_ref, w_ref, o_ref):
        o_ref[...] = _dg(_rms(m_ref[...], g_ref[...]).astype(bf16), w_ref[...], 1, 0)

    return pl.pallas_call(
        body, name=name, out_shape=jax.ShapeDtypeStruct((MEM_LEN, 2 * X_Q), f32), compiler_params=_cp(),
    )(mem, g.reshape(1, D), w)


def memkv_bwd(mem, g, w, dkv, name):
    def body(m_ref, g_ref, w_ref, d_ref, dg_ref, dw_ref):
        n, vjp = jax.vjp(lambda gg: _rms(m_ref[...], gg), g_ref[...])
        db = d_ref[...].astype(bf16)
        dw_ref[...] = _dg(n.astype(bf16), db, 0, 0)
        dg_ref[...] = vjp(_dg(db, w_ref[...], 1, 1))[0]

    return pl.pallas_call(
        body, name=name,
        out_shape=[jax.ShapeDtypeStruct((1, D), f32), jax.ShapeDtypeStruct((D, 2 * X_Q), f32)],
        compiler_params=_cp(),
    )(mem, g.reshape(1, D), w, dkv)


def _xattn_f(xq, mk, mv):
    lane = lax.broadcasted_iota(jnp.int32, (1, X_Q), 1)
    out = jnp.zeros(xq.shape, f32)
    for hd in range(4):
        msk = (lane // 64 == hd).astype(f32)
        s = bdot_nt(xq * msk, mk) * (64 ** -0.5)
        m = lax.stop_gradient(jnp.max(s, axis=-1, keepdims=True))
        p = jnp.exp(s - m)
        p = p / jnp.sum(p, axis=-1, keepdims=True)
        out = out + bdot(p, mv * msk)
    return out


def xattn_fwd(proj, col, kv, name):
    S = proj.shape[0]
    t = min(512, S)
    cb = col // X_Q

    def body(q_ref, k_ref, v_ref, o_ref):
        o_ref[...] = _xattn_f(q_ref[...].astype(f32), k_ref[...], v_ref[...]).astype(bf16)

    return pl.pallas_call(
        body, name=name, grid=(S // t,),
        in_specs=[pl.BlockSpec((t, X_Q), lambda i: (i, cb)), pl.BlockSpec((MEM_LEN, X_Q), lambda i: (0, 0)),
                  pl.BlockSpec((MEM_LEN, X_Q), lambda i: (0, 1))],
        out_specs=pl.BlockSpec((t, X_Q), lambda i: (i, 0)),
        out_shape=jax.ShapeDtypeStruct((S, X_Q), bf16),
        compiler_params=_cp(("parallel",)),
    )(proj, kv, kv)


def xattn_bwd(proj, col, kv, dmix, name):
    S = proj.shape[0]
    t = min(512, S)
    cb = col // X_Q

    def body(q_ref, k_ref, v_ref, do_ref, dq_ref, dk_ref, dv_ref):
        @pl.when(pl.program_id(0) == 0)
        def _():
            dk_ref[...] = jnp.zeros_like(dk_ref)
            dv_ref[...] = jnp.zeros_like(dv_ref)
        _, vjp = jax.vjp(_xattn_f, q_ref[...].astype(f32), k_ref[...], v_ref[...])
        dq, dk, dv = vjp(do_ref[...])
        dq_ref[...] = dq.astype(bf16)
        dk_ref[...] += dk
        dv_ref[...] += dv

    kvb = pl.BlockSpec((MEM_LEN, X_Q), lambda i: (0, 0))
    dq, dk, dv = pl.pallas_call(
        body, name=name, grid=(S // t,),
        in_specs=[pl.BlockSpec((t, X_Q), lambda i: (i, cb)), kvb,
                  pl.BlockSpec((MEM_LEN, X_Q), lambda i: (0, 1)), pl.BlockSpec((t, X_Q), lambda i: (i, 3))],
        out_specs=[pl.BlockSpec((t, X_Q), lambda i: (i, 0)), kvb, kvb],
        out_shape=[jax.ShapeDtypeStruct((S, X_Q), bf16), jax.ShapeDtypeStruct((MEM_LEN, X_Q), f32),
                   jax.ShapeDtypeStruct((MEM_LEN, X_Q), f32)],
        compiler_params=_cp(("arbitrary",)),
    )(proj, kv, kv, dmix)
    return dq, jnp.concatenate([dk, dv], axis=1)


def _bucket_map():
    qi = np.arange(BLK)[:, None]
    kj = np.arange(2 * BLK)[None, :]
    n = np.maximum(BLK + qi - kj, 0)
    max_exact = N_BUCKETS // 2
    nf = np.maximum(n, 1).astype(np.float64)
    large = max_exact + (np.log(nf / max_exact) / math.log(MAX_DIST / max_exact)
                         * (N_BUCKETS - max_exact)).astype(np.int32)
    large = np.minimum(large, N_BUCKETS - 1)
    return np.where(n < max_exact, n, large).astype(np.int32)


def bias_build(rel_bias):
    def body(rb_ref, bk_ref, o_ref):
        bk = bk_ref[...]
        for h in range(A_HEADS):
            acc = jnp.zeros((BLK, 2 * BLK), f32)
            for b in range(N_BUCKETS):
                acc = jnp.where(bk == b, rb_ref[b, h], acc)
            o_ref[h] = acc

    return pl.pallas_call(
        body, name="bias_build",
        in_specs=[pl.BlockSpec(memory_space=pltpu.SMEM), pl.BlockSpec(memory_space=pltpu.VMEM)],
        out_specs=pl.BlockSpec(memory_space=pltpu.VMEM),
        out_shape=jax.ShapeDtypeStruct((A_HEADS, BLK, 2 * BLK), f32), compiler_params=_cp(),
    )(rel_bias, jnp.asarray(_bucket_map()))


def bias_grad(dbias):
    def body(d_ref, bk_ref, o_ref):
        bk = bk_ref[...]
        row = lax.broadcasted_iota(jnp.int32, (N_BUCKETS, LANE), 0)
        lane = lax.broadcasted_iota(jnp.int32, (N_BUCKETS, LANE), 1)
        acc = jnp.zeros((N_BUCKETS, LANE), f32)
        for h in range(A_HEADS):
            d = d_ref[h]
            for b in range(N_BUCKETS):
                s = jnp.sum(jnp.where(bk == b, d, 0.0), keepdims=True)
                acc = acc + jnp.where((row == b) & (lane == h), s, 0.0)
        o_ref[...] = acc

    return pl.pallas_call(
        body, name="bias_grad", out_shape=jax.ShapeDtypeStruct((N_BUCKETS, LANE), f32), compiler_params=_cp(),
    )(dbias, jnp.asarray(_bucket_map()))


def _swa_f(qb, kp, kc, vp, vc, bias, sk, first):
    kband = jnp.concatenate([kp, kc], axis=0)
    vband = jnp.concatenate([vp, vc], axis=0)
    qi = lax.broadcasted_iota(jnp.int32, (BLK, 2 * BLK), 0)
    kj = lax.broadcasted_iota(jnp.int32, (BLK, 2 * BLK), 1)
    rel = kj - qi
    ok = (rel >= 1) & (rel <= BLK) & ((kj >= BLK) | jnp.logical_not(first))
    lane = lax.broadcasted_iota(jnp.int32, (1, LANE), 1)
    lane_b = lax.broadcasted_iota(jnp.int32, (BLK, LANE), 1)
    outs = []
    for p in range(A_HEADS // 2):
        qp = qb[:, LANE * p:LANE * (p + 1)]
        acc = jnp.zeros((BLK, LANE), f32)
        for g in range(2):
            h = g * (A_HEADS // 2) + p
            msk = (lane // A_DH == g).astype(f32)
            s = bdot_nt(qp * msk, kband) * (A_DH ** -0.5) + bias[h]
            s = jnp.where(ok, s, -1e30)
            skb = jnp.broadcast_to(sk[h:h + 1, :], (BLK, LANE))
            sink = jnp.sum(jnp.where(lane_b == 0, skb, 0.0), axis=-1, keepdims=True)
            m = lax.stop_gradient(jnp.maximum(jnp.max(s, axis=-1, keepdims=True), sink))
            e = jnp.exp(s - m)
            prob = e / (jnp.sum(e, axis=-1, keepdims=True) + jnp.exp(sink - m))
            acc = acc + bdot(prob, vband) * msk
        outs.append(acc)
    return jnp.concatenate(outs, axis=1)


def _swa_specs(nb, rev):
    bi = (lambda i: nb - 1 - i) if rev else (lambda i: i)
    return [
        pl.BlockSpec((BLK, A_Q), lambda i: (bi(i), 0)),
        pl.BlockSpec((BLK, LANE), lambda i: (jnp.maximum(bi(i) - 1, 0), 6)),
        pl.BlockSpec((BLK, LANE), lambda i: (bi(i), 6)),
        pl.BlockSpec((BLK, LANE), lambda i: (jnp.maximum(bi(i) - 1, 0), 7)),
        pl.BlockSpec((BLK, LANE), lambda i: (bi(i), 7)),
        pl.BlockSpec((A_HEADS, BLK, 2 * BLK), lambda i: (0, 0, 0)),
        pl.BlockSpec((16, LANE), lambda i: (0, 0)),
    ]


def swa_fwd(proj, bias, sk):
    S = proj.shape[0]
    nb = S // BLK

    def body(q_ref, kp_ref, kc_ref, vp_ref, vc_ref, b_ref, s_ref, o_ref):
        qkv = [r[...].astype(f32) for r in (q_ref, kp_ref, kc_ref, vp_ref, vc_ref)]
        o_ref[...] = _swa_f(*qkv, b_ref[...], s_ref[...], pl.program_id(0) == 0).astype(bf16)

    return pl.pallas_call(
        body, name="swa_fwd", grid=(nb,), in_specs=_swa_specs(nb, False),
        out_specs=pl.BlockSpec((BLK, A_Q), lambda i: (i, 0)),
        out_shape=jax.ShapeDtypeStruct((S, A_Q), bf16), compiler_params=_cp(("parallel",)),
    )(proj, proj, proj, proj, proj, bias, sk)


def swa_bwd(proj, bias, sk, dmix):
    S = proj.shape[0]
    nb = S // BLK

    def body(q_ref, kp_ref, kc_ref, vp_ref, vc_ref, b_ref, s_ref, do_ref, dqkv_ref, db_ref, ds_ref, ck, cv):
        i = pl.program_id(0)

        @pl.when(i == 0)
        def _():
            db_ref[...] = jnp.zeros_like(db_ref)
            ds_ref[...] = jnp.zeros_like(ds_ref)
            ck[...] = jnp.zeros_like(ck)
            cv[...] = jnp.zeros_like(cv)
        first = i == nb - 1
        qkv = [r[...].astype(f32) for r in (q_ref, kp_ref, kc_ref, vp_ref, vc_ref)]
        _, vjp = jax.vjp(lambda *a: _swa_f(*a, first), *qkv, b_ref[...], s_ref[...])
        dq, dkp, dkc, dvp, dvc, db, ds = vjp(do_ref[...])
        dqkv_ref[...] = jnp.concatenate([dq, dkc + ck[...], dvc + cv[...]], axis=1).astype(bf16)
        ck[...] = dkp
        cv[...] = dvp
        db_ref[...] += db
        ds_ref[...] += ds

    return pl.pallas_call(
        body, name="swa_bwd", grid=(nb,),
        in_specs=_swa_specs(nb, True) + [pl.BlockSpec((BLK, A_Q), lambda i: (nb - 1 - i, 0))],
        out_specs=[pl.BlockSpec((BLK, D), lambda i: (nb - 1 - i, 0)),
                   pl.BlockSpec((A_HEADS, BLK, 2 * BLK), lambda i: (0, 0, 0)),
                   pl.BlockSpec((16, LANE), lambda i: (0, 0))],
        out_shape=[jax.ShapeDtypeStruct((S, D), bf16), jax.ShapeDtypeStruct((A_HEADS, BLK, 2 * BLK), f32),
                   jax.ShapeDtypeStruct((16, LANE), f32)],
        scratch_shapes=[pltpu.VMEM((BLK, LANE), f32), pltpu.VMEM((BLK, LANE), f32)],
        compiler_params=_cp(("arbitrary",)),
    )(proj, proj, proj, proj, proj, bias, sk, dmix)


def _dnprep_f(xext, w, is_qk):
    c = (w[3:4] * xext + w[2:3] * shift_down(xext, 1) + w[1:2] * shift_down(xext, 2) + w[0:1] * shift_down(xext, 3))
    a = _silu(c)[HALO:]
    n = a * lax.rsqrt(jnp.sum(a * a, axis=-1, keepdims=True) + EPS)
    return jnp.where(is_qk, n, a)


def dnprep_fwd(proj, cw):
    S = proj.shape[0]
    nblk = B_QKV // LANE
    T = S

    def body(x_ref, w_ref, o_ref):
        is_qk = pl.program_id(0) < 2 * B_QK // LANE
        wv = w_ref[...]

        def tile(r0, first):
            o_ref[pl.ds(r0, T), :] = _dnprep_f(_glu_gext(x_ref, r0, first, T), wv, is_qk)

        tile(0, True)

    return pl.pallas_call(
        body, name="dnprep_fwd", grid=(nblk,),
        in_specs=[pl.BlockSpec((S, LANE), lambda j: (0, j)), pl.BlockSpec((4, LANE), lambda j: (0, j))],
        out_specs=pl.BlockSpec((S, LANE), lambda j: (0, j)),
        out_shape=jax.ShapeDtypeStruct((S, B_QKV), f32), compiler_params=_cp(("parallel",)),
    )(proj, cw)


def dnprep_bwd(proj, cw, dqkvn):
    S = proj.shape[0]
    nblk = B_QKV // LANE

    T = S

    def body(x_ref, w_ref, d_ref, dx_ref, dw_ref):
        is_qk = pl.program_id(0) < 2 * B_QK // LANE
        wv = w_ref[...]

        def tile(r0, first):
            _, vjp = jax.vjp(lambda a, b: _dnprep_f(a, b, is_qk), _glu_gext(x_ref, r0, first, T), wv)
            dx, dw = vjp(d_ref[pl.ds(r0, T), :])
            dx_ref[pl.ds(r0, T), :] = dx[HALO:].astype(bf16)
            if not first:
                dx_ref[pl.ds(r0 - HALO, HALO), :] += dx[:HALO]
            return dw

        dw_ref[...] = tile(0, True)

    col = pl.BlockSpec((S, LANE), lambda j: (0, j))
    wsp = pl.BlockSpec((4, LANE), lambda j: (0, j))
    return pl.pallas_call(
        body, name="dnprep_bwd", grid=(nblk,), in_specs=[col, wsp, col], out_specs=[col, wsp],
        out_shape=[jax.ShapeDtypeStruct((S, B_QKV), bf16), jax.ShapeDtypeStruct((4, B_QKV), f32)],
        compiler_params=_cp(("parallel",)),
    )(proj, cw, dqkvn)


def _hdot(a, b, ca=1, cb=0):
    return _dg(a, b, ca, cb, HI)


def _bdg(a, b, ca, cb):
    dn = (((ca,), (cb,)), ((0,), (0,)))
    ah, bh = a.astype(bf16), b.astype(bf16)
    al, bl = (a - ah.astype(f32)).astype(bf16), (b - bh.astype(f32)).astype(bf16)
    return (lax.dot_general(ah, bh, dn, preferred_element_type=f32)
            + lax.dot_general(ah, bl, dn, preferred_element_type=f32)
            + lax.dot_general(al, bh, dn, preferred_element_type=f32))


@jax.custom_vjp
def hbd(a, b):
    return _bdg(a, b, 2, 1)


@jax.custom_vjp
def hbd_nt(a, b):
    return _bdg(a, b, 2, 2)


@jax.custom_vjp
def hbd_tn(a, b):
    return _bdg(a, b, 1, 1)


hbd.defvjp(lambda a, b: (hbd(a, b), (a, b)), lambda r, g: (hbd_nt(g, r[1]), hbd_tn(r[0], g)))
hbd_nt.defvjp(lambda a, b: (hbd_nt(a, b), (a, b)), lambda r, g: (hbd(g, r[1]), hbd_tn(g, r[0])))
hbd_tn.defvjp(lambda a, b: (hbd_tn(a, b), (a, b)), lambda r, g: (hbd_nt(r[1], g), hbd(r[0], g)))


def _stack(xs):
    return jnp.concatenate([x[None] for x in xs], axis=0)


def _lane_col(x, j):
    lane = lax.broadcasted_iota(jnp.int32, (1, LANE), 1)
    return jnp.sum(jnp.where(lane == j, x, 0.0), axis=-1, keepdims=True)


def _tri_inv(a_mat):
    r = lax.broadcasted_iota(jnp.int32, (1, CHUNK, CHUNK), 1)
    c = lax.broadcasted_iota(jnp.int32, (1, CHUNK, CHUNK), 2)
    pw = -a_mat
    inv = (r == c).astype(f32) + pw
    for _ in range(5):
        pw = hbd(pw, pw)
        inv = inv + hbd(inv, pw)
    return inv


@jax.custom_vjp
def _tri_inv_known(a_mat, inv):
    return inv


_tri_inv_known.defvjp(lambda a, inv: (inv, inv),
                      lambda inv, g: (-hbd_tn(inv, hbd_nt(g, inv)), jnp.zeros_like(inv)))


def _dnc_f(q, k, v, seg, prm, inverse=_tri_inv):
    C = CHUNK
    B = q.shape[0]
    rows = seg.shape[0]
    beta_all = _sigmoid(seg)
    xx = seg + prm[1:2]
    g_all = -jnp.exp(prm[0:1]) * (jnp.maximum(xx, 0.0) + jnp.log(1.0 + jnp.exp(-jnp.abs(xx))))
    r2 = lax.broadcasted_iota(jnp.int32, (rows, rows), 0)
    c2 = lax.broadcasted_iota(jnp.int32, (rows, rows), 1)
    within = (r2 >= c2) & (r2 // C == c2 // C)
    gc_all = _hdot(within.astype(f32), g_all)
    beta = _stack([_lane_col(beta_all[C * j:C * (j + 1)], h) for j in range(rows // C) for h in range(6)])
    gc = _stack([_lane_col(gc_all[C * j:C * (j + 1)], 6 + h) for j in range(rows // C) for h in range(6)])
    r = lax.broadcasted_iota(jnp.int32, (1, C, C), 1)
    c = lax.broadcasted_iota(jnp.int32, (1, C, C), 2)
    incl = r >= c
    strict = r > c
    gct = [gc_all[C * j:C * (j + 1)].T for j in range(rows // C)]
    g_row = _stack([jnp.broadcast_to(gct[j][6 + h:7 + h, :], (C, C))
                    for j in range(rows // C) for h in range(6)])
    decay = jnp.where(incl, jnp.exp(jnp.where(incl, gc - g_row, 0.0)), 0.0)
    a_mat = beta * sbd_nt(k, k) * jnp.where(strict, decay, 0.0)
    eg = jnp.exp(gc)
    inv = inverse(a_mat)
    u = hbd(inv, beta * v)
    w = hbd(inv, (beta * eg) * k)
    qc = q * (B_DH ** -0.5)
    attn = sbd_nt(qc, k) * decay
    last = (lax.broadcasted_iota(jnp.int32, (1, C, 1), 1) == C - 1).astype(f32)
    g_last = jnp.sum(gc * last, axis=1, keepdims=True)
    dc = jnp.broadcast_to(jnp.exp(g_last), (B, 1, LANE)).reshape(B, LANE)
    return u, w, qc * eg, k * jnp.exp(g_last - gc), attn, dc, inv


def _b1(a, b, ca, cb):
    return lax.dot_general(a.astype(bf16), b.astype(bf16), (((ca,), (cb,)), ((0,), (0,))), preferred_element_type=f32)


@jax.custom_vjp
def sbd(a, b):
    return _b1(a, b, 2, 1)


@jax.custom_vjp
def sbd_nt(a, b):
    return _b1(a, b, 2, 2)


@jax.custom_vjp
def sbd_tn(a, b):
    return _b1(a, b, 1, 1)


sbd.defvjp(lambda a, b: (sbd(a, b), (a, b)), lambda r, g: (sbd_nt(g, r[1]), sbd_tn(r[0], g)))
sbd_nt.defvjp(lambda a, b: (sbd_nt(a, b), (a, b)), lambda r, g: (sbd(g, r[1]), sbd_tn(g, r[0])))
sbd_tn.defvjp(lambda a, b: (sbd_tn(a, b), (a, b)), lambda r, g: (sbd_nt(r[1], g), sbd(r[0], g)))


def _dns_f(S0, u, w, qd, kt, attn, dcrows):
    dc = _lane_col(dcrows, 0).reshape(6, 1, 1)
    delta = u - sbd(w, S0)
    out = sbd(qd, S0) + sbd(attn, delta)
    return out, dc * S0 + sbd_tn(kt, delta)


def _dnpost_f(o, z, grow):
    outs = []
    for h in range(6):
        oh = o[:, LANE * h:LANE * (h + 1)]
        outs.append(oh * lax.rsqrt(jnp.mean(oh * oh, axis=-1, keepdims=True) + EPS) * grow
                    * _silu(z[:, LANE * h:LANE * (h + 1)]))
    return jnp.concatenate(outs, axis=1)


def _hs(h):
    return slice(LANE * h, LANE * (h + 1))


DN_CHUNKS = 4


def _heads(ref, share):
    return _stack([ref[CHUNK * j:CHUNK * (j + 1), _hs(h // share)]
                   for j in range(ref.shape[0] // CHUNK) for h in range(6)])


def _put_heads(ref, val):
    for j in range(ref.shape[0] // CHUNK):
        for h in range(6):
            ref[CHUNK * j:CHUNK * (j + 1), _hs(h)] = val[6 * j + h].astype(ref.dtype)


def _dnc_in_specs():
    rows = CHUNK * DN_CHUNKS
    return [
        pl.BlockSpec((rows, B_QK), lambda n: (n, 0)),
        pl.BlockSpec((rows, B_QK), lambda n: (n, 1)),
        pl.BlockSpec((rows, B_V), lambda n: (n, 1)),
        pl.BlockSpec((rows, LANE), lambda n: (n, 20)),
        pl.BlockSpec((8, LANE), lambda n: (0, 0)),
    ]


def _dnc_out_specs(rev_nc=None, chunks=1):
    ci = (lambda n: n) if rev_nc is None else (lambda n: rev_nc - 1 - n)
    wide = pl.BlockSpec((CHUNK * chunks, B_V), lambda n: (ci(n), 0))
    return [wide, wide, wide, wide, pl.BlockSpec((chunks, 6, CHUNK, CHUNK), lambda n: (ci(n), 0, 0, 0)),
            pl.BlockSpec((chunks, 8, LANE), lambda n: (ci(n), 0, 0))]


def _dc_rows(dc):
    pad = jnp.zeros((2, LANE), f32)
    return _stack([jnp.concatenate([dc[6 * j:6 * (j + 1)], pad], axis=0) for j in range(dc.shape[0] // 6)])


def _dnc_shapes(S, mm=f32):
    nc = S // CHUNK
    return [jax.ShapeDtypeStruct((S, B_V), f32)] + [jax.ShapeDtypeStruct((S, B_V), mm)] * 3 + [
        jax.ShapeDtypeStruct((nc, 6, CHUNK, CHUNK), mm), jax.ShapeDtypeStruct((nc, 8, LANE), f32)]


def dnc_fwd(qkvn, proj, prm):
    S = proj.shape[0]

    def body(q_ref, k_ref, v_ref, s_ref, p_ref, u_ref, w_ref, qd_ref, kt_ref, at_ref, dc_ref, inv_ref):
        u, w, qd, kt, attn, dc, inv = _dnc_f(_heads(q_ref, 2), _heads(k_ref, 2), _heads(v_ref, 1), s_ref[...],
                                             p_ref[...])
        inv_ref[...] = inv.reshape(inv_ref.shape)
        _put_heads(u_ref, u)
        _put_heads(w_ref, w)
        _put_heads(qd_ref, qd)
        _put_heads(kt_ref, kt)
        at_ref[...] = attn.reshape(at_ref.shape).astype(at_ref.dtype)
        dc_ref[...] = _dc_rows(dc)

    outs = _dnc_out_specs(chunks=DN_CHUNKS)
    out = pl.pallas_call(
        body, name="dn_chunk_fwd", grid=(S // (CHUNK * DN_CHUNKS),), in_specs=_dnc_in_specs(),
        out_specs=outs + [outs[4]], out_shape=_dnc_shapes(S, bf16) + [_dnc_shapes(S)[4]],
        compiler_params=_cp(("parallel",)),
    )(qkvn, qkvn, qkvn, proj, prm)
    return out[:6], out[6]


def dnc_bwd(qkvn, proj, prm, inv, cots):
    S = proj.shape[0]

    def body(q_ref, k_ref, v_ref, s_ref, p_ref, inv_ref, du_ref, dw_ref, dqd_ref, dkt_ref, dat_ref, ddc_ref,
             dx_ref, dseg_ref, dprm_ref):
        @pl.when(pl.program_id(0) == 0)
        def _():
            dprm_ref[...] = jnp.zeros_like(dprm_ref)
        nb = 6 * DN_CHUNKS
        known = functools.partial(_tri_inv_known, inv=inv_ref[...].reshape(nb, CHUNK, CHUNK))
        _, vjp = jax.vjp(lambda *a: _dnc_f(*a, inverse=known)[:6], _heads(q_ref, 2), _heads(k_ref, 2),
                         _heads(v_ref, 1), s_ref[...], p_ref[...])
        ddc = jnp.concatenate([ddc_ref[j, 0:6, :] for j in range(DN_CHUNKS)], axis=0)
        dq, dk, dv, dseg, dprm = vjp((_heads(du_ref, 1), _heads(dw_ref, 1), _heads(dqd_ref, 1), _heads(dkt_ref, 1),
                                      dat_ref[...].reshape(nb, CHUNK, CHUNK), ddc))
        for j in range(DN_CHUNKS):
            o = 6 * j
            dx_ref[CHUNK * j:CHUNK * (j + 1), :] = jnp.concatenate(
                [dq[o] + dq[o + 1], dq[o + 2] + dq[o + 3], dq[o + 4] + dq[o + 5],
                 dk[o] + dk[o + 1], dk[o + 2] + dk[o + 3], dk[o + 4] + dk[o + 5]] + [dv[o + h] for h in range(6)], axis=1)
        dseg_ref[...] = dseg.astype(bf16)
        dprm_ref[...] += dprm

    rows = CHUNK * DN_CHUNKS
    outs = _dnc_out_specs(chunks=DN_CHUNKS)
    return pl.pallas_call(
        body, name="dn_chunk_bwd", grid=(S // rows,),
        in_specs=_dnc_in_specs() + [outs[4]] + outs,
        out_specs=[pl.BlockSpec((rows, B_QKV), lambda n: (n, 0)), pl.BlockSpec((rows, LANE), lambda n: (n, 0)),
                   pl.BlockSpec((8, LANE), lambda n: (0, 0))],
        out_shape=[jax.ShapeDtypeStruct((S, B_QKV), f32), jax.ShapeDtypeStruct((S, LANE), bf16),
                   jax.ShapeDtypeStruct((8, LANE), f32)],
        compiler_params=_cp(("arbitrary",)),
    )(qkvn, qkvn, qkvn, proj, prm, inv, *cots)


def dns_fwd(chunked):
    u = chunked[0]
    S = u.shape[0]
    nc = S // CHUNK

    def body(u_ref, w_ref, qd_ref, kt_ref, at_ref, dc_ref, o_ref, st_ref, st):
        @pl.when(pl.program_id(0) == 0)
        def _():
            st[...] = jnp.zeros_like(st)
        S0 = st[...]
        st_ref[0] = S0
        out, S1 = _dns_f(S0, _heads(u_ref, 1), _heads(w_ref, 1), _heads(qd_ref, 1), _heads(kt_ref, 1),
                         at_ref[0], dc_ref[0, 0:6, :])
        _put_heads(o_ref, out)
        st[...] = S1

    return pl.pallas_call(
        body, name="dn_scan_fwd", grid=(nc,), in_specs=_dnc_out_specs(),
        out_specs=[pl.BlockSpec((CHUNK, B_V), lambda n: (n, 0)),
                   pl.BlockSpec((1, 6, B_DH, B_DH), lambda n: (n, 0, 0, 0))],
        out_shape=[jax.ShapeDtypeStruct((S, B_V), f32), jax.ShapeDtypeStruct((nc, 6, B_DH, B_DH), f32)],
        scratch_shapes=[pltpu.VMEM((6, B_DH, B_DH), f32)],
        compiler_params=_cp(("arbitrary",)),
    )(*chunked)


def dns_bwd(chunked, states, do):
    S = do.shape[0]
    nc = S // CHUNK

    def body(u_ref, w_ref, qd_ref, kt_ref, at_ref, dc_ref, st_ref, do_ref,
             du_ref, dw_ref, dqd_ref, dkt_ref, dat_ref, ddc_ref, dst):
        @pl.when(pl.program_id(0) == 0)
        def _():
            dst[...] = jnp.zeros_like(dst)
        _, vjp = jax.vjp(_dns_f, st_ref[0], _heads(u_ref, 1), _heads(w_ref, 1).astype(f32),
                         _heads(qd_ref, 1).astype(f32), _heads(kt_ref, 1).astype(f32), at_ref[0].astype(f32),
                         dc_ref[0, 0:6, :])
        dS0, du, dw, dqd, dkt, dat, ddc = vjp((_heads(do_ref, 1), dst[...]))
        dst[...] = dS0
        _put_heads(du_ref, du)
        _put_heads(dw_ref, dw)
        _put_heads(dqd_ref, dqd)
        _put_heads(dkt_ref, dkt)
        dat_ref[0] = dat
        ddc_ref[0] = jnp.concatenate([ddc, jnp.zeros((2, LANE), f32)], axis=0)

    return pl.pallas_call(
        body, name="dn_scan_bwd", grid=(nc,),
        in_specs=_dnc_out_specs(nc) + [pl.BlockSpec((1, 6, B_DH, B_DH), lambda n: (nc - 1 - n, 0, 0, 0)),
                                       pl.BlockSpec((CHUNK, B_V), lambda n: (nc - 1 - n, 0))],
        out_specs=_dnc_out_specs(nc), out_shape=_dnc_shapes(S),
        scratch_shapes=[pltpu.VMEM((6, B_DH, B_DH), f32)],
        compiler_params=_cp(("arbitrary",)),
    )(*chunked, states, do)


def dnpost_fwd(o, proj, prm):
    S = o.shape[0]
    t = min(512, S)

    def body(o_ref, z_ref, p_ref, y_ref):
        y_ref[...] = _dnpost_f(o_ref[...], z_ref[...], p_ref[2:3, :]).astype(bf16)

    tok = pl.BlockSpec((t, B_V), lambda i: (i, 0))
    return pl.pallas_call(
        body, name="dn_post_fwd", grid=(S // t,),
        in_specs=[tok, pl.BlockSpec((t, B_V), lambda i: (i, 2)), pl.BlockSpec((8, LANE), lambda i: (0, 0))],
        out_specs=tok, out_shape=jax.ShapeDtypeStruct((S, B_V), bf16), compiler_params=_cp(("parallel",)),
    )(o, proj, prm)


def dnpost_bwd(o, proj, prm, dmix):
    S = o.shape[0]
    t = min(512, S)

    def body(o_ref, z_ref, p_ref, dy_ref, do_ref, dz_ref, dg_ref):
        @pl.when(pl.program_id(0) == 0)
        def _():
            dg_ref[...] = jnp.zeros_like(dg_ref)
        _, vjp = jax.vjp(_dnpost_f, o_ref[...], z_ref[...], p_ref[2:3, :])
        do, dz, dg = vjp(dy_ref[...])
        do_ref[...] = do
        dz_ref[...] = dz.astype(bf16)
        dg_ref[...] += dg

    tok = pl.BlockSpec((t, B_V), lambda i: (i, 0))
    return pl.pallas_call(
        body, name="dn_post_bwd", grid=(S // t,),
        in_specs=[tok, pl.BlockSpec((t, B_V), lambda i: (i, 2)), pl.BlockSpec((8, LANE), lambda i: (0, 0)), tok],
        out_specs=[tok, tok, pl.BlockSpec((1, LANE), lambda i: (0, 0))],
        out_shape=[jax.ShapeDtypeStruct((S, B_V), f32), jax.ShapeDtypeStruct((S, B_V), bf16),
                   jax.ShapeDtypeStruct((1, LANE), f32)],
        compiler_params=_cp(("arbitrary",)),
    )(o, proj, prm, dmix)


N_FF_BLK = D_FF // LANE
GU_SHARD = 2 * D_FF // 4


GLU_ROWS = 256
HALO = 16


def _glu_conv(gext, w, b):
    return (w[2:3] * gext + w[1:2] * shift_down(gext, 1) + w[0:1] * shift_down(gext, 2) + b)[HALO:]


def _glu_gate(c, up):
    return _silu(c) * up


def _glu_gext(g_ref, r0, first, T=GLU_ROWS):
    if first:
        return jnp.concatenate([jnp.zeros((HALO, LANE), f32), g_ref[0:T, :].astype(f32)], axis=0)
    return g_ref[pl.ds(r0 - HALO, T + HALO), :].astype(f32)


def glu_fwd(gu, w, b, name):
    S = gu.shape[0]
    T = min(GLU_ROWS, S // 2)

    def body(g_ref, u_ref, w_ref, b_ref, o_ref, c_ref):
        wv, bv = w_ref[...], b_ref[...]

        def tile(r0, first):
            c = _glu_conv(_glu_gext(g_ref, r0, first, T), wv, bv)
            c_ref[pl.ds(r0, T), :] = c.astype(bf16)
            o_ref[pl.ds(r0, T), :] = _glu_gate(c, u_ref[pl.ds(r0, T), :].astype(f32)).astype(bf16)

        tile(0, True)

        @pl.loop(1, S // T)
        def _(t):
            tile(pl.multiple_of(t * T, T), False)

    col = pl.BlockSpec((S, LANE), lambda j: (0, j))
    return pl.pallas_call(
        body, name=name, grid=(N_FF_BLK,),
        in_specs=[col, pl.BlockSpec((S, LANE), lambda j: (0, N_FF_BLK + j)), pl.BlockSpec((3, LANE), lambda j: (0, j)),
                  pl.BlockSpec((1, LANE), lambda j: (0, j))],
        out_specs=[col, col], out_shape=[jax.ShapeDtypeStruct((S, D_FF), bf16)] * 2,
        compiler_params=_cp(("parallel",)),
    )(gu, gu, w, b.reshape(1, D_FF))


def glu_bwd(gu, c, w, b, dact, name):
    S = gu.shape[0]
    T = min(GLU_ROWS, S // 2)

    def body(g_ref, u_ref, c_ref, w_ref, b_ref, d_ref, dg_ref, dw_ref, db_ref, acc):
        wv, bv = w_ref[...], b_ref[...]

        def tile(r0, first):
            rows = pl.ds(r0, T)
            _, vjp_gate = jax.vjp(_glu_gate, c_ref[rows, :].astype(f32), u_ref[rows, :].astype(f32))
            dc, du = vjp_gate(d_ref[rows, :].astype(f32))
            _, vjp_conv = jax.vjp(_glu_conv, _glu_gext(g_ref, r0, first, T), wv, bv)
            dgx, dw, db = vjp_conv(dc)
            acc[pl.ds(r0, T), :] = dgx[HALO:]
            if not first:
                acc[pl.ds(r0 - HALO, HALO), :] += dgx[:HALO]
            dg_ref[1, pl.ds(r0, T), :] = du.astype(bf16)
            return dw, db

        dw0, db0 = tile(0, True)
        dw_ref[...] = dw0
        db_ref[...] = db0

        @pl.loop(1, S // T)
        def _(t):
            dw, db = tile(pl.multiple_of(t * T, T), False)
            dw_ref[...] += dw
            db_ref[...] += db

        dg_ref[0] = acc[...].astype(bf16)

    col = pl.BlockSpec((S, LANE), lambda j: (0, j))
    wsp = pl.BlockSpec((3, LANE), lambda j: (0, j))
    bsp = pl.BlockSpec((1, LANE), lambda j: (0, j))
    return pl.pallas_call(
        body, name=name, grid=(N_FF_BLK,),
        in_specs=[col, pl.BlockSpec((S, LANE), lambda j: (0, N_FF_BLK + j)), col, wsp, bsp, col],
        out_specs=[pl.BlockSpec((2, S, LANE), lambda j: (0, 0, j)), wsp, bsp],
        out_shape=[jax.ShapeDtypeStruct((2, S, D_FF), bf16), jax.ShapeDtypeStruct((3, D_FF), f32),
                   jax.ShapeDtypeStruct((1, D_FF), f32)],
        scratch_shapes=[pltpu.VMEM((S, LANE), f32)],
        compiler_params=_cp(("parallel",)),
    )(gu, gu, c, w, b.reshape(1, D_FF), dact)


def gu_fwd(n2, wg, name):
    S = n2.shape[0]
    tm = min(MM_ROWS, S)

    def body(a_ref, w_ref, o_ref):
        o_ref[...] = _dg(a_ref[...], w_ref[...], 1, 0).astype(bf16)

    return pl.pallas_call(
        body, name=name, grid=(4, S // tm),
        in_specs=[pl.BlockSpec((tm, D), lambda s, m: (m, 0)), pl.BlockSpec((None, D, GU_SHARD), lambda s, m: (s, 0, 0))],
        out_specs=pl.BlockSpec((tm, GU_SHARD), lambda s, m: (m, s)),
        out_shape=jax.ShapeDtypeStruct((S, 2 * D_FF), bf16), compiler_params=_cp(("parallel", "parallel")),
    )(n2, wg)


def gu_bwd_x(dgu, wg, norm, name):
    S = dgu.shape[1]
    tm = min(NORM_ROWS, S)

    def body(d_ref, w_ref, h_ref, g_ref, r_ref, o_ref, dg_ref):
        _acc_then_norm_bwd(_dg(d_ref[...], w_ref[...], 1, 1), 4, h_ref, g_ref, r_ref, o_ref, dg_ref)

    tok = pl.BlockSpec((tm, D), lambda m, s: (m, 0))
    vec = pl.BlockSpec((1, D), lambda m, s: (0, 0))
    return pl.pallas_call(
        body, name=name, grid=(S // tm, 4),
        in_specs=[pl.BlockSpec((None, tm, GU_SHARD), lambda m, s: (s // 2, m, s % 2)),
                  pl.BlockSpec((None, D, GU_SHARD), lambda m, s: (s, 0, 0)), tok, vec, tok],
        out_specs=[tok, vec],
        out_shape=[jax.ShapeDtypeStruct((S, D), f32), jax.ShapeDtypeStruct((1, D), f32)],
        compiler_params=_cp(("arbitrary", "arbitrary")),
    )(dgu, wg, norm[0], norm[1].reshape(1, D), norm[2])


def gu_bwd_w(n2, dgu, name):
    S = n2.shape[0]
    tm = min(MM_ROWS, S)
    nm = S // tm

    def body(a_ref, d_ref, o_ref, acc):
        @pl.when(pl.program_id(1) == 0)
        def _():
            acc[...] = jnp.zeros_like(acc)
        acc[...] += _dg(a_ref[...], d_ref[...], 0, 0)

        @pl.when(pl.program_id(1) == nm - 1)
        def _():
            o_ref[...] = acc[...].astype(bf16)

    return pl.pallas_call(
        body, name=name, grid=(4, nm),
        in_specs=[pl.BlockSpec((tm, D), lambda s, m: (m, 0)),
                  pl.BlockSpec((None, tm, GU_SHARD), lambda s, m: (s // 2, m, s % 2))],
        out_specs=pl.BlockSpec((None, D, GU_SHARD), lambda s, m: (s, 0, 0)),
        out_shape=jax.ShapeDtypeStruct((4, D, GU_SHARD), bf16),
        scratch_shapes=[pltpu.VMEM((D, GU_SHARD), f32)],
        compiler_params=_cp(("parallel", "arbitrary")),
    )(n2, dgu)


def _pair_cols(w):
    lead = w.shape[:-1]
    return w.reshape(lead + (2, 6, A_DH)).swapaxes(-3, -2).reshape(lead + (A_Q,))


def _unpair_cols(w):
    lead = w.shape[:-1]
    return w.reshape(lead + (6, 2, A_DH)).swapaxes(-3, -2).reshape(lead + (A_Q,))


def _lay_in_a(w):
    return jnp.concatenate([_pair_cols(w[:, :A_Q]), w[:, A_Q:]], axis=1)


def _unlay_in_a(w):
    return jnp.concatenate([_unpair_cols(w[:, :A_Q]), w[:, A_Q:]], axis=1)


def _lay_out_a(w):
    return jnp.concatenate([_pair_cols(w[:A_Q].T).T, w[A_Q:]], axis=0)


def _unlay_out_a(w):
    return jnp.concatenate([_unpair_cols(w[:A_Q].T).T, w[A_Q:]], axis=0)


def _lay_in_b(w):
    return jnp.concatenate([w[:, :2304], w[:, 2316:], w[:, 2304:2316],
                            jnp.zeros((w.shape[0], LANE - 12), w.dtype)], axis=1)


def _unlay_in_b(w):
    return jnp.concatenate([w[:, :2304], w[:, 2560:2572], w[:, 2304:2560]], axis=1)


def _chip_cols(w):
    return jnp.moveaxis(w.reshape(w.shape[0], 4, w.shape[1] // 4), 1, 0)


def _unchip_cols(w):
    return jnp.moveaxis(w, 0, 1).reshape(w.shape[1], 4 * w.shape[2])


def _local_step(x, mem, target, P):
    arrive = P.get("arrive", lambda key, after: None)
    ready = P.get("ready", lambda key, grads, dep: dep)
    sk = jnp.zeros((16, LANE), f32).at[:A_HEADS].set(jnp.broadcast_to(P["sinks"][:, None], (A_HEADS, LANE)))
    prm = jnp.zeros((8, LANE), f32).at[0, 6:12].set(P["a_log"]).at[1, 6:12].set(P["dt_bias"]).at[2].set(P["out_norm_g"])
    bias = bias_build(P["rel_bias"])
    saved = []
    h = x
    n1 = rms_fwd(h, P["g_mix"][0], "rms_mix0")
    for i in range(2):
        arrive(("w_in", i), n1)
        if i == 0:
            proj = mm_nn(n1, P["w_in_a"], out_dtype=bf16, name="proj_a")
        else:
            proj = mm_nn(n1, P["w_in_b"], name="proj_b")
        arrive(("w_mem", i), proj)
        kv = memkv_fwd(mem, P["g_mem"][i], P["w_mem"][i], f"memkv{i}")
        if i == 0:
            self_out = swa_fwd(proj, bias, sk)
            cross = xattn_fwd(proj, A_Q + 2 * LANE, kv, "xattn_a")
            extra = ()
        else:
            qkvn = dnprep_fwd(proj, P["conv_qkv"])
            chunked, inv = dnc_fwd(qkvn, proj, prm)
            o, states = dns_fwd(chunked)
            self_out = dnpost_fwd(o, proj, prm)
            cross = xattn_fwd(proj, 2304, kv, "xattn_b")
            extra = (qkvn, chunked, inv, states, o)
        mix = (self_out, cross)
        arrive(("w_out", i), cross)
        h2, n2 = mm_res_norm(mix, P["w_out"][i], h, P["g_ffn"][i], f"out_proj{i}")
        arrive(("w_gu", i), n2)
        gu = gu_fwd(n2, P["w_gu"][i], f"gate_up{i}")
        act, pre = glu_fwd(gu, P["ffn_cw"][i], P["ffn_cb"][i], f"glu{i}")
        arrive(("w_down", i), act)
        saved.append((h, n1, kv, proj, mix, h2, n2, gu, pre, act, extra))
        if i == 0:
            h, n1 = mm_res_norm(act, P["w_down"][i], h2, P["g_mix"][1], f"down{i}")
        else:
            h = mm_nn(act, P["w_down"][i], res=h2, name=f"down{i}")

    loss, dh, dg_fin = loss_head(h, P["g_fin"], target)
    G = {"g_fin": dg_fin[0], "g_mix": [None, None], "g_mem": [None, None], "g_ffn": [None, None],
         "w_mem": [None, None], "w_out": [None, None], "w_gu": [None, None], "w_down": [None, None],
         "ffn_cw": [None, None], "ffn_cb": [None, None]}
    for i in (1, 0):
        hin, n1, kv, proj, mix, h2, n2, gu, pre, act, extra = saved[i]
        dact = mm_nt(dh, P["w_down"][i], out_dtype=bf16, name=f"d_act{i}")
        G["w_down"][i] = mm_tn(act, dh, name=f"dw_down{i}")
        dgu, dcw, dcb = glu_bwd(gu, pre, P["ffn_cw"][i], P["ffn_cb"][i], dact, f"glu_bwd{i}")
        G["ffn_cw"][i], G["ffn_cb"][i] = dcw, dcb[0]
        G["w_gu"][i] = gu_bwd_w(n2, dgu, f"dw_gu{i}")
        g_ffn = ready(("ffn", i), G, P["g_ffn"][i])
        dh2, dg = gu_bwd_x(dgu, P["w_gu"][i], (h2, g_ffn, dh), f"d_n2_{i}")
        G["g_ffn"][i] = dg[0]
        dmix = mm_nt(dh2, P["w_out"][i], name=f"d_mix{i}")
        G["w_out"][i] = mm_tn(mix, dh2, name=f"dw_out{i}")
        if i == 0:
            dqkv, dbias, dsk = swa_bwd(proj, bias, sk, dmix)
            dxq, dkv = xattn_bwd(proj, A_Q + 2 * LANE, kv, dmix, "xattn_a_bwd")
            dproj = (dqkv, dxq)
            G["sinks"] = dsk[:A_HEADS, 0]
            G["rel_bias"] = bias_grad(dbias)[:, :A_HEADS]
            w_in, gname = P["w_in_a"], "w_in_a"
        else:
            qkvn, chunked, inv, states, o = extra
            do, dz, dgo = dnpost_bwd(o, proj, prm, dmix)
            dqkvn, dseg, dprm = dnc_bwd(qkvn, proj, prm, inv, dns_bwd(chunked, states, do))
            draw, dconv = dnprep_bwd(proj, P["conv_qkv"], dqkvn)
            dxq, dkv = xattn_bwd(proj, 2304, kv, dmix, "xattn_b_bwd")
            dproj = (draw, dz, dxq, dseg)
            G["conv_qkv"] = dconv
            G["a_log"], G["dt_bias"], G["out_norm_g"] = dprm[0, 6:12], dprm[1, 6:12], dgo[0]
            w_in, gname = P["w_in_b"], "w_in_b"
        G[gname] = mm_tn(n1, dproj, name=f"d{gname}")
        dh, dg = mm_nt_norm(dproj, w_in, (hin, P["g_mix"][i], dh2), f"d_n1_{i}")
        G["g_mix"][i] = dg[0]
        dgm, dwm = memkv_bwd(mem, P["g_mem"][i], P["w_mem"][i], dkv, f"memkv_bwd{i}")
        G["g_mem"][i], G["w_mem"][i] = dgm[0], dwm
        ready(("mix", i), G, None)
    return loss, dh, G


def _grads_to_ref(G):
    return {
        "rel_bias": G["rel_bias"], "norm_mix_g": jnp.stack(G["g_mix"]), "norm_mem_g": jnp.stack(G["g_mem"]),
        "w_mem_kv": jnp.stack(G["w_mem"]),
        "w_out": jnp.stack([_unlay_out_a(G["w_out"][0]), G["w_out"][1]]),
        "w_in_a": _unlay_in_a(G["w_in_a"])[None], "sinks_a": G["sinks"][None],
        "w_in_b": _unlay_in_b(G["w_in_b"])[None], "conv_qkv_b": G["conv_qkv"][None],
        "a_log_b": G["a_log"][None], "dt_bias_b": G["dt_bias"][None], "out_norm_g_b": G["out_norm_g"][None],
        "norm_ffn_g": jnp.stack(G["g_ffn"]),
        "w_gate_up": jnp.stack([_unchip_cols(G["w_gu"][0]), _unchip_cols(G["w_gu"][1])]).astype(f32),
        "ffn_conv_w": jnp.stack(G["ffn_cw"]), "ffn_conv_b": jnp.stack(G["ffn_cb"]),
        "w_down": jnp.stack(G["w_down"]), "final_norm_g": G["g_fin"],
    }


ANY = pl.BlockSpec(memory_space=pl.ANY)


def _place():
    return lax.axis_index("x"), lax.axis_index("y"), lax.axis_index("c")


def allreduce_small(buf):
    R = buf.shape[0]

    def body(b_ref, o_ref, recv, ssem, rsem):
        x, y, c = _place()
        me = 4 * x + 2 * y + c

        def peer(k):
            return (1 - x if k & 4 else x, 1 - y if k & 2 else y, 1 - c if k & 1 else c)

        def remote(k, slot):
            return pltpu.make_async_remote_copy(
                src_ref=b_ref, dst_ref=recv.at[slot], send_sem=ssem.at[k - 1], recv_sem=rsem.at[k - 1],
                device_id=peer(k), device_id_type=MESH)

        sends = [remote(k, me) for k in range(1, 8)]
        for cp in sends:
            cp.start()
        recv[me] = b_ref[...]
        for k in range(1, 8):
            px, py, pc = peer(k)
            remote(k, 4 * px + 2 * py + pc).wait_recv()
        for cp in sends:
            cp.wait_send()
        total = recv[0]
        for j in range(1, 8):
            total = total + recv[j]
        o_ref[...] = total

    return pl.pallas_call(
        body, name="small_allreduce",
        in_specs=[pl.BlockSpec(memory_space=pltpu.VMEM)], out_specs=pl.BlockSpec(memory_space=pltpu.VMEM),
        out_shape=jax.ShapeDtypeStruct(buf.shape, f32),
        scratch_shapes=[pltpu.VMEM((8, R, LANE), f32), pltpu.SemaphoreType.DMA((7,)), pltpu.SemaphoreType.DMA((7,))],
    )(buf)


def sum_slots(own, recv, chip, core, name):
    _, R, C = recv.shape
    tr = _row_tile(R, 256)
    nt = R // tr

    def body(p_ref, a_ref, r_ref, o_ref):
        acc = jnp.zeros((tr, C), f32)
        for s in range(4):
            acc = acc + jnp.where(p_ref[0] == s, a_ref[s], r_ref[s]).astype(f32)
        o_ref[...] = acc

    slots = pl.BlockSpec((4, tr, C), lambda i, p_ref: (0, i, 0))
    return pl.pallas_call(
        body, name=name, out_shape=jax.ShapeDtypeStruct((2 * R, C), f32),
        grid_spec=pltpu.PrefetchScalarGridSpec(
            num_scalar_prefetch=1, grid=(nt,), in_specs=[slots, slots],
            out_specs=pl.BlockSpec((tr, C), lambda i, p_ref: (p_ref[1] * nt + i, 0))),
        compiler_params=_cp(("parallel",)),
    )(jnp.stack([chip, core]).astype(jnp.int32), own, recv)


def _half(ref, core, axis=0):
    half = ref.shape[axis] // 2
    idx = (slice(None),) * axis + (pl.ds(core * half, half),)
    return ref.at[idx]


IN_HBM = pl.BlockSpec(memory_space=pltpu.HBM)
IN_SEM = pl.BlockSpec(memory_space=pltpu.SEMAPHORE)
SIDE_EFFECT = pltpu.SideEffectType.DATAFLOW_SIDE_EFFECTING


def _gather_copy(buf, i, k, ssem, rsem, place, landing):
    x, y, c = place
    px, py = [(1 - x, y), (x, 1 - y), (1 - x, 1 - y)][k]
    me = 2 * x + y
    return pltpu.make_async_remote_copy(
        src_ref=buf.at[me], dst_ref=buf.at[me if landing == "theirs" else 2 * px + py],
        send_sem=ssem.at[3 * i + k], recv_sem=rsem.at[3 * i + k], device_id=(px, py, c), device_id_type=MESH)


def gather_start(groups):
    flat = [b for grp in groups for b in grp]
    n, ng = len(flat), len(groups)

    def body(*refs):
        bufs, sems = refs[:n], refs[n:n + 2 * ng]
        place = _place()
        j = 0
        for g, grp in enumerate(groups):
            for i in range(len(grp)):
                for k in range(3):
                    _gather_copy(bufs[j], i, k, sems[2 * g], sems[2 * g + 1], place, "theirs").start()
                j += 1

    sem_shapes = [pltpu.SemaphoreType.DMA((3 * len(grp),)) for grp in groups for _ in range(2)]
    out = pl.pallas_call(
        body, name="gather_start", in_specs=[IN_HBM] * n, out_specs=(*[IN_SEM] * (2 * ng), *[IN_HBM] * n),
        out_shape=(*sem_shapes, *[pltpu.HBM(b.shape, b.dtype) for b in flat]),
        input_output_aliases={i: 2 * ng + i for i in range(n)},
        compiler_params=pltpu.CompilerParams(has_side_effects=SIDE_EFFECT),
    )(*[pltpu.with_memory_space_constraint(b, pltpu.HBM) for b in flat])
    sems, bufs = out[:2 * ng], list(out[2 * ng:])
    flights, j = [], 0
    for g, grp in enumerate(groups):
        flights.append((bufs[j:j + len(grp)], sems[2 * g], sems[2 * g + 1]))
        j += len(grp)
    return flights


def gather_wait(flight, after, name):
    bufs, ssem, rsem = flight
    n = len(bufs)

    def body(*refs):
        place = _place()
        for i in range(n):
            for k in range(3):
                cp = _gather_copy(refs[i], i, k, refs[n], refs[n + 1], place, "mine")
                cp.wait_send()
                cp.wait_recv()

    return pl.pallas_call(
        body, name=name, in_specs=[IN_HBM] * n + [IN_SEM, IN_SEM, ANY], out_specs=[IN_HBM] * n,
        out_shape=[pltpu.HBM(b.shape, b.dtype) for b in bufs], input_output_aliases={i: i for i in range(n)},
        compiler_params=pltpu.CompilerParams(has_side_effects=SIDE_EFFECT),
    )(*bufs, ssem, rsem, after)


def _scatter_copy(src, land, j, k, ssem, rsem, place, landing):
    x, y, c = place
    px, py = [(1 - x, y), (x, 1 - y), (1 - x, 1 - y)][k]
    return pltpu.make_async_remote_copy(
        src_ref=src.at[2 * px + py], dst_ref=land.at[2 * x + y if landing == "theirs" else 2 * px + py],
        send_sem=ssem.at[3 * j + k], recv_sem=rsem.at[3 * j + k], device_id=(px, py, c), device_id_type=MESH)


def scatter_start(srcs, name):
    n = len(srcs)
    lands = [lax.empty(g.shape, g.dtype) for g in srcs]

    def body(*refs):
        place = _place()
        for j in range(n):
            for k in range(3):
                _scatter_copy(refs[j], refs[n + j], j, k, refs[2 * n], refs[2 * n + 1], place, "theirs").start()
        refs[-1][...] = jnp.zeros_like(refs[-1])

    sem = pltpu.SemaphoreType.DMA((3 * n,))
    hbm = [pltpu.with_memory_space_constraint(b, pltpu.HBM) for b in list(srcs) + lands]
    out = pl.pallas_call(
        body, name=name, in_specs=[IN_HBM] * (2 * n),
        out_specs=(IN_SEM, IN_SEM, *[IN_HBM] * (2 * n), pl.BlockSpec(memory_space=pltpu.VMEM)),
        out_shape=(sem, sem, *[pltpu.HBM(b.shape, b.dtype) for b in hbm], jax.ShapeDtypeStruct((8, LANE), f32)),
        input_output_aliases={i: 2 + i for i in range(2 * n)},
        compiler_params=pltpu.CompilerParams(has_side_effects=SIDE_EFFECT),
    )(*hbm)
    return (list(out[2:2 + n]), list(out[2 + n:2 + 2 * n]), out[0], out[1]), out[-1]


def scatter_wait(flight, after, name):
    srcs, lands, ssem, rsem = flight
    n = len(srcs)

    def body(*refs):
        place = _place()
        for j in range(n):
            for k in range(3):
                cp = _scatter_copy(refs[j], refs[n + j], j, k, refs[2 * n], refs[2 * n + 1], place, "mine")
                cp.wait_send()
                cp.wait_recv()

    out = pl.pallas_call(
        body, name=name, in_specs=[IN_HBM] * (2 * n) + [IN_SEM, IN_SEM, ANY], out_specs=[IN_HBM] * (2 * n),
        out_shape=[pltpu.HBM(b.shape, b.dtype) for b in list(srcs) + list(lands)],
        input_output_aliases={i: i for i in range(2 * n)},
        compiler_params=pltpu.CompilerParams(has_side_effects=SIDE_EFFECT),
    )(*srcs, *lands, ssem, rsem, after)
    return list(out[:n]), list(out[n:])


def pair_exchange(gbufs, name):
    n = len(gbufs)

    def body(*refs):
        ins, outs = refs[:n], refs[n:2 * n]
        ssem, rsem = refs[2 * n:]
        x, y, c = _place()
        cps = [pltpu.make_async_remote_copy(
            src_ref=_half(ins[j], 1 - c, axis=1), dst_ref=outs[j], send_sem=ssem.at[j], recv_sem=rsem.at[j],
            device_id=(x, y, 1 - c), device_id_type=MESH) for j in range(n)]
        for cp in cps:
            cp.start()
        for cp in cps:
            cp.wait()

    return pl.pallas_call(
        body, name=name, in_specs=[ANY] * n, out_specs=[ANY] * n,
        out_shape=[jax.ShapeDtypeStruct((4, g.shape[1] // 2, g.shape[2]), g.dtype) for g in gbufs],
        scratch_shapes=[pltpu.SemaphoreType.DMA((n,)), pltpu.SemaphoreType.DMA((n,))],
    )(*gbufs)


def _row_tile(rows, cap=512):
    return max(t for t in range(16, min(rows, cap) + 1, 16) if rows % t == 0)


def pair_sum(mine, theirs, core, name):
    _, R, C = mine.shape
    half = R // 2
    tr = _row_tile(half)
    nt = half // tr

    def body(c_ref, a_ref, b_ref, o_ref):
        o_ref[...] = (a_ref[...].astype(f32) + b_ref[...].astype(f32)).astype(bf16)

    return pl.pallas_call(
        body, name=name, out_shape=jax.ShapeDtypeStruct(theirs.shape, bf16),
        grid_spec=pltpu.PrefetchScalarGridSpec(
            num_scalar_prefetch=1, grid=(4, nt),
            in_specs=[pl.BlockSpec((None, tr, C), lambda s, i, c_ref: (s, c_ref[0] * nt + i, 0)),
                      pl.BlockSpec((None, tr, C), lambda s, i, c_ref: (s, i, 0))],
            out_specs=pl.BlockSpec((None, tr, C), lambda s, i, c_ref: (s, i, 0))),
        compiler_params=_cp(("parallel", "parallel")),
    )(jnp.reshape(core, (1,)).astype(jnp.int32), mine, theirs)


def final_exchange(fins):
    n = len(fins)

    def body(*refs):
        outs = refs[n:2 * n]
        ssem, rsem = refs[2 * n:]
        x, y, c = _place()
        cps = [pltpu.make_async_remote_copy(
            src_ref=_half(outs[j], c), dst_ref=_half(outs[j], c), send_sem=ssem.at[j], recv_sem=rsem.at[j],
            device_id=(x, y, 1 - c), device_id_type=MESH) for j in range(n)]
        for cp in cps:
            cp.start()
        for cp in cps:
            cp.wait()

    return pl.pallas_call(
        body, name="final_exchange", in_specs=[ANY] * n, out_specs=[ANY] * n,
        out_shape=[jax.ShapeDtypeStruct(f.shape, f.dtype) for f in fins],
        input_output_aliases={j: j for j in range(n)},
        scratch_shapes=[pltpu.SemaphoreType.DMA((n,)), pltpu.SemaphoreType.DMA((n,))],
    )(*fins)


def adamw_big(w, m, v, gs, row0, name):
    L, R, C = w.shape
    tr = _row_tile(math.gcd(R, row0) if row0 else R, max(16, 262144 // C // 16 * 16))
    b0 = row0 // tr

    def body(*refs):
        w_ref, m_ref, v_ref = refs[:3]
        g_refs = refs[3:3 + L]
        g_ref, d_ref, nm_ref, nv_ref = refs[3 + L:]
        g = g_refs[0][...]
        for l in range(1, L):
            g = jnp.where(pl.program_id(0) == l, g_refs[l][...], g)
        d, nm, nv = _adamw_math(w_ref[...], g, m_ref[...], v_ref[...])
        g_ref[...] = g
        d_ref[...] = d
        nm_ref[...] = nm
        nv_ref[...] = nv

    own = pl.BlockSpec((None, tr, C), lambda l, i: (l, i, 0))
    off = pl.BlockSpec((tr, C), lambda l, i: (b0 + i, 0))
    return pl.pallas_call(
        body, name=name, grid=(L, R // tr), in_specs=[own, own, own] + [off] * L, out_specs=[own] * 4,
        out_shape=[jax.ShapeDtypeStruct((L, R, C), f32)] * 4, compiler_params=_cp(("parallel", "parallel")),
    )(w, m, v, *gs)


def _adamw_math(w, g, m, v):
    m = B1 * m + (1.0 - B1) * g
    v = B2 * v + (1.0 - B2) * (g * g)
    m_hat = m / (1.0 - B1 ** STEP)
    v_hat = v / (1.0 - B2 ** STEP)
    delta = -LR * (m_hat / (jnp.sqrt(v_hat) + AEPS) + WD * w)
    return delta, m, v


def adamw_small(w, m, v, g):
    def body(w_ref, m_ref, v_ref, g_ref, d_ref, nm_ref, nv_ref):
        d, nm, nv = _adamw_math(w_ref[...], g_ref[...], m_ref[...], v_ref[...])
        d_ref[...] = d
        nm_ref[...] = nm
        nv_ref[...] = nv

    return pl.pallas_call(body, name="adamw_small", out_shape=[jax.ShapeDtypeStruct(w.shape, f32)] * 3)(w, m, v, g)


CONV =(("conv_qkv_b", 2), ("ffn_conv_w", 2))
SMALL = ("rel_bias", "norm_mix_g", "norm_mem_g", "sinks_a", "a_log_b", "dt_bias_b", "out_norm_g_b", "norm_ffn_g",
         "ffn_conv_b", "final_norm_g")
WEIGHTS = ("rel_bias", "norm_mix_g", "norm_mem_g", "w_mem_kv", "w_out", "w_in_a", "sinks_a", "w_in_b", "conv_qkv_b",
           "a_log_b", "dt_bias_b", "out_norm_g_b", "norm_ffn_g", "w_gate_up", "ffn_conv_w", "ffn_conv_b", "w_down",
           "final_norm_g")
ARGS = ("x", "mem") + WEIGHTS + ("loss_target",) + tuple("m_" + n for n in WEIGHTS) + tuple("v_" + n for n in WEIGHTS)


def _rows(a, width):
    flat = a.reshape(-1)
    pad = (-flat.shape[0]) % (8 * width)
    if pad:
        flat = jnp.concatenate([flat, jnp.zeros((pad,), a.dtype)])
    return flat.reshape(-1, width)


def _nrows(shape, width):
    return _pad_to(-(-math.prod(shape) // width), 8)


def _pack(arrs, width, total_rows, dtype):
    parts = [_rows(a.astype(dtype), width) for a in arrs]
    used = sum(p.shape[0] for p in parts)
    if total_rows > used:
        parts.append(jnp.zeros((total_rows - used, width), dtype))
    return jnp.concatenate(parts, axis=0)


def _unpack(buf, shapes, width):
    out, r = [], 0
    for s in shapes:
        n = _nrows(s, width)
        out.append(buf[r:r + n].reshape(-1)[:math.prod(s)].reshape(s))
        r += n
    return out


def _pad_to(n, mult):
    return -(-n // mult) * mult


def kernel(x, mem, rel_bias, norm_mix_g, norm_mem_g, w_mem_kv, w_out, w_in_a, sinks_a, w_in_b, conv_qkv_b, a_log_b, dt_bias_b, out_norm_g_b, norm_ffn_g, w_gate_up, ffn_conv_w, ffn_conv_b, w_down, final_norm_g, loss_target, m_rel_bias, m_norm_mix_g, m_norm_mem_g, m_w_mem_kv, m_w_out, m_w_in_a, m_sinks_a, m_w_in_b, m_conv_qkv_b, m_a_log_b, m_dt_bias_b, m_out_norm_g_b, m_norm_ffn_g, m_w_gate_up, m_ffn_conv_w, m_ffn_conv_b, m_w_down, m_final_norm_g, v_rel_bias, v_norm_mix_g, v_norm_mem_g, v_w_mem_kv, v_w_out, v_w_in_a, v_sinks_a, v_w_in_b, v_conv_qkv_b, v_a_log_b, v_dt_bias_b, v_out_norm_g_b, v_norm_ffn_g, v_w_gate_up, v_ffn_conv_w, v_ffn_conv_b, v_w_down, v_final_norm_g):
    A = dict(zip(ARGS, (x, mem, rel_bias, norm_mix_g, norm_mem_g, w_mem_kv, w_out, w_in_a, sinks_a, w_in_b, conv_qkv_b, a_log_b, dt_bias_b, out_norm_g_b, norm_ffn_g, w_gate_up, ffn_conv_w, ffn_conv_b, w_down, final_norm_g, loss_target, m_rel_bias, m_norm_mix_g, m_norm_mem_g, m_w_mem_kv, m_w_out, m_w_in_a, m_sinks_a, m_w_in_b, m_conv_qkv_b, m_a_log_b, m_dt_bias_b, m_out_norm_g_b, m_norm_ffn_g, m_w_gate_up, m_ffn_conv_w, m_ffn_conv_b, m_w_down, m_final_norm_g, v_rel_bias, v_norm_mix_g, v_norm_mem_g, v_w_mem_kv, v_w_out, v_w_in_a, v_sinks_a, v_w_in_b, v_conv_qkv_b, v_a_log_b, v_dt_bias_b, v_out_norm_g_b, v_norm_ffn_g, v_w_gate_up, v_ffn_conv_w, v_ffn_conv_b, v_w_down, v_final_norm_g)))
    chip = 2 * lax.axis_index("x") + lax.axis_index("y")
    core = lax.axis_index("c")

    def own_slot(shard):
        return lax.dynamic_update_index_in_dim(lax.empty((4,) + shard.shape, shard.dtype), shard, chip, 0)

    def bslot(w):
        return own_slot(w.astype(bf16))

    groups = {
        ("w_in", 0): [bslot(w_in_a[0])],
        ("w_mem", 0): [bslot(w_mem_kv[0]), bslot(w_mem_kv[1]), own_slot(conv_qkv_b[0]),
                       own_slot(ffn_conv_w.reshape(6, -1))],
        ("w_out", 0): [bslot(w_out[0])], ("w_gu", 0): [bslot(w_gate_up[0])], ("w_down", 0): [bslot(w_down[0])],
        ("w_in", 1): [bslot(w_in_b[0])],
        ("w_out", 1): [bslot(w_out[1])], ("w_gu", 1): [bslot(w_gate_up[1])], ("w_down", 1): [bslot(w_down[1])],
    }
    flights = dict(zip(groups, gather_start(list(groups.values()))))
    P = {"rel_bias": rel_bias, "sinks": sinks_a[0], "a_log": a_log_b[0], "dt_bias": dt_bias_b[0],
         "out_norm_g": out_norm_g_b[0], "g_mix": norm_mix_g, "g_mem": norm_mem_g, "g_ffn": norm_ffn_g,
         "g_fin": final_norm_g, "ffn_cb": [ffn_conv_b[0], ffn_conv_b[1]], "w_mem": [None, None], "w_out": [None, None],
         "w_gu": [None, None], "w_down": [None, None], "ffn_cw": [None, None]}

    def rows4(g):
        return g.reshape(4 * g.shape[1], g.shape[2])

    def arrive(key, after):
        if key not in flights:
            return
        got = gather_wait(flights.pop(key), after, "gather_wait_%s%d" % key)
        name, i = key
        if name == "w_in":
            P["w_in_a" if i == 0 else "w_in_b"] = (_lay_in_a if i == 0 else _lay_in_b)(_unchip_cols(got[0]))
        elif name == "w_mem":
            P["w_mem"] = [rows4(got[0]), rows4(got[1])]
            P["conv_qkv"] = _unchip_cols(got[2])
            cw = _unchip_cols(got[3]).reshape(2, 3, D_FF)
            P["ffn_cw"] = [cw[0], cw[1]]
        elif name == "w_out":
            P["w_out"][i] = _lay_out_a(rows4(got[0])) if i == 0 else rows4(got[0])
        elif name == "w_gu":
            P["w_gu"][i] = got[0]
        else:
            P["w_down"][i] = rows4(got[0])

    def chip_rows(g):
        return g.reshape(4, g.shape[0] // 4, g.shape[-1])

    sent, started = {}, []

    def ready(key, G, dep):
        kind, i = key
        tag = "%s%d" % key
        if kind == "ffn":
            names, partial = ("gu", "down"), [G["w_gu"][i], chip_rows(G["w_down"][i]).astype(bf16)]
        else:
            g_out = _unlay_out_a(G["w_out"][0]) if i == 0 else G["w_out"][1]
            g_in = _unlay_in_a(G["w_in_a"]) if i == 0 else _unlay_in_b(G["w_in_b"])
            names = ("out", "in", "mem")
            partial = [chip_rows(g_out).astype(bf16), _chip_cols(g_in).astype(bf16), chip_rows(G["w_mem"][i]).astype(bf16)]
        theirs = pair_exchange(partial, "pair_exchange_" + tag)
        pair = [pair_sum(p, t, core, "pair_sum_%s%d" % (nm, i)) for p, t, nm in zip(partial, theirs, names)]
        flight, token = scatter_start(pair, "scatter_start_" + tag)
        sent[key] = (names, flight, token)
        started.append(token[0, 0])
        if dep is not None:
            while started:
                dep = dep + started.pop()
        return dep

    P["arrive"], P["ready"] = arrive, ready

    loss, dx, G = _local_step(x[0], mem[0], loss_target[0], P)
    gfull = _grads_to_ref(G)

    fin, after = {}, sent["mix", 0][2]
    for key in (("ffn", 1), ("mix", 1), ("ffn", 0), ("mix", 0)):
        names, flight, _ = sent[key]
        pair, arrived = scatter_wait(flight, after, "scatter_wait_%s%d" % key)
        for nm, p, r in zip(names, pair, arrived):
            after = fin[nm, key[1]] = sum_slots(p, r, chip, core, "sum_slots_%s%d" % (nm, key[1]))
    order = list(fin)
    done = dict(zip(order, final_exchange([fin[k] for k in order])))

    sm_shapes = [A[n].shape for n in SMALL] + [gfull[n].shape for n, _ in CONV] + [(LANE,)]
    sm_rows = _pad_to(sum(_nrows(s, LANE) for s in sm_shapes), 8)
    sbuf = _pack([gfull[n] for n in SMALL] + [gfull[n] for n, _ in CONV] + [loss[0]], LANE, sm_rows, f32)
    tot = _unpack(allreduce_small(sbuf), sm_shapes, LANE)
    gsmall = dict(zip(SMALL, tot[:len(SMALL)]))
    for (n, axis), t in zip(CONV, tot[len(SMALL):len(SMALL) + len(CONV)]):
        sh = A[n].shape[axis]
        gsmall[n] = lax.dynamic_slice_in_dim(t, chip * sh, sh, axis)
    loss_out = tot[-1][0]

    out = {}
    plan = (("w_gate_up", [done["gu", 0], done["gu", 1]]), ("w_down", [done["down", 0], done["down", 1]]),
            ("w_out", [done["out", 0], done["out", 1]]), ("w_mem_kv", [done["mem", 0], done["mem", 1]]),
            ("w_in_a", [done["in", 0]]), ("w_in_b", [done["in", 1]]))
    for n, gs in plan:
        shape3 = (len(gs),) + gs[0].shape
        res = adamw_big(A[n].reshape(shape3), A["m_" + n].reshape(shape3), A["v_" + n].reshape(shape3), gs, 0,
                        "adamw_" + n)
        for key, r in zip(("grad_", "delta_", "new_m_", "new_v_"), res):
            out[key + n] = r.reshape(A[n].shape)
    names = SMALL + tuple(n for n, _ in CONV)
    shapes = [A[n].shape for n in names]
    rows = _pad_to(sum(_nrows(s, LANE) for s in shapes), 8)
    packs = [_pack([src[n] for n in names], LANE, rows, f32)
             for src in ({n: A[n] for n in names}, {n: A["m_" + n] for n in names}, {n: A["v_" + n] for n in names}, gsmall)]
    res = adamw_small(*packs)
    for key, r in zip(("delta_", "new_m_", "new_v_"), res):
        for n, a in zip(names, _unpack(r, shapes, LANE)):
            out[key + n] = a
    for n in names:
        out["grad_" + n] = gsmall[n]
    return (loss_out, dx[None], *[out["grad_" + n] for n in WEIGHTS], *[out["delta_" + n] for n in WEIGHTS],
            *[out["new_m_" + n] for n in WEIGHTS], *[out["new_v_" + n] for n in WEIGHTS])
```

```python
import functools
import math

import numpy as np
import jax
import jax.numpy as jnp
from jax import lax
from jax.experimental import pallas as pl
from jax.experimental.pallas import tpu as pltpu

f32 = jnp.float32
bf16 = jnp.bfloat16
HI = lax.Precision.HIGHEST
MESH = pl.DeviceIdType.MESH

D = 1024
MEM_LEN = 256
EPS = 1e-6
A_HEADS, A_KV, A_DH = 12, 2, 64
A_Q = 768
BLK = 128
N_BUCKETS, MAX_DIST = 32, 128
B_QK, B_V, B_DH = 384, 768, 128
B_QKV = 1536
CHUNK = 64
X_Q = 256
D_FF = 2816
LANE = 128
VMEM_LIMIT = 56 * 1024 * 1024
MM_ROWS = 1024

LR, B1, B2, AEPS, WD, STEP = 0.001, 0.9, 0.999, 1e-08, 0.01, 10


def _cp(sem=None):
    return pltpu.CompilerParams(dimension_semantics=sem, vmem_limit_bytes=VMEM_LIMIT)


def _dg(a, b, ca, cb, prec=None):
    return lax.dot_general(a, b, (((ca,), (cb,)), ((), ())), precision=prec, preferred_element_type=f32)


@jax.custom_vjp
def bdot(a, b):
    return _dg(a.astype(bf16), b.astype(bf16), 1, 0)


def _bdot_f(a, b):
    return bdot(a, b), (a, b)


def _bdot_b(res, g):
    a, b = res
    gb = g.astype(bf16)
    return _dg(gb, b.astype(bf16), 1, 1), _dg(a.astype(bf16), gb, 0, 0)


bdot.defvjp(_bdot_f, _bdot_b)


@jax.custom_vjp
def bdot_nt(a, b):
    return _dg(a.astype(bf16), b.astype(bf16), 1, 1)


def _bdot_nt_f(a, b):
    return bdot_nt(a, b), (a, b)


def _bdot_nt_b(res, g):
    a, b = res
    gb = g.astype(bf16)
    return _dg(gb, b.astype(bf16), 1, 0), _dg(gb, a.astype(bf16), 0, 0)


bdot_nt.defvjp(_bdot_nt_f, _bdot_nt_b)


def _shift_rows(x, s, down):
    n = x.shape[0]
    row = lax.broadcasted_iota(jnp.int32, x.shape, 0)
    if down:
        return jnp.where(row >= s, pltpu.roll(x, s, 0), 0.0)
    return jnp.where(row < n - s, pltpu.roll(x, n - s, 0), 0.0)


@functools.partial(jax.custom_vjp, nondiff_argnums=(1,))
def shift_down(x, s):
    return _shift_rows(x, s, True)


def _sd_f(x, s):
    return _shift_rows(x, s, True), None


def _sd_b(s, _, g):
    return (_shift_rows(g, s, False),)


shift_down.defvjp(_sd_f, _sd_b)


def _sigmoid(x):
    return 1.0 / (1.0 + jnp.exp(-x))


def _silu(x):
    return x * _sigmoid(x)


def _rms(x, g):
    return x * lax.rsqrt(jnp.mean(x * x, axis=-1, keepdims=True) + EPS) * g


def _tile(n, cap):
    u = n // LANE
    best = 1
    for d in range(1, u + 1):
        if u % d == 0 and d * LANE <= cap:
            best = d
    return best * LANE


def mm_nn(a, w, res=None, out_dtype=f32, name="mm_nn"):
    M, K = a.shape
    N = w.shape[1]
    tm, tn = min(MM_ROWS, M), _tile(N, 1024)

    def body(*refs):
        if res is None:
            a_ref, w_ref, o_ref = refs
            o_ref[...] = _dg(a_ref[...].astype(bf16), w_ref[...], 1, 0).astype(out_dtype)
        else:
            a_ref, w_ref, r_ref, o_ref = refs
            o_ref[...] = (r_ref[...] + _dg(a_ref[...].astype(bf16), w_ref[...], 1, 0)).astype(out_dtype)

    in_specs = [pl.BlockSpec((tm, K), lambda n, m: (m, 0)), pl.BlockSpec((K, tn), lambda n, m: (0, n))]
    args = [a, w]
    if res is not None:
        in_specs.append(pl.BlockSpec((tm, tn), lambda n, m: (m, n)))
        args.append(res)
    return pl.pallas_call(
        body, name=name, grid=(N // tn, M // tm), in_specs=in_specs,
        out_specs=pl.BlockSpec((tm, tn), lambda n, m: (m, n)),
        out_shape=jax.ShapeDtypeStruct((M, N), out_dtype),
        compiler_params=_cp(("parallel", "parallel")),
    )(*args)


def mm_res_norm(a, w, res, g, name):
    pieces = a if isinstance(a, tuple) else (a,)
    na = len(pieces)
    M, K = pieces[0].shape[0], sum(p.shape[1] for p in pieces)
    tm = min(MM_ROWS, M)

    def body(*refs):
        w_ref, r_ref, g_ref, o_ref, n_ref = refs[na:]
        h = r_ref[...] + _dg(_cols(refs[:na]).astype(bf16), w_ref[...], 1, 0)
        o_ref[...] = h
        n_ref[...] = _rms(h, g_ref[...]).astype(bf16)

    tok = pl.BlockSpec((tm, D), lambda m: (m, 0))
    return pl.pallas_call(
        body, name=name, grid=(M // tm,),
        in_specs=[pl.BlockSpec((tm, p.shape[1]), lambda m: (m, 0)) for p in pieces]
        + [pl.BlockSpec((K, D), lambda m: (0, 0)), tok, pl.BlockSpec((1, D), lambda m: (0, 0))],
        out_specs=[tok, tok],
        out_shape=[jax.ShapeDtypeStruct((M, D), f32), jax.ShapeDtypeStruct((M, D), bf16)],
        compiler_params=_cp(("parallel",)),
    )(*pieces, w, res, g.reshape(1, D))


def mm_nt(dy, w, out_dtype=f32, name="mm_nt"):
    M, N = dy.shape
    K = w.shape[0]
    tm, tn = min(MM_ROWS, M), _tile(N, 1024)
    assert out_dtype == f32 or tn == N

    def body(dy_ref, w_ref, o_ref):
        part = _dg(dy_ref[...].astype(bf16), w_ref[...], 1, 1)
        if tn == N:
            o_ref[...] = part.astype(out_dtype)
        else:
            @pl.when(pl.program_id(1) == 0)
            def _():
                o_ref[...] = jnp.zeros_like(o_ref)
            o_ref[...] += part

    return pl.pallas_call(
        body, name=name, grid=(M // tm, N // tn),
        in_specs=[pl.BlockSpec((tm, tn), lambda m, n: (m, n)), pl.BlockSpec((K, tn), lambda m, n: (0, n))],
        out_specs=pl.BlockSpec((tm, K), lambda m, n: (m, 0)),
        out_shape=jax.ShapeDtypeStruct((M, K), out_dtype),
        compiler_params=_cp(("parallel", "arbitrary")),
    )(dy, w)


NORM_ROWS = 1024


def _acc_then_norm_bwd(part, steps, h_ref, g_ref, r_ref, o_ref, dg_ref):
    k = pl.program_id(1)

    @pl.when((pl.program_id(0) == 0) & (k == 0))
    def _():
        dg_ref[...] = jnp.zeros_like(dg_ref)

    @pl.when(k == 0)
    def _():
        o_ref[...] = part

    @pl.when(k > 0)
    def _():
        o_ref[...] += part

    @pl.when(k == steps - 1)
    def _():
        _, vjp = jax.vjp(_rms, h_ref[...], g_ref[...])
        dh, dg = vjp(o_ref[...])
        o_ref[...] = r_ref[...] + dh
        dg_ref[...] += dg


def mm_nt_norm(dy, w, norm, name):
    pieces = dy if isinstance(dy, tuple) else (dy,)
    nd = len(pieces)
    M, N = pieces[0].shape[0], sum(p.shape[1] for p in pieces)
    tm, tn = (min(NORM_ROWS, M), _tile(N, 1024)) if nd == 1 else (min(512, M), N)

    def body(*refs):
        w_ref, h_ref, g_ref, r_ref, o_ref, dg_ref = refs[nd:]
        _acc_then_norm_bwd(_dg(_cols(refs[:nd]).astype(bf16), w_ref[...], 1, 1), N // tn, h_ref, g_ref, r_ref, o_ref,
                           dg_ref)

    tok = pl.BlockSpec((tm, D), lambda m, n: (m, 0))
    vec = pl.BlockSpec((1, D), lambda m, n: (0, 0))
    return pl.pallas_call(
        body, name=name, grid=(M // tm, N // tn),
        in_specs=[pl.BlockSpec((tm, tn if nd == 1 else p.shape[1]), lambda m, n: (m, n)) for p in pieces]
        + [pl.BlockSpec((D, tn), lambda m, n: (0, n)), tok, vec, tok],
        out_specs=[tok, vec],
        out_shape=[jax.ShapeDtypeStruct((M, D), f32), jax.ShapeDtypeStruct((1, D), f32)],
        compiler_params=_cp(("arbitrary", "arbitrary")),
    )(*pieces, w, norm[0], norm[1].reshape(1, D), norm[2])


def _cols(refs):
    return refs[0][...] if len(refs) == 1 else jnp.concatenate([r[...] for r in refs], axis=1)


def mm_tn(a, dy, name="mm_tn"):
    pieces = a if isinstance(a, tuple) else (a,)
    dpieces = dy if isinstance(dy, tuple) else (dy,)
    na, nd = len(pieces), len(dpieces)
    M, K = pieces[0].shape[0], sum(p.shape[1] for p in pieces)
    N = sum(p.shape[1] for p in dpieces)
    tm = min(MM_ROWS, M)
    tk = _tile(K, 1408) if na == 1 else K
    tn = _tile(N, 1024) if nd == 1 else N

    def body(*refs):
        o_ref = refs[-1]

        @pl.when(pl.program_id(2) == 0)
        def _():
            o_ref[...] = jnp.zeros_like(o_ref)
        o_ref[...] += _dg(_cols(refs[:na]).astype(bf16), _cols(refs[na:na + nd]).astype(bf16), 0, 0)

    a_specs = [pl.BlockSpec((tm, tk if na == 1 else p.shape[1]), lambda k, n, m: (m, k)) for p in pieces]
    d_specs = [pl.BlockSpec((tm, tn if nd == 1 else p.shape[1]), lambda k, n, m: (m, n)) for p in dpieces]
    return pl.pallas_call(
        body, name=name, grid=(K // tk, N // tn, M // tm), in_specs=a_specs + d_specs,
        out_specs=pl.BlockSpec((tk, tn), lambda k, n, m: (k, n)),
        out_shape=jax.ShapeDtypeStruct((K, N), f32),
        compiler_params=_cp(("parallel", "parallel", "arbitrary")),
    )(*pieces, *dpieces)


def rms_fwd(h, g, name):
    S = h.shape[0]
    t = min(512, S)

    def body(h_ref, g_ref, o_ref):
        o_ref[...] = _rms(h_ref[...], g_ref[...]).astype(bf16)

    return pl.pallas_call(
        body, name=name, grid=(S // t,),
        in_specs=[pl.BlockSpec((t, D), lambda i: (i, 0)), pl.BlockSpec((1, D), lambda i: (0, 0))],
        out_specs=pl.BlockSpec((t, D), lambda i: (i, 0)),
        out_shape=jax.ShapeDtypeStruct((S, D), bf16),
        compiler_params=_cp(("parallel",)),
    )(h, g.reshape(1, D))


def loss_head(h, g, target):
    S = h.shape[0]
    t = min(512, S)

    def f(hh, gg, tt):
        err = _rms(hh, gg) - tt
        return 0.5 * jnp.sum(jnp.mean(err * err, axis=-1, keepdims=True), axis=0, keepdims=True)

    def body(h_ref, g_ref, t_ref, loss_ref, dh_ref, dg_ref):
        @pl.when(pl.program_id(0) == 0)
        def _():
            dg_ref[...] = jnp.zeros_like(dg_ref)
            loss_ref[...] = jnp.zeros_like(loss_ref)
        val, vjp = jax.vjp(lambda a, b: f(a, b, t_ref[...]), h_ref[...], g_ref[...])
        dh, dg = vjp(jnp.ones((1, 1), f32))
        dh_ref[...] = dh
        dg_ref[...] += dg
        loss_ref[...] += jnp.broadcast_to(val, loss_ref.shape)

    tok = pl.BlockSpec((t, D), lambda i: (i, 0))
    vec = pl.BlockSpec((1, D), lambda i: (0, 0))
    return pl.pallas_call(
        body, name="loss_head", grid=(S // t,), in_specs=[tok, vec, tok],
        out_specs=[pl.BlockSpec((1, LANE), lambda i: (0, 0)), tok, vec],
        out_shape=[jax.ShapeDtypeStruct((1, LANE), f32), jax.ShapeDtypeStruct((S, D), f32),
                   jax.ShapeDtypeStruct((1, D), f32)],
        compiler_params=_cp(("arbitrary",)),
    )(h, g.reshape(1, D), target)


def memkv_fwd(mem, g, w, name):
    def body(m_ref, g_ref, w_ref, o_ref):
        o_ref[...] = _dg(_rms(m_ref[...], g_ref[...]).astype(bf16), w_ref[...], 1, 0)

    return pl.pallas_call(
        body, name=name, out_shape=jax.ShapeDtypeStruct((MEM_LEN, 2 * X_Q), f32), compiler_params=_cp(),
    )(mem, g.reshape(1, D), w)


def memkv_bwd(mem, g, w, dkv, name):
    def body(m_ref, g_ref, w_ref, d_ref, dg_ref, dw_ref):
        n, vjp = jax.vjp(lambda gg: _rms(m_ref[...], gg), g_ref[...])
        db = d_ref[...].astype(bf16)
        dw_ref[...] = _dg(n.astype(bf16), db, 0, 0)
        dg_ref[...] = vjp(_dg(db, w_ref[...], 1, 1))[0]

    return pl.pallas_call(
        body, name=name,
        out_shape=[jax.ShapeDtypeStruct((1, D), f32), jax.ShapeDtypeStruct((D, 2 * X_Q), f32)],
        compiler_params=_cp(),
    )(mem, g.reshape(1, D), w, dkv)


def _xattn_f(xq, mk, mv):
    lane = lax.broadcasted_iota(jnp.int32, (1, X_Q), 1)
    out = jnp.zeros(xq.shape, f32)
    for hd in range(4):
        msk = (lane // 64 == hd).astype(f32)
        s = bdot_nt(xq * msk, mk) * (64 ** -0.5)
        m = lax.stop_gradient(jnp.max(s, axis=-1, keepdims=True))
        p = jnp.exp(s - m)
        p = p / jnp.sum(p, axis=-1, keepdims=True)
        out = out + bdot(p, mv * msk)
    return out


def xattn_fwd(proj, col, kv, name):
    S = proj.shape[0]
    t = min(512, S)
    cb = col // X_Q

    def body(q_ref, k_ref, v_ref, o_ref):
        o_ref[...] = _xattn_f(q_ref[...].astype(f32), k_ref[...], v_ref[...]).astype(bf16)

    return pl.pallas_call(
        body, name=name, grid=(S // t,),
        in_specs=[pl.BlockSpec((t, X_Q), lambda i: (i, cb)), pl.BlockSpec((MEM_LEN, X_Q), lambda i: (0, 0)),
                  pl.BlockSpec((MEM_LEN, X_Q), lambda i: (0, 1))],
        out_specs=pl.BlockSpec((t, X_Q), lambda i: (i, 0)),
        out_shape=jax.ShapeDtypeStruct((S, X_Q), bf16),
        compiler_params=_cp(("parallel",)),
    )(proj, kv, kv)


def xattn_bwd(proj, col, kv, dmix, name):
    S = proj.shape[0]
    t = min(512, S)
    cb = col // X_Q

    def body(q_ref, k_ref, v_ref, do_ref, dq_ref, dk_ref, dv_ref):
        @pl.when(pl.program_id(0) == 0)
        def _():
            dk_ref[...] = jnp.zeros_like(dk_ref)
            dv_ref[...] = jnp.zeros_like(dv_ref)
        _, vjp = jax.vjp(_xattn_f, q_ref[...].astype(f32), k_ref[...], v_ref[...])
        dq, dk, dv = vjp(do_ref[...])
        dq_ref[...] = dq.astype(bf16)
        dk_ref[...] += dk
        dv_ref[...] += dv

    kvb = pl.BlockSpec((MEM_LEN, X_Q), lambda i: (0, 0))
    dq, dk, dv = pl.pallas_call(
        body, name=name, grid=(S // t,),
        in_specs=[pl.BlockSpec((t, X_Q), lambda i: (i, cb)), kvb,
                  pl.BlockSpec((MEM_LEN, X_Q), lambda i: (0, 1)), pl.BlockSpec((t, X_Q), lambda i: (i, 3))],
        out_specs=[pl.BlockSpec((t, X_Q), lambda i: (i, 0)), kvb, kvb],
        out_shape=[jax.ShapeDtypeStruct((S, X_Q), bf16), jax.ShapeDtypeStruct((MEM_LEN, X_Q), f32),
                   jax.ShapeDtypeStruct((MEM_LEN, X_Q), f32)],
        compiler_params=_cp(("arbitrary",)),
    )(proj, kv, kv, dmix)
    return dq, jnp.concatenate([dk, dv], axis=1)


def _bucket_map():
    qi = np.arange(BLK)[:, None]
    kj = np.arange(2 * BLK)[None, :]
    n = np.maximum(BLK + qi - kj, 0)
    max_exact = N_BUCKETS // 2
    nf = np.maximum(n, 1).astype(np.float64)
    large = max_exact + (np.log(nf / max_exact) / math.log(MAX_DIST / max_exact)
                         * (N_BUCKETS - max_exact)).astype(np.int32)
    large = np.minimum(large, N_BUCKETS - 1)
    return np.where(n < max_exact, n, large).astype(np.int32)


def bias_build(rel_bias):
    def body(rb_ref, bk_ref, o_ref):
        bk = bk_ref[...]
        for h in range(A_HEADS):
            acc = jnp.zeros((BLK, 2 * BLK), f32)
            for b in range(N_BUCKETS):
                acc = jnp.where(bk == b, rb_ref[b, h], acc)
            o_ref[h] = acc

    return pl.pallas_call(
        body, name="bias_build",
        in_specs=[pl.BlockSpec(memory_space=pltpu.SMEM), pl.BlockSpec(memory_space=pltpu.VMEM)],
        out_specs=pl.BlockSpec(memory_space=pltpu.VMEM),
        out_shape=jax.ShapeDtypeStruct((A_HEADS, BLK, 2 * BLK), f32), compiler_params=_cp(),
    )(rel_bias, jnp.asarray(_bucket_map()))


def bias_grad(dbias):
    def body(d_ref, bk_ref, o_ref):
        bk = bk_ref[...]
        row = lax.broadcasted_iota(jnp.int32, (N_BUCKETS, LANE), 0)
        lane = lax.broadcasted_iota(jnp.int32, (N_BUCKETS, LANE), 1)
        acc = jnp.zeros((N_BUCKETS, LANE), f32)
        for h in range(A_HEADS):
            d = d_ref[h]
            for b in range(N_BUCKETS):
                s = jnp.sum(jnp.where(bk == b, d, 0.0), keepdims=True)
                acc = acc + jnp.where((row == b) & (lane == h), s, 0.0)
        o_ref[...] = acc

    return pl.pallas_call(
        body, name="bias_grad", out_shape=jax.ShapeDtypeStruct((N_BUCKETS, LANE), f32), compiler_params=_cp(),
    )(dbias, jnp.asarray(_bucket_map()))


def _swa_f(qb, kp, kc, vp, vc, bias, sk, first):
    kband = jnp.concatenate([kp, kc], axis=0)
    vband = jnp.concatenate([vp, vc], axis=0)
    qi = lax.broadcasted_iota(jnp.int32, (BLK, 2 * BLK), 0)
    kj = lax.broadcasted_iota(jnp.int32, (BLK, 2 * BLK), 1)
    rel = kj - qi
    ok = (rel >= 1) & (rel <= BLK) & ((kj >= BLK) | jnp.logical_not(first))
    lane = lax.broadcasted_iota(jnp.int32, (1, LANE), 1)
    lane_b = lax.broadcasted_iota(jnp.int32, (BLK, LANE), 1)
    outs = []
    for p in range(A_HEADS // 2):
        qp = qb[:, LANE * p:LANE * (p + 1)]
        acc = jnp.zeros((BLK, LANE), f32)
        for g in range(2):
            h = g * (A_HEADS // 2) + p
            msk = (lane // A_DH == g).astype(f32)
            s = bdot_nt(qp * msk, kband) * (A_DH ** -0.5) + bias[h]
            s = jnp.where(ok, s, -1e30)
            skb = jnp.broadcast_to(sk[h:h + 1, :], (BLK, LANE))
            sink = jnp.sum(jnp.where(lane_b == 0, skb, 0.0), axis=-1, keepdims=True)
            m = lax.stop_gradient(jnp.maximum(jnp.max(s, axis=-1, keepdims=True), sink))
            e = jnp.exp(s - m)
            prob = e / (jnp.sum(e, axis=-1, keepdims=True) + jnp.exp(sink - m))
            acc = acc + bdot(prob, vband) * msk
        outs.append(acc)
    return jnp.concatenate(outs, axis=1)


def _swa_specs(nb, rev):
    bi = (lambda i: nb - 1 - i) if rev else (lambda i: i)
    return [
        pl.BlockSpec((BLK, A_Q), lambda i: (bi(i), 0)),
        pl.BlockSpec((BLK, LANE), lambda i: (jnp.maximum(bi(i) - 1, 0), 6)),
        pl.BlockSpec((BLK, LANE), lambda i: (bi(i), 6)),
        pl.BlockSpec((BLK, LANE), lambda i: (jnp.maximum(bi(i) - 1, 0), 7)),
        pl.BlockSpec((BLK, LANE), lambda i: (bi(i), 7)),
        pl.BlockSpec((A_HEADS, BLK, 2 * BLK), lambda i: (0, 0, 0)),
        pl.BlockSpec((16, LANE), lambda i: (0, 0)),
    ]


def swa_fwd(proj, bias, sk):
    S = proj.shape[0]
    nb = S // BLK

    def body(q_ref, kp_ref, kc_ref, vp_ref, vc_ref, b_ref, s_ref, o_ref):
        qkv = [r[...].astype(f32) for r in (q_ref, kp_ref, kc_ref, vp_ref, vc_ref)]
        o_ref[...] = _swa_f(*qkv, b_ref[...], s_ref[...], pl.program_id(0) == 0).astype(bf16)

    return pl.pallas_call(
        body, name="swa_fwd", grid=(nb,), in_specs=_swa_specs(nb, False),
        out_specs=pl.BlockSpec((BLK, A_Q), lambda i: (i, 0)),
        out_shape=jax.ShapeDtypeStruct((S, A_Q), bf16), compiler_params=_cp(("parallel",)),
    )(proj, proj, proj, proj, proj, bias, sk)


def swa_bwd(proj, bias, sk, dmix):
    S = proj.shape[0]
    nb = S // BLK

    def body(q_ref, kp_ref, kc_ref, vp_ref, vc_ref, b_ref, s_ref, do_ref, dqkv_ref, db_ref, ds_ref, ck, cv):
        i = pl.program_id(0)

        @pl.when(i == 0)
        def _():
            db_ref[...] = jnp.zeros_like(db_ref)
            ds_ref[...] = jnp.zeros_like(ds_ref)
            ck[...] = jnp.zeros_like(ck)
            cv[...] = jnp.zeros_like(cv)
        first = i == nb - 1
        qkv = [r[...].astype(f32) for r in (q_ref, kp_ref, kc_ref, vp_ref, vc_ref)]
        _, vjp = jax.vjp(lambda *a: _swa_f(*a, first), *qkv, b_ref[...], s_ref[...])
        dq, dkp, dkc, dvp, dvc, db, ds = vjp(do_ref[...])
        dqkv_ref[...] = jnp.concatenate([dq, dkc + ck[...], dvc + cv[...]], axis=1).astype(bf16)
        ck[...] = dkp
        cv[...] = dvp
        db_ref[...] += db
        ds_ref[...] += ds

    return pl.pallas_call(
        body, name="swa_bwd", grid=(nb,),
        in_specs=_swa_specs(nb, True) + [pl.BlockSpec((BLK, A_Q), lambda i: (nb - 1 - i, 0))],
        out_specs=[pl.BlockSpec((BLK, D), lambda i: (nb - 1 - i, 0)),
                   pl.BlockSpec((A_HEADS, BLK, 2 * BLK), lambda i: (0, 0, 0)),
                   pl.BlockSpec((16, LANE), lambda i: (0, 0))],
        out_shape=[jax.ShapeDtypeStruct((S, D), bf16), jax.ShapeDtypeStruct((A_HEADS, BLK, 2 * BLK), f32),
                   jax.ShapeDtypeStruct((16, LANE), f32)],
        scratch_shapes=[pltpu.VMEM((BLK, LANE), f32), pltpu.VMEM((BLK, LANE), f32)],
        compiler_params=_cp(("arbitrary",)),
    )(proj, proj, proj, proj, proj, bias, sk, dmix)


def _dnprep_f(xext, w, is_qk):
    c = (w[3:4] * xext + w[2:3] * shift_down(xext, 1) + w[1:2] * shift_down(xext, 2) + w[0:1] * shift_down(xext, 3))
    a = _silu(c)[HALO:]
    n = a * lax.rsqrt(jnp.sum(a * a, axis=-1, keepdims=True) + EPS)
    return jnp.where(is_qk, n, a)


def dnprep_fwd(proj, cw):
    S = proj.shape[0]
    nblk = B_QKV // LANE
    T = S

    def body(x_ref, w_ref, o_ref):
        is_qk = pl.program_id(0) < 2 * B_QK // LANE
        wv = w_ref[...]

        def tile(r0, first):
            o_ref[pl.ds(r0, T), :] = _dnprep_f(_glu_gext(x_ref, r0, first, T), wv, is_qk)

        tile(0, True)

    return pl.pallas_call(
        body, name="dnprep_fwd", grid=(nblk,),
        in_specs=[pl.BlockSpec((S, LANE), lambda j: (0, j)), pl.BlockSpec((4, LANE), lambda j: (0, j))],
        out_specs=pl.BlockSpec((S, LANE), lambda j: (0, j)),
        out_shape=jax.ShapeDtypeStruct((S, B_QKV), f32), compiler_params=_cp(("parallel",)),
    )(proj, cw)


def dnprep_bwd(proj, cw, dqkvn):
    S = proj.shape[0]
    nblk = B_QKV // LANE

    T = S

    def body(x_ref, w_ref, d_ref, dx_ref, dw_ref):
        is_qk = pl.program_id(0) < 2 * B_QK // LANE
        wv = w_ref[...]

        def tile(r0, first):
            _, vjp = jax.vjp(lambda a, b: _dnprep_f(a, b, is_qk), _glu_gext(x_ref, r0, first, T), wv)
            dx, dw = vjp(d_ref[pl.ds(r0, T), :])
            dx_ref[pl.ds(r0, T), :] = dx[HALO:].astype(bf16)
            if not first:
                dx_ref[pl.ds(r0 - HALO, HALO), :] += dx[:HALO]
            return dw

        dw_ref[...] = tile(0, True)

    col = pl.BlockSpec((S, LANE), lambda j: (0, j))
    wsp = pl.BlockSpec((4, LANE), lambda j: (0, j))
    return pl.pallas_call(
        body, name="dnprep_bwd", grid=(nblk,), in_specs=[col, wsp, col], out_specs=[col, wsp],
        out_shape=[jax.ShapeDtypeStruct((S, B_QKV), bf16), jax.ShapeDtypeStruct((4, B_QKV), f32)],
        compiler_params=_cp(("parallel",)),
    )(proj, cw, dqkvn)


def _hdot(a, b, ca=1, cb=0):
    return _dg(a, b, ca, cb, HI)


def _bdg(a, b, ca, cb):
    dn = (((ca,), (cb,)), ((0,), (0,)))
    ah, bh = a.astype(bf16), b.astype(bf16)
    al, bl = (a - ah.astype(f32)).astype(bf16), (b - bh.astype(f32)).astype(bf16)
    return (lax.dot_general(ah, bh, dn, preferred_element_type=f32)
            + lax.dot_general(ah, bl, dn, preferred_element_type=f32)
            + lax.dot_general(al, bh, dn, preferred_element_type=f32))


@jax.custom_vjp
def hbd(a, b):
    return _bdg(a, b, 2, 1)


@jax.custom_vjp
def hbd_nt(a, b):
    return _bdg(a, b, 2, 2)


@jax.custom_vjp
def hbd_tn(a, b):
    return _bdg(a, b, 1, 1)


hbd.defvjp(lambda a, b: (hbd(a, b), (a, b)), lambda r, g: (hbd_nt(g, r[1]), hbd_tn(r[0], g)))
hbd_nt.defvjp(lambda a, b: (hbd_nt(a, b), (a, b)), lambda r, g: (hbd(g, r[1]), hbd_tn(g, r[0])))
hbd_tn.defvjp(lambda a, b: (hbd_tn(a, b), (a, b)), lambda r, g: (hbd_nt(r[1], g), hbd(r[0], g)))


def _stack(xs):
    return jnp.concatenate([x[None] for x in xs], axis=0)


def _lane_col(x, j):
    lane = lax.broadcasted_iota(jnp.int32, (1, LANE), 1)
    return jnp.sum(jnp.where(lane == j, x, 0.0), axis=-1, keepdims=True)


def _tri_inv(a_mat):
    r = lax.broadcasted_iota(jnp.int32, (1, CHUNK, CHUNK), 1)
    c = lax.broadcasted_iota(jnp.int32, (1, CHUNK, CHUNK), 2)
    pw = -a_mat
    inv = (r == c).astype(f32) + pw
    for _ in range(5):
        pw = hbd(pw, pw)
        inv = inv + hbd(inv, pw)
    return inv


@jax.custom_vjp
def _tri_inv_known(a_mat, inv):
    return inv


_tri_inv_known.defvjp(lambda a, inv: (inv, inv),
                      lambda inv, g: (-hbd_tn(inv, hbd_nt(g, inv)), jnp.zeros_like(inv)))


def _dnc_f(q, k, v, seg, prm, inverse=_tri_inv):
    C = CHUNK
    B = q.shape[0]
    rows = seg.shape[0]
    beta_all = _sigmoid(seg)
    xx = seg + prm[1:2]
    g_all = -jnp.exp(prm[0:1]) * (jnp.maximum(xx, 0.0) + jnp.log(1.0 + jnp.exp(-jnp.abs(xx))))
    r2 = lax.broadcasted_iota(jnp.int32, (rows, rows), 0)
    c2 = lax.broadcasted_iota(jnp.int32, (rows, rows), 1)
    within = (r2 >= c2) & (r2 // C == c2 // C)
    gc_all = _hdot(within.astype(f32), g_all)
    beta = _stack([_lane_col(beta_all[C * j:C * (j + 1)], h) for j in range(rows // C) for h in range(6)])
    gc = _stack([_lane_col(gc_all[C * j:C * (j + 1)], 6 + h) for j in range(rows // C) for h in range(6)])
    r = lax.broadcasted_iota(jnp.int32, (1, C, C), 1)
    c = lax.broadcasted_iota(jnp.int32, (1, C, C), 2)
    incl = r >= c
    strict = r > c
    gct = [gc_all[C * j:C * (j + 1)].T for j in range(rows // C)]
    g_row = _stack([jnp.broadcast_to(gct[j][6 + h:7 + h, :], (C, C))
                    for j in range(rows // C) for h in range(6)])
    decay = jnp.where(incl, jnp.exp(jnp.where(incl, gc - g_row, 0.0)), 0.0)
    a_mat = beta * sbd_nt(k, k) * jnp.where(strict, decay, 0.0)
    eg = jnp.exp(gc)
    inv = inverse(a_mat)
    u = hbd(inv, beta * v)
    w = hbd(inv, (beta * eg) * k)
    qc = q * (B_DH ** -0.5)
    attn = sbd_nt(qc, k) * decay
    last = (lax.broadcasted_iota(jnp.int32, (1, C, 1), 1) == C - 1).astype(f32)
    g_last = jnp.sum(gc * last, axis=1, keepdims=True)
    dc = jnp.broadcast_to(jnp.exp(g_last), (B, 1, LANE)).reshape(B, LANE)
    return u, w, qc * eg, k * jnp.exp(g_last - gc), attn, dc, inv


def _b1(a, b, ca, cb):
    return lax.dot_general(a.astype(bf16), b.astype(bf16), (((ca,), (cb,)), ((0,), (0,))), preferred_element_type=f32)


@jax.custom_vjp
def sbd(a, b):
    return _b1(a, b, 2, 1)


@jax.custom_vjp
def sbd_nt(a, b):
    return _b1(a, b, 2, 2)


@jax.custom_vjp
def sbd_tn(a, b):
    return _b1(a, b, 1, 1)


sbd.defvjp(lambda a, b: (sbd(a, b), (a, b)), lambda r, g: (sbd_nt(g, r[1]), sbd_tn(r[0], g)))
sbd_nt.defvjp(lambda a, b: (sbd_nt(a, b), (a, b)), lambda r, g: (sbd(g, r[1]), sbd_tn(g, r[0])))
sbd_tn.defvjp(lambda a, b: (sbd_tn(a, b), (a, b)), lambda r, g: (sbd_nt(r[1], g), sbd(r[0], g)))


def _dns_f(S0, u, w, qd, kt, attn, dcrows):
    dc = _lane_col(dcrows, 0).reshape(6, 1, 1)
    delta = u - sbd(w, S0)
    out = sbd(qd, S0) + sbd(attn, delta)
    return out, dc * S0 + sbd_tn(kt, delta)


def _dnpost_f(o, z, grow):
    outs = []
    for h in range(6):
        oh = o[:, LANE * h:LANE * (h + 1)]
        outs.append(oh * lax.rsqrt(jnp.mean(oh * oh, axis=-1, keepdims=True) + EPS) * grow
                    * _silu(z[:, LANE * h:LANE * (h + 1)]))
    return jnp.concatenate(outs, axis=1)


def _hs(h):
    return slice(LANE * h, LANE * (h + 1))


DN_CHUNKS = 4


def _heads(ref, share):
    return _stack([ref[CHUNK * j:CHUNK * (j + 1), _hs(h // share)]
                   for j in range(ref.shape[0] // CHUNK) for h in range(6)])


def _put_heads(ref, val):
    for j in range(ref.shape[0] // CHUNK):
        for h in range(6):
            ref[CHUNK * j:CHUNK * (j + 1), _hs(h)] = val[6 * j + h].astype(ref.dtype)


def _dnc_in_specs():
    rows = CHUNK * DN_CHUNKS
    return [
        pl.BlockSpec((rows, B_QK), lambda n: (n, 0)),
        pl.BlockSpec((rows, B_QK), lambda n: (n, 1)),
        pl.BlockSpec((rows, B_V), lambda n: (n, 1)),
        pl.BlockSpec((rows, LANE), lambda n: (n, 20)),
        pl.BlockSpec((8, LANE), lambda n: (0, 0)),
    ]


def _dnc_out_specs(rev_nc=None, chunks=1):
    ci = (lambda n: n) if rev_nc is None else (lambda n: rev_nc - 1 - n)
    wide = pl.BlockSpec((CHUNK * chunks, B_V), lambda n: (ci(n), 0))
    return [wide, wide, wide, wide, pl.BlockSpec((chunks, 6, CHUNK, CHUNK), lambda n: (ci(n), 0, 0, 0)),
            pl.BlockSpec((chunks, 8, LANE), lambda n: (ci(n), 0, 0))]


def _dc_rows(dc):
    pad = jnp.zeros((2, LANE), f32)
    return _stack([jnp.concatenate([dc[6 * j:6 * (j + 1)], pad], axis=0) for j in range(dc.shape[0] // 6)])


def _dnc_shapes(S, mm=f32):
    nc = S // CHUNK
    return [jax.ShapeDtypeStruct((S, B_V), f32)] + [jax.ShapeDtypeStruct((S, B_V), mm)] * 3 + [
        jax.ShapeDtypeStruct((nc, 6, CHUNK, CHUNK), mm), jax.ShapeDtypeStruct((nc, 8, LANE), f32)]


def dnc_fwd(qkvn, proj, prm):
    S = proj.shape[0]

    def body(q_ref, k_ref, v_ref, s_ref, p_ref, u_ref, w_ref, qd_ref, kt_ref, at_ref, dc_ref, inv_ref):
        u, w, qd, kt, attn, dc, inv = _dnc_f(_heads(q_ref, 2), _heads(k_ref, 2), _heads(v_ref, 1), s_ref[...],
                                             p_ref[...])
        inv_ref[...] = inv.reshape(inv_ref.shape)
        _put_heads(u_ref, u)
        _put_heads(w_ref, w)
        _put_heads(qd_ref, qd)
        _put_heads(kt_ref, kt)
        at_ref[...] = attn.reshape(at_ref.shape).astype(at_ref.dtype)
        dc_ref[...] = _dc_rows(dc)

    outs = _dnc_out_specs(chunks=DN_CHUNKS)
    out = pl.pallas_call(
        body, name="dn_chunk_fwd", grid=(S // (CHUNK * DN_CHUNKS),), in_specs=_dnc_in_specs(),
        out_specs=outs + [outs[4]], out_shape=_dnc_shapes(S, bf16) + [_dnc_shapes(S)[4]],
        compiler_params=_cp(("parallel",)),
    )(qkvn, qkvn, qkvn, proj, prm)
    return out[:6], out[6]


def dnc_bwd(qkvn, proj, prm, inv, cots):
    S = proj.shape[0]

    def body(q_ref, k_ref, v_ref, s_ref, p_ref, inv_ref, du_ref, dw_ref, dqd_ref, dkt_ref, dat_ref, ddc_ref,
             dx_ref, dseg_ref, dprm_ref):
        @pl.when(pl.program_id(0) == 0)
        def _():
            dprm_ref[...] = jnp.zeros_like(dprm_ref)
        nb = 6 * DN_CHUNKS
        known = functools.partial(_tri_inv_known, inv=inv_ref[...].reshape(nb, CHUNK, CHUNK))
        _, vjp = jax.vjp(lambda *a: _dnc_f(*a, inverse=known)[:6], _heads(q_ref, 2), _heads(k_ref, 2),
                         _heads(v_ref, 1), s_ref[...], p_ref[...])
        ddc = jnp.concatenate([ddc_ref[j, 0:6, :] for j in range(DN_CHUNKS)], axis=0)
        dq, dk, dv, dseg, dprm = vjp((_heads(du_ref, 1), _heads(dw_ref, 1), _heads(dqd_ref, 1), _heads(dkt_ref, 1),
                                      dat_ref[...].reshape(nb, CHUNK, CHUNK), ddc))
        for j in range(DN_CHUNKS):
            o = 6 * j
            dx_ref[CHUNK * j:CHUNK * (j + 1), :] = jnp.concatenate(
                [dq[o] + dq[o + 1], dq[o + 2] + dq[o + 3], dq[o + 4] + dq[o + 5],
                 dk[o] + dk[o + 1], dk[o + 2] + dk[o + 3], dk[o + 4] + dk[o + 5]] + [dv[o + h] for h in range(6)], axis=1)
        dseg_ref[...] = dseg.astype(bf16)
        dprm_ref[...] += dprm

    rows = CHUNK * DN_CHUNKS
    outs = _dnc_out_specs(chunks=DN_CHUNKS)
    return pl.pallas_call(
        body, name="dn_chunk_bwd", grid=(S // rows,),
        in_specs=_dnc_in_specs() + [outs[4]] + outs,
        out_specs=[pl.BlockSpec((rows, B_QKV), lambda n: (n, 0)), pl.BlockSpec((rows, LANE), lambda n: (n, 0)),
                   pl.BlockSpec((8, LANE), lambda n: (0, 0))],
        out_shape=[jax.ShapeDtypeStruct((S, B_QKV), f32), jax.ShapeDtypeStruct((S, LANE), bf16),
                   jax.ShapeDtypeStruct((8, LANE), f32)],
        compiler_params=_cp(("arbitrary",)),
    )(qkvn, qkvn, qkvn, proj, prm, inv, *cots)


def dns_fwd(chunked):
    u = chunked[0]
    S = u.shape[0]
    nc = S // CHUNK

    def body(u_ref, w_ref, qd_ref, kt_ref, at_ref, dc_ref, o_ref, st_ref, st):
        @pl.when(pl.program_id(0) == 0)
        def _():
            st[...] = jnp.zeros_like(st)
        S0 = st[...]
        st_ref[0] = S0
        out, S1 = _dns_f(S0, _heads(u_ref, 1), _heads(w_ref, 1), _heads(qd_ref, 1), _heads(kt_ref, 1),
                         at_ref[0], dc_ref[0, 0:6, :])
        _put_heads(o_ref, out)
        st[...] = S1

    return pl.pallas_call(
        body, name="dn_scan_fwd", grid=(nc,), in_specs=_dnc_out_specs(),
        out_specs=[pl.BlockSpec((CHUNK, B_V), lambda n: (n, 0)),
                   pl.BlockSpec((1, 6, B_DH, B_DH), lambda n: (n, 0, 0, 0))],
        out_shape=[jax.ShapeDtypeStruct((S, B_V), f32), jax.ShapeDtypeStruct((nc, 6, B_DH, B_DH), f32)],
        scratch_shapes=[pltpu.VMEM((6, B_DH, B_DH), f32)],
        compiler_params=_cp(("arbitrary",)),
    )(*chunked)


def dns_bwd(chunked, states, do):
    S = do.shape[0]
    nc = S // CHUNK

    def body(u_ref, w_ref, qd_ref, kt_ref, at_ref, dc_ref, st_ref, do_ref,
             du_ref, dw_ref, dqd_ref, dkt_ref, dat_ref, ddc_ref, dst):
        @pl.when(pl.program_id(0) == 0)
        def _():
            dst[...] = jnp.zeros_like(dst)
        _, vjp = jax.vjp(_dns_f, st_ref[0], _heads(u_ref, 1), _heads(w_ref, 1).astype(f32),
                         _heads(qd_ref, 1).astype(f32), _heads(kt_ref, 1).astype(f32), at_ref[0].astype(f32),
                         dc_ref[0, 0:6, :])
        dS0, du, dw, dqd, dkt, dat, ddc = vjp((_heads(do_ref, 1), dst[...]))
        dst[...] = dS0
        _put_heads(du_ref, du)
        _put_heads(dw_ref, dw)
        _put_heads(dqd_ref, dqd)
        _put_heads(dkt_ref, dkt)
        dat_ref[0] = dat
        ddc_ref[0] = jnp.concatenate([ddc, jnp.zeros((2, LANE), f32)], axis=0)

    return pl.pallas_call(
        body, name="dn_scan_bwd", grid=(nc,),
        in_specs=_dnc_out_specs(nc) + [pl.BlockSpec((1, 6, B_DH, B_DH), lambda n: (nc - 1 - n, 0, 0, 0)),
                                       pl.BlockSpec((CHUNK, B_V), lambda n: (nc - 1 - n, 0))],
        out_specs=_dnc_out_specs(nc), out_shape=_dnc_shapes(S),
        scratch_shapes=[pltpu.VMEM((6, B_DH, B_DH), f32)],
        compiler_params=_cp(("arbitrary",)),
    )(*chunked, states, do)


def dnpost_fwd(o, proj, prm):
    S = o.shape[0]
    t = min(512, S)

    def body(o_ref, z_ref, p_ref, y_ref):
        y_ref[...] = _dnpost_f(o_ref[...], z_ref[...], p_ref[2:3, :]).astype(bf16)

    tok = pl.BlockSpec((t, B_V), lambda i: (i, 0))
    return pl.pallas_call(
        body, name="dn_post_fwd", grid=(S // t,),
        in_specs=[tok, pl.BlockSpec((t, B_V), lambda i: (i, 2)), pl.BlockSpec((8, LANE), lambda i: (0, 0))],
        out_specs=tok, out_shape=jax.ShapeDtypeStruct((S, B_V), bf16), compiler_params=_cp(("parallel",)),
    )(o, proj, prm)


def dnpost_bwd(o, proj, prm, dmix):
    S = o.shape[0]
    t = min(512, S)

    def body(o_ref, z_ref, p_ref, dy_ref, do_ref, dz_ref, dg_ref):
        @pl.when(pl.program_id(0) == 0)
        def _():
            dg_ref[...] = jnp.zeros_like(dg_ref)
        _, vjp = jax.vjp(_dnpost_f, o_ref[...], z_ref[...], p_ref[2:3, :])
        do, dz, dg = vjp(dy_ref[...])
        do_ref[...] = do
        dz_ref[...] = dz.astype(bf16)
        dg_ref[...] += dg

    tok = pl.BlockSpec((t, B_V), lambda i: (i, 0))
    return pl.pallas_call(
        body, name="dn_post_bwd", grid=(S // t,),
        in_specs=[tok, pl.BlockSpec((t, B_V), lambda i: (i, 2)), pl.BlockSpec((8, LANE), lambda i: (0, 0)), tok],
        out_specs=[tok, tok, pl.BlockSpec((1, LANE), lambda i: (0, 0))],
        out_shape=[jax.ShapeDtypeStruct((S, B_V), f32), jax.ShapeDtypeStruct((S, B_V), bf16),
                   jax.ShapeDtypeStruct((1, LANE), f32)],
        compiler_params=_cp(("arbitrary",)),
    )(o, proj, prm, dmix)


N_FF_BLK = D_FF // LANE
GU_SHARD = 2 * D_FF // 4


GLU_ROWS = 256
HALO = 16


def _glu_conv(gext, w, b):
    return (w[2:3] * gext + w[1:2] * shift_down(gext, 1) + w[0:1] * shift_down(gext, 2) + b)[HALO:]


def _glu_gate(c, up):
    return _silu(c) * up


def _glu_gext(g_ref, r0, first, T=GLU_ROWS):
    if first:
        return jnp.concatenate([jnp.zeros((HALO, LANE), f32), g_ref[0:T, :].astype(f32)], axis=0)
    return g_ref[pl.ds(r0 - HALO, T + HALO), :].astype(f32)


def glu_fwd(gu, w, b, name):
    S = gu.shape[0]
    T = min(GLU_ROWS, S // 2)

    def body(g_ref, u_ref, w_ref, b_ref, o_ref, c_ref):
        wv, bv = w_ref[...], b_ref[...]

        def tile(r0, first):
            c = _glu_conv(_glu_gext(g_ref, r0, first, T), wv, bv)
            c_ref[pl.ds(r0, T), :] = c.astype(bf16)
            o_ref[pl.ds(r0, T), :] = _glu_gate(c, u_ref[pl.ds(r0, T), :].astype(f32)).astype(bf16)

        tile(0, True)

        @pl.loop(1, S // T)
        def _(t):
            tile(pl.multiple_of(t * T, T), False)

    col = pl.BlockSpec((S, LANE), lambda j: (0, j))
    return pl.pallas_call(
        body, name=name, grid=(N_FF_BLK,),
        in_specs=[col, pl.BlockSpec((S, LANE), lambda j: (0, N_FF_BLK + j)), pl.BlockSpec((3, LANE), lambda j: (0, j)),
                  pl.BlockSpec((1, LANE), lambda j: (0, j))],
        out_specs=[col, col], out_shape=[jax.ShapeDtypeStruct((S, D_FF), bf16)] * 2,
        compiler_params=_cp(("parallel",)),
    )(gu, gu, w, b.reshape(1, D_FF))


def glu_bwd(gu, c, w, b, dact, name):
    S = gu.shape[0]
    T = min(GLU_ROWS, S // 2)

    def body(g_ref, u_ref, c_ref, w_ref, b_ref, d_ref, dg_ref, dw_ref, db_ref, acc):
        wv, bv = w_ref[...], b_ref[...]

        def tile(r0, first):
            rows = pl.ds(r0, T)
            _, vjp_gate = jax.vjp(_glu_gate, c_ref[rows, :].astype(f32), u_ref[rows, :].astype(f32))
            dc, du = vjp_gate(d_ref[rows, :].astype(f32))
            _, vjp_conv = jax.vjp(_glu_conv, _glu_gext(g_ref, r0, first, T), wv, bv)
            dgx, dw, db = vjp_conv(dc)
            acc[pl.ds(r0, T), :] = dgx[HALO:]
            if not first:
                acc[pl.ds(r0 - HALO, HALO), :] += dgx[:HALO]
            dg_ref[1, pl.ds(r0, T), :] = du.astype(bf16)
            return dw, db

        dw0, db0 = tile(0, True)
        dw_ref[...] = dw0
        db_ref[...] = db0

        @pl.loop(1, S // T)
        def _(t):
            dw, db = tile(pl.multiple_of(t * T, T), False)
            dw_ref[...] += dw
            db_ref[...] += db

        dg_ref[0] = acc[...].astype(bf16)

    col = pl.BlockSpec((S, LANE), lambda j: (0, j))
    wsp = pl.BlockSpec((3, LANE), lambda j: (0, j))
    bsp = pl.BlockSpec((1, LANE), lambda j: (0, j))
    return pl.pallas_call(
        body, name=name, grid=(N_FF_BLK,),
        in_specs=[col, pl.BlockSpec((S, LANE), lambda j: (0, N_FF_BLK + j)), col, wsp, bsp, col],
        out_specs=[pl.BlockSpec((2, S, LANE), lambda j: (0, 0, j)), wsp, bsp],
        out_shape=[jax.ShapeDtypeStruct((2, S, D_FF), bf16), jax.ShapeDtypeStruct((3, D_FF), f32),
                   jax.ShapeDtypeStruct((1, D_FF), f32)],
        scratch_shapes=[pltpu.VMEM((S, LANE), f32)],
        compiler_params=_cp(("parallel",)),
    )(gu, gu, c, w, b.reshape(1, D_FF), dact)


def gu_fwd(n2, wg, name):
    S = n2.shape[0]
    tm = min(MM_ROWS, S)

    def body(a_ref, w_ref, o_ref):
        o_ref[...] = _dg(a_ref[...], w_ref[...], 1, 0).astype(bf16)

    return pl.pallas_call(
        body, name=name, grid=(4, S // tm),
        in_specs=[pl.BlockSpec((tm, D), lambda s, m: (m, 0)), pl.BlockSpec((None, D, GU_SHARD), lambda s, m: (s, 0, 0))],
        out_specs=pl.BlockSpec((tm, GU_SHARD), lambda s, m: (m, s)),
        out_shape=jax.ShapeDtypeStruct((S, 2 * D_FF), bf16), compiler_params=_cp(("parallel", "parallel")),
    )(n2, wg)


def gu_bwd_x(dgu, wg, norm, name):
    S = dgu.shape[1]
    tm = min(NORM_ROWS, S)

    def body(d_ref, w_ref, h_ref, g_ref, r_ref, o_ref, dg_ref):
        _acc_then_norm_bwd(_dg(d_ref[...], w_ref[...], 1, 1), 4, h_ref, g_ref, r_ref, o_ref, dg_ref)

    tok = pl.BlockSpec((tm, D), lambda m, s: (m, 0))
    vec = pl.BlockSpec((1, D), lambda m, s: (0, 0))
    return pl.pallas_call(
        body, name=name, grid=(S // tm, 4),
        in_specs=[pl.BlockSpec((None, tm, GU_SHARD), lambda m, s: (s // 2, m, s % 2)),
                  pl.BlockSpec((None, D, GU_SHARD), lambda m, s: (s, 0, 0)), tok, vec, tok],
        out_specs=[tok, vec],
        out_shape=[jax.ShapeDtypeStruct((S, D), f32), jax.ShapeDtypeStruct((1, D), f32)],
        compiler_params=_cp(("arbitrary", "arbitrary")),
    )(dgu, wg, norm[0], norm[1].reshape(1, D), norm[2])


def gu_bwd_w(n2, dgu, name):
    S = n2.shape[0]
    tm = min(MM_ROWS, S)
    nm = S // tm

    def body(a_ref, d_ref, o_ref, acc):
        @pl.when(pl.program_id(1) == 0)
        def _():
            acc[...] = jnp.zeros_like(acc)
        acc[...] += _dg(a_ref[...], d_ref[...], 0, 0)

        @pl.when(pl.program_id(1) == nm - 1)
        def _():
            o_ref[...] = acc[...].astype(bf16)

    return pl.pallas_call(
        body, name=name, grid=(4, nm),
        in_specs=[pl.BlockSpec((tm, D), lambda s, m: (m, 0)),
                  pl.BlockSpec((None, tm, GU_SHARD), lambda s, m: (s // 2, m, s % 2))],
        out_specs=pl.BlockSpec((None, D, GU_SHARD), lambda s, m: (s, 0, 0)),
        out_shape=jax.ShapeDtypeStruct((4, D, GU_SHARD), bf16),
        scratch_shapes=[pltpu.VMEM((D, GU_SHARD), f32)],
        compiler_params=_cp(("parallel", "arbitrary")),
    )(n2, dgu)


def _pair_cols(w):
    lead = w.shape[:-1]
    return w.reshape(lead + (2, 6, A_DH)).swapaxes(-3, -2).reshape(lead + (A_Q,))


def _unpair_cols(w):
    lead = w.shape[:-1]
    return w.reshape(lead + (6, 2, A_DH)).swapaxes(-3, -2).reshape(lead + (A_Q,))


def _lay_in_a(w):
    return jnp.concatenate([_pair_cols(w[:, :A_Q]), w[:, A_Q:]], axis=1)


def _unlay_in_a(w):
    return jnp.concatenate([_unpair_cols(w[:, :A_Q]), w[:, A_Q:]], axis=1)


def _lay_out_a(w):
    return jnp.concatenate([_pair_cols(w[:A_Q].T).T, w[A_Q:]], axis=0)


def _unlay_out_a(w):
    return jnp.concatenate([_unpair_cols(w[:A_Q].T).T, w[A_Q:]], axis=0)


def _lay_in_b(w):
    return jnp.concatenate([w[:, :2304], w[:, 2316:], w[:, 2304:2316],
                            jnp.zeros((w.shape[0], LANE - 12), w.dtype)], axis=1)


def _unlay_in_b(w):
    return jnp.concatenate([w[:, :2304], w[:, 2560:2572], w[:, 2304:2560]], axis=1)


def _chip_cols(w):
    return jnp.moveaxis(w.reshape(w.shape[0], 4, w.shape[1] // 4), 1, 0)


def _unchip_cols(w):
    return jnp.moveaxis(w, 0, 1).reshape(w.shape[1], 4 * w.shape[2])


def _local_step(x, mem, target, P):
    arrive = P.get("arrive", lambda key, after: None)
    ready = P.get("ready", lambda key, grads, dep: dep)
    sk = jnp.zeros((16, LANE), f32).at[:A_HEADS].set(jnp.broadcast_to(P["sinks"][:, None], (A_HEADS, LANE)))
    prm = jnp.zeros((8, LANE), f32).at[0, 6:12].set(P["a_log"]).at[1, 6:12].set(P["dt_bias"]).at[2].set(P["out_norm_g"])
    bias = bias_build(P["rel_bias"])
    saved = []
    h = x
    n1 = rms_fwd(h, P["g_mix"][0], "rms_mix0")
    for i in range(2):
        arrive(("w_in", i), n1)
        if i == 0:
            proj = mm_nn(n1, P["w_in_a"], out_dtype=bf16, name="proj_a")
        else:
            proj = mm_nn(n1, P["w_in_b"], name="proj_b")
        arrive(("w_mem", i), proj)
        kv = memkv_fwd(mem, P["g_mem"][i], P["w_mem"][i], f"memkv{i}")
        if i == 0:
            self_out = swa_fwd(proj, bias, sk)
            cross = xattn_fwd(proj, A_Q + 2 * LANE, kv, "xattn_a")
            extra = ()
        else:
            qkvn = dnprep_fwd(proj, P["conv_qkv"])
            chunked, inv = dnc_fwd(qkvn, proj, prm)
            o, states = dns_fwd(chunked)
            self_out = dnpost_fwd(o, proj, prm)
            cross = xattn_fwd(proj, 2304, kv, "xattn_b")
            extra = (qkvn, chunked, inv, states, o)
        mix = (self_out, cross)
        arrive(("w_out", i), cross)
        h2, n2 = mm_res_norm(mix, P["w_out"][i], h, P["g_ffn"][i], f"out_proj{i}")
        arrive(("w_gu", i), n2)
        gu = gu_fwd(n2, P["w_gu"][i], f"gate_up{i}")
        act, pre = glu_fwd(gu, P["ffn_cw"][i], P["ffn_cb"][i], f"glu{i}")
        arrive(("w_down", i), act)
        saved.append((h, n1, kv, proj, mix, h2, n2, gu, pre, act, extra))
        if i == 0:
            h, n1 = mm_res_norm(act, P["w_down"][i], h2, P["g_mix"][1], f"down{i}")
        else:
            h = mm_nn(act, P["w_down"][i], res=h2, name=f"down{i}")

    loss, dh, dg_fin = loss_head(h, P["g_fin"], target)
    G = {"g_fin": dg_fin[0], "g_mix": [None, None], "g_mem": [None, None], "g_ffn": [None, None],
         "w_mem": [None, None], "w_out": [None, None], "w_gu": [None, None], "w_down": [None, None],
         "ffn_cw": [None, None], "ffn_cb": [None, None]}
    for i in (1, 0):
        hin, n1, kv, proj, mix, h2, n2, gu, pre, act, extra = saved[i]
        dact = mm_nt(dh, P["w_down"][i], out_dtype=bf16, name=f"d_act{i}")
        G["w_down"][i] = mm_tn(act, dh, name=f"dw_down{i}")
        dgu, dcw, dcb = glu_bwd(gu, pre, P["ffn_cw"][i], P["ffn_cb"][i], dact, f"glu_bwd{i}")
        G["ffn_cw"][i], G["ffn_cb"][i] = dcw, dcb[0]
        G["w_gu"][i] = gu_bwd_w(n2, dgu, f"dw_gu{i}")
        g_ffn = ready(("ffn", i), G, P["g_ffn"][i])
        dh2, dg = gu_bwd_x(dgu, P["w_gu"][i], (h2, g_ffn, dh), f"d_n2_{i}")
        G["g_ffn"][i] = dg[0]
        dmix = mm_nt(dh2, P["w_out"][i], name=f"d_mix{i}")
        G["w_out"][i] = mm_tn(mix, dh2, name=f"dw_out{i}")
        if i == 0:
            dqkv, dbias, dsk = swa_bwd(proj, bias, sk, dmix)
            dxq, dkv = xattn_bwd(proj, A_Q + 2 * LANE, kv, dmix, "xattn_a_bwd")
            dproj = (dqkv, dxq)
            G["sinks"] = dsk[:A_HEADS, 0]
            G["rel_bias"] = bias_grad(dbias)[:, :A_HEADS]
            w_in, gname = P["w_in_a"], "w_in_a"
        else:
            qkvn, chunked, inv, states, o = extra
            do, dz, dgo = dnpost_bwd(o, proj, prm, dmix)
            dqkvn, dseg, dprm = dnc_bwd(qkvn, proj, prm, inv, dns_bwd(chunked, states, do))
            draw, dconv = dnprep_bwd(proj, P["conv_qkv"], dqkvn)
            dxq, dkv = xattn_bwd(proj, 2304, kv, dmix, "xattn_b_bwd")
            dproj = (draw, dz, dxq, dseg)
            G["conv_qkv"] = dconv
            G["a_log"], G["dt_bias"], G["out_norm_g"] = dprm[0, 6:12], dprm[1, 6:12], dgo[0]
            w_in, gname = P["w_in_b"], "w_in_b"
        G[gname] = mm_tn(n1, dproj, name=f"d{gname}")
        dh, dg = mm_nt_norm(dproj, w_in, (hin, P["g_mix"][i], dh2), f"d_n1_{i}")
        G["g_mix"][i] = dg[0]
        dgm, dwm = memkv_bwd(mem, P["g_mem"][i], P["w_mem"][i], dkv, f"memkv_bwd{i}")
        G["g_mem"][i], G["w_mem"][i] = dgm[0], dwm
        ready(("mix", i), G, None)
    return loss, dh, G


def _grads_to_ref(G):
    return {
        "rel_bias": G["rel_bias"], "norm_mix_g": jnp.stack(G["g_mix"]), "norm_mem_g": jnp.stack(G["g_mem"]),
        "w_mem_kv": jnp.stack(G["w_mem"]),
        "w_out": jnp.stack([_unlay_out_a(G["w_out"][0]), G["w_out"][1]]),
        "w_in_a": _unlay_in_a(G["w_in_a"])[None], "sinks_a": G["sinks"][None],
        "w_in_b": _unlay_in_b(G["w_in_b"])[None], "conv_qkv_b": G["conv_qkv"][None],
        "a_log_b": G["a_log"][None], "dt_bias_b": G["dt_bias"][None], "out_norm_g_b": G["out_norm_g"][None],
        "norm_ffn_g": jnp.stack(G["g_ffn"]),
        "w_gate_up": jnp.stack([_unchip_cols(G["w_gu"][0]), _unchip_cols(G["w_gu"][1])]).astype(f32),
        "ffn_conv_w": jnp.stack(G["ffn_cw"]), "ffn_conv_b": jnp.stack(G["ffn_cb"]),
        "w_down": jnp.stack(G["w_down"]), "final_norm_g": G["g_fin"],
    }


ANY = pl.BlockSpec(memory_space=pl.ANY)


def _place():
    return lax.axis_index("x"), lax.axis_index("y"), lax.axis_index("c")


def allreduce_small(buf):
    R = buf.shape[0]

    def body(b_ref, o_ref, recv, ssem, rsem):
        x, y, c = _place()
        me = 4 * x + 2 * y + c

        def peer(k):
            return (1 - x if k & 4 else x, 1 - y if k & 2 else y, 1 - c if k & 1 else c)

        def remote(k, slot):
            return pltpu.make_async_remote_copy(
                src_ref=b_ref, dst_ref=recv.at[slot], send_sem=ssem.at[k - 1], recv_sem=rsem.at[k - 1],
                device_id=peer(k), device_id_type=MESH)

        sends = [remote(k, me) for k in range(1, 8)]
        for cp in sends:
            cp.start()
        recv[me] = b_ref[...]
        for k in range(1, 8):
            px, py, pc = peer(k)
            remote(k, 4 * px + 2 * py + pc).wait_recv()
        for cp in sends:
            cp.wait_send()
        total = recv[0]
        for j in range(1, 8):
            total = total + recv[j]
        o_ref[...] = total

    return pl.pallas_call(
        body, name="small_allreduce",
        in_specs=[pl.BlockSpec(memory_space=pltpu.VMEM)], out_specs=pl.BlockSpec(memory_space=pltpu.VMEM),
        out_shape=jax.ShapeDtypeStruct(buf.shape, f32),
        scratch_shapes=[pltpu.VMEM((8, R, LANE), f32), pltpu.SemaphoreType.DMA((7,)), pltpu.SemaphoreType.DMA((7,))],
    )(buf)


def sum_slots(own, recv, chip, core, name):
    _, R, C = recv.shape
    tr = _row_tile(R, 256)
    nt = R // tr

    def body(p_ref, a_ref, r_ref, o_ref):
        acc = jnp.zeros((tr, C), f32)
        for s in range(4):
            acc = acc + jnp.where(p_ref[0] == s, a_ref[s], r_ref[s]).astype(f32)
        o_ref[...] = acc

    slots = pl.BlockSpec((4, tr, C), lambda i, p_ref: (0, i, 0))
    return pl.pallas_call(
        body, name=name, out_shape=jax.ShapeDtypeStruct((2 * R, C), f32),
        grid_spec=pltpu.PrefetchScalarGridSpec(
            num_scalar_prefetch=1, grid=(nt,), in_specs=[slots, slots],
            out_specs=pl.BlockSpec((tr, C), lambda i, p_ref: (p_ref[1] * nt + i, 0))),
        compiler_params=_cp(("parallel",)),
    )(jnp.stack([chip, core]).astype(jnp.int32), own, recv)


def _half(ref, core, axis=0):
    half = ref.shape[axis] // 2
    idx = (slice(None),) * axis + (pl.ds(core * half, half),)
    return ref.at[idx]


IN_HBM = pl.BlockSpec(memory_space=pltpu.HBM)
IN_SEM = pl.BlockSpec(memory_space=pltpu.SEMAPHORE)
SIDE_EFFECT = pltpu.SideEffectType.DATAFLOW_SIDE_EFFECTING


def _gather_copy(buf, i, k, ssem, rsem, place, landing):
    x, y, c = place
    px, py = [(1 - x, y), (x, 1 - y), (1 - x, 1 - y)][k]
    me = 2 * x + y
    return pltpu.make_async_remote_copy(
        src_ref=buf.at[me], dst_ref=buf.at[me if landing == "theirs" else 2 * px + py],
        send_sem=ssem.at[3 * i + k], recv_sem=rsem.at[3 * i + k], device_id=(px, py, c), device_id_type=MESH)


def gather_start(groups):
    flat = [b for grp in groups for b in grp]
    n, ng = len(flat), len(groups)

    def body(*refs):
        bufs, sems = refs[:n], refs[n:n + 2 * ng]
        place = _place()
        j = 0
        for g, grp in enumerate(groups):
            for i in range(len(grp)):
                for k in range(3):
                    _gather_copy(bufs[j], i, k, sems[2 * g], sems[2 * g + 1], place, "theirs").start()
                j += 1

    sem_shapes = [pltpu.SemaphoreType.DMA((3 * len(grp),)) for grp in groups for _ in range(2)]
    out = pl.pallas_call(
        body, name="gather_start", in_specs=[IN_HBM] * n, out_specs=(*[IN_SEM] * (2 * ng), *[IN_HBM] * n),
        out_shape=(*sem_shapes, *[pltpu.HBM(b.shape, b.dtype) for b in flat]),
        input_output_aliases={i: 2 * ng + i for i in range(n)},
        compiler_params=pltpu.CompilerParams(has_side_effects=SIDE_EFFECT),
    )(*[pltpu.with_memory_space_constraint(b, pltpu.HBM) for b in flat])
    sems, bufs = out[:2 * ng], list(out[2 * ng:])
    flights, j = [], 0
    for g, grp in enumerate(groups):
        flights.append((bufs[j:j + len(grp)], sems[2 * g], sems[2 * g + 1]))
        j += len(grp)
    return flights


def gather_wait(flight, after, name):
    bufs, ssem, rsem = flight
    n = len(bufs)

    def body(*refs):
        place = _place()
        for i in range(n):
            for k in range(3):
                cp = _gather_copy(refs[i], i, k, refs[n], refs[n + 1], place, "mine")
                cp.wait_send()
                cp.wait_recv()

    return pl.pallas_call(
        body, name=name, in_specs=[IN_HBM] * n + [IN_SEM, IN_SEM, ANY], out_specs=[IN_HBM] * n,
        out_shape=[pltpu.HBM(b.shape, b.dtype) for b in bufs], input_output_aliases={i: i for i in range(n)},
        compiler_params=pltpu.CompilerParams(has_side_effects=SIDE_EFFECT),
    )(*bufs, ssem, rsem, after)


def _scatter_copy(src, land, j, k, ssem, rsem, place, landing):
    x, y, c = place
    px, py = [(1 - x, y), (x, 1 - y), (1 - x, 1 - y)][k]
    return pltpu.make_async_remote_copy(
        src_ref=src.at[2 * px + py], dst_ref=land.at[2 * x + y if landing == "theirs" else 2 * px + py],
        send_sem=ssem.at[3 * j + k], recv_sem=rsem.at[3 * j + k], device_id=(px, py, c), device_id_type=MESH)


def scatter_start(srcs, name):
    n = len(srcs)
    lands = [lax.empty(g.shape, g.dtype) for g in srcs]

    def body(*refs):
        place = _place()
        for j in range(n):
            for k in range(3):
                _scatter_copy(refs[j], refs[n + j], j, k, refs[2 * n], refs[2 * n + 1], place, "theirs").start()
        refs[-1][...] = jnp.zeros_like(refs[-1])

    sem = pltpu.SemaphoreType.DMA((3 * n,))
    hbm = [pltpu.with_memory_space_constraint(b, pltpu.HBM) for b in list(srcs) + lands]
    out = pl.pallas_call(
        body, name=name, in_specs=[IN_HBM] * (2 * n),
        out_specs=(IN_SEM, IN_SEM, *[IN_HBM] * (2 * n), pl.BlockSpec(memory_space=pltpu.VMEM)),
        out_shape=(sem, sem, *[pltpu.HBM(b.shape, b.dtype) for b in hbm], jax.ShapeDtypeStruct((8, LANE), f32)),
        input_output_aliases={i: 2 + i for i in range(2 * n)},
        compiler_params=pltpu.CompilerParams(has_side_effects=SIDE_EFFECT),
    )(*hbm)
    return (list(out[2:2 + n]), list(out[2 + n:2 + 2 * n]), out[0], out[1]), out[-1]


def scatter_wait(flight, after, name):
    srcs, lands, ssem, rsem = flight
    n = len(srcs)

    def body(*refs):
        place = _place()
        for j in range(n):
            for k in range(3):
                cp = _scatter_copy(refs[j], refs[n + j], j, k, refs[2 * n], refs[2 * n + 1], place, "mine")
                cp.wait_send()
                cp.wait_recv()

    out = pl.pallas_call(
        body, name=name, in_specs=[IN_HBM] * (2 * n) + [IN_SEM, IN_SEM, ANY], out_specs=[IN_HBM] * (2 * n),
        out_shape=[pltpu.HBM(b.shape, b.dtype) for b in list(srcs) + list(lands)],
        input_output_aliases={i: i for i in range(2 * n)},
        compiler_params=pltpu.CompilerParams(has_side_effects=SIDE_EFFECT),
    )(*srcs, *lands, ssem, rsem, after)
    return list(out[:n]), list(out[n:])


def pair_exchange(gbufs, name):
    n = len(gbufs)

    def body(*refs):
        ins, outs = refs[:n], refs[n:2 * n]
        ssem, rsem = refs[2 * n:]
        x, y, c = _place()
        cps = [pltpu.make_async_remote_copy(
            src_ref=_half(ins[j], 1 - c, axis=1), dst_ref=outs[j], send_sem=ssem.at[j], recv_sem=rsem.at[j],
            device_id=(x, y, 1 - c), device_id_type=MESH) for j in range(n)]
        for cp in cps:
            cp.start()
        for cp in cps:
            cp.wait()

    return pl.pallas_call(
        body, name=name, in_specs=[ANY] * n, out_specs=[ANY] * n,
        out_shape=[jax.ShapeDtypeStruct((4, g.shape[1] // 2, g.shape[2]), g.dtype) for g in gbufs],
        scratch_shapes=[pltpu.SemaphoreType.DMA((n,)), pltpu.SemaphoreType.DMA((n,))],
    )(*gbufs)


def _row_tile(rows, cap=512):
    return max(t for t in range(16, min(rows, cap) + 1, 16) if rows % t == 0)


def pair_sum(mine, theirs, core, name):
    _, R, C = mine.shape
    half = R // 2
    tr = _row_tile(half)
    nt = half // tr

    def body(c_ref, a_ref, b_ref, o_ref):
        o_ref[...] = (a_ref[...].astype(f32) + b_ref[...].astype(f32)).astype(bf16)

    return pl.pallas_call(
        body, name=name, out_shape=jax.ShapeDtypeStruct(theirs.shape, bf16),
        grid_spec=pltpu.PrefetchScalarGridSpec(
            num_scalar_prefetch=1, grid=(4, nt),
            in_specs=[pl.BlockSpec((None, tr, C), lambda s, i, c_ref: (s, c_ref[0] * nt + i, 0)),
                      pl.BlockSpec((None, tr, C), lambda s, i, c_ref: (s, i, 0))],
            out_specs=pl.BlockSpec((None, tr, C), lambda s, i, c_ref: (s, i, 0))),
        compiler_params=_cp(("parallel", "parallel")),
    )(jnp.reshape(core, (1,)).astype(jnp.int32), mine, theirs)


def final_exchange(fins):
    n = len(fins)

    def body(*refs):
        outs = refs[n:2 * n]
        ssem, rsem = refs[2 * n:]
        x, y, c = _place()
        cps = [pltpu.make_async_remote_copy(
            src_ref=_half(outs[j], c), dst_ref=_half(outs[j], c), send_sem=ssem.at[j], recv_sem=rsem.at[j],
            device_id=(x, y, 1 - c), device_id_type=MESH) for j in range(n)]
        for cp in cps:
            cp.start()
        for cp in cps:
            cp.wait()

    return pl.pallas_call(
        body, name="final_exchange", in_specs=[ANY] * n, out_specs=[ANY] * n,
        out_shape=[jax.ShapeDtypeStruct(f.shape, f.dtype) for f in fins],
        input_output_aliases={j: j for j in range(n)},
        scratch_shapes=[pltpu.SemaphoreType.DMA((n,)), pltpu.SemaphoreType.DMA((n,))],
    )(*fins)


def adamw_big(w, m, v, gs, row0, name):
    L, R, C = w.shape
    tr = _row_tile(math.gcd(R, row0) if row0 else R, max(16, 262144 // C // 16 * 16))
    b0 = row0 // tr

    def body(*refs):
        w_ref, m_ref, v_ref = refs[:3]
        g_refs = refs[3:3 + L]
        g_ref, d_ref, nm_ref, nv_ref = refs[3 + L:]
        g = g_refs[0][...]
        for l in range(1, L):
            g = jnp.where(pl.program_id(0) == l, g_refs[l][...], g)
        d, nm, nv = _adamw_math(w_ref[...], g, m_ref[...], v_ref[...])
        g_ref[...] = g
        d_ref[...] = d
        nm_ref[...] = nm
        nv_ref[...] = nv

    own = pl.BlockSpec((None, tr, C), lambda l, i: (l, i, 0))
    off = pl.BlockSpec((tr, C), lambda l, i: (b0 + i, 0))
    return pl.pallas_call(
        body, name=name, grid=(L, R // tr), in_specs=[own, own, own] + [off] * L, out_specs=[own] * 4,
        out_shape=[jax.ShapeDtypeStruct((L, R, C), f32)] * 4, compiler_params=_cp(("parallel", "parallel")),
    )(w, m, v, *gs)


def _adamw_math(w, g, m, v):
    m = B1 * m + (1.0 - B1) * g
    v = B2 * v + (1.0 - B2) * (g * g)
    m_hat = m / (1.0 - B1 ** STEP)
    v_hat = v / (1.0 - B2 ** STEP)
    delta = -LR * (m_hat / (jnp.sqrt(v_hat) + AEPS) + WD * w)
    return delta, m, v


def adamw_small(w, m, v, g):
    def body(w_ref, m_ref, v_ref, g_ref, d_ref, nm_ref, nv_ref):
        d, nm, nv = _adamw_math(w_ref[...], g_ref[...], m_ref[...], v_ref[...])
        d_ref[...] = d
        nm_ref[...] = nm
        nv_ref[...] = nv

    return pl.pallas_call(body, name="adamw_small", out_shape=[jax.ShapeDtypeStruct(w.shape, f32)] * 3)(w, m, v, g)


CONV =(("conv_qkv_b", 2), ("ffn_conv_w", 2))
SMALL = ("rel_bias", "norm_mix_g", "norm_mem_g", "sinks_a", "a_log_b", "dt_bias_b", "out_norm_g_b", "norm_ffn_g",
         "ffn_conv_b", "final_norm_g")
WEIGHTS = ("rel_bias", "norm_mix_g", "norm_mem_g", "w_mem_kv", "w_out", "w_in_a", "sinks_a", "w_in_b", "conv_qkv_b",
           "a_log_b", "dt_bias_b", "out_norm_g_b", "norm_ffn_g", "w_gate_up", "ffn_conv_w", "ffn_conv_b", "w_down",
           "final_norm_g")
ARGS = ("x", "mem") + WEIGHTS + ("loss_target",) + tuple("m_" + n for n in WEIGHTS) + tuple("v_" + n for n in WEIGHTS)


def _rows(a, width):
    flat = a.reshape(-1)
    pad = (-flat.shape[0]) % (8 * width)
    if pad:
        flat = jnp.concatenate([flat, jnp.zeros((pad,), a.dtype)])
    return flat.reshape(-1, width)


def _nrows(shape, width):
    return _pad_to(-(-math.prod(shape) // width), 8)


def _pack(arrs, width, total_rows, dtype):
    parts = [_rows(a.astype(dtype), width) for a in arrs]
    used = sum(p.shape[0] for p in parts)
    if total_rows > used:
        parts.append(jnp.zeros((total_rows - used, width), dtype))
    return jnp.concatenate(parts, axis=0)


def _unpack(buf, shapes, width):
    out, r = [], 0
    for s in shapes:
        n = _nrows(s, width)
        out.append(buf[r:r + n].reshape(-1)[:math.prod(s)].reshape(s))
        r += n
    return out


def _pad_to(n, mult):
    return -(-n // mult) * mult


def kernel(x, mem, rel_bias, norm_mix_g, norm_mem_g, w_mem_kv, w_out, w_in_a, sinks_a, w_in_b, conv_qkv_b, a_log_b, dt_bias_b, out_norm_g_b, norm_ffn_g, w_gate_up, ffn_conv_w, ffn_conv_b, w_down, final_norm_g, loss_target, m_rel_bias, m_norm_mix_g, m_norm_mem_g, m_w_mem_kv, m_w_out, m_w_in_a, m_sinks_a, m_w_in_b, m_conv_qkv_b, m_a_log_b, m_dt_bias_b, m_out_norm_g_b, m_norm_ffn_g, m_w_gate_up, m_ffn_conv_w, m_ffn_conv_b, m_w_down, m_final_norm_g, v_rel_bias, v_norm_mix_g, v_norm_mem_g, v_w_mem_kv, v_w_out, v_w_in_a, v_sinks_a, v_w_in_b, v_conv_qkv_b, v_a_log_b, v_dt_bias_b, v_out_norm_g_b, v_norm_ffn_g, v_w_gate_up, v_ffn_conv_w, v_ffn_conv_b, v_w_down, v_final_norm_g):
    A = dict(zip(ARGS, (x, mem, rel_bias, norm_mix_g, norm_mem_g, w_mem_kv, w_out, w_in_a, sinks_a, w_in_b, conv_qkv_b, a_log_b, dt_bias_b, out_norm_g_b, norm_ffn_g, w_gate_up, ffn_conv_w, ffn_conv_b, w_down, final_norm_g, loss_target, m_rel_bias, m_norm_mix_g, m_norm_mem_g, m_w_mem_kv, m_w_out, m_w_in_a, m_sinks_a, m_w_in_b, m_conv_qkv_b, m_a_log_b, m_dt_bias_b, m_out_norm_g_b, m_norm_ffn_g, m_w_gate_up, m_ffn_conv_w, m_ffn_conv_b, m_w_down, m_final_norm_g, v_rel_bias, v_norm_mix_g, v_norm_mem_g, v_w_mem_kv, v_w_out, v_w_in_a, v_sinks_a, v_w_in_b, v_conv_qkv_b, v_a_log_b, v_dt_bias_b, v_out_norm_g_b, v_norm_ffn_g, v_w_gate_up, v_ffn_conv_w, v_ffn_conv_b, v_w_down, v_final_norm_g)))
    chip = 2 * lax.axis_index("x") + lax.axis_index("y")
    core = lax.axis_index("c")

    def own_slot(shard):
        return lax.dynamic_update_index_in_dim(lax.empty((4,) + shard.shape, shard.dtype), shard, chip, 0)

    def bslot(w):
        return own_slot(w.astype(bf16))

    groups = {
        ("w_in", 0): [bslot(w_in_a[0])],
        ("w_mem", 0): [bslot(w_mem_kv[0]), own_slot(ffn_conv_w.reshape(6, -1))],
        ("w_out", 0): [bslot(w_out[0])], ("w_gu", 0): [bslot(w_gate_up[0])], ("w_down", 0): [bslot(w_down[0])],
        ("w_in", 1): [bslot(w_in_b[0])], ("w_mem", 1): [bslot(w_mem_kv[1]), own_slot(conv_qkv_b[0])],
        ("w_out", 1): [bslot(w_out[1])], ("w_gu", 1): [bslot(w_gate_up[1])], ("w_down", 1): [bslot(w_down[1])],
    }
    flights = dict(zip(groups, gather_start(list(groups.values()))))
    P = {"rel_bias": rel_bias, "sinks": sinks_a[0], "a_log": a_log_b[0], "dt_bias": dt_bias_b[0],
         "out_norm_g": out_norm_g_b[0], "g_mix": norm_mix_g, "g_mem": norm_mem_g, "g_ffn": norm_ffn_g,
         "g_fin": final_norm_g, "ffn_cb": [ffn_conv_b[0], ffn_conv_b[1]], "w_mem": [None, None], "w_out": [None, None],
         "w_gu": [None, None], "w_down": [None, None], "ffn_cw": [None, None]}

    def rows4(g):
        return g.reshape(4 * g.shape[1], g.shape[2])

    def arrive(key, after):
        if key not in flights:
            return
        got = gather_wait(flights.pop(key), after, "gather_wait_%s%d" % key)
        name, i = key
        if name == "w_in":
            P["w_in_a" if i == 0 else "w_in_b"] = (_lay_in_a if i == 0 else _lay_in_b)(_unchip_cols(got[0]))
        elif name == "w_mem":
            P["w_mem"][i] = rows4(got[0])
            if i == 0:
                cw = _unchip_cols(got[1]).reshape(2, 3, D_FF)
                P["ffn_cw"] = [cw[0], cw[1]]
            else:
                P["conv_qkv"] = _unchip_cols(got[1])
        elif name == "w_out":
            P["w_out"][i] = _lay_out_a(rows4(got[0])) if i == 0 else rows4(got[0])
        elif name == "w_gu":
            P["w_gu"][i] = got[0]
        else:
            P["w_down"][i] = rows4(got[0])

    def chip_rows(g):
        return g.reshape(4, g.shape[0] // 4, g.shape[-1])

    sent, started = {}, []

    def ready(key, G, dep):
        kind, i = key
        tag = "%s%d" % key
        if kind == "ffn":
            names, partial = ("gu", "down"), [G["w_gu"][i], chip_rows(G["w_down"][i]).astype(bf16)]
        else:
            g_out = _unlay_out_a(G["w_out"][0]) if i == 0 else G["w_out"][1]
            g_in = _unlay_in_a(G["w_in_a"]) if i == 0 else _unlay_in_b(G["w_in_b"])
            names = ("out", "in", "mem")
            partial = [chip_rows(g_out).astype(bf16), _chip_cols(g_in).astype(bf16), chip_rows(G["w_mem"][i]).astype(bf16)]
        theirs = pair_exchange(partial, "pair_exchange_" + tag)
        pair = [pair_sum(p, t, core, "pair_sum_%s%d" % (nm, i)) for p, t, nm in zip(partial, theirs, names)]
        flight, token = scatter_start(pair, "scatter_start_" + tag)
        sent[key] = (names, flight, token)
        started.append(token[0, 0])
        if dep is not None:
            while started:
                dep = dep + started.pop()
        return dep

    P["arrive"], P["ready"] = arrive, ready

    loss, dx, G = _local_step(x[0], mem[0], loss_target[0], P)
    gfull = _grads_to_ref(G)

    fin, after = {}, sent["mix", 0][2]
    for key in (("ffn", 1), ("mix", 1), ("ffn", 0), ("mix", 0)):
        names, flight, _ = sent[key]
        pair, arrived = scatter_wait(flight, after, "scatter_wait_%s%d" % key)
        for nm, p, r in zip(names, pair, arrived):
            after = fin[nm, key[1]] = sum_slots(p, r, chip, core, "sum_slots_%s%d" % (nm, key[1]))
    order = list(fin)
    done = dict(zip(order, final_exchange([fin[k] for k in order])))

    sm_shapes = [A[n].shape for n in SMALL] + [gfull[n].shape for n, _ in CONV] + [(LANE,)]
    sm_rows = _pad_to(sum(_nrows(s, LANE) for s in sm_shapes), 8)
    sbuf = _pack([gfull[n] for n in SMALL] + [gfull[n] for n, _ in CONV] + [loss[0]], LANE, sm_rows, f32)
    tot = _unpack(allreduce_small(sbuf), sm_shapes, LANE)
    gsmall = dict(zip(SMALL, tot[:len(SMALL)]))
    for (n, axis), t in zip(CONV, tot[len(SMALL):len(SMALL) + len(CONV)]):
        sh = A[n].shape[axis]
        gsmall[n] = lax.dynamic_slice_in_dim(t, chip * sh, sh, axis)
    loss_out = tot[-1][0]

    out = {}
    plan = (("w_gate_up", [done["gu", 0], done["gu", 1]]), ("w_down", [done["down", 0], done["down", 1]]),
            ("w_out", [done["out", 0], done["out", 1]]), ("w_mem_kv", [done["mem", 0], done["mem", 1]]),
            ("w_in_a", [done["in", 0]]), ("w_in_b", [done["in", 1]]))
    for n, gs in plan:
        shape3 = (len(gs),) + gs[0].shape
        res = adamw_big(A[n].reshape(shape3), A["m_" + n].reshape(shape3), A["v_" + n].reshape(shape3), gs, 0,
                        "adamw_" + n)
        for key, r in zip(("grad_", "delta_", "new_m_", "new_v_"), res):
            out[key + n] = r.reshape(A[n].shape)
    names = SMALL + tuple(n for n, _ in CONV)
    shapes = [A[n].shape for n in names]
    rows = _pad_to(sum(_nrows(s, LANE) for s in shapes), 8)
    packs = [_pack([src[n] for n in names], LANE, rows, f32)
             for src in ({n: A[n] for n in names}, {n: A["m_" + n] for n in names}, {n: A["v_" + n] for n in names}, gsmall)]
    res = adamw_small(*packs)
    for key, r in zip(("delta_", "new_m_", "new_v_"), res):
        for n, a in zip(names, _unpack(r, shapes, LANE)):
            out[key + n] = a
    for n in names:
        out["grad_" + n] = gsmall[n]
    return (loss_out, dx[None], *[out["grad_" + n] for n in WEIGHTS], *[out["delta_" + n] for n in WEIGHTS],
            *[out["new_m_" + n] for n in WEIGHTS], *[out["new_v_" + n] for n in WEIGHTS])
```

```python
import functools
import math

import numpy as np
import jax
import jax.numpy as jnp
from jax import lax
from jax.experimental import pallas as pl
from jax.experimental.pallas import tpu as pltpu

f32 = jnp.float32
bf16 = jnp.bfloat16
HI = lax.Precision.HIGHEST
MESH = pl.DeviceIdType.MESH

D = 1024
MEM_LEN = 256
EPS = 1e-6
A_HEADS, A_KV, A_DH = 12, 2, 64
A_Q = 768
BLK = 128
N_BUCKETS, MAX_DIST = 32, 128
B_QK, B_V, B_DH = 384, 768, 128
B_QKV = 1536
CHUNK = 64
X_Q = 256
D_FF = 2816
LANE = 128
VMEM_LIMIT = 56 * 1024 * 1024
MM_ROWS = 1024

LR, B1, B2, AEPS, WD, STEP = 0.001, 0.9, 0.999, 1e-08, 0.01, 10


def _cp(sem=None):
    return pltpu.CompilerParams(dimension_semantics=sem, vmem_limit_bytes=VMEM_LIMIT)


def _dg(a, b, ca, cb, prec=None):
    return lax.dot_general(a, b, (((ca,), (cb,)), ((), ())), precision=prec, preferred_element_type=f32)


@jax.custom_vjp
def bdot(a, b):
    return _dg(a.astype(bf16), b.astype(bf16), 1, 0)


def _bdot_f(a, b):
    return bdot(a, b), (a, b)


def _bdot_b(res, g):
    a, b = res
    gb = g.astype(bf16)
    return _dg(gb, b.astype(bf16), 1, 1), _dg(a.astype(bf16), gb, 0, 0)


bdot.defvjp(_bdot_f, _bdot_b)


@jax.custom_vjp
def bdot_nt(a, b):
    return _dg(a.astype(bf16), b.astype(bf16), 1, 1)


def _bdot_nt_f(a, b):
    return bdot_nt(a, b), (a, b)


def _bdot_nt_b(res, g):
    a, b = res
    gb = g.astype(bf16)
    return _dg(gb, b.astype(bf16), 1, 0), _dg(gb, a.astype(bf16), 0, 0)


bdot_nt.defvjp(_bdot_nt_f, _bdot_nt_b)


def _shift_rows(x, s, down):
    n = x.shape[0]
    row = lax.broadcasted_iota(jnp.int32, x.shape, 0)
    if down:
        return jnp.where(row >= s, pltpu.roll(x, s, 0), 0.0)
    return jnp.where(row < n - s, pltpu.roll(x, n - s, 0), 0.0)


@functools.partial(jax.custom_vjp, nondiff_argnums=(1,))
def shift_down(x, s):
    return _shift_rows(x, s, True)


def _sd_f(x, s):
    return _shift_rows(x, s, True), None


def _sd_b(s, _, g):
    return (_shift_rows(g, s, False),)


shift_down.defvjp(_sd_f, _sd_b)


def _sigmoid(x):
    return 1.0 / (1.0 + jnp.exp(-x))


def _silu(x):
    return x * _sigmoid(x)


def _rms(x, g):
    return x * lax.rsqrt(jnp.mean(x * x, axis=-1, keepdims=True) + EPS) * g


def _tile(n, cap):
    u = n // LANE
    best = 1
    for d in range(1, u + 1):
        if u % d == 0 and d * LANE <= cap:
            best = d
    return best * LANE


def mm_nn(a, w, res=None, out_dtype=f32, name="mm_nn"):
    M, K = a.shape
    N = w.shape[1]
    tm, tn = min(MM_ROWS, M), _tile(N, 1024)

    def body(*refs):
        if res is None:
            a_ref, w_ref, o_ref = refs
            o_ref[...] = _dg(a_ref[...].astype(bf16), w_ref[...], 1, 0).astype(out_dtype)
        else:
            a_ref, w_ref, r_ref, o_ref = refs
            o_ref[...] = (r_ref[...] + _dg(a_ref[...].astype(bf16), w_ref[...], 1, 0)).astype(out_dtype)

    in_specs = [pl.BlockSpec((tm, K), lambda n, m: (m, 0)), pl.BlockSpec((K, tn), lambda n, m: (0, n))]
    args = [a, w]
    if res is not None:
        in_specs.append(pl.BlockSpec((tm, tn), lambda n, m: (m, n)))
        args.append(res)
    return pl.pallas_call(
        body, name=name, grid=(N // tn, M // tm), in_specs=in_specs,
        out_specs=pl.BlockSpec((tm, tn), lambda n, m: (m, n)),
        out_shape=jax.ShapeDtypeStruct((M, N), out_dtype),
        compiler_params=_cp(("parallel", "parallel")),
    )(*args)


def mm_res_norm(a, w, res, g, name):
    pieces = a if isinstance(a, tuple) else (a,)
    na = len(pieces)
    M, K = pieces[0].shape[0], sum(p.shape[1] for p in pieces)
    tm = min(MM_ROWS, M)

    def body(*refs):
        w_ref, r_ref, g_ref, o_ref, n_ref = refs[na:]
        h = r_ref[...] + _dg(_cols(refs[:na]).astype(bf16), w_ref[...], 1, 0)
        o_ref[...] = h
        n_ref[...] = _rms(h, g_ref[...]).astype(bf16)

    tok = pl.BlockSpec((tm, D), lambda m: (m, 0))
    return pl.pallas_call(
        body, name=name, grid=(M // tm,),
        in_specs=[pl.BlockSpec((tm, p.shape[1]), lambda m: (m, 0)) for p in pieces]
        + [pl.BlockSpec((K, D), lambda m: (0, 0)), tok, pl.BlockSpec((1, D), lambda m: (0, 0))],
        out_specs=[tok, tok],
        out_shape=[jax.ShapeDtypeStruct((M, D), f32), jax.ShapeDtypeStruct((M, D), bf16)],
        compiler_params=_cp(("parallel",)),
    )(*pieces, w, res, g.reshape(1, D))


def mm_nt(dy, w, out_dtype=f32, name="mm_nt"):
    M, N = dy.shape
    K = w.shape[0]
    tm, tn = min(MM_ROWS, M), _tile(N, 1024)
    assert out_dtype == f32 or tn == N

    def body(dy_ref, w_ref, o_ref):
        part = _dg(dy_ref[...].astype(bf16), w_ref[...], 1, 1)
        if tn == N:
            o_ref[...] = part.astype(out_dtype)
        else:
            @pl.when(pl.program_id(1) == 0)
            def _():
                o_ref[...] = jnp.zeros_like(o_ref)
            o_ref[...] += part

    return pl.pallas_call(
        body, name=name, grid=(M // tm, N // tn),
        in_specs=[pl.BlockSpec((tm, tn), lambda m, n: (m, n)), pl.BlockSpec((K, tn), lambda m, n: (0, n))],
        out_specs=pl.BlockSpec((tm, K), lambda m, n: (m, 0)),
        out_shape=jax.ShapeDtypeStruct((M, K), out_dtype),
        compiler_params=_cp(("parallel", "arbitrary")),
    )(dy, w)


NORM_ROWS = 1024


def _acc_then_norm_bwd(part, steps, h_ref, g_ref, r_ref, o_ref, dg_ref):
    k = pl.program_id(1)

    @pl.when((pl.program_id(0) == 0) & (k == 0))
    def _():
        dg_ref[...] = jnp.zeros_like(dg_ref)

    @pl.when(k == 0)
    def _():
        o_ref[...] = part

    @pl.when(k > 0)
    def _():
        o_ref[...] += part

    @pl.when(k == steps - 1)
    def _():
        _, vjp = jax.vjp(_rms, h_ref[...], g_ref[...])
        dh, dg = vjp(o_ref[...])
        o_ref[...] = r_ref[...] + dh
        dg_ref[...] += dg


def mm_nt_norm(dy, w, norm, name):
    pieces = dy if isinstance(dy, tuple) else (dy,)
    nd = len(pieces)
    M, N = pieces[0].shape[0], sum(p.shape[1] for p in pieces)
    tm, tn = (min(NORM_ROWS, M), _tile(N, 1024)) if nd == 1 else (min(512, M), N)

    def body(*refs):
        w_ref, h_ref, g_ref, r_ref, o_ref, dg_ref = refs[nd:]
        _acc_then_norm_bwd(_dg(_cols(refs[:nd]).astype(bf16), w_ref[...], 1, 1), N // tn, h_ref, g_ref, r_ref, o_ref,
                           dg_ref)

    tok = pl.BlockSpec((tm, D), lambda m, n: (m, 0))
    vec = pl.BlockSpec((1, D), lambda m, n: (0, 0))
    return pl.pallas_call(
        body, name=name, grid=(M // tm, N // tn),
        in_specs=[pl.BlockSpec((tm, tn if nd == 1 else p.shape[1]), lambda m, n: (m, n)) for p in pieces]
        + [pl.BlockSpec((D, tn), lambda m, n: (0, n)), tok, vec, tok],
        out_specs=[tok, vec],
        out_shape=[jax.ShapeDtypeStruct((M, D), f32), jax.ShapeDtypeStruct((1, D), f32)],
        compiler_params=_cp(("arbitrary", "arbitrary")),
    )(*pieces, w, norm[0], norm[1].reshape(1, D), norm[2])


def _cols(refs):
    return refs[0][...] if len(refs) == 1 else jnp.concatenate([r[...] for r in refs], axis=1)


def mm_tn(a, dy, name="mm_tn"):
    pieces = a if isinstance(a, tuple) else (a,)
    dpieces = dy if isinstance(dy, tuple) else (dy,)
    na, nd = len(pieces), len(dpieces)
    M, K = pieces[0].shape[0], sum(p.shape[1] for p in pieces)
    N = sum(p.shape[1] for p in dpieces)
    tm = min(MM_ROWS, M)
    tk = _tile(K, 1408) if na == 1 else K
    tn = _tile(N, 1024) if nd == 1 else N

    def body(*refs):
        o_ref = refs[-1]

        @pl.when(pl.program_id(2) == 0)
        def _():
            o_ref[...] = jnp.zeros_like(o_ref)
        o_ref[...] += _dg(_cols(refs[:na]).astype(bf16), _cols(refs[na:na + nd]).astype(bf16), 0, 0)

    a_specs = [pl.BlockSpec((tm, tk if na == 1 else p.shape[1]), lambda k, n, m: (m, k)) for p in pieces]
    d_specs = [pl.BlockSpec((tm, tn if nd == 1 else p.shape[1]), lambda k, n, m: (m, n)) for p in dpieces]
    return pl.pallas_call(
        body, name=name, grid=(K // tk, N // tn, M // tm), in_specs=a_specs + d_specs,
        out_specs=pl.BlockSpec((tk, tn), lambda k, n, m: (k, n)),
        out_shape=jax.ShapeDtypeStruct((K, N), f32),
        compiler_params=_cp(("parallel", "parallel", "arbitrary")),
    )(*pieces, *dpieces)


def rms_fwd(h, g, name):
    S = h.shape[0]
    t = min(512, S)

    def body(h_ref, g_ref, o_ref):
        o_ref[...] = _rms(h_ref[...], g_ref[...]).astype(bf16)

    return pl.pallas_call(
        body, name=name, grid=(S // t,),
        in_specs=[pl.BlockSpec((t, D), lambda i: (i, 0)), pl.BlockSpec((1, D), lambda i: (0, 0))],
        out_specs=pl.BlockSpec((t, D), lambda i: (i, 0)),
        out_shape=jax.ShapeDtypeStruct((S, D), bf16),
        compiler_params=_cp(("parallel",)),
    )(h, g.reshape(1, D))


def loss_head(h, g, target):
    S = h.shape[0]
    t = min(512, S)

    def f(hh, gg, tt):
        err = _rms(hh, gg) - tt
        return 0.5 * jnp.sum(jnp.mean(err * err, axis=-1, keepdims=True), axis=0, keepdims=True)

    def body(h_ref, g_ref, t_ref, loss_ref, dh_ref, dg_ref):
        @pl.when(pl.program_id(0) == 0)
        def _():
            dg_ref[...] = jnp.zeros_like(dg_ref)
            loss_ref[...] = jnp.zeros_like(loss_ref)
        val, vjp = jax.vjp(lambda a, b: f(a, b, t_ref[...]), h_ref[...], g_ref[...])
        dh, dg = vjp(jnp.ones((1, 1), f32))
        dh_ref[...] = dh
        dg_ref[...] += dg
        loss_ref[...] += jnp.broadcast_to(val, loss_ref.shape)

    tok = pl.BlockSpec((t, D), lambda i: (i, 0))
    vec = pl.BlockSpec((1, D), lambda i: (0, 0))
    return pl.pallas_call(
        body, name="loss_head", grid=(S // t,), in_specs=[tok, vec, tok],
        out_specs=[pl.BlockSpec((1, LANE), lambda i: (0, 0)), tok, vec],
        out_shape=[jax.ShapeDtypeStruct((1, LANE), f32), jax.ShapeDtypeStruct((S, D), f32),
                   jax.ShapeDtypeStruct((1, D), f32)],
        compiler_params=_cp(("arbitrary",)),
    )(h, g.reshape(1, D), target)


def memkv_fwd(mem, g, w, name):
    def body(m_ref, g_ref, w_ref, o_ref):
        o_ref[...] = _dg(_rms(m_ref[...], g_ref[...]).astype(bf16), w_ref[...], 1, 0)

    return pl.pallas_call(
        body, name=name, out_shape=jax.ShapeDtypeStruct((MEM_LEN, 2 * X_Q), f32), compiler_params=_cp(),
    )(mem, g.reshape(1, D), w)


def memkv_bwd(mem, g, w, dkv, name):
    def body(m_ref, g_ref, w_ref, d_ref, dg_ref, dw_ref):
        n, vjp = jax.vjp(lambda gg: _rms(m_ref[...], gg), g_ref[...])
        db = d_ref[...].astype(bf16)
        dw_ref[...] = _dg(n.astype(bf16), db, 0, 0)
        dg_ref[...] = vjp(_dg(db, w_ref[...], 1, 1))[0]

    return pl.pallas_call(
        body, name=name,
        out_shape=[jax.ShapeDtypeStruct((1, D), f32), jax.ShapeDtypeStruct((D, 2 * X_Q), f32)],
        compiler_params=_cp(),
    )(mem, g.reshape(1, D), w, dkv)


def _xattn_f(xq, mk, mv):
    lane = lax.broadcasted_iota(jnp.int32, (1, X_Q), 1)
    out = jnp.zeros(xq.shape, f32)
    for hd in range(4):
        msk = (lane // 64 == hd).astype(f32)
        s = bdot_nt(xq * msk, mk) * (64 ** -0.5)
        m = lax.stop_gradient(jnp.max(s, axis=-1, keepdims=True))
        p = jnp.exp(s - m)
        p = p / jnp.sum(p, axis=-1, keepdims=True)
        out = out + bdot(p, mv * msk)
    return out


def xattn_fwd(proj, col, kv, name):
    S = proj.shape[0]
    t = min(512, S)
    cb = col // X_Q

    def body(q_ref, k_ref, v_ref, o_ref):
        o_ref[...] = _xattn_f(q_ref[...].astype(f32), k_ref[...], v_ref[...]).astype(bf16)

    return pl.pallas_call(
        body, name=name, grid=(S // t,),
        in_specs=[pl.BlockSpec((t, X_Q), lambda i: (i, cb)), pl.BlockSpec((MEM_LEN, X_Q), lambda i: (0, 0)),
                  pl.BlockSpec((MEM_LEN, X_Q), lambda i: (0, 1))],
        out_specs=pl.BlockSpec((t, X_Q), lambda i: (i, 0)),
        out_shape=jax.ShapeDtypeStruct((S, X_Q), bf16),
        compiler_params=_cp(("parallel",)),
    )(proj, kv, kv)


def xattn_bwd(proj, col, kv, dmix, name):
    S = proj.shape[0]
    t = min(512, S)
    cb = col // X_Q

    def body(q_ref, k_ref, v_ref, do_ref, dq_ref, dk_ref, dv_ref):
        @pl.when(pl.program_id(0) == 0)
        def _():
            dk_ref[...] = jnp.zeros_like(dk_ref)
            dv_ref[...] = jnp.zeros_like(dv_ref)
        _, vjp = jax.vjp(_xattn_f, q_ref[...].astype(f32), k_ref[...], v_ref[...])
        dq, dk, dv = vjp(do_ref[...])
        dq_ref[...] = dq.astype(bf16)
        dk_ref[...] += dk
        dv_ref[...] += dv

    kvb = pl.BlockSpec((MEM_LEN, X_Q), lambda i: (0, 0))
    dq, dk, dv = pl.pallas_call(
        body, name=name, grid=(S // t,),
        in_specs=[pl.BlockSpec((t, X_Q), lambda i: (i, cb)), kvb,
                  pl.BlockSpec((MEM_LEN, X_Q), lambda i: (0, 1)), pl.BlockSpec((t, X_Q), lambda i: (i, 3))],
        out_specs=[pl.BlockSpec((t, X_Q), lambda i: (i, 0)), kvb, kvb],
        out_shape=[jax.ShapeDtypeStruct((S, X_Q), bf16), jax.ShapeDtypeStruct((MEM_LEN, X_Q), f32),
                   jax.ShapeDtypeStruct((MEM_LEN, X_Q), f32)],
        compiler_params=_cp(("arbitrary",)),
    )(proj, kv, kv, dmix)
    return dq, jnp.concatenate([dk, dv], axis=1)


def _bucket_map():
    qi = np.arange(BLK)[:, None]
    kj = np.arange(2 * BLK)[None, :]
    n = np.maximum(BLK + qi - kj, 0)
    max_exact = N_BUCKETS // 2
    nf = np.maximum(n, 1).astype(np.float64)
    large = max_exact + (np.log(nf / max_exact) / math.log(MAX_DIST / max_exact)
                         * (N_BUCKETS - max_exact)).astype(np.int32)
    large = np.minimum(large, N_BUCKETS - 1)
    return np.where(n < max_exact, n, large).astype(np.int32)


def bias_build(rel_bias):
    def body(rb_ref, bk_ref, o_ref):
        bk = bk_ref[...]
        for h in range(A_HEADS):
            acc = jnp.zeros((BLK, 2 * BLK), f32)
            for b in range(N_BUCKETS):
                acc = jnp.where(bk == b, rb_ref[b, h], acc)
            o_ref[h] = acc

    return pl.pallas_call(
        body, name="bias_build",
        in_specs=[pl.BlockSpec(memory_space=pltpu.SMEM), pl.BlockSpec(memory_space=pltpu.VMEM)],
        out_specs=pl.BlockSpec(memory_space=pltpu.VMEM),
        out_shape=jax.ShapeDtypeStruct((A_HEADS, BLK, 2 * BLK), f32), compiler_params=_cp(),
    )(rel_bias, jnp.asarray(_bucket_map()))


def bias_grad(dbias):
    def body(d_ref, bk_ref, o_ref):
        bk = bk_ref[...]
        row = lax.broadcasted_iota(jnp.int32, (N_BUCKETS, LANE), 0)
        lane = lax.broadcasted_iota(jnp.int32, (N_BUCKETS, LANE), 1)
        acc = jnp.zeros((N_BUCKETS, LANE), f32)
        for h in range(A_HEADS):
            d = d_ref[h]
            for b in range(N_BUCKETS):
                s = jnp.sum(jnp.where(bk == b, d, 0.0), keepdims=True)
                acc = acc + jnp.where((row == b) & (lane == h), s, 0.0)
        o_ref[...] = acc

    return pl.pallas_call(
        body, name="bias_grad", out_shape=jax.ShapeDtypeStruct((N_BUCKETS, LANE), f32), compiler_params=_cp(),
    )(dbias, jnp.asarray(_bucket_map()))


def _swa_f(qb, kp, kc, vp, vc, bias, sk, first):
    kband = jnp.concatenate([kp, kc], axis=0)
    vband = jnp.concatenate([vp, vc], axis=0)
    qi = lax.broadcasted_iota(jnp.int32, (BLK, 2 * BLK), 0)
    kj = lax.broadcasted_iota(jnp.int32, (BLK, 2 * BLK), 1)
    rel = kj - qi
    ok = (rel >= 1) & (rel <= BLK) & ((kj >= BLK) | jnp.logical_not(first))
    lane = lax.broadcasted_iota(jnp.int32, (1, LANE), 1)
    lane_b = lax.broadcasted_iota(jnp.int32, (BLK, LANE), 1)
    outs = []
    for p in range(A_HEADS // 2):
        qp = qb[:, LANE * p:LANE * (p + 1)]
        acc = jnp.zeros((BLK, LANE), f32)
        for g in range(2):
            h = g * (A_HEADS // 2) + p
            msk = (lane // A_DH == g).astype(f32)
            s = bdot_nt(qp * msk, kband) * (A_DH ** -0.5) + bias[h]
            s = jnp.where(ok, s, -1e30)
            skb = jnp.broadcast_to(sk[h:h + 1, :], (BLK, LANE))
            sink = jnp.sum(jnp.where(lane_b == 0, skb, 0.0), axis=-1, keepdims=True)
            m = lax.stop_gradient(jnp.maximum(jnp.max(s, axis=-1, keepdims=True), sink))
            e = jnp.exp(s - m)
            prob = e / (jnp.sum(e, axis=-1, keepdims=True) + jnp.exp(sink - m))
            acc = acc + bdot(prob, vband) * msk
        outs.append(acc)
    return jnp.concatenate(outs, axis=1)


def _swa_specs(nb, rev):
    bi = (lambda i: nb - 1 - i) if rev else (lambda i: i)
    return [
        pl.BlockSpec((BLK, A_Q), lambda i: (bi(i), 0)),
        pl.BlockSpec((BLK, LANE), lambda i: (jnp.maximum(bi(i) - 1, 0), 6)),
        pl.BlockSpec((BLK, LANE), lambda i: (bi(i), 6)),
        pl.BlockSpec((BLK, LANE), lambda i: (jnp.maximum(bi(i) - 1, 0), 7)),
        pl.BlockSpec((BLK, LANE), lambda i: (bi(i), 7)),
        pl.BlockSpec((A_HEADS, BLK, 2 * BLK), lambda i: (0, 0, 0)),
        pl.BlockSpec((16, LANE), lambda i: (0, 0)),
    ]


def swa_fwd(proj, bias, sk):
    S = proj.shape[0]
    nb = S // BLK

    def body(q_ref, kp_ref, kc_ref, vp_ref, vc_ref, b_ref, s_ref, o_ref):
        qkv = [r[...].astype(f32) for r in (q_ref, kp_ref, kc_ref, vp_ref, vc_ref)]
        o_ref[...] = _swa_f(*qkv, b_ref[...], s_ref[...], pl.program_id(0) == 0).astype(bf16)

    return pl.pallas_call(
        body, name="swa_fwd", grid=(nb,), in_specs=_swa_specs(nb, False),
        out_specs=pl.BlockSpec((BLK, A_Q), lambda i: (i, 0)),
        out_shape=jax.ShapeDtypeStruct((S, A_Q), bf16), compiler_params=_cp(("parallel",)),
    )(proj, proj, proj, proj, proj, bias, sk)


def swa_bwd(proj, bias, sk, dmix):
    S = proj.shape[0]
    nb = S // BLK

    def body(q_ref, kp_ref, kc_ref, vp_ref, vc_ref, b_ref, s_ref, do_ref, dqkv_ref, db_ref, ds_ref, ck, cv):
        i = pl.program_id(0)

        @pl.when(i == 0)
        def _():
            db_ref[...] = jnp.zeros_like(db_ref)
            ds_ref[...] = jnp.zeros_like(ds_ref)
            ck[...] = jnp.zeros_like(ck)
            cv[...] = jnp.zeros_like(cv)
        first = i == nb - 1
        qkv = [r[...].astype(f32) for r in (q_ref, kp_ref, kc_ref, vp_ref, vc_ref)]
        _, vjp = jax.vjp(lambda *a: _swa_f(*a, first), *qkv, b_ref[...], s_ref[...])
        dq, dkp, dkc, dvp, dvc, db, ds = vjp(do_ref[...])
        dqkv_ref[...] = jnp.concatenate([dq, dkc + ck[...], dvc + cv[...]], axis=1).astype(bf16)
        ck[...] = dkp
        cv[...] = dvp
        db_ref[...] += db
        ds_ref[...] += ds

    return pl.pallas_call(
        body, name="swa_bwd", grid=(nb,),
        in_specs=_swa_specs(nb, True) + [pl.BlockSpec((BLK, A_Q), lambda i: (nb - 1 - i, 0))],
        out_specs=[pl.BlockSpec((BLK, D), lambda i: (nb - 1 - i, 0)),
                   pl.BlockSpec((A_HEADS, BLK, 2 * BLK), lambda i: (0, 0, 0)),
                   pl.BlockSpec((16, LANE), lambda i: (0, 0))],
        out_shape=[jax.ShapeDtypeStruct((S, D), bf16), jax.ShapeDtypeStruct((A_HEADS, BLK, 2 * BLK), f32),
                   jax.ShapeDtypeStruct((16, LANE), f32)],
        scratch_shapes=[pltpu.VMEM((BLK, LANE), f32), pltpu.VMEM((BLK, LANE), f32)],
        compiler_params=_cp(("arbitrary",)),
    )(proj, proj, proj, proj, proj, bias, sk, dmix)


def _dnprep_f(xext, w, is_qk):
    c = (w[3:4] * xext + w[2:3] * shift_down(xext, 1) + w[1:2] * shift_down(xext, 2) + w[0:1] * shift_down(xext, 3))
    a = _silu(c)[HALO:]
    n = a * lax.rsqrt(jnp.sum(a * a, axis=-1, keepdims=True) + EPS)
    return jnp.where(is_qk, n, a)


def dnprep_fwd(proj, cw):
    S = proj.shape[0]
    nblk = B_QKV // LANE
    T = S

    def body(x_ref, w_ref, o_ref):
        is_qk = pl.program_id(0) < 2 * B_QK // LANE
        wv = w_ref[...]

        def tile(r0, first):
            o_ref[pl.ds(r0, T), :] = _dnprep_f(_glu_gext(x_ref, r0, first, T), wv, is_qk)

        tile(0, True)

    return pl.pallas_call(
        body, name="dnprep_fwd", grid=(nblk,),
        in_specs=[pl.BlockSpec((S, LANE), lambda j: (0, j)), pl.BlockSpec((4, LANE), lambda j: (0, j))],
        out_specs=pl.BlockSpec((S, LANE), lambda j: (0, j)),
        out_shape=jax.ShapeDtypeStruct((S, B_QKV), f32), compiler_params=_cp(("parallel",)),
    )(proj, cw)


def dnprep_bwd(proj, cw, dqkvn):
    S = proj.shape[0]
    nblk = B_QKV // LANE

    T = S

    def body(x_ref, w_ref, d_ref, dx_ref, dw_ref):
        is_qk = pl.program_id(0) < 2 * B_QK // LANE
        wv = w_ref[...]

        def tile(r0, first):
            _, vjp = jax.vjp(lambda a, b: _dnprep_f(a, b, is_qk), _glu_gext(x_ref, r0, first, T), wv)
            dx, dw = vjp(d_ref[pl.ds(r0, T), :])
            dx_ref[pl.ds(r0, T), :] = dx[HALO:].astype(bf16)
            if not first:
                dx_ref[pl.ds(r0 - HALO, HALO), :] += dx[:HALO]
            return dw

        dw_ref[...] = tile(0, True)

    col = pl.BlockSpec((S, LANE), lambda j: (0, j))
    wsp = pl.BlockSpec((4, LANE), lambda j: (0, j))
    return pl.pallas_call(
        body, name="dnprep_bwd", grid=(nblk,), in_specs=[col, wsp, col], out_specs=[col, wsp],
        out_shape=[jax.ShapeDtypeStruct((S, B_QKV), bf16), jax.ShapeDtypeStruct((4, B_QKV), f32)],
        compiler_params=_cp(("parallel",)),
    )(proj, cw, dqkvn)


def _hdot(a, b, ca=1, cb=0):
    return _dg(a, b, ca, cb, HI)


def _bdg(a, b, ca, cb):
    dn = (((ca,), (cb,)), ((0,), (0,)))
    ah, bh = a.astype(bf16), b.astype(bf16)
    al, bl = (a - ah.astype(f32)).astype(bf16), (b - bh.astype(f32)).astype(bf16)
    return (lax.dot_general(ah, bh, dn, preferred_element_type=f32)
            + lax.dot_general(ah, bl, dn, preferred_element_type=f32)
            + lax.dot_general(al, bh, dn, preferred_element_type=f32))


@jax.custom_vjp
def hbd(a, b):
    return _bdg(a, b, 2, 1)


@jax.custom_vjp
def hbd_nt(a, b):
    return _bdg(a, b, 2, 2)


@jax.custom_vjp
def hbd_tn(a, b):
    return _bdg(a, b, 1, 1)


hbd.defvjp(lambda a, b: (hbd(a, b), (a, b)), lambda r, g: (hbd_nt(g, r[1]), hbd_tn(r[0], g)))
hbd_nt.defvjp(lambda a, b: (hbd_nt(a, b), (a, b)), lambda r, g: (hbd(g, r[1]), hbd_tn(g, r[0])))
hbd_tn.defvjp(lambda a, b: (hbd_tn(a, b), (a, b)), lambda r, g: (hbd_nt(r[1], g), hbd(r[0], g)))


def _stack(xs):
    return jnp.concatenate([x[None] for x in xs], axis=0)


def _lane_col(x, j):
    lane = lax.broadcasted_iota(jnp.int32, (1, LANE), 1)
    return jnp.sum(jnp.where(lane == j, x, 0.0), axis=-1, keepdims=True)


def _tri_inv(a_mat):
    r = lax.broadcasted_iota(jnp.int32, (1, CHUNK, CHUNK), 1)
    c = lax.broadcasted_iota(jnp.int32, (1, CHUNK, CHUNK), 2)
    pw = -a_mat
    inv = (r == c).astype(f32) + pw
    for _ in range(5):
        pw = hbd(pw, pw)
        inv = inv + hbd(inv, pw)
    return inv


@jax.custom_vjp
def _tri_inv_known(a_mat, inv):
    return inv


_tri_inv_known.defvjp(lambda a, inv: (inv, inv),
                      lambda inv, g: (-hbd_tn(inv, hbd_nt(g, inv)), jnp.zeros_like(inv)))


def _dnc_f(q, k, v, seg, prm, inverse=_tri_inv):
    C = CHUNK
    B = q.shape[0]
    rows = seg.shape[0]
    beta_all = _sigmoid(seg)
    xx = seg + prm[1:2]
    g_all = -jnp.exp(prm[0:1]) * (jnp.maximum(xx, 0.0) + jnp.log(1.0 + jnp.exp(-jnp.abs(xx))))
    r2 = lax.broadcasted_iota(jnp.int32, (rows, rows), 0)
    c2 = lax.broadcasted_iota(jnp.int32, (rows, rows), 1)
    within = (r2 >= c2) & (r2 // C == c2 // C)
    gc_all = _hdot(within.astype(f32), g_all)
    beta = _stack([_lane_col(beta_all[C * j:C * (j + 1)], h) for j in range(rows // C) for h in range(6)])
    gc = _stack([_lane_col(gc_all[C * j:C * (j + 1)], 6 + h) for j in range(rows // C) for h in range(6)])
    r = lax.broadcasted_iota(jnp.int32, (1, C, C), 1)
    c = lax.broadcasted_iota(jnp.int32, (1, C, C), 2)
    incl = r >= c
    strict = r > c
    gct = [gc_all[C * j:C * (j + 1)].T for j in range(rows // C)]
    g_row = _stack([jnp.broadcast_to(gct[j][6 + h:7 + h, :], (C, C))
                    for j in range(rows // C) for h in range(6)])
    decay = jnp.where(incl, jnp.exp(jnp.where(incl, gc - g_row, 0.0)), 0.0)
    a_mat = beta * sbd_nt(k, k) * jnp.where(strict, decay, 0.0)
    eg = jnp.exp(gc)
    inv = inverse(a_mat)
    u = hbd(inv, beta * v)
    w = hbd(inv, (beta * eg) * k)
    qc = q * (B_DH ** -0.5)
    attn = sbd_nt(qc, k) * decay
    last = (lax.broadcasted_iota(jnp.int32, (1, C, 1), 1) == C - 1).astype(f32)
    g_last = jnp.sum(gc * last, axis=1, keepdims=True)
    dc = jnp.broadcast_to(jnp.exp(g_last), (B, 1, LANE)).reshape(B, LANE)
    return u, w, qc * eg, k * jnp.exp(g_last - gc), attn, dc, inv


def _b1(a, b, ca, cb):
    return lax.dot_general(a.astype(bf16), b.astype(bf16), (((ca,), (cb,)), ((0,), (0,))), preferred_element_type=f32)


@jax.custom_vjp
def sbd(a, b):
    return _b1(a, b, 2, 1)


@jax.custom_vjp
def sbd_nt(a, b):
    return _b1(a, b, 2, 2)


@jax.custom_vjp
def sbd_tn(a, b):
    return _b1(a, b, 1, 1)


sbd.defvjp(lambda a, b: (sbd(a, b), (a, b)), lambda r, g: (sbd_nt(g, r[1]), sbd_tn(r[0], g)))
sbd_nt.defvjp(lambda a, b: (sbd_nt(a, b), (a, b)), lambda r, g: (sbd(g, r[1]), sbd_tn(g, r[0])))
sbd_tn.defvjp(lambda a, b: (sbd_tn(a, b), (a, b)), lambda r, g: (sbd_nt(r[1], g), sbd(r[0], g)))


def _dns_f(S0, u, w, qd, kt, attn, dcrows):
    dc = _lane_col(dcrows, 0).reshape(6, 1, 1)
    delta = u - sbd(w, S0)
    out = sbd(qd, S0) + sbd(attn, delta)
    return out, dc * S0 + sbd_tn(kt, delta)


def _dnpost_f(o, z, grow):
    outs = []
    for h in range(6):
        oh = o[:, LANE * h:LANE * (h + 1)]
        outs.append(oh * lax.rsqrt(jnp.mean(oh * oh, axis=-1, keepdims=True) + EPS) * grow
                    * _silu(z[:, LANE * h:LANE * (h + 1)]))
    return jnp.concatenate(outs, axis=1)


def _hs(h):
    return slice(LANE * h, LANE * (h + 1))


DN_CHUNKS = 4


def _heads(ref, share):
    return _stack([ref[CHUNK * j:CHUNK * (j + 1), _hs(h // share)]
                   for j in range(ref.shape[0] // CHUNK) for h in range(6)])


def _put_heads(ref, val):
    for j in range(ref.shape[0] // CHUNK):
        for h in range(6):
            ref[CHUNK * j:CHUNK * (j + 1), _hs(h)] = val[6 * j + h].astype(ref.dtype)


def _dnc_in_specs():
    rows = CHUNK * DN_CHUNKS
    return [
        pl.BlockSpec((rows, B_QK), lambda n: (n, 0)),
        pl.BlockSpec((rows, B_QK), lambda n: (n, 1)),
        pl.BlockSpec((rows, B_V), lambda n: (n, 1)),
        pl.BlockSpec((rows, LANE), lambda n: (n, 20)),
        pl.BlockSpec((8, LANE), lambda n: (0, 0)),
    ]


def _dnc_out_specs(rev_nc=None, chunks=1):
    ci = (lambda n: n) if rev_nc is None else (lambda n: rev_nc - 1 - n)
    wide = pl.BlockSpec((CHUNK * chunks, B_V), lambda n: (ci(n), 0))
    return [wide, wide, wide, wide, pl.BlockSpec((chunks, 6, CHUNK, CHUNK), lambda n: (ci(n), 0, 0, 0)),
            pl.BlockSpec((chunks, 8, LANE), lambda n: (ci(n), 0, 0))]


def _dc_rows(dc):
    pad = jnp.zeros((2, LANE), f32)
    return _stack([jnp.concatenate([dc[6 * j:6 * (j + 1)], pad], axis=0) for j in range(dc.shape[0] // 6)])


def _dnc_shapes(S, mm=f32):
    nc = S // CHUNK
    return [jax.ShapeDtypeStruct((S, B_V), f32)] + [jax.ShapeDtypeStruct((S, B_V), mm)] * 3 + [
        jax.ShapeDtypeStruct((nc, 6, CHUNK, CHUNK), mm), jax.ShapeDtypeStruct((nc, 8, LANE), f32)]


def dnc_fwd(qkvn, proj, prm):
    S = proj.shape[0]

    def body(q_ref, k_ref, v_ref, s_ref, p_ref, u_ref, w_ref, qd_ref, kt_ref, at_ref, dc_ref, inv_ref):
        u, w, qd, kt, attn, dc, inv = _dnc_f(_heads(q_ref, 2), _heads(k_ref, 2), _heads(v_ref, 1), s_ref[...],
                                             p_ref[...])
        inv_ref[...] = inv.reshape(inv_ref.shape)
        _put_heads(u_ref, u)
        _put_heads(w_ref, w)
        _put_heads(qd_ref, qd)
        _put_heads(kt_ref, kt)
        at_ref[...] = attn.reshape(at_ref.shape).astype(at_ref.dtype)
        dc_ref[...] = _dc_rows(dc)

    outs = _dnc_out_specs(chunks=DN_CHUNKS)
    out = pl.pallas_call(
        body, name="dn_chunk_fwd", grid=(S // (CHUNK * DN_CHUNKS),), in_specs=_dnc_in_specs(),
        out_specs=outs + [outs[4]], out_shape=_dnc_shapes(S, bf16) + [_dnc_shapes(S)[4]],
        compiler_params=_cp(("parallel",)),
    )(qkvn, qkvn, qkvn, proj, prm)
    return out[:6], out[6]


def dnc_bwd(qkvn, proj, prm, inv, cots):
    S = proj.shape[0]

    def body(q_ref, k_ref, v_ref, s_ref, p_ref, inv_ref, du_ref, dw_ref, dqd_ref, dkt_ref, dat_ref, ddc_ref,
             dx_ref, dseg_ref, dprm_ref):
        @pl.when(pl.program_id(0) == 0)
        def _():
            dprm_ref[...] = jnp.zeros_like(dprm_ref)
        nb = 6 * DN_CHUNKS
        known = functools.partial(_tri_inv_known, inv=inv_ref[...].reshape(nb, CHUNK, CHUNK))
        _, vjp = jax.vjp(lambda *a: _dnc_f(*a, inverse=known)[:6], _heads(q_ref, 2), _heads(k_ref, 2),
                         _heads(v_ref, 1), s_ref[...], p_ref[...])
        ddc = jnp.concatenate([ddc_ref[j, 0:6, :] for j in range(DN_CHUNKS)], axis=0)
        dq, dk, dv, dseg, dprm = vjp((_heads(du_ref, 1), _heads(dw_ref, 1), _heads(dqd_ref, 1), _heads(dkt_ref, 1),
                                      dat_ref[...].reshape(nb, CHUNK, CHUNK), ddc))
        for j in range(DN_CHUNKS):
            o = 6 * j
            dx_ref[CHUNK * j:CHUNK * (j + 1), :] = jnp.concatenate(
                [dq[o] + dq[o + 1], dq[o + 2] + dq[o + 3], dq[o + 4] + dq[o + 5],
                 dk[o] + dk[o + 1], dk[o + 2] + dk[o + 3], dk[o + 4] + dk[o + 5]] + [dv[o + h] for h in range(6)], axis=1)
        dseg_ref[...] = dseg.astype(bf16)
        dprm_ref[...] += dprm

    rows = CHUNK * DN_CHUNKS
    outs = _dnc_out_specs(chunks=DN_CHUNKS)
    return pl.pallas_call(
        body, name="dn_chunk_bwd", grid=(S // rows,),
        in_specs=_dnc_in_specs() + [outs[4]] + outs,
        out_specs=[pl.BlockSpec((rows, B_QKV), lambda n: (n, 0)), pl.BlockSpec((rows, LANE), lambda n: (n, 0)),
                   pl.BlockSpec((8, LANE), lambda n: (0, 0))],
        out_shape=[jax.ShapeDtypeStruct((S, B_QKV), f32), jax.ShapeDtypeStruct((S, LANE), bf16),
                   jax.ShapeDtypeStruct((8, LANE), f32)],
        compiler_params=_cp(("arbitrary",)),
    )(qkvn, qkvn, qkvn, proj, prm, inv, *cots)


def dns_fwd(chunked):
    u = chunked[0]
    S = u.shape[0]
    nc = S // CHUNK

    def body(u_ref, w_ref, qd_ref, kt_ref, at_ref, dc_ref, o_ref, st_ref, st):
        @pl.when(pl.program_id(0) == 0)
        def _():
            st[...] = jnp.zeros_like(st)
        S0 = st[...]
        st_ref[0] = S0
        out, S1 = _dns_f(S0, _heads(u_ref, 1), _heads(w_ref, 1), _heads(qd_ref, 1), _heads(kt_ref, 1),
                         at_ref[0], dc_ref[0, 0:6, :])
        _put_heads(o_ref, out)
        st[...] = S1

    return pl.pallas_call(
        body, name="dn_scan_fwd", grid=(nc,), in_specs=_dnc_out_specs(),
        out_specs=[pl.BlockSpec((CHUNK, B_V), lambda n: (n, 0)),
                   pl.BlockSpec((1, 6, B_DH, B_DH), lambda n: (n, 0, 0, 0))],
        out_shape=[jax.ShapeDtypeStruct((S, B_V), f32), jax.ShapeDtypeStruct((nc, 6, B_DH, B_DH), f32)],
        scratch_shapes=[pltpu.VMEM((6, B_DH, B_DH), f32)],
        compiler_params=_cp(("arbitrary",)),
    )(*chunked)


def dns_bwd(chunked, states, do):
    S = do.shape[0]
    nc = S // CHUNK

    def body(u_ref, w_ref, qd_ref, kt_ref, at_ref, dc_ref, st_ref, do_ref,
             du_ref, dw_ref, dqd_ref, dkt_ref, dat_ref, ddc_ref, dst):
        @pl.when(pl.program_id(0) == 0)
        def _():
            dst[...] = jnp.zeros_like(dst)
        _, vjp = jax.vjp(_dns_f, st_ref[0], _heads(u_ref, 1), _heads(w_ref, 1).astype(f32),
                         _heads(qd_ref, 1).astype(f32), _heads(kt_ref, 1).astype(f32), at_ref[0].astype(f32),
                         dc_ref[0, 0:6, :])
        dS0, du, dw, dqd, dkt, dat, ddc = vjp((_heads(do_ref, 1), dst[...]))
        dst[...] = dS0
        _put_heads(du_ref, du)
        _put_heads(dw_ref, dw)
        _put_heads(dqd_ref, dqd)
        _put_heads(dkt_ref, dkt)
        dat_ref[0] = dat
        ddc_ref[0] = jnp.concatenate([ddc, jnp.zeros((2, LANE), f32)], axis=0)

    return pl.pallas_call(
        body, name="dn_scan_bwd", grid=(nc,),
        in_specs=_dnc_out_specs(nc) + [pl.BlockSpec((1, 6, B_DH, B_DH), lambda n: (nc - 1 - n, 0, 0, 0)),
                                       pl.BlockSpec((CHUNK, B_V), lambda n: (nc - 1 - n, 0))],
        out_specs=_dnc_out_specs(nc), out_shape=_dnc_shapes(S),
        scratch_shapes=[pltpu.VMEM((6, B_DH, B_DH), f32)],
        compiler_params=_cp(("arbitrary",)),
    )(*chunked, states, do)


def dnpost_fwd(o, proj, prm):
    S = o.shape[0]
    t = min(512, S)

    def body(o_ref, z_ref, p_ref, y_ref):
        y_ref[...] = _dnpost_f(o_ref[...], z_ref[...], p_ref[2:3, :]).astype(bf16)

    tok = pl.BlockSpec((t, B_V), lambda i: (i, 0))
    return pl.pallas_call(
        body, name="dn_post_fwd", grid=(S // t,),
        in_specs=[tok, pl.BlockSpec((t, B_V), lambda i: (i, 2)), pl.BlockSpec((8, LANE), lambda i: (0, 0))],
        out_specs=tok, out_shape=jax.ShapeDtypeStruct((S, B_V), bf16), compiler_params=_cp(("parallel",)),
    )(o, proj, prm)


def dnpost_bwd(o, proj, prm, dmix):
    S = o.shape[0]
    t = min(512, S)

    def body(o_ref, z_ref, p_ref, dy_ref, do_ref, dz_ref, dg_ref):
        @pl.when(pl.program_id(0) == 0)
        def _():
            dg_ref[...] = jnp.zeros_like(dg_ref)
        _, vjp = jax.vjp(_dnpost_f, o_ref[...], z_ref[...], p_ref[2:3, :])
        do, dz, dg = vjp(dy_ref[...])
        do_ref[...] = do
        dz_ref[...] = dz.astype(bf16)
        dg_ref[...] += dg

    tok = pl.BlockSpec((t, B_V), lambda i: (i, 0))
    return pl.pallas_call(
        body, name="dn_post_bwd", grid=(S // t,),
        in_specs=[tok, pl.BlockSpec((t, B_V), lambda i: (i, 2)), pl.BlockSpec((8, LANE), lambda i: (0, 0)), tok],
        out_specs=[tok, tok, pl.BlockSpec((1, LANE), lambda i: (0, 0))],
        out_shape=[jax.ShapeDtypeStruct((S, B_V), f32), jax.ShapeDtypeStruct((S, B_V), bf16),
                   jax.ShapeDtypeStruct((1, LANE), f32)],
        compiler_params=_cp(("arbitrary",)),
    )(o, proj, prm, dmix)


N_FF_BLK = D_FF // LANE
GU_SHARD = 2 * D_FF // 4


GLU_ROWS = 256
HALO = 16


def _glu_conv(gext, w, b):
    return (w[2:3] * gext + w[1:2] * shift_down(gext, 1) + w[0:1] * shift_down(gext, 2) + b)[HALO:]


def _glu_gate(c, up):
    return _silu(c) * up


def _glu_gext(g_ref, r0, first, T=GLU_ROWS):
    if first:
        return jnp.concatenate([jnp.zeros((HALO, LANE), f32), g_ref[0:T, :].astype(f32)], axis=0)
    return g_ref[pl.ds(r0 - HALO, T + HALO), :].astype(f32)


def glu_fwd(gu, w, b, name):
    S = gu.shape[0]
    T = min(GLU_ROWS, S // 2)

    def body(g_ref, u_ref, w_ref, b_ref, o_ref, c_ref):
        wv, bv = w_ref[...], b_ref[...]

        def tile(r0, first):
            c = _glu_conv(_glu_gext(g_ref, r0, first, T), wv, bv)
            c_ref[pl.ds(r0, T), :] = c.astype(bf16)
            o_ref[pl.ds(r0, T), :] = _glu_gate(c, u_ref[pl.ds(r0, T), :].astype(f32)).astype(bf16)

        tile(0, True)

        @pl.loop(1, S // T)
        def _(t):
            tile(pl.multiple_of(t * T, T), False)

    col = pl.BlockSpec((S, LANE), lambda j: (0, j))
    return pl.pallas_call(
        body, name=name, grid=(N_FF_BLK,),
        in_specs=[col, pl.BlockSpec((S, LANE), lambda j: (0, N_FF_BLK + j)), pl.BlockSpec((3, LANE), lambda j: (0, j)),
                  pl.BlockSpec((1, LANE), lambda j: (0, j))],
        out_specs=[col, col], out_shape=[jax.ShapeDtypeStruct((S, D_FF), bf16)] * 2,
        compiler_params=_cp(("parallel",)),
    )(gu, gu, w, b.reshape(1, D_FF))


def glu_bwd(gu, c, w, b, dact, name):
    S = gu.shape[0]
    T = min(GLU_ROWS, S // 2)

    def body(g_ref, u_ref, c_ref, w_ref, b_ref, d_ref, dg_ref, dw_ref, db_ref, acc):
        wv, bv = w_ref[...], b_ref[...]

        def tile(r0, first):
            rows = pl.ds(r0, T)
            _, vjp_gate = jax.vjp(_glu_gate, c_ref[rows, :].astype(f32), u_ref[rows, :].astype(f32))
            dc, du = vjp_gate(d_ref[rows, :].astype(f32))
            _, vjp_conv = jax.vjp(_glu_conv, _glu_gext(g_ref, r0, first, T), wv, bv)
            dgx, dw, db = vjp_conv(dc)
            acc[pl.ds(r0, T), :] = dgx[HALO:]
            if not first:
                acc[pl.ds(r0 - HALO, HALO), :] += dgx[:HALO]
            dg_ref[1, pl.ds(r0, T), :] = du.astype(bf16)
            return dw, db

        dw0, db0 = tile(0, True)
        dw_ref[...] = dw0
        db_ref[...] = db0

        @pl.loop(1, S // T)
        def _(t):
            dw, db = tile(pl.multiple_of(t * T, T), False)
            dw_ref[...] += dw
            db_ref[...] += db

        dg_ref[0] = acc[...].astype(bf16)

    col = pl.BlockSpec((S, LANE), lambda j: (0, j))
    wsp = pl.BlockSpec((3, LANE), lambda j: (0, j))
    bsp = pl.BlockSpec((1, LANE), lambda j: (0, j))
    return pl.pallas_call(
        body, name=name, grid=(N_FF_BLK,),
        in_specs=[col, pl.BlockSpec((S, LANE), lambda j: (0, N_FF_BLK + j)), col, wsp, bsp, col],
        out_specs=[pl.BlockSpec((2, S, LANE), lambda j: (0, 0, j)), wsp, bsp],
        out_shape=[jax.ShapeDtypeStruct((2, S, D_FF), bf16), jax.ShapeDtypeStruct((3, D_FF), f32),
                   jax.ShapeDtypeStruct((1, D_FF), f32)],
        scratch_shapes=[pltpu.VMEM((S, LANE), f32)],
        compiler_params=_cp(("parallel",)),
    )(gu, gu, c, w, b.reshape(1, D_FF), dact)


def gu_fwd(n2, wg, name):
    S = n2.shape[0]
    tm = min(MM_ROWS, S)

    def body(a_ref, w_ref, o_ref):
        o_ref[...] = _dg(a_ref[...], w_ref[...], 1, 0).astype(bf16)

    return pl.pallas_call(
        body, name=name, grid=(4, S // tm),
        in_specs=[pl.BlockSpec((tm, D), lambda s, m: (m, 0)), pl.BlockSpec((None, D, GU_SHARD), lambda s, m: (s, 0, 0))],
        out_specs=pl.BlockSpec((tm, GU_SHARD), lambda s, m: (m, s)),
        out_shape=jax.ShapeDtypeStruct((S, 2 * D_FF), bf16), compiler_params=_cp(("parallel", "parallel")),
    )(n2, wg)


def gu_bwd_x(dgu, wg, norm, name):
    S = dgu.shape[1]
    tm = min(NORM_ROWS, S)

    def body(d_ref, w_ref, h_ref, g_ref, r_ref, o_ref, dg_ref):
        _acc_then_norm_bwd(_dg(d_ref[...], w_ref[...], 1, 1), 4, h_ref, g_ref, r_ref, o_ref, dg_ref)

    tok = pl.BlockSpec((tm, D), lambda m, s: (m, 0))
    vec = pl.BlockSpec((1, D), lambda m, s: (0, 0))
    return pl.pallas_call(
        body, name=name, grid=(S // tm, 4),
        in_specs=[pl.BlockSpec((None, tm, GU_SHARD), lambda m, s: (s // 2, m, s % 2)),
                  pl.BlockSpec((None, D, GU_SHARD), lambda m, s: (s, 0, 0)), tok, vec, tok],
        out_specs=[tok, vec],
        out_shape=[jax.ShapeDtypeStruct((S, D), f32), jax.ShapeDtypeStruct((1, D), f32)],
        compiler_params=_cp(("arbitrary", "arbitrary")),
    )(dgu, wg, norm[0], norm[1].reshape(1, D), norm[2])


def gu_bwd_w(n2, dgu, name):
    S = n2.shape[0]
    tm = min(MM_ROWS, S)
    nm = S // tm

    def body(a_ref, d_ref, o_ref, acc):
        @pl.when(pl.program_id(1) == 0)
        def _():
            acc[...] = jnp.zeros_like(acc)
        acc[...] += _dg(a_ref[...], d_ref[...], 0, 0)

        @pl.when(pl.program_id(1) == nm - 1)
        def _():
            o_ref[...] = acc[...].astype(bf16)

    return pl.pallas_call(
        body, name=name, grid=(4, nm),
        in_specs=[pl.BlockSpec((tm, D), lambda s, m: (m, 0)),
                  pl.BlockSpec((None, tm, GU_SHARD), lambda s, m: (s // 2, m, s % 2))],
        out_specs=pl.BlockSpec((None, D, GU_SHARD), lambda s, m: (s, 0, 0)),
        out_shape=jax.ShapeDtypeStruct((4, D, GU_SHARD), bf16),
        scratch_shapes=[pltpu.VMEM((D, GU_SHARD), f32)],
        compiler_params=_cp(("parallel", "arbitrary")),
    )(n2, dgu)


def _pair_cols(w):
    lead = w.shape[:-1]
    return w.reshape(lead + (2, 6, A_DH)).swapaxes(-3, -2).reshape(lead + (A_Q,))


def _unpair_cols(w):
    lead = w.shape[:-1]
    return w.reshape(lead + (6, 2, A_DH)).swapaxes(-3, -2).reshape(lead + (A_Q,))


def _lay_in_a(w):
    return jnp.concatenate([_pair_cols(w[:, :A_Q]), w[:, A_Q:]], axis=1)


def _unlay_in_a(w):
    return jnp.concatenate([_unpair_cols(w[:, :A_Q]), w[:, A_Q:]], axis=1)


def _lay_out_a(w):
    return jnp.concatenate([_pair_cols(w[:A_Q].T).T, w[A_Q:]], axis=0)


def _unlay_out_a(w):
    return jnp.concatenate([_unpair_cols(w[:A_Q].T).T, w[A_Q:]], axis=0)


def _lay_in_b(w):
    return jnp.concatenate([w[:, :2304], w[:, 2316:], w[:, 2304:2316],
                            jnp.zeros((w.shape[0], LANE - 12), w.dtype)], axis=1)


def _unlay_in_b(w):
    return jnp.concatenate([w[:, :2304], w[:, 2560:2572], w[:, 2304:2560]], axis=1)


def _chip_cols(w):
    return jnp.moveaxis(w.reshape(w.shape[0], 4, w.shape[1] // 4), 1, 0)


def _unchip_cols(w):
    return jnp.moveaxis(w, 0, 1).reshape(w.shape[1], 4 * w.shape[2])


def _local_step(x, mem, target, P):
    arrive = P.get("arrive", lambda key, after: None)
    ready = P.get("ready", lambda key, grads, dep: dep)
    sk = jnp.zeros((16, LANE), f32).at[:A_HEADS].set(jnp.broadcast_to(P["sinks"][:, None], (A_HEADS, LANE)))
    prm = jnp.zeros((8, LANE), f32).at[0, 6:12].set(P["a_log"]).at[1, 6:12].set(P["dt_bias"]).at[2].set(P["out_norm_g"])
    bias = bias_build(P["rel_bias"])
    saved = []
    h = x
    n1 = rms_fwd(h, P["g_mix"][0], "rms_mix0")
    for i in range(2):
        arrive(("w_in", i), n1)
        if i == 0:
            proj = mm_nn(n1, P["w_in_a"], out_dtype=bf16, name="proj_a")
        else:
            proj = mm_nn(n1, P["w_in_b"], name="proj_b")
        arrive(("w_mem", i), proj)
        kv = memkv_fwd(mem, P["g_mem"][i], P["w_mem"][i], f"memkv{i}")
        if i == 0:
            self_out = swa_fwd(proj, bias, sk)
            cross = xattn_fwd(proj, A_Q + 2 * LANE, kv, "xattn_a")
            extra = ()
        else:
            qkvn = dnprep_fwd(proj, P["conv_qkv"])
            chunked, inv = dnc_fwd(qkvn, proj, prm)
            o, states = dns_fwd(chunked)
            self_out = dnpost_fwd(o, proj, prm)
            cross = xattn_fwd(proj, 2304, kv, "xattn_b")
            extra = (qkvn, chunked, inv, states, o)
        mix = (self_out, cross)
        arrive(("w_out", i), cross)
        h2, n2 = mm_res_norm(mix, P["w_out"][i], h, P["g_ffn"][i], f"out_proj{i}")
        arrive(("w_gu", i), n2)
        gu = gu_fwd(n2, P["w_gu"][i], f"gate_up{i}")
        act, pre = glu_fwd(gu, P["ffn_cw"][i], P["ffn_cb"][i], f"glu{i}")
        arrive(("w_down", i), act)
        saved.append((h, n1, kv, proj, mix, h2, n2, gu, pre, act, extra))
        if i == 0:
            h, n1 = mm_res_norm(act, P["w_down"][i], h2, P["g_mix"][1], f"down{i}")
        else:
            h = mm_nn(act, P["w_down"][i], res=h2, name=f"down{i}")

    loss, dh, dg_fin = loss_head(h, P["g_fin"], target)
    G = {"g_fin": dg_fin[0], "g_mix": [None, None], "g_mem": [None, None], "g_ffn": [None, None],
         "w_mem": [None, None], "w_out": [None, None], "w_gu": [None, None], "w_down": [None, None],
         "ffn_cw": [None, None], "ffn_cb": [None, None]}
    for i in (1, 0):
        hin, n1, kv, proj, mix, h2, n2, gu, pre, act, extra = saved[i]
        dact = mm_nt(dh, P["w_down"][i], out_dtype=bf16, name=f"d_act{i}")
        G["w_down"][i] = mm_tn(act, dh, name=f"dw_down{i}")
        dgu, dcw, dcb = glu_bwd(gu, pre, P["ffn_cw"][i], P["ffn_cb"][i], dact, f"glu_bwd{i}")
        G["ffn_cw"][i], G["ffn_cb"][i] = dcw, dcb[0]
        G["w_gu"][i] = gu_bwd_w(n2, dgu, f"dw_gu{i}")
        g_ffn = ready(("ffn", i), G, P["g_ffn"][i])
        dh2, dg = gu_bwd_x(dgu, P["w_gu"][i], (h2, g_ffn, dh), f"d_n2_{i}")
        G["g_ffn"][i] = dg[0]
        dmix = mm_nt(dh2, P["w_out"][i], name=f"d_mix{i}")
        G["w_out"][i] = mm_tn(mix, dh2, name=f"dw_out{i}")
        if i == 0:
            dqkv, dbias, dsk = swa_bwd(proj, bias, sk, dmix)
            dxq, dkv = xattn_bwd(proj, A_Q + 2 * LANE, kv, dmix, "xattn_a_bwd")
            dproj = (dqkv, dxq)
            G["sinks"] = dsk[:A_HEADS, 0]
            G["rel_bias"] = bias_grad(dbias)[:, :A_HEADS]
            w_in, gname = P["w_in_a"], "w_in_a"
        else:
            qkvn, chunked, inv, states, o = extra
            do, dz, dgo = dnpost_bwd(o, proj, prm, dmix)
            dqkvn, dseg, dprm = dnc_bwd(qkvn, proj, prm, inv, dns_bwd(chunked, states, do))
            draw, dconv = dnprep_bwd(proj, P["conv_qkv"], dqkvn)
            dxq, dkv = xattn_bwd(proj, 2304, kv, dmix, "xattn_b_bwd")
            dproj = (draw, dz, dxq, dseg)
            G["conv_qkv"] = dconv
            G["a_log"], G["dt_bias"], G["out_norm_g"] = dprm[0, 6:12], dprm[1, 6:12], dgo[0]
            w_in, gname = P["w_in_b"], "w_in_b"
        G[gname] = mm_tn(n1, dproj, name=f"d{gname}")
        dh, dg = mm_nt_norm(dproj, w_in, (hin, P["g_mix"][i], dh2), f"d_n1_{i}")
        G["g_mix"][i] = dg[0]
        dgm, dwm = memkv_bwd(mem, P["g_mem"][i], P["w_mem"][i], dkv, f"memkv_bwd{i}")
        G["g_mem"][i], G["w_mem"][i] = dgm[0], dwm
        ready(("mix", i), G, None)
    return loss, dh, G


def _grads_to_ref(G):
    return {
        "rel_bias": G["rel_bias"], "norm_mix_g": jnp.stack(G["g_mix"]), "norm_mem_g": jnp.stack(G["g_mem"]),
        "w_mem_kv": jnp.stack(G["w_mem"]),
        "w_out": jnp.stack([_unlay_out_a(G["w_out"][0]), G["w_out"][1]]),
        "w_in_a": _unlay_in_a(G["w_in_a"])[None], "sinks_a": G["sinks"][None],
        "w_in_b": _unlay_in_b(G["w_in_b"])[None], "conv_qkv_b": G["conv_qkv"][None],
        "a_log_b": G["a_log"][None], "dt_bias_b": G["dt_bias"][None], "out_norm_g_b": G["out_norm_g"][None],
        "norm_ffn_g": jnp.stack(G["g_ffn"]),
        "w_gate_up": jnp.stack([_unchip_cols(G["w_gu"][0]), _unchip_cols(G["w_gu"][1])]).astype(f32),
        "ffn_conv_w": jnp.stack(G["ffn_cw"]), "ffn_conv_b": jnp.stack(G["ffn_cb"]),
        "w_down": jnp.stack(G["w_down"]), "final_norm_g": G["g_fin"],
    }


ANY = pl.BlockSpec(memory_space=pl.ANY)


def _place():
    return lax.axis_index("x"), lax.axis_index("y"), lax.axis_index("c")


def allreduce_small(buf):
    R = buf.shape[0]

    def body(b_ref, o_ref, recv, ssem, rsem):
        x, y, c = _place()
        me = 4 * x + 2 * y + c

        def peer(k):
            return (1 - x if k & 4 else x, 1 - y if k & 2 else y, 1 - c if k & 1 else c)

        def remote(k, slot):
            return pltpu.make_async_remote_copy(
                src_ref=b_ref, dst_ref=recv.at[slot], send_sem=ssem.at[k - 1], recv_sem=rsem.at[k - 1],
                device_id=peer(k), device_id_type=MESH)

        sends = [remote(k, me) for k in range(1, 8)]
        for cp in sends:
            cp.start()
        recv[me] = b_ref[...]
        for k in range(1, 8):
            px, py, pc = peer(k)
            remote(k, 4 * px + 2 * py + pc).wait_recv()
        for cp in sends:
            cp.wait_send()
        total = recv[0]
        for j in range(1, 8):
            total = total + recv[j]
        o_ref[...] = total

    return pl.pallas_call(
        body, name="small_allreduce",
        in_specs=[pl.BlockSpec(memory_space=pltpu.VMEM)], out_specs=pl.BlockSpec(memory_space=pltpu.VMEM),
        out_shape=jax.ShapeDtypeStruct(buf.shape, f32),
        scratch_shapes=[pltpu.VMEM((8, R, LANE), f32), pltpu.SemaphoreType.DMA((7,)), pltpu.SemaphoreType.DMA((7,))],
    )(buf)


def sum_slots(own, recv, chip, core, name):
    _, R, C = recv.shape
    tr = _row_tile(R, 256)
    nt = R // tr

    def body(p_ref, a_ref, r_ref, o_ref):
        acc = jnp.zeros((tr, C), f32)
        for s in range(4):
            acc = acc + jnp.where(p_ref[0] == s, a_ref[s], r_ref[s]).astype(f32)
        o_ref[...] = acc

    slots = pl.BlockSpec((4, tr, C), lambda i, p_ref: (0, i, 0))
    return pl.pallas_call(
        body, name=name, out_shape=jax.ShapeDtypeStruct((2 * R, C), f32),
        grid_spec=pltpu.PrefetchScalarGridSpec(
            num_scalar_prefetch=1, grid=(nt,), in_specs=[slots, slots],
            out_specs=pl.BlockSpec((tr, C), lambda i, p_ref: (p_ref[1] * nt + i, 0))),
        compiler_params=_cp(("parallel",)),
    )(jnp.stack([chip, core]).astype(jnp.int32), own, recv)


def _half(ref, core, axis=0):
    half = ref.shape[axis] // 2
    idx = (slice(None),) * axis + (pl.ds(core * half, half),)
    return ref.at[idx]


IN_HBM = pl.BlockSpec(memory_space=pltpu.HBM)
IN_SEM = pl.BlockSpec(memory_space=pltpu.SEMAPHORE)
SIDE_EFFECT = pltpu.SideEffectType.DATAFLOW_SIDE_EFFECTING


def _gather_copy(buf, i, k, ssem, rsem, place, landing):
    x, y, c = place
    px, py = [(1 - x, y), (x, 1 - y), (1 - x, 1 - y)][k]
    me = 2 * x + y
    return pltpu.make_async_remote_copy(
        src_ref=buf.at[me], dst_ref=buf.at[me if landing == "theirs" else 2 * px + py],
        send_sem=ssem.at[3 * i + k], recv_sem=rsem.at[3 * i + k], device_id=(px, py, c), device_id_type=MESH)


def gather_start(groups, name):
    flat = [b for grp in groups for b in grp]
    n, ng = len(flat), len(groups)

    def body(*refs):
        bufs, sems = refs[:n], refs[n:n + 2 * ng]
        place = _place()
        j = 0
        for g, grp in enumerate(groups):
            for i in range(len(grp)):
                for k in range(3):
                    _gather_copy(bufs[j], i, k, sems[2 * g], sems[2 * g + 1], place, "theirs").start()
                j += 1
        refs[-1][...] = jnp.zeros_like(refs[-1])

    sem_shapes = [pltpu.SemaphoreType.DMA((3 * len(grp),)) for grp in groups for _ in range(2)]
    out = pl.pallas_call(
        body, name=name, in_specs=[IN_HBM] * n,
        out_specs=(*[IN_SEM] * (2 * ng), *[IN_HBM] * n, pl.BlockSpec(memory_space=pltpu.VMEM)),
        out_shape=(*sem_shapes, *[pltpu.HBM(b.shape, b.dtype) for b in flat], jax.ShapeDtypeStruct((8, LANE), f32)),
        input_output_aliases={i: 2 * ng + i for i in range(n)},
        compiler_params=pltpu.CompilerParams(has_side_effects=SIDE_EFFECT),
    )(*[pltpu.with_memory_space_constraint(b, pltpu.HBM) for b in flat])
    sems, bufs = out[:2 * ng], list(out[2 * ng:2 * ng + n])
    flights, j = [], 0
    for g, grp in enumerate(groups):
        flights.append((bufs[j:j + len(grp)], sems[2 * g], sems[2 * g + 1]))
        j += len(grp)
    return flights, out[-1]


def gather_wait(flight, after, name):
    bufs, ssem, rsem = flight
    n = len(bufs)

    def body(*refs):
        place = _place()
        for i in range(n):
            for k in range(3):
                cp = _gather_copy(refs[i], i, k, refs[n], refs[n + 1], place, "mine")
                cp.wait_send()
                cp.wait_recv()

    return pl.pallas_call(
        body, name=name, in_specs=[IN_HBM] * n + [IN_SEM, IN_SEM, ANY], out_specs=[IN_HBM] * n,
        out_shape=[pltpu.HBM(b.shape, b.dtype) for b in bufs], input_output_aliases={i: i for i in range(n)},
        compiler_params=pltpu.CompilerParams(has_side_effects=SIDE_EFFECT),
    )(*bufs, ssem, rsem, after)


def _scatter_copy(src, land, j, k, ssem, rsem, place, landing):
    x, y, c = place
    px, py = [(1 - x, y), (x, 1 - y), (1 - x, 1 - y)][k]
    return pltpu.make_async_remote_copy(
        src_ref=src.at[2 * px + py], dst_ref=land.at[2 * x + y if landing == "theirs" else 2 * px + py],
        send_sem=ssem.at[3 * j + k], recv_sem=rsem.at[3 * j + k], device_id=(px, py, c), device_id_type=MESH)


def scatter_start(srcs, name):
    n = len(srcs)
    lands = [lax.empty(g.shape, g.dtype) for g in srcs]

    def body(*refs):
        place = _place()
        for j in range(n):
            for k in range(3):
                _scatter_copy(refs[j], refs[n + j], j, k, refs[2 * n], refs[2 * n + 1], place, "theirs").start()
        refs[-1][...] = jnp.zeros_like(refs[-1])

    sem = pltpu.SemaphoreType.DMA((3 * n,))
    hbm = [pltpu.with_memory_space_constraint(b, pltpu.HBM) for b in list(srcs) + lands]
    out = pl.pallas_call(
        body, name=name, in_specs=[IN_HBM] * (2 * n),
        out_specs=(IN_SEM, IN_SEM, *[IN_HBM] * (2 * n), pl.BlockSpec(memory_space=pltpu.VMEM)),
        out_shape=(sem, sem, *[pltpu.HBM(b.shape, b.dtype) for b in hbm], jax.ShapeDtypeStruct((8, LANE), f32)),
        input_output_aliases={i: 2 + i for i in range(2 * n)},
        compiler_params=pltpu.CompilerParams(has_side_effects=SIDE_EFFECT),
    )(*hbm)
    return (list(out[2:2 + n]), list(out[2 + n:2 + 2 * n]), out[0], out[1]), out[-1]


def scatter_wait(flight, after, name):
    srcs, lands, ssem, rsem = flight
    n = len(srcs)

    def body(*refs):
        place = _place()
        for j in range(n):
            for k in range(3):
                cp = _scatter_copy(refs[j], refs[n + j], j, k, refs[2 * n], refs[2 * n + 1], place, "mine")
                cp.wait_send()
                cp.wait_recv()

    out = pl.pallas_call(
        body, name=name, in_specs=[IN_HBM] * (2 * n) + [IN_SEM, IN_SEM, ANY], out_specs=[IN_HBM] * (2 * n),
        out_shape=[pltpu.HBM(b.shape, b.dtype) for b in list(srcs) + list(lands)],
        input_output_aliases={i: i for i in range(2 * n)},
        compiler_params=pltpu.CompilerParams(has_side_effects=SIDE_EFFECT),
    )(*srcs, *lands, ssem, rsem, after)
    return list(out[:n]), list(out[n:])


def pair_exchange(gbufs, name):
    n = len(gbufs)

    def body(*refs):
        ins, outs = refs[:n], refs[n:2 * n]
        ssem, rsem = refs[2 * n:]
        x, y, c = _place()
        cps = [pltpu.make_async_remote_copy(
            src_ref=_half(ins[j], 1 - c, axis=1), dst_ref=outs[j], send_sem=ssem.at[j], recv_sem=rsem.at[j],
            device_id=(x, y, 1 - c), device_id_type=MESH) for j in range(n)]
        for cp in cps:
            cp.start()
        for cp in cps:
            cp.wait()

    return pl.pallas_call(
        body, name=name, in_specs=[ANY] * n, out_specs=[ANY] * n,
        out_shape=[jax.ShapeDtypeStruct((4, g.shape[1] // 2, g.shape[2]), g.dtype) for g in gbufs],
        scratch_shapes=[pltpu.SemaphoreType.DMA((n,)), pltpu.SemaphoreType.DMA((n,))],
    )(*gbufs)


def _row_tile(rows, cap=512):
    return max(t for t in range(16, min(rows, cap) + 1, 16) if rows % t == 0)


def pair_sum(mine, theirs, core, name):
    _, R, C = mine.shape
    half = R // 2
    tr = _row_tile(half)
    nt = half // tr

    def body(c_ref, a_ref, b_ref, o_ref):
        o_ref[...] = (a_ref[...].astype(f32) + b_ref[...].astype(f32)).astype(bf16)

    return pl.pallas_call(
        body, name=name, out_shape=jax.ShapeDtypeStruct(theirs.shape, bf16),
        grid_spec=pltpu.PrefetchScalarGridSpec(
            num_scalar_prefetch=1, grid=(4, nt),
            in_specs=[pl.BlockSpec((None, tr, C), lambda s, i, c_ref: (s, c_ref[0] * nt + i, 0)),
                      pl.BlockSpec((None, tr, C), lambda s, i, c_ref: (s, i, 0))],
            out_specs=pl.BlockSpec((None, tr, C), lambda s, i, c_ref: (s, i, 0))),
        compiler_params=_cp(("parallel", "parallel")),
    )(jnp.reshape(core, (1,)).astype(jnp.int32), mine, theirs)


def final_exchange(fins):
    n = len(fins)

    def body(*refs):
        outs = refs[n:2 * n]
        ssem, rsem = refs[2 * n:]
        x, y, c = _place()
        cps = [pltpu.make_async_remote_copy(
            src_ref=_half(outs[j], c), dst_ref=_half(outs[j], c), send_sem=ssem.at[j], recv_sem=rsem.at[j],
            device_id=(x, y, 1 - c), device_id_type=MESH) for j in range(n)]
        for cp in cps:
            cp.start()
        for cp in cps:
            cp.wait()

    return pl.pallas_call(
        body, name="final_exchange", in_specs=[ANY] * n, out_specs=[ANY] * n,
        out_shape=[jax.ShapeDtypeStruct(f.shape, f.dtype) for f in fins],
        input_output_aliases={j: j for j in range(n)},
        scratch_shapes=[pltpu.SemaphoreType.DMA((n,)), pltpu.SemaphoreType.DMA((n,))],
    )(*fins)


def adamw_big(w, m, v, gs, row0, name):
    L, R, C = w.shape
    tr = _row_tile(math.gcd(R, row0) if row0 else R, max(16, 262144 // C // 16 * 16))
    b0 = row0 // tr

    def body(*refs):
        w_ref, m_ref, v_ref = refs[:3]
        g_refs = refs[3:3 + L]
        g_ref, d_ref, nm_ref, nv_ref = refs[3 + L:]
        g = g_refs[0][...]
        for l in range(1, L):
            g = jnp.where(pl.program_id(0) == l, g_refs[l][...], g)
        d, nm, nv = _adamw_math(w_ref[...], g, m_ref[...], v_ref[...])
        g_ref[...] = g
        d_ref[...] = d
        nm_ref[...] = nm
        nv_ref[...] = nv

    own = pl.BlockSpec((None, tr, C), lambda l, i: (l, i, 0))
    off = pl.BlockSpec((tr, C), lambda l, i: (b0 + i, 0))
    return pl.pallas_call(
        body, name=name, grid=(L, R // tr), in_specs=[own, own, own] + [off] * L, out_specs=[own] * 4,
        out_shape=[jax.ShapeDtypeStruct((L, R, C), f32)] * 4, compiler_params=_cp(("parallel", "parallel")),
    )(w, m, v, *gs)


def _adamw_math(w, g, m, v):
    m = B1 * m + (1.0 - B1) * g
    v = B2 * v + (1.0 - B2) * (g * g)
    m_hat = m / (1.0 - B1 ** STEP)
    v_hat = v / (1.0 - B2 ** STEP)
    delta = -LR * (m_hat / (jnp.sqrt(v_hat) + AEPS) + WD * w)
    return delta, m, v


def adamw_small(w, m, v, g):
    def body(w_ref, m_ref, v_ref, g_ref, d_ref, nm_ref, nv_ref):
        d, nm, nv = _adamw_math(w_ref[...], g_ref[...], m_ref[...], v_ref[...])
        d_ref[...] = d
        nm_ref[...] = nm
        nv_ref[...] = nv

    return pl.pallas_call(body, name="adamw_small", out_shape=[jax.ShapeDtypeStruct(w.shape, f32)] * 3)(w, m, v, g)


CONV =(("conv_qkv_b", 2), ("ffn_conv_w", 2))
SMALL = ("rel_bias", "norm_mix_g", "norm_mem_g", "sinks_a", "a_log_b", "dt_bias_b", "out_norm_g_b", "norm_ffn_g",
         "ffn_conv_b", "final_norm_g")
WEIGHTS = ("rel_bias", "norm_mix_g", "norm_mem_g", "w_mem_kv", "w_out", "w_in_a", "sinks_a", "w_in_b", "conv_qkv_b",
           "a_log_b", "dt_bias_b", "out_norm_g_b", "norm_ffn_g", "w_gate_up", "ffn_conv_w", "ffn_conv_b", "w_down",
           "final_norm_g")
ARGS = ("x", "mem") + WEIGHTS + ("loss_target",) + tuple("m_" + n for n in WEIGHTS) + tuple("v_" + n for n in WEIGHTS)


def _rows(a, width):
    flat = a.reshape(-1)
    pad = (-flat.shape[0]) % (8 * width)
    if pad:
        flat = jnp.concatenate([flat, jnp.zeros((pad,), a.dtype)])
    return flat.reshape(-1, width)


def _nrows(shape, width):
    return _pad_to(-(-math.prod(shape) // width), 8)


def _pack(arrs, width, total_rows, dtype):
    parts = [_rows(a.astype(dtype), width) for a in arrs]
    used = sum(p.shape[0] for p in parts)
    if total_rows > used:
        parts.append(jnp.zeros((total_rows - used, width), dtype))
    return jnp.concatenate(parts, axis=0)


def _unpack(buf, shapes, width):
    out, r = [], 0
    for s in shapes:
        n = _nrows(s, width)
        out.append(buf[r:r + n].reshape(-1)[:math.prod(s)].reshape(s))
        r += n
    return out


def _pad_to(n, mult):
    return -(-n // mult) * mult


def kernel(x, mem, rel_bias, norm_mix_g, norm_mem_g, w_mem_kv, w_out, w_in_a, sinks_a, w_in_b, conv_qkv_b, a_log_b, dt_bias_b, out_norm_g_b, norm_ffn_g, w_gate_up, ffn_conv_w, ffn_conv_b, w_down, final_norm_g, loss_target, m_rel_bias, m_norm_mix_g, m_norm_mem_g, m_w_mem_kv, m_w_out, m_w_in_a, m_sinks_a, m_w_in_b, m_conv_qkv_b, m_a_log_b, m_dt_bias_b, m_out_norm_g_b, m_norm_ffn_g, m_w_gate_up, m_ffn_conv_w, m_ffn_conv_b, m_w_down, m_final_norm_g, v_rel_bias, v_norm_mix_g, v_norm_mem_g, v_w_mem_kv, v_w_out, v_w_in_a, v_sinks_a, v_w_in_b, v_conv_qkv_b, v_a_log_b, v_dt_bias_b, v_out_norm_g_b, v_norm_ffn_g, v_w_gate_up, v_ffn_conv_w, v_ffn_conv_b, v_w_down, v_final_norm_g):
    A = dict(zip(ARGS, (x, mem, rel_bias, norm_mix_g, norm_mem_g, w_mem_kv, w_out, w_in_a, sinks_a, w_in_b, conv_qkv_b, a_log_b, dt_bias_b, out_norm_g_b, norm_ffn_g, w_gate_up, ffn_conv_w, ffn_conv_b, w_down, final_norm_g, loss_target, m_rel_bias, m_norm_mix_g, m_norm_mem_g, m_w_mem_kv, m_w_out, m_w_in_a, m_sinks_a, m_w_in_b, m_conv_qkv_b, m_a_log_b, m_dt_bias_b, m_out_norm_g_b, m_norm_ffn_g, m_w_gate_up, m_ffn_conv_w, m_ffn_conv_b, m_w_down, m_final_norm_g, v_rel_bias, v_norm_mix_g, v_norm_mem_g, v_w_mem_kv, v_w_out, v_w_in_a, v_sinks_a, v_w_in_b, v_conv_qkv_b, v_a_log_b, v_dt_bias_b, v_out_norm_g_b, v_norm_ffn_g, v_w_gate_up, v_ffn_conv_w, v_ffn_conv_b, v_w_down, v_final_norm_g)))
    chip = 2 * lax.axis_index("x") + lax.axis_index("y")
    core = lax.axis_index("c")

    def own_slot(shard):
        return lax.dynamic_update_index_in_dim(lax.empty((4,) + shard.shape, shard.dtype), shard, chip, 0)

    def bslot(w, tie=0.0):
        return own_slot((w + tie).astype(bf16))

    early = {
        ("w_in", 0): [bslot(w_in_a[0])],
        ("w_mem", 0): [bslot(w_mem_kv[0]), own_slot(ffn_conv_w.reshape(6, -1))],
        ("w_out", 0): [bslot(w_out[0])],
    }
    flights_a, gone = gather_start(list(early.values()), "gather_start_first")
    z = gone[0, 0]
    late = {
        ("w_gu", 0): [bslot(w_gate_up[0], z)], ("w_down", 0): [bslot(w_down[0], z)],
        ("w_in", 1): [bslot(w_in_b[0], z)], ("w_mem", 1): [bslot(w_mem_kv[1], z), own_slot(conv_qkv_b[0] + z)],
        ("w_out", 1): [bslot(w_out[1], z)], ("w_gu", 1): [bslot(w_gate_up[1], z)], ("w_down", 1): [bslot(w_down[1], z)],
    }
    flights_b, started_all = gather_start(list(late.values()), "gather_start_rest")
    flights = dict(zip(list(early) + list(late), flights_a + flights_b))
    P = {"rel_bias": rel_bias, "sinks": sinks_a[0], "a_log": a_log_b[0], "dt_bias": dt_bias_b[0],
         "out_norm_g": out_norm_g_b[0], "g_mix": norm_mix_g, "g_mem": norm_mem_g, "g_ffn": norm_ffn_g,
         "g_fin": final_norm_g, "ffn_cb": [ffn_conv_b[0], ffn_conv_b[1]], "w_mem": [None, None], "w_out": [None, None],
         "w_gu": [None, None], "w_down": [None, None], "ffn_cw": [None, None]}

    def rows4(g):
        return g.reshape(4 * g.shape[1], g.shape[2])

    def arrive(key, after):
        if key not in flights:
            return
        got = gather_wait(flights.pop(key), started_all if key == ("w_in", 0) else after, "gather_wait_%s%d" % key)
        name, i = key
        if name == "w_in":
            P["w_in_a" if i == 0 else "w_in_b"] = (_lay_in_a if i == 0 else _lay_in_b)(_unchip_cols(got[0]))
        elif name == "w_mem":
            P["w_mem"][i] = rows4(got[0])
            if i == 0:
                cw = _unchip_cols(got[1]).reshape(2, 3, D_FF)
                P["ffn_cw"] = [cw[0], cw[1]]
            else:
                P["conv_qkv"] = _unchip_cols(got[1])
        elif name == "w_out":
            P["w_out"][i] = _lay_out_a(rows4(got[0])) if i == 0 else rows4(got[0])
        elif name == "w_gu":
            P["w_gu"][i] = got[0]
        else:
            P["w_down"][i] = rows4(got[0])

    def chip_rows(g):
        return g.reshape(4, g.shape[0] // 4, g.shape[-1])

    sent, started = {}, []

    def ready(key, G, dep):
        kind, i = key
        tag = "%s%d" % key
        if kind == "ffn":
            names, partial = ("gu", "down"), [G["w_gu"][i], chip_rows(G["w_down"][i]).astype(bf16)]
        else:
            g_out = _unlay_out_a(G["w_out"][0]) if i == 0 else G["w_out"][1]
            g_in = _unlay_in_a(G["w_in_a"]) if i == 0 else _unlay_in_b(G["w_in_b"])
            names = ("out", "in", "mem")
            partial = [chip_rows(g_out).astype(bf16), _chip_cols(g_in).astype(bf16), chip_rows(G["w_mem"][i]).astype(bf16)]
        theirs = pair_exchange(partial, "pair_exchange_" + tag)
        pair = [pair_sum(p, t, core, "pair_sum_%s%d" % (nm, i)) for p, t, nm in zip(partial, theirs, names)]
        flight, token = scatter_start(pair, "scatter_start_" + tag)
        sent[key] = (names, flight, token)
        started.append(token[0, 0])
        if dep is not None:
            while started:
                dep = dep + started.pop()
        return dep

    P["arrive"], P["ready"] = arrive, ready

    loss, dx, G = _local_step(x[0], mem[0], loss_target[0], P)
    gfull = _grads_to_ref(G)

    fin, after = {}, sent["mix", 0][2]
    for key in (("ffn", 1), ("mix", 1), ("ffn", 0), ("mix", 0)):
        names, flight, _ = sent[key]
        pair, arrived = scatter_wait(flight, after, "scatter_wait_%s%d" % key)
        for nm, p, r in zip(names, pair, arrived):
            after = fin[nm, key[1]] = sum_slots(p, r, chip, core, "sum_slots_%s%d" % (nm, key[1]))
    order = list(fin)
    done = dict(zip(order, final_exchange([fin[k] for k in order])))

    sm_shapes = [A[n].shape for n in SMALL] + [gfull[n].shape for n, _ in CONV] + [(LANE,)]
    sm_rows = _pad_to(sum(_nrows(s, LANE) for s in sm_shapes), 8)
    sbuf = _pack([gfull[n] for n in SMALL] + [gfull[n] for n, _ in CONV] + [loss[0]], LANE, sm_rows, f32)
    tot = _unpack(allreduce_small(sbuf), sm_shapes, LANE)
    gsmall = dict(zip(SMALL, tot[:len(SMALL)]))
    for (n, axis), t in zip(CONV, tot[len(SMALL):len(SMALL) + len(CONV)]):
        sh = A[n].shape[axis]
        gsmall[n] = lax.dynamic_slice_in_dim(t, chip * sh, sh, axis)
    loss_out = tot[-1][0]

    out = {}
    plan = (("w_gate_up", [done["gu", 0], done["gu", 1]]), ("w_down", [done["down", 0], done["down", 1]]),
            ("w_out", [done["out", 0], done["out", 1]]), ("w_mem_kv", [done["mem", 0], done["mem", 1]]),
            ("w_in_a", [done["in", 0]]), ("w_in_b", [done["in", 1]]))
    for n, gs in plan:
        shape3 = (len(gs),) + gs[0].shape
        res = adamw_big(A[n].reshape(shape3), A["m_" + n].reshape(shape3), A["v_" + n].reshape(shape3), gs, 0,
                        "adamw_" + n)
        for key, r in zip(("grad_", "delta_", "new_m_", "new_v_"), res):
            out[key + n] = r.reshape(A[n].shape)
    names = SMALL + tuple(n for n, _ in CONV)
    shapes = [A[n].shape for n in names]
    rows = _pad_to(sum(_nrows(s, LANE) for s in shapes), 8)
    packs = [_pack([src[n] for n in names], LANE, rows, f32)
             for src in ({n: A[n] for n in names}, {n: A["m_" + n] for n in names}, {n: A["v_" + n] for n in names}, gsmall)]
    res = adamw_small(*packs)
    for key, r in zip(("delta_", "new_m_", "new_v_"), res):
        for n, a in zip(names, _unpack(r, shapes, LANE)):
            out[key + n] = a
    for n in names:
        out["grad_" + n] = gsmall[n]
    return (loss_out, dx[None], *[out["grad_" + n] for n in WEIGHTS], *[out["delta_" + n] for n in WEIGHTS],
            *[out["new_m_" + n] for n in WEIGHTS], *[out["new_v_" + n] for n in WEIGHTS])
```

```python
import functools
import math

import numpy as np
import jax
import jax.numpy as jnp
from jax import lax
from jax.experimental import pallas as pl
from jax.experimental.pallas import tpu as pltpu

f32 = jnp.float32
bf16 = jnp.bfloat16
HI = lax.Precision.HIGHEST
MESH = pl.DeviceIdType.MESH

D = 1024
MEM_LEN = 256
EPS = 1e-6
A_HEADS, A_KV, A_DH = 12, 2, 64
A_Q = 768
BLK = 128
N_BUCKETS, MAX_DIST = 32, 128
B_QK, B_V, B_DH = 384, 768, 128
B_QKV = 1536
CHUNK = 64
X_Q = 256
D_FF = 2816
LANE = 128
VMEM_LIMIT = 56 * 1024 * 1024
MM_ROWS = 1024

LR, B1, B2, AEPS, WD, STEP = 0.001, 0.9, 0.999, 1e-08, 0.01, 10


def _cp(sem=None):
    return pltpu.CompilerParams(dimension_semantics=sem, vmem_limit_bytes=VMEM_LIMIT)


def _dg(a, b, ca, cb, prec=None):
    return lax.dot_general(a, b, (((ca,), (cb,)), ((), ())), precision=prec, preferred_element_type=f32)


@jax.custom_vjp
def bdot(a, b):
    return _dg(a.astype(bf16), b.astype(bf16), 1, 0)


def _bdot_f(a, b):
    return bdot(a, b), (a, b)


def _bdot_b(res, g):
    a, b = res
    gb = g.astype(bf16)
    return _dg(gb, b.astype(bf16), 1, 1), _dg(a.astype(bf16), gb, 0, 0)


bdot.defvjp(_bdot_f, _bdot_b)


@jax.custom_vjp
def bdot_nt(a, b):
    return _dg(a.astype(bf16), b.astype(bf16), 1, 1)


def _bdot_nt_f(a, b):
    return bdot_nt(a, b), (a, b)


def _bdot_nt_b(res, g):
    a, b = res
    gb = g.astype(bf16)
    return _dg(gb, b.astype(bf16), 1, 0), _dg(gb, a.astype(bf16), 0, 0)


bdot_nt.defvjp(_bdot_nt_f, _bdot_nt_b)


def _shift_rows(x, s, down):
    n = x.shape[0]
    row = lax.broadcasted_iota(jnp.int32, x.shape, 0)
    if down:
        return jnp.where(row >= s, pltpu.roll(x, s, 0), 0.0)
    return jnp.where(row < n - s, pltpu.roll(x, n - s, 0), 0.0)


@functools.partial(jax.custom_vjp, nondiff_argnums=(1,))
def shift_down(x, s):
    return _shift_rows(x, s, True)


def _sd_f(x, s):
    return _shift_rows(x, s, True), None


def _sd_b(s, _, g):
    return (_shift_rows(g, s, False),)


shift_down.defvjp(_sd_f, _sd_b)


def _sigmoid(x):
    return 1.0 / (1.0 + jnp.exp(-x))


def _silu(x):
    return x * _sigmoid(x)


def _rms(x, g):
    return x * lax.rsqrt(jnp.mean(x * x, axis=-1, keepdims=True) + EPS) * g


def _tile(n, cap):
    u = n // LANE
    best = 1
    for d in range(1, u + 1):
        if u % d == 0 and d * LANE <= cap:
            best = d
    return best * LANE


def mm_nn(a, w, res=None, out_dtype=f32, name="mm_nn"):
    M, K = a.shape
    N = w.shape[1]
    tm, tn = min(MM_ROWS, M), _tile(N, 1024)

    def body(*refs):
        if res is None:
            a_ref, w_ref, o_ref = refs
            o_ref[...] = _dg(a_ref[...].astype(bf16), w_ref[...], 1, 0).astype(out_dtype)
        else:
            a_ref, w_ref, r_ref, o_ref = refs
            o_ref[...] = (r_ref[...] + _dg(a_ref[...].astype(bf16), w_ref[...], 1, 0)).astype(out_dtype)

    in_specs = [pl.BlockSpec((tm, K), lambda n, m: (m, 0)), pl.BlockSpec((K, tn), lambda n, m: (0, n))]
    args = [a, w]
    if res is not None:
        in_specs.append(pl.BlockSpec((tm, tn), lambda n, m: (m, n)))
        args.append(res)
    return pl.pallas_call(
        body, name=name, grid=(N // tn, M // tm), in_specs=in_specs,
        out_specs=pl.BlockSpec((tm, tn), lambda n, m: (m, n)),
        out_shape=jax.ShapeDtypeStruct((M, N), out_dtype),
        compiler_params=_cp(("parallel", "parallel")),
    )(*args)


def mm_res_norm(a, w, res, g, name):
    pieces = a if isinstance(a, tuple) else (a,)
    na = len(pieces)
    M, K = pieces[0].shape[0], sum(p.shape[1] for p in pieces)
    tm = min(MM_ROWS, M)

    def body(*refs):
        w_ref, r_ref, g_ref, o_ref, n_ref = refs[na:]
        h = r_ref[...] + _dg(_cols(refs[:na]).astype(bf16), w_ref[...], 1, 0)
        o_ref[...] = h
        n_ref[...] = _rms(h, g_ref[...]).astype(bf16)

    tok = pl.BlockSpec((tm, D), lambda m: (m, 0))
    return pl.pallas_call(
        body, name=name, grid=(M // tm,),
        in_specs=[pl.BlockSpec((tm, p.shape[1]), lambda m: (m, 0)) for p in pieces]
        + [pl.BlockSpec((K, D), lambda m: (0, 0)), tok, pl.BlockSpec((1, D), lambda m: (0, 0))],
        out_specs=[tok, tok],
        out_shape=[jax.ShapeDtypeStruct((M, D), f32), jax.ShapeDtypeStruct((M, D), bf16)],
        compiler_params=_cp(("parallel",)),
    )(*pieces, w, res, g.reshape(1, D))


def mm_nt(dy, w, out_dtype=f32, name="mm_nt"):
    M, N = dy.shape
    K = w.shape[0]
    tm, tn = min(MM_ROWS, M), _tile(N, 1024)
    assert out_dtype == f32 or tn == N

    def body(dy_ref, w_ref, o_ref):
        part = _dg(dy_ref[...].astype(bf16), w_ref[...], 1, 1)
        if tn == N:
            o_ref[...] = part.astype(out_dtype)
        else:
            @pl.when(pl.program_id(1) == 0)
            def _():
                o_ref[...] = jnp.zeros_like(o_ref)
            o_ref[...] += part

    return pl.pallas_call(
        body, name=name, grid=(M // tm, N // tn),
        in_specs=[pl.BlockSpec((tm, tn), lambda m, n: (m, n)), pl.BlockSpec((K, tn), lambda m, n: (0, n))],
        out_specs=pl.BlockSpec((tm, K), lambda m, n: (m, 0)),
        out_shape=jax.ShapeDtypeStruct((M, K), out_dtype),
        compiler_params=_cp(("parallel", "arbitrary")),
    )(dy, w)


NORM_ROWS = 1024


def _acc_then_norm_bwd(part, steps, h_ref, g_ref, r_ref, o_ref, dg_ref):
    k = pl.program_id(1)

    @pl.when((pl.program_id(0) == 0) & (k == 0))
    def _():
        dg_ref[...] = jnp.zeros_like(dg_ref)

    @pl.when(k == 0)
    def _():
        o_ref[...] = part

    @pl.when(k > 0)
    def _():
        o_ref[...] += part

    @pl.when(k == steps - 1)
    def _():
        _, vjp = jax.vjp(_rms, h_ref[...], g_ref[...])
        dh, dg = vjp(o_ref[...])
        o_ref[...] = r_ref[...] + dh
        dg_ref[...] += dg


def mm_nt_norm(dy, w, norm, name):
    pieces = dy if isinstance(dy, tuple) else (dy,)
    nd = len(pieces)
    M, N = pieces[0].shape[0], sum(p.shape[1] for p in pieces)
    tm, tn = (min(NORM_ROWS, M), _tile(N, 1024)) if nd == 1 else (min(512, M), N)

    def body(*refs):
        w_ref, h_ref, g_ref, r_ref, o_ref, dg_ref = refs[nd:]
        _acc_then_norm_bwd(_dg(_cols(refs[:nd]).astype(bf16), w_ref[...], 1, 1), N // tn, h_ref, g_ref, r_ref, o_ref,
                           dg_ref)

    tok = pl.BlockSpec((tm, D), lambda m, n: (m, 0))
    vec = pl.BlockSpec((1, D), lambda m, n: (0, 0))
    return pl.pallas_call(
        body, name=name, grid=(M // tm, N // tn),
        in_specs=[pl.BlockSpec((tm, tn if nd == 1 else p.shape[1]), lambda m, n: (m, n)) for p in pieces]
        + [pl.BlockSpec((D, tn), lambda m, n: (0, n)), tok, vec, tok],
        out_specs=[tok, vec],
        out_shape=[jax.ShapeDtypeStruct((M, D), f32), jax.ShapeDtypeStruct((1, D), f32)],
        compiler_params=_cp(("arbitrary", "arbitrary")),
    )(*pieces, w, norm[0], norm[1].reshape(1, D), norm[2])


def _cols(refs):
    return refs[0][...] if len(refs) == 1 else jnp.concatenate([r[...] for r in refs], axis=1)


def mm_tn(a, dy, name="mm_tn"):
    pieces = a if isinstance(a, tuple) else (a,)
    dpieces = dy if isinstance(dy, tuple) else (dy,)
    na, nd = len(pieces), len(dpieces)
    M, K = pieces[0].shape[0], sum(p.shape[1] for p in pieces)
    N = sum(p.shape[1] for p in dpieces)
    tm = min(MM_ROWS, M)
    tk = _tile(K, 1408) if na == 1 else K
    tn = _tile(N, 1024) if nd == 1 else N

    def body(*refs):
        o_ref = refs[-1]

        @pl.when(pl.program_id(2) == 0)
        def _():
            o_ref[...] = jnp.zeros_like(o_ref)
        o_ref[...] += _dg(_cols(refs[:na]).astype(bf16), _cols(refs[na:na + nd]).astype(bf16), 0, 0)

    a_specs = [pl.BlockSpec((tm, tk if na == 1 else p.shape[1]), lambda k, n, m: (m, k)) for p in pieces]
    d_specs = [pl.BlockSpec((tm, tn if nd == 1 else p.shape[1]), lambda k, n, m: (m, n)) for p in dpieces]
    return pl.pallas_call(
        body, name=name, grid=(K // tk, N // tn, M // tm), in_specs=a_specs + d_specs,
        out_specs=pl.BlockSpec((tk, tn), lambda k, n, m: (k, n)),
        out_shape=jax.ShapeDtypeStruct((K, N), f32),
        compiler_params=_cp(("parallel", "parallel", "arbitrary")),
    )(*pieces, *dpieces)


def rms_fwd(h, g, name):
    S = h.shape[0]
    t = min(512, S)

    def body(h_ref, g_ref, o_ref):
        o_ref[...] = _rms(h_ref[...], g_ref[...]).astype(bf16)

    return pl.pallas_call(
        body, name=name, grid=(S // t,),
        in_specs=[pl.BlockSpec((t, D), lambda i: (i, 0)), pl.BlockSpec((1, D), lambda i: (0, 0))],
        out_specs=pl.BlockSpec((t, D), lambda i: (i, 0)),
        out_shape=jax.ShapeDtypeStruct((S, D), bf16),
        compiler_params=_cp(("parallel",)),
    )(h, g.reshape(1, D))


def loss_head(h, g, target):
    S = h.shape[0]
    t = min(512, S)

    def f(hh, gg, tt):
        err = _rms(hh, gg) - tt
        return 0.5 * jnp.sum(jnp.mean(err * err, axis=-1, keepdims=True), axis=0, keepdims=True)

    def body(h_ref, g_ref, t_ref, loss_ref, dh_ref, dg_ref):
        @pl.when(pl.program_id(0) == 0)
        def _():
            dg_ref[...] = jnp.zeros_like(dg_ref)
            loss_ref[...] = jnp.zeros_like(loss_ref)
        val, vjp = jax.vjp(lambda a, b: f(a, b, t_ref[...]), h_ref[...], g_ref[...])
        dh, dg = vjp(jnp.ones((1, 1), f32))
        dh_ref[...] = dh
        dg_ref[...] += dg
        loss_ref[...] += jnp.broadcast_to(val, loss_ref.shape)

    tok = pl.BlockSpec((t, D), lambda i: (i, 0))
    vec = pl.BlockSpec((1, D), lambda i: (0, 0))
    return pl.pallas_call(
        body, name="loss_head", grid=(S // t,), in_specs=[tok, vec, tok],
        out_specs=[pl.BlockSpec((1, LANE), lambda i: (0, 0)), tok, vec],
        out_shape=[jax.ShapeDtypeStruct((1, LANE), f32), jax.ShapeDtypeStruct((S, D), f32),
                   jax.ShapeDtypeStruct((1, D), f32)],
        compiler_params=_cp(("arbitrary",)),
    )(h, g.reshape(1, D), target)


def memkv_fwd(mem, g, w, name):
    def body(m_ref, g_ref, w_ref, o_ref):
        o_ref[...] = _dg(_rms(m_ref[...], g_ref[...]).astype(bf16), w_ref[...], 1, 0)

    return pl.pallas_call(
        body, name=name, out_shape=jax.ShapeDtypeStruct((MEM_LEN, 2 * X_Q), f32), compiler_params=_cp(),
    )(mem, g.reshape(1, D), w)


def memkv_bwd(mem, g, w, dkv, name):
    def body(m_ref, g_ref, w_ref, d_ref, dg_ref, dw_ref):
        n, vjp = jax.vjp(lambda gg: _rms(m_ref[...], gg), g_ref[...])
        db = d_ref[...].astype(bf16)
        dw_ref[...] = _dg(n.astype(bf16), db, 0, 0)
        dg_ref[...] = vjp(_dg(db, w_ref[...], 1, 1))[0]

    return pl.pallas_call(
        body, name=name,
        out_shape=[jax.ShapeDtypeStruct((1, D), f32), jax.ShapeDtypeStruct((D, 2 * X_Q), f32)],
        compiler_params=_cp(),
    )(mem, g.reshape(1, D), w, dkv)


def _xattn_f(xq, mk, mv):
    lane = lax.broadcasted_iota(jnp.int32, (1, X_Q), 1)
    out = jnp.zeros(xq.shape, f32)
    for hd in range(4):
        msk = (lane // 64 == hd).astype(f32)
        s = bdot_nt(xq * msk, mk) * (64 ** -0.5)
        m = lax.stop_gradient(jnp.max(s, axis=-1, keepdims=True))
        p = jnp.exp(s - m)
        p = p / jnp.sum(p, axis=-1, keepdims=True)
        out = out + bdot(p, mv * msk)
    return out


def xattn_fwd(proj, col, kv, name):
    S = proj.shape[0]
    t = min(512, S)
    cb = col // X_Q

    def body(q_ref, k_ref, v_ref, o_ref):
        o_ref[...] = _xattn_f(q_ref[...].astype(f32), k_ref[...], v_ref[...]).astype(bf16)

    return pl.pallas_call(
        body, name=name, grid=(S // t,),
        in_specs=[pl.BlockSpec((t, X_Q), lambda i: (i, cb)), pl.BlockSpec((MEM_LEN, X_Q), lambda i: (0, 0)),
                  pl.BlockSpec((MEM_LEN, X_Q), lambda i: (0, 1))],
        out_specs=pl.BlockSpec((t, X_Q), lambda i: (i, 0)),
        out_shape=jax.ShapeDtypeStruct((S, X_Q), bf16),
        compiler_params=_cp(("parallel",)),
    )(proj, kv, kv)


def xattn_bwd(proj, col, kv, dmix, name):
    S = proj.shape[0]
    t = min(512, S)
    cb = col // X_Q

    def body(q_ref, k_ref, v_ref, do_ref, dq_ref, dk_ref, dv_ref):
        @pl.when(pl.program_id(0) == 0)
        def _():
            dk_ref[...] = jnp.zeros_like(dk_ref)
            dv_ref[...] = jnp.zeros_like(dv_ref)
        _, vjp = jax.vjp(_xattn_f, q_ref[...].astype(f32), k_ref[...], v_ref[...])
        dq, dk, dv = vjp(do_ref[...])
        dq_ref[...] = dq.astype(bf16)
        dk_ref[...] += dk
        dv_ref[...] += dv

    kvb = pl.BlockSpec((MEM_LEN, X_Q), lambda i: (0, 0))
    dq, dk, dv = pl.pallas_call(
        body, name=name, grid=(S // t,),
        in_specs=[pl.BlockSpec((t, X_Q), lambda i: (i, cb)), kvb,
                  pl.BlockSpec((MEM_LEN, X_Q), lambda i: (0, 1)), pl.BlockSpec((t, X_Q), lambda i: (i, 3))],
        out_specs=[pl.BlockSpec((t, X_Q), lambda i: (i, 0)), kvb, kvb],
        out_shape=[jax.ShapeDtypeStruct((S, X_Q), bf16), jax.ShapeDtypeStruct((MEM_LEN, X_Q), f32),
                   jax.ShapeDtypeStruct((MEM_LEN, X_Q), f32)],
        compiler_params=_cp(("arbitrary",)),
    )(proj, kv, kv, dmix)
    return dq, jnp.concatenate([dk, dv], axis=1)


def _bucket_map():
    qi = np.arange(BLK)[:, None]
    kj = np.arange(2 * BLK)[None, :]
    n = np.maximum(BLK + qi - kj, 0)
    max_exact = N_BUCKETS // 2
    nf = np.maximum(n, 1).astype(np.float64)
    large = max_exact + (np.log(nf / max_exact) / math.log(MAX_DIST / max_exact)
                         * (N_BUCKETS - max_exact)).astype(np.int32)
    large = np.minimum(large, N_BUCKETS - 1)
    return np.where(n < max_exact, n, large).astype(np.int32)


def bias_build(rel_bias):
    def body(rb_ref, bk_ref, o_ref):
        bk = bk_ref[...]
        for h in range(A_HEADS):
            acc = jnp.zeros((BLK, 2 * BLK), f32)
            for b in range(N_BUCKETS):
                acc = jnp.where(bk == b, rb_ref[b, h], acc)
            o_ref[h] = acc

    return pl.pallas_call(
        body, name="bias_build",
        in_specs=[pl.BlockSpec(memory_space=pltpu.SMEM), pl.BlockSpec(memory_space=pltpu.VMEM)],
        out_specs=pl.BlockSpec(memory_space=pltpu.VMEM),
        out_shape=jax.ShapeDtypeStruct((A_HEADS, BLK, 2 * BLK), f32), compiler_params=_cp(),
    )(rel_bias, jnp.asarray(_bucket_map()))


def bias_grad(dbias):
    def body(d_ref, bk_ref, o_ref):
        bk = bk_ref[...]
        row = lax.broadcasted_iota(jnp.int32, (N_BUCKETS, LANE), 0)
        lane = lax.broadcasted_iota(jnp.int32, (N_BUCKETS, LANE), 1)
        acc = jnp.zeros((N_BUCKETS, LANE), f32)
        for h in range(A_HEADS):
            d = d_ref[h]
            for b in range(N_BUCKETS):
                s = jnp.sum(jnp.where(bk == b, d, 0.0), keepdims=True)
                acc = acc + jnp.where((row == b) & (lane == h), s, 0.0)
        o_ref[...] = acc

    return pl.pallas_call(
        body, name="bias_grad", out_shape=jax.ShapeDtypeStruct((N_BUCKETS, LANE), f32), compiler_params=_cp(),
    )(dbias, jnp.asarray(_bucket_map()))


def _swa_f(qb, kp, kc, vp, vc, bias, sk, first):
    kband = jnp.concatenate([kp, kc], axis=0)
    vband = jnp.concatenate([vp, vc], axis=0)
    qi = lax.broadcasted_iota(jnp.int32, (BLK, 2 * BLK), 0)
    kj = lax.broadcasted_iota(jnp.int32, (BLK, 2 * BLK), 1)
    rel = kj - qi
    ok = (rel >= 1) & (rel <= BLK) & ((kj >= BLK) | jnp.logical_not(first))
    lane = lax.broadcasted_iota(jnp.int32, (1, LANE), 1)
    lane_b = lax.broadcasted_iota(jnp.int32, (BLK, LANE), 1)
    outs = []
    for p in range(A_HEADS // 2):
        qp = qb[:, LANE * p:LANE * (p + 1)]
        acc = jnp.zeros((BLK, LANE), f32)
        for g in range(2):
            h = g * (A_HEADS // 2) + p
            msk = (lane // A_DH == g).astype(f32)
            s = bdot_nt(qp * msk, kband) * (A_DH ** -0.5) + bias[h]
            s = jnp.where(ok, s, -1e30)
            skb = jnp.broadcast_to(sk[h:h + 1, :], (BLK, LANE))
            sink = jnp.sum(jnp.where(lane_b == 0, skb, 0.0), axis=-1, keepdims=True)
            m = lax.stop_gradient(jnp.maximum(jnp.max(s, axis=-1, keepdims=True), sink))
            e = jnp.exp(s - m)
            prob = e / (jnp.sum(e, axis=-1, keepdims=True) + jnp.exp(sink - m))
            acc = acc + bdot(prob, vband) * msk
        outs.append(acc)
    return jnp.concatenate(outs, axis=1)


def _swa_specs(nb, rev):
    bi = (lambda i: nb - 1 - i) if rev else (lambda i: i)
    return [
        pl.BlockSpec((BLK, A_Q), lambda i: (bi(i), 0)),
        pl.BlockSpec((BLK, LANE), lambda i: (jnp.maximum(bi(i) - 1, 0), 6)),
        pl.BlockSpec((BLK, LANE), lambda i: (bi(i), 6)),
        pl.BlockSpec((BLK, LANE), lambda i: (jnp.maximum(bi(i) - 1, 0), 7)),
        pl.BlockSpec((BLK, LANE), lambda i: (bi(i), 7)),
        pl.BlockSpec((A_HEADS, BLK, 2 * BLK), lambda i: (0, 0, 0)),
        pl.BlockSpec((16, LANE), lambda i: (0, 0)),
    ]


def swa_fwd(proj, bias, sk):
    S = proj.shape[0]
    nb = S // BLK

    def body(q_ref, kp_ref, kc_ref, vp_ref, vc_ref, b_ref, s_ref, o_ref):
        qkv = [r[...].astype(f32) for r in (q_ref, kp_ref, kc_ref, vp_ref, vc_ref)]
        o_ref[...] = _swa_f(*qkv, b_ref[...], s_ref[...], pl.program_id(0) == 0).astype(bf16)

    return pl.pallas_call(
        body, name="swa_fwd", grid=(nb,), in_specs=_swa_specs(nb, False),
        out_specs=pl.BlockSpec((BLK, A_Q), lambda i: (i, 0)),
        out_shape=jax.ShapeDtypeStruct((S, A_Q), bf16), compiler_params=_cp(("parallel",)),
    )(proj, proj, proj, proj, proj, bias, sk)


def swa_bwd(proj, bias, sk, dmix):
    S = proj.shape[0]
    nb = S // BLK

    def body(q_ref, kp_ref, kc_ref, vp_ref, vc_ref, b_ref, s_ref, do_ref, dqkv_ref, db_ref, ds_ref, ck, cv):
        i = pl.program_id(0)

        @pl.when(i == 0)
        def _():
            db_ref[...] = jnp.zeros_like(db_ref)
            ds_ref[...] = jnp.zeros_like(ds_ref)
            ck[...] = jnp.zeros_like(ck)
            cv[...] = jnp.zeros_like(cv)
        first = i == nb - 1
        qkv = [r[...].astype(f32) for r in (q_ref, kp_ref, kc_ref, vp_ref, vc_ref)]
        _, vjp = jax.vjp(lambda *a: _swa_f(*a, first), *qkv, b_ref[...], s_ref[...])
        dq, dkp, dkc, dvp, dvc, db, ds = vjp(do_ref[...])
        dqkv_ref[...] = jnp.concatenate([dq, dkc + ck[...], dvc + cv[...]], axis=1).astype(bf16)
        ck[...] = dkp
        cv[...] = dvp
        db_ref[...] += db
        ds_ref[...] += ds

    return pl.pallas_call(
        body, name="swa_bwd", grid=(nb,),
        in_specs=_swa_specs(nb, True) + [pl.BlockSpec((BLK, A_Q), lambda i: (nb - 1 - i, 0))],
        out_specs=[pl.BlockSpec((BLK, D), lambda i: (nb - 1 - i, 0)),
                   pl.BlockSpec((A_HEADS, BLK, 2 * BLK), lambda i: (0, 0, 0)),
                   pl.BlockSpec((16, LANE), lambda i: (0, 0))],
        out_shape=[jax.ShapeDtypeStruct((S, D), bf16), jax.ShapeDtypeStruct((A_HEADS, BLK, 2 * BLK), f32),
                   jax.ShapeDtypeStruct((16, LANE), f32)],
        scratch_shapes=[pltpu.VMEM((BLK, LANE), f32), pltpu.VMEM((BLK, LANE), f32)],
        compiler_params=_cp(("arbitrary",)),
    )(proj, proj, proj, proj, proj, bias, sk, dmix)


def _dnprep_f(xext, w, is_qk):
    c = (w[3:4] * xext + w[2:3] * shift_down(xext, 1) + w[1:2] * shift_down(xext, 2) + w[0:1] * shift_down(xext, 3))
    a = _silu(c)[HALO:]
    n = a * lax.rsqrt(jnp.sum(a * a, axis=-1, keepdims=True) + EPS)
    return jnp.where(is_qk, n, a)


def dnprep_fwd(proj, cw):
    S = proj.shape[0]
    nblk = B_QKV // LANE
    T = S

    def body(x_ref, w_ref, o_ref):
        is_qk = pl.program_id(0) < 2 * B_QK // LANE
        wv = w_ref[...]

        def tile(r0, first):
            o_ref[pl.ds(r0, T), :] = _dnprep_f(_glu_gext(x_ref, r0, first, T), wv, is_qk)

        tile(0, True)

    return pl.pallas_call(
        body, name="dnprep_fwd", grid=(nblk,),
        in_specs=[pl.BlockSpec((S, LANE), lambda j: (0, j)), pl.BlockSpec((4, LANE), lambda j: (0, j))],
        out_specs=pl.BlockSpec((S, LANE), lambda j: (0, j)),
        out_shape=jax.ShapeDtypeStruct((S, B_QKV), f32), compiler_params=_cp(("parallel",)),
    )(proj, cw)


def dnprep_bwd(proj, cw, dqkvn):
    S = proj.shape[0]
    nblk = B_QKV // LANE

    T = S

    def body(x_ref, w_ref, d_ref, dx_ref, dw_ref):
        is_qk = pl.program_id(0) < 2 * B_QK // LANE
        wv = w_ref[...]

        def tile(r0, first):
            _, vjp = jax.vjp(lambda a, b: _dnprep_f(a, b, is_qk), _glu_gext(x_ref, r0, first, T), wv)
            dx, dw = vjp(d_ref[pl.ds(r0, T), :])
            dx_ref[pl.ds(r0, T), :] = dx[HALO:].astype(bf16)
            if not first:
                dx_ref[pl.ds(r0 - HALO, HALO), :] += dx[:HALO]
            return dw

        dw_ref[...] = tile(0, True)

    col = pl.BlockSpec((S, LANE), lambda j: (0, j))
    wsp = pl.BlockSpec((4, LANE), lambda j: (0, j))
    return pl.pallas_call(
        body, name="dnprep_bwd", grid=(nblk,), in_specs=[col, wsp, col], out_specs=[col, wsp],
        out_shape=[jax.ShapeDtypeStruct((S, B_QKV), bf16), jax.ShapeDtypeStruct((4, B_QKV), f32)],
        compiler_params=_cp(("parallel",)),
    )(proj, cw, dqkvn)


def _hdot(a, b, ca=1, cb=0):
    return _dg(a, b, ca, cb, HI)


def _bdg(a, b, ca, cb):
    dn = (((ca,), (cb,)), ((0,), (0,)))
    ah, bh = a.astype(bf16), b.astype(bf16)
    al, bl = (a - ah.astype(f32)).astype(bf16), (b - bh.astype(f32)).astype(bf16)
    return (lax.dot_general(ah, bh, dn, preferred_element_type=f32)
            + lax.dot_general(ah, bl, dn, preferred_element_type=f32)
            + lax.dot_general(al, bh, dn, preferred_element_type=f32))


@jax.custom_vjp
def hbd(a, b):
    return _bdg(a, b, 2, 1)


@jax.custom_vjp
def hbd_nt(a, b):
    return _bdg(a, b, 2, 2)


@jax.custom_vjp
def hbd_tn(a, b):
    return _bdg(a, b, 1, 1)


hbd.defvjp(lambda a, b: (hbd(a, b), (a, b)), lambda r, g: (hbd_nt(g, r[1]), hbd_tn(r[0], g)))
hbd_nt.defvjp(lambda a, b: (hbd_nt(a, b), (a, b)), lambda r, g: (hbd(g, r[1]), hbd_tn(g, r[0])))
hbd_tn.defvjp(lambda a, b: (hbd_tn(a, b), (a, b)), lambda r, g: (hbd_nt(r[1], g), hbd(r[0], g)))


def _stack(xs):
    return jnp.concatenate([x[None] for x in xs], axis=0)


def _lane_col(x, j):
    lane = lax.broadcasted_iota(jnp.int32, (1, LANE), 1)
    return jnp.sum(jnp.where(lane == j, x, 0.0), axis=-1, keepdims=True)


def _tri_inv(a_mat):
    r = lax.broadcasted_iota(jnp.int32, (1, CHUNK, CHUNK), 1)
    c = lax.broadcasted_iota(jnp.int32, (1, CHUNK, CHUNK), 2)
    pw = -a_mat
    inv = (r == c).astype(f32) + pw
    for _ in range(5):
        pw = hbd(pw, pw)
        inv = inv + hbd(inv, pw)
    return inv


@jax.custom_vjp
def _tri_inv_known(a_mat, inv):
    return inv


_tri_inv_known.defvjp(lambda a, inv: (inv, inv),
                      lambda inv, g: (-hbd_tn(inv, hbd_nt(g, inv)), jnp.zeros_like(inv)))


def _dnc_f(q, k, v, seg, prm, inverse=_tri_inv):
    C = CHUNK
    B = q.shape[0]
    rows = seg.shape[0]
    beta_all = _sigmoid(seg)
    xx = seg + prm[1:2]
    g_all = -jnp.exp(prm[0:1]) * (jnp.maximum(xx, 0.0) + jnp.log(1.0 + jnp.exp(-jnp.abs(xx))))
    r2 = lax.broadcasted_iota(jnp.int32, (rows, rows), 0)
    c2 = lax.broadcasted_iota(jnp.int32, (rows, rows), 1)
    within = (r2 >= c2) & (r2 // C == c2 // C)
    gc_all = _hdot(within.astype(f32), g_all)
    beta = _stack([_lane_col(beta_all[C * j:C * (j + 1)], h) for j in range(rows // C) for h in range(6)])
    gc = _stack([_lane_col(gc_all[C * j:C * (j + 1)], 6 + h) for j in range(rows // C) for h in range(6)])
    r = lax.broadcasted_iota(jnp.int32, (1, C, C), 1)
    c = lax.broadcasted_iota(jnp.int32, (1, C, C), 2)
    incl = r >= c
    strict = r > c
    gct = [gc_all[C * j:C * (j + 1)].T for j in range(rows // C)]
    g_row = _stack([jnp.broadcast_to(gct[j][6 + h:7 + h, :], (C, C))
                    for j in range(rows // C) for h in range(6)])
    decay = jnp.where(incl, jnp.exp(jnp.where(incl, gc - g_row, 0.0)), 0.0)
    a_mat = beta * sbd_nt(k, k) * jnp.where(strict, decay, 0.0)
    eg = jnp.exp(gc)
    inv = inverse(a_mat)
    u = hbd(inv, beta * v)
    w = hbd(inv, (beta * eg) * k)
    qc = q * (B_DH ** -0.5)
    attn = sbd_nt(qc, k) * decay
    last = (lax.broadcasted_iota(jnp.int32, (1, C, 1), 1) == C - 1).astype(f32)
    g_last = jnp.sum(gc * last, axis=1, keepdims=True)
    dc = jnp.broadcast_to(jnp.exp(g_last), (B, 1, LANE)).reshape(B, LANE)
    return u, w, qc * eg, k * jnp.exp(g_last - gc), attn, dc, inv


def _b1(a, b, ca, cb):
    return lax.dot_general(a.astype(bf16), b.astype(bf16), (((ca,), (cb,)), ((0,), (0,))), preferred_element_type=f32)


@jax.custom_vjp
def sbd(a, b):
    return _b1(a, b, 2, 1)


@jax.custom_vjp
def sbd_nt(a, b):
    return _b1(a, b, 2, 2)


@jax.custom_vjp
def sbd_tn(a, b):
    return _b1(a, b, 1, 1)


sbd.defvjp(lambda a, b: (sbd(a, b), (a, b)), lambda r, g: (sbd_nt(g, r[1]), sbd_tn(r[0], g)))
sbd_nt.defvjp(lambda a, b: (sbd_nt(a, b), (a, b)), lambda r, g: (sbd(g, r[1]), sbd_tn(g, r[0])))
sbd_tn.defvjp(lambda a, b: (sbd_tn(a, b), (a, b)), lambda r, g: (sbd_nt(r[1], g), sbd(r[0], g)))


def _dns_f(S0, u, w, qd, kt, attn, dcrows):
    dc = _lane_col(dcrows, 0).reshape(6, 1, 1)
    delta = u - sbd(w, S0)
    out = sbd(qd, S0) + sbd(attn, delta)
    return out, dc * S0 + sbd_tn(kt, delta)


def _dnpost_f(o, z, grow):
    outs = []
    for h in range(6):
        oh = o[:, LANE * h:LANE * (h + 1)]
        outs.append(oh * lax.rsqrt(jnp.mean(oh * oh, axis=-1, keepdims=True) + EPS) * grow
                    * _silu(z[:, LANE * h:LANE * (h + 1)]))
    return jnp.concatenate(outs, axis=1)


def _hs(h):
    return slice(LANE * h, LANE * (h + 1))


DN_CHUNKS = 4


def _heads(ref, share):
    return _stack([ref[CHUNK * j:CHUNK * (j + 1), _hs(h // share)]
                   for j in range(ref.shape[0] // CHUNK) for h in range(6)])


def _put_heads(ref, val):
    for j in range(ref.shape[0] // CHUNK):
        for h in range(6):
            ref[CHUNK * j:CHUNK * (j + 1), _hs(h)] = val[6 * j + h].astype(ref.dtype)


def _dnc_in_specs():
    rows = CHUNK * DN_CHUNKS
    return [
        pl.BlockSpec((rows, B_QK), lambda n: (n, 0)),
        pl.BlockSpec((rows, B_QK), lambda n: (n, 1)),
        pl.BlockSpec((rows, B_V), lambda n: (n, 1)),
        pl.BlockSpec((rows, LANE), lambda n: (n, 20)),
        pl.BlockSpec((8, LANE), lambda n: (0, 0)),
    ]


def _dnc_out_specs(rev_nc=None, chunks=1):
    ci = (lambda n: n) if rev_nc is None else (lambda n: rev_nc - 1 - n)
    wide = pl.BlockSpec((CHUNK * chunks, B_V), lambda n: (ci(n), 0))
    return [wide, wide, wide, wide, pl.BlockSpec((chunks, 6, CHUNK, CHUNK), lambda n: (ci(n), 0, 0, 0)),
            pl.BlockSpec((chunks, 8, LANE), lambda n: (ci(n), 0, 0))]


def _dc_rows(dc):
    pad = jnp.zeros((2, LANE), f32)
    return _stack([jnp.concatenate([dc[6 * j:6 * (j + 1)], pad], axis=0) for j in range(dc.shape[0] // 6)])


def _dnc_shapes(S, mm=f32):
    nc = S // CHUNK
    return [jax.ShapeDtypeStruct((S, B_V), f32)] + [jax.ShapeDtypeStruct((S, B_V), mm)] * 3 + [
        jax.ShapeDtypeStruct((nc, 6, CHUNK, CHUNK), mm), jax.ShapeDtypeStruct((nc, 8, LANE), f32)]


def dnc_fwd(qkvn, proj, prm):
    S = proj.shape[0]

    def body(q_ref, k_ref, v_ref, s_ref, p_ref, u_ref, w_ref, qd_ref, kt_ref, at_ref, dc_ref, inv_ref):
        u, w, qd, kt, attn, dc, inv = _dnc_f(_heads(q_ref, 2), _heads(k_ref, 2), _heads(v_ref, 1), s_ref[...],
                                             p_ref[...])
        inv_ref[...] = inv.reshape(inv_ref.shape)
        _put_heads(u_ref, u)
        _put_heads(w_ref, w)
        _put_heads(qd_ref, qd)
        _put_heads(kt_ref, kt)
        at_ref[...] = attn.reshape(at_ref.shape).astype(at_ref.dtype)
        dc_ref[...] = _dc_rows(dc)

    outs = _dnc_out_specs(chunks=DN_CHUNKS)
    out = pl.pallas_call(
        body, name="dn_chunk_fwd", grid=(S // (CHUNK * DN_CHUNKS),), in_specs=_dnc_in_specs(),
        out_specs=outs + [outs[4]], out_shape=_dnc_shapes(S, bf16) + [_dnc_shapes(S)[4]],
        compiler_params=_cp(("parallel",)),
    )(qkvn, qkvn, qkvn, proj, prm)
    return out[:6], out[6]


def dnc_bwd(qkvn, proj, prm, inv, cots):
    S = proj.shape[0]

    def body(q_ref, k_ref, v_ref, s_ref, p_ref, inv_ref, du_ref, dw_ref, dqd_ref, dkt_ref, dat_ref, ddc_ref,
             dx_ref, dseg_ref, dprm_ref):
        @pl.when(pl.program_id(0) == 0)
        def _():
            dprm_ref[...] = jnp.zeros_like(dprm_ref)
        nb = 6 * DN_CHUNKS
        known = functools.partial(_tri_inv_known, inv=inv_ref[...].reshape(nb, CHUNK, CHUNK))
        _, vjp = jax.vjp(lambda *a: _dnc_f(*a, inverse=known)[:6], _heads(q_ref, 2), _heads(k_ref, 2),
                         _heads(v_ref, 1), s_ref[...], p_ref[...])
        ddc = jnp.concatenate([ddc_ref[j, 0:6, :] for j in range(DN_CHUNKS)], axis=0)
        dq, dk, dv, dseg, dprm = vjp((_heads(du_ref, 1), _heads(dw_ref, 1), _heads(dqd_ref, 1), _heads(dkt_ref, 1),
                                      dat_ref[...].reshape(nb, CHUNK, CHUNK), ddc))
        for j in range(DN_CHUNKS):
            o = 6 * j
            dx_ref[CHUNK * j:CHUNK * (j + 1), :] = jnp.concatenate(
                [dq[o] + dq[o + 1], dq[o + 2] + dq[o + 3], dq[o + 4] + dq[o + 5],
                 dk[o] + dk[o + 1], dk[o + 2] + dk[o + 3], dk[o + 4] + dk[o + 5]] + [dv[o + h] for h in range(6)], axis=1)
        dseg_ref[...] = dseg.astype(bf16)
        dprm_ref[...] += dprm

    rows = CHUNK * DN_CHUNKS
    outs = _dnc_out_specs(chunks=DN_CHUNKS)
    return pl.pallas_call(
        body, name="dn_chunk_bwd", grid=(S // rows,),
        in_specs=_dnc_in_specs() + [outs[4]] + outs,
        out_specs=[pl.BlockSpec((rows, B_QKV), lambda n: (n, 0)), pl.BlockSpec((rows, LANE), lambda n: (n, 0)),
                   pl.BlockSpec((8, LANE), lambda n: (0, 0))],
        out_shape=[jax.ShapeDtypeStruct((S, B_QKV), f32), jax.ShapeDtypeStruct((S, LANE), bf16),
                   jax.ShapeDtypeStruct((8, LANE), f32)],
        compiler_params=_cp(("arbitrary",)),
    )(qkvn, qkvn, qkvn, proj, prm, inv, *cots)


def dns_fwd(chunked):
    u = chunked[0]
    S = u.shape[0]
    nc = S // CHUNK

    def body(u_ref, w_ref, qd_ref, kt_ref, at_ref, dc_ref, o_ref, st_ref, st):
        @pl.when(pl.program_id(0) == 0)
        def _():
            st[...] = jnp.zeros_like(st)
        S0 = st[...]
        st_ref[0] = S0
        out, S1 = _dns_f(S0, _heads(u_ref, 1), _heads(w_ref, 1), _heads(qd_ref, 1), _heads(kt_ref, 1),
                         at_ref[0], dc_ref[0, 0:6, :])
        _put_heads(o_ref, out)
        st[...] = S1

    return pl.pallas_call(
        body, name="dn_scan_fwd", grid=(nc,), in_specs=_dnc_out_specs(),
        out_specs=[pl.BlockSpec((CHUNK, B_V), lambda n: (n, 0)),
                   pl.BlockSpec((1, 6, B_DH, B_DH), lambda n: (n, 0, 0, 0))],
        out_shape=[jax.ShapeDtypeStruct((S, B_V), f32), jax.ShapeDtypeStruct((nc, 6, B_DH, B_DH), f32)],
        scratch_shapes=[pltpu.VMEM((6, B_DH, B_DH), f32)],
        compiler_params=_cp(("arbitrary",)),
    )(*chunked)


def dns_bwd(chunked, states, do):
    S = do.shape[0]
    nc = S // CHUNK

    def body(u_ref, w_ref, qd_ref, kt_ref, at_ref, dc_ref, st_ref, do_ref,
             du_ref, dw_ref, dqd_ref, dkt_ref, dat_ref, ddc_ref, dst):
        @pl.when(pl.program_id(0) == 0)
        def _():
            dst[...] = jnp.zeros_like(dst)
        _, vjp = jax.vjp(_dns_f, st_ref[0], _heads(u_ref, 1), _heads(w_ref, 1).astype(f32),
                         _heads(qd_ref, 1).astype(f32), _heads(kt_ref, 1).astype(f32), at_ref[0].astype(f32),
                         dc_ref[0, 0:6, :])
        dS0, du, dw, dqd, dkt, dat, ddc = vjp((_heads(do_ref, 1), dst[...]))
        dst[...] = dS0
        _put_heads(du_ref, du)
        _put_heads(dw_ref, dw)
        _put_heads(dqd_ref, dqd)
        _put_heads(dkt_ref, dkt)
        dat_ref[0] = dat
        ddc_ref[0] = jnp.concatenate([ddc, jnp.zeros((2, LANE), f32)], axis=0)

    return pl.pallas_call(
        body, name="dn_scan_bwd", grid=(nc,),
        in_specs=_dnc_out_specs(nc) + [pl.BlockSpec((1, 6, B_DH, B_DH), lambda n: (nc - 1 - n, 0, 0, 0)),
                                       pl.BlockSpec((CHUNK, B_V), lambda n: (nc - 1 - n, 0))],
        out_specs=_dnc_out_specs(nc), out_shape=_dnc_shapes(S),
        scratch_shapes=[pltpu.VMEM((6, B_DH, B_DH), f32)],
        compiler_params=_cp(("arbitrary",)),
    )(*chunked, states, do)


def dnpost_fwd(o, proj, prm):
    S = o.shape[0]
    t = min(512, S)

    def body(o_ref, z_ref, p_ref, y_ref):
        y_ref[...] = _dnpost_f(o_ref[...], z_ref[...], p_ref[2:3, :]).astype(bf16)

    tok = pl.BlockSpec((t, B_V), lambda i: (i, 0))
    return pl.pallas_call(
        body, name="dn_post_fwd", grid=(S // t,),
        in_specs=[tok, pl.BlockSpec((t, B_V), lambda i: (i, 2)), pl.BlockSpec((8, LANE), lambda i: (0, 0))],
        out_specs=tok, out_shape=jax.ShapeDtypeStruct((S, B_V), bf16), compiler_params=_cp(("parallel",)),
    )(o, proj, prm)


def dnpost_bwd(o, proj, prm, dmix):
    S = o.shape[0]
    t = min(512, S)

    def body(o_ref, z_ref, p_ref, dy_ref, do_ref, dz_ref, dg_ref):
        @pl.when(pl.program_id(0) == 0)
        def _():
            dg_ref[...] = jnp.zeros_like(dg_ref)
        _, vjp = jax.vjp(_dnpost_f, o_ref[...], z_ref[...], p_ref[2:3, :])
        do, dz, dg = vjp(dy_ref[...])
        do_ref[...] = do
        dz_ref[...] = dz.astype(bf16)
        dg_ref[...] += dg

    tok = pl.BlockSpec((t, B_V), lambda i: (i, 0))
    return pl.pallas_call(
        body, name="dn_post_bwd", grid=(S // t,),
        in_specs=[tok, pl.BlockSpec((t, B_V), lambda i: (i, 2)), pl.BlockSpec((8, LANE), lambda i: (0, 0)), tok],
        out_specs=[tok, tok, pl.BlockSpec((1, LANE), lambda i: (0, 0))],
        out_shape=[jax.ShapeDtypeStruct((S, B_V), f32), jax.ShapeDtypeStruct((S, B_V), bf16),
                   jax.ShapeDtypeStruct((1, LANE), f32)],
        compiler_params=_cp(("arbitrary",)),
    )(o, proj, prm, dmix)


N_FF_BLK = D_FF // LANE
GU_SHARD = 2 * D_FF // 4


GLU_ROWS = 256
HALO = 16


def _glu_conv(gext, w, b):
    return (w[2:3] * gext + w[1:2] * shift_down(gext, 1) + w[0:1] * shift_down(gext, 2) + b)[HALO:]


def _glu_gate(c, up):
    return _silu(c) * up


def _glu_gext(g_ref, r0, first, T=GLU_ROWS):
    if first:
        return jnp.concatenate([jnp.zeros((HALO, LANE), f32), g_ref[0:T, :].astype(f32)], axis=0)
    return g_ref[pl.ds(r0 - HALO, T + HALO), :].astype(f32)


def glu_fwd(gu, w, b, name):
    S = gu.shape[0]
    T = min(GLU_ROWS, S // 2)

    def body(g_ref, u_ref, w_ref, b_ref, o_ref, c_ref):
        wv, bv = w_ref[...], b_ref[...]

        def tile(r0, first):
            c = _glu_conv(_glu_gext(g_ref, r0, first, T), wv, bv)
            c_ref[pl.ds(r0, T), :] = c.astype(bf16)
            o_ref[pl.ds(r0, T), :] = _glu_gate(c, u_ref[pl.ds(r0, T), :].astype(f32)).astype(bf16)

        tile(0, True)

        @pl.loop(1, S // T)
        def _(t):
            tile(pl.multiple_of(t * T, T), False)

    col = pl.BlockSpec((S, LANE), lambda j: (0, j))
    return pl.pallas_call(
        body, name=name, grid=(N_FF_BLK,),
        in_specs=[col, pl.BlockSpec((S, LANE), lambda j: (0, N_FF_BLK + j)), pl.BlockSpec((3, LANE), lambda j: (0, j)),
                  pl.BlockSpec((1, LANE), lambda j: (0, j))],
        out_specs=[col, col], out_shape=[jax.ShapeDtypeStruct((S, D_FF), bf16)] * 2,
        compiler_params=_cp(("parallel",)),
    )(gu, gu, w, b.reshape(1, D_FF))


def glu_bwd(gu, c, w, b, dact, name):
    S = gu.shape[0]
    T = min(GLU_ROWS, S // 2)

    def body(g_ref, u_ref, c_ref, w_ref, b_ref, d_ref, dg_ref, dw_ref, db_ref, acc):
        wv, bv = w_ref[...], b_ref[...]

        def tile(r0, first):
            rows = pl.ds(r0, T)
            _, vjp_gate = jax.vjp(_glu_gate, c_ref[rows, :].astype(f32), u_ref[rows, :].astype(f32))
            dc, du = vjp_gate(d_ref[rows, :].astype(f32))
            _, vjp_conv = jax.vjp(_glu_conv, _glu_gext(g_ref, r0, first, T), wv, bv)
            dgx, dw, db = vjp_conv(dc)
            acc[pl.ds(r0, T), :] = dgx[HALO:]
            if not first:
                acc[pl.ds(r0 - HALO, HALO), :] += dgx[:HALO]
            dg_ref[1, pl.ds(r0, T), :] = du.astype(bf16)
            return dw, db

        dw0, db0 = tile(0, True)
        dw_ref[...] = dw0
        db_ref[...] = db0

        @pl.loop(1, S // T)
        def _(t):
            dw, db = tile(pl.multiple_of(t * T, T), False)
            dw_ref[...] += dw
            db_ref[...] += db

        dg_ref[0] = acc[...].astype(bf16)

    col = pl.BlockSpec((S, LANE), lambda j: (0, j))
    wsp = pl.BlockSpec((3, LANE), lambda j: (0, j))
    bsp = pl.BlockSpec((1, LANE), lambda j: (0, j))
    return pl.pallas_call(
        body, name=name, grid=(N_FF_BLK,),
        in_specs=[col, pl.BlockSpec((S, LANE), lambda j: (0, N_FF_BLK + j)), col, wsp, bsp, col],
        out_specs=[pl.BlockSpec((2, S, LANE), lambda j: (0, 0, j)), wsp, bsp],
        out_shape=[jax.ShapeDtypeStruct((2, S, D_FF), bf16), jax.ShapeDtypeStruct((3, D_FF), f32),
                   jax.ShapeDtypeStruct((1, D_FF), f32)],
        scratch_shapes=[pltpu.VMEM((S, LANE), f32)],
        compiler_params=_cp(("parallel",)),
    )(gu, gu, c, w, b.reshape(1, D_FF), dact)


def gu_fwd(n2, wg, name):
    S = n2.shape[0]
    tm = min(MM_ROWS, S)

    def body(a_ref, w_ref, o_ref):
        o_ref[...] = _dg(a_ref[...], w_ref[...], 1, 0).astype(bf16)

    return pl.pallas_call(
        body, name=name, grid=(4, S // tm),
        in_specs=[pl.BlockSpec((tm, D), lambda s, m: (m, 0)), pl.BlockSpec((None, D, GU_SHARD), lambda s, m: (s, 0, 0))],
        out_specs=pl.BlockSpec((tm, GU_SHARD), lambda s, m: (m, s)),
        out_shape=jax.ShapeDtypeStruct((S, 2 * D_FF), bf16), compiler_params=_cp(("parallel", "parallel")),
    )(n2, wg)


def gu_bwd_x(dgu, wg, norm, name):
    S = dgu.shape[1]
    tm = min(NORM_ROWS, S)

    def body(d_ref, w_ref, h_ref, g_ref, r_ref, o_ref, dg_ref):
        _acc_then_norm_bwd(_dg(d_ref[...], w_ref[...], 1, 1), 4, h_ref, g_ref, r_ref, o_ref, dg_ref)

    tok = pl.BlockSpec((tm, D), lambda m, s: (m, 0))
    vec = pl.BlockSpec((1, D), lambda m, s: (0, 0))
    return pl.pallas_call(
        body, name=name, grid=(S // tm, 4),
        in_specs=[pl.BlockSpec((None, tm, GU_SHARD), lambda m, s: (s // 2, m, s % 2)),
                  pl.BlockSpec((None, D, GU_SHARD), lambda m, s: (s, 0, 0)), tok, vec, tok],
        out_specs=[tok, vec],
        out_shape=[jax.ShapeDtypeStruct((S, D), f32), jax.ShapeDtypeStruct((1, D), f32)],
        compiler_params=_cp(("arbitrary", "arbitrary")),
    )(dgu, wg, norm[0], norm[1].reshape(1, D), norm[2])


def gu_bwd_w(n2, dgu, name):
    S = n2.shape[0]
    tm = min(MM_ROWS, S)
    nm = S // tm

    def body(a_ref, d_ref, o_ref, acc):
        @pl.when(pl.program_id(1) == 0)
        def _():
            acc[...] = jnp.zeros_like(acc)
        acc[...] += _dg(a_ref[...], d_ref[...], 0, 0)

        @pl.when(pl.program_id(1) == nm - 1)
        def _():
            o_ref[...] = acc[...].astype(bf16)

    return pl.pallas_call(
        body, name=name, grid=(4, nm),
        in_specs=[pl.BlockSpec((tm, D), lambda s, m: (m, 0)),
                  pl.BlockSpec((None, tm, GU_SHARD), lambda s, m: (s // 2, m, s % 2))],
        out_specs=pl.BlockSpec((None, D, GU_SHARD), lambda s, m: (s, 0, 0)),
        out_shape=jax.ShapeDtypeStruct((4, D, GU_SHARD), bf16),
        scratch_shapes=[pltpu.VMEM((D, GU_SHARD), f32)],
        compiler_params=_cp(("parallel", "arbitrary")),
    )(n2, dgu)


def _pair_cols(w):
    lead = w.shape[:-1]
    return w.reshape(lead + (2, 6, A_DH)).swapaxes(-3, -2).reshape(lead + (A_Q,))


def _unpair_cols(w):
    lead = w.shape[:-1]
    return w.reshape(lead + (6, 2, A_DH)).swapaxes(-3, -2).reshape(lead + (A_Q,))


def _lay_in_a(w):
    return jnp.concatenate([_pair_cols(w[:, :A_Q]), w[:, A_Q:]], axis=1)


def _unlay_in_a(w):
    return jnp.concatenate([_unpair_cols(w[:, :A_Q]), w[:, A_Q:]], axis=1)


def _lay_out_a(w):
    return jnp.concatenate([_pair_cols(w[:A_Q].T).T, w[A_Q:]], axis=0)


def _unlay_out_a(w):
    return jnp.concatenate([_unpair_cols(w[:A_Q].T).T, w[A_Q:]], axis=0)


def _lay_in_b(w):
    return jnp.concatenate([w[:, :2304], w[:, 2316:], w[:, 2304:2316],
                            jnp.zeros((w.shape[0], LANE - 12), w.dtype)], axis=1)


def _unlay_in_b(w):
    return jnp.concatenate([w[:, :2304], w[:, 2560:2572], w[:, 2304:2560]], axis=1)


def _chip_cols(w):
    return jnp.moveaxis(w.reshape(w.shape[0], 4, w.shape[1] // 4), 1, 0)


def _unchip_cols(w):
    return jnp.moveaxis(w, 0, 1).reshape(w.shape[1], 4 * w.shape[2])


def _local_step(x, mem, target, P):
    arrive = P.get("arrive", lambda key, after: None)
    ready = P.get("ready", lambda key, grads, dep: dep)
    sk = jnp.zeros((16, LANE), f32).at[:A_HEADS].set(jnp.broadcast_to(P["sinks"][:, None], (A_HEADS, LANE)))
    prm = jnp.zeros((8, LANE), f32).at[0, 6:12].set(P["a_log"]).at[1, 6:12].set(P["dt_bias"]).at[2].set(P["out_norm_g"])
    bias = bias_build(P["rel_bias"])
    saved = []
    h = x
    n1 = rms_fwd(h, P["g_mix"][0], "rms_mix0")
    for i in range(2):
        arrive(("w_in", i), n1)
        if i == 0:
            proj = mm_nn(n1, P["w_in_a"], out_dtype=bf16, name="proj_a")
        else:
            proj = mm_nn(n1, P["w_in_b"], name="proj_b")
        arrive(("w_mem", i), proj)
        kv = memkv_fwd(mem, P["g_mem"][i], P["w_mem"][i], f"memkv{i}")
        if i == 0:
            self_out = swa_fwd(proj, bias, sk)
            cross = xattn_fwd(proj, A_Q + 2 * LANE, kv, "xattn_a")
            extra = ()
        else:
            qkvn = dnprep_fwd(proj, P["conv_qkv"])
            chunked, inv = dnc_fwd(qkvn, proj, prm)
            o, states = dns_fwd(chunked)
            self_out = dnpost_fwd(o, proj, prm)
            cross = xattn_fwd(proj, 2304, kv, "xattn_b")
            extra = (qkvn, chunked, inv, states, o)
        mix = (self_out, cross)
        arrive(("w_out", i), cross)
        h2, n2 = mm_res_norm(mix, P["w_out"][i], h, P["g_ffn"][i], f"out_proj{i}")
        arrive(("w_gu", i), n2)
        gu = gu_fwd(n2, P["w_gu"][i], f"gate_up{i}")
        act, pre = glu_fwd(gu, P["ffn_cw"][i], P["ffn_cb"][i], f"glu{i}")
        arrive(("w_down", i), act)
        saved.append((h, n1, kv, proj, mix, h2, n2, gu, pre, act, extra))
        if i == 0:
            h, n1 = mm_res_norm(act, P["w_down"][i], h2, P["g_mix"][1], f"down{i}")
        else:
            h = mm_nn(act, P["w_down"][i], res=h2, name=f"down{i}")

    loss, dh, dg_fin = loss_head(h, P["g_fin"], target)
    G = {"g_fin": dg_fin[0], "g_mix": [None, None], "g_mem": [None, None], "g_ffn": [None, None],
         "w_mem": [None, None], "w_out": [None, None], "w_gu": [None, None], "w_down": [None, None],
         "ffn_cw": [None, None], "ffn_cb": [None, None]}
    for i in (1, 0):
        hin, n1, kv, proj, mix, h2, n2, gu, pre, act, extra = saved[i]
        dact = mm_nt(dh, P["w_down"][i], out_dtype=bf16, name=f"d_act{i}")
        G["w_down"][i] = mm_tn(act, dh, name=f"dw_down{i}")
        dgu, dcw, dcb = glu_bwd(gu, pre, P["ffn_cw"][i], P["ffn_cb"][i], dact, f"glu_bwd{i}")
        G["ffn_cw"][i], G["ffn_cb"][i] = dcw, dcb[0]
        G["w_gu"][i] = gu_bwd_w(n2, dgu, f"dw_gu{i}")
        g_ffn = ready(("ffn", i), G, P["g_ffn"][i])
        dh2, dg = gu_bwd_x(dgu, P["w_gu"][i], (h2, g_ffn, dh), f"d_n2_{i}")
        G["g_ffn"][i] = dg[0]
        dmix = mm_nt(dh2, P["w_out"][i], name=f"d_mix{i}")
        G["w_out"][i] = mm_tn(mix, dh2, name=f"dw_out{i}")
        g_mix = ready(("tick", i), G, P["g_mix"][i])
        if i == 0:
            dqkv, dbias, dsk = swa_bwd(proj, bias, sk, dmix)
            dxq, dkv = xattn_bwd(proj, A_Q + 2 * LANE, kv, dmix, "xattn_a_bwd")
            dproj = (dqkv, dxq)
            G["sinks"] = dsk[:A_HEADS, 0]
            G["rel_bias"] = bias_grad(dbias)[:, :A_HEADS]
            w_in, gname = P["w_in_a"], "w_in_a"
        else:
            qkvn, chunked, inv, states, o = extra
            do, dz, dgo = dnpost_bwd(o, proj, prm, dmix)
            dqkvn, dseg, dprm = dnc_bwd(qkvn, proj, prm, inv, dns_bwd(chunked, states, do))
            draw, dconv = dnprep_bwd(proj, P["conv_qkv"], dqkvn)
            dxq, dkv = xattn_bwd(proj, 2304, kv, dmix, "xattn_b_bwd")
            dproj = (draw, dz, dxq, dseg)
            G["conv_qkv"] = dconv
            G["a_log"], G["dt_bias"], G["out_norm_g"] = dprm[0, 6:12], dprm[1, 6:12], dgo[0]
            w_in, gname = P["w_in_b"], "w_in_b"
        G[gname] = mm_tn(n1, dproj, name=f"d{gname}")
        dh, dg = mm_nt_norm(dproj, w_in, (hin, g_mix, dh2), f"d_n1_{i}")
        G["g_mix"][i] = dg[0]
        dgm, dwm = memkv_bwd(mem, P["g_mem"][i], P["w_mem"][i], dkv, f"memkv_bwd{i}")
        G["g_mem"][i], G["w_mem"][i] = dgm[0], dwm
        ready(("mix", i), G, None)
    return loss, dh, G


def _grads_to_ref(G):
    return {
        "rel_bias": G["rel_bias"], "norm_mix_g": jnp.stack(G["g_mix"]), "norm_mem_g": jnp.stack(G["g_mem"]),
        "w_mem_kv": jnp.stack(G["w_mem"]),
        "w_out": jnp.stack([_unlay_out_a(G["w_out"][0]), G["w_out"][1]]),
        "w_in_a": _unlay_in_a(G["w_in_a"])[None], "sinks_a": G["sinks"][None],
        "w_in_b": _unlay_in_b(G["w_in_b"])[None], "conv_qkv_b": G["conv_qkv"][None],
        "a_log_b": G["a_log"][None], "dt_bias_b": G["dt_bias"][None], "out_norm_g_b": G["out_norm_g"][None],
        "norm_ffn_g": jnp.stack(G["g_ffn"]),
        "w_gate_up": jnp.stack([_unchip_cols(G["w_gu"][0]), _unchip_cols(G["w_gu"][1])]).astype(f32),
        "ffn_conv_w": jnp.stack(G["ffn_cw"]), "ffn_conv_b": jnp.stack(G["ffn_cb"]),
        "w_down": jnp.stack(G["w_down"]), "final_norm_g": G["g_fin"],
    }


ANY = pl.BlockSpec(memory_space=pl.ANY)


def _place():
    return lax.axis_index("x"), lax.axis_index("y"), lax.axis_index("c")


def allreduce_small(buf):
    R = buf.shape[0]

    def body(b_ref, o_ref, recv, ssem, rsem):
        x, y, c = _place()
        me = 4 * x + 2 * y + c

        def peer(k):
            return (1 - x if k & 4 else x, 1 - y if k & 2 else y, 1 - c if k & 1 else c)

        def remote(k, slot):
            return pltpu.make_async_remote_copy(
                src_ref=b_ref, dst_ref=recv.at[slot], send_sem=ssem.at[k - 1], recv_sem=rsem.at[k - 1],
                device_id=peer(k), device_id_type=MESH)

        sends = [remote(k, me) for k in range(1, 8)]
        for cp in sends:
            cp.start()
        recv[me] = b_ref[...]
        for k in range(1, 8):
            px, py, pc = peer(k)
            remote(k, 4 * px + 2 * py + pc).wait_recv()
        for cp in sends:
            cp.wait_send()
        total = recv[0]
        for j in range(1, 8):
            total = total + recv[j]
        o_ref[...] = total

    return pl.pallas_call(
        body, name="small_allreduce",
        in_specs=[pl.BlockSpec(memory_space=pltpu.VMEM)], out_specs=pl.BlockSpec(memory_space=pltpu.VMEM),
        out_shape=jax.ShapeDtypeStruct(buf.shape, f32),
        scratch_shapes=[pltpu.VMEM((8, R, LANE), f32), pltpu.SemaphoreType.DMA((7,)), pltpu.SemaphoreType.DMA((7,))],
    )(buf)


def sum_slots(own, recv, chip, core, name):
    _, R, C = recv.shape
    tr = _row_tile(R, 256)
    nt = R // tr

    def body(p_ref, a_ref, r_ref, o_ref):
        acc = jnp.zeros((tr, C), f32)
        for s in range(4):
            acc = acc + jnp.where(p_ref[0] == s, a_ref[s], r_ref[s]).astype(f32)
        o_ref[...] = acc

    slots = pl.BlockSpec((4, tr, C), lambda i, p_ref: (0, i, 0))
    return pl.pallas_call(
        body, name=name, out_shape=jax.ShapeDtypeStruct((2 * R, C), f32),
        grid_spec=pltpu.PrefetchScalarGridSpec(
            num_scalar_prefetch=1, grid=(nt,), in_specs=[slots, slots],
            out_specs=pl.BlockSpec((tr, C), lambda i, p_ref: (p_ref[1] * nt + i, 0))),
        compiler_params=_cp(("parallel",)),
    )(jnp.stack([chip, core]).astype(jnp.int32), own, recv)


def _half(ref, core, axis=0):
    half = ref.shape[axis] // 2
    idx = (slice(None),) * axis + (pl.ds(core * half, half),)
    return ref.at[idx]


IN_HBM = pl.BlockSpec(memory_space=pltpu.HBM)
IN_SEM = pl.BlockSpec(memory_space=pltpu.SEMAPHORE)
SIDE_EFFECT = pltpu.SideEffectType.DATAFLOW_SIDE_EFFECTING


def _gather_copy(buf, i, k, ssem, rsem, place, landing):
    x, y, c = place
    px, py = [(1 - x, y), (x, 1 - y), (1 - x, 1 - y)][k]
    me = 2 * x + y
    return pltpu.make_async_remote_copy(
        src_ref=buf.at[me], dst_ref=buf.at[me if landing == "theirs" else 2 * px + py],
        send_sem=ssem.at[3 * i + k], recv_sem=rsem.at[3 * i + k], device_id=(px, py, c), device_id_type=MESH)


def gather_start(groups, name):
    flat = [b for grp in groups for b in grp]
    n, ng = len(flat), len(groups)

    def body(*refs):
        bufs, sems = refs[:n], refs[n:n + 2 * ng]
        place = _place()
        j = 0
        for g, grp in enumerate(groups):
            for i in range(len(grp)):
                for k in range(3):
                    _gather_copy(bufs[j], i, k, sems[2 * g], sems[2 * g + 1], place, "theirs").start()
                j += 1
        refs[-1][...] = jnp.zeros_like(refs[-1])

    sem_shapes = [pltpu.SemaphoreType.DMA((3 * len(grp),)) for grp in groups for _ in range(2)]
    out = pl.pallas_call(
        body, name=name, in_specs=[IN_HBM] * n,
        out_specs=(*[IN_SEM] * (2 * ng), *[IN_HBM] * n, pl.BlockSpec(memory_space=pltpu.VMEM)),
        out_shape=(*sem_shapes, *[pltpu.HBM(b.shape, b.dtype) for b in flat], jax.ShapeDtypeStruct((8, LANE), f32)),
        input_output_aliases={i: 2 * ng + i for i in range(n)},
        compiler_params=pltpu.CompilerParams(has_side_effects=SIDE_EFFECT),
    )(*[pltpu.with_memory_space_constraint(b, pltpu.HBM) for b in flat])
    sems, bufs = out[:2 * ng], list(out[2 * ng:2 * ng + n])
    flights, j = [], 0
    for g, grp in enumerate(groups):
        flights.append((bufs[j:j + len(grp)], sems[2 * g], sems[2 * g + 1]))
        j += len(grp)
    return flights, out[-1]


def gather_wait(flight, after, name):
    bufs, ssem, rsem = flight
    n = len(bufs)

    def body(*refs):
        place = _place()
        for i in range(n):
            for k in range(3):
                cp = _gather_copy(refs[i], i, k, refs[n], refs[n + 1], place, "mine")
                cp.wait_send()
                cp.wait_recv()

    return pl.pallas_call(
        body, name=name, in_specs=[IN_HBM] * n + [IN_SEM, IN_SEM, ANY], out_specs=[IN_HBM] * n,
        out_shape=[pltpu.HBM(b.shape, b.dtype) for b in bufs], input_output_aliases={i: i for i in range(n)},
        compiler_params=pltpu.CompilerParams(has_side_effects=SIDE_EFFECT),
    )(*bufs, ssem, rsem, after)


def _scatter_copy(src, land, j, k, ssem, rsem, place, landing):
    x, y, c = place
    px, py = [(1 - x, y), (x, 1 - y), (1 - x, 1 - y)][k]
    return pltpu.make_async_remote_copy(
        src_ref=src.at[2 * px + py], dst_ref=land.at[2 * x + y if landing == "theirs" else 2 * px + py],
        send_sem=ssem.at[3 * j + k], recv_sem=rsem.at[3 * j + k], device_id=(px, py, c), device_id_type=MESH)


def scatter_start(srcs, name):
    n = len(srcs)
    lands = [lax.empty(g.shape, g.dtype) for g in srcs]

    def body(*refs):
        place = _place()
        for j in range(n):
            for k in range(3):
                _scatter_copy(refs[j], refs[n + j], j, k, refs[2 * n], refs[2 * n + 1], place, "theirs").start()
        refs[-1][...] = jnp.zeros_like(refs[-1])

    sem = pltpu.SemaphoreType.DMA((3 * n,))
    hbm = [pltpu.with_memory_space_constraint(b, pltpu.HBM) for b in list(srcs) + lands]
    out = pl.pallas_call(
        body, name=name, in_specs=[IN_HBM] * (2 * n),
        out_specs=(IN_SEM, IN_SEM, *[IN_HBM] * (2 * n), pl.BlockSpec(memory_space=pltpu.VMEM)),
        out_shape=(sem, sem, *[pltpu.HBM(b.shape, b.dtype) for b in hbm], jax.ShapeDtypeStruct((8, LANE), f32)),
        input_output_aliases={i: 2 + i for i in range(2 * n)},
        compiler_params=pltpu.CompilerParams(has_side_effects=SIDE_EFFECT),
    )(*hbm)
    return (list(out[2:2 + n]), list(out[2 + n:2 + 2 * n]), out[0], out[1]), out[-1]


def scatter_wait(flight, after, name):
    srcs, lands, ssem, rsem = flight
    n = len(srcs)

    def body(*refs):
        place = _place()
        for j in range(n):
            for k in range(3):
                cp = _scatter_copy(refs[j], refs[n + j], j, k, refs[2 * n], refs[2 * n + 1], place, "mine")
                cp.wait_send()
                cp.wait_recv()

    out = pl.pallas_call(
        body, name=name, in_specs=[IN_HBM] * (2 * n) + [IN_SEM, IN_SEM, ANY], out_specs=[IN_HBM] * (2 * n),
        out_shape=[pltpu.HBM(b.shape, b.dtype) for b in list(srcs) + list(lands)],
        input_output_aliases={i: i for i in range(2 * n)},
        compiler_params=pltpu.CompilerParams(has_side_effects=SIDE_EFFECT),
    )(*srcs, *lands, ssem, rsem, after)
    return list(out[:n]), list(out[n:])


def _pair_copy(src, land, j, ssem, rsem, place):
    x, y, c = place
    return pltpu.make_async_remote_copy(
        src_ref=_half(src, 1 - c, axis=1), dst_ref=land, send_sem=ssem.at[j], recv_sem=rsem.at[j],
        device_id=(x, y, 1 - c), device_id_type=MESH)


def pair_start(srcs, name):
    n = len(srcs)
    lands = [lax.empty((4, g.shape[1] // 2, g.shape[2]), g.dtype) for g in srcs]

    def body(*refs):
        place = _place()
        for j in range(n):
            _pair_copy(refs[j], refs[n + j], j, refs[2 * n], refs[2 * n + 1], place).start()
        refs[-1][...] = jnp.zeros_like(refs[-1])

    sem = pltpu.SemaphoreType.DMA((n,))
    hbm = [pltpu.with_memory_space_constraint(b, pltpu.HBM) for b in list(srcs) + lands]
    out = pl.pallas_call(
        body, name=name, in_specs=[IN_HBM] * (2 * n),
        out_specs=(IN_SEM, IN_SEM, *[IN_HBM] * (2 * n), pl.BlockSpec(memory_space=pltpu.VMEM)),
        out_shape=(sem, sem, *[pltpu.HBM(b.shape, b.dtype) for b in hbm], jax.ShapeDtypeStruct((8, LANE), f32)),
        input_output_aliases={i: 2 + i for i in range(2 * n)},
        compiler_params=pltpu.CompilerParams(has_side_effects=SIDE_EFFECT),
    )(*hbm)
    return (list(out[2:2 + n]), list(out[2 + n:2 + 2 * n]), out[0], out[1]), out[-1]


def pair_wait(flight, after, name):
    srcs, lands, ssem, rsem = flight
    n = len(srcs)

    def body(*refs):
        place = _place()
        for j in range(n):
            cp = _pair_copy(refs[j], refs[n + j], j, refs[2 * n], refs[2 * n + 1], place)
            cp.wait_send()
            cp.wait_recv()

    out = pl.pallas_call(
        body, name=name, in_specs=[IN_HBM] * (2 * n) + [IN_SEM, IN_SEM, ANY], out_specs=[IN_HBM] * (2 * n),
        out_shape=[pltpu.HBM(b.shape, b.dtype) for b in list(srcs) + list(lands)],
        input_output_aliases={i: i for i in range(2 * n)},
        compiler_params=pltpu.CompilerParams(has_side_effects=SIDE_EFFECT),
    )(*srcs, *lands, ssem, rsem, after)
    return list(out[:n]), list(out[n:])


def _row_tile(rows, cap=512):
    return max(t for t in range(16, min(rows, cap) + 1, 16) if rows % t == 0)


def pair_sum(mine, theirs, core, name):
    _, R, C = mine.shape
    half = R // 2
    tr = _row_tile(half)
    nt = half // tr

    def body(c_ref, a_ref, b_ref, o_ref):
        o_ref[...] = (a_ref[...].astype(f32) + b_ref[...].astype(f32)).astype(bf16)

    return pl.pallas_call(
        body, name=name, out_shape=jax.ShapeDtypeStruct(theirs.shape, bf16),
        grid_spec=pltpu.PrefetchScalarGridSpec(
            num_scalar_prefetch=1, grid=(4, nt),
            in_specs=[pl.BlockSpec((None, tr, C), lambda s, i, c_ref: (s, c_ref[0] * nt + i, 0)),
                      pl.BlockSpec((None, tr, C), lambda s, i, c_ref: (s, i, 0))],
            out_specs=pl.BlockSpec((None, tr, C), lambda s, i, c_ref: (s, i, 0))),
        compiler_params=_cp(("parallel", "parallel")),
    )(jnp.reshape(core, (1,)).astype(jnp.int32), mine, theirs)


def final_exchange(fins):
    n = len(fins)

    def body(*refs):
        outs = refs[n:2 * n]
        ssem, rsem = refs[2 * n:]
        x, y, c = _place()
        cps = [pltpu.make_async_remote_copy(
            src_ref=_half(outs[j], c), dst_ref=_half(outs[j], c), send_sem=ssem.at[j], recv_sem=rsem.at[j],
            device_id=(x, y, 1 - c), device_id_type=MESH) for j in range(n)]
        for cp in cps:
            cp.start()
        for cp in cps:
            cp.wait()

    return pl.pallas_call(
        body, name="final_exchange", in_specs=[ANY] * n, out_specs=[ANY] * n,
        out_shape=[jax.ShapeDtypeStruct(f.shape, f.dtype) for f in fins],
        input_output_aliases={j: j for j in range(n)},
        scratch_shapes=[pltpu.SemaphoreType.DMA((n,)), pltpu.SemaphoreType.DMA((n,))],
    )(*fins)


def adamw_big(w, m, v, gs, row0, name):
    L, R, C = w.shape
    tr = _row_tile(math.gcd(R, row0) if row0 else R, max(16, 262144 // C // 16 * 16))
    b0 = row0 // tr

    def body(*refs):
        w_ref, m_ref, v_ref = refs[:3]
        g_refs = refs[3:3 + L]
        g_ref, d_ref, nm_ref, nv_ref = refs[3 + L:]
        g = g_refs[0][...]
        for l in range(1, L):
            g = jnp.where(pl.program_id(0) == l, g_refs[l][...], g)
        d, nm, nv = _adamw_math(w_ref[...], g, m_ref[...], v_ref[...])
        g_ref[...] = g
        d_ref[...] = d
        nm_ref[...] = nm
        nv_ref[...] = nv

    own = pl.BlockSpec((None, tr, C), lambda l, i: (l, i, 0))
    off = pl.BlockSpec((tr, C), lambda l, i: (b0 + i, 0))
    return pl.pallas_call(
        body, name=name, grid=(L, R // tr), in_specs=[own, own, own] + [off] * L, out_specs=[own] * 4,
        out_shape=[jax.ShapeDtypeStruct((L, R, C), f32)] * 4, compiler_params=_cp(("parallel", "parallel")),
    )(w, m, v, *gs)


def _adamw_math(w, g, m, v):
    m = B1 * m + (1.0 - B1) * g
    v = B2 * v + (1.0 - B2) * (g * g)
    m_hat = m / (1.0 - B1 ** STEP)
    v_hat = v / (1.0 - B2 ** STEP)
    delta = -LR * (m_hat / (jnp.sqrt(v_hat) + AEPS) + WD * w)
    return delta, m, v


def adamw_small(w, m, v, g):
    def body(w_ref, m_ref, v_ref, g_ref, d_ref, nm_ref, nv_ref):
        d, nm, nv = _adamw_math(w_ref[...], g_ref[...], m_ref[...], v_ref[...])
        d_ref[...] = d
        nm_ref[...] = nm
        nv_ref[...] = nv

    return pl.pallas_call(body, name="adamw_small", out_shape=[jax.ShapeDtypeStruct(w.shape, f32)] * 3)(w, m, v, g)


CONV =(("conv_qkv_b", 2), ("ffn_conv_w", 2))
SMALL = ("rel_bias", "norm_mix_g", "norm_mem_g", "sinks_a", "a_log_b", "dt_bias_b", "out_norm_g_b", "norm_ffn_g",
         "ffn_conv_b", "final_norm_g")
WEIGHTS = ("rel_bias", "norm_mix_g", "norm_mem_g", "w_mem_kv", "w_out", "w_in_a", "sinks_a", "w_in_b", "conv_qkv_b",
           "a_log_b", "dt_bias_b", "out_norm_g_b", "norm_ffn_g", "w_gate_up", "ffn_conv_w", "ffn_conv_b", "w_down",
           "final_norm_g")
ARGS = ("x", "mem") + WEIGHTS + ("loss_target",) + tuple("m_" + n for n in WEIGHTS) + tuple("v_" + n for n in WEIGHTS)


def _rows(a, width):
    flat = a.reshape(-1)
    pad = (-flat.shape[0]) % (8 * width)
    if pad:
        flat = jnp.concatenate([flat, jnp.zeros((pad,), a.dtype)])
    return flat.reshape(-1, width)


def _nrows(shape, width):
    return _pad_to(-(-math.prod(shape) // width), 8)


def _pack(arrs, width, total_rows, dtype):
    parts = [_rows(a.astype(dtype), width) for a in arrs]
    used = sum(p.shape[0] for p in parts)
    if total_rows > used:
        parts.append(jnp.zeros((total_rows - used, width), dtype))
    return jnp.concatenate(parts, axis=0)


def _unpack(buf, shapes, width):
    out, r = [], 0
    for s in shapes:
        n = _nrows(s, width)
        out.append(buf[r:r + n].reshape(-1)[:math.prod(s)].reshape(s))
        r += n
    return out


def _pad_to(n, mult):
    return -(-n // mult) * mult


def kernel(x, mem, rel_bias, norm_mix_g, norm_mem_g, w_mem_kv, w_out, w_in_a, sinks_a, w_in_b, conv_qkv_b, a_log_b, dt_bias_b, out_norm_g_b, norm_ffn_g, w_gate_up, ffn_conv_w, ffn_conv_b, w_down, final_norm_g, loss_target, m_rel_bias, m_norm_mix_g, m_norm_mem_g, m_w_mem_kv, m_w_out, m_w_in_a, m_sinks_a, m_w_in_b, m_conv_qkv_b, m_a_log_b, m_dt_bias_b, m_out_norm_g_b, m_norm_ffn_g, m_w_gate_up, m_ffn_conv_w, m_ffn_conv_b, m_w_down, m_final_norm_g, v_rel_bias, v_norm_mix_g, v_norm_mem_g, v_w_mem_kv, v_w_out, v_w_in_a, v_sinks_a, v_w_in_b, v_conv_qkv_b, v_a_log_b, v_dt_bias_b, v_out_norm_g_b, v_norm_ffn_g, v_w_gate_up, v_ffn_conv_w, v_ffn_conv_b, v_w_down, v_final_norm_g):
    A = dict(zip(ARGS, (x, mem, rel_bias, norm_mix_g, norm_mem_g, w_mem_kv, w_out, w_in_a, sinks_a, w_in_b, conv_qkv_b, a_log_b, dt_bias_b, out_norm_g_b, norm_ffn_g, w_gate_up, ffn_conv_w, ffn_conv_b, w_down, final_norm_g, loss_target, m_rel_bias, m_norm_mix_g, m_norm_mem_g, m_w_mem_kv, m_w_out, m_w_in_a, m_sinks_a, m_w_in_b, m_conv_qkv_b, m_a_log_b, m_dt_bias_b, m_out_norm_g_b, m_norm_ffn_g, m_w_gate_up, m_ffn_conv_w, m_ffn_conv_b, m_w_down, m_final_norm_g, v_rel_bias, v_norm_mix_g, v_norm_mem_g, v_w_mem_kv, v_w_out, v_w_in_a, v_sinks_a, v_w_in_b, v_conv_qkv_b, v_a_log_b, v_dt_bias_b, v_out_norm_g_b, v_norm_ffn_g, v_w_gate_up, v_ffn_conv_w, v_ffn_conv_b, v_w_down, v_final_norm_g)))
    chip = 2 * lax.axis_index("x") + lax.axis_index("y")
    core = lax.axis_index("c")

    def own_slot(shard):
        return lax.dynamic_update_index_in_dim(lax.empty((4,) + shard.shape, shard.dtype), shard, chip, 0)

    def bslot(w, tie=0.0):
        return own_slot((w + tie).astype(bf16))

    early = {
        ("w_in", 0): [bslot(w_in_a[0])],
        ("w_mem", 0): [bslot(w_mem_kv[0]), own_slot(ffn_conv_w.reshape(6, -1))],
        ("w_out", 0): [bslot(w_out[0])],
    }
    flights_a, gone = gather_start(list(early.values()), "gather_start_first")
    z = gone[0, 0]
    late = {
        ("w_gu", 0): [bslot(w_gate_up[0], z)], ("w_down", 0): [bslot(w_down[0], z)],
        ("w_in", 1): [bslot(w_in_b[0], z)], ("w_mem", 1): [bslot(w_mem_kv[1], z), own_slot(conv_qkv_b[0] + z)],
        ("w_out", 1): [bslot(w_out[1], z)], ("w_gu", 1): [bslot(w_gate_up[1], z)], ("w_down", 1): [bslot(w_down[1], z)],
    }
    flights_b, started_all = gather_start(list(late.values()), "gather_start_rest")
    flights = dict(zip(list(early) + list(late), flights_a + flights_b))
    P = {"rel_bias": rel_bias, "sinks": sinks_a[0], "a_log": a_log_b[0], "dt_bias": dt_bias_b[0],
         "out_norm_g": out_norm_g_b[0], "g_mix": norm_mix_g, "g_mem": norm_mem_g, "g_ffn": norm_ffn_g,
         "g_fin": final_norm_g, "ffn_cb": [ffn_conv_b[0], ffn_conv_b[1]], "w_mem": [None, None], "w_out": [None, None],
         "w_gu": [None, None], "w_down": [None, None], "ffn_cw": [None, None]}

    def rows4(g):
        return g.reshape(4 * g.shape[1], g.shape[2])

    def arrive(key, after):
        if key not in flights:
            return
        got = gather_wait(flights.pop(key), started_all if key == ("w_in", 0) else after, "gather_wait_%s%d" % key)
        name, i = key
        if name == "w_in":
            P["w_in_a" if i == 0 else "w_in_b"] = (_lay_in_a if i == 0 else _lay_in_b)(_unchip_cols(got[0]))
        elif name == "w_mem":
            P["w_mem"][i] = rows4(got[0])
            if i == 0:
                cw = _unchip_cols(got[1]).reshape(2, 3, D_FF)
                P["ffn_cw"] = [cw[0], cw[1]]
            else:
                P["conv_qkv"] = _unchip_cols(got[1])
        elif name == "w_out":
            P["w_out"][i] = _lay_out_a(rows4(got[0])) if i == 0 else rows4(got[0])
        elif name == "w_gu":
            P["w_gu"][i] = got[0]
        else:
            P["w_down"][i] = rows4(got[0])

    def chip_rows(g):
        return g.reshape(4, g.shape[0] // 4, g.shape[-1])

    sent, started, pending = {}, [], []

    def finish(after):
        key, names, flight = pending.pop()
        tag = "%s%d" % key
        partial, theirs = pair_wait(flight, after, "pair_wait_" + tag)
        pair = [pair_sum(p, t, core, "pair_sum_%s%d" % (nm, key[1])) for p, t, nm in zip(partial, theirs, names)]
        flight, token = scatter_start(pair, "scatter_start_" + tag)
        sent[key] = (names, flight, token)
        started.append(token[0, 0])

    def ready(key, G, dep):
        kind, i = key
        if kind == "tick":
            finish(G["w_out"][i])
        else:
            if kind == "ffn":
                if pending:
                    finish(G["w_gu"][i])
                names, partial = ("gu", "down"), [G["w_gu"][i], chip_rows(G["w_down"][i]).astype(bf16)]
            else:
                g_out = _unlay_out_a(G["w_out"][0]) if i == 0 else G["w_out"][1]
                g_in = _unlay_in_a(G["w_in_a"]) if i == 0 else _unlay_in_b(G["w_in_b"])
                names = ("out", "in", "mem")
                partial = [chip_rows(g_out).astype(bf16), _chip_cols(g_in).astype(bf16),
                           chip_rows(G["w_mem"][i]).astype(bf16)]
            flight, token = pair_start(partial, "pair_start_%s%d" % key)
            pending.append((key, names, flight))
            started.append(token[0, 0])
        if dep is not None:
            while started:
                dep = dep + started.pop()
        return dep

    P["arrive"], P["ready"] = arrive, ready

    loss, dx, G = _local_step(x[0], mem[0], loss_target[0], P)
    gfull = _grads_to_ref(G)
    finish(dx)

    fin, after = {}, sent["mix", 0][2]
    for key in (("ffn", 1), ("mix", 1), ("ffn", 0), ("mix", 0)):
        names, flight, _ = sent[key]
        pair, arrived = scatter_wait(flight, after, "scatter_wait_%s%d" % key)
        for nm, p, r in zip(names, pair, arrived):
            after = fin[nm, key[1]] = sum_slots(p, r, chip, core, "sum_slots_%s%d" % (nm, key[1]))
    order = list(fin)
    done = dict(zip(order, final_exchange([fin[k] for k in order])))

    sm_shapes = [A[n].shape for n in SMALL] + [gfull[n].shape for n, _ in CONV] + [(LANE,)]
    sm_rows = _pad_to(sum(_nrows(s, LANE) for s in sm_shapes), 8)
    sbuf = _pack([gfull[n] for n in SMALL] + [gfull[n] for n, _ in CONV] + [loss[0]], LANE, sm_rows, f32)
    tot = _unpack(allreduce_small(sbuf), sm_shapes, LANE)
    gsmall = dict(zip(SMALL, tot[:len(SMALL)]))
    for (n, axis), t in zip(CONV, tot[len(SMALL):len(SMALL) + len(CONV)]):
        sh = A[n].shape[axis]
        gsmall[n] = lax.dynamic_slice_in_dim(t, chip * sh, sh, axis)
    loss_out = tot[-1][0]

    out = {}
    plan = (("w_gate_up", [done["gu", 0], done["gu", 1]]), ("w_down", [done["down", 0], done["down", 1]]),
            ("w_out", [done["out", 0], done["out", 1]]), ("w_mem_kv", [done["mem", 0], done["mem", 1]]),
            ("w_in_a", [done["in", 0]]), ("w_in_b", [done["in", 1]]))
    for n, gs in plan:
        shape3 = (len(gs),) + gs[0].shape
        res = adamw_big(A[n].reshape(shape3), A["m_" + n].reshape(shape3), A["v_" + n].reshape(shape3), gs, 0,
                        "adamw_" + n)
        for key, r in zip(("grad_", "delta_", "new_m_", "new_v_"), res):
            out[key + n] = r.reshape(A[n].shape)
    names = SMALL + tuple(n for n, _ in CONV)
    shapes = [A[n].shape for n in names]
    rows = _pad_to(sum(_nrows(s, LANE) for s in shapes), 8)
    packs = [_pack([src[n] for n in names], LANE, rows, f32)
             for src in ({n: A[n] for n in names}, {n: A["m_" + n] for n in names}, {n: A["v_" + n] for n in names}, gsmall)]
    res = adamw_small(*packs)
    for key, r in zip(("delta_", "new_m_", "new_v_"), res):
        for n, a in zip(names, _unpack(r, shapes, LANE)):
            out[key + n] = a
    for n in names:
        out["grad_" + n] = gsmall[n]
    return (loss_out, dx[None], *[out["grad_" + n] for n in WEIGHTS], *[out["delta_" + n] for n in WEIGHTS],
            *[out["new_m_" + n] for n in WEIGHTS], *[out["new_v_" + n] for n in WEIGHTS])
```

```python
import functools
import math

import numpy as np
import jax
import jax.numpy as jnp
from jax import lax
from jax.experimental import pallas as pl
from jax.experimental.pallas import tpu as pltpu

f32 = jnp.float32
bf16 = jnp.bfloat16
HI = lax.Precision.HIGHEST
MESH = pl.DeviceIdType.MESH

D = 1024
MEM_LEN = 256
EPS = 1e-6
A_HEADS, A_KV, A_DH = 12, 2, 64
A_Q = 768
BLK = 128
N_BUCKETS, MAX_DIST = 32, 128
B_QK, B_V, B_DH = 384, 768, 128
B_QKV = 1536
CHUNK = 64
X_Q = 256
D_FF = 2816
LANE = 128
VMEM_LIMIT = 56 * 1024 * 1024
MM_ROWS = 1024

LR, B1, B2, AEPS, WD, STEP = 0.001, 0.9, 0.999, 1e-08, 0.01, 10


def _cp(sem=None):
    return pltpu.CompilerParams(dimension_semantics=sem, vmem_limit_bytes=VMEM_LIMIT)


def _dg(a, b, ca, cb, prec=None):
    return lax.dot_general(a, b, (((ca,), (cb,)), ((), ())), precision=prec, preferred_element_type=f32)


@jax.custom_vjp
def bdot(a, b):
    return _dg(a.astype(bf16), b.astype(bf16), 1, 0)


def _bdot_f(a, b):
    return bdot(a, b), (a, b)


def _bdot_b(res, g):
    a, b = res
    gb = g.astype(bf16)
    return _dg(gb, b.astype(bf16), 1, 1), _dg(a.astype(bf16), gb, 0, 0)


bdot.defvjp(_bdot_f, _bdot_b)


@jax.custom_vjp
def bdot_nt(a, b):
    return _dg(a.astype(bf16), b.astype(bf16), 1, 1)


def _bdot_nt_f(a, b):
    return bdot_nt(a, b), (a, b)


def _bdot_nt_b(res, g):
    a, b = res
    gb = g.astype(bf16)
    return _dg(gb, b.astype(bf16), 1, 0), _dg(gb, a.astype(bf16), 0, 0)


bdot_nt.defvjp(_bdot_nt_f, _bdot_nt_b)


def _shift_rows(x, s, down):
    n = x.shape[0]
    row = lax.broadcasted_iota(jnp.int32, x.shape, 0)
    if down:
        return jnp.where(row >= s, pltpu.roll(x, s, 0), 0.0)
    return jnp.where(row < n - s, pltpu.roll(x, n - s, 0), 0.0)


@functools.partial(jax.custom_vjp, nondiff_argnums=(1,))
def shift_down(x, s):
    return _shift_rows(x, s, True)


def _sd_f(x, s):
    return _shift_rows(x, s, True), None


def _sd_b(s, _, g):
    return (_shift_rows(g, s, False),)


shift_down.defvjp(_sd_f, _sd_b)


def _sigmoid(x):
    return 1.0 / (1.0 + jnp.exp(-x))


def _silu(x):
    return x * _sigmoid(x)


def _rms(x, g):
    return x * lax.rsqrt(jnp.mean(x * x, axis=-1, keepdims=True) + EPS) * g


def _tile(n, cap):
    u = n // LANE
    best = 1
    for d in range(1, u + 1):
        if u % d == 0 and d * LANE <= cap:
            best = d
    return best * LANE


def mm_nn(a, w, res=None, out_dtype=f32, name="mm_nn"):
    M, K = a.shape
    N = w.shape[1]
    tm, tn = min(MM_ROWS, M), _tile(N, 1024)

    def body(*refs):
        if res is None:
            a_ref, w_ref, o_ref = refs
            o_ref[...] = _dg(a_ref[...].astype(bf16), w_ref[...], 1, 0).astype(out_dtype)
        else:
            a_ref, w_ref, r_ref, o_ref = refs
            o_ref[...] = (r_ref[...] + _dg(a_ref[...].astype(bf16), w_ref[...], 1, 0)).astype(out_dtype)

    in_specs = [pl.BlockSpec((tm, K), lambda n, m: (m, 0)), pl.BlockSpec((K, tn), lambda n, m: (0, n))]
    args = [a, w]
    if res is not None:
        in_specs.append(pl.BlockSpec((tm, tn), lambda n, m: (m, n)))
        args.append(res)
    return pl.pallas_call(
        body, name=name, grid=(N // tn, M // tm), in_specs=in_specs,
        out_specs=pl.BlockSpec((tm, tn), lambda n, m: (m, n)),
        out_shape=jax.ShapeDtypeStruct((M, N), out_dtype),
        compiler_params=_cp(("parallel", "parallel")),
    )(*args)


def mm_res_norm(a, w, res, g, name):
    pieces = a if isinstance(a, tuple) else (a,)
    na = len(pieces)
    M, K = pieces[0].shape[0], sum(p.shape[1] for p in pieces)
    tm = min(MM_ROWS, M)

    def body(*refs):
        w_ref, r_ref, g_ref, o_ref, n_ref = refs[na:]
        h = r_ref[...] + _dg(_cols(refs[:na]).astype(bf16), w_ref[...], 1, 0)
        o_ref[...] = h
        n_ref[...] = _rms(h, g_ref[...]).astype(bf16)

    tok = pl.BlockSpec((tm, D), lambda m: (m, 0))
    return pl.pallas_call(
        body, name=name, grid=(M // tm,),
        in_specs=[pl.BlockSpec((tm, p.shape[1]), lambda m: (m, 0)) for p in pieces]
        + [pl.BlockSpec((K, D), lambda m: (0, 0)), tok, pl.BlockSpec((1, D), lambda m: (0, 0))],
        out_specs=[tok, tok],
        out_shape=[jax.ShapeDtypeStruct((M, D), f32), jax.ShapeDtypeStruct((M, D), bf16)],
        compiler_params=_cp(("parallel",)),
    )(*pieces, w, res, g.reshape(1, D))


def mm_nt(dy, w, out_dtype=f32, name="mm_nt"):
    M, N = dy.shape
    K = w.shape[0]
    tm, tn = min(MM_ROWS, M), _tile(N, 1024)
    assert out_dtype == f32 or tn == N

    def body(dy_ref, w_ref, o_ref):
        part = _dg(dy_ref[...].astype(bf16), w_ref[...], 1, 1)
        if tn == N:
            o_ref[...] = part.astype(out_dtype)
        else:
            @pl.when(pl.program_id(1) == 0)
            def _():
                o_ref[...] = jnp.zeros_like(o_ref)
            o_ref[...] += part

    return pl.pallas_call(
        body, name=name, grid=(M // tm, N // tn),
        in_specs=[pl.BlockSpec((tm, tn), lambda m, n: (m, n)), pl.BlockSpec((K, tn), lambda m, n: (0, n))],
        out_specs=pl.BlockSpec((tm, K), lambda m, n: (m, 0)),
        out_shape=jax.ShapeDtypeStruct((M, K), out_dtype),
        compiler_params=_cp(("parallel", "arbitrary")),
    )(dy, w)


NORM_ROWS = 1024


def _acc_then_norm_bwd(part, steps, h_ref, g_ref, r_ref, o_ref, dg_ref):
    k = pl.program_id(1)

    @pl.when((pl.program_id(0) == 0) & (k == 0))
    def _():
        dg_ref[...] = jnp.zeros_like(dg_ref)

    @pl.when(k == 0)
    def _():
        o_ref[...] = part

    @pl.when(k > 0)
    def _():
        o_ref[...] += part

    @pl.when(k == steps - 1)
    def _():
        _, vjp = jax.vjp(_rms, h_ref[...], g_ref[...])
        dh, dg = vjp(o_ref[...])
        o_ref[...] = r_ref[...] + dh
        dg_ref[...] += dg


def mm_nt_norm(dy, w, norm, name):
    pieces = dy if isinstance(dy, tuple) else (dy,)
    nd = len(pieces)
    M, N = pieces[0].shape[0], sum(p.shape[1] for p in pieces)
    tm, tn = (min(NORM_ROWS, M), _tile(N, 1024)) if nd == 1 else (min(512, M), N)

    def body(*refs):
        w_ref, h_ref, g_ref, r_ref, o_ref, dg_ref = refs[nd:]
        _acc_then_norm_bwd(_dg(_cols(refs[:nd]).astype(bf16), w_ref[...], 1, 1), N // tn, h_ref, g_ref, r_ref, o_ref,
                           dg_ref)

    tok = pl.BlockSpec((tm, D), lambda m, n: (m, 0))
    vec = pl.BlockSpec((1, D), lambda m, n: (0, 0))
    return pl.pallas_call(
        body, name=name, grid=(M // tm, N // tn),
        in_specs=[pl.BlockSpec((tm, tn if nd == 1 else p.shape[1]), lambda m, n: (m, n)) for p in pieces]
        + [pl.BlockSpec((D, tn), lambda m, n: (0, n)), tok, vec, tok],
        out_specs=[tok, vec],
        out_shape=[jax.ShapeDtypeStruct((M, D), f32), jax.ShapeDtypeStruct((1, D), f32)],
        compiler_params=_cp(("arbitrary", "arbitrary")),
    )(*pieces, w, norm[0], norm[1].reshape(1, D), norm[2])


def _cols(refs):
    return refs[0][...] if len(refs) == 1 else jnp.concatenate([r[...] for r in refs], axis=1)


def mm_tn(a, dy, name="mm_tn"):
    pieces = a if isinstance(a, tuple) else (a,)
    dpieces = dy if isinstance(dy, tuple) else (dy,)
    na, nd = len(pieces), len(dpieces)
    M, K = pieces[0].shape[0], sum(p.shape[1] for p in pieces)
    N = sum(p.shape[1] for p in dpieces)
    tm = min(MM_ROWS, M)
    tk = _tile(K, 1408) if na == 1 else K
    tn = _tile(N, 1024) if nd == 1 else N

    def body(*refs):
        o_ref = refs[-1]

        @pl.when(pl.program_id(2) == 0)
        def _():
            o_ref[...] = jnp.zeros_like(o_ref)
        o_ref[...] += _dg(_cols(refs[:na]).astype(bf16), _cols(refs[na:na + nd]).astype(bf16), 0, 0)

    a_specs = [pl.BlockSpec((tm, tk if na == 1 else p.shape[1]), lambda k, n, m: (m, k)) for p in pieces]
    d_specs = [pl.BlockSpec((tm, tn if nd == 1 else p.shape[1]), lambda k, n, m: (m, n)) for p in dpieces]
    return pl.pallas_call(
        body, name=name, grid=(K // tk, N // tn, M // tm), in_specs=a_specs + d_specs,
        out_specs=pl.BlockSpec((tk, tn), lambda k, n, m: (k, n)),
        out_shape=jax.ShapeDtypeStruct((K, N), f32),
        compiler_params=_cp(("parallel", "parallel", "arbitrary")),
    )(*pieces, *dpieces)


def rms_fwd(h, g, name):
    S = h.shape[0]
    t = min(512, S)

    def body(h_ref, g_ref, o_ref):
        o_ref[...] = _rms(h_ref[...], g_ref[...]).astype(bf16)

    return pl.pallas_call(
        body, name=name, grid=(S // t,),
        in_specs=[pl.BlockSpec((t, D), lambda i: (i, 0)), pl.BlockSpec((1, D), lambda i: (0, 0))],
        out_specs=pl.BlockSpec((t, D), lambda i: (i, 0)),
        out_shape=jax.ShapeDtypeStruct((S, D), bf16),
        compiler_params=_cp(("parallel",)),
    )(h, g.reshape(1, D))


def loss_head(h, g, target):
    S = h.shape[0]
    t = min(512, S)

    def f(hh, gg, tt):
        err = _rms(hh, gg) - tt
        return 0.5 * jnp.sum(jnp.mean(err * err, axis=-1, keepdims=True), axis=0, keepdims=True)

    def body(h_ref, g_ref, t_ref, loss_ref, dh_ref, dg_ref):
        @pl.when(pl.program_id(0) == 0)
        def _():
            dg_ref[...] = jnp.zeros_like(dg_ref)
            loss_ref[...] = jnp.zeros_like(loss_ref)
        val, vjp = jax.vjp(lambda a, b: f(a, b, t_ref[...]), h_ref[...], g_ref[...])
        dh, dg = vjp(jnp.ones((1, 1), f32))
        dh_ref[...] = dh
        dg_ref[...] += dg
        loss_ref[...] += jnp.broadcast_to(val, loss_ref.shape)

    tok = pl.BlockSpec((t, D), lambda i: (i, 0))
    vec = pl.BlockSpec((1, D), lambda i: (0, 0))
    return pl.pallas_call(
        body, name="loss_head", grid=(S // t,), in_specs=[tok, vec, tok],
        out_specs=[pl.BlockSpec((1, LANE), lambda i: (0, 0)), tok, vec],
        out_shape=[jax.ShapeDtypeStruct((1, LANE), f32), jax.ShapeDtypeStruct((S, D), f32),
                   jax.ShapeDtypeStruct((1, D), f32)],
        compiler_params=_cp(("arbitrary",)),
    )(h, g.reshape(1, D), target)


def memkv_fwd(mem, g, w, name):
    def body(m_ref, g_ref, w_ref, o_ref):
        o_ref[...] = _dg(_rms(m_ref[...], g_ref[...]).astype(bf16), w_ref[...], 1, 0)

    return pl.pallas_call(
        body, name=name, out_shape=jax.ShapeDtypeStruct((MEM_LEN, 2 * X_Q), f32), compiler_params=_cp(),
    )(mem, g.reshape(1, D), w)


def memkv_bwd(mem, g, w, dkv, name):
    def body(m_ref, g_ref, w_ref, d_ref, dg_ref, dw_ref):
        n, vjp = jax.vjp(lambda gg: _rms(m_ref[...], gg), g_ref[...])
        db = d_ref[...].astype(bf16)
        dw_ref[...] = _dg(n.astype(bf16), db, 0, 0)
        dg_ref[...] = vjp(_dg(db, w_ref[...], 1, 1))[0]

    return pl.pallas_call(
        body, name=name,
        out_shape=[jax.ShapeDtypeStruct((1, D), f32), jax.ShapeDtypeStruct((D, 2 * X_Q), f32)],
        compiler_params=_cp(),
    )(mem, g.reshape(1, D), w, dkv)


def _xattn_f(xq, mk, mv):
    lane = lax.broadcasted_iota(jnp.int32, (1, X_Q), 1)
    out = jnp.zeros(xq.shape, f32)
    for hd in range(4):
        msk = (lane // 64 == hd).astype(f32)
        s = bdot_nt(xq * msk, mk) * (64 ** -0.5)
        m = lax.stop_gradient(jnp.max(s, axis=-1, keepdims=True))
        p = jnp.exp(s - m)
        p = p / jnp.sum(p, axis=-1, keepdims=True)
        out = out + bdot(p, mv * msk)
    return out


def xattn_fwd(proj, col, kv, name):
    S = proj.shape[0]
    t = min(512, S)
    cb = col // X_Q

    def body(q_ref, k_ref, v_ref, o_ref):
        o_ref[...] = _xattn_f(q_ref[...].astype(f32), k_ref[...], v_ref[...]).astype(bf16)

    return pl.pallas_call(
        body, name=name, grid=(S // t,),
        in_specs=[pl.BlockSpec((t, X_Q), lambda i: (i, cb)), pl.BlockSpec((MEM_LEN, X_Q), lambda i: (0, 0)),
                  pl.BlockSpec((MEM_LEN, X_Q), lambda i: (0, 1))],
        out_specs=pl.BlockSpec((t, X_Q), lambda i: (i, 0)),
        out_shape=jax.ShapeDtypeStruct((S, X_Q), bf16),
        compiler_params=_cp(("parallel",)),
    )(proj, kv, kv)


def xattn_bwd(proj, col, kv, dmix, name):
    S = proj.shape[0]
    t = min(512, S)
    cb = col // X_Q

    def body(q_ref, k_ref, v_ref, do_ref, dq_ref, dk_ref, dv_ref):
        @pl.when(pl.program_id(0) == 0)
        def _():
            dk_ref[...] = jnp.zeros_like(dk_ref)
            dv_ref[...] = jnp.zeros_like(dv_ref)
        _, vjp = jax.vjp(_xattn_f, q_ref[...].astype(f32), k_ref[...], v_ref[...])
        dq, dk, dv = vjp(do_ref[...])
        dq_ref[...] = dq.astype(bf16)
        dk_ref[...] += dk
        dv_ref[...] += dv

    kvb = pl.BlockSpec((MEM_LEN, X_Q), lambda i: (0, 0))
    dq, dk, dv = pl.pallas_call(
        body, name=name, grid=(S // t,),
        in_specs=[pl.BlockSpec((t, X_Q), lambda i: (i, cb)), kvb,
                  pl.BlockSpec((MEM_LEN, X_Q), lambda i: (0, 1)), pl.BlockSpec((t, X_Q), lambda i: (i, 3))],
        out_specs=[pl.BlockSpec((t, X_Q), lambda i: (i, 0)), kvb, kvb],
        out_shape=[jax.ShapeDtypeStruct((S, X_Q), bf16), jax.ShapeDtypeStruct((MEM_LEN, X_Q), f32),
                   jax.ShapeDtypeStruct((MEM_LEN, X_Q), f32)],
        compiler_params=_cp(("arbitrary",)),
    )(proj, kv, kv, dmix)
    return dq, jnp.concatenate([dk, dv], axis=1)


def _bucket_map():
    qi = np.arange(BLK)[:, None]
    kj = np.arange(2 * BLK)[None, :]
    n = np.maximum(BLK + qi - kj, 0)
    max_exact = N_BUCKETS // 2
    nf = np.maximum(n, 1).astype(np.float64)
    large = max_exact + (np.log(nf / max_exact) / math.log(MAX_DIST / max_exact)
                         * (N_BUCKETS - max_exact)).astype(np.int32)
    large = np.minimum(large, N_BUCKETS - 1)
    return np.where(n < max_exact, n, large).astype(np.int32)


def bias_build(rel_bias):
    def body(rb_ref, bk_ref, o_ref):
        bk = bk_ref[...]
        for h in range(A_HEADS):
            acc = jnp.zeros((BLK, 2 * BLK), f32)
            for b in range(N_BUCKETS):
                acc = jnp.where(bk == b, rb_ref[b, h], acc)
            o_ref[h] = acc

    return pl.pallas_call(
        body, name="bias_build",
        in_specs=[pl.BlockSpec(memory_space=pltpu.SMEM), pl.BlockSpec(memory_space=pltpu.VMEM)],
        out_specs=pl.BlockSpec(memory_space=pltpu.VMEM),
        out_shape=jax.ShapeDtypeStruct((A_HEADS, BLK, 2 * BLK), f32), compiler_params=_cp(),
    )(rel_bias, jnp.asarray(_bucket_map()))


def bias_grad(dbias):
    def body(d_ref, bk_ref, o_ref):
        bk = bk_ref[...]
        row = lax.broadcasted_iota(jnp.int32, (N_BUCKETS, LANE), 0)
        lane = lax.broadcasted_iota(jnp.int32, (N_BUCKETS, LANE), 1)
        acc = jnp.zeros((N_BUCKETS, LANE), f32)
        for h in range(A_HEADS):
            d = d_ref[h]
            for b in range(N_BUCKETS):
                s = jnp.sum(jnp.where(bk == b, d, 0.0), keepdims=True)
                acc = acc + jnp.where((row == b) & (lane == h), s, 0.0)
        o_ref[...] = acc

    return pl.pallas_call(
        body, name="bias_grad", out_shape=jax.ShapeDtypeStruct((N_BUCKETS, LANE), f32), compiler_params=_cp(),
    )(dbias, jnp.asarray(_bucket_map()))


def _swa_f(qb, kp, kc, vp, vc, bias, sk, first):
    kband = jnp.concatenate([kp, kc], axis=0)
    vband = jnp.concatenate([vp, vc], axis=0)
    qi = lax.broadcasted_iota(jnp.int32, (BLK, 2 * BLK), 0)
    kj = lax.broadcasted_iota(jnp.int32, (BLK, 2 * BLK), 1)
    rel = kj - qi
    ok = (rel >= 1) & (rel <= BLK) & ((kj >= BLK) | jnp.logical_not(first))
    lane = lax.broadcasted_iota(jnp.int32, (1, LANE), 1)
    lane_b = lax.broadcasted_iota(jnp.int32, (BLK, LANE), 1)
    outs = []
    for p in range(A_HEADS // 2):
        qp = qb[:, LANE * p:LANE * (p + 1)]
        acc = jnp.zeros((BLK, LANE), f32)
        for g in range(2):
            h = g * (A_HEADS // 2) + p
            msk = (lane // A_DH == g).astype(f32)
            s = bdot_nt(qp * msk, kband) * (A_DH ** -0.5) + bias[h]
            s = jnp.where(ok, s, -1e30)
            skb = jnp.broadcast_to(sk[h:h + 1, :], (BLK, LANE))
            sink = jnp.sum(jnp.where(lane_b == 0, skb, 0.0), axis=-1, keepdims=True)
            m = lax.stop_gradient(jnp.maximum(jnp.max(s, axis=-1, keepdims=True), sink))
            e = jnp.exp(s - m)
            prob = e / (jnp.sum(e, axis=-1, keepdims=True) + jnp.exp(sink - m))
            acc = acc + bdot(prob, vband) * msk
        outs.append(acc)
    return jnp.concatenate(outs, axis=1)


def _swa_specs(nb, rev):
    bi = (lambda i: nb - 1 - i) if rev else (lambda i: i)
    return [
        pl.BlockSpec((BLK, A_Q), lambda i: (bi(i), 0)),
        pl.BlockSpec((BLK, LANE), lambda i: (jnp.maximum(bi(i) - 1, 0), 6)),
        pl.BlockSpec((BLK, LANE), lambda i: (bi(i), 6)),
        pl.BlockSpec((BLK, LANE), lambda i: (jnp.maximum(bi(i) - 1, 0), 7)),
        pl.BlockSpec((BLK, LANE), lambda i: (bi(i), 7)),
        pl.BlockSpec((A_HEADS, BLK, 2 * BLK), lambda i: (0, 0, 0)),
        pl.BlockSpec((16, LANE), lambda i: (0, 0)),
    ]


def swa_fwd(proj, bias, sk):
    S = proj.shape[0]
    nb = S // BLK

    def body(q_ref, kp_ref, kc_ref, vp_ref, vc_ref, b_ref, s_ref, o_ref):
        qkv = [r[...].astype(f32) for r in (q_ref, kp_ref, kc_ref, vp_ref, vc_ref)]
        o_ref[...] = _swa_f(*qkv, b_ref[...], s_ref[...], pl.program_id(0) == 0).astype(bf16)

    return pl.pallas_call(
        body, name="swa_fwd", grid=(nb,), in_specs=_swa_specs(nb, False),
        out_specs=pl.BlockSpec((BLK, A_Q), lambda i: (i, 0)),
        out_shape=jax.ShapeDtypeStruct((S, A_Q), bf16), compiler_params=_cp(("parallel",)),
    )(proj, proj, proj, proj, proj, bias, sk)


def swa_bwd(proj, bias, sk, dmix):
    S = proj.shape[0]
    nb = S // BLK

    def body(q_ref, kp_ref, kc_ref, vp_ref, vc_ref, b_ref, s_ref, do_ref, dqkv_ref, db_ref, ds_ref, ck, cv):
        i = pl.program_id(0)

        @pl.when(i == 0)
        def _():
            db_ref[...] = jnp.zeros_like(db_ref)
            ds_ref[...] = jnp.zeros_like(ds_ref)
            ck[...] = jnp.zeros_like(ck)
            cv[...] = jnp.zeros_like(cv)
        first = i == nb - 1
        qkv = [r[...].astype(f32) for r in (q_ref, kp_ref, kc_ref, vp_ref, vc_ref)]
        _, vjp = jax.vjp(lambda *a: _swa_f(*a, first), *qkv, b_ref[...], s_ref[...])
        dq, dkp, dkc, dvp, dvc, db, ds = vjp(do_ref[...])
        dqkv_ref[...] = jnp.concatenate([dq, dkc + ck[...], dvc + cv[...]], axis=1).astype(bf16)
        ck[...] = dkp
        cv[...] = dvp
        db_ref[...] += db
        ds_ref[...] += ds

    return pl.pallas_call(
        body, name="swa_bwd", grid=(nb,),
        in_specs=_swa_specs(nb, True) + [pl.BlockSpec((BLK, A_Q), lambda i: (nb - 1 - i, 0))],
        out_specs=[pl.BlockSpec((BLK, D), lambda i: (nb - 1 - i, 0)),
                   pl.BlockSpec((A_HEADS, BLK, 2 * BLK), lambda i: (0, 0, 0)),
                   pl.BlockSpec((16, LANE), lambda i: (0, 0))],
        out_shape=[jax.ShapeDtypeStruct((S, D), bf16), jax.ShapeDtypeStruct((A_HEADS, BLK, 2 * BLK), f32),
                   jax.ShapeDtypeStruct((16, LANE), f32)],
        scratch_shapes=[pltpu.VMEM((BLK, LANE), f32), pltpu.VMEM((BLK, LANE), f32)],
        compiler_params=_cp(("arbitrary",)),
    )(proj, proj, proj, proj, proj, bias, sk, dmix)


def _dnprep_f(xext, w, is_qk):
    c = (w[3:4] * xext + w[2:3] * shift_down(xext, 1) + w[1:2] * shift_down(xext, 2) + w[0:1] * shift_down(xext, 3))
    a = _silu(c)[HALO:]
    n = a * lax.rsqrt(jnp.sum(a * a, axis=-1, keepdims=True) + EPS)
    return jnp.where(is_qk, n, a)


def dnprep_fwd(proj, cw):
    S = proj.shape[0]
    nblk = B_QKV // LANE
    T = S

    def body(x_ref, w_ref, o_ref):
        is_qk = pl.program_id(0) < 2 * B_QK // LANE
        wv = w_ref[...]

        def tile(r0, first):
            o_ref[pl.ds(r0, T), :] = _dnprep_f(_glu_gext(x_ref, r0, first, T), wv, is_qk)

        tile(0, True)

    return pl.pallas_call(
        body, name="dnprep_fwd", grid=(nblk,),
        in_specs=[pl.BlockSpec((S, LANE), lambda j: (0, j)), pl.BlockSpec((4, LANE), lambda j: (0, j))],
        out_specs=pl.BlockSpec((S, LANE), lambda j: (0, j)),
        out_shape=jax.ShapeDtypeStruct((S, B_QKV), f32), compiler_params=_cp(("parallel",)),
    )(proj, cw)


def dnprep_bwd(proj, cw, dqkvn):
    S = proj.shape[0]
    nblk = B_QKV // LANE

    T = S

    def body(x_ref, w_ref, d_ref, dx_ref, dw_ref):
        is_qk = pl.program_id(0) < 2 * B_QK // LANE
        wv = w_ref[...]

        def tile(r0, first):
            _, vjp = jax.vjp(lambda a, b: _dnprep_f(a, b, is_qk), _glu_gext(x_ref, r0, first, T), wv)
            dx, dw = vjp(d_ref[pl.ds(r0, T), :])
            dx_ref[pl.ds(r0, T), :] = dx[HALO:].astype(bf16)
            if not first:
                dx_ref[pl.ds(r0 - HALO, HALO), :] += dx[:HALO]
            return dw

        dw_ref[...] = tile(0, True)

    col = pl.BlockSpec((S, LANE), lambda j: (0, j))
    wsp = pl.BlockSpec((4, LANE), lambda j: (0, j))
    return pl.pallas_call(
        body, name="dnprep_bwd", grid=(nblk,), in_specs=[col, wsp, col], out_specs=[col, wsp],
        out_shape=[jax.ShapeDtypeStruct((S, B_QKV), bf16), jax.ShapeDtypeStruct((4, B_QKV), f32)],
        compiler_params=_cp(("parallel",)),
    )(proj, cw, dqkvn)


def _hdot(a, b, ca=1, cb=0):
    return _dg(a, b, ca, cb, HI)


def _bdg(a, b, ca, cb):
    dn = (((ca,), (cb,)), ((0,), (0,)))
    ah, bh = a.astype(bf16), b.astype(bf16)
    al, bl = (a - ah.astype(f32)).astype(bf16), (b - bh.astype(f32)).astype(bf16)
    return (lax.dot_general(ah, bh, dn, preferred_element_type=f32)
            + lax.dot_general(ah, bl, dn, preferred_element_type=f32)
            + lax.dot_general(al, bh, dn, preferred_element_type=f32))


@jax.custom_vjp
def hbd(a, b):
    return _bdg(a, b, 2, 1)


@jax.custom_vjp
def hbd_nt(a, b):
    return _bdg(a, b, 2, 2)


@jax.custom_vjp
def hbd_tn(a, b):
    return _bdg(a, b, 1, 1)


hbd.defvjp(lambda a, b: (hbd(a, b), (a, b)), lambda r, g: (hbd_nt(g, r[1]), hbd_tn(r[0], g)))
hbd_nt.defvjp(lambda a, b: (hbd_nt(a, b), (a, b)), lambda r, g: (hbd(g, r[1]), hbd_tn(g, r[0])))
hbd_tn.defvjp(lambda a, b: (hbd_tn(a, b), (a, b)), lambda r, g: (hbd_nt(r[1], g), hbd(r[0], g)))


def _stack(xs):
    return jnp.concatenate([x[None] for x in xs], axis=0)


def _lane_col(x, j):
    lane = lax.broadcasted_iota(jnp.int32, (1, LANE), 1)
    return jnp.sum(jnp.where(lane == j, x, 0.0), axis=-1, keepdims=True)


def _tri_inv(a_mat):
    r = lax.broadcasted_iota(jnp.int32, (1, CHUNK, CHUNK), 1)
    c = lax.broadcasted_iota(jnp.int32, (1, CHUNK, CHUNK), 2)
    pw = -a_mat
    inv = (r == c).astype(f32) + pw
    for _ in range(5):
        pw = hbd(pw, pw)
        inv = inv + hbd(inv, pw)
    return inv


@jax.custom_vjp
def _tri_inv_known(a_mat, inv):
    return inv


_tri_inv_known.defvjp(lambda a, inv: (inv, inv),
                      lambda inv, g: (-hbd_tn(inv, hbd_nt(g, inv)), jnp.zeros_like(inv)))


def _dnc_f(q, k, v, seg, prm, inverse=_tri_inv):
    C = CHUNK
    B = q.shape[0]
    rows = seg.shape[0]
    beta_all = _sigmoid(seg)
    xx = seg + prm[1:2]
    g_all = -jnp.exp(prm[0:1]) * (jnp.maximum(xx, 0.0) + jnp.log(1.0 + jnp.exp(-jnp.abs(xx))))
    r2 = lax.broadcasted_iota(jnp.int32, (rows, rows), 0)
    c2 = lax.broadcasted_iota(jnp.int32, (rows, rows), 1)
    within = (r2 >= c2) & (r2 // C == c2 // C)
    gc_all = _hdot(within.astype(f32), g_all)
    beta = _stack([_lane_col(beta_all[C * j:C * (j + 1)], h) for j in range(rows // C) for h in range(6)])
    gc = _stack([_lane_col(gc_all[C * j:C * (j + 1)], 6 + h) for j in range(rows // C) for h in range(6)])
    r = lax.broadcasted_iota(jnp.int32, (1, C, C), 1)
    c = lax.broadcasted_iota(jnp.int32, (1, C, C), 2)
    incl = r >= c
    strict = r > c
    gct = [gc_all[C * j:C * (j + 1)].T for j in range(rows // C)]
    g_row = _stack([jnp.broadcast_to(gct[j][6 + h:7 + h, :], (C, C))
                    for j in range(rows // C) for h in range(6)])
    decay = jnp.where(incl, jnp.exp(jnp.where(incl, gc - g_row, 0.0)), 0.0)
    a_mat = beta * sbd_nt(k, k) * jnp.where(strict, decay, 0.0)
    eg = jnp.exp(gc)
    inv = inverse(a_mat)
    u = hbd(inv, beta * v)
    w = hbd(inv, (beta * eg) * k)
    qc = q * (B_DH ** -0.5)
    attn = sbd_nt(qc, k) * decay
    last = (lax.broadcasted_iota(jnp.int32, (1, C, 1), 1) == C - 1).astype(f32)
    g_last = jnp.sum(gc * last, axis=1, keepdims=True)
    dc = jnp.broadcast_to(jnp.exp(g_last), (B, 1, LANE)).reshape(B, LANE)
    return u, w, qc * eg, k * jnp.exp(g_last - gc), attn, dc, inv


def _b1(a, b, ca, cb):
    return lax.dot_general(a.astype(bf16), b.astype(bf16), (((ca,), (cb,)), ((0,), (0,))), preferred_element_type=f32)


@jax.custom_vjp
def sbd(a, b):
    return _b1(a, b, 2, 1)


@jax.custom_vjp
def sbd_nt(a, b):
    return _b1(a, b, 2, 2)


@jax.custom_vjp
def sbd_tn(a, b):
    return _b1(a, b, 1, 1)


sbd.defvjp(lambda a, b: (sbd(a, b), (a, b)), lambda r, g: (sbd_nt(g, r[1]), sbd_tn(r[0], g)))
sbd_nt.defvjp(lambda a, b: (sbd_nt(a, b), (a, b)), lambda r, g: (sbd(g, r[1]), sbd_tn(g, r[0])))
sbd_tn.defvjp(lambda a, b: (sbd_tn(a, b), (a, b)), lambda r, g: (sbd_nt(r[1], g), sbd(r[0], g)))


def _dns_f(S0, u, w, qd, kt, attn, dcrows):
    dc = _lane_col(dcrows, 0).reshape(6, 1, 1)
    delta = u - sbd(w, S0)
    out = sbd(qd, S0) + sbd(attn, delta)
    return out, dc * S0 + sbd_tn(kt, delta)


def _dnpost_f(o, z, grow):
    outs = []
    for h in range(6):
        oh = o[:, LANE * h:LANE * (h + 1)]
        outs.append(oh * lax.rsqrt(jnp.mean(oh * oh, axis=-1, keepdims=True) + EPS) * grow
                    * _silu(z[:, LANE * h:LANE * (h + 1)]))
    return jnp.concatenate(outs, axis=1)


def _hs(h):
    return slice(LANE * h, LANE * (h + 1))


DN_CHUNKS = 4


def _heads(ref, share):
    return _stack([ref[CHUNK * j:CHUNK * (j + 1), _hs(h // share)]
                   for j in range(ref.shape[0] // CHUNK) for h in range(6)])


def _put_heads(ref, val):
    for j in range(ref.shape[0] // CHUNK):
        for h in range(6):
            ref[CHUNK * j:CHUNK * (j + 1), _hs(h)] = val[6 * j + h].astype(ref.dtype)


def _dnc_in_specs():
    rows = CHUNK * DN_CHUNKS
    return [
        pl.BlockSpec((rows, B_QK), lambda n: (n, 0)),
        pl.BlockSpec((rows, B_QK), lambda n: (n, 1)),
        pl.BlockSpec((rows, B_V), lambda n: (n, 1)),
        pl.BlockSpec((rows, LANE), lambda n: (n, 20)),
        pl.BlockSpec((8, LANE), lambda n: (0, 0)),
    ]


def _dnc_out_specs(rev_nc=None, chunks=1):
    ci = (lambda n: n) if rev_nc is None else (lambda n: rev_nc - 1 - n)
    wide = pl.BlockSpec((CHUNK * chunks, B_V), lambda n: (ci(n), 0))
    return [wide, wide, wide, wide, pl.BlockSpec((chunks, 6, CHUNK, CHUNK), lambda n: (ci(n), 0, 0, 0)),
            pl.BlockSpec((chunks, 8, LANE), lambda n: (ci(n), 0, 0))]


def _dc_rows(dc):
    pad = jnp.zeros((2, LANE), f32)
    return _stack([jnp.concatenate([dc[6 * j:6 * (j + 1)], pad], axis=0) for j in range(dc.shape[0] // 6)])


def _dnc_shapes(S, mm=f32):
    nc = S // CHUNK
    return [jax.ShapeDtypeStruct((S, B_V), f32)] + [jax.ShapeDtypeStruct((S, B_V), mm)] * 3 + [
        jax.ShapeDtypeStruct((nc, 6, CHUNK, CHUNK), mm), jax.ShapeDtypeStruct((nc, 8, LANE), f32)]


def dnc_fwd(qkvn, proj, prm):
    S = proj.shape[0]

    def body(q_ref, k_ref, v_ref, s_ref, p_ref, u_ref, w_ref, qd_ref, kt_ref, at_ref, dc_ref, inv_ref):
        u, w, qd, kt, attn, dc, inv = _dnc_f(_heads(q_ref, 2), _heads(k_ref, 2), _heads(v_ref, 1), s_ref[...],
                                             p_ref[...])
        inv_ref[...] = inv.reshape(inv_ref.shape)
        _put_heads(u_ref, u)
        _put_heads(w_ref, w)
        _put_heads(qd_ref, qd)
        _put_heads(kt_ref, kt)
        at_ref[...] = attn.reshape(at_ref.shape).astype(at_ref.dtype)
        dc_ref[...] = _dc_rows(dc)

    outs = _dnc_out_specs(chunks=DN_CHUNKS)
    out = pl.pallas_call(
        body, name="dn_chunk_fwd", grid=(S // (CHUNK * DN_CHUNKS),), in_specs=_dnc_in_specs(),
        out_specs=outs + [outs[4]], out_shape=_dnc_shapes(S, bf16) + [_dnc_shapes(S)[4]],
        compiler_params=_cp(("parallel",)),
    )(qkvn, qkvn, qkvn, proj, prm)
    return out[:6], out[6]


def dnc_bwd(qkvn, proj, prm, inv, cots):
    S = proj.shape[0]

    def body(q_ref, k_ref, v_ref, s_ref, p_ref, inv_ref, du_ref, dw_ref, dqd_ref, dkt_ref, dat_ref, ddc_ref,
             dx_ref, dseg_ref, dprm_ref):
        @pl.when(pl.program_id(0) == 0)
        def _():
            dprm_ref[...] = jnp.zeros_like(dprm_ref)
        nb = 6 * DN_CHUNKS
        known = functools.partial(_tri_inv_known, inv=inv_ref[...].reshape(nb, CHUNK, CHUNK))
        _, vjp = jax.vjp(lambda *a: _dnc_f(*a, inverse=known)[:6], _heads(q_ref, 2), _heads(k_ref, 2),
                         _heads(v_ref, 1), s_ref[...], p_ref[...])
        ddc = jnp.concatenate([ddc_ref[j, 0:6, :] for j in range(DN_CHUNKS)], axis=0)
        dq, dk, dv, dseg, dprm = vjp((_heads(du_ref, 1), _heads(dw_ref, 1), _heads(dqd_ref, 1), _heads(dkt_ref, 1),
                                      dat_ref[...].reshape(nb, CHUNK, CHUNK), ddc))
        for j in range(DN_CHUNKS):
            o = 6 * j
            dx_ref[CHUNK * j:CHUNK * (j + 1), :] = jnp.concatenate(
                [dq[o] + dq[o + 1], dq[o + 2] + dq[o + 3], dq[o + 4] + dq[o + 5],
                 dk[o] + dk[o + 1], dk[o + 2] + dk[o + 3], dk[o + 4] + dk[o + 5]] + [dv[o + h] for h in range(6)], axis=1)
        dseg_ref[...] = dseg.astype(bf16)
        dprm_ref[...] += dprm

    rows = CHUNK * DN_CHUNKS
    outs = _dnc_out_specs(chunks=DN_CHUNKS)
    return pl.pallas_call(
        body, name="dn_chunk_bwd", grid=(S // rows,),
        in_specs=_dnc_in_specs() + [outs[4]] + outs,
        out_specs=[pl.BlockSpec((rows, B_QKV), lambda n: (n, 0)), pl.BlockSpec((rows, LANE), lambda n: (n, 0)),
                   pl.BlockSpec((8, LANE), lambda n: (0, 0))],
        out_shape=[jax.ShapeDtypeStruct((S, B_QKV), f32), jax.ShapeDtypeStruct((S, LANE), bf16),
                   jax.ShapeDtypeStruct((8, LANE), f32)],
        compiler_params=_cp(("arbitrary",)),
    )(qkvn, qkvn, qkvn, proj, prm, inv, *cots)


def dns_fwd(chunked):
    u = chunked[0]
    S = u.shape[0]
    nc = S // CHUNK

    def body(u_ref, w_ref, qd_ref, kt_ref, at_ref, dc_ref, o_ref, st_ref, st):
        @pl.when(pl.program_id(0) == 0)
        def _():
            st[...] = jnp.zeros_like(st)
        S0 = st[...]
        st_ref[0] = S0
        out, S1 = _dns_f(S0, _heads(u_ref, 1), _heads(w_ref, 1), _heads(qd_ref, 1), _heads(kt_ref, 1),
                         at_ref[0], dc_ref[0, 0:6, :])
        _put_heads(o_ref, out)
        st[...] = S1

    return pl.pallas_call(
        body, name="dn_scan_fwd", grid=(nc,), in_specs=_dnc_out_specs(),
        out_specs=[pl.BlockSpec((CHUNK, B_V), lambda n: (n, 0)),
                   pl.BlockSpec((1, 6, B_DH, B_DH), lambda n: (n, 0, 0, 0))],
        out_shape=[jax.ShapeDtypeStruct((S, B_V), f32), jax.ShapeDtypeStruct((nc, 6, B_DH, B_DH), f32)],
        scratch_shapes=[pltpu.VMEM((6, B_DH, B_DH), f32)],
        compiler_params=_cp(("arbitrary",)),
    )(*chunked)


def dns_bwd(chunked, states, do):
    S = do.shape[0]
    nc = S // CHUNK

    def body(u_ref, w_ref, qd_ref, kt_ref, at_ref, dc_ref, st_ref, do_ref,
             du_ref, dw_ref, dqd_ref, dkt_ref, dat_ref, ddc_ref, dst):
        @pl.when(pl.program_id(0) == 0)
        def _():
            dst[...] = jnp.zeros_like(dst)
        _, vjp = jax.vjp(_dns_f, st_ref[0], _heads(u_ref, 1), _heads(w_ref, 1).astype(f32),
                         _heads(qd_ref, 1).astype(f32), _heads(kt_ref, 1).astype(f32), at_ref[0].astype(f32),
                         dc_ref[0, 0:6, :])
        dS0, du, dw, dqd, dkt, dat, ddc = vjp((_heads(do_ref, 1), dst[...]))
        dst[...] = dS0
        _put_heads(du_ref, du)
        _put_heads(dw_ref, dw)
        _put_heads(dqd_ref, dqd)
        _put_heads(dkt_ref, dkt)
        dat_ref[0] = dat
        ddc_ref[0] = jnp.concatenate([ddc, jnp.zeros((2, LANE), f32)], axis=0)

    return pl.pallas_call(
        body, name="dn_scan_bwd", grid=(nc,),
        in_specs=_dnc_out_specs(nc) + [pl.BlockSpec((1, 6, B_DH, B_DH), lambda n: (nc - 1 - n, 0, 0, 0)),
                                       pl.BlockSpec((CHUNK, B_V), lambda n: (nc - 1 - n, 0))],
        out_specs=_dnc_out_specs(nc), out_shape=_dnc_shapes(S),
        scratch_shapes=[pltpu.VMEM((6, B_DH, B_DH), f32)],
        compiler_params=_cp(("arbitrary",)),
    )(*chunked, states, do)


def dnpost_fwd(o, proj, prm):
    S = o.shape[0]
    t = min(512, S)

    def body(o_ref, z_ref, p_ref, y_ref):
        y_ref[...] = _dnpost_f(o_ref[...], z_ref[...], p_ref[2:3, :]).astype(bf16)

    tok = pl.BlockSpec((t, B_V), lambda i: (i, 0))
    return pl.pallas_call(
        body, name="dn_post_fwd", grid=(S // t,),
        in_specs=[tok, pl.BlockSpec((t, B_V), lambda i: (i, 2)), pl.BlockSpec((8, LANE), lambda i: (0, 0))],
        out_specs=tok, out_shape=jax.ShapeDtypeStruct((S, B_V), bf16), compiler_params=_cp(("parallel",)),
    )(o, proj, prm)


def dnpost_bwd(o, proj, prm, dmix):
    S = o.shape[0]
    t = min(512, S)

    def body(o_ref, z_ref, p_ref, dy_ref, do_ref, dz_ref, dg_ref):
        @pl.when(pl.program_id(0) == 0)
        def _():
            dg_ref[...] = jnp.zeros_like(dg_ref)
        _, vjp = jax.vjp(_dnpost_f, o_ref[...], z_ref[...], p_ref[2:3, :])
        do, dz, dg = vjp(dy_ref[...])
        do_ref[...] = do
        dz_ref[...] = dz.astype(bf16)
        dg_ref[...] += dg

    tok = pl.BlockSpec((t, B_V), lambda i: (i, 0))
    return pl.pallas_call(
        body, name="dn_post_bwd", grid=(S // t,),
        in_specs=[tok, pl.BlockSpec((t, B_V), lambda i: (i, 2)), pl.BlockSpec((8, LANE), lambda i: (0, 0)), tok],
        out_specs=[tok, tok, pl.BlockSpec((1, LANE), lambda i: (0, 0))],
        out_shape=[jax.ShapeDtypeStruct((S, B_V), f32), jax.ShapeDtypeStruct((S, B_V), bf16),
                   jax.ShapeDtypeStruct((1, LANE), f32)],
        compiler_params=_cp(("arbitrary",)),
    )(o, proj, prm, dmix)


N_FF_BLK = D_FF // LANE
GU_SHARD = 2 * D_FF // 4


GLU_ROWS = 256
HALO = 16


def _glu_conv(gext, w, b):
    return (w[2:3] * gext + w[1:2] * shift_down(gext, 1) + w[0:1] * shift_down(gext, 2) + b)[HALO:]


def _glu_gate(c, up):
    return _silu(c) * up


def _glu_gext(g_ref, r0, first, T=GLU_ROWS):
    if first:
        return jnp.concatenate([jnp.zeros((HALO, LANE), f32), g_ref[0:T, :].astype(f32)], axis=0)
    return g_ref[pl.ds(r0 - HALO, T + HALO), :].astype(f32)


def glu_fwd(gu, w, b, name):
    S = gu.shape[0]
    T = min(GLU_ROWS, S // 2)

    def body(g_ref, u_ref, w_ref, b_ref, o_ref, c_ref):
        wv, bv = w_ref[...], b_ref[...]

        def tile(r0, first):
            c = _glu_conv(_glu_gext(g_ref, r0, first, T), wv, bv)
            c_ref[pl.ds(r0, T), :] = c.astype(bf16)
            o_ref[pl.ds(r0, T), :] = _glu_gate(c, u_ref[pl.ds(r0, T), :].astype(f32)).astype(bf16)

        tile(0, True)

        @pl.loop(1, S // T)
        def _(t):
            tile(pl.multiple_of(t * T, T), False)

    col = pl.BlockSpec((S, LANE), lambda j: (0, j))
    return pl.pallas_call(
        body, name=name, grid=(N_FF_BLK,),
        in_specs=[col, pl.BlockSpec((S, LANE), lambda j: (0, N_FF_BLK + j)), pl.BlockSpec((3, LANE), lambda j: (0, j)),
                  pl.BlockSpec((1, LANE), lambda j: (0, j))],
        out_specs=[col, col], out_shape=[jax.ShapeDtypeStruct((S, D_FF), bf16)] * 2,
        compiler_params=_cp(("parallel",)),
    )(gu, gu, w, b.reshape(1, D_FF))


def glu_bwd(gu, c, w, b, dact, name):
    S = gu.shape[0]
    T = min(GLU_ROWS, S // 2)

    def body(g_ref, u_ref, c_ref, w_ref, b_ref, d_ref, dg_ref, dw_ref, db_ref, acc):
        wv, bv = w_ref[...], b_ref[...]

        def tile(r0, first):
            rows = pl.ds(r0, T)
            _, vjp_gate = jax.vjp(_glu_gate, c_ref[rows, :].astype(f32), u_ref[rows, :].astype(f32))
            dc, du = vjp_gate(d_ref[rows, :].astype(f32))
            _, vjp_conv = jax.vjp(_glu_conv, _glu_gext(g_ref, r0, first, T), wv, bv)
            dgx, dw, db = vjp_conv(dc)
            acc[pl.ds(r0, T), :] = dgx[HALO:]
            if not first:
                acc[pl.ds(r0 - HALO, HALO), :] += dgx[:HALO]
            dg_ref[1, pl.ds(r0, T), :] = du.astype(bf16)
            return dw, db

        dw0, db0 = tile(0, True)
        dw_ref[...] = dw0
        db_ref[...] = db0

        @pl.loop(1, S // T)
        def _(t):
            dw, db = tile(pl.multiple_of(t * T, T), False)
            dw_ref[...] += dw
            db_ref[...] += db

        dg_ref[0] = acc[...].astype(bf16)

    col = pl.BlockSpec((S, LANE), lambda j: (0, j))
    wsp = pl.BlockSpec((3, LANE), lambda j: (0, j))
    bsp = pl.BlockSpec((1, LANE), lambda j: (0, j))
    return pl.pallas_call(
        body, name=name, grid=(N_FF_BLK,),
        in_specs=[col, pl.BlockSpec((S, LANE), lambda j: (0, N_FF_BLK + j)), col, wsp, bsp, col],
        out_specs=[pl.BlockSpec((2, S, LANE), lambda j: (0, 0, j)), wsp, bsp],
        out_shape=[jax.ShapeDtypeStruct((2, S, D_FF), bf16), jax.ShapeDtypeStruct((3, D_FF), f32),
                   jax.ShapeDtypeStruct((1, D_FF), f32)],
        scratch_shapes=[pltpu.VMEM((S, LANE), f32)],
        compiler_params=_cp(("parallel",)),
    )(gu, gu, c, w, b.reshape(1, D_FF), dact)


def gu_fwd(n2, wg, name):
    S = n2.shape[0]
    tm = min(MM_ROWS, S)

    def body(a_ref, w_ref, o_ref):
        o_ref[...] = _dg(a_ref[...], w_ref[...], 1, 0).astype(bf16)

    return pl.pallas_call(
        body, name=name, grid=(4, S // tm),
        in_specs=[pl.BlockSpec((tm, D), lambda s, m: (m, 0)), pl.BlockSpec((None, D, GU_SHARD), lambda s, m: (s, 0, 0))],
        out_specs=pl.BlockSpec((tm, GU_SHARD), lambda s, m: (m, s)),
        out_shape=jax.ShapeDtypeStruct((S, 2 * D_FF), bf16), compiler_params=_cp(("parallel", "parallel")),
    )(n2, wg)


def gu_bwd_x(dgu, wg, norm, name):
    S = dgu.shape[1]
    tm = min(NORM_ROWS, S)

    def body(d_ref, w_ref, h_ref, g_ref, r_ref, o_ref, dg_ref):
        _acc_then_norm_bwd(_dg(d_ref[...], w_ref[...], 1, 1), 4, h_ref, g_ref, r_ref, o_ref, dg_ref)

    tok = pl.BlockSpec((tm, D), lambda m, s: (m, 0))
    vec = pl.BlockSpec((1, D), lambda m, s: (0, 0))
    return pl.pallas_call(
        body, name=name, grid=(S // tm, 4),
        in_specs=[pl.BlockSpec((None, tm, GU_SHARD), lambda m, s: (s // 2, m, s % 2)),
                  pl.BlockSpec((None, D, GU_SHARD), lambda m, s: (s, 0, 0)), tok, vec, tok],
        out_specs=[tok, vec],
        out_shape=[jax.ShapeDtypeStruct((S, D), f32), jax.ShapeDtypeStruct((1, D), f32)],
        compiler_params=_cp(("arbitrary", "arbitrary")),
    )(dgu, wg, norm[0], norm[1].reshape(1, D), norm[2])


def gu_bwd_w(n2, dgu, name):
    S = n2.shape[0]
    tm = min(MM_ROWS, S)
    nm = S // tm

    def body(a_ref, d_ref, o_ref, acc):
        @pl.when(pl.program_id(1) == 0)
        def _():
            acc[...] = jnp.zeros_like(acc)
        acc[...] += _dg(a_ref[...], d_ref[...], 0, 0)

        @pl.when(pl.program_id(1) == nm - 1)
        def _():
            o_ref[...] = acc[...].astype(bf16)

    return pl.pallas_call(
        body, name=name, grid=(4, nm),
        in_specs=[pl.BlockSpec((tm, D), lambda s, m: (m, 0)),
                  pl.BlockSpec((None, tm, GU_SHARD), lambda s, m: (s // 2, m, s % 2))],
        out_specs=pl.BlockSpec((None, D, GU_SHARD), lambda s, m: (s, 0, 0)),
        out_shape=jax.ShapeDtypeStruct((4, D, GU_SHARD), bf16),
        scratch_shapes=[pltpu.VMEM((D, GU_SHARD), f32)],
        compiler_params=_cp(("parallel", "arbitrary")),
    )(n2, dgu)


def _pair_cols(w):
    lead = w.shape[:-1]
    return w.reshape(lead + (2, 6, A_DH)).swapaxes(-3, -2).reshape(lead + (A_Q,))


def _unpair_cols(w):
    lead = w.shape[:-1]
    return w.reshape(lead + (6, 2, A_DH)).swapaxes(-3, -2).reshape(lead + (A_Q,))


def _lay_in_a(w):
    return jnp.concatenate([_pair_cols(w[:, :A_Q]), w[:, A_Q:]], axis=1)


def _unlay_in_a(w):
    return jnp.concatenate([_unpair_cols(w[:, :A_Q]), w[:, A_Q:]], axis=1)


def _lay_out_a(w):
    return jnp.concatenate([_pair_cols(w[:A_Q].T).T, w[A_Q:]], axis=0)


def _unlay_out_a(w):
    return jnp.concatenate([_unpair_cols(w[:A_Q].T).T, w[A_Q:]], axis=0)


def _lay_in_b(w):
    return jnp.concatenate([w[:, :2304], w[:, 2316:], w[:, 2304:2316],
                            jnp.zeros((w.shape[0], LANE - 12), w.dtype)], axis=1)


def _unlay_in_b(w):
    return jnp.concatenate([w[:, :2304], w[:, 2560:2572], w[:, 2304:2560]], axis=1)


def _chip_cols(w):
    return jnp.moveaxis(w.reshape(w.shape[0], 4, w.shape[1] // 4), 1, 0)


def _unchip_cols(w):
    return jnp.moveaxis(w, 0, 1).reshape(w.shape[1], 4 * w.shape[2])


def _local_step(x, mem, target, P):
    arrive = P.get("arrive", lambda key, after: None)
    ready = P.get("ready", lambda key, grads, dep: dep)
    sk = jnp.zeros((16, LANE), f32).at[:A_HEADS].set(jnp.broadcast_to(P["sinks"][:, None], (A_HEADS, LANE)))
    prm = jnp.zeros((8, LANE), f32).at[0, 6:12].set(P["a_log"]).at[1, 6:12].set(P["dt_bias"]).at[2].set(P["out_norm_g"])
    bias = bias_build(P["rel_bias"])
    saved = []
    h = x
    n1 = rms_fwd(h, P["g_mix"][0], "rms_mix0")
    for i in range(2):
        arrive(("w_in", i), n1)
        if i == 0:
            proj = mm_nn(n1, P["w_in_a"], out_dtype=bf16, name="proj_a")
        else:
            proj = mm_nn(n1, P["w_in_b"], name="proj_b")
        arrive(("w_mem", i), proj)
        kv = memkv_fwd(mem, P["g_mem"][i], P["w_mem"][i], f"memkv{i}")
        if i == 0:
            self_out = swa_fwd(proj, bias, sk)
            cross = xattn_fwd(proj, A_Q + 2 * LANE, kv, "xattn_a")
            extra = ()
        else:
            qkvn = dnprep_fwd(proj, P["conv_qkv"])
            chunked, inv = dnc_fwd(qkvn, proj, prm)
            o, states = dns_fwd(chunked)
            self_out = dnpost_fwd(o, proj, prm)
            cross = xattn_fwd(proj, 2304, kv, "xattn_b")
            extra = (qkvn, chunked, inv, states, o)
        mix = (self_out, cross)
        arrive(("w_out", i), cross)
        h2, n2 = mm_res_norm(mix, P["w_out"][i], h, P["g_ffn"][i], f"out_proj{i}")
        arrive(("w_gu", i), n2)
        gu = gu_fwd(n2, P["w_gu"][i], f"gate_up{i}")
        act, pre = glu_fwd(gu, P["ffn_cw"][i], P["ffn_cb"][i], f"glu{i}")
        arrive(("w_down", i), act)
        saved.append((h, n1, kv, proj, mix, h2, n2, gu, pre, act, extra))
        if i == 0:
            h, n1 = mm_res_norm(act, P["w_down"][i], h2, P["g_mix"][1], f"down{i}")
        else:
            h = mm_nn(act, P["w_down"][i], res=h2, name=f"down{i}")

    loss, dh, dg_fin = loss_head(h, P["g_fin"], target)
    G = {"g_fin": dg_fin[0], "g_mix": [None, None], "g_mem": [None, None], "g_ffn": [None, None],
         "w_mem": [None, None], "w_out": [None, None], "w_gu": [None, None], "w_down": [None, None],
         "ffn_cw": [None, None], "ffn_cb": [None, None]}
    for i in (1, 0):
        hin, n1, kv, proj, mix, h2, n2, gu, pre, act, extra = saved[i]
        dact = mm_nt(dh, P["w_down"][i], out_dtype=bf16, name=f"d_act{i}")
        G["w_down"][i] = mm_tn(act, dh, name=f"dw_down{i}")
        dgu, dcw, dcb = glu_bwd(gu, pre, P["ffn_cw"][i], P["ffn_cb"][i], dact, f"glu_bwd{i}")
        G["ffn_cw"][i], G["ffn_cb"][i] = dcw, dcb[0]
        G["w_gu"][i] = gu_bwd_w(n2, dgu, f"dw_gu{i}")
        g_ffn = ready(("ffn", i), G, P["g_ffn"][i])
        dh2, dg = gu_bwd_x(dgu, P["w_gu"][i], (h2, g_ffn, dh), f"d_n2_{i}")
        G["g_ffn"][i] = dg[0]
        dmix = mm_nt(dh2, P["w_out"][i], name=f"d_mix{i}")
        G["w_out"][i] = mm_tn(mix, dh2, name=f"dw_out{i}")
        g_mix = ready(("tick", i), G, P["g_mix"][i])
        if i == 0:
            dqkv, dbias, dsk = swa_bwd(proj, bias, sk, dmix)
            dxq, dkv = xattn_bwd(proj, A_Q + 2 * LANE, kv, dmix, "xattn_a_bwd")
            dproj = (dqkv, dxq)
            G["sinks"] = dsk[:A_HEADS, 0]
            G["rel_bias"] = bias_grad(dbias)[:, :A_HEADS]
            w_in, gname = P["w_in_a"], "w_in_a"
        else:
            qkvn, chunked, inv, states, o = extra
            do, dz, dgo = dnpost_bwd(o, proj, prm, dmix)
            dqkvn, dseg, dprm = dnc_bwd(qkvn, proj, prm, inv, dns_bwd(chunked, states, do))
            draw, dconv = dnprep_bwd(proj, P["conv_qkv"], dqkvn)
            dxq, dkv = xattn_bwd(proj, 2304, kv, dmix, "xattn_b_bwd")
            dproj = (draw, dz, dxq, dseg)
            G["conv_qkv"] = dconv
            G["a_log"], G["dt_bias"], G["out_norm_g"] = dprm[0, 6:12], dprm[1, 6:12], dgo[0]
            w_in, gname = P["w_in_b"], "w_in_b"
        G[gname] = mm_tn(n1, dproj, name=f"d{gname}")
        dh, dg = mm_nt_norm(dproj, w_in, (hin, g_mix, dh2), f"d_n1_{i}")
        G["g_mix"][i] = dg[0]
        dgm, dwm = memkv_bwd(mem, P["g_mem"][i], P["w_mem"][i], dkv, f"memkv_bwd{i}")
        G["g_mem"][i], G["w_mem"][i] = dgm[0], dwm
        ready(("mix", i), G, None)
    return loss, dh, G


def _grads_to_ref(G):
    return {
        "rel_bias": G["rel_bias"], "norm_mix_g": jnp.stack(G["g_mix"]), "norm_mem_g": jnp.stack(G["g_mem"]),
        "w_mem_kv": jnp.stack(G["w_mem"]),
        "w_out": jnp.stack([_unlay_out_a(G["w_out"][0]), G["w_out"][1]]),
        "w_in_a": _unlay_in_a(G["w_in_a"])[None], "sinks_a": G["sinks"][None],
        "w_in_b": _unlay_in_b(G["w_in_b"])[None], "conv_qkv_b": G["conv_qkv"][None],
        "a_log_b": G["a_log"][None], "dt_bias_b": G["dt_bias"][None], "out_norm_g_b": G["out_norm_g"][None],
        "norm_ffn_g": jnp.stack(G["g_ffn"]),
        "w_gate_up": jnp.stack([_unchip_cols(G["w_gu"][0]), _unchip_cols(G["w_gu"][1])]).astype(f32),
        "ffn_conv_w": jnp.stack(G["ffn_cw"]), "ffn_conv_b": jnp.stack(G["ffn_cb"]),
        "w_down": jnp.stack(G["w_down"]), "final_norm_g": G["g_fin"],
    }


ANY = pl.BlockSpec(memory_space=pl.ANY)


def _place():
    return lax.axis_index("x"), lax.axis_index("y"), lax.axis_index("c")


def allreduce_small(buf):
    R = buf.shape[0]

    def body(b_ref, o_ref, recv, ssem, rsem):
        x, y, c = _place()
        me = 4 * x + 2 * y + c

        def peer(k):
            return (1 - x if k & 4 else x, 1 - y if k & 2 else y, 1 - c if k & 1 else c)

        def remote(k, slot):
            return pltpu.make_async_remote_copy(
                src_ref=b_ref, dst_ref=recv.at[slot], send_sem=ssem.at[k - 1], recv_sem=rsem.at[k - 1],
                device_id=peer(k), device_id_type=MESH)

        sends = [remote(k, me) for k in range(1, 8)]
        for cp in sends:
            cp.start()
        recv[me] = b_ref[...]
        for k in range(1, 8):
            px, py, pc = peer(k)
            remote(k, 4 * px + 2 * py + pc).wait_recv()
        for cp in sends:
            cp.wait_send()
        total = recv[0]
        for j in range(1, 8):
            total = total + recv[j]
        o_ref[...] = total

    return pl.pallas_call(
        body, name="small_allreduce",
        in_specs=[pl.BlockSpec(memory_space=pltpu.VMEM)], out_specs=pl.BlockSpec(memory_space=pltpu.VMEM),
        out_shape=jax.ShapeDtypeStruct(buf.shape, f32),
        scratch_shapes=[pltpu.VMEM((8, R, LANE), f32), pltpu.SemaphoreType.DMA((7,)), pltpu.SemaphoreType.DMA((7,))],
    )(buf)


def sum_slots(own, recv, chip, core, name):
    _, R, C = recv.shape
    tr = _row_tile(R, 256)
    nt = R // tr

    def body(p_ref, a_ref, r_ref, o_ref):
        acc = jnp.zeros((tr, C), f32)
        for s in range(4):
            acc = acc + jnp.where(p_ref[0] == s, a_ref[s], r_ref[s]).astype(f32)
        o_ref[...] = acc

    slots = pl.BlockSpec((4, tr, C), lambda i, p_ref: (0, i, 0))
    return pl.pallas_call(
        body, name=name, out_shape=jax.ShapeDtypeStruct((2 * R, C), f32),
        grid_spec=pltpu.PrefetchScalarGridSpec(
            num_scalar_prefetch=1, grid=(nt,), in_specs=[slots, slots],
            out_specs=pl.BlockSpec((tr, C), lambda i, p_ref: (p_ref[1] * nt + i, 0))),
        compiler_params=_cp(("parallel",)),
    )(jnp.stack([chip, core]).astype(jnp.int32), own, recv)


def _half(ref, core, axis=0):
    half = ref.shape[axis] // 2
    idx = (slice(None),) * axis + (pl.ds(core * half, half),)
    return ref.at[idx]


IN_HBM = pl.BlockSpec(memory_space=pltpu.HBM)
IN_SEM = pl.BlockSpec(memory_space=pltpu.SEMAPHORE)
SIDE_EFFECT = pltpu.SideEffectType.DATAFLOW_SIDE_EFFECTING


def _gather_copy(buf, i, k, ssem, rsem, place, landing):
    x, y, c = place
    px, py = [(1 - x, y), (x, 1 - y), (1 - x, 1 - y)][k]
    me = 2 * x + y
    return pltpu.make_async_remote_copy(
        src_ref=buf.at[me], dst_ref=buf.at[me if landing == "theirs" else 2 * px + py],
        send_sem=ssem.at[3 * i + k], recv_sem=rsem.at[3 * i + k], device_id=(px, py, c), device_id_type=MESH)


def gather_start(groups, name):
    flat = [b for grp in groups for b in grp]
    n, ng = len(flat), len(groups)

    def body(*refs):
        bufs, sems = refs[:n], refs[n:n + 2 * ng]
        place = _place()
        j = 0
        for g, grp in enumerate(groups):
            for i in range(len(grp)):
                for k in range(3):
                    _gather_copy(bufs[j], i, k, sems[2 * g], sems[2 * g + 1], place, "theirs").start()
                j += 1
        refs[-1][...] = jnp.zeros_like(refs[-1])

    sem_shapes = [pltpu.SemaphoreType.DMA((3 * len(grp),)) for grp in groups for _ in range(2)]
    out = pl.pallas_call(
        body, name=name, in_specs=[IN_HBM] * n,
        out_specs=(*[IN_SEM] * (2 * ng), *[IN_HBM] * n, pl.BlockSpec(memory_space=pltpu.VMEM)),
        out_shape=(*sem_shapes, *[pltpu.HBM(b.shape, b.dtype) for b in flat], jax.ShapeDtypeStruct((8, LANE), f32)),
        input_output_aliases={i: 2 * ng + i for i in range(n)},
        compiler_params=pltpu.CompilerParams(has_side_effects=SIDE_EFFECT),
    )(*[pltpu.with_memory_space_constraint(b, pltpu.HBM) for b in flat])
    sems, bufs = out[:2 * ng], list(out[2 * ng:2 * ng + n])
    flights, j = [], 0
    for g, grp in enumerate(groups):
        flights.append((bufs[j:j + len(grp)], sems[2 * g], sems[2 * g + 1]))
        j += len(grp)
    return flights, out[-1]


def gather_wait(flight, after, name):
    bufs, ssem, rsem = flight
    n = len(bufs)

    def body(*refs):
        place = _place()
        for i in range(n):
            for k in range(3):
                cp = _gather_copy(refs[i], i, k, refs[n], refs[n + 1], place, "mine")
                cp.wait_send()
                cp.wait_recv()

    return pl.pallas_call(
        body, name=name, in_specs=[IN_HBM] * n + [IN_SEM, IN_SEM, ANY], out_specs=[IN_HBM] * n,
        out_shape=[pltpu.HBM(b.shape, b.dtype) for b in bufs], input_output_aliases={i: i for i in range(n)},
        compiler_params=pltpu.CompilerParams(has_side_effects=SIDE_EFFECT),
    )(*bufs, ssem, rsem, after)


def _scatter_copy(src, land, j, k, ssem, rsem, place, landing):
    x, y, c = place
    px, py = [(1 - x, y), (x, 1 - y), (1 - x, 1 - y)][k]
    return pltpu.make_async_remote_copy(
        src_ref=src.at[2 * px + py], dst_ref=land.at[2 * x + y if landing == "theirs" else 2 * px + py],
        send_sem=ssem.at[3 * j + k], recv_sem=rsem.at[3 * j + k], device_id=(px, py, c), device_id_type=MESH)


def scatter_start(srcs, name):
    n = len(srcs)
    lands = [lax.empty(g.shape, g.dtype) for g in srcs]

    def body(*refs):
        place = _place()
        for j in range(n):
            for k in range(3):
                _scatter_copy(refs[j], refs[n + j], j, k, refs[2 * n], refs[2 * n + 1], place, "theirs").start()
        refs[-1][...] = jnp.zeros_like(refs[-1])

    sem = pltpu.SemaphoreType.DMA((3 * n,))
    hbm = [pltpu.with_memory_space_constraint(b, pltpu.HBM) for b in list(srcs) + lands]
    out = pl.pallas_call(
        body, name=name, in_specs=[IN_HBM] * (2 * n),
        out_specs=(IN_SEM, IN_SEM, *[IN_HBM] * (2 * n), pl.BlockSpec(memory_space=pltpu.VMEM)),
        out_shape=(sem, sem, *[pltpu.HBM(b.shape, b.dtype) for b in hbm], jax.ShapeDtypeStruct((8, LANE), f32)),
        input_output_aliases={i: 2 + i for i in range(2 * n)},
        compiler_params=pltpu.CompilerParams(has_side_effects=SIDE_EFFECT),
    )(*hbm)
    return (list(out[2:2 + n]), list(out[2 + n:2 + 2 * n]), out[0], out[1]), out[-1]


def scatter_wait(flight, after, name):
    srcs, lands, ssem, rsem = flight
    n = len(srcs)

    def body(*refs):
        place = _place()
        for j in range(n):
            for k in range(3):
                cp = _scatter_copy(refs[j], refs[n + j], j, k, refs[2 * n], refs[2 * n + 1], place, "mine")
                cp.wait_send()
                cp.wait_recv()

    out = pl.pallas_call(
        body, name=name, in_specs=[IN_HBM] * (2 * n) + [IN_SEM, IN_SEM, ANY], out_specs=[IN_HBM] * (2 * n),
        out_shape=[pltpu.HBM(b.shape, b.dtype) for b in list(srcs) + list(lands)],
        input_output_aliases={i: i for i in range(2 * n)},
        compiler_params=pltpu.CompilerParams(has_side_effects=SIDE_EFFECT),
    )(*srcs, *lands, ssem, rsem, after)
    return list(out[:n]), list(out[n:])


def _pair_copy(src, land, j, ssem, rsem, place):
    x, y, c = place
    return pltpu.make_async_remote_copy(
        src_ref=_half(src, 1 - c, axis=1), dst_ref=land, send_sem=ssem.at[j], recv_sem=rsem.at[j],
        device_id=(x, y, 1 - c), device_id_type=MESH)


def pair_start(srcs, name):
    n = len(srcs)
    lands = [lax.empty((4, g.shape[1] // 2, g.shape[2]), g.dtype) for g in srcs]

    def body(*refs):
        place = _place()
        for j in range(n):
            _pair_copy(refs[j], refs[n + j], j, refs[2 * n], refs[2 * n + 1], place).start()
        refs[-1][...] = jnp.zeros_like(refs[-1])

    sem = pltpu.SemaphoreType.DMA((n,))
    hbm = [pltpu.with_memory_space_constraint(b, pltpu.HBM) for b in list(srcs) + lands]
    out = pl.pallas_call(
        body, name=name, in_specs=[IN_HBM] * (2 * n),
        out_specs=(IN_SEM, IN_SEM, *[IN_HBM] * (2 * n), pl.BlockSpec(memory_space=pltpu.VMEM)),
        out_shape=(sem, sem, *[pltpu.HBM(b.shape, b.dtype) for b in hbm], jax.ShapeDtypeStruct((8, LANE), f32)),
        input_output_aliases={i: 2 + i for i in range(2 * n)},
        compiler_params=pltpu.CompilerParams(has_side_effects=SIDE_EFFECT),
    )(*hbm)
    return (list(out[2:2 + n]), list(out[2 + n:2 + 2 * n]), out[0], out[1]), out[-1]


def pair_wait(flight, after, name):
    srcs, lands, ssem, rsem = flight
    n = len(srcs)

    def body(*refs):
        place = _place()
        for j in range(n):
            cp = _pair_copy(refs[j], refs[n + j], j, refs[2 * n], refs[2 * n + 1], place)
            cp.wait_send()
            cp.wait_recv()

    out = pl.pallas_call(
        body, name=name, in_specs=[IN_HBM] * (2 * n) + [IN_SEM, IN_SEM, ANY], out_specs=[IN_HBM] * (2 * n),
        out_shape=[pltpu.HBM(b.shape, b.dtype) for b in list(srcs) + list(lands)],
        input_output_aliases={i: i for i in range(2 * n)},
        compiler_params=pltpu.CompilerParams(has_side_effects=SIDE_EFFECT),
    )(*srcs, *lands, ssem, rsem, after)
    return list(out[:n]), list(out[n:])


def _row_tile(rows, cap=512):
    return max(t for t in range(16, min(rows, cap) + 1, 16) if rows % t == 0)


def pair_sum(mine, theirs, core, name):
    _, R, C = mine.shape
    half = R // 2
    tr = _row_tile(half)
    nt = half // tr

    def body(c_ref, a_ref, b_ref, o_ref):
        o_ref[...] = (a_ref[...].astype(f32) + b_ref[...].astype(f32)).astype(bf16)

    return pl.pallas_call(
        body, name=name, out_shape=jax.ShapeDtypeStruct(theirs.shape, bf16),
        grid_spec=pltpu.PrefetchScalarGridSpec(
            num_scalar_prefetch=1, grid=(4, nt),
            in_specs=[pl.BlockSpec((None, tr, C), lambda s, i, c_ref: (s, c_ref[0] * nt + i, 0)),
                      pl.BlockSpec((None, tr, C), lambda s, i, c_ref: (s, i, 0))],
            out_specs=pl.BlockSpec((None, tr, C), lambda s, i, c_ref: (s, i, 0))),
        compiler_params=_cp(("parallel", "parallel")),
    )(jnp.reshape(core, (1,)).astype(jnp.int32), mine, theirs)


def _final_copy(buf, j, ssem, rsem, place, landing):
    x, y, c = place
    return pltpu.make_async_remote_copy(
        src_ref=_half(buf, c), dst_ref=_half(buf, c if landing == "theirs" else 1 - c),
        send_sem=ssem.at[j], recv_sem=rsem.at[j], device_id=(x, y, 1 - c), device_id_type=MESH)


def final_start(fins, name):
    n = len(fins)

    def body(*refs):
        place = _place()
        for j in range(n):
            _final_copy(refs[j], j, refs[n], refs[n + 1], place, "theirs").start()
        refs[-1][...] = jnp.zeros_like(refs[-1])

    sem = pltpu.SemaphoreType.DMA((n,))
    hbm = [pltpu.with_memory_space_constraint(b, pltpu.HBM) for b in fins]
    out = pl.pallas_call(
        body, name=name, in_specs=[IN_HBM] * n,
        out_specs=(IN_SEM, IN_SEM, *[IN_HBM] * n, pl.BlockSpec(memory_space=pltpu.VMEM)),
        out_shape=(sem, sem, *[pltpu.HBM(b.shape, b.dtype) for b in hbm], jax.ShapeDtypeStruct((8, LANE), f32)),
        input_output_aliases={i: 2 + i for i in range(n)},
        compiler_params=pltpu.CompilerParams(has_side_effects=SIDE_EFFECT),
    )(*hbm)
    return (list(out[2:2 + n]), out[0], out[1]), out[-1]


def final_wait(flight, after, name):
    bufs, ssem, rsem = flight
    n = len(bufs)

    def body(*refs):
        place = _place()
        for j in range(n):
            cp = _final_copy(refs[j], j, refs[n], refs[n + 1], place, "mine")
            cp.wait_send()
            cp.wait_recv()

    out = pl.pallas_call(
        body, name=name, in_specs=[IN_HBM] * n + [IN_SEM, IN_SEM, ANY], out_specs=[IN_HBM] * n,
        out_shape=[pltpu.HBM(b.shape, b.dtype) for b in bufs], input_output_aliases={i: i for i in range(n)},
        compiler_params=pltpu.CompilerParams(has_side_effects=SIDE_EFFECT),
    )(*bufs, ssem, rsem, after)
    return list(out)


def adamw_big(w, m, v, gs, row0, name):
    L, R, C = w.shape
    tr = _row_tile(math.gcd(R, row0) if row0 else R, max(16, 262144 // C // 16 * 16))
    b0 = row0 // tr

    def body(*refs):
        w_ref, m_ref, v_ref = refs[:3]
        g_refs = refs[3:3 + L]
        g_ref, d_ref, nm_ref, nv_ref = refs[3 + L:]
        g = g_refs[0][...]
        for l in range(1, L):
            g = jnp.where(pl.program_id(0) == l, g_refs[l][...], g)
        d, nm, nv = _adamw_math(w_ref[...], g, m_ref[...], v_ref[...])
        g_ref[...] = g
        d_ref[...] = d
        nm_ref[...] = nm
        nv_ref[...] = nv

    own = pl.BlockSpec((None, tr, C), lambda l, i: (l, i, 0))
    off = pl.BlockSpec((tr, C), lambda l, i: (b0 + i, 0))
    return pl.pallas_call(
        body, name=name, grid=(L, R // tr), in_specs=[own, own, own] + [off] * L, out_specs=[own] * 4,
        out_shape=[jax.ShapeDtypeStruct((L, R, C), f32)] * 4, compiler_params=_cp(("parallel", "parallel")),
    )(w, m, v, *gs)


def _adamw_math(w, g, m, v):
    m = B1 * m + (1.0 - B1) * g
    v = B2 * v + (1.0 - B2) * (g * g)
    m_hat = m / (1.0 - B1 ** STEP)
    v_hat = v / (1.0 - B2 ** STEP)
    delta = -LR * (m_hat / (jnp.sqrt(v_hat) + AEPS) + WD * w)
    return delta, m, v


def adamw_small(w, m, v, g):
    def body(w_ref, m_ref, v_ref, g_ref, d_ref, nm_ref, nv_ref):
        d, nm, nv = _adamw_math(w_ref[...], g_ref[...], m_ref[...], v_ref[...])
        d_ref[...] = d
        nm_ref[...] = nm
        nv_ref[...] = nv

    return pl.pallas_call(body, name="adamw_small", out_shape=[jax.ShapeDtypeStruct(w.shape, f32)] * 3)(w, m, v, g)


CONV =(("conv_qkv_b", 2), ("ffn_conv_w", 2))
SMALL = ("rel_bias", "norm_mix_g", "norm_mem_g", "sinks_a", "a_log_b", "dt_bias_b", "out_norm_g_b", "norm_ffn_g",
         "ffn_conv_b", "final_norm_g")
WEIGHTS = ("rel_bias", "norm_mix_g", "norm_mem_g", "w_mem_kv", "w_out", "w_in_a", "sinks_a", "w_in_b", "conv_qkv_b",
           "a_log_b", "dt_bias_b", "out_norm_g_b", "norm_ffn_g", "w_gate_up", "ffn_conv_w", "ffn_conv_b", "w_down",
           "final_norm_g")
ARGS = ("x", "mem") + WEIGHTS + ("loss_target",) + tuple("m_" + n for n in WEIGHTS) + tuple("v_" + n for n in WEIGHTS)


def _rows(a, width):
    flat = a.reshape(-1)
    pad = (-flat.shape[0]) % (8 * width)
    if pad:
        flat = jnp.concatenate([flat, jnp.zeros((pad,), a.dtype)])
    return flat.reshape(-1, width)


def _nrows(shape, width):
    return _pad_to(-(-math.prod(shape) // width), 8)


def _pack(arrs, width, total_rows, dtype):
    parts = [_rows(a.astype(dtype), width) for a in arrs]
    used = sum(p.shape[0] for p in parts)
    if total_rows > used:
        parts.append(jnp.zeros((total_rows - used, width), dtype))
    return jnp.concatenate(parts, axis=0)


def _unpack(buf, shapes, width):
    out, r = [], 0
    for s in shapes:
        n = _nrows(s, width)
        out.append(buf[r:r + n].reshape(-1)[:math.prod(s)].reshape(s))
        r += n
    return out


def _pad_to(n, mult):
    return -(-n // mult) * mult


def kernel(x, mem, rel_bias, norm_mix_g, norm_mem_g, w_mem_kv, w_out, w_in_a, sinks_a, w_in_b, conv_qkv_b, a_log_b, dt_bias_b, out_norm_g_b, norm_ffn_g, w_gate_up, ffn_conv_w, ffn_conv_b, w_down, final_norm_g, loss_target, m_rel_bias, m_norm_mix_g, m_norm_mem_g, m_w_mem_kv, m_w_out, m_w_in_a, m_sinks_a, m_w_in_b, m_conv_qkv_b, m_a_log_b, m_dt_bias_b, m_out_norm_g_b, m_norm_ffn_g, m_w_gate_up, m_ffn_conv_w, m_ffn_conv_b, m_w_down, m_final_norm_g, v_rel_bias, v_norm_mix_g, v_norm_mem_g, v_w_mem_kv, v_w_out, v_w_in_a, v_sinks_a, v_w_in_b, v_conv_qkv_b, v_a_log_b, v_dt_bias_b, v_out_norm_g_b, v_norm_ffn_g, v_w_gate_up, v_ffn_conv_w, v_ffn_conv_b, v_w_down, v_final_norm_g):
    A = dict(zip(ARGS, (x, mem, rel_bias, norm_mix_g, norm_mem_g, w_mem_kv, w_out, w_in_a, sinks_a, w_in_b, conv_qkv_b, a_log_b, dt_bias_b, out_norm_g_b, norm_ffn_g, w_gate_up, ffn_conv_w, ffn_conv_b, w_down, final_norm_g, loss_target, m_rel_bias, m_norm_mix_g, m_norm_mem_g, m_w_mem_kv, m_w_out, m_w_in_a, m_sinks_a, m_w_in_b, m_conv_qkv_b, m_a_log_b, m_dt_bias_b, m_out_norm_g_b, m_norm_ffn_g, m_w_gate_up, m_ffn_conv_w, m_ffn_conv_b, m_w_down, m_final_norm_g, v_rel_bias, v_norm_mix_g, v_norm_mem_g, v_w_mem_kv, v_w_out, v_w_in_a, v_sinks_a, v_w_in_b, v_conv_qkv_b, v_a_log_b, v_dt_bias_b, v_out_norm_g_b, v_norm_ffn_g, v_w_gate_up, v_ffn_conv_w, v_ffn_conv_b, v_w_down, v_final_norm_g)))
    chip = 2 * lax.axis_index("x") + lax.axis_index("y")
    core = lax.axis_index("c")

    def own_slot(shard):
        return lax.dynamic_update_index_in_dim(lax.empty((4,) + shard.shape, shard.dtype), shard, chip, 0)

    def bslot(w, tie=0.0):
        return own_slot((w + tie).astype(bf16))

    early = {
        ("w_in", 0): [bslot(w_in_a[0])],
        ("w_mem", 0): [bslot(w_mem_kv[0]), own_slot(ffn_conv_w.reshape(6, -1))],
        ("w_out", 0): [bslot(w_out[0])],
    }
    flights_a, gone = gather_start(list(early.values()), "gather_start_first")
    z = gone[0, 0]
    late = {
        ("w_gu", 0): [bslot(w_gate_up[0], z)], ("w_down", 0): [bslot(w_down[0], z)],
        ("w_in", 1): [bslot(w_in_b[0], z)], ("w_mem", 1): [bslot(w_mem_kv[1], z), own_slot(conv_qkv_b[0] + z)],
        ("w_out", 1): [bslot(w_out[1], z)], ("w_gu", 1): [bslot(w_gate_up[1], z)], ("w_down", 1): [bslot(w_down[1], z)],
    }
    flights_b, started_all = gather_start(list(late.values()), "gather_start_rest")
    flights = dict(zip(list(early) + list(late), flights_a + flights_b))
    P = {"rel_bias": rel_bias, "sinks": sinks_a[0], "a_log": a_log_b[0], "dt_bias": dt_bias_b[0],
         "out_norm_g": out_norm_g_b[0], "g_mix": norm_mix_g, "g_mem": norm_mem_g, "g_ffn": norm_ffn_g,
         "g_fin": final_norm_g, "ffn_cb": [ffn_conv_b[0], ffn_conv_b[1]], "w_mem": [None, None], "w_out": [None, None],
         "w_gu": [None, None], "w_down": [None, None], "ffn_cw": [None, None]}

    def rows4(g):
        return g.reshape(4 * g.shape[1], g.shape[2])

    def arrive(key, after):
        if key not in flights:
            return
        got = gather_wait(flights.pop(key), started_all if key == ("w_in", 0) else after, "gather_wait_%s%d" % key)
        name, i = key
        if name == "w_in":
            P["w_in_a" if i == 0 else "w_in_b"] = (_lay_in_a if i == 0 else _lay_in_b)(_unchip_cols(got[0]))
        elif name == "w_mem":
            P["w_mem"][i] = rows4(got[0])
            if i == 0:
                cw = _unchip_cols(got[1]).reshape(2, 3, D_FF)
                P["ffn_cw"] = [cw[0], cw[1]]
            else:
                P["conv_qkv"] = _unchip_cols(got[1])
        elif name == "w_out":
            P["w_out"][i] = _lay_out_a(rows4(got[0])) if i == 0 else rows4(got[0])
        elif name == "w_gu":
            P["w_gu"][i] = got[0]
        else:
            P["w_down"][i] = rows4(got[0])

    def chip_rows(g):
        return g.reshape(4, g.shape[0] // 4, g.shape[-1])

    sent, started, pending = {}, [], []

    def finish(after):
        key, names, flight = pending.pop()
        tag = "%s%d" % key
        partial, theirs = pair_wait(flight, after, "pair_wait_" + tag)
        pair = [pair_sum(p, t, core, "pair_sum_%s%d" % (nm, key[1])) for p, t, nm in zip(partial, theirs, names)]
        flight, token = scatter_start(pair, "scatter_start_" + tag)
        sent[key] = (names, flight, token)
        started.append(token[0, 0])

    def ready(key, G, dep):
        kind, i = key
        if kind == "tick":
            finish(G["w_out"][i])
        else:
            if kind == "ffn":
                if pending:
                    finish(G["w_gu"][i])
                names, partial = ("gu", "down"), [G["w_gu"][i], chip_rows(G["w_down"][i]).astype(bf16)]
            else:
                g_out = _unlay_out_a(G["w_out"][0]) if i == 0 else G["w_out"][1]
                g_in = _unlay_in_a(G["w_in_a"]) if i == 0 else _unlay_in_b(G["w_in_b"])
                names = ("out", "in", "mem")
                partial = [chip_rows(g_out).astype(bf16), _chip_cols(g_in).astype(bf16),
                           chip_rows(G["w_mem"][i]).astype(bf16)]
            flight, token = pair_start(partial, "pair_start_%s%d" % key)
            pending.append((key, names, flight))
            started.append(token[0, 0])
        if dep is not None:
            while started:
                dep = dep + started.pop()
        return dep

    P["arrive"], P["ready"] = arrive, ready

    loss, dx, G = _local_step(x[0], mem[0], loss_target[0], P)
    gfull = _grads_to_ref(G)
    finish(dx)

    after, halves = sent["mix", 0][2], []
    for key in (("ffn", 1), ("mix", 1), ("ffn", 0), ("mix", 0)):
        names, flight, _ = sent[key]
        pair, arrived = scatter_wait(flight, after, "scatter_wait_%s%d" % key)
        fins = [sum_slots(p, r, chip, core, "sum_slots_%s%d" % (nm, key[1])) for nm, p, r in zip(names, pair, arrived)]
        flight, after = final_start(fins, "final_start_%s%d" % key)
        halves.append(([(nm, key[1]) for nm in names], flight, key))
    done = {}
    for ids, flight, key in halves:
        done.update(zip(ids, final_wait(flight, after, "final_wait_%s%d" % key)))

    sm_shapes = [A[n].shape for n in SMALL] + [gfull[n].shape for n, _ in CONV] + [(LANE,)]
    sm_rows = _pad_to(sum(_nrows(s, LANE) for s in sm_shapes), 8)
    sbuf = _pack([gfull[n] for n in SMALL] + [gfull[n] for n, _ in CONV] + [loss[0]], LANE, sm_rows, f32)
    tot = _unpack(allreduce_small(sbuf), sm_shapes, LANE)
    gsmall = dict(zip(SMALL, tot[:len(SMALL)]))
    for (n, axis), t in zip(CONV, tot[len(SMALL):len(SMALL) + len(CONV)]):
        sh = A[n].shape[axis]
        gsmall[n] = lax.dynamic_slice_in_dim(t, chip * sh, sh, axis)
    loss_out = tot[-1][0]

    out = {}
    plan = (("w_gate_up", [done["gu", 0], done["gu", 1]]), ("w_down", [done["down", 0], done["down", 1]]),
            ("w_out", [done["out", 0], done["out", 1]]), ("w_mem_kv", [done["mem", 0], done["mem", 1]]),
            ("w_in_a", [done["in", 0]]), ("w_in_b", [done["in", 1]]))
    for n, gs in plan:
        shape3 = (len(gs),) + gs[0].shape
        res = adamw_big(A[n].reshape(shape3), A["m_" + n].reshape(shape3), A["v_" + n].reshape(shape3), gs, 0,
                        "adamw_" + n)
        for key, r in zip(("grad_", "delta_", "new_m_", "new_v_"), res):
            out[key + n] = r.reshape(A[n].shape)
    names = SMALL + tuple(n for n, _ in CONV)
    shapes = [A[n].shape for n in names]
    rows = _pad_to(sum(_nrows(s, LANE) for s in shapes), 8)
    packs = [_pack([src[n] for n in names], LANE, rows, f32)
             for src in ({n: A[n] for n in names}, {n: A["m_" + n] for n in names}, {n: A["v_" + n] for n in names}, gsmall)]
    res = adamw_small(*packs)
    for key, r in zip(("delta_", "new_m_", "new_v_"), res):
        for n, a in zip(names, _unpack(r, shapes, LANE)):
            out[key + n] = a
    for n in names:
        out["grad_" + n] = gsmall[n]
    return (loss_out, dx[None], *[out["grad_" + n] for n in WEIGHTS], *[out["delta_" + n] for n in WEIGHTS],
            *[out["new_m_" + n] for n in WEIGHTS], *[out["new_v_" + n] for n in WEIGHTS])
```

```python
import functools
import math

import numpy as np
import jax
import jax.numpy as jnp
from jax import lax
from jax.experimental import pallas as pl
from jax.experimental.pallas import tpu as pltpu

f32 = jnp.float32
bf16 = jnp.bfloat16
HI = lax.Precision.HIGHEST
MESH = pl.DeviceIdType.MESH

D = 1024
MEM_LEN = 256
EPS = 1e-6
A_HEADS, A_KV, A_DH = 12, 2, 64
A_Q = 768
BLK = 128
N_BUCKETS, MAX_DIST = 32, 128
B_QK, B_V, B_DH = 384, 768, 128
B_QKV = 1536
CHUNK = 64
X_Q = 256
D_FF = 2816
LANE = 128
VMEM_LIMIT = 56 * 1024 * 1024
MM_ROWS = 1024

LR, B1, B2, AEPS, WD, STEP = 0.001, 0.9, 0.999, 1e-08, 0.01, 10


def _cp(sem=None):
    return pltpu.CompilerParams(dimension_semantics=sem, vmem_limit_bytes=VMEM_LIMIT)


def _dg(a, b, ca, cb, prec=None):
    return lax.dot_general(a, b, (((ca,), (cb,)), ((), ())), precision=prec, preferred_element_type=f32)


@jax.custom_vjp
def bdot(a, b):
    return _dg(a.astype(bf16), b.astype(bf16), 1, 0)


def _bdot_f(a, b):
    return bdot(a, b), (a, b)


def _bdot_b(res, g):
    a, b = res
    gb = g.astype(bf16)
    return _dg(gb, b.astype(bf16), 1, 1), _dg(a.astype(bf16), gb, 0, 0)


bdot.defvjp(_bdot_f, _bdot_b)


@jax.custom_vjp
def bdot_nt(a, b):
    return _dg(a.astype(bf16), b.astype(bf16), 1, 1)


def _bdot_nt_f(a, b):
    return bdot_nt(a, b), (a, b)


def _bdot_nt_b(res, g):
    a, b = res
    gb = g.astype(bf16)
    return _dg(gb, b.astype(bf16), 1, 0), _dg(gb, a.astype(bf16), 0, 0)


bdot_nt.defvjp(_bdot_nt_f, _bdot_nt_b)


def _shift_rows(x, s, down):
    n = x.shape[0]
    row = lax.broadcasted_iota(jnp.int32, x.shape, 0)
    if down:
        return jnp.where(row >= s, pltpu.roll(x, s, 0), 0.0)
    return jnp.where(row < n - s, pltpu.roll(x, n - s, 0), 0.0)


@functools.partial(jax.custom_vjp, nondiff_argnums=(1,))
def shift_down(x, s):
    return _shift_rows(x, s, True)


def _sd_f(x, s):
    return _shift_rows(x, s, True), None


def _sd_b(s, _, g):
    return (_shift_rows(g, s, False),)


shift_down.defvjp(_sd_f, _sd_b)


def _sigmoid(x):
    return 1.0 / (1.0 + jnp.exp(-x))


def _silu(x):
    return x * _sigmoid(x)


def _rms(x, g):
    return x * lax.rsqrt(jnp.mean(x * x, axis=-1, keepdims=True) + EPS) * g


def _tile(n, cap):
    u = n // LANE
    best = 1
    for d in range(1, u + 1):
        if u % d == 0 and d * LANE <= cap:
            best = d
    return best * LANE


def mm_nn(a, w, res=None, out_dtype=f32, name="mm_nn"):
    M, K = a.shape
    N = w.shape[1]
    tm, tn = min(MM_ROWS, M), _tile(N, 1024)

    def body(*refs):
        if res is None:
            a_ref, w_ref, o_ref = refs
            o_ref[...] = _dg(a_ref[...].astype(bf16), w_ref[...], 1, 0).astype(out_dtype)
        else:
            a_ref, w_ref, r_ref, o_ref = refs
            o_ref[...] = (r_ref[...] + _dg(a_ref[...].astype(bf16), w_ref[...], 1, 0)).astype(out_dtype)

    in_specs = [pl.BlockSpec((tm, K), lambda n, m: (m, 0)), pl.BlockSpec((K, tn), lambda n, m: (0, n))]
    args = [a, w]
    if res is not None:
        in_specs.append(pl.BlockSpec((tm, tn), lambda n, m: (m, n)))
        args.append(res)
    return pl.pallas_call(
        body, name=name, grid=(N // tn, M // tm), in_specs=in_specs,
        out_specs=pl.BlockSpec((tm, tn), lambda n, m: (m, n)),
        out_shape=jax.ShapeDtypeStruct((M, N), out_dtype),
        compiler_params=_cp(("parallel", "parallel")),
    )(*args)


def mm_res_norm(a, w, res, g, name):
    pieces = a if isinstance(a, tuple) else (a,)
    na = len(pieces)
    M, K = pieces[0].shape[0], sum(p.shape[1] for p in pieces)
    tm = min(MM_ROWS, M)

    def body(*refs):
        w_ref, r_ref, g_ref, o_ref, n_ref = refs[na:]
        h = r_ref[...] + _dg(_cols(refs[:na]).astype(bf16), w_ref[...], 1, 0)
        o_ref[...] = h
        n_ref[...] = _rms(h, g_ref[...]).astype(bf16)

    tok = pl.BlockSpec((tm, D), lambda m: (m, 0))
    return pl.pallas_call(
        body, name=name, grid=(M // tm,),
        in_specs=[pl.BlockSpec((tm, p.shape[1]), lambda m: (m, 0)) for p in pieces]
        + [pl.BlockSpec((K, D), lambda m: (0, 0)), tok, pl.BlockSpec((1, D), lambda m: (0, 0))],
        out_specs=[tok, tok],
        out_shape=[jax.ShapeDtypeStruct((M, D), f32), jax.ShapeDtypeStruct((M, D), bf16)],
        compiler_params=_cp(("parallel",)),
    )(*pieces, w, res, g.reshape(1, D))


def mm_nt(dy, w, out_dtype=f32, name="mm_nt"):
    M, N = dy.shape
    K = w.shape[0]
    tm, tn = min(MM_ROWS, M), _tile(N, 1024)
    assert out_dtype == f32 or tn == N

    def body(dy_ref, w_ref, o_ref):
        part = _dg(dy_ref[...].astype(bf16), w_ref[...], 1, 1)
        if tn == N:
            o_ref[...] = part.astype(out_dtype)
        else:
            @pl.when(pl.program_id(1) == 0)
            def _():
                o_ref[...] = jnp.zeros_like(o_ref)
            o_ref[...] += part

    return pl.pallas_call(
        body, name=name, grid=(M // tm, N // tn),
        in_specs=[pl.BlockSpec((tm, tn), lambda m, n: (m, n)), pl.BlockSpec((K, tn), lambda m, n: (0, n))],
        out_specs=pl.BlockSpec((tm, K), lambda m, n: (m, 0)),
        out_shape=jax.ShapeDtypeStruct((M, K), out_dtype),
        compiler_params=_cp(("parallel", "arbitrary")),
    )(dy, w)


NORM_ROWS = 1024


def _acc_then_norm_bwd(part, steps, h_ref, g_ref, r_ref, o_ref, dg_ref):
    k = pl.program_id(1)

    @pl.when((pl.program_id(0) == 0) & (k == 0))
    def _():
        dg_ref[...] = jnp.zeros_like(dg_ref)

    @pl.when(k == 0)
    def _():
        o_ref[...] = part

    @pl.when(k > 0)
    def _():
        o_ref[...] += part

    @pl.when(k == steps - 1)
    def _():
        _, vjp = jax.vjp(_rms, h_ref[...], g_ref[...])
        dh, dg = vjp(o_ref[...])
        o_ref[...] = r_ref[...] + dh
        dg_ref[...] += dg


def mm_nt_norm(dy, w, norm, name):
    pieces = dy if isinstance(dy, tuple) else (dy,)
    nd = len(pieces)
    M, N = pieces[0].shape[0], sum(p.shape[1] for p in pieces)
    tm, tn = (min(NORM_ROWS, M), _tile(N, 1024)) if nd == 1 else (min(512, M), N)

    def body(*refs):
        w_ref, h_ref, g_ref, r_ref, o_ref, dg_ref = refs[nd:]
        _acc_then_norm_bwd(_dg(_cols(refs[:nd]).astype(bf16), w_ref[...], 1, 1), N // tn, h_ref, g_ref, r_ref, o_ref,
                           dg_ref)

    tok = pl.BlockSpec((tm, D), lambda m, n: (m, 0))
    vec = pl.BlockSpec((1, D), lambda m, n: (0, 0))
    return pl.pallas_call(
        body, name=name, grid=(M // tm, N // tn),
        in_specs=[pl.BlockSpec((tm, tn if nd == 1 else p.shape[1]), lambda m, n: (m, n)) for p in pieces]
        + [pl.BlockSpec((D, tn), lambda m, n: (0, n)), tok, vec, tok],
        out_specs=[tok, vec],
        out_shape=[jax.ShapeDtypeStruct((M, D), f32), jax.ShapeDtypeStruct((1, D), f32)],
        compiler_params=_cp(("arbitrary", "arbitrary")),
    )(*pieces, w, norm[0], norm[1].reshape(1, D), norm[2])


def _cols(refs):
    return refs[0][...] if len(refs) == 1 else jnp.concatenate([r[...] for r in refs], axis=1)


def mm_tn(a, dy, name="mm_tn"):
    pieces = a if isinstance(a, tuple) else (a,)
    dpieces = dy if isinstance(dy, tuple) else (dy,)
    na, nd = len(pieces), len(dpieces)
    M, K = pieces[0].shape[0], sum(p.shape[1] for p in pieces)
    N = sum(p.shape[1] for p in dpieces)
    tm = min(MM_ROWS, M)
    tk = _tile(K, 1408) if na == 1 else K
    tn = _tile(N, 1024) if nd == 1 else N

    def body(*refs):
        o_ref = refs[-1]

        @pl.when(pl.program_id(2) == 0)
        def _():
            o_ref[...] = jnp.zeros_like(o_ref)
        o_ref[...] += _dg(_cols(refs[:na]).astype(bf16), _cols(refs[na:na + nd]).astype(bf16), 0, 0)

    a_specs = [pl.BlockSpec((tm, tk if na == 1 else p.shape[1]), lambda k, n, m: (m, k)) for p in pieces]
    d_specs = [pl.BlockSpec((tm, tn if nd == 1 else p.shape[1]), lambda k, n, m: (m, n)) for p in dpieces]
    return pl.pallas_call(
        body, name=name, grid=(K // tk, N // tn, M // tm), in_specs=a_specs + d_specs,
        out_specs=pl.BlockSpec((tk, tn), lambda k, n, m: (k, n)),
        out_shape=jax.ShapeDtypeStruct((K, N), f32),
        compiler_params=_cp(("parallel", "parallel", "arbitrary")),
    )(*pieces, *dpieces)


def rms_fwd(h, g, name):
    S = h.shape[0]
    t = min(512, S)

    def body(h_ref, g_ref, o_ref):
        o_ref[...] = _rms(h_ref[...], g_ref[...]).astype(bf16)

    return pl.pallas_call(
        body, name=name, grid=(S // t,),
        in_specs=[pl.BlockSpec((t, D), lambda i: (i, 0)), pl.BlockSpec((1, D), lambda i: (0, 0))],
        out_specs=pl.BlockSpec((t, D), lambda i: (i, 0)),
        out_shape=jax.ShapeDtypeStruct((S, D), bf16),
        compiler_params=_cp(("parallel",)),
    )(h, g.reshape(1, D))


def loss_head(h, g, target):
    S = h.shape[0]
    t = min(512, S)

    def f(hh, gg, tt):
        err = _rms(hh, gg) - tt
        return 0.5 * jnp.sum(jnp.mean(err * err, axis=-1, keepdims=True), axis=0, keepdims=True)

    def body(h_ref, g_ref, t_ref, loss_ref, dh_ref, dg_ref):
        @pl.when(pl.program_id(0) == 0)
        def _():
            dg_ref[...] = jnp.zeros_like(dg_ref)
            loss_ref[...] = jnp.zeros_like(loss_ref)
        val, vjp = jax.vjp(lambda a, b: f(a, b, t_ref[...]), h_ref[...], g_ref[...])
        dh, dg = vjp(jnp.ones((1, 1), f32))
        dh_ref[...] = dh
        dg_ref[...] += dg
        loss_ref[...] += jnp.broadcast_to(val, loss_ref.shape)

    tok = pl.BlockSpec((t, D), lambda i: (i, 0))
    vec = pl.BlockSpec((1, D), lambda i: (0, 0))
    return pl.pallas_call(
        body, name="loss_head", grid=(S // t,), in_specs=[tok, vec, tok],
        out_specs=[pl.BlockSpec((1, LANE), lambda i: (0, 0)), tok, vec],
        out_shape=[jax.ShapeDtypeStruct((1, LANE), f32), jax.ShapeDtypeStruct((S, D), f32),
                   jax.ShapeDtypeStruct((1, D), f32)],
        compiler_params=_cp(("arbitrary",)),
    )(h, g.reshape(1, D), target)


def memkv_fwd(mem, g, w, name):
    def body(m_ref, g_ref, w_ref, o_ref):
        o_ref[...] = _dg(_rms(m_ref[...], g_ref[...]).astype(bf16), w_ref[...], 1, 0)

    return pl.pallas_call(
        body, name=name, out_shape=jax.ShapeDtypeStruct((MEM_LEN, 2 * X_Q), f32), compiler_params=_cp(),
    )(mem, g.reshape(1, D), w)


def memkv_bwd(mem, g, w, dkv, name):
    def body(m_ref, g_ref, w_ref, d_ref, dg_ref, dw_ref):
        n, vjp = jax.vjp(lambda gg: _rms(m_ref[...], gg), g_ref[...])
        db = d_ref[...].astype(bf16)
        dw_ref[...] = _dg(n.astype(bf16), db, 0, 0)
        dg_ref[...] = vjp(_dg(db, w_ref[...], 1, 1))[0]

    return pl.pallas_call(
        body, name=name,
        out_shape=[jax.ShapeDtypeStruct((1, D), f32), jax.ShapeDtypeStruct((D, 2 * X_Q), f32)],
        compiler_params=_cp(),
    )(mem, g.reshape(1, D), w, dkv)


def _xattn_f(xq, mk, mv):
    lane = lax.broadcasted_iota(jnp.int32, (1, X_Q), 1)
    out = jnp.zeros(xq.shape, f32)
    for hd in range(4):
        msk = (lane // 64 == hd).astype(f32)
        s = bdot_nt(xq * msk, mk) * (64 ** -0.5)
        m = lax.stop_gradient(jnp.max(s, axis=-1, keepdims=True))
        p = jnp.exp(s - m)
        p = p / jnp.sum(p, axis=-1, keepdims=True)
        out = out + bdot(p, mv * msk)
    return out


def xattn_fwd(proj, col, kv, name):
    S = proj.shape[0]
    t = min(512, S)
    cb = col // X_Q

    def body(q_ref, k_ref, v_ref, o_ref):
        o_ref[...] = _xattn_f(q_ref[...].astype(f32), k_ref[...], v_ref[...]).astype(bf16)

    return pl.pallas_call(
        body, name=name, grid=(S // t,),
        in_specs=[pl.BlockSpec((t, X_Q), lambda i: (i, cb)), pl.BlockSpec((MEM_LEN, X_Q), lambda i: (0, 0)),
                  pl.BlockSpec((MEM_LEN, X_Q), lambda i: (0, 1))],
        out_specs=pl.BlockSpec((t, X_Q), lambda i: (i, 0)),
        out_shape=jax.ShapeDtypeStruct((S, X_Q), bf16),
        compiler_params=_cp(("parallel",)),
    )(proj, kv, kv)


def xattn_bwd(proj, col, kv, dmix, name):
    S = proj.shape[0]
    t = min(512, S)
    cb = col // X_Q

    def body(q_ref, k_ref, v_ref, do_ref, dq_ref, dk_ref, dv_ref):
        @pl.when(pl.program_id(0) == 0)
        def _():
            dk_ref[...] = jnp.zeros_like(dk_ref)
            dv_ref[...] = jnp.zeros_like(dv_ref)
        _, vjp = jax.vjp(_xattn_f, q_ref[...].astype(f32), k_ref[...], v_ref[...])
        dq, dk, dv = vjp(do_ref[...])
        dq_ref[...] = dq.astype(bf16)
        dk_ref[...] += dk
        dv_ref[...] += dv

    kvb = pl.BlockSpec((MEM_LEN, X_Q), lambda i: (0, 0))
    dq, dk, dv = pl.pallas_call(
        body, name=name, grid=(S // t,),
        in_specs=[pl.BlockSpec((t, X_Q), lambda i: (i, cb)), kvb,
                  pl.BlockSpec((MEM_LEN, X_Q), lambda i: (0, 1)), pl.BlockSpec((t, X_Q), lambda i: (i, 3))],
        out_specs=[pl.BlockSpec((t, X_Q), lambda i: (i, 0)), kvb, kvb],
        out_shape=[jax.ShapeDtypeStruct((S, X_Q), bf16), jax.ShapeDtypeStruct((MEM_LEN, X_Q), f32),
                   jax.ShapeDtypeStruct((MEM_LEN, X_Q), f32)],
        compiler_params=_cp(("arbitrary",)),
    )(proj, kv, kv, dmix)
    return dq, jnp.concatenate([dk, dv], axis=1)


def _bucket_map():
    qi = np.arange(BLK)[:, None]
    kj = np.arange(2 * BLK)[None, :]
    n = np.maximum(BLK + qi - kj, 0)
    max_exact = N_BUCKETS // 2
    nf = np.maximum(n, 1).astype(np.float64)
    large = max_exact + (np.log(nf / max_exact) / math.log(MAX_DIST / max_exact)
                         * (N_BUCKETS - max_exact)).astype(np.int32)
    large = np.minimum(large, N_BUCKETS - 1)
    return np.where(n < max_exact, n, large).astype(np.int32)


def bias_build(rel_bias):
    def body(rb_ref, bk_ref, o_ref):
        bk = bk_ref[...]
        for h in range(A_HEADS):
            acc = jnp.zeros((BLK, 2 * BLK), f32)
            for b in range(N_BUCKETS):
                acc = jnp.where(bk == b, rb_ref[b, h], acc)
            o_ref[h] = acc

    return pl.pallas_call(
        body, name="bias_build",
        in_specs=[pl.BlockSpec(memory_space=pltpu.SMEM), pl.BlockSpec(memory_space=pltpu.VMEM)],
        out_specs=pl.BlockSpec(memory_space=pltpu.VMEM),
        out_shape=jax.ShapeDtypeStruct((A_HEADS, BLK, 2 * BLK), f32), compiler_params=_cp(),
    )(rel_bias, jnp.asarray(_bucket_map()))


def bias_grad(dbias):
    def body(d_ref, bk_ref, o_ref):
        bk = bk_ref[...]
        row = lax.broadcasted_iota(jnp.int32, (N_BUCKETS, LANE), 0)
        lane = lax.broadcasted_iota(jnp.int32, (N_BUCKETS, LANE), 1)
        acc = jnp.zeros((N_BUCKETS, LANE), f32)
        for h in range(A_HEADS):
            d = d_ref[h]
            for b in range(N_BUCKETS):
                s = jnp.sum(jnp.where(bk == b, d, 0.0), keepdims=True)
                acc = acc + jnp.where((row == b) & (lane == h), s, 0.0)
        o_ref[...] = acc

    return pl.pallas_call(
        body, name="bias_grad", out_shape=jax.ShapeDtypeStruct((N_BUCKETS, LANE), f32), compiler_params=_cp(),
    )(dbias, jnp.asarray(_bucket_map()))


def _swa_f(qb, kp, kc, vp, vc, bias, sk, first):
    kband = jnp.concatenate([kp, kc], axis=0)
    vband = jnp.concatenate([vp, vc], axis=0)
    qi = lax.broadcasted_iota(jnp.int32, (BLK, 2 * BLK), 0)
    kj = lax.broadcasted_iota(jnp.int32, (BLK, 2 * BLK), 1)
    rel = kj - qi
    ok = (rel >= 1) & (rel <= BLK) & ((kj >= BLK) | jnp.logical_not(first))
    lane = lax.broadcasted_iota(jnp.int32, (1, LANE), 1)
    lane_b = lax.broadcasted_iota(jnp.int32, (BLK, LANE), 1)
    outs = []
    for p in range(A_HEADS // 2):
        qp = qb[:, LANE * p:LANE * (p + 1)]
        acc = jnp.zeros((BLK, LANE), f32)
        for g in range(2):
            h = g * (A_HEADS // 2) + p
            msk = (lane // A_DH == g).astype(f32)
            s = bdot_nt(qp * msk, kband) * (A_DH ** -0.5) + bias[h]
            s = jnp.where(ok, s, -1e30)
            skb = jnp.broadcast_to(sk[h:h + 1, :], (BLK, LANE))
            sink = jnp.sum(jnp.where(lane_b == 0, skb, 0.0), axis=-1, keepdims=True)
            m = lax.stop_gradient(jnp.maximum(jnp.max(s, axis=-1, keepdims=True), sink))
            e = jnp.exp(s - m)
            prob = e / (jnp.sum(e, axis=-1, keepdims=True) + jnp.exp(sink - m))
            acc = acc + bdot(prob, vband) * msk
        outs.append(acc)
    return jnp.concatenate(outs, axis=1)


def _swa_specs(nb, rev):
    bi = (lambda i: nb - 1 - i) if rev else (lambda i: i)
    return [
        pl.BlockSpec((BLK, A_Q), lambda i: (bi(i), 0)),
        pl.BlockSpec((BLK, LANE), lambda i: (jnp.maximum(bi(i) - 1, 0), 6)),
        pl.BlockSpec((BLK, LANE), lambda i: (bi(i), 6)),
        pl.BlockSpec((BLK, LANE), lambda i: (jnp.maximum(bi(i) - 1, 0), 7)),
        pl.BlockSpec((BLK, LANE), lambda i: (bi(i), 7)),
        pl.BlockSpec((A_HEADS, BLK, 2 * BLK), lambda i: (0, 0, 0)),
        pl.BlockSpec((16, LANE), lambda i: (0, 0)),
    ]


def swa_fwd(proj, bias, sk):
    S = proj.shape[0]
    nb = S // BLK

    def body(q_ref, kp_ref, kc_ref, vp_ref, vc_ref, b_ref, s_ref, o_ref):
        qkv = [r[...].astype(f32) for r in (q_ref, kp_ref, kc_ref, vp_ref, vc_ref)]
        o_ref[...] = _swa_f(*qkv, b_ref[...], s_ref[...], pl.program_id(0) == 0).astype(bf16)

    return pl.pallas_call(
        body, name="swa_fwd", grid=(nb,), in_specs=_swa_specs(nb, False),
        out_specs=pl.BlockSpec((BLK, A_Q), lambda i: (i, 0)),
        out_shape=jax.ShapeDtypeStruct((S, A_Q), bf16), compiler_params=_cp(("parallel",)),
    )(proj, proj, proj, proj, proj, bias, sk)


def swa_bwd(proj, bias, sk, dmix):
    S = proj.shape[0]
    nb = S // BLK

    def body(q_ref, kp_ref, kc_ref, vp_ref, vc_ref, b_ref, s_ref, do_ref, dqkv_ref, db_ref, ds_ref, ck, cv):
        i = pl.program_id(0)

        @pl.when(i == 0)
        def _():
            db_ref[...] = jnp.zeros_like(db_ref)
            ds_ref[...] = jnp.zeros_like(ds_ref)
            ck[...] = jnp.zeros_like(ck)
            cv[...] = jnp.zeros_like(cv)
        first = i == nb - 1
        qkv = [r[...].astype(f32) for r in (q_ref, kp_ref, kc_ref, vp_ref, vc_ref)]
        _, vjp = jax.vjp(lambda *a: _swa_f(*a, first), *qkv, b_ref[...], s_ref[...])
        dq, dkp, dkc, dvp, dvc, db, ds = vjp(do_ref[...])
        dqkv_ref[...] = jnp.concatenate([dq, dkc + ck[...], dvc + cv[...]], axis=1).astype(bf16)
        ck[...] = dkp
        cv[...] = dvp
        db_ref[...] += db
        ds_ref[...] += ds

    return pl.pallas_call(
        body, name="swa_bwd", grid=(nb,),
        in_specs=_swa_specs(nb, True) + [pl.BlockSpec((BLK, A_Q), lambda i: (nb - 1 - i, 0))],
        out_specs=[pl.BlockSpec((BLK, D), lambda i: (nb - 1 - i, 0)),
                   pl.BlockSpec((A_HEADS, BLK, 2 * BLK), lambda i: (0, 0, 0)),
                   pl.BlockSpec((16, LANE), lambda i: (0, 0))],
        out_shape=[jax.ShapeDtypeStruct((S, D), bf16), jax.ShapeDtypeStruct((A_HEADS, BLK, 2 * BLK), f32),
                   jax.ShapeDtypeStruct((16, LANE), f32)],
        scratch_shapes=[pltpu.VMEM((BLK, LANE), f32), pltpu.VMEM((BLK, LANE), f32)],
        compiler_params=_cp(("arbitrary",)),
    )(proj, proj, proj, proj, proj, bias, sk, dmix)


def _dnprep_f(xext, w, is_qk):
    c = (w[3:4] * xext + w[2:3] * shift_down(xext, 1) + w[1:2] * shift_down(xext, 2) + w[0:1] * shift_down(xext, 3))
    a = _silu(c)[HALO:]
    n = a * lax.rsqrt(jnp.sum(a * a, axis=-1, keepdims=True) + EPS)
    return jnp.where(is_qk, n, a)


def dnprep_fwd(proj, cw):
    S = proj.shape[0]
    nblk = B_QKV // LANE
    T = S

    def body(x_ref, w_ref, o_ref):
        is_qk = pl.program_id(0) < 2 * B_QK // LANE
        wv = w_ref[...]

        def tile(r0, first):
            o_ref[pl.ds(r0, T), :] = _dnprep_f(_glu_gext(x_ref, r0, first, T), wv, is_qk)

        tile(0, True)

    return pl.pallas_call(
        body, name="dnprep_fwd", grid=(nblk,),
        in_specs=[pl.BlockSpec((S, LANE), lambda j: (0, j)), pl.BlockSpec((4, LANE), lambda j: (0, j))],
        out_specs=pl.BlockSpec((S, LANE), lambda j: (0, j)),
        out_shape=jax.ShapeDtypeStruct((S, B_QKV), f32), compiler_params=_cp(("parallel",)),
    )(proj, cw)


def dnprep_bwd(proj, cw, dqkvn):
    S = proj.shape[0]
    nblk = B_QKV // LANE

    T = S

    def body(x_ref, w_ref, d_ref, dx_ref, dw_ref):
        is_qk = pl.program_id(0) < 2 * B_QK // LANE
        wv = w_ref[...]

        def tile(r0, first):
            _, vjp = jax.vjp(lambda a, b: _dnprep_f(a, b, is_qk), _glu_gext(x_ref, r0, first, T), wv)
            dx, dw = vjp(d_ref[pl.ds(r0, T), :])
            dx_ref[pl.ds(r0, T), :] = dx[HALO:].astype(bf16)
            if not first:
                dx_ref[pl.ds(r0 - HALO, HALO), :] += dx[:HALO]
            return dw

        dw_ref[...] = tile(0, True)

    col = pl.BlockSpec((S, LANE), lambda j: (0, j))
    wsp = pl.BlockSpec((4, LANE), lambda j: (0, j))
    return pl.pallas_call(
        body, name="dnprep_bwd", grid=(nblk,), in_specs=[col, wsp, col], out_specs=[col, wsp],
        out_shape=[jax.ShapeDtypeStruct((S, B_QKV), bf16), jax.ShapeDtypeStruct((4, B_QKV), f32)],
        compiler_params=_cp(("parallel",)),
    )(proj, cw, dqkvn)


def _hdot(a, b, ca=1, cb=0):
    return _dg(a, b, ca, cb, HI)


def _bdg(a, b, ca, cb):
    dn = (((ca,), (cb,)), ((0,), (0,)))
    ah, bh = a.astype(bf16), b.astype(bf16)
    al, bl = (a - ah.astype(f32)).astype(bf16), (b - bh.astype(f32)).astype(bf16)
    return (lax.dot_general(ah, bh, dn, preferred_element_type=f32)
            + lax.dot_general(ah, bl, dn, preferred_element_type=f32)
            + lax.dot_general(al, bh, dn, preferred_element_type=f32))


@jax.custom_vjp
def hbd(a, b):
    return _bdg(a, b, 2, 1)


@jax.custom_vjp
def hbd_nt(a, b):
    return _bdg(a, b, 2, 2)


@jax.custom_vjp
def hbd_tn(a, b):
    return _bdg(a, b, 1, 1)


hbd.defvjp(lambda a, b: (hbd(a, b), (a, b)), lambda r, g: (hbd_nt(g, r[1]), hbd_tn(r[0], g)))
hbd_nt.defvjp(lambda a, b: (hbd_nt(a, b), (a, b)), lambda r, g: (hbd(g, r[1]), hbd_tn(g, r[0])))
hbd_tn.defvjp(lambda a, b: (hbd_tn(a, b), (a, b)), lambda r, g: (hbd_nt(r[1], g), hbd(r[0], g)))


def _stack(xs):
    return jnp.concatenate([x[None] for x in xs], axis=0)


def _lane_col(x, j):
    lane = lax.broadcasted_iota(jnp.int32, (1, LANE), 1)
    return jnp.sum(jnp.where(lane == j, x, 0.0), axis=-1, keepdims=True)


def _tri_inv(a_mat):
    r = lax.broadcasted_iota(jnp.int32, (1, CHUNK, CHUNK), 1)
    c = lax.broadcasted_iota(jnp.int32, (1, CHUNK, CHUNK), 2)
    pw = -a_mat
    inv = (r == c).astype(f32) + pw
    for _ in range(5):
        pw = hbd(pw, pw)
        inv = inv + hbd(inv, pw)
    return inv


@jax.custom_vjp
def _tri_inv_known(a_mat, inv):
    return inv


_tri_inv_known.defvjp(lambda a, inv: (inv, inv),
                      lambda inv, g: (-hbd_tn(inv, hbd_nt(g, inv)), jnp.zeros_like(inv)))


def _dnc_f(q, k, v, seg, prm, inverse=_tri_inv):
    C = CHUNK
    B = q.shape[0]
    rows = seg.shape[0]
    beta_all = _sigmoid(seg)
    xx = seg + prm[1:2]
    g_all = -jnp.exp(prm[0:1]) * (jnp.maximum(xx, 0.0) + jnp.log(1.0 + jnp.exp(-jnp.abs(xx))))
    r2 = lax.broadcasted_iota(jnp.int32, (rows, rows), 0)
    c2 = lax.broadcasted_iota(jnp.int32, (rows, rows), 1)
    within = (r2 >= c2) & (r2 // C == c2 // C)
    gc_all = _hdot(within.astype(f32), g_all)
    beta = _stack([_lane_col(beta_all[C * j:C * (j + 1)], h) for j in range(rows // C) for h in range(6)])
    gc = _stack([_lane_col(gc_all[C * j:C * (j + 1)], 6 + h) for j in range(rows // C) for h in range(6)])
    r = lax.broadcasted_iota(jnp.int32, (1, C, C), 1)
    c = lax.broadcasted_iota(jnp.int32, (1, C, C), 2)
    incl = r >= c
    strict = r > c
    gct = [gc_all[C * j:C * (j + 1)].T for j in range(rows // C)]
    g_row = _stack([jnp.broadcast_to(gct[j][6 + h:7 + h, :], (C, C))
                    for j in range(rows // C) for h in range(6)])
    decay = jnp.where(incl, jnp.exp(jnp.where(incl, gc - g_row, 0.0)), 0.0)
    a_mat = beta * sbd_nt(k, k) * jnp.where(strict, decay, 0.0)
    eg = jnp.exp(gc)
    inv = inverse(a_mat)
    u = hbd(inv, beta * v)
    w = hbd(inv, (beta * eg) * k)
    qc = q * (B_DH ** -0.5)
    attn = sbd_nt(qc, k) * decay
    last = (lax.broadcasted_iota(jnp.int32, (1, C, 1), 1) == C - 1).astype(f32)
    g_last = jnp.sum(gc * last, axis=1, keepdims=True)
    dc = jnp.broadcast_to(jnp.exp(g_last), (B, 1, LANE)).reshape(B, LANE)
    return u, w, qc * eg, k * jnp.exp(g_last - gc), attn, dc, inv


def _b1(a, b, ca, cb):
    return lax.dot_general(a.astype(bf16), b.astype(bf16), (((ca,), (cb,)), ((0,), (0,))), preferred_element_type=f32)


@jax.custom_vjp
def sbd(a, b):
    return _b1(a, b, 2, 1)


@jax.custom_vjp
def sbd_nt(a, b):
    return _b1(a, b, 2, 2)


@jax.custom_vjp
def sbd_tn(a, b):
    return _b1(a, b, 1, 1)


sbd.defvjp(lambda a, b: (sbd(a, b), (a, b)), lambda r, g: (sbd_nt(g, r[1]), sbd_tn(r[0], g)))
sbd_nt.defvjp(lambda a, b: (sbd_nt(a, b), (a, b)), lambda r, g: (sbd(g, r[1]), sbd_tn(g, r[0])))
sbd_tn.defvjp(lambda a, b: (sbd_tn(a, b), (a, b)), lambda r, g: (sbd_nt(r[1], g), sbd(r[0], g)))


def _dns_f(S0, u, w, qd, kt, attn, dcrows):
    dc = _lane_col(dcrows, 0).reshape(6, 1, 1)
    delta = u - sbd(w, S0)
    out = sbd(qd, S0) + sbd(attn, delta)
    return out, dc * S0 + sbd_tn(kt, delta)


def _dnpost_f(o, z, grow):
    outs = []
    for h in range(6):
        oh = o[:, LANE * h:LANE * (h + 1)]
        outs.append(oh * lax.rsqrt(jnp.mean(oh * oh, axis=-1, keepdims=True) + EPS) * grow
                    * _silu(z[:, LANE * h:LANE * (h + 1)]))
    return jnp.concatenate(outs, axis=1)


def _hs(h):
    return slice(LANE * h, LANE * (h + 1))


DN_CHUNKS = 4


def _heads(ref, share):
    return _stack([ref[CHUNK * j:CHUNK * (j + 1), _hs(h // share)]
                   for j in range(ref.shape[0] // CHUNK) for h in range(6)])


def _put_heads(ref, val):
    for j in range(ref.shape[0] // CHUNK):
        for h in range(6):
            ref[CHUNK * j:CHUNK * (j + 1), _hs(h)] = val[6 * j + h].astype(ref.dtype)


def _dnc_in_specs():
    rows = CHUNK * DN_CHUNKS
    return [
        pl.BlockSpec((rows, B_QK), lambda n: (n, 0)),
        pl.BlockSpec((rows, B_QK), lambda n: (n, 1)),
        pl.BlockSpec((rows, B_V), lambda n: (n, 1)),
        pl.BlockSpec((rows, LANE), lambda n: (n, 20)),
        pl.BlockSpec((8, LANE), lambda n: (0, 0)),
    ]


def _dnc_out_specs(rev_nc=None, chunks=1):
    ci = (lambda n: n) if rev_nc is None else (lambda n: rev_nc - 1 - n)
    wide = pl.BlockSpec((CHUNK * chunks, B_V), lambda n: (ci(n), 0))
    return [wide, wide, wide, wide, pl.BlockSpec((chunks, 6, CHUNK, CHUNK), lambda n: (ci(n), 0, 0, 0)),
            pl.BlockSpec((chunks, 8, LANE), lambda n: (ci(n), 0, 0))]


def _dc_rows(dc):
    pad = jnp.zeros((2, LANE), f32)
    return _stack([jnp.concatenate([dc[6 * j:6 * (j + 1)], pad], axis=0) for j in range(dc.shape[0] // 6)])


def _dnc_shapes(S, mm=f32):
    nc = S // CHUNK
    return [jax.ShapeDtypeStruct((S, B_V), f32)] + [jax.ShapeDtypeStruct((S, B_V), mm)] * 3 + [
        jax.ShapeDtypeStruct((nc, 6, CHUNK, CHUNK), mm), jax.ShapeDtypeStruct((nc, 8, LANE), f32)]


def dnc_fwd(qkvn, proj, prm):
    S = proj.shape[0]

    def body(q_ref, k_ref, v_ref, s_ref, p_ref, u_ref, w_ref, qd_ref, kt_ref, at_ref, dc_ref, inv_ref):
        u, w, qd, kt, attn, dc, inv = _dnc_f(_heads(q_ref, 2), _heads(k_ref, 2), _heads(v_ref, 1), s_ref[...],
                                             p_ref[...])
        inv_ref[...] = inv.reshape(inv_ref.shape)
        _put_heads(u_ref, u)
        _put_heads(w_ref, w)
        _put_heads(qd_ref, qd)
        _put_heads(kt_ref, kt)
        at_ref[...] = attn.reshape(at_ref.shape).astype(at_ref.dtype)
        dc_ref[...] = _dc_rows(dc)

    outs = _dnc_out_specs(chunks=DN_CHUNKS)
    out = pl.pallas_call(
        body, name="dn_chunk_fwd", grid=(S // (CHUNK * DN_CHUNKS),), in_specs=_dnc_in_specs(),
        out_specs=outs + [outs[4]], out_shape=_dnc_shapes(S, bf16) + [_dnc_shapes(S)[4]],
        compiler_params=_cp(("parallel",)),
    )(qkvn, qkvn, qkvn, proj, prm)
    return out[:6], out[6]


def dnc_bwd(qkvn, proj, prm, inv, cots):
    S = proj.shape[0]

    def body(q_ref, k_ref, v_ref, s_ref, p_ref, inv_ref, du_ref, dw_ref, dqd_ref, dkt_ref, dat_ref, ddc_ref,
             dx_ref, dseg_ref, dprm_ref):
        @pl.when(pl.program_id(0) == 0)
        def _():
            dprm_ref[...] = jnp.zeros_like(dprm_ref)
        nb = 6 * DN_CHUNKS
        known = functools.partial(_tri_inv_known, inv=inv_ref[...].reshape(nb, CHUNK, CHUNK))
        _, vjp = jax.vjp(lambda *a: _dnc_f(*a, inverse=known)[:6], _heads(q_ref, 2), _heads(k_ref, 2),
                         _heads(v_ref, 1), s_ref[...], p_ref[...])
        ddc = jnp.concatenate([ddc_ref[j, 0:6, :] for j in range(DN_CHUNKS)], axis=0)
        dq, dk, dv, dseg, dprm = vjp((_heads(du_ref, 1), _heads(dw_ref, 1), _heads(dqd_ref, 1), _heads(dkt_ref, 1),
                                      dat_ref[...].reshape(nb, CHUNK, CHUNK), ddc))
        for j in range(DN_CHUNKS):
            o = 6 * j
            dx_ref[CHUNK * j:CHUNK * (j + 1), :] = jnp.concatenate(
                [dq[o] + dq[o + 1], dq[o + 2] + dq[o + 3], dq[o + 4] + dq[o + 5],
                 dk[o] + dk[o + 1], dk[o + 2] + dk[o + 3], dk[o + 4] + dk[o + 5]] + [dv[o + h] for h in range(6)], axis=1)
        dseg_ref[...] = dseg.astype(bf16)
        dprm_ref[...] += dprm

    rows = CHUNK * DN_CHUNKS
    outs = _dnc_out_specs(chunks=DN_CHUNKS)
    return pl.pallas_call(
        body, name="dn_chunk_bwd", grid=(S // rows,),
        in_specs=_dnc_in_specs() + [outs[4]] + outs,
        out_specs=[pl.BlockSpec((rows, B_QKV), lambda n: (n, 0)), pl.BlockSpec((rows, LANE), lambda n: (n, 0)),
                   pl.BlockSpec((8, LANE), lambda n: (0, 0))],
        out_shape=[jax.ShapeDtypeStruct((S, B_QKV), f32), jax.ShapeDtypeStruct((S, LANE), bf16),
                   jax.ShapeDtypeStruct((8, LANE), f32)],
        compiler_params=_cp(("arbitrary",)),
    )(qkvn, qkvn, qkvn, proj, prm, inv, *cots)


def dns_fwd(chunked):
    u = chunked[0]
    S = u.shape[0]
    nc = S // CHUNK

    def body(u_ref, w_ref, qd_ref, kt_ref, at_ref, dc_ref, o_ref, st_ref, st):
        @pl.when(pl.program_id(0) == 0)
        def _():
            st[...] = jnp.zeros_like(st)
        S0 = st[...]
        st_ref[0] = S0
        out, S1 = _dns_f(S0, _heads(u_ref, 1), _heads(w_ref, 1), _heads(qd_ref, 1), _heads(kt_ref, 1),
                         at_ref[0], dc_ref[0, 0:6, :])
        _put_heads(o_ref, out)
        st[...] = S1

    return pl.pallas_call(
        body, name="dn_scan_fwd", grid=(nc,), in_specs=_dnc_out_specs(),
        out_specs=[pl.BlockSpec((CHUNK, B_V), lambda n: (n, 0)),
                   pl.BlockSpec((1, 6, B_DH, B_DH), lambda n: (n, 0, 0, 0))],
        out_shape=[jax.ShapeDtypeStruct((S, B_V), f32), jax.ShapeDtypeStruct((nc, 6, B_DH, B_DH), f32)],
        scratch_shapes=[pltpu.VMEM((6, B_DH, B_DH), f32)],
        compiler_params=_cp(("arbitrary",)),
    )(*chunked)


def dns_bwd(chunked, states, do):
    S = do.shape[0]
    nc = S // CHUNK

    def body(u_ref, w_ref, qd_ref, kt_ref, at_ref, dc_ref, st_ref, do_ref,
             du_ref, dw_ref, dqd_ref, dkt_ref, dat_ref, ddc_ref, dst):
        @pl.when(pl.program_id(0) == 0)
        def _():
            dst[...] = jnp.zeros_like(dst)
        _, vjp = jax.vjp(_dns_f, st_ref[0], _heads(u_ref, 1), _heads(w_ref, 1).astype(f32),
                         _heads(qd_ref, 1).astype(f32), _heads(kt_ref, 1).astype(f32), at_ref[0].astype(f32),
                         dc_ref[0, 0:6, :])
        dS0, du, dw, dqd, dkt, dat, ddc = vjp((_heads(do_ref, 1), dst[...]))
        dst[...] = dS0
        _put_heads(du_ref, du)
        _put_heads(dw_ref, dw)
        _put_heads(dqd_ref, dqd)
        _put_heads(dkt_ref, dkt)
        dat_ref[0] = dat
        ddc_ref[0] = jnp.concatenate([ddc, jnp.zeros((2, LANE), f32)], axis=0)

    return pl.pallas_call(
        body, name="dn_scan_bwd", grid=(nc,),
        in_specs=_dnc_out_specs(nc) + [pl.BlockSpec((1, 6, B_DH, B_DH), lambda n: (nc - 1 - n, 0, 0, 0)),
                                       pl.BlockSpec((CHUNK, B_V), lambda n: (nc - 1 - n, 0))],
        out_specs=_dnc_out_specs(nc), out_shape=_dnc_shapes(S),
        scratch_shapes=[pltpu.VMEM((6, B_DH, B_DH), f32)],
        compiler_params=_cp(("arbitrary",)),
    )(*chunked, states, do)


def dnpost_fwd(o, proj, prm):
    S = o.shape[0]
    t = min(512, S)

    def body(o_ref, z_ref, p_ref, y_ref):
        y_ref[...] = _dnpost_f(o_ref[...], z_ref[...], p_ref[2:3, :]).astype(bf16)

    tok = pl.BlockSpec((t, B_V), lambda i: (i, 0))
    return pl.pallas_call(
        body, name="dn_post_fwd", grid=(S // t,),
        in_specs=[tok, pl.BlockSpec((t, B_V), lambda i: (i, 2)), pl.BlockSpec((8, LANE), lambda i: (0, 0))],
        out_specs=tok, out_shape=jax.ShapeDtypeStruct((S, B_V), bf16), compiler_params=_cp(("parallel",)),
    )(o, proj, prm)


def dnpost_bwd(o, proj, prm, dmix):
    S = o.shape[0]
    t = min(512, S)

    def body(o_ref, z_ref, p_ref, dy_ref, do_ref, dz_ref, dg_ref):
        @pl.when(pl.program_id(0) == 0)
        def _():
            dg_ref[...] = jnp.zeros_like(dg_ref)
        _, vjp = jax.vjp(_dnpost_f, o_ref[...], z_ref[...], p_ref[2:3, :])
        do, dz, dg = vjp(dy_ref[...])
        do_ref[...] = do
        dz_ref[...] = dz.astype(bf16)
        dg_ref[...] += dg

    tok = pl.BlockSpec((t, B_V), lambda i: (i, 0))
    return pl.pallas_call(
        body, name="dn_post_bwd", grid=(S // t,),
        in_specs=[tok, pl.BlockSpec((t, B_V), lambda i: (i, 2)), pl.BlockSpec((8, LANE), lambda i: (0, 0)), tok],
        out_specs=[tok, tok, pl.BlockSpec((1, LANE), lambda i: (0, 0))],
        out_shape=[jax.ShapeDtypeStruct((S, B_V), f32), jax.ShapeDtypeStruct((S, B_V), bf16),
                   jax.ShapeDtypeStruct((1, LANE), f32)],
        compiler_params=_cp(("arbitrary",)),
    )(o, proj, prm, dmix)


N_FF_BLK = D_FF // LANE
GU_SHARD = 2 * D_FF // 4


GLU_ROWS = 256
HALO = 16


def _glu_conv(gext, w, b):
    return (w[2:3] * gext + w[1:2] * shift_down(gext, 1) + w[0:1] * shift_down(gext, 2) + b)[HALO:]


def _glu_gate(c, up):
    return _silu(c) * up


def _glu_gext(g_ref, r0, first, T=GLU_ROWS):
    if first:
        return jnp.concatenate([jnp.zeros((HALO, LANE), f32), g_ref[0:T, :].astype(f32)], axis=0)
    return g_ref[pl.ds(r0 - HALO, T + HALO), :].astype(f32)


def glu_fwd(gu, w, b, name):
    S = gu.shape[0]
    T = min(GLU_ROWS, S // 2)

    def body(g_ref, u_ref, w_ref, b_ref, o_ref, c_ref):
        wv, bv = w_ref[...], b_ref[...]

        def tile(r0, first):
            c = _glu_conv(_glu_gext(g_ref, r0, first, T), wv, bv)
            c_ref[pl.ds(r0, T), :] = c.astype(bf16)
            o_ref[pl.ds(r0, T), :] = _glu_gate(c, u_ref[pl.ds(r0, T), :].astype(f32)).astype(bf16)

        tile(0, True)

        @pl.loop(1, S // T)
        def _(t):
            tile(pl.multiple_of(t * T, T), False)

    col = pl.BlockSpec((S, LANE), lambda j: (0, j))
    return pl.pallas_call(
        body, name=name, grid=(N_FF_BLK,),
        in_specs=[col, pl.BlockSpec((S, LANE), lambda j: (0, N_FF_BLK + j)), pl.BlockSpec((3, LANE), lambda j: (0, j)),
                  pl.BlockSpec((1, LANE), lambda j: (0, j))],
        out_specs=[col, col], out_shape=[jax.ShapeDtypeStruct((S, D_FF), bf16)] * 2,
        compiler_params=_cp(("parallel",)),
    )(gu, gu, w, b.reshape(1, D_FF))


def glu_bwd(gu, c, w, b, dact, name):
    S = gu.shape[0]
    T = min(GLU_ROWS, S // 2)

    def body(g_ref, u_ref, c_ref, w_ref, b_ref, d_ref, dg_ref, dw_ref, db_ref, acc):
        wv, bv = w_ref[...], b_ref[...]

        def tile(r0, first):
            rows = pl.ds(r0, T)
            _, vjp_gate = jax.vjp(_glu_gate, c_ref[rows, :].astype(f32), u_ref[rows, :].astype(f32))
            dc, du = vjp_gate(d_ref[rows, :].astype(f32))
            _, vjp_conv = jax.vjp(_glu_conv, _glu_gext(g_ref, r0, first, T), wv, bv)
            dgx, dw, db = vjp_conv(dc)
            acc[pl.ds(r0, T), :] = dgx[HALO:]
            if not first:
                acc[pl.ds(r0 - HALO, HALO), :] += dgx[:HALO]
            dg_ref[1, pl.ds(r0, T), :] = du.astype(bf16)
            return dw, db

        dw0, db0 = tile(0, True)
        dw_ref[...] = dw0
        db_ref[...] = db0

        @pl.loop(1, S // T)
        def _(t):
            dw, db = tile(pl.multiple_of(t * T, T), False)
            dw_ref[...] += dw
            db_ref[...] += db

        dg_ref[0] = acc[...].astype(bf16)

    col = pl.BlockSpec((S, LANE), lambda j: (0, j))
    wsp = pl.BlockSpec((3, LANE), lambda j: (0, j))
    bsp = pl.BlockSpec((1, LANE), lambda j: (0, j))
    return pl.pallas_call(
        body, name=name, grid=(N_FF_BLK,),
        in_specs=[col, pl.BlockSpec((S, LANE), lambda j: (0, N_FF_BLK + j)), col, wsp, bsp, col],
        out_specs=[pl.BlockSpec((2, S, LANE), lambda j: (0, 0, j)), wsp, bsp],
        out_shape=[jax.ShapeDtypeStruct((2, S, D_FF), bf16), jax.ShapeDtypeStruct((3, D_FF), f32),
                   jax.ShapeDtypeStruct((1, D_FF), f32)],
        scratch_shapes=[pltpu.VMEM((S, LANE), f32)],
        compiler_params=_cp(("parallel",)),
    )(gu, gu, c, w, b.reshape(1, D_FF), dact)


def gu_fwd(n2, wg, name):
    S = n2.shape[0]
    tm = min(MM_ROWS, S)

    def body(a_ref, w_ref, o_ref):
        o_ref[...] = _dg(a_ref[...], w_ref[...], 1, 0).astype(bf16)

    return pl.pallas_call(
        body, name=name, grid=(4, S // tm),
        in_specs=[pl.BlockSpec((tm, D), lambda s, m: (m, 0)), pl.BlockSpec((None, D, GU_SHARD), lambda s, m: (s, 0, 0))],
        out_specs=pl.BlockSpec((tm, GU_SHARD), lambda s, m: (m, s)),
        out_shape=jax.ShapeDtypeStruct((S, 2 * D_FF), bf16), compiler_params=_cp(("parallel", "parallel")),
    )(n2, wg)


def gu_bwd_x(dgu, wg, norm, name):
    S = dgu.shape[1]
    tm = min(NORM_ROWS, S)

    def body(d_ref, w_ref, h_ref, g_ref, r_ref, o_ref, dg_ref):
        _acc_then_norm_bwd(_dg(d_ref[...], w_ref[...], 1, 1), 4, h_ref, g_ref, r_ref, o_ref, dg_ref)

    tok = pl.BlockSpec((tm, D), lambda m, s: (m, 0))
    vec = pl.BlockSpec((1, D), lambda m, s: (0, 0))
    return pl.pallas_call(
        body, name=name, grid=(S // tm, 4),
        in_specs=[pl.BlockSpec((None, tm, GU_SHARD), lambda m, s: (s // 2, m, s % 2)),
                  pl.BlockSpec((None, D, GU_SHARD), lambda m, s: (s, 0, 0)), tok, vec, tok],
        out_specs=[tok, vec],
        out_shape=[jax.ShapeDtypeStruct((S, D), f32), jax.ShapeDtypeStruct((1, D), f32)],
        compiler_params=_cp(("arbitrary", "arbitrary")),
    )(dgu, wg, norm[0], norm[1].reshape(1, D), norm[2])


def gu_bwd_w(n2, dgu, name):
    S = n2.shape[0]
    tm = min(MM_ROWS, S)
    nm = S // tm

    def body(a_ref, d_ref, o_ref, acc):
        @pl.when(pl.program_id(1) == 0)
        def _():
            acc[...] = jnp.zeros_like(acc)
        acc[...] += _dg(a_ref[...], d_ref[...], 0, 0)

        @pl.when(pl.program_id(1) == nm - 1)
        def _():
            o_ref[...] = acc[...].astype(bf16)

    return pl.pallas_call(
        body, name=name, grid=(4, nm),
        in_specs=[pl.BlockSpec((tm, D), lambda s, m: (m, 0)),
                  pl.BlockSpec((None, tm, GU_SHARD), lambda s, m: (s // 2, m, s % 2))],
        out_specs=pl.BlockSpec((None, D, GU_SHARD), lambda s, m: (s, 0, 0)),
        out_shape=jax.ShapeDtypeStruct((4, D, GU_SHARD), bf16),
        scratch_shapes=[pltpu.VMEM((D, GU_SHARD), f32)],
        compiler_params=_cp(("parallel", "arbitrary")),
    )(n2, dgu)


def _pair_cols(w):
    lead = w.shape[:-1]
    return w.reshape(lead + (2, 6, A_DH)).swapaxes(-3, -2).reshape(lead + (A_Q,))


def _unpair_cols(w):
    lead = w.shape[:-1]
    return w.reshape(lead + (6, 2, A_DH)).swapaxes(-3, -2).reshape(lead + (A_Q,))


def _lay_in_a(w):
    return jnp.concatenate([_pair_cols(w[:, :A_Q]), w[:, A_Q:]], axis=1)


def _unlay_in_a(w):
    return jnp.concatenate([_unpair_cols(w[:, :A_Q]), w[:, A_Q:]], axis=1)


def _lay_out_a(w):
    return jnp.concatenate([_pair_cols(w[:A_Q].T).T, w[A_Q:]], axis=0)


def _unlay_out_a(w):
    return jnp.concatenate([_unpair_cols(w[:A_Q].T).T, w[A_Q:]], axis=0)


def _lay_in_b(w):
    return jnp.concatenate([w[:, :2304], w[:, 2316:], w[:, 2304:2316],
                            jnp.zeros((w.shape[0], LANE - 12), w.dtype)], axis=1)


def _unlay_in_b(w):
    return jnp.concatenate([w[:, :2304], w[:, 2560:2572], w[:, 2304:2560]], axis=1)


def _chip_cols(w):
    return jnp.moveaxis(w.reshape(w.shape[0], 4, w.shape[1] // 4), 1, 0)


def _unchip_cols(w):
    return jnp.moveaxis(w, 0, 1).reshape(w.shape[1], 4 * w.shape[2])


def _local_step(x, mem, target, P):
    arrive = P.get("arrive", lambda key, after: None)
    ready = P.get("ready", lambda key, grads, dep: dep)
    sk = jnp.zeros((16, LANE), f32).at[:A_HEADS].set(jnp.broadcast_to(P["sinks"][:, None], (A_HEADS, LANE)))
    prm = jnp.zeros((8, LANE), f32).at[0, 6:12].set(P["a_log"]).at[1, 6:12].set(P["dt_bias"]).at[2].set(P["out_norm_g"])
    bias = bias_build(P["rel_bias"])
    saved = []
    h = x
    n1 = rms_fwd(h, P["g_mix"][0], "rms_mix0")
    for i in range(2):
        arrive(("w_in", i), n1)
        if i == 0:
            proj = mm_nn(n1, P["w_in_a"], out_dtype=bf16, name="proj_a")
        else:
            proj = mm_nn(n1, P["w_in_b"], name="proj_b")
        arrive(("w_mem", i), proj)
        kv = memkv_fwd(mem, P["g_mem"][i], P["w_mem"][i], f"memkv{i}")
        if i == 0:
            self_out = swa_fwd(proj, bias, sk)
            cross = xattn_fwd(proj, A_Q + 2 * LANE, kv, "xattn_a")
            extra = ()
        else:
            qkvn = dnprep_fwd(proj, P["conv_qkv"])
            chunked, inv = dnc_fwd(qkvn, proj, prm)
            o, states = dns_fwd(chunked)
            self_out = dnpost_fwd(o, proj, prm)
            cross = xattn_fwd(proj, 2304, kv, "xattn_b")
            extra = (qkvn, chunked, inv, states, o)
        mix = (self_out, cross)
        arrive(("w_out", i), cross)
        h2, n2 = mm_res_norm(mix, P["w_out"][i], h, P["g_ffn"][i], f"out_proj{i}")
        arrive(("w_gu", i), n2)
        gu = gu_fwd(n2, P["w_gu"][i], f"gate_up{i}")
        act, pre = glu_fwd(gu, P["ffn_cw"][i], P["ffn_cb"][i], f"glu{i}")
        arrive(("w_down", i), act)
        saved.append((h, n1, kv, proj, mix, h2, n2, gu, pre, act, extra))
        if i == 0:
            h, n1 = mm_res_norm(act, P["w_down"][i], h2, P["g_mix"][1], f"down{i}")
        else:
            h = mm_nn(act, P["w_down"][i], res=h2, name=f"down{i}")

    loss, dh, dg_fin = loss_head(h, P["g_fin"], target)
    G = {"g_fin": dg_fin[0], "g_mix": [None, None], "g_mem": [None, None], "g_ffn": [None, None],
         "w_mem": [None, None], "w_out": [None, None], "w_gu": [None, None], "w_down": [None, None],
         "ffn_cw": [None, None], "ffn_cb": [None, None]}
    for i in (1, 0):
        hin, n1, kv, proj, mix, h2, n2, gu, pre, act, extra = saved[i]
        dact = mm_nt(dh, P["w_down"][i], out_dtype=bf16, name=f"d_act{i}")
        G["w_down"][i] = mm_tn(act, dh, name=f"dw_down{i}")
        dgu, dcw, dcb = glu_bwd(gu, pre, P["ffn_cw"][i], P["ffn_cb"][i], dact, f"glu_bwd{i}")
        G["ffn_cw"][i], G["ffn_cb"][i] = dcw, dcb[0]
        G["w_gu"][i] = gu_bwd_w(n2, dgu, f"dw_gu{i}")
        g_ffn = ready(("ffn", i), G, P["g_ffn"][i])
        dh2, dg = gu_bwd_x(dgu, P["w_gu"][i], (h2, g_ffn, dh), f"d_n2_{i}")
        G["g_ffn"][i] = dg[0]
        dmix = mm_nt(dh2, P["w_out"][i], name=f"d_mix{i}")
        G["w_out"][i] = mm_tn(mix, dh2, name=f"dw_out{i}")
        g_mix = ready(("tick", i), G, P["g_mix"][i])
        if i == 0:
            dqkv, dbias, dsk = swa_bwd(proj, bias, sk, dmix)
            dxq, dkv = xattn_bwd(proj, A_Q + 2 * LANE, kv, dmix, "xattn_a_bwd")
            dproj = (dqkv, dxq)
            G["sinks"] = dsk[:A_HEADS, 0]
            G["rel_bias"] = bias_grad(dbias)[:, :A_HEADS]
            w_in, gname = P["w_in_a"], "w_in_a"
        else:
            qkvn, chunked, inv, states, o = extra
            do, dz, dgo = dnpost_bwd(o, proj, prm, dmix)
            dqkvn, dseg, dprm = dnc_bwd(qkvn, proj, prm, inv, dns_bwd(chunked, states, do))
            draw, dconv = dnprep_bwd(proj, P["conv_qkv"], dqkvn)
            dxq, dkv = xattn_bwd(proj, 2304, kv, dmix, "xattn_b_bwd")
            dproj = (draw, dz, dxq, dseg)
            G["conv_qkv"] = dconv
            G["a_log"], G["dt_bias"], G["out_norm_g"] = dprm[0, 6:12], dprm[1, 6:12], dgo[0]
            w_in, gname = P["w_in_b"], "w_in_b"
        G[gname] = mm_tn(n1, dproj, name=f"d{gname}")
        dh, dg = mm_nt_norm(dproj, w_in, (hin, g_mix, dh2), f"d_n1_{i}")
        G["g_mix"][i] = dg[0]
        dgm, dwm = memkv_bwd(mem, P["g_mem"][i], P["w_mem"][i], dkv, f"memkv_bwd{i}")
        G["g_mem"][i], G["w_mem"][i] = dgm[0], dwm
        ready(("mix", i), G, None)
    return loss, dh, G


def _grads_to_ref(G):
    return {
        "rel_bias": G["rel_bias"], "norm_mix_g": jnp.stack(G["g_mix"]), "norm_mem_g": jnp.stack(G["g_mem"]),
        "w_mem_kv": jnp.stack(G["w_mem"]),
        "w_out": jnp.stack([_unlay_out_a(G["w_out"][0]), G["w_out"][1]]),
        "w_in_a": _unlay_in_a(G["w_in_a"])[None], "sinks_a": G["sinks"][None],
        "w_in_b": _unlay_in_b(G["w_in_b"])[None], "conv_qkv_b": G["conv_qkv"][None],
        "a_log_b": G["a_log"][None], "dt_bias_b": G["dt_bias"][None], "out_norm_g_b": G["out_norm_g"][None],
        "norm_ffn_g": jnp.stack(G["g_ffn"]),
        "w_gate_up": jnp.stack([_unchip_cols(G["w_gu"][0]), _unchip_cols(G["w_gu"][1])]).astype(f32),
        "ffn_conv_w": jnp.stack(G["ffn_cw"]), "ffn_conv_b": jnp.stack(G["ffn_cb"]),
        "w_down": jnp.stack(G["w_down"]), "final_norm_g": G["g_fin"],
    }


ANY = pl.BlockSpec(memory_space=pl.ANY)


def _place():
    return lax.axis_index("x"), lax.axis_index("y"), lax.axis_index("c")


def _small_copy(buf, land, k, ssem, rsem, place, landing):
    x, y, c = place
    px, py, pc = (1 - x if k & 4 else x, 1 - y if k & 2 else y, 1 - c if k & 1 else c)
    slot = 4 * x + 2 * y + c if landing == "theirs" else 4 * px + 2 * py + pc
    return pltpu.make_async_remote_copy(
        src_ref=buf, dst_ref=land.at[slot], send_sem=ssem.at[k - 1], recv_sem=rsem.at[k - 1],
        device_id=(px, py, pc), device_id_type=MESH)


def small_start(buf):
    land = lax.empty((8,) + buf.shape, f32)

    def body(b_ref, l_ref, ssem, rsem, b_thru, l_thru):
        place = _place()
        for k in range(1, 8):
            _small_copy(b_ref, l_ref, k, ssem, rsem, place, "theirs").start()

    sem = pltpu.SemaphoreType.DMA((7,))
    hbm = [pltpu.with_memory_space_constraint(b, pltpu.HBM) for b in (buf, land)]
    out = pl.pallas_call(
        body, name="small_start", in_specs=[IN_HBM] * 2, out_specs=(IN_SEM, IN_SEM, IN_HBM, IN_HBM),
        out_shape=(sem, sem, *[pltpu.HBM(b.shape, b.dtype) for b in hbm]),
        input_output_aliases={0: 2, 1: 3}, compiler_params=pltpu.CompilerParams(has_side_effects=SIDE_EFFECT),
    )(*hbm)
    return out[2], out[3], out[0], out[1]


def small_wait(flight, after):
    buf, land, ssem, rsem = flight

    def body(b_ref, l_ref, ssem_ref, rsem_ref, after_ref, b_thru, l_thru):
        place = _place()
        for k in range(1, 8):
            cp = _small_copy(b_ref, l_ref, k, ssem_ref, rsem_ref, place, "mine")
            cp.wait_send()
            cp.wait_recv()

    out = pl.pallas_call(
        body, name="small_wait", in_specs=[IN_HBM, IN_HBM, IN_SEM, IN_SEM, ANY], out_specs=[IN_HBM, IN_HBM],
        out_shape=[pltpu.HBM(buf.shape, buf.dtype), pltpu.HBM(land.shape, land.dtype)],
        input_output_aliases={0: 0, 1: 1}, compiler_params=pltpu.CompilerParams(has_side_effects=SIDE_EFFECT),
    )(buf, land, ssem, rsem, after)
    return out[0], out[1]


def small_sum(buf, land, me):
    def body(p_ref, b_ref, l_ref, o_ref):
        total = jnp.zeros(b_ref.shape, f32)
        for j in range(8):
            total = total + jnp.where(p_ref[0] == j, b_ref[...], l_ref[j])
        o_ref[...] = total

    R = buf.shape[0]
    return pl.pallas_call(
        body, name="small_sum", out_shape=jax.ShapeDtypeStruct(buf.shape, f32),
        grid_spec=pltpu.PrefetchScalarGridSpec(
            num_scalar_prefetch=1, grid=(1,),
            in_specs=[pl.BlockSpec((R, LANE), lambda i, p: (0, 0)), pl.BlockSpec((8, R, LANE), lambda i, p: (0, 0, 0))],
            out_specs=pl.BlockSpec((R, LANE), lambda i, p: (0, 0))),
    )(jnp.reshape(me, (1,)).astype(jnp.int32), buf, land)


def sum_slots(own, recv, chip, core, name):
    _, R, C = recv.shape
    tr = _row_tile(R, 256)
    nt = R // tr

    def body(p_ref, a_ref, r_ref, o_ref):
        acc = jnp.zeros((tr, C), f32)
        for s in range(4):
            acc = acc + jnp.where(p_ref[0] == s, a_ref[s], r_ref[s]).astype(f32)
        o_ref[...] = acc

    slots = pl.BlockSpec((4, tr, C), lambda i, p_ref: (0, i, 0))
    return pl.pallas_call(
        body, name=name, out_shape=jax.ShapeDtypeStruct((2 * R, C), f32),
        grid_spec=pltpu.PrefetchScalarGridSpec(
            num_scalar_prefetch=1, grid=(nt,), in_specs=[slots, slots],
            out_specs=pl.BlockSpec((tr, C), lambda i, p_ref: (p_ref[1] * nt + i, 0))),
        compiler_params=_cp(("parallel",)),
    )(jnp.stack([chip, core]).astype(jnp.int32), own, recv)


def _half(ref, core, axis=0):
    half = ref.shape[axis] // 2
    idx = (slice(None),) * axis + (pl.ds(core * half, half),)
    return ref.at[idx]


IN_HBM = pl.BlockSpec(memory_space=pltpu.HBM)
IN_SEM = pl.BlockSpec(memory_space=pltpu.SEMAPHORE)
SIDE_EFFECT = pltpu.SideEffectType.DATAFLOW_SIDE_EFFECTING


def _gather_copy(buf, i, k, ssem, rsem, place, landing):
    x, y, c = place
    px, py = [(1 - x, y), (x, 1 - y), (1 - x, 1 - y)][k]
    me = 2 * x + y
    return pltpu.make_async_remote_copy(
        src_ref=buf.at[me], dst_ref=buf.at[me if landing == "theirs" else 2 * px + py],
        send_sem=ssem.at[3 * i + k], recv_sem=rsem.at[3 * i + k], device_id=(px, py, c), device_id_type=MESH)


def gather_start(groups, name):
    flat = [b for grp in groups for b in grp]
    n, ng = len(flat), len(groups)

    def body(*refs):
        bufs, sems = refs[:n], refs[n:n + 2 * ng]
        place = _place()
        j = 0
        for g, grp in enumerate(groups):
            for i in range(len(grp)):
                for k in range(3):
                    _gather_copy(bufs[j], i, k, sems[2 * g], sems[2 * g + 1], place, "theirs").start()
                j += 1
        refs[-1][...] = jnp.zeros_like(refs[-1])

    sem_shapes = [pltpu.SemaphoreType.DMA((3 * len(grp),)) for grp in groups for _ in range(2)]
    out = pl.pallas_call(
        body, name=name, in_specs=[IN_HBM] * n,
        out_specs=(*[IN_SEM] * (2 * ng), *[IN_HBM] * n, pl.BlockSpec(memory_space=pltpu.VMEM)),
        out_shape=(*sem_shapes, *[pltpu.HBM(b.shape, b.dtype) for b in flat], jax.ShapeDtypeStruct((8, LANE), f32)),
        input_output_aliases={i: 2 * ng + i for i in range(n)},
        compiler_params=pltpu.CompilerParams(has_side_effects=SIDE_EFFECT),
    )(*[pltpu.with_memory_space_constraint(b, pltpu.HBM) for b in flat])
    sems, bufs = out[:2 * ng], list(out[2 * ng:2 * ng + n])
    flights, j = [], 0
    for g, grp in enumerate(groups):
        flights.append((bufs[j:j + len(grp)], sems[2 * g], sems[2 * g + 1]))
        j += len(grp)
    return flights, out[-1]


def gather_wait(flight, after, name):
    bufs, ssem, rsem = flight
    n = len(bufs)

    def body(*refs):
        place = _place()
        for i in range(n):
            for k in range(3):
                cp = _gather_copy(refs[i], i, k, refs[n], refs[n + 1], place, "mine")
                cp.wait_send()
                cp.wait_recv()

    return pl.pallas_call(
        body, name=name, in_specs=[IN_HBM] * n + [IN_SEM, IN_SEM, ANY], out_specs=[IN_HBM] * n,
        out_shape=[pltpu.HBM(b.shape, b.dtype) for b in bufs], input_output_aliases={i: i for i in range(n)},
        compiler_params=pltpu.CompilerParams(has_side_effects=SIDE_EFFECT),
    )(*bufs, ssem, rsem, after)


def _scatter_copy(src, land, j, k, ssem, rsem, place, landing):
    x, y, c = place
    px, py = [(1 - x, y), (x, 1 - y), (1 - x, 1 - y)][k]
    return pltpu.make_async_remote_copy(
        src_ref=src.at[2 * px + py], dst_ref=land.at[2 * x + y if landing == "theirs" else 2 * px + py],
        send_sem=ssem.at[3 * j + k], recv_sem=rsem.at[3 * j + k], device_id=(px, py, c), device_id_type=MESH)


def scatter_start(srcs, name):
    n = len(srcs)
    lands = [lax.empty(g.shape, g.dtype) for g in srcs]

    def body(*refs):
        place = _place()
        for j in range(n):
            for k in range(3):
                _scatter_copy(refs[j], refs[n + j], j, k, refs[2 * n], refs[2 * n + 1], place, "theirs").start()
        refs[-1][...] = jnp.zeros_like(refs[-1])

    sem = pltpu.SemaphoreType.DMA((3 * n,))
    hbm = [pltpu.with_memory_space_constraint(b, pltpu.HBM) for b in list(srcs) + lands]
    out = pl.pallas_call(
        body, name=name, in_specs=[IN_HBM] * (2 * n),
        out_specs=(IN_SEM, IN_SEM, *[IN_HBM] * (2 * n), pl.BlockSpec(memory_space=pltpu.VMEM)),
        out_shape=(sem, sem, *[pltpu.HBM(b.shape, b.dtype) for b in hbm], jax.ShapeDtypeStruct((8, LANE), f32)),
        input_output_aliases={i: 2 + i for i in range(2 * n)},
        compiler_params=pltpu.CompilerParams(has_side_effects=SIDE_EFFECT),
    )(*hbm)
    return (list(out[2:2 + n]), list(out[2 + n:2 + 2 * n]), out[0], out[1]), out[-1]


def scatter_wait(flight, after, name):
    srcs, lands, ssem, rsem = flight
    n = len(srcs)

    def body(*refs):
        place = _place()
        for j in range(n):
            for k in range(3):
                cp = _scatter_copy(refs[j], refs[n + j], j, k, refs[2 * n], refs[2 * n + 1], place, "mine")
                cp.wait_send()
                cp.wait_recv()

    out = pl.pallas_call(
        body, name=name, in_specs=[IN_HBM] * (2 * n) + [IN_SEM, IN_SEM, ANY], out_specs=[IN_HBM] * (2 * n),
        out_shape=[pltpu.HBM(b.shape, b.dtype) for b in list(srcs) + list(lands)],
        input_output_aliases={i: i for i in range(2 * n)},
        compiler_params=pltpu.CompilerParams(has_side_effects=SIDE_EFFECT),
    )(*srcs, *lands, ssem, rsem, after)
    return list(out[:n]), list(out[n:])


def _pair_copy(src, land, j, ssem, rsem, place):
    x, y, c = place
    return pltpu.make_async_remote_copy(
        src_ref=_half(src, 1 - c, axis=1), dst_ref=land, send_sem=ssem.at[j], recv_sem=rsem.at[j],
        device_id=(x, y, 1 - c), device_id_type=MESH)


def pair_start(srcs, name):
    n = len(srcs)
    lands = [lax.empty((4, g.shape[1] // 2, g.shape[2]), g.dtype) for g in srcs]

    def body(*refs):
        place = _place()
        for j in range(n):
            _pair_copy(refs[j], refs[n + j], j, refs[2 * n], refs[2 * n + 1], place).start()
        refs[-1][...] = jnp.zeros_like(refs[-1])

    sem = pltpu.SemaphoreType.DMA((n,))
    hbm = [pltpu.with_memory_space_constraint(b, pltpu.HBM) for b in list(srcs) + lands]
    out = pl.pallas_call(
        body, name=name, in_specs=[IN_HBM] * (2 * n),
        out_specs=(IN_SEM, IN_SEM, *[IN_HBM] * (2 * n), pl.BlockSpec(memory_space=pltpu.VMEM)),
        out_shape=(sem, sem, *[pltpu.HBM(b.shape, b.dtype) for b in hbm], jax.ShapeDtypeStruct((8, LANE), f32)),
        input_output_aliases={i: 2 + i for i in range(2 * n)},
        compiler_params=pltpu.CompilerParams(has_side_effects=SIDE_EFFECT),
    )(*hbm)
    return (list(out[2:2 + n]), list(out[2 + n:2 + 2 * n]), out[0], out[1]), out[-1]


def pair_wait(flight, after, name):
    srcs, lands, ssem, rsem = flight
    n = len(srcs)

    def body(*refs):
        place = _place()
        for j in range(n):
            cp = _pair_copy(refs[j], refs[n + j], j, refs[2 * n], refs[2 * n + 1], place)
            cp.wait_send()
            cp.wait_recv()

    out = pl.pallas_call(
        body, name=name, in_specs=[IN_HBM] * (2 * n) + [IN_SEM, IN_SEM, ANY], out_specs=[IN_HBM] * (2 * n),
        out_shape=[pltpu.HBM(b.shape, b.dtype) for b in list(srcs) + list(lands)],
        input_output_aliases={i: i for i in range(2 * n)},
        compiler_params=pltpu.CompilerParams(has_side_effects=SIDE_EFFECT),
    )(*srcs, *lands, ssem, rsem, after)
    return list(out[:n]), list(out[n:])


def _row_tile(rows, cap=512):
    return max(t for t in range(16, min(rows, cap) + 1, 16) if rows % t == 0)


def pair_sum(mine, theirs, core, name):
    _, R, C = mine.shape
    half = R // 2
    tr = _row_tile(half)
    nt = half // tr

    def body(c_ref, a_ref, b_ref, o_ref):
        o_ref[...] = (a_ref[...].astype(f32) + b_ref[...].astype(f32)).astype(bf16)

    return pl.pallas_call(
        body, name=name, out_shape=jax.ShapeDtypeStruct(theirs.shape, bf16),
        grid_spec=pltpu.PrefetchScalarGridSpec(
            num_scalar_prefetch=1, grid=(4, nt),
            in_specs=[pl.BlockSpec((None, tr, C), lambda s, i, c_ref: (s, c_ref[0] * nt + i, 0)),
                      pl.BlockSpec((None, tr, C), lambda s, i, c_ref: (s, i, 0))],
            out_specs=pl.BlockSpec((None, tr, C), lambda s, i, c_ref: (s, i, 0))),
        compiler_params=_cp(("parallel", "parallel")),
    )(jnp.reshape(core, (1,)).astype(jnp.int32), mine, theirs)


def _final_copy(buf, j, ssem, rsem, place, landing):
    x, y, c = place
    return pltpu.make_async_remote_copy(
        src_ref=_half(buf, c), dst_ref=_half(buf, c if landing == "theirs" else 1 - c),
        send_sem=ssem.at[j], recv_sem=rsem.at[j], device_id=(x, y, 1 - c), device_id_type=MESH)


def final_start(fins, name):
    n = len(fins)

    def body(*refs):
        place = _place()
        for j in range(n):
            _final_copy(refs[j], j, refs[n], refs[n + 1], place, "theirs").start()
        refs[-1][...] = jnp.zeros_like(refs[-1])

    sem = pltpu.SemaphoreType.DMA((n,))
    hbm = [pltpu.with_memory_space_constraint(b, pltpu.HBM) for b in fins]
    out = pl.pallas_call(
        body, name=name, in_specs=[IN_HBM] * n,
        out_specs=(IN_SEM, IN_SEM, *[IN_HBM] * n, pl.BlockSpec(memory_space=pltpu.VMEM)),
        out_shape=(sem, sem, *[pltpu.HBM(b.shape, b.dtype) for b in hbm], jax.ShapeDtypeStruct((8, LANE), f32)),
        input_output_aliases={i: 2 + i for i in range(n)},
        compiler_params=pltpu.CompilerParams(has_side_effects=SIDE_EFFECT),
    )(*hbm)
    return (list(out[2:2 + n]), out[0], out[1]), out[-1]


def final_wait(flight, after, name):
    bufs, ssem, rsem = flight
    n = len(bufs)

    def body(*refs):
        place = _place()
        for j in range(n):
            cp = _final_copy(refs[j], j, refs[n], refs[n + 1], place, "mine")
            cp.wait_send()
            cp.wait_recv()

    out = pl.pallas_call(
        body, name=name, in_specs=[IN_HBM] * n + [IN_SEM, IN_SEM, ANY], out_specs=[IN_HBM] * n,
        out_shape=[pltpu.HBM(b.shape, b.dtype) for b in bufs], input_output_aliases={i: i for i in range(n)},
        compiler_params=pltpu.CompilerParams(has_side_effects=SIDE_EFFECT),
    )(*bufs, ssem, rsem, after)
    return list(out)


def adamw_big(w, m, v, gs, row0, name):
    L, R, C = w.shape
    tr = _row_tile(math.gcd(R, row0) if row0 else R, max(16, 262144 // C // 16 * 16))
    b0 = row0 // tr

    def body(*refs):
        w_ref, m_ref, v_ref = refs[:3]
        g_refs = refs[3:3 + L]
        g_ref, d_ref, nm_ref, nv_ref = refs[3 + L:]
        g = g_refs[0][...]
        for l in range(1, L):
            g = jnp.where(pl.program_id(0) == l, g_refs[l][...], g)
        d, nm, nv = _adamw_math(w_ref[...], g, m_ref[...], v_ref[...])
        g_ref[...] = g
        d_ref[...] = d
        nm_ref[...] = nm
        nv_ref[...] = nv

    own = pl.BlockSpec((None, tr, C), lambda l, i: (l, i, 0))
    off = pl.BlockSpec((tr, C), lambda l, i: (b0 + i, 0))
    return pl.pallas_call(
        body, name=name, grid=(L, R // tr), in_specs=[own, own, own] + [off] * L, out_specs=[own] * 4,
        out_shape=[jax.ShapeDtypeStruct((L, R, C), f32)] * 4, compiler_params=_cp(("parallel", "parallel")),
    )(w, m, v, *gs)


def _adamw_math(w, g, m, v):
    m = B1 * m + (1.0 - B1) * g
    v = B2 * v + (1.0 - B2) * (g * g)
    m_hat = m / (1.0 - B1 ** STEP)
    v_hat = v / (1.0 - B2 ** STEP)
    delta = -LR * (m_hat / (jnp.sqrt(v_hat) + AEPS) + WD * w)
    return delta, m, v


def adamw_small(w, m, v, g):
    def body(w_ref, m_ref, v_ref, g_ref, d_ref, nm_ref, nv_ref):
        d, nm, nv = _adamw_math(w_ref[...], g_ref[...], m_ref[...], v_ref[...])
        d_ref[...] = d
        nm_ref[...] = nm
        nv_ref[...] = nv

    return pl.pallas_call(body, name="adamw_small", out_shape=[jax.ShapeDtypeStruct(w.shape, f32)] * 3)(w, m, v, g)


CONV =(("conv_qkv_b", 2), ("ffn_conv_w", 2))
SMALL = ("rel_bias", "norm_mix_g", "norm_mem_g", "sinks_a", "a_log_b", "dt_bias_b", "out_norm_g_b", "norm_ffn_g",
         "ffn_conv_b", "final_norm_g")
WEIGHTS = ("rel_bias", "norm_mix_g", "norm_mem_g", "w_mem_kv", "w_out", "w_in_a", "sinks_a", "w_in_b", "conv_qkv_b",
           "a_log_b", "dt_bias_b", "out_norm_g_b", "norm_ffn_g", "w_gate_up", "ffn_conv_w", "ffn_conv_b", "w_down",
           "final_norm_g")
ARGS = ("x", "mem") + WEIGHTS + ("loss_target",) + tuple("m_" + n for n in WEIGHTS) + tuple("v_" + n for n in WEIGHTS)


def _rows(a, width):
    flat = a.reshape(-1)
    pad = (-flat.shape[0]) % (8 * width)
    if pad:
        flat = jnp.concatenate([flat, jnp.zeros((pad,), a.dtype)])
    return flat.reshape(-1, width)


def _nrows(shape, width):
    return _pad_to(-(-math.prod(shape) // width), 8)


def _pack(arrs, width, total_rows, dtype):
    parts = [_rows(a.astype(dtype), width) for a in arrs]
    used = sum(p.shape[0] for p in parts)
    if total_rows > used:
        parts.append(jnp.zeros((total_rows - used, width), dtype))
    return jnp.concatenate(parts, axis=0)


def _unpack(buf, shapes, width):
    out, r = [], 0
    for s in shapes:
        n = _nrows(s, width)
        out.append(buf[r:r + n].reshape(-1)[:math.prod(s)].reshape(s))
        r += n
    return out


def _pad_to(n, mult):
    return -(-n // mult) * mult


def kernel(x, mem, rel_bias, norm_mix_g, norm_mem_g, w_mem_kv, w_out, w_in_a, sinks_a, w_in_b, conv_qkv_b, a_log_b, dt_bias_b, out_norm_g_b, norm_ffn_g, w_gate_up, ffn_conv_w, ffn_conv_b, w_down, final_norm_g, loss_target, m_rel_bias, m_norm_mix_g, m_norm_mem_g, m_w_mem_kv, m_w_out, m_w_in_a, m_sinks_a, m_w_in_b, m_conv_qkv_b, m_a_log_b, m_dt_bias_b, m_out_norm_g_b, m_norm_ffn_g, m_w_gate_up, m_ffn_conv_w, m_ffn_conv_b, m_w_down, m_final_norm_g, v_rel_bias, v_norm_mix_g, v_norm_mem_g, v_w_mem_kv, v_w_out, v_w_in_a, v_sinks_a, v_w_in_b, v_conv_qkv_b, v_a_log_b, v_dt_bias_b, v_out_norm_g_b, v_norm_ffn_g, v_w_gate_up, v_ffn_conv_w, v_ffn_conv_b, v_w_down, v_final_norm_g):
    A = dict(zip(ARGS, (x, mem, rel_bias, norm_mix_g, norm_mem_g, w_mem_kv, w_out, w_in_a, sinks_a, w_in_b, conv_qkv_b, a_log_b, dt_bias_b, out_norm_g_b, norm_ffn_g, w_gate_up, ffn_conv_w, ffn_conv_b, w_down, final_norm_g, loss_target, m_rel_bias, m_norm_mix_g, m_norm_mem_g, m_w_mem_kv, m_w_out, m_w_in_a, m_sinks_a, m_w_in_b, m_conv_qkv_b, m_a_log_b, m_dt_bias_b, m_out_norm_g_b, m_norm_ffn_g, m_w_gate_up, m_ffn_conv_w, m_ffn_conv_b, m_w_down, m_final_norm_g, v_rel_bias, v_norm_mix_g, v_norm_mem_g, v_w_mem_kv, v_w_out, v_w_in_a, v_sinks_a, v_w_in_b, v_conv_qkv_b, v_a_log_b, v_dt_bias_b, v_out_norm_g_b, v_norm_ffn_g, v_w_gate_up, v_ffn_conv_w, v_ffn_conv_b, v_w_down, v_final_norm_g)))
    chip = 2 * lax.axis_index("x") + lax.axis_index("y")
    core = lax.axis_index("c")

    def own_slot(shard):
        return lax.dynamic_update_index_in_dim(lax.empty((4,) + shard.shape, shard.dtype), shard, chip, 0)

    def bslot(w, tie=0.0):
        return own_slot((w + tie).astype(bf16))

    early = {
        ("w_in", 0): [bslot(w_in_a[0])],
        ("w_mem", 0): [bslot(w_mem_kv[0]), own_slot(ffn_conv_w.reshape(6, -1))],
        ("w_out", 0): [bslot(w_out[0])],
    }
    flights_a, gone = gather_start(list(early.values()), "gather_start_first")
    z = gone[0, 0]
    late = {
        ("w_gu", 0): [bslot(w_gate_up[0], z)], ("w_down", 0): [bslot(w_down[0], z)],
        ("w_in", 1): [bslot(w_in_b[0], z)], ("w_mem", 1): [bslot(w_mem_kv[1], z), own_slot(conv_qkv_b[0] + z)],
        ("w_out", 1): [bslot(w_out[1], z)], ("w_gu", 1): [bslot(w_gate_up[1], z)], ("w_down", 1): [bslot(w_down[1], z)],
    }
    flights_b, started_all = gather_start(list(late.values()), "gather_start_rest")
    flights = dict(zip(list(early) + list(late), flights_a + flights_b))
    P = {"rel_bias": rel_bias, "sinks": sinks_a[0], "a_log": a_log_b[0], "dt_bias": dt_bias_b[0],
         "out_norm_g": out_norm_g_b[0], "g_mix": norm_mix_g, "g_mem": norm_mem_g, "g_ffn": norm_ffn_g,
         "g_fin": final_norm_g, "ffn_cb": [ffn_conv_b[0], ffn_conv_b[1]], "w_mem": [None, None], "w_out": [None, None],
         "w_gu": [None, None], "w_down": [None, None], "ffn_cw": [None, None]}

    def rows4(g):
        return g.reshape(4 * g.shape[1], g.shape[2])

    def arrive(key, after):
        if key not in flights:
            return
        got = gather_wait(flights.pop(key), started_all if key == ("w_in", 0) else after, "gather_wait_%s%d" % key)
        name, i = key
        if name == "w_in":
            P["w_in_a" if i == 0 else "w_in_b"] = (_lay_in_a if i == 0 else _lay_in_b)(_unchip_cols(got[0]))
        elif name == "w_mem":
            P["w_mem"][i] = rows4(got[0])
            if i == 0:
                cw = _unchip_cols(got[1]).reshape(2, 3, D_FF)
                P["ffn_cw"] = [cw[0], cw[1]]
            else:
                P["conv_qkv"] = _unchip_cols(got[1])
        elif name == "w_out":
            P["w_out"][i] = _lay_out_a(rows4(got[0])) if i == 0 else rows4(got[0])
        elif name == "w_gu":
            P["w_gu"][i] = got[0]
        else:
            P["w_down"][i] = rows4(got[0])

    def chip_rows(g):
        return g.reshape(4, g.shape[0] // 4, g.shape[-1])

    sent, started, pending = {}, [], []

    def finish(after):
        key, names, flight = pending.pop()
        tag = "%s%d" % key
        partial, theirs = pair_wait(flight, after, "pair_wait_" + tag)
        pair = [pair_sum(p, t, core, "pair_sum_%s%d" % (nm, key[1])) for p, t, nm in zip(partial, theirs, names)]
        flight, token = scatter_start(pair, "scatter_start_" + tag)
        sent[key] = (names, flight, token)
        started.append(token[0, 0])

    def ready(key, G, dep):
        kind, i = key
        if kind == "tick":
            finish(G["w_out"][i])
        else:
            if kind == "ffn":
                if pending:
                    finish(G["w_gu"][i])
                names, partial = ("gu", "down"), [G["w_gu"][i], chip_rows(G["w_down"][i]).astype(bf16)]
            else:
                g_out = _unlay_out_a(G["w_out"][0]) if i == 0 else G["w_out"][1]
                g_in = _unlay_in_a(G["w_in_a"]) if i == 0 else _unlay_in_b(G["w_in_b"])
                names = ("out", "in", "mem")
                partial = [chip_rows(g_out).astype(bf16), _chip_cols(g_in).astype(bf16),
                           chip_rows(G["w_mem"][i]).astype(bf16)]
            flight, token = pair_start(partial, "pair_start_%s%d" % key)
            pending.append((key, names, flight))
            started.append(token[0, 0])
        if dep is not None:
            while started:
                dep = dep + started.pop()
        return dep

    P["arrive"], P["ready"] = arrive, ready

    loss, dx, G = _local_step(x[0], mem[0], loss_target[0], P)
    gfull = _grads_to_ref(G)
    finish(dx)

    sm_shapes = [A[n].shape for n in SMALL] + [gfull[n].shape for n, _ in CONV] + [(LANE,)]
    sm_rows = _pad_to(sum(_nrows(s, LANE) for s in sm_shapes), 8)
    sbuf = _pack([gfull[n] for n in SMALL] + [gfull[n] for n, _ in CONV] + [loss[0]], LANE, sm_rows, f32)
    small_flight = small_start(sbuf)

    after, halves = sent["mix", 0][2], []
    for key in (("ffn", 1), ("mix", 1), ("ffn", 0), ("mix", 0)):
        names, flight, _ = sent[key]
        pair, arrived = scatter_wait(flight, after, "scatter_wait_%s%d" % key)
        fins = [sum_slots(p, r, chip, core, "sum_slots_%s%d" % (nm, key[1])) for nm, p, r in zip(names, pair, arrived)]
        flight, after = final_start(fins, "final_start_%s%d" % key)
        halves.append(([(nm, key[1]) for nm in names], flight, key))
    done = {}
    for ids, flight, key in halves:
        done.update(zip(ids, final_wait(flight, after, "final_wait_%s%d" % key)))

    sown, sland = small_wait(small_flight, done["in", 0])
    tot = _unpack(small_sum(sown, sland, 2 * chip + core), sm_shapes, LANE)
    gsmall = dict(zip(SMALL, tot[:len(SMALL)]))
    for (n, axis), t in zip(CONV, tot[len(SMALL):len(SMALL) + len(CONV)]):
        sh = A[n].shape[axis]
        gsmall[n] = lax.dynamic_slice_in_dim(t, chip * sh, sh, axis)
    loss_out = tot[-1][0]

    out = {}
    plan = (("w_gate_up", [done["gu", 0], done["gu", 1]]), ("w_down", [done["down", 0], done["down", 1]]),
            ("w_out", [done["out", 0], done["out", 1]]), ("w_mem_kv", [done["mem", 0], done["mem", 1]]),
            ("w_in_a", [done["in", 0]]), ("w_in_b", [done["in", 1]]))
    for n, gs in plan:
        shape3 = (len(gs),) + gs[0].shape
        res = adamw_big(A[n].reshape(shape3), A["m_" + n].reshape(shape3), A["v_" + n].reshape(shape3), gs, 0,
                        "adamw_" + n)
        for key, r in zip(("grad_", "delta_", "new_m_", "new_v_"), res):
            out[key + n] = r.reshape(A[n].shape)
    names = SMALL + tuple(n for n, _ in CONV)
    shapes = [A[n].shape for n in names]
    rows = _pad_to(sum(_nrows(s, LANE) for s in shapes), 8)
    packs = [_pack([src[n] for n in names], LANE, rows, f32)
             for src in ({n: A[n] for n in names}, {n: A["m_" + n] for n in names}, {n: A["v_" + n] for n in names}, gsmall)]
    res = adamw_small(*packs)
    for key, r in zip(("delta_", "new_m_", "new_v_"), res):
        for n, a in zip(names, _unpack(r, shapes, LANE)):
            out[key + n] = a
    for n in names:
        out["grad_" + n] = gsmall[n]
    return (loss_out, dx[None], *[out["grad_" + n] for n in WEIGHTS], *[out["delta_" + n] for n in WEIGHTS],
            *[out["new_m_" + n] for n in WEIGHTS], *[out["new_v_" + n] for n in WEIGHTS])
```

```python
import functools
import math

import numpy as np
import jax
import jax.numpy as jnp
from jax import lax
from jax.experimental import pallas as pl
from jax.experimental.pallas import tpu as pltpu

f32 = jnp.float32
bf16 = jnp.bfloat16
HI = lax.Precision.HIGHEST
MESH = pl.DeviceIdType.MESH

D = 1024
MEM_LEN = 256
EPS = 1e-6
A_HEADS, A_KV, A_DH = 12, 2, 64
A_Q = 768
BLK = 128
N_BUCKETS, MAX_DIST = 32, 128
B_QK, B_V, B_DH = 384, 768, 128
B_QKV = 1536
CHUNK = 64
X_Q = 256
D_FF = 2816
LANE = 128
VMEM_LIMIT = 56 * 1024 * 1024
MM_ROWS = 1024

LR, B1, B2, AEPS, WD, STEP = 0.001, 0.9, 0.999, 1e-08, 0.01, 10


def _cp(sem=None):
    return pltpu.CompilerParams(dimension_semantics=sem, vmem_limit_bytes=VMEM_LIMIT)


def _dg(a, b, ca, cb, prec=None):
    return lax.dot_general(a, b, (((ca,), (cb,)), ((), ())), precision=prec, preferred_element_type=f32)


@jax.custom_vjp
def bdot(a, b):
    return _dg(a.astype(bf16), b.astype(bf16), 1, 0)


def _bdot_f(a, b):
    return bdot(a, b), (a, b)


def _bdot_b(res, g):
    a, b = res
    gb = g.astype(bf16)
    return _dg(gb, b.astype(bf16), 1, 1), _dg(a.astype(bf16), gb, 0, 0)


bdot.defvjp(_bdot_f, _bdot_b)


@jax.custom_vjp
def bdot_nt(a, b):
    return _dg(a.astype(bf16), b.astype(bf16), 1, 1)


def _bdot_nt_f(a, b):
    return bdot_nt(a, b), (a, b)


def _bdot_nt_b(res, g):
    a, b = res
    gb = g.astype(bf16)
    return _dg(gb, b.astype(bf16), 1, 0), _dg(gb, a.astype(bf16), 0, 0)


bdot_nt.defvjp(_bdot_nt_f, _bdot_nt_b)


def _shift_rows(x, s, down):
    n = x.shape[0]
    row = lax.broadcasted_iota(jnp.int32, x.shape, 0)
    if down:
        return jnp.where(row >= s, pltpu.roll(x, s, 0), 0.0)
    return jnp.where(row < n - s, pltpu.roll(x, n - s, 0), 0.0)


@functools.partial(jax.custom_vjp, nondiff_argnums=(1,))
def shift_down(x, s):
    return _shift_rows(x, s, True)


def _sd_f(x, s):
    return _shift_rows(x, s, True), None


def _sd_b(s, _, g):
    return (_shift_rows(g, s, False),)


shift_down.defvjp(_sd_f, _sd_b)


def _sigmoid(x):
    return 1.0 / (1.0 + jnp.exp(-x))


def _silu(x):
    return x * _sigmoid(x)


def _rms(x, g):
    return x * lax.rsqrt(jnp.mean(x * x, axis=-1, keepdims=True) + EPS) * g


def _tile(n, cap):
    u = n // LANE
    best = 1
    for d in range(1, u + 1):
        if u % d == 0 and d * LANE <= cap:
            best = d
    return best * LANE


def mm_nn(a, w, res=None, out_dtype=f32, name="mm_nn"):
    M, K = a.shape
    N = w.shape[1]
    tm, tn = min(MM_ROWS, M), _tile(N, 1024)

    def body(*refs):
        if res is None:
            a_ref, w_ref, o_ref = refs
            o_ref[...] = _dg(a_ref[...].astype(bf16), w_ref[...], 1, 0).astype(out_dtype)
        else:
            a_ref, w_ref, r_ref, o_ref = refs
            o_ref[...] = (r_ref[...] + _dg(a_ref[...].astype(bf16), w_ref[...], 1, 0)).astype(out_dtype)

    in_specs = [pl.BlockSpec((tm, K), lambda n, m: (m, 0)), pl.BlockSpec((K, tn), lambda n, m: (0, n))]
    args = [a, w]
    if res is not None:
        in_specs.append(pl.BlockSpec((tm, tn), lambda n, m: (m, n)))
        args.append(res)
    return pl.pallas_call(
        body, name=name, grid=(N // tn, M // tm), in_specs=in_specs,
        out_specs=pl.BlockSpec((tm, tn), lambda n, m: (m, n)),
        out_shape=jax.ShapeDtypeStruct((M, N), out_dtype),
        compiler_params=_cp(("parallel", "parallel")),
    )(*args)


def mm_res_norm(a, w, res, g, name):
    pieces = a if isinstance(a, tuple) else (a,)
    na = len(pieces)
    M, K = pieces[0].shape[0], sum(p.shape[1] for p in pieces)
    tm = min(MM_ROWS, M)

    def body(*refs):
        w_ref, r_ref, g_ref, o_ref, n_ref = refs[na:]
        h = r_ref[...] + _dg(_cols(refs[:na]).astype(bf16), w_ref[...], 1, 0)
        o_ref[...] = h
        n_ref[...] = _rms(h, g_ref[...]).astype(bf16)

    tok = pl.BlockSpec((tm, D), lambda m: (m, 0))
    return pl.pallas_call(
        body, name=name, grid=(M // tm,),
        in_specs=[pl.BlockSpec((tm, p.shape[1]), lambda m: (m, 0)) for p in pieces]
        + [pl.BlockSpec((K, D), lambda m: (0, 0)), tok, pl.BlockSpec((1, D), lambda m: (0, 0))],
        out_specs=[tok, tok],
        out_shape=[jax.ShapeDtypeStruct((M, D), f32), jax.ShapeDtypeStruct((M, D), bf16)],
        compiler_params=_cp(("parallel",)),
    )(*pieces, w, res, g.reshape(1, D))


def mm_nt(dy, w, out_dtype=f32, name="mm_nt"):
    M, N = dy.shape
    K = w.shape[0]
    tm, tn = min(MM_ROWS, M), _tile(N, 1024)
    assert out_dtype == f32 or tn == N

    def body(dy_ref, w_ref, o_ref):
        part = _dg(dy_ref[...].astype(bf16), w_ref[...], 1, 1)
        if tn == N:
            o_ref[...] = part.astype(out_dtype)
        else:
            @pl.when(pl.program_id(1) == 0)
            def _():
                o_ref[...] = jnp.zeros_like(o_ref)
            o_ref[...] += part

    return pl.pallas_call(
        body, name=name, grid=(M // tm, N // tn),
        in_specs=[pl.BlockSpec((tm, tn), lambda m, n: (m, n)), pl.BlockSpec((K, tn), lambda m, n: (0, n))],
        out_specs=pl.BlockSpec((tm, K), lambda m, n: (m, 0)),
        out_shape=jax.ShapeDtypeStruct((M, K), out_dtype),
        compiler_params=_cp(("parallel", "arbitrary")),
    )(dy, w)


NORM_ROWS = 1024


def _acc_then_norm_bwd(part, steps, h_ref, g_ref, r_ref, o_ref, dg_ref):
    k = pl.program_id(1)

    @pl.when((pl.program_id(0) == 0) & (k == 0))
    def _():
        dg_ref[...] = jnp.zeros_like(dg_ref)

    @pl.when(k == 0)
    def _():
        o_ref[...] = part

    @pl.when(k > 0)
    def _():
        o_ref[...] += part

    @pl.when(k == steps - 1)
    def _():
        _, vjp = jax.vjp(_rms, h_ref[...], g_ref[...])
        dh, dg = vjp(o_ref[...])
        o_ref[...] = r_ref[...] + dh
        dg_ref[...] += dg


def mm_nt_norm(dy, w, norm, name):
    pieces = dy if isinstance(dy, tuple) else (dy,)
    nd = len(pieces)
    M, N = pieces[0].shape[0], sum(p.shape[1] for p in pieces)
    tm, tn = (min(NORM_ROWS, M), _tile(N, 1024)) if nd == 1 else (min(512, M), N)

    def body(*refs):
        w_ref, h_ref, g_ref, r_ref, o_ref, dg_ref = refs[nd:]
        _acc_then_norm_bwd(_dg(_cols(refs[:nd]).astype(bf16), w_ref[...], 1, 1), N // tn, h_ref, g_ref, r_ref, o_ref,
                           dg_ref)

    tok = pl.BlockSpec((tm, D), lambda m, n: (m, 0))
    vec = pl.BlockSpec((1, D), lambda m, n: (0, 0))
    return pl.pallas_call(
        body, name=name, grid=(M // tm, N // tn),
        in_specs=[pl.BlockSpec((tm, tn if nd == 1 else p.shape[1]), lambda m, n: (m, n)) for p in pieces]
        + [pl.BlockSpec((D, tn), lambda m, n: (0, n)), tok, vec, tok],
        out_specs=[tok, vec],
        out_shape=[jax.ShapeDtypeStruct((M, D), f32), jax.ShapeDtypeStruct((1, D), f32)],
        compiler_params=_cp(("arbitrary", "arbitrary")),
    )(*pieces, w, norm[0], norm[1].reshape(1, D), norm[2])


def _cols(refs):
    return refs[0][...] if len(refs) == 1 else jnp.concatenate([r[...] for r in refs], axis=1)


def mm_tn(a, dy, name="mm_tn"):
    pieces = a if isinstance(a, tuple) else (a,)
    dpieces = dy if isinstance(dy, tuple) else (dy,)
    na, nd = len(pieces), len(dpieces)
    M, K = pieces[0].shape[0], sum(p.shape[1] for p in pieces)
    N = sum(p.shape[1] for p in dpieces)
    tm = min(MM_ROWS, M)
    tk = _tile(K, 1408) if na == 1 else K
    tn = _tile(N, 1024) if nd == 1 else N

    def body(*refs):
        o_ref = refs[-1]

        @pl.when(pl.program_id(2) == 0)
        def _():
            o_ref[...] = jnp.zeros_like(o_ref)
        o_ref[...] += _dg(_cols(refs[:na]).astype(bf16), _cols(refs[na:na + nd]).astype(bf16), 0, 0)

    a_specs = [pl.BlockSpec((tm, tk if na == 1 else p.shape[1]), lambda k, n, m: (m, k)) for p in pieces]
    d_specs = [pl.BlockSpec((tm, tn if nd == 1 else p.shape[1]), lambda k, n, m: (m, n)) for p in dpieces]
    return pl.pallas_call(
        body, name=name, grid=(K // tk, N // tn, M // tm), in_specs=a_specs + d_specs,
        out_specs=pl.BlockSpec((tk, tn), lambda k, n, m: (k, n)),
        out_shape=jax.ShapeDtypeStruct((K, N), f32),
        compiler_params=_cp(("parallel", "parallel", "arbitrary")),
    )(*pieces, *dpieces)


def rms_fwd(h, g, name):
    S = h.shape[0]
    t = min(512, S)

    def body(h_ref, g_ref, o_ref):
        o_ref[...] = _rms(h_ref[...], g_ref[...]).astype(bf16)

    return pl.pallas_call(
        body, name=name, grid=(S // t,),
        in_specs=[pl.BlockSpec((t, D), lambda i: (i, 0)), pl.BlockSpec((1, D), lambda i: (0, 0))],
        out_specs=pl.BlockSpec((t, D), lambda i: (i, 0)),
        out_shape=jax.ShapeDtypeStruct((S, D), bf16),
        compiler_params=_cp(("parallel",)),
    )(h, g.reshape(1, D))


def loss_head(h, g, target):
    S = h.shape[0]
    t = min(512, S)

    def f(hh, gg, tt):
        err = _rms(hh, gg) - tt
        return 0.5 * jnp.sum(jnp.mean(err * err, axis=-1, keepdims=True), axis=0, keepdims=True)

    def body(h_ref, g_ref, t_ref, loss_ref, dh_ref, dg_ref):
        @pl.when(pl.program_id(0) == 0)
        def _():
            dg_ref[...] = jnp.zeros_like(dg_ref)
            loss_ref[...] = jnp.zeros_like(loss_ref)
        val, vjp = jax.vjp(lambda a, b: f(a, b, t_ref[...]), h_ref[...], g_ref[...])
        dh, dg = vjp(jnp.ones((1, 1), f32))
        dh_ref[...] = dh
        dg_ref[...] += dg
        loss_ref[...] += jnp.broadcast_to(val, loss_ref.shape)

    tok = pl.BlockSpec((t, D), lambda i: (i, 0))
    vec = pl.BlockSpec((1, D), lambda i: (0, 0))
    return pl.pallas_call(
        body, name="loss_head", grid=(S // t,), in_specs=[tok, vec, tok],
        out_specs=[pl.BlockSpec((1, LANE), lambda i: (0, 0)), tok, vec],
        out_shape=[jax.ShapeDtypeStruct((1, LANE), f32), jax.ShapeDtypeStruct((S, D), f32),
                   jax.ShapeDtypeStruct((1, D), f32)],
        compiler_params=_cp(("arbitrary",)),
    )(h, g.reshape(1, D), target)


def memkv_fwd(mem, g, w, name):
    def body(m_ref, g_ref, w_ref, o_ref):
        o_ref[...] = _dg(_rms(m_ref[...], g_ref[...]).astype(bf16), w_ref[...], 1, 0)

    return pl.pallas_call(
        body, name=name, out_shape=jax.ShapeDtypeStruct((MEM_LEN, 2 * X_Q), f32), compiler_params=_cp(),
    )(mem, g.reshape(1, D), w)


def memkv_bwd(mem, g, w, dkv, name):
    def body(m_ref, g_ref, w_ref, d_ref, dg_ref, dw_ref):
        n, vjp = jax.vjp(lambda gg: _rms(m_ref[...], gg), g_ref[...])
        db = d_ref[...].astype(bf16)
        dw_ref[...] = _dg(n.astype(bf16), db, 0, 0)
        dg_ref[...] = vjp(_dg(db, w_ref[...], 1, 1))[0]

    return pl.pallas_call(
        body, name=name,
        out_shape=[jax.ShapeDtypeStruct((1, D), f32), jax.ShapeDtypeStruct((D, 2 * X_Q), f32)],
        compiler_params=_cp(),
    )(mem, g.reshape(1, D), w, dkv)


def _xattn_f(xq, mk, mv):
    lane = lax.broadcasted_iota(jnp.int32, (1, X_Q), 1)
    out = jnp.zeros(xq.shape, f32)
    for hd in range(4):
        msk = (lane // 64 == hd).astype(f32)
        s = bdot_nt(xq * (msk * (64 ** -0.5)), mk)
        m = lax.stop_gradient(jnp.max(s, axis=-1, keepdims=True))
        p = jnp.exp(s - m)
        p = p / jnp.sum(p, axis=-1, keepdims=True)
        out = out + bdot(p, mv * msk)
    return out


def xattn_fwd(proj, col, kv, name):
    S = proj.shape[0]
    t = min(512, S)
    cb = col // X_Q

    def body(q_ref, k_ref, v_ref, o_ref):
        o_ref[...] = _xattn_f(q_ref[...].astype(f32), k_ref[...], v_ref[...]).astype(bf16)

    return pl.pallas_call(
        body, name=name, grid=(S // t,),
        in_specs=[pl.BlockSpec((t, X_Q), lambda i: (i, cb)), pl.BlockSpec((MEM_LEN, X_Q), lambda i: (0, 0)),
                  pl.BlockSpec((MEM_LEN, X_Q), lambda i: (0, 1))],
        out_specs=pl.BlockSpec((t, X_Q), lambda i: (i, 0)),
        out_shape=jax.ShapeDtypeStruct((S, X_Q), bf16),
        compiler_params=_cp(("parallel",)),
    )(proj, kv, kv)


def xattn_bwd(proj, col, kv, dmix, name):
    S = proj.shape[0]
    t = min(512, S)
    cb = col // X_Q

    def body(q_ref, k_ref, v_ref, do_ref, dq_ref, dk_ref, dv_ref):
        @pl.when(pl.program_id(0) == 0)
        def _():
            dk_ref[...] = jnp.zeros_like(dk_ref)
            dv_ref[...] = jnp.zeros_like(dv_ref)
        _, vjp = jax.vjp(_xattn_f, q_ref[...].astype(f32), k_ref[...], v_ref[...])
        dq, dk, dv = vjp(do_ref[...])
        dq_ref[...] = dq.astype(bf16)
        dk_ref[...] += dk
        dv_ref[...] += dv

    kvb = pl.BlockSpec((MEM_LEN, X_Q), lambda i: (0, 0))
    dq, dk, dv = pl.pallas_call(
        body, name=name, grid=(S // t,),
        in_specs=[pl.BlockSpec((t, X_Q), lambda i: (i, cb)), kvb,
                  pl.BlockSpec((MEM_LEN, X_Q), lambda i: (0, 1)), pl.BlockSpec((t, X_Q), lambda i: (i, 3))],
        out_specs=[pl.BlockSpec((t, X_Q), lambda i: (i, 0)), kvb, kvb],
        out_shape=[jax.ShapeDtypeStruct((S, X_Q), bf16), jax.ShapeDtypeStruct((MEM_LEN, X_Q), f32),
                   jax.ShapeDtypeStruct((MEM_LEN, X_Q), f32)],
        compiler_params=_cp(("arbitrary",)),
    )(proj, kv, kv, dmix)
    return dq, jnp.concatenate([dk, dv], axis=1)


def _bucket_map():
    qi = np.arange(BLK)[:, None]
    kj = np.arange(2 * BLK)[None, :]
    n = np.maximum(BLK + qi - kj, 0)
    max_exact = N_BUCKETS // 2
    nf = np.maximum(n, 1).astype(np.float64)
    large = max_exact + (np.log(nf / max_exact) / math.log(MAX_DIST / max_exact)
                         * (N_BUCKETS - max_exact)).astype(np.int32)
    large = np.minimum(large, N_BUCKETS - 1)
    return np.where(n < max_exact, n, large).astype(np.int32)


def bias_build(rel_bias):
    def body(rb_ref, bk_ref, o_ref):
        bk = bk_ref[...]
        for h in range(A_HEADS):
            acc = jnp.zeros((BLK, 2 * BLK), f32)
            for b in range(N_BUCKETS):
                acc = jnp.where(bk == b, rb_ref[b, h], acc)
            o_ref[h] = acc

    return pl.pallas_call(
        body, name="bias_build",
        in_specs=[pl.BlockSpec(memory_space=pltpu.SMEM), pl.BlockSpec(memory_space=pltpu.VMEM)],
        out_specs=pl.BlockSpec(memory_space=pltpu.VMEM),
        out_shape=jax.ShapeDtypeStruct((A_HEADS, BLK, 2 * BLK), f32), compiler_params=_cp(),
    )(rel_bias, jnp.asarray(_bucket_map()))


def bias_grad(dbias):
    def body(d_ref, bk_ref, o_ref):
        bk = bk_ref[...]
        row = lax.broadcasted_iota(jnp.int32, (N_BUCKETS, LANE), 0)
        lane = lax.broadcasted_iota(jnp.int32, (N_BUCKETS, LANE), 1)
        acc = jnp.zeros((N_BUCKETS, LANE), f32)
        for h in range(A_HEADS):
            d = d_ref[h]
            for b in range(N_BUCKETS):
                s = jnp.sum(jnp.where(bk == b, d, 0.0), keepdims=True)
                acc = acc + jnp.where((row == b) & (lane == h), s, 0.0)
        o_ref[...] = acc

    return pl.pallas_call(
        body, name="bias_grad", out_shape=jax.ShapeDtypeStruct((N_BUCKETS, LANE), f32), compiler_params=_cp(),
    )(dbias, jnp.asarray(_bucket_map()))


def _swa_f(qb, kp, kc, vp, vc, bias, sk, first):
    kband = jnp.concatenate([kp, kc], axis=0)
    vband = jnp.concatenate([vp, vc], axis=0)
    qi = lax.broadcasted_iota(jnp.int32, (BLK, 2 * BLK), 0)
    kj = lax.broadcasted_iota(jnp.int32, (BLK, 2 * BLK), 1)
    rel = kj - qi
    ok = (rel >= 1) & (rel <= BLK) & ((kj >= BLK) | jnp.logical_not(first))
    lane = lax.broadcasted_iota(jnp.int32, (1, LANE), 1)
    lane_b = lax.broadcasted_iota(jnp.int32, (BLK, LANE), 1)
    outs = []
    for p in range(A_HEADS // 2):
        qp = qb[:, LANE * p:LANE * (p + 1)]
        acc = jnp.zeros((BLK, LANE), f32)
        for g in range(2):
            h = g * (A_HEADS // 2) + p
            msk = (lane // A_DH == g).astype(f32)
            s = bdot_nt(qp * (msk * (A_DH ** -0.5)), kband) + bias[h]
            s = jnp.where(ok, s, -1e30)
            skb = jnp.broadcast_to(sk[h:h + 1, :], (BLK, LANE))
            sink = jnp.sum(jnp.where(lane_b == 0, skb, 0.0), axis=-1, keepdims=True)
            m = lax.stop_gradient(jnp.maximum(jnp.max(s, axis=-1, keepdims=True), sink))
            e = jnp.exp(s - m)
            prob = e / (jnp.sum(e, axis=-1, keepdims=True) + jnp.exp(sink - m))
            acc = acc + bdot(prob, vband) * msk
        outs.append(acc)
    return jnp.concatenate(outs, axis=1)


def _swa_specs(nb, rev):
    bi = (lambda i: nb - 1 - i) if rev else (lambda i: i)
    return [
        pl.BlockSpec((BLK, A_Q), lambda i: (bi(i), 0)),
        pl.BlockSpec((BLK, LANE), lambda i: (jnp.maximum(bi(i) - 1, 0), 6)),
        pl.BlockSpec((BLK, LANE), lambda i: (bi(i), 6)),
        pl.BlockSpec((BLK, LANE), lambda i: (jnp.maximum(bi(i) - 1, 0), 7)),
        pl.BlockSpec((BLK, LANE), lambda i: (bi(i), 7)),
        pl.BlockSpec((A_HEADS, BLK, 2 * BLK), lambda i: (0, 0, 0)),
        pl.BlockSpec((16, LANE), lambda i: (0, 0)),
    ]


def swa_fwd(proj, bias, sk):
    S = proj.shape[0]
    nb = S // BLK

    def body(q_ref, kp_ref, kc_ref, vp_ref, vc_ref, b_ref, s_ref, o_ref):
        qkv = [r[...].astype(f32) for r in (q_ref, kp_ref, kc_ref, vp_ref, vc_ref)]
        o_ref[...] = _swa_f(*qkv, b_ref[...], s_ref[...], pl.program_id(0) == 0).astype(bf16)

    return pl.pallas_call(
        body, name="swa_fwd", grid=(nb,), in_specs=_swa_specs(nb, False),
        out_specs=pl.BlockSpec((BLK, A_Q), lambda i: (i, 0)),
        out_shape=jax.ShapeDtypeStruct((S, A_Q), bf16), compiler_params=_cp(("parallel",)),
    )(proj, proj, proj, proj, proj, bias, sk)


def swa_bwd(proj, bias, sk, dmix):
    S = proj.shape[0]
    nb = S // BLK

    def body(q_ref, kp_ref, kc_ref, vp_ref, vc_ref, b_ref, s_ref, do_ref, dqkv_ref, db_ref, ds_ref, ck, cv):
        i = pl.program_id(0)

        @pl.when(i == 0)
        def _():
            db_ref[...] = jnp.zeros_like(db_ref)
            ds_ref[...] = jnp.zeros_like(ds_ref)
            ck[...] = jnp.zeros_like(ck)
            cv[...] = jnp.zeros_like(cv)
        first = i == nb - 1
        qkv = [r[...].astype(f32) for r in (q_ref, kp_ref, kc_ref, vp_ref, vc_ref)]
        _, vjp = jax.vjp(lambda *a: _swa_f(*a, first), *qkv, b_ref[...], s_ref[...])
        dq, dkp, dkc, dvp, dvc, db, ds = vjp(do_ref[...])
        dqkv_ref[...] = jnp.concatenate([dq, dkc + ck[...], dvc + cv[...]], axis=1).astype(bf16)
        ck[...] = dkp
        cv[...] = dvp
        db_ref[...] += db
        ds_ref[...] += ds

    return pl.pallas_call(
        body, name="swa_bwd", grid=(nb,),
        in_specs=_swa_specs(nb, True) + [pl.BlockSpec((BLK, A_Q), lambda i: (nb - 1 - i, 0))],
        out_specs=[pl.BlockSpec((BLK, D), lambda i: (nb - 1 - i, 0)),
                   pl.BlockSpec((A_HEADS, BLK, 2 * BLK), lambda i: (0, 0, 0)),
                   pl.BlockSpec((16, LANE), lambda i: (0, 0))],
        out_shape=[jax.ShapeDtypeStruct((S, D), bf16), jax.ShapeDtypeStruct((A_HEADS, BLK, 2 * BLK), f32),
                   jax.ShapeDtypeStruct((16, LANE), f32)],
        scratch_shapes=[pltpu.VMEM((BLK, LANE), f32), pltpu.VMEM((BLK, LANE), f32)],
        compiler_params=_cp(("arbitrary",)),
    )(proj, proj, proj, proj, proj, bias, sk, dmix)


def _dnprep_f(xext, w, is_qk):
    c = (w[3:4] * xext + w[2:3] * shift_down(xext, 1) + w[1:2] * shift_down(xext, 2) + w[0:1] * shift_down(xext, 3))
    a = _silu(c)[HALO:]
    n = a * lax.rsqrt(jnp.sum(a * a, axis=-1, keepdims=True) + EPS)
    return jnp.where(is_qk, n, a)


def dnprep_fwd(proj, cw):
    S = proj.shape[0]
    nblk = B_QKV // LANE
    T = S

    def body(x_ref, w_ref, o_ref):
        is_qk = pl.program_id(0) < 2 * B_QK // LANE
        wv = w_ref[...]

        def tile(r0, first):
            o_ref[pl.ds(r0, T), :] = _dnprep_f(_glu_gext(x_ref, r0, first, T), wv, is_qk)

        tile(0, True)

    return pl.pallas_call(
        body, name="dnprep_fwd", grid=(nblk,),
        in_specs=[pl.BlockSpec((S, LANE), lambda j: (0, j)), pl.BlockSpec((4, LANE), lambda j: (0, j))],
        out_specs=pl.BlockSpec((S, LANE), lambda j: (0, j)),
        out_shape=jax.ShapeDtypeStruct((S, B_QKV), f32), compiler_params=_cp(("parallel",)),
    )(proj, cw)


def dnprep_bwd(proj, cw, dqkvn):
    S = proj.shape[0]
    nblk = B_QKV // LANE

    T = S

    def body(x_ref, w_ref, d_ref, dx_ref, dw_ref):
        is_qk = pl.program_id(0) < 2 * B_QK // LANE
        wv = w_ref[...]

        def tile(r0, first):
            _, vjp = jax.vjp(lambda a, b: _dnprep_f(a, b, is_qk), _glu_gext(x_ref, r0, first, T), wv)
            dx, dw = vjp(d_ref[pl.ds(r0, T), :])
            dx_ref[pl.ds(r0, T), :] = dx[HALO:].astype(bf16)
            if not first:
                dx_ref[pl.ds(r0 - HALO, HALO), :] += dx[:HALO]
            return dw

        dw_ref[...] = tile(0, True)

    col = pl.BlockSpec((S, LANE), lambda j: (0, j))
    wsp = pl.BlockSpec((4, LANE), lambda j: (0, j))
    return pl.pallas_call(
        body, name="dnprep_bwd", grid=(nblk,), in_specs=[col, wsp, col], out_specs=[col, wsp],
        out_shape=[jax.ShapeDtypeStruct((S, B_QKV), bf16), jax.ShapeDtypeStruct((4, B_QKV), f32)],
        compiler_params=_cp(("parallel",)),
    )(proj, cw, dqkvn)


def _hdot(a, b, ca=1, cb=0):
    return _dg(a, b, ca, cb, HI)


def _bdg(a, b, ca, cb):
    dn = (((ca,), (cb,)), ((0,), (0,)))
    ah, bh = a.astype(bf16), b.astype(bf16)
    al, bl = (a - ah.astype(f32)).astype(bf16), (b - bh.astype(f32)).astype(bf16)
    return (lax.dot_general(ah, bh, dn, preferred_element_type=f32)
            + lax.dot_general(ah, bl, dn, preferred_element_type=f32)
            + lax.dot_general(al, bh, dn, preferred_element_type=f32))


@jax.custom_vjp
def hbd(a, b):
    return _bdg(a, b, 2, 1)


@jax.custom_vjp
def hbd_nt(a, b):
    return _bdg(a, b, 2, 2)


@jax.custom_vjp
def hbd_tn(a, b):
    return _bdg(a, b, 1, 1)


hbd.defvjp(lambda a, b: (hbd(a, b), (a, b)), lambda r, g: (hbd_nt(g, r[1]), hbd_tn(r[0], g)))
hbd_nt.defvjp(lambda a, b: (hbd_nt(a, b), (a, b)), lambda r, g: (hbd(g, r[1]), hbd_tn(g, r[0])))
hbd_tn.defvjp(lambda a, b: (hbd_tn(a, b), (a, b)), lambda r, g: (hbd_nt(r[1], g), hbd(r[0], g)))


def _stack(xs):
    return jnp.concatenate([x[None] for x in xs], axis=0)


def _lane_col(x, j):
    lane = lax.broadcasted_iota(jnp.int32, (1, LANE), 1)
    return jnp.sum(jnp.where(lane == j, x, 0.0), axis=-1, keepdims=True)


def _tri_inv(a_mat):
    r = lax.broadcasted_iota(jnp.int32, (1, CHUNK, CHUNK), 1)
    c = lax.broadcasted_iota(jnp.int32, (1, CHUNK, CHUNK), 2)
    pw = -a_mat
    inv = (r == c).astype(f32) + pw
    for _ in range(5):
        pw = hbd(pw, pw)
        inv = inv + hbd(inv, pw)
    return inv


@jax.custom_vjp
def _tri_inv_known(a_mat, inv):
    return inv


_tri_inv_known.defvjp(lambda a, inv: (inv, inv),
                      lambda inv, g: (-hbd_tn(inv, hbd_nt(g, inv)), jnp.zeros_like(inv)))


def _dnc_f(q, k, v, seg, prm, inverse=_tri_inv):
    C = CHUNK
    B = q.shape[0]
    rows = seg.shape[0]
    beta_all = _sigmoid(seg)
    xx = seg + prm[1:2]
    g_all = -jnp.exp(prm[0:1]) * (jnp.maximum(xx, 0.0) + jnp.log(1.0 + jnp.exp(-jnp.abs(xx))))
    r2 = lax.broadcasted_iota(jnp.int32, (rows, rows), 0)
    c2 = lax.broadcasted_iota(jnp.int32, (rows, rows), 1)
    within = (r2 >= c2) & (r2 // C == c2 // C)
    gc_all = _hdot(within.astype(f32), g_all)
    beta = _stack([_lane_col(beta_all[C * j:C * (j + 1)], h) for j in range(rows // C) for h in range(6)])
    gc = _stack([_lane_col(gc_all[C * j:C * (j + 1)], 6 + h) for j in range(rows // C) for h in range(6)])
    r = lax.broadcasted_iota(jnp.int32, (1, C, C), 1)
    c = lax.broadcasted_iota(jnp.int32, (1, C, C), 2)
    incl = r >= c
    strict = r > c
    gct = [gc_all[C * j:C * (j + 1)].T for j in range(rows // C)]
    g_row = _stack([jnp.broadcast_to(gct[j][6 + h:7 + h, :], (C, C))
                    for j in range(rows // C) for h in range(6)])
    decay = jnp.where(incl, jnp.exp(jnp.where(incl, gc - g_row, 0.0)), 0.0)
    a_mat = beta * sbd_nt(k, k) * jnp.where(strict, decay, 0.0)
    eg = jnp.exp(gc)
    inv = inverse(a_mat)
    u = hbd(inv, beta * v)
    w = hbd(inv, (beta * eg) * k)
    qc = q * (B_DH ** -0.5)
    attn = sbd_nt(qc, k) * decay
    last = (lax.broadcasted_iota(jnp.int32, (1, C, 1), 1) == C - 1).astype(f32)
    g_last = jnp.sum(gc * last, axis=1, keepdims=True)
    dc = jnp.broadcast_to(jnp.exp(g_last), (B, 1, LANE)).reshape(B, LANE)
    return u, w, qc * eg, k * jnp.exp(g_last - gc), attn, dc, inv


def _b1(a, b, ca, cb):
    return lax.dot_general(a.astype(bf16), b.astype(bf16), (((ca,), (cb,)), ((0,), (0,))), preferred_element_type=f32)


@jax.custom_vjp
def sbd(a, b):
    return _b1(a, b, 2, 1)


@jax.custom_vjp
def sbd_nt(a, b):
    return _b1(a, b, 2, 2)


@jax.custom_vjp
def sbd_tn(a, b):
    return _b1(a, b, 1, 1)


sbd.defvjp(lambda a, b: (sbd(a, b), (a, b)), lambda r, g: (sbd_nt(g, r[1]), sbd_tn(r[0], g)))
sbd_nt.defvjp(lambda a, b: (sbd_nt(a, b), (a, b)), lambda r, g: (sbd(g, r[1]), sbd_tn(g, r[0])))
sbd_tn.defvjp(lambda a, b: (sbd_tn(a, b), (a, b)), lambda r, g: (sbd_nt(r[1], g), sbd(r[0], g)))


def _dns_f(S0, u, w, qd, kt, attn, dcrows):
    dc = _lane_col(dcrows, 0).reshape(6, 1, 1)
    delta = u - sbd(w, S0)
    out = sbd(qd, S0) + sbd(attn, delta)
    return out, dc * S0 + sbd_tn(kt, delta)


def _dnpost_f(o, z, grow):
    outs = []
    for h in range(6):
        oh = o[:, LANE * h:LANE * (h + 1)]
        outs.append(oh * lax.rsqrt(jnp.mean(oh * oh, axis=-1, keepdims=True) + EPS) * grow
                    * _silu(z[:, LANE * h:LANE * (h + 1)]))
    return jnp.concatenate(outs, axis=1)


def _hs(h):
    return slice(LANE * h, LANE * (h + 1))


DN_CHUNKS = 4


def _heads(ref, share):
    return _stack([ref[CHUNK * j:CHUNK * (j + 1), _hs(h // share)]
                   for j in range(ref.shape[0] // CHUNK) for h in range(6)])


def _put_heads(ref, val):
    for j in range(ref.shape[0] // CHUNK):
        for h in range(6):
            ref[CHUNK * j:CHUNK * (j + 1), _hs(h)] = val[6 * j + h].astype(ref.dtype)


def _dnc_in_specs():
    rows = CHUNK * DN_CHUNKS
    return [
        pl.BlockSpec((rows, B_QK), lambda n: (n, 0)),
        pl.BlockSpec((rows, B_QK), lambda n: (n, 1)),
        pl.BlockSpec((rows, B_V), lambda n: (n, 1)),
        pl.BlockSpec((rows, LANE), lambda n: (n, 20)),
        pl.BlockSpec((8, LANE), lambda n: (0, 0)),
    ]


def _dnc_out_specs(rev_nc=None, chunks=1):
    ci = (lambda n: n) if rev_nc is None else (lambda n: rev_nc - 1 - n)
    wide = pl.BlockSpec((CHUNK * chunks, B_V), lambda n: (ci(n), 0))
    return [wide, wide, wide, wide, pl.BlockSpec((chunks, 6, CHUNK, CHUNK), lambda n: (ci(n), 0, 0, 0)),
            pl.BlockSpec((chunks, 8, LANE), lambda n: (ci(n), 0, 0))]


def _dc_rows(dc):
    pad = jnp.zeros((2, LANE), f32)
    return _stack([jnp.concatenate([dc[6 * j:6 * (j + 1)], pad], axis=0) for j in range(dc.shape[0] // 6)])


def _dnc_shapes(S, mm=f32):
    nc = S // CHUNK
    return [jax.ShapeDtypeStruct((S, B_V), f32)] + [jax.ShapeDtypeStruct((S, B_V), mm)] * 3 + [
        jax.ShapeDtypeStruct((nc, 6, CHUNK, CHUNK), mm), jax.ShapeDtypeStruct((nc, 8, LANE), f32)]


def dnc_fwd(qkvn, proj, prm):
    S = proj.shape[0]

    def body(q_ref, k_ref, v_ref, s_ref, p_ref, u_ref, w_ref, qd_ref, kt_ref, at_ref, dc_ref, inv_ref):
        u, w, qd, kt, attn, dc, inv = _dnc_f(_heads(q_ref, 2), _heads(k_ref, 2), _heads(v_ref, 1), s_ref[...],
                                             p_ref[...])
        inv_ref[...] = inv.reshape(inv_ref.shape)
        _put_heads(u_ref, u)
        _put_heads(w_ref, w)
        _put_heads(qd_ref, qd)
        _put_heads(kt_ref, kt)
        at_ref[...] = attn.reshape(at_ref.shape).astype(at_ref.dtype)
        dc_ref[...] = _dc_rows(dc)

    outs = _dnc_out_specs(chunks=DN_CHUNKS)
    out = pl.pallas_call(
        body, name="dn_chunk_fwd", grid=(S // (CHUNK * DN_CHUNKS),), in_specs=_dnc_in_specs(),
        out_specs=outs + [outs[4]], out_shape=_dnc_shapes(S, bf16) + [_dnc_shapes(S)[4]],
        compiler_params=_cp(("parallel",)),
    )(qkvn, qkvn, qkvn, proj, prm)
    return out[:6], out[6]


def dnc_bwd(qkvn, proj, prm, inv, cots):
    S = proj.shape[0]

    def body(q_ref, k_ref, v_ref, s_ref, p_ref, inv_ref, du_ref, dw_ref, dqd_ref, dkt_ref, dat_ref, ddc_ref,
             dx_ref, dseg_ref, dprm_ref):
        @pl.when(pl.program_id(0) == 0)
        def _():
            dprm_ref[...] = jnp.zeros_like(dprm_ref)
        nb = 6 * DN_CHUNKS
        known = functools.partial(_tri_inv_known, inv=inv_ref[...].reshape(nb, CHUNK, CHUNK))
        _, vjp = jax.vjp(lambda *a: _dnc_f(*a, inverse=known)[:6], _heads(q_ref, 2), _heads(k_ref, 2),
                         _heads(v_ref, 1), s_ref[...], p_ref[...])
        ddc = jnp.concatenate([ddc_ref[j, 0:6, :] for j in range(DN_CHUNKS)], axis=0)
        dq, dk, dv, dseg, dprm = vjp((_heads(du_ref, 1), _heads(dw_ref, 1), _heads(dqd_ref, 1), _heads(dkt_ref, 1),
                                      dat_ref[...].reshape(nb, CHUNK, CHUNK), ddc))
        for j in range(DN_CHUNKS):
            o = 6 * j
            dx_ref[CHUNK * j:CHUNK * (j + 1), :] = jnp.concatenate(
                [dq[o] + dq[o + 1], dq[o + 2] + dq[o + 3], dq[o + 4] + dq[o + 5],
                 dk[o] + dk[o + 1], dk[o + 2] + dk[o + 3], dk[o + 4] + dk[o + 5]] + [dv[o + h] for h in range(6)], axis=1)
        dseg_ref[...] = dseg.astype(bf16)
        dprm_ref[...] += dprm

    rows = CHUNK * DN_CHUNKS
    outs = _dnc_out_specs(chunks=DN_CHUNKS)
    return pl.pallas_call(
        body, name="dn_chunk_bwd", grid=(S // rows,),
        in_specs=_dnc_in_specs() + [outs[4]] + outs,
        out_specs=[pl.BlockSpec((rows, B_QKV), lambda n: (n, 0)), pl.BlockSpec((rows, LANE), lambda n: (n, 0)),
                   pl.BlockSpec((8, LANE), lambda n: (0, 0))],
        out_shape=[jax.ShapeDtypeStruct((S, B_QKV), f32), jax.ShapeDtypeStruct((S, LANE), bf16),
                   jax.ShapeDtypeStruct((8, LANE), f32)],
        compiler_params=_cp(("arbitrary",)),
    )(qkvn, qkvn, qkvn, proj, prm, inv, *cots)


def dns_fwd(chunked):
    u = chunked[0]
    S = u.shape[0]
    nc = S // CHUNK

    def body(u_ref, w_ref, qd_ref, kt_ref, at_ref, dc_ref, o_ref, st_ref, st):
        @pl.when(pl.program_id(0) == 0)
        def _():
            st[...] = jnp.zeros_like(st)
        S0 = st[...]
        st_ref[0] = S0
        out, S1 = _dns_f(S0, _heads(u_ref, 1), _heads(w_ref, 1), _heads(qd_ref, 1), _heads(kt_ref, 1),
                         at_ref[0], dc_ref[0, 0:6, :])
        _put_heads(o_ref, out)
        st[...] = S1

    return pl.pallas_call(
        body, name="dn_scan_fwd", grid=(nc,), in_specs=_dnc_out_specs(),
        out_specs=[pl.BlockSpec((CHUNK, B_V), lambda n: (n, 0)),
                   pl.BlockSpec((1, 6, B_DH, B_DH), lambda n: (n, 0, 0, 0))],
        out_shape=[jax.ShapeDtypeStruct((S, B_V), f32), jax.ShapeDtypeStruct((nc, 6, B_DH, B_DH), f32)],
        scratch_shapes=[pltpu.VMEM((6, B_DH, B_DH), f32)],
        compiler_params=_cp(("arbitrary",)),
    )(*chunked)


def dns_bwd(chunked, states, do):
    S = do.shape[0]
    nc = S // CHUNK

    def body(u_ref, w_ref, qd_ref, kt_ref, at_ref, dc_ref, st_ref, do_ref,
             du_ref, dw_ref, dqd_ref, dkt_ref, dat_ref, ddc_ref, dst):
        @pl.when(pl.program_id(0) == 0)
        def _():
            dst[...] = jnp.zeros_like(dst)
        _, vjp = jax.vjp(_dns_f, st_ref[0], _heads(u_ref, 1), _heads(w_ref, 1).astype(f32),
                         _heads(qd_ref, 1).astype(f32), _heads(kt_ref, 1).astype(f32), at_ref[0].astype(f32),
                         dc_ref[0, 0:6, :])
        dS0, du, dw, dqd, dkt, dat, ddc = vjp((_heads(do_ref, 1), dst[...]))
        dst[...] = dS0
        _put_heads(du_ref, du)
        _put_heads(dw_ref, dw)
        _put_heads(dqd_ref, dqd)
        _put_heads(dkt_ref, dkt)
        dat_ref[0] = dat
        ddc_ref[0] = jnp.concatenate([ddc, jnp.zeros((2, LANE), f32)], axis=0)

    return pl.pallas_call(
        body, name="dn_scan_bwd", grid=(nc,),
        in_specs=_dnc_out_specs(nc) + [pl.BlockSpec((1, 6, B_DH, B_DH), lambda n: (nc - 1 - n, 0, 0, 0)),
                                       pl.BlockSpec((CHUNK, B_V), lambda n: (nc - 1 - n, 0))],
        out_specs=_dnc_out_specs(nc), out_shape=_dnc_shapes(S),
        scratch_shapes=[pltpu.VMEM((6, B_DH, B_DH), f32)],
        compiler_params=_cp(("arbitrary",)),
    )(*chunked, states, do)


def dnpost_fwd(o, proj, prm):
    S = o.shape[0]
    t = min(512, S)

    def body(o_ref, z_ref, p_ref, y_ref):
        y_ref[...] = _dnpost_f(o_ref[...], z_ref[...], p_ref[2:3, :]).astype(bf16)

    tok = pl.BlockSpec((t, B_V), lambda i: (i, 0))
    return pl.pallas_call(
        body, name="dn_post_fwd", grid=(S // t,),
        in_specs=[tok, pl.BlockSpec((t, B_V), lambda i: (i, 2)), pl.BlockSpec((8, LANE), lambda i: (0, 0))],
        out_specs=tok, out_shape=jax.ShapeDtypeStruct((S, B_V), bf16), compiler_params=_cp(("parallel",)),
    )(o, proj, prm)


def dnpost_bwd(o, proj, prm, dmix):
    S = o.shape[0]
    t = min(512, S)

    def body(o_ref, z_ref, p_ref, dy_ref, do_ref, dz_ref, dg_ref):
        @pl.when(pl.program_id(0) == 0)
        def _():
            dg_ref[...] = jnp.zeros_like(dg_ref)
        _, vjp = jax.vjp(_dnpost_f, o_ref[...], z_ref[...], p_ref[2:3, :])
        do, dz, dg = vjp(dy_ref[...])
        do_ref[...] = do
        dz_ref[...] = dz.astype(bf16)
        dg_ref[...] += dg

    tok = pl.BlockSpec((t, B_V), lambda i: (i, 0))
    return pl.pallas_call(
        body, name="dn_post_bwd", grid=(S // t,),
        in_specs=[tok, pl.BlockSpec((t, B_V), lambda i: (i, 2)), pl.BlockSpec((8, LANE), lambda i: (0, 0)), tok],
        out_specs=[tok, tok, pl.BlockSpec((1, LANE), lambda i: (0, 0))],
        out_shape=[jax.ShapeDtypeStruct((S, B_V), f32), jax.ShapeDtypeStruct((S, B_V), bf16),
                   jax.ShapeDtypeStruct((1, LANE), f32)],
        compiler_params=_cp(("arbitrary",)),
    )(o, proj, prm, dmix)


N_FF_BLK = D_FF // LANE
GU_SHARD = 2 * D_FF // 4


GLU_ROWS = 256
HALO = 16


def _glu_conv(gext, w, b):
    return (w[2:3] * gext + w[1:2] * shift_down(gext, 1) + w[0:1] * shift_down(gext, 2) + b)[HALO:]


def _glu_gate(c, up):
    return _silu(c) * up


def _glu_gext(g_ref, r0, first, T=GLU_ROWS):
    if first:
        return jnp.concatenate([jnp.zeros((HALO, LANE), f32), g_ref[0:T, :].astype(f32)], axis=0)
    return g_ref[pl.ds(r0 - HALO, T + HALO), :].astype(f32)


def glu_fwd(gu, w, b, name):
    S = gu.shape[0]
    T = min(GLU_ROWS, S // 2)

    def body(g_ref, u_ref, w_ref, b_ref, o_ref, c_ref):
        wv, bv = w_ref[...], b_ref[...]

        def tile(r0, first):
            c = _glu_conv(_glu_gext(g_ref, r0, first, T), wv, bv)
            c_ref[pl.ds(r0, T), :] = c.astype(bf16)
            o_ref[pl.ds(r0, T), :] = _glu_gate(c, u_ref[pl.ds(r0, T), :].astype(f32)).astype(bf16)

        tile(0, True)

        @pl.loop(1, S // T)
        def _(t):
            tile(pl.multiple_of(t * T, T), False)

    col = pl.BlockSpec((S, LANE), lambda j: (0, j))
    return pl.pallas_call(
        body, name=name, grid=(N_FF_BLK,),
        in_specs=[col, pl.BlockSpec((S, LANE), lambda j: (0, N_FF_BLK + j)), pl.BlockSpec((3, LANE), lambda j: (0, j)),
                  pl.BlockSpec((1, LANE), lambda j: (0, j))],
        out_specs=[col, col], out_shape=[jax.ShapeDtypeStruct((S, D_FF), bf16)] * 2,
        compiler_params=_cp(("parallel",)),
    )(gu, gu, w, b.reshape(1, D_FF))


def glu_bwd(gu, c, w, b, dact, name):
    S = gu.shape[0]
    T = min(GLU_ROWS, S // 2)

    def body(g_ref, u_ref, c_ref, w_ref, b_ref, d_ref, dg_ref, dw_ref, db_ref, acc):
        wv, bv = w_ref[...], b_ref[...]

        def tile(r0, first):
            rows = pl.ds(r0, T)
            _, vjp_gate = jax.vjp(_glu_gate, c_ref[rows, :].astype(f32), u_ref[rows, :].astype(f32))
            dc, du = vjp_gate(d_ref[rows, :].astype(f32))
            _, vjp_conv = jax.vjp(_glu_conv, _glu_gext(g_ref, r0, first, T), wv, bv)
            dgx, dw, db = vjp_conv(dc)
            acc[pl.ds(r0, T), :] = dgx[HALO:]
            if not first:
                acc[pl.ds(r0 - HALO, HALO), :] += dgx[:HALO]
            dg_ref[1, pl.ds(r0, T), :] = du.astype(bf16)
            return dw, db

        dw0, db0 = tile(0, True)
        dw_ref[...] = dw0
        db_ref[...] = db0

        @pl.loop(1, S // T)
        def _(t):
            dw, db = tile(pl.multiple_of(t * T, T), False)
            dw_ref[...] += dw
            db_ref[...] += db

        dg_ref[0] = acc[...].astype(bf16)

    col = pl.BlockSpec((S, LANE), lambda j: (0, j))
    wsp = pl.BlockSpec((3, LANE), lambda j: (0, j))
    bsp = pl.BlockSpec((1, LANE), lambda j: (0, j))
    return pl.pallas_call(
        body, name=name, grid=(N_FF_BLK,),
        in_specs=[col, pl.BlockSpec((S, LANE), lambda j: (0, N_FF_BLK + j)), col, wsp, bsp, col],
        out_specs=[pl.BlockSpec((2, S, LANE), lambda j: (0, 0, j)), wsp, bsp],
        out_shape=[jax.ShapeDtypeStruct((2, S, D_FF), bf16), jax.ShapeDtypeStruct((3, D_FF), f32),
                   jax.ShapeDtypeStruct((1, D_FF), f32)],
        scratch_shapes=[pltpu.VMEM((S, LANE), f32)],
        compiler_params=_cp(("parallel",)),
    )(gu, gu, c, w, b.reshape(1, D_FF), dact)


def gu_fwd(n2, wg, name):
    S = n2.shape[0]
    tm = min(MM_ROWS, S)

    def body(a_ref, w_ref, o_ref):
        o_ref[...] = _dg(a_ref[...], w_ref[...], 1, 0).astype(bf16)

    return pl.pallas_call(
        body, name=name, grid=(4, S // tm),
        in_specs=[pl.BlockSpec((tm, D), lambda s, m: (m, 0)), pl.BlockSpec((None, D, GU_SHARD), lambda s, m: (s, 0, 0))],
        out_specs=pl.BlockSpec((tm, GU_SHARD), lambda s, m: (m, s)),
        out_shape=jax.ShapeDtypeStruct((S, 2 * D_FF), bf16), compiler_params=_cp(("parallel", "parallel")),
    )(n2, wg)


def gu_bwd_x(dgu, wg, norm, name):
    S = dgu.shape[1]
    tm = min(NORM_ROWS, S)

    def body(d_ref, w_ref, h_ref, g_ref, r_ref, o_ref, dg_ref):
        _acc_then_norm_bwd(_dg(d_ref[...], w_ref[...], 1, 1), 4, h_ref, g_ref, r_ref, o_ref, dg_ref)

    tok = pl.BlockSpec((tm, D), lambda m, s: (m, 0))
    vec = pl.BlockSpec((1, D), lambda m, s: (0, 0))
    return pl.pallas_call(
        body, name=name, grid=(S // tm, 4),
        in_specs=[pl.BlockSpec((None, tm, GU_SHARD), lambda m, s: (s // 2, m, s % 2)),
                  pl.BlockSpec((None, D, GU_SHARD), lambda m, s: (s, 0, 0)), tok, vec, tok],
        out_specs=[tok, vec],
        out_shape=[jax.ShapeDtypeStruct((S, D), f32), jax.ShapeDtypeStruct((1, D), f32)],
        compiler_params=_cp(("arbitrary", "arbitrary")),
    )(dgu, wg, norm[0], norm[1].reshape(1, D), norm[2])


def gu_bwd_w(n2, dgu, name):
    S = n2.shape[0]
    tm = min(MM_ROWS, S)
    nm = S // tm

    def body(a_ref, d_ref, o_ref, acc):
        @pl.when(pl.program_id(1) == 0)
        def _():
            acc[...] = jnp.zeros_like(acc)
        acc[...] += _dg(a_ref[...], d_ref[...], 0, 0)

        @pl.when(pl.program_id(1) == nm - 1)
        def _():
            o_ref[...] = acc[...].astype(bf16)

    return pl.pallas_call(
        body, name=name, grid=(4, nm),
        in_specs=[pl.BlockSpec((tm, D), lambda s, m: (m, 0)),
                  pl.BlockSpec((None, tm, GU_SHARD), lambda s, m: (s // 2, m, s % 2))],
        out_specs=pl.BlockSpec((None, D, GU_SHARD), lambda s, m: (s, 0, 0)),
        out_shape=jax.ShapeDtypeStruct((4, D, GU_SHARD), bf16),
        scratch_shapes=[pltpu.VMEM((D, GU_SHARD), f32)],
        compiler_params=_cp(("parallel", "arbitrary")),
    )(n2, dgu)


def _pair_cols(w):
    lead = w.shape[:-1]
    return w.reshape(lead + (2, 6, A_DH)).swapaxes(-3, -2).reshape(lead + (A_Q,))


def _unpair_cols(w):
    lead = w.shape[:-1]
    return w.reshape(lead + (6, 2, A_DH)).swapaxes(-3, -2).reshape(lead + (A_Q,))


def _lay_in_a(w):
    return jnp.concatenate([_pair_cols(w[:, :A_Q]), w[:, A_Q:]], axis=1)


def _unlay_in_a(w):
    return jnp.concatenate([_unpair_cols(w[:, :A_Q]), w[:, A_Q:]], axis=1)


def _lay_out_a(w):
    return jnp.concatenate([_pair_cols(w[:A_Q].T).T, w[A_Q:]], axis=0)


def _unlay_out_a(w):
    return jnp.concatenate([_unpair_cols(w[:A_Q].T).T, w[A_Q:]], axis=0)


def _lay_in_b(w):
    return jnp.concatenate([w[:, :2304], w[:, 2316:], w[:, 2304:2316],
                            jnp.zeros((w.shape[0], LANE - 12), w.dtype)], axis=1)


def _unlay_in_b(w):
    return jnp.concatenate([w[:, :2304], w[:, 2560:2572], w[:, 2304:2560]], axis=1)


def _chip_cols(w):
    return jnp.moveaxis(w.reshape(w.shape[0], 4, w.shape[1] // 4), 1, 0)


def _unchip_cols(w):
    return jnp.moveaxis(w, 0, 1).reshape(w.shape[1], 4 * w.shape[2])


def _local_step(x, mem, target, P):
    arrive = P.get("arrive", lambda key, after: None)
    ready = P.get("ready", lambda key, grads, dep: dep)
    sk = jnp.zeros((16, LANE), f32).at[:A_HEADS].set(jnp.broadcast_to(P["sinks"][:, None], (A_HEADS, LANE)))
    prm = jnp.zeros((8, LANE), f32).at[0, 6:12].set(P["a_log"]).at[1, 6:12].set(P["dt_bias"]).at[2].set(P["out_norm_g"])
    bias = bias_build(P["rel_bias"])
    saved = []
    h = x
    n1 = rms_fwd(h, P["g_mix"][0], "rms_mix0")
    for i in range(2):
        arrive(("w_in", i), n1)
        if i == 0:
            proj = mm_nn(n1, P["w_in_a"], out_dtype=bf16, name="proj_a")
        else:
            proj = mm_nn(n1, P["w_in_b"], name="proj_b")
        arrive(("w_mem", i), proj)
        kv = memkv_fwd(mem, P["g_mem"][i], P["w_mem"][i], f"memkv{i}")
        if i == 0:
            self_out = swa_fwd(proj, bias, sk)
            cross = xattn_fwd(proj, A_Q + 2 * LANE, kv, "xattn_a")
            extra = ()
        else:
            qkvn = dnprep_fwd(proj, P["conv_qkv"])
            chunked, inv = dnc_fwd(qkvn, proj, prm)
            o, states = dns_fwd(chunked)
            self_out = dnpost_fwd(o, proj, prm)
            cross = xattn_fwd(proj, 2304, kv, "xattn_b")
            extra = (qkvn, chunked, inv, states, o)
        mix = (self_out, cross)
        arrive(("w_out", i), cross)
        h2, n2 = mm_res_norm(mix, P["w_out"][i], h, P["g_ffn"][i], f"out_proj{i}")
        arrive(("w_gu", i), n2)
        gu = gu_fwd(n2, P["w_gu"][i], f"gate_up{i}")
        act, pre = glu_fwd(gu, P["ffn_cw"][i], P["ffn_cb"][i], f"glu{i}")
        arrive(("w_down", i), act)
        saved.append((h, n1, kv, proj, mix, h2, n2, gu, pre, act, extra))
        if i == 0:
            h, n1 = mm_res_norm(act, P["w_down"][i], h2, P["g_mix"][1], f"down{i}")
        else:
            h = mm_nn(act, P["w_down"][i], res=h2, name=f"down{i}")

    loss, dh, dg_fin = loss_head(h, P["g_fin"], target)
    G = {"g_fin": dg_fin[0], "g_mix": [None, None], "g_mem": [None, None], "g_ffn": [None, None],
         "w_mem": [None, None], "w_out": [None, None], "w_gu": [None, None], "w_down": [None, None],
         "ffn_cw": [None, None], "ffn_cb": [None, None]}
    for i in (1, 0):
        hin, n1, kv, proj, mix, h2, n2, gu, pre, act, extra = saved[i]
        dact = mm_nt(dh, P["w_down"][i], out_dtype=bf16, name=f"d_act{i}")
        G["w_down"][i] = mm_tn(act, dh, name=f"dw_down{i}")
        dgu, dcw, dcb = glu_bwd(gu, pre, P["ffn_cw"][i], P["ffn_cb"][i], dact, f"glu_bwd{i}")
        G["ffn_cw"][i], G["ffn_cb"][i] = dcw, dcb[0]
        G["w_gu"][i] = gu_bwd_w(n2, dgu, f"dw_gu{i}")
        g_ffn = ready(("ffn", i), G, P["g_ffn"][i])
        dh2, dg = gu_bwd_x(dgu, P["w_gu"][i], (h2, g_ffn, dh), f"d_n2_{i}")
        G["g_ffn"][i] = dg[0]
        dmix = mm_nt(dh2, P["w_out"][i], name=f"d_mix{i}")
        G["w_out"][i] = mm_tn(mix, dh2, name=f"dw_out{i}")
        g_mix = ready(("tick", i), G, P["g_mix"][i])
        if i == 0:
            dqkv, dbias, dsk = swa_bwd(proj, bias, sk, dmix)
            dxq, dkv = xattn_bwd(proj, A_Q + 2 * LANE, kv, dmix, "xattn_a_bwd")
            dproj = (dqkv, dxq)
            G["sinks"] = dsk[:A_HEADS, 0]
            G["rel_bias"] = bias_grad(dbias)[:, :A_HEADS]
            w_in, gname = P["w_in_a"], "w_in_a"
        else:
            qkvn, chunked, inv, states, o = extra
            do, dz, dgo = dnpost_bwd(o, proj, prm, dmix)
            dqkvn, dseg, dprm = dnc_bwd(qkvn, proj, prm, inv, dns_bwd(chunked, states, do))
            draw, dconv = dnprep_bwd(proj, P["conv_qkv"], dqkvn)
            dxq, dkv = xattn_bwd(proj, 2304, kv, dmix, "xattn_b_bwd")
            dproj = (draw, dz, dxq, dseg)
            G["conv_qkv"] = dconv
            G["a_log"], G["dt_bias"], G["out_norm_g"] = dprm[0, 6:12], dprm[1, 6:12], dgo[0]
            w_in, gname = P["w_in_b"], "w_in_b"
        G[gname] = mm_tn(n1, dproj, name=f"d{gname}")
        dh, dg = mm_nt_norm(dproj, w_in, (hin, g_mix, dh2), f"d_n1_{i}")
        G["g_mix"][i] = dg[0]
        dgm, dwm = memkv_bwd(mem, P["g_mem"][i], P["w_mem"][i], dkv, f"memkv_bwd{i}")
        G["g_mem"][i], G["w_mem"][i] = dgm[0], dwm
        ready(("mix", i), G, None)
    return loss, dh, G


def _grads_to_ref(G):
    return {
        "rel_bias": G["rel_bias"], "norm_mix_g": jnp.stack(G["g_mix"]), "norm_mem_g": jnp.stack(G["g_mem"]),
        "w_mem_kv": jnp.stack(G["w_mem"]),
        "w_out": jnp.stack([_unlay_out_a(G["w_out"][0]), G["w_out"][1]]),
        "w_in_a": _unlay_in_a(G["w_in_a"])[None], "sinks_a": G["sinks"][None],
        "w_in_b": _unlay_in_b(G["w_in_b"])[None], "conv_qkv_b": G["conv_qkv"][None],
        "a_log_b": G["a_log"][None], "dt_bias_b": G["dt_bias"][None], "out_norm_g_b": G["out_norm_g"][None],
        "norm_ffn_g": jnp.stack(G["g_ffn"]),
        "w_gate_up": jnp.stack([_unchip_cols(G["w_gu"][0]), _unchip_cols(G["w_gu"][1])]).astype(f32),
        "ffn_conv_w": jnp.stack(G["ffn_cw"]), "ffn_conv_b": jnp.stack(G["ffn_cb"]),
        "w_down": jnp.stack(G["w_down"]), "final_norm_g": G["g_fin"],
    }


ANY = pl.BlockSpec(memory_space=pl.ANY)


def _place():
    return lax.axis_index("x"), lax.axis_index("y"), lax.axis_index("c")


def allreduce_small(buf):
    R = buf.shape[0]

    def body(b_ref, o_ref, recv, ssem, rsem):
        x, y, c = _place()
        me = 4 * x + 2 * y + c

        def peer(k):
            return (1 - x if k & 4 else x, 1 - y if k & 2 else y, 1 - c if k & 1 else c)

        def remote(k, slot):
            return pltpu.make_async_remote_copy(
                src_ref=b_ref, dst_ref=recv.at[slot], send_sem=ssem.at[k - 1], recv_sem=rsem.at[k - 1],
                device_id=peer(k), device_id_type=MESH)

        sends = [remote(k, me) for k in range(1, 8)]
        for cp in sends:
            cp.start()
        recv[me] = b_ref[...]
        for k in range(1, 8):
            px, py, pc = peer(k)
            remote(k, 4 * px + 2 * py + pc).wait_recv()
        for cp in sends:
            cp.wait_send()
        total = recv[0]
        for j in range(1, 8):
            total = total + recv[j]
        o_ref[...] = total

    return pl.pallas_call(
        body, name="small_allreduce",
        in_specs=[pl.BlockSpec(memory_space=pltpu.VMEM)], out_specs=pl.BlockSpec(memory_space=pltpu.VMEM),
        out_shape=jax.ShapeDtypeStruct(buf.shape, f32),
        scratch_shapes=[pltpu.VMEM((8, R, LANE), f32), pltpu.SemaphoreType.DMA((7,)), pltpu.SemaphoreType.DMA((7,))],
    )(buf)


def sum_slots(own, recv, chip, core, name):
    _, R, C = recv.shape
    tr = _row_tile(R, 256)
    nt = R // tr

    def body(p_ref, a_ref, r_ref, o_ref):
        acc = jnp.zeros((tr, C), f32)
        for s in range(4):
            acc = acc + jnp.where(p_ref[0] == s, a_ref[s], r_ref[s]).astype(f32)
        o_ref[...] = acc

    slots = pl.BlockSpec((4, tr, C), lambda i, p_ref: (0, i, 0))
    return pl.pallas_call(
        body, name=name, out_shape=jax.ShapeDtypeStruct((2 * R, C), f32),
        grid_spec=pltpu.PrefetchScalarGridSpec(
            num_scalar_prefetch=1, grid=(nt,), in_specs=[slots, slots],
            out_specs=pl.BlockSpec((tr, C), lambda i, p_ref: (p_ref[1] * nt + i, 0))),
        compiler_params=_cp(("parallel",)),
    )(jnp.stack([chip, core]).astype(jnp.int32), own, recv)


def _half(ref, core, axis=0):
    half = ref.shape[axis] // 2
    idx = (slice(None),) * axis + (pl.ds(core * half, half),)
    return ref.at[idx]


IN_HBM = pl.BlockSpec(memory_space=pltpu.HBM)
IN_SEM = pl.BlockSpec(memory_space=pltpu.SEMAPHORE)
SIDE_EFFECT = pltpu.SideEffectType.DATAFLOW_SIDE_EFFECTING


def _gather_copy(buf, i, k, ssem, rsem, place, landing):
    x, y, c = place
    px, py = [(1 - x, y), (x, 1 - y), (1 - x, 1 - y)][k]
    me = 2 * x + y
    return pltpu.make_async_remote_copy(
        src_ref=buf.at[me], dst_ref=buf.at[me if landing == "theirs" else 2 * px + py],
        send_sem=ssem.at[3 * i + k], recv_sem=rsem.at[3 * i + k], device_id=(px, py, c), device_id_type=MESH)


def gather_start(groups, name):
    flat = [b for grp in groups for b in grp]
    n, ng = len(flat), len(groups)

    def body(*refs):
        bufs, sems = refs[:n], refs[n:n + 2 * ng]
        place = _place()
        j = 0
        for g, grp in enumerate(groups):
            for i in range(len(grp)):
                for k in range(3):
                    _gather_copy(bufs[j], i, k, sems[2 * g], sems[2 * g + 1], place, "theirs").start()
                j += 1
        refs[-1][...] = jnp.zeros_like(refs[-1])

    sem_shapes = [pltpu.SemaphoreType.DMA((3 * len(grp),)) for grp in groups for _ in range(2)]
    out = pl.pallas_call(
        body, name=name, in_specs=[IN_HBM] * n,
        out_specs=(*[IN_SEM] * (2 * ng), *[IN_HBM] * n, pl.BlockSpec(memory_space=pltpu.VMEM)),
        out_shape=(*sem_shapes, *[pltpu.HBM(b.shape, b.dtype) for b in flat], jax.ShapeDtypeStruct((8, LANE), f32)),
        input_output_aliases={i: 2 * ng + i for i in range(n)},
        compiler_params=pltpu.CompilerParams(has_side_effects=SIDE_EFFECT),
    )(*[pltpu.with_memory_space_constraint(b, pltpu.HBM) for b in flat])
    sems, bufs = out[:2 * ng], list(out[2 * ng:2 * ng + n])
    flights, j = [], 0
    for g, grp in enumerate(groups):
        flights.append((bufs[j:j + len(grp)], sems[2 * g], sems[2 * g + 1]))
        j += len(grp)
    return flights, out[-1]


def gather_wait(flight, after, name):
    bufs, ssem, rsem = flight
    n = len(bufs)

    def body(*refs):
        place = _place()
        for i in range(n):
            for k in range(3):
                cp = _gather_copy(refs[i], i, k, refs[n], refs[n + 1], place, "mine")
                cp.wait_send()
                cp.wait_recv()

    return pl.pallas_call(
        body, name=name, in_specs=[IN_HBM] * n + [IN_SEM, IN_SEM, ANY], out_specs=[IN_HBM] * n,
        out_shape=[pltpu.HBM(b.shape, b.dtype) for b in bufs], input_output_aliases={i: i for i in range(n)},
        compiler_params=pltpu.CompilerParams(has_side_effects=SIDE_EFFECT),
    )(*bufs, ssem, rsem, after)


def _scatter_copy(src, land, j, k, ssem, rsem, place, landing):
    x, y, c = place
    px, py = [(1 - x, y), (x, 1 - y), (1 - x, 1 - y)][k]
    return pltpu.make_async_remote_copy(
        src_ref=src.at[2 * px + py], dst_ref=land.at[2 * x + y if landing == "theirs" else 2 * px + py],
        send_sem=ssem.at[3 * j + k], recv_sem=rsem.at[3 * j + k], device_id=(px, py, c), device_id_type=MESH)


def scatter_start(srcs, name):
    n = len(srcs)
    lands = [lax.empty(g.shape, g.dtype) for g in srcs]

    def body(*refs):
        place = _place()
        for j in range(n):
            for k in range(3):
                _scatter_copy(refs[j], refs[n + j], j, k, refs[2 * n], refs[2 * n + 1], place, "theirs").start()
        refs[-1][...] = jnp.zeros_like(refs[-1])

    sem = pltpu.SemaphoreType.DMA((3 * n,))
    hbm = [pltpu.with_memory_space_constraint(b, pltpu.HBM) for b in list(srcs) + lands]
    out = pl.pallas_call(
        body, name=name, in_specs=[IN_HBM] * (2 * n),
        out_specs=(IN_SEM, IN_SEM, *[IN_HBM] * (2 * n), pl.BlockSpec(memory_space=pltpu.VMEM)),
        out_shape=(sem, sem, *[pltpu.HBM(b.shape, b.dtype) for b in hbm], jax.ShapeDtypeStruct((8, LANE), f32)),
        input_output_aliases={i: 2 + i for i in range(2 * n)},
        compiler_params=pltpu.CompilerParams(has_side_effects=SIDE_EFFECT),
    )(*hbm)
    return (list(out[2:2 + n]), list(out[2 + n:2 + 2 * n]), out[0], out[1]), out[-1]


def scatter_wait(flight, after, name):
    srcs, lands, ssem, rsem = flight
    n = len(srcs)

    def body(*refs):
        place = _place()
        for j in range(n):
            for k in range(3):
                cp = _scatter_copy(refs[j], refs[n + j], j, k, refs[2 * n], refs[2 * n + 1], place, "mine")
                cp.wait_send()
                cp.wait_recv()

    out = pl.pallas_call(
        body, name=name, in_specs=[IN_HBM] * (2 * n) + [IN_SEM, IN_SEM, ANY], out_specs=[IN_HBM] * (2 * n),
        out_shape=[pltpu.HBM(b.shape, b.dtype) for b in list(srcs) + list(lands)],
        input_output_aliases={i: i for i in range(2 * n)},
        compiler_params=pltpu.CompilerParams(has_side_effects=SIDE_EFFECT),
    )(*srcs, *lands, ssem, rsem, after)
    return list(out[:n]), list(out[n:])


def _pair_copy(src, land, j, ssem, rsem, place):
    x, y, c = place
    return pltpu.make_async_remote_copy(
        src_ref=_half(src, 1 - c, axis=1), dst_ref=land, send_sem=ssem.at[j], recv_sem=rsem.at[j],
        device_id=(x, y, 1 - c), device_id_type=MESH)


def pair_start(srcs, name):
    n = len(srcs)
    lands = [lax.empty((4, g.shape[1] // 2, g.shape[2]), g.dtype) for g in srcs]

    def body(*refs):
        place = _place()
        for j in range(n):
            _pair_copy(refs[j], refs[n + j], j, refs[2 * n], refs[2 * n + 1], place).start()
        refs[-1][...] = jnp.zeros_like(refs[-1])

    sem = pltpu.SemaphoreType.DMA((n,))
    hbm = [pltpu.with_memory_space_constraint(b, pltpu.HBM) for b in list(srcs) + lands]
    out = pl.pallas_call(
        body, name=name, in_specs=[IN_HBM] * (2 * n),
        out_specs=(IN_SEM, IN_SEM, *[IN_HBM] * (2 * n), pl.BlockSpec(memory_space=pltpu.VMEM)),
        out_shape=(sem, sem, *[pltpu.HBM(b.shape, b.dtype) for b in hbm], jax.ShapeDtypeStruct((8, LANE), f32)),
        input_output_aliases={i: 2 + i for i in range(2 * n)},
        compiler_params=pltpu.CompilerParams(has_side_effects=SIDE_EFFECT),
    )(*hbm)
    return (list(out[2:2 + n]), list(out[2 + n:2 + 2 * n]), out[0], out[1]), out[-1]


def pair_wait(flight, after, name):
    srcs, lands, ssem, rsem = flight
    n = len(srcs)

    def body(*refs):
        place = _place()
        for j in range(n):
            cp = _pair_copy(refs[j], refs[n + j], j, refs[2 * n], refs[2 * n + 1], place)
            cp.wait_send()
            cp.wait_recv()

    out = pl.pallas_call(
        body, name=name, in_specs=[IN_HBM] * (2 * n) + [IN_SEM, IN_SEM, ANY], out_specs=[IN_HBM] * (2 * n),
        out_shape=[pltpu.HBM(b.shape, b.dtype) for b in list(srcs) + list(lands)],
        input_output_aliases={i: i for i in range(2 * n)},
        compiler_params=pltpu.CompilerParams(has_side_effects=SIDE_EFFECT),
    )(*srcs, *lands, ssem, rsem, after)
    return list(out[:n]), list(out[n:])


def _row_tile(rows, cap=512):
    return max(t for t in range(16, min(rows, cap) + 1, 16) if rows % t == 0)


def pair_sum(mine, theirs, core, name):
    _, R, C = mine.shape
    half = R // 2
    tr = _row_tile(half)
    nt = half // tr

    def body(c_ref, a_ref, b_ref, o_ref):
        o_ref[...] = (a_ref[...].astype(f32) + b_ref[...].astype(f32)).astype(bf16)

    return pl.pallas_call(
        body, name=name, out_shape=jax.ShapeDtypeStruct(theirs.shape, bf16),
        grid_spec=pltpu.PrefetchScalarGridSpec(
            num_scalar_prefetch=1, grid=(4, nt),
            in_specs=[pl.BlockSpec((None, tr, C), lambda s, i, c_ref: (s, c_ref[0] * nt + i, 0)),
                      pl.BlockSpec((None, tr, C), lambda s, i, c_ref: (s, i, 0))],
            out_specs=pl.BlockSpec((None, tr, C), lambda s, i, c_ref: (s, i, 0))),
        compiler_params=_cp(("parallel", "parallel")),
    )(jnp.reshape(core, (1,)).astype(jnp.int32), mine, theirs)


def _final_copy(buf, j, ssem, rsem, place, landing):
    x, y, c = place
    return pltpu.make_async_remote_copy(
        src_ref=_half(buf, c), dst_ref=_half(buf, c if landing == "theirs" else 1 - c),
        send_sem=ssem.at[j], recv_sem=rsem.at[j], device_id=(x, y, 1 - c), device_id_type=MESH)


def final_start(fins, name):
    n = len(fins)

    def body(*refs):
        place = _place()
        for j in range(n):
            _final_copy(refs[j], j, refs[n], refs[n + 1], place, "theirs").start()
        refs[-1][...] = jnp.zeros_like(refs[-1])

    sem = pltpu.SemaphoreType.DMA((n,))
    hbm = [pltpu.with_memory_space_constraint(b, pltpu.HBM) for b in fins]
    out = pl.pallas_call(
        body, name=name, in_specs=[IN_HBM] * n,
        out_specs=(IN_SEM, IN_SEM, *[IN_HBM] * n, pl.BlockSpec(memory_space=pltpu.VMEM)),
        out_shape=(sem, sem, *[pltpu.HBM(b.shape, b.dtype) for b in hbm], jax.ShapeDtypeStruct((8, LANE), f32)),
        input_output_aliases={i: 2 + i for i in range(n)},
        compiler_params=pltpu.CompilerParams(has_side_effects=SIDE_EFFECT),
    )(*hbm)
    return (list(out[2:2 + n]), out[0], out[1]), out[-1]


def final_wait(flight, after, name):
    bufs, ssem, rsem = flight
    n = len(bufs)

    def body(*refs):
        place = _place()
        for j in range(n):
            cp = _final_copy(refs[j], j, refs[n], refs[n + 1], place, "mine")
            cp.wait_send()
            cp.wait_recv()

    out = pl.pallas_call(
        body, name=name, in_specs=[IN_HBM] * n + [IN_SEM, IN_SEM, ANY], out_specs=[IN_HBM] * n,
        out_shape=[pltpu.HBM(b.shape, b.dtype) for b in bufs], input_output_aliases={i: i for i in range(n)},
        compiler_params=pltpu.CompilerParams(has_side_effects=SIDE_EFFECT),
    )(*bufs, ssem, rsem, after)
    return list(out)


def adamw_big(w, m, v, gs, row0, name):
    L, R, C = w.shape
    tr = _row_tile(math.gcd(R, row0) if row0 else R, max(16, 262144 // C // 16 * 16))
    b0 = row0 // tr

    def body(*refs):
        w_ref, m_ref, v_ref = refs[:3]
        g_refs = refs[3:3 + L]
        g_ref, d_ref, nm_ref, nv_ref = refs[3 + L:]
        g = g_refs[0][...]
        for l in range(1, L):
            g = jnp.where(pl.program_id(0) == l, g_refs[l][...], g)
        d, nm, nv = _adamw_math(w_ref[...], g, m_ref[...], v_ref[...])
        g_ref[...] = g
        d_ref[...] = d
        nm_ref[...] = nm
        nv_ref[...] = nv

    own = pl.BlockSpec((None, tr, C), lambda l, i: (l, i, 0))
    off = pl.BlockSpec((tr, C), lambda l, i: (b0 + i, 0))
    return pl.pallas_call(
        body, name=name, grid=(L, R // tr), in_specs=[own, own, own] + [off] * L, out_specs=[own] * 4,
        out_shape=[jax.ShapeDtypeStruct((L, R, C), f32)] * 4, compiler_params=_cp(("parallel", "parallel")),
    )(w, m, v, *gs)


def _adamw_math(w, g, m, v):
    m = B1 * m + (1.0 - B1) * g
    v = B2 * v + (1.0 - B2) * (g * g)
    m_hat = m / (1.0 - B1 ** STEP)
    v_hat = v / (1.0 - B2 ** STEP)
    delta = -LR * (m_hat / (jnp.sqrt(v_hat) + AEPS) + WD * w)
    return delta, m, v


def adamw_small(w, m, v, g):
    def body(w_ref, m_ref, v_ref, g_ref, d_ref, nm_ref, nv_ref):
        d, nm, nv = _adamw_math(w_ref[...], g_ref[...], m_ref[...], v_ref[...])
        d_ref[...] = d
        nm_ref[...] = nm
        nv_ref[...] = nv

    return pl.pallas_call(body, name="adamw_small", out_shape=[jax.ShapeDtypeStruct(w.shape, f32)] * 3)(w, m, v, g)


CONV =(("conv_qkv_b", 2), ("ffn_conv_w", 2))
SMALL = ("rel_bias", "norm_mix_g", "norm_mem_g", "sinks_a", "a_log_b", "dt_bias_b", "out_norm_g_b", "norm_ffn_g",
         "ffn_conv_b", "final_norm_g")
WEIGHTS = ("rel_bias", "norm_mix_g", "norm_mem_g", "w_mem_kv", "w_out", "w_in_a", "sinks_a", "w_in_b", "conv_qkv_b",
           "a_log_b", "dt_bias_b", "out_norm_g_b", "norm_ffn_g", "w_gate_up", "ffn_conv_w", "ffn_conv_b", "w_down",
           "final_norm_g")
ARGS = ("x", "mem") + WEIGHTS + ("loss_target",) + tuple("m_" + n for n in WEIGHTS) + tuple("v_" + n for n in WEIGHTS)


def _rows(a, width):
    flat = a.reshape(-1)
    pad = (-flat.shape[0]) % (8 * width)
    if pad:
        flat = jnp.concatenate([flat, jnp.zeros((pad,), a.dtype)])
    return flat.reshape(-1, width)


def _nrows(shape, width):
    return _pad_to(-(-math.prod(shape) // width), 8)


def _pack(arrs, width, total_rows, dtype):
    parts = [_rows(a.astype(dtype), width) for a in arrs]
    used = sum(p.shape[0] for p in parts)
    if total_rows > used:
        parts.append(jnp.zeros((total_rows - used, width), dtype))
    return jnp.concatenate(parts, axis=0)


def _unpack(buf, shapes, width):
    out, r = [], 0
    for s in shapes:
        n = _nrows(s, width)
        out.append(buf[r:r + n].reshape(-1)[:math.prod(s)].reshape(s))
        r += n
    return out


def _pad_to(n, mult):
    return -(-n // mult) * mult


def kernel(x, mem, rel_bias, norm_mix_g, norm_mem_g, w_mem_kv, w_out, w_in_a, sinks_a, w_in_b, conv_qkv_b, a_log_b, dt_bias_b, out_norm_g_b, norm_ffn_g, w_gate_up, ffn_conv_w, ffn_conv_b, w_down, final_norm_g, loss_target, m_rel_bias, m_norm_mix_g, m_norm_mem_g, m_w_mem_kv, m_w_out, m_w_in_a, m_sinks_a, m_w_in_b, m_conv_qkv_b, m_a_log_b, m_dt_bias_b, m_out_norm_g_b, m_norm_ffn_g, m_w_gate_up, m_ffn_conv_w, m_ffn_conv_b, m_w_down, m_final_norm_g, v_rel_bias, v_norm_mix_g, v_norm_mem_g, v_w_mem_kv, v_w_out, v_w_in_a, v_sinks_a, v_w_in_b, v_conv_qkv_b, v_a_log_b, v_dt_bias_b, v_out_norm_g_b, v_norm_ffn_g, v_w_gate_up, v_ffn_conv_w, v_ffn_conv_b, v_w_down, v_final_norm_g):
    A = dict(zip(ARGS, (x, mem, rel_bias, norm_mix_g, norm_mem_g, w_mem_kv, w_out, w_in_a, sinks_a, w_in_b, conv_qkv_b, a_log_b, dt_bias_b, out_norm_g_b, norm_ffn_g, w_gate_up, ffn_conv_w, ffn_conv_b, w_down, final_norm_g, loss_target, m_rel_bias, m_norm_mix_g, m_norm_mem_g, m_w_mem_kv, m_w_out, m_w_in_a, m_sinks_a, m_w_in_b, m_conv_qkv_b, m_a_log_b, m_dt_bias_b, m_out_norm_g_b, m_norm_ffn_g, m_w_gate_up, m_ffn_conv_w, m_ffn_conv_b, m_w_down, m_final_norm_g, v_rel_bias, v_norm_mix_g, v_norm_mem_g, v_w_mem_kv, v_w_out, v_w_in_a, v_sinks_a, v_w_in_b, v_conv_qkv_b, v_a_log_b, v_dt_bias_b, v_out_norm_g_b, v_norm_ffn_g, v_w_gate_up, v_ffn_conv_w, v_ffn_conv_b, v_w_down, v_final_norm_g)))
    chip = 2 * lax.axis_index("x") + lax.axis_index("y")
    core = lax.axis_index("c")

    def own_slot(shard):
        return lax.dynamic_update_index_in_dim(lax.empty((4,) + shard.shape, shard.dtype), shard, chip, 0)

    def bslot(w, tie=0.0):
        return own_slot((w + tie).astype(bf16))

    early = {
        ("w_in", 0): [bslot(w_in_a[0])],
        ("w_mem", 0): [bslot(w_mem_kv[0]), own_slot(ffn_conv_w.reshape(6, -1))],
        ("w_out", 0): [bslot(w_out[0])],
    }
    flights_a, gone = gather_start(list(early.values()), "gather_start_first")
    z = gone[0, 0]
    late = {
        ("w_gu", 0): [bslot(w_gate_up[0], z)], ("w_down", 0): [bslot(w_down[0], z)],
        ("w_in", 1): [bslot(w_in_b[0], z)], ("w_mem", 1): [bslot(w_mem_kv[1], z), own_slot(conv_qkv_b[0] + z)],
        ("w_out", 1): [bslot(w_out[1], z)], ("w_gu", 1): [bslot(w_gate_up[1], z)], ("w_down", 1): [bslot(w_down[1], z)],
    }
    flights_b, started_all = gather_start(list(late.values()), "gather_start_rest")
    flights = dict(zip(list(early) + list(late), flights_a + flights_b))
    P = {"rel_bias": rel_bias, "sinks": sinks_a[0], "a_log": a_log_b[0], "dt_bias": dt_bias_b[0],
         "out_norm_g": out_norm_g_b[0], "g_mix": norm_mix_g, "g_mem": norm_mem_g, "g_ffn": norm_ffn_g,
         "g_fin": final_norm_g, "ffn_cb": [ffn_conv_b[0], ffn_conv_b[1]], "w_mem": [None, None], "w_out": [None, None],
         "w_gu": [None, None], "w_down": [None, None], "ffn_cw": [None, None]}

    def rows4(g):
        return g.reshape(4 * g.shape[1], g.shape[2])

    def arrive(key, after):
        if key not in flights:
            return
        got = gather_wait(flights.pop(key), started_all if key == ("w_in", 0) else after, "gather_wait_%s%d" % key)
        name, i = key
        if name == "w_in":
            P["w_in_a" if i == 0 else "w_in_b"] = (_lay_in_a if i == 0 else _lay_in_b)(_unchip_cols(got[0]))
        elif name == "w_mem":
            P["w_mem"][i] = rows4(got[0])
            if i == 0:
                cw = _unchip_cols(got[1]).reshape(2, 3, D_FF)
                P["ffn_cw"] = [cw[0], cw[1]]
            else:
                P["conv_qkv"] = _unchip_cols(got[1])
        elif name == "w_out":
            P["w_out"][i] = _lay_out_a(rows4(got[0])) if i == 0 else rows4(got[0])
        elif name == "w_gu":
            P["w_gu"][i] = got[0]
        else:
            P["w_down"][i] = rows4(got[0])

    def chip_rows(g):
        return g.reshape(4, g.shape[0] // 4, g.shape[-1])

    sent, started, pending = {}, [], []

    def finish(after):
        key, names, flight = pending.pop()
        tag = "%s%d" % key
        partial, theirs = pair_wait(flight, after, "pair_wait_" + tag)
        pair = [pair_sum(p, t, core, "pair_sum_%s%d" % (nm, key[1])) for p, t, nm in zip(partial, theirs, names)]
        flight, token = scatter_start(pair, "scatter_start_" + tag)
        sent[key] = (names, flight, token)
        started.append(token[0, 0])

    def ready(key, G, dep):
        kind, i = key
        if kind == "tick":
            finish(G["w_out"][i])
        else:
            if kind == "ffn":
                if pending:
                    finish(G["w_gu"][i])
                names, partial = ("gu", "down"), [G["w_gu"][i], chip_rows(G["w_down"][i]).astype(bf16)]
            else:
                g_out = _unlay_out_a(G["w_out"][0]) if i == 0 else G["w_out"][1]
                g_in = _unlay_in_a(G["w_in_a"]) if i == 0 else _unlay_in_b(G["w_in_b"])
                names = ("out", "in", "mem")
                partial = [chip_rows(g_out).astype(bf16), _chip_cols(g_in).astype(bf16),
                           chip_rows(G["w_mem"][i]).astype(bf16)]
            flight, token = pair_start(partial, "pair_start_%s%d" % key)
            pending.append((key, names, flight))
            started.append(token[0, 0])
        if dep is not None:
            while started:
                dep = dep + started.pop()
        return dep

    P["arrive"], P["ready"] = arrive, ready

    loss, dx, G = _local_step(x[0], mem[0], loss_target[0], P)
    gfull = _grads_to_ref(G)
    finish(dx)

    after, halves = sent["mix", 0][2], []
    for key in (("ffn", 1), ("mix", 1), ("ffn", 0), ("mix", 0)):
        names, flight, _ = sent[key]
        pair, arrived = scatter_wait(flight, after, "scatter_wait_%s%d" % key)
        fins = [sum_slots(p, r, chip, core, "sum_slots_%s%d" % (nm, key[1])) for nm, p, r in zip(names, pair, arrived)]
        flight, after = final_start(fins, "final_start_%s%d" % key)
        halves.append(([(nm, key[1]) for nm in names], flight, key))
    done = {}
    for ids, flight, key in halves:
        done.update(zip(ids, final_wait(flight, after, "final_wait_%s%d" % key)))

    sm_shapes = [A[n].shape for n in SMALL] + [gfull[n].shape for n, _ in CONV] + [(LANE,)]
    sm_rows = _pad_to(sum(_nrows(s, LANE) for s in sm_shapes), 8)
    sbuf = _pack([gfull[n] for n in SMALL] + [gfull[n] for n, _ in CONV] + [loss[0]], LANE, sm_rows, f32)
    tot = _unpack(allreduce_small(sbuf), sm_shapes, LANE)
    gsmall = dict(zip(SMALL, tot[:len(SMALL)]))
    for (n, axis), t in zip(CONV, tot[len(SMALL):len(SMALL) + len(CONV)]):
        sh = A[n].shape[axis]
        gsmall[n] = lax.dynamic_slice_in_dim(t, chip * sh, sh, axis)
    loss_out = tot[-1][0]

    out = {}
    plan = (("w_gate_up", [done["gu", 0], done["gu", 1]]), ("w_down", [done["down", 0], done["down", 1]]),
            ("w_out", [done["out", 0], done["out", 1]]), ("w_mem_kv", [done["mem", 0], done["mem", 1]]),
            ("w_in_a", [done["in", 0]]), ("w_in_b", [done["in", 1]]))
    for n, gs in plan:
        shape3 = (len(gs),) + gs[0].shape
        res = adamw_big(A[n].reshape(shape3), A["m_" + n].reshape(shape3), A["v_" + n].reshape(shape3), gs, 0,
                        "adamw_" + n)
        for key, r in zip(("grad_", "delta_", "new_m_", "new_v_"), res):
            out[key + n] = r.reshape(A[n].shape)
    names = SMALL + tuple(n for n, _ in CONV)
    shapes = [A[n].shape for n in names]
    rows = _pad_to(sum(_nrows(s, LANE) for s in shapes), 8)
    packs = [_pack([src[n] for n in names], LANE, rows, f32)
             for src in ({n: A[n] for n in names}, {n: A["m_" + n] for n in names}, {n: A["v_" + n] for n in names}, gsmall)]
    res = adamw_small(*packs)
    for key, r in zip(("delta_", "new_m_", "new_v_"), res):
        for n, a in zip(names, _unpack(r, shapes, LANE)):
            out[key + n] = a
    for n in names:
        out["grad_" + n] = gsmall[n]
    return (loss_out, dx[None], *[out["grad_" + n] for n in WEIGHTS], *[out["delta_" + n] for n in WEIGHTS],
            *[out["new_m_" + n] for n in WEIGHTS], *[out["new_v_" + n] for n in WEIGHTS])
```

```python
import functools
import math

import numpy as np
import jax
import jax.numpy as jnp
from jax import lax
from jax.experimental import pallas as pl
from jax.experimental.pallas import tpu as pltpu

f32 = jnp.float32
bf16 = jnp.bfloat16
HI = lax.Precision.HIGHEST
MESH = pl.DeviceIdType.MESH

D = 1024
MEM_LEN = 256
EPS = 1e-6
A_HEADS, A_KV, A_DH = 12, 2, 64
A_Q = 768
BLK = 128
N_BUCKETS, MAX_DIST = 32, 128
B_QK, B_V, B_DH = 384, 768, 128
B_QKV = 1536
CHUNK = 64
X_Q = 256
D_FF = 2816
LANE = 128
VMEM_LIMIT = 56 * 1024 * 1024
MM_ROWS = 1024

LR, B1, B2, AEPS, WD, STEP = 0.001, 0.9, 0.999, 1e-08, 0.01, 10


def _cp(sem=None):
    return pltpu.CompilerParams(dimension_semantics=sem, vmem_limit_bytes=VMEM_LIMIT)


def _dg(a, b, ca, cb, prec=None):
    return lax.dot_general(a, b, (((ca,), (cb,)), ((), ())), precision=prec, preferred_element_type=f32)


@jax.custom_vjp
def bdot(a, b):
    return _dg(a.astype(bf16), b.astype(bf16), 1, 0)


def _bdot_f(a, b):
    return bdot(a, b), (a, b)


def _bdot_b(res, g):
    a, b = res
    gb = g.astype(bf16)
    return _dg(gb, b.astype(bf16), 1, 1), _dg(a.astype(bf16), gb, 0, 0)


bdot.defvjp(_bdot_f, _bdot_b)


@jax.custom_vjp
def bdot_nt(a, b):
    return _dg(a.astype(bf16), b.astype(bf16), 1, 1)


def _bdot_nt_f(a, b):
    return bdot_nt(a, b), (a, b)


def _bdot_nt_b(res, g):
    a, b = res
    gb = g.astype(bf16)
    return _dg(gb, b.astype(bf16), 1, 0), _dg(gb, a.astype(bf16), 0, 0)


bdot_nt.defvjp(_bdot_nt_f, _bdot_nt_b)


def _shift_rows(x, s, down):
    n = x.shape[0]
    row = lax.broadcasted_iota(jnp.int32, x.shape, 0)
    if down:
        return jnp.where(row >= s, pltpu.roll(x, s, 0), 0.0)
    return jnp.where(row < n - s, pltpu.roll(x, n - s, 0), 0.0)


@functools.partial(jax.custom_vjp, nondiff_argnums=(1,))
def shift_down(x, s):
    return _shift_rows(x, s, True)


def _sd_f(x, s):
    return _shift_rows(x, s, True), None


def _sd_b(s, _, g):
    return (_shift_rows(g, s, False),)


shift_down.defvjp(_sd_f, _sd_b)


def _sigmoid(x):
    return 1.0 / (1.0 + jnp.exp(-x))


def _silu(x):
    return x * _sigmoid(x)


def _rms(x, g):
    return x * lax.rsqrt(jnp.mean(x * x, axis=-1, keepdims=True) + EPS) * g


def _tile(n, cap):
    u = n // LANE
    best = 1
    for d in range(1, u + 1):
        if u % d == 0 and d * LANE <= cap:
            best = d
    return best * LANE


def mm_nn(a, w, res=None, out_dtype=f32, name="mm_nn"):
    M, K = a.shape
    N = w.shape[1]
    tm, tn = min(MM_ROWS, M), _tile(N, 1024)

    def body(*refs):
        if res is None:
            a_ref, w_ref, o_ref = refs
            o_ref[...] = _dg(a_ref[...].astype(bf16), w_ref[...], 1, 0).astype(out_dtype)
        else:
            a_ref, w_ref, r_ref, o_ref = refs
            o_ref[...] = (r_ref[...] + _dg(a_ref[...].astype(bf16), w_ref[...], 1, 0)).astype(out_dtype)

    in_specs = [pl.BlockSpec((tm, K), lambda n, m: (m, 0)), pl.BlockSpec((K, tn), lambda n, m: (0, n))]
    args = [a, w]
    if res is not None:
        in_specs.append(pl.BlockSpec((tm, tn), lambda n, m: (m, n)))
        args.append(res)
    return pl.pallas_call(
        body, name=name, grid=(N // tn, M // tm), in_specs=in_specs,
        out_specs=pl.BlockSpec((tm, tn), lambda n, m: (m, n)),
        out_shape=jax.ShapeDtypeStruct((M, N), out_dtype),
        compiler_params=_cp(("parallel", "parallel")),
    )(*args)


def mm_res_norm(a, w, res, g, name):
    pieces = a if isinstance(a, tuple) else (a,)
    na = len(pieces)
    M, K = pieces[0].shape[0], sum(p.shape[1] for p in pieces)
    tm = min(MM_ROWS, M)

    def body(*refs):
        w_ref, r_ref, g_ref, o_ref, n_ref = refs[na:]
        h = r_ref[...] + _dg(_cols(refs[:na]).astype(bf16), w_ref[...], 1, 0)
        o_ref[...] = h
        n_ref[...] = _rms(h, g_ref[...]).astype(bf16)

    tok = pl.BlockSpec((tm, D), lambda m: (m, 0))
    return pl.pallas_call(
        body, name=name, grid=(M // tm,),
        in_specs=[pl.BlockSpec((tm, p.shape[1]), lambda m: (m, 0)) for p in pieces]
        + [pl.BlockSpec((K, D), lambda m: (0, 0)), tok, pl.BlockSpec((1, D), lambda m: (0, 0))],
        out_specs=[tok, tok],
        out_shape=[jax.ShapeDtypeStruct((M, D), f32), jax.ShapeDtypeStruct((M, D), bf16)],
        compiler_params=_cp(("parallel",)),
    )(*pieces, w, res, g.reshape(1, D))


def mm_nt(dy, w, out_dtype=f32, name="mm_nt"):
    M, N = dy.shape
    K = w.shape[0]
    tm, tn = min(MM_ROWS, M), _tile(N, 1024)
    assert out_dtype == f32 or tn == N

    def body(dy_ref, w_ref, o_ref):
        part = _dg(dy_ref[...].astype(bf16), w_ref[...], 1, 1)
        if tn == N:
            o_ref[...] = part.astype(out_dtype)
        else:
            @pl.when(pl.program_id(1) == 0)
            def _():
                o_ref[...] = jnp.zeros_like(o_ref)
            o_ref[...] += part

    return pl.pallas_call(
        body, name=name, grid=(M // tm, N // tn),
        in_specs=[pl.BlockSpec((tm, tn), lambda m, n: (m, n)), pl.BlockSpec((K, tn), lambda m, n: (0, n))],
        out_specs=pl.BlockSpec((tm, K), lambda m, n: (m, 0)),
        out_shape=jax.ShapeDtypeStruct((M, K), out_dtype),
        compiler_params=_cp(("parallel", "arbitrary")),
    )(dy, w)


NORM_ROWS = 1024


def _acc_then_norm_bwd(part, steps, h_ref, g_ref, r_ref, o_ref, dg_ref):
    k = pl.program_id(1)

    @pl.when((pl.program_id(0) == 0) & (k == 0))
    def _():
        dg_ref[...] = jnp.zeros_like(dg_ref)

    @pl.when(k == 0)
    def _():
        o_ref[...] = part

    @pl.when(k > 0)
    def _():
        o_ref[...] += part

    @pl.when(k == steps - 1)
    def _():
        _, vjp = jax.vjp(_rms, h_ref[...], g_ref[...])
        dh, dg = vjp(o_ref[...])
        o_ref[...] = r_ref[...] + dh
        dg_ref[...] += dg


def mm_nt_norm(dy, w, norm, name):
    pieces = dy if isinstance(dy, tuple) else (dy,)
    nd = len(pieces)
    M, N = pieces[0].shape[0], sum(p.shape[1] for p in pieces)
    tm, tn = (min(NORM_ROWS, M), _tile(N, 1024)) if nd == 1 else (min(512, M), N)

    def body(*refs):
        w_ref, h_ref, g_ref, r_ref, o_ref, dg_ref = refs[nd:]
        _acc_then_norm_bwd(_dg(_cols(refs[:nd]).astype(bf16), w_ref[...], 1, 1), N // tn, h_ref, g_ref, r_ref, o_ref,
                           dg_ref)

    tok = pl.BlockSpec((tm, D), lambda m, n: (m, 0))
    vec = pl.BlockSpec((1, D), lambda m, n: (0, 0))
    return pl.pallas_call(
        body, name=name, grid=(M // tm, N // tn),
        in_specs=[pl.BlockSpec((tm, tn if nd == 1 else p.shape[1]), lambda m, n: (m, n)) for p in pieces]
        + [pl.BlockSpec((D, tn), lambda m, n: (0, n)), tok, vec, tok],
        out_specs=[tok, vec],
        out_shape=[jax.ShapeDtypeStruct((M, D), f32), jax.ShapeDtypeStruct((1, D), f32)],
        compiler_params=_cp(("arbitrary", "arbitrary")),
    )(*pieces, w, norm[0], norm[1].reshape(1, D), norm[2])


def _cols(refs):
    return refs[0][...] if len(refs) == 1 else jnp.concatenate([r[...] for r in refs], axis=1)


def mm_tn(a, dy, name="mm_tn"):
    pieces = a if isinstance(a, tuple) else (a,)
    dpieces = dy if isinstance(dy, tuple) else (dy,)
    na, nd = len(pieces), len(dpieces)
    M, K = pieces[0].shape[0], sum(p.shape[1] for p in pieces)
    N = sum(p.shape[1] for p in dpieces)
    tm = min(MM_ROWS, M)
    tk = _tile(K, 1408) if na == 1 else K
    tn = _tile(N, 1024) if nd == 1 else N

    def body(*refs):
        o_ref = refs[-1]

        @pl.when(pl.program_id(2) == 0)
        def _():
            o_ref[...] = jnp.zeros_like(o_ref)
        o_ref[...] += _dg(_cols(refs[:na]).astype(bf16), _cols(refs[na:na + nd]).astype(bf16), 0, 0)

    a_specs = [pl.BlockSpec((tm, tk if na == 1 else p.shape[1]), lambda k, n, m: (m, k)) for p in pieces]
    d_specs = [pl.BlockSpec((tm, tn if nd == 1 else p.shape[1]), lambda k, n, m: (m, n)) for p in dpieces]
    return pl.pallas_call(
        body, name=name, grid=(K // tk, N // tn, M // tm), in_specs=a_specs + d_specs,
        out_specs=pl.BlockSpec((tk, tn), lambda k, n, m: (k, n)),
        out_shape=jax.ShapeDtypeStruct((K, N), f32),
        compiler_params=_cp(("parallel", "parallel", "arbitrary")),
    )(*pieces, *dpieces)


def rms_fwd(h, g, name):
    S = h.shape[0]
    t = min(512, S)

    def body(h_ref, g_ref, o_ref):
        o_ref[...] = _rms(h_ref[...], g_ref[...]).astype(bf16)

    return pl.pallas_call(
        body, name=name, grid=(S // t,),
        in_specs=[pl.BlockSpec((t, D), lambda i: (i, 0)), pl.BlockSpec((1, D), lambda i: (0, 0))],
        out_specs=pl.BlockSpec((t, D), lambda i: (i, 0)),
        out_shape=jax.ShapeDtypeStruct((S, D), bf16),
        compiler_params=_cp(("parallel",)),
    )(h, g.reshape(1, D))


def loss_head(h, g, target):
    S = h.shape[0]
    t = min(512, S)

    def f(hh, gg, tt):
        err = _rms(hh, gg) - tt
        return 0.5 * jnp.sum(jnp.mean(err * err, axis=-1, keepdims=True), axis=0, keepdims=True)

    def body(h_ref, g_ref, t_ref, loss_ref, dh_ref, dg_ref):
        @pl.when(pl.program_id(0) == 0)
        def _():
            dg_ref[...] = jnp.zeros_like(dg_ref)
            loss_ref[...] = jnp.zeros_like(loss_ref)
        val, vjp = jax.vjp(lambda a, b: f(a, b, t_ref[...]), h_ref[...], g_ref[...])
        dh, dg = vjp(jnp.ones((1, 1), f32))
        dh_ref[...] = dh
        dg_ref[...] += dg
        loss_ref[...] += jnp.broadcast_to(val, loss_ref.shape)

    tok = pl.BlockSpec((t, D), lambda i: (i, 0))
    vec = pl.BlockSpec((1, D), lambda i: (0, 0))
    return pl.pallas_call(
        body, name="loss_head", grid=(S // t,), in_specs=[tok, vec, tok],
        out_specs=[pl.BlockSpec((1, LANE), lambda i: (0, 0)), tok, vec],
        out_shape=[jax.ShapeDtypeStruct((1, LANE), f32), jax.ShapeDtypeStruct((S, D), f32),
                   jax.ShapeDtypeStruct((1, D), f32)],
        compiler_params=_cp(("arbitrary",)),
    )(h, g.reshape(1, D), target)


def memkv_fwd(mem, g, w, name):
    def body(m_ref, g_ref, w_ref, o_ref):
        o_ref[...] = _dg(_rms(m_ref[...], g_ref[...]).astype(bf16), w_ref[...], 1, 0)

    return pl.pallas_call(
        body, name=name, out_shape=jax.ShapeDtypeStruct((MEM_LEN, 2 * X_Q), f32), compiler_params=_cp(),
    )(mem, g.reshape(1, D), w)


def memkv_bwd(mem, g, w, dkv, name):
    def body(m_ref, g_ref, w_ref, d_ref, dg_ref, dw_ref):
        n, vjp = jax.vjp(lambda gg: _rms(m_ref[...], gg), g_ref[...])
        db = d_ref[...].astype(bf16)
        dw_ref[...] = _dg(n.astype(bf16), db, 0, 0)
        dg_ref[...] = vjp(_dg(db, w_ref[...], 1, 1))[0]

    return pl.pallas_call(
        body, name=name,
        out_shape=[jax.ShapeDtypeStruct((1, D), f32), jax.ShapeDtypeStruct((D, 2 * X_Q), f32)],
        compiler_params=_cp(),
    )(mem, g.reshape(1, D), w, dkv)


def _xattn_f(xq, mk, mv):
    lane = lax.broadcasted_iota(jnp.int32, (1, X_Q), 1)
    out = jnp.zeros(xq.shape, f32)
    for hd in range(4):
        msk = (lane // 64 == hd).astype(f32)
        s = bdot_nt(xq * msk, mk) * (64 ** -0.5)
        m = lax.stop_gradient(jnp.max(s, axis=-1, keepdims=True))
        p = jnp.exp(s - m)
        p = p / jnp.sum(p, axis=-1, keepdims=True)
        out = out + bdot(p, mv * msk)
    return out


def xattn_fwd(proj, col, kv, name):
    S = proj.shape[0]
    t = min(512, S)
    cb = col // X_Q

    def body(q_ref, k_ref, v_ref, o_ref):
        o_ref[...] = _xattn_f(q_ref[...].astype(f32), k_ref[...], v_ref[...]).astype(bf16)

    return pl.pallas_call(
        body, name=name, grid=(S // t,),
        in_specs=[pl.BlockSpec((t, X_Q), lambda i: (i, cb)), pl.BlockSpec((MEM_LEN, X_Q), lambda i: (0, 0)),
                  pl.BlockSpec((MEM_LEN, X_Q), lambda i: (0, 1))],
        out_specs=pl.BlockSpec((t, X_Q), lambda i: (i, 0)),
        out_shape=jax.ShapeDtypeStruct((S, X_Q), bf16),
        compiler_params=_cp(("parallel",)),
    )(proj, kv, kv)


def xattn_bwd(proj, col, kv, dmix, name):
    S = proj.shape[0]
    t = min(512, S)
    cb = col // X_Q

    def body(q_ref, k_ref, v_ref, do_ref, dq_ref, dk_ref, dv_ref):
        @pl.when(pl.program_id(0) == 0)
        def _():
            dk_ref[...] = jnp.zeros_like(dk_ref)
            dv_ref[...] = jnp.zeros_like(dv_ref)
        _, vjp = jax.vjp(_xattn_f, q_ref[...].astype(f32), k_ref[...], v_ref[...])
        dq, dk, dv = vjp(do_ref[...])
        dq_ref[...] = dq.astype(bf16)
        dk_ref[...] += dk
        dv_ref[...] += dv

    kvb = pl.BlockSpec((MEM_LEN, X_Q), lambda i: (0, 0))
    dq, dk, dv = pl.pallas_call(
        body, name=name, grid=(S // t,),
        in_specs=[pl.BlockSpec((t, X_Q), lambda i: (i, cb)), kvb,
                  pl.BlockSpec((MEM_LEN, X_Q), lambda i: (0, 1)), pl.BlockSpec((t, X_Q), lambda i: (i, 3))],
        out_specs=[pl.BlockSpec((t, X_Q), lambda i: (i, 0)), kvb, kvb],
        out_shape=[jax.ShapeDtypeStruct((S, X_Q), bf16), jax.ShapeDtypeStruct((MEM_LEN, X_Q), f32),
                   jax.ShapeDtypeStruct((MEM_LEN, X_Q), f32)],
        compiler_params=_cp(("arbitrary",)),
    )(proj, kv, kv, dmix)
    return dq, jnp.concatenate([dk, dv], axis=1)


def _bucket_map():
    qi = np.arange(BLK)[:, None]
    kj = np.arange(2 * BLK)[None, :]
    n = np.maximum(BLK + qi - kj, 0)
    max_exact = N_BUCKETS // 2
    nf = np.maximum(n, 1).astype(np.float64)
    large = max_exact + (np.log(nf / max_exact) / math.log(MAX_DIST / max_exact)
                         * (N_BUCKETS - max_exact)).astype(np.int32)
    large = np.minimum(large, N_BUCKETS - 1)
    return np.where(n < max_exact, n, large).astype(np.int32)


def bias_build(rel_bias):
    def body(rb_ref, bk_ref, o_ref):
        bk = bk_ref[...]
        for h in range(A_HEADS):
            acc = jnp.zeros((BLK, 2 * BLK), f32)
            for b in range(N_BUCKETS):
                acc = jnp.where(bk == b, rb_ref[b, h], acc)
            o_ref[h] = acc

    return pl.pallas_call(
        body, name="bias_build",
        in_specs=[pl.BlockSpec(memory_space=pltpu.SMEM), pl.BlockSpec(memory_space=pltpu.VMEM)],
        out_specs=pl.BlockSpec(memory_space=pltpu.VMEM),
        out_shape=jax.ShapeDtypeStruct((A_HEADS, BLK, 2 * BLK), f32), compiler_params=_cp(),
    )(rel_bias, jnp.asarray(_bucket_map()))


def bias_grad(dbias):
    def body(d_ref, bk_ref, o_ref):
        bk = bk_ref[...]
        row = lax.broadcasted_iota(jnp.int32, (N_BUCKETS, LANE), 0)
        lane = lax.broadcasted_iota(jnp.int32, (N_BUCKETS, LANE), 1)
        acc = jnp.zeros((N_BUCKETS, LANE), f32)
        for h in range(A_HEADS):
            d = d_ref[h]
            for b in range(N_BUCKETS):
                s = jnp.sum(jnp.where(bk == b, d, 0.0), keepdims=True)
                acc = acc + jnp.where((row == b) & (lane == h), s, 0.0)
        o_ref[...] = acc

    return pl.pallas_call(
        body, name="bias_grad", out_shape=jax.ShapeDtypeStruct((N_BUCKETS, LANE), f32), compiler_params=_cp(),
    )(dbias, jnp.asarray(_bucket_map()))


def _swa_f(qb, kp, kc, vp, vc, bias, sk, first):
    kband = jnp.concatenate([kp, kc], axis=0)
    vband = jnp.concatenate([vp, vc], axis=0)
    qi = lax.broadcasted_iota(jnp.int32, (BLK, 2 * BLK), 0)
    kj = lax.broadcasted_iota(jnp.int32, (BLK, 2 * BLK), 1)
    rel = kj - qi
    ok = (rel >= 1) & (rel <= BLK) & ((kj >= BLK) | jnp.logical_not(first))
    lane = lax.broadcasted_iota(jnp.int32, (1, LANE), 1)
    lane_b = lax.broadcasted_iota(jnp.int32, (BLK, LANE), 1)
    outs = []
    for p in range(A_HEADS // 2):
        qp = qb[:, LANE * p:LANE * (p + 1)]
        acc = jnp.zeros((BLK, LANE), f32)
        for g in range(2):
            h = g * (A_HEADS // 2) + p
            msk = (lane // A_DH == g).astype(f32)
            s = bdot_nt(qp * msk, kband) * (A_DH ** -0.5) + bias[h]
            s = jnp.where(ok, s, -1e30)
            skb = jnp.broadcast_to(sk[h:h + 1, :], (BLK, LANE))
            sink = jnp.sum(jnp.where(lane_b == 0, skb, 0.0), axis=-1, keepdims=True)
            m = lax.stop_gradient(jnp.maximum(jnp.max(s, axis=-1, keepdims=True), sink))
            e = jnp.exp(s - m)
            prob = e / (jnp.sum(e, axis=-1, keepdims=True) + jnp.exp(sink - m))
            acc = acc + bdot(prob, vband) * msk
        outs.append(acc)
    return jnp.concatenate(outs, axis=1)


def _swa_specs(nb, rev):
    bi = (lambda i: nb - 1 - i) if rev else (lambda i: i)
    return [
        pl.BlockSpec((BLK, A_Q), lambda i: (bi(i), 0)),
        pl.BlockSpec((BLK, LANE), lambda i: (jnp.maximum(bi(i) - 1, 0), 6)),
        pl.BlockSpec((BLK, LANE), lambda i: (bi(i), 6)),
        pl.BlockSpec((BLK, LANE), lambda i: (jnp.maximum(bi(i) - 1, 0), 7)),
        pl.BlockSpec((BLK, LANE), lambda i: (bi(i), 7)),
        pl.BlockSpec((A_HEADS, BLK, 2 * BLK), lambda i: (0, 0, 0)),
        pl.BlockSpec((16, LANE), lambda i: (0, 0)),
    ]


def swa_fwd(proj, bias, sk):
    S = proj.shape[0]
    nb = S // BLK

    def body(q_ref, kp_ref, kc_ref, vp_ref, vc_ref, b_ref, s_ref, o_ref):
        qkv = [r[...].astype(f32) for r in (q_ref, kp_ref, kc_ref, vp_ref, vc_ref)]
        o_ref[...] = _swa_f(*qkv, b_ref[...], s_ref[...], pl.program_id(0) == 0).astype(bf16)

    return pl.pallas_call(
        body, name="swa_fwd", grid=(nb,), in_specs=_swa_specs(nb, False),
        out_specs=pl.BlockSpec((BLK, A_Q), lambda i: (i, 0)),
        out_shape=jax.ShapeDtypeStruct((S, A_Q), bf16), compiler_params=_cp(("parallel",)),
    )(proj, proj, proj, proj, proj, bias, sk)


def swa_bwd(proj, bias, sk, dmix):
    S = proj.shape[0]
    nb = S // BLK

    def body(q_ref, kp_ref, kc_ref, vp_ref, vc_ref, b_ref, s_ref, do_ref, dqkv_ref, db_ref, ds_ref, ck, cv):
        i = pl.program_id(0)

        @pl.when(i == 0)
        def _():
            db_ref[...] = jnp.zeros_like(db_ref)
            ds_ref[...] = jnp.zeros_like(ds_ref)
            ck[...] = jnp.zeros_like(ck)
            cv[...] = jnp.zeros_like(cv)
        first = i == nb - 1
        qkv = [r[...].astype(f32) for r in (q_ref, kp_ref, kc_ref, vp_ref, vc_ref)]
        _, vjp = jax.vjp(lambda *a: _swa_f(*a, first), *qkv, b_ref[...], s_ref[...])
        dq, dkp, dkc, dvp, dvc, db, ds = vjp(do_ref[...])
        dqkv_ref[...] = jnp.concatenate([dq, dkc + ck[...], dvc + cv[...]], axis=1).astype(bf16)
        ck[...] = dkp
        cv[...] = dvp
        db_ref[...] += db
        ds_ref[...] += ds

    return pl.pallas_call(
        body, name="swa_bwd", grid=(nb,),
        in_specs=_swa_specs(nb, True) + [pl.BlockSpec((BLK, A_Q), lambda i: (nb - 1 - i, 0))],
        out_specs=[pl.BlockSpec((BLK, D), lambda i: (nb - 1 - i, 0)),
                   pl.BlockSpec((A_HEADS, BLK, 2 * BLK), lambda i: (0, 0, 0)),
                   pl.BlockSpec((16, LANE), lambda i: (0, 0))],
        out_shape=[jax.ShapeDtypeStruct((S, D), bf16), jax.ShapeDtypeStruct((A_HEADS, BLK, 2 * BLK), f32),
                   jax.ShapeDtypeStruct((16, LANE), f32)],
        scratch_shapes=[pltpu.VMEM((BLK, LANE), f32), pltpu.VMEM((BLK, LANE), f32)],
        compiler_params=_cp(("arbitrary",)),
    )(proj, proj, proj, proj, proj, bias, sk, dmix)


def _dnprep_f(xext, w, is_qk):
    c = (w[3:4] * xext + w[2:3] * shift_down(xext, 1) + w[1:2] * shift_down(xext, 2) + w[0:1] * shift_down(xext, 3))
    a = _silu(c)[HALO:]
    n = a * lax.rsqrt(jnp.sum(a * a, axis=-1, keepdims=True) + EPS)
    return jnp.where(is_qk, n, a)


def dnprep_fwd(proj, cw):
    S = proj.shape[0]
    nblk = B_QKV // LANE
    T = S

    def body(x_ref, w_ref, o_ref):
        is_qk = pl.program_id(0) < 2 * B_QK // LANE
        wv = w_ref[...]

        def tile(r0, first):
            o_ref[pl.ds(r0, T), :] = _dnprep_f(_glu_gext(x_ref, r0, first, T), wv, is_qk)

        tile(0, True)

    return pl.pallas_call(
        body, name="dnprep_fwd", grid=(nblk,),
        in_specs=[pl.BlockSpec((S, LANE), lambda j: (0, j)), pl.BlockSpec((4, LANE), lambda j: (0, j))],
        out_specs=pl.BlockSpec((S, LANE), lambda j: (0, j)),
        out_shape=jax.ShapeDtypeStruct((S, B_QKV), f32), compiler_params=_cp(("parallel",)),
    )(proj, cw)


def dnprep_bwd(proj, cw, dqkvn):
    S = proj.shape[0]
    nblk = B_QKV // LANE

    T = S

    def body(x_ref, w_ref, d_ref, dx_ref, dw_ref):
        is_qk = pl.program_id(0) < 2 * B_QK // LANE
        wv = w_ref[...]

        def tile(r0, first):
            _, vjp = jax.vjp(lambda a, b: _dnprep_f(a, b, is_qk), _glu_gext(x_ref, r0, first, T), wv)
            dx, dw = vjp(d_ref[pl.ds(r0, T), :])
            dx_ref[pl.ds(r0, T), :] = dx[HALO:].astype(bf16)
            if not first:
                dx_ref[pl.ds(r0 - HALO, HALO), :] += dx[:HALO]
            return dw

        dw_ref[...] = tile(0, True)

    col = pl.BlockSpec((S, LANE), lambda j: (0, j))
    wsp = pl.BlockSpec((4, LANE), lambda j: (0, j))
    return pl.pallas_call(
        body, name="dnprep_bwd", grid=(nblk,), in_specs=[col, wsp, col], out_specs=[col, wsp],
        out_shape=[jax.ShapeDtypeStruct((S, B_QKV), bf16), jax.ShapeDtypeStruct((4, B_QKV), f32)],
        compiler_params=_cp(("parallel",)),
    )(proj, cw, dqkvn)


def _hdot(a, b, ca=1, cb=0):
    return _dg(a, b, ca, cb, HI)


def _bdg(a, b, ca, cb):
    dn = (((ca,), (cb,)), ((0,), (0,)))
    ah, bh = a.astype(bf16), b.astype(bf16)
    al, bl = (a - ah.astype(f32)).astype(bf16), (b - bh.astype(f32)).astype(bf16)
    return (lax.dot_general(ah, bh, dn, preferred_element_type=f32)
            + lax.dot_general(ah, bl, dn, preferred_element_type=f32)
            + lax.dot_general(al, bh, dn, preferred_element_type=f32))


@jax.custom_vjp
def hbd(a, b):
    return _bdg(a, b, 2, 1)


@jax.custom_vjp
def hbd_nt(a, b):
    return _bdg(a, b, 2, 2)


@jax.custom_vjp
def hbd_tn(a, b):
    return _bdg(a, b, 1, 1)


hbd.defvjp(lambda a, b: (hbd(a, b), (a, b)), lambda r, g: (hbd_nt(g, r[1]), hbd_tn(r[0], g)))
hbd_nt.defvjp(lambda a, b: (hbd_nt(a, b), (a, b)), lambda r, g: (hbd(g, r[1]), hbd_tn(g, r[0])))
hbd_tn.defvjp(lambda a, b: (hbd_tn(a, b), (a, b)), lambda r, g: (hbd_nt(r[1], g), hbd(r[0], g)))


def _stack(xs):
    return jnp.concatenate([x[None] for x in xs], axis=0)


def _lane_col(x, j):
    lane = lax.broadcasted_iota(jnp.int32, (1, LANE), 1)
    return jnp.sum(jnp.where(lane == j, x, 0.0), axis=-1, keepdims=True)


def _tri_inv(a_mat):
    r = lax.broadcasted_iota(jnp.int32, (1, CHUNK, CHUNK), 1)
    c = lax.broadcasted_iota(jnp.int32, (1, CHUNK, CHUNK), 2)
    pw = -a_mat
    inv = (r == c).astype(f32) + pw
    for _ in range(5):
        pw = hbd(pw, pw)
        inv = inv + hbd(inv, pw)
    return inv


@jax.custom_vjp
def _tri_inv_known(a_mat, inv):
    return inv


_tri_inv_known.defvjp(lambda a, inv: (inv, inv),
                      lambda inv, g: (-hbd_tn(inv, hbd_nt(g, inv)), jnp.zeros_like(inv)))


def _dnc_f(q, k, v, seg, prm, inverse=_tri_inv):
    C = CHUNK
    B = q.shape[0]
    rows = seg.shape[0]
    beta_all = _sigmoid(seg)
    xx = seg + prm[1:2]
    g_all = -jnp.exp(prm[0:1]) * (jnp.maximum(xx, 0.0) + jnp.log(1.0 + jnp.exp(-jnp.abs(xx))))
    r2 = lax.broadcasted_iota(jnp.int32, (rows, rows), 0)
    c2 = lax.broadcasted_iota(jnp.int32, (rows, rows), 1)
    within = (r2 >= c2) & (r2 // C == c2 // C)
    gc_all = _hdot(within.astype(f32), g_all)
    beta = _stack([_lane_col(beta_all[C * j:C * (j + 1)], h) for j in range(rows // C) for h in range(6)])
    gc = _stack([_lane_col(gc_all[C * j:C * (j + 1)], 6 + h) for j in range(rows // C) for h in range(6)])
    r = lax.broadcasted_iota(jnp.int32, (1, C, C), 1)
    c = lax.broadcasted_iota(jnp.int32, (1, C, C), 2)
    incl = r >= c
    strict = r > c
    gct = [gc_all[C * j:C * (j + 1)].T for j in range(rows // C)]
    g_row = _stack([jnp.broadcast_to(gct[j][6 + h:7 + h, :], (C, C))
                    for j in range(rows // C) for h in range(6)])
    decay = jnp.where(incl, jnp.exp(jnp.where(incl, gc - g_row, 0.0)), 0.0)
    a_mat = beta * sbd_nt(k, k) * jnp.where(strict, decay, 0.0)
    eg = jnp.exp(gc)
    inv = inverse(a_mat)
    u = hbd(inv, beta * v)
    w = hbd(inv, (beta * eg) * k)
    qc = q * (B_DH ** -0.5)
    attn = sbd_nt(qc, k) * decay
    last = (lax.broadcasted_iota(jnp.int32, (1, C, 1), 1) == C - 1).astype(f32)
    g_last = jnp.sum(gc * last, axis=1, keepdims=True)
    dc = jnp.broadcast_to(jnp.exp(g_last), (B, 1, LANE)).reshape(B, LANE)
    return u, w, qc * eg, k * jnp.exp(g_last - gc), attn, dc, inv


def _b1(a, b, ca, cb):
    return lax.dot_general(a.astype(bf16), b.astype(bf16), (((ca,), (cb,)), ((0,), (0,))), preferred_element_type=f32)


@jax.custom_vjp
def sbd(a, b):
    return _b1(a, b, 2, 1)


@jax.custom_vjp
def sbd_nt(a, b):
    return _b1(a, b, 2, 2)


@jax.custom_vjp
def sbd_tn(a, b):
    return _b1(a, b, 1, 1)


sbd.defvjp(lambda a, b: (sbd(a, b), (a, b)), lambda r, g: (sbd_nt(g, r[1]), sbd_tn(r[0], g)))
sbd_nt.defvjp(lambda a, b: (sbd_nt(a, b), (a, b)), lambda r, g: (sbd(g, r[1]), sbd_tn(g, r[0])))
sbd_tn.defvjp(lambda a, b: (sbd_tn(a, b), (a, b)), lambda r, g: (sbd_nt(r[1], g), sbd(r[0], g)))


def _dns_f(S0, u, w, qd, kt, attn, dcrows):
    dc = _lane_col(dcrows, 0).reshape(6, 1, 1)
    delta = u - sbd(w, S0)
    out = sbd(qd, S0) + sbd(attn, delta)
    return out, dc * S0 + sbd_tn(kt, delta)


def _dnpost_f(o, z, grow):
    outs = []
    for h in range(6):
        oh = o[:, LANE * h:LANE * (h + 1)]
        outs.append(oh * lax.rsqrt(jnp.mean(oh * oh, axis=-1, keepdims=True) + EPS) * grow
                    * _silu(z[:, LANE * h:LANE * (h + 1)]))
    return jnp.concatenate(outs, axis=1)


def _hs(h):
    return slice(LANE * h, LANE * (h + 1))


DN_CHUNKS = 4


def _heads(ref, share):
    return _stack([ref[CHUNK * j:CHUNK * (j + 1), _hs(h // share)]
                   for j in range(ref.shape[0] // CHUNK) for h in range(6)])


def _put_heads(ref, val):
    for j in range(ref.shape[0] // CHUNK):
        for h in range(6):
            ref[CHUNK * j:CHUNK * (j + 1), _hs(h)] = val[6 * j + h].astype(ref.dtype)


def _dnc_in_specs():
    rows = CHUNK * DN_CHUNKS
    return [
        pl.BlockSpec((rows, B_QK), lambda n: (n, 0)),
        pl.BlockSpec((rows, B_QK), lambda n: (n, 1)),
        pl.BlockSpec((rows, B_V), lambda n: (n, 1)),
        pl.BlockSpec((rows, LANE), lambda n: (n, 20)),
        pl.BlockSpec((8, LANE), lambda n: (0, 0)),
    ]


def _dnc_out_specs(rev_nc=None, chunks=1):
    ci = (lambda n: n) if rev_nc is None else (lambda n: rev_nc - 1 - n)
    wide = pl.BlockSpec((CHUNK * chunks, B_V), lambda n: (ci(n), 0))
    return [wide, wide, wide, wide, pl.BlockSpec((chunks, 6, CHUNK, CHUNK), lambda n: (ci(n), 0, 0, 0)),
            pl.BlockSpec((chunks, 8, LANE), lambda n: (ci(n), 0, 0))]


def _dc_rows(dc):
    pad = jnp.zeros((2, LANE), f32)
    return _stack([jnp.concatenate([dc[6 * j:6 * (j + 1)], pad], axis=0) for j in range(dc.shape[0] // 6)])


def _dnc_shapes(S, mm=f32):
    nc = S // CHUNK
    return [jax.ShapeDtypeStruct((S, B_V), f32)] + [jax.ShapeDtypeStruct((S, B_V), mm)] * 3 + [
        jax.ShapeDtypeStruct((nc, 6, CHUNK, CHUNK), mm), jax.ShapeDtypeStruct((nc, 8, LANE), f32)]


def dnc_fwd(qkvn, proj, prm):
    S = proj.shape[0]

    def body(q_ref, k_ref, v_ref, s_ref, p_ref, u_ref, w_ref, qd_ref, kt_ref, at_ref, dc_ref, inv_ref):
        u, w, qd, kt, attn, dc, inv = _dnc_f(_heads(q_ref, 2), _heads(k_ref, 2), _heads(v_ref, 1), s_ref[...],
                                             p_ref[...])
        inv_ref[...] = inv.reshape(inv_ref.shape)
        _put_heads(u_ref, u)
        _put_heads(w_ref, w)
        _put_heads(qd_ref, qd)
        _put_heads(kt_ref, kt)
        at_ref[...] = attn.reshape(at_ref.shape).astype(at_ref.dtype)
        dc_ref[...] = _dc_rows(dc)

    outs = _dnc_out_specs(chunks=DN_CHUNKS)
    out = pl.pallas_call(
        body, name="dn_chunk_fwd", grid=(S // (CHUNK * DN_CHUNKS),), in_specs=_dnc_in_specs(),
        out_specs=outs + [outs[4]], out_shape=_dnc_shapes(S, bf16) + [_dnc_shapes(S)[4]],
        compiler_params=_cp(("parallel",)),
    )(qkvn, qkvn, qkvn, proj, prm)
    return out[:6], out[6]


def dnc_bwd(qkvn, proj, prm, inv, cots):
    S = proj.shape[0]

    def body(q_ref, k_ref, v_ref, s_ref, p_ref, inv_ref, du_ref, dw_ref, dqd_ref, dkt_ref, dat_ref, ddc_ref,
             dx_ref, dseg_ref, dprm_ref):
        @pl.when(pl.program_id(0) == 0)
        def _():
            dprm_ref[...] = jnp.zeros_like(dprm_ref)
        nb = 6 * DN_CHUNKS
        known = functools.partial(_tri_inv_known, inv=inv_ref[...].reshape(nb, CHUNK, CHUNK))
        _, vjp = jax.vjp(lambda *a: _dnc_f(*a, inverse=known)[:6], _heads(q_ref, 2), _heads(k_ref, 2),
                         _heads(v_ref, 1), s_ref[...], p_ref[...])
        ddc = jnp.concatenate([ddc_ref[j, 0:6, :] for j in range(DN_CHUNKS)], axis=0)
        dq, dk, dv, dseg, dprm = vjp((_heads(du_ref, 1), _heads(dw_ref, 1), _heads(dqd_ref, 1), _heads(dkt_ref, 1),
                                      dat_ref[...].reshape(nb, CHUNK, CHUNK), ddc))
        for j in range(DN_CHUNKS):
            o = 6 * j
            dx_ref[CHUNK * j:CHUNK * (j + 1), :] = jnp.concatenate(
                [dq[o] + dq[o + 1], dq[o + 2] + dq[o + 3], dq[o + 4] + dq[o + 5],
                 dk[o] + dk[o + 1], dk[o + 2] + dk[o + 3], dk[o + 4] + dk[o + 5]] + [dv[o + h] for h in range(6)], axis=1)
        dseg_ref[...] = dseg.astype(bf16)
        dprm_ref[...] += dprm

    rows = CHUNK * DN_CHUNKS
    outs = _dnc_out_specs(chunks=DN_CHUNKS)
    return pl.pallas_call(
        body, name="dn_chunk_bwd", grid=(S // rows,),
        in_specs=_dnc_in_specs() + [outs[4]] + outs,
        out_specs=[pl.BlockSpec((rows, B_QKV), lambda n: (n, 0)), pl.BlockSpec((rows, LANE), lambda n: (n, 0)),
                   pl.BlockSpec((8, LANE), lambda n: (0, 0))],
        out_shape=[jax.ShapeDtypeStruct((S, B_QKV), f32), jax.ShapeDtypeStruct((S, LANE), bf16),
                   jax.ShapeDtypeStruct((8, LANE), f32)],
        compiler_params=_cp(("arbitrary",)),
    )(qkvn, qkvn, qkvn, proj, prm, inv, *cots)


def dns_fwd(chunked):
    u = chunked[0]
    S = u.shape[0]
    nc = S // CHUNK

    def body(u_ref, w_ref, qd_ref, kt_ref, at_ref, dc_ref, o_ref, st_ref, st):
        @pl.when(pl.program_id(0) == 0)
        def _():
            st[...] = jnp.zeros_like(st)
        S0 = st[...]
        st_ref[0] = S0
        out, S1 = _dns_f(S0, _heads(u_ref, 1), _heads(w_ref, 1), _heads(qd_ref, 1), _heads(kt_ref, 1),
                         at_ref[0], dc_ref[0, 0:6, :])
        _put_heads(o_ref, out)
        st[...] = S1

    return pl.pallas_call(
        body, name="dn_scan_fwd", grid=(nc,), in_specs=_dnc_out_specs(),
        out_specs=[pl.BlockSpec((CHUNK, B_V), lambda n: (n, 0)),
                   pl.BlockSpec((1, 6, B_DH, B_DH), lambda n: (n, 0, 0, 0))],
        out_shape=[jax.ShapeDtypeStruct((S, B_V), f32), jax.ShapeDtypeStruct((nc, 6, B_DH, B_DH), f32)],
        scratch_shapes=[pltpu.VMEM((6, B_DH, B_DH), f32)],
        compiler_params=_cp(("arbitrary",)),
    )(*chunked)


def dns_bwd(chunked, states, do):
    S = do.shape[0]
    nc = S // CHUNK

    def body(u_ref, w_ref, qd_ref, kt_ref, at_ref, dc_ref, st_ref, do_ref,
             du_ref, dw_ref, dqd_ref, dkt_ref, dat_ref, ddc_ref, dst):
        @pl.when(pl.program_id(0) == 0)
        def _():
            dst[...] = jnp.zeros_like(dst)
        _, vjp = jax.vjp(_dns_f, st_ref[0], _heads(u_ref, 1), _heads(w_ref, 1).astype(f32),
                         _heads(qd_ref, 1).astype(f32), _heads(kt_ref, 1).astype(f32), at_ref[0].astype(f32),
                         dc_ref[0, 0:6, :])
        dS0, du, dw, dqd, dkt, dat, ddc = vjp((_heads(do_ref, 1), dst[...]))
        dst[...] = dS0
        _put_heads(du_ref, du)
        _put_heads(dw_ref, dw)
        _put_heads(dqd_ref, dqd)
        _put_heads(dkt_ref, dkt)
        dat_ref[0] = dat
        ddc_ref[0] = jnp.concatenate([ddc, jnp.zeros((2, LANE), f32)], axis=0)

    return pl.pallas_call(
        body, name="dn_scan_bwd", grid=(nc,),
        in_specs=_dnc_out_specs(nc) + [pl.BlockSpec((1, 6, B_DH, B_DH), lambda n: (nc - 1 - n, 0, 0, 0)),
                                       pl.BlockSpec((CHUNK, B_V), lambda n: (nc - 1 - n, 0))],
        out_specs=_dnc_out_specs(nc), out_shape=_dnc_shapes(S),
        scratch_shapes=[pltpu.VMEM((6, B_DH, B_DH), f32)],
        compiler_params=_cp(("arbitrary",)),
    )(*chunked, states, do)


def dnpost_fwd(o, proj, prm):
    S = o.shape[0]
    t = min(512, S)

    def body(o_ref, z_ref, p_ref, y_ref):
        y_ref[...] = _dnpost_f(o_ref[...], z_ref[...], p_ref[2:3, :]).astype(bf16)

    tok = pl.BlockSpec((t, B_V), lambda i: (i, 0))
    return pl.pallas_call(
        body, name="dn_post_fwd", grid=(S // t,),
        in_specs=[tok, pl.BlockSpec((t, B_V), lambda i: (i, 2)), pl.BlockSpec((8, LANE), lambda i: (0, 0))],
        out_specs=tok, out_shape=jax.ShapeDtypeStruct((S, B_V), bf16), compiler_params=_cp(("parallel",)),
    )(o, proj, prm)


def dnpost_bwd(o, proj, prm, dmix):
    S = o.shape[0]
    t = min(512, S)

    def body(o_ref, z_ref, p_ref, dy_ref, do_ref, dz_ref, dg_ref):
        @pl.when(pl.program_id(0) == 0)
        def _():
            dg_ref[...] = jnp.zeros_like(dg_ref)
        _, vjp = jax.vjp(_dnpost_f, o_ref[...], z_ref[...], p_ref[2:3, :])
        do, dz, dg = vjp(dy_ref[...])
        do_ref[...] = do
        dz_ref[...] = dz.astype(bf16)
        dg_ref[...] += dg

    tok = pl.BlockSpec((t, B_V), lambda i: (i, 0))
    return pl.pallas_call(
        body, name="dn_post_bwd", grid=(S // t,),
        in_specs=[tok, pl.BlockSpec((t, B_V), lambda i: (i, 2)), pl.BlockSpec((8, LANE), lambda i: (0, 0)), tok],
        out_specs=[tok, tok, pl.BlockSpec((1, LANE), lambda i: (0, 0))],
        out_shape=[jax.ShapeDtypeStruct((S, B_V), f32), jax.ShapeDtypeStruct((S, B_V), bf16),
                   jax.ShapeDtypeStruct((1, LANE), f32)],
        compiler_params=_cp(("arbitrary",)),
    )(o, proj, prm, dmix)


N_FF_BLK = D_FF // LANE
GU_SHARD = 2 * D_FF // 4


GLU_ROWS = 128
HALO = 16


def _glu_conv(gext, w, b):
    return (w[2:3] * gext + w[1:2] * shift_down(gext, 1) + w[0:1] * shift_down(gext, 2) + b)[HALO:]


def _glu_gate(c, up):
    return _silu(c) * up


def _glu_gext(g_ref, r0, first, T=GLU_ROWS):
    if first:
        return jnp.concatenate([jnp.zeros((HALO, LANE), f32), g_ref[0:T, :].astype(f32)], axis=0)
    return g_ref[pl.ds(r0 - HALO, T + HALO), :].astype(f32)


def glu_fwd(gu, w, b, name):
    S = gu.shape[0]
    T = min(GLU_ROWS, S // 2)

    def body(g_ref, u_ref, w_ref, b_ref, o_ref, c_ref):
        wv, bv = w_ref[...], b_ref[...]

        def tile(r0, first):
            c = _glu_conv(_glu_gext(g_ref, r0, first, T), wv, bv)
            c_ref[pl.ds(r0, T), :] = c.astype(bf16)
            o_ref[pl.ds(r0, T), :] = _glu_gate(c, u_ref[pl.ds(r0, T), :].astype(f32)).astype(bf16)

        tile(0, True)

        @pl.loop(1, S // T)
        def _(t):
            tile(pl.multiple_of(t * T, T), False)

    col = pl.BlockSpec((S, LANE), lambda j: (0, j))
    return pl.pallas_call(
        body, name=name, grid=(N_FF_BLK,),
        in_specs=[col, pl.BlockSpec((S, LANE), lambda j: (0, N_FF_BLK + j)), pl.BlockSpec((3, LANE), lambda j: (0, j)),
                  pl.BlockSpec((1, LANE), lambda j: (0, j))],
        out_specs=[col, col], out_shape=[jax.ShapeDtypeStruct((S, D_FF), bf16)] * 2,
        compiler_params=_cp(("parallel",)),
    )(gu, gu, w, b.reshape(1, D_FF))


def glu_bwd(gu, c, w, b, dact, name):
    S = gu.shape[0]
    T = min(GLU_ROWS, S // 2)

    def body(g_ref, u_ref, c_ref, w_ref, b_ref, d_ref, dg_ref, dw_ref, db_ref, acc):
        wv, bv = w_ref[...], b_ref[...]

        def tile(r0, first):
            rows = pl.ds(r0, T)
            _, vjp_gate = jax.vjp(_glu_gate, c_ref[rows, :].astype(f32), u_ref[rows, :].astype(f32))
            dc, du = vjp_gate(d_ref[rows, :].astype(f32))
            _, vjp_conv = jax.vjp(_glu_conv, _glu_gext(g_ref, r0, first, T), wv, bv)
            dgx, dw, db = vjp_conv(dc)
            acc[pl.ds(r0, T), :] = dgx[HALO:]
            if not first:
                acc[pl.ds(r0 - HALO, HALO), :] += dgx[:HALO]
            dg_ref[1, pl.ds(r0, T), :] = du.astype(bf16)
            return dw, db

        dw0, db0 = tile(0, True)
        dw_ref[...] = dw0
        db_ref[...] = db0

        @pl.loop(1, S // T)
        def _(t):
            dw, db = tile(pl.multiple_of(t * T, T), False)
            dw_ref[...] += dw
            db_ref[...] += db

        dg_ref[0] = acc[...].astype(bf16)

    col = pl.BlockSpec((S, LANE), lambda j: (0, j))
    wsp = pl.BlockSpec((3, LANE), lambda j: (0, j))
    bsp = pl.BlockSpec((1, LANE), lambda j: (0, j))
    return pl.pallas_call(
        body, name=name, grid=(N_FF_BLK,),
        in_specs=[col, pl.BlockSpec((S, LANE), lambda j: (0, N_FF_BLK + j)), col, wsp, bsp, col],
        out_specs=[pl.BlockSpec((2, S, LANE), lambda j: (0, 0, j)), wsp, bsp],
        out_shape=[jax.ShapeDtypeStruct((2, S, D_FF), bf16), jax.ShapeDtypeStruct((3, D_FF), f32),
                   jax.ShapeDtypeStruct((1, D_FF), f32)],
        scratch_shapes=[pltpu.VMEM((S, LANE), f32)],
        compiler_params=_cp(("parallel",)),
    )(gu, gu, c, w, b.reshape(1, D_FF), dact)


def gu_fwd(n2, wg, name):
    S = n2.shape[0]
    tm = min(MM_ROWS, S)

    def body(a_ref, w_ref, o_ref):
        o_ref[...] = _dg(a_ref[...], w_ref[...], 1, 0).astype(bf16)

    return pl.pallas_call(
        body, name=name, grid=(4, S // tm),
        in_specs=[pl.BlockSpec((tm, D), lambda s, m: (m, 0)), pl.BlockSpec((None, D, GU_SHARD), lambda s, m: (s, 0, 0))],
        out_specs=pl.BlockSpec((tm, GU_SHARD), lambda s, m: (m, s)),
        out_shape=jax.ShapeDtypeStruct((S, 2 * D_FF), bf16), compiler_params=_cp(("parallel", "parallel")),
    )(n2, wg)


def gu_bwd_x(dgu, wg, norm, name):
    S = dgu.shape[1]
    tm = min(NORM_ROWS, S)

    def body(d_ref, w_ref, h_ref, g_ref, r_ref, o_ref, dg_ref):
        _acc_then_norm_bwd(_dg(d_ref[...], w_ref[...], 1, 1), 4, h_ref, g_ref, r_ref, o_ref, dg_ref)

    tok = pl.BlockSpec((tm, D), lambda m, s: (m, 0))
    vec = pl.BlockSpec((1, D), lambda m, s: (0, 0))
    return pl.pallas_call(
        body, name=name, grid=(S // tm, 4),
        in_specs=[pl.BlockSpec((None, tm, GU_SHARD), lambda m, s: (s // 2, m, s % 2)),
                  pl.BlockSpec((None, D, GU_SHARD), lambda m, s: (s, 0, 0)), tok, vec, tok],
        out_specs=[tok, vec],
        out_shape=[jax.ShapeDtypeStruct((S, D), f32), jax.ShapeDtypeStruct((1, D), f32)],
        compiler_params=_cp(("arbitrary", "arbitrary")),
    )(dgu, wg, norm[0], norm[1].reshape(1, D), norm[2])


def gu_bwd_w(n2, dgu, name):
    S = n2.shape[0]
    tm = min(MM_ROWS, S)
    nm = S // tm

    def body(a_ref, d_ref, o_ref, acc):
        @pl.when(pl.program_id(1) == 0)
        def _():
            acc[...] = jnp.zeros_like(acc)
        acc[...] += _dg(a_ref[...], d_ref[...], 0, 0)

        @pl.when(pl.program_id(1) == nm - 1)
        def _():
            o_ref[...] = acc[...].astype(bf16)

    return pl.pallas_call(
        body, name=name, grid=(4, nm),
        in_specs=[pl.BlockSpec((tm, D), lambda s, m: (m, 0)),
                  pl.BlockSpec((None, tm, GU_SHARD), lambda s, m: (s // 2, m, s % 2))],
        out_specs=pl.BlockSpec((None, D, GU_SHARD), lambda s, m: (s, 0, 0)),
        out_shape=jax.ShapeDtypeStruct((4, D, GU_SHARD), bf16),
        scratch_shapes=[pltpu.VMEM((D, GU_SHARD), f32)],
        compiler_params=_cp(("parallel", "arbitrary")),
    )(n2, dgu)


def _pair_cols(w):
    lead = w.shape[:-1]
    return w.reshape(lead + (2, 6, A_DH)).swapaxes(-3, -2).reshape(lead + (A_Q,))


def _unpair_cols(w):
    lead = w.shape[:-1]
    return w.reshape(lead + (6, 2, A_DH)).swapaxes(-3, -2).reshape(lead + (A_Q,))


def _lay_in_a(w):
    return jnp.concatenate([_pair_cols(w[:, :A_Q]), w[:, A_Q:]], axis=1)


def _unlay_in_a(w):
    return jnp.concatenate([_unpair_cols(w[:, :A_Q]), w[:, A_Q:]], axis=1)


def _lay_out_a(w):
    return jnp.concatenate([_pair_cols(w[:A_Q].T).T, w[A_Q:]], axis=0)


def _unlay_out_a(w):
    return jnp.concatenate([_unpair_cols(w[:A_Q].T).T, w[A_Q:]], axis=0)


def _lay_in_b(w):
    return jnp.concatenate([w[:, :2304], w[:, 2316:], w[:, 2304:2316],
                            jnp.zeros((w.shape[0], LANE - 12), w.dtype)], axis=1)


def _unlay_in_b(w):
    return jnp.concatenate([w[:, :2304], w[:, 2560:2572], w[:, 2304:2560]], axis=1)


def _chip_cols(w):
    return jnp.moveaxis(w.reshape(w.shape[0], 4, w.shape[1] // 4), 1, 0)


def _unchip_cols(w):
    return jnp.moveaxis(w, 0, 1).reshape(w.shape[1], 4 * w.shape[2])


def _local_step(x, mem, target, P):
    arrive = P.get("arrive", lambda key, after: None)
    ready = P.get("ready", lambda key, grads, dep: dep)
    sk = jnp.zeros((16, LANE), f32).at[:A_HEADS].set(jnp.broadcast_to(P["sinks"][:, None], (A_HEADS, LANE)))
    prm = jnp.zeros((8, LANE), f32).at[0, 6:12].set(P["a_log"]).at[1, 6:12].set(P["dt_bias"]).at[2].set(P["out_norm_g"])
    bias = bias_build(P["rel_bias"])
    saved = []
    h = x
    n1 = rms_fwd(h, P["g_mix"][0], "rms_mix0")
    for i in range(2):
        arrive(("w_in", i), n1)
        if i == 0:
            proj = mm_nn(n1, P["w_in_a"], out_dtype=bf16, name="proj_a")
        else:
            proj = mm_nn(n1, P["w_in_b"], name="proj_b")
        arrive(("w_mem", i), proj)
        kv = memkv_fwd(mem, P["g_mem"][i], P["w_mem"][i], f"memkv{i}")
        if i == 0:
            self_out = swa_fwd(proj, bias, sk)
            cross = xattn_fwd(proj, A_Q + 2 * LANE, kv, "xattn_a")
            extra = ()
        else:
            qkvn = dnprep_fwd(proj, P["conv_qkv"])
            chunked, inv = dnc_fwd(qkvn, proj, prm)
            o, states = dns_fwd(chunked)
            self_out = dnpost_fwd(o, proj, prm)
            cross = xattn_fwd(proj, 2304, kv, "xattn_b")
            extra = (qkvn, chunked, inv, states, o)
        mix = (self_out, cross)
        arrive(("w_out", i), cross)
        h2, n2 = mm_res_norm(mix, P["w_out"][i], h, P["g_ffn"][i], f"out_proj{i}")
        arrive(("w_gu", i), n2)
        gu = gu_fwd(n2, P["w_gu"][i], f"gate_up{i}")
        act, pre = glu_fwd(gu, P["ffn_cw"][i], P["ffn_cb"][i], f"glu{i}")
        arrive(("w_down", i), act)
        saved.append((h, n1, kv, proj, mix, h2, n2, gu, pre, act, extra))
        if i == 0:
            h, n1 = mm_res_norm(act, P["w_down"][i], h2, P["g_mix"][1], f"down{i}")
        else:
            h = mm_nn(act, P["w_down"][i], res=h2, name=f"down{i}")

    loss, dh, dg_fin = loss_head(h, P["g_fin"], target)
    G = {"g_fin": dg_fin[0], "g_mix": [None, None], "g_mem": [None, None], "g_ffn": [None, None],
         "w_mem": [None, None], "w_out": [None, None], "w_gu": [None, None], "w_down": [None, None],
         "ffn_cw": [None, None], "ffn_cb": [None, None]}
    for i in (1, 0):
        hin, n1, kv, proj, mix, h2, n2, gu, pre, act, extra = saved[i]
        dact = mm_nt(dh, P["w_down"][i], out_dtype=bf16, name=f"d_act{i}")
        G["w_down"][i] = mm_tn(act, dh, name=f"dw_down{i}")
        dgu, dcw, dcb = glu_bwd(gu, pre, P["ffn_cw"][i], P["ffn_cb"][i], dact, f"glu_bwd{i}")
        G["ffn_cw"][i], G["ffn_cb"][i] = dcw, dcb[0]
        G["w_gu"][i] = gu_bwd_w(n2, dgu, f"dw_gu{i}")
        g_ffn = ready(("ffn", i), G, P["g_ffn"][i])
        dh2, dg = gu_bwd_x(dgu, P["w_gu"][i], (h2, g_ffn, dh), f"d_n2_{i}")
        G["g_ffn"][i] = dg[0]
        dmix = mm_nt(dh2, P["w_out"][i], name=f"d_mix{i}")
        G["w_out"][i] = mm_tn(mix, dh2, name=f"dw_out{i}")
        g_mix = ready(("tick", i), G, P["g_mix"][i])
        if i == 0:
            dqkv, dbias, dsk = swa_bwd(proj, bias, sk, dmix)
            dxq, dkv = xattn_bwd(proj, A_Q + 2 * LANE, kv, dmix, "xattn_a_bwd")
            dproj = (dqkv, dxq)
            G["sinks"] = dsk[:A_HEADS, 0]
            G["rel_bias"] = bias_grad(dbias)[:, :A_HEADS]
            w_in, gname = P["w_in_a"], "w_in_a"
        else:
            qkvn, chunked, inv, states, o = extra
            do, dz, dgo = dnpost_bwd(o, proj, prm, dmix)
            dqkvn, dseg, dprm = dnc_bwd(qkvn, proj, prm, inv, dns_bwd(chunked, states, do))
            draw, dconv = dnprep_bwd(proj, P["conv_qkv"], dqkvn)
            dxq, dkv = xattn_bwd(proj, 2304, kv, dmix, "xattn_b_bwd")
            dproj = (draw, dz, dxq, dseg)
            G["conv_qkv"] = dconv
            G["a_log"], G["dt_bias"], G["out_norm_g"] = dprm[0, 6:12], dprm[1, 6:12], dgo[0]
            w_in, gname = P["w_in_b"], "w_in_b"
        G[gname] = mm_tn(n1, dproj, name=f"d{gname}")
        dh, dg = mm_nt_norm(dproj, w_in, (hin, g_mix, dh2), f"d_n1_{i}")
        G["g_mix"][i] = dg[0]
        dgm, dwm = memkv_bwd(mem, P["g_mem"][i], P["w_mem"][i], dkv, f"memkv_bwd{i}")
        G["g_mem"][i], G["w_mem"][i] = dgm[0], dwm
        ready(("mix", i), G, None)
    return loss, dh, G


def _grads_to_ref(G):
    return {
        "rel_bias": G["rel_bias"], "norm_mix_g": jnp.stack(G["g_mix"]), "norm_mem_g": jnp.stack(G["g_mem"]),
        "w_mem_kv": jnp.stack(G["w_mem"]),
        "w_out": jnp.stack([_unlay_out_a(G["w_out"][0]), G["w_out"][1]]),
        "w_in_a": _unlay_in_a(G["w_in_a"])[None], "sinks_a": G["sinks"][None],
        "w_in_b": _unlay_in_b(G["w_in_b"])[None], "conv_qkv_b": G["conv_qkv"][None],
        "a_log_b": G["a_log"][None], "dt_bias_b": G["dt_bias"][None], "out_norm_g_b": G["out_norm_g"][None],
        "norm_ffn_g": jnp.stack(G["g_ffn"]),
        "w_gate_up": jnp.stack([_unchip_cols(G["w_gu"][0]), _unchip_cols(G["w_gu"][1])]).astype(f32),
        "ffn_conv_w": jnp.stack(G["ffn_cw"]), "ffn_conv_b": jnp.stack(G["ffn_cb"]),
        "w_down": jnp.stack(G["w_down"]), "final_norm_g": G["g_fin"],
    }


ANY = pl.BlockSpec(memory_space=pl.ANY)


def _place():
    return lax.axis_index("x"), lax.axis_index("y"), lax.axis_index("c")


def allreduce_small(buf):
    R = buf.shape[0]

    def body(b_ref, o_ref, recv, ssem, rsem):
        x, y, c = _place()
        me = 4 * x + 2 * y + c

        def peer(k):
            return (1 - x if k & 4 else x, 1 - y if k & 2 else y, 1 - c if k & 1 else c)

        def remote(k, slot):
            return pltpu.make_async_remote_copy(
                src_ref=b_ref, dst_ref=recv.at[slot], send_sem=ssem.at[k - 1], recv_sem=rsem.at[k - 1],
                device_id=peer(k), device_id_type=MESH)

        sends = [remote(k, me) for k in range(1, 8)]
        for cp in sends:
            cp.start()
        recv[me] = b_ref[...]
        for k in range(1, 8):
            px, py, pc = peer(k)
            remote(k, 4 * px + 2 * py + pc).wait_recv()
        for cp in sends:
            cp.wait_send()
        total = recv[0]
        for j in range(1, 8):
            total = total + recv[j]
        o_ref[...] = total

    return pl.pallas_call(
        body, name="small_allreduce",
        in_specs=[pl.BlockSpec(memory_space=pltpu.VMEM)], out_specs=pl.BlockSpec(memory_space=pltpu.VMEM),
        out_shape=jax.ShapeDtypeStruct(buf.shape, f32),
        scratch_shapes=[pltpu.VMEM((8, R, LANE), f32), pltpu.SemaphoreType.DMA((7,)), pltpu.SemaphoreType.DMA((7,))],
    )(buf)


def sum_slots(own, recv, chip, core, name):
    _, R, C = recv.shape
    tr = _row_tile(R, 256)
    nt = R // tr

    def body(p_ref, a_ref, r_ref, o_ref):
        acc = jnp.zeros((tr, C), f32)
        for s in range(4):
            acc = acc + jnp.where(p_ref[0] == s, a_ref[s], r_ref[s]).astype(f32)
        o_ref[...] = acc

    slots = pl.BlockSpec((4, tr, C), lambda i, p_ref: (0, i, 0))
    return pl.pallas_call(
        body, name=name, out_shape=jax.ShapeDtypeStruct((2 * R, C), f32),
        grid_spec=pltpu.PrefetchScalarGridSpec(
            num_scalar_prefetch=1, grid=(nt,), in_specs=[slots, slots],
            out_specs=pl.BlockSpec((tr, C), lambda i, p_ref: (p_ref[1] * nt + i, 0))),
        compiler_params=_cp(("parallel",)),
    )(jnp.stack([chip, core]).astype(jnp.int32), own, recv)


def _half(ref, core, axis=0):
    half = ref.shape[axis] // 2
    idx = (slice(None),) * axis + (pl.ds(core * half, half),)
    return ref.at[idx]


IN_HBM = pl.BlockSpec(memory_space=pltpu.HBM)
IN_SEM = pl.BlockSpec(memory_space=pltpu.SEMAPHORE)
SIDE_EFFECT = pltpu.SideEffectType.DATAFLOW_SIDE_EFFECTING


def _gather_copy(buf, i, k, ssem, rsem, place, landing):
    x, y, c = place
    px, py = [(1 - x, y), (x, 1 - y), (1 - x, 1 - y)][k]
    me = 2 * x + y
    return pltpu.make_async_remote_copy(
        src_ref=buf.at[me], dst_ref=buf.at[me if landing == "theirs" else 2 * px + py],
        send_sem=ssem.at[3 * i + k], recv_sem=rsem.at[3 * i + k], device_id=(px, py, c), device_id_type=MESH)


def gather_start(groups, name):
    flat = [b for grp in groups for b in grp]
    n, ng = len(flat), len(groups)

    def body(*refs):
        bufs, sems = refs[:n], refs[n:n + 2 * ng]
        place = _place()
        j = 0
        for g, grp in enumerate(groups):
            for i in range(len(grp)):
                for k in range(3):
                    _gather_copy(bufs[j], i, k, sems[2 * g], sems[2 * g + 1], place, "theirs").start()
                j += 1
        refs[-1][...] = jnp.zeros_like(refs[-1])

    sem_shapes = [pltpu.SemaphoreType.DMA((3 * len(grp),)) for grp in groups for _ in range(2)]
    out = pl.pallas_call(
        body, name=name, in_specs=[IN_HBM] * n,
        out_specs=(*[IN_SEM] * (2 * ng), *[IN_HBM] * n, pl.BlockSpec(memory_space=pltpu.VMEM)),
        out_shape=(*sem_shapes, *[pltpu.HBM(b.shape, b.dtype) for b in flat], jax.ShapeDtypeStruct((8, LANE), f32)),
        input_output_aliases={i: 2 * ng + i for i in range(n)},
        compiler_params=pltpu.CompilerParams(has_side_effects=SIDE_EFFECT),
    )(*[pltpu.with_memory_space_constraint(b, pltpu.HBM) for b in flat])
    sems, bufs = out[:2 * ng], list(out[2 * ng:2 * ng + n])
    flights, j = [], 0
    for g, grp in enumerate(groups):
        flights.append((bufs[j:j + len(grp)], sems[2 * g], sems[2 * g + 1]))
        j += len(grp)
    return flights, out[-1]


def gather_wait(flight, after, name):
    bufs, ssem, rsem = flight
    n = len(bufs)

    def body(*refs):
        place = _place()
        for i in range(n):
            for k in range(3):
                cp = _gather_copy(refs[i], i, k, refs[n], refs[n + 1], place, "mine")
                cp.wait_send()
                cp.wait_recv()

    return pl.pallas_call(
        body, name=name, in_specs=[IN_HBM] * n + [IN_SEM, IN_SEM, ANY], out_specs=[IN_HBM] * n,
        out_shape=[pltpu.HBM(b.shape, b.dtype) for b in bufs], input_output_aliases={i: i for i in range(n)},
        compiler_params=pltpu.CompilerParams(has_side_effects=SIDE_EFFECT),
    )(*bufs, ssem, rsem, after)


def _scatter_copy(src, land, j, k, ssem, rsem, place, landing):
    x, y, c = place
    px, py = [(1 - x, y), (x, 1 - y), (1 - x, 1 - y)][k]
    return pltpu.make_async_remote_copy(
        src_ref=src.at[2 * px + py], dst_ref=land.at[2 * x + y if landing == "theirs" else 2 * px + py],
        send_sem=ssem.at[3 * j + k], recv_sem=rsem.at[3 * j + k], device_id=(px, py, c), device_id_type=MESH)


def scatter_start(srcs, name):
    n = len(srcs)
    lands = [lax.empty(g.shape, g.dtype) for g in srcs]

    def body(*refs):
        place = _place()
        for j in range(n):
            for k in range(3):
                _scatter_copy(refs[j], refs[n + j], j, k, refs[2 * n], refs[2 * n + 1], place, "theirs").start()
        refs[-1][...] = jnp.zeros_like(refs[-1])

    sem = pltpu.SemaphoreType.DMA((3 * n,))
    hbm = [pltpu.with_memory_space_constraint(b, pltpu.HBM) for b in list(srcs) + lands]
    out = pl.pallas_call(
        body, name=name, in_specs=[IN_HBM] * (2 * n),
        out_specs=(IN_SEM, IN_SEM, *[IN_HBM] * (2 * n), pl.BlockSpec(memory_space=pltpu.VMEM)),
        out_shape=(sem, sem, *[pltpu.HBM(b.shape, b.dtype) for b in hbm], jax.ShapeDtypeStruct((8, LANE), f32)),
        input_output_aliases={i: 2 + i for i in range(2 * n)},
        compiler_params=pltpu.CompilerParams(has_side_effects=SIDE_EFFECT),
    )(*hbm)
    return (list(out[2:2 + n]), list(out[2 + n:2 + 2 * n]), out[0], out[1]), out[-1]


def scatter_wait(flight, after, name):
    srcs, lands, ssem, rsem = flight
    n = len(srcs)

    def body(*refs):
        place = _place()
        for j in range(n):
            for k in range(3):
                cp = _scatter_copy(refs[j], refs[n + j], j, k, refs[2 * n], refs[2 * n + 1], place, "mine")
                cp.wait_send()
                cp.wait_recv()

    out = pl.pallas_call(
        body, name=name, in_specs=[IN_HBM] * (2 * n) + [IN_SEM, IN_SEM, ANY], out_specs=[IN_HBM] * (2 * n),
        out_shape=[pltpu.HBM(b.shape, b.dtype) for b in list(srcs) + list(lands)],
        input_output_aliases={i: i for i in range(2 * n)},
        compiler_params=pltpu.CompilerParams(has_side_effects=SIDE_EFFECT),
    )(*srcs, *lands, ssem, rsem, after)
    return list(out[:n]), list(out[n:])


def _pair_copy(src, land, j, ssem, rsem, place):
    x, y, c = place
    return pltpu.make_async_remote_copy(
        src_ref=_half(src, 1 - c, axis=1), dst_ref=land, send_sem=ssem.at[j], recv_sem=rsem.at[j],
        device_id=(x, y, 1 - c), device_id_type=MESH)


def pair_start(srcs, name):
    n = len(srcs)
    lands = [lax.empty((4, g.shape[1] // 2, g.shape[2]), g.dtype) for g in srcs]

    def body(*refs):
        place = _place()
        for j in range(n):
            _pair_copy(refs[j], refs[n + j], j, refs[2 * n], refs[2 * n + 1], place).start()
        refs[-1][...] = jnp.zeros_like(refs[-1])

    sem = pltpu.SemaphoreType.DMA((n,))
    hbm = [pltpu.with_memory_space_constraint(b, pltpu.HBM) for b in list(srcs) + lands]
    out = pl.pallas_call(
        body, name=name, in_specs=[IN_HBM] * (2 * n),
        out_specs=(IN_SEM, IN_SEM, *[IN_HBM] * (2 * n), pl.BlockSpec(memory_space=pltpu.VMEM)),
        out_shape=(sem, sem, *[pltpu.HBM(b.shape, b.dtype) for b in hbm], jax.ShapeDtypeStruct((8, LANE), f32)),
        input_output_aliases={i: 2 + i for i in range(2 * n)},
        compiler_params=pltpu.CompilerParams(has_side_effects=SIDE_EFFECT),
    )(*hbm)
    return (list(out[2:2 + n]), list(out[2 + n:2 + 2 * n]), out[0], out[1]), out[-1]


def pair_wait(flight, after, name):
    srcs, lands, ssem, rsem = flight
    n = len(srcs)

    def body(*refs):
        place = _place()
        for j in range(n):
            cp = _pair_copy(refs[j], refs[n + j], j, refs[2 * n], refs[2 * n + 1], place)
            cp.wait_send()
            cp.wait_recv()

    out = pl.pallas_call(
        body, name=name, in_specs=[IN_HBM] * (2 * n) + [IN_SEM, IN_SEM, ANY], out_specs=[IN_HBM] * (2 * n),
        out_shape=[pltpu.HBM(b.shape, b.dtype) for b in list(srcs) + list(lands)],
        input_output_aliases={i: i for i in range(2 * n)},
        compiler_params=pltpu.CompilerParams(has_side_effects=SIDE_EFFECT),
    )(*srcs, *lands, ssem, rsem, after)
    return list(out[:n]), list(out[n:])


def _row_tile(rows, cap=512):
    return max(t for t in range(16, min(rows, cap) + 1, 16) if rows % t == 0)


def pair_sum(mine, theirs, core, name):
    _, R, C = mine.shape
    half = R // 2
    tr = _row_tile(half)
    nt = half // tr

    def body(c_ref, a_ref, b_ref, o_ref):
        o_ref[...] = (a_ref[...].astype(f32) + b_ref[...].astype(f32)).astype(bf16)

    return pl.pallas_call(
        body, name=name, out_shape=jax.ShapeDtypeStruct(theirs.shape, bf16),
        grid_spec=pltpu.PrefetchScalarGridSpec(
            num_scalar_prefetch=1, grid=(4, nt),
            in_specs=[pl.BlockSpec((None, tr, C), lambda s, i, c_ref: (s, c_ref[0] * nt + i, 0)),
                      pl.BlockSpec((None, tr, C), lambda s, i, c_ref: (s, i, 0))],
            out_specs=pl.BlockSpec((None, tr, C), lambda s, i, c_ref: (s, i, 0))),
        compiler_params=_cp(("parallel", "parallel")),
    )(jnp.reshape(core, (1,)).astype(jnp.int32), mine, theirs)


def _final_copy(buf, j, ssem, rsem, place, landing):
    x, y, c = place
    return pltpu.make_async_remote_copy(
        src_ref=_half(buf, c), dst_ref=_half(buf, c if landing == "theirs" else 1 - c),
        send_sem=ssem.at[j], recv_sem=rsem.at[j], device_id=(x, y, 1 - c), device_id_type=MESH)


def final_start(fins, name):
    n = len(fins)

    def body(*refs):
        place = _place()
        for j in range(n):
            _final_copy(refs[j], j, refs[n], refs[n + 1], place, "theirs").start()
        refs[-1][...] = jnp.zeros_like(refs[-1])

    sem = pltpu.SemaphoreType.DMA((n,))
    hbm = [pltpu.with_memory_space_constraint(b, pltpu.HBM) for b in fins]
    out = pl.pallas_call(
        body, name=name, in_specs=[IN_HBM] * n,
        out_specs=(IN_SEM, IN_SEM, *[IN_HBM] * n, pl.BlockSpec(memory_space=pltpu.VMEM)),
        out_shape=(sem, sem, *[pltpu.HBM(b.shape, b.dtype) for b in hbm], jax.ShapeDtypeStruct((8, LANE), f32)),
        input_output_aliases={i: 2 + i for i in range(n)},
        compiler_params=pltpu.CompilerParams(has_side_effects=SIDE_EFFECT),
    )(*hbm)
    return (list(out[2:2 + n]), out[0], out[1]), out[-1]


def final_wait(flight, after, name):
    bufs, ssem, rsem = flight
    n = len(bufs)

    def body(*refs):
        place = _place()
        for j in range(n):
            cp = _final_copy(refs[j], j, refs[n], refs[n + 1], place, "mine")
            cp.wait_send()
            cp.wait_recv()

    out = pl.pallas_call(
        body, name=name, in_specs=[IN_HBM] * n + [IN_SEM, IN_SEM, ANY], out_specs=[IN_HBM] * n,
        out_shape=[pltpu.HBM(b.shape, b.dtype) for b in bufs], input_output_aliases={i: i for i in range(n)},
        compiler_params=pltpu.CompilerParams(has_side_effects=SIDE_EFFECT),
    )(*bufs, ssem, rsem, after)
    return list(out)


def adamw_big(w, m, v, gs, row0, name):
    L, R, C = w.shape
    tr = _row_tile(math.gcd(R, row0) if row0 else R, max(16, 262144 // C // 16 * 16))
    b0 = row0 // tr

    def body(*refs):
        w_ref, m_ref, v_ref = refs[:3]
        g_refs = refs[3:3 + L]
        g_ref, d_ref, nm_ref, nv_ref = refs[3 + L:]
        g = g_refs[0][...]
        for l in range(1, L):
            g = jnp.where(pl.program_id(0) == l, g_refs[l][...], g)
        d, nm, nv = _adamw_math(w_ref[...], g, m_ref[...], v_ref[...])
        g_ref[...] = g
        d_ref[...] = d
        nm_ref[...] = nm
        nv_ref[...] = nv

    own = pl.BlockSpec((None, tr, C), lambda l, i: (l, i, 0))
    off = pl.BlockSpec((tr, C), lambda l, i: (b0 + i, 0))
    return pl.pallas_call(
        body, name=name, grid=(L, R // tr), in_specs=[own, own, own] + [off] * L, out_specs=[own] * 4,
        out_shape=[jax.ShapeDtypeStruct((L, R, C), f32)] * 4, compiler_params=_cp(("parallel", "parallel")),
    )(w, m, v, *gs)


def _adamw_math(w, g, m, v):
    m = B1 * m + (1.0 - B1) * g
    v = B2 * v + (1.0 - B2) * (g * g)
    m_hat = m / (1.0 - B1 ** STEP)
    v_hat = v / (1.0 - B2 ** STEP)
    delta = -LR * (m_hat / (jnp.sqrt(v_hat) + AEPS) + WD * w)
    return delta, m, v


def adamw_small(w, m, v, g):
    def body(w_ref, m_ref, v_ref, g_ref, d_ref, nm_ref, nv_ref):
        d, nm, nv = _adamw_math(w_ref[...], g_ref[...], m_ref[...], v_ref[...])
        d_ref[...] = d
        nm_ref[...] = nm
        nv_ref[...] = nv

    return pl.pallas_call(body, name="adamw_small", out_shape=[jax.ShapeDtypeStruct(w.shape, f32)] * 3)(w, m, v, g)


CONV =(("conv_qkv_b", 2), ("ffn_conv_w", 2))
SMALL = ("rel_bias", "norm_mix_g", "norm_mem_g", "sinks_a", "a_log_b", "dt_bias_b", "out_norm_g_b", "norm_ffn_g",
         "ffn_conv_b", "final_norm_g")
WEIGHTS = ("rel_bias", "norm_mix_g", "norm_mem_g", "w_mem_kv", "w_out", "w_in_a", "sinks_a", "w_in_b", "conv_qkv_b",
           "a_log_b", "dt_bias_b", "out_norm_g_b", "norm_ffn_g", "w_gate_up", "ffn_conv_w", "ffn_conv_b", "w_down",
           "final_norm_g")
ARGS = ("x", "mem") + WEIGHTS + ("loss_target",) + tuple("m_" + n for n in WEIGHTS) + tuple("v_" + n for n in WEIGHTS)


def _rows(a, width):
    flat = a.reshape(-1)
    pad = (-flat.shape[0]) % (8 * width)
    if pad:
        flat = jnp.concatenate([flat, jnp.zeros((pad,), a.dtype)])
    return flat.reshape(-1, width)


def _nrows(shape, width):
    return _pad_to(-(-math.prod(shape) // width), 8)


def _pack(arrs, width, total_rows, dtype):
    parts = [_rows(a.astype(dtype), width) for a in arrs]
    used = sum(p.shape[0] for p in parts)
    if total_rows > used:
        parts.append(jnp.zeros((total_rows - used, width), dtype))
    return jnp.concatenate(parts, axis=0)


def _unpack(buf, shapes, width):
    out, r = [], 0
    for s in shapes:
        n = _nrows(s, width)
        out.append(buf[r:r + n].reshape(-1)[:math.prod(s)].reshape(s))
        r += n
    return out


def _pad_to(n, mult):
    return -(-n // mult) * mult


def kernel(x, mem, rel_bias, norm_mix_g, norm_mem_g, w_mem_kv, w_out, w_in_a, sinks_a, w_in_b, conv_qkv_b, a_log_b, dt_bias_b, out_norm_g_b, norm_ffn_g, w_gate_up, ffn_conv_w, ffn_conv_b, w_down, final_norm_g, loss_target, m_rel_bias, m_norm_mix_g, m_norm_mem_g, m_w_mem_kv, m_w_out, m_w_in_a, m_sinks_a, m_w_in_b, m_conv_qkv_b, m_a_log_b, m_dt_bias_b, m_out_norm_g_b, m_norm_ffn_g, m_w_gate_up, m_ffn_conv_w, m_ffn_conv_b, m_w_down, m_final_norm_g, v_rel_bias, v_norm_mix_g, v_norm_mem_g, v_w_mem_kv, v_w_out, v_w_in_a, v_sinks_a, v_w_in_b, v_conv_qkv_b, v_a_log_b, v_dt_bias_b, v_out_norm_g_b, v_norm_ffn_g, v_w_gate_up, v_ffn_conv_w, v_ffn_conv_b, v_w_down, v_final_norm_g):
    A = dict(zip(ARGS, (x, mem, rel_bias, norm_mix_g, norm_mem_g, w_mem_kv, w_out, w_in_a, sinks_a, w_in_b, conv_qkv_b, a_log_b, dt_bias_b, out_norm_g_b, norm_ffn_g, w_gate_up, ffn_conv_w, ffn_conv_b, w_down, final_norm_g, loss_target, m_rel_bias, m_norm_mix_g, m_norm_mem_g, m_w_mem_kv, m_w_out, m_w_in_a, m_sinks_a, m_w_in_b, m_conv_qkv_b, m_a_log_b, m_dt_bias_b, m_out_norm_g_b, m_norm_ffn_g, m_w_gate_up, m_ffn_conv_w, m_ffn_conv_b, m_w_down, m_final_norm_g, v_rel_bias, v_norm_mix_g, v_norm_mem_g, v_w_mem_kv, v_w_out, v_w_in_a, v_sinks_a, v_w_in_b, v_conv_qkv_b, v_a_log_b, v_dt_bias_b, v_out_norm_g_b, v_norm_ffn_g, v_w_gate_up, v_ffn_conv_w, v_ffn_conv_b, v_w_down, v_final_norm_g)))
    chip = 2 * lax.axis_index("x") + lax.axis_index("y")
    core = lax.axis_index("c")

    def own_slot(shard):
        return lax.dynamic_update_index_in_dim(lax.empty((4,) + shard.shape, shard.dtype), shard, chip, 0)

    def bslot(w, tie=0.0):
        return own_slot((w + tie).astype(bf16))

    early = {
        ("w_in", 0): [bslot(w_in_a[0])],
        ("w_mem", 0): [bslot(w_mem_kv[0]), own_slot(ffn_conv_w.reshape(6, -1))],
        ("w_out", 0): [bslot(w_out[0])],
    }
    flights_a, gone = gather_start(list(early.values()), "gather_start_first")
    z = gone[0, 0]
    late = {
        ("w_gu", 0): [bslot(w_gate_up[0], z)], ("w_down", 0): [bslot(w_down[0], z)],
        ("w_in", 1): [bslot(w_in_b[0], z)], ("w_mem", 1): [bslot(w_mem_kv[1], z), own_slot(conv_qkv_b[0] + z)],
        ("w_out", 1): [bslot(w_out[1], z)], ("w_gu", 1): [bslot(w_gate_up[1], z)], ("w_down", 1): [bslot(w_down[1], z)],
    }
    flights_b, started_all = gather_start(list(late.values()), "gather_start_rest")
    flights = dict(zip(list(early) + list(late), flights_a + flights_b))
    P = {"rel_bias": rel_bias, "sinks": sinks_a[0], "a_log": a_log_b[0], "dt_bias": dt_bias_b[0],
         "out_norm_g": out_norm_g_b[0], "g_mix": norm_mix_g, "g_mem": norm_mem_g, "g_ffn": norm_ffn_g,
         "g_fin": final_norm_g, "ffn_cb": [ffn_conv_b[0], ffn_conv_b[1]], "w_mem": [None, None], "w_out": [None, None],
         "w_gu": [None, None], "w_down": [None, None], "ffn_cw": [None, None]}

    def rows4(g):
        return g.reshape(4 * g.shape[1], g.shape[2])

    def arrive(key, after):
        if key not in flights:
            return
        got = gather_wait(flights.pop(key), started_all if key == ("w_in", 0) else after, "gather_wait_%s%d" % key)
        name, i = key
        if name == "w_in":
            P["w_in_a" if i == 0 else "w_in_b"] = (_lay_in_a if i == 0 else _lay_in_b)(_unchip_cols(got[0]))
        elif name == "w_mem":
            P["w_mem"][i] = rows4(got[0])
            if i == 0:
                cw = _unchip_cols(got[1]).reshape(2, 3, D_FF)
                P["ffn_cw"] = [cw[0], cw[1]]
            else:
                P["conv_qkv"] = _unchip_cols(got[1])
        elif name == "w_out":
            P["w_out"][i] = _lay_out_a(rows4(got[0])) if i == 0 else rows4(got[0])
        elif name == "w_gu":
            P["w_gu"][i] = got[0]
        else:
            P["w_down"][i] = rows4(got[0])

    def chip_rows(g):
        return g.reshape(4, g.shape[0] // 4, g.shape[-1])

    sent, started, pending = {}, [], []

    def finish(after):
        key, names, flight = pending.pop()
        tag = "%s%d" % key
        partial, theirs = pair_wait(flight, after, "pair_wait_" + tag)
        pair = [pair_sum(p, t, core, "pair_sum_%s%d" % (nm, key[1])) for p, t, nm in zip(partial, theirs, names)]
        flight, token = scatter_start(pair, "scatter_start_" + tag)
        sent[key] = (names, flight, token)
        started.append(token[0, 0])

    def ready(key, G, dep):
        kind, i = key
        if kind == "tick":
            finish(G["w_out"][i])
        else:
            if kind == "ffn":
                if pending:
                    finish(G["w_gu"][i])
                names, partial = ("gu", "down"), [G["w_gu"][i], chip_rows(G["w_down"][i]).astype(bf16)]
            else:
                g_out = _unlay_out_a(G["w_out"][0]) if i == 0 else G["w_out"][1]
                g_in = _unlay_in_a(G["w_in_a"]) if i == 0 else _unlay_in_b(G["w_in_b"])
                names = ("out", "in", "mem")
                partial = [chip_rows(g_out).astype(bf16), _chip_cols(g_in).astype(bf16),
                           chip_rows(G["w_mem"][i]).astype(bf16)]
            flight, token = pair_start(partial, "pair_start_%s%d" % key)
            pending.append((key, names, flight))
            started.append(token[0, 0])
        if dep is not None:
            while started:
                dep = dep + started.pop()
        return dep

    P["arrive"], P["ready"] = arrive, ready

    loss, dx, G = _local_step(x[0], mem[0], loss_target[0], P)
    gfull = _grads_to_ref(G)
    finish(dx)

    after, halves = sent["mix", 0][2], []
    for key in (("ffn", 1), ("mix", 1), ("ffn", 0), ("mix", 0)):
        names, flight, _ = sent[key]
        pair, arrived = scatter_wait(flight, after, "scatter_wait_%s%d" % key)
        fins = [sum_slots(p, r, chip, core, "sum_slots_%s%d" % (nm, key[1])) for nm, p, r in zip(names, pair, arrived)]
        flight, after = final_start(fins, "final_start_%s%d" % key)
        halves.append(([(nm, key[1]) for nm in names], flight, key))
    done = {}
    for ids, flight, key in halves:
        done.update(zip(ids, final_wait(flight, after, "final_wait_%s%d" % key)))

    sm_shapes = [A[n].shape for n in SMALL] + [gfull[n].shape for n, _ in CONV] + [(LANE,)]
    sm_rows = _pad_to(sum(_nrows(s, LANE) for s in sm_shapes), 8)
    sbuf = _pack([gfull[n] for n in SMALL] + [gfull[n] for n, _ in CONV] + [loss[0]], LANE, sm_rows, f32)
    tot = _unpack(allreduce_small(sbuf), sm_shapes, LANE)
    gsmall = dict(zip(SMALL, tot[:len(SMALL)]))
    for (n, axis), t in zip(CONV, tot[len(SMALL):len(SMALL) + len(CONV)]):
        sh = A[n].shape[axis]
        gsmall[n] = lax.dynamic_slice_in_dim(t, chip * sh, sh, axis)
    loss_out = tot[-1][0]

    out = {}
    plan = (("w_gate_up", [done["gu", 0], done["gu", 1]]), ("w_down", [done["down", 0], done["down", 1]]),
            ("w_out", [done["out", 0], done["out", 1]]), ("w_mem_kv", [done["mem", 0], done["mem", 1]]),
            ("w_in_a", [done["in", 0]]), ("w_in_b", [done["in", 1]]))
    for n, gs in plan:
        shape3 = (len(gs),) + gs[0].shape
        res = adamw_big(A[n].reshape(shape3), A["m_" + n].reshape(shape3), A["v_" + n].reshape(shape3), gs, 0,
                        "adamw_" + n)
        for key, r in zip(("grad_", "delta_", "new_m_", "new_v_"), res):
            out[key + n] = r.reshape(A[n].shape)
    names = SMALL + tuple(n for n, _ in CONV)
    shapes = [A[n].shape for n in names]
    rows = _pad_to(sum(_nrows(s, LANE) for s in shapes), 8)
    packs = [_pack([src[n] for n in names], LANE, rows, f32)
             for src in ({n: A[n] for n in names}, {n: A["m_" + n] for n in names}, {n: A["v_" + n] for n in names}, gsmall)]
    res = adamw_small(*packs)
    for key, r in zip(("delta_", "new_m_", "new_v_"), res):
        for n, a in zip(names, _unpack(r, shapes, LANE)):
            out[key + n] = a
    for n in names:
        out["grad_" + n] = gsmall[n]
    return (loss_out, dx[None], *[out["grad_" + n] for n in WEIGHTS], *[out["delta_" + n] for n in WEIGHTS],
            *[out["new_m_" + n] for n in WEIGHTS], *[out["new_v_" + n] for n in WEIGHTS])
```

```python
import functools
import math

import numpy as np
import jax
import jax.numpy as jnp
from jax import lax
from jax.experimental import pallas as pl
from jax.experimental.pallas import tpu as pltpu

f32 = jnp.float32
bf16 = jnp.bfloat16
HI = lax.Precision.HIGHEST
MESH = pl.DeviceIdType.MESH

D = 1024
MEM_LEN = 256
EPS = 1e-6
A_HEADS, A_KV, A_DH = 12, 2, 64
A_Q = 768
BLK = 128
N_BUCKETS, MAX_DIST = 32, 128
B_QK, B_V, B_DH = 384, 768, 128
B_QKV = 1536
CHUNK = 64
X_Q = 256
D_FF = 2816
LANE = 128
VMEM_LIMIT = 56 * 1024 * 1024
MM_ROWS = 1024

LR, B1, B2, AEPS, WD, STEP = 0.001, 0.9, 0.999, 1e-08, 0.01, 10


def _cp(sem=None):
    return pltpu.CompilerParams(dimension_semantics=sem, vmem_limit_bytes=VMEM_LIMIT)


def _dg(a, b, ca, cb, prec=None):
    return lax.dot_general(a, b, (((ca,), (cb,)), ((), ())), precision=prec, preferred_element_type=f32)


@jax.custom_vjp
def bdot(a, b):
    return _dg(a.astype(bf16), b.astype(bf16), 1, 0)


def _bdot_f(a, b):
    return bdot(a, b), (a, b)


def _bdot_b(res, g):
    a, b = res
    gb = g.astype(bf16)
    return _dg(gb, b.astype(bf16), 1, 1), _dg(a.astype(bf16), gb, 0, 0)


bdot.defvjp(_bdot_f, _bdot_b)


@jax.custom_vjp
def bdot_nt(a, b):
    return _dg(a.astype(bf16), b.astype(bf16), 1, 1)


def _bdot_nt_f(a, b):
    return bdot_nt(a, b), (a, b)


def _bdot_nt_b(res, g):
    a, b = res
    gb = g.astype(bf16)
    return _dg(gb, b.astype(bf16), 1, 0), _dg(gb, a.astype(bf16), 0, 0)


bdot_nt.defvjp(_bdot_nt_f, _bdot_nt_b)


def _shift_rows(x, s, down):
    n = x.shape[0]
    row = lax.broadcasted_iota(jnp.int32, x.shape, 0)
    if down:
        return jnp.where(row >= s, pltpu.roll(x, s, 0), 0.0)
    return jnp.where(row < n - s, pltpu.roll(x, n - s, 0), 0.0)


@functools.partial(jax.custom_vjp, nondiff_argnums=(1,))
def shift_down(x, s):
    return _shift_rows(x, s, True)


def _sd_f(x, s):
    return _shift_rows(x, s, True), None


def _sd_b(s, _, g):
    return (_shift_rows(g, s, False),)


shift_down.defvjp(_sd_f, _sd_b)


def _sigmoid(x):
    return 1.0 / (1.0 + jnp.exp(-x))


def _silu(x):
    return x * _sigmoid(x)


def _rms(x, g):
    return x * lax.rsqrt(jnp.mean(x * x, axis=-1, keepdims=True) + EPS) * g


def _tile(n, cap):
    u = n // LANE
    best = 1
    for d in range(1, u + 1):
        if u % d == 0 and d * LANE <= cap:
            best = d
    return best * LANE


def mm_nn(a, w, res=None, out_dtype=f32, name="mm_nn"):
    M, K = a.shape
    N = w.shape[1]
    tm, tn = min(MM_ROWS, M), _tile(N, 1024)

    def body(*refs):
        if res is None:
            a_ref, w_ref, o_ref = refs
            o_ref[...] = _dg(a_ref[...].astype(bf16), w_ref[...], 1, 0).astype(out_dtype)
        else:
            a_ref, w_ref, r_ref, o_ref = refs
            o_ref[...] = (r_ref[...] + _dg(a_ref[...].astype(bf16), w_ref[...], 1, 0)).astype(out_dtype)

    in_specs = [pl.BlockSpec((tm, K), lambda n, m: (m, 0)), pl.BlockSpec((K, tn), lambda n, m: (0, n))]
    args = [a, w]
    if res is not None:
        in_specs.append(pl.BlockSpec((tm, tn), lambda n, m: (m, n)))
        args.append(res)
    return pl.pallas_call(
        body, name=name, grid=(N // tn, M // tm), in_specs=in_specs,
        out_specs=pl.BlockSpec((tm, tn), lambda n, m: (m, n)),
        out_shape=jax.ShapeDtypeStruct((M, N), out_dtype),
        compiler_params=_cp(("parallel", "parallel")),
    )(*args)


def mm_res_norm(a, w, res, g, name):
    pieces = a if isinstance(a, tuple) else (a,)
    na = len(pieces)
    M, K = pieces[0].shape[0], sum(p.shape[1] for p in pieces)
    tm = min(MM_ROWS, M)

    def body(*refs):
        w_ref, r_ref, g_ref, o_ref, n_ref = refs[na:]
        h = r_ref[...] + _dg(_cols(refs[:na]).astype(bf16), w_ref[...], 1, 0)
        o_ref[...] = h
        n_ref[...] = _rms(h, g_ref[...]).astype(bf16)

    tok = pl.BlockSpec((tm, D), lambda m: (m, 0))
    return pl.pallas_call(
        body, name=name, grid=(M // tm,),
        in_specs=[pl.BlockSpec((tm, p.shape[1]), lambda m: (m, 0)) for p in pieces]
        + [pl.BlockSpec((K, D), lambda m: (0, 0)), tok, pl.BlockSpec((1, D), lambda m: (0, 0))],
        out_specs=[tok, tok],
        out_shape=[jax.ShapeDtypeStruct((M, D), f32), jax.ShapeDtypeStruct((M, D), bf16)],
        compiler_params=_cp(("parallel",)),
    )(*pieces, w, res, g.reshape(1, D))


def mm_nt(dy, w, out_dtype=f32, name="mm_nt"):
    M, N = dy.shape
    K = w.shape[0]
    tm, tn = min(MM_ROWS, M), _tile(N, 1024)
    assert out_dtype == f32 or tn == N

    def body(dy_ref, w_ref, o_ref):
        part = _dg(dy_ref[...].astype(bf16), w_ref[...], 1, 1)
        if tn == N:
            o_ref[...] = part.astype(out_dtype)
        else:
            @pl.when(pl.program_id(1) == 0)
            def _():
                o_ref[...] = jnp.zeros_like(o_ref)
            o_ref[...] += part

    return pl.pallas_call(
        body, name=name, grid=(M // tm, N // tn),
        in_specs=[pl.BlockSpec((tm, tn), lambda m, n: (m, n)), pl.BlockSpec((K, tn), lambda m, n: (0, n))],
        out_specs=pl.BlockSpec((tm, K), lambda m, n: (m, 0)),
        out_shape=jax.ShapeDtypeStruct((M, K), out_dtype),
        compiler_params=_cp(("parallel", "arbitrary")),
    )(dy, w)


NORM_ROWS = 1024


def _acc_then_norm_bwd(part, steps, h_ref, g_ref, r_ref, o_ref, dg_ref):
    k = pl.program_id(1)

    @pl.when((pl.program_id(0) == 0) & (k == 0))
    def _():
        dg_ref[...] = jnp.zeros_like(dg_ref)

    @pl.when(k == 0)
    def _():
        o_ref[...] = part

    @pl.when(k > 0)
    def _():
        o_ref[...] += part

    @pl.when(k == steps - 1)
    def _():
        _, vjp = jax.vjp(_rms, h_ref[...], g_ref[...])
        dh, dg = vjp(o_ref[...])
        o_ref[...] = r_ref[...] + dh
        dg_ref[...] += dg


def mm_nt_norm(dy, w, norm, name):
    pieces = dy if isinstance(dy, tuple) else (dy,)
    nd = len(pieces)
    M, N = pieces[0].shape[0], sum(p.shape[1] for p in pieces)
    tm, tn = (min(NORM_ROWS, M), _tile(N, 1024)) if nd == 1 else (min(512, M), N)

    def body(*refs):
        w_ref, h_ref, g_ref, r_ref, o_ref, dg_ref = refs[nd:]
        _acc_then_norm_bwd(_dg(_cols(refs[:nd]).astype(bf16), w_ref[...], 1, 1), N // tn, h_ref, g_ref, r_ref, o_ref,
                           dg_ref)

    tok = pl.BlockSpec((tm, D), lambda m, n: (m, 0))
    vec = pl.BlockSpec((1, D), lambda m, n: (0, 0))
    return pl.pallas_call(
        body, name=name, grid=(M // tm, N // tn),
        in_specs=[pl.BlockSpec((tm, tn if nd == 1 else p.shape[1]), lambda m, n: (m, n)) for p in pieces]
        + [pl.BlockSpec((D, tn), lambda m, n: (0, n)), tok, vec, tok],
        out_specs=[tok, vec],
        out_shape=[jax.ShapeDtypeStruct((M, D), f32), jax.ShapeDtypeStruct((1, D), f32)],
        compiler_params=_cp(("arbitrary", "arbitrary")),
    )(*pieces, w, norm[0], norm[1].reshape(1, D), norm[2])


def _cols(refs):
    return refs[0][...] if len(refs) == 1 else jnp.concatenate([r[...] for r in refs], axis=1)


def mm_tn(a, dy, name="mm_tn"):
    pieces = a if isinstance(a, tuple) else (a,)
    dpieces = dy if isinstance(dy, tuple) else (dy,)
    na, nd = len(pieces), len(dpieces)
    M, K = pieces[0].shape[0], sum(p.shape[1] for p in pieces)
    N = sum(p.shape[1] for p in dpieces)
    tm = min(MM_ROWS, M)
    tk = _tile(K, 1408) if na == 1 else K
    tn = _tile(N, 1024) if nd == 1 else N

    def body(*refs):
        o_ref = refs[-1]

        @pl.when(pl.program_id(2) == 0)
        def _():
            o_ref[...] = jnp.zeros_like(o_ref)
        o_ref[...] += _dg(_cols(refs[:na]).astype(bf16), _cols(refs[na:na + nd]).astype(bf16), 0, 0)

    a_specs = [pl.BlockSpec((tm, tk if na == 1 else p.shape[1]), lambda k, n, m: (m, k)) for p in pieces]
    d_specs = [pl.BlockSpec((tm, tn if nd == 1 else p.shape[1]), lambda k, n, m: (m, n)) for p in dpieces]
    return pl.pallas_call(
        body, name=name, grid=(K // tk, N // tn, M // tm), in_specs=a_specs + d_specs,
        out_specs=pl.BlockSpec((tk, tn), lambda k, n, m: (k, n)),
        out_shape=jax.ShapeDtypeStruct((K, N), f32),
        compiler_params=_cp(("parallel", "parallel", "arbitrary")),
    )(*pieces, *dpieces)


def rms_fwd(h, g, name):
    S = h.shape[0]
    t = min(512, S)

    def body(h_ref, g_ref, o_ref):
        o_ref[...] = _rms(h_ref[...], g_ref[...]).astype(bf16)

    return pl.pallas_call(
        body, name=name, grid=(S // t,),
        in_specs=[pl.BlockSpec((t, D), lambda i: (i, 0)), pl.BlockSpec((1, D), lambda i: (0, 0))],
        out_specs=pl.BlockSpec((t, D), lambda i: (i, 0)),
        out_shape=jax.ShapeDtypeStruct((S, D), bf16),
        compiler_params=_cp(("parallel",)),
    )(h, g.reshape(1, D))


def loss_head(h, g, target):
    S = h.shape[0]
    t = min(512, S)

    def f(hh, gg, tt):
        err = _rms(hh, gg) - tt
        return 0.5 * jnp.sum(jnp.mean(err * err, axis=-1, keepdims=True), axis=0, keepdims=True)

    def body(h_ref, g_ref, t_ref, loss_ref, dh_ref, dg_ref):
        @pl.when(pl.program_id(0) == 0)
        def _():
            dg_ref[...] = jnp.zeros_like(dg_ref)
            loss_ref[...] = jnp.zeros_like(loss_ref)
        val, vjp = jax.vjp(lambda a, b: f(a, b, t_ref[...]), h_ref[...], g_ref[...])
        dh, dg = vjp(jnp.ones((1, 1), f32))
        dh_ref[...] = dh
        dg_ref[...] += dg
        loss_ref[...] += jnp.broadcast_to(val, loss_ref.shape)

    tok = pl.BlockSpec((t, D), lambda i: (i, 0))
    vec = pl.BlockSpec((1, D), lambda i: (0, 0))
    return pl.pallas_call(
        body, name="loss_head", grid=(S // t,), in_specs=[tok, vec, tok],
        out_specs=[pl.BlockSpec((1, LANE), lambda i: (0, 0)), tok, vec],
        out_shape=[jax.ShapeDtypeStruct((1, LANE), f32), jax.ShapeDtypeStruct((S, D), f32),
                   jax.ShapeDtypeStruct((1, D), f32)],
        compiler_params=_cp(("arbitrary",)),
    )(h, g.reshape(1, D), target)


def memkv_fwd(mem, g, w, name):
    def body(m_ref, g_ref, w_ref, o_ref):
        o_ref[...] = _dg(_rms(m_ref[...], g_ref[...]).astype(bf16), w_ref[...], 1, 0)

    return pl.pallas_call(
        body, name=name, out_shape=jax.ShapeDtypeStruct((MEM_LEN, 2 * X_Q), f32), compiler_params=_cp(),
    )(mem, g.reshape(1, D), w)


def memkv_bwd(mem, g, w, dkv, name):
    def body(m_ref, g_ref, w_ref, d_ref, dg_ref, dw_ref):
        n, vjp = jax.vjp(lambda gg: _rms(m_ref[...], gg), g_ref[...])
        db = d_ref[...].astype(bf16)
        dw_ref[...] = _dg(n.astype(bf16), db, 0, 0)
        dg_ref[...] = vjp(_dg(db, w_ref[...], 1, 1))[0]

    return pl.pallas_call(
        body, name=name,
        out_shape=[jax.ShapeDtypeStruct((1, D), f32), jax.ShapeDtypeStruct((D, 2 * X_Q), f32)],
        compiler_params=_cp(),
    )(mem, g.reshape(1, D), w, dkv)


def _xattn_f(xq, mk, mv):
    lane = lax.broadcasted_iota(jnp.int32, (1, X_Q), 1)
    out = jnp.zeros(xq.shape, f32)
    for hd in range(4):
        msk = (lane // 64 == hd).astype(f32)
        s = bdot_nt(xq * msk, mk) * (64 ** -0.5)
        m = lax.stop_gradient(jnp.max(s, axis=-1, keepdims=True))
        p = jnp.exp(s - m)
        p = p / jnp.sum(p, axis=-1, keepdims=True)
        out = out + bdot(p, mv * msk)
    return out


def xattn_fwd(proj, col, kv, name):
    S = proj.shape[0]
    t = min(512, S)
    cb = col // X_Q

    def body(q_ref, k_ref, v_ref, o_ref):
        o_ref[...] = _xattn_f(q_ref[...].astype(f32), k_ref[...], v_ref[...]).astype(bf16)

    return pl.pallas_call(
        body, name=name, grid=(S // t,),
        in_specs=[pl.BlockSpec((t, X_Q), lambda i: (i, cb)), pl.BlockSpec((MEM_LEN, X_Q), lambda i: (0, 0)),
                  pl.BlockSpec((MEM_LEN, X_Q), lambda i: (0, 1))],
        out_specs=pl.BlockSpec((t, X_Q), lambda i: (i, 0)),
        out_shape=jax.ShapeDtypeStruct((S, X_Q), bf16),
        compiler_params=_cp(("parallel",)),
    )(proj, kv, kv)


def xattn_bwd(proj, col, kv, dmix, name):
    S = proj.shape[0]
    t = min(512, S)
    cb = col // X_Q

    def body(q_ref, k_ref, v_ref, do_ref, dq_ref, dk_ref, dv_ref):
        @pl.when(pl.program_id(0) == 0)
        def _():
            dk_ref[...] = jnp.zeros_like(dk_ref)
            dv_ref[...] = jnp.zeros_like(dv_ref)
        _, vjp = jax.vjp(_xattn_f, q_ref[...].astype(f32), k_ref[...], v_ref[...])
        dq, dk, dv = vjp(do_ref[...])
        dq_ref[...] = dq.astype(bf16)
        dk_ref[...] += dk
        dv_ref[...] += dv

    kvb = pl.BlockSpec((MEM_LEN, X_Q), lambda i: (0, 0))
    dq, dk, dv = pl.pallas_call(
        body, name=name, grid=(S // t,),
        in_specs=[pl.BlockSpec((t, X_Q), lambda i: (i, cb)), kvb,
                  pl.BlockSpec((MEM_LEN, X_Q), lambda i: (0, 1)), pl.BlockSpec((t, X_Q), lambda i: (i, 3))],
        out_specs=[pl.BlockSpec((t, X_Q), lambda i: (i, 0)), kvb, kvb],
        out_shape=[jax.ShapeDtypeStruct((S, X_Q), bf16), jax.ShapeDtypeStruct((MEM_LEN, X_Q), f32),
                   jax.ShapeDtypeStruct((MEM_LEN, X_Q), f32)],
        compiler_params=_cp(("arbitrary",)),
    )(proj, kv, kv, dmix)
    return dq, jnp.concatenate([dk, dv], axis=1)


def _bucket_map():
    qi = np.arange(BLK)[:, None]
    kj = np.arange(2 * BLK)[None, :]
    n = np.maximum(BLK + qi - kj, 0)
    max_exact = N_BUCKETS // 2
    nf = np.maximum(n, 1).astype(np.float64)
    large = max_exact + (np.log(nf / max_exact) / math.log(MAX_DIST / max_exact)
                         * (N_BUCKETS - max_exact)).astype(np.int32)
    large = np.minimum(large, N_BUCKETS - 1)
    return np.where(n < max_exact, n, large).astype(np.int32)


def bias_build(rel_bias):
    def body(rb_ref, bk_ref, o_ref):
        bk = bk_ref[...]
        for h in range(A_HEADS):
            acc = jnp.zeros((BLK, 2 * BLK), f32)
            for b in range(N_BUCKETS):
                acc = jnp.where(bk == b, rb_ref[b, h], acc)
            o_ref[h] = acc

    return pl.pallas_call(
        body, name="bias_build",
        in_specs=[pl.BlockSpec(memory_space=pltpu.SMEM), pl.BlockSpec(memory_space=pltpu.VMEM)],
        out_specs=pl.BlockSpec(memory_space=pltpu.VMEM),
        out_shape=jax.ShapeDtypeStruct((A_HEADS, BLK, 2 * BLK), f32), compiler_params=_cp(),
    )(rel_bias, jnp.asarray(_bucket_map()))


def bias_grad(dbias):
    def body(d_ref, bk_ref, o_ref):
        bk = bk_ref[...]
        row = lax.broadcasted_iota(jnp.int32, (N_BUCKETS, LANE), 0)
        lane = lax.broadcasted_iota(jnp.int32, (N_BUCKETS, LANE), 1)
        acc = jnp.zeros((N_BUCKETS, LANE), f32)
        for h in range(A_HEADS):
            d = d_ref[h]
            for b in range(N_BUCKETS):
                s = jnp.sum(jnp.where(bk == b, d, 0.0), keepdims=True)
                acc = acc + jnp.where((row == b) & (lane == h), s, 0.0)
        o_ref[...] = acc

    return pl.pallas_call(
        body, name="bias_grad", out_shape=jax.ShapeDtypeStruct((N_BUCKETS, LANE), f32), compiler_params=_cp(),
    )(dbias, jnp.asarray(_bucket_map()))


def _swa_f(qb, kp, kc, vp, vc, bias, sk, first):
    kband = jnp.concatenate([kp, kc], axis=0)
    vband = jnp.concatenate([vp, vc], axis=0)
    qi = lax.broadcasted_iota(jnp.int32, (BLK, 2 * BLK), 0)
    kj = lax.broadcasted_iota(jnp.int32, (BLK, 2 * BLK), 1)
    rel = kj - qi
    ok = (rel >= 1) & (rel <= BLK) & ((kj >= BLK) | jnp.logical_not(first))
    lane = lax.broadcasted_iota(jnp.int32, (1, LANE), 1)
    lane_b = lax.broadcasted_iota(jnp.int32, (BLK, LANE), 1)
    outs = []
    for p in range(A_HEADS // 2):
        qp = qb[:, LANE * p:LANE * (p + 1)]
        acc = jnp.zeros((BLK, LANE), f32)
        for g in range(2):
            h = g * (A_HEADS // 2) + p
            msk = (lane // A_DH == g).astype(f32)
            s = bdot_nt(qp * msk, kband) * (A_DH ** -0.5) + bias[h]
            s = jnp.where(ok, s, -1e30)
            skb = jnp.broadcast_to(sk[h:h + 1, :], (BLK, LANE))
            sink = jnp.sum(jnp.where(lane_b == 0, skb, 0.0), axis=-1, keepdims=True)
            m = lax.stop_gradient(jnp.maximum(jnp.max(s, axis=-1, keepdims=True), sink))
            e = jnp.exp(s - m)
            prob = e / (jnp.sum(e, axis=-1, keepdims=True) + jnp.exp(sink - m))
            acc = acc + bdot(prob, vband) * msk
        outs.append(acc)
    return jnp.concatenate(outs, axis=1)


def _swa_specs(nb, rev):
    bi = (lambda i: nb - 1 - i) if rev else (lambda i: i)
    return [
        pl.BlockSpec((BLK, A_Q), lambda i: (bi(i), 0)),
        pl.BlockSpec((BLK, LANE), lambda i: (jnp.maximum(bi(i) - 1, 0), 6)),
        pl.BlockSpec((BLK, LANE), lambda i: (bi(i), 6)),
        pl.BlockSpec((BLK, LANE), lambda i: (jnp.maximum(bi(i) - 1, 0), 7)),
        pl.BlockSpec((BLK, LANE), lambda i: (bi(i), 7)),
        pl.BlockSpec((A_HEADS, BLK, 2 * BLK), lambda i: (0, 0, 0)),
        pl.BlockSpec((16, LANE), lambda i: (0, 0)),
    ]


def swa_fwd(proj, bias, sk):
    S = proj.shape[0]
    nb = S // BLK

    def body(q_ref, kp_ref, kc_ref, vp_ref, vc_ref, b_ref, s_ref, o_ref):
        qkv = [r[...].astype(f32) for r in (q_ref, kp_ref, kc_ref, vp_ref, vc_ref)]
        o_ref[...] = _swa_f(*qkv, b_ref[...], s_ref[...], pl.program_id(0) == 0).astype(bf16)

    return pl.pallas_call(
        body, name="swa_fwd", grid=(nb,), in_specs=_swa_specs(nb, False),
        out_specs=pl.BlockSpec((BLK, A_Q), lambda i: (i, 0)),
        out_shape=jax.ShapeDtypeStruct((S, A_Q), bf16), compiler_params=_cp(("parallel",)),
    )(proj, proj, proj, proj, proj, bias, sk)


def swa_bwd(proj, bias, sk, dmix):
    S = proj.shape[0]
    nb = S // BLK

    def body(q_ref, kp_ref, kc_ref, vp_ref, vc_ref, b_ref, s_ref, do_ref, dqkv_ref, db_ref, ds_ref, ck, cv):
        i = pl.program_id(0)

        @pl.when(i == 0)
        def _():
            db_ref[...] = jnp.zeros_like(db_ref)
            ds_ref[...] = jnp.zeros_like(ds_ref)
            ck[...] = jnp.zeros_like(ck)
            cv[...] = jnp.zeros_like(cv)
        first = i == nb - 1
        qkv = [r[...].astype(f32) for r in (q_ref, kp_ref, kc_ref, vp_ref, vc_ref)]
        _, vjp = jax.vjp(lambda *a: _swa_f(*a, first), *qkv, b_ref[...], s_ref[...])
        dq, dkp, dkc, dvp, dvc, db, ds = vjp(do_ref[...])
        dqkv_ref[...] = jnp.concatenate([dq, dkc + ck[...], dvc + cv[...]], axis=1).astype(bf16)
        ck[...] = dkp
        cv[...] = dvp
        db_ref[...] += db
        ds_ref[...] += ds

    return pl.pallas_call(
        body, name="swa_bwd", grid=(nb,),
        in_specs=_swa_specs(nb, True) + [pl.BlockSpec((BLK, A_Q), lambda i: (nb - 1 - i, 0))],
        out_specs=[pl.BlockSpec((BLK, D), lambda i: (nb - 1 - i, 0)),
                   pl.BlockSpec((A_HEADS, BLK, 2 * BLK), lambda i: (0, 0, 0)),
                   pl.BlockSpec((16, LANE), lambda i: (0, 0))],
        out_shape=[jax.ShapeDtypeStruct((S, D), bf16), jax.ShapeDtypeStruct((A_HEADS, BLK, 2 * BLK), f32),
                   jax.ShapeDtypeStruct((16, LANE), f32)],
        scratch_shapes=[pltpu.VMEM((BLK, LANE), f32), pltpu.VMEM((BLK, LANE), f32)],
        compiler_params=_cp(("arbitrary",)),
    )(proj, proj, proj, proj, proj, bias, sk, dmix)


def _dnprep_f(xext, w, is_qk):
    c = (w[3:4] * xext + w[2:3] * shift_down(xext, 1) + w[1:2] * shift_down(xext, 2) + w[0:1] * shift_down(xext, 3))
    a = _silu(c)[HALO:]
    n = a * lax.rsqrt(jnp.sum(a * a, axis=-1, keepdims=True) + EPS)
    return jnp.where(is_qk, n, a)


def dnprep_fwd(proj, cw):
    S = proj.shape[0]
    nblk = B_QKV // LANE
    T = S

    def body(x_ref, w_ref, o_ref):
        is_qk = pl.program_id(0) < 2 * B_QK // LANE
        wv = w_ref[...]

        def tile(r0, first):
            o_ref[pl.ds(r0, T), :] = _dnprep_f(_glu_gext(x_ref, r0, first, T), wv, is_qk)

        tile(0, True)

    return pl.pallas_call(
        body, name="dnprep_fwd", grid=(nblk,),
        in_specs=[pl.BlockSpec((S, LANE), lambda j: (0, j)), pl.BlockSpec((4, LANE), lambda j: (0, j))],
        out_specs=pl.BlockSpec((S, LANE), lambda j: (0, j)),
        out_shape=jax.ShapeDtypeStruct((S, B_QKV), f32), compiler_params=_cp(("parallel",)),
    )(proj, cw)


def dnprep_bwd(proj, cw, dqkvn):
    S = proj.shape[0]
    nblk = B_QKV // LANE

    T = S

    def body(x_ref, w_ref, d_ref, dx_ref, dw_ref):
        is_qk = pl.program_id(0) < 2 * B_QK // LANE
        wv = w_ref[...]

        def tile(r0, first):
            _, vjp = jax.vjp(lambda a, b: _dnprep_f(a, b, is_qk), _glu_gext(x_ref, r0, first, T), wv)
            dx, dw = vjp(d_ref[pl.ds(r0, T), :])
            dx_ref[pl.ds(r0, T), :] = dx[HALO:].astype(bf16)
            if not first:
                dx_ref[pl.ds(r0 - HALO, HALO), :] += dx[:HALO]
            return dw

        dw_ref[...] = tile(0, True)

    col = pl.BlockSpec((S, LANE), lambda j: (0, j))
    wsp = pl.BlockSpec((4, LANE), lambda j: (0, j))
    return pl.pallas_call(
        body, name="dnprep_bwd", grid=(nblk,), in_specs=[col, wsp, col], out_specs=[col, wsp],
        out_shape=[jax.ShapeDtypeStruct((S, B_QKV), bf16), jax.ShapeDtypeStruct((4, B_QKV), f32)],
        compiler_params=_cp(("parallel",)),
    )(proj, cw, dqkvn)


def _hdot(a, b, ca=1, cb=0):
    return _dg(a, b, ca, cb, HI)


def _bdg(a, b, ca, cb):
    dn = (((ca,), (cb,)), ((0,), (0,)))
    ah, bh = a.astype(bf16), b.astype(bf16)
    al, bl = (a - ah.astype(f32)).astype(bf16), (b - bh.astype(f32)).astype(bf16)
    return (lax.dot_general(ah, bh, dn, preferred_element_type=f32)
            + lax.dot_general(ah, bl, dn, preferred_element_type=f32)
            + lax.dot_general(al, bh, dn, preferred_element_type=f32))


@jax.custom_vjp
def hbd(a, b):
    return _bdg(a, b, 2, 1)


@jax.custom_vjp
def hbd_nt(a, b):
    return _bdg(a, b, 2, 2)


@jax.custom_vjp
def hbd_tn(a, b):
    return _bdg(a, b, 1, 1)


hbd.defvjp(lambda a, b: (hbd(a, b), (a, b)), lambda r, g: (hbd_nt(g, r[1]), hbd_tn(r[0], g)))
hbd_nt.defvjp(lambda a, b: (hbd_nt(a, b), (a, b)), lambda r, g: (hbd(g, r[1]), hbd_tn(g, r[0])))
hbd_tn.defvjp(lambda a, b: (hbd_tn(a, b), (a, b)), lambda r, g: (hbd_nt(r[1], g), hbd(r[0], g)))


def _stack(xs):
    return jnp.concatenate([x[None] for x in xs], axis=0)


def _lane_col(x, j):
    lane = lax.broadcasted_iota(jnp.int32, (1, LANE), 1)
    return jnp.sum(jnp.where(lane == j, x, 0.0), axis=-1, keepdims=True)


def _tri_inv(a_mat):
    r = lax.broadcasted_iota(jnp.int32, (1, CHUNK, CHUNK), 1)
    c = lax.broadcasted_iota(jnp.int32, (1, CHUNK, CHUNK), 2)
    pw = -a_mat
    inv = (r == c).astype(f32) + pw
    for _ in range(5):
        pw = hbd(pw, pw)
        inv = inv + hbd(inv, pw)
    return inv


@jax.custom_vjp
def _tri_inv_known(a_mat, inv):
    return inv


_tri_inv_known.defvjp(lambda a, inv: (inv, inv),
                      lambda inv, g: (-hbd_tn(inv, hbd_nt(g, inv)), jnp.zeros_like(inv)))


def _dnc_f(q, k, v, seg, prm, inverse=_tri_inv):
    C = CHUNK
    B = q.shape[0]
    rows = seg.shape[0]
    beta_all = _sigmoid(seg)
    xx = seg + prm[1:2]
    g_all = -jnp.exp(prm[0:1]) * (jnp.maximum(xx, 0.0) + jnp.log(1.0 + jnp.exp(-jnp.abs(xx))))
    r2 = lax.broadcasted_iota(jnp.int32, (rows, rows), 0)
    c2 = lax.broadcasted_iota(jnp.int32, (rows, rows), 1)
    within = (r2 >= c2) & (r2 // C == c2 // C)
    gc_all = _hdot(within.astype(f32), g_all)
    beta = _stack([_lane_col(beta_all[C * j:C * (j + 1)], h) for j in range(rows // C) for h in range(6)])
    gc = _stack([_lane_col(gc_all[C * j:C * (j + 1)], 6 + h) for j in range(rows // C) for h in range(6)])
    r = lax.broadcasted_iota(jnp.int32, (1, C, C), 1)
    c = lax.broadcasted_iota(jnp.int32, (1, C, C), 2)
    incl = r >= c
    strict = r > c
    gct = [gc_all[C * j:C * (j + 1)].T for j in range(rows // C)]
    g_row = _stack([jnp.broadcast_to(gct[j][6 + h:7 + h, :], (C, C))
                    for j in range(rows // C) for h in range(6)])
    decay = jnp.where(incl, jnp.exp(jnp.where(incl, gc - g_row, 0.0)), 0.0)
    a_mat = beta * sbd_nt(k, k) * jnp.where(strict, decay, 0.0)
    eg = jnp.exp(gc)
    inv = inverse(a_mat)
    u = hbd(inv, beta * v)
    w = hbd(inv, (beta * eg) * k)
    qc = q * (B_DH ** -0.5)
    attn = sbd_nt(qc, k) * decay
    last = (lax.broadcasted_iota(jnp.int32, (1, C, 1), 1) == C - 1).astype(f32)
    g_last = jnp.sum(gc * last, axis=1, keepdims=True)
    dc = jnp.broadcast_to(jnp.exp(g_last), (B, 1, LANE)).reshape(B, LANE)
    return u, w, qc * eg, k * jnp.exp(g_last - gc), attn, dc, inv


def _b1(a, b, ca, cb):
    return lax.dot_general(a.astype(bf16), b.astype(bf16), (((ca,), (cb,)), ((0,), (0,))), preferred_element_type=f32)


@jax.custom_vjp
def sbd(a, b):
    return _b1(a, b, 2, 1)


@jax.custom_vjp
def sbd_nt(a, b):
    return _b1(a, b, 2, 2)


@jax.custom_vjp
def sbd_tn(a, b):
    return _b1(a, b, 1, 1)


sbd.defvjp(lambda a, b: (sbd(a, b), (a, b)), lambda r, g: (sbd_nt(g, r[1]), sbd_tn(r[0], g)))
sbd_nt.defvjp(lambda a, b: (sbd_nt(a, b), (a, b)), lambda r, g: (sbd(g, r[1]), sbd_tn(g, r[0])))
sbd_tn.defvjp(lambda a, b: (sbd_tn(a, b), (a, b)), lambda r, g: (sbd_nt(r[1], g), sbd(r[0], g)))


def _dns_f(S0, u, w, qd, kt, attn, dcrows):
    dc = _lane_col(dcrows, 0).reshape(6, 1, 1)
    delta = u - sbd(w, S0)
    out = sbd(qd, S0) + sbd(attn, delta)
    return out, dc * S0 + sbd_tn(kt, delta)


def _dnpost_f(o, z, grow):
    outs = []
    for h in range(6):
        oh = o[:, LANE * h:LANE * (h + 1)]
        outs.append(oh * lax.rsqrt(jnp.mean(oh * oh, axis=-1, keepdims=True) + EPS) * grow
                    * _silu(z[:, LANE * h:LANE * (h + 1)]))
    return jnp.concatenate(outs, axis=1)


def _hs(h):
    return slice(LANE * h, LANE * (h + 1))


DN_CHUNKS = 4


def _heads(ref, share):
    return _stack([ref[CHUNK * j:CHUNK * (j + 1), _hs(h // share)]
                   for j in range(ref.shape[0] // CHUNK) for h in range(6)])


def _put_heads(ref, val):
    for j in range(ref.shape[0] // CHUNK):
        for h in range(6):
            ref[CHUNK * j:CHUNK * (j + 1), _hs(h)] = val[6 * j + h].astype(ref.dtype)


def _dnc_in_specs():
    rows = CHUNK * DN_CHUNKS
    return [
        pl.BlockSpec((rows, B_QK), lambda n: (n, 0)),
        pl.BlockSpec((rows, B_QK), lambda n: (n, 1)),
        pl.BlockSpec((rows, B_V), lambda n: (n, 1)),
        pl.BlockSpec((rows, LANE), lambda n: (n, 20)),
        pl.BlockSpec((8, LANE), lambda n: (0, 0)),
    ]


def _dnc_out_specs(rev_nc=None, chunks=1):
    ci = (lambda n: n) if rev_nc is None else (lambda n: rev_nc - 1 - n)
    wide = pl.BlockSpec((CHUNK * chunks, B_V), lambda n: (ci(n), 0))
    return [wide, wide, wide, wide, pl.BlockSpec((chunks, 6, CHUNK, CHUNK), lambda n: (ci(n), 0, 0, 0)),
            pl.BlockSpec((chunks, 8, LANE), lambda n: (ci(n), 0, 0))]


def _dc_rows(dc):
    pad = jnp.zeros((2, LANE), f32)
    return _stack([jnp.concatenate([dc[6 * j:6 * (j + 1)], pad], axis=0) for j in range(dc.shape[0] // 6)])


def _dnc_shapes(S, mm=f32):
    nc = S // CHUNK
    return [jax.ShapeDtypeStruct((S, B_V), f32)] + [jax.ShapeDtypeStruct((S, B_V), mm)] * 3 + [
        jax.ShapeDtypeStruct((nc, 6, CHUNK, CHUNK), mm), jax.ShapeDtypeStruct((nc, 8, LANE), f32)]


def dnc_fwd(qkvn, proj, prm):
    S = proj.shape[0]

    def body(q_ref, k_ref, v_ref, s_ref, p_ref, u_ref, w_ref, qd_ref, kt_ref, at_ref, dc_ref, inv_ref):
        u, w, qd, kt, attn, dc, inv = _dnc_f(_heads(q_ref, 2), _heads(k_ref, 2), _heads(v_ref, 1), s_ref[...],
                                             p_ref[...])
        inv_ref[...] = inv.reshape(inv_ref.shape)
        _put_heads(u_ref, u)
        _put_heads(w_ref, w)
        _put_heads(qd_ref, qd)
        _put_heads(kt_ref, kt)
        at_ref[...] = attn.reshape(at_ref.shape).astype(at_ref.dtype)
        dc_ref[...] = _dc_rows(dc)

    outs = _dnc_out_specs(chunks=DN_CHUNKS)
    out = pl.pallas_call(
        body, name="dn_chunk_fwd", grid=(S // (CHUNK * DN_CHUNKS),), in_specs=_dnc_in_specs(),
        out_specs=outs + [outs[4]], out_shape=_dnc_shapes(S, bf16) + [_dnc_shapes(S)[4]],
        compiler_params=_cp(("parallel",)),
    )(qkvn, qkvn, qkvn, proj, prm)
    return out[:6], out[6]


def dnc_bwd(qkvn, proj, prm, inv, cots):
    S = proj.shape[0]

    def body(q_ref, k_ref, v_ref, s_ref, p_ref, inv_ref, du_ref, dw_ref, dqd_ref, dkt_ref, dat_ref, ddc_ref,
             dx_ref, dseg_ref, dprm_ref):
        @pl.when(pl.program_id(0) == 0)
        def _():
            dprm_ref[...] = jnp.zeros_like(dprm_ref)
        nb = 6 * DN_CHUNKS
        known = functools.partial(_tri_inv_known, inv=inv_ref[...].reshape(nb, CHUNK, CHUNK))
        _, vjp = jax.vjp(lambda *a: _dnc_f(*a, inverse=known)[:6], _heads(q_ref, 2), _heads(k_ref, 2),
                         _heads(v_ref, 1), s_ref[...], p_ref[...])
        ddc = jnp.concatenate([ddc_ref[j, 0:6, :] for j in range(DN_CHUNKS)], axis=0)
        dq, dk, dv, dseg, dprm = vjp((_heads(du_ref, 1), _heads(dw_ref, 1), _heads(dqd_ref, 1), _heads(dkt_ref, 1),
                                      dat_ref[...].reshape(nb, CHUNK, CHUNK), ddc))
        for j in range(DN_CHUNKS):
            o = 6 * j
            dx_ref[CHUNK * j:CHUNK * (j + 1), :] = jnp.concatenate(
                [dq[o] + dq[o + 1], dq[o + 2] + dq[o + 3], dq[o + 4] + dq[o + 5],
                 dk[o] + dk[o + 1], dk[o + 2] + dk[o + 3], dk[o + 4] + dk[o + 5]] + [dv[o + h] for h in range(6)], axis=1)
        dseg_ref[...] = dseg.astype(bf16)
        dprm_ref[...] += dprm

    rows = CHUNK * DN_CHUNKS
    outs = _dnc_out_specs(chunks=DN_CHUNKS)
    return pl.pallas_call(
        body, name="dn_chunk_bwd", grid=(S // rows,),
        in_specs=_dnc_in_specs() + [outs[4]] + outs,
        out_specs=[pl.BlockSpec((rows, B_QKV), lambda n: (n, 0)), pl.BlockSpec((rows, LANE), lambda n: (n, 0)),
                   pl.BlockSpec((8, LANE), lambda n: (0, 0))],
        out_shape=[jax.ShapeDtypeStruct((S, B_QKV), f32), jax.ShapeDtypeStruct((S, LANE), bf16),
                   jax.ShapeDtypeStruct((8, LANE), f32)],
        compiler_params=_cp(("arbitrary",)),
    )(qkvn, qkvn, qkvn, proj, prm, inv, *cots)


def dns_fwd(chunked):
    u = chunked[0]
    S = u.shape[0]
    nc = S // CHUNK

    def body(u_ref, w_ref, qd_ref, kt_ref, at_ref, dc_ref, o_ref, st_ref, st):
        @pl.when(pl.program_id(0) == 0)
        def _():
            st[...] = jnp.zeros_like(st)
        S0 = st[...]
        st_ref[0] = S0
        out, S1 = _dns_f(S0, _heads(u_ref, 1), _heads(w_ref, 1), _heads(qd_ref, 1), _heads(kt_ref, 1),
                         at_ref[0], dc_ref[0, 0:6, :])
        _put_heads(o_ref, out)
        st[...] = S1

    return pl.pallas_call(
        body, name="dn_scan_fwd", grid=(nc,), in_specs=_dnc_out_specs(),
        out_specs=[pl.BlockSpec((CHUNK, B_V), lambda n: (n, 0)),
                   pl.BlockSpec((1, 6, B_DH, B_DH), lambda n: (n, 0, 0, 0))],
        out_shape=[jax.ShapeDtypeStruct((S, B_V), f32), jax.ShapeDtypeStruct((nc, 6, B_DH, B_DH), f32)],
        scratch_shapes=[pltpu.VMEM((6, B_DH, B_DH), f32)],
        compiler_params=_cp(("arbitrary",)),
    )(*chunked)


def dns_bwd(chunked, states, do):
    S = do.shape[0]
    nc = S // CHUNK

    def body(u_ref, w_ref, qd_ref, kt_ref, at_ref, dc_ref, st_ref, do_ref,
             du_ref, dw_ref, dqd_ref, dkt_ref, dat_ref, ddc_ref, dst):
        @pl.when(pl.program_id(0) == 0)
        def _():
            dst[...] = jnp.zeros_like(dst)
        _, vjp = jax.vjp(_dns_f, st_ref[0], _heads(u_ref, 1), _heads(w_ref, 1).astype(f32),
                         _heads(qd_ref, 1).astype(f32), _heads(kt_ref, 1).astype(f32), at_ref[0].astype(f32),
                         dc_ref[0, 0:6, :])
        dS0, du, dw, dqd, dkt, dat, ddc = vjp((_heads(do_ref, 1), dst[...]))
        dst[...] = dS0
        _put_heads(du_ref, du)
        _put_heads(dw_ref, dw)
        _put_heads(dqd_ref, dqd)
        _put_heads(dkt_ref, dkt)
        dat_ref[0] = dat
        ddc_ref[0] = jnp.concatenate([ddc, jnp.zeros((2, LANE), f32)], axis=0)

    return pl.pallas_call(
        body, name="dn_scan_bwd", grid=(nc,),
        in_specs=_dnc_out_specs(nc) + [pl.BlockSpec((1, 6, B_DH, B_DH), lambda n: (nc - 1 - n, 0, 0, 0)),
                                       pl.BlockSpec((CHUNK, B_V), lambda n: (nc - 1 - n, 0))],
        out_specs=_dnc_out_specs(nc), out_shape=_dnc_shapes(S),
        scratch_shapes=[pltpu.VMEM((6, B_DH, B_DH), f32)],
        compiler_params=_cp(("arbitrary",)),
    )(*chunked, states, do)


def dnpost_fwd(o, proj, prm):
    S = o.shape[0]
    t = min(512, S)

    def body(o_ref, z_ref, p_ref, y_ref):
        y_ref[...] = _dnpost_f(o_ref[...], z_ref[...], p_ref[2:3, :]).astype(bf16)

    tok = pl.BlockSpec((t, B_V), lambda i: (i, 0))
    return pl.pallas_call(
        body, name="dn_post_fwd", grid=(S // t,),
        in_specs=[tok, pl.BlockSpec((t, B_V), lambda i: (i, 2)), pl.BlockSpec((8, LANE), lambda i: (0, 0))],
        out_specs=tok, out_shape=jax.ShapeDtypeStruct((S, B_V), bf16), compiler_params=_cp(("parallel",)),
    )(o, proj, prm)


def dnpost_bwd(o, proj, prm, dmix):
    S = o.shape[0]
    t = min(512, S)

    def body(o_ref, z_ref, p_ref, dy_ref, do_ref, dz_ref, dg_ref):
        @pl.when(pl.program_id(0) == 0)
        def _():
            dg_ref[...] = jnp.zeros_like(dg_ref)
        _, vjp = jax.vjp(_dnpost_f, o_ref[...], z_ref[...], p_ref[2:3, :])
        do, dz, dg = vjp(dy_ref[...])
        do_ref[...] = do
        dz_ref[...] = dz.astype(bf16)
        dg_ref[...] += dg

    tok = pl.BlockSpec((t, B_V), lambda i: (i, 0))
    return pl.pallas_call(
        body, name="dn_post_bwd", grid=(S // t,),
        in_specs=[tok, pl.BlockSpec((t, B_V), lambda i: (i, 2)), pl.BlockSpec((8, LANE), lambda i: (0, 0)), tok],
        out_specs=[tok, tok, pl.BlockSpec((1, LANE), lambda i: (0, 0))],
        out_shape=[jax.ShapeDtypeStruct((S, B_V), f32), jax.ShapeDtypeStruct((S, B_V), bf16),
                   jax.ShapeDtypeStruct((1, LANE), f32)],
        compiler_params=_cp(("arbitrary",)),
    )(o, proj, prm, dmix)


N_FF_BLK = D_FF // LANE
GU_SHARD = 2 * D_FF // 4


GLU_ROWS = 512
HALO = 16


def _glu_conv(gext, w, b):
    return (w[2:3] * gext + w[1:2] * shift_down(gext, 1) + w[0:1] * shift_down(gext, 2) + b)[HALO:]


def _glu_gate(c, up):
    return _silu(c) * up


def _glu_gext(g_ref, r0, first, T=GLU_ROWS):
    if first:
        return jnp.concatenate([jnp.zeros((HALO, LANE), f32), g_ref[0:T, :].astype(f32)], axis=0)
    return g_ref[pl.ds(r0 - HALO, T + HALO), :].astype(f32)


def glu_fwd(gu, w, b, name):
    S = gu.shape[0]
    T = min(GLU_ROWS, S // 2)

    def body(g_ref, u_ref, w_ref, b_ref, o_ref, c_ref):
        wv, bv = w_ref[...], b_ref[...]

        def tile(r0, first):
            c = _glu_conv(_glu_gext(g_ref, r0, first, T), wv, bv)
            c_ref[pl.ds(r0, T), :] = c.astype(bf16)
            o_ref[pl.ds(r0, T), :] = _glu_gate(c, u_ref[pl.ds(r0, T), :].astype(f32)).astype(bf16)

        tile(0, True)

        @pl.loop(1, S // T)
        def _(t):
            tile(pl.multiple_of(t * T, T), False)

    col = pl.BlockSpec((S, LANE), lambda j: (0, j))
    return pl.pallas_call(
        body, name=name, grid=(N_FF_BLK,),
        in_specs=[col, pl.BlockSpec((S, LANE), lambda j: (0, N_FF_BLK + j)), pl.BlockSpec((3, LANE), lambda j: (0, j)),
                  pl.BlockSpec((1, LANE), lambda j: (0, j))],
        out_specs=[col, col], out_shape=[jax.ShapeDtypeStruct((S, D_FF), bf16)] * 2,
        compiler_params=_cp(("parallel",)),
    )(gu, gu, w, b.reshape(1, D_FF))


def glu_bwd(gu, c, w, b, dact, name):
    S = gu.shape[0]
    T = min(GLU_ROWS, S // 2)

    def body(g_ref, u_ref, c_ref, w_ref, b_ref, d_ref, dg_ref, dw_ref, db_ref, acc):
        wv, bv = w_ref[...], b_ref[...]

        def tile(r0, first):
            rows = pl.ds(r0, T)
            _, vjp_gate = jax.vjp(_glu_gate, c_ref[rows, :].astype(f32), u_ref[rows, :].astype(f32))
            dc, du = vjp_gate(d_ref[rows, :].astype(f32))
            _, vjp_conv = jax.vjp(_glu_conv, _glu_gext(g_ref, r0, first, T), wv, bv)
            dgx, dw, db = vjp_conv(dc)
            acc[pl.ds(r0, T), :] = dgx[HALO:]
            if not first:
                acc[pl.ds(r0 - HALO, HALO), :] += dgx[:HALO]
            dg_ref[1, pl.ds(r0, T), :] = du.astype(bf16)
            return dw, db

        dw0, db0 = tile(0, True)
        dw_ref[...] = dw0
        db_ref[...] = db0

        @pl.loop(1, S // T)
        def _(t):
            dw, db = tile(pl.multiple_of(t * T, T), False)
            dw_ref[...] += dw
            db_ref[...] += db

        dg_ref[0] = acc[...].astype(bf16)

    col = pl.BlockSpec((S, LANE), lambda j: (0, j))
    wsp = pl.BlockSpec((3, LANE), lambda j: (0, j))
    bsp = pl.BlockSpec((1, LANE), lambda j: (0, j))
    return pl.pallas_call(
        body, name=name, grid=(N_FF_BLK,),
        in_specs=[col, pl.BlockSpec((S, LANE), lambda j: (0, N_FF_BLK + j)), col, wsp, bsp, col],
        out_specs=[pl.BlockSpec((2, S, LANE), lambda j: (0, 0, j)), wsp, bsp],
        out_shape=[jax.ShapeDtypeStruct((2, S, D_FF), bf16), jax.ShapeDtypeStruct((3, D_FF), f32),
                   jax.ShapeDtypeStruct((1, D_FF), f32)],
        scratch_shapes=[pltpu.VMEM((S, LANE), f32)],
        compiler_params=_cp(("parallel",)),
    )(gu, gu, c, w, b.reshape(1, D_FF), dact)


def gu_fwd(n2, wg, name):
    S = n2.shape[0]
    tm = min(MM_ROWS, S)

    def body(a_ref, w_ref, o_ref):
        o_ref[...] = _dg(a_ref[...], w_ref[...], 1, 0).astype(bf16)

    return pl.pallas_call(
        body, name=name, grid=(4, S // tm),
        in_specs=[pl.BlockSpec((tm, D), lambda s, m: (m, 0)), pl.BlockSpec((None, D, GU_SHARD), lambda s, m: (s, 0, 0))],
        out_specs=pl.BlockSpec((tm, GU_SHARD), lambda s, m: (m, s)),
        out_shape=jax.ShapeDtypeStruct((S, 2 * D_FF), bf16), compiler_params=_cp(("parallel", "parallel")),
    )(n2, wg)


def gu_bwd_x(dgu, wg, norm, name):
    S = dgu.shape[1]
    tm = min(NORM_ROWS, S)

    def body(d_ref, w_ref, h_ref, g_ref, r_ref, o_ref, dg_ref):
        _acc_then_norm_bwd(_dg(d_ref[...], w_ref[...], 1, 1), 4, h_ref, g_ref, r_ref, o_ref, dg_ref)

    tok = pl.BlockSpec((tm, D), lambda m, s: (m, 0))
    vec = pl.BlockSpec((1, D), lambda m, s: (0, 0))
    return pl.pallas_call(
        body, name=name, grid=(S // tm, 4),
        in_specs=[pl.BlockSpec((None, tm, GU_SHARD), lambda m, s: (s // 2, m, s % 2)),
                  pl.BlockSpec((None, D, GU_SHARD), lambda m, s: (s, 0, 0)), tok, vec, tok],
        out_specs=[tok, vec],
        out_shape=[jax.ShapeDtypeStruct((S, D), f32), jax.ShapeDtypeStruct((1, D), f32)],
        compiler_params=_cp(("arbitrary", "arbitrary")),
    )(dgu, wg, norm[0], norm[1].reshape(1, D), norm[2])


def gu_bwd_w(n2, dgu, name):
    S = n2.shape[0]
    tm = min(MM_ROWS, S)
    nm = S // tm

    def body(a_ref, d_ref, o_ref, acc):
        @pl.when(pl.program_id(1) == 0)
        def _():
            acc[...] = jnp.zeros_like(acc)
        acc[...] += _dg(a_ref[...], d_ref[...], 0, 0)

        @pl.when(pl.program_id(1) == nm - 1)
        def _():
            o_ref[...] = acc[...].astype(bf16)

    return pl.pallas_call(
        body, name=name, grid=(4, nm),
        in_specs=[pl.BlockSpec((tm, D), lambda s, m: (m, 0)),
                  pl.BlockSpec((None, tm, GU_SHARD), lambda s, m: (s // 2, m, s % 2))],
        out_specs=pl.BlockSpec((None, D, GU_SHARD), lambda s, m: (s, 0, 0)),
        out_shape=jax.ShapeDtypeStruct((4, D, GU_SHARD), bf16),
        scratch_shapes=[pltpu.VMEM((D, GU_SHARD), f32)],
        compiler_params=_cp(("parallel", "arbitrary")),
    )(n2, dgu)


def _pair_cols(w):
    lead = w.shape[:-1]
    return w.reshape(lead + (2, 6, A_DH)).swapaxes(-3, -2).reshape(lead + (A_Q,))


def _unpair_cols(w):
    lead = w.shape[:-1]
    return w.reshape(lead + (6, 2, A_DH)).swapaxes(-3, -2).reshape(lead + (A_Q,))


def _lay_in_a(w):
    return jnp.concatenate([_pair_cols(w[:, :A_Q]), w[:, A_Q:]], axis=1)


def _unlay_in_a(w):
    return jnp.concatenate([_unpair_cols(w[:, :A_Q]), w[:, A_Q:]], axis=1)


def _lay_out_a(w):
    return jnp.concatenate([_pair_cols(w[:A_Q].T).T, w[A_Q:]], axis=0)


def _unlay_out_a(w):
    return jnp.concatenate([_unpair_cols(w[:A_Q].T).T, w[A_Q:]], axis=0)


def _lay_in_b(w):
    return jnp.concatenate([w[:, :2304], w[:, 2316:], w[:, 2304:2316],
                            jnp.zeros((w.shape[0], LANE - 12), w.dtype)], axis=1)


def _unlay_in_b(w):
    return jnp.concatenate([w[:, :2304], w[:, 2560:2572], w[:, 2304:2560]], axis=1)


def _chip_cols(w):
    return jnp.moveaxis(w.reshape(w.shape[0], 4, w.shape[1] // 4), 1, 0)


def _unchip_cols(w):
    return jnp.moveaxis(w, 0, 1).reshape(w.shape[1], 4 * w.shape[2])


def _local_step(x, mem, target, P):
    arrive = P.get("arrive", lambda key, after: None)
    ready = P.get("ready", lambda key, grads, dep: dep)
    sk = jnp.zeros((16, LANE), f32).at[:A_HEADS].set(jnp.broadcast_to(P["sinks"][:, None], (A_HEADS, LANE)))
    prm = jnp.zeros((8, LANE), f32).at[0, 6:12].set(P["a_log"]).at[1, 6:12].set(P["dt_bias"]).at[2].set(P["out_norm_g"])
    bias = bias_build(P["rel_bias"])
    saved = []
    h = x
    n1 = rms_fwd(h, P["g_mix"][0], "rms_mix0")
    for i in range(2):
        arrive(("w_in", i), n1)
        if i == 0:
            proj = mm_nn(n1, P["w_in_a"], out_dtype=bf16, name="proj_a")
        else:
            proj = mm_nn(n1, P["w_in_b"], name="proj_b")
        arrive(("w_mem", i), proj)
        kv = memkv_fwd(mem, P["g_mem"][i], P["w_mem"][i], f"memkv{i}")
        if i == 0:
            self_out = swa_fwd(proj, bias, sk)
            cross = xattn_fwd(proj, A_Q + 2 * LANE, kv, "xattn_a")
            extra = ()
        else:
            qkvn = dnprep_fwd(proj, P["conv_qkv"])
            chunked, inv = dnc_fwd(qkvn, proj, prm)
            o, states = dns_fwd(chunked)
            self_out = dnpost_fwd(o, proj, prm)
            cross = xattn_fwd(proj, 2304, kv, "xattn_b")
            extra = (qkvn, chunked, inv, states, o)
        mix = (self_out, cross)
        arrive(("w_out", i), cross)
        h2, n2 = mm_res_norm(mix, P["w_out"][i], h, P["g_ffn"][i], f"out_proj{i}")
        arrive(("w_gu", i), n2)
        gu = gu_fwd(n2, P["w_gu"][i], f"gate_up{i}")
        act, pre = glu_fwd(gu, P["ffn_cw"][i], P["ffn_cb"][i], f"glu{i}")
        arrive(("w_down", i), act)
        saved.append((h, n1, kv, proj, mix, h2, n2, gu, pre, act, extra))
        if i == 0:
            h, n1 = mm_res_norm(act, P["w_down"][i], h2, P["g_mix"][1], f"down{i}")
        else:
            h = mm_nn(act, P["w_down"][i], res=h2, name=f"down{i}")

    loss, dh, dg_fin = loss_head(h, P["g_fin"], target)
    G = {"g_fin": dg_fin[0], "g_mix": [None, None], "g_mem": [None, None], "g_ffn": [None, None],
         "w_mem": [None, None], "w_out": [None, None], "w_gu": [None, None], "w_down": [None, None],
         "ffn_cw": [None, None], "ffn_cb": [None, None]}
    for i in (1, 0):
        hin, n1, kv, proj, mix, h2, n2, gu, pre, act, extra = saved[i]
        dact = mm_nt(dh, P["w_down"][i], out_dtype=bf16, name=f"d_act{i}")
        G["w_down"][i] = mm_tn(act, dh, name=f"dw_down{i}")
        dgu, dcw, dcb = glu_bwd(gu, pre, P["ffn_cw"][i], P["ffn_cb"][i], dact, f"glu_bwd{i}")
        G["ffn_cw"][i], G["ffn_cb"][i] = dcw, dcb[0]
        G["w_gu"][i] = gu_bwd_w(n2, dgu, f"dw_gu{i}")
        g_ffn = ready(("ffn", i), G, P["g_ffn"][i])
        dh2, dg = gu_bwd_x(dgu, P["w_gu"][i], (h2, g_ffn, dh), f"d_n2_{i}")
        G["g_ffn"][i] = dg[0]
        dmix = mm_nt(dh2, P["w_out"][i], name=f"d_mix{i}")
        G["w_out"][i] = mm_tn(mix, dh2, name=f"dw_out{i}")
        g_mix = ready(("tick", i), G, P["g_mix"][i])
        if i == 0:
            dqkv, dbias, dsk = swa_bwd(proj, bias, sk, dmix)
            dxq, dkv = xattn_bwd(proj, A_Q + 2 * LANE, kv, dmix, "xattn_a_bwd")
            dproj = (dqkv, dxq)
            G["sinks"] = dsk[:A_HEADS, 0]
            G["rel_bias"] = bias_grad(dbias)[:, :A_HEADS]
            w_in, gname = P["w_in_a"], "w_in_a"
        else:
            qkvn, chunked, inv, states, o = extra
            do, dz, dgo = dnpost_bwd(o, proj, prm, dmix)
            dqkvn, dseg, dprm = dnc_bwd(qkvn, proj, prm, inv, dns_bwd(chunked, states, do))
            draw, dconv = dnprep_bwd(proj, P["conv_qkv"], dqkvn)
            dxq, dkv = xattn_bwd(proj, 2304, kv, dmix, "xattn_b_bwd")
            dproj = (draw, dz, dxq, dseg)
            G["conv_qkv"] = dconv
            G["a_log"], G["dt_bias"], G["out_norm_g"] = dprm[0, 6:12], dprm[1, 6:12], dgo[0]
            w_in, gname = P["w_in_b"], "w_in_b"
        G[gname] = mm_tn(n1, dproj, name=f"d{gname}")
        dh, dg = mm_nt_norm(dproj, w_in, (hin, g_mix, dh2), f"d_n1_{i}")
        G["g_mix"][i] = dg[0]
        dgm, dwm = memkv_bwd(mem, P["g_mem"][i], P["w_mem"][i], dkv, f"memkv_bwd{i}")
        G["g_mem"][i], G["w_mem"][i] = dgm[0], dwm
        ready(("mix", i), G, None)
    return loss, dh, G


def _grads_to_ref(G):
    return {
        "rel_bias": G["rel_bias"], "norm_mix_g": jnp.stack(G["g_mix"]), "norm_mem_g": jnp.stack(G["g_mem"]),
        "w_mem_kv": jnp.stack(G["w_mem"]),
        "w_out": jnp.stack([_unlay_out_a(G["w_out"][0]), G["w_out"][1]]),
        "w_in_a": _unlay_in_a(G["w_in_a"])[None], "sinks_a": G["sinks"][None],
        "w_in_b": _unlay_in_b(G["w_in_b"])[None], "conv_qkv_b": G["conv_qkv"][None],
        "a_log_b": G["a_log"][None], "dt_bias_b": G["dt_bias"][None], "out_norm_g_b": G["out_norm_g"][None],
        "norm_ffn_g": jnp.stack(G["g_ffn"]),
        "w_gate_up": jnp.stack([_unchip_cols(G["w_gu"][0]), _unchip_cols(G["w_gu"][1])]).astype(f32),
        "ffn_conv_w": jnp.stack(G["ffn_cw"]), "ffn_conv_b": jnp.stack(G["ffn_cb"]),
        "w_down": jnp.stack(G["w_down"]), "final_norm_g": G["g_fin"],
    }


ANY = pl.BlockSpec(memory_space=pl.ANY)


def _place():
    return lax.axis_index("x"), lax.axis_index("y"), lax.axis_index("c")


def allreduce_small(buf):
    R = buf.shape[0]

    def body(b_ref, o_ref, recv, ssem, rsem):
        x, y, c = _place()
        me = 4 * x + 2 * y + c

        def peer(k):
            return (1 - x if k & 4 else x, 1 - y if k & 2 else y, 1 - c if k & 1 else c)

        def remote(k, slot):
            return pltpu.make_async_remote_copy(
                src_ref=b_ref, dst_ref=recv.at[slot], send_sem=ssem.at[k - 1], recv_sem=rsem.at[k - 1],
                device_id=peer(k), device_id_type=MESH)

        sends = [remote(k, me) for k in range(1, 8)]
        for cp in sends:
            cp.start()
        recv[me] = b_ref[...]
        for k in range(1, 8):
            px, py, pc = peer(k)
            remote(k, 4 * px + 2 * py + pc).wait_recv()
        for cp in sends:
            cp.wait_send()
        total = recv[0]
        for j in range(1, 8):
            total = total + recv[j]
        o_ref[...] = total

    return pl.pallas_call(
        body, name="small_allreduce",
        in_specs=[pl.BlockSpec(memory_space=pltpu.VMEM)], out_specs=pl.BlockSpec(memory_space=pltpu.VMEM),
        out_shape=jax.ShapeDtypeStruct(buf.shape, f32),
        scratch_shapes=[pltpu.VMEM((8, R, LANE), f32), pltpu.SemaphoreType.DMA((7,)), pltpu.SemaphoreType.DMA((7,))],
    )(buf)


def sum_slots(own, recv, chip, core, name):
    _, R, C = recv.shape
    tr = _row_tile(R, 256)
    nt = R // tr

    def body(p_ref, a_ref, r_ref, o_ref):
        acc = jnp.zeros((tr, C), f32)
        for s in range(4):
            acc = acc + jnp.where(p_ref[0] == s, a_ref[s], r_ref[s]).astype(f32)
        o_ref[...] = acc

    slots = pl.BlockSpec((4, tr, C), lambda i, p_ref: (0, i, 0))
    return pl.pallas_call(
        body, name=name, out_shape=jax.ShapeDtypeStruct((2 * R, C), f32),
        grid_spec=pltpu.PrefetchScalarGridSpec(
            num_scalar_prefetch=1, grid=(nt,), in_specs=[slots, slots],
            out_specs=pl.BlockSpec((tr, C), lambda i, p_ref: (p_ref[1] * nt + i, 0))),
        compiler_params=_cp(("parallel",)),
    )(jnp.stack([chip, core]).astype(jnp.int32), own, recv)


def _half(ref, core, axis=0):
    half = ref.shape[axis] // 2
    idx = (slice(None),) * axis + (pl.ds(core * half, half),)
    return ref.at[idx]


IN_HBM = pl.BlockSpec(memory_space=pltpu.HBM)
IN_SEM = pl.BlockSpec(memory_space=pltpu.SEMAPHORE)
SIDE_EFFECT = pltpu.SideEffectType.DATAFLOW_SIDE_EFFECTING


def _gather_copy(buf, i, k, ssem, rsem, place, landing):
    x, y, c = place
    px, py = [(1 - x, y), (x, 1 - y), (1 - x, 1 - y)][k]
    me = 2 * x + y
    return pltpu.make_async_remote_copy(
        src_ref=buf.at[me], dst_ref=buf.at[me if landing == "theirs" else 2 * px + py],
        send_sem=ssem.at[3 * i + k], recv_sem=rsem.at[3 * i + k], device_id=(px, py, c), device_id_type=MESH)


def gather_start(groups, name):
    flat = [b for grp in groups for b in grp]
    n, ng = len(flat), len(groups)

    def body(*refs):
        bufs, sems = refs[:n], refs[n:n + 2 * ng]
        place = _place()
        j = 0
        for g, grp in enumerate(groups):
            for i in range(len(grp)):
                for k in range(3):
                    _gather_copy(bufs[j], i, k, sems[2 * g], sems[2 * g + 1], place, "theirs").start()
                j += 1
        refs[-1][...] = jnp.zeros_like(refs[-1])

    sem_shapes = [pltpu.SemaphoreType.DMA((3 * len(grp),)) for grp in groups for _ in range(2)]
    out = pl.pallas_call(
        body, name=name, in_specs=[IN_HBM] * n,
        out_specs=(*[IN_SEM] * (2 * ng), *[IN_HBM] * n, pl.BlockSpec(memory_space=pltpu.VMEM)),
        out_shape=(*sem_shapes, *[pltpu.HBM(b.shape, b.dtype) for b in flat], jax.ShapeDtypeStruct((8, LANE), f32)),
        input_output_aliases={i: 2 * ng + i for i in range(n)},
        compiler_params=pltpu.CompilerParams(has_side_effects=SIDE_EFFECT),
    )(*[pltpu.with_memory_space_constraint(b, pltpu.HBM) for b in flat])
    sems, bufs = out[:2 * ng], list(out[2 * ng:2 * ng + n])
    flights, j = [], 0
    for g, grp in enumerate(groups):
        flights.append((bufs[j:j + len(grp)], sems[2 * g], sems[2 * g + 1]))
        j += len(grp)
    return flights, out[-1]


def gather_wait(flight, after, name):
    bufs, ssem, rsem = flight
    n = len(bufs)

    def body(*refs):
        place = _place()
        for i in range(n):
            for k in range(3):
                cp = _gather_copy(refs[i], i, k, refs[n], refs[n + 1], place, "mine")
                cp.wait_send()
                cp.wait_recv()

    return pl.pallas_call(
        body, name=name, in_specs=[IN_HBM] * n + [IN_SEM, IN_SEM, ANY], out_specs=[IN_HBM] * n,
        out_shape=[pltpu.HBM(b.shape, b.dtype) for b in bufs], input_output_aliases={i: i for i in range(n)},
        compiler_params=pltpu.CompilerParams(has_side_effects=SIDE_EFFECT),
    )(*bufs, ssem, rsem, after)


def _scatter_copy(src, land, j, k, ssem, rsem, place, landing):
    x, y, c = place
    px, py = [(1 - x, y), (x, 1 - y), (1 - x, 1 - y)][k]
    return pltpu.make_async_remote_copy(
        src_ref=src.at[2 * px + py], dst_ref=land.at[2 * x + y if landing == "theirs" else 2 * px + py],
        send_sem=ssem.at[3 * j + k], recv_sem=rsem.at[3 * j + k], device_id=(px, py, c), device_id_type=MESH)


def scatter_start(srcs, name):
    n = len(srcs)
    lands = [lax.empty(g.shape, g.dtype) for g in srcs]

    def body(*refs):
        place = _place()
        for j in range(n):
            for k in range(3):
                _scatter_copy(refs[j], refs[n + j], j, k, refs[2 * n], refs[2 * n + 1], place, "theirs").start()
        refs[-1][...] = jnp.zeros_like(refs[-1])

    sem = pltpu.SemaphoreType.DMA((3 * n,))
    hbm = [pltpu.with_memory_space_constraint(b, pltpu.HBM) for b in list(srcs) + lands]
    out = pl.pallas_call(
        body, name=name, in_specs=[IN_HBM] * (2 * n),
        out_specs=(IN_SEM, IN_SEM, *[IN_HBM] * (2 * n), pl.BlockSpec(memory_space=pltpu.VMEM)),
        out_shape=(sem, sem, *[pltpu.HBM(b.shape, b.dtype) for b in hbm], jax.ShapeDtypeStruct((8, LANE), f32)),
        input_output_aliases={i: 2 + i for i in range(2 * n)},
        compiler_params=pltpu.CompilerParams(has_side_effects=SIDE_EFFECT),
    )(*hbm)
    return (list(out[2:2 + n]), list(out[2 + n:2 + 2 * n]), out[0], out[1]), out[-1]


def scatter_wait(flight, after, name):
    srcs, lands, ssem, rsem = flight
    n = len(srcs)

    def body(*refs):
        place = _place()
        for j in range(n):
            for k in range(3):
                cp = _scatter_copy(refs[j], refs[n + j], j, k, refs[2 * n], refs[2 * n + 1], place, "mine")
                cp.wait_send()
                cp.wait_recv()

    out = pl.pallas_call(
        body, name=name, in_specs=[IN_HBM] * (2 * n) + [IN_SEM, IN_SEM, ANY], out_specs=[IN_HBM] * (2 * n),
        out_shape=[pltpu.HBM(b.shape, b.dtype) for b in list(srcs) + list(lands)],
        input_output_aliases={i: i for i in range(2 * n)},
        compiler_params=pltpu.CompilerParams(has_side_effects=SIDE_EFFECT),
    )(*srcs, *lands, ssem, rsem, after)
    return list(out[:n]), list(out[n:])


def _pair_copy(src, land, j, ssem, rsem, place):
    x, y, c = place
    return pltpu.make_async_remote_copy(
        src_ref=_half(src, 1 - c, axis=1), dst_ref=land, send_sem=ssem.at[j], recv_sem=rsem.at[j],
        device_id=(x, y, 1 - c), device_id_type=MESH)


def pair_start(srcs, name):
    n = len(srcs)
    lands = [lax.empty((4, g.shape[1] // 2, g.shape[2]), g.dtype) for g in srcs]

    def body(*refs):
        place = _place()
        for j in range(n):
            _pair_copy(refs[j], refs[n + j], j, refs[2 * n], refs[2 * n + 1], place).start()
        refs[-1][...] = jnp.zeros_like(refs[-1])

    sem = pltpu.SemaphoreType.DMA((n,))
    hbm = [pltpu.with_memory_space_constraint(b, pltpu.HBM) for b in list(srcs) + lands]
    out = pl.pallas_call(
        body, name=name, in_specs=[IN_HBM] * (2 * n),
        out_specs=(IN_SEM, IN_SEM, *[IN_HBM] * (2 * n), pl.BlockSpec(memory_space=pltpu.VMEM)),
        out_shape=(sem, sem, *[pltpu.HBM(b.shape, b.dtype) for b in hbm], jax.ShapeDtypeStruct((8, LANE), f32)),
        input_output_aliases={i: 2 + i for i in range(2 * n)},
        compiler_params=pltpu.CompilerParams(has_side_effects=SIDE_EFFECT),
    )(*hbm)
    return (list(out[2:2 + n]), list(out[2 + n:2 + 2 * n]), out[0], out[1]), out[-1]


def pair_wait(flight, after, name):
    srcs, lands, ssem, rsem = flight
    n = len(srcs)

    def body(*refs):
        place = _place()
        for j in range(n):
            cp = _pair_copy(refs[j], refs[n + j], j, refs[2 * n], refs[2 * n + 1], place)
            cp.wait_send()
            cp.wait_recv()

    out = pl.pallas_call(
        body, name=name, in_specs=[IN_HBM] * (2 * n) + [IN_SEM, IN_SEM, ANY], out_specs=[IN_HBM] * (2 * n),
        out_shape=[pltpu.HBM(b.shape, b.dtype) for b in list(srcs) + list(lands)],
        input_output_aliases={i: i for i in range(2 * n)},
        compiler_params=pltpu.CompilerParams(has_side_effects=SIDE_EFFECT),
    )(*srcs, *lands, ssem, rsem, after)
    return list(out[:n]), list(out[n:])


def _row_tile(rows, cap=512):
    return max(t for t in range(16, min(rows, cap) + 1, 16) if rows % t == 0)


def pair_sum(mine, theirs, core, name):
    _, R, C = mine.shape
    half = R // 2
    tr = _row_tile(half)
    nt = half // tr

    def body(c_ref, a_ref, b_ref, o_ref):
        o_ref[...] = (a_ref[...].astype(f32) + b_ref[...].astype(f32)).astype(bf16)

    return pl.pallas_call(
        body, name=name, out_shape=jax.ShapeDtypeStruct(theirs.shape, bf16),
        grid_spec=pltpu.PrefetchScalarGridSpec(
            num_scalar_prefetch=1, grid=(4, nt),
            in_specs=[pl.BlockSpec((None, tr, C), lambda s, i, c_ref: (s, c_ref[0] * nt + i, 0)),
                      pl.BlockSpec((None, tr, C), lambda s, i, c_ref: (s, i, 0))],
            out_specs=pl.BlockSpec((None, tr, C), lambda s, i, c_ref: (s, i, 0))),
        compiler_params=_cp(("parallel", "parallel")),
    )(jnp.reshape(core, (1,)).astype(jnp.int32), mine, theirs)


def _final_copy(buf, j, ssem, rsem, place, landing):
    x, y, c = place
    return pltpu.make_async_remote_copy(
        src_ref=_half(buf, c), dst_ref=_half(buf, c if landing == "theirs" else 1 - c),
        send_sem=ssem.at[j], recv_sem=rsem.at[j], device_id=(x, y, 1 - c), device_id_type=MESH)


def final_start(fins, name):
    n = len(fins)

    def body(*refs):
        place = _place()
        for j in range(n):
            _final_copy(refs[j], j, refs[n], refs[n + 1], place, "theirs").start()
        refs[-1][...] = jnp.zeros_like(refs[-1])

    sem = pltpu.SemaphoreType.DMA((n,))
    hbm = [pltpu.with_memory_space_constraint(b, pltpu.HBM) for b in fins]
    out = pl.pallas_call(
        body, name=name, in_specs=[IN_HBM] * n,
        out_specs=(IN_SEM, IN_SEM, *[IN_HBM] * n, pl.BlockSpec(memory_space=pltpu.VMEM)),
        out_shape=(sem, sem, *[pltpu.HBM(b.shape, b.dtype) for b in hbm], jax.ShapeDtypeStruct((8, LANE), f32)),
        input_output_aliases={i: 2 + i for i in range(n)},
        compiler_params=pltpu.CompilerParams(has_side_effects=SIDE_EFFECT),
    )(*hbm)
    return (list(out[2:2 + n]), out[0], out[1]), out[-1]


def final_wait(flight, after, name):
    bufs, ssem, rsem = flight
    n = len(bufs)

    def body(*refs):
        place = _place()
        for j in range(n):
            cp = _final_copy(refs[j], j, refs[n], refs[n + 1], place, "mine")
            cp.wait_send()
            cp.wait_recv()

    out = pl.pallas_call(
        body, name=name, in_specs=[IN_HBM] * n + [IN_SEM, IN_SEM, ANY], out_specs=[IN_HBM] * n,
        out_shape=[pltpu.HBM(b.shape, b.dtype) for b in bufs], input_output_aliases={i: i for i in range(n)},
        compiler_params=pltpu.CompilerParams(has_side_effects=SIDE_EFFECT),
    )(*bufs, ssem, rsem, after)
    return list(out)


def adamw_big(w, m, v, gs, row0, name):
    L, R, C = w.shape
    tr = _row_tile(math.gcd(R, row0) if row0 else R, max(16, 262144 // C // 16 * 16))
    b0 = row0 // tr

    def body(*refs):
        w_ref, m_ref, v_ref = refs[:3]
        g_refs = refs[3:3 + L]
        g_ref, d_ref, nm_ref, nv_ref = refs[3 + L:]
        g = g_refs[0][...]
        for l in range(1, L):
            g = jnp.where(pl.program_id(0) == l, g_refs[l][...], g)
        d, nm, nv = _adamw_math(w_ref[...], g, m_ref[...], v_ref[...])
        g_ref[...] = g
        d_ref[...] = d
        nm_ref[...] = nm
        nv_ref[...] = nv

    own = pl.BlockSpec((None, tr, C), lambda l, i: (l, i, 0))
    off = pl.BlockSpec((tr, C), lambda l, i: (b0 + i, 0))
    return pl.pallas_call(
        body, name=name, grid=(L, R // tr), in_specs=[own, own, own] + [off] * L, out_specs=[own] * 4,
        out_shape=[jax.ShapeDtypeStruct((L, R, C), f32)] * 4, compiler_params=_cp(("parallel", "parallel")),
    )(w, m, v, *gs)


def _adamw_math(w, g, m, v):
    m = B1 * m + (1.0 - B1) * g
    v = B2 * v + (1.0 - B2) * (g * g)
    m_hat = m / (1.0 - B1 ** STEP)
    v_hat = v / (1.0 - B2 ** STEP)
    delta = -LR * (m_hat / (jnp.sqrt(v_hat) + AEPS) + WD * w)
    return delta, m, v


def adamw_small(w, m, v, g):
    def body(w_ref, m_ref, v_ref, g_ref, d_ref, nm_ref, nv_ref):
        d, nm, nv = _adamw_math(w_ref[...], g_ref[...], m_ref[...], v_ref[...])
        d_ref[...] = d
        nm_ref[...] = nm
        nv_ref[...] = nv

    return pl.pallas_call(body, name="adamw_small", out_shape=[jax.ShapeDtypeStruct(w.shape, f32)] * 3)(w, m, v, g)


CONV =(("conv_qkv_b", 2), ("ffn_conv_w", 2))
SMALL = ("rel_bias", "norm_mix_g", "norm_mem_g", "sinks_a", "a_log_b", "dt_bias_b", "out_norm_g_b", "norm_ffn_g",
         "ffn_conv_b", "final_norm_g")
WEIGHTS = ("rel_bias", "norm_mix_g", "norm_mem_g", "w_mem_kv", "w_out", "w_in_a", "sinks_a", "w_in_b", "conv_qkv_b",
           "a_log_b", "dt_bias_b", "out_norm_g_b", "norm_ffn_g", "w_gate_up", "ffn_conv_w", "ffn_conv_b", "w_down",
           "final_norm_g")
ARGS = ("x", "mem") + WEIGHTS + ("loss_target",) + tuple("m_" + n for n in WEIGHTS) + tuple("v_" + n for n in WEIGHTS)


def _rows(a, width):
    flat = a.reshape(-1)
    pad = (-flat.shape[0]) % (8 * width)
    if pad:
        flat = jnp.concatenate([flat, jnp.zeros((pad,), a.dtype)])
    return flat.reshape(-1, width)


def _nrows(shape, width):
    return _pad_to(-(-math.prod(shape) // width), 8)


def _pack(arrs, width, total_rows, dtype):
    parts = [_rows(a.astype(dtype), width) for a in arrs]
    used = sum(p.shape[0] for p in parts)
    if total_rows > used:
        parts.append(jnp.zeros((total_rows - used, width), dtype))
    return jnp.concatenate(parts, axis=0)


def _unpack(buf, shapes, width):
    out, r = [], 0
    for s in shapes:
        n = _nrows(s, width)
        out.append(buf[r:r + n].reshape(-1)[:math.prod(s)].reshape(s))
        r += n
    return out


def _pad_to(n, mult):
    return -(-n // mult) * mult


def kernel(x, mem, rel_bias, norm_mix_g, norm_mem_g, w_mem_kv, w_out, w_in_a, sinks_a, w_in_b, conv_qkv_b, a_log_b, dt_bias_b, out_norm_g_b, norm_ffn_g, w_gate_up, ffn_conv_w, ffn_conv_b, w_down, final_norm_g, loss_target, m_rel_bias, m_norm_mix_g, m_norm_mem_g, m_w_mem_kv, m_w_out, m_w_in_a, m_sinks_a, m_w_in_b, m_conv_qkv_b, m_a_log_b, m_dt_bias_b, m_out_norm_g_b, m_norm_ffn_g, m_w_gate_up, m_ffn_conv_w, m_ffn_conv_b, m_w_down, m_final_norm_g, v_rel_bias, v_norm_mix_g, v_norm_mem_g, v_w_mem_kv, v_w_out, v_w_in_a, v_sinks_a, v_w_in_b, v_conv_qkv_b, v_a_log_b, v_dt_bias_b, v_out_norm_g_b, v_norm_ffn_g, v_w_gate_up, v_ffn_conv_w, v_ffn_conv_b, v_w_down, v_final_norm_g):
    A = dict(zip(ARGS, (x, mem, rel_bias, norm_mix_g, norm_mem_g, w_mem_kv, w_out, w_in_a, sinks_a, w_in_b, conv_qkv_b, a_log_b, dt_bias_b, out_norm_g_b, norm_ffn_g, w_gate_up, ffn_conv_w, ffn_conv_b, w_down, final_norm_g, loss_target, m_rel_bias, m_norm_mix_g, m_norm_mem_g, m_w_mem_kv, m_w_out, m_w_in_a, m_sinks_a, m_w_in_b, m_conv_qkv_b, m_a_log_b, m_dt_bias_b, m_out_norm_g_b, m_norm_ffn_g, m_w_gate_up, m_ffn_conv_w, m_ffn_conv_b, m_w_down, m_final_norm_g, v_rel_bias, v_norm_mix_g, v_norm_mem_g, v_w_mem_kv, v_w_out, v_w_in_a, v_sinks_a, v_w_in_b, v_conv_qkv_b, v_a_log_b, v_dt_bias_b, v_out_norm_g_b, v_norm_ffn_g, v_w_gate_up, v_ffn_conv_w, v_ffn_conv_b, v_w_down, v_final_norm_g)))
    chip = 2 * lax.axis_index("x") + lax.axis_index("y")
    core = lax.axis_index("c")

    def own_slot(shard):
        return lax.dynamic_update_index_in_dim(lax.empty((4,) + shard.shape, shard.dtype), shard, chip, 0)

    def bslot(w, tie=0.0):
        return own_slot((w + tie).astype(bf16))

    early = {
        ("w_in", 0): [bslot(w_in_a[0])],
        ("w_mem", 0): [bslot(w_mem_kv[0]), own_slot(ffn_conv_w.reshape(6, -1))],
        ("w_out", 0): [bslot(w_out[0])],
    }
    flights_a, gone = gather_start(list(early.values()), "gather_start_first")
    z = gone[0, 0]
    late = {
        ("w_gu", 0): [bslot(w_gate_up[0], z)], ("w_down", 0): [bslot(w_down[0], z)],
        ("w_in", 1): [bslot(w_in_b[0], z)], ("w_mem", 1): [bslot(w_mem_kv[1], z), own_slot(conv_qkv_b[0] + z)],
        ("w_out", 1): [bslot(w_out[1], z)], ("w_gu", 1): [bslot(w_gate_up[1], z)], ("w_down", 1): [bslot(w_down[1], z)],
    }
    flights_b, started_all = gather_start(list(late.values()), "gather_start_rest")
    flights = dict(zip(list(early) + list(late), flights_a + flights_b))
    P = {"rel_bias": rel_bias, "sinks": sinks_a[0], "a_log": a_log_b[0], "dt_bias": dt_bias_b[0],
         "out_norm_g": out_norm_g_b[0], "g_mix": norm_mix_g, "g_mem": norm_mem_g, "g_ffn": norm_ffn_g,
         "g_fin": final_norm_g, "ffn_cb": [ffn_conv_b[0], ffn_conv_b[1]], "w_mem": [None, None], "w_out": [None, None],
         "w_gu": [None, None], "w_down": [None, None], "ffn_cw": [None, None]}

    def rows4(g):
        return g.reshape(4 * g.shape[1], g.shape[2])

    def arrive(key, after):
        if key not in flights:
            return
        got = gather_wait(flights.pop(key), started_all if key == ("w_in", 0) else after, "gather_wait_%s%d" % key)
        name, i = key
        if name == "w_in":
            P["w_in_a" if i == 0 else "w_in_b"] = (_lay_in_a if i == 0 else _lay_in_b)(_unchip_cols(got[0]))
        elif name == "w_mem":
            P["w_mem"][i] = rows4(got[0])
            if i == 0:
                cw = _unchip_cols(got[1]).reshape(2, 3, D_FF)
                P["ffn_cw"] = [cw[0], cw[1]]
            else:
                P["conv_qkv"] = _unchip_cols(got[1])
        elif name == "w_out":
            P["w_out"][i] = _lay_out_a(rows4(got[0])) if i == 0 else rows4(got[0])
        elif name == "w_gu":
            P["w_gu"][i] = got[0]
        else:
            P["w_down"][i] = rows4(got[0])

    def chip_rows(g):
        return g.reshape(4, g.shape[0] // 4, g.shape[-1])

    sent, started, pending = {}, [], []

    def finish(after):
        key, names, flight = pending.pop()
        tag = "%s%d" % key
        partial, theirs = pair_wait(flight, after, "pair_wait_" + tag)
        pair = [pair_sum(p, t, core, "pair_sum_%s%d" % (nm, key[1])) for p, t, nm in zip(partial, theirs, names)]
        flight, token = scatter_start(pair, "scatter_start_" + tag)
        sent[key] = (names, flight, token)
        started.append(token[0, 0])

    def ready(key, G, dep):
        kind, i = key
        if kind == "tick":
            finish(G["w_out"][i])
        else:
            if kind == "ffn":
                if pending:
                    finish(G["w_gu"][i])
                names, partial = ("gu", "down"), [G["w_gu"][i], chip_rows(G["w_down"][i]).astype(bf16)]
            else:
                g_out = _unlay_out_a(G["w_out"][0]) if i == 0 else G["w_out"][1]
                g_in = _unlay_in_a(G["w_in_a"]) if i == 0 else _unlay_in_b(G["w_in_b"])
                names = ("out", "in", "mem")
                partial = [chip_rows(g_out).astype(bf16), _chip_cols(g_in).astype(bf16),
                           chip_rows(G["w_mem"][i]).astype(bf16)]
            flight, token = pair_start(partial, "pair_start_%s%d" % key)
            pending.append((key, names, flight))
            started.append(token[0, 0])
        if dep is not None:
            while started:
                dep = dep + started.pop()
        return dep

    P["arrive"], P["ready"] = arrive, ready

    loss, dx, G = _local_step(x[0], mem[0], loss_target[0], P)
    gfull = _grads_to_ref(G)
    finish(dx)

    after, halves = sent["mix", 0][2], []
    for key in (("ffn", 1), ("mix", 1), ("ffn", 0), ("mix", 0)):
        names, flight, _ = sent[key]
        pair, arrived = scatter_wait(flight, after, "scatter_wait_%s%d" % key)
        fins = [sum_slots(p, r, chip, core, "sum_slots_%s%d" % (nm, key[1])) for nm, p, r in zip(names, pair, arrived)]
        flight, after = final_start(fins, "final_start_%s%d" % key)
        halves.append(([(nm, key[1]) for nm in names], flight, key))
    done = {}
    for ids, flight, key in halves:
        done.update(zip(ids, final_wait(flight, after, "final_wait_%s%d" % key)))

    sm_shapes = [A[n].shape for n in SMALL] + [gfull[n].shape for n, _ in CONV] + [(LANE,)]
    sm_rows = _pad_to(sum(_nrows(s, LANE) for s in sm_shapes), 8)
    sbuf = _pack([gfull[n] for n in SMALL] + [gfull[n] for n, _ in CONV] + [loss[0]], LANE, sm_rows, f32)
    tot = _unpack(allreduce_small(sbuf), sm_shapes, LANE)
    gsmall = dict(zip(SMALL, tot[:len(SMALL)]))
    for (n, axis), t in zip(CONV, tot[len(SMALL):len(SMALL) + len(CONV)]):
        sh = A[n].shape[axis]
        gsmall[n] = lax.dynamic_slice_in_dim(t, chip * sh, sh, axis)
    loss_out = tot[-1][0]

    out = {}
    plan = (("w_gate_up", [done["gu", 0], done["gu", 1]]), ("w_down", [done["down", 0], done["down", 1]]),
            ("w_out", [done["out", 0], done["out", 1]]), ("w_mem_kv", [done["mem", 0], done["mem", 1]]),
            ("w_in_a", [done["in", 0]]), ("w_in_b", [done["in", 1]]))
    for n, gs in plan:
        shape3 = (len(gs),) + gs[0].shape
        res = adamw_big(A[n].reshape(shape3), A["m_" + n].reshape(shape3), A["v_" + n].reshape(shape3), gs, 0,
                        "adamw_" + n)
        for key, r in zip(("grad_", "delta_", "new_m_", "new_v_"), res):
            out[key + n] = r.reshape(A[n].shape)
    names = SMALL + tuple(n for n, _ in CONV)
    shapes = [A[n].shape for n in names]
    rows = _pad_to(sum(_nrows(s, LANE) for s in shapes), 8)
    packs = [_pack([src[n] for n in names], LANE, rows, f32)
             for src in ({n: A[n] for n in names}, {n: A["m_" + n] for n in names}, {n: A["v_" + n] for n in names}, gsmall)]
    res = adamw_small(*packs)
    for key, r in zip(("delta_", "new_m_", "new_v_"), res):
        for n, a in zip(names, _unpack(r, shapes, LANE)):
            out[key + n] = a
    for n in names:
        out["grad_" + n] = gsmall[n]
    return (loss_out, dx[None], *[out["grad_" + n] for n in WEIGHTS], *[out["delta_" + n] for n in WEIGHTS],
            *[out["new_m_" + n] for n in WEIGHTS], *[out["new_v_" + n] for n in WEIGHTS])
```
